```python
import math
import jax, jax.numpy as jnp
from jax import lax
import numpy as np

D_MODEL = 1024
BATCH = 8
SEQ = 4096
DEPTH = 1

EXPAND = 2
D_MIX = EXPAND * D_MODEL
D_CONV = D_MIX // 2
D_LRU = D_MIX - D_CONV
N_CONV_HEADS = 8
N_LRU_HEADS = 16
LRU_HEAD_DIM = D_LRU // N_LRU_HEADS
SHORT_CONV_WIDTH = 3
LRU_CONV_WIDTH = 4
RG_LRU_C = 8.0
RMS_EPS = 1e-6
IN_COLS = 4 * D_CONV + 2 * D_LRU

kernel_name = "hymba_shortconv_rglru_hybrid"


def rms_norm(x, g):
    xf = x.astype(jnp.float32)
    xf = xf * lax.rsqrt(jnp.mean(xf * xf, axis=-1, keepdims=True) + RMS_EPS)
    return xf.astype(x.dtype) * g


def headwise_rms_norm(y, n_heads, g):
    b, s, d = y.shape
    yh = y.reshape(b, s, n_heads, d // n_heads).astype(jnp.float32)
    yh = yh * lax.rsqrt(jnp.mean(yh * yh, axis=-1, keepdims=True) + RMS_EPS)
    return yh.reshape(b, s, d).astype(y.dtype) * g


def causal_depthwise_conv(u, w):
    k_width = w.shape[0]
    s = u.shape[1]
    up = jnp.pad(u, ((0, 0), (k_width - 1, 0), (0, 0)))
    out = up[:, 0:s, :] * w[0]
    for k in range(1, k_width):
        out = out + up[:, k:k + s, :] * w[k]
    return out


def short_conv_mixer(b_gate, c_gate, x_in, conv_w):
    return b_gate * causal_depthwise_conv(c_gate * x_in, conv_w)


def _lru_combine(left, right):
    a1, b1 = left
    a2, b2 = right
    return a1 * a2, a2 * b1 + b2


def rg_lru_mixer(x_in, conv_w, conv_b, w_a, b_a, w_i, b_i, lam):
    bsz, s, d = x_in.shape
    u = causal_depthwise_conv(x_in, conv_w) + conv_b
    uh = u.reshape(bsz, s, N_LRU_HEADS, LRU_HEAD_DIM)
    r = jax.nn.sigmoid(jnp.einsum('bshd,hde->bshe', uh, w_a).reshape(bsz, s, d) + b_a)
    i = jax.nn.sigmoid(jnp.einsum('bshd,hde->bshe', uh, w_i).reshape(bsz, s, d) + b_i)
    log_a = RG_LRU_C * r.astype(jnp.float32) * jax.nn.log_sigmoid(lam.astype(jnp.float32))
    a = jnp.exp(log_a)
    mult = jnp.sqrt(-jnp.expm1(2.0 * log_a))
    drive = mult * (i * u).astype(jnp.float32)
    _, h = lax.associative_scan(_lru_combine, (a, drive), axis=1)
    return h.astype(x_in.dtype)


def _fwd_setup_inputs(seed: int = 0) -> dict:
    key = jax.random.key(seed)
    ks = jax.random.split(key, 16)
    f32 = jnp.float32
    x = jax.random.normal(ks[0], (BATCH, SEQ, D_MODEL), f32)
    ln_g = 1.0 + 0.02 * jax.random.normal(ks[1], (D_MODEL,), f32)
    w_in = jax.random.normal(ks[2], (D_MODEL, IN_COLS), f32) * D_MODEL ** -0.5
    conv_w = jax.random.normal(ks[3], (SHORT_CONV_WIDTH, D_CONV), f32) * SHORT_CONV_WIDTH ** -0.5
    lru_conv_w = jax.random.normal(ks[4], (LRU_CONV_WIDTH, D_LRU), f32) * LRU_CONV_WIDTH ** -0.5
    lru_conv_b = 0.02 * jax.random.normal(ks[5], (D_LRU,), f32)
    w_a = jax.random.normal(ks[6], (N_LRU_HEADS, LRU_HEAD_DIM, LRU_HEAD_DIM), f32) * LRU_HEAD_DIM ** -0.5
    b_a = 0.02 * jax.random.normal(ks[7], (D_LRU,), f32)
    w_i = jax.random.normal(ks[8], (N_LRU_HEADS, LRU_HEAD_DIM, LRU_HEAD_DIM), f32) * LRU_HEAD_DIM ** -0.5
    b_i = 0.02 * jax.random.normal(ks[9], (D_LRU,), f32)
    a_init = jax.random.uniform(ks[10], (D_LRU,), f32, minval=0.9, maxval=0.999)
    lam = jnp.log(a_init) - jnp.log1p(-a_init)
    conv_out_g = 1.0 + 0.02 * jax.random.normal(ks[11], (D_CONV,), f32)
    lru_out_g = 1.0 + 0.02 * jax.random.normal(ks[12], (D_LRU,), f32)
    w_out = jax.random.normal(ks[13], (D_MIX, D_MODEL), f32) * D_MIX ** -0.5
    final_g = 1.0 + 0.02 * jax.random.normal(ks[14], (D_MODEL,), f32)
    return {"x": x, "ln_g": ln_g, "w_in": w_in, "conv_w": conv_w,
            "lru_conv_w": lru_conv_w, "lru_conv_b": lru_conv_b,
            "w_a": w_a, "b_a": b_a, "w_i": w_i, "b_i": b_i, "lam": lam,
            "conv_out_g": conv_out_g, "lru_out_g": lru_out_g,
            "w_out": w_out, "final_g": final_g}


def _fwd_reference(x, ln_g, w_in, conv_w, lru_conv_w, lru_conv_b, w_a, b_a, w_i, b_i,
              lam, conv_out_g, lru_out_g, w_out, final_g):
    h = x
    for _ in range(DEPTH):
        xn = rms_norm(h, ln_g)
        proj = jnp.einsum('bsd,de->bse', xn, w_in)
        splits = [D_CONV, 2 * D_CONV, 3 * D_CONV, 4 * D_CONV, 4 * D_CONV + D_LRU]
        b_gate, c_gate, x_conv, g_conv, x_lru, g_lru = jnp.split(proj, splits, axis=-1)
        y_conv = short_conv_mixer(b_gate, c_gate, x_conv, conv_w)
        y_conv = headwise_rms_norm(y_conv, N_CONV_HEADS, conv_out_g) * jax.nn.silu(g_conv)
        y_lru = rg_lru_mixer(x_lru, lru_conv_w, lru_conv_b, w_a, b_a, w_i, b_i, lam)
        y_lru = headwise_rms_norm(y_lru, N_LRU_HEADS, lru_out_g) * jax.nn.silu(g_lru)
        y = jnp.concatenate([y_conv, y_lru], axis=-1)
        h = h + jnp.einsum('bse,ed->bsd', y, w_out)
    return rms_norm(h, final_g)


import jax as _jax
import jax.numpy as _jnp

TWIN_FORMAT = 'train_step'
FWD_PARAMS = ['x', 'ln_g', 'w_in', 'conv_w', 'lru_conv_w', 'lru_conv_b', 'w_a', 'b_a', 'w_i', 'b_i', 'lam', 'conv_out_g', 'lru_out_g', 'w_out', 'final_g']
TWIN_WEIGHTS = ['ln_g', 'w_in', 'conv_w', 'lru_conv_w', 'lru_conv_b', 'w_a', 'b_a', 'w_i', 'b_i', 'lam', 'conv_out_g', 'lru_out_g', 'w_out', 'final_g']
TWIN_DIFF_INPUT = 'x'
TWIN_INPUTS = ['x', 'ln_g', 'w_in', 'conv_w', 'lru_conv_w', 'lru_conv_b', 'w_a', 'b_a', 'w_i', 'b_i', 'lam', 'conv_out_g', 'lru_out_g', 'w_out', 'final_g', 'loss_target', 'm_ln_g', 'm_w_in', 'm_conv_w', 'm_lru_conv_w', 'm_lru_conv_b', 'm_w_a', 'm_b_a', 'm_w_i', 'm_b_i', 'm_lam', 'm_conv_out_g', 'm_lru_out_g', 'm_w_out', 'm_final_g', 'v_ln_g', 'v_w_in', 'v_conv_w', 'v_lru_conv_w', 'v_lru_conv_b', 'v_w_a', 'v_b_a', 'v_w_i', 'v_b_i', 'v_lam', 'v_conv_out_g', 'v_lru_out_g', 'v_w_out', 'v_final_g']
TWIN_OUTPUTS = ['loss', 'grad_x', 'grad_ln_g', 'grad_w_in', 'grad_conv_w', 'grad_lru_conv_w', 'grad_lru_conv_b', 'grad_w_a', 'grad_b_a', 'grad_w_i', 'grad_b_i', 'grad_lam', 'grad_conv_out_g', 'grad_lru_out_g', 'grad_w_out', 'grad_final_g', 'delta_ln_g', 'delta_w_in', 'delta_conv_w', 'delta_lru_conv_w', 'delta_lru_conv_b', 'delta_w_a', 'delta_b_a', 'delta_w_i', 'delta_b_i', 'delta_lam', 'delta_conv_out_g', 'delta_lru_out_g', 'delta_w_out', 'delta_final_g', 'new_m_ln_g', 'new_m_w_in', 'new_m_conv_w', 'new_m_lru_conv_w', 'new_m_lru_conv_b', 'new_m_w_a', 'new_m_b_a', 'new_m_w_i', 'new_m_b_i', 'new_m_lam', 'new_m_conv_out_g', 'new_m_lru_out_g', 'new_m_w_out', 'new_m_final_g', 'new_v_ln_g', 'new_v_w_in', 'new_v_conv_w', 'new_v_lru_conv_w', 'new_v_lru_conv_b', 'new_v_w_a', 'new_v_b_a', 'new_v_w_i', 'new_v_b_i', 'new_v_lam', 'new_v_conv_out_g', 'new_v_lru_out_g', 'new_v_w_out', 'new_v_final_g']
TWIN_LEAF_KINDS = {'loss': 'loss', 'grad_x': 'grad_x', 'grad_ln_g': 'grad_w', 'grad_w_in': 'grad_w', 'grad_conv_w': 'grad_w', 'grad_lru_conv_w': 'grad_w', 'grad_lru_conv_b': 'grad_w', 'grad_w_a': 'grad_w', 'grad_b_a': 'grad_w', 'grad_w_i': 'grad_w', 'grad_b_i': 'grad_w', 'grad_lam': 'grad_w', 'grad_conv_out_g': 'grad_w', 'grad_lru_out_g': 'grad_w', 'grad_w_out': 'grad_w', 'grad_final_g': 'grad_w', 'delta_ln_g': 'delta_w', 'delta_w_in': 'delta_w', 'delta_conv_w': 'delta_w', 'delta_lru_conv_w': 'delta_w', 'delta_lru_conv_b': 'delta_w', 'delta_w_a': 'delta_w', 'delta_b_a': 'delta_w', 'delta_w_i': 'delta_w', 'delta_b_i': 'delta_w', 'delta_lam': 'delta_w', 'delta_conv_out_g': 'delta_w', 'delta_lru_out_g': 'delta_w', 'delta_w_out': 'delta_w', 'delta_final_g': 'delta_w', 'new_m_ln_g': 'new_m', 'new_m_w_in': 'new_m', 'new_m_conv_w': 'new_m', 'new_m_lru_conv_w': 'new_m', 'new_m_lru_conv_b': 'new_m', 'new_m_w_a': 'new_m', 'new_m_b_a': 'new_m', 'new_m_w_i': 'new_m', 'new_m_b_i': 'new_m', 'new_m_lam': 'new_m', 'new_m_conv_out_g': 'new_m', 'new_m_lru_out_g': 'new_m', 'new_m_w_out': 'new_m', 'new_m_final_g': 'new_m', 'new_v_ln_g': 'new_v', 'new_v_w_in': 'new_v', 'new_v_conv_w': 'new_v', 'new_v_lru_conv_w': 'new_v', 'new_v_lru_conv_b': 'new_v', 'new_v_w_a': 'new_v', 'new_v_b_a': 'new_v', 'new_v_w_i': 'new_v', 'new_v_b_i': 'new_v', 'new_v_lam': 'new_v', 'new_v_conv_out_g': 'new_v', 'new_v_lru_out_g': 'new_v', 'new_v_w_out': 'new_v', 'new_v_final_g': 'new_v'}


def _forward(args):
    return _fwd_reference(*[args[k] for k in FWD_PARAMS])


def _output_shape():
    out = _jax.eval_shape(lambda: _forward(_fwd_setup_inputs(0)))
    return out.shape, out.dtype

N_MICROBATCH = 1
ADAM_LR = 0.001
ADAM_B1 = 0.9
ADAM_B2 = 0.999
ADAM_EPS = 1e-08
ADAM_WD = 0.01
ADAM_STEP = 10
PER_EXAMPLE_BATCH_AXIS = {'x': 0, 'loss_target': 0}
SHARED_INPUTS = []
_WEIGHT_DTYPES = {'ln_g': _jnp.float32, 'w_in': _jnp.float32, 'conv_w': _jnp.float32, 'lru_conv_w': _jnp.float32, 'lru_conv_b': _jnp.float32, 'w_a': _jnp.float32, 'b_a': _jnp.float32, 'w_i': _jnp.float32, 'b_i': _jnp.float32, 'lam': _jnp.float32, 'conv_out_g': _jnp.float32, 'lru_out_g': _jnp.float32, 'w_out': _jnp.float32, 'final_g': _jnp.float32}
MOMENT_SCALE = {'ln_g': 1.628388e-01, 'w_in': 6.660486e-02, 'conv_w': 7.140444e-02, 'lru_conv_w': 7.176444e-02, 'lru_conv_b': 3.528719e-01, 'w_a': 1.780541e-02, 'b_a': 1.587273e-02, 'w_i': 3.128715e-02, 'b_i': 2.511032e-02, 'lam': 3.163907e-02, 'conv_out_g': 6.528589e-02, 'lru_out_g': 7.419389e-02, 'w_out': 9.048590e-02, 'final_g': 3.196254e+01}


def _to_microbatches(a, axis):
    t = _jnp.moveaxis(a, axis, 0)
    t = t.reshape((N_MICROBATCH, t.shape[0] // N_MICROBATCH) + t.shape[1:])
    return _jnp.moveaxis(t, 1, axis + 1)


def setup_inputs(seed: int = 0) -> dict:
    inp = _fwd_setup_inputs(seed)
    key = _jax.random.fold_in(_jax.random.key(seed), 7919)
    shape, _ = _output_shape()
    out = dict(inp)
    out["loss_target"] = _jax.random.normal(_jax.random.fold_in(key, 0), shape, _jnp.float32)
    for i, name in enumerate(TWIN_WEIGHTS):
        w = inp[name].astype(_jnp.float32)
        if MOMENT_SCALE is None:
            s = _jnp.sqrt(_jnp.mean(_jnp.square(w)) + 1e-30)
        else:
            s = MOMENT_SCALE[name]
        km, kv = _jax.random.split(_jax.random.fold_in(key, i + 1))
        out[name] = w
        out["m_" + name] = s * _jax.random.normal(km, w.shape, _jnp.float32)
        out["v_" + name] = (s * s) * _jax.random.uniform(kv, w.shape, _jnp.float32, 0.5, 1.5)
    if N_MICROBATCH > 1:
        for name, axis in PER_EXAMPLE_BATCH_AXIS.items():
            out[name] = _to_microbatches(out[name], axis)
    return {'x': out['x'], 'ln_g': out['ln_g'], 'w_in': out['w_in'], 'conv_w': out['conv_w'], 'lru_conv_w': out['lru_conv_w'], 'lru_conv_b': out['lru_conv_b'], 'w_a': out['w_a'], 'b_a': out['b_a'], 'w_i': out['w_i'], 'b_i': out['b_i'], 'lam': out['lam'], 'conv_out_g': out['conv_out_g'], 'lru_out_g': out['lru_out_g'], 'w_out': out['w_out'], 'final_g': out['final_g'], 'loss_target': out['loss_target'], 'm_ln_g': out['m_ln_g'], 'm_w_in': out['m_w_in'], 'm_conv_w': out['m_conv_w'], 'm_lru_conv_w': out['m_lru_conv_w'], 'm_lru_conv_b': out['m_lru_conv_b'], 'm_w_a': out['m_w_a'], 'm_b_a': out['m_b_a'], 'm_w_i': out['m_w_i'], 'm_b_i': out['m_b_i'], 'm_lam': out['m_lam'], 'm_conv_out_g': out['m_conv_out_g'], 'm_lru_out_g': out['m_lru_out_g'], 'm_w_out': out['m_w_out'], 'm_final_g': out['m_final_g'], 'v_ln_g': out['v_ln_g'], 'v_w_in': out['v_w_in'], 'v_conv_w': out['v_conv_w'], 'v_lru_conv_w': out['v_lru_conv_w'], 'v_lru_conv_b': out['v_lru_conv_b'], 'v_w_a': out['v_w_a'], 'v_b_a': out['v_b_a'], 'v_w_i': out['v_w_i'], 'v_b_i': out['v_b_i'], 'v_lam': out['v_lam'], 'v_conv_out_g': out['v_conv_out_g'], 'v_lru_out_g': out['v_lru_out_g'], 'v_w_out': out['v_w_out'], 'v_final_g': out['v_final_g']}


def _loss(weights, diff, rest, loss_target):
    with _jax.named_scope("forward"):
        args = {**rest, TWIN_DIFF_INPUT: diff, **{k: w.astype(_WEIGHT_DTYPES[k]) for k, w in weights.items()}}
        y = _forward(args)
    with _jax.named_scope("loss_head"):
        err = _jnp.square(y.astype(_jnp.float32) - loss_target)
        return 0.5 * _jnp.sum(_jnp.mean(err, axis=-1)) if err.ndim else 0.5 * err


def _adamw(w, g, m, v):
    m = ADAM_B1 * m + (1.0 - ADAM_B1) * g
    v = ADAM_B2 * v + (1.0 - ADAM_B2) * _jnp.square(g)
    m_hat = m / (1.0 - ADAM_B1 ** ADAM_STEP)
    v_hat = v / (1.0 - ADAM_B2 ** ADAM_STEP)
    delta = -ADAM_LR * (m_hat / (_jnp.sqrt(v_hat) + ADAM_EPS) + ADAM_WD * w)
    return delta, m, v


def reference(x, ln_g, w_in, conv_w, lru_conv_w, lru_conv_b, w_a, b_a, w_i, b_i, lam, conv_out_g, lru_out_g, w_out, final_g, loss_target, m_ln_g, m_w_in, m_conv_w, m_lru_conv_w, m_lru_conv_b, m_w_a, m_b_a, m_w_i, m_b_i, m_lam, m_conv_out_g, m_lru_out_g, m_w_out, m_final_g, v_ln_g, v_w_in, v_conv_w, v_lru_conv_w, v_lru_conv_b, v_w_a, v_b_a, v_w_i, v_b_i, v_lam, v_conv_out_g, v_lru_out_g, v_w_out, v_final_g):
    given = dict(x=x, ln_g=ln_g, w_in=w_in, conv_w=conv_w, lru_conv_w=lru_conv_w, lru_conv_b=lru_conv_b, w_a=w_a, b_a=b_a, w_i=w_i, b_i=b_i, lam=lam, conv_out_g=conv_out_g, lru_out_g=lru_out_g, w_out=w_out, final_g=final_g, loss_target=loss_target, m_ln_g=m_ln_g, m_w_in=m_w_in, m_conv_w=m_conv_w, m_lru_conv_w=m_lru_conv_w, m_lru_conv_b=m_lru_conv_b, m_w_a=m_w_a, m_b_a=m_b_a, m_w_i=m_w_i, m_b_i=m_b_i, m_lam=m_lam, m_conv_out_g=m_conv_out_g, m_lru_out_g=m_lru_out_g, m_w_out=m_w_out, m_final_g=m_final_g, v_ln_g=v_ln_g, v_w_in=v_w_in, v_conv_w=v_conv_w, v_lru_conv_w=v_lru_conv_w, v_lru_conv_b=v_lru_conv_b, v_w_a=v_w_a, v_b_a=v_b_a, v_w_i=v_w_i, v_b_i=v_b_i, v_lam=v_lam, v_conv_out_g=v_conv_out_g, v_lru_out_g=v_lru_out_g, v_w_out=v_w_out, v_final_g=v_final_g)
    weights = {n: given[n] for n in TWIN_WEIGHTS}
    shared = {n: given[n] for n in SHARED_INPUTS}
    per_example = {n: given[n] for n in ['x']}
    grad_fn = _jax.value_and_grad(_loss, argnums=(0, 1))

    def one_microbatch(ex, loss_target):
        ex = dict(ex)
        diff = ex.pop(TWIN_DIFF_INPUT)
        return grad_fn(weights, diff, {**shared, **ex}, loss_target)

    if N_MICROBATCH == 1:
        loss, (grad_w, grad_x) = one_microbatch(per_example, given["loss_target"])
    else:
        def body(carry, xs):
            loss_sum, grad_sum = carry
            l_k, (gw_k, gx_k) = one_microbatch(xs[0], xs[1])
            with _jax.named_scope("update"):
                return (loss_sum + l_k, _jax.tree.map(_jnp.add, grad_sum, gw_k)), gx_k

        init = (_jnp.zeros((), _jnp.float32), _jax.tree.map(_jnp.zeros_like, weights))
        (loss, grad_w), grad_x = _jax.lax.scan(body, init, (per_example, given["loss_target"]))
    with _jax.named_scope("update"):
        delta_w, new_m, new_v = {}, {}, {}
        for n in TWIN_WEIGHTS:
            delta_w[n], new_m[n], new_v[n] = _adamw(weights[n], grad_w[n], given["m_" + n], given["v_" + n])
    return (loss, grad_x, *[grad_w[n] for n in TWIN_WEIGHTS], *[delta_w[n] for n in TWIN_WEIGHTS],
            *[new_m[n] for n in TWIN_WEIGHTS], *[new_v[n] for n in TWIN_WEIGHTS])
```

```python
import functools

import jax
import jax.numpy as jnp
from jax import lax
from jax.experimental import pallas as pl
from jax.experimental.pallas import tpu as pltpu

F32 = jnp.float32
MXU_DTYPE = jnp.bfloat16

D_MODEL = 1024
D_PART = 1024
N_PARTS = 6
CHUNK = 512
CHUNKS_PER_BLOCK = 3
N_CHUNKS = 12
N_CHIPS = 4
SUBLANES = 8
LANES = 128
LW = 256
NS = D_PART // LW
CONV_HEAD = 128
LRU_HEAD = 64
HEADS_PER_STRIP = LW // LRU_HEAD
RMS_EPS = 1e-6
RG_LRU_C = 8.0
ADAM_LR = 0.001
ADAM_B1 = 0.9
ADAM_B2 = 0.999
ADAM_EPS = 1e-08
ADAM_WD = 0.01
ADAM_STEP = 10

PV_CONV_W = 0
PV_LRU_W = 3
PV_LRU_B = 7
PV_BA = 8
PV_BI = 9
PV_LAM = 10
PV_CG = 11
PV_LG = 12
PV_ROWS = 16
N_ACC = 13

MESH = pl.DeviceIdType.MESH
VMEM_LIMIT = 48 * 1024 * 1024
ARB = "arbitrary"


def _cp(*sem, **kw):
    return pltpu.CompilerParams(dimension_semantics=sem or None, vmem_limit_bytes=VMEM_LIMIT, **kw)


def _mm(a, b):
    return jnp.dot(a, b, preferred_element_type=F32)


def _mm_nt(a, b):
    return lax.dot_general(a, b, (((1,), (1,)), ((), ())), preferred_element_type=F32)


def _mm_tn(a, b):
    return lax.dot_general(a, b, (((0,), (0,)), ((), ())), preferred_element_type=F32)


def _sigmoid(x):
    return 1.0 / (1.0 + jnp.exp(-x))


def _one_minus_exp(x, e):
    em1 = jnp.where(e == 1.0, x, (e - 1.0) * x / jnp.log(e))
    return jnp.where(x > -0.5, -em1, 1.0 - e)


def _log_sigmoid(x):
    z = jnp.exp(-jnp.abs(x))
    u = 1.0 + z
    log1p = jnp.where(u == 1.0, z, jnp.log(u) * z / (u - 1.0))
    return jnp.minimum(x, 0.0) - log1p


def _head_mean(z, head):
    out = []
    for k in range(z.shape[1] // LANES):
        zk = z[:, LANES * k:LANES * (k + 1)]
        if head == LANES:
            m = jnp.sum(zk, axis=-1, keepdims=True) * (1.0 / head)
            out.append(jnp.broadcast_to(m, zk.shape))
        else:
            lo = lax.broadcasted_iota(jnp.int32, zk.shape, 1) < head
            s_lo = jnp.sum(jnp.where(lo, zk, 0.0), axis=-1, keepdims=True)
            s_hi = jnp.sum(jnp.where(lo, 0.0, zk), axis=-1, keepdims=True)
            out.append(jnp.where(lo, s_lo, s_hi) * (1.0 / head))
    return jnp.concatenate(out, axis=1)


def _shift_down(cur, prev, d, row):
    return jnp.where(row >= d, pltpu.roll(cur, d, 0), pltpu.roll(prev, d, 0))


def _shift_up(cur, nxt, d, row):
    return jnp.where(row < SUBLANES - d, pltpu.roll(cur, SUBLANES - d, 0), pltpu.roll(nxt, SUBLANES - d, 0))


def _scan8_fwd(a, b, row):
    A, B = a, b
    for d in (1, 2, 4):
        m = row >= d
        a_s = jnp.where(m, pltpu.roll(A, d, 0), 1.0)
        b_s = jnp.where(m, pltpu.roll(B, d, 0), 0.0)
        B = A * b_s + B
        A = A * a_s
    return A, B


def _scan8_rev(a, b, row):
    A, B = a, b
    for d in (1, 2, 4):
        m = row < SUBLANES - d
        a_s = jnp.where(m, pltpu.roll(A, SUBLANES - d, 0), 1.0)
        b_s = jnp.where(m, pltpu.roll(B, SUBLANES - d, 0), 0.0)
        B = A * b_s + B
        A = A * a_s
    return A, B


def _gates(ra, ia, u, lsb):
    r = _sigmoid(ra)
    ig = _sigmoid(ia)
    la = (RG_LRU_C * r) * lsb
    a = jnp.exp(la)
    e2 = jnp.exp(2.0 * la)
    mult = jnp.sqrt(_one_minus_exp(2.0 * la, e2))
    return r, ig, a, e2, mult


def _mesh_pos():
    x, y, c = lax.axis_index("x"), lax.axis_index("y"), lax.axis_index("c")
    chips = [(1 - x, y), (x, 1 - y), (1 - x, 1 - y)]
    return x, y, c, chips


def _allgather_weights(w_in, w_out, small):
    half_i = w_in.shape[0] // 2
    half_o = w_out.shape[0] // 2

    def body(wi_ref, wo_ref, sm_ref, w12_ref, wo4_ref, sm4_ref, send_sems, recv_sems):
        x, y, c, chips = _mesh_pos()
        k = 2 * x + y
        sib = (x, y, 1 - c)
        for s in range(CHUNKS_PER_BLOCK):
            w12_ref[CHUNKS_PER_BLOCK * k + s] = wi_ref[:, CHUNK * s:CHUNK * (s + 1)].astype(MXU_DTYPE)
        wo4_ref[k] = wo_ref[...].astype(MXU_DTYPE)
        sm4_ref[k] = sm_ref[...]

        def in_half(chip, core):
            kk = 2 * chip[0] + chip[1]
            return w12_ref.at[pl.ds(CHUNKS_PER_BLOCK * kk, CHUNKS_PER_BLOCK), pl.ds(pl.multiple_of(half_i * core, half_i), half_i), :]

        def out_half(chip, core):
            kk = 2 * chip[0] + chip[1]
            return wo4_ref.at[kk, pl.ds(pl.multiple_of(half_o * core, half_o), half_o), :]

        def copy(ref, sem, to):
            return pltpu.make_async_remote_copy(src_ref=ref, dst_ref=ref, send_sem=send_sems.at[sem],
                                                recv_sem=recv_sems.at[sem], device_id=to, device_id_type=MESH)

        me = (x, y)
        first = []
        for m, chip in enumerate(chips):
            first.append(copy(in_half(me, c), m, (*chip, c)))
            first.append(copy(out_half(me, c), 3 + m, (*chip, c)))
            first.append(copy(sm4_ref.at[k], 6 + m, (*chip, c)))
        for cp in first:
            cp.start()
        passed = []
        for m, chip in enumerate(chips):
            copy(in_half(chip, c), m, sib).wait_recv()
            fwd = copy(in_half(chip, c), 9 + m, sib)
            fwd.start()
            passed.append(fwd)
            copy(out_half(chip, c), 3 + m, sib).wait_recv()
            fwd = copy(out_half(chip, c), 12 + m, sib)
            fwd.start()
            passed.append(fwd)
            kk = 2 * chip[0] + chip[1]
            copy(sm4_ref.at[kk], 6 + m, sib).wait_recv()
        for m, chip in enumerate(chips):
            copy(in_half(chip, 1 - c), 9 + m, sib).wait_recv()
            copy(out_half(chip, 1 - c), 12 + m, sib).wait_recv()
        for cp in first + passed:
            cp.wait_send()

    vm = pl.BlockSpec(memory_space=pltpu.VMEM)
    return pl.pallas_call(
        body,
        out_shape=(jax.ShapeDtypeStruct((N_CHUNKS, w_in.shape[0], CHUNK), MXU_DTYPE),
                   jax.ShapeDtypeStruct((N_CHIPS,) + w_out.shape, MXU_DTYPE),
                   jax.ShapeDtypeStruct((N_CHIPS,) + small.shape, F32)),
        in_specs=[vm, vm, vm], out_specs=(vm, vm, vm),
        scratch_shapes=[pltpu.SemaphoreType.DMA((15,)), pltpu.SemaphoreType.DMA((15,))],
        compiler_params=_cp(), name="allgather_weights",
    )(w_in, w_out, small)


def _allreduce_small(buf):
    def body(in_ref, out_ref, r0, r1, r2, send_sems, recv_sems):
        x, y, c, _ = _mesh_pos()
        peers = [(x, y, 1 - c), (1 - x, y, c), (x, 1 - y, c)]
        out_ref[...] = in_ref[...]
        for ph, (peer, rbuf) in enumerate(zip(peers, (r0, r1, r2))):
            cp = pltpu.make_async_remote_copy(src_ref=out_ref, dst_ref=rbuf, send_sem=send_sems.at[ph],
                                              recv_sem=recv_sems.at[ph], device_id=peer, device_id_type=MESH)
            cp.start()
            cp.wait()
            out_ref[...] = out_ref[...] + rbuf[...]

    vm = pl.BlockSpec(memory_space=pltpu.VMEM)
    return pl.pallas_call(
        body, out_shape=jax.ShapeDtypeStruct(buf.shape, F32), in_specs=[vm], out_specs=vm,
        scratch_shapes=[pltpu.VMEM(buf.shape, F32)] * 3 + [pltpu.SemaphoreType.DMA((3,)), pltpu.SemaphoreType.DMA((3,))],
        compiler_params=_cp(), name="allreduce_small",
    )(buf)


def _exchange_sibling_halves(g12, go4):
    hi, ho = g12.shape[1] // 2, go4.shape[1] // 2

    def body(g12_ref, go4_ref, ri_ref, ro_ref, send_sems, recv_sems):
        x, y, c, _ = _mesh_pos()
        sib = (x, y, 1 - c)
        cps = [
            pltpu.make_async_remote_copy(src_ref=g12_ref.at[:, pl.ds(pl.multiple_of(hi * (1 - c), hi), hi), :], dst_ref=ri_ref,
                                         send_sem=send_sems.at[0], recv_sem=recv_sems.at[0], device_id=sib, device_id_type=MESH),
            pltpu.make_async_remote_copy(src_ref=go4_ref.at[:, pl.ds(pl.multiple_of(ho * (1 - c), ho), ho), :], dst_ref=ro_ref,
                                         send_sem=send_sems.at[1], recv_sem=recv_sems.at[1], device_id=sib, device_id_type=MESH),
        ]
        for cp in cps:
            cp.start()
        for cp in cps:
            cp.wait()

    hbm = pl.BlockSpec(memory_space=pl.ANY)
    return pl.pallas_call(
        body,
        out_shape=(jax.ShapeDtypeStruct((g12.shape[0], hi, g12.shape[2]), F32),
                   jax.ShapeDtypeStruct((go4.shape[0], ho, go4.shape[2]), F32)),
        in_specs=[hbm, hbm], out_specs=(hbm, hbm),
        scratch_shapes=[pltpu.SemaphoreType.DMA((2,)), pltpu.SemaphoreType.DMA((2,))],
        compiler_params=_cp(), name="exchange_sibling_halves",
    )(g12, go4)


def _add_own_half(g, r, c_arr, name):
    n, rr, cc = r.shape

    def body(c_ref, g_ref, r_ref, o_ref):
        o_ref[...] = g_ref[...] + r_ref[...]

    return pl.pallas_call(
        body, out_shape=jax.ShapeDtypeStruct(r.shape, F32),
        grid_spec=pltpu.PrefetchScalarGridSpec(
            num_scalar_prefetch=1, grid=(n,),
            in_specs=[pl.BlockSpec((1, rr, cc), lambda q, c_ref: (q, c_ref[0], 0)),
                      pl.BlockSpec((1, rr, cc), lambda q, c_ref: (q, 0, 0))],
            out_specs=pl.BlockSpec((1, rr, cc), lambda q, c_ref: (q, 0, 0))),
        compiler_params=_cp(ARB), name=name,
    )(c_arr, g, r)


def _exchange_chip_blocks(s_in, s_out):
    def body(si_ref, so_ref, ri_ref, ro_ref, send_sems, recv_sems):
        x, y, c, chips = _mesh_pos()
        cps = []
        for m, chip in enumerate(chips):
            kk = 2 * chip[0] + chip[1]
            cps.append(pltpu.make_async_remote_copy(
                src_ref=si_ref.at[pl.ds(CHUNKS_PER_BLOCK * kk, CHUNKS_PER_BLOCK)], dst_ref=ri_ref.at[m],
                send_sem=send_sems.at[m], recv_sem=recv_sems.at[m], device_id=(*chip, c), device_id_type=MESH))
            cps.append(pltpu.make_async_remote_copy(
                src_ref=so_ref.at[kk], dst_ref=ro_ref.at[m],
                send_sem=send_sems.at[3 + m], recv_sem=recv_sems.at[3 + m], device_id=(*chip, c), device_id_type=MESH))
        for cp in cps:
            cp.start()
        for cp in cps:
            cp.wait()

    hbm = pl.BlockSpec(memory_space=pl.ANY)
    return pl.pallas_call(
        body,
        out_shape=(jax.ShapeDtypeStruct((3, CHUNKS_PER_BLOCK) + s_in.shape[1:], F32),
                   jax.ShapeDtypeStruct((3,) + s_out.shape[1:], F32)),
        in_specs=[hbm, hbm], out_specs=(hbm, hbm),
        scratch_shapes=[pltpu.SemaphoreType.DMA((6,)), pltpu.SemaphoreType.DMA((6,))],
        compiler_params=_cp(), name="exchange_chip_blocks",
    )(s_in, s_out)


def _sum_chip_blocks(s, r, kc_arr, n_sub, name):
    _, rr, cc = s.shape

    def body(kc_ref, s_ref, r_ref, o_ref):
        o_ref[...] = ((s_ref[...] + r_ref[0]) + r_ref[1]) + r_ref[2]

    return pl.pallas_call(
        body, out_shape=jax.ShapeDtypeStruct((n_sub, 2 * rr, cc), F32),
        grid_spec=pltpu.PrefetchScalarGridSpec(
            num_scalar_prefetch=1, grid=(n_sub,),
            in_specs=[pl.BlockSpec((1, rr, cc), lambda q, kc: (n_sub * kc[0] + q, 0, 0)),
                      pl.BlockSpec((3, 1, rr, cc), lambda q, kc: (0, q, 0, 0))],
            out_specs=pl.BlockSpec((1, rr, cc), lambda q, kc: (q, kc[1], 0))),
        compiler_params=_cp(ARB), name=name,
    )(kc_arr, s, r)


def _swap_sibling_halves(f_in, f_out):
    hi, ho = f_in.shape[1] // 2, f_out.shape[1] // 2

    def body(fi_in, fo_in, fi_ref, fo_ref, send_sems, recv_sems):
        del fi_in, fo_in
        x, y, c, _ = _mesh_pos()
        sib = (x, y, 1 - c)
        si = fi_ref.at[:, pl.ds(pl.multiple_of(hi * c, hi), hi), :]
        so = fo_ref.at[:, pl.ds(pl.multiple_of(ho * c, ho), ho), :]
        cps = [
            pltpu.make_async_remote_copy(src_ref=si, dst_ref=si, send_sem=send_sems.at[0], recv_sem=recv_sems.at[0],
                                         device_id=sib, device_id_type=MESH),
            pltpu.make_async_remote_copy(src_ref=so, dst_ref=so, send_sem=send_sems.at[1], recv_sem=recv_sems.at[1],
                                         device_id=sib, device_id_type=MESH),
        ]
        for cp in cps:
            cp.start()
        for cp in cps:
            cp.wait()

    hbm = pl.BlockSpec(memory_space=pl.ANY)
    return pl.pallas_call(
        body,
        out_shape=(jax.ShapeDtypeStruct(f_in.shape, F32), jax.ShapeDtypeStruct(f_out.shape, F32)),
        in_specs=[hbm, hbm], out_specs=(hbm, hbm), input_output_aliases={0: 0, 1: 1},
        scratch_shapes=[pltpu.SemaphoreType.DMA((2,)), pltpu.SemaphoreType.DMA((2,))],
        compiler_params=_cp(), name="swap_sibling_halves",
    )(f_in, f_out)


def _in_projection(x, ln_g, w12):
    t = x.shape[0]
    tm = 512
    wide = CHUNKS_PER_BLOCK * CHUNK

    def body(x_ref, g_ref, w_ref, proj_ref, xn_ref, xn_s):
        @pl.when(pl.program_id(1) == 0)
        def _():
            xf = x_ref[...]
            r = lax.rsqrt(jnp.mean(xf * xf, axis=-1, keepdims=True) + RMS_EPS)
            xn = ((xf * r) * g_ref[...]).astype(MXU_DTYPE)
            xn_s[...] = xn
            xn_ref[...] = xn

        xn = xn_s[...]
        for s in range(CHUNKS_PER_BLOCK):
            proj_ref[:, CHUNK * s:CHUNK * (s + 1)] = _mm(xn, w_ref[s])

    return pl.pallas_call(
        body, grid=(t // tm, N_CHIPS),
        in_specs=[pl.BlockSpec((tm, D_MODEL), lambda i, j: (i, 0)),
                  pl.BlockSpec((1, D_MODEL), lambda i, j: (0, 0)),
                  pl.BlockSpec((CHUNKS_PER_BLOCK, D_MODEL, CHUNK), lambda i, j: (j, 0, 0))],
        out_specs=(pl.BlockSpec((tm, wide), lambda i, j: (i, j)),
                   pl.BlockSpec((tm, D_MODEL), lambda i, j: (i, 0))),
        out_shape=(jax.ShapeDtypeStruct((t, N_CHUNKS * CHUNK), F32), jax.ShapeDtypeStruct((t, D_MODEL), MXU_DTYPE)),
        scratch_shapes=[pltpu.VMEM((tm, D_MODEL), MXU_DTYPE)],
        compiler_params=_cp(ARB, ARB), name="in_projection",
    )(x, ln_g, w12)


def _out_projection_loss(yc, yl, x, target, wo, final_g):
    t = x.shape[0]
    tm = 256

    def body(yc_ref, yl_ref, x_ref, t_ref, wo_ref, fg_ref, do_ref, dob_ref, dy_ref, st_ref):
        @pl.when(pl.program_id(0) == 0)
        def _():
            st_ref[...] = jnp.zeros_like(st_ref)

        o = x_ref[...] + (_mm(yc_ref[...], wo_ref[0:D_PART, :]) + _mm(yl_ref[...], wo_ref[D_PART:2 * D_PART, :]))
        r2 = lax.rsqrt(jnp.mean(o * o, axis=-1, keepdims=True) + RMS_EPS)
        ohat = o * r2
        fg = fg_ref[...]
        diff = ohat * fg - t_ref[...]
        dout = diff * (1.0 / D_MODEL)
        gp = dout * fg
        do = r2 * (gp - ohat * jnp.mean(gp * ohat, axis=-1, keepdims=True))
        do_ref[...] = do
        dob = do.astype(MXU_DTYPE)
        dob_ref[...] = dob
        dy_ref[...] = _mm_nt(dob, wo_ref[...])
        st_ref[0:1, :] += jnp.sum(dout * ohat, axis=0, keepdims=True)
        loss = 0.5 * jnp.sum(jnp.sum(diff * diff, axis=-1, keepdims=True) * (1.0 / D_MODEL), axis=0, keepdims=True)
        st_ref[1:2, :] += jnp.broadcast_to(loss, (1, D_MODEL))

    row = lambda i: (i, 0)
    fix = lambda i: (0, 0)
    return pl.pallas_call(
        body, grid=(t // tm,),
        in_specs=[pl.BlockSpec((tm, D_PART), row), pl.BlockSpec((tm, D_PART), row),
                  pl.BlockSpec((tm, D_MODEL), row), pl.BlockSpec((tm, D_MODEL), row),
                  pl.BlockSpec((2 * D_PART, D_MODEL), fix), pl.BlockSpec((1, D_MODEL), fix)],
        out_specs=(pl.BlockSpec((tm, D_MODEL), row), pl.BlockSpec((tm, D_MODEL), row),
                   pl.BlockSpec((tm, 2 * D_PART), row), pl.BlockSpec((SUBLANES, D_MODEL), fix)),
        out_shape=(jax.ShapeDtypeStruct((t, D_MODEL), F32), jax.ShapeDtypeStruct((t, D_MODEL), MXU_DTYPE),
                   jax.ShapeDtypeStruct((t, 2 * D_PART), F32), jax.ShapeDtypeStruct((SUBLANES, D_MODEL), F32)),
        compiler_params=_cp(ARB), name="out_projection_loss",
    )(yc, yl, x, target, wo, final_g)


def _input_grad(dps, w12, x, do, ln_g):
    t = x.shape[0]
    tm = 512

    def body(*refs):
        dp_refs = refs[:N_PARTS]
        w_ref, x_ref, do_ref, g_ref, gx_ref, st_ref, acc = refs[N_PARTS:]
        i, p = pl.program_id(0), pl.program_id(1)

        @pl.when((i == 0) & (p == 0))
        def _():
            st_ref[...] = jnp.zeros_like(st_ref)

        @pl.when(p == 0)
        def _():
            acc[...] = jnp.zeros_like(acc)

        for q in range(N_PARTS):
            @pl.when(p == q)
            def _(q=q):
                acc[...] += (_mm_nt(dp_refs[q][:, 0:CHUNK], w_ref[0]) + _mm_nt(dp_refs[q][:, CHUNK:2 * CHUNK], w_ref[1]))

        @pl.when(p == N_PARTS - 1)
        def _():
            xf = x_ref[...]
            r = lax.rsqrt(jnp.mean(xf * xf, axis=-1, keepdims=True) + RMS_EPS)
            xhat = xf * r
            dxn = acc[...]
            st_ref[0:1, :] += jnp.sum(dxn * xhat, axis=0, keepdims=True)
            dxh = dxn * g_ref[...]
            gx_ref[...] = do_ref[...] + r * (dxh - xhat * jnp.mean(dxh * xhat, axis=-1, keepdims=True))

    row = lambda i, p: (i, 0)
    fix = lambda i, p: (0, 0)
    return pl.pallas_call(
        body, grid=(t // tm, N_PARTS),
        in_specs=[pl.BlockSpec((tm, D_PART), row)] * N_PARTS + [
            pl.BlockSpec((2, D_MODEL, CHUNK), lambda i, p: (p, 0, 0)),
            pl.BlockSpec((tm, D_MODEL), row), pl.BlockSpec((tm, D_MODEL), row), pl.BlockSpec((1, D_MODEL), fix)],
        out_specs=(pl.BlockSpec((tm, D_MODEL), row), pl.BlockSpec((SUBLANES, D_MODEL), fix)),
        out_shape=(jax.ShapeDtypeStruct((t, D_MODEL), F32), jax.ShapeDtypeStruct((SUBLANES, D_MODEL), F32)),
        scratch_shapes=[pltpu.VMEM((tm, D_MODEL), F32)],
        compiler_params=_cp(ARB, ARB), name="input_grad",
    )(*dps, w12, x, do, ln_g)


def _w_in_grad(xn, dps):
    t = xn.shape[0]
    tk = 512

    def body(*refs):
        xn_ref = refs[0]
        dp_refs = refs[1:1 + N_PARTS]
        o_ref = refs[1 + N_PARTS]
        p, kk = pl.program_id(0), pl.program_id(1)

        @pl.when(kk == 0)
        def _():
            o_ref[...] = jnp.zeros_like(o_ref)

        for q in range(N_PARTS):
            @pl.when(p == q)
            def _(q=q):
                xnv = xn_ref[...]
                o_ref[0] += _mm_tn(xnv, dp_refs[q][:, 0:CHUNK])
                o_ref[1] += _mm_tn(xnv, dp_refs[q][:, CHUNK:2 * CHUNK])

    def dp_map(q):
        return lambda p, kk: (jnp.where(p == q, kk, 0), 0)

    return pl.pallas_call(
        body, grid=(N_PARTS, t // tk),
        in_specs=[pl.BlockSpec((tk, D_MODEL), lambda p, kk: (kk, 0))] + [pl.BlockSpec((tk, D_PART), dp_map(q)) for q in range(N_PARTS)],
        out_specs=pl.BlockSpec((2, D_MODEL, CHUNK), lambda p, kk: (p, 0, 0)),
        out_shape=jax.ShapeDtypeStruct((N_CHUNKS, D_MODEL, CHUNK), F32),
        compiler_params=_cp(ARB, ARB), name="w_in_grad",
    )(xn, *dps)


def _w_out_grad(yc, yl, dob):
    t = yc.shape[0]
    tk = 512

    def body(yc_ref, yl_ref, do_ref, o_ref):
        @pl.when(pl.program_id(0) == 0)
        def _():
            o_ref[...] = jnp.zeros_like(o_ref)

        dov = do_ref[...]
        o_ref[0:D_PART, :] += _mm_tn(yc_ref[...], dov)
        o_ref[D_PART:2 * D_PART, :] += _mm_tn(yl_ref[...], dov)

    row = lambda kk: (kk, 0)
    out = pl.pallas_call(
        body, grid=(t // tk,),
        in_specs=[pl.BlockSpec((tk, D_PART), row), pl.BlockSpec((tk, D_PART), row), pl.BlockSpec((tk, D_MODEL), row)],
        out_specs=pl.BlockSpec((2 * D_PART, D_MODEL), lambda kk: (0, 0)),
        out_shape=jax.ShapeDtypeStruct((2 * D_PART, D_MODEL), F32),
        compiler_params=_cp(ARB), name="w_out_grad",
    )(yc, yl, dob)
    return out.reshape(N_CHIPS, 2 * D_PART // N_CHIPS, D_MODEL)


def _pvb(pv_ref, r):
    return jnp.broadcast_to(pv_ref[r:r + 1, :], (SUBLANES, pv_ref.shape[1]))


def _conv3(pv_ref, u, u1, u2):
    return (_pvb(pv_ref, PV_CONV_W) * u2 + _pvb(pv_ref, PV_CONV_W + 1) * u1) + _pvb(pv_ref, PV_CONV_W + 2) * u


def _conv4(pv_ref, v, v1, v2, v3):
    return ((((_pvb(pv_ref, PV_LRU_W) * v3 + _pvb(pv_ref, PV_LRU_W + 1) * v2) + _pvb(pv_ref, PV_LRU_W + 2) * v1)
             + _pvb(pv_ref, PV_LRU_W + 3) * v) + _pvb(pv_ref, PV_LRU_B))


def _mixer_forward(proj, pvec, wai):
    t = proj.shape[0]
    tb = 512
    ng = tb // SUBLANES
    nt = t // tb

    def body(bg_ref, cg_ref, xc_ref, gc_ref, xl_ref, gl_ref, pv_ref, wai_ref,
             yc_ref, yl_ref, h_ref,
             ucp_s, xlp_s, ls_s, hbuf_s, u_s, gate_s, zc_s, zl_s):
        @pl.when(pl.program_id(1) == 0)
        def _():
            ucp_s[...] = jnp.zeros_like(ucp_s)
            xlp_s[...] = jnp.zeros_like(xlp_s)
            hbuf_s[0:SUBLANES, :] = jnp.zeros((SUBLANES, LW), F32)

        row = lax.broadcasted_iota(jnp.int32, (SUBLANES, LW), 0)
        ls_s[...] = _log_sigmoid(_pvb(pv_ref, PV_LAM))

        def conv_group(g, carry):
            ucp, xlp = carry
            sl = pl.ds(pl.multiple_of(g * SUBLANES, SUBLANES), SUBLANES)
            uc = cg_ref[sl, :] * xc_ref[sl, :]
            v = _conv3(pv_ref, uc, _shift_down(uc, ucp, 1, row), _shift_down(uc, ucp, 2, row))
            yc = bg_ref[sl, :] * v
            rr = lax.rsqrt(_head_mean(yc * yc, CONV_HEAD) + RMS_EPS)
            gc = gc_ref[sl, :]
            zc_s[sl, :] = ((yc * rr) * _pvb(pv_ref, PV_CG)) * (gc * _sigmoid(gc))
            xl = xl_ref[sl, :]
            u_s[sl, :] = _conv4(pv_ref, xl, _shift_down(xl, xlp, 1, row), _shift_down(xl, xlp, 2, row),
                                _shift_down(xl, xlp, 3, row))
            return uc, xl

        ucp, xlp = lax.fori_loop(0, ng, conv_group, (ucp_s[...], xlp_s[...]))
        ucp_s[...] = ucp
        xlp_s[...] = xlp

        gate_s[...] = _mm(u_s[...].astype(MXU_DTYPE), wai_ref[0])

        def lru_group(g, carry):
            r0 = pl.multiple_of(g * SUBLANES, SUBLANES)
            sl = pl.ds(r0, SUBLANES)
            u = u_s[sl, :]
            r, ig, a, e2, mult = _gates(gate_s[sl, 0:LW] + _pvb(pv_ref, PV_BA), gate_s[sl, LW:2 * LW] + _pvb(pv_ref, PV_BI),
                                        u, ls_s[...])
            A, B = _scan8_fwd(a, mult * (ig * u), row)
            h = B + A * jnp.broadcast_to(hbuf_s[pl.ds(r0, SUBLANES), :][SUBLANES - 1:SUBLANES, :], (SUBLANES, LW))
            hbuf_s[pl.ds(r0 + SUBLANES, SUBLANES), :] = h
            rr = lax.rsqrt(_head_mean(h * h, LRU_HEAD) + RMS_EPS)
            gl = gl_ref[sl, :]
            zl_s[sl, :] = ((h * rr) * _pvb(pv_ref, PV_LG)) * (gl * _sigmoid(gl))
            return carry

        lax.fori_loop(0, ng, lru_group, 0)
        h_ref[...] = hbuf_s[SUBLANES:SUBLANES + tb, :]
        hbuf_s[0:SUBLANES, :] = hbuf_s[tb:tb + SUBLANES, :]
        yc_ref[...] = zc_s[...].astype(MXU_DTYPE)
        yl_ref[...] = zl_s[...].astype(MXU_DTYPE)

    def part(p):
        return pl.BlockSpec((tb, LW), lambda c, i: (i, p * NS + c))

    strip = pl.BlockSpec((tb, LW), lambda c, i: (i, c))
    return pl.pallas_call(
        body, grid=(NS, nt),
        in_specs=[part(p) for p in range(N_PARTS)] + [
            pl.BlockSpec((PV_ROWS, LW), lambda c, i: (0, c)),
            pl.BlockSpec((1, LW, 2 * LW), lambda c, i: (c, 0, 0))],
        out_specs=(strip, strip, strip),
        out_shape=(jax.ShapeDtypeStruct((t, D_PART), MXU_DTYPE), jax.ShapeDtypeStruct((t, D_PART), MXU_DTYPE),
                   jax.ShapeDtypeStruct((t, D_PART), F32)),
        scratch_shapes=[pltpu.VMEM((SUBLANES, LW), F32), pltpu.VMEM((SUBLANES, LW), F32), pltpu.VMEM((SUBLANES, LW), F32),
                        pltpu.VMEM((tb + SUBLANES, LW), F32), pltpu.VMEM((tb, LW), F32), pltpu.VMEM((tb, 2 * LW), F32),
                        pltpu.VMEM((tb, LW), F32), pltpu.VMEM((tb, LW), F32)],
        compiler_params=_cp(ARB, ARB), name="mixer_forward",
    )(proj, proj, proj, proj, proj, proj, pvec, wai)


def _mixer_backward(proj, h, dy, pvec, wai):
    t = proj.shape[0]
    tb = 512
    ng = tb // SUBLANES
    nt = t // tb
    gpb = tb // SUBLANES

    def body(bg_ref, cg_ref, xc_ref, gc_ref, xl_ref, gl_ref, h_ref, dyc_ref, dyl_ref,
             cgh_ref, xch_ref, xlh_ref, hh_ref, pv_ref, wai_ref,
             dp0, dp1, dp2, dp3, dp4, dp5, gw_ref, sv_ref,
             ls_s, u_s, uce_s, xle_s, he_s, gate_s, dgate_s, du_s, gbuf_s,
             p0_s, p1_s, p2_s, p3_s, p4_s, p5_s, acc_s, an_s, dvn_s, dun_s):
        i = pl.program_id(1)
        first_block = i == nt - 1

        @pl.when(i == 0)
        def _():
            acc_s[...] = jnp.zeros_like(acc_s)
            gw_ref[...] = jnp.zeros_like(gw_ref)
            an_s[...] = jnp.zeros_like(an_s)
            dvn_s[...] = jnp.zeros_like(dvn_s)
            dun_s[...] = jnp.zeros_like(dun_s)
            gbuf_s[tb:tb + SUBLANES, :] = jnp.zeros((SUBLANES, LW), F32)

        row = lax.broadcasted_iota(jnp.int32, (SUBLANES, LW), 0)
        ls_s[...] = _log_sigmoid(_pvb(pv_ref, PV_LAM))
        keep = jnp.where(first_block, 0.0, 1.0)
        uce_s[0:SUBLANES, :] = (cgh_ref[...] * xch_ref[...]) * keep
        xle_s[0:SUBLANES, :] = xlh_ref[...] * keep
        he_s[0:SUBLANES, :] = hh_ref[...] * keep
        xle_s[SUBLANES:SUBLANES + tb, :] = xl_ref[...]
        he_s[SUBLANES:SUBLANES + tb, :] = h_ref[...]

        def recompute_group(g, carry):
            r0 = pl.multiple_of(g * SUBLANES, SUBLANES)
            sl = pl.ds(r0, SUBLANES)
            uce_s[pl.ds(r0 + SUBLANES, SUBLANES), :] = cg_ref[sl, :] * xc_ref[sl, :]
            xl = xle_s[pl.ds(r0 + SUBLANES, SUBLANES), :]
            xlp = xle_s[sl, :]
            u_s[sl, :] = _conv4(pv_ref, xl, _shift_down(xl, xlp, 1, row), _shift_down(xl, xlp, 2, row),
                                _shift_down(xl, xlp, 3, row))
            return carry

        lax.fori_loop(0, ng, recompute_group, 0)
        gate_s[...] = _mm(u_s[...].astype(MXU_DTYPE), wai_ref[0])

        def acc_add(k, v):
            acc_s[k] += v

        def main_group(gi, carry):
            a_next, dv_next = carry
            g = ng - 1 - gi
            r0 = pl.multiple_of(g * SUBLANES, SUBLANES)
            sl = pl.ds(r0, SUBLANES)
            sl_e = pl.ds(r0 + SUBLANES, SUBLANES)
            lsb = ls_s[...]
            u = u_s[sl, :]
            r, ig, a, e2, mult = _gates(gate_s[sl, 0:LW] + _pvb(pv_ref, PV_BA), gate_s[sl, LW:2 * LW] + _pvb(pv_ref, PV_BI),
                                        u, lsb)
            gl = gl_ref[sl, :]
            sg = _sigmoid(gl)
            s_l = gl * sg
            h8 = he_s[sl_e, :]
            hprev = _shift_down(h8, he_s[sl, :], 1, row)
            rr = lax.rsqrt(_head_mean(h8 * h8, LRU_HEAD) + RMS_EPS)
            n = h8 * rr
            dz = dyl_ref[sl, :]
            lg = _pvb(pv_ref, PV_LG)
            acc_add(PV_LG, (dz * n) * s_l)
            p5_s[sl, :] = ((dz * n) * lg) * (sg * (1.0 + gl * (1.0 - sg)))
            dn = (dz * lg) * s_l
            dh = rr * (dn - n * _head_mean(dn * n, LRU_HEAD))
            A, B = _scan8_rev(_shift_up(a, a_next, 1, row), dh, row)
            gg = B + A * jnp.broadcast_to(gbuf_s[sl_e, :][0:1, :], (SUBLANES, LW))
            gbuf_s[sl, :] = gg
            da = gg * hprev
            iu = ig * u
            diu = gg * mult
            dla = da * a - (gg * iu) * (e2 / mult)
            acc_add(PV_LAM, dla * (RG_LRU_C * r))
            dra = (dla * (RG_LRU_C * lsb)) * (r * (1.0 - r))
            dia = (diu * u) * (ig * (1.0 - ig))
            dgate_s[sl, 0:LW] = dra
            dgate_s[sl, LW:2 * LW] = dia
            acc_add(PV_BA, dra)
            acc_add(PV_BI, dia)
            du_s[sl, :] = diu * ig
            bg = bg_ref[sl, :]
            gc = gc_ref[sl, :]
            uc = uce_s[sl_e, :]
            ucp = uce_s[sl, :]
            uc1 = _shift_down(uc, ucp, 1, row)
            uc2 = _shift_down(uc, ucp, 2, row)
            v = _conv3(pv_ref, uc, uc1, uc2)
            yc = bg * v
            rrc = lax.rsqrt(_head_mean(yc * yc, CONV_HEAD) + RMS_EPS)
            nc = yc * rrc
            sgc = _sigmoid(gc)
            s_c = gc * sgc
            dzc = dyc_ref[sl, :]
            cgain = _pvb(pv_ref, PV_CG)
            acc_add(PV_CG, (dzc * nc) * s_c)
            p3_s[sl, :] = ((dzc * nc) * cgain) * (sgc * (1.0 + gc * (1.0 - sgc)))
            dnc = (dzc * cgain) * s_c
            dyc = rrc * (dnc - nc * _head_mean(dnc * nc, CONV_HEAD))
            p0_s[sl, :] = dyc * v
            dv = dyc * bg
            duc = (_pvb(pv_ref, PV_CONV_W + 2) * dv + _pvb(pv_ref, PV_CONV_W + 1) * _shift_up(dv, dv_next, 1, row)
                   + _pvb(pv_ref, PV_CONV_W) * _shift_up(dv, dv_next, 2, row))
            acc_add(PV_CONV_W + 2, dv * uc)
            acc_add(PV_CONV_W + 1, dv * uc1)
            acc_add(PV_CONV_W, dv * uc2)
            p1_s[sl, :] = duc * xc_ref[sl, :]
            p2_s[sl, :] = duc * cg_ref[sl, :]
            return a, dv

        a_next, dv_next = lax.fori_loop(0, ng, main_group, (an_s[...], dvn_s[...]))
        an_s[...] = a_next
        dvn_s[...] = dv_next
        gbuf_s[tb:tb + SUBLANES, :] = gbuf_s[0:SUBLANES, :]

        dgb = dgate_s[...].astype(MXU_DTYPE)
        du_s[...] += _mm_nt(dgb, wai_ref[0])
        gw_ref[0] += _mm_tn(u_s[...].astype(MXU_DTYPE), dgb)

        def lru_conv_group(gi, du_next):
            g = ng - 1 - gi
            r0 = pl.multiple_of(g * SUBLANES, SUBLANES)
            sl = pl.ds(r0, SUBLANES)
            du = du_s[sl, :]
            xl = xle_s[pl.ds(r0 + SUBLANES, SUBLANES), :]
            xlp = xle_s[sl, :]
            acc_add(PV_LRU_B, du)
            acc_add(PV_LRU_W + 3, du * xl)
            acc_add(PV_LRU_W + 2, du * _shift_down(xl, xlp, 1, row))
            acc_add(PV_LRU_W + 1, du * _shift_down(xl, xlp, 2, row))
            acc_add(PV_LRU_W, du * _shift_down(xl, xlp, 3, row))
            p4_s[sl, :] = (((_pvb(pv_ref, PV_LRU_W + 3) * du + _pvb(pv_ref, PV_LRU_W + 2) * _shift_up(du, du_next, 1, row))
                            + _pvb(pv_ref, PV_LRU_W + 1) * _shift_up(du, du_next, 2, row))
                           + _pvb(pv_ref, PV_LRU_W) * _shift_up(du, du_next, 3, row))
            return du

        dun_s[...] = lax.fori_loop(0, ng, lru_conv_group, dun_s[...])

        for dp_ref, p_s in zip((dp0, dp1, dp2, dp3, dp4, dp5), (p0_s, p1_s, p2_s, p3_s, p4_s, p5_s)):
            dp_ref[...] = p_s[...].astype(MXU_DTYPE)

        @pl.when(first_block)
        def _():
            sv_ref[...] = jnp.zeros_like(sv_ref)
            for k in range(N_ACC):
                tot = jnp.sum(acc_s[k], axis=0, keepdims=True)
                if k == PV_LAM:
                    tot = tot * _sigmoid(-pv_ref[PV_LAM:PV_LAM + 1, :])
                sv_ref[k:k + 1, :] = tot

    def part(p):
        return pl.BlockSpec((tb, LW), lambda c, i: (nt - 1 - i, p * NS + c))

    def halo(p):
        return pl.BlockSpec((SUBLANES, LW), lambda c, i: (jnp.maximum((nt - 1 - i) * gpb - 1, 0), p * NS + c))

    strip = pl.BlockSpec((tb, LW), lambda c, i: (nt - 1 - i, c))
    big = pltpu.VMEM((tb, LW), F32)
    big_e = pltpu.VMEM((tb + SUBLANES, LW), F32)
    wide = pltpu.VMEM((tb, 2 * LW), F32)
    small = pltpu.VMEM((SUBLANES, LW), F32)
    outs = pl.pallas_call(
        body, grid=(NS, nt),
        in_specs=[part(p) for p in range(N_PARTS)] + [
            strip, strip, pl.BlockSpec((tb, LW), lambda c, i: (nt - 1 - i, NS + c)),
            halo(1), halo(2), halo(4),
            pl.BlockSpec((SUBLANES, LW), lambda c, i: (jnp.maximum((nt - 1 - i) * gpb - 1, 0), c)),
            pl.BlockSpec((PV_ROWS, LW), lambda c, i: (0, c)),
            pl.BlockSpec((1, LW, 2 * LW), lambda c, i: (c, 0, 0))],
        out_specs=(strip,) * N_PARTS + (pl.BlockSpec((1, LW, 2 * LW), lambda c, i: (c, 0, 0)),
                                        pl.BlockSpec((PV_ROWS, LW), lambda c, i: (0, c))),
        out_shape=(jax.ShapeDtypeStruct((t, D_PART), MXU_DTYPE),) * N_PARTS + (
            jax.ShapeDtypeStruct((NS, LW, 2 * LW), F32), jax.ShapeDtypeStruct((PV_ROWS, D_PART), F32)),
        scratch_shapes=[small, big, big_e, big_e, big_e, wide, wide, big, big_e,
                        big, big, big, big, big, big, pltpu.VMEM((N_ACC, SUBLANES, LW), F32), small, small, small],
        compiler_params=_cp(ARB, ARB), name="mixer_backward",
    )(proj, proj, proj, proj, proj, proj, h, dy, dy, proj, proj, proj, h, pvec, wai)
    return outs[:N_PARTS], outs[N_PARTS], outs[N_PARTS + 1]


def _adamw(w, g, m, v):
    m = ADAM_B1 * m + (1.0 - ADAM_B1) * g
    v = ADAM_B2 * v + (1.0 - ADAM_B2) * (g * g)
    m_hat = m / (1.0 - ADAM_B1 ** ADAM_STEP)
    v_hat = v / (1.0 - ADAM_B2 ** ADAM_STEP)
    delta = -ADAM_LR * (m_hat / (jnp.sqrt(v_hat) + ADAM_EPS) + ADAM_WD * w)
    return delta, m, v


def _adam_w_in(w, m, v, g3):
    rows, cols = w.shape
    tr = 128

    def body(w_ref, m_ref, v_ref, g_ref, go_ref, d_ref, mo_ref, vo_ref):
        for s in range(CHUNKS_PER_BLOCK):
            cs = slice(CHUNK * s, CHUNK * (s + 1))
            g = g_ref[s]
            d, mn, vn = _adamw(w_ref[:, cs], g, m_ref[:, cs], v_ref[:, cs])
            go_ref[:, cs] = g
            d_ref[:, cs] = d
            mo_ref[:, cs] = mn
            vo_ref[:, cs] = vn

    blk = pl.BlockSpec((tr, cols), lambda i: (i, 0))
    return pl.pallas_call(
        body, grid=(rows // tr,),
        in_specs=[blk, blk, blk, pl.BlockSpec((CHUNKS_PER_BLOCK, tr, CHUNK), lambda i: (0, i, 0))],
        out_specs=(blk,) * 4, out_shape=(jax.ShapeDtypeStruct(w.shape, F32),) * 4,
        compiler_params=_cp(ARB), name="adam_w_in",
    )(w, m, v, g3)


def _adam_w_out(w, m, v, g):
    rows, cols = w.shape
    tr = 128

    def body(w_ref, m_ref, v_ref, g_ref, d_ref, mo_ref, vo_ref):
        d_ref[...], mo_ref[...], vo_ref[...] = _adamw(w_ref[...], g_ref[...], m_ref[...], v_ref[...])

    blk = pl.BlockSpec((tr, cols), lambda i: (i, 0))
    return pl.pallas_call(
        body, grid=(rows // tr,), in_specs=[blk] * 4, out_specs=(blk,) * 3,
        out_shape=(jax.ShapeDtypeStruct(w.shape, F32),) * 3,
        compiler_params=_cp(ARB), name="adam_w_out",
    )(w, m, v, g)


def _adam_small(ws, ms, vs, gs):
    n = len(ws)

    def body(*refs):
        w_r, m_r, v_r, g_r = refs[0:n], refs[n:2 * n], refs[2 * n:3 * n], refs[3 * n:4 * n]
        d_o, m_o, v_o = refs[4 * n:5 * n], refs[5 * n:6 * n], refs[6 * n:7 * n]
        for j in range(n):
            d_o[j][...], m_o[j][...], v_o[j][...] = _adamw(w_r[j][...], g_r[j][...], m_r[j][...], v_r[j][...])

    vm = pl.BlockSpec(memory_space=pltpu.VMEM)
    shapes = tuple(jax.ShapeDtypeStruct(w.shape, F32) for w in ws)
    outs = pl.pallas_call(
        body, in_specs=[vm] * (4 * n), out_specs=(vm,) * (3 * n), out_shape=shapes * 3,
        compiler_params=_cp(), name="adam_small",
    )(*ws, *ms, *vs, *gs)
    return outs[0:n], outs[n:2 * n], outs[2 * n:3 * n]


def _block_diag_strips(w):
    w4 = w.reshape(NS, HEADS_PER_STRIP, LRU_HEAD, LRU_HEAD)
    bd = jnp.zeros((NS, HEADS_PER_STRIP, LRU_HEAD, HEADS_PER_STRIP, LRU_HEAD), w.dtype)
    for hh in range(HEADS_PER_STRIP):
        bd = bd.at[:, hh, :, hh, :].set(w4[:, hh])
    return bd.reshape(NS, LW, LW)


def _strip_diag_blocks(g):
    g5 = g.reshape(NS, HEADS_PER_STRIP, LRU_HEAD, HEADS_PER_STRIP, LRU_HEAD)
    return jnp.stack([g5[:, hh, :, hh, :] for hh in range(HEADS_PER_STRIP)], axis=1).reshape(NS * HEADS_PER_STRIP, LRU_HEAD, LRU_HEAD)


def kernel(x, ln_g, w_in, conv_w, lru_conv_w, lru_conv_b, w_a, b_a, w_i, b_i, lam, conv_out_g, lru_out_g, w_out, final_g, loss_target, m_ln_g, m_w_in, m_conv_w, m_lru_conv_w, m_lru_conv_b, m_w_a, m_b_a, m_w_i, m_b_i, m_lam, m_conv_out_g, m_lru_out_g, m_w_out, m_final_g, v_ln_g, v_w_in, v_conv_w, v_lru_conv_w, v_lru_conv_b, v_w_a, v_b_a, v_w_i, v_b_i, v_lam, v_conv_out_g, v_lru_out_g, v_w_out, v_final_g):
    xi, yi, ci = lax.axis_index("x"), lax.axis_index("y"), lax.axis_index("c")
    k = 2 * xi + yi
    t = x.shape[1]
    x2 = x.reshape(t, D_MODEL)
    tgt2 = loss_target.reshape(t, D_MODEL)
    row = lambda a: a.reshape(1, -1)

    small = jnp.concatenate([conv_w, lru_conv_w, jnp.zeros((1, conv_w.shape[1]), F32)], axis=0)
    w12, wo4, sm4 = _allgather_weights(w_in, w_out, small)
    wo = wo4.reshape(2 * D_PART, D_MODEL)
    convs = jnp.transpose(sm4, (1, 0, 2)).reshape(SUBLANES, D_PART)
    pvec = jnp.concatenate(
        [convs[0:7], row(lru_conv_b), row(b_a), row(b_i), row(lam), row(conv_out_g), row(lru_out_g),
         jnp.zeros((PV_ROWS - N_ACC, D_PART), F32)], axis=0)
    wai = jnp.concatenate([_block_diag_strips(w_a), _block_diag_strips(w_i)], axis=2).astype(MXU_DTYPE)

    proj, xn = _in_projection(x2, row(ln_g), w12)
    yc, yl, h = _mixer_forward(proj, pvec, wai)
    do, dob, dy, st_out = _out_projection_loss(yc, yl, x2, tgt2, wo, row(final_g))
    dps, g_wai, svec = _mixer_backward(proj, h, dy, pvec, wai)
    grad_x, st_in = _input_grad(dps, w12, x2, do, row(ln_g))
    g12 = _w_in_grad(xn, dps)
    go4 = _w_out_grad(yc, yl, dob)

    gwa = _strip_diag_blocks(g_wai[:, :, 0:LW]).reshape(LRU_HEAD, D_PART)
    gwi = _strip_diag_blocks(g_wai[:, :, LW:2 * LW]).reshape(LRU_HEAD, D_PART)
    red = _allreduce_small(jnp.concatenate([svec, st_out, st_in, gwa, gwi], axis=0))
    r_out = PV_ROWS
    r_in = PV_ROWS + SUBLANES
    r_wa = PV_ROWS + 2 * SUBLANES
    r_wi = r_wa + LRU_HEAD
    loss = red[r_out + 1, 0]

    c_arr = jnp.reshape(ci, (1,)).astype(jnp.int32)
    kc_arr = jnp.stack([k, ci]).astype(jnp.int32)
    ri, ro = _exchange_sibling_halves(g12, go4)
    s_in = _add_own_half(g12, ri, c_arr, "add_own_half_in")
    s_out = _add_own_half(go4, ro, c_arr, "add_own_half_out")
    r2i, r2o = _exchange_chip_blocks(s_in, s_out)
    f_in = _sum_chip_blocks(s_in, r2i, kc_arr, CHUNKS_PER_BLOCK, "sum_chip_blocks_in")
    f_out = _sum_chip_blocks(s_out, r2o.reshape(3, 1, *r2o.shape[1:]), kc_arr, 1, "sum_chip_blocks_out")
    f_in, f_out = _swap_sibling_halves(f_in, f_out)

    g_w_in, d_w_in, nm_w_in, nv_w_in = _adam_w_in(w_in, m_w_in, v_w_in, f_in)
    g_w_out = f_out[0]
    d_w_out, nm_w_out, nv_w_out = _adam_w_out(w_out, m_w_out, v_w_out, g_w_out)

    ncol = conv_w.shape[1]
    conv_cols = lax.dynamic_slice(red, (0, k * ncol), (SUBLANES, ncol))
    g_small = {
        "ln_g": red[r_in], "conv_w": conv_cols[0:3], "lru_conv_w": conv_cols[3:7], "lru_conv_b": red[PV_LRU_B],
        "w_a": red[r_wa:r_wa + LRU_HEAD].reshape(w_a.shape), "b_a": red[PV_BA],
        "w_i": red[r_wi:r_wi + LRU_HEAD].reshape(w_i.shape), "b_i": red[PV_BI], "lam": red[PV_LAM],
        "conv_out_g": red[PV_CG], "lru_out_g": red[PV_LG], "final_g": red[r_out],
    }
    w_small = {"ln_g": ln_g, "conv_w": conv_w, "lru_conv_w": lru_conv_w, "lru_conv_b": lru_conv_b, "w_a": w_a, "b_a": b_a,
               "w_i": w_i, "b_i": b_i, "lam": lam, "conv_out_g": conv_out_g, "lru_out_g": lru_out_g, "final_g": final_g}
    m_small = {"ln_g": m_ln_g, "conv_w": m_conv_w, "lru_conv_w": m_lru_conv_w, "lru_conv_b": m_lru_conv_b, "w_a": m_w_a,
               "b_a": m_b_a, "w_i": m_w_i, "b_i": m_b_i, "lam": m_lam, "conv_out_g": m_conv_out_g,
               "lru_out_g": m_lru_out_g, "final_g": m_final_g}
    v_small = {"ln_g": v_ln_g, "conv_w": v_conv_w, "lru_conv_w": v_lru_conv_w, "lru_conv_b": v_lru_conv_b, "w_a": v_w_a,
               "b_a": v_b_a, "w_i": v_w_i, "b_i": v_b_i, "lam": v_lam, "conv_out_g": v_conv_out_g,
               "lru_out_g": v_lru_out_g, "final_g": v_final_g}
    names = list(w_small)
    as2d = lambda a: a.reshape(1, -1) if a.ndim == 1 else a
    d_s, m_s, v_s = _adam_small([as2d(w_small[n]) for n in names], [as2d(m_small[n]) for n in names],
                                [as2d(v_small[n]) for n in names], [as2d(g_small[n]) for n in names])
    back = lambda n, a: a.reshape(w_small[n].shape)
    grads = {n: g_small[n] for n in names}
    deltas = {n: back(n, a) for n, a in zip(names, d_s)}
    new_m = {n: back(n, a) for n, a in zip(names, m_s)}
    new_v = {n: back(n, a) for n, a in zip(names, v_s)}
    grads["w_in"], deltas["w_in"], new_m["w_in"], new_v["w_in"] = g_w_in, d_w_in, nm_w_in, nv_w_in
    grads["w_out"], deltas["w_out"], new_m["w_out"], new_v["w_out"] = g_w_out, d_w_out, nm_w_out, nv_w_out

    order = ["ln_g", "w_in", "conv_w", "lru_conv_w", "lru_conv_b", "w_a", "b_a", "w_i", "b_i", "lam", "conv_out_g",
             "lru_out_g", "w_out", "final_g"]
    return (loss, grad_x.reshape(x.shape), *[grads[n] for n in order], *[deltas[n] for n in order],
            *[new_m[n] for n in order], *[new_v[n] for n in order])
```

```python
import functools

import jax
import jax.numpy as jnp
from jax import lax
from jax.experimental import pallas as pl
from jax.experimental.pallas import tpu as pltpu

F32 = jnp.float32
MXU_DTYPE = jnp.bfloat16

D_MODEL = 1024
D_PART = 1024
N_PARTS = 6
CHUNK = 512
CHUNKS_PER_BLOCK = 3
N_CHUNKS = 12
N_CHIPS = 4
SUBLANES = 8
LANES = 128
LW = 256
UNROLL = 8
NS = D_PART // LW
CONV_HEAD = 128
LRU_HEAD = 64
HEADS_PER_STRIP = LW // LRU_HEAD
RMS_EPS = 1e-6
RG_LRU_C = 8.0
ADAM_LR = 0.001
ADAM_B1 = 0.9
ADAM_B2 = 0.999
ADAM_EPS = 1e-08
ADAM_WD = 0.01
ADAM_STEP = 10

PV_CONV_W = 0
PV_LRU_W = 3
PV_LRU_B = 7
PV_BA = 8
PV_BI = 9
PV_LAM = 10
PV_CG = 11
PV_LG = 12
PV_ROWS = 16
N_ACC = 13

MESH = pl.DeviceIdType.MESH
VMEM_LIMIT = 48 * 1024 * 1024
ARB = "arbitrary"


def _cp(*sem, **kw):
    return pltpu.CompilerParams(dimension_semantics=sem or None, vmem_limit_bytes=VMEM_LIMIT, **kw)


def _mm(a, b):
    return jnp.dot(a, b, preferred_element_type=F32)


def _mm_nt(a, b):
    return lax.dot_general(a, b, (((1,), (1,)), ((), ())), preferred_element_type=F32)


def _mm_tn(a, b):
    return lax.dot_general(a, b, (((0,), (0,)), ((), ())), preferred_element_type=F32)


def _sigmoid(x):
    return 1.0 / (1.0 + jnp.exp(-x))


def _one_minus_exp(x, e):
    em1 = jnp.where(e == 1.0, x, (e - 1.0) * x / jnp.log(e))
    return jnp.where(x > -0.5, -em1, 1.0 - e)


def _log_sigmoid(x):
    z = jnp.exp(-jnp.abs(x))
    u = 1.0 + z
    log1p = jnp.where(u == 1.0, z, jnp.log(u) * z / (u - 1.0))
    return jnp.minimum(x, 0.0) - log1p


def _head_mean(z, head):
    out = []
    for k in range(z.shape[1] // LANES):
        zk = z[:, LANES * k:LANES * (k + 1)]
        if head == LANES:
            m = jnp.sum(zk, axis=-1, keepdims=True) * (1.0 / head)
            out.append(jnp.broadcast_to(m, zk.shape))
        else:
            lo = lax.broadcasted_iota(jnp.int32, zk.shape, 1) < head
            s_lo = jnp.sum(jnp.where(lo, zk, 0.0), axis=-1, keepdims=True)
            s_hi = jnp.sum(jnp.where(lo, 0.0, zk), axis=-1, keepdims=True)
            out.append(jnp.where(lo, s_lo, s_hi) * (1.0 / head))
    return jnp.concatenate(out, axis=1)


def _shift_down(cur, prev, d, row):
    return jnp.where(row >= d, pltpu.roll(cur, d, 0), pltpu.roll(prev, d, 0))


def _shift_up(cur, nxt, d, row):
    return jnp.where(row < SUBLANES - d, pltpu.roll(cur, SUBLANES - d, 0), pltpu.roll(nxt, SUBLANES - d, 0))


def _scan8_fwd(a, b, row):
    A, B = a, b
    for d in (1, 2, 4):
        m = row >= d
        a_s = jnp.where(m, pltpu.roll(A, d, 0), 1.0)
        b_s = jnp.where(m, pltpu.roll(B, d, 0), 0.0)
        B = A * b_s + B
        A = A * a_s
    return A, B


def _scan8_rev(a, b, row):
    A, B = a, b
    for d in (1, 2, 4):
        m = row < SUBLANES - d
        a_s = jnp.where(m, pltpu.roll(A, SUBLANES - d, 0), 1.0)
        b_s = jnp.where(m, pltpu.roll(B, SUBLANES - d, 0), 0.0)
        B = A * b_s + B
        A = A * a_s
    return A, B


def _gates(ra, ia, u, lsb):
    r = _sigmoid(ra)
    ig = _sigmoid(ia)
    la = (RG_LRU_C * r) * lsb
    a = jnp.exp(la)
    e2 = jnp.exp(2.0 * la)
    mult = jnp.sqrt(_one_minus_exp(2.0 * la, e2))
    return r, ig, a, e2, mult


def _mesh_pos():
    x, y, c = lax.axis_index("x"), lax.axis_index("y"), lax.axis_index("c")
    chips = [(1 - x, y), (x, 1 - y), (1 - x, 1 - y)]
    return x, y, c, chips


def _allgather_weights(w_in, w_out, small):
    half_i = w_in.shape[0] // 2
    half_o = w_out.shape[0] // 2

    def body(wi_ref, wo_ref, sm_ref, w12_ref, wo4_ref, sm4_ref, send_sems, recv_sems):
        x, y, c, chips = _mesh_pos()
        k = 2 * x + y
        sib = (x, y, 1 - c)
        for s in range(CHUNKS_PER_BLOCK):
            w12_ref[CHUNKS_PER_BLOCK * k + s] = wi_ref[:, CHUNK * s:CHUNK * (s + 1)].astype(MXU_DTYPE)
        wo4_ref[k] = wo_ref[...].astype(MXU_DTYPE)
        sm4_ref[k] = sm_ref[...]

        def in_half(chip, core):
            kk = 2 * chip[0] + chip[1]
            return w12_ref.at[pl.ds(CHUNKS_PER_BLOCK * kk, CHUNKS_PER_BLOCK), pl.ds(pl.multiple_of(half_i * core, half_i), half_i), :]

        def out_half(chip, core):
            kk = 2 * chip[0] + chip[1]
            return wo4_ref.at[kk, pl.ds(pl.multiple_of(half_o * core, half_o), half_o), :]

        def copy(ref, sem, to):
            return pltpu.make_async_remote_copy(src_ref=ref, dst_ref=ref, send_sem=send_sems.at[sem],
                                                recv_sem=recv_sems.at[sem], device_id=to, device_id_type=MESH)

        me = (x, y)
        first = []
        for m, chip in enumerate(chips):
            first.append(copy(in_half(me, c), m, (*chip, c)))
            first.append(copy(out_half(me, c), 3 + m, (*chip, c)))
            first.append(copy(sm4_ref.at[k], 6 + m, (*chip, c)))
        for cp in first:
            cp.start()
        passed = []
        for m, chip in enumerate(chips):
            copy(in_half(chip, c), m, sib).wait_recv()
            fwd = copy(in_half(chip, c), 9 + m, sib)
            fwd.start()
            passed.append(fwd)
            copy(out_half(chip, c), 3 + m, sib).wait_recv()
            fwd = copy(out_half(chip, c), 12 + m, sib)
            fwd.start()
            passed.append(fwd)
            kk = 2 * chip[0] + chip[1]
            copy(sm4_ref.at[kk], 6 + m, sib).wait_recv()
        for m, chip in enumerate(chips):
            copy(in_half(chip, 1 - c), 9 + m, sib).wait_recv()
            copy(out_half(chip, 1 - c), 12 + m, sib).wait_recv()
        for cp in first + passed:
            cp.wait_send()

    vm = pl.BlockSpec(memory_space=pltpu.VMEM)
    return pl.pallas_call(
        body,
        out_shape=(jax.ShapeDtypeStruct((N_CHUNKS, w_in.shape[0], CHUNK), MXU_DTYPE),
                   jax.ShapeDtypeStruct((N_CHIPS,) + w_out.shape, MXU_DTYPE),
                   jax.ShapeDtypeStruct((N_CHIPS,) + small.shape, F32)),
        in_specs=[vm, vm, vm], out_specs=(vm, vm, vm),
        scratch_shapes=[pltpu.SemaphoreType.DMA((15,)), pltpu.SemaphoreType.DMA((15,))],
        compiler_params=_cp(), name="allgather_weights",
    )(w_in, w_out, small)


def _allreduce_small(buf):
    def body(in_ref, out_ref, r0, r1, r2, send_sems, recv_sems):
        x, y, c, _ = _mesh_pos()
        peers = [(x, y, 1 - c), (1 - x, y, c), (x, 1 - y, c)]
        out_ref[...] = in_ref[...]
        for ph, (peer, rbuf) in enumerate(zip(peers, (r0, r1, r2))):
            cp = pltpu.make_async_remote_copy(src_ref=out_ref, dst_ref=rbuf, send_sem=send_sems.at[ph],
                                              recv_sem=recv_sems.at[ph], device_id=peer, device_id_type=MESH)
            cp.start()
            cp.wait()
            out_ref[...] = out_ref[...] + rbuf[...]

    vm = pl.BlockSpec(memory_space=pltpu.VMEM)
    return pl.pallas_call(
        body, out_shape=jax.ShapeDtypeStruct(buf.shape, F32), in_specs=[vm], out_specs=vm,
        scratch_shapes=[pltpu.VMEM(buf.shape, F32)] * 3 + [pltpu.SemaphoreType.DMA((3,)), pltpu.SemaphoreType.DMA((3,))],
        compiler_params=_cp(), name="allreduce_small",
    )(buf)


def _exchange_sibling_halves(g12, go4):
    hi, ho = g12.shape[1] // 2, go4.shape[1] // 2

    def body(g12_ref, go4_ref, ri_ref, ro_ref, send_sems, recv_sems):
        x, y, c, _ = _mesh_pos()
        sib = (x, y, 1 - c)
        cps = [
            pltpu.make_async_remote_copy(src_ref=g12_ref.at[:, pl.ds(pl.multiple_of(hi * (1 - c), hi), hi), :], dst_ref=ri_ref,
                                         send_sem=send_sems.at[0], recv_sem=recv_sems.at[0], device_id=sib, device_id_type=MESH),
            pltpu.make_async_remote_copy(src_ref=go4_ref.at[:, pl.ds(pl.multiple_of(ho * (1 - c), ho), ho), :], dst_ref=ro_ref,
                                         send_sem=send_sems.at[1], recv_sem=recv_sems.at[1], device_id=sib, device_id_type=MESH),
        ]
        for cp in cps:
            cp.start()
        for cp in cps:
            cp.wait()

    hbm = pl.BlockSpec(memory_space=pl.ANY)
    return pl.pallas_call(
        body,
        out_shape=(jax.ShapeDtypeStruct((g12.shape[0], hi, g12.shape[2]), F32),
                   jax.ShapeDtypeStruct((go4.shape[0], ho, go4.shape[2]), F32)),
        in_specs=[hbm, hbm], out_specs=(hbm, hbm),
        scratch_shapes=[pltpu.SemaphoreType.DMA((2,)), pltpu.SemaphoreType.DMA((2,))],
        compiler_params=_cp(), name="exchange_sibling_halves",
    )(g12, go4)


def _add_own_half(g, r, c_arr, name):
    n, rr, cc = r.shape

    def body(c_ref, g_ref, r_ref, o_ref):
        o_ref[...] = g_ref[...] + r_ref[...]

    return pl.pallas_call(
        body, out_shape=jax.ShapeDtypeStruct(r.shape, F32),
        grid_spec=pltpu.PrefetchScalarGridSpec(
            num_scalar_prefetch=1, grid=(n,),
            in_specs=[pl.BlockSpec((1, rr, cc), lambda q, c_ref: (q, c_ref[0], 0)),
                      pl.BlockSpec((1, rr, cc), lambda q, c_ref: (q, 0, 0))],
            out_specs=pl.BlockSpec((1, rr, cc), lambda q, c_ref: (q, 0, 0))),
        compiler_params=_cp(ARB), name=name,
    )(c_arr, g, r)


def _exchange_chip_blocks(s_in, s_out):
    def body(si_ref, so_ref, ri_ref, ro_ref, send_sems, recv_sems):
        x, y, c, chips = _mesh_pos()
        cps = []
        for m, chip in enumerate(chips):
            kk = 2 * chip[0] + chip[1]
            cps.append(pltpu.make_async_remote_copy(
                src_ref=si_ref.at[pl.ds(CHUNKS_PER_BLOCK * kk, CHUNKS_PER_BLOCK)], dst_ref=ri_ref.at[m],
                send_sem=send_sems.at[m], recv_sem=recv_sems.at[m], device_id=(*chip, c), device_id_type=MESH))
            cps.append(pltpu.make_async_remote_copy(
                src_ref=so_ref.at[kk], dst_ref=ro_ref.at[m],
                send_sem=send_sems.at[3 + m], recv_sem=recv_sems.at[3 + m], device_id=(*chip, c), device_id_type=MESH))
        for cp in cps:
            cp.start()
        for cp in cps:
            cp.wait()

    hbm = pl.BlockSpec(memory_space=pl.ANY)
    return pl.pallas_call(
        body,
        out_shape=(jax.ShapeDtypeStruct((3, CHUNKS_PER_BLOCK) + s_in.shape[1:], F32),
                   jax.ShapeDtypeStruct((3,) + s_out.shape[1:], F32)),
        in_specs=[hbm, hbm], out_specs=(hbm, hbm),
        scratch_shapes=[pltpu.SemaphoreType.DMA((6,)), pltpu.SemaphoreType.DMA((6,))],
        compiler_params=_cp(), name="exchange_chip_blocks",
    )(s_in, s_out)


def _sum_chip_blocks(s, r, kc_arr, n_sub, name):
    _, rr, cc = s.shape

    def body(kc_ref, s_ref, r_ref, o_ref):
        o_ref[...] = ((s_ref[...] + r_ref[0]) + r_ref[1]) + r_ref[2]

    return pl.pallas_call(
        body, out_shape=jax.ShapeDtypeStruct((n_sub, 2 * rr, cc), F32),
        grid_spec=pltpu.PrefetchScalarGridSpec(
            num_scalar_prefetch=1, grid=(n_sub,),
            in_specs=[pl.BlockSpec((1, rr, cc), lambda q, kc: (n_sub * kc[0] + q, 0, 0)),
                      pl.BlockSpec((3, 1, rr, cc), lambda q, kc: (0, q, 0, 0))],
            out_specs=pl.BlockSpec((1, rr, cc), lambda q, kc: (q, kc[1], 0))),
        compiler_params=_cp(ARB), name=name,
    )(kc_arr, s, r)


def _swap_sibling_halves(f_in, f_out):
    hi, ho = f_in.shape[1] // 2, f_out.shape[1] // 2

    def body(fi_in, fo_in, fi_ref, fo_ref, send_sems, recv_sems):
        del fi_in, fo_in
        x, y, c, _ = _mesh_pos()
        sib = (x, y, 1 - c)
        si = fi_ref.at[:, pl.ds(pl.multiple_of(hi * c, hi), hi), :]
        so = fo_ref.at[:, pl.ds(pl.multiple_of(ho * c, ho), ho), :]
        cps = [
            pltpu.make_async_remote_copy(src_ref=si, dst_ref=si, send_sem=send_sems.at[0], recv_sem=recv_sems.at[0],
                                         device_id=sib, device_id_type=MESH),
            pltpu.make_async_remote_copy(src_ref=so, dst_ref=so, send_sem=send_sems.at[1], recv_sem=recv_sems.at[1],
                                         device_id=sib, device_id_type=MESH),
        ]
        for cp in cps:
            cp.start()
        for cp in cps:
            cp.wait()

    hbm = pl.BlockSpec(memory_space=pl.ANY)
    return pl.pallas_call(
        body,
        out_shape=(jax.ShapeDtypeStruct(f_in.shape, F32), jax.ShapeDtypeStruct(f_out.shape, F32)),
        in_specs=[hbm, hbm], out_specs=(hbm, hbm), input_output_aliases={0: 0, 1: 1},
        scratch_shapes=[pltpu.SemaphoreType.DMA((2,)), pltpu.SemaphoreType.DMA((2,))],
        compiler_params=_cp(), name="swap_sibling_halves",
    )(f_in, f_out)


def _in_projection(x, ln_g, w12):
    t = x.shape[0]
    tm = 512
    wide = CHUNKS_PER_BLOCK * CHUNK

    def body(x_ref, g_ref, w_ref, proj_ref, xn_ref, xn_s):
        @pl.when(pl.program_id(1) == 0)
        def _():
            xf = x_ref[...]
            r = lax.rsqrt(jnp.mean(xf * xf, axis=-1, keepdims=True) + RMS_EPS)
            xn = ((xf * r) * g_ref[...]).astype(MXU_DTYPE)
            xn_s[...] = xn
            xn_ref[...] = xn

        xn = xn_s[...]
        for s in range(CHUNKS_PER_BLOCK):
            proj_ref[:, CHUNK * s:CHUNK * (s + 1)] = _mm(xn, w_ref[s])

    return pl.pallas_call(
        body, grid=(t // tm, N_CHIPS),
        in_specs=[pl.BlockSpec((tm, D_MODEL), lambda i, j: (i, 0)),
                  pl.BlockSpec((1, D_MODEL), lambda i, j: (0, 0)),
                  pl.BlockSpec((CHUNKS_PER_BLOCK, D_MODEL, CHUNK), lambda i, j: (j, 0, 0))],
        out_specs=(pl.BlockSpec((tm, wide), lambda i, j: (i, j)),
                   pl.BlockSpec((tm, D_MODEL), lambda i, j: (i, 0))),
        out_shape=(jax.ShapeDtypeStruct((t, N_CHUNKS * CHUNK), F32), jax.ShapeDtypeStruct((t, D_MODEL), MXU_DTYPE)),
        scratch_shapes=[pltpu.VMEM((tm, D_MODEL), MXU_DTYPE)],
        compiler_params=_cp(ARB, ARB), name="in_projection",
    )(x, ln_g, w12)


def _out_projection_loss(yc, yl, x, target, wo, final_g):
    t = x.shape[0]
    tm = 256

    def body(yc_ref, yl_ref, x_ref, t_ref, wo_ref, fg_ref, do_ref, dob_ref, dy_ref, st_ref):
        @pl.when(pl.program_id(0) == 0)
        def _():
            st_ref[...] = jnp.zeros_like(st_ref)

        o = x_ref[...] + (_mm(yc_ref[...], wo_ref[0:D_PART, :]) + _mm(yl_ref[...], wo_ref[D_PART:2 * D_PART, :]))
        r2 = lax.rsqrt(jnp.mean(o * o, axis=-1, keepdims=True) + RMS_EPS)
        ohat = o * r2
        fg = fg_ref[...]
        diff = ohat * fg - t_ref[...]
        dout = diff * (1.0 / D_MODEL)
        gp = dout * fg
        do = r2 * (gp - ohat * jnp.mean(gp * ohat, axis=-1, keepdims=True))
        do_ref[...] = do
        dob = do.astype(MXU_DTYPE)
        dob_ref[...] = dob
        dy_ref[...] = _mm_nt(dob, wo_ref[...])
        st_ref[0:1, :] += jnp.sum(dout * ohat, axis=0, keepdims=True)
        loss = 0.5 * jnp.sum(jnp.sum(diff * diff, axis=-1, keepdims=True) * (1.0 / D_MODEL), axis=0, keepdims=True)
        st_ref[1:2, :] += jnp.broadcast_to(loss, (1, D_MODEL))

    row = lambda i: (i, 0)
    fix = lambda i: (0, 0)
    return pl.pallas_call(
        body, grid=(t // tm,),
        in_specs=[pl.BlockSpec((tm, D_PART), row), pl.BlockSpec((tm, D_PART), row),
                  pl.BlockSpec((tm, D_MODEL), row), pl.BlockSpec((tm, D_MODEL), row),
                  pl.BlockSpec((2 * D_PART, D_MODEL), fix), pl.BlockSpec((1, D_MODEL), fix)],
        out_specs=(pl.BlockSpec((tm, D_MODEL), row), pl.BlockSpec((tm, D_MODEL), row),
                   pl.BlockSpec((tm, 2 * D_PART), row), pl.BlockSpec((SUBLANES, D_MODEL), fix)),
        out_shape=(jax.ShapeDtypeStruct((t, D_MODEL), F32), jax.ShapeDtypeStruct((t, D_MODEL), MXU_DTYPE),
                   jax.ShapeDtypeStruct((t, 2 * D_PART), F32), jax.ShapeDtypeStruct((SUBLANES, D_MODEL), F32)),
        compiler_params=_cp(ARB), name="out_projection_loss",
    )(yc, yl, x, target, wo, final_g)


def _input_grad(dps, w12, x, do, ln_g):
    t = x.shape[0]
    tm = 512

    def body(*refs):
        dp_refs = refs[:N_PARTS]
        w_ref, x_ref, do_ref, g_ref, gx_ref, st_ref, acc = refs[N_PARTS:]
        i, p = pl.program_id(0), pl.program_id(1)

        @pl.when((i == 0) & (p == 0))
        def _():
            st_ref[...] = jnp.zeros_like(st_ref)

        @pl.when(p == 0)
        def _():
            acc[...] = jnp.zeros_like(acc)

        for q in range(N_PARTS):
            @pl.when(p == q)
            def _(q=q):
                acc[...] += (_mm_nt(dp_refs[q][:, 0:CHUNK], w_ref[0]) + _mm_nt(dp_refs[q][:, CHUNK:2 * CHUNK], w_ref[1]))

        @pl.when(p == N_PARTS - 1)
        def _():
            xf = x_ref[...]
            r = lax.rsqrt(jnp.mean(xf * xf, axis=-1, keepdims=True) + RMS_EPS)
            xhat = xf * r
            dxn = acc[...]
            st_ref[0:1, :] += jnp.sum(dxn * xhat, axis=0, keepdims=True)
            dxh = dxn * g_ref[...]
            gx_ref[...] = do_ref[...] + r * (dxh - xhat * jnp.mean(dxh * xhat, axis=-1, keepdims=True))

    row = lambda i, p: (i, 0)
    fix = lambda i, p: (0, 0)
    return pl.pallas_call(
        body, grid=(t // tm, N_PARTS),
        in_specs=[pl.BlockSpec((tm, D_PART), row)] * N_PARTS + [
            pl.BlockSpec((2, D_MODEL, CHUNK), lambda i, p: (p, 0, 0)),
            pl.BlockSpec((tm, D_MODEL), row), pl.BlockSpec((tm, D_MODEL), row), pl.BlockSpec((1, D_MODEL), fix)],
        out_specs=(pl.BlockSpec((tm, D_MODEL), row), pl.BlockSpec((SUBLANES, D_MODEL), fix)),
        out_shape=(jax.ShapeDtypeStruct((t, D_MODEL), F32), jax.ShapeDtypeStruct((SUBLANES, D_MODEL), F32)),
        scratch_shapes=[pltpu.VMEM((tm, D_MODEL), F32)],
        compiler_params=_cp(ARB, ARB), name="input_grad",
    )(*dps, w12, x, do, ln_g)


def _w_in_grad(xn, dps):
    t = xn.shape[0]
    tk = 512

    def body(*refs):
        xn_ref = refs[0]
        dp_refs = refs[1:1 + N_PARTS]
        o_ref = refs[1 + N_PARTS]
        p, kk = pl.program_id(0), pl.program_id(1)

        @pl.when(kk == 0)
        def _():
            o_ref[...] = jnp.zeros_like(o_ref)

        for q in range(N_PARTS):
            @pl.when(p == q)
            def _(q=q):
                xnv = xn_ref[...]
                o_ref[0] += _mm_tn(xnv, dp_refs[q][:, 0:CHUNK])
                o_ref[1] += _mm_tn(xnv, dp_refs[q][:, CHUNK:2 * CHUNK])

    def dp_map(q):
        return lambda p, kk: (jnp.where(p == q, kk, 0), 0)

    return pl.pallas_call(
        body, grid=(N_PARTS, t // tk),
        in_specs=[pl.BlockSpec((tk, D_MODEL), lambda p, kk: (kk, 0))] + [pl.BlockSpec((tk, D_PART), dp_map(q)) for q in range(N_PARTS)],
        out_specs=pl.BlockSpec((2, D_MODEL, CHUNK), lambda p, kk: (p, 0, 0)),
        out_shape=jax.ShapeDtypeStruct((N_CHUNKS, D_MODEL, CHUNK), F32),
        compiler_params=_cp(ARB, ARB), name="w_in_grad",
    )(xn, *dps)


def _w_out_grad(yc, yl, dob):
    t = yc.shape[0]
    tk = 512

    def body(yc_ref, yl_ref, do_ref, o_ref):
        @pl.when(pl.program_id(0) == 0)
        def _():
            o_ref[...] = jnp.zeros_like(o_ref)

        dov = do_ref[...]
        o_ref[0:D_PART, :] += _mm_tn(yc_ref[...], dov)
        o_ref[D_PART:2 * D_PART, :] += _mm_tn(yl_ref[...], dov)

    row = lambda kk: (kk, 0)
    out = pl.pallas_call(
        body, grid=(t // tk,),
        in_specs=[pl.BlockSpec((tk, D_PART), row), pl.BlockSpec((tk, D_PART), row), pl.BlockSpec((tk, D_MODEL), row)],
        out_specs=pl.BlockSpec((2 * D_PART, D_MODEL), lambda kk: (0, 0)),
        out_shape=jax.ShapeDtypeStruct((2 * D_PART, D_MODEL), F32),
        compiler_params=_cp(ARB), name="w_out_grad",
    )(yc, yl, dob)
    return out.reshape(N_CHIPS, 2 * D_PART // N_CHIPS, D_MODEL)


def _for_groups(n, fn, init):
    def trip(j, carry):
        for uu in range(UNROLL):
            carry = fn(j * UNROLL + uu, carry)
        return carry

    return lax.fori_loop(0, n // UNROLL, trip, init)


def _pvb(pv_ref, r):
    return jnp.broadcast_to(pv_ref[r:r + 1, :], (SUBLANES, pv_ref.shape[1]))


def _conv3(pv_ref, u, u1, u2):
    return (_pvb(pv_ref, PV_CONV_W) * u2 + _pvb(pv_ref, PV_CONV_W + 1) * u1) + _pvb(pv_ref, PV_CONV_W + 2) * u


def _conv4(pv_ref, v, v1, v2, v3):
    return ((((_pvb(pv_ref, PV_LRU_W) * v3 + _pvb(pv_ref, PV_LRU_W + 1) * v2) + _pvb(pv_ref, PV_LRU_W + 2) * v1)
             + _pvb(pv_ref, PV_LRU_W + 3) * v) + _pvb(pv_ref, PV_LRU_B))


def _mixer_forward(proj, pvec, wai):
    t = proj.shape[0]
    tb = 512
    ng = tb // SUBLANES
    nt = t // tb

    def body(bg_ref, cg_ref, xc_ref, gc_ref, xl_ref, gl_ref, pv_ref, wai_ref,
             yc_ref, yl_ref, h_ref,
             ucp_s, xlp_s, ls_s, hbuf_s, u_s, gate_s, zc_s, zl_s):
        @pl.when(pl.program_id(1) == 0)
        def _():
            ucp_s[...] = jnp.zeros_like(ucp_s)
            xlp_s[...] = jnp.zeros_like(xlp_s)
            hbuf_s[...] = jnp.zeros_like(hbuf_s)

        row = lax.broadcasted_iota(jnp.int32, (SUBLANES, LW), 0)
        ls_s[...] = _log_sigmoid(_pvb(pv_ref, PV_LAM))

        def conv_group(g, carry):
            ucp, xlp = carry
            sl = pl.ds(pl.multiple_of(g * SUBLANES, SUBLANES), SUBLANES)
            uc = cg_ref[sl, :] * xc_ref[sl, :]
            v = _conv3(pv_ref, uc, _shift_down(uc, ucp, 1, row), _shift_down(uc, ucp, 2, row))
            yc = bg_ref[sl, :] * v
            rr = lax.rsqrt(_head_mean(yc * yc, CONV_HEAD) + RMS_EPS)
            gc = gc_ref[sl, :]
            zc_s[sl, :] = ((yc * rr) * _pvb(pv_ref, PV_CG)) * (gc * _sigmoid(gc))
            xl = xl_ref[sl, :]
            u_s[sl, :] = _conv4(pv_ref, xl, _shift_down(xl, xlp, 1, row), _shift_down(xl, xlp, 2, row),
                                _shift_down(xl, xlp, 3, row))
            return uc, xl

        ucp, xlp = _for_groups(ng, conv_group, (ucp_s[...], xlp_s[...]))
        ucp_s[...] = ucp
        xlp_s[...] = xlp

        gate_s[...] = _mm(u_s[...].astype(MXU_DTYPE), wai_ref[0])

        def lru_group(g, h_before):
            sl = pl.ds(pl.multiple_of(g * SUBLANES, SUBLANES), SUBLANES)
            u = u_s[sl, :]
            r, ig, a, e2, mult = _gates(gate_s[sl, 0:LW] + _pvb(pv_ref, PV_BA), gate_s[sl, LW:2 * LW] + _pvb(pv_ref, PV_BI),
                                        u, ls_s[...])
            A, B = _scan8_fwd(a, mult * (ig * u), row)
            h = B + A * jnp.broadcast_to(h_before[SUBLANES - 1:SUBLANES, :], (SUBLANES, LW))
            h_ref[sl, :] = h
            rr = lax.rsqrt(_head_mean(h * h, LRU_HEAD) + RMS_EPS)
            gl = gl_ref[sl, :]
            zl_s[sl, :] = ((h * rr) * _pvb(pv_ref, PV_LG)) * (gl * _sigmoid(gl))
            return h

        hbuf_s[...] = _for_groups(ng, lru_group, hbuf_s[...])
        yc_ref[...] = zc_s[...].astype(MXU_DTYPE)
        yl_ref[...] = zl_s[...].astype(MXU_DTYPE)

    def part(p):
        return pl.BlockSpec((tb, LW), lambda c, i: (i, p * NS + c))

    strip = pl.BlockSpec((tb, LW), lambda c, i: (i, c))
    return pl.pallas_call(
        body, grid=(NS, nt),
        in_specs=[part(p) for p in range(N_PARTS)] + [
            pl.BlockSpec((PV_ROWS, LW), lambda c, i: (0, c)),
            pl.BlockSpec((1, LW, 2 * LW), lambda c, i: (c, 0, 0))],
        out_specs=(strip, strip, strip),
        out_shape=(jax.ShapeDtypeStruct((t, D_PART), MXU_DTYPE), jax.ShapeDtypeStruct((t, D_PART), MXU_DTYPE),
                   jax.ShapeDtypeStruct((t, D_PART), F32)),
        scratch_shapes=[pltpu.VMEM((SUBLANES, LW), F32), pltpu.VMEM((SUBLANES, LW), F32), pltpu.VMEM((SUBLANES, LW), F32),
                        pltpu.VMEM((SUBLANES, LW), F32), pltpu.VMEM((tb, LW), F32), pltpu.VMEM((tb, 2 * LW), F32),
                        pltpu.VMEM((tb, LW), F32), pltpu.VMEM((tb, LW), F32)],
        compiler_params=_cp(ARB, ARB), name="mixer_forward",
    )(proj, proj, proj, proj, proj, proj, pvec, wai)


def _mixer_backward(proj, h, dy, pvec, wai):
    t = proj.shape[0]
    tb = 512
    ng = tb // SUBLANES
    nt = t // tb
    gpb = tb // SUBLANES

    def body(bg_ref, cg_ref, xc_ref, gc_ref, xl_ref, gl_ref, h_ref, dyc_ref, dyl_ref,
             cgh_ref, xch_ref, xlh_ref, hh_ref, pv_ref, wai_ref,
             dp0, dp1, dp2, dp3, dp4, dp5, gw_ref, sv_ref,
             ls_s, u_s, uce_s, xle_s, he_s, gate_s, dgate_s, du_s, gbuf_s,
             p0_s, p1_s, p2_s, p3_s, p4_s, p5_s, acc_s, an_s, dvn_s, dun_s):
        i = pl.program_id(1)
        first_block = i == nt - 1

        @pl.when(i == 0)
        def _():
            acc_s[...] = jnp.zeros_like(acc_s)
            gw_ref[...] = jnp.zeros_like(gw_ref)
            an_s[...] = jnp.zeros_like(an_s)
            dvn_s[...] = jnp.zeros_like(dvn_s)
            dun_s[...] = jnp.zeros_like(dun_s)
            gbuf_s[...] = jnp.zeros_like(gbuf_s)

        row = lax.broadcasted_iota(jnp.int32, (SUBLANES, LW), 0)
        ls_s[...] = _log_sigmoid(_pvb(pv_ref, PV_LAM))
        keep = jnp.where(first_block, 0.0, 1.0)
        uce_s[0:SUBLANES, :] = (cgh_ref[...] * xch_ref[...]) * keep
        xle_s[0:SUBLANES, :] = xlh_ref[...] * keep
        he_s[0:SUBLANES, :] = hh_ref[...] * keep
        xle_s[SUBLANES:SUBLANES + tb, :] = xl_ref[...]
        he_s[SUBLANES:SUBLANES + tb, :] = h_ref[...]

        def recompute_group(g, carry):
            r0 = pl.multiple_of(g * SUBLANES, SUBLANES)
            sl = pl.ds(r0, SUBLANES)
            uce_s[pl.ds(r0 + SUBLANES, SUBLANES), :] = cg_ref[sl, :] * xc_ref[sl, :]
            xl = xle_s[pl.ds(r0 + SUBLANES, SUBLANES), :]
            xlp = xle_s[sl, :]
            u_s[sl, :] = _conv4(pv_ref, xl, _shift_down(xl, xlp, 1, row), _shift_down(xl, xlp, 2, row),
                                _shift_down(xl, xlp, 3, row))
            return carry

        _for_groups(ng, recompute_group, 0)
        gate_s[...] = _mm(u_s[...].astype(MXU_DTYPE), wai_ref[0])

        def acc_add(k, v):
            acc_s[k] += v

        def main_group(gi, carry):
            a_next, dv_next, g_next = carry
            g = ng - 1 - gi
            r0 = pl.multiple_of(g * SUBLANES, SUBLANES)
            sl = pl.ds(r0, SUBLANES)
            sl_e = pl.ds(r0 + SUBLANES, SUBLANES)
            lsb = ls_s[...]
            u = u_s[sl, :]
            r, ig, a, e2, mult = _gates(gate_s[sl, 0:LW] + _pvb(pv_ref, PV_BA), gate_s[sl, LW:2 * LW] + _pvb(pv_ref, PV_BI),
                                        u, lsb)
            gl = gl_ref[sl, :]
            sg = _sigmoid(gl)
            s_l = gl * sg
            h8 = he_s[sl_e, :]
            hprev = _shift_down(h8, he_s[sl, :], 1, row)
            rr = lax.rsqrt(_head_mean(h8 * h8, LRU_HEAD) + RMS_EPS)
            n = h8 * rr
            dz = dyl_ref[sl, :]
            lg = _pvb(pv_ref, PV_LG)
            acc_add(PV_LG, (dz * n) * s_l)
            p5_s[sl, :] = ((dz * n) * lg) * (sg * (1.0 + gl * (1.0 - sg)))
            dn = (dz * lg) * s_l
            dh = rr * (dn - n * _head_mean(dn * n, LRU_HEAD))
            A, B = _scan8_rev(_shift_up(a, a_next, 1, row), dh, row)
            gg = B + A * jnp.broadcast_to(g_next[0:1, :], (SUBLANES, LW))
            da = gg * hprev
            iu = ig * u
            diu = gg * mult
            dla = da * a - (gg * iu) * (e2 / mult)
            acc_add(PV_LAM, dla * (RG_LRU_C * r))
            dra = (dla * (RG_LRU_C * lsb)) * (r * (1.0 - r))
            dia = (diu * u) * (ig * (1.0 - ig))
            dgate_s[sl, 0:LW] = dra
            dgate_s[sl, LW:2 * LW] = dia
            acc_add(PV_BA, dra)
            acc_add(PV_BI, dia)
            du_s[sl, :] = diu * ig
            bg = bg_ref[sl, :]
            gc = gc_ref[sl, :]
            uc = uce_s[sl_e, :]
            ucp = uce_s[sl, :]
            uc1 = _shift_down(uc, ucp, 1, row)
            uc2 = _shift_down(uc, ucp, 2, row)
            v = _conv3(pv_ref, uc, uc1, uc2)
            yc = bg * v
            rrc = lax.rsqrt(_head_mean(yc * yc, CONV_HEAD) + RMS_EPS)
            nc = yc * rrc
            sgc = _sigmoid(gc)
            s_c = gc * sgc
            dzc = dyc_ref[sl, :]
            cgain = _pvb(pv_ref, PV_CG)
            acc_add(PV_CG, (dzc * nc) * s_c)
            p3_s[sl, :] = ((dzc * nc) * cgain) * (sgc * (1.0 + gc * (1.0 - sgc)))
            dnc = (dzc * cgain) * s_c
            dyc = rrc * (dnc - nc * _head_mean(dnc * nc, CONV_HEAD))
            p0_s[sl, :] = dyc * v
            dv = dyc * bg
            duc = (_pvb(pv_ref, PV_CONV_W + 2) * dv + _pvb(pv_ref, PV_CONV_W + 1) * _shift_up(dv, dv_next, 1, row)
                   + _pvb(pv_ref, PV_CONV_W) * _shift_up(dv, dv_next, 2, row))
            acc_add(PV_CONV_W + 2, dv * uc)
            acc_add(PV_CONV_W + 1, dv * uc1)
            acc_add(PV_CONV_W, dv * uc2)
            p1_s[sl, :] = duc * xc_ref[sl, :]
            p2_s[sl, :] = duc * cg_ref[sl, :]
            return a, dv, gg

        a_next, dv_next, g_next = _for_groups(ng, main_group, (an_s[...], dvn_s[...], gbuf_s[...]))
        an_s[...] = a_next
        dvn_s[...] = dv_next
        gbuf_s[...] = g_next

        dgb = dgate_s[...].astype(MXU_DTYPE)
        du_s[...] += _mm_nt(dgb, wai_ref[0])
        gw_ref[0] += _mm_tn(u_s[...].astype(MXU_DTYPE), dgb)

        def lru_conv_group(gi, du_next):
            g = ng - 1 - gi
            r0 = pl.multiple_of(g * SUBLANES, SUBLANES)
            sl = pl.ds(r0, SUBLANES)
            du = du_s[sl, :]
            xl = xle_s[pl.ds(r0 + SUBLANES, SUBLANES), :]
            xlp = xle_s[sl, :]
            acc_add(PV_LRU_B, du)
            acc_add(PV_LRU_W + 3, du * xl)
            acc_add(PV_LRU_W + 2, du * _shift_down(xl, xlp, 1, row))
            acc_add(PV_LRU_W + 1, du * _shift_down(xl, xlp, 2, row))
            acc_add(PV_LRU_W, du * _shift_down(xl, xlp, 3, row))
            p4_s[sl, :] = (((_pvb(pv_ref, PV_LRU_W + 3) * du + _pvb(pv_ref, PV_LRU_W + 2) * _shift_up(du, du_next, 1, row))
                            + _pvb(pv_ref, PV_LRU_W + 1) * _shift_up(du, du_next, 2, row))
                           + _pvb(pv_ref, PV_LRU_W) * _shift_up(du, du_next, 3, row))
            return du

        dun_s[...] = _for_groups(ng, lru_conv_group, dun_s[...])

        for dp_ref, p_s in zip((dp0, dp1, dp2, dp3, dp4, dp5), (p0_s, p1_s, p2_s, p3_s, p4_s, p5_s)):
            dp_ref[...] = p_s[...].astype(MXU_DTYPE)

        @pl.when(first_block)
        def _():
            sv_ref[...] = jnp.zeros_like(sv_ref)
            for k in range(N_ACC):
                tot = jnp.sum(acc_s[k], axis=0, keepdims=True)
                if k == PV_LAM:
                    tot = tot * _sigmoid(-pv_ref[PV_LAM:PV_LAM + 1, :])
                sv_ref[k:k + 1, :] = tot

    def part(p):
        return pl.BlockSpec((tb, LW), lambda c, i: (nt - 1 - i, p * NS + c))

    def halo(p):
        return pl.BlockSpec((SUBLANES, LW), lambda c, i: (jnp.maximum((nt - 1 - i) * gpb - 1, 0), p * NS + c))

    strip = pl.BlockSpec((tb, LW), lambda c, i: (nt - 1 - i, c))
    big = pltpu.VMEM((tb, LW), F32)
    big_e = pltpu.VMEM((tb + SUBLANES, LW), F32)
    wide = pltpu.VMEM((tb, 2 * LW), F32)
    small = pltpu.VMEM((SUBLANES, LW), F32)
    outs = pl.pallas_call(
        body, grid=(NS, nt),
        in_specs=[part(p) for p in range(N_PARTS)] + [
            strip, strip, pl.BlockSpec((tb, LW), lambda c, i: (nt - 1 - i, NS + c)),
            halo(1), halo(2), halo(4),
            pl.BlockSpec((SUBLANES, LW), lambda c, i: (jnp.maximum((nt - 1 - i) * gpb - 1, 0), c)),
            pl.BlockSpec((PV_ROWS, LW), lambda c, i: (0, c)),
            pl.BlockSpec((1, LW, 2 * LW), lambda c, i: (c, 0, 0))],
        out_specs=(strip,) * N_PARTS + (pl.BlockSpec((1, LW, 2 * LW), lambda c, i: (c, 0, 0)),
                                        pl.BlockSpec((PV_ROWS, LW), lambda c, i: (0, c))),
        out_shape=(jax.ShapeDtypeStruct((t, D_PART), MXU_DTYPE),) * N_PARTS + (
            jax.ShapeDtypeStruct((NS, LW, 2 * LW), F32), jax.ShapeDtypeStruct((PV_ROWS, D_PART), F32)),
        scratch_shapes=[small, big, big_e, big_e, big_e, wide, wide, big, small,
                        big, big, big, big, big, big, pltpu.VMEM((N_ACC, SUBLANES, LW), F32), small, small, small],
        compiler_params=_cp(ARB, ARB), name="mixer_backward",
    )(proj, proj, proj, proj, proj, proj, h, dy, dy, proj, proj, proj, h, pvec, wai)
    return outs[:N_PARTS], outs[N_PARTS], outs[N_PARTS + 1]


def _adamw(w, g, m, v):
    m = ADAM_B1 * m + (1.0 - ADAM_B1) * g
    v = ADAM_B2 * v + (1.0 - ADAM_B2) * (g * g)
    m_hat = m / (1.0 - ADAM_B1 ** ADAM_STEP)
    v_hat = v / (1.0 - ADAM_B2 ** ADAM_STEP)
    delta = -ADAM_LR * (m_hat / (jnp.sqrt(v_hat) + ADAM_EPS) + ADAM_WD * w)
    return delta, m, v


def _adam_w_in(w, m, v, g3):
    rows, cols = w.shape
    tr = 128

    def body(w_ref, m_ref, v_ref, g_ref, go_ref, d_ref, mo_ref, vo_ref):
        for s in range(CHUNKS_PER_BLOCK):
            cs = slice(CHUNK * s, CHUNK * (s + 1))
            g = g_ref[s]
            d, mn, vn = _adamw(w_ref[:, cs], g, m_ref[:, cs], v_ref[:, cs])
            go_ref[:, cs] = g
            d_ref[:, cs] = d
            mo_ref[:, cs] = mn
            vo_ref[:, cs] = vn

    blk = pl.BlockSpec((tr, cols), lambda i: (i, 0))
    return pl.pallas_call(
        body, grid=(rows // tr,),
        in_specs=[blk, blk, blk, pl.BlockSpec((CHUNKS_PER_BLOCK, tr, CHUNK), lambda i: (0, i, 0))],
        out_specs=(blk,) * 4, out_shape=(jax.ShapeDtypeStruct(w.shape, F32),) * 4,
        compiler_params=_cp(ARB), name="adam_w_in",
    )(w, m, v, g3)


def _adam_w_out(w, m, v, g):
    rows, cols = w.shape
    tr = 128

    def body(w_ref, m_ref, v_ref, g_ref, d_ref, mo_ref, vo_ref):
        d_ref[...], mo_ref[...], vo_ref[...] = _adamw(w_ref[...], g_ref[...], m_ref[...], v_ref[...])

    blk = pl.BlockSpec((tr, cols), lambda i: (i, 0))
    return pl.pallas_call(
        body, grid=(rows // tr,), in_specs=[blk] * 4, out_specs=(blk,) * 3,
        out_shape=(jax.ShapeDtypeStruct(w.shape, F32),) * 3,
        compiler_params=_cp(ARB), name="adam_w_out",
    )(w, m, v, g)


def _adam_small(ws, ms, vs, gs):
    n = len(ws)

    def body(*refs):
        w_r, m_r, v_r, g_r = refs[0:n], refs[n:2 * n], refs[2 * n:3 * n], refs[3 * n:4 * n]
        d_o, m_o, v_o = refs[4 * n:5 * n], refs[5 * n:6 * n], refs[6 * n:7 * n]
        for j in range(n):
            d_o[j][...], m_o[j][...], v_o[j][...] = _adamw(w_r[j][...], g_r[j][...], m_r[j][...], v_r[j][...])

    vm = pl.BlockSpec(memory_space=pltpu.VMEM)
    shapes = tuple(jax.ShapeDtypeStruct(w.shape, F32) for w in ws)
    outs = pl.pallas_call(
        body, in_specs=[vm] * (4 * n), out_specs=(vm,) * (3 * n), out_shape=shapes * 3,
        compiler_params=_cp(), name="adam_small",
    )(*ws, *ms, *vs, *gs)
    return outs[0:n], outs[n:2 * n], outs[2 * n:3 * n]


def _block_diag_strips(w):
    w4 = w.reshape(NS, HEADS_PER_STRIP, LRU_HEAD, LRU_HEAD)
    bd = jnp.zeros((NS, HEADS_PER_STRIP, LRU_HEAD, HEADS_PER_STRIP, LRU_HEAD), w.dtype)
    for hh in range(HEADS_PER_STRIP):
        bd = bd.at[:, hh, :, hh, :].set(w4[:, hh])
    return bd.reshape(NS, LW, LW)


def _strip_diag_blocks(g):
    g5 = g.reshape(NS, HEADS_PER_STRIP, LRU_HEAD, HEADS_PER_STRIP, LRU_HEAD)
    return jnp.stack([g5[:, hh, :, hh, :] for hh in range(HEADS_PER_STRIP)], axis=1).reshape(NS * HEADS_PER_STRIP, LRU_HEAD, LRU_HEAD)


def kernel(x, ln_g, w_in, conv_w, lru_conv_w, lru_conv_b, w_a, b_a, w_i, b_i, lam, conv_out_g, lru_out_g, w_out, final_g, loss_target, m_ln_g, m_w_in, m_conv_w, m_lru_conv_w, m_lru_conv_b, m_w_a, m_b_a, m_w_i, m_b_i, m_lam, m_conv_out_g, m_lru_out_g, m_w_out, m_final_g, v_ln_g, v_w_in, v_conv_w, v_lru_conv_w, v_lru_conv_b, v_w_a, v_b_a, v_w_i, v_b_i, v_lam, v_conv_out_g, v_lru_out_g, v_w_out, v_final_g):
    xi, yi, ci = lax.axis_index("x"), lax.axis_index("y"), lax.axis_index("c")
    k = 2 * xi + yi
    t = x.shape[1]
    x2 = x.reshape(t, D_MODEL)
    tgt2 = loss_target.reshape(t, D_MODEL)
    row = lambda a: a.reshape(1, -1)

    small = jnp.concatenate([conv_w, lru_conv_w, jnp.zeros((1, conv_w.shape[1]), F32)], axis=0)
    w12, wo4, sm4 = _allgather_weights(w_in, w_out, small)
    wo = wo4.reshape(2 * D_PART, D_MODEL)
    convs = jnp.transpose(sm4, (1, 0, 2)).reshape(SUBLANES, D_PART)
    pvec = jnp.concatenate(
        [convs[0:7], row(lru_conv_b), row(b_a), row(b_i), row(lam), row(conv_out_g), row(lru_out_g),
         jnp.zeros((PV_ROWS - N_ACC, D_PART), F32)], axis=0)
    wai = jnp.concatenate([_block_diag_strips(w_a), _block_diag_strips(w_i)], axis=2).astype(MXU_DTYPE)

    proj, xn = _in_projection(x2, row(ln_g), w12)
    yc, yl, h = _mixer_forward(proj, pvec, wai)
    do, dob, dy, st_out = _out_projection_loss(yc, yl, x2, tgt2, wo, row(final_g))
    dps, g_wai, svec = _mixer_backward(proj, h, dy, pvec, wai)
    grad_x, st_in = _input_grad(dps, w12, x2, do, row(ln_g))
    g12 = _w_in_grad(xn, dps)
    go4 = _w_out_grad(yc, yl, dob)

    gwa = _strip_diag_blocks(g_wai[:, :, 0:LW]).reshape(LRU_HEAD, D_PART)
    gwi = _strip_diag_blocks(g_wai[:, :, LW:2 * LW]).reshape(LRU_HEAD, D_PART)
    red = _allreduce_small(jnp.concatenate([svec, st_out, st_in, gwa, gwi], axis=0))
    r_out = PV_ROWS
    r_in = PV_ROWS + SUBLANES
    r_wa = PV_ROWS + 2 * SUBLANES
    r_wi = r_wa + LRU_HEAD
    loss = red[r_out + 1, 0]

    c_arr = jnp.reshape(ci, (1,)).astype(jnp.int32)
    kc_arr = jnp.stack([k, ci]).astype(jnp.int32)
    ri, ro = _exchange_sibling_halves(g12, go4)
    s_in = _add_own_half(g12, ri, c_arr, "add_own_half_in")
    s_out = _add_own_half(go4, ro, c_arr, "add_own_half_out")
    r2i, r2o = _exchange_chip_blocks(s_in, s_out)
    f_in = _sum_chip_blocks(s_in, r2i, kc_arr, CHUNKS_PER_BLOCK, "sum_chip_blocks_in")
    f_out = _sum_chip_blocks(s_out, r2o.reshape(3, 1, *r2o.shape[1:]), kc_arr, 1, "sum_chip_blocks_out")
    f_in, f_out = _swap_sibling_halves(f_in, f_out)

    g_w_in, d_w_in, nm_w_in, nv_w_in = _adam_w_in(w_in, m_w_in, v_w_in, f_in)
    g_w_out = f_out[0]
    d_w_out, nm_w_out, nv_w_out = _adam_w_out(w_out, m_w_out, v_w_out, g_w_out)

    ncol = conv_w.shape[1]
    conv_cols = lax.dynamic_slice(red, (0, k * ncol), (SUBLANES, ncol))
    g_small = {
        "ln_g": red[r_in], "conv_w": conv_cols[0:3], "lru_conv_w": conv_cols[3:7], "lru_conv_b": red[PV_LRU_B],
        "w_a": red[r_wa:r_wa + LRU_HEAD].reshape(w_a.shape), "b_a": red[PV_BA],
        "w_i": red[r_wi:r_wi + LRU_HEAD].reshape(w_i.shape), "b_i": red[PV_BI], "lam": red[PV_LAM],
        "conv_out_g": red[PV_CG], "lru_out_g": red[PV_LG], "final_g": red[r_out],
    }
    w_small = {"ln_g": ln_g, "conv_w": conv_w, "lru_conv_w": lru_conv_w, "lru_conv_b": lru_conv_b, "w_a": w_a, "b_a": b_a,
               "w_i": w_i, "b_i": b_i, "lam": lam, "conv_out_g": conv_out_g, "lru_out_g": lru_out_g, "final_g": final_g}
    m_small = {"ln_g": m_ln_g, "conv_w": m_conv_w, "lru_conv_w": m_lru_conv_w, "lru_conv_b": m_lru_conv_b, "w_a": m_w_a,
               "b_a": m_b_a, "w_i": m_w_i, "b_i": m_b_i, "lam": m_lam, "conv_out_g": m_conv_out_g,
               "lru_out_g": m_lru_out_g, "final_g": m_final_g}
    v_small = {"ln_g": v_ln_g, "conv_w": v_conv_w, "lru_conv_w": v_lru_conv_w, "lru_conv_b": v_lru_conv_b, "w_a": v_w_a,
               "b_a": v_b_a, "w_i": v_w_i, "b_i": v_b_i, "lam": v_lam, "conv_out_g": v_conv_out_g,
               "lru_out_g": v_lru_out_g, "final_g": v_final_g}
    names = list(w_small)
    as2d = lambda a: a.reshape(1, -1) if a.ndim == 1 else a
    d_s, m_s, v_s = _adam_small([as2d(w_small[n]) for n in names], [as2d(m_small[n]) for n in names],
                                [as2d(v_small[n]) for n in names], [as2d(g_small[n]) for n in names])
    back = lambda n, a: a.reshape(w_small[n].shape)
    grads = {n: g_small[n] for n in names}
    deltas = {n: back(n, a) for n, a in zip(names, d_s)}
    new_m = {n: back(n, a) for n, a in zip(names, m_s)}
    new_v = {n: back(n, a) for n, a in zip(names, v_s)}
    grads["w_in"], deltas["w_in"], new_m["w_in"], new_v["w_in"] = g_w_in, d_w_in, nm_w_in, nv_w_in
    grads["w_out"], deltas["w_out"], new_m["w_out"], new_v["w_out"] = g_w_out, d_w_out, nm_w_out, nv_w_out

    order = ["ln_g", "w_in", "conv_w", "lru_conv_w", "lru_conv_b", "w_a", "b_a", "w_i", "b_i", "lam", "conv_out_g",
             "lru_out_g", "w_out", "final_g"]
    return (loss, grad_x.reshape(x.shape), *[grads[n] for n in order], *[deltas[n] for n in order],
            *[new_m[n] for n in order], *[new_v[n] for n in order])
```

```python
import functools

import jax
import jax.numpy as jnp
from jax import lax
from jax.experimental import pallas as pl
from jax.experimental.pallas import tpu as pltpu

F32 = jnp.float32
MXU_DTYPE = jnp.bfloat16

D_MODEL = 1024
D_PART = 1024
N_PARTS = 6
CHUNK = 512
CHUNKS_PER_BLOCK = 3
N_CHUNKS = 12
N_CHIPS = 4
SUBLANES = 8
LANES = 128
LW = 256
UNROLL = 8
NS = D_PART // LW
CONV_HEAD = 128
LRU_HEAD = 64
HEADS_PER_STRIP = LW // LRU_HEAD
RMS_EPS = 1e-6
RG_LRU_C = 8.0
ADAM_LR = 0.001
ADAM_B1 = 0.9
ADAM_B2 = 0.999
ADAM_EPS = 1e-08
ADAM_WD = 0.01
ADAM_STEP = 10

PV_CONV_W = 0
PV_LRU_W = 3
PV_LRU_B = 7
PV_BA = 8
PV_BI = 9
PV_LAM = 10
PV_CG = 11
PV_LG = 12
PV_ROWS = 16
N_ACC = 13

SLAB = 128
MESH = pl.DeviceIdType.MESH
VMEM_LIMIT = 56 * 1024 * 1024
ARB = "arbitrary"


def _cp(*sem, **kw):
    return pltpu.CompilerParams(dimension_semantics=sem or None, vmem_limit_bytes=VMEM_LIMIT, **kw)


def _mm(a, b):
    return jnp.dot(a, b, preferred_element_type=F32)


def _mm_nt(a, b):
    return lax.dot_general(a, b, (((1,), (1,)), ((), ())), preferred_element_type=F32)


def _mm_tn(a, b):
    return lax.dot_general(a, b, (((0,), (0,)), ((), ())), preferred_element_type=F32)


def _sigmoid(x):
    return 0.5 * jnp.tanh(0.5 * x) + 0.5


def _log_sigmoid(x):
    z = jnp.exp(-jnp.abs(x))
    u = 1.0 + z
    log1p = jnp.where(u == 1.0, z, jnp.log(u) * z / (u - 1.0))
    return jnp.minimum(x, 0.0) - log1p


def _head_mean(z, head):
    out = []
    for k in range(z.shape[1] // LANES):
        zk = z[:, LANES * k:LANES * (k + 1)]
        if head == LANES:
            m = jnp.sum(zk, axis=-1, keepdims=True) * (1.0 / head)
            out.append(jnp.broadcast_to(m, zk.shape))
        else:
            lo = lax.broadcasted_iota(jnp.int32, zk.shape, 1) < head
            s_lo = jnp.sum(jnp.where(lo, zk, 0.0), axis=-1, keepdims=True)
            s_hi = jnp.sum(jnp.where(lo, 0.0, zk), axis=-1, keepdims=True)
            out.append(jnp.where(lo, s_lo, s_hi) * (1.0 / head))
    return jnp.concatenate(out, axis=1)


def _shift_down(cur, prev, d, row):
    return pltpu.roll(jnp.where(row < SUBLANES - d, cur, prev), d, 0)


def _shift_up(cur, nxt, d, row):
    return pltpu.roll(jnp.where(row >= d, cur, nxt), SUBLANES - d, 0)


def _scan8_fwd(a, b, row):
    A, B = a, b
    for d in (1, 2, 4):
        m = row >= d
        a_s = jnp.where(m, pltpu.roll(A, d, 0), 1.0)
        b_s = jnp.where(m, pltpu.roll(B, d, 0), 0.0)
        B = A * b_s + B
        A = A * a_s
    return A, B


def _scan8_rev(a, b, row):
    A, B = a, b
    for d in (1, 2, 4):
        m = row < SUBLANES - d
        a_s = jnp.where(m, pltpu.roll(A, SUBLANES - d, 0), 1.0)
        b_s = jnp.where(m, pltpu.roll(B, SUBLANES - d, 0), 0.0)
        B = A * b_s + B
        A = A * a_s
    return A, B


def _gates(ra, ia, u, lsb):
    r = _sigmoid(ra)
    ig = _sigmoid(ia)
    la = (RG_LRU_C * r) * lsb
    a = jnp.exp(la)
    e2 = a * a
    em = -jnp.tanh(la) * (1.0 + e2)
    inv_mult = lax.rsqrt(em)
    return r, ig, a, e2, em * inv_mult, inv_mult


def _mesh_pos():
    x, y, c = lax.axis_index("x"), lax.axis_index("y"), lax.axis_index("c")
    chips = [(1 - x, y), (x, 1 - y), (1 - x, 1 - y)]
    return x, y, c, chips


def _allgather_weights(w_in, w_out, small):
    half_i = w_in.shape[0] // 2
    half_o = w_out.shape[0] // 2

    def body(wi_ref, wo_ref, sm_ref, w12_ref, wo4_ref, sm4_ref, send_sems, recv_sems):
        x, y, c, chips = _mesh_pos()
        k = 2 * x + y
        sib = (x, y, 1 - c)
        for s in range(CHUNKS_PER_BLOCK):
            w12_ref[CHUNKS_PER_BLOCK * k + s] = wi_ref[:, CHUNK * s:CHUNK * (s + 1)].astype(MXU_DTYPE)
        wo4_ref[k] = wo_ref[...].astype(MXU_DTYPE)
        sm4_ref[k] = sm_ref[...]

        def in_half(chip, core):
            kk = 2 * chip[0] + chip[1]
            return w12_ref.at[pl.ds(CHUNKS_PER_BLOCK * kk, CHUNKS_PER_BLOCK), pl.ds(pl.multiple_of(half_i * core, half_i), half_i), :]

        def out_half(chip, core):
            kk = 2 * chip[0] + chip[1]
            return wo4_ref.at[kk, pl.ds(pl.multiple_of(half_o * core, half_o), half_o), :]

        def copy(ref, sem, to):
            return pltpu.make_async_remote_copy(src_ref=ref, dst_ref=ref, send_sem=send_sems.at[sem],
                                                recv_sem=recv_sems.at[sem], device_id=to, device_id_type=MESH)

        me = (x, y)
        first = []
        for m, chip in enumerate(chips):
            first.append(copy(in_half(me, c), m, (*chip, c)))
            first.append(copy(out_half(me, c), 3 + m, (*chip, c)))
            first.append(copy(sm4_ref.at[k], 6 + m, (*chip, c)))
        for cp in first:
            cp.start()
        passed = []
        for m, chip in enumerate(chips):
            copy(in_half(chip, c), m, sib).wait_recv()
            fwd = copy(in_half(chip, c), 9 + m, sib)
            fwd.start()
            passed.append(fwd)
            copy(out_half(chip, c), 3 + m, sib).wait_recv()
            fwd = copy(out_half(chip, c), 12 + m, sib)
            fwd.start()
            passed.append(fwd)
            kk = 2 * chip[0] + chip[1]
            copy(sm4_ref.at[kk], 6 + m, sib).wait_recv()
        for m, chip in enumerate(chips):
            copy(in_half(chip, 1 - c), 9 + m, sib).wait_recv()
            copy(out_half(chip, 1 - c), 12 + m, sib).wait_recv()
        for cp in first + passed:
            cp.wait_send()

    vm = pl.BlockSpec(memory_space=pltpu.VMEM)
    return pl.pallas_call(
        body,
        out_shape=(jax.ShapeDtypeStruct((N_CHUNKS, w_in.shape[0], CHUNK), MXU_DTYPE),
                   jax.ShapeDtypeStruct((N_CHIPS,) + w_out.shape, MXU_DTYPE),
                   jax.ShapeDtypeStruct((N_CHIPS,) + small.shape, F32)),
        in_specs=[vm, vm, vm], out_specs=(vm, vm, vm),
        scratch_shapes=[pltpu.SemaphoreType.DMA((15,)), pltpu.SemaphoreType.DMA((15,))],
        compiler_params=_cp(), name="allgather_weights",
    )(w_in, w_out, small)


def _allreduce_small(buf):
    def body(in_ref, out_ref, r0, r1, r2, send_sems, recv_sems):
        x, y, c, _ = _mesh_pos()
        peers = [(x, y, 1 - c), (1 - x, y, c), (x, 1 - y, c)]
        out_ref[...] = in_ref[...]
        for ph, (peer, rbuf) in enumerate(zip(peers, (r0, r1, r2))):
            cp = pltpu.make_async_remote_copy(src_ref=out_ref, dst_ref=rbuf, send_sem=send_sems.at[ph],
                                              recv_sem=recv_sems.at[ph], device_id=peer, device_id_type=MESH)
            cp.start()
            cp.wait()
            out_ref[...] = out_ref[...] + rbuf[...]

    vm = pl.BlockSpec(memory_space=pltpu.VMEM)
    return pl.pallas_call(
        body, out_shape=jax.ShapeDtypeStruct(buf.shape, F32), in_specs=[vm], out_specs=vm,
        scratch_shapes=[pltpu.VMEM(buf.shape, F32)] * 3 + [pltpu.SemaphoreType.DMA((3,)), pltpu.SemaphoreType.DMA((3,))],
        compiler_params=_cp(), name="allreduce_small",
    )(buf)


def _exchange_sibling_halves(g12, go4):
    hi, ho = g12.shape[1] // 2, go4.shape[1] // 2

    def body(g12_ref, go4_ref, ri_ref, ro_ref, send_sems, recv_sems):
        x, y, c, _ = _mesh_pos()
        sib = (x, y, 1 - c)
        cps = [
            pltpu.make_async_remote_copy(src_ref=g12_ref.at[:, pl.ds(pl.multiple_of(hi * (1 - c), hi), hi), :], dst_ref=ri_ref,
                                         send_sem=send_sems.at[0], recv_sem=recv_sems.at[0], device_id=sib, device_id_type=MESH),
            pltpu.make_async_remote_copy(src_ref=go4_ref.at[:, pl.ds(pl.multiple_of(ho * (1 - c), ho), ho), :], dst_ref=ro_ref,
                                         send_sem=send_sems.at[1], recv_sem=recv_sems.at[1], device_id=sib, device_id_type=MESH),
        ]
        for cp in cps:
            cp.start()
        for cp in cps:
            cp.wait()

    hbm = pl.BlockSpec(memory_space=pl.ANY)
    return pl.pallas_call(
        body,
        out_shape=(jax.ShapeDtypeStruct((g12.shape[0], hi, g12.shape[2]), F32),
                   jax.ShapeDtypeStruct((go4.shape[0], ho, go4.shape[2]), F32)),
        in_specs=[hbm, hbm], out_specs=(hbm, hbm),
        scratch_shapes=[pltpu.SemaphoreType.DMA((2,)), pltpu.SemaphoreType.DMA((2,))],
        compiler_params=_cp(), name="exchange_sibling_halves",
    )(g12, go4)


def _add_own_half(g, r, c_arr, name):
    n, rr, cc = r.shape

    def body(c_ref, g_ref, r_ref, o_ref, ob_ref):
        s = g_ref[...] + r_ref[...]
        o_ref[...] = s
        ob_ref[...] = s.astype(jnp.bfloat16)

    blk = pl.BlockSpec((1, rr, cc), lambda q, c_ref: (q, 0, 0))
    return pl.pallas_call(
        body, out_shape=(jax.ShapeDtypeStruct(r.shape, F32), jax.ShapeDtypeStruct(r.shape, jnp.bfloat16)),
        grid_spec=pltpu.PrefetchScalarGridSpec(
            num_scalar_prefetch=1, grid=(n,),
            in_specs=[pl.BlockSpec((1, rr, cc), lambda q, c_ref: (q, c_ref[0], 0)), blk],
            out_specs=(blk, blk)),
        compiler_params=_cp(ARB), name=name,
    )(c_arr, g, r)


def _exchange_chip_blocks(s_in, s_out):
    def body(si_ref, so_ref, ri_ref, ro_ref, send_sems, recv_sems):
        x, y, c, chips = _mesh_pos()
        cps = []
        for m, chip in enumerate(chips):
            kk = 2 * chip[0] + chip[1]
            cps.append(pltpu.make_async_remote_copy(
                src_ref=si_ref.at[pl.ds(CHUNKS_PER_BLOCK * kk, CHUNKS_PER_BLOCK)], dst_ref=ri_ref.at[m],
                send_sem=send_sems.at[m], recv_sem=recv_sems.at[m], device_id=(*chip, c), device_id_type=MESH))
            cps.append(pltpu.make_async_remote_copy(
                src_ref=so_ref.at[kk], dst_ref=ro_ref.at[m],
                send_sem=send_sems.at[3 + m], recv_sem=recv_sems.at[3 + m], device_id=(*chip, c), device_id_type=MESH))
        for cp in cps:
            cp.start()
        for cp in cps:
            cp.wait()

    hbm = pl.BlockSpec(memory_space=pl.ANY)
    return pl.pallas_call(
        body,
        out_shape=(jax.ShapeDtypeStruct((3, CHUNKS_PER_BLOCK) + s_in.shape[1:], s_in.dtype),
                   jax.ShapeDtypeStruct((3,) + s_out.shape[1:], s_out.dtype)),
        in_specs=[hbm, hbm], out_specs=(hbm, hbm),
        scratch_shapes=[pltpu.SemaphoreType.DMA((6,)), pltpu.SemaphoreType.DMA((6,))],
        compiler_params=_cp(), name="exchange_chip_blocks",
    )(s_in, s_out)


def _sum_chip_blocks(s, r, kc_arr, n_sub, name):
    _, rr, cc = s.shape

    def body(kc_ref, s_ref, r_ref, o_ref):
        o_ref[...] = ((s_ref[...] + r_ref[0].astype(F32)) + r_ref[1].astype(F32)) + r_ref[2].astype(F32)

    return pl.pallas_call(
        body, out_shape=jax.ShapeDtypeStruct((n_sub, 2 * rr, cc), F32),
        grid_spec=pltpu.PrefetchScalarGridSpec(
            num_scalar_prefetch=1, grid=(n_sub,),
            in_specs=[pl.BlockSpec((1, rr, cc), lambda q, kc: (n_sub * kc[0] + q, 0, 0)),
                      pl.BlockSpec((3, 1, rr, cc), lambda q, kc: (0, q, 0, 0))],
            out_specs=pl.BlockSpec((1, rr, cc), lambda q, kc: (q, kc[1], 0))),
        compiler_params=_cp(ARB), name=name,
    )(kc_arr, s, r)


def _swap_sibling_halves(f_in, f_out):
    hi, ho = f_in.shape[1] // 2, f_out.shape[1] // 2

    def body(fi_in, fo_in, fi_ref, fo_ref, send_sems, recv_sems):
        del fi_in, fo_in
        x, y, c, _ = _mesh_pos()
        sib = (x, y, 1 - c)
        si = fi_ref.at[:, pl.ds(pl.multiple_of(hi * c, hi), hi), :]
        so = fo_ref.at[:, pl.ds(pl.multiple_of(ho * c, ho), ho), :]
        cps = [
            pltpu.make_async_remote_copy(src_ref=si, dst_ref=si, send_sem=send_sems.at[0], recv_sem=recv_sems.at[0],
                                         device_id=sib, device_id_type=MESH),
            pltpu.make_async_remote_copy(src_ref=so, dst_ref=so, send_sem=send_sems.at[1], recv_sem=recv_sems.at[1],
                                         device_id=sib, device_id_type=MESH),
        ]
        for cp in cps:
            cp.start()
        for cp in cps:
            cp.wait()

    hbm = pl.BlockSpec(memory_space=pl.ANY)
    return pl.pallas_call(
        body,
        out_shape=(jax.ShapeDtypeStruct(f_in.shape, F32), jax.ShapeDtypeStruct(f_out.shape, F32)),
        in_specs=[hbm, hbm], out_specs=(hbm, hbm), input_output_aliases={0: 0, 1: 1},
        scratch_shapes=[pltpu.SemaphoreType.DMA((2,)), pltpu.SemaphoreType.DMA((2,))],
        compiler_params=_cp(), name="swap_sibling_halves",
    )(f_in, f_out)


def _in_projection(x, ln_g, w12):
    t = x.shape[0]
    tm = 1024
    wide = CHUNKS_PER_BLOCK * CHUNK

    def body(x_ref, g_ref, w_ref, proj_ref, xn_ref, xn_s):
        @pl.when(pl.program_id(1) == 0)
        def _():
            def norm_slab(s, carry):
                rows = pl.ds(pl.multiple_of(s * SLAB, SLAB), SLAB)
                xf = x_ref[rows, :]
                r = lax.rsqrt(jnp.mean(xf * xf, axis=-1, keepdims=True) + RMS_EPS)
                xn = ((xf * r) * g_ref[...]).astype(MXU_DTYPE)
                xn_s[rows, :] = xn
                xn_ref[rows, :] = xn
                return carry

            lax.fori_loop(0, tm // SLAB, norm_slab, 0)

        xn = xn_s[...]
        for s in range(CHUNKS_PER_BLOCK):
            proj_ref[:, CHUNK * s:CHUNK * (s + 1)] = _mm(xn, w_ref[s])

    return pl.pallas_call(
        body, grid=(t // tm, N_CHIPS),
        in_specs=[pl.BlockSpec((tm, D_MODEL), lambda i, j: (i, 0)),
                  pl.BlockSpec((1, D_MODEL), lambda i, j: (0, 0)),
                  pl.BlockSpec((CHUNKS_PER_BLOCK, D_MODEL, CHUNK), lambda i, j: (j, 0, 0))],
        out_specs=(pl.BlockSpec((tm, wide), lambda i, j: (i, j)),
                   pl.BlockSpec((tm, D_MODEL), lambda i, j: (i, 0))),
        out_shape=(jax.ShapeDtypeStruct((t, N_CHUNKS * CHUNK), F32), jax.ShapeDtypeStruct((t, D_MODEL), MXU_DTYPE)),
        scratch_shapes=[pltpu.VMEM((tm, D_MODEL), MXU_DTYPE)],
        compiler_params=_cp(ARB, ARB), name="in_projection",
    )(x, ln_g, w12)


def _out_projection_loss(yc, yl, x, target, wo, final_g):
    t = x.shape[0]
    tm = 256

    def body(yc_ref, yl_ref, x_ref, t_ref, wo_ref, fg_ref, do_ref, dob_ref, dy_ref, st_ref):
        @pl.when(pl.program_id(0) == 0)
        def _():
            st_ref[...] = jnp.zeros_like(st_ref)

        o = x_ref[...] + (_mm(yc_ref[...], wo_ref[0:D_PART, :]) + _mm(yl_ref[...], wo_ref[D_PART:2 * D_PART, :]))
        r2 = lax.rsqrt(jnp.mean(o * o, axis=-1, keepdims=True) + RMS_EPS)
        ohat = o * r2
        fg = fg_ref[...]
        diff = ohat * fg - t_ref[...]
        dout = diff * (1.0 / D_MODEL)
        gp = dout * fg
        do = r2 * (gp - ohat * jnp.mean(gp * ohat, axis=-1, keepdims=True))
        do_ref[...] = do
        dob = do.astype(MXU_DTYPE)
        dob_ref[...] = dob
        dy_ref[...] = _mm_nt(dob, wo_ref[...])
        st_ref[0:1, :] += jnp.sum(dout * ohat, axis=0, keepdims=True)
        loss = 0.5 * jnp.sum(jnp.sum(diff * diff, axis=-1, keepdims=True) * (1.0 / D_MODEL), axis=0, keepdims=True)
        st_ref[1:2, :] += jnp.broadcast_to(loss, (1, D_MODEL))

    row = lambda i: (i, 0)
    fix = lambda i: (0, 0)
    return pl.pallas_call(
        body, grid=(t // tm,),
        in_specs=[pl.BlockSpec((tm, D_PART), row), pl.BlockSpec((tm, D_PART), row),
                  pl.BlockSpec((tm, D_MODEL), row), pl.BlockSpec((tm, D_MODEL), row),
                  pl.BlockSpec((2 * D_PART, D_MODEL), fix), pl.BlockSpec((1, D_MODEL), fix)],
        out_specs=(pl.BlockSpec((tm, D_MODEL), row), pl.BlockSpec((tm, D_MODEL), row),
                   pl.BlockSpec((tm, 2 * D_PART), row), pl.BlockSpec((SUBLANES, D_MODEL), fix)),
        out_shape=(jax.ShapeDtypeStruct((t, D_MODEL), F32), jax.ShapeDtypeStruct((t, D_MODEL), MXU_DTYPE),
                   jax.ShapeDtypeStruct((t, 2 * D_PART), F32), jax.ShapeDtypeStruct((SUBLANES, D_MODEL), F32)),
        compiler_params=_cp(ARB), name="out_projection_loss",
    )(yc, yl, x, target, wo, final_g)


def _input_grad(dproj, w12, x, do, ln_g):
    t = x.shape[0]
    tm = 1024

    def body(dp_ref, w_ref, x_ref, do_ref, g_ref, gx_ref, st_ref, acc):
        i, p = pl.program_id(0), pl.program_id(1)

        @pl.when((i == 0) & (p == 0))
        def _():
            st_ref[...] = jnp.zeros_like(st_ref)

        part = _mm_nt(dp_ref[0, :, 0:CHUNK], w_ref[0]) + _mm_nt(dp_ref[0, :, CHUNK:2 * CHUNK], w_ref[1])

        @pl.when(p == 0)
        def _():
            acc[...] = part

        @pl.when(p > 0)
        def _():
            acc[...] += part

        @pl.when(p == N_PARTS - 1)
        def _():
            def norm_bwd_slab(s, g_sum):
                rows = pl.ds(pl.multiple_of(s * SLAB, SLAB), SLAB)
                xf = x_ref[rows, :]
                r = lax.rsqrt(jnp.mean(xf * xf, axis=-1, keepdims=True) + RMS_EPS)
                xhat = xf * r
                dxn = acc[rows, :]
                dxh = dxn * g_ref[...]
                gx_ref[rows, :] = do_ref[rows, :] + r * (dxh - xhat * jnp.mean(dxh * xhat, axis=-1, keepdims=True))
                return g_sum + jnp.sum(dxn * xhat, axis=0, keepdims=True)

            st_ref[0:1, :] += lax.fori_loop(0, tm // SLAB, norm_bwd_slab, jnp.zeros((1, D_MODEL), F32))

    row = lambda i, p: (i, 0)
    fix = lambda i, p: (0, 0)
    return pl.pallas_call(
        body, grid=(t // tm, N_PARTS),
        in_specs=[
            pl.BlockSpec((1, tm, D_PART), lambda i, p: (p, i, 0)),
            pl.BlockSpec((2, D_MODEL, CHUNK), lambda i, p: (p, 0, 0)),
            pl.BlockSpec((tm, D_MODEL), row), pl.BlockSpec((tm, D_MODEL), row), pl.BlockSpec((1, D_MODEL), fix)],
        out_specs=(pl.BlockSpec((tm, D_MODEL), row), pl.BlockSpec((SUBLANES, D_MODEL), fix)),
        out_shape=(jax.ShapeDtypeStruct((t, D_MODEL), F32), jax.ShapeDtypeStruct((SUBLANES, D_MODEL), F32)),
        scratch_shapes=[pltpu.VMEM((tm, D_MODEL), F32)],
        compiler_params=_cp(ARB, ARB), name="input_grad",
    )(dproj, w12, x, do, ln_g)


def _w_in_grad(xn, dproj):
    t = xn.shape[0]
    tk = 1024

    def body(xn_ref, dp_ref, o_ref):
        kk = pl.program_id(1)
        xnv = xn_ref[...]
        parts = [_mm_tn(xnv, dp_ref[0, :, CHUNK * s:CHUNK * (s + 1)]) for s in range(2)]

        @pl.when(kk == 0)
        def _():
            for s in range(2):
                o_ref[s] = parts[s]

        @pl.when(kk > 0)
        def _():
            for s in range(2):
                o_ref[s] += parts[s]

    return pl.pallas_call(
        body, grid=(N_PARTS, t // tk),
        in_specs=[pl.BlockSpec((tk, D_MODEL), lambda p, kk: (kk, 0)),
                  pl.BlockSpec((1, tk, D_PART), lambda p, kk: (p, kk, 0))],
        out_specs=pl.BlockSpec((2, D_MODEL, CHUNK), lambda p, kk: (p, 0, 0)),
        out_shape=jax.ShapeDtypeStruct((N_CHUNKS, D_MODEL, CHUNK), F32),
        compiler_params=_cp(ARB, ARB), name="w_in_grad",
    )(xn, dproj)


def _w_out_grad(yc, yl, dob):
    t = yc.shape[0]
    tk = 512

    def body(yc_ref, yl_ref, do_ref, o_ref):
        @pl.when(pl.program_id(0) == 0)
        def _():
            o_ref[...] = jnp.zeros_like(o_ref)

        dov = do_ref[...]
        o_ref[0:D_PART, :] += _mm_tn(yc_ref[...], dov)
        o_ref[D_PART:2 * D_PART, :] += _mm_tn(yl_ref[...], dov)

    row = lambda kk: (kk, 0)
    out = pl.pallas_call(
        body, grid=(t // tk,),
        in_specs=[pl.BlockSpec((tk, D_PART), row), pl.BlockSpec((tk, D_PART), row), pl.BlockSpec((tk, D_MODEL), row)],
        out_specs=pl.BlockSpec((2 * D_PART, D_MODEL), lambda kk: (0, 0)),
        out_shape=jax.ShapeDtypeStruct((2 * D_PART, D_MODEL), F32),
        compiler_params=_cp(ARB), name="w_out_grad",
    )(yc, yl, dob)
    return out.reshape(N_CHIPS, 2 * D_PART // N_CHIPS, D_MODEL)


def _for_groups(n, fn, init):
    def trip(j, carry):
        for uu in range(UNROLL):
            carry = fn(j * UNROLL + uu, carry)
        return carry

    return lax.fori_loop(0, n // UNROLL, trip, init)


def _pvb(pv_ref, r):
    return jnp.broadcast_to(pv_ref[r:r + 1, :], (SUBLANES, pv_ref.shape[1]))


def _conv3(pv_ref, u, u1, u2):
    return (_pvb(pv_ref, PV_CONV_W) * u2 + _pvb(pv_ref, PV_CONV_W + 1) * u1) + _pvb(pv_ref, PV_CONV_W + 2) * u


def _conv4(pv_ref, v, v1, v2, v3):
    return ((((_pvb(pv_ref, PV_LRU_W) * v3 + _pvb(pv_ref, PV_LRU_W + 1) * v2) + _pvb(pv_ref, PV_LRU_W + 2) * v1)
             + _pvb(pv_ref, PV_LRU_W + 3) * v) + _pvb(pv_ref, PV_LRU_B))


def _mixer_forward(proj, pvec, wai):
    t = proj.shape[0]
    tb = 512
    ng = tb // SUBLANES
    nt = t // tb

    def body(bg_ref, cg_ref, xc_ref, gc_ref, xl_ref, gl_ref, pv_ref, wai_ref,
             yc_ref, yl_ref, h_ref,
             ucp_s, xlp_s, ls_s, hbuf_s, u_s, gate_s, zc_s, zl_s):
        @pl.when(pl.program_id(1) == 0)
        def _():
            ucp_s[...] = jnp.zeros_like(ucp_s)
            xlp_s[...] = jnp.zeros_like(xlp_s)
            hbuf_s[...] = jnp.zeros_like(hbuf_s)

        row = lax.broadcasted_iota(jnp.int32, (SUBLANES, LW), 0)
        ls_s[...] = _log_sigmoid(_pvb(pv_ref, PV_LAM))

        def conv_group(g, carry):
            ucp, xlp = carry
            sl = pl.ds(pl.multiple_of(g * SUBLANES, SUBLANES), SUBLANES)
            uc = cg_ref[sl, :] * xc_ref[sl, :]
            v = _conv3(pv_ref, uc, _shift_down(uc, ucp, 1, row), _shift_down(uc, ucp, 2, row))
            yc = bg_ref[sl, :] * v
            rr = lax.rsqrt(_head_mean(yc * yc, CONV_HEAD) + RMS_EPS)
            gc = gc_ref[sl, :]
            zc_s[sl, :] = ((yc * rr) * _pvb(pv_ref, PV_CG)) * (gc * _sigmoid(gc))
            xl = xl_ref[sl, :]
            u_s[sl, :] = _conv4(pv_ref, xl, _shift_down(xl, xlp, 1, row), _shift_down(xl, xlp, 2, row),
                                _shift_down(xl, xlp, 3, row))
            return uc, xl

        ucp, xlp = _for_groups(ng, conv_group, (ucp_s[...], xlp_s[...]))
        ucp_s[...] = ucp
        xlp_s[...] = xlp

        gate_s[...] = _mm(u_s[...].astype(MXU_DTYPE), wai_ref[0])

        def lru_group(g, h_before):
            sl = pl.ds(pl.multiple_of(g * SUBLANES, SUBLANES), SUBLANES)
            u = u_s[sl, :]
            r, ig, a, e2, mult, _ = _gates(gate_s[sl, 0:LW] + _pvb(pv_ref, PV_BA),
                                           gate_s[sl, LW:2 * LW] + _pvb(pv_ref, PV_BI), u, ls_s[...])
            A, B = _scan8_fwd(a, mult * (ig * u), row)
            h = B + A * jnp.broadcast_to(h_before[SUBLANES - 1:SUBLANES, :], (SUBLANES, LW))
            h_ref[sl, :] = h
            rr = lax.rsqrt(_head_mean(h * h, LRU_HEAD) + RMS_EPS)
            gl = gl_ref[sl, :]
            zl_s[sl, :] = ((h * rr) * _pvb(pv_ref, PV_LG)) * (gl * _sigmoid(gl))
            return h

        hbuf_s[...] = _for_groups(ng, lru_group, hbuf_s[...])
        yc_ref[...] = zc_s[...].astype(MXU_DTYPE)
        yl_ref[...] = zl_s[...].astype(MXU_DTYPE)

    def part(p):
        return pl.BlockSpec((tb, LW), lambda c, i: (i, p * NS + c))

    strip = pl.BlockSpec((tb, LW), lambda c, i: (i, c))
    return pl.pallas_call(
        body, grid=(NS, nt),
        in_specs=[part(p) for p in range(N_PARTS)] + [
            pl.BlockSpec((PV_ROWS, LW), lambda c, i: (0, c)),
            pl.BlockSpec((1, LW, 2 * LW), lambda c, i: (c, 0, 0))],
        out_specs=(strip, strip, strip),
        out_shape=(jax.ShapeDtypeStruct((t, D_PART), MXU_DTYPE), jax.ShapeDtypeStruct((t, D_PART), MXU_DTYPE),
                   jax.ShapeDtypeStruct((t, D_PART), F32)),
        scratch_shapes=[pltpu.VMEM((SUBLANES, LW), F32), pltpu.VMEM((SUBLANES, LW), F32), pltpu.VMEM((SUBLANES, LW), F32),
                        pltpu.VMEM((SUBLANES, LW), F32), pltpu.VMEM((tb, LW), F32), pltpu.VMEM((tb, 2 * LW), F32),
                        pltpu.VMEM((tb, LW), F32), pltpu.VMEM((tb, LW), F32)],
        compiler_params=_cp(ARB, ARB), name="mixer_forward",
    )(proj, proj, proj, proj, proj, proj, pvec, wai)


def _mixer_backward(proj, h, dy, pvec, wai):
    t = proj.shape[0]
    tb = 512
    ng = tb // SUBLANES
    nt = t // tb
    gpb = tb // SUBLANES

    def body(bg_ref, cg_ref, xc_ref, gc_ref, xl_ref, gl_ref, h_ref, dyc_ref, dyl_ref,
             cgh_ref, xch_ref, xlh_ref, hh_ref, pv_ref, wai_ref,
             dp_ref, gw_ref, sv_ref,
             ls_s, u_s, uce_s, xle_s, he_s, gate_s, dgate_s, du_s, gbuf_s,
             p0_s, p1_s, p2_s, p3_s, p4_s, p5_s, acc_s, an_s, dvn_s, dun_s):
        i = pl.program_id(1)
        first_block = i == nt - 1

        @pl.when(i == 0)
        def _():
            acc_s[...] = jnp.zeros_like(acc_s)
            gw_ref[...] = jnp.zeros_like(gw_ref)
            an_s[...] = jnp.zeros_like(an_s)
            dvn_s[...] = jnp.zeros_like(dvn_s)
            dun_s[...] = jnp.zeros_like(dun_s)
            gbuf_s[...] = jnp.zeros_like(gbuf_s)

        row = lax.broadcasted_iota(jnp.int32, (SUBLANES, LW), 0)
        ls_s[...] = _log_sigmoid(_pvb(pv_ref, PV_LAM))
        keep = jnp.where(first_block, 0.0, 1.0)
        uce_s[0:SUBLANES, :] = (cgh_ref[...] * xch_ref[...]) * keep
        xle_s[0:SUBLANES, :] = xlh_ref[...] * keep
        he_s[0:SUBLANES, :] = hh_ref[...] * keep
        xle_s[SUBLANES:SUBLANES + tb, :] = xl_ref[...]
        he_s[SUBLANES:SUBLANES + tb, :] = h_ref[...]

        def recompute_group(g, carry):
            r0 = pl.multiple_of(g * SUBLANES, SUBLANES)
            sl = pl.ds(r0, SUBLANES)
            uce_s[pl.ds(r0 + SUBLANES, SUBLANES), :] = cg_ref[sl, :] * xc_ref[sl, :]
            xl = xle_s[pl.ds(r0 + SUBLANES, SUBLANES), :]
            xlp = xle_s[sl, :]
            u_s[sl, :] = _conv4(pv_ref, xl, _shift_down(xl, xlp, 1, row), _shift_down(xl, xlp, 2, row),
                                _shift_down(xl, xlp, 3, row))
            return carry

        _for_groups(ng, recompute_group, 0)
        gate_s[...] = _mm(u_s[...].astype(MXU_DTYPE), wai_ref[0])

        def acc_add(k, v):
            acc_s[k] += v

        def main_group(gi, carry):
            a_next, dv_next, g_next = carry
            g = ng - 1 - gi
            r0 = pl.multiple_of(g * SUBLANES, SUBLANES)
            sl = pl.ds(r0, SUBLANES)
            sl_e = pl.ds(r0 + SUBLANES, SUBLANES)
            lsb = ls_s[...]
            u = u_s[sl, :]
            r, ig, a, e2, mult, inv_mult = _gates(gate_s[sl, 0:LW] + _pvb(pv_ref, PV_BA),
                                                  gate_s[sl, LW:2 * LW] + _pvb(pv_ref, PV_BI), u, lsb)
            gl = gl_ref[sl, :]
            sg = _sigmoid(gl)
            s_l = gl * sg
            h8 = he_s[sl_e, :]
            hprev = _shift_down(h8, he_s[sl, :], 1, row)
            rr = lax.rsqrt(_head_mean(h8 * h8, LRU_HEAD) + RMS_EPS)
            n = h8 * rr
            dz = dyl_ref[sl, :]
            lg = _pvb(pv_ref, PV_LG)
            acc_add(PV_LG, (dz * n) * s_l)
            p5_s[sl, :] = ((dz * n) * lg) * (sg * (1.0 + gl * (1.0 - sg)))
            dn = (dz * lg) * s_l
            dh = rr * (dn - n * _head_mean(dn * n, LRU_HEAD))
            A, B = _scan8_rev(_shift_up(a, a_next, 1, row), dh, row)
            gg = B + A * jnp.broadcast_to(g_next[0:1, :], (SUBLANES, LW))
            da = gg * hprev
            iu = ig * u
            diu = gg * mult
            dla = da * a - (gg * iu) * (e2 * inv_mult)
            acc_add(PV_LAM, dla * (RG_LRU_C * r))
            dra = (dla * (RG_LRU_C * lsb)) * (r * (1.0 - r))
            dia = (diu * u) * (ig * (1.0 - ig))
            dgate_s[sl, 0:LW] = dra
            dgate_s[sl, LW:2 * LW] = dia
            acc_add(PV_BA, dra)
            acc_add(PV_BI, dia)
            du_s[sl, :] = diu * ig
            bg = bg_ref[sl, :]
            gc = gc_ref[sl, :]
            uc = uce_s[sl_e, :]
            ucp = uce_s[sl, :]
            uc1 = _shift_down(uc, ucp, 1, row)
            uc2 = _shift_down(uc, ucp, 2, row)
            v = _conv3(pv_ref, uc, uc1, uc2)
            yc = bg * v
            rrc = lax.rsqrt(_head_mean(yc * yc, CONV_HEAD) + RMS_EPS)
            nc = yc * rrc
            sgc = _sigmoid(gc)
            s_c = gc * sgc
            dzc = dyc_ref[sl, :]
            cgain = _pvb(pv_ref, PV_CG)
            acc_add(PV_CG, (dzc * nc) * s_c)
            p3_s[sl, :] = ((dzc * nc) * cgain) * (sgc * (1.0 + gc * (1.0 - sgc)))
            dnc = (dzc * cgain) * s_c
            dyc = rrc * (dnc - nc * _head_mean(dnc * nc, CONV_HEAD))
            p0_s[sl, :] = dyc * v
            dv = dyc * bg
            duc = (_pvb(pv_ref, PV_CONV_W + 2) * dv + _pvb(pv_ref, PV_CONV_W + 1) * _shift_up(dv, dv_next, 1, row)
                   + _pvb(pv_ref, PV_CONV_W) * _shift_up(dv, dv_next, 2, row))
            acc_add(PV_CONV_W + 2, dv * uc)
            acc_add(PV_CONV_W + 1, dv * uc1)
            acc_add(PV_CONV_W, dv * uc2)
            p1_s[sl, :] = duc * xc_ref[sl, :]
            p2_s[sl, :] = duc * cg_ref[sl, :]
            return a, dv, gg

        a_next, dv_next, g_next = _for_groups(ng, main_group, (an_s[...], dvn_s[...], gbuf_s[...]))
        an_s[...] = a_next
        dvn_s[...] = dv_next
        gbuf_s[...] = g_next

        dgb = dgate_s[...].astype(MXU_DTYPE)
        du_s[...] += _mm_nt(dgb, wai_ref[0])
        gw_ref[0] += _mm_tn(u_s[...].astype(MXU_DTYPE), dgb)

        def lru_conv_group(gi, du_next):
            g = ng - 1 - gi
            r0 = pl.multiple_of(g * SUBLANES, SUBLANES)
            sl = pl.ds(r0, SUBLANES)
            du = du_s[sl, :]
            xl = xle_s[pl.ds(r0 + SUBLANES, SUBLANES), :]
            xlp = xle_s[sl, :]
            acc_add(PV_LRU_B, du)
            acc_add(PV_LRU_W + 3, du * xl)
            acc_add(PV_LRU_W + 2, du * _shift_down(xl, xlp, 1, row))
            acc_add(PV_LRU_W + 1, du * _shift_down(xl, xlp, 2, row))
            acc_add(PV_LRU_W, du * _shift_down(xl, xlp, 3, row))
            p4_s[sl, :] = (((_pvb(pv_ref, PV_LRU_W + 3) * du + _pvb(pv_ref, PV_LRU_W + 2) * _shift_up(du, du_next, 1, row))
                            + _pvb(pv_ref, PV_LRU_W + 1) * _shift_up(du, du_next, 2, row))
                           + _pvb(pv_ref, PV_LRU_W) * _shift_up(du, du_next, 3, row))
            return du

        dun_s[...] = _for_groups(ng, lru_conv_group, dun_s[...])

        for p, p_s in enumerate((p0_s, p1_s, p2_s, p3_s, p4_s, p5_s)):
            dp_ref[p] = p_s[...].astype(MXU_DTYPE)

        @pl.when(first_block)
        def _():
            sv_ref[...] = jnp.zeros_like(sv_ref)
            for k in range(N_ACC):
                tot = jnp.sum(acc_s[k], axis=0, keepdims=True)
                if k == PV_LAM:
                    tot = tot / (1.0 + jnp.exp(pv_ref[PV_LAM:PV_LAM + 1, :]))
                sv_ref[k:k + 1, :] = tot

    def part(p):
        return pl.BlockSpec((tb, LW), lambda c, i: (nt - 1 - i, p * NS + c))

    def halo(p):
        return pl.BlockSpec((SUBLANES, LW), lambda c, i: (jnp.maximum((nt - 1 - i) * gpb - 1, 0), p * NS + c))

    strip = pl.BlockSpec((tb, LW), lambda c, i: (nt - 1 - i, c))
    big = pltpu.VMEM((tb, LW), F32)
    big_e = pltpu.VMEM((tb + SUBLANES, LW), F32)
    wide = pltpu.VMEM((tb, 2 * LW), F32)
    small = pltpu.VMEM((SUBLANES, LW), F32)
    outs = pl.pallas_call(
        body, grid=(NS, nt),
        in_specs=[part(p) for p in range(N_PARTS)] + [
            strip, strip, pl.BlockSpec((tb, LW), lambda c, i: (nt - 1 - i, NS + c)),
            halo(1), halo(2), halo(4),
            pl.BlockSpec((SUBLANES, LW), lambda c, i: (jnp.maximum((nt - 1 - i) * gpb - 1, 0), c)),
            pl.BlockSpec((PV_ROWS, LW), lambda c, i: (0, c)),
            pl.BlockSpec((1, LW, 2 * LW), lambda c, i: (c, 0, 0))],
        out_specs=(pl.BlockSpec((N_PARTS, tb, LW), lambda c, i: (0, nt - 1 - i, c)),
                   pl.BlockSpec((1, LW, 2 * LW), lambda c, i: (c, 0, 0)),
                   pl.BlockSpec((PV_ROWS, LW), lambda c, i: (0, c))),
        out_shape=(jax.ShapeDtypeStruct((N_PARTS, t, D_PART), MXU_DTYPE),
                   jax.ShapeDtypeStruct((NS, LW, 2 * LW), F32), jax.ShapeDtypeStruct((PV_ROWS, D_PART), F32)),
        scratch_shapes=[small, big, big_e, big_e, big_e, wide, wide, big, small,
                        big, big, big, big, big, big, pltpu.VMEM((N_ACC, SUBLANES, LW), F32), small, small, small],
        compiler_params=_cp(ARB, ARB), name="mixer_backward",
    )(proj, proj, proj, proj, proj, proj, h, dy, dy, proj, proj, proj, h, pvec, wai)
    return outs


def _adamw(w, g, m, v):
    m = ADAM_B1 * m + (1.0 - ADAM_B1) * g
    v = ADAM_B2 * v + (1.0 - ADAM_B2) * (g * g)
    m_hat = m / (1.0 - ADAM_B1 ** ADAM_STEP)
    v_hat = v / (1.0 - ADAM_B2 ** ADAM_STEP)
    delta = -ADAM_LR * (m_hat / (jnp.sqrt(v_hat) + ADAM_EPS) + ADAM_WD * w)
    return delta, m, v


def _adam_w_in(w, m, v, g3):
    rows, cols = w.shape
    tr = 128

    def body(w_ref, m_ref, v_ref, g_ref, go_ref, d_ref, mo_ref, vo_ref):
        for s in range(CHUNKS_PER_BLOCK):
            cs = slice(CHUNK * s, CHUNK * (s + 1))
            g = g_ref[s]
            d, mn, vn = _adamw(w_ref[:, cs], g, m_ref[:, cs], v_ref[:, cs])
            go_ref[:, cs] = g
            d_ref[:, cs] = d
            mo_ref[:, cs] = mn
            vo_ref[:, cs] = vn

    blk = pl.BlockSpec((tr, cols), lambda i: (i, 0))
    return pl.pallas_call(
        body, grid=(rows // tr,),
        in_specs=[blk, blk, blk, pl.BlockSpec((CHUNKS_PER_BLOCK, tr, CHUNK), lambda i: (0, i, 0))],
        out_specs=(blk,) * 4, out_shape=(jax.ShapeDtypeStruct(w.shape, F32),) * 4,
        compiler_params=_cp(ARB), name="adam_w_in",
    )(w, m, v, g3)


def _adam_w_out(w, m, v, g):
    rows, cols = w.shape
    tr = 128

    def body(w_ref, m_ref, v_ref, g_ref, d_ref, mo_ref, vo_ref):
        d_ref[...], mo_ref[...], vo_ref[...] = _adamw(w_ref[...], g_ref[...], m_ref[...], v_ref[...])

    blk = pl.BlockSpec((tr, cols), lambda i: (i, 0))
    return pl.pallas_call(
        body, grid=(rows // tr,), in_specs=[blk] * 4, out_specs=(blk,) * 3,
        out_shape=(jax.ShapeDtypeStruct(w.shape, F32),) * 3,
        compiler_params=_cp(ARB), name="adam_w_out",
    )(w, m, v, g)


def _adam_small(ws, ms, vs, gs):
    n = len(ws)

    def body(*refs):
        w_r, m_r, v_r, g_r = refs[0:n], refs[n:2 * n], refs[2 * n:3 * n], refs[3 * n:4 * n]
        d_o, m_o, v_o = refs[4 * n:5 * n], refs[5 * n:6 * n], refs[6 * n:7 * n]
        for j in range(n):
            d_o[j][...], m_o[j][...], v_o[j][...] = _adamw(w_r[j][...], g_r[j][...], m_r[j][...], v_r[j][...])

    vm = pl.BlockSpec(memory_space=pltpu.VMEM)
    shapes = tuple(jax.ShapeDtypeStruct(w.shape, F32) for w in ws)
    outs = pl.pallas_call(
        body, in_specs=[vm] * (4 * n), out_specs=(vm,) * (3 * n), out_shape=shapes * 3,
        compiler_params=_cp(), name="adam_small",
    )(*ws, *ms, *vs, *gs)
    return outs[0:n], outs[n:2 * n], outs[2 * n:3 * n]


def _block_diag_strips(w):
    w4 = w.reshape(NS, HEADS_PER_STRIP, LRU_HEAD, LRU_HEAD)
    bd = jnp.zeros((NS, HEADS_PER_STRIP, LRU_HEAD, HEADS_PER_STRIP, LRU_HEAD), w.dtype)
    for hh in range(HEADS_PER_STRIP):
        bd = bd.at[:, hh, :, hh, :].set(w4[:, hh])
    return bd.reshape(NS, LW, LW)


def _strip_diag_blocks(g):
    g5 = g.reshape(NS, HEADS_PER_STRIP, LRU_HEAD, HEADS_PER_STRIP, LRU_HEAD)
    return jnp.stack([g5[:, hh, :, hh, :] for hh in range(HEADS_PER_STRIP)], axis=1).reshape(NS * HEADS_PER_STRIP, LRU_HEAD, LRU_HEAD)


def kernel(x, ln_g, w_in, conv_w, lru_conv_w, lru_conv_b, w_a, b_a, w_i, b_i, lam, conv_out_g, lru_out_g, w_out, final_g, loss_target, m_ln_g, m_w_in, m_conv_w, m_lru_conv_w, m_lru_conv_b, m_w_a, m_b_a, m_w_i, m_b_i, m_lam, m_conv_out_g, m_lru_out_g, m_w_out, m_final_g, v_ln_g, v_w_in, v_conv_w, v_lru_conv_w, v_lru_conv_b, v_w_a, v_b_a, v_w_i, v_b_i, v_lam, v_conv_out_g, v_lru_out_g, v_w_out, v_final_g):
    xi, yi, ci = lax.axis_index("x"), lax.axis_index("y"), lax.axis_index("c")
    k = 2 * xi + yi
    t = x.shape[1]
    x2 = x.reshape(t, D_MODEL)
    tgt2 = loss_target.reshape(t, D_MODEL)
    row = lambda a: a.reshape(1, -1)

    small = jnp.concatenate([conv_w, lru_conv_w, jnp.zeros((1, conv_w.shape[1]), F32)], axis=0)
    w12, wo4, sm4 = _allgather_weights(w_in, w_out, small)
    wo = wo4.reshape(2 * D_PART, D_MODEL)
    convs = jnp.transpose(sm4, (1, 0, 2)).reshape(SUBLANES, D_PART)
    pvec = jnp.concatenate(
        [convs[0:7], row(lru_conv_b), row(b_a), row(b_i), row(lam), row(conv_out_g), row(lru_out_g),
         jnp.zeros((PV_ROWS - N_ACC, D_PART), F32)], axis=0)
    wai = jnp.concatenate([_block_diag_strips(w_a), _block_diag_strips(w_i)], axis=2).astype(MXU_DTYPE)

    proj, xn = _in_projection(x2, row(ln_g), w12)
    yc, yl, h = _mixer_forward(proj, pvec, wai)
    do, dob, dy, st_out = _out_projection_loss(yc, yl, x2, tgt2, wo, row(final_g))
    dproj, g_wai, svec = _mixer_backward(proj, h, dy, pvec, wai)
    grad_x, st_in = _input_grad(dproj, w12, x2, do, row(ln_g))
    g12 = _w_in_grad(xn, dproj)
    go4 = _w_out_grad(yc, yl, dob)

    gwa = _strip_diag_blocks(g_wai[:, :, 0:LW]).reshape(LRU_HEAD, D_PART)
    gwi = _strip_diag_blocks(g_wai[:, :, LW:2 * LW]).reshape(LRU_HEAD, D_PART)
    red = _allreduce_small(jnp.concatenate([svec, st_out, st_in, gwa, gwi], axis=0))
    r_out = PV_ROWS
    r_in = PV_ROWS + SUBLANES
    r_wa = PV_ROWS + 2 * SUBLANES
    r_wi = r_wa + LRU_HEAD
    loss = red[r_out + 1, 0]

    c_arr = jnp.reshape(ci, (1,)).astype(jnp.int32)
    kc_arr = jnp.stack([k, ci]).astype(jnp.int32)
    ri, ro = _exchange_sibling_halves(g12, go4)
    s_in, sb_in = _add_own_half(g12, ri, c_arr, "add_own_half_in")
    s_out, sb_out = _add_own_half(go4, ro, c_arr, "add_own_half_out")
    r2i, r2o = _exchange_chip_blocks(sb_in, sb_out)
    f_in = _sum_chip_blocks(s_in, r2i, kc_arr, CHUNKS_PER_BLOCK, "sum_chip_blocks_in")
    f_out = _sum_chip_blocks(s_out, r2o.reshape(3, 1, *r2o.shape[1:]), kc_arr, 1, "sum_chip_blocks_out")
    f_in, f_out = _swap_sibling_halves(f_in, f_out)

    g_w_in, d_w_in, nm_w_in, nv_w_in = _adam_w_in(w_in, m_w_in, v_w_in, f_in)
    g_w_out = f_out[0]
    d_w_out, nm_w_out, nv_w_out = _adam_w_out(w_out, m_w_out, v_w_out, g_w_out)

    ncol = conv_w.shape[1]
    conv_cols = lax.dynamic_slice(red, (0, k * ncol), (SUBLANES, ncol))
    g_small = {
        "ln_g": red[r_in], "conv_w": conv_cols[0:3], "lru_conv_w": conv_cols[3:7], "lru_conv_b": red[PV_LRU_B],
        "w_a": red[r_wa:r_wa + LRU_HEAD].reshape(w_a.shape), "b_a": red[PV_BA],
        "w_i": red[r_wi:r_wi + LRU_HEAD].reshape(w_i.shape), "b_i": red[PV_BI], "lam": red[PV_LAM],
        "conv_out_g": red[PV_CG], "lru_out_g": red[PV_LG], "final_g": red[r_out],
    }
    w_small = {"ln_g": ln_g, "conv_w": conv_w, "lru_conv_w": lru_conv_w, "lru_conv_b": lru_conv_b, "w_a": w_a, "b_a": b_a,
               "w_i": w_i, "b_i": b_i, "lam": lam, "conv_out_g": conv_out_g, "lru_out_g": lru_out_g, "final_g": final_g}
    m_small = {"ln_g": m_ln_g, "conv_w": m_conv_w, "lru_conv_w": m_lru_conv_w, "lru_conv_b": m_lru_conv_b, "w_a": m_w_a,
               "b_a": m_b_a, "w_i": m_w_i, "b_i": m_b_i, "lam": m_lam, "conv_out_g": m_conv_out_g,
               "lru_out_g": m_lru_out_g, "final_g": m_final_g}
    v_small = {"ln_g": v_ln_g, "conv_w": v_conv_w, "lru_conv_w": v_lru_conv_w, "lru_conv_b": v_lru_conv_b, "w_a": v_w_a,
               "b_a": v_b_a, "w_i": v_w_i, "b_i": v_b_i, "lam": v_lam, "conv_out_g": v_conv_out_g,
               "lru_out_g": v_lru_out_g, "final_g": v_final_g}
    names = list(w_small)
    as2d = lambda a: a.reshape(1, -1) if a.ndim == 1 else a
    d_s, m_s, v_s = _adam_small([as2d(w_small[n]) for n in names], [as2d(m_small[n]) for n in names],
                                [as2d(v_small[n]) for n in names], [as2d(g_small[n]) for n in names])
    back = lambda n, a: a.reshape(w_small[n].shape)
    grads = {n: g_small[n] for n in names}
    deltas = {n: back(n, a) for n, a in zip(names, d_s)}
    new_m = {n: back(n, a) for n, a in zip(names, m_s)}
    new_v = {n: back(n, a) for n, a in zip(names, v_s)}
    grads["w_in"], deltas["w_in"], new_m["w_in"], new_v["w_in"] = g_w_in, d_w_in, nm_w_in, nv_w_in
    grads["w_out"], deltas["w_out"], new_m["w_out"], new_v["w_out"] = g_w_out, d_w_out, nm_w_out, nv_w_out

    order = ["ln_g", "w_in", "conv_w", "lru_conv_w", "lru_conv_b", "w_a", "b_a", "w_i", "b_i", "lam", "conv_out_g",
             "lru_out_g", "w_out", "final_g"]
    return (loss, grad_x.reshape(x.shape), *[grads[n] for n in order], *[deltas[n] for n in order],
            *[new_m[n] for n in order], *[new_v[n] for n in order])
```

```python
import functools

import jax
import jax.numpy as jnp
from jax import lax
from jax.experimental import pallas as pl
from jax.experimental.pallas import tpu as pltpu

F32 = jnp.float32
MXU_DTYPE = jnp.bfloat16

D_MODEL = 1024
D_PART = 1024
N_PARTS = 6
CHUNK = 512
CHUNKS_PER_BLOCK = 3
N_CHUNKS = 12
N_CHIPS = 4
SUBLANES = 8
LANES = 128
LW = 256
UNROLL = 8
NS = D_PART // LW
CONV_HEAD = 128
LRU_HEAD = 64
HEADS_PER_STRIP = LW // LRU_HEAD
RMS_EPS = 1e-6
RG_LRU_C = 8.0
ADAM_LR = 0.001
ADAM_B1 = 0.9
ADAM_B2 = 0.999
ADAM_EPS = 1e-08
ADAM_WD = 0.01
ADAM_STEP = 10

PV_CONV_W = 0
PV_LRU_W = 3
PV_LRU_B = 7
PV_BA = 8
PV_BI = 9
PV_LAM = 10
PV_CG = 11
PV_LG = 12
PV_ROWS = 16
N_ACC = 13

SLAB = 128
MESH = pl.DeviceIdType.MESH
VMEM_LIMIT = 56 * 1024 * 1024
ARB = "arbitrary"


def _cp(*sem, **kw):
    return pltpu.CompilerParams(dimension_semantics=sem or None, vmem_limit_bytes=VMEM_LIMIT, **kw)


def _mm(a, b):
    return jnp.dot(a, b, preferred_element_type=F32)


def _mm_nt(a, b):
    return lax.dot_general(a, b, (((1,), (1,)), ((), ())), preferred_element_type=F32)


def _mm_tn(a, b):
    return lax.dot_general(a, b, (((0,), (0,)), ((), ())), preferred_element_type=F32)


def _sigmoid(x):
    return 0.5 * jnp.tanh(0.5 * x) + 0.5


def _log_sigmoid(x):
    z = jnp.exp(-jnp.abs(x))
    u = 1.0 + z
    log1p = jnp.where(u == 1.0, z, jnp.log(u) * z / (u - 1.0))
    return jnp.minimum(x, 0.0) - log1p


def _head_mean(z, head):
    out = []
    for k in range(z.shape[1] // LANES):
        zk = z[:, LANES * k:LANES * (k + 1)]
        if head == LANES:
            m = jnp.sum(zk, axis=-1, keepdims=True) * (1.0 / head)
            out.append(jnp.broadcast_to(m, zk.shape))
        else:
            lo = lax.broadcasted_iota(jnp.int32, zk.shape, 1) < head
            s_lo = jnp.sum(jnp.where(lo, zk, 0.0), axis=-1, keepdims=True)
            s_hi = jnp.sum(jnp.where(lo, 0.0, zk), axis=-1, keepdims=True)
            out.append(jnp.where(lo, s_lo, s_hi) * (1.0 / head))
    return jnp.concatenate(out, axis=1)


def _shift_down(cur, prev, d, row):
    return pltpu.roll(jnp.where(row < SUBLANES - d, cur, prev), d, 0)


def _shift_up(cur, nxt, d, row):
    return pltpu.roll(jnp.where(row >= d, cur, nxt), SUBLANES - d, 0)


def _scan8_fwd(a, b, row):
    A, B = a, b
    for d in (1, 2, 4):
        m = row >= d
        a_s = jnp.where(m, pltpu.roll(A, d, 0), 1.0)
        b_s = jnp.where(m, pltpu.roll(B, d, 0), 0.0)
        B = A * b_s + B
        A = A * a_s
    return A, B


def _scan8_rev(a, b, row):
    A, B = a, b
    for d in (1, 2, 4):
        m = row < SUBLANES - d
        a_s = jnp.where(m, pltpu.roll(A, SUBLANES - d, 0), 1.0)
        b_s = jnp.where(m, pltpu.roll(B, SUBLANES - d, 0), 0.0)
        B = A * b_s + B
        A = A * a_s
    return A, B


def _gates(ra, ia, u, lsb):
    r = _sigmoid(ra)
    ig = _sigmoid(ia)
    la = (RG_LRU_C * r) * lsb
    a = jnp.exp(la)
    e2 = a * a
    em = -jnp.tanh(la) * (1.0 + e2)
    inv_mult = lax.rsqrt(em)
    return r, ig, a, e2, em * inv_mult, inv_mult


def _mesh_pos():
    x, y, c = lax.axis_index("x"), lax.axis_index("y"), lax.axis_index("c")
    chips = [(1 - x, y), (x, 1 - y), (1 - x, 1 - y)]
    return x, y, c, chips


def _allgather_weights(w_in, small):
    half_i = w_in.shape[0] // 2

    def body(wi_ref, sm_ref, w12_ref, sm4_ref, send_sems, recv_sems):
        x, y, c, chips = _mesh_pos()
        k = 2 * x + y
        sib = (x, y, 1 - c)
        for s in range(CHUNKS_PER_BLOCK):
            w12_ref[CHUNKS_PER_BLOCK * k + s] = wi_ref[:, CHUNK * s:CHUNK * (s + 1)].astype(MXU_DTYPE)
        sm4_ref[k] = sm_ref[...]

        def in_half(chip, core):
            kk = 2 * chip[0] + chip[1]
            return w12_ref.at[pl.ds(CHUNKS_PER_BLOCK * kk, CHUNKS_PER_BLOCK), pl.ds(pl.multiple_of(half_i * core, half_i), half_i), :]

        def copy(ref, sem, to):
            return pltpu.make_async_remote_copy(src_ref=ref, dst_ref=ref, send_sem=send_sems.at[sem],
                                                recv_sem=recv_sems.at[sem], device_id=to, device_id_type=MESH)

        me = (x, y)
        first = []
        for m, chip in enumerate(chips):
            first.append(copy(in_half(me, c), m, (*chip, c)))
            first.append(copy(sm4_ref.at[k], 3 + m, (*chip, c)))
        for cp in first:
            cp.start()
        passed = []
        for m, chip in enumerate(chips):
            copy(in_half(chip, c), m, sib).wait_recv()
            fwd = copy(in_half(chip, c), 6 + m, sib)
            fwd.start()
            passed.append(fwd)
            kk = 2 * chip[0] + chip[1]
            copy(sm4_ref.at[kk], 3 + m, sib).wait_recv()
        for m, chip in enumerate(chips):
            copy(in_half(chip, 1 - c), 6 + m, sib).wait_recv()
        for cp in first + passed:
            cp.wait_send()

    vm = pl.BlockSpec(memory_space=pltpu.VMEM)
    return pl.pallas_call(
        body,
        out_shape=(jax.ShapeDtypeStruct((N_CHUNKS, w_in.shape[0], CHUNK), MXU_DTYPE),
                   jax.ShapeDtypeStruct((N_CHIPS,) + small.shape, F32)),
        in_specs=[vm, vm], out_specs=(vm, vm),
        scratch_shapes=[pltpu.SemaphoreType.DMA((9,)), pltpu.SemaphoreType.DMA((9,))],
        compiler_params=_cp(), name="allgather_weights",
    )(w_in, small)


def _allreduce_small(buf):
    def body(in_ref, out_ref, r0, r1, r2, send_sems, recv_sems):
        x, y, c, _ = _mesh_pos()
        peers = [(x, y, 1 - c), (1 - x, y, c), (x, 1 - y, c)]
        out_ref[...] = in_ref[...]
        for ph, (peer, rbuf) in enumerate(zip(peers, (r0, r1, r2))):
            cp = pltpu.make_async_remote_copy(src_ref=out_ref, dst_ref=rbuf, send_sem=send_sems.at[ph],
                                              recv_sem=recv_sems.at[ph], device_id=peer, device_id_type=MESH)
            cp.start()
            cp.wait()
            out_ref[...] = out_ref[...] + rbuf[...]

    vm = pl.BlockSpec(memory_space=pltpu.VMEM)
    return pl.pallas_call(
        body, out_shape=jax.ShapeDtypeStruct(buf.shape, F32), in_specs=[vm], out_specs=vm,
        scratch_shapes=[pltpu.VMEM(buf.shape, F32)] * 3 + [pltpu.SemaphoreType.DMA((3,)), pltpu.SemaphoreType.DMA((3,))],
        compiler_params=_cp(), name="allreduce_small",
    )(buf)


def _exchange_sibling_halves(g, name):
    n, rows, cols = g.shape
    half = rows // 2

    def body(g_ref, r_ref, send_sem, recv_sem):
        x, y, c, _ = _mesh_pos()
        cp = pltpu.make_async_remote_copy(src_ref=g_ref.at[:, pl.ds(pl.multiple_of(half * (1 - c), half), half), :],
                                          dst_ref=r_ref, send_sem=send_sem, recv_sem=recv_sem,
                                          device_id=(x, y, 1 - c), device_id_type=MESH)
        cp.start()
        cp.wait()

    hbm = pl.BlockSpec(memory_space=pl.ANY)
    return pl.pallas_call(
        body, out_shape=jax.ShapeDtypeStruct((n, half, cols), F32), in_specs=[hbm], out_specs=hbm,
        scratch_shapes=[pltpu.SemaphoreType.DMA, pltpu.SemaphoreType.DMA],
        compiler_params=_cp(), name=name,
    )(g)


def _add_own_half(g, r, c_arr, name):
    n, rr, cc = r.shape

    def body(c_ref, g_ref, r_ref, o_ref, ob_ref):
        s = g_ref[...] + r_ref[...]
        o_ref[...] = s
        ob_ref[...] = s.astype(jnp.bfloat16)

    blk = pl.BlockSpec((1, rr, cc), lambda q, c_ref: (q, 0, 0))
    return pl.pallas_call(
        body, out_shape=(jax.ShapeDtypeStruct(r.shape, F32), jax.ShapeDtypeStruct(r.shape, jnp.bfloat16)),
        grid_spec=pltpu.PrefetchScalarGridSpec(
            num_scalar_prefetch=1, grid=(n,),
            in_specs=[pl.BlockSpec((1, rr, cc), lambda q, c_ref: (q, c_ref[0], 0)), blk],
            out_specs=(blk, blk)),
        compiler_params=_cp(ARB), name=name,
    )(c_arr, g, r)


def _chip_block_copies(s_ref, r_ref, n_sub, send_sems, recv_sems):
    x, y, c, chips = _mesh_pos()
    cps = []
    for m, chip in enumerate(chips):
        kk = 2 * chip[0] + chip[1]
        cps.append(pltpu.make_async_remote_copy(
            src_ref=s_ref.at[pl.ds(n_sub * kk, n_sub)], dst_ref=r_ref.at[m],
            send_sem=send_sems.at[m], recv_sem=recv_sems.at[m], device_id=(*chip, c), device_id_type=MESH))
    return cps


def _gather_w_out(step, n_steps, wo_ref, wob_s, wo4_ref, local_sem, send_sems, recv_sems):
    x, y, c, chips = _mesh_pos()
    sib = (x, y, 1 - c)
    half = wo_ref.shape[0] // 2

    def rows(core):
        return pl.ds(pl.multiple_of(half * core, half), half)

    def block_half(chip, core):
        return wo4_ref.at[2 * chip[0] + chip[1], rows(core), :]

    def remote(src, dst, sem, to):
        return pltpu.make_async_remote_copy(src_ref=src, dst_ref=dst, send_sem=send_sems.at[sem], recv_sem=recv_sems.at[sem],
                                            device_id=to, device_id_type=MESH)

    local = pltpu.make_async_copy(wob_s, wo4_ref.at[2 * x + y], local_sem)
    ici = [remote(wob_s.at[rows(c), :], block_half((x, y), c), m, (*chip, c)) for m, chip in enumerate(chips)]
    fwd = [remote(block_half(chip, c), block_half(chip, c), 3 + m, sib) for m, chip in enumerate(chips)]

    @pl.when(step == 0)
    def _():
        wob_s[...] = wo_ref[...].astype(MXU_DTYPE)
        local.start()
        for cp in ici:
            cp.start()

    @pl.when(step == n_steps // 2)
    def _():
        for m, chip in enumerate(chips):
            remote(block_half(chip, c), block_half(chip, c), m, sib).wait_recv()
            fwd[m].start()

    @pl.when(step == n_steps - 1)
    def _():
        for m, chip in enumerate(chips):
            remote(block_half(chip, 1 - c), block_half(chip, 1 - c), 3 + m, sib).wait_recv()
        for cp in ici + fwd:
            cp.wait_send()
        local.wait()


def _chip_blocks_shape(s, n_sub):
    return jax.ShapeDtypeStruct((3, n_sub) + s.shape[1:], s.dtype)


def _sum_chip_blocks(s, r, kc_arr, n_sub, name):
    _, rr, cc = s.shape

    def body(kc_ref, s_ref, r_ref, o_ref):
        o_ref[...] = ((s_ref[...] + r_ref[0].astype(F32)) + r_ref[1].astype(F32)) + r_ref[2].astype(F32)

    return pl.pallas_call(
        body, out_shape=jax.ShapeDtypeStruct((n_sub, 2 * rr, cc), F32),
        grid_spec=pltpu.PrefetchScalarGridSpec(
            num_scalar_prefetch=1, grid=(n_sub,),
            in_specs=[pl.BlockSpec((1, rr, cc), lambda q, kc: (n_sub * kc[0] + q, 0, 0)),
                      pl.BlockSpec((3, 1, rr, cc), lambda q, kc: (0, q, 0, 0))],
            out_specs=pl.BlockSpec((1, rr, cc), lambda q, kc: (q, kc[1], 0))),
        compiler_params=_cp(ARB), name=name,
    )(kc_arr, s, r)


def _swap_sibling_halves(f_in, f_out):
    hi, ho = f_in.shape[1] // 2, f_out.shape[1] // 2

    def body(fi_in, fo_in, fi_ref, fo_ref, send_sems, recv_sems):
        del fi_in, fo_in
        x, y, c, _ = _mesh_pos()
        sib = (x, y, 1 - c)
        si = fi_ref.at[:, pl.ds(pl.multiple_of(hi * c, hi), hi), :]
        so = fo_ref.at[:, pl.ds(pl.multiple_of(ho * c, ho), ho), :]
        cps = [
            pltpu.make_async_remote_copy(src_ref=si, dst_ref=si, send_sem=send_sems.at[0], recv_sem=recv_sems.at[0],
                                         device_id=sib, device_id_type=MESH),
            pltpu.make_async_remote_copy(src_ref=so, dst_ref=so, send_sem=send_sems.at[1], recv_sem=recv_sems.at[1],
                                         device_id=sib, device_id_type=MESH),
        ]
        for cp in cps:
            cp.start()
        for cp in cps:
            cp.wait()

    hbm = pl.BlockSpec(memory_space=pl.ANY)
    return pl.pallas_call(
        body,
        out_shape=(jax.ShapeDtypeStruct(f_in.shape, F32), jax.ShapeDtypeStruct(f_out.shape, F32)),
        in_specs=[hbm, hbm], out_specs=(hbm, hbm), input_output_aliases={0: 0, 1: 1},
        scratch_shapes=[pltpu.SemaphoreType.DMA((2,)), pltpu.SemaphoreType.DMA((2,))],
        compiler_params=_cp(), name="swap_sibling_halves",
    )(f_in, f_out)


def _in_projection(x, ln_g, w12):
    t = x.shape[0]
    tm = 1024
    wide = CHUNKS_PER_BLOCK * CHUNK

    def body(x_ref, g_ref, w_ref, proj_ref, xn_ref, xn_s):
        @pl.when(pl.program_id(1) == 0)
        def _():
            def norm_slab(s, carry):
                rows = pl.ds(pl.multiple_of(s * SLAB, SLAB), SLAB)
                xf = x_ref[rows, :]
                r = lax.rsqrt(jnp.mean(xf * xf, axis=-1, keepdims=True) + RMS_EPS)
                xn = ((xf * r) * g_ref[...]).astype(MXU_DTYPE)
                xn_s[rows, :] = xn
                xn_ref[rows, :] = xn
                return carry

            lax.fori_loop(0, tm // SLAB, norm_slab, 0)

        xn = xn_s[...]
        for s in range(CHUNKS_PER_BLOCK):
            proj_ref[:, CHUNK * s:CHUNK * (s + 1)] = _mm(xn, w_ref[s])

    return pl.pallas_call(
        body, grid=(t // tm, N_CHIPS),
        in_specs=[pl.BlockSpec((tm, D_MODEL), lambda i, j: (i, 0)),
                  pl.BlockSpec((1, D_MODEL), lambda i, j: (0, 0)),
                  pl.BlockSpec((CHUNKS_PER_BLOCK, D_MODEL, CHUNK), lambda i, j: (j, 0, 0))],
        out_specs=(pl.BlockSpec((tm, wide), lambda i, j: (i, j)),
                   pl.BlockSpec((tm, D_MODEL), lambda i, j: (i, 0))),
        out_shape=(jax.ShapeDtypeStruct((t, N_CHUNKS * CHUNK), F32), jax.ShapeDtypeStruct((t, D_MODEL), MXU_DTYPE)),
        scratch_shapes=[pltpu.VMEM((tm, D_MODEL), MXU_DTYPE)],
        compiler_params=_cp(ARB, ARB), name="in_projection",
    )(x, ln_g, w12)


def _out_projection_loss(yc, yl, x, target, wo, final_g):
    t = x.shape[0]
    tm = 256

    def body(yc_ref, yl_ref, x_ref, t_ref, wo_ref, fg_ref, do_ref, dob_ref, dy_ref, st_ref):
        @pl.when(pl.program_id(0) == 0)
        def _():
            st_ref[...] = jnp.zeros_like(st_ref)

        o = x_ref[...] + (_mm(yc_ref[...], wo_ref[0:D_PART, :]) + _mm(yl_ref[...], wo_ref[D_PART:2 * D_PART, :]))
        r2 = lax.rsqrt(jnp.mean(o * o, axis=-1, keepdims=True) + RMS_EPS)
        ohat = o * r2
        fg = fg_ref[...]
        diff = ohat * fg - t_ref[...]
        dout = diff * (1.0 / D_MODEL)
        gp = dout * fg
        do = r2 * (gp - ohat * jnp.mean(gp * ohat, axis=-1, keepdims=True))
        do_ref[...] = do
        dob = do.astype(MXU_DTYPE)
        dob_ref[...] = dob
        dy_ref[...] = _mm_nt(dob, wo_ref[...])
        st_ref[0:1, :] += jnp.sum(dout * ohat, axis=0, keepdims=True)
        loss = 0.5 * jnp.sum(jnp.sum(diff * diff, axis=-1, keepdims=True) * (1.0 / D_MODEL), axis=0, keepdims=True)
        st_ref[1:2, :] += jnp.broadcast_to(loss, (1, D_MODEL))

    row = lambda i: (i, 0)
    fix = lambda i: (0, 0)
    return pl.pallas_call(
        body, grid=(t // tm,),
        in_specs=[pl.BlockSpec((tm, D_PART), row), pl.BlockSpec((tm, D_PART), row),
                  pl.BlockSpec((tm, D_MODEL), row), pl.BlockSpec((tm, D_MODEL), row),
                  pl.BlockSpec((2 * D_PART, D_MODEL), fix), pl.BlockSpec((1, D_MODEL), fix)],
        out_specs=(pl.BlockSpec((tm, D_MODEL), row), pl.BlockSpec((tm, D_MODEL), row),
                   pl.BlockSpec((tm, 2 * D_PART), row), pl.BlockSpec((SUBLANES, D_MODEL), fix)),
        out_shape=(jax.ShapeDtypeStruct((t, D_MODEL), F32), jax.ShapeDtypeStruct((t, D_MODEL), MXU_DTYPE),
                   jax.ShapeDtypeStruct((t, 2 * D_PART), F32), jax.ShapeDtypeStruct((SUBLANES, D_MODEL), F32)),
        compiler_params=_cp(ARB), name="out_projection_loss",
    )(yc, yl, x, target, wo, final_g)


def _input_grad(dproj, w12, x, do, ln_g, sb_in):
    t = x.shape[0]
    tm = 1024

    def body(dp_ref, w_ref, x_ref, do_ref, g_ref, s_ref, gx_ref, st_ref, r_ref, acc, send_sems, recv_sems):
        i, p = pl.program_id(0), pl.program_id(1)

        @pl.when((i == 0) & (p == 0))
        def _():
            st_ref[...] = jnp.zeros_like(st_ref)
            for cp in _chip_block_copies(s_ref, r_ref, CHUNKS_PER_BLOCK, send_sems, recv_sems):
                cp.start()

        @pl.when((i == t // tm - 1) & (p == N_PARTS - 1))
        def _():
            for cp in _chip_block_copies(s_ref, r_ref, CHUNKS_PER_BLOCK, send_sems, recv_sems):
                cp.wait()

        part = _mm_nt(dp_ref[0, :, 0:CHUNK], w_ref[0]) + _mm_nt(dp_ref[0, :, CHUNK:2 * CHUNK], w_ref[1])

        @pl.when(p == 0)
        def _():
            acc[...] = part

        @pl.when(p > 0)
        def _():
            acc[...] += part

        @pl.when(p == N_PARTS - 1)
        def _():
            def norm_bwd_slab(s, g_sum):
                rows = pl.ds(pl.multiple_of(s * SLAB, SLAB), SLAB)
                xf = x_ref[rows, :]
                r = lax.rsqrt(jnp.mean(xf * xf, axis=-1, keepdims=True) + RMS_EPS)
                xhat = xf * r
                dxn = acc[rows, :]
                dxh = dxn * g_ref[...]
                gx_ref[rows, :] = do_ref[rows, :] + r * (dxh - xhat * jnp.mean(dxh * xhat, axis=-1, keepdims=True))
                return g_sum + jnp.sum(dxn * xhat, axis=0, keepdims=True)

            st_ref[0:1, :] += lax.fori_loop(0, tm // SLAB, norm_bwd_slab, jnp.zeros((1, D_MODEL), F32))

    row = lambda i, p: (i, 0)
    fix = lambda i, p: (0, 0)
    return pl.pallas_call(
        body, grid=(t // tm, N_PARTS),
        in_specs=[
            pl.BlockSpec((1, tm, D_PART), lambda i, p: (p, i, 0)),
            pl.BlockSpec((2, D_MODEL, CHUNK), lambda i, p: (p, 0, 0)),
            pl.BlockSpec((tm, D_MODEL), row), pl.BlockSpec((tm, D_MODEL), row), pl.BlockSpec((1, D_MODEL), fix),
            pl.BlockSpec(memory_space=pl.ANY)],
        out_specs=(pl.BlockSpec((tm, D_MODEL), row), pl.BlockSpec((SUBLANES, D_MODEL), fix),
                   pl.BlockSpec(memory_space=pl.ANY)),
        out_shape=(jax.ShapeDtypeStruct((t, D_MODEL), F32), jax.ShapeDtypeStruct((SUBLANES, D_MODEL), F32),
                   _chip_blocks_shape(sb_in, CHUNKS_PER_BLOCK)),
        scratch_shapes=[pltpu.VMEM((tm, D_MODEL), F32), pltpu.SemaphoreType.DMA((3,)), pltpu.SemaphoreType.DMA((3,))],
        compiler_params=_cp(ARB, ARB), name="input_grad",
    )(dproj, w12, x, do, ln_g, sb_in)


def _w_in_grad(xn, dproj):
    t = xn.shape[0]
    tk = 1024

    def body(xn_ref, dp_ref, o_ref):
        kk = pl.program_id(1)
        xnv = xn_ref[...]
        parts = [_mm_tn(xnv, dp_ref[0, :, CHUNK * s:CHUNK * (s + 1)]) for s in range(2)]

        @pl.when(kk == 0)
        def _():
            for s in range(2):
                o_ref[s] = parts[s]

        @pl.when(kk > 0)
        def _():
            for s in range(2):
                o_ref[s] += parts[s]

    return pl.pallas_call(
        body, grid=(N_PARTS, t // tk),
        in_specs=[pl.BlockSpec((tk, D_MODEL), lambda p, kk: (kk, 0)),
                  pl.BlockSpec((1, tk, D_PART), lambda p, kk: (p, kk, 0))],
        out_specs=pl.BlockSpec((2, D_MODEL, CHUNK), lambda p, kk: (p, 0, 0)),
        out_shape=jax.ShapeDtypeStruct((N_CHUNKS, D_MODEL, CHUNK), F32),
        compiler_params=_cp(ARB, ARB), name="w_in_grad",
    )(xn, dproj)


def _w_out_grad(yc, yl, dob):
    t = yc.shape[0]
    tk = 512

    def body(yc_ref, yl_ref, do_ref, o_ref):
        @pl.when(pl.program_id(0) == 0)
        def _():
            o_ref[...] = jnp.zeros_like(o_ref)

        dov = do_ref[...]
        o_ref[0:D_PART, :] += _mm_tn(yc_ref[...], dov)
        o_ref[D_PART:2 * D_PART, :] += _mm_tn(yl_ref[...], dov)

    row = lambda kk: (kk, 0)
    out = pl.pallas_call(
        body, grid=(t // tk,),
        in_specs=[pl.BlockSpec((tk, D_PART), row), pl.BlockSpec((tk, D_PART), row), pl.BlockSpec((tk, D_MODEL), row)],
        out_specs=pl.BlockSpec((2 * D_PART, D_MODEL), lambda kk: (0, 0)),
        out_shape=jax.ShapeDtypeStruct((2 * D_PART, D_MODEL), F32),
        compiler_params=_cp(ARB), name="w_out_grad",
    )(yc, yl, dob)
    return out.reshape(N_CHIPS, 2 * D_PART // N_CHIPS, D_MODEL)


def _for_groups(n, fn, init):
    def trip(j, carry):
        for uu in range(UNROLL):
            carry = fn(j * UNROLL + uu, carry)
        return carry

    return lax.fori_loop(0, n // UNROLL, trip, init)


def _pvb(pv_ref, r):
    return jnp.broadcast_to(pv_ref[r:r + 1, :], (SUBLANES, pv_ref.shape[1]))


def _conv3(pv_ref, u, u1, u2):
    return (_pvb(pv_ref, PV_CONV_W) * u2 + _pvb(pv_ref, PV_CONV_W + 1) * u1) + _pvb(pv_ref, PV_CONV_W + 2) * u


def _conv4(pv_ref, v, v1, v2, v3):
    return ((((_pvb(pv_ref, PV_LRU_W) * v3 + _pvb(pv_ref, PV_LRU_W + 1) * v2) + _pvb(pv_ref, PV_LRU_W + 2) * v1)
             + _pvb(pv_ref, PV_LRU_W + 3) * v) + _pvb(pv_ref, PV_LRU_B))


def _mixer_forward(proj, pvec, wai, w_out):
    t = proj.shape[0]
    tb = 512
    ng = tb // SUBLANES
    nt = t // tb

    def body(bg_ref, cg_ref, xc_ref, gc_ref, xl_ref, gl_ref, pv_ref, wai_ref, wo_ref,
             yc_ref, yl_ref, h_ref, wo4_ref,
             ucp_s, xlp_s, ls_s, hbuf_s, u_s, gate_s, zc_s, zl_s, wob_s, local_sem, send_sems, recv_sems):
        _gather_w_out(pl.program_id(0) * nt + pl.program_id(1), NS * nt, wo_ref, wob_s, wo4_ref, local_sem, send_sems, recv_sems)

        @pl.when(pl.program_id(1) == 0)
        def _():
            ucp_s[...] = jnp.zeros_like(ucp_s)
            xlp_s[...] = jnp.zeros_like(xlp_s)
            hbuf_s[...] = jnp.zeros_like(hbuf_s)

        row = lax.broadcasted_iota(jnp.int32, (SUBLANES, LW), 0)
        ls_s[...] = _log_sigmoid(_pvb(pv_ref, PV_LAM))

        def conv_group(g, carry):
            ucp, xlp = carry
            sl = pl.ds(pl.multiple_of(g * SUBLANES, SUBLANES), SUBLANES)
            uc = cg_ref[sl, :] * xc_ref[sl, :]
            v = _conv3(pv_ref, uc, _shift_down(uc, ucp, 1, row), _shift_down(uc, ucp, 2, row))
            yc = bg_ref[sl, :] * v
            rr = lax.rsqrt(_head_mean(yc * yc, CONV_HEAD) + RMS_EPS)
            gc = gc_ref[sl, :]
            zc_s[sl, :] = ((yc * rr) * _pvb(pv_ref, PV_CG)) * (gc * _sigmoid(gc))
            xl = xl_ref[sl, :]
            u_s[sl, :] = _conv4(pv_ref, xl, _shift_down(xl, xlp, 1, row), _shift_down(xl, xlp, 2, row),
                                _shift_down(xl, xlp, 3, row))
            return uc, xl

        ucp, xlp = _for_groups(ng, conv_group, (ucp_s[...], xlp_s[...]))
        ucp_s[...] = ucp
        xlp_s[...] = xlp

        gate_s[...] = _mm(u_s[...].astype(MXU_DTYPE), wai_ref[0])

        def lru_group(g, h_before):
            sl = pl.ds(pl.multiple_of(g * SUBLANES, SUBLANES), SUBLANES)
            u = u_s[sl, :]
            r, ig, a, e2, mult, _ = _gates(gate_s[sl, 0:LW] + _pvb(pv_ref, PV_BA),
                                           gate_s[sl, LW:2 * LW] + _pvb(pv_ref, PV_BI), u, ls_s[...])
            A, B = _scan8_fwd(a, mult * (ig * u), row)
            h = B + A * jnp.broadcast_to(h_before[SUBLANES - 1:SUBLANES, :], (SUBLANES, LW))
            h_ref[sl, :] = h
            rr = lax.rsqrt(_head_mean(h * h, LRU_HEAD) + RMS_EPS)
            gl = gl_ref[sl, :]
            zl_s[sl, :] = ((h * rr) * _pvb(pv_ref, PV_LG)) * (gl * _sigmoid(gl))
            return h

        hbuf_s[...] = _for_groups(ng, lru_group, hbuf_s[...])
        yc_ref[...] = zc_s[...].astype(MXU_DTYPE)
        yl_ref[...] = zl_s[...].astype(MXU_DTYPE)

    def part(p):
        return pl.BlockSpec((tb, LW), lambda c, i: (i, p * NS + c))

    strip = pl.BlockSpec((tb, LW), lambda c, i: (i, c))
    return pl.pallas_call(
        body, grid=(NS, nt),
        in_specs=[part(p) for p in range(N_PARTS)] + [
            pl.BlockSpec((PV_ROWS, LW), lambda c, i: (0, c)),
            pl.BlockSpec((1, LW, 2 * LW), lambda c, i: (c, 0, 0)),
            pl.BlockSpec(w_out.shape, lambda c, i: (0, 0))],
        out_specs=(strip, strip, strip, pl.BlockSpec(memory_space=pl.ANY)),
        out_shape=(jax.ShapeDtypeStruct((t, D_PART), MXU_DTYPE), jax.ShapeDtypeStruct((t, D_PART), MXU_DTYPE),
                   jax.ShapeDtypeStruct((t, D_PART), F32), jax.ShapeDtypeStruct((N_CHIPS,) + w_out.shape, MXU_DTYPE)),
        scratch_shapes=[pltpu.VMEM((SUBLANES, LW), F32), pltpu.VMEM((SUBLANES, LW), F32), pltpu.VMEM((SUBLANES, LW), F32),
                        pltpu.VMEM((SUBLANES, LW), F32), pltpu.VMEM((tb, LW), F32), pltpu.VMEM((tb, 2 * LW), F32),
                        pltpu.VMEM((tb, LW), F32), pltpu.VMEM((tb, LW), F32), pltpu.VMEM(w_out.shape, MXU_DTYPE),
                        pltpu.SemaphoreType.DMA, pltpu.SemaphoreType.DMA((6,)), pltpu.SemaphoreType.DMA((6,))],
        compiler_params=_cp(ARB, ARB), name="mixer_forward",
    )(proj, proj, proj, proj, proj, proj, pvec, wai, w_out)


def _mixer_backward(proj, h, dy, pvec, wai, sb_out):
    t = proj.shape[0]
    tb = 512
    ng = tb // SUBLANES
    nt = t // tb
    gpb = tb // SUBLANES

    def body(bg_ref, cg_ref, xc_ref, gc_ref, xl_ref, gl_ref, h_ref, dyc_ref, dyl_ref,
             cgh_ref, xch_ref, xlh_ref, hh_ref, pv_ref, wai_ref, so_ref,
             dp_ref, gw_ref, sv_ref, ro_ref,
             ls_s, u_s, uce_s, xle_s, he_s, gate_s, dgate_s, du_s, gbuf_s,
             p0_s, p1_s, p2_s, p3_s, p4_s, p5_s, acc_s, an_s, dvn_s, dun_s, send_sems, recv_sems):
        i = pl.program_id(1)
        first_block = i == nt - 1

        @pl.when((pl.program_id(0) == 0) & (i == 0))
        def _():
            for cp in _chip_block_copies(so_ref, ro_ref, 1, send_sems, recv_sems):
                cp.start()

        @pl.when((pl.program_id(0) == NS - 1) & (i == nt - 1))
        def _():
            for cp in _chip_block_copies(so_ref, ro_ref, 1, send_sems, recv_sems):
                cp.wait()

        @pl.when(i == 0)
        def _():
            acc_s[...] = jnp.zeros_like(acc_s)
            gw_ref[...] = jnp.zeros_like(gw_ref)
            an_s[...] = jnp.zeros_like(an_s)
            dvn_s[...] = jnp.zeros_like(dvn_s)
            dun_s[...] = jnp.zeros_like(dun_s)
            gbuf_s[...] = jnp.zeros_like(gbuf_s)

        row = lax.broadcasted_iota(jnp.int32, (SUBLANES, LW), 0)
        ls_s[...] = _log_sigmoid(_pvb(pv_ref, PV_LAM))
        keep = jnp.where(first_block, 0.0, 1.0)
        uce_s[0:SUBLANES, :] = (cgh_ref[...] * xch_ref[...]) * keep
        xle_s[0:SUBLANES, :] = xlh_ref[...] * keep
        he_s[0:SUBLANES, :] = hh_ref[...] * keep
        xle_s[SUBLANES:SUBLANES + tb, :] = xl_ref[...]
        he_s[SUBLANES:SUBLANES + tb, :] = h_ref[...]

        def recompute_group(g, carry):
            r0 = pl.multiple_of(g * SUBLANES, SUBLANES)
            sl = pl.ds(r0, SUBLANES)
            uce_s[pl.ds(r0 + SUBLANES, SUBLANES), :] = cg_ref[sl, :] * xc_ref[sl, :]
            xl = xle_s[pl.ds(r0 + SUBLANES, SUBLANES), :]
            xlp = xle_s[sl, :]
            u_s[sl, :] = _conv4(pv_ref, xl, _shift_down(xl, xlp, 1, row), _shift_down(xl, xlp, 2, row),
                                _shift_down(xl, xlp, 3, row))
            return carry

        _for_groups(ng, recompute_group, 0)
        gate_s[...] = _mm(u_s[...].astype(MXU_DTYPE), wai_ref[0])

        def acc_add(k, v):
            acc_s[k] += v

        def main_group(gi, carry):
            a_next, dv_next, g_next = carry
            g = ng - 1 - gi
            r0 = pl.multiple_of(g * SUBLANES, SUBLANES)
            sl = pl.ds(r0, SUBLANES)
            sl_e = pl.ds(r0 + SUBLANES, SUBLANES)
            lsb = ls_s[...]
            u = u_s[sl, :]
            r, ig, a, e2, mult, inv_mult = _gates(gate_s[sl, 0:LW] + _pvb(pv_ref, PV_BA),
                                                  gate_s[sl, LW:2 * LW] + _pvb(pv_ref, PV_BI), u, lsb)
            gl = gl_ref[sl, :]
            sg = _sigmoid(gl)
            s_l = gl * sg
            h8 = he_s[sl_e, :]
            hprev = _shift_down(h8, he_s[sl, :], 1, row)
            rr = lax.rsqrt(_head_mean(h8 * h8, LRU_HEAD) + RMS_EPS)
            n = h8 * rr
            dz = dyl_ref[sl, :]
            lg = _pvb(pv_ref, PV_LG)
            acc_add(PV_LG, (dz * n) * s_l)
            p5_s[sl, :] = ((dz * n) * lg) * (sg * (1.0 + gl * (1.0 - sg)))
            dn = (dz * lg) * s_l
            dh = rr * (dn - n * _head_mean(dn * n, LRU_HEAD))
            A, B = _scan8_rev(_shift_up(a, a_next, 1, row), dh, row)
            gg = B + A * jnp.broadcast_to(g_next[0:1, :], (SUBLANES, LW))
            da = gg * hprev
            iu = ig * u
            diu = gg * mult
            dla = da * a - (gg * iu) * (e2 * inv_mult)
            acc_add(PV_LAM, dla * (RG_LRU_C * r))
            dra = (dla * (RG_LRU_C * lsb)) * (r * (1.0 - r))
            dia = (diu * u) * (ig * (1.0 - ig))
            dgate_s[sl, 0:LW] = dra
            dgate_s[sl, LW:2 * LW] = dia
            acc_add(PV_BA, dra)
            acc_add(PV_BI, dia)
            du_s[sl, :] = diu * ig
            bg = bg_ref[sl, :]
            gc = gc_ref[sl, :]
            uc = uce_s[sl_e, :]
            ucp = uce_s[sl, :]
            uc1 = _shift_down(uc, ucp, 1, row)
            uc2 = _shift_down(uc, ucp, 2, row)
            v = _conv3(pv_ref, uc, uc1, uc2)
            yc = bg * v
            rrc = lax.rsqrt(_head_mean(yc * yc, CONV_HEAD) + RMS_EPS)
            nc = yc * rrc
            sgc = _sigmoid(gc)
            s_c = gc * sgc
            dzc = dyc_ref[sl, :]
            cgain = _pvb(pv_ref, PV_CG)
            acc_add(PV_CG, (dzc * nc) * s_c)
            p3_s[sl, :] = ((dzc * nc) * cgain) * (sgc * (1.0 + gc * (1.0 - sgc)))
            dnc = (dzc * cgain) * s_c
            dyc = rrc * (dnc - nc * _head_mean(dnc * nc, CONV_HEAD))
            p0_s[sl, :] = dyc * v
            dv = dyc * bg
            duc = (_pvb(pv_ref, PV_CONV_W + 2) * dv + _pvb(pv_ref, PV_CONV_W + 1) * _shift_up(dv, dv_next, 1, row)
                   + _pvb(pv_ref, PV_CONV_W) * _shift_up(dv, dv_next, 2, row))
            acc_add(PV_CONV_W + 2, dv * uc)
            acc_add(PV_CONV_W + 1, dv * uc1)
            acc_add(PV_CONV_W, dv * uc2)
            p1_s[sl, :] = duc * xc_ref[sl, :]
            p2_s[sl, :] = duc * cg_ref[sl, :]
            return a, dv, gg

        a_next, dv_next, g_next = _for_groups(ng, main_group, (an_s[...], dvn_s[...], gbuf_s[...]))
        an_s[...] = a_next
        dvn_s[...] = dv_next
        gbuf_s[...] = g_next

        dgb = dgate_s[...].astype(MXU_DTYPE)
        du_s[...] += _mm_nt(dgb, wai_ref[0])
        gw_ref[0] += _mm_tn(u_s[...].astype(MXU_DTYPE), dgb)

        def lru_conv_group(gi, du_next):
            g = ng - 1 - gi
            r0 = pl.multiple_of(g * SUBLANES, SUBLANES)
            sl = pl.ds(r0, SUBLANES)
            du = du_s[sl, :]
            xl = xle_s[pl.ds(r0 + SUBLANES, SUBLANES), :]
            xlp = xle_s[sl, :]
            acc_add(PV_LRU_B, du)
            acc_add(PV_LRU_W + 3, du * xl)
            acc_add(PV_LRU_W + 2, du * _shift_down(xl, xlp, 1, row))
            acc_add(PV_LRU_W + 1, du * _shift_down(xl, xlp, 2, row))
            acc_add(PV_LRU_W, du * _shift_down(xl, xlp, 3, row))
            p4_s[sl, :] = (((_pvb(pv_ref, PV_LRU_W + 3) * du + _pvb(pv_ref, PV_LRU_W + 2) * _shift_up(du, du_next, 1, row))
                            + _pvb(pv_ref, PV_LRU_W + 1) * _shift_up(du, du_next, 2, row))
                           + _pvb(pv_ref, PV_LRU_W) * _shift_up(du, du_next, 3, row))
            return du

        dun_s[...] = _for_groups(ng, lru_conv_group, dun_s[...])

        for p, p_s in enumerate((p0_s, p1_s, p2_s, p3_s, p4_s, p5_s)):
            dp_ref[p] = p_s[...].astype(MXU_DTYPE)

        @pl.when(first_block)
        def _():
            sv_ref[...] = jnp.zeros_like(sv_ref)
            for k in range(N_ACC):
                tot = jnp.sum(acc_s[k], axis=0, keepdims=True)
                if k == PV_LAM:
                    tot = tot / (1.0 + jnp.exp(pv_ref[PV_LAM:PV_LAM + 1, :]))
                sv_ref[k:k + 1, :] = tot

    def part(p):
        return pl.BlockSpec((tb, LW), lambda c, i: (nt - 1 - i, p * NS + c))

    def halo(p):
        return pl.BlockSpec((SUBLANES, LW), lambda c, i: (jnp.maximum((nt - 1 - i) * gpb - 1, 0), p * NS + c))

    strip = pl.BlockSpec((tb, LW), lambda c, i: (nt - 1 - i, c))
    big = pltpu.VMEM((tb, LW), F32)
    big_e = pltpu.VMEM((tb + SUBLANES, LW), F32)
    wide = pltpu.VMEM((tb, 2 * LW), F32)
    small = pltpu.VMEM((SUBLANES, LW), F32)
    outs = pl.pallas_call(
        body, grid=(NS, nt),
        in_specs=[part(p) for p in range(N_PARTS)] + [
            strip, strip, pl.BlockSpec((tb, LW), lambda c, i: (nt - 1 - i, NS + c)),
            halo(1), halo(2), halo(4),
            pl.BlockSpec((SUBLANES, LW), lambda c, i: (jnp.maximum((nt - 1 - i) * gpb - 1, 0), c)),
            pl.BlockSpec((PV_ROWS, LW), lambda c, i: (0, c)),
            pl.BlockSpec((1, LW, 2 * LW), lambda c, i: (c, 0, 0)),
            pl.BlockSpec(memory_space=pl.ANY)],
        out_specs=(pl.BlockSpec((N_PARTS, tb, LW), lambda c, i: (0, nt - 1 - i, c)),
                   pl.BlockSpec((1, LW, 2 * LW), lambda c, i: (c, 0, 0)),
                   pl.BlockSpec((PV_ROWS, LW), lambda c, i: (0, c)),
                   pl.BlockSpec(memory_space=pl.ANY)),
        out_shape=(jax.ShapeDtypeStruct((N_PARTS, t, D_PART), MXU_DTYPE),
                   jax.ShapeDtypeStruct((NS, LW, 2 * LW), F32), jax.ShapeDtypeStruct((PV_ROWS, D_PART), F32),
                   _chip_blocks_shape(sb_out, 1)),
        scratch_shapes=[small, big, big_e, big_e, big_e, wide, wide, big, small,
                        big, big, big, big, big, big, pltpu.VMEM((N_ACC, SUBLANES, LW), F32), small, small, small,
                        pltpu.SemaphoreType.DMA((3,)), pltpu.SemaphoreType.DMA((3,))],
        compiler_params=_cp(ARB, ARB), name="mixer_backward",
    )(proj, proj, proj, proj, proj, proj, h, dy, dy, proj, proj, proj, h, pvec, wai, sb_out)
    return outs


def _adamw(w, g, m, v):
    m = ADAM_B1 * m + (1.0 - ADAM_B1) * g
    v = ADAM_B2 * v + (1.0 - ADAM_B2) * (g * g)
    m_hat = m / (1.0 - ADAM_B1 ** ADAM_STEP)
    v_hat = v / (1.0 - ADAM_B2 ** ADAM_STEP)
    delta = -ADAM_LR * (m_hat / (jnp.sqrt(v_hat) + ADAM_EPS) + ADAM_WD * w)
    return delta, m, v


def _adam_w_in(w, m, v, g3):
    rows, cols = w.shape
    tr = 128

    def body(w_ref, m_ref, v_ref, g_ref, go_ref, d_ref, mo_ref, vo_ref):
        for s in range(CHUNKS_PER_BLOCK):
            cs = slice(CHUNK * s, CHUNK * (s + 1))
            g = g_ref[s]
            d, mn, vn = _adamw(w_ref[:, cs], g, m_ref[:, cs], v_ref[:, cs])
            go_ref[:, cs] = g
            d_ref[:, cs] = d
            mo_ref[:, cs] = mn
            vo_ref[:, cs] = vn

    blk = pl.BlockSpec((tr, cols), lambda i: (i, 0))
    return pl.pallas_call(
        body, grid=(rows // tr,),
        in_specs=[blk, blk, blk, pl.BlockSpec((CHUNKS_PER_BLOCK, tr, CHUNK), lambda i: (0, i, 0))],
        out_specs=(blk,) * 4, out_shape=(jax.ShapeDtypeStruct(w.shape, F32),) * 4,
        compiler_params=_cp(ARB), name="adam_w_in",
    )(w, m, v, g3)


def _adam_w_out(w, m, v, g):
    rows, cols = w.shape
    tr = 128

    def body(w_ref, m_ref, v_ref, g_ref, d_ref, mo_ref, vo_ref):
        d_ref[...], mo_ref[...], vo_ref[...] = _adamw(w_ref[...], g_ref[...], m_ref[...], v_ref[...])

    blk = pl.BlockSpec((tr, cols), lambda i: (i, 0))
    return pl.pallas_call(
        body, grid=(rows // tr,), in_specs=[blk] * 4, out_specs=(blk,) * 3,
        out_shape=(jax.ShapeDtypeStruct(w.shape, F32),) * 3,
        compiler_params=_cp(ARB), name="adam_w_out",
    )(w, m, v, g)


def _adam_small(ws, ms, vs, gs):
    n = len(ws)

    def body(*refs):
        w_r, m_r, v_r, g_r = refs[0:n], refs[n:2 * n], refs[2 * n:3 * n], refs[3 * n:4 * n]
        d_o, m_o, v_o = refs[4 * n:5 * n], refs[5 * n:6 * n], refs[6 * n:7 * n]
        for j in range(n):
            d_o[j][...], m_o[j][...], v_o[j][...] = _adamw(w_r[j][...], g_r[j][...], m_r[j][...], v_r[j][...])

    vm = pl.BlockSpec(memory_space=pltpu.VMEM)
    shapes = tuple(jax.ShapeDtypeStruct(w.shape, F32) for w in ws)
    outs = pl.pallas_call(
        body, in_specs=[vm] * (4 * n), out_specs=(vm,) * (3 * n), out_shape=shapes * 3,
        compiler_params=_cp(), name="adam_small",
    )(*ws, *ms, *vs, *gs)
    return outs[0:n], outs[n:2 * n], outs[2 * n:3 * n]


def _block_diag_strips(w):
    w4 = w.reshape(NS, HEADS_PER_STRIP, LRU_HEAD, LRU_HEAD)
    bd = jnp.zeros((NS, HEADS_PER_STRIP, LRU_HEAD, HEADS_PER_STRIP, LRU_HEAD), w.dtype)
    for hh in range(HEADS_PER_STRIP):
        bd = bd.at[:, hh, :, hh, :].set(w4[:, hh])
    return bd.reshape(NS, LW, LW)


def _strip_diag_blocks(g):
    g5 = g.reshape(NS, HEADS_PER_STRIP, LRU_HEAD, HEADS_PER_STRIP, LRU_HEAD)
    return jnp.stack([g5[:, hh, :, hh, :] for hh in range(HEADS_PER_STRIP)], axis=1).reshape(NS * HEADS_PER_STRIP, LRU_HEAD, LRU_HEAD)


def kernel(x, ln_g, w_in, conv_w, lru_conv_w, lru_conv_b, w_a, b_a, w_i, b_i, lam, conv_out_g, lru_out_g, w_out, final_g, loss_target, m_ln_g, m_w_in, m_conv_w, m_lru_conv_w, m_lru_conv_b, m_w_a, m_b_a, m_w_i, m_b_i, m_lam, m_conv_out_g, m_lru_out_g, m_w_out, m_final_g, v_ln_g, v_w_in, v_conv_w, v_lru_conv_w, v_lru_conv_b, v_w_a, v_b_a, v_w_i, v_b_i, v_lam, v_conv_out_g, v_lru_out_g, v_w_out, v_final_g):
    xi, yi, ci = lax.axis_index("x"), lax.axis_index("y"), lax.axis_index("c")
    k = 2 * xi + yi
    t = x.shape[1]
    x2 = x.reshape(t, D_MODEL)
    tgt2 = loss_target.reshape(t, D_MODEL)
    row = lambda a: a.reshape(1, -1)

    small = jnp.concatenate([conv_w, lru_conv_w, jnp.zeros((1, conv_w.shape[1]), F32)], axis=0)
    w12, sm4 = _allgather_weights(w_in, small)
    convs = jnp.transpose(sm4, (1, 0, 2)).reshape(SUBLANES, D_PART)
    pvec = jnp.concatenate(
        [convs[0:7], row(lru_conv_b), row(b_a), row(b_i), row(lam), row(conv_out_g), row(lru_out_g),
         jnp.zeros((PV_ROWS - N_ACC, D_PART), F32)], axis=0)
    wai = jnp.concatenate([_block_diag_strips(w_a), _block_diag_strips(w_i)], axis=2).astype(MXU_DTYPE)

    c_arr = jnp.reshape(ci, (1,)).astype(jnp.int32)
    kc_arr = jnp.stack([k, ci]).astype(jnp.int32)
    proj, xn = _in_projection(x2, row(ln_g), w12)
    yc, yl, h, wo4 = _mixer_forward(proj, pvec, wai, w_out)
    wo = wo4.reshape(2 * D_PART, D_MODEL)
    do, dob, dy, st_out = _out_projection_loss(yc, yl, x2, tgt2, wo, row(final_g))
    go4 = _w_out_grad(yc, yl, dob)
    s_out, sb_out = _add_own_half(go4, _exchange_sibling_halves(go4, "exchange_sibling_halves_out"), c_arr, "add_own_half_out")
    dproj, g_wai, svec, r2o = _mixer_backward(proj, h, dy, pvec, wai, sb_out)
    g12 = _w_in_grad(xn, dproj)
    s_in, sb_in = _add_own_half(g12, _exchange_sibling_halves(g12, "exchange_sibling_halves_in"), c_arr, "add_own_half_in")
    grad_x, st_in, r2i = _input_grad(dproj, w12, x2, do, row(ln_g), sb_in)
    f_in = _sum_chip_blocks(s_in, r2i, kc_arr, CHUNKS_PER_BLOCK, "sum_chip_blocks_in")
    f_out = _sum_chip_blocks(s_out, r2o, kc_arr, 1, "sum_chip_blocks_out")
    f_in, f_out = _swap_sibling_halves(f_in, f_out)

    gwa = _strip_diag_blocks(g_wai[:, :, 0:LW]).reshape(LRU_HEAD, D_PART)
    gwi = _strip_diag_blocks(g_wai[:, :, LW:2 * LW]).reshape(LRU_HEAD, D_PART)
    red = _allreduce_small(jnp.concatenate([svec, st_out, st_in, gwa, gwi], axis=0))
    r_out = PV_ROWS
    r_in = PV_ROWS + SUBLANES
    r_wa = PV_ROWS + 2 * SUBLANES
    r_wi = r_wa + LRU_HEAD
    loss = red[r_out + 1, 0]

    g_w_in, d_w_in, nm_w_in, nv_w_in = _adam_w_in(w_in, m_w_in, v_w_in, f_in)
    g_w_out = f_out[0]
    d_w_out, nm_w_out, nv_w_out = _adam_w_out(w_out, m_w_out, v_w_out, g_w_out)

    ncol = conv_w.shape[1]
    conv_cols = lax.dynamic_slice(red, (0, k * ncol), (SUBLANES, ncol))
    g_small = {
        "ln_g": red[r_in], "conv_w": conv_cols[0:3], "lru_conv_w": conv_cols[3:7], "lru_conv_b": red[PV_LRU_B],
        "w_a": red[r_wa:r_wa + LRU_HEAD].reshape(w_a.shape), "b_a": red[PV_BA],
        "w_i": red[r_wi:r_wi + LRU_HEAD].reshape(w_i.shape), "b_i": red[PV_BI], "lam": red[PV_LAM],
        "conv_out_g": red[PV_CG], "lru_out_g": red[PV_LG], "final_g": red[r_out],
    }
    w_small = {"ln_g": ln_g, "conv_w": conv_w, "lru_conv_w": lru_conv_w, "lru_conv_b": lru_conv_b, "w_a": w_a, "b_a": b_a,
               "w_i": w_i, "b_i": b_i, "lam": lam, "conv_out_g": conv_out_g, "lru_out_g": lru_out_g, "final_g": final_g}
    m_small = {"ln_g": m_ln_g, "conv_w": m_conv_w, "lru_conv_w": m_lru_conv_w, "lru_conv_b": m_lru_conv_b, "w_a": m_w_a,
               "b_a": m_b_a, "w_i": m_w_i, "b_i": m_b_i, "lam": m_lam, "conv_out_g": m_conv_out_g,
               "lru_out_g": m_lru_out_g, "final_g": m_final_g}
    v_small = {"ln_g": v_ln_g, "conv_w": v_conv_w, "lru_conv_w": v_lru_conv_w, "lru_conv_b": v_lru_conv_b, "w_a": v_w_a,
               "b_a": v_b_a, "w_i": v_w_i, "b_i": v_b_i, "lam": v_lam, "conv_out_g": v_conv_out_g,
               "lru_out_g": v_lru_out_g, "final_g": v_final_g}
    names = list(w_small)
    as2d = lambda a: a.reshape(1, -1) if a.ndim == 1 else a
    d_s, m_s, v_s = _adam_small([as2d(w_small[n]) for n in names], [as2d(m_small[n]) for n in names],
                                [as2d(v_small[n]) for n in names], [as2d(g_small[n]) for n in names])
    back = lambda n, a: a.reshape(w_small[n].shape)
    grads = {n: g_small[n] for n in names}
    deltas = {n: back(n, a) for n, a in zip(names, d_s)}
    new_m = {n: back(n, a) for n, a in zip(names, m_s)}
    new_v = {n: back(n, a) for n, a in zip(names, v_s)}
    grads["w_in"], deltas["w_in"], new_m["w_in"], new_v["w_in"] = g_w_in, d_w_in, nm_w_in, nv_w_in
    grads["w_out"], deltas["w_out"], new_m["w_out"], new_v["w_out"] = g_w_out, d_w_out, nm_w_out, nv_w_out

    order = ["ln_g", "w_in", "conv_w", "lru_conv_w", "lru_conv_b", "w_a", "b_a", "w_i", "b_i", "lam", "conv_out_g",
             "lru_out_g", "w_out", "final_g"]
    return (loss, grad_x.reshape(x.shape), *[grads[n] for n in order], *[deltas[n] for n in order],
            *[new_m[n] for n in order], *[new_v[n] for n in order])
```

```python
import functools

import jax
import jax.numpy as jnp
from jax import lax
from jax.experimental import pallas as pl
from jax.experimental.pallas import tpu as pltpu

F32 = jnp.float32
MXU_DTYPE = jnp.bfloat16

D_MODEL = 1024
D_PART = 1024
N_PARTS = 6
CHUNK = 512
CHUNKS_PER_BLOCK = 3
N_CHUNKS = 12
N_CHIPS = 4
SUBLANES = 8
LANES = 128
LW = 256
UNROLL = 8
NS = D_PART // LW
STRIPS_PER_CHUNK = CHUNK // LW
CONV_HEAD = 128
LRU_HEAD = 64
HEADS_PER_STRIP = LW // LRU_HEAD
RMS_EPS = 1e-6
RG_LRU_C = 8.0
ADAM_LR = 0.001
ADAM_B1 = 0.9
ADAM_B2 = 0.999
ADAM_EPS = 1e-08
ADAM_WD = 0.01
ADAM_STEP = 10

PV_CONV_W = 0
PV_LRU_W = 3
PV_LRU_B = 7
PV_BA = 8
PV_BI = 9
PV_LAM = 10
PV_CG = 11
PV_LG = 12
PV_ROWS = 16
N_ACC = 13

SLAB = 128
MESH = pl.DeviceIdType.MESH
VMEM_LIMIT = 56 * 1024 * 1024
ARB = "arbitrary"


def _cp(*sem, **kw):
    return pltpu.CompilerParams(dimension_semantics=sem or None, vmem_limit_bytes=VMEM_LIMIT, **kw)


def _mm(a, b):
    return jnp.dot(a, b, preferred_element_type=F32)


def _mm_nt(a, b):
    return lax.dot_general(a, b, (((1,), (1,)), ((), ())), preferred_element_type=F32)


def _mm_tn(a, b):
    return lax.dot_general(a, b, (((0,), (0,)), ((), ())), preferred_element_type=F32)


def _sigmoid(x):
    return 0.5 * jnp.tanh(0.5 * x) + 0.5


def _log_sigmoid(x):
    z = jnp.exp(-jnp.abs(x))
    u = 1.0 + z
    log1p = jnp.where(u == 1.0, z, jnp.log(u) * z / (u - 1.0))
    return jnp.minimum(x, 0.0) - log1p


def _head_mean(z, head):
    out = []
    for k in range(z.shape[1] // LANES):
        zk = z[:, LANES * k:LANES * (k + 1)]
        if head == LANES:
            m = jnp.sum(zk, axis=-1, keepdims=True) * (1.0 / head)
            out.append(jnp.broadcast_to(m, zk.shape))
        else:
            lo = lax.broadcasted_iota(jnp.int32, zk.shape, 1) < head
            s_lo = jnp.sum(jnp.where(lo, zk, 0.0), axis=-1, keepdims=True)
            s_hi = jnp.sum(jnp.where(lo, 0.0, zk), axis=-1, keepdims=True)
            out.append(jnp.where(lo, s_lo, s_hi) * (1.0 / head))
    return jnp.concatenate(out, axis=1)


def _shift_down(cur, prev, d, row):
    return pltpu.roll(jnp.where(row < SUBLANES - d, cur, prev), d, 0)


def _shift_up(cur, nxt, d, row):
    return pltpu.roll(jnp.where(row >= d, cur, nxt), SUBLANES - d, 0)


def _scan8_fwd(a, b, row):
    A, B = a, b
    for d in (1, 2, 4):
        m = row >= d
        a_s = jnp.where(m, pltpu.roll(A, d, 0), 1.0)
        b_s = jnp.where(m, pltpu.roll(B, d, 0), 0.0)
        B = A * b_s + B
        A = A * a_s
    return A, B


def _scan8_rev(a, b, row):
    A, B = a, b
    for d in (1, 2, 4):
        m = row < SUBLANES - d
        a_s = jnp.where(m, pltpu.roll(A, SUBLANES - d, 0), 1.0)
        b_s = jnp.where(m, pltpu.roll(B, SUBLANES - d, 0), 0.0)
        B = A * b_s + B
        A = A * a_s
    return A, B


def _gates(ra, ia, u, lsb):
    r = _sigmoid(ra)
    ig = _sigmoid(ia)
    la = (RG_LRU_C * r) * lsb
    a = jnp.exp(la)
    e2 = a * a
    em = -jnp.tanh(la) * (1.0 + e2)
    inv_mult = lax.rsqrt(em)
    return r, ig, a, e2, em * inv_mult, inv_mult


def _mesh_pos():
    x, y, c = lax.axis_index("x"), lax.axis_index("y"), lax.axis_index("c")
    chips = [(1 - x, y), (x, 1 - y), (1 - x, 1 - y)]
    return x, y, c, chips


def _gather_in_projection(x, ln_g, w_in, small):
    t = x.shape[0]
    rb_x = 512
    rb_mm = 1024
    n_mm = t // rb_mm
    half = w_in.shape[0] // 2
    n_ici = 3 * CHUNKS_PER_BLOCK

    def body(x_hbm, g_ref, wi_ref, sm_ref, proj_hbm, xn_ref, w12_ref, sm4_ref,
             xbuf, obuf, x_sems, o_sems, send_sems, recv_sems):
        x_, y_, c, chips = _mesh_pos()
        k = 2 * x_ + y_
        sib = (x_, y_, 1 - c)
        my_rows = pl.ds(pl.multiple_of(half * c, half), half)
        sib_rows = pl.ds(pl.multiple_of(half * (1 - c), half), half)

        for s in range(CHUNKS_PER_BLOCK):
            w12_ref[CHUNKS_PER_BLOCK * k + s] = wi_ref[:, CHUNK * s:CHUNK * (s + 1)].astype(MXU_DTYPE)
        sm4_ref[k] = sm_ref[...]

        def remote(ref, sem, to):
            return pltpu.make_async_remote_copy(src_ref=ref, dst_ref=ref, send_sem=send_sems.at[sem],
                                                recv_sem=recv_sems.at[sem], device_id=to, device_id_type=MESH)

        def chunk_of(chip, s):
            return CHUNKS_PER_BLOCK * (2 * chip[0] + chip[1]) + s

        ici = lambda m, s: 3 * s + m
        fwd = lambda m, s: n_ici + 3 * s + m
        sml = lambda m: 2 * n_ici + m

        sends = []
        for s in range(CHUNKS_PER_BLOCK):
            for m, chip in enumerate(chips):
                sends.append(remote(w12_ref.at[chunk_of((x_, y_), s), my_rows, :], ici(m, s), (*chip, c)))
        for m, chip in enumerate(chips):
            sends.append(remote(sm4_ref.at[k], sml(m), (*chip, c)))
        for cp in sends:
            cp.start()

        def x_copy(rb, slot):
            return pltpu.make_async_copy(x_hbm.at[pl.ds(rb * rb_x, rb_x), :], xbuf.at[slot], x_sems.at[slot])

        x_copy(0, 0).start()
        for rb in range(t // rb_x):
            slot = rb % 2
            x_copy(rb, slot).wait()
            if rb + 1 < t // rb_x:
                x_copy(rb + 1, 1 - slot).start()

            def norm_slab(sl, carry, rb=rb, slot=slot):
                xf = xbuf[slot, pl.ds(pl.multiple_of(sl * SLAB, SLAB), SLAB), :]
                r = lax.rsqrt(jnp.mean(xf * xf, axis=-1, keepdims=True) + RMS_EPS)
                xn_ref[pl.ds(pl.multiple_of(rb * rb_x + sl * SLAB, SLAB), SLAB), :] = ((xf * r) * g_ref[...]).astype(MXU_DTYPE)
                return carry

            lax.fori_loop(0, rb_x // SLAB, norm_slab, 0)

        def out_copy(q, i, slot):
            return pltpu.make_async_copy(obuf.at[slot], proj_hbm.at[q, pl.ds(pl.multiple_of(i * rb_mm, rb_mm), rb_mm), :],
                                         o_sems.at[slot])

        def project(q, very_first):
            def row_block(i, carry):
                slot = i % 2

                def wait_buffer():
                    out_copy(q, i, slot).wait()

                if very_first:
                    pl.when(i >= 2)(wait_buffer)
                else:
                    wait_buffer()
                obuf[slot] = _mm(xn_ref[pl.ds(pl.multiple_of(i * rb_mm, rb_mm), rb_mm), :], w12_ref[q])
                out_copy(q, i, slot).start()
                return carry

            lax.fori_loop(0, n_mm, row_block, 0)

        for s in range(CHUNKS_PER_BLOCK):
            project(chunk_of((x_, y_), s), very_first=(s == 0))

        order = [(m, s) for s in range(CHUNKS_PER_BLOCK) for m in range(3)]
        forwards = []
        for j, (m, s) in enumerate(order):
            q = chunk_of(chips[m], s)
            remote(w12_ref.at[q, my_rows, :], ici(m, s), sib).wait_recv()
            f = remote(w12_ref.at[q, my_rows, :], fwd(m, s), sib)
            f.start()
            forwards.append(f)
            if j > 0:
                pm, ps = order[j - 1]
                pq = chunk_of(chips[pm], ps)
                remote(w12_ref.at[pq, sib_rows, :], fwd(pm, ps), sib).wait_recv()
                project(pq, very_first=False)
        pm, ps = order[-1]
        pq = chunk_of(chips[pm], ps)
        remote(w12_ref.at[pq, sib_rows, :], fwd(pm, ps), sib).wait_recv()
        project(pq, very_first=False)

        for m, chip in enumerate(chips):
            remote(sm4_ref.at[2 * chip[0] + chip[1]], sml(m), sib).wait_recv()
        for cp in sends + forwards:
            cp.wait_send()
        for slot in range(2):
            out_copy(0, slot, slot).wait()

    assert n_mm % 2 == 0 and n_mm >= 2
    vm = pl.BlockSpec(memory_space=pltpu.VMEM)
    hbm = pl.BlockSpec(memory_space=pl.ANY)
    n_sems = 2 * n_ici + 3
    return pl.pallas_call(
        body,
        out_shape=(jax.ShapeDtypeStruct((N_CHUNKS, t, CHUNK), F32), jax.ShapeDtypeStruct((t, D_MODEL), MXU_DTYPE),
                   jax.ShapeDtypeStruct((N_CHUNKS, w_in.shape[0], CHUNK), MXU_DTYPE),
                   jax.ShapeDtypeStruct((N_CHIPS,) + small.shape, F32)),
        in_specs=[hbm, vm, vm, vm], out_specs=(hbm, vm, vm, vm),
        scratch_shapes=[pltpu.VMEM((2, rb_x, D_MODEL), F32), pltpu.VMEM((2, rb_mm, CHUNK), F32),
                        pltpu.SemaphoreType.DMA((2,)), pltpu.SemaphoreType.DMA((2,)),
                        pltpu.SemaphoreType.DMA((n_sems,)), pltpu.SemaphoreType.DMA((n_sems,))],
        compiler_params=_cp(), name="gather_in_projection",
    )(x, ln_g, w_in, small)


def _allreduce_small(buf):
    def body(in_ref, out_ref, r0, r1, r2, send_sems, recv_sems):
        x, y, c, _ = _mesh_pos()
        peers = [(x, y, 1 - c), (1 - x, y, c), (x, 1 - y, c)]
        out_ref[...] = in_ref[...]
        for ph, (peer, rbuf) in enumerate(zip(peers, (r0, r1, r2))):
            cp = pltpu.make_async_remote_copy(src_ref=out_ref, dst_ref=rbuf, send_sem=send_sems.at[ph],
                                              recv_sem=recv_sems.at[ph], device_id=peer, device_id_type=MESH)
            cp.start()
            cp.wait()
            out_ref[...] = out_ref[...] + rbuf[...]

    vm = pl.BlockSpec(memory_space=pltpu.VMEM)
    return pl.pallas_call(
        body, out_shape=jax.ShapeDtypeStruct(buf.shape, F32), in_specs=[vm], out_specs=vm,
        scratch_shapes=[pltpu.VMEM(buf.shape, F32)] * 3 + [pltpu.SemaphoreType.DMA((3,)), pltpu.SemaphoreType.DMA((3,))],
        compiler_params=_cp(), name="allreduce_small",
    )(buf)


def _exchange_sibling_halves(g, name):
    n, rows, cols = g.shape
    half = rows // 2

    def body(g_ref, r_ref, send_sem, recv_sem):
        x, y, c, _ = _mesh_pos()
        cp = pltpu.make_async_remote_copy(src_ref=g_ref.at[:, pl.ds(pl.multiple_of(half * (1 - c), half), half), :],
                                          dst_ref=r_ref, send_sem=send_sem, recv_sem=recv_sem,
                                          device_id=(x, y, 1 - c), device_id_type=MESH)
        cp.start()
        cp.wait()

    hbm = pl.BlockSpec(memory_space=pl.ANY)
    return pl.pallas_call(
        body, out_shape=jax.ShapeDtypeStruct((n, half, cols), F32), in_specs=[hbm], out_specs=hbm,
        scratch_shapes=[pltpu.SemaphoreType.DMA, pltpu.SemaphoreType.DMA],
        compiler_params=_cp(), name=name,
    )(g)


def _add_own_half(g, r, c_arr, name):
    n, rr, cc = r.shape

    def body(c_ref, g_ref, r_ref, o_ref, ob_ref):
        s = g_ref[...] + r_ref[...]
        o_ref[...] = s
        ob_ref[...] = s.astype(jnp.bfloat16)

    blk = pl.BlockSpec((1, rr, cc), lambda q, c_ref: (q, 0, 0))
    return pl.pallas_call(
        body, out_shape=(jax.ShapeDtypeStruct(r.shape, F32), jax.ShapeDtypeStruct(r.shape, jnp.bfloat16)),
        grid_spec=pltpu.PrefetchScalarGridSpec(
            num_scalar_prefetch=1, grid=(n,),
            in_specs=[pl.BlockSpec((1, rr, cc), lambda q, c_ref: (q, c_ref[0], 0)), blk],
            out_specs=(blk, blk)),
        compiler_params=_cp(ARB), name=name,
    )(c_arr, g, r)


def _chip_block_copies(s_ref, r_ref, n_sub, send_sems, recv_sems):
    x, y, c, chips = _mesh_pos()
    cps = []
    for m, chip in enumerate(chips):
        kk = 2 * chip[0] + chip[1]
        cps.append(pltpu.make_async_remote_copy(
            src_ref=s_ref.at[pl.ds(n_sub * kk, n_sub)], dst_ref=r_ref.at[m],
            send_sem=send_sems.at[m], recv_sem=recv_sems.at[m], device_id=(*chip, c), device_id_type=MESH))
    return cps


def _gather_w_out(step, n_steps, wo_ref, wob_s, wo4_ref, local_sem, send_sems, recv_sems):
    x, y, c, chips = _mesh_pos()
    sib = (x, y, 1 - c)
    half = wo_ref.shape[0] // 2

    def rows(core):
        return pl.ds(pl.multiple_of(half * core, half), half)

    def block_half(chip, core):
        return wo4_ref.at[2 * chip[0] + chip[1], rows(core), :]

    def remote(src, dst, sem, to):
        return pltpu.make_async_remote_copy(src_ref=src, dst_ref=dst, send_sem=send_sems.at[sem], recv_sem=recv_sems.at[sem],
                                            device_id=to, device_id_type=MESH)

    local = pltpu.make_async_copy(wob_s, wo4_ref.at[2 * x + y], local_sem)
    ici = [remote(wob_s.at[rows(c), :], block_half((x, y), c), m, (*chip, c)) for m, chip in enumerate(chips)]
    fwd = [remote(block_half(chip, c), block_half(chip, c), 3 + m, sib) for m, chip in enumerate(chips)]

    @pl.when(step == 0)
    def _():
        wob_s[...] = wo_ref[...].astype(MXU_DTYPE)
        local.start()
        for cp in ici:
            cp.start()

    @pl.when(step == n_steps // 2)
    def _():
        for m, chip in enumerate(chips):
            remote(block_half(chip, c), block_half(chip, c), m, sib).wait_recv()
            fwd[m].start()

    @pl.when(step == n_steps - 1)
    def _():
        for m, chip in enumerate(chips):
            remote(block_half(chip, 1 - c), block_half(chip, 1 - c), 3 + m, sib).wait_recv()
        for cp in ici + fwd:
            cp.wait_send()
        local.wait()


def _chip_blocks_shape(s, n_sub):
    return jax.ShapeDtypeStruct((3, n_sub) + s.shape[1:], s.dtype)


def _sum_chip_blocks(s, r, kc_arr, n_sub, name):
    _, rr, cc = s.shape

    def body(kc_ref, s_ref, r_ref, o_ref):
        o_ref[...] = ((s_ref[...] + r_ref[0].astype(F32)) + r_ref[1].astype(F32)) + r_ref[2].astype(F32)

    return pl.pallas_call(
        body, out_shape=jax.ShapeDtypeStruct((n_sub, 2 * rr, cc), F32),
        grid_spec=pltpu.PrefetchScalarGridSpec(
            num_scalar_prefetch=1, grid=(n_sub,),
            in_specs=[pl.BlockSpec((1, rr, cc), lambda q, kc: (n_sub * kc[0] + q, 0, 0)),
                      pl.BlockSpec((3, 1, rr, cc), lambda q, kc: (0, q, 0, 0))],
            out_specs=pl.BlockSpec((1, rr, cc), lambda q, kc: (q, kc[1], 0))),
        compiler_params=_cp(ARB), name=name,
    )(kc_arr, s, r)


def _swap_sibling_halves(f_in, f_out):
    hi, ho = f_in.shape[1] // 2, f_out.shape[1] // 2

    def body(fi_in, fo_in, fi_ref, fo_ref, send_sems, recv_sems):
        del fi_in, fo_in
        x, y, c, _ = _mesh_pos()
        sib = (x, y, 1 - c)
        si = fi_ref.at[:, pl.ds(pl.multiple_of(hi * c, hi), hi), :]
        so = fo_ref.at[:, pl.ds(pl.multiple_of(ho * c, ho), ho), :]
        cps = [
            pltpu.make_async_remote_copy(src_ref=si, dst_ref=si, send_sem=send_sems.at[0], recv_sem=recv_sems.at[0],
                                         device_id=sib, device_id_type=MESH),
            pltpu.make_async_remote_copy(src_ref=so, dst_ref=so, send_sem=send_sems.at[1], recv_sem=recv_sems.at[1],
                                         device_id=sib, device_id_type=MESH),
        ]
        for cp in cps:
            cp.start()
        for cp in cps:
            cp.wait()

    hbm = pl.BlockSpec(memory_space=pl.ANY)
    return pl.pallas_call(
        body,
        out_shape=(jax.ShapeDtypeStruct(f_in.shape, F32), jax.ShapeDtypeStruct(f_out.shape, F32)),
        in_specs=[hbm, hbm], out_specs=(hbm, hbm), input_output_aliases={0: 0, 1: 1},
        scratch_shapes=[pltpu.SemaphoreType.DMA((2,)), pltpu.SemaphoreType.DMA((2,))],
        compiler_params=_cp(), name="swap_sibling_halves",
    )(f_in, f_out)


def _out_projection_loss(yc, yl, x, target, wo, final_g):
    t = x.shape[0]
    tm = 256

    def body(yc_ref, yl_ref, x_ref, t_ref, wo_ref, fg_ref, do_ref, dob_ref, dy_ref, st_ref):
        @pl.when(pl.program_id(0) == 0)
        def _():
            st_ref[...] = jnp.zeros_like(st_ref)

        o = x_ref[...] + (_mm(yc_ref[...], wo_ref[0:D_PART, :]) + _mm(yl_ref[...], wo_ref[D_PART:2 * D_PART, :]))
        r2 = lax.rsqrt(jnp.mean(o * o, axis=-1, keepdims=True) + RMS_EPS)
        ohat = o * r2
        fg = fg_ref[...]
        diff = ohat * fg - t_ref[...]
        dout = diff * (1.0 / D_MODEL)
        gp = dout * fg
        do = r2 * (gp - ohat * jnp.mean(gp * ohat, axis=-1, keepdims=True))
        do_ref[...] = do
        dob = do.astype(MXU_DTYPE)
        dob_ref[...] = dob
        dy_ref[...] = _mm_nt(dob, wo_ref[...])
        st_ref[0:1, :] += jnp.sum(dout * ohat, axis=0, keepdims=True)
        loss = 0.5 * jnp.sum(jnp.sum(diff * diff, axis=-1, keepdims=True) * (1.0 / D_MODEL), axis=0, keepdims=True)
        st_ref[1:2, :] += jnp.broadcast_to(loss, (1, D_MODEL))

    row = lambda i: (i, 0)
    fix = lambda i: (0, 0)
    return pl.pallas_call(
        body, grid=(t // tm,),
        in_specs=[pl.BlockSpec((tm, D_PART), row), pl.BlockSpec((tm, D_PART), row),
                  pl.BlockSpec((tm, D_MODEL), row), pl.BlockSpec((tm, D_MODEL), row),
                  pl.BlockSpec((2 * D_PART, D_MODEL), fix), pl.BlockSpec((1, D_MODEL), fix)],
        out_specs=(pl.BlockSpec((tm, D_MODEL), row), pl.BlockSpec((tm, D_MODEL), row),
                   pl.BlockSpec((tm, 2 * D_PART), row), pl.BlockSpec((SUBLANES, D_MODEL), fix)),
        out_shape=(jax.ShapeDtypeStruct((t, D_MODEL), F32), jax.ShapeDtypeStruct((t, D_MODEL), MXU_DTYPE),
                   jax.ShapeDtypeStruct((t, 2 * D_PART), F32), jax.ShapeDtypeStruct((SUBLANES, D_MODEL), F32)),
        compiler_params=_cp(ARB), name="out_projection_loss",
    )(yc, yl, x, target, wo, final_g)


def _input_grad(dproj, w12, x, do, ln_g, sb_in):
    t = x.shape[0]
    tm = 1024

    def body(dp_ref, w_ref, x_ref, do_ref, g_ref, s_ref, gx_ref, st_ref, r_ref, acc, send_sems, recv_sems):
        i, p = pl.program_id(0), pl.program_id(1)

        @pl.when((i == 0) & (p == 0))
        def _():
            st_ref[...] = jnp.zeros_like(st_ref)
            for cp in _chip_block_copies(s_ref, r_ref, CHUNKS_PER_BLOCK, send_sems, recv_sems):
                cp.start()

        @pl.when((i == t // tm - 1) & (p == N_PARTS - 1))
        def _():
            for cp in _chip_block_copies(s_ref, r_ref, CHUNKS_PER_BLOCK, send_sems, recv_sems):
                cp.wait()

        part = _mm_nt(dp_ref[0, :, 0:CHUNK], w_ref[0]) + _mm_nt(dp_ref[0, :, CHUNK:2 * CHUNK], w_ref[1])

        @pl.when(p == 0)
        def _():
            acc[...] = part

        @pl.when(p > 0)
        def _():
            acc[...] += part

        @pl.when(p == N_PARTS - 1)
        def _():
            def norm_bwd_slab(s, g_sum):
                rows = pl.ds(pl.multiple_of(s * SLAB, SLAB), SLAB)
                xf = x_ref[rows, :]
                r = lax.rsqrt(jnp.mean(xf * xf, axis=-1, keepdims=True) + RMS_EPS)
                xhat = xf * r
                dxn = acc[rows, :]
                dxh = dxn * g_ref[...]
                gx_ref[rows, :] = do_ref[rows, :] + r * (dxh - xhat * jnp.mean(dxh * xhat, axis=-1, keepdims=True))
                return g_sum + jnp.sum(dxn * xhat, axis=0, keepdims=True)

            st_ref[0:1, :] += lax.fori_loop(0, tm // SLAB, norm_bwd_slab, jnp.zeros((1, D_MODEL), F32))

    row = lambda i, p: (i, 0)
    fix = lambda i, p: (0, 0)
    return pl.pallas_call(
        body, grid=(t // tm, N_PARTS),
        in_specs=[
            pl.BlockSpec((1, tm, D_PART), lambda i, p: (p, i, 0)),
            pl.BlockSpec((2, D_MODEL, CHUNK), lambda i, p: (p, 0, 0)),
            pl.BlockSpec((tm, D_MODEL), row), pl.BlockSpec((tm, D_MODEL), row), pl.BlockSpec((1, D_MODEL), fix),
            pl.BlockSpec(memory_space=pl.ANY)],
        out_specs=(pl.BlockSpec((tm, D_MODEL), row), pl.BlockSpec((SUBLANES, D_MODEL), fix),
                   pl.BlockSpec(memory_space=pl.ANY)),
        out_shape=(jax.ShapeDtypeStruct((t, D_MODEL), F32), jax.ShapeDtypeStruct((SUBLANES, D_MODEL), F32),
                   _chip_blocks_shape(sb_in, CHUNKS_PER_BLOCK)),
        scratch_shapes=[pltpu.VMEM((tm, D_MODEL), F32), pltpu.SemaphoreType.DMA((3,)), pltpu.SemaphoreType.DMA((3,))],
        compiler_params=_cp(ARB, ARB), name="input_grad",
    )(dproj, w12, x, do, ln_g, sb_in)


def _w_in_grad(xn, dproj):
    t = xn.shape[0]
    tk = 1024

    def body(xn_ref, dp_ref, o_ref):
        kk = pl.program_id(1)
        xnv = xn_ref[...]
        parts = [_mm_tn(xnv, dp_ref[0, :, CHUNK * s:CHUNK * (s + 1)]) for s in range(2)]

        @pl.when(kk == 0)
        def _():
            for s in range(2):
                o_ref[s] = parts[s]

        @pl.when(kk > 0)
        def _():
            for s in range(2):
                o_ref[s] += parts[s]

    return pl.pallas_call(
        body, grid=(N_PARTS, t // tk),
        in_specs=[pl.BlockSpec((tk, D_MODEL), lambda p, kk: (kk, 0)),
                  pl.BlockSpec((1, tk, D_PART), lambda p, kk: (p, kk, 0))],
        out_specs=pl.BlockSpec((2, D_MODEL, CHUNK), lambda p, kk: (p, 0, 0)),
        out_shape=jax.ShapeDtypeStruct((N_CHUNKS, D_MODEL, CHUNK), F32),
        compiler_params=_cp(ARB, ARB), name="w_in_grad",
    )(xn, dproj)


def _w_out_grad(yc, yl, dob):
    t = yc.shape[0]
    tk = 512

    def body(yc_ref, yl_ref, do_ref, o_ref):
        @pl.when(pl.program_id(0) == 0)
        def _():
            o_ref[...] = jnp.zeros_like(o_ref)

        dov = do_ref[...]
        o_ref[0:D_PART, :] += _mm_tn(yc_ref[...], dov)
        o_ref[D_PART:2 * D_PART, :] += _mm_tn(yl_ref[...], dov)

    row = lambda kk: (kk, 0)
    out = pl.pallas_call(
        body, grid=(t // tk,),
        in_specs=[pl.BlockSpec((tk, D_PART), row), pl.BlockSpec((tk, D_PART), row), pl.BlockSpec((tk, D_MODEL), row)],
        out_specs=pl.BlockSpec((2 * D_PART, D_MODEL), lambda kk: (0, 0)),
        out_shape=jax.ShapeDtypeStruct((2 * D_PART, D_MODEL), F32),
        compiler_params=_cp(ARB), name="w_out_grad",
    )(yc, yl, dob)
    return out.reshape(N_CHIPS, 2 * D_PART // N_CHIPS, D_MODEL)


def _for_groups(n, fn, init):
    def trip(j, carry):
        for uu in range(UNROLL):
            carry = fn(j * UNROLL + uu, carry)
        return carry

    return lax.fori_loop(0, n // UNROLL, trip, init)


def _pvb(pv_ref, r):
    return jnp.broadcast_to(pv_ref[r:r + 1, :], (SUBLANES, pv_ref.shape[1]))


def _conv3(pv_ref, u, u1, u2):
    return (_pvb(pv_ref, PV_CONV_W) * u2 + _pvb(pv_ref, PV_CONV_W + 1) * u1) + _pvb(pv_ref, PV_CONV_W + 2) * u


def _conv4(pv_ref, v, v1, v2, v3):
    return ((((_pvb(pv_ref, PV_LRU_W) * v3 + _pvb(pv_ref, PV_LRU_W + 1) * v2) + _pvb(pv_ref, PV_LRU_W + 2) * v1)
             + _pvb(pv_ref, PV_LRU_W + 3) * v) + _pvb(pv_ref, PV_LRU_B))


def _mixer_forward(proj, pvec, wai, w_out):
    t = proj.shape[1]
    tb = 512
    ng = tb // SUBLANES
    nt = t // tb

    def body(bg_ref, cg_ref, xc_ref, gc_ref, xl_ref, gl_ref, pv_ref, wai_ref, wo_ref,
             yc_ref, yl_ref, h_ref, wo4_ref,
             ucp_s, xlp_s, ls_s, hbuf_s, u_s, gate_s, zc_s, zl_s, wob_s, local_sem, send_sems, recv_sems):
        _gather_w_out(pl.program_id(0) * nt + pl.program_id(1), NS * nt, wo_ref, wob_s, wo4_ref, local_sem, send_sems, recv_sems)

        @pl.when(pl.program_id(1) == 0)
        def _():
            ucp_s[...] = jnp.zeros_like(ucp_s)
            xlp_s[...] = jnp.zeros_like(xlp_s)
            hbuf_s[...] = jnp.zeros_like(hbuf_s)

        row = lax.broadcasted_iota(jnp.int32, (SUBLANES, LW), 0)
        ls_s[...] = _log_sigmoid(_pvb(pv_ref, PV_LAM))

        def conv_group(g, carry):
            ucp, xlp = carry
            sl = pl.ds(pl.multiple_of(g * SUBLANES, SUBLANES), SUBLANES)
            uc = cg_ref[sl, :] * xc_ref[sl, :]
            v = _conv3(pv_ref, uc, _shift_down(uc, ucp, 1, row), _shift_down(uc, ucp, 2, row))
            yc = bg_ref[sl, :] * v
            rr = lax.rsqrt(_head_mean(yc * yc, CONV_HEAD) + RMS_EPS)
            gc = gc_ref[sl, :]
            zc_s[sl, :] = ((yc * rr) * _pvb(pv_ref, PV_CG)) * (gc * _sigmoid(gc))
            xl = xl_ref[sl, :]
            u_s[sl, :] = _conv4(pv_ref, xl, _shift_down(xl, xlp, 1, row), _shift_down(xl, xlp, 2, row),
                                _shift_down(xl, xlp, 3, row))
            return uc, xl

        ucp, xlp = _for_groups(ng, conv_group, (ucp_s[...], xlp_s[...]))
        ucp_s[...] = ucp
        xlp_s[...] = xlp

        gate_s[...] = _mm(u_s[...].astype(MXU_DTYPE), wai_ref[0])

        def lru_group(g, h_before):
            sl = pl.ds(pl.multiple_of(g * SUBLANES, SUBLANES), SUBLANES)
            u = u_s[sl, :]
            r, ig, a, e2, mult, _ = _gates(gate_s[sl, 0:LW] + _pvb(pv_ref, PV_BA),
                                           gate_s[sl, LW:2 * LW] + _pvb(pv_ref, PV_BI), u, ls_s[...])
            A, B = _scan8_fwd(a, mult * (ig * u), row)
            h = B + A * jnp.broadcast_to(h_before[SUBLANES - 1:SUBLANES, :], (SUBLANES, LW))
            h_ref[sl, :] = h
            rr = lax.rsqrt(_head_mean(h * h, LRU_HEAD) + RMS_EPS)
            gl = gl_ref[sl, :]
            zl_s[sl, :] = ((h * rr) * _pvb(pv_ref, PV_LG)) * (gl * _sigmoid(gl))
            return h

        hbuf_s[...] = _for_groups(ng, lru_group, hbuf_s[...])
        yc_ref[...] = zc_s[...].astype(MXU_DTYPE)
        yl_ref[...] = zl_s[...].astype(MXU_DTYPE)

    def part(p):
        return pl.BlockSpec((None, tb, LW), lambda c, i: (2 * p + c // STRIPS_PER_CHUNK, i, c % STRIPS_PER_CHUNK))

    strip = pl.BlockSpec((tb, LW), lambda c, i: (i, c))
    return pl.pallas_call(
        body, grid=(NS, nt),
        in_specs=[part(p) for p in range(N_PARTS)] + [
            pl.BlockSpec((PV_ROWS, LW), lambda c, i: (0, c)),
            pl.BlockSpec((1, LW, 2 * LW), lambda c, i: (c, 0, 0)),
            pl.BlockSpec(w_out.shape, lambda c, i: (0, 0))],
        out_specs=(strip, strip, strip, pl.BlockSpec(memory_space=pl.ANY)),
        out_shape=(jax.ShapeDtypeStruct((t, D_PART), MXU_DTYPE), jax.ShapeDtypeStruct((t, D_PART), MXU_DTYPE),
                   jax.ShapeDtypeStruct((t, D_PART), F32), jax.ShapeDtypeStruct((N_CHIPS,) + w_out.shape, MXU_DTYPE)),
        scratch_shapes=[pltpu.VMEM((SUBLANES, LW), F32), pltpu.VMEM((SUBLANES, LW), F32), pltpu.VMEM((SUBLANES, LW), F32),
                        pltpu.VMEM((SUBLANES, LW), F32), pltpu.VMEM((tb, LW), F32), pltpu.VMEM((tb, 2 * LW), F32),
                        pltpu.VMEM((tb, LW), F32), pltpu.VMEM((tb, LW), F32), pltpu.VMEM(w_out.shape, MXU_DTYPE),
                        pltpu.SemaphoreType.DMA, pltpu.SemaphoreType.DMA((6,)), pltpu.SemaphoreType.DMA((6,))],
        compiler_params=_cp(ARB, ARB), name="mixer_forward",
    )(proj, proj, proj, proj, proj, proj, pvec, wai, w_out)


def _mixer_backward(proj, h, dy, pvec, wai, sb_out):
    t = proj.shape[1]
    tb = 512
    ng = tb // SUBLANES
    nt = t // tb
    gpb = tb // SUBLANES

    def body(bg_ref, cg_ref, xc_ref, gc_ref, xl_ref, gl_ref, h_ref, dyc_ref, dyl_ref,
             cgh_ref, xch_ref, xlh_ref, hh_ref, pv_ref, wai_ref, so_ref,
             dp_ref, gw_ref, sv_ref, ro_ref,
             ls_s, u_s, uce_s, xle_s, he_s, gate_s, dgate_s, du_s, gbuf_s,
             p0_s, p1_s, p2_s, p3_s, p4_s, p5_s, acc_s, an_s, dvn_s, dun_s, send_sems, recv_sems):
        i = pl.program_id(1)
        first_block = i == nt - 1

        @pl.when((pl.program_id(0) == 0) & (i == 0))
        def _():
            for cp in _chip_block_copies(so_ref, ro_ref, 1, send_sems, recv_sems):
                cp.start()

        @pl.when((pl.program_id(0) == NS - 1) & (i == nt - 1))
        def _():
            for cp in _chip_block_copies(so_ref, ro_ref, 1, send_sems, recv_sems):
                cp.wait()

        @pl.when(i == 0)
        def _():
            acc_s[...] = jnp.zeros_like(acc_s)
            gw_ref[...] = jnp.zeros_like(gw_ref)
            an_s[...] = jnp.zeros_like(an_s)
            dvn_s[...] = jnp.zeros_like(dvn_s)
            dun_s[...] = jnp.zeros_like(dun_s)
            gbuf_s[...] = jnp.zeros_like(gbuf_s)

        row = lax.broadcasted_iota(jnp.int32, (SUBLANES, LW), 0)
        ls_s[...] = _log_sigmoid(_pvb(pv_ref, PV_LAM))
        keep = jnp.where(first_block, 0.0, 1.0)
        uce_s[0:SUBLANES, :] = (cgh_ref[...] * xch_ref[...]) * keep
        xle_s[0:SUBLANES, :] = xlh_ref[...] * keep
        he_s[0:SUBLANES, :] = hh_ref[...] * keep
        xle_s[SUBLANES:SUBLANES + tb, :] = xl_ref[...]
        he_s[SUBLANES:SUBLANES + tb, :] = h_ref[...]

        def recompute_group(g, carry):
            r0 = pl.multiple_of(g * SUBLANES, SUBLANES)
            sl = pl.ds(r0, SUBLANES)
            uce_s[pl.ds(r0 + SUBLANES, SUBLANES), :] = cg_ref[sl, :] * xc_ref[sl, :]
            xl = xle_s[pl.ds(r0 + SUBLANES, SUBLANES), :]
            xlp = xle_s[sl, :]
            u_s[sl, :] = _conv4(pv_ref, xl, _shift_down(xl, xlp, 1, row), _shift_down(xl, xlp, 2, row),
                                _shift_down(xl, xlp, 3, row))
            return carry

        _for_groups(ng, recompute_group, 0)
        gate_s[...] = _mm(u_s[...].astype(MXU_DTYPE), wai_ref[0])

        def acc_add(k, v):
            acc_s[k] += v

        def main_group(gi, carry):
            a_next, dv_next, g_next = carry
            g = ng - 1 - gi
            r0 = pl.multiple_of(g * SUBLANES, SUBLANES)
            sl = pl.ds(r0, SUBLANES)
            sl_e = pl.ds(r0 + SUBLANES, SUBLANES)
            lsb = ls_s[...]
            u = u_s[sl, :]
            r, ig, a, e2, mult, inv_mult = _gates(gate_s[sl, 0:LW] + _pvb(pv_ref, PV_BA),
                                                  gate_s[sl, LW:2 * LW] + _pvb(pv_ref, PV_BI), u, lsb)
            gl = gl_ref[sl, :]
            sg = _sigmoid(gl)
            s_l = gl * sg
            h8 = he_s[sl_e, :]
            hprev = _shift_down(h8, he_s[sl, :], 1, row)
            rr = lax.rsqrt(_head_mean(h8 * h8, LRU_HEAD) + RMS_EPS)
            n = h8 * rr
            dz = dyl_ref[sl, :]
            lg = _pvb(pv_ref, PV_LG)
            acc_add(PV_LG, (dz * n) * s_l)
            p5_s[sl, :] = ((dz * n) * lg) * (sg * (1.0 + gl * (1.0 - sg)))
            dn = (dz * lg) * s_l
            dh = rr * (dn - n * _head_mean(dn * n, LRU_HEAD))
            A, B = _scan8_rev(_shift_up(a, a_next, 1, row), dh, row)
            gg = B + A * jnp.broadcast_to(g_next[0:1, :], (SUBLANES, LW))
            da = gg * hprev
            iu = ig * u
            diu = gg * mult
            dla = da * a - (gg * iu) * (e2 * inv_mult)
            acc_add(PV_LAM, dla * (RG_LRU_C * r))
            dra = (dla * (RG_LRU_C * lsb)) * (r * (1.0 - r))
            dia = (diu * u) * (ig * (1.0 - ig))
            dgate_s[sl, 0:LW] = dra
            dgate_s[sl, LW:2 * LW] = dia
            acc_add(PV_BA, dra)
            acc_add(PV_BI, dia)
            du_s[sl, :] = diu * ig
            bg = bg_ref[sl, :]
            gc = gc_ref[sl, :]
            uc = uce_s[sl_e, :]
            ucp = uce_s[sl, :]
            uc1 = _shift_down(uc, ucp, 1, row)
            uc2 = _shift_down(uc, ucp, 2, row)
            v = _conv3(pv_ref, uc, uc1, uc2)
            yc = bg * v
            rrc = lax.rsqrt(_head_mean(yc * yc, CONV_HEAD) + RMS_EPS)
            nc = yc * rrc
            sgc = _sigmoid(gc)
            s_c = gc * sgc
            dzc = dyc_ref[sl, :]
            cgain = _pvb(pv_ref, PV_CG)
            acc_add(PV_CG, (dzc * nc) * s_c)
            p3_s[sl, :] = ((dzc * nc) * cgain) * (sgc * (1.0 + gc * (1.0 - sgc)))
            dnc = (dzc * cgain) * s_c
            dyc = rrc * (dnc - nc * _head_mean(dnc * nc, CONV_HEAD))
            p0_s[sl, :] = dyc * v
            dv = dyc * bg
            duc = (_pvb(pv_ref, PV_CONV_W + 2) * dv + _pvb(pv_ref, PV_CONV_W + 1) * _shift_up(dv, dv_next, 1, row)
                   + _pvb(pv_ref, PV_CONV_W) * _shift_up(dv, dv_next, 2, row))
            acc_add(PV_CONV_W + 2, dv * uc)
            acc_add(PV_CONV_W + 1, dv * uc1)
            acc_add(PV_CONV_W, dv * uc2)
            p1_s[sl, :] = duc * xc_ref[sl, :]
            p2_s[sl, :] = duc * cg_ref[sl, :]
            return a, dv, gg

        a_next, dv_next, g_next = _for_groups(ng, main_group, (an_s[...], dvn_s[...], gbuf_s[...]))
        an_s[...] = a_next
        dvn_s[...] = dv_next
        gbuf_s[...] = g_next

        dgb = dgate_s[...].astype(MXU_DTYPE)
        du_s[...] += _mm_nt(dgb, wai_ref[0])
        gw_ref[0] += _mm_tn(u_s[...].astype(MXU_DTYPE), dgb)

        def lru_conv_group(gi, du_next):
            g = ng - 1 - gi
            r0 = pl.multiple_of(g * SUBLANES, SUBLANES)
            sl = pl.ds(r0, SUBLANES)
            du = du_s[sl, :]
            xl = xle_s[pl.ds(r0 + SUBLANES, SUBLANES), :]
            xlp = xle_s[sl, :]
            acc_add(PV_LRU_B, du)
            acc_add(PV_LRU_W + 3, du * xl)
            acc_add(PV_LRU_W + 2, du * _shift_down(xl, xlp, 1, row))
            acc_add(PV_LRU_W + 1, du * _shift_down(xl, xlp, 2, row))
            acc_add(PV_LRU_W, du * _shift_down(xl, xlp, 3, row))
            p4_s[sl, :] = (((_pvb(pv_ref, PV_LRU_W + 3) * du + _pvb(pv_ref, PV_LRU_W + 2) * _shift_up(du, du_next, 1, row))
                            + _pvb(pv_ref, PV_LRU_W + 1) * _shift_up(du, du_next, 2, row))
                           + _pvb(pv_ref, PV_LRU_W) * _shift_up(du, du_next, 3, row))
            return du

        dun_s[...] = _for_groups(ng, lru_conv_group, dun_s[...])

        for p, p_s in enumerate((p0_s, p1_s, p2_s, p3_s, p4_s, p5_s)):
            dp_ref[p] = p_s[...].astype(MXU_DTYPE)

        @pl.when(first_block)
        def _():
            sv_ref[...] = jnp.zeros_like(sv_ref)
            for k in range(N_ACC):
                tot = jnp.sum(acc_s[k], axis=0, keepdims=True)
                if k == PV_LAM:
                    tot = tot / (1.0 + jnp.exp(pv_ref[PV_LAM:PV_LAM + 1, :]))
                sv_ref[k:k + 1, :] = tot

    def part(p):
        return pl.BlockSpec((None, tb, LW), lambda c, i: (2 * p + c // STRIPS_PER_CHUNK, nt - 1 - i, c % STRIPS_PER_CHUNK))

    def halo(p):
        return pl.BlockSpec((None, SUBLANES, LW), lambda c, i: (2 * p + c // STRIPS_PER_CHUNK,
                                                                jnp.maximum((nt - 1 - i) * gpb - 1, 0), c % STRIPS_PER_CHUNK))

    strip = pl.BlockSpec((tb, LW), lambda c, i: (nt - 1 - i, c))
    big = pltpu.VMEM((tb, LW), F32)
    big_e = pltpu.VMEM((tb + SUBLANES, LW), F32)
    wide = pltpu.VMEM((tb, 2 * LW), F32)
    small = pltpu.VMEM((SUBLANES, LW), F32)
    outs = pl.pallas_call(
        body, grid=(NS, nt),
        in_specs=[part(p) for p in range(N_PARTS)] + [
            strip, strip, pl.BlockSpec((tb, LW), lambda c, i: (nt - 1 - i, NS + c)),
            halo(1), halo(2), halo(4),
            pl.BlockSpec((SUBLANES, LW), lambda c, i: (jnp.maximum((nt - 1 - i) * gpb - 1, 0), c)),
            pl.BlockSpec((PV_ROWS, LW), lambda c, i: (0, c)),
            pl.BlockSpec((1, LW, 2 * LW), lambda c, i: (c, 0, 0)),
            pl.BlockSpec(memory_space=pl.ANY)],
        out_specs=(pl.BlockSpec((N_PARTS, tb, LW), lambda c, i: (0, nt - 1 - i, c)),
                   pl.BlockSpec((1, LW, 2 * LW), lambda c, i: (c, 0, 0)),
                   pl.BlockSpec((PV_ROWS, LW), lambda c, i: (0, c)),
                   pl.BlockSpec(memory_space=pl.ANY)),
        out_shape=(jax.ShapeDtypeStruct((N_PARTS, t, D_PART), MXU_DTYPE),
                   jax.ShapeDtypeStruct((NS, LW, 2 * LW), F32), jax.ShapeDtypeStruct((PV_ROWS, D_PART), F32),
                   _chip_blocks_shape(sb_out, 1)),
        scratch_shapes=[small, big, big_e, big_e, big_e, wide, wide, big, small,
                        big, big, big, big, big, big, pltpu.VMEM((N_ACC, SUBLANES, LW), F32), small, small, small,
                        pltpu.SemaphoreType.DMA((3,)), pltpu.SemaphoreType.DMA((3,))],
        compiler_params=_cp(ARB, ARB), name="mixer_backward",
    )(proj, proj, proj, proj, proj, proj, h, dy, dy, proj, proj, proj, h, pvec, wai, sb_out)
    return outs


def _adamw(w, g, m, v):
    m = ADAM_B1 * m + (1.0 - ADAM_B1) * g
    v = ADAM_B2 * v + (1.0 - ADAM_B2) * (g * g)
    m_hat = m / (1.0 - ADAM_B1 ** ADAM_STEP)
    v_hat = v / (1.0 - ADAM_B2 ** ADAM_STEP)
    delta = -ADAM_LR * (m_hat / (jnp.sqrt(v_hat) + ADAM_EPS) + ADAM_WD * w)
    return delta, m, v


def _adam_w_in(w, m, v, g3):
    rows, cols = w.shape
    tr = 128

    def body(w_ref, m_ref, v_ref, g_ref, go_ref, d_ref, mo_ref, vo_ref):
        for s in range(CHUNKS_PER_BLOCK):
            cs = slice(CHUNK * s, CHUNK * (s + 1))
            g = g_ref[s]
            d, mn, vn = _adamw(w_ref[:, cs], g, m_ref[:, cs], v_ref[:, cs])
            go_ref[:, cs] = g
            d_ref[:, cs] = d
            mo_ref[:, cs] = mn
            vo_ref[:, cs] = vn

    blk = pl.BlockSpec((tr, cols), lambda i: (i, 0))
    return pl.pallas_call(
        body, grid=(rows // tr,),
        in_specs=[blk, blk, blk, pl.BlockSpec((CHUNKS_PER_BLOCK, tr, CHUNK), lambda i: (0, i, 0))],
        out_specs=(blk,) * 4, out_shape=(jax.ShapeDtypeStruct(w.shape, F32),) * 4,
        compiler_params=_cp(ARB), name="adam_w_in",
    )(w, m, v, g3)


def _adam_w_out(w, m, v, g):
    rows, cols = w.shape
    tr = 128

    def body(w_ref, m_ref, v_ref, g_ref, d_ref, mo_ref, vo_ref):
        d_ref[...], mo_ref[...], vo_ref[...] = _adamw(w_ref[...], g_ref[...], m_ref[...], v_ref[...])

    blk = pl.BlockSpec((tr, cols), lambda i: (i, 0))
    return pl.pallas_call(
        body, grid=(rows // tr,), in_specs=[blk] * 4, out_specs=(blk,) * 3,
        out_shape=(jax.ShapeDtypeStruct(w.shape, F32),) * 3,
        compiler_params=_cp(ARB), name="adam_w_out",
    )(w, m, v, g)


def _adam_small(ws, ms, vs, gs):
    n = len(ws)

    def body(*refs):
        w_r, m_r, v_r, g_r = refs[0:n], refs[n:2 * n], refs[2 * n:3 * n], refs[3 * n:4 * n]
        d_o, m_o, v_o = refs[4 * n:5 * n], refs[5 * n:6 * n], refs[6 * n:7 * n]
        for j in range(n):
            d_o[j][...], m_o[j][...], v_o[j][...] = _adamw(w_r[j][...], g_r[j][...], m_r[j][...], v_r[j][...])

    vm = pl.BlockSpec(memory_space=pltpu.VMEM)
    shapes = tuple(jax.ShapeDtypeStruct(w.shape, F32) for w in ws)
    outs = pl.pallas_call(
        body, in_specs=[vm] * (4 * n), out_specs=(vm,) * (3 * n), out_shape=shapes * 3,
        compiler_params=_cp(), name="adam_small",
    )(*ws, *ms, *vs, *gs)
    return outs[0:n], outs[n:2 * n], outs[2 * n:3 * n]


def _block_diag_strips(w):
    w4 = w.reshape(NS, HEADS_PER_STRIP, LRU_HEAD, LRU_HEAD)
    bd = jnp.zeros((NS, HEADS_PER_STRIP, LRU_HEAD, HEADS_PER_STRIP, LRU_HEAD), w.dtype)
    for hh in range(HEADS_PER_STRIP):
        bd = bd.at[:, hh, :, hh, :].set(w4[:, hh])
    return bd.reshape(NS, LW, LW)


def _strip_diag_blocks(g):
    g5 = g.reshape(NS, HEADS_PER_STRIP, LRU_HEAD, HEADS_PER_STRIP, LRU_HEAD)
    return jnp.stack([g5[:, hh, :, hh, :] for hh in range(HEADS_PER_STRIP)], axis=1).reshape(NS * HEADS_PER_STRIP, LRU_HEAD, LRU_HEAD)


def kernel(x, ln_g, w_in, conv_w, lru_conv_w, lru_conv_b, w_a, b_a, w_i, b_i, lam, conv_out_g, lru_out_g, w_out, final_g, loss_target, m_ln_g, m_w_in, m_conv_w, m_lru_conv_w, m_lru_conv_b, m_w_a, m_b_a, m_w_i, m_b_i, m_lam, m_conv_out_g, m_lru_out_g, m_w_out, m_final_g, v_ln_g, v_w_in, v_conv_w, v_lru_conv_w, v_lru_conv_b, v_w_a, v_b_a, v_w_i, v_b_i, v_lam, v_conv_out_g, v_lru_out_g, v_w_out, v_final_g):
    xi, yi, ci = lax.axis_index("x"), lax.axis_index("y"), lax.axis_index("c")
    k = 2 * xi + yi
    t = x.shape[1]
    x2 = x.reshape(t, D_MODEL)
    tgt2 = loss_target.reshape(t, D_MODEL)
    row = lambda a: a.reshape(1, -1)

    small = jnp.concatenate([conv_w, lru_conv_w, jnp.zeros((1, conv_w.shape[1]), F32)], axis=0)
    proj, xn, w12, sm4 = _gather_in_projection(x2, row(ln_g), w_in, small)
    convs = jnp.transpose(sm4, (1, 0, 2)).reshape(SUBLANES, D_PART)
    pvec = jnp.concatenate(
        [convs[0:7], row(lru_conv_b), row(b_a), row(b_i), row(lam), row(conv_out_g), row(lru_out_g),
         jnp.zeros((PV_ROWS - N_ACC, D_PART), F32)], axis=0)
    wai = jnp.concatenate([_block_diag_strips(w_a), _block_diag_strips(w_i)], axis=2).astype(MXU_DTYPE)

    c_arr = jnp.reshape(ci, (1,)).astype(jnp.int32)
    kc_arr = jnp.stack([k, ci]).astype(jnp.int32)
    yc, yl, h, wo4 = _mixer_forward(proj, pvec, wai, w_out)
    wo = wo4.reshape(2 * D_PART, D_MODEL)
    do, dob, dy, st_out = _out_projection_loss(yc, yl, x2, tgt2, wo, row(final_g))
    go4 = _w_out_grad(yc, yl, dob)
    s_out, sb_out = _add_own_half(go4, _exchange_sibling_halves(go4, "exchange_sibling_halves_out"), c_arr, "add_own_half_out")
    dproj, g_wai, svec, r2o = _mixer_backward(proj, h, dy, pvec, wai, sb_out)
    g12 = _w_in_grad(xn, dproj)
    s_in, sb_in = _add_own_half(g12, _exchange_sibling_halves(g12, "exchange_sibling_halves_in"), c_arr, "add_own_half_in")
    grad_x, st_in, r2i = _input_grad(dproj, w12, x2, do, row(ln_g), sb_in)
    f_in = _sum_chip_blocks(s_in, r2i, kc_arr, CHUNKS_PER_BLOCK, "sum_chip_blocks_in")
    f_out = _sum_chip_blocks(s_out, r2o, kc_arr, 1, "sum_chip_blocks_out")
    f_in, f_out = _swap_sibling_halves(f_in, f_out)

    gwa = _strip_diag_blocks(g_wai[:, :, 0:LW]).reshape(LRU_HEAD, D_PART)
    gwi = _strip_diag_blocks(g_wai[:, :, LW:2 * LW]).reshape(LRU_HEAD, D_PART)
    red = _allreduce_small(jnp.concatenate([svec, st_out, st_in, gwa, gwi], axis=0))
    r_out = PV_ROWS
    r_in = PV_ROWS + SUBLANES
    r_wa = PV_ROWS + 2 * SUBLANES
    r_wi = r_wa + LRU_HEAD
    loss = red[r_out + 1, 0]

    g_w_in, d_w_in, nm_w_in, nv_w_in = _adam_w_in(w_in, m_w_in, v_w_in, f_in)
    g_w_out = f_out[0]
    d_w_out, nm_w_out, nv_w_out = _adam_w_out(w_out, m_w_out, v_w_out, g_w_out)

    ncol = conv_w.shape[1]
    conv_cols = lax.dynamic_slice(red, (0, k * ncol), (SUBLANES, ncol))
    g_small = {
        "ln_g": red[r_in], "conv_w": conv_cols[0:3], "lru_conv_w": conv_cols[3:7], "lru_conv_b": red[PV_LRU_B],
        "w_a": red[r_wa:r_wa + LRU_HEAD].reshape(w_a.shape), "b_a": red[PV_BA],
        "w_i": red[r_wi:r_wi + LRU_HEAD].reshape(w_i.shape), "b_i": red[PV_BI], "lam": red[PV_LAM],
        "conv_out_g": red[PV_CG], "lru_out_g": red[PV_LG], "final_g": red[r_out],
    }
    w_small = {"ln_g": ln_g, "conv_w": conv_w, "lru_conv_w": lru_conv_w, "lru_conv_b": lru_conv_b, "w_a": w_a, "b_a": b_a,
               "w_i": w_i, "b_i": b_i, "lam": lam, "conv_out_g": conv_out_g, "lru_out_g": lru_out_g, "final_g": final_g}
    m_small = {"ln_g": m_ln_g, "conv_w": m_conv_w, "lru_conv_w": m_lru_conv_w, "lru_conv_b": m_lru_conv_b, "w_a": m_w_a,
               "b_a": m_b_a, "w_i": m_w_i, "b_i": m_b_i, "lam": m_lam, "conv_out_g": m_conv_out_g,
               "lru_out_g": m_lru_out_g, "final_g": m_final_g}
    v_small = {"ln_g": v_ln_g, "conv_w": v_conv_w, "lru_conv_w": v_lru_conv_w, "lru_conv_b": v_lru_conv_b, "w_a": v_w_a,
               "b_a": v_b_a, "w_i": v_w_i, "b_i": v_b_i, "lam": v_lam, "conv_out_g": v_conv_out_g,
               "lru_out_g": v_lru_out_g, "final_g": v_final_g}
    names = list(w_small)
    as2d = lambda a: a.reshape(1, -1) if a.ndim == 1 else a
    d_s, m_s, v_s = _adam_small([as2d(w_small[n]) for n in names], [as2d(m_small[n]) for n in names],
                                [as2d(v_small[n]) for n in names], [as2d(g_small[n]) for n in names])
    back = lambda n, a: a.reshape(w_small[n].shape)
    grads = {n: g_small[n] for n in names}
    deltas = {n: back(n, a) for n, a in zip(names, d_s)}
    new_m = {n: back(n, a) for n, a in zip(names, m_s)}
    new_v = {n: back(n, a) for n, a in zip(names, v_s)}
    grads["w_in"], deltas["w_in"], new_m["w_in"], new_v["w_in"] = g_w_in, d_w_in, nm_w_in, nv_w_in
    grads["w_out"], deltas["w_out"], new_m["w_out"], new_v["w_out"] = g_w_out, d_w_out, nm_w_out, nv_w_out

    order = ["ln_g", "w_in", "conv_w", "lru_conv_w", "lru_conv_b", "w_a", "b_a", "w_i", "b_i", "lam", "conv_out_g",
             "lru_out_g", "w_out", "final_g"]
    return (loss, grad_x.reshape(x.shape), *[grads[n] for n in order], *[deltas[n] for n in order],
            *[new_m[n] for n in order], *[new_v[n] for n in order])
```

```python
import functools

import jax
import jax.numpy as jnp
from jax import lax
from jax.experimental import pallas as pl
from jax.experimental.pallas import tpu as pltpu

F32 = jnp.float32
MXU_DTYPE = jnp.bfloat16

D_MODEL = 1024
D_PART = 1024
N_PARTS = 6
CHUNK = 512
CHUNKS_PER_BLOCK = 3
N_CHUNKS = 12
N_CHIPS = 4
SUBLANES = 8
LANES = 128
LW = 256
UNROLL = 8
NS = D_PART // LW
STRIPS_PER_CHUNK = CHUNK // LW
CONV_HEAD = 128
LRU_HEAD = 64
HEADS_PER_STRIP = LW // LRU_HEAD
RMS_EPS = 1e-6
RG_LRU_C = 8.0
ADAM_LR = 0.001
ADAM_B1 = 0.9
ADAM_B2 = 0.999
ADAM_EPS = 1e-08
ADAM_WD = 0.01
ADAM_STEP = 10

PV_CONV_W = 0
PV_LRU_W = 3
PV_LRU_B = 7
PV_BA = 8
PV_BI = 9
PV_LAM = 10
PV_CG = 11
PV_LG = 12
PV_ROWS = 16
N_ACC = 13

SLAB = 128
MESH = pl.DeviceIdType.MESH
VMEM_LIMIT = 56 * 1024 * 1024
ARB = "arbitrary"


def _cp(*sem, **kw):
    return pltpu.CompilerParams(dimension_semantics=sem or None, vmem_limit_bytes=VMEM_LIMIT, **kw)


def _mm(a, b):
    return jnp.dot(a, b, preferred_element_type=F32)


def _mm_nt(a, b):
    return lax.dot_general(a, b, (((1,), (1,)), ((), ())), preferred_element_type=F32)


def _mm_tn(a, b):
    return lax.dot_general(a, b, (((0,), (0,)), ((), ())), preferred_element_type=F32)


def _sigmoid(x):
    return 0.5 * jnp.tanh(0.5 * x) + 0.5


def _log_sigmoid(x):
    z = jnp.exp(-jnp.abs(x))
    u = 1.0 + z
    log1p = jnp.where(u == 1.0, z, jnp.log(u) * z / (u - 1.0))
    return jnp.minimum(x, 0.0) - log1p


def _head_mean(z, head):
    out = []
    for k in range(z.shape[1] // LANES):
        zk = z[:, LANES * k:LANES * (k + 1)]
        if head == LANES:
            m = jnp.sum(zk, axis=-1, keepdims=True) * (1.0 / head)
            out.append(jnp.broadcast_to(m, zk.shape))
        else:
            lo = lax.broadcasted_iota(jnp.int32, zk.shape, 1) < head
            s_lo = jnp.sum(jnp.where(lo, zk, 0.0), axis=-1, keepdims=True)
            s_hi = jnp.sum(jnp.where(lo, 0.0, zk), axis=-1, keepdims=True)
            out.append(jnp.where(lo, s_lo, s_hi) * (1.0 / head))
    return jnp.concatenate(out, axis=1)


def _shift_down(cur, prev, d, row):
    return pltpu.roll(jnp.where(row < SUBLANES - d, cur, prev), d, 0)


def _shift_up(cur, nxt, d, row):
    return pltpu.roll(jnp.where(row >= d, cur, nxt), SUBLANES - d, 0)


def _scan8_fwd(a, b, row):
    A, B = a, b
    for d in (1, 2, 4):
        m = row >= d
        a_s = jnp.where(m, pltpu.roll(A, d, 0), 1.0)
        b_s = jnp.where(m, pltpu.roll(B, d, 0), 0.0)
        B = A * b_s + B
        A = A * a_s
    return A, B


def _scan8_rev(a, b, row):
    A, B = a, b
    for d in (1, 2, 4):
        m = row < SUBLANES - d
        a_s = jnp.where(m, pltpu.roll(A, SUBLANES - d, 0), 1.0)
        b_s = jnp.where(m, pltpu.roll(B, SUBLANES - d, 0), 0.0)
        B = A * b_s + B
        A = A * a_s
    return A, B


def _gates(ra, ia, u, lsb):
    r = _sigmoid(ra)
    ig = _sigmoid(ia)
    la = (RG_LRU_C * r) * lsb
    a = jnp.exp(la)
    e2 = a * a
    em = -jnp.tanh(la) * (1.0 + e2)
    inv_mult = lax.rsqrt(em)
    return r, ig, a, e2, em * inv_mult, inv_mult


def _mesh_pos():
    x, y, c = lax.axis_index("x"), lax.axis_index("y"), lax.axis_index("c")
    chips = [(1 - x, y), (x, 1 - y), (1 - x, 1 - y)]
    return x, y, c, chips


def _gather_in_projection(x, ln_g, w_in, small):
    t = x.shape[0]
    rb_x = 512
    rb_mm = 1024
    n_mm = t // rb_mm
    half = w_in.shape[0] // 2
    n_ici = 3 * CHUNKS_PER_BLOCK

    def body(x_hbm, g_ref, wi_ref, sm_ref, proj_hbm, xn_ref, w12_ref, sm4_ref,
             xbuf, obuf, x_sems, o_sems, send_sems, recv_sems):
        x_, y_, c, chips = _mesh_pos()
        k = 2 * x_ + y_
        sib = (x_, y_, 1 - c)
        my_rows = pl.ds(pl.multiple_of(half * c, half), half)
        sib_rows = pl.ds(pl.multiple_of(half * (1 - c), half), half)

        for s in range(CHUNKS_PER_BLOCK):
            w12_ref[CHUNKS_PER_BLOCK * k + s] = wi_ref[:, CHUNK * s:CHUNK * (s + 1)].astype(MXU_DTYPE)
        sm4_ref[k] = sm_ref[...]

        def remote(ref, sem, to):
            return pltpu.make_async_remote_copy(src_ref=ref, dst_ref=ref, send_sem=send_sems.at[sem],
                                                recv_sem=recv_sems.at[sem], device_id=to, device_id_type=MESH)

        def chunk_of(chip, s):
            return CHUNKS_PER_BLOCK * (2 * chip[0] + chip[1]) + s

        ici = lambda m, s: 3 * s + m
        fwd = lambda m, s: n_ici + 3 * s + m
        sml = lambda m: 2 * n_ici + m

        sends = []
        for s in range(CHUNKS_PER_BLOCK):
            for m, chip in enumerate(chips):
                sends.append(remote(w12_ref.at[chunk_of((x_, y_), s), my_rows, :], ici(m, s), (*chip, c)))
        for m, chip in enumerate(chips):
            sends.append(remote(sm4_ref.at[k], sml(m), (*chip, c)))
        for cp in sends:
            cp.start()

        def x_copy(rb, slot):
            return pltpu.make_async_copy(x_hbm.at[pl.ds(rb * rb_x, rb_x), :], xbuf.at[slot], x_sems.at[slot])

        x_copy(0, 0).start()
        for rb in range(t // rb_x):
            slot = rb % 2
            x_copy(rb, slot).wait()
            if rb + 1 < t // rb_x:
                x_copy(rb + 1, 1 - slot).start()

            def norm_slab(sl, carry, rb=rb, slot=slot):
                xf = xbuf[slot, pl.ds(pl.multiple_of(sl * SLAB, SLAB), SLAB), :]
                r = lax.rsqrt(jnp.mean(xf * xf, axis=-1, keepdims=True) + RMS_EPS)
                xn_ref[pl.ds(pl.multiple_of(rb * rb_x + sl * SLAB, SLAB), SLAB), :] = ((xf * r) * g_ref[...]).astype(MXU_DTYPE)
                return carry

            lax.fori_loop(0, rb_x // SLAB, norm_slab, 0)

        def out_copy(q, i, slot):
            return pltpu.make_async_copy(obuf.at[slot], proj_hbm.at[q, pl.ds(pl.multiple_of(i * rb_mm, rb_mm), rb_mm), :],
                                         o_sems.at[slot])

        def project(q, very_first):
            def row_block(i, carry):
                slot = i % 2

                def wait_buffer():
                    out_copy(q, i, slot).wait()

                if very_first:
                    pl.when(i >= 2)(wait_buffer)
                else:
                    wait_buffer()
                obuf[slot] = _mm(xn_ref[pl.ds(pl.multiple_of(i * rb_mm, rb_mm), rb_mm), :], w12_ref[q])
                out_copy(q, i, slot).start()
                return carry

            lax.fori_loop(0, n_mm, row_block, 0)

        for s in range(CHUNKS_PER_BLOCK):
            project(chunk_of((x_, y_), s), very_first=(s == 0))

        order = [(m, s) for s in range(CHUNKS_PER_BLOCK) for m in range(3)]
        forwards = []
        for j, (m, s) in enumerate(order):
            q = chunk_of(chips[m], s)
            remote(w12_ref.at[q, my_rows, :], ici(m, s), sib).wait_recv()
            f = remote(w12_ref.at[q, my_rows, :], fwd(m, s), sib)
            f.start()
            forwards.append(f)
            if j > 0:
                pm, ps = order[j - 1]
                pq = chunk_of(chips[pm], ps)
                remote(w12_ref.at[pq, sib_rows, :], fwd(pm, ps), sib).wait_recv()
                project(pq, very_first=False)
        pm, ps = order[-1]
        pq = chunk_of(chips[pm], ps)
        remote(w12_ref.at[pq, sib_rows, :], fwd(pm, ps), sib).wait_recv()
        project(pq, very_first=False)

        for m, chip in enumerate(chips):
            remote(sm4_ref.at[2 * chip[0] + chip[1]], sml(m), sib).wait_recv()
        for cp in sends + forwards:
            cp.wait_send()
        for slot in range(2):
            out_copy(0, slot, slot).wait()

    assert n_mm % 2 == 0 and n_mm >= 2
    vm = pl.BlockSpec(memory_space=pltpu.VMEM)
    hbm = pl.BlockSpec(memory_space=pl.ANY)
    n_sems = 2 * n_ici + 3
    return pl.pallas_call(
        body,
        out_shape=(jax.ShapeDtypeStruct((N_CHUNKS, t, CHUNK), F32), jax.ShapeDtypeStruct((t, D_MODEL), MXU_DTYPE),
                   jax.ShapeDtypeStruct((N_CHUNKS, w_in.shape[0], CHUNK), MXU_DTYPE),
                   jax.ShapeDtypeStruct((N_CHIPS,) + small.shape, F32)),
        in_specs=[hbm, vm, vm, vm], out_specs=(hbm, vm, vm, vm),
        scratch_shapes=[pltpu.VMEM((2, rb_x, D_MODEL), F32), pltpu.VMEM((2, rb_mm, CHUNK), F32),
                        pltpu.SemaphoreType.DMA((2,)), pltpu.SemaphoreType.DMA((2,)),
                        pltpu.SemaphoreType.DMA((n_sems,)), pltpu.SemaphoreType.DMA((n_sems,))],
        compiler_params=_cp(), name="gather_in_projection",
    )(x, ln_g, w_in, small)


def _allreduce_small(buf):
    def body(in_ref, out_ref, r0, r1, r2, send_sems, recv_sems):
        x, y, c, _ = _mesh_pos()
        peers = [(x, y, 1 - c), (1 - x, y, c), (x, 1 - y, c)]
        out_ref[...] = in_ref[...]
        for ph, (peer, rbuf) in enumerate(zip(peers, (r0, r1, r2))):
            cp = pltpu.make_async_remote_copy(src_ref=out_ref, dst_ref=rbuf, send_sem=send_sems.at[ph],
                                              recv_sem=recv_sems.at[ph], device_id=peer, device_id_type=MESH)
            cp.start()
            cp.wait()
            out_ref[...] = out_ref[...] + rbuf[...]

    vm = pl.BlockSpec(memory_space=pltpu.VMEM)
    return pl.pallas_call(
        body, out_shape=jax.ShapeDtypeStruct(buf.shape, F32), in_specs=[vm], out_specs=vm,
        scratch_shapes=[pltpu.VMEM(buf.shape, F32)] * 3 + [pltpu.SemaphoreType.DMA((3,)), pltpu.SemaphoreType.DMA((3,))],
        compiler_params=_cp(), name="allreduce_small",
    )(buf)


def _exchange_sibling_halves(g, name):
    n, rows, cols = g.shape
    half = rows // 2

    def body(g_ref, r_ref, send_sem, recv_sem):
        x, y, c, _ = _mesh_pos()
        cp = pltpu.make_async_remote_copy(src_ref=g_ref.at[:, pl.ds(pl.multiple_of(half * (1 - c), half), half), :],
                                          dst_ref=r_ref, send_sem=send_sem, recv_sem=recv_sem,
                                          device_id=(x, y, 1 - c), device_id_type=MESH)
        cp.start()
        cp.wait()

    hbm = pl.BlockSpec(memory_space=pl.ANY)
    return pl.pallas_call(
        body, out_shape=jax.ShapeDtypeStruct((n, half, cols), F32), in_specs=[hbm], out_specs=hbm,
        scratch_shapes=[pltpu.SemaphoreType.DMA, pltpu.SemaphoreType.DMA],
        compiler_params=_cp(), name=name,
    )(g)


def _add_own_half(g, r, c_arr, name):
    n, rr, cc = r.shape

    def body(c_ref, g_ref, r_ref, o_ref, ob_ref):
        s = g_ref[...] + r_ref[...]
        o_ref[...] = s
        ob_ref[...] = s.astype(jnp.bfloat16)

    blk = pl.BlockSpec((1, rr, cc), lambda q, c_ref: (q, 0, 0))
    return pl.pallas_call(
        body, out_shape=(jax.ShapeDtypeStruct(r.shape, F32), jax.ShapeDtypeStruct(r.shape, jnp.bfloat16)),
        grid_spec=pltpu.PrefetchScalarGridSpec(
            num_scalar_prefetch=1, grid=(n,),
            in_specs=[pl.BlockSpec((1, rr, cc), lambda q, c_ref: (q, c_ref[0], 0)), blk],
            out_specs=(blk, blk)),
        compiler_params=_cp(ARB), name=name,
    )(c_arr, g, r)


def _chip_block_copies(s_ref, r_ref, n_sub, send_sems, recv_sems):
    x, y, c, chips = _mesh_pos()
    cps = []
    for m, chip in enumerate(chips):
        kk = 2 * chip[0] + chip[1]
        cps.append(pltpu.make_async_remote_copy(
            src_ref=s_ref.at[pl.ds(n_sub * kk, n_sub)], dst_ref=r_ref.at[m],
            send_sem=send_sems.at[m], recv_sem=recv_sems.at[m], device_id=(*chip, c), device_id_type=MESH))
    return cps


def _gather_w_out(step, n_steps, wo_ref, wob_s, wo4_ref, local_sem, send_sems, recv_sems):
    x, y, c, chips = _mesh_pos()
    sib = (x, y, 1 - c)
    half = wo_ref.shape[0] // 2

    def rows(core):
        return pl.ds(pl.multiple_of(half * core, half), half)

    def block_half(chip, core):
        return wo4_ref.at[2 * chip[0] + chip[1], rows(core), :]

    def remote(src, dst, sem, to):
        return pltpu.make_async_remote_copy(src_ref=src, dst_ref=dst, send_sem=send_sems.at[sem], recv_sem=recv_sems.at[sem],
                                            device_id=to, device_id_type=MESH)

    local = pltpu.make_async_copy(wob_s, wo4_ref.at[2 * x + y], local_sem)
    ici = [remote(wob_s.at[rows(c), :], block_half((x, y), c), m, (*chip, c)) for m, chip in enumerate(chips)]
    fwd = [remote(block_half(chip, c), block_half(chip, c), 3 + m, sib) for m, chip in enumerate(chips)]

    @pl.when(step == 0)
    def _():
        wob_s[...] = wo_ref[...].astype(MXU_DTYPE)
        local.start()
        for cp in ici:
            cp.start()

    @pl.when(step == n_steps // 2)
    def _():
        for m, chip in enumerate(chips):
            remote(block_half(chip, c), block_half(chip, c), m, sib).wait_recv()
            fwd[m].start()

    @pl.when(step == n_steps - 1)
    def _():
        for m, chip in enumerate(chips):
            remote(block_half(chip, 1 - c), block_half(chip, 1 - c), 3 + m, sib).wait_recv()
        for cp in ici + fwd:
            cp.wait_send()
        local.wait()


def _chip_blocks_shape(s, n_sub):
    return jax.ShapeDtypeStruct((3, n_sub) + s.shape[1:], s.dtype)


def _sum_chip_blocks(s, r, kc_arr, n_sub, name):
    _, rr, cc = s.shape

    def body(kc_ref, s_ref, r_ref, o_ref):
        o_ref[...] = ((s_ref[...] + r_ref[0].astype(F32)) + r_ref[1].astype(F32)) + r_ref[2].astype(F32)

    return pl.pallas_call(
        body, out_shape=jax.ShapeDtypeStruct((n_sub, 2 * rr, cc), F32),
        grid_spec=pltpu.PrefetchScalarGridSpec(
            num_scalar_prefetch=1, grid=(n_sub,),
            in_specs=[pl.BlockSpec((1, rr, cc), lambda q, kc: (n_sub * kc[0] + q, 0, 0)),
                      pl.BlockSpec((3, 1, rr, cc), lambda q, kc: (0, q, 0, 0))],
            out_specs=pl.BlockSpec((1, rr, cc), lambda q, kc: (q, kc[1], 0))),
        compiler_params=_cp(ARB), name=name,
    )(kc_arr, s, r)


def _swap_sibling_halves(f_in, f_out):
    hi, ho = f_in.shape[1] // 2, f_out.shape[1] // 2

    def body(fi_in, fo_in, fi_ref, fo_ref, send_sems, recv_sems):
        del fi_in, fo_in
        x, y, c, _ = _mesh_pos()
        sib = (x, y, 1 - c)
        si = fi_ref.at[:, pl.ds(pl.multiple_of(hi * c, hi), hi), :]
        so = fo_ref.at[:, pl.ds(pl.multiple_of(ho * c, ho), ho), :]
        cps = [
            pltpu.make_async_remote_copy(src_ref=si, dst_ref=si, send_sem=send_sems.at[0], recv_sem=recv_sems.at[0],
                                         device_id=sib, device_id_type=MESH),
            pltpu.make_async_remote_copy(src_ref=so, dst_ref=so, send_sem=send_sems.at[1], recv_sem=recv_sems.at[1],
                                         device_id=sib, device_id_type=MESH),
        ]
        for cp in cps:
            cp.start()
        for cp in cps:
            cp.wait()

    hbm = pl.BlockSpec(memory_space=pl.ANY)
    return pl.pallas_call(
        body,
        out_shape=(jax.ShapeDtypeStruct(f_in.shape, F32), jax.ShapeDtypeStruct(f_out.shape, F32)),
        in_specs=[hbm, hbm], out_specs=(hbm, hbm), input_output_aliases={0: 0, 1: 1},
        scratch_shapes=[pltpu.SemaphoreType.DMA((2,)), pltpu.SemaphoreType.DMA((2,))],
        compiler_params=_cp(), name="swap_sibling_halves",
    )(f_in, f_out)


def _out_projection_loss(yc, yl, x, target, wo, final_g):
    t = x.shape[0]
    tm = 256

    def body(yc_ref, yl_ref, x_ref, t_ref, wo_ref, fg_ref, do_ref, dob_ref, dy_ref, st_ref):
        @pl.when(pl.program_id(0) == 0)
        def _():
            st_ref[...] = jnp.zeros_like(st_ref)

        o = x_ref[...] + (_mm(yc_ref[...], wo_ref[0:D_PART, :]) + _mm(yl_ref[...], wo_ref[D_PART:2 * D_PART, :]))
        r2 = lax.rsqrt(jnp.mean(o * o, axis=-1, keepdims=True) + RMS_EPS)
        ohat = o * r2
        fg = fg_ref[...]
        diff = ohat * fg - t_ref[...]
        dout = diff * (1.0 / D_MODEL)
        gp = dout * fg
        do = r2 * (gp - ohat * jnp.mean(gp * ohat, axis=-1, keepdims=True))
        do_ref[...] = do
        dob = do.astype(MXU_DTYPE)
        dob_ref[...] = dob
        dy_ref[...] = _mm_nt(dob, wo_ref[...])
        st_ref[0:1, :] += jnp.sum(dout * ohat, axis=0, keepdims=True)
        loss = 0.5 * jnp.sum(jnp.sum(diff * diff, axis=-1, keepdims=True) * (1.0 / D_MODEL), axis=0, keepdims=True)
        st_ref[1:2, :] += jnp.broadcast_to(loss, (1, D_MODEL))

    row = lambda i: (i, 0)
    fix = lambda i: (0, 0)
    return pl.pallas_call(
        body, grid=(t // tm,),
        in_specs=[pl.BlockSpec((tm, D_PART), row), pl.BlockSpec((tm, D_PART), row),
                  pl.BlockSpec((tm, D_MODEL), row), pl.BlockSpec((tm, D_MODEL), row),
                  pl.BlockSpec((2 * D_PART, D_MODEL), fix), pl.BlockSpec((1, D_MODEL), fix)],
        out_specs=(pl.BlockSpec((tm, D_MODEL), row), pl.BlockSpec((tm, D_MODEL), row),
                   pl.BlockSpec((tm, 2 * D_PART), row), pl.BlockSpec((SUBLANES, D_MODEL), fix)),
        out_shape=(jax.ShapeDtypeStruct((t, D_MODEL), F32), jax.ShapeDtypeStruct((t, D_MODEL), MXU_DTYPE),
                   jax.ShapeDtypeStruct((t, 2 * D_PART), F32), jax.ShapeDtypeStruct((SUBLANES, D_MODEL), F32)),
        compiler_params=_cp(ARB), name="out_projection_loss",
    )(yc, yl, x, target, wo, final_g)


def _input_grad(dproj, w12, x, do, ln_g, sb_in):
    t = x.shape[0]
    tm = 1024

    def body(dp_ref, w_ref, x_ref, do_ref, g_ref, s_ref, gx_ref, st_ref, r_ref, acc, send_sems, recv_sems):
        i, p = pl.program_id(0), pl.program_id(1)

        @pl.when((i == 0) & (p == 0))
        def _():
            st_ref[...] = jnp.zeros_like(st_ref)
            for cp in _chip_block_copies(s_ref, r_ref, CHUNKS_PER_BLOCK, send_sems, recv_sems):
                cp.start()

        @pl.when((i == t // tm - 1) & (p == N_PARTS - 1))
        def _():
            for cp in _chip_block_copies(s_ref, r_ref, CHUNKS_PER_BLOCK, send_sems, recv_sems):
                cp.wait()

        part = _mm_nt(dp_ref[0, :, 0:CHUNK], w_ref[0]) + _mm_nt(dp_ref[0, :, CHUNK:2 * CHUNK], w_ref[1])

        @pl.when(p == 0)
        def _():
            acc[...] = part

        @pl.when(p > 0)
        def _():
            acc[...] += part

        @pl.when(p == N_PARTS - 1)
        def _():
            def norm_bwd_slab(s, g_sum):
                rows = pl.ds(pl.multiple_of(s * SLAB, SLAB), SLAB)
                xf = x_ref[rows, :]
                r = lax.rsqrt(jnp.mean(xf * xf, axis=-1, keepdims=True) + RMS_EPS)
                xhat = xf * r
                dxn = acc[rows, :]
                dxh = dxn * g_ref[...]
                gx_ref[rows, :] = do_ref[rows, :] + r * (dxh - xhat * jnp.mean(dxh * xhat, axis=-1, keepdims=True))
                return g_sum + jnp.sum(dxn * xhat, axis=0, keepdims=True)

            st_ref[0:1, :] += lax.fori_loop(0, tm // SLAB, norm_bwd_slab, jnp.zeros((1, D_MODEL), F32))

    row = lambda i, p: (i, 0)
    fix = lambda i, p: (0, 0)
    return pl.pallas_call(
        body, grid=(t // tm, N_PARTS),
        in_specs=[
            pl.BlockSpec((1, tm, D_PART), lambda i, p: (p, i, 0)),
            pl.BlockSpec((2, D_MODEL, CHUNK), lambda i, p: (p, 0, 0)),
            pl.BlockSpec((tm, D_MODEL), row), pl.BlockSpec((tm, D_MODEL), row), pl.BlockSpec((1, D_MODEL), fix),
            pl.BlockSpec(memory_space=pl.ANY)],
        out_specs=(pl.BlockSpec((tm, D_MODEL), row), pl.BlockSpec((SUBLANES, D_MODEL), fix),
                   pl.BlockSpec(memory_space=pl.ANY)),
        out_shape=(jax.ShapeDtypeStruct((t, D_MODEL), F32), jax.ShapeDtypeStruct((SUBLANES, D_MODEL), F32),
                   _chip_blocks_shape(sb_in, CHUNKS_PER_BLOCK)),
        scratch_shapes=[pltpu.VMEM((tm, D_MODEL), F32), pltpu.SemaphoreType.DMA((3,)), pltpu.SemaphoreType.DMA((3,))],
        compiler_params=_cp(ARB, ARB), name="input_grad",
    )(dproj, w12, x, do, ln_g, sb_in)


def _w_in_grad(xn, dproj):
    t = xn.shape[0]

    def body(xn_ref, dp_ref, o_ref):
        xnv = xn_ref[...]
        for s in range(2):
            o_ref[s] = _mm_tn(xnv, dp_ref[0, :, CHUNK * s:CHUNK * (s + 1)])

    return pl.pallas_call(
        body, grid=(N_PARTS,),
        in_specs=[pl.BlockSpec((t, D_MODEL), lambda p: (0, 0)),
                  pl.BlockSpec((1, t, D_PART), lambda p: (p, 0, 0))],
        out_specs=pl.BlockSpec((2, D_MODEL, CHUNK), lambda p: (p, 0, 0)),
        out_shape=jax.ShapeDtypeStruct((N_CHUNKS, D_MODEL, CHUNK), F32),
        compiler_params=_cp(ARB), name="w_in_grad",
    )(xn, dproj)


def _w_out_grad(yc, yl, dob):
    t = yc.shape[0]
    tk = 2048

    def body(yc_ref, yl_ref, do_ref, o_ref):
        j, kk = pl.program_id(0), pl.program_id(1)

        def accumulate(y_ref):
            part = _mm_tn(y_ref[...], do_ref[...])

            @pl.when(kk == 0)
            def _():
                o_ref[...] = part

            @pl.when(kk > 0)
            def _():
                o_ref[...] += part

        pl.when(j == 0)(functools.partial(accumulate, yc_ref))
        pl.when(j == 1)(functools.partial(accumulate, yl_ref))

    def rows_of(half):
        return lambda j, kk: (jnp.where(j == half, kk, 0), 0)

    out = pl.pallas_call(
        body, grid=(2, t // tk),
        in_specs=[pl.BlockSpec((tk, D_PART), rows_of(0)), pl.BlockSpec((tk, D_PART), rows_of(1)),
                  pl.BlockSpec((tk, D_MODEL), lambda j, kk: (kk, 0))],
        out_specs=pl.BlockSpec((D_PART, D_MODEL), lambda j, kk: (j, 0)),
        out_shape=jax.ShapeDtypeStruct((2 * D_PART, D_MODEL), F32),
        compiler_params=_cp(ARB, ARB), name="w_out_grad",
    )(yc, yl, dob)
    return out.reshape(N_CHIPS, 2 * D_PART // N_CHIPS, D_MODEL)


def _for_groups(n, fn, init, unroll=UNROLL):
    def trip(j, carry):
        for uu in range(unroll):
            carry = fn(j * unroll + uu, carry)
        return carry

    return lax.fori_loop(0, n // unroll, trip, init)


def _pvb(pv_ref, r):
    return jnp.broadcast_to(pv_ref[r:r + 1, :], (SUBLANES, pv_ref.shape[1]))


def _conv3(pv_ref, u, u1, u2):
    return (_pvb(pv_ref, PV_CONV_W) * u2 + _pvb(pv_ref, PV_CONV_W + 1) * u1) + _pvb(pv_ref, PV_CONV_W + 2) * u


def _conv4(pv_ref, v, v1, v2, v3):
    return ((((_pvb(pv_ref, PV_LRU_W) * v3 + _pvb(pv_ref, PV_LRU_W + 1) * v2) + _pvb(pv_ref, PV_LRU_W + 2) * v1)
             + _pvb(pv_ref, PV_LRU_W + 3) * v) + _pvb(pv_ref, PV_LRU_B))


def _mixer_forward(proj, pvec, wai, w_out):
    t = proj.shape[1]
    tb = 512
    ng = tb // SUBLANES
    nt = t // tb

    def body(bg_ref, cg_ref, xc_ref, gc_ref, xl_ref, gl_ref, pv_ref, wai_ref, wo_ref,
             yc_ref, yl_ref, h_ref, wo4_ref,
             ucp_s, xlp_s, ls_s, hbuf_s, u_s, gate_s, zc_s, zl_s, wob_s, local_sem, send_sems, recv_sems):
        _gather_w_out(pl.program_id(0) * nt + pl.program_id(1), NS * nt, wo_ref, wob_s, wo4_ref, local_sem, send_sems, recv_sems)

        @pl.when(pl.program_id(1) == 0)
        def _():
            ucp_s[...] = jnp.zeros_like(ucp_s)
            xlp_s[...] = jnp.zeros_like(xlp_s)
            hbuf_s[...] = jnp.zeros_like(hbuf_s)

        row = lax.broadcasted_iota(jnp.int32, (SUBLANES, LW), 0)
        ls_s[...] = _log_sigmoid(_pvb(pv_ref, PV_LAM))

        def conv_group(g, carry):
            ucp, xlp = carry
            sl = pl.ds(pl.multiple_of(g * SUBLANES, SUBLANES), SUBLANES)
            uc = cg_ref[sl, :] * xc_ref[sl, :]
            v = _conv3(pv_ref, uc, _shift_down(uc, ucp, 1, row), _shift_down(uc, ucp, 2, row))
            yc = bg_ref[sl, :] * v
            rr = lax.rsqrt(_head_mean(yc * yc, CONV_HEAD) + RMS_EPS)
            gc = gc_ref[sl, :]
            zc_s[sl, :] = ((yc * rr) * _pvb(pv_ref, PV_CG)) * (gc * _sigmoid(gc))
            xl = xl_ref[sl, :]
            u_s[sl, :] = _conv4(pv_ref, xl, _shift_down(xl, xlp, 1, row), _shift_down(xl, xlp, 2, row),
                                _shift_down(xl, xlp, 3, row))
            return uc, xl

        ucp, xlp = _for_groups(ng, conv_group, (ucp_s[...], xlp_s[...]), unroll=2 * UNROLL)
        ucp_s[...] = ucp
        xlp_s[...] = xlp

        gate_s[...] = _mm(u_s[...].astype(MXU_DTYPE), wai_ref[0])

        def lru_group(g, h_before):
            sl = pl.ds(pl.multiple_of(g * SUBLANES, SUBLANES), SUBLANES)
            u = u_s[sl, :]
            r, ig, a, e2, mult, _ = _gates(gate_s[sl, 0:LW] + _pvb(pv_ref, PV_BA),
                                           gate_s[sl, LW:2 * LW] + _pvb(pv_ref, PV_BI), u, ls_s[...])
            A, B = _scan8_fwd(a, mult * (ig * u), row)
            h = B + A * jnp.broadcast_to(h_before[SUBLANES - 1:SUBLANES, :], (SUBLANES, LW))
            h_ref[sl, :] = h
            rr = lax.rsqrt(_head_mean(h * h, LRU_HEAD) + RMS_EPS)
            gl = gl_ref[sl, :]
            zl_s[sl, :] = ((h * rr) * _pvb(pv_ref, PV_LG)) * (gl * _sigmoid(gl))
            return h

        hbuf_s[...] = _for_groups(ng, lru_group, hbuf_s[...], unroll=2 * UNROLL)
        yc_ref[...] = zc_s[...].astype(MXU_DTYPE)
        yl_ref[...] = zl_s[...].astype(MXU_DTYPE)

    def part(p):
        return pl.BlockSpec((None, tb, LW), lambda c, i: (2 * p + c // STRIPS_PER_CHUNK, i, c % STRIPS_PER_CHUNK))

    strip = pl.BlockSpec((tb, LW), lambda c, i: (i, c))
    return pl.pallas_call(
        body, grid=(NS, nt),
        in_specs=[part(p) for p in range(N_PARTS)] + [
            pl.BlockSpec((PV_ROWS, LW), lambda c, i: (0, c)),
            pl.BlockSpec((1, LW, 2 * LW), lambda c, i: (c, 0, 0)),
            pl.BlockSpec(w_out.shape, lambda c, i: (0, 0))],
        out_specs=(strip, strip, strip, pl.BlockSpec(memory_space=pl.ANY)),
        out_shape=(jax.ShapeDtypeStruct((t, D_PART), MXU_DTYPE), jax.ShapeDtypeStruct((t, D_PART), MXU_DTYPE),
                   jax.ShapeDtypeStruct((t, D_PART), F32), jax.ShapeDtypeStruct((N_CHIPS,) + w_out.shape, MXU_DTYPE)),
        scratch_shapes=[pltpu.VMEM((SUBLANES, LW), F32), pltpu.VMEM((SUBLANES, LW), F32), pltpu.VMEM((SUBLANES, LW), F32),
                        pltpu.VMEM((SUBLANES, LW), F32), pltpu.VMEM((tb, LW), F32), pltpu.VMEM((tb, 2 * LW), F32),
                        pltpu.VMEM((tb, LW), F32), pltpu.VMEM((tb, LW), F32), pltpu.VMEM(w_out.shape, MXU_DTYPE),
                        pltpu.SemaphoreType.DMA, pltpu.SemaphoreType.DMA((6,)), pltpu.SemaphoreType.DMA((6,))],
        compiler_params=_cp(ARB, ARB), name="mixer_forward",
    )(proj, proj, proj, proj, proj, proj, pvec, wai, w_out)


def _mixer_backward(proj, h, dy, pvec, wai, sb_out):
    t = proj.shape[1]
    tb = 512
    ng = tb // SUBLANES
    nt = t // tb
    gpb = tb // SUBLANES

    def body(bg_ref, cg_ref, xc_ref, gc_ref, xl_ref, gl_ref, h_ref, dyc_ref, dyl_ref,
             cgh_ref, xch_ref, xlh_ref, hh_ref, pv_ref, wai_ref, so_ref,
             dp_ref, gw_ref, sv_ref, ro_ref,
             ls_s, u_s, uce_s, xle_s, he_s, gate_s, dgate_s, du_s, gbuf_s,
             p0_s, p1_s, p2_s, p3_s, p4_s, p5_s, acc_s, an_s, dvn_s, dun_s, send_sems, recv_sems):
        i = pl.program_id(1)
        first_block = i == nt - 1

        @pl.when((pl.program_id(0) == 0) & (i == 0))
        def _():
            for cp in _chip_block_copies(so_ref, ro_ref, 1, send_sems, recv_sems):
                cp.start()

        @pl.when((pl.program_id(0) == NS - 1) & (i == nt - 1))
        def _():
            for cp in _chip_block_copies(so_ref, ro_ref, 1, send_sems, recv_sems):
                cp.wait()

        @pl.when(i == 0)
        def _():
            acc_s[...] = jnp.zeros_like(acc_s)
            gw_ref[...] = jnp.zeros_like(gw_ref)
            an_s[...] = jnp.zeros_like(an_s)
            dvn_s[...] = jnp.zeros_like(dvn_s)
            dun_s[...] = jnp.zeros_like(dun_s)
            gbuf_s[...] = jnp.zeros_like(gbuf_s)

        row = lax.broadcasted_iota(jnp.int32, (SUBLANES, LW), 0)
        ls_s[...] = _log_sigmoid(_pvb(pv_ref, PV_LAM))
        keep = jnp.where(first_block, 0.0, 1.0)
        uce_s[0:SUBLANES, :] = (cgh_ref[...] * xch_ref[...]) * keep
        xle_s[0:SUBLANES, :] = xlh_ref[...] * keep
        he_s[0:SUBLANES, :] = hh_ref[...] * keep
        xle_s[SUBLANES:SUBLANES + tb, :] = xl_ref[...]
        he_s[SUBLANES:SUBLANES + tb, :] = h_ref[...]

        def recompute_group(g, carry):
            r0 = pl.multiple_of(g * SUBLANES, SUBLANES)
            sl = pl.ds(r0, SUBLANES)
            uce_s[pl.ds(r0 + SUBLANES, SUBLANES), :] = cg_ref[sl, :] * xc_ref[sl, :]
            xl = xle_s[pl.ds(r0 + SUBLANES, SUBLANES), :]
            xlp = xle_s[sl, :]
            u_s[sl, :] = _conv4(pv_ref, xl, _shift_down(xl, xlp, 1, row), _shift_down(xl, xlp, 2, row),
                                _shift_down(xl, xlp, 3, row))
            return carry

        _for_groups(ng, recompute_group, 0)
        gate_s[...] = _mm(u_s[...].astype(MXU_DTYPE), wai_ref[0])

        def acc_add(k, v):
            acc_s[k] += v

        def main_group(gi, carry):
            a_next, dv_next, g_next = carry
            g = ng - 1 - gi
            r0 = pl.multiple_of(g * SUBLANES, SUBLANES)
            sl = pl.ds(r0, SUBLANES)
            sl_e = pl.ds(r0 + SUBLANES, SUBLANES)
            lsb = ls_s[...]
            u = u_s[sl, :]
            r, ig, a, e2, mult, inv_mult = _gates(gate_s[sl, 0:LW] + _pvb(pv_ref, PV_BA),
                                                  gate_s[sl, LW:2 * LW] + _pvb(pv_ref, PV_BI), u, lsb)
            gl = gl_ref[sl, :]
            sg = _sigmoid(gl)
            s_l = gl * sg
            h8 = he_s[sl_e, :]
            hprev = _shift_down(h8, he_s[sl, :], 1, row)
            rr = lax.rsqrt(_head_mean(h8 * h8, LRU_HEAD) + RMS_EPS)
            n = h8 * rr
            dz = dyl_ref[sl, :]
            lg = _pvb(pv_ref, PV_LG)
            acc_add(PV_LG, (dz * n) * s_l)
            p5_s[sl, :] = ((dz * n) * lg) * (sg * (1.0 + gl * (1.0 - sg)))
            dn = (dz * lg) * s_l
            dh = rr * (dn - n * _head_mean(dn * n, LRU_HEAD))
            A, B = _scan8_rev(_shift_up(a, a_next, 1, row), dh, row)
            gg = B + A * jnp.broadcast_to(g_next[0:1, :], (SUBLANES, LW))
            da = gg * hprev
            iu = ig * u
            diu = gg * mult
            dla = da * a - (gg * iu) * (e2 * inv_mult)
            acc_add(PV_LAM, dla * (RG_LRU_C * r))
            dra = (dla * (RG_LRU_C * lsb)) * (r * (1.0 - r))
            dia = (diu * u) * (ig * (1.0 - ig))
            dgate_s[sl, 0:LW] = dra
            dgate_s[sl, LW:2 * LW] = dia
            acc_add(PV_BA, dra)
            acc_add(PV_BI, dia)
            du_s[sl, :] = diu * ig
            bg = bg_ref[sl, :]
            gc = gc_ref[sl, :]
            uc = uce_s[sl_e, :]
            ucp = uce_s[sl, :]
            uc1 = _shift_down(uc, ucp, 1, row)
            uc2 = _shift_down(uc, ucp, 2, row)
            v = _conv3(pv_ref, uc, uc1, uc2)
            yc = bg * v
            rrc = lax.rsqrt(_head_mean(yc * yc, CONV_HEAD) + RMS_EPS)
            nc = yc * rrc
            sgc = _sigmoid(gc)
            s_c = gc * sgc
            dzc = dyc_ref[sl, :]
            cgain = _pvb(pv_ref, PV_CG)
            acc_add(PV_CG, (dzc * nc) * s_c)
            p3_s[sl, :] = ((dzc * nc) * cgain) * (sgc * (1.0 + gc * (1.0 - sgc)))
            dnc = (dzc * cgain) * s_c
            dyc = rrc * (dnc - nc * _head_mean(dnc * nc, CONV_HEAD))
            p0_s[sl, :] = dyc * v
            dv = dyc * bg
            duc = (_pvb(pv_ref, PV_CONV_W + 2) * dv + _pvb(pv_ref, PV_CONV_W + 1) * _shift_up(dv, dv_next, 1, row)
                   + _pvb(pv_ref, PV_CONV_W) * _shift_up(dv, dv_next, 2, row))
            acc_add(PV_CONV_W + 2, dv * uc)
            acc_add(PV_CONV_W + 1, dv * uc1)
            acc_add(PV_CONV_W, dv * uc2)
            p1_s[sl, :] = duc * xc_ref[sl, :]
            p2_s[sl, :] = duc * cg_ref[sl, :]
            return a, dv, gg

        a_next, dv_next, g_next = _for_groups(ng, main_group, (an_s[...], dvn_s[...], gbuf_s[...]))
        an_s[...] = a_next
        dvn_s[...] = dv_next
        gbuf_s[...] = g_next

        dgb = dgate_s[...].astype(MXU_DTYPE)
        du_s[...] += _mm_nt(dgb, wai_ref[0])
        gw_ref[0] += _mm_tn(u_s[...].astype(MXU_DTYPE), dgb)

        def lru_conv_group(gi, du_next):
            g = ng - 1 - gi
            r0 = pl.multiple_of(g * SUBLANES, SUBLANES)
            sl = pl.ds(r0, SUBLANES)
            du = du_s[sl, :]
            xl = xle_s[pl.ds(r0 + SUBLANES, SUBLANES), :]
            xlp = xle_s[sl, :]
            acc_add(PV_LRU_B, du)
            acc_add(PV_LRU_W + 3, du * xl)
            acc_add(PV_LRU_W + 2, du * _shift_down(xl, xlp, 1, row))
            acc_add(PV_LRU_W + 1, du * _shift_down(xl, xlp, 2, row))
            acc_add(PV_LRU_W, du * _shift_down(xl, xlp, 3, row))
            p4_s[sl, :] = (((_pvb(pv_ref, PV_LRU_W + 3) * du + _pvb(pv_ref, PV_LRU_W + 2) * _shift_up(du, du_next, 1, row))
                            + _pvb(pv_ref, PV_LRU_W + 1) * _shift_up(du, du_next, 2, row))
                           + _pvb(pv_ref, PV_LRU_W) * _shift_up(du, du_next, 3, row))
            return du

        dun_s[...] = _for_groups(ng, lru_conv_group, dun_s[...])

        for p, p_s in enumerate((p0_s, p1_s, p2_s, p3_s, p4_s, p5_s)):
            dp_ref[p] = p_s[...].astype(MXU_DTYPE)

        @pl.when(first_block)
        def _():
            sv_ref[...] = jnp.zeros_like(sv_ref)
            for k in range(N_ACC):
                tot = jnp.sum(acc_s[k], axis=0, keepdims=True)
                if k == PV_LAM:
                    tot = tot / (1.0 + jnp.exp(pv_ref[PV_LAM:PV_LAM + 1, :]))
                sv_ref[k:k + 1, :] = tot

    def part(p):
        return pl.BlockSpec((None, tb, LW), lambda c, i: (2 * p + c // STRIPS_PER_CHUNK, nt - 1 - i, c % STRIPS_PER_CHUNK))

    def halo(p):
        return pl.BlockSpec((None, SUBLANES, LW), lambda c, i: (2 * p + c // STRIPS_PER_CHUNK,
                                                                jnp.maximum((nt - 1 - i) * gpb - 1, 0), c % STRIPS_PER_CHUNK))

    strip = pl.BlockSpec((tb, LW), lambda c, i: (nt - 1 - i, c))
    big = pltpu.VMEM((tb, LW), F32)
    big_e = pltpu.VMEM((tb + SUBLANES, LW), F32)
    wide = pltpu.VMEM((tb, 2 * LW), F32)
    small = pltpu.VMEM((SUBLANES, LW), F32)
    outs = pl.pallas_call(
        body, grid=(NS, nt),
        in_specs=[part(p) for p in range(N_PARTS)] + [
            strip, strip, pl.BlockSpec((tb, LW), lambda c, i: (nt - 1 - i, NS + c)),
            halo(1), halo(2), halo(4),
            pl.BlockSpec((SUBLANES, LW), lambda c, i: (jnp.maximum((nt - 1 - i) * gpb - 1, 0), c)),
            pl.BlockSpec((PV_ROWS, LW), lambda c, i: (0, c)),
            pl.BlockSpec((1, LW, 2 * LW), lambda c, i: (c, 0, 0)),
            pl.BlockSpec(memory_space=pl.ANY)],
        out_specs=(pl.BlockSpec((N_PARTS, tb, LW), lambda c, i: (0, nt - 1 - i, c)),
                   pl.BlockSpec((1, LW, 2 * LW), lambda c, i: (c, 0, 0)),
                   pl.BlockSpec((PV_ROWS, LW), lambda c, i: (0, c)),
                   pl.BlockSpec(memory_space=pl.ANY)),
        out_shape=(jax.ShapeDtypeStruct((N_PARTS, t, D_PART), MXU_DTYPE),
                   jax.ShapeDtypeStruct((NS, LW, 2 * LW), F32), jax.ShapeDtypeStruct((PV_ROWS, D_PART), F32),
                   _chip_blocks_shape(sb_out, 1)),
        scratch_shapes=[small, big, big_e, big_e, big_e, wide, wide, big, small,
                        big, big, big, big, big, big, pltpu.VMEM((N_ACC, SUBLANES, LW), F32), small, small, small,
                        pltpu.SemaphoreType.DMA((3,)), pltpu.SemaphoreType.DMA((3,))],
        compiler_params=_cp(ARB, ARB), name="mixer_backward",
    )(proj, proj, proj, proj, proj, proj, h, dy, dy, proj, proj, proj, h, pvec, wai, sb_out)
    return outs


def _adamw(w, g, m, v):
    m = ADAM_B1 * m + (1.0 - ADAM_B1) * g
    v = ADAM_B2 * v + (1.0 - ADAM_B2) * (g * g)
    m_hat = m / (1.0 - ADAM_B1 ** ADAM_STEP)
    v_hat = v / (1.0 - ADAM_B2 ** ADAM_STEP)
    delta = -ADAM_LR * (m_hat / (jnp.sqrt(v_hat) + ADAM_EPS) + ADAM_WD * w)
    return delta, m, v


def _adam_w_in(w, m, v, g3):
    rows, cols = w.shape
    tr = 128

    def body(w_ref, m_ref, v_ref, g_ref, go_ref, d_ref, mo_ref, vo_ref):
        for s in range(CHUNKS_PER_BLOCK):
            cs = slice(CHUNK * s, CHUNK * (s + 1))
            g = g_ref[s]
            d, mn, vn = _adamw(w_ref[:, cs], g, m_ref[:, cs], v_ref[:, cs])
            go_ref[:, cs] = g
            d_ref[:, cs] = d
            mo_ref[:, cs] = mn
            vo_ref[:, cs] = vn

    blk = pl.BlockSpec((tr, cols), lambda i: (i, 0))
    return pl.pallas_call(
        body, grid=(rows // tr,),
        in_specs=[blk, blk, blk, pl.BlockSpec((CHUNKS_PER_BLOCK, tr, CHUNK), lambda i: (0, i, 0))],
        out_specs=(blk,) * 4, out_shape=(jax.ShapeDtypeStruct(w.shape, F32),) * 4,
        compiler_params=_cp(ARB), name="adam_w_in",
    )(w, m, v, g3)


def _adam_w_out(w, m, v, g):
    rows, cols = w.shape
    tr = 128

    def body(w_ref, m_ref, v_ref, g_ref, d_ref, mo_ref, vo_ref):
        d_ref[...], mo_ref[...], vo_ref[...] = _adamw(w_ref[...], g_ref[...], m_ref[...], v_ref[...])

    blk = pl.BlockSpec((tr, cols), lambda i: (i, 0))
    return pl.pallas_call(
        body, grid=(rows // tr,), in_specs=[blk] * 4, out_specs=(blk,) * 3,
        out_shape=(jax.ShapeDtypeStruct(w.shape, F32),) * 3,
        compiler_params=_cp(ARB), name="adam_w_out",
    )(w, m, v, g)


def _adam_small(ws, ms, vs, gs):
    n = len(ws)

    def body(*refs):
        w_r, m_r, v_r, g_r = refs[0:n], refs[n:2 * n], refs[2 * n:3 * n], refs[3 * n:4 * n]
        d_o, m_o, v_o = refs[4 * n:5 * n], refs[5 * n:6 * n], refs[6 * n:7 * n]
        for j in range(n):
            d_o[j][...], m_o[j][...], v_o[j][...] = _adamw(w_r[j][...], g_r[j][...], m_r[j][...], v_r[j][...])

    vm = pl.BlockSpec(memory_space=pltpu.VMEM)
    shapes = tuple(jax.ShapeDtypeStruct(w.shape, F32) for w in ws)
    outs = pl.pallas_call(
        body, in_specs=[vm] * (4 * n), out_specs=(vm,) * (3 * n), out_shape=shapes * 3,
        compiler_params=_cp(), name="adam_small",
    )(*ws, *ms, *vs, *gs)
    return outs[0:n], outs[n:2 * n], outs[2 * n:3 * n]


def _block_diag_strips(w):
    w4 = w.reshape(NS, HEADS_PER_STRIP, LRU_HEAD, LRU_HEAD)
    bd = jnp.zeros((NS, HEADS_PER_STRIP, LRU_HEAD, HEADS_PER_STRIP, LRU_HEAD), w.dtype)
    for hh in range(HEADS_PER_STRIP):
        bd = bd.at[:, hh, :, hh, :].set(w4[:, hh])
    return bd.reshape(NS, LW, LW)


def _strip_diag_blocks(g):
    g5 = g.reshape(NS, HEADS_PER_STRIP, LRU_HEAD, HEADS_PER_STRIP, LRU_HEAD)
    return jnp.stack([g5[:, hh, :, hh, :] for hh in range(HEADS_PER_STRIP)], axis=1).reshape(NS * HEADS_PER_STRIP, LRU_HEAD, LRU_HEAD)


def kernel(x, ln_g, w_in, conv_w, lru_conv_w, lru_conv_b, w_a, b_a, w_i, b_i, lam, conv_out_g, lru_out_g, w_out, final_g, loss_target, m_ln_g, m_w_in, m_conv_w, m_lru_conv_w, m_lru_conv_b, m_w_a, m_b_a, m_w_i, m_b_i, m_lam, m_conv_out_g, m_lru_out_g, m_w_out, m_final_g, v_ln_g, v_w_in, v_conv_w, v_lru_conv_w, v_lru_conv_b, v_w_a, v_b_a, v_w_i, v_b_i, v_lam, v_conv_out_g, v_lru_out_g, v_w_out, v_final_g):
    xi, yi, ci = lax.axis_index("x"), lax.axis_index("y"), lax.axis_index("c")
    k = 2 * xi + yi
    t = x.shape[1]
    x2 = x.reshape(t, D_MODEL)
    tgt2 = loss_target.reshape(t, D_MODEL)
    row = lambda a: a.reshape(1, -1)

    small = jnp.concatenate([conv_w, lru_conv_w, jnp.zeros((1, conv_w.shape[1]), F32)], axis=0)
    proj, xn, w12, sm4 = _gather_in_projection(x2, row(ln_g), w_in, small)
    convs = jnp.transpose(sm4, (1, 0, 2)).reshape(SUBLANES, D_PART)
    pvec = jnp.concatenate(
        [convs[0:7], row(lru_conv_b), row(b_a), row(b_i), row(lam), row(conv_out_g), row(lru_out_g),
         jnp.zeros((PV_ROWS - N_ACC, D_PART), F32)], axis=0)
    wai = jnp.concatenate([_block_diag_strips(w_a), _block_diag_strips(w_i)], axis=2).astype(MXU_DTYPE)

    c_arr = jnp.reshape(ci, (1,)).astype(jnp.int32)
    kc_arr = jnp.stack([k, ci]).astype(jnp.int32)
    yc, yl, h, wo4 = _mixer_forward(proj, pvec, wai, w_out)
    wo = wo4.reshape(2 * D_PART, D_MODEL)
    do, dob, dy, st_out = _out_projection_loss(yc, yl, x2, tgt2, wo, row(final_g))
    go4 = _w_out_grad(yc, yl, dob)
    s_out, sb_out = _add_own_half(go4, _exchange_sibling_halves(go4, "exchange_sibling_halves_out"), c_arr, "add_own_half_out")
    dproj, g_wai, svec, r2o = _mixer_backward(proj, h, dy, pvec, wai, sb_out)
    g12 = _w_in_grad(xn, dproj)
    s_in, sb_in = _add_own_half(g12, _exchange_sibling_halves(g12, "exchange_sibling_halves_in"), c_arr, "add_own_half_in")
    grad_x, st_in, r2i = _input_grad(dproj, w12, x2, do, row(ln_g), sb_in)
    f_in = _sum_chip_blocks(s_in, r2i, kc_arr, CHUNKS_PER_BLOCK, "sum_chip_blocks_in")
    f_out = _sum_chip_blocks(s_out, r2o, kc_arr, 1, "sum_chip_blocks_out")
    f_in, f_out = _swap_sibling_halves(f_in, f_out)

    gwa = _strip_diag_blocks(g_wai[:, :, 0:LW]).reshape(LRU_HEAD, D_PART)
    gwi = _strip_diag_blocks(g_wai[:, :, LW:2 * LW]).reshape(LRU_HEAD, D_PART)
    red = _allreduce_small(jnp.concatenate([svec, st_out, st_in, gwa, gwi], axis=0))
    r_out = PV_ROWS
    r_in = PV_ROWS + SUBLANES
    r_wa = PV_ROWS + 2 * SUBLANES
    r_wi = r_wa + LRU_HEAD
    loss = red[r_out + 1, 0]

    g_w_in, d_w_in, nm_w_in, nv_w_in = _adam_w_in(w_in, m_w_in, v_w_in, f_in)
    g_w_out = f_out[0]
    d_w_out, nm_w_out, nv_w_out = _adam_w_out(w_out, m_w_out, v_w_out, g_w_out)

    ncol = conv_w.shape[1]
    conv_cols = lax.dynamic_slice(red, (0, k * ncol), (SUBLANES, ncol))
    g_small = {
        "ln_g": red[r_in], "conv_w": conv_cols[0:3], "lru_conv_w": conv_cols[3:7], "lru_conv_b": red[PV_LRU_B],
        "w_a": red[r_wa:r_wa + LRU_HEAD].reshape(w_a.shape), "b_a": red[PV_BA],
        "w_i": red[r_wi:r_wi + LRU_HEAD].reshape(w_i.shape), "b_i": red[PV_BI], "lam": red[PV_LAM],
        "conv_out_g": red[PV_CG], "lru_out_g": red[PV_LG], "final_g": red[r_out],
    }
    w_small = {"ln_g": ln_g, "conv_w": conv_w, "lru_conv_w": lru_conv_w, "lru_conv_b": lru_conv_b, "w_a": w_a, "b_a": b_a,
               "w_i": w_i, "b_i": b_i, "lam": lam, "conv_out_g": conv_out_g, "lru_out_g": lru_out_g, "final_g": final_g}
    m_small = {"ln_g": m_ln_g, "conv_w": m_conv_w, "lru_conv_w": m_lru_conv_w, "lru_conv_b": m_lru_conv_b, "w_a": m_w_a,
               "b_a": m_b_a, "w_i": m_w_i, "b_i": m_b_i, "lam": m_lam, "conv_out_g": m_conv_out_g,
               "lru_out_g": m_lru_out_g, "final_g": m_final_g}
    v_small = {"ln_g": v_ln_g, "conv_w": v_conv_w, "lru_conv_w": v_lru_conv_w, "lru_conv_b": v_lru_conv_b, "w_a": v_w_a,
               "b_a": v_b_a, "w_i": v_w_i, "b_i": v_b_i, "lam": v_lam, "conv_out_g": v_conv_out_g,
               "lru_out_g": v_lru_out_g, "final_g": v_final_g}
    names = list(w_small)
    as2d = lambda a: a.reshape(1, -1) if a.ndim == 1 else a
    d_s, m_s, v_s = _adam_small([as2d(w_small[n]) for n in names], [as2d(m_small[n]) for n in names],
                                [as2d(v_small[n]) for n in names], [as2d(g_small[n]) for n in names])
    back = lambda n, a: a.reshape(w_small[n].shape)
    grads = {n: g_small[n] for n in names}
    deltas = {n: back(n, a) for n, a in zip(names, d_s)}
    new_m = {n: back(n, a) for n, a in zip(names, m_s)}
    new_v = {n: back(n, a) for n, a in zip(names, v_s)}
    grads["w_in"], deltas["w_in"], new_m["w_in"], new_v["w_in"] = g_w_in, d_w_in, nm_w_in, nv_w_in
    grads["w_out"], deltas["w_out"], new_m["w_out"], new_v["w_out"] = g_w_out, d_w_out, nm_w_out, nv_w_out

    order = ["ln_g", "w_in", "conv_w", "lru_conv_w", "lru_conv_b", "w_a", "b_a", "w_i", "b_i", "lam", "conv_out_g",
             "lru_out_g", "w_out", "final_g"]
    return (loss, grad_x.reshape(x.shape), *[grads[n] for n in order], *[deltas[n] for n in order],
            *[new_m[n] for n in order], *[new_v[n] for n in order])
```

```python
import functools

import jax
import jax.numpy as jnp
from jax import lax
from jax.experimental import pallas as pl
from jax.experimental.pallas import tpu as pltpu

F32 = jnp.float32
MXU_DTYPE = jnp.bfloat16

D_MODEL = 1024
D_PART = 1024
N_PARTS = 6
CHUNK = 512
CHUNKS_PER_BLOCK = 3
N_CHUNKS = 12
N_CHIPS = 4
SUBLANES = 8
LANES = 128
LW = 256
UNROLL = 8
NS = D_PART // LW
STRIPS_PER_CHUNK = CHUNK // LW
CONV_HEAD = 128
LRU_HEAD = 64
HEADS_PER_STRIP = LW // LRU_HEAD
RMS_EPS = 1e-6
RG_LRU_C = 8.0
ADAM_LR = 0.001
ADAM_B1 = 0.9
ADAM_B2 = 0.999
ADAM_EPS = 1e-08
ADAM_WD = 0.01
ADAM_STEP = 10

PV_CONV_W = 0
PV_LRU_W = 3
PV_LRU_B = 7
PV_BA = 8
PV_BI = 9
PV_LAM = 10
PV_CG = 11
PV_LG = 12
PV_ROWS = 16
N_ACC = 13

SLAB = 128
MESH = pl.DeviceIdType.MESH
VMEM_LIMIT = 56 * 1024 * 1024
ARB = "arbitrary"


def _cp(*sem, **kw):
    return pltpu.CompilerParams(dimension_semantics=sem or None, vmem_limit_bytes=VMEM_LIMIT, **kw)


def _mm(a, b):
    return jnp.dot(a, b, preferred_element_type=F32)


def _mm_nt(a, b):
    return lax.dot_general(a, b, (((1,), (1,)), ((), ())), preferred_element_type=F32)


def _mm_tn(a, b):
    return lax.dot_general(a, b, (((0,), (0,)), ((), ())), preferred_element_type=F32)


def _sigmoid(x):
    return 0.5 * jnp.tanh(0.5 * x) + 0.5


def _log_sigmoid(x):
    z = jnp.exp(-jnp.abs(x))
    u = 1.0 + z
    log1p = jnp.where(u == 1.0, z, jnp.log(u) * z / (u - 1.0))
    return jnp.minimum(x, 0.0) - log1p


def _head_mean(z, head):
    out = []
    for k in range(z.shape[1] // LANES):
        zk = z[:, LANES * k:LANES * (k + 1)]
        if head == LANES:
            m = jnp.sum(zk, axis=-1, keepdims=True) * (1.0 / head)
            out.append(jnp.broadcast_to(m, zk.shape))
        else:
            lo = lax.broadcasted_iota(jnp.int32, zk.shape, 1) < head
            s_lo = jnp.sum(jnp.where(lo, zk, 0.0), axis=-1, keepdims=True)
            s_hi = jnp.sum(jnp.where(lo, 0.0, zk), axis=-1, keepdims=True)
            out.append(jnp.where(lo, s_lo, s_hi) * (1.0 / head))
    return jnp.concatenate(out, axis=1)


def _shift_down(cur, prev, d, row):
    return pltpu.roll(jnp.where(row < SUBLANES - d, cur, prev), d, 0)


def _shift_up(cur, nxt, d, row):
    return pltpu.roll(jnp.where(row >= d, cur, nxt), SUBLANES - d, 0)


def _scan8_fwd(a, b, row):
    A, B = a, b
    for d in (1, 2, 4):
        m = row >= d
        a_s = jnp.where(m, pltpu.roll(A, d, 0), 1.0)
        b_s = jnp.where(m, pltpu.roll(B, d, 0), 0.0)
        B = A * b_s + B
        A = A * a_s
    return A, B


def _scan8_rev(a, b, row):
    A, B = a, b
    for d in (1, 2, 4):
        m = row < SUBLANES - d
        a_s = jnp.where(m, pltpu.roll(A, SUBLANES - d, 0), 1.0)
        b_s = jnp.where(m, pltpu.roll(B, SUBLANES - d, 0), 0.0)
        B = A * b_s + B
        A = A * a_s
    return A, B


def _gates(ra, ia, u, lsb):
    r = _sigmoid(ra)
    ig = _sigmoid(ia)
    la = (RG_LRU_C * r) * lsb
    a = jnp.exp(la)
    e2 = a * a
    em = -jnp.tanh(la) * (1.0 + e2)
    inv_mult = lax.rsqrt(em)
    return r, ig, a, e2, em * inv_mult, inv_mult


def _mesh_pos():
    x, y, c = lax.axis_index("x"), lax.axis_index("y"), lax.axis_index("c")
    chips = [(1 - x, y), (x, 1 - y), (1 - x, 1 - y)]
    return x, y, c, chips


def _gather_in_projection(x, ln_g, w_in, small):
    t = x.shape[0]
    rb_x = 512
    rb_mm = 1024
    n_mm = t // rb_mm
    half = w_in.shape[0] // 2
    n_ici = 3 * CHUNKS_PER_BLOCK

    def body(x_hbm, g_ref, wi_ref, sm_ref, proj_hbm, xn_ref, w12_ref, sm4_ref,
             xbuf, obuf, x_sems, o_sems, send_sems, recv_sems):
        x_, y_, c, chips = _mesh_pos()
        k = 2 * x_ + y_
        sib = (x_, y_, 1 - c)
        my_rows = pl.ds(pl.multiple_of(half * c, half), half)
        sib_rows = pl.ds(pl.multiple_of(half * (1 - c), half), half)

        for s in range(CHUNKS_PER_BLOCK):
            w12_ref[CHUNKS_PER_BLOCK * k + s] = wi_ref[:, CHUNK * s:CHUNK * (s + 1)].astype(MXU_DTYPE)
        sm4_ref[k] = sm_ref[...]

        def remote(ref, sem, to):
            return pltpu.make_async_remote_copy(src_ref=ref, dst_ref=ref, send_sem=send_sems.at[sem],
                                                recv_sem=recv_sems.at[sem], device_id=to, device_id_type=MESH)

        def chunk_of(chip, s):
            return CHUNKS_PER_BLOCK * (2 * chip[0] + chip[1]) + s

        ici = lambda m, s: 3 * s + m
        fwd = lambda m, s: n_ici + 3 * s + m
        sml = lambda m: 2 * n_ici + m

        sends = []
        for s in range(CHUNKS_PER_BLOCK):
            for m, chip in enumerate(chips):
                sends.append(remote(w12_ref.at[chunk_of((x_, y_), s), my_rows, :], ici(m, s), (*chip, c)))
        for m, chip in enumerate(chips):
            sends.append(remote(sm4_ref.at[k], sml(m), (*chip, c)))
        for cp in sends:
            cp.start()

        def x_copy(rb, slot):
            return pltpu.make_async_copy(x_hbm.at[pl.ds(rb * rb_x, rb_x), :], xbuf.at[slot], x_sems.at[slot])

        x_copy(0, 0).start()
        for rb in range(t // rb_x):
            slot = rb % 2
            x_copy(rb, slot).wait()
            if rb + 1 < t // rb_x:
                x_copy(rb + 1, 1 - slot).start()

            def norm_slab(sl, carry, rb=rb, slot=slot):
                xf = xbuf[slot, pl.ds(pl.multiple_of(sl * SLAB, SLAB), SLAB), :]
                r = lax.rsqrt(jnp.mean(xf * xf, axis=-1, keepdims=True) + RMS_EPS)
                xn_ref[pl.ds(pl.multiple_of(rb * rb_x + sl * SLAB, SLAB), SLAB), :] = ((xf * r) * g_ref[...]).astype(MXU_DTYPE)
                return carry

            lax.fori_loop(0, rb_x // SLAB, norm_slab, 0)

        def out_copy(q, i, slot):
            return pltpu.make_async_copy(obuf.at[slot], proj_hbm.at[q, pl.ds(pl.multiple_of(i * rb_mm, rb_mm), rb_mm), :],
                                         o_sems.at[slot])

        def project(q, very_first):
            def row_block(i, carry):
                slot = i % 2

                def wait_buffer():
                    out_copy(q, i, slot).wait()

                if very_first:
                    pl.when(i >= 2)(wait_buffer)
                else:
                    wait_buffer()
                obuf[slot] = _mm(xn_ref[pl.ds(pl.multiple_of(i * rb_mm, rb_mm), rb_mm), :], w12_ref[q])
                out_copy(q, i, slot).start()
                return carry

            lax.fori_loop(0, n_mm, row_block, 0)

        for s in range(CHUNKS_PER_BLOCK):
            project(chunk_of((x_, y_), s), very_first=(s == 0))

        order = [(m, s) for s in range(CHUNKS_PER_BLOCK) for m in range(3)]
        forwards = []
        for j, (m, s) in enumerate(order):
            q = chunk_of(chips[m], s)
            remote(w12_ref.at[q, my_rows, :], ici(m, s), sib).wait_recv()
            f = remote(w12_ref.at[q, my_rows, :], fwd(m, s), sib)
            f.start()
            forwards.append(f)
            if j > 0:
                pm, ps = order[j - 1]
                pq = chunk_of(chips[pm], ps)
                remote(w12_ref.at[pq, sib_rows, :], fwd(pm, ps), sib).wait_recv()
                project(pq, very_first=False)
        pm, ps = order[-1]
        pq = chunk_of(chips[pm], ps)
        remote(w12_ref.at[pq, sib_rows, :], fwd(pm, ps), sib).wait_recv()
        project(pq, very_first=False)

        for m, chip in enumerate(chips):
            remote(sm4_ref.at[2 * chip[0] + chip[1]], sml(m), sib).wait_recv()
        for cp in sends + forwards:
            cp.wait_send()
        for slot in range(2):
            out_copy(0, slot, slot).wait()

    assert n_mm % 2 == 0 and n_mm >= 2
    vm = pl.BlockSpec(memory_space=pltpu.VMEM)
    hbm = pl.BlockSpec(memory_space=pl.ANY)
    n_sems = 2 * n_ici + 3
    return pl.pallas_call(
        body,
        out_shape=(jax.ShapeDtypeStruct((N_CHUNKS, t, CHUNK), F32), jax.ShapeDtypeStruct((t, D_MODEL), MXU_DTYPE),
                   jax.ShapeDtypeStruct((N_CHUNKS, w_in.shape[0], CHUNK), MXU_DTYPE),
                   jax.ShapeDtypeStruct((N_CHIPS,) + small.shape, F32)),
        in_specs=[hbm, vm, vm, vm], out_specs=(hbm, vm, vm, vm),
        scratch_shapes=[pltpu.VMEM((2, rb_x, D_MODEL), F32), pltpu.VMEM((2, rb_mm, CHUNK), F32),
                        pltpu.SemaphoreType.DMA((2,)), pltpu.SemaphoreType.DMA((2,)),
                        pltpu.SemaphoreType.DMA((n_sems,)), pltpu.SemaphoreType.DMA((n_sems,))],
        compiler_params=_cp(), name="gather_in_projection",
    )(x, ln_g, w_in, small)


def _allreduce_small(buf):
    def body(in_ref, out_ref, r0, r1, r2, send_sems, recv_sems):
        x, y, c, _ = _mesh_pos()
        peers = [(x, y, 1 - c), (1 - x, y, c), (x, 1 - y, c)]
        out_ref[...] = in_ref[...]
        for ph, (peer, rbuf) in enumerate(zip(peers, (r0, r1, r2))):
            cp = pltpu.make_async_remote_copy(src_ref=out_ref, dst_ref=rbuf, send_sem=send_sems.at[ph],
                                              recv_sem=recv_sems.at[ph], device_id=peer, device_id_type=MESH)
            cp.start()
            cp.wait()
            out_ref[...] = out_ref[...] + rbuf[...]

    vm = pl.BlockSpec(memory_space=pltpu.VMEM)
    return pl.pallas_call(
        body, out_shape=jax.ShapeDtypeStruct(buf.shape, F32), in_specs=[vm], out_specs=vm,
        scratch_shapes=[pltpu.VMEM(buf.shape, F32)] * 3 + [pltpu.SemaphoreType.DMA((3,)), pltpu.SemaphoreType.DMA((3,))],
        compiler_params=_cp(), name="allreduce_small",
    )(buf)


def _allreduce_behind(step, when, in_ref, acc_s, rbufs, out_ref, send_sems, recv_sems):
    x, y, c, _ = _mesh_pos()
    peers = [(x, y, 1 - c), (1 - x, y, c), (x, 1 - y, c)]

    def exchange(ph):
        return pltpu.make_async_remote_copy(src_ref=acc_s, dst_ref=rbufs[ph], send_sem=send_sems.at[ph],
                                            recv_sem=recv_sems.at[ph], device_id=peers[ph], device_id_type=MESH)

    @pl.when(step == when[0])
    def _():
        acc_s[...] = in_ref[...]
        exchange(0).start()

    for ph in (1, 2):
        @pl.when(step == when[ph])
        def _(ph=ph):
            exchange(ph - 1).wait()
            acc_s[...] = acc_s[...] + rbufs[ph - 1][...]
            exchange(ph).start()

    @pl.when(step == when[3])
    def _():
        exchange(2).wait()
        out_ref[...] = acc_s[...] + rbufs[2][...]


def _exchange_sibling_halves(g, name):
    n, rows, cols = g.shape
    half = rows // 2

    def body(g_ref, r_ref, send_sem, recv_sem):
        x, y, c, _ = _mesh_pos()
        cp = pltpu.make_async_remote_copy(src_ref=g_ref.at[:, pl.ds(pl.multiple_of(half * (1 - c), half), half), :],
                                          dst_ref=r_ref, send_sem=send_sem, recv_sem=recv_sem,
                                          device_id=(x, y, 1 - c), device_id_type=MESH)
        cp.start()
        cp.wait()

    hbm = pl.BlockSpec(memory_space=pl.ANY)
    return pl.pallas_call(
        body, out_shape=jax.ShapeDtypeStruct((n, half, cols), F32), in_specs=[hbm], out_specs=hbm,
        scratch_shapes=[pltpu.SemaphoreType.DMA, pltpu.SemaphoreType.DMA],
        compiler_params=_cp(), name=name,
    )(g)


def _add_own_half(g, r, c_arr, name):
    n, rr, cc = r.shape

    def body(c_ref, g_ref, r_ref, o_ref, ob_ref):
        s = g_ref[...] + r_ref[...]
        o_ref[...] = s
        ob_ref[...] = s.astype(jnp.bfloat16)

    blk = pl.BlockSpec((1, rr, cc), lambda q, c_ref: (q, 0, 0))
    return pl.pallas_call(
        body, out_shape=(jax.ShapeDtypeStruct(r.shape, F32), jax.ShapeDtypeStruct(r.shape, jnp.bfloat16)),
        grid_spec=pltpu.PrefetchScalarGridSpec(
            num_scalar_prefetch=1, grid=(n,),
            in_specs=[pl.BlockSpec((1, rr, cc), lambda q, c_ref: (q, c_ref[0], 0)), blk],
            out_specs=(blk, blk)),
        compiler_params=_cp(ARB), name=name,
    )(c_arr, g, r)


def _chip_block_copies(s_ref, r_ref, n_sub, send_sems, recv_sems):
    x, y, c, chips = _mesh_pos()
    cps = []
    for m, chip in enumerate(chips):
        kk = 2 * chip[0] + chip[1]
        cps.append(pltpu.make_async_remote_copy(
            src_ref=s_ref.at[pl.ds(n_sub * kk, n_sub)], dst_ref=r_ref.at[m],
            send_sem=send_sems.at[m], recv_sem=recv_sems.at[m], device_id=(*chip, c), device_id_type=MESH))
    return cps


def _gather_w_out(step, n_steps, wo_ref, wob_s, wo4_ref, local_sem, send_sems, recv_sems):
    x, y, c, chips = _mesh_pos()
    sib = (x, y, 1 - c)
    half = wo_ref.shape[0] // 2

    def rows(core):
        return pl.ds(pl.multiple_of(half * core, half), half)

    def block_half(chip, core):
        return wo4_ref.at[2 * chip[0] + chip[1], rows(core), :]

    def remote(src, dst, sem, to):
        return pltpu.make_async_remote_copy(src_ref=src, dst_ref=dst, send_sem=send_sems.at[sem], recv_sem=recv_sems.at[sem],
                                            device_id=to, device_id_type=MESH)

    local = pltpu.make_async_copy(wob_s, wo4_ref.at[2 * x + y], local_sem)
    ici = [remote(wob_s.at[rows(c), :], block_half((x, y), c), m, (*chip, c)) for m, chip in enumerate(chips)]
    fwd = [remote(block_half(chip, c), block_half(chip, c), 3 + m, sib) for m, chip in enumerate(chips)]

    @pl.when(step == 0)
    def _():
        wob_s[...] = wo_ref[...].astype(MXU_DTYPE)
        local.start()
        for cp in ici:
            cp.start()

    @pl.when(step == n_steps // 2)
    def _():
        for m, chip in enumerate(chips):
            remote(block_half(chip, c), block_half(chip, c), m, sib).wait_recv()
            fwd[m].start()

    @pl.when(step == n_steps - 1)
    def _():
        for m, chip in enumerate(chips):
            remote(block_half(chip, 1 - c), block_half(chip, 1 - c), 3 + m, sib).wait_recv()
        for cp in ici + fwd:
            cp.wait_send()
        local.wait()


def _chip_blocks_shape(s, n_sub):
    return jax.ShapeDtypeStruct((3, n_sub) + s.shape[1:], s.dtype)


def _sum_chip_blocks(s, r, kc_arr, n_sub, name):
    _, rr, cc = s.shape

    def body(kc_ref, s_ref, r_ref, o_ref):
        o_ref[...] = ((s_ref[...] + r_ref[0].astype(F32)) + r_ref[1].astype(F32)) + r_ref[2].astype(F32)

    return pl.pallas_call(
        body, out_shape=jax.ShapeDtypeStruct((n_sub, 2 * rr, cc), F32),
        grid_spec=pltpu.PrefetchScalarGridSpec(
            num_scalar_prefetch=1, grid=(n_sub,),
            in_specs=[pl.BlockSpec((1, rr, cc), lambda q, kc: (n_sub * kc[0] + q, 0, 0)),
                      pl.BlockSpec((3, 1, rr, cc), lambda q, kc: (0, q, 0, 0))],
            out_specs=pl.BlockSpec((1, rr, cc), lambda q, kc: (q, kc[1], 0))),
        compiler_params=_cp(ARB), name=name,
    )(kc_arr, s, r)


def _swap_sibling_halves(f_in, f_out):
    hi, ho = f_in.shape[1] // 2, f_out.shape[1] // 2

    def body(fi_in, fo_in, fi_ref, fo_ref, send_sems, recv_sems):
        del fi_in, fo_in
        x, y, c, _ = _mesh_pos()
        sib = (x, y, 1 - c)
        si = fi_ref.at[:, pl.ds(pl.multiple_of(hi * c, hi), hi), :]
        so = fo_ref.at[:, pl.ds(pl.multiple_of(ho * c, ho), ho), :]
        cps = [
            pltpu.make_async_remote_copy(src_ref=si, dst_ref=si, send_sem=send_sems.at[0], recv_sem=recv_sems.at[0],
                                         device_id=sib, device_id_type=MESH),
            pltpu.make_async_remote_copy(src_ref=so, dst_ref=so, send_sem=send_sems.at[1], recv_sem=recv_sems.at[1],
                                         device_id=sib, device_id_type=MESH),
        ]
        for cp in cps:
            cp.start()
        for cp in cps:
            cp.wait()

    hbm = pl.BlockSpec(memory_space=pl.ANY)
    return pl.pallas_call(
        body,
        out_shape=(jax.ShapeDtypeStruct(f_in.shape, F32), jax.ShapeDtypeStruct(f_out.shape, F32)),
        in_specs=[hbm, hbm], out_specs=(hbm, hbm), input_output_aliases={0: 0, 1: 1},
        scratch_shapes=[pltpu.SemaphoreType.DMA((2,)), pltpu.SemaphoreType.DMA((2,))],
        compiler_params=_cp(), name="swap_sibling_halves",
    )(f_in, f_out)


def _out_projection_loss(yc, yl, x, target, wo, final_g):
    t = x.shape[0]
    tm = 512

    def body(yc_ref, yl_ref, x_ref, t_ref, wo_ref, fg_ref, do_ref, dob_ref, dy_ref, st_ref, y_wo):
        @pl.when(pl.program_id(0) == 0)
        def _():
            st_ref[...] = jnp.zeros_like(st_ref)

        y_wo[...] = _mm(yc_ref[...], wo_ref[0:D_PART, :]) + _mm(yl_ref[...], wo_ref[D_PART:2 * D_PART, :])

        def norm_loss_slab(s, carry):
            g_sum, loss_sum = carry
            rows = pl.ds(pl.multiple_of(s * SLAB, SLAB), SLAB)
            o = x_ref[rows, :] + y_wo[rows, :]
            r2 = lax.rsqrt(jnp.mean(o * o, axis=-1, keepdims=True) + RMS_EPS)
            ohat = o * r2
            fg = fg_ref[...]
            diff = ohat * fg - t_ref[rows, :]
            dout = diff * (1.0 / D_MODEL)
            gp = dout * fg
            do = r2 * (gp - ohat * jnp.mean(gp * ohat, axis=-1, keepdims=True))
            do_ref[rows, :] = do
            dob_ref[rows, :] = do.astype(MXU_DTYPE)
            loss = 0.5 * jnp.sum(jnp.sum(diff * diff, axis=-1, keepdims=True) * (1.0 / D_MODEL), axis=0, keepdims=True)
            return g_sum + jnp.sum(dout * ohat, axis=0, keepdims=True), loss_sum + loss

        g_sum, loss_sum = lax.fori_loop(0, tm // SLAB, norm_loss_slab,
                                        (jnp.zeros((1, D_MODEL), F32), jnp.zeros((1, 1), F32)))
        st_ref[0:1, :] += g_sum
        st_ref[1:2, :] += jnp.broadcast_to(loss_sum, (1, D_MODEL))
        dy_ref[...] = _mm_nt(dob_ref[...], wo_ref[...])

    row = lambda i: (i, 0)
    fix = lambda i: (0, 0)
    return pl.pallas_call(
        body, grid=(t // tm,),
        in_specs=[pl.BlockSpec((tm, D_PART), row), pl.BlockSpec((tm, D_PART), row),
                  pl.BlockSpec((tm, D_MODEL), row), pl.BlockSpec((tm, D_MODEL), row),
                  pl.BlockSpec((2 * D_PART, D_MODEL), fix), pl.BlockSpec((1, D_MODEL), fix)],
        out_specs=(pl.BlockSpec((tm, D_MODEL), row), pl.BlockSpec((tm, D_MODEL), row),
                   pl.BlockSpec((tm, 2 * D_PART), row), pl.BlockSpec((SUBLANES, D_MODEL), fix)),
        out_shape=(jax.ShapeDtypeStruct((t, D_MODEL), F32), jax.ShapeDtypeStruct((t, D_MODEL), MXU_DTYPE),
                   jax.ShapeDtypeStruct((t, 2 * D_PART), F32), jax.ShapeDtypeStruct((SUBLANES, D_MODEL), F32)),
        scratch_shapes=[pltpu.VMEM((tm, D_MODEL), F32)],
        compiler_params=_cp(ARB), name="out_projection_loss",
    )(yc, yl, x, target, wo, final_g)


def _input_grad(dproj, w12, x, do, ln_g, sb_in):
    t = x.shape[0]
    tm = 1024

    def body(dp_ref, w_ref, x_ref, do_ref, g_ref, s_ref, gx_ref, st_ref, r_ref, acc, send_sems, recv_sems):
        i, p = pl.program_id(0), pl.program_id(1)

        @pl.when((i == 0) & (p == 0))
        def _():
            st_ref[...] = jnp.zeros_like(st_ref)
            for cp in _chip_block_copies(s_ref, r_ref, CHUNKS_PER_BLOCK, send_sems, recv_sems):
                cp.start()

        @pl.when((i == t // tm - 1) & (p == N_PARTS - 1))
        def _():
            for cp in _chip_block_copies(s_ref, r_ref, CHUNKS_PER_BLOCK, send_sems, recv_sems):
                cp.wait()

        part = _mm_nt(dp_ref[0, :, 0:CHUNK], w_ref[0]) + _mm_nt(dp_ref[0, :, CHUNK:2 * CHUNK], w_ref[1])

        @pl.when(p == 0)
        def _():
            acc[...] = part

        @pl.when(p > 0)
        def _():
            acc[...] += part

        @pl.when(p == N_PARTS - 1)
        def _():
            def norm_bwd_slab(s, g_sum):
                rows = pl.ds(pl.multiple_of(s * SLAB, SLAB), SLAB)
                xf = x_ref[rows, :]
                r = lax.rsqrt(jnp.mean(xf * xf, axis=-1, keepdims=True) + RMS_EPS)
                xhat = xf * r
                dxn = acc[rows, :]
                dxh = dxn * g_ref[...]
                gx_ref[rows, :] = do_ref[rows, :] + r * (dxh - xhat * jnp.mean(dxh * xhat, axis=-1, keepdims=True))
                return g_sum + jnp.sum(dxn * xhat, axis=0, keepdims=True)

            st_ref[0:1, :] += lax.fori_loop(0, tm // SLAB, norm_bwd_slab, jnp.zeros((1, D_MODEL), F32))

    row = lambda i, p: (i, 0)
    fix = lambda i, p: (0, 0)
    return pl.pallas_call(
        body, grid=(t // tm, N_PARTS),
        in_specs=[
            pl.BlockSpec((1, tm, D_PART), lambda i, p: (p, i, 0)),
            pl.BlockSpec((2, D_MODEL, CHUNK), lambda i, p: (p, 0, 0)),
            pl.BlockSpec((tm, D_MODEL), row), pl.BlockSpec((tm, D_MODEL), row), pl.BlockSpec((1, D_MODEL), fix),
            pl.BlockSpec(memory_space=pl.ANY)],
        out_specs=(pl.BlockSpec((tm, D_MODEL), row), pl.BlockSpec((SUBLANES, D_MODEL), fix),
                   pl.BlockSpec(memory_space=pl.ANY)),
        out_shape=(jax.ShapeDtypeStruct((t, D_MODEL), F32), jax.ShapeDtypeStruct((SUBLANES, D_MODEL), F32),
                   _chip_blocks_shape(sb_in, CHUNKS_PER_BLOCK)),
        scratch_shapes=[pltpu.VMEM((tm, D_MODEL), F32), pltpu.SemaphoreType.DMA((3,)), pltpu.SemaphoreType.DMA((3,))],
        compiler_params=_cp(ARB, ARB), name="input_grad",
    )(dproj, w12, x, do, ln_g, sb_in)


def _w_in_grad(xn, dproj, small):
    t = xn.shape[0]
    small_shape = pltpu.VMEM(small.shape, F32)

    def body(xn_ref, dp_ref, sm_ref, o_ref, red_ref, acc_s, r0, r1, r2, send_sems, recv_sems):
        _allreduce_behind(pl.program_id(0), (0, 1, 3, N_PARTS - 1), sm_ref, acc_s, (r0, r1, r2), red_ref, send_sems, recv_sems)
        xnv = xn_ref[...]
        for s in range(2):
            o_ref[s] = _mm_tn(xnv, dp_ref[0, :, CHUNK * s:CHUNK * (s + 1)])

    whole = pl.BlockSpec(small.shape, lambda p: (0, 0))
    return pl.pallas_call(
        body, grid=(N_PARTS,),
        in_specs=[pl.BlockSpec((t, D_MODEL), lambda p: (0, 0)),
                  pl.BlockSpec((1, t, D_PART), lambda p: (p, 0, 0)), whole],
        out_specs=(pl.BlockSpec((2, D_MODEL, CHUNK), lambda p: (p, 0, 0)), whole),
        out_shape=(jax.ShapeDtypeStruct((N_CHUNKS, D_MODEL, CHUNK), F32), jax.ShapeDtypeStruct(small.shape, F32)),
        scratch_shapes=[small_shape] * 4 + [pltpu.SemaphoreType.DMA((3,)), pltpu.SemaphoreType.DMA((3,))],
        compiler_params=_cp(ARB), name="w_in_grad",
    )(xn, dproj, small)


def _w_out_grad(yc, yl, dob):
    t = yc.shape[0]
    tk = 2048

    def body(yc_ref, yl_ref, do_ref, o_ref):
        j, kk = pl.program_id(0), pl.program_id(1)

        def accumulate(y_ref):
            part = _mm_tn(y_ref[...], do_ref[...])

            @pl.when(kk == 0)
            def _():
                o_ref[...] = part

            @pl.when(kk > 0)
            def _():
                o_ref[...] += part

        pl.when(j == 0)(functools.partial(accumulate, yc_ref))
        pl.when(j == 1)(functools.partial(accumulate, yl_ref))

    def rows_of(half):
        return lambda j, kk: (jnp.where(j == half, kk, 0), 0)

    out = pl.pallas_call(
        body, grid=(2, t // tk),
        in_specs=[pl.BlockSpec((tk, D_PART), rows_of(0)), pl.BlockSpec((tk, D_PART), rows_of(1)),
                  pl.BlockSpec((tk, D_MODEL), lambda j, kk: (kk, 0))],
        out_specs=pl.BlockSpec((D_PART, D_MODEL), lambda j, kk: (j, 0)),
        out_shape=jax.ShapeDtypeStruct((2 * D_PART, D_MODEL), F32),
        compiler_params=_cp(ARB, ARB), name="w_out_grad",
    )(yc, yl, dob)
    return out.reshape(N_CHIPS, 2 * D_PART // N_CHIPS, D_MODEL)


def _for_groups(n, fn, init, unroll=UNROLL):
    def trip(j, carry):
        for uu in range(unroll):
            carry = fn(j * unroll + uu, carry)
        return carry

    return lax.fori_loop(0, n // unroll, trip, init)


def _pvb(pv_ref, r):
    return jnp.broadcast_to(pv_ref[r:r + 1, :], (SUBLANES, pv_ref.shape[1]))


def _conv3(pv_ref, u, u1, u2):
    return (_pvb(pv_ref, PV_CONV_W) * u2 + _pvb(pv_ref, PV_CONV_W + 1) * u1) + _pvb(pv_ref, PV_CONV_W + 2) * u


def _conv4(pv_ref, v, v1, v2, v3):
    return ((((_pvb(pv_ref, PV_LRU_W) * v3 + _pvb(pv_ref, PV_LRU_W + 1) * v2) + _pvb(pv_ref, PV_LRU_W + 2) * v1)
             + _pvb(pv_ref, PV_LRU_W + 3) * v) + _pvb(pv_ref, PV_LRU_B))


def _mixer_forward(proj, pvec, wai, w_out):
    t = proj.shape[1]
    tb = 512
    ng = tb // SUBLANES
    nt = t // tb

    def body(bg_ref, cg_ref, xc_ref, gc_ref, xl_ref, gl_ref, pv_ref, wai_ref, wo_ref,
             yc_ref, yl_ref, h_ref, wo4_ref,
             ucp_s, xlp_s, ls_s, hbuf_s, u_s, gate_s, zc_s, zl_s, wob_s, local_sem, send_sems, recv_sems):
        _gather_w_out(pl.program_id(0) * nt + pl.program_id(1), NS * nt, wo_ref, wob_s, wo4_ref, local_sem, send_sems, recv_sems)

        @pl.when(pl.program_id(1) == 0)
        def _():
            ucp_s[...] = jnp.zeros_like(ucp_s)
            xlp_s[...] = jnp.zeros_like(xlp_s)
            hbuf_s[...] = jnp.zeros_like(hbuf_s)

        row = lax.broadcasted_iota(jnp.int32, (SUBLANES, LW), 0)
        ls_s[...] = _log_sigmoid(_pvb(pv_ref, PV_LAM))

        def conv_group(g, carry):
            ucp, xlp = carry
            sl = pl.ds(pl.multiple_of(g * SUBLANES, SUBLANES), SUBLANES)
            uc = cg_ref[sl, :] * xc_ref[sl, :]
            v = _conv3(pv_ref, uc, _shift_down(uc, ucp, 1, row), _shift_down(uc, ucp, 2, row))
            yc = bg_ref[sl, :] * v
            rr = lax.rsqrt(_head_mean(yc * yc, CONV_HEAD) + RMS_EPS)
            gc = gc_ref[sl, :]
            zc_s[sl, :] = ((yc * rr) * _pvb(pv_ref, PV_CG)) * (gc * _sigmoid(gc))
            xl = xl_ref[sl, :]
            u_s[sl, :] = _conv4(pv_ref, xl, _shift_down(xl, xlp, 1, row), _shift_down(xl, xlp, 2, row),
                                _shift_down(xl, xlp, 3, row))
            return uc, xl

        ucp, xlp = _for_groups(ng, conv_group, (ucp_s[...], xlp_s[...]), unroll=2 * UNROLL)
        ucp_s[...] = ucp
        xlp_s[...] = xlp

        gate_s[...] = _mm(u_s[...].astype(MXU_DTYPE), wai_ref[0])

        def lru_group(g, h_before):
            sl = pl.ds(pl.multiple_of(g * SUBLANES, SUBLANES), SUBLANES)
            u = u_s[sl, :]
            r, ig, a, e2, mult, _ = _gates(gate_s[sl, 0:LW] + _pvb(pv_ref, PV_BA),
                                           gate_s[sl, LW:2 * LW] + _pvb(pv_ref, PV_BI), u, ls_s[...])
            A, B = _scan8_fwd(a, mult * (ig * u), row)
            h = B + A * jnp.broadcast_to(h_before[SUBLANES - 1:SUBLANES, :], (SUBLANES, LW))
            h_ref[sl, :] = h
            rr = lax.rsqrt(_head_mean(h * h, LRU_HEAD) + RMS_EPS)
            gl = gl_ref[sl, :]
            zl_s[sl, :] = ((h * rr) * _pvb(pv_ref, PV_LG)) * (gl * _sigmoid(gl))
            return h

        hbuf_s[...] = _for_groups(ng, lru_group, hbuf_s[...], unroll=2 * UNROLL)
        yc_ref[...] = zc_s[...].astype(MXU_DTYPE)
        yl_ref[...] = zl_s[...].astype(MXU_DTYPE)

    def part(p):
        return pl.BlockSpec((None, tb, LW), lambda c, i: (2 * p + c // STRIPS_PER_CHUNK, i, c % STRIPS_PER_CHUNK))

    strip = pl.BlockSpec((tb, LW), lambda c, i: (i, c))
    return pl.pallas_call(
        body, grid=(NS, nt),
        in_specs=[part(p) for p in range(N_PARTS)] + [
            pl.BlockSpec((PV_ROWS, LW), lambda c, i: (0, c)),
            pl.BlockSpec((1, LW, 2 * LW), lambda c, i: (c, 0, 0)),
            pl.BlockSpec(w_out.shape, lambda c, i: (0, 0))],
        out_specs=(strip, strip, strip, pl.BlockSpec(memory_space=pl.ANY)),
        out_shape=(jax.ShapeDtypeStruct((t, D_PART), MXU_DTYPE), jax.ShapeDtypeStruct((t, D_PART), MXU_DTYPE),
                   jax.ShapeDtypeStruct((t, D_PART), F32), jax.ShapeDtypeStruct((N_CHIPS,) + w_out.shape, MXU_DTYPE)),
        scratch_shapes=[pltpu.VMEM((SUBLANES, LW), F32), pltpu.VMEM((SUBLANES, LW), F32), pltpu.VMEM((SUBLANES, LW), F32),
                        pltpu.VMEM((SUBLANES, LW), F32), pltpu.VMEM((tb, LW), F32), pltpu.VMEM((tb, 2 * LW), F32),
                        pltpu.VMEM((tb, LW), F32), pltpu.VMEM((tb, LW), F32), pltpu.VMEM(w_out.shape, MXU_DTYPE),
                        pltpu.SemaphoreType.DMA, pltpu.SemaphoreType.DMA((6,)), pltpu.SemaphoreType.DMA((6,))],
        compiler_params=_cp(ARB, ARB), name="mixer_forward",
    )(proj, proj, proj, proj, proj, proj, pvec, wai, w_out)


def _mixer_backward(proj, h, dy, pvec, wai, sb_out):
    t = proj.shape[1]
    tb = 512
    ng = tb // SUBLANES
    nt = t // tb
    gpb = tb // SUBLANES

    def body(bg_ref, cg_ref, xc_ref, gc_ref, xl_ref, gl_ref, h_ref, dyc_ref, dyl_ref,
             cgh_ref, xch_ref, xlh_ref, hh_ref, pv_ref, wai_ref, so_ref,
             dp_ref, gw_ref, sv_ref, ro_ref,
             ls_s, u_s, uce_s, xle_s, he_s, gate_s, dgate_s, du_s, gbuf_s,
             p0_s, p1_s, p2_s, p3_s, p4_s, p5_s, acc_s, an_s, dvn_s, dun_s, send_sems, recv_sems):
        i = pl.program_id(1)
        first_block = i == nt - 1

        @pl.when((pl.program_id(0) == 0) & (i == 0))
        def _():
            for cp in _chip_block_copies(so_ref, ro_ref, 1, send_sems, recv_sems):
                cp.start()

        @pl.when((pl.program_id(0) == NS - 1) & (i == nt - 1))
        def _():
            for cp in _chip_block_copies(so_ref, ro_ref, 1, send_sems, recv_sems):
                cp.wait()

        @pl.when(i == 0)
        def _():
            acc_s[...] = jnp.zeros_like(acc_s)
            gw_ref[...] = jnp.zeros_like(gw_ref)
            an_s[...] = jnp.zeros_like(an_s)
            dvn_s[...] = jnp.zeros_like(dvn_s)
            dun_s[...] = jnp.zeros_like(dun_s)
            gbuf_s[...] = jnp.zeros_like(gbuf_s)

        row = lax.broadcasted_iota(jnp.int32, (SUBLANES, LW), 0)
        ls_s[...] = _log_sigmoid(_pvb(pv_ref, PV_LAM))
        keep = jnp.where(first_block, 0.0, 1.0)
        uce_s[0:SUBLANES, :] = (cgh_ref[...] * xch_ref[...]) * keep
        xle_s[0:SUBLANES, :] = xlh_ref[...] * keep
        he_s[0:SUBLANES, :] = hh_ref[...] * keep
        xle_s[SUBLANES:SUBLANES + tb, :] = xl_ref[...]
        he_s[SUBLANES:SUBLANES + tb, :] = h_ref[...]

        def recompute_group(g, carry):
            r0 = pl.multiple_of(g * SUBLANES, SUBLANES)
            sl = pl.ds(r0, SUBLANES)
            uce_s[pl.ds(r0 + SUBLANES, SUBLANES), :] = cg_ref[sl, :] * xc_ref[sl, :]
            xl = xle_s[pl.ds(r0 + SUBLANES, SUBLANES), :]
            xlp = xle_s[sl, :]
            u_s[sl, :] = _conv4(pv_ref, xl, _shift_down(xl, xlp, 1, row), _shift_down(xl, xlp, 2, row),
                                _shift_down(xl, xlp, 3, row))
            return carry

        _for_groups(ng, recompute_group, 0)
        gate_s[...] = _mm(u_s[...].astype(MXU_DTYPE), wai_ref[0])

        def acc_add(k, v):
            acc_s[k] += v

        def main_group(gi, carry):
            a_next, dv_next, g_next = carry
            g = ng - 1 - gi
            r0 = pl.multiple_of(g * SUBLANES, SUBLANES)
            sl = pl.ds(r0, SUBLANES)
            sl_e = pl.ds(r0 + SUBLANES, SUBLANES)
            lsb = ls_s[...]
            u = u_s[sl, :]
            r, ig, a, e2, mult, inv_mult = _gates(gate_s[sl, 0:LW] + _pvb(pv_ref, PV_BA),
                                                  gate_s[sl, LW:2 * LW] + _pvb(pv_ref, PV_BI), u, lsb)
            gl = gl_ref[sl, :]
            sg = _sigmoid(gl)
            s_l = gl * sg
            h8 = he_s[sl_e, :]
            hprev = _shift_down(h8, he_s[sl, :], 1, row)
            rr = lax.rsqrt(_head_mean(h8 * h8, LRU_HEAD) + RMS_EPS)
            n = h8 * rr
            dz = dyl_ref[sl, :]
            lg = _pvb(pv_ref, PV_LG)
            acc_add(PV_LG, (dz * n) * s_l)
            p5_s[sl, :] = ((dz * n) * lg) * (sg * (1.0 + gl * (1.0 - sg)))
            dn = (dz * lg) * s_l
            dh = rr * (dn - n * _head_mean(dn * n, LRU_HEAD))
            A, B = _scan8_rev(_shift_up(a, a_next, 1, row), dh, row)
            gg = B + A * jnp.broadcast_to(g_next[0:1, :], (SUBLANES, LW))
            da = gg * hprev
            iu = ig * u
            diu = gg * mult
            dla = da * a - (gg * iu) * (e2 * inv_mult)
            acc_add(PV_LAM, dla * (RG_LRU_C * r))
            dra = (dla * (RG_LRU_C * lsb)) * (r * (1.0 - r))
            dia = (diu * u) * (ig * (1.0 - ig))
            dgate_s[sl, 0:LW] = dra
            dgate_s[sl, LW:2 * LW] = dia
            acc_add(PV_BA, dra)
            acc_add(PV_BI, dia)
            du_s[sl, :] = diu * ig
            bg = bg_ref[sl, :]
            gc = gc_ref[sl, :]
            uc = uce_s[sl_e, :]
            ucp = uce_s[sl, :]
            uc1 = _shift_down(uc, ucp, 1, row)
            uc2 = _shift_down(uc, ucp, 2, row)
            v = _conv3(pv_ref, uc, uc1, uc2)
            yc = bg * v
            rrc = lax.rsqrt(_head_mean(yc * yc, CONV_HEAD) + RMS_EPS)
            nc = yc * rrc
            sgc = _sigmoid(gc)
            s_c = gc * sgc
            dzc = dyc_ref[sl, :]
            cgain = _pvb(pv_ref, PV_CG)
            acc_add(PV_CG, (dzc * nc) * s_c)
            p3_s[sl, :] = ((dzc * nc) * cgain) * (sgc * (1.0 + gc * (1.0 - sgc)))
            dnc = (dzc * cgain) * s_c
            dyc = rrc * (dnc - nc * _head_mean(dnc * nc, CONV_HEAD))
            p0_s[sl, :] = dyc * v
            dv = dyc * bg
            duc = (_pvb(pv_ref, PV_CONV_W + 2) * dv + _pvb(pv_ref, PV_CONV_W + 1) * _shift_up(dv, dv_next, 1, row)
                   + _pvb(pv_ref, PV_CONV_W) * _shift_up(dv, dv_next, 2, row))
            acc_add(PV_CONV_W + 2, dv * uc)
            acc_add(PV_CONV_W + 1, dv * uc1)
            acc_add(PV_CONV_W, dv * uc2)
            p1_s[sl, :] = duc * xc_ref[sl, :]
            p2_s[sl, :] = duc * cg_ref[sl, :]
            return a, dv, gg

        a_next, dv_next, g_next = _for_groups(ng, main_group, (an_s[...], dvn_s[...], gbuf_s[...]))
        an_s[...] = a_next
        dvn_s[...] = dv_next
        gbuf_s[...] = g_next

        dgb = dgate_s[...].astype(MXU_DTYPE)
        du_s[...] += _mm_nt(dgb, wai_ref[0])
        gw_ref[0] += _mm_tn(u_s[...].astype(MXU_DTYPE), dgb)

        def lru_conv_group(gi, du_next):
            g = ng - 1 - gi
            r0 = pl.multiple_of(g * SUBLANES, SUBLANES)
            sl = pl.ds(r0, SUBLANES)
            du = du_s[sl, :]
            xl = xle_s[pl.ds(r0 + SUBLANES, SUBLANES), :]
            xlp = xle_s[sl, :]
            acc_add(PV_LRU_B, du)
            acc_add(PV_LRU_W + 3, du * xl)
            acc_add(PV_LRU_W + 2, du * _shift_down(xl, xlp, 1, row))
            acc_add(PV_LRU_W + 1, du * _shift_down(xl, xlp, 2, row))
            acc_add(PV_LRU_W, du * _shift_down(xl, xlp, 3, row))
            p4_s[sl, :] = (((_pvb(pv_ref, PV_LRU_W + 3) * du + _pvb(pv_ref, PV_LRU_W + 2) * _shift_up(du, du_next, 1, row))
                            + _pvb(pv_ref, PV_LRU_W + 1) * _shift_up(du, du_next, 2, row))
                           + _pvb(pv_ref, PV_LRU_W) * _shift_up(du, du_next, 3, row))
            return du

        dun_s[...] = _for_groups(ng, lru_conv_group, dun_s[...])

        for p, p_s in enumerate((p0_s, p1_s, p2_s, p3_s, p4_s, p5_s)):
            dp_ref[p] = p_s[...].astype(MXU_DTYPE)

        @pl.when(first_block)
        def _():
            sv_ref[...] = jnp.zeros_like(sv_ref)
            for k in range(N_ACC):
                tot = jnp.sum(acc_s[k], axis=0, keepdims=True)
                if k == PV_LAM:
                    tot = tot / (1.0 + jnp.exp(pv_ref[PV_LAM:PV_LAM + 1, :]))
                sv_ref[k:k + 1, :] = tot

    def part(p):
        return pl.BlockSpec((None, tb, LW), lambda c, i: (2 * p + c // STRIPS_PER_CHUNK, nt - 1 - i, c % STRIPS_PER_CHUNK))

    def halo(p):
        return pl.BlockSpec((None, SUBLANES, LW), lambda c, i: (2 * p + c // STRIPS_PER_CHUNK,
                                                                jnp.maximum((nt - 1 - i) * gpb - 1, 0), c % STRIPS_PER_CHUNK))

    strip = pl.BlockSpec((tb, LW), lambda c, i: (nt - 1 - i, c))
    big = pltpu.VMEM((tb, LW), F32)
    big_e = pltpu.VMEM((tb + SUBLANES, LW), F32)
    wide = pltpu.VMEM((tb, 2 * LW), F32)
    small = pltpu.VMEM((SUBLANES, LW), F32)
    outs = pl.pallas_call(
        body, grid=(NS, nt),
        in_specs=[part(p) for p in range(N_PARTS)] + [
            strip, strip, pl.BlockSpec((tb, LW), lambda c, i: (nt - 1 - i, NS + c)),
            halo(1), halo(2), halo(4),
            pl.BlockSpec((SUBLANES, LW), lambda c, i: (jnp.maximum((nt - 1 - i) * gpb - 1, 0), c)),
            pl.BlockSpec((PV_ROWS, LW), lambda c, i: (0, c)),
            pl.BlockSpec((1, LW, 2 * LW), lambda c, i: (c, 0, 0)),
            pl.BlockSpec(memory_space=pl.ANY)],
        out_specs=(pl.BlockSpec((N_PARTS, tb, LW), lambda c, i: (0, nt - 1 - i, c)),
                   pl.BlockSpec((1, LW, 2 * LW), lambda c, i: (c, 0, 0)),
                   pl.BlockSpec((PV_ROWS, LW), lambda c, i: (0, c)),
                   pl.BlockSpec(memory_space=pl.ANY)),
        out_shape=(jax.ShapeDtypeStruct((N_PARTS, t, D_PART), MXU_DTYPE),
                   jax.ShapeDtypeStruct((NS, LW, 2 * LW), F32), jax.ShapeDtypeStruct((PV_ROWS, D_PART), F32),
                   _chip_blocks_shape(sb_out, 1)),
        scratch_shapes=[small, big, big_e, big_e, big_e, wide, wide, big, small,
                        big, big, big, big, big, big, pltpu.VMEM((N_ACC, SUBLANES, LW), F32), small, small, small,
                        pltpu.SemaphoreType.DMA((3,)), pltpu.SemaphoreType.DMA((3,))],
        compiler_params=_cp(ARB, ARB), name="mixer_backward",
    )(proj, proj, proj, proj, proj, proj, h, dy, dy, proj, proj, proj, h, pvec, wai, sb_out)
    return outs


def _adamw(w, g, m, v):
    m = ADAM_B1 * m + (1.0 - ADAM_B1) * g
    v = ADAM_B2 * v + (1.0 - ADAM_B2) * (g * g)
    m_hat = m / (1.0 - ADAM_B1 ** ADAM_STEP)
    v_hat = v / (1.0 - ADAM_B2 ** ADAM_STEP)
    delta = -ADAM_LR * (m_hat / (jnp.sqrt(v_hat) + ADAM_EPS) + ADAM_WD * w)
    return delta, m, v


def _adam_w_in(w, m, v, g3):
    rows, cols = w.shape
    tr = 128

    def body(w_ref, m_ref, v_ref, g_ref, go_ref, d_ref, mo_ref, vo_ref):
        for s in range(CHUNKS_PER_BLOCK):
            cs = slice(CHUNK * s, CHUNK * (s + 1))
            g = g_ref[s]
            d, mn, vn = _adamw(w_ref[:, cs], g, m_ref[:, cs], v_ref[:, cs])
            go_ref[:, cs] = g
            d_ref[:, cs] = d
            mo_ref[:, cs] = mn
            vo_ref[:, cs] = vn

    blk = pl.BlockSpec((tr, cols), lambda i: (i, 0))
    return pl.pallas_call(
        body, grid=(rows // tr,),
        in_specs=[blk, blk, blk, pl.BlockSpec((CHUNKS_PER_BLOCK, tr, CHUNK), lambda i: (0, i, 0))],
        out_specs=(blk,) * 4, out_shape=(jax.ShapeDtypeStruct(w.shape, F32),) * 4,
        compiler_params=_cp(ARB), name="adam_w_in",
    )(w, m, v, g3)


def _adam_w_out(w, m, v, g):
    rows, cols = w.shape
    tr = 128

    def body(w_ref, m_ref, v_ref, g_ref, d_ref, mo_ref, vo_ref):
        d_ref[...], mo_ref[...], vo_ref[...] = _adamw(w_ref[...], g_ref[...], m_ref[...], v_ref[...])

    blk = pl.BlockSpec((tr, cols), lambda i: (i, 0))
    return pl.pallas_call(
        body, grid=(rows // tr,), in_specs=[blk] * 4, out_specs=(blk,) * 3,
        out_shape=(jax.ShapeDtypeStruct(w.shape, F32),) * 3,
        compiler_params=_cp(ARB), name="adam_w_out",
    )(w, m, v, g)


def _adam_small(ws, ms, vs, gs):
    n = len(ws)

    def body(*refs):
        w_r, m_r, v_r, g_r = refs[0:n], refs[n:2 * n], refs[2 * n:3 * n], refs[3 * n:4 * n]
        d_o, m_o, v_o = refs[4 * n:5 * n], refs[5 * n:6 * n], refs[6 * n:7 * n]
        for j in range(n):
            d_o[j][...], m_o[j][...], v_o[j][...] = _adamw(w_r[j][...], g_r[j][...], m_r[j][...], v_r[j][...])

    vm = pl.BlockSpec(memory_space=pltpu.VMEM)
    shapes = tuple(jax.ShapeDtypeStruct(w.shape, F32) for w in ws)
    outs = pl.pallas_call(
        body, in_specs=[vm] * (4 * n), out_specs=(vm,) * (3 * n), out_shape=shapes * 3,
        compiler_params=_cp(), name="adam_small",
    )(*ws, *ms, *vs, *gs)
    return outs[0:n], outs[n:2 * n], outs[2 * n:3 * n]


def _block_diag_strips(w):
    w4 = w.reshape(NS, HEADS_PER_STRIP, LRU_HEAD, LRU_HEAD)
    rows = [jnp.pad(w4[:, hh], ((0, 0), (0, 0), (LRU_HEAD * hh, LW - LRU_HEAD * (hh + 1)))) for hh in range(HEADS_PER_STRIP)]
    return jnp.concatenate(rows, axis=1)


def _strip_diag_blocks(g):
    g5 = g.reshape(NS, HEADS_PER_STRIP, LRU_HEAD, HEADS_PER_STRIP, LRU_HEAD)
    return jnp.stack([g5[:, hh, :, hh, :] for hh in range(HEADS_PER_STRIP)], axis=1).reshape(NS * HEADS_PER_STRIP, LRU_HEAD, LRU_HEAD)


def kernel(x, ln_g, w_in, conv_w, lru_conv_w, lru_conv_b, w_a, b_a, w_i, b_i, lam, conv_out_g, lru_out_g, w_out, final_g, loss_target, m_ln_g, m_w_in, m_conv_w, m_lru_conv_w, m_lru_conv_b, m_w_a, m_b_a, m_w_i, m_b_i, m_lam, m_conv_out_g, m_lru_out_g, m_w_out, m_final_g, v_ln_g, v_w_in, v_conv_w, v_lru_conv_w, v_lru_conv_b, v_w_a, v_b_a, v_w_i, v_b_i, v_lam, v_conv_out_g, v_lru_out_g, v_w_out, v_final_g):
    xi, yi, ci = lax.axis_index("x"), lax.axis_index("y"), lax.axis_index("c")
    k = 2 * xi + yi
    t = x.shape[1]
    x2 = x.reshape(t, D_MODEL)
    tgt2 = loss_target.reshape(t, D_MODEL)
    row = lambda a: a.reshape(1, -1)

    small = jnp.concatenate([conv_w, lru_conv_w, jnp.zeros((1, conv_w.shape[1]), F32)], axis=0)
    proj, xn, w12, sm4 = _gather_in_projection(x2, row(ln_g), w_in, small)
    convs = jnp.transpose(sm4, (1, 0, 2)).reshape(SUBLANES, D_PART)
    pvec = jnp.concatenate(
        [convs[0:7], row(lru_conv_b), row(b_a), row(b_i), row(lam), row(conv_out_g), row(lru_out_g),
         jnp.zeros((PV_ROWS - N_ACC, D_PART), F32)], axis=0)
    wai = jnp.concatenate([_block_diag_strips(w_a), _block_diag_strips(w_i)], axis=2).astype(MXU_DTYPE)

    c_arr = jnp.reshape(ci, (1,)).astype(jnp.int32)
    kc_arr = jnp.stack([k, ci]).astype(jnp.int32)
    yc, yl, h, wo4 = _mixer_forward(proj, pvec, wai, w_out)
    wo = wo4.reshape(2 * D_PART, D_MODEL)
    do, dob, dy, st_out = _out_projection_loss(yc, yl, x2, tgt2, wo, row(final_g))
    go4 = _w_out_grad(yc, yl, dob)
    s_out, sb_out = _add_own_half(go4, _exchange_sibling_halves(go4, "exchange_sibling_halves_out"), c_arr, "add_own_half_out")
    dproj, g_wai, svec, r2o = _mixer_backward(proj, h, dy, pvec, wai, sb_out)
    gwa = _strip_diag_blocks(g_wai[:, :, 0:LW]).reshape(LRU_HEAD, D_PART)
    gwi = _strip_diag_blocks(g_wai[:, :, LW:2 * LW]).reshape(LRU_HEAD, D_PART)
    g12, red = _w_in_grad(xn, dproj, jnp.concatenate([svec, st_out, gwa, gwi], axis=0))
    s_in, sb_in = _add_own_half(g12, _exchange_sibling_halves(g12, "exchange_sibling_halves_in"), c_arr, "add_own_half_in")
    grad_x, st_in, r2i = _input_grad(dproj, w12, x2, do, row(ln_g), sb_in)
    f_in = _sum_chip_blocks(s_in, r2i, kc_arr, CHUNKS_PER_BLOCK, "sum_chip_blocks_in")
    f_out = _sum_chip_blocks(s_out, r2o, kc_arr, 1, "sum_chip_blocks_out")
    f_in, f_out = _swap_sibling_halves(f_in, f_out)

    red_ln = _allreduce_small(st_in)
    r_out = PV_ROWS
    r_wa = PV_ROWS + SUBLANES
    r_wi = r_wa + LRU_HEAD
    loss = red[r_out + 1, 0]

    g_w_in, d_w_in, nm_w_in, nv_w_in = _adam_w_in(w_in, m_w_in, v_w_in, f_in)
    g_w_out = f_out[0]
    d_w_out, nm_w_out, nv_w_out = _adam_w_out(w_out, m_w_out, v_w_out, g_w_out)

    ncol = conv_w.shape[1]
    conv_cols = lax.dynamic_slice(red, (0, k * ncol), (SUBLANES, ncol))
    g_small = {
        "ln_g": red_ln[0], "conv_w": conv_cols[0:3], "lru_conv_w": conv_cols[3:7], "lru_conv_b": red[PV_LRU_B],
        "w_a": red[r_wa:r_wa + LRU_HEAD].reshape(w_a.shape), "b_a": red[PV_BA],
        "w_i": red[r_wi:r_wi + LRU_HEAD].reshape(w_i.shape), "b_i": red[PV_BI], "lam": red[PV_LAM],
        "conv_out_g": red[PV_CG], "lru_out_g": red[PV_LG], "final_g": red[r_out],
    }
    w_small = {"ln_g": ln_g, "conv_w": conv_w, "lru_conv_w": lru_conv_w, "lru_conv_b": lru_conv_b, "w_a": w_a, "b_a": b_a,
               "w_i": w_i, "b_i": b_i, "lam": lam, "conv_out_g": conv_out_g, "lru_out_g": lru_out_g, "final_g": final_g}
    m_small = {"ln_g": m_ln_g, "conv_w": m_conv_w, "lru_conv_w": m_lru_conv_w, "lru_conv_b": m_lru_conv_b, "w_a": m_w_a,
               "b_a": m_b_a, "w_i": m_w_i, "b_i": m_b_i, "lam": m_lam, "conv_out_g": m_conv_out_g,
               "lru_out_g": m_lru_out_g, "final_g": m_final_g}
    v_small = {"ln_g": v_ln_g, "conv_w": v_conv_w, "lru_conv_w": v_lru_conv_w, "lru_conv_b": v_lru_conv_b, "w_a": v_w_a,
               "b_a": v_b_a, "w_i": v_w_i, "b_i": v_b_i, "lam": v_lam, "conv_out_g": v_conv_out_g,
               "lru_out_g": v_lru_out_g, "final_g": v_final_g}
    names = list(w_small)
    as2d = lambda a: a.reshape(1, -1) if a.ndim == 1 else a
    d_s, m_s, v_s = _adam_small([as2d(w_small[n]) for n in names], [as2d(m_small[n]) for n in names],
                                [as2d(v_small[n]) for n in names], [as2d(g_small[n]) for n in names])
    back = lambda n, a: a.reshape(w_small[n].shape)
    grads = {n: g_small[n] for n in names}
    deltas = {n: back(n, a) for n, a in zip(names, d_s)}
    new_m = {n: back(n, a) for n, a in zip(names, m_s)}
    new_v = {n: back(n, a) for n, a in zip(names, v_s)}
    grads["w_in"], deltas["w_in"], new_m["w_in"], new_v["w_in"] = g_w_in, d_w_in, nm_w_in, nv_w_in
    grads["w_out"], deltas["w_out"], new_m["w_out"], new_v["w_out"] = g_w_out, d_w_out, nm_w_out, nv_w_out

    order = ["ln_g", "w_in", "conv_w", "lru_conv_w", "lru_conv_b", "w_a", "b_a", "w_i", "b_i", "lam", "conv_out_g",
             "lru_out_g", "w_out", "final_g"]
    return (loss, grad_x.reshape(x.shape), *[grads[n] for n in order], *[deltas[n] for n in order],
            *[new_m[n] for n in order], *[new_v[n] for n in order])
```

```python
import functools

import jax
import jax.numpy as jnp
from jax import lax
from jax.experimental import pallas as pl
from jax.experimental.pallas import tpu as pltpu

F32 = jnp.float32
MXU_DTYPE = jnp.bfloat16

D_MODEL = 1024
D_PART = 1024
N_PARTS = 6
CHUNK = 512
CHUNKS_PER_BLOCK = 3
N_CHUNKS = 12
N_CHIPS = 4
SUBLANES = 8
LANES = 128
LW = 256
UNROLL = 8
NS = D_PART // LW
STRIPS_PER_CHUNK = CHUNK // LW
CONV_HEAD = 128
LRU_HEAD = 64
HEADS_PER_STRIP = LW // LRU_HEAD
RMS_EPS = 1e-6
RG_LRU_C = 8.0
ADAM_LR = 0.001
ADAM_B1 = 0.9
ADAM_B2 = 0.999
ADAM_EPS = 1e-08
ADAM_WD = 0.01
ADAM_STEP = 10

PV_CONV_W = 0
PV_LRU_W = 3
PV_LRU_B = 7
PV_BA = 8
PV_BI = 9
PV_LAM = 10
PV_CG = 11
PV_LG = 12
PV_ROWS = 16
N_ACC = 13

SLAB = 128
MESH = pl.DeviceIdType.MESH
VMEM_LIMIT = 56 * 1024 * 1024
ARB = "arbitrary"


def _cp(*sem, **kw):
    return pltpu.CompilerParams(dimension_semantics=sem or None, vmem_limit_bytes=VMEM_LIMIT, **kw)


def _mm(a, b):
    return jnp.dot(a, b, preferred_element_type=F32)


def _mm_nt(a, b):
    return lax.dot_general(a, b, (((1,), (1,)), ((), ())), preferred_element_type=F32)


def _mm_tn(a, b):
    return lax.dot_general(a, b, (((0,), (0,)), ((), ())), preferred_element_type=F32)


def _sigmoid(x):
    return 0.5 * jnp.tanh(0.5 * x) + 0.5


def _log_sigmoid(x):
    z = jnp.exp(-jnp.abs(x))
    u = 1.0 + z
    log1p = jnp.where(u == 1.0, z, jnp.log(u) * z / (u - 1.0))
    return jnp.minimum(x, 0.0) - log1p


def _head_mean(z, head):
    out = []
    for k in range(z.shape[1] // LANES):
        zk = z[:, LANES * k:LANES * (k + 1)]
        if head == LANES:
            m = jnp.sum(zk, axis=-1, keepdims=True) * (1.0 / head)
            out.append(jnp.broadcast_to(m, zk.shape))
        else:
            lo = lax.broadcasted_iota(jnp.int32, zk.shape, 1) < head
            s_lo = jnp.sum(jnp.where(lo, zk, 0.0), axis=-1, keepdims=True)
            s_hi = jnp.sum(jnp.where(lo, 0.0, zk), axis=-1, keepdims=True)
            out.append(jnp.where(lo, s_lo, s_hi) * (1.0 / head))
    return jnp.concatenate(out, axis=1)


def _shift_down(cur, prev, d, row):
    return pltpu.roll(jnp.where(row < SUBLANES - d, cur, prev), d, 0)


def _shift_up(cur, nxt, d, row):
    return pltpu.roll(jnp.where(row >= d, cur, nxt), SUBLANES - d, 0)


def _scan8_fwd(a, b, row):
    A, B = a, b
    for d in (1, 2, 4):
        m = row >= d
        a_s = jnp.where(m, pltpu.roll(A, d, 0), 1.0)
        b_s = jnp.where(m, pltpu.roll(B, d, 0), 0.0)
        B = A * b_s + B
        A = A * a_s
    return A, B


def _scan8_rev(a, b, row):
    A, B = a, b
    for d in (1, 2, 4):
        m = row < SUBLANES - d
        a_s = jnp.where(m, pltpu.roll(A, SUBLANES - d, 0), 1.0)
        b_s = jnp.where(m, pltpu.roll(B, SUBLANES - d, 0), 0.0)
        B = A * b_s + B
        A = A * a_s
    return A, B


def _gates(ra, ia, u, lsb):
    r = _sigmoid(ra)
    ig = _sigmoid(ia)
    la = (RG_LRU_C * r) * lsb
    a = jnp.exp(la)
    e2 = a * a
    em = -jnp.tanh(la) * (1.0 + e2)
    inv_mult = lax.rsqrt(em)
    return r, ig, a, e2, em * inv_mult, inv_mult


def _mesh_pos():
    x, y, c = lax.axis_index("x"), lax.axis_index("y"), lax.axis_index("c")
    chips = [(1 - x, y), (x, 1 - y), (1 - x, 1 - y)]
    return x, y, c, chips


def _gather_in_projection(x, ln_g, w_in, small):
    t = x.shape[0]
    rb_x = 512
    rb_mm = 1024
    n_mm = t // rb_mm
    half = w_in.shape[0] // 2
    n_ici = 3 * CHUNKS_PER_BLOCK

    def body(x_hbm, g_ref, wi_ref, sm_ref, proj_hbm, xn_ref, w12_ref, sm4_ref,
             xbuf, obuf, x_sems, o_sems, send_sems, recv_sems):
        x_, y_, c, chips = _mesh_pos()
        k = 2 * x_ + y_
        sib = (x_, y_, 1 - c)
        my_rows = pl.ds(pl.multiple_of(half * c, half), half)
        sib_rows = pl.ds(pl.multiple_of(half * (1 - c), half), half)

        sm4_ref[k] = sm_ref[...]

        def remote(ref, sem, to):
            return pltpu.make_async_remote_copy(src_ref=ref, dst_ref=ref, send_sem=send_sems.at[sem],
                                                recv_sem=recv_sems.at[sem], device_id=to, device_id_type=MESH)

        def chunk_of(chip, s):
            return CHUNKS_PER_BLOCK * (2 * chip[0] + chip[1]) + s

        ici = lambda m, s: 3 * s + m
        fwd = lambda m, s: n_ici + 3 * s + m
        sml = lambda m: 2 * n_ici + m

        sends = []
        for s in range(CHUNKS_PER_BLOCK):
            w12_ref[chunk_of((x_, y_), s)] = wi_ref[:, CHUNK * s:CHUNK * (s + 1)].astype(MXU_DTYPE)
            for m, chip in enumerate(chips):
                sends.append(remote(w12_ref.at[chunk_of((x_, y_), s), my_rows, :], ici(m, s), (*chip, c)))
                sends[-1].start()
        for m, chip in enumerate(chips):
            sends.append(remote(sm4_ref.at[k], sml(m), (*chip, c)))
            sends[-1].start()

        def x_copy(rb, slot):
            return pltpu.make_async_copy(x_hbm.at[pl.ds(rb * rb_x, rb_x), :], xbuf.at[slot], x_sems.at[slot])

        x_copy(0, 0).start()
        for rb in range(t // rb_x):
            slot = rb % 2
            x_copy(rb, slot).wait()
            if rb + 1 < t // rb_x:
                x_copy(rb + 1, 1 - slot).start()

            def norm_slab(sl, carry, rb=rb, slot=slot):
                xf = xbuf[slot, pl.ds(pl.multiple_of(sl * SLAB, SLAB), SLAB), :]
                r = lax.rsqrt(jnp.mean(xf * xf, axis=-1, keepdims=True) + RMS_EPS)
                xn_ref[pl.ds(pl.multiple_of(rb * rb_x + sl * SLAB, SLAB), SLAB), :] = ((xf * r) * g_ref[...]).astype(MXU_DTYPE)
                return carry

            lax.fori_loop(0, rb_x // SLAB, norm_slab, 0)

        def out_copy(q, i, slot):
            return pltpu.make_async_copy(obuf.at[slot], proj_hbm.at[q, pl.ds(pl.multiple_of(i * rb_mm, rb_mm), rb_mm), :],
                                         o_sems.at[slot])

        def project(q, very_first):
            def row_block(i, carry):
                slot = i % 2

                def wait_buffer():
                    out_copy(q, i, slot).wait()

                if very_first:
                    pl.when(i >= 2)(wait_buffer)
                else:
                    wait_buffer()
                obuf[slot] = _mm(xn_ref[pl.ds(pl.multiple_of(i * rb_mm, rb_mm), rb_mm), :], w12_ref[q])
                out_copy(q, i, slot).start()
                return carry

            lax.fori_loop(0, n_mm, row_block, 0)

        for s in range(CHUNKS_PER_BLOCK):
            project(chunk_of((x_, y_), s), very_first=(s == 0))

        order = [(m, s) for s in range(CHUNKS_PER_BLOCK) for m in range(3)]
        forwards = []
        for j, (m, s) in enumerate(order):
            q = chunk_of(chips[m], s)
            remote(w12_ref.at[q, my_rows, :], ici(m, s), sib).wait_recv()
            f = remote(w12_ref.at[q, my_rows, :], fwd(m, s), sib)
            f.start()
            forwards.append(f)
            if j > 0:
                pm, ps = order[j - 1]
                pq = chunk_of(chips[pm], ps)
                remote(w12_ref.at[pq, sib_rows, :], fwd(pm, ps), sib).wait_recv()
                project(pq, very_first=False)
        pm, ps = order[-1]
        pq = chunk_of(chips[pm], ps)
        remote(w12_ref.at[pq, sib_rows, :], fwd(pm, ps), sib).wait_recv()
        project(pq, very_first=False)

        for m, chip in enumerate(chips):
            remote(sm4_ref.at[2 * chip[0] + chip[1]], sml(m), sib).wait_recv()
        for cp in sends + forwards:
            cp.wait_send()
        for slot in range(2):
            out_copy(0, slot, slot).wait()

    assert n_mm % 2 == 0 and n_mm >= 2
    vm = pl.BlockSpec(memory_space=pltpu.VMEM)
    hbm = pl.BlockSpec(memory_space=pl.ANY)
    n_sems = 2 * n_ici + 3
    return pl.pallas_call(
        body,
        out_shape=(jax.ShapeDtypeStruct((N_CHUNKS, t, CHUNK), F32), jax.ShapeDtypeStruct((t, D_MODEL), MXU_DTYPE),
                   jax.ShapeDtypeStruct((N_CHUNKS, w_in.shape[0], CHUNK), MXU_DTYPE),
                   jax.ShapeDtypeStruct((N_CHIPS,) + small.shape, F32)),
        in_specs=[hbm, vm, vm, vm], out_specs=(hbm, vm, vm, vm),
        scratch_shapes=[pltpu.VMEM((2, rb_x, D_MODEL), F32), pltpu.VMEM((2, rb_mm, CHUNK), F32),
                        pltpu.SemaphoreType.DMA((2,)), pltpu.SemaphoreType.DMA((2,)),
                        pltpu.SemaphoreType.DMA((n_sems,)), pltpu.SemaphoreType.DMA((n_sems,))],
        compiler_params=_cp(), name="gather_in_projection",
    )(x, ln_g, w_in, small)


def _sum_over_devices(v):
    n_dev = 8

    def body(v_ref, o_ref, slots, send_sems, recv_sems):
        x, y, c, _ = _mesh_pos()
        me = 4 * x + 2 * y + c
        slots[me] = v_ref[...]
        cps = []
        for d in range(1, n_dev):
            peer = (1 - x if d & 4 else x, 1 - y if d & 2 else y, 1 - c if d & 1 else c)
            cps.append(pltpu.make_async_remote_copy(src_ref=slots.at[me], dst_ref=slots.at[me], send_sem=send_sems.at[d - 1],
                                                    recv_sem=recv_sems.at[d - 1], device_id=peer, device_id_type=MESH))
        for cp in cps:
            cp.start()
        for cp in cps:
            cp.wait()
        total = slots[0]
        for dev in range(1, n_dev):
            total = total + slots[dev]
        o_ref[...] = total

    vm = pl.BlockSpec(memory_space=pltpu.VMEM)
    return pl.pallas_call(
        body, out_shape=jax.ShapeDtypeStruct(v.shape, F32), in_specs=[vm], out_specs=vm,
        scratch_shapes=[pltpu.VMEM((n_dev,) + v.shape, F32), pltpu.SemaphoreType.DMA((n_dev - 1,)),
                        pltpu.SemaphoreType.DMA((n_dev - 1,))],
        compiler_params=_cp(), name="sum_over_devices",
    )(v)


def _allreduce_behind(step, when, in_ref, acc_s, rbufs, out_ref, send_sems, recv_sems):
    x, y, c, _ = _mesh_pos()
    peers = [(x, y, 1 - c), (1 - x, y, c), (x, 1 - y, c)]

    def exchange(ph):
        return pltpu.make_async_remote_copy(src_ref=acc_s, dst_ref=rbufs[ph], send_sem=send_sems.at[ph],
                                            recv_sem=recv_sems.at[ph], device_id=peers[ph], device_id_type=MESH)

    @pl.when(step == when[0])
    def _():
        acc_s[...] = in_ref[...]
        exchange(0).start()

    for ph in (1, 2):
        @pl.when(step == when[ph])
        def _(ph=ph):
            exchange(ph - 1).wait()
            acc_s[...] = acc_s[...] + rbufs[ph - 1][...]
            exchange(ph).start()

    @pl.when(step == when[3])
    def _():
        exchange(2).wait()
        out_ref[...] = acc_s[...] + rbufs[2][...]


def _exchange_sibling_halves(g, name):
    n, rows, cols = g.shape
    half = rows // 2

    def body(g_ref, r_ref, send_sem, recv_sem):
        x, y, c, _ = _mesh_pos()
        cp = pltpu.make_async_remote_copy(src_ref=g_ref.at[:, pl.ds(pl.multiple_of(half * (1 - c), half), half), :],
                                          dst_ref=r_ref, send_sem=send_sem, recv_sem=recv_sem,
                                          device_id=(x, y, 1 - c), device_id_type=MESH)
        cp.start()
        cp.wait()

    hbm = pl.BlockSpec(memory_space=pl.ANY)
    return pl.pallas_call(
        body, out_shape=jax.ShapeDtypeStruct((n, half, cols), g.dtype), in_specs=[hbm], out_specs=hbm,
        scratch_shapes=[pltpu.SemaphoreType.DMA, pltpu.SemaphoreType.DMA],
        compiler_params=_cp(), name=name,
    )(g)


def _add_own_half(g, r, c_arr, name):
    n, rr, cc = r.shape

    def body(c_ref, g_ref, r_ref, o_ref, ob_ref):
        s = g_ref[...] + r_ref[...].astype(F32)
        o_ref[...] = s
        ob_ref[...] = s.astype(jnp.bfloat16)

    blk = pl.BlockSpec((1, rr, cc), lambda q, c_ref: (q, 0, 0))
    return pl.pallas_call(
        body, out_shape=(jax.ShapeDtypeStruct(r.shape, F32), jax.ShapeDtypeStruct(r.shape, jnp.bfloat16)),
        grid_spec=pltpu.PrefetchScalarGridSpec(
            num_scalar_prefetch=1, grid=(n,),
            in_specs=[pl.BlockSpec((1, rr, cc), lambda q, c_ref: (q, c_ref[0], 0)), blk],
            out_specs=(blk, blk)),
        compiler_params=_cp(ARB), name=name,
    )(c_arr, g, r)


def _chip_block_copies(s_ref, r_ref, n_sub, send_sems, recv_sems):
    x, y, c, chips = _mesh_pos()
    cps = []
    for m, chip in enumerate(chips):
        kk = 2 * chip[0] + chip[1]
        cps.append(pltpu.make_async_remote_copy(
            src_ref=s_ref.at[pl.ds(n_sub * kk, n_sub)], dst_ref=r_ref.at[m],
            send_sem=send_sems.at[m], recv_sem=recv_sems.at[m], device_id=(*chip, c), device_id_type=MESH))
    return cps


def _gather_w_out(step, n_steps, wo_ref, wob_s, wo4_ref, local_sem, send_sems, recv_sems):
    x, y, c, chips = _mesh_pos()
    sib = (x, y, 1 - c)
    half = wo_ref.shape[0] // 2

    def rows(core):
        return pl.ds(pl.multiple_of(half * core, half), half)

    def block_half(chip, core):
        return wo4_ref.at[2 * chip[0] + chip[1], rows(core), :]

    def remote(src, dst, sem, to):
        return pltpu.make_async_remote_copy(src_ref=src, dst_ref=dst, send_sem=send_sems.at[sem], recv_sem=recv_sems.at[sem],
                                            device_id=to, device_id_type=MESH)

    local = pltpu.make_async_copy(wob_s, wo4_ref.at[2 * x + y], local_sem)
    ici = [remote(wob_s.at[rows(c), :], block_half((x, y), c), m, (*chip, c)) for m, chip in enumerate(chips)]
    fwd = [remote(block_half(chip, c), block_half(chip, c), 3 + m, sib) for m, chip in enumerate(chips)]

    @pl.when(step == 0)
    def _():
        wob_s[...] = wo_ref[...].astype(MXU_DTYPE)
        local.start()
        for cp in ici:
            cp.start()

    @pl.when(step == n_steps // 2)
    def _():
        for m, chip in enumerate(chips):
            remote(block_half(chip, c), block_half(chip, c), m, sib).wait_recv()
            fwd[m].start()

    @pl.when(step == n_steps - 1)
    def _():
        for m, chip in enumerate(chips):
            remote(block_half(chip, 1 - c), block_half(chip, 1 - c), 3 + m, sib).wait_recv()
        for cp in ici + fwd:
            cp.wait_send()
        local.wait()


def _chip_blocks_shape(s, n_sub):
    return jax.ShapeDtypeStruct((3, n_sub) + s.shape[1:], s.dtype)


def _sum_chip_blocks(s, r, kc_arr, n_sub, name):
    _, rr, cc = s.shape

    def body(kc_ref, s_ref, r_ref, o_ref):
        o_ref[...] = ((s_ref[...] + r_ref[0].astype(F32)) + r_ref[1].astype(F32)) + r_ref[2].astype(F32)

    return pl.pallas_call(
        body, out_shape=jax.ShapeDtypeStruct((n_sub, 2 * rr, cc), F32),
        grid_spec=pltpu.PrefetchScalarGridSpec(
            num_scalar_prefetch=1, grid=(n_sub,),
            in_specs=[pl.BlockSpec((1, rr, cc), lambda q, kc: (n_sub * kc[0] + q, 0, 0)),
                      pl.BlockSpec((3, 1, rr, cc), lambda q, kc: (0, q, 0, 0))],
            out_specs=pl.BlockSpec((1, rr, cc), lambda q, kc: (q, kc[1], 0))),
        compiler_params=_cp(ARB), name=name,
    )(kc_arr, s, r)


def _swap_sibling_halves(f_in, f_out):
    hi, ho = f_in.shape[1] // 2, f_out.shape[1] // 2

    def body(fi_in, fo_in, fi_ref, fo_ref, send_sems, recv_sems):
        del fi_in, fo_in
        x, y, c, _ = _mesh_pos()
        sib = (x, y, 1 - c)
        si = fi_ref.at[:, pl.ds(pl.multiple_of(hi * c, hi), hi), :]
        so = fo_ref.at[:, pl.ds(pl.multiple_of(ho * c, ho), ho), :]
        cps = [
            pltpu.make_async_remote_copy(src_ref=si, dst_ref=si, send_sem=send_sems.at[0], recv_sem=recv_sems.at[0],
                                         device_id=sib, device_id_type=MESH),
            pltpu.make_async_remote_copy(src_ref=so, dst_ref=so, send_sem=send_sems.at[1], recv_sem=recv_sems.at[1],
                                         device_id=sib, device_id_type=MESH),
        ]
        for cp in cps:
            cp.start()
        for cp in cps:
            cp.wait()

    hbm = pl.BlockSpec(memory_space=pl.ANY)
    return pl.pallas_call(
        body,
        out_shape=(jax.ShapeDtypeStruct(f_in.shape, F32), jax.ShapeDtypeStruct(f_out.shape, F32)),
        in_specs=[hbm, hbm], out_specs=(hbm, hbm), input_output_aliases={0: 0, 1: 1},
        scratch_shapes=[pltpu.SemaphoreType.DMA((2,)), pltpu.SemaphoreType.DMA((2,))],
        compiler_params=_cp(), name="swap_sibling_halves",
    )(f_in, f_out)


def _out_projection_loss(yc, yl, x, target, wo, final_g):
    t = x.shape[0]
    tm = 512

    def body(yc_ref, yl_ref, x_ref, t_ref, wo_ref, fg_ref, do_ref, dob_ref, dy_ref, st_ref, y_wo):
        @pl.when(pl.program_id(0) == 0)
        def _():
            st_ref[...] = jnp.zeros_like(st_ref)

        y_wo[...] = _mm(yc_ref[...], wo_ref[0:D_PART, :]) + _mm(yl_ref[...], wo_ref[D_PART:2 * D_PART, :])

        def norm_loss_slab(s, carry):
            g_sum, loss_sum = carry
            rows = pl.ds(pl.multiple_of(s * SLAB, SLAB), SLAB)
            o = x_ref[rows, :] + y_wo[rows, :]
            r2 = lax.rsqrt(jnp.mean(o * o, axis=-1, keepdims=True) + RMS_EPS)
            ohat = o * r2
            fg = fg_ref[...]
            diff = ohat * fg - t_ref[rows, :]
            dout = diff * (1.0 / D_MODEL)
            gp = dout * fg
            do = r2 * (gp - ohat * jnp.mean(gp * ohat, axis=-1, keepdims=True))
            do_ref[rows, :] = do
            dob_ref[rows, :] = do.astype(MXU_DTYPE)
            loss = 0.5 * jnp.sum(jnp.sum(diff * diff, axis=-1, keepdims=True) * (1.0 / D_MODEL), axis=0, keepdims=True)
            return g_sum + jnp.sum(dout * ohat, axis=0, keepdims=True), loss_sum + loss

        g_sum, loss_sum = lax.fori_loop(0, tm // SLAB, norm_loss_slab,
                                        (jnp.zeros((1, D_MODEL), F32), jnp.zeros((1, 1), F32)))
        st_ref[0:1, :] += g_sum
        st_ref[1:2, :] += jnp.broadcast_to(loss_sum, (1, D_MODEL))
        dy_ref[...] = _mm_nt(dob_ref[...], wo_ref[...])

    row = lambda i: (i, 0)
    fix = lambda i: (0, 0)
    return pl.pallas_call(
        body, grid=(t // tm,),
        in_specs=[pl.BlockSpec((tm, D_PART), row), pl.BlockSpec((tm, D_PART), row),
                  pl.BlockSpec((tm, D_MODEL), row), pl.BlockSpec((tm, D_MODEL), row),
                  pl.BlockSpec((2 * D_PART, D_MODEL), fix), pl.BlockSpec((1, D_MODEL), fix)],
        out_specs=(pl.BlockSpec((tm, D_MODEL), row), pl.BlockSpec((tm, D_MODEL), row),
                   pl.BlockSpec((tm, 2 * D_PART), row), pl.BlockSpec((SUBLANES, D_MODEL), fix)),
        out_shape=(jax.ShapeDtypeStruct((t, D_MODEL), F32), jax.ShapeDtypeStruct((t, D_MODEL), MXU_DTYPE),
                   jax.ShapeDtypeStruct((t, 2 * D_PART), F32), jax.ShapeDtypeStruct((SUBLANES, D_MODEL), F32)),
        scratch_shapes=[pltpu.VMEM((tm, D_MODEL), F32)],
        compiler_params=_cp(ARB), name="out_projection_loss",
    )(yc, yl, x, target, wo, final_g)


def _input_grad(dproj, w12, x, do, ln_g, sb_in):
    t = x.shape[0]
    tm = 1024

    def body(dp_ref, w_ref, x_ref, do_ref, g_ref, s_ref, gx_ref, st_ref, r_ref, acc, send_sems, recv_sems):
        i, p = pl.program_id(0), pl.program_id(1)

        @pl.when((i == 0) & (p == 0))
        def _():
            st_ref[...] = jnp.zeros_like(st_ref)
            for cp in _chip_block_copies(s_ref, r_ref, CHUNKS_PER_BLOCK, send_sems, recv_sems):
                cp.start()

        @pl.when((i == t // tm - 1) & (p == N_PARTS - 1))
        def _():
            for cp in _chip_block_copies(s_ref, r_ref, CHUNKS_PER_BLOCK, send_sems, recv_sems):
                cp.wait()

        part = _mm_nt(dp_ref[0, :, 0:CHUNK], w_ref[0]) + _mm_nt(dp_ref[0, :, CHUNK:2 * CHUNK], w_ref[1])

        @pl.when(p == 0)
        def _():
            acc[...] = part

        @pl.when(p > 0)
        def _():
            acc[...] += part

        @pl.when(p == N_PARTS - 1)
        def _():
            def norm_bwd_slab(s, g_sum):
                rows = pl.ds(pl.multiple_of(s * SLAB, SLAB), SLAB)
                xf = x_ref[rows, :]
                r = lax.rsqrt(jnp.mean(xf * xf, axis=-1, keepdims=True) + RMS_EPS)
                xhat = xf * r
                dxn = acc[rows, :]
                dxh = dxn * g_ref[...]
                gx_ref[rows, :] = do_ref[rows, :] + r * (dxh - xhat * jnp.mean(dxh * xhat, axis=-1, keepdims=True))
                return g_sum + jnp.sum(dxn * xhat, axis=0, keepdims=True)

            st_ref[0:1, :] += lax.fori_loop(0, tm // SLAB, norm_bwd_slab, jnp.zeros((1, D_MODEL), F32))

    row = lambda i, p: (i, 0)
    fix = lambda i, p: (0, 0)
    return pl.pallas_call(
        body, grid=(t // tm, N_PARTS),
        in_specs=[
            pl.BlockSpec((1, tm, D_PART), lambda i, p: (p, i, 0)),
            pl.BlockSpec((2, D_MODEL, CHUNK), lambda i, p: (p, 0, 0)),
            pl.BlockSpec((tm, D_MODEL), row), pl.BlockSpec((tm, D_MODEL), row), pl.BlockSpec((1, D_MODEL), fix),
            pl.BlockSpec(memory_space=pl.ANY)],
        out_specs=(pl.BlockSpec((tm, D_MODEL), row), pl.BlockSpec((SUBLANES, D_MODEL), fix),
                   pl.BlockSpec(memory_space=pl.ANY)),
        out_shape=(jax.ShapeDtypeStruct((t, D_MODEL), F32), jax.ShapeDtypeStruct((SUBLANES, D_MODEL), F32),
                   _chip_blocks_shape(sb_in, CHUNKS_PER_BLOCK)),
        scratch_shapes=[pltpu.VMEM((tm, D_MODEL), F32), pltpu.SemaphoreType.DMA((3,)), pltpu.SemaphoreType.DMA((3,))],
        compiler_params=_cp(ARB, ARB), name="input_grad",
    )(dproj, w12, x, do, ln_g, sb_in)


def _w_in_grad(xn, dproj, small):
    t = xn.shape[0]
    small_shape = pltpu.VMEM(small.shape, F32)

    def body(xn_ref, dp_ref, sm_ref, o_ref, ob_ref, red_ref, acc_s, r0, r1, r2, send_sems, recv_sems):
        _allreduce_behind(pl.program_id(0), (0, 1, 3, N_PARTS - 1), sm_ref, acc_s, (r0, r1, r2), red_ref, send_sems, recv_sems)
        xnv = xn_ref[...]
        for s in range(2):
            g = _mm_tn(xnv, dp_ref[0, :, CHUNK * s:CHUNK * (s + 1)])
            o_ref[s] = g
            ob_ref[s] = g.astype(jnp.bfloat16)

    whole = pl.BlockSpec(small.shape, lambda p: (0, 0))
    pair = pl.BlockSpec((2, D_MODEL, CHUNK), lambda p: (p, 0, 0))
    return pl.pallas_call(
        body, grid=(N_PARTS,),
        in_specs=[pl.BlockSpec((t, D_MODEL), lambda p: (0, 0)),
                  pl.BlockSpec((1, t, D_PART), lambda p: (p, 0, 0)), whole],
        out_specs=(pair, pair, whole),
        out_shape=(jax.ShapeDtypeStruct((N_CHUNKS, D_MODEL, CHUNK), F32),
                   jax.ShapeDtypeStruct((N_CHUNKS, D_MODEL, CHUNK), jnp.bfloat16), jax.ShapeDtypeStruct(small.shape, F32)),
        scratch_shapes=[small_shape] * 4 + [pltpu.SemaphoreType.DMA((3,)), pltpu.SemaphoreType.DMA((3,))],
        compiler_params=_cp(ARB), name="w_in_grad",
    )(xn, dproj, small)


def _w_out_grad(yc, yl, dob):
    t = yc.shape[0]
    tk = 2048

    def body(yc_ref, yl_ref, do_ref, o_ref, ob_ref):
        j, kk = pl.program_id(0), pl.program_id(1)

        def accumulate(y_ref):
            part = _mm_tn(y_ref[...], do_ref[...])

            @pl.when(kk == 0)
            def _():
                o_ref[...] = part

            @pl.when(kk > 0)
            def _():
                o_ref[...] += part

            @pl.when(kk == t // tk - 1)
            def _():
                ob_ref[...] = o_ref[...].astype(jnp.bfloat16)

        pl.when(j == 0)(functools.partial(accumulate, yc_ref))
        pl.when(j == 1)(functools.partial(accumulate, yl_ref))

    def rows_of(half):
        return lambda j, kk: (jnp.where(j == half, kk, 0), 0)

    half = pl.BlockSpec((D_PART, D_MODEL), lambda j, kk: (j, 0))
    out, out_b = pl.pallas_call(
        body, grid=(2, t // tk),
        in_specs=[pl.BlockSpec((tk, D_PART), rows_of(0)), pl.BlockSpec((tk, D_PART), rows_of(1)),
                  pl.BlockSpec((tk, D_MODEL), lambda j, kk: (kk, 0))],
        out_specs=(half, half),
        out_shape=(jax.ShapeDtypeStruct((2 * D_PART, D_MODEL), F32), jax.ShapeDtypeStruct((2 * D_PART, D_MODEL), jnp.bfloat16)),
        compiler_params=_cp(ARB, ARB), name="w_out_grad",
    )(yc, yl, dob)
    blocks = (N_CHIPS, 2 * D_PART // N_CHIPS, D_MODEL)
    return out.reshape(blocks), out_b.reshape(blocks)


def _for_groups(n, fn, init, unroll=UNROLL):
    def trip(j, carry):
        for uu in range(unroll):
            carry = fn(j * unroll + uu, carry)
        return carry

    return lax.fori_loop(0, n // unroll, trip, init)


def _pvb(pv_ref, r):
    return jnp.broadcast_to(pv_ref[r:r + 1, :], (SUBLANES, pv_ref.shape[1]))


def _conv3(pv_ref, u, u1, u2):
    return (_pvb(pv_ref, PV_CONV_W) * u2 + _pvb(pv_ref, PV_CONV_W + 1) * u1) + _pvb(pv_ref, PV_CONV_W + 2) * u


def _conv4(pv_ref, v, v1, v2, v3):
    return ((((_pvb(pv_ref, PV_LRU_W) * v3 + _pvb(pv_ref, PV_LRU_W + 1) * v2) + _pvb(pv_ref, PV_LRU_W + 2) * v1)
             + _pvb(pv_ref, PV_LRU_W + 3) * v) + _pvb(pv_ref, PV_LRU_B))


def _mixer_forward(proj, pvec, wai, w_out):
    t = proj.shape[1]
    tb = 512
    ng = tb // SUBLANES
    nt = t // tb

    def body(bg_ref, cg_ref, xc_ref, gc_ref, xl_ref, gl_ref, pv_ref, wai_ref, wo_ref,
             yc_ref, yl_ref, h_ref, wo4_ref,
             ucp_s, xlp_s, ls_s, hbuf_s, u_s, gate_s, zc_s, zl_s, wob_s, local_sem, send_sems, recv_sems):
        _gather_w_out(pl.program_id(0) * nt + pl.program_id(1), NS * nt, wo_ref, wob_s, wo4_ref, local_sem, send_sems, recv_sems)

        @pl.when(pl.program_id(1) == 0)
        def _():
            ucp_s[...] = jnp.zeros_like(ucp_s)
            xlp_s[...] = jnp.zeros_like(xlp_s)
            hbuf_s[...] = jnp.zeros_like(hbuf_s)

        row = lax.broadcasted_iota(jnp.int32, (SUBLANES, LW), 0)
        ls_s[...] = _log_sigmoid(_pvb(pv_ref, PV_LAM))

        def conv_group(g, carry):
            ucp, xlp = carry
            sl = pl.ds(pl.multiple_of(g * SUBLANES, SUBLANES), SUBLANES)
            uc = cg_ref[sl, :] * xc_ref[sl, :]
            v = _conv3(pv_ref, uc, _shift_down(uc, ucp, 1, row), _shift_down(uc, ucp, 2, row))
            yc = bg_ref[sl, :] * v
            rr = lax.rsqrt(_head_mean(yc * yc, CONV_HEAD) + RMS_EPS)
            gc = gc_ref[sl, :]
            zc_s[sl, :] = ((yc * rr) * _pvb(pv_ref, PV_CG)) * (gc * _sigmoid(gc))
            xl = xl_ref[sl, :]
            u_s[sl, :] = _conv4(pv_ref, xl, _shift_down(xl, xlp, 1, row), _shift_down(xl, xlp, 2, row),
                                _shift_down(xl, xlp, 3, row))
            return uc, xl

        ucp, xlp = _for_groups(ng, conv_group, (ucp_s[...], xlp_s[...]), unroll=2 * UNROLL)
        ucp_s[...] = ucp
        xlp_s[...] = xlp

        gate_s[...] = _mm(u_s[...].astype(MXU_DTYPE), wai_ref[0])

        def lru_group(g, h_before):
            sl = pl.ds(pl.multiple_of(g * SUBLANES, SUBLANES), SUBLANES)
            u = u_s[sl, :]
            r, ig, a, e2, mult, _ = _gates(gate_s[sl, 0:LW] + _pvb(pv_ref, PV_BA),
                                           gate_s[sl, LW:2 * LW] + _pvb(pv_ref, PV_BI), u, ls_s[...])
            A, B = _scan8_fwd(a, mult * (ig * u), row)
            h = B + A * jnp.broadcast_to(h_before[SUBLANES - 1:SUBLANES, :], (SUBLANES, LW))
            h_ref[sl, :] = h
            rr = lax.rsqrt(_head_mean(h * h, LRU_HEAD) + RMS_EPS)
            gl = gl_ref[sl, :]
            zl_s[sl, :] = ((h * rr) * _pvb(pv_ref, PV_LG)) * (gl * _sigmoid(gl))
            return h

        hbuf_s[...] = _for_groups(ng, lru_group, hbuf_s[...], unroll=2 * UNROLL)
        yc_ref[...] = zc_s[...].astype(MXU_DTYPE)
        yl_ref[...] = zl_s[...].astype(MXU_DTYPE)

    def part(p):
        return pl.BlockSpec((None, tb, LW), lambda c, i: (2 * p + c // STRIPS_PER_CHUNK, i, c % STRIPS_PER_CHUNK))

    strip = pl.BlockSpec((tb, LW), lambda c, i: (i, c))
    return pl.pallas_call(
        body, grid=(NS, nt),
        in_specs=[part(p) for p in range(N_PARTS)] + [
            pl.BlockSpec((PV_ROWS, LW), lambda c, i: (0, c)),
            pl.BlockSpec((1, LW, 2 * LW), lambda c, i: (c, 0, 0)),
            pl.BlockSpec(w_out.shape, lambda c, i: (0, 0))],
        out_specs=(strip, strip, strip, pl.BlockSpec(memory_space=pl.ANY)),
        out_shape=(jax.ShapeDtypeStruct((t, D_PART), MXU_DTYPE), jax.ShapeDtypeStruct((t, D_PART), MXU_DTYPE),
                   jax.ShapeDtypeStruct((t, D_PART), F32), jax.ShapeDtypeStruct((N_CHIPS,) + w_out.shape, MXU_DTYPE)),
        scratch_shapes=[pltpu.VMEM((SUBLANES, LW), F32), pltpu.VMEM((SUBLANES, LW), F32), pltpu.VMEM((SUBLANES, LW), F32),
                        pltpu.VMEM((SUBLANES, LW), F32), pltpu.VMEM((tb, LW), F32), pltpu.VMEM((tb, 2 * LW), F32),
                        pltpu.VMEM((tb, LW), F32), pltpu.VMEM((tb, LW), F32), pltpu.VMEM(w_out.shape, MXU_DTYPE),
                        pltpu.SemaphoreType.DMA, pltpu.SemaphoreType.DMA((6,)), pltpu.SemaphoreType.DMA((6,))],
        compiler_params=_cp(ARB, ARB), name="mixer_forward",
    )(proj, proj, proj, proj, proj, proj, pvec, wai, w_out)


def _mixer_backward(proj, h, dy, pvec, wai, sb_out):
    t = proj.shape[1]
    tb = 512
    ng = tb // SUBLANES
    nt = t // tb
    gpb = tb // SUBLANES

    def body(bg_ref, cg_ref, xc_ref, gc_ref, xl_ref, gl_ref, h_ref, dyc_ref, dyl_ref,
             cgh_ref, xch_ref, xlh_ref, hh_ref, pv_ref, wai_ref, so_ref,
             dp_ref, gw_ref, sv_ref, ro_ref,
             ls_s, u_s, uce_s, xle_s, he_s, gate_s, dgate_s, du_s, gbuf_s,
             p0_s, p1_s, p2_s, p3_s, p4_s, p5_s, acc_s, an_s, dvn_s, dun_s, send_sems, recv_sems):
        i = pl.program_id(1)
        first_block = i == nt - 1

        @pl.when((pl.program_id(0) == 0) & (i == 0))
        def _():
            for cp in _chip_block_copies(so_ref, ro_ref, 1, send_sems, recv_sems):
                cp.start()

        @pl.when((pl.program_id(0) == NS - 1) & (i == nt - 1))
        def _():
            for cp in _chip_block_copies(so_ref, ro_ref, 1, send_sems, recv_sems):
                cp.wait()

        @pl.when(i == 0)
        def _():
            acc_s[...] = jnp.zeros_like(acc_s)
            gw_ref[...] = jnp.zeros_like(gw_ref)
            an_s[...] = jnp.zeros_like(an_s)
            dvn_s[...] = jnp.zeros_like(dvn_s)
            dun_s[...] = jnp.zeros_like(dun_s)
            gbuf_s[...] = jnp.zeros_like(gbuf_s)

        row = lax.broadcasted_iota(jnp.int32, (SUBLANES, LW), 0)
        ls_s[...] = _log_sigmoid(_pvb(pv_ref, PV_LAM))
        keep = jnp.where(first_block, 0.0, 1.0)
        uce_s[0:SUBLANES, :] = (cgh_ref[...] * xch_ref[...]) * keep
        xle_s[0:SUBLANES, :] = xlh_ref[...] * keep
        he_s[0:SUBLANES, :] = hh_ref[...] * keep
        xle_s[SUBLANES:SUBLANES + tb, :] = xl_ref[...]
        he_s[SUBLANES:SUBLANES + tb, :] = h_ref[...]

        def recompute_group(g, carry):
            r0 = pl.multiple_of(g * SUBLANES, SUBLANES)
            sl = pl.ds(r0, SUBLANES)
            uce_s[pl.ds(r0 + SUBLANES, SUBLANES), :] = cg_ref[sl, :] * xc_ref[sl, :]
            xl = xle_s[pl.ds(r0 + SUBLANES, SUBLANES), :]
            xlp = xle_s[sl, :]
            u_s[sl, :] = _conv4(pv_ref, xl, _shift_down(xl, xlp, 1, row), _shift_down(xl, xlp, 2, row),
                                _shift_down(xl, xlp, 3, row))
            return carry

        _for_groups(ng, recompute_group, 0)
        gate_s[...] = _mm(u_s[...].astype(MXU_DTYPE), wai_ref[0])

        def acc_add(k, v):
            acc_s[k] += v

        def main_group(gi, carry):
            a_next, dv_next, g_next = carry
            g = ng - 1 - gi
            r0 = pl.multiple_of(g * SUBLANES, SUBLANES)
            sl = pl.ds(r0, SUBLANES)
            sl_e = pl.ds(r0 + SUBLANES, SUBLANES)
            lsb = ls_s[...]
            u = u_s[sl, :]
            r, ig, a, e2, mult, inv_mult = _gates(gate_s[sl, 0:LW] + _pvb(pv_ref, PV_BA),
                                                  gate_s[sl, LW:2 * LW] + _pvb(pv_ref, PV_BI), u, lsb)
            gl = gl_ref[sl, :]
            sg = _sigmoid(gl)
            s_l = gl * sg
            h8 = he_s[sl_e, :]
            hprev = _shift_down(h8, he_s[sl, :], 1, row)
            rr = lax.rsqrt(_head_mean(h8 * h8, LRU_HEAD) + RMS_EPS)
            n = h8 * rr
            dz = dyl_ref[sl, :]
            lg = _pvb(pv_ref, PV_LG)
            acc_add(PV_LG, (dz * n) * s_l)
            p5_s[sl, :] = ((dz * n) * lg) * (sg * (1.0 + gl * (1.0 - sg)))
            dn = (dz * lg) * s_l
            dh = rr * (dn - n * _head_mean(dn * n, LRU_HEAD))
            A, B = _scan8_rev(_shift_up(a, a_next, 1, row), dh, row)
            gg = B + A * jnp.broadcast_to(g_next[0:1, :], (SUBLANES, LW))
            da = gg * hprev
            iu = ig * u
            diu = gg * mult
            dla = da * a - (gg * iu) * (e2 * inv_mult)
            acc_add(PV_LAM, dla * (RG_LRU_C * r))
            dra = (dla * (RG_LRU_C * lsb)) * (r * (1.0 - r))
            dia = (diu * u) * (ig * (1.0 - ig))
            dgate_s[sl, 0:LW] = dra
            dgate_s[sl, LW:2 * LW] = dia
            acc_add(PV_BA, dra)
            acc_add(PV_BI, dia)
            du_s[sl, :] = diu * ig
            bg = bg_ref[sl, :]
            gc = gc_ref[sl, :]
            uc = uce_s[sl_e, :]
            ucp = uce_s[sl, :]
            uc1 = _shift_down(uc, ucp, 1, row)
            uc2 = _shift_down(uc, ucp, 2, row)
            v = _conv3(pv_ref, uc, uc1, uc2)
            yc = bg * v
            rrc = lax.rsqrt(_head_mean(yc * yc, CONV_HEAD) + RMS_EPS)
            nc = yc * rrc
            sgc = _sigmoid(gc)
            s_c = gc * sgc
            dzc = dyc_ref[sl, :]
            cgain = _pvb(pv_ref, PV_CG)
            acc_add(PV_CG, (dzc * nc) * s_c)
            p3_s[sl, :] = ((dzc * nc) * cgain) * (sgc * (1.0 + gc * (1.0 - sgc)))
            dnc = (dzc * cgain) * s_c
            dyc = rrc * (dnc - nc * _head_mean(dnc * nc, CONV_HEAD))
            p0_s[sl, :] = dyc * v
            dv = dyc * bg
            duc = (_pvb(pv_ref, PV_CONV_W + 2) * dv + _pvb(pv_ref, PV_CONV_W + 1) * _shift_up(dv, dv_next, 1, row)
                   + _pvb(pv_ref, PV_CONV_W) * _shift_up(dv, dv_next, 2, row))
            acc_add(PV_CONV_W + 2, dv * uc)
            acc_add(PV_CONV_W + 1, dv * uc1)
            acc_add(PV_CONV_W, dv * uc2)
            p1_s[sl, :] = duc * xc_ref[sl, :]
            p2_s[sl, :] = duc * cg_ref[sl, :]
            return a, dv, gg

        a_next, dv_next, g_next = _for_groups(ng, main_group, (an_s[...], dvn_s[...], gbuf_s[...]))
        an_s[...] = a_next
        dvn_s[...] = dv_next
        gbuf_s[...] = g_next

        dgb = dgate_s[...].astype(MXU_DTYPE)
        du_s[...] += _mm_nt(dgb, wai_ref[0])
        gw_ref[0] += _mm_tn(u_s[...].astype(MXU_DTYPE), dgb)

        def lru_conv_group(gi, du_next):
            g = ng - 1 - gi
            r0 = pl.multiple_of(g * SUBLANES, SUBLANES)
            sl = pl.ds(r0, SUBLANES)
            du = du_s[sl, :]
            xl = xle_s[pl.ds(r0 + SUBLANES, SUBLANES), :]
            xlp = xle_s[sl, :]
            acc_add(PV_LRU_B, du)
            acc_add(PV_LRU_W + 3, du * xl)
            acc_add(PV_LRU_W + 2, du * _shift_down(xl, xlp, 1, row))
            acc_add(PV_LRU_W + 1, du * _shift_down(xl, xlp, 2, row))
            acc_add(PV_LRU_W, du * _shift_down(xl, xlp, 3, row))
            p4_s[sl, :] = (((_pvb(pv_ref, PV_LRU_W + 3) * du + _pvb(pv_ref, PV_LRU_W + 2) * _shift_up(du, du_next, 1, row))
                            + _pvb(pv_ref, PV_LRU_W + 1) * _shift_up(du, du_next, 2, row))
                           + _pvb(pv_ref, PV_LRU_W) * _shift_up(du, du_next, 3, row))
            return du

        dun_s[...] = _for_groups(ng, lru_conv_group, dun_s[...])

        for p, p_s in enumerate((p0_s, p1_s, p2_s, p3_s, p4_s, p5_s)):
            dp_ref[p] = p_s[...].astype(MXU_DTYPE)

        @pl.when(first_block)
        def _():
            sv_ref[...] = jnp.zeros_like(sv_ref)
            for k in range(N_ACC):
                tot = jnp.sum(acc_s[k], axis=0, keepdims=True)
                if k == PV_LAM:
                    tot = tot / (1.0 + jnp.exp(pv_ref[PV_LAM:PV_LAM + 1, :]))
                sv_ref[k:k + 1, :] = tot

    def part(p):
        return pl.BlockSpec((None, tb, LW), lambda c, i: (2 * p + c // STRIPS_PER_CHUNK, nt - 1 - i, c % STRIPS_PER_CHUNK))

    def halo(p):
        return pl.BlockSpec((None, SUBLANES, LW), lambda c, i: (2 * p + c // STRIPS_PER_CHUNK,
                                                                jnp.maximum((nt - 1 - i) * gpb - 1, 0), c % STRIPS_PER_CHUNK))

    strip = pl.BlockSpec((tb, LW), lambda c, i: (nt - 1 - i, c))
    big = pltpu.VMEM((tb, LW), F32)
    big_e = pltpu.VMEM((tb + SUBLANES, LW), F32)
    wide = pltpu.VMEM((tb, 2 * LW), F32)
    small = pltpu.VMEM((SUBLANES, LW), F32)
    outs = pl.pallas_call(
        body, grid=(NS, nt),
        in_specs=[part(p) for p in range(N_PARTS)] + [
            strip, strip, pl.BlockSpec((tb, LW), lambda c, i: (nt - 1 - i, NS + c)),
            halo(1), halo(2), halo(4),
            pl.BlockSpec((SUBLANES, LW), lambda c, i: (jnp.maximum((nt - 1 - i) * gpb - 1, 0), c)),
            pl.BlockSpec((PV_ROWS, LW), lambda c, i: (0, c)),
            pl.BlockSpec((1, LW, 2 * LW), lambda c, i: (c, 0, 0)),
            pl.BlockSpec(memory_space=pl.ANY)],
        out_specs=(pl.BlockSpec((N_PARTS, tb, LW), lambda c, i: (0, nt - 1 - i, c)),
                   pl.BlockSpec((1, LW, 2 * LW), lambda c, i: (c, 0, 0)),
                   pl.BlockSpec((PV_ROWS, LW), lambda c, i: (0, c)),
                   pl.BlockSpec(memory_space=pl.ANY)),
        out_shape=(jax.ShapeDtypeStruct((N_PARTS, t, D_PART), MXU_DTYPE),
                   jax.ShapeDtypeStruct((NS, LW, 2 * LW), F32), jax.ShapeDtypeStruct((PV_ROWS, D_PART), F32),
                   _chip_blocks_shape(sb_out, 1)),
        scratch_shapes=[small, big, big_e, big_e, big_e, wide, wide, big, small,
                        big, big, big, big, big, big, pltpu.VMEM((N_ACC, SUBLANES, LW), F32), small, small, small,
                        pltpu.SemaphoreType.DMA((3,)), pltpu.SemaphoreType.DMA((3,))],
        compiler_params=_cp(ARB, ARB), name="mixer_backward",
    )(proj, proj, proj, proj, proj, proj, h, dy, dy, proj, proj, proj, h, pvec, wai, sb_out)
    return outs


def _adamw(w, g, m, v):
    m = ADAM_B1 * m + (1.0 - ADAM_B1) * g
    v = ADAM_B2 * v + (1.0 - ADAM_B2) * (g * g)
    m_hat = m / (1.0 - ADAM_B1 ** ADAM_STEP)
    v_hat = v / (1.0 - ADAM_B2 ** ADAM_STEP)
    delta = -ADAM_LR * (m_hat / (jnp.sqrt(v_hat) + ADAM_EPS) + ADAM_WD * w)
    return delta, m, v


def _adam_w_in(w, m, v, g3):
    rows, cols = w.shape
    tr = 128

    def body(w_ref, m_ref, v_ref, g_ref, go_ref, d_ref, mo_ref, vo_ref):
        for s in range(CHUNKS_PER_BLOCK):
            cs = slice(CHUNK * s, CHUNK * (s + 1))
            g = g_ref[s]
            d, mn, vn = _adamw(w_ref[:, cs], g, m_ref[:, cs], v_ref[:, cs])
            go_ref[:, cs] = g
            d_ref[:, cs] = d
            mo_ref[:, cs] = mn
            vo_ref[:, cs] = vn

    blk = pl.BlockSpec((tr, cols), lambda i: (i, 0))
    return pl.pallas_call(
        body, grid=(rows // tr,),
        in_specs=[blk, blk, blk, pl.BlockSpec((CHUNKS_PER_BLOCK, tr, CHUNK), lambda i: (0, i, 0))],
        out_specs=(blk,) * 4, out_shape=(jax.ShapeDtypeStruct(w.shape, F32),) * 4,
        compiler_params=_cp(ARB), name="adam_w_in",
    )(w, m, v, g3)


def _adam_w_out(w, m, v, g):
    rows, cols = w.shape
    tr = 128

    def body(w_ref, m_ref, v_ref, g_ref, d_ref, mo_ref, vo_ref):
        d_ref[...], mo_ref[...], vo_ref[...] = _adamw(w_ref[...], g_ref[...], m_ref[...], v_ref[...])

    blk = pl.BlockSpec((tr, cols), lambda i: (i, 0))
    return pl.pallas_call(
        body, grid=(rows // tr,), in_specs=[blk] * 4, out_specs=(blk,) * 3,
        out_shape=(jax.ShapeDtypeStruct(w.shape, F32),) * 3,
        compiler_params=_cp(ARB), name="adam_w_out",
    )(w, m, v, g)


def _adam_small(ws, ms, vs, gs):
    n = len(ws)

    def body(*refs):
        w_r, m_r, v_r, g_r = refs[0:n], refs[n:2 * n], refs[2 * n:3 * n], refs[3 * n:4 * n]
        d_o, m_o, v_o = refs[4 * n:5 * n], refs[5 * n:6 * n], refs[6 * n:7 * n]
        for j in range(n):
            d_o[j][...], m_o[j][...], v_o[j][...] = _adamw(w_r[j][...], g_r[j][...], m_r[j][...], v_r[j][...])

    vm = pl.BlockSpec(memory_space=pltpu.VMEM)
    shapes = tuple(jax.ShapeDtypeStruct(w.shape, F32) for w in ws)
    outs = pl.pallas_call(
        body, in_specs=[vm] * (4 * n), out_specs=(vm,) * (3 * n), out_shape=shapes * 3,
        compiler_params=_cp(), name="adam_small",
    )(*ws, *ms, *vs, *gs)
    return outs[0:n], outs[n:2 * n], outs[2 * n:3 * n]


def _block_diag_strips(w):
    w4 = w.reshape(NS, HEADS_PER_STRIP, LRU_HEAD, LRU_HEAD)
    rows = [jnp.pad(w4[:, hh], ((0, 0), (0, 0), (LRU_HEAD * hh, LW - LRU_HEAD * (hh + 1)))) for hh in range(HEADS_PER_STRIP)]
    return jnp.concatenate(rows, axis=1)


def _strip_diag_blocks(g):
    g5 = g.reshape(NS, HEADS_PER_STRIP, LRU_HEAD, HEADS_PER_STRIP, LRU_HEAD)
    return jnp.stack([g5[:, hh, :, hh, :] for hh in range(HEADS_PER_STRIP)], axis=1).reshape(NS * HEADS_PER_STRIP, LRU_HEAD, LRU_HEAD)


def kernel(x, ln_g, w_in, conv_w, lru_conv_w, lru_conv_b, w_a, b_a, w_i, b_i, lam, conv_out_g, lru_out_g, w_out, final_g, loss_target, m_ln_g, m_w_in, m_conv_w, m_lru_conv_w, m_lru_conv_b, m_w_a, m_b_a, m_w_i, m_b_i, m_lam, m_conv_out_g, m_lru_out_g, m_w_out, m_final_g, v_ln_g, v_w_in, v_conv_w, v_lru_conv_w, v_lru_conv_b, v_w_a, v_b_a, v_w_i, v_b_i, v_lam, v_conv_out_g, v_lru_out_g, v_w_out, v_final_g):
    xi, yi, ci = lax.axis_index("x"), lax.axis_index("y"), lax.axis_index("c")
    k = 2 * xi + yi
    t = x.shape[1]
    x2 = x.reshape(t, D_MODEL)
    tgt2 = loss_target.reshape(t, D_MODEL)
    row = lambda a: a.reshape(1, -1)

    small = jnp.concatenate([conv_w, lru_conv_w, jnp.zeros((1, conv_w.shape[1]), F32)], axis=0)
    proj, xn, w12, sm4 = _gather_in_projection(x2, row(ln_g), w_in, small)
    convs = jnp.transpose(sm4, (1, 0, 2)).reshape(SUBLANES, D_PART)
    pvec = jnp.concatenate(
        [convs[0:7], row(lru_conv_b), row(b_a), row(b_i), row(lam), row(conv_out_g), row(lru_out_g),
         jnp.zeros((PV_ROWS - N_ACC, D_PART), F32)], axis=0)
    wai = jnp.concatenate([_block_diag_strips(w_a), _block_diag_strips(w_i)], axis=2).astype(MXU_DTYPE)

    c_arr = jnp.reshape(ci, (1,)).astype(jnp.int32)
    kc_arr = jnp.stack([k, ci]).astype(jnp.int32)
    yc, yl, h, wo4 = _mixer_forward(proj, pvec, wai, w_out)
    wo = wo4.reshape(2 * D_PART, D_MODEL)
    do, dob, dy, st_out = _out_projection_loss(yc, yl, x2, tgt2, wo, row(final_g))
    go4, go4b = _w_out_grad(yc, yl, dob)
    s_out, sb_out = _add_own_half(go4, _exchange_sibling_halves(go4b, "exchange_sibling_halves_out"), c_arr, "add_own_half_out")
    dproj, g_wai, svec, r2o = _mixer_backward(proj, h, dy, pvec, wai, sb_out)
    gwa = _strip_diag_blocks(g_wai[:, :, 0:LW]).reshape(LRU_HEAD, D_PART)
    gwi = _strip_diag_blocks(g_wai[:, :, LW:2 * LW]).reshape(LRU_HEAD, D_PART)
    g12, g12b, red = _w_in_grad(xn, dproj, jnp.concatenate([svec, st_out, gwa, gwi], axis=0))
    s_in, sb_in = _add_own_half(g12, _exchange_sibling_halves(g12b, "exchange_sibling_halves_in"), c_arr, "add_own_half_in")
    grad_x, st_in, r2i = _input_grad(dproj, w12, x2, do, row(ln_g), sb_in)
    f_in = _sum_chip_blocks(s_in, r2i, kc_arr, CHUNKS_PER_BLOCK, "sum_chip_blocks_in")
    f_out = _sum_chip_blocks(s_out, r2o, kc_arr, 1, "sum_chip_blocks_out")
    f_in, f_out = _swap_sibling_halves(f_in, f_out)

    red_ln = _sum_over_devices(st_in)
    r_out = PV_ROWS
    r_wa = PV_ROWS + SUBLANES
    r_wi = r_wa + LRU_HEAD
    loss = red[r_out + 1, 0]

    g_w_in, d_w_in, nm_w_in, nv_w_in = _adam_w_in(w_in, m_w_in, v_w_in, f_in)
    g_w_out = f_out[0]
    d_w_out, nm_w_out, nv_w_out = _adam_w_out(w_out, m_w_out, v_w_out, g_w_out)

    ncol = conv_w.shape[1]
    conv_cols = lax.dynamic_slice(red, (0, k * ncol), (SUBLANES, ncol))
    g_small = {
        "ln_g": red_ln[0], "conv_w": conv_cols[0:3], "lru_conv_w": conv_cols[3:7], "lru_conv_b": red[PV_LRU_B],
        "w_a": red[r_wa:r_wa + LRU_HEAD].reshape(w_a.shape), "b_a": red[PV_BA],
        "w_i": red[r_wi:r_wi + LRU_HEAD].reshape(w_i.shape), "b_i": red[PV_BI], "lam": red[PV_LAM],
        "conv_out_g": red[PV_CG], "lru_out_g": red[PV_LG], "final_g": red[r_out],
    }
    w_small = {"ln_g": ln_g, "conv_w": conv_w, "lru_conv_w": lru_conv_w, "lru_conv_b": lru_conv_b, "w_a": w_a, "b_a": b_a,
               "w_i": w_i, "b_i": b_i, "lam": lam, "conv_out_g": conv_out_g, "lru_out_g": lru_out_g, "final_g": final_g}
    m_small = {"ln_g": m_ln_g, "conv_w": m_conv_w, "lru_conv_w": m_lru_conv_w, "lru_conv_b": m_lru_conv_b, "w_a": m_w_a,
               "b_a": m_b_a, "w_i": m_w_i, "b_i": m_b_i, "lam": m_lam, "conv_out_g": m_conv_out_g,
               "lru_out_g": m_lru_out_g, "final_g": m_final_g}
    v_small = {"ln_g": v_ln_g, "conv_w": v_conv_w, "lru_conv_w": v_lru_conv_w, "lru_conv_b": v_lru_conv_b, "w_a": v_w_a,
               "b_a": v_b_a, "w_i": v_w_i, "b_i": v_b_i, "lam": v_lam, "conv_out_g": v_conv_out_g,
               "lru_out_g": v_lru_out_g, "final_g": v_final_g}
    names = list(w_small)
    as2d = lambda a: a.reshape(1, -1) if a.ndim == 1 else a
    d_s, m_s, v_s = _adam_small([as2d(w_small[n]) for n in names], [as2d(m_small[n]) for n in names],
                                [as2d(v_small[n]) for n in names], [as2d(g_small[n]) for n in names])
    back = lambda n, a: a.reshape(w_small[n].shape)
    grads = {n: g_small[n] for n in names}
    deltas = {n: back(n, a) for n, a in zip(names, d_s)}
    new_m = {n: back(n, a) for n, a in zip(names, m_s)}
    new_v = {n: back(n, a) for n, a in zip(names, v_s)}
    grads["w_in"], deltas["w_in"], new_m["w_in"], new_v["w_in"] = g_w_in, d_w_in, nm_w_in, nv_w_in
    grads["w_out"], deltas["w_out"], new_m["w_out"], new_v["w_out"] = g_w_out, d_w_out, nm_w_out, nv_w_out

    order = ["ln_g", "w_in", "conv_w", "lru_conv_w", "lru_conv_b", "w_a", "b_a", "w_i", "b_i", "lam", "conv_out_g",
             "lru_out_g", "w_out", "final_g"]
    return (loss, grad_x.reshape(x.shape), *[grads[n] for n in order], *[deltas[n] for n in order],
            *[new_m[n] for n in order], *[new_v[n] for n in order])
```

```python
import functools

import jax
import jax.numpy as jnp
from jax import lax
from jax.experimental import pallas as pl
from jax.experimental.pallas import tpu as pltpu

F32 = jnp.float32
MXU_DTYPE = jnp.bfloat16

D_MODEL = 1024
D_PART = 1024
N_PARTS = 6
CHUNK = 512
CHUNKS_PER_BLOCK = 3
N_CHUNKS = 12
N_CHIPS = 4
SUBLANES = 8
LANES = 128
LW = 256
UNROLL = 8
NS = D_PART // LW
STRIPS_PER_CHUNK = CHUNK // LW
CONV_HEAD = 128
LRU_HEAD = 64
HEADS_PER_STRIP = LW // LRU_HEAD
RMS_EPS = 1e-6
RG_LRU_C = 8.0
ADAM_LR = 0.001
ADAM_B1 = 0.9
ADAM_B2 = 0.999
ADAM_EPS = 1e-08
ADAM_WD = 0.01
ADAM_STEP = 10

PV_CONV_W = 0
PV_LRU_W = 3
PV_LRU_B = 7
PV_BA = 8
PV_BI = 9
PV_LAM = 10
PV_CG = 11
PV_LG = 12
PV_ROWS = 16
N_ACC = 13

SLAB = 128
MESH = pl.DeviceIdType.MESH
VMEM_LIMIT = 56 * 1024 * 1024
ARB = "arbitrary"


def _cp(*sem, **kw):
    return pltpu.CompilerParams(dimension_semantics=sem or None, vmem_limit_bytes=VMEM_LIMIT, **kw)


def _mm(a, b):
    return jnp.dot(a, b, preferred_element_type=F32)


def _mm_nt(a, b):
    return lax.dot_general(a, b, (((1,), (1,)), ((), ())), preferred_element_type=F32)


def _mm_tn(a, b):
    return lax.dot_general(a, b, (((0,), (0,)), ((), ())), preferred_element_type=F32)


def _sigmoid(x):
    return 0.5 * jnp.tanh(0.5 * x) + 0.5


def _log_sigmoid(x):
    z = jnp.exp(-jnp.abs(x))
    u = 1.0 + z
    log1p = jnp.where(u == 1.0, z, jnp.log(u) * z / (u - 1.0))
    return jnp.minimum(x, 0.0) - log1p


def _head_mean(z, head):
    out = []
    for k in range(z.shape[1] // LANES):
        zk = z[:, LANES * k:LANES * (k + 1)]
        if head == LANES:
            m = jnp.sum(zk, axis=-1, keepdims=True) * (1.0 / head)
            out.append(jnp.broadcast_to(m, zk.shape))
        else:
            lo = lax.broadcasted_iota(jnp.int32, zk.shape, 1) < head
            s_lo = jnp.sum(jnp.where(lo, zk, 0.0), axis=-1, keepdims=True)
            s_hi = jnp.sum(jnp.where(lo, 0.0, zk), axis=-1, keepdims=True)
            out.append(jnp.where(lo, s_lo, s_hi) * (1.0 / head))
    return jnp.concatenate(out, axis=1)


def _shift_down(cur, prev, d, row):
    return pltpu.roll(jnp.where(row < SUBLANES - d, cur, prev), d, 0)


def _shift_up(cur, nxt, d, row):
    return pltpu.roll(jnp.where(row >= d, cur, nxt), SUBLANES - d, 0)


def _scan8_fwd(a, b, row):
    A, B = a, b
    for d in (1, 2, 4):
        m = row >= d
        a_s = jnp.where(m, pltpu.roll(A, d, 0), 1.0)
        b_s = jnp.where(m, pltpu.roll(B, d, 0), 0.0)
        B = A * b_s + B
        A = A * a_s
    return A, B


def _scan8_rev(a, b, row):
    A, B = a, b
    for d in (1, 2, 4):
        m = row < SUBLANES - d
        a_s = jnp.where(m, pltpu.roll(A, SUBLANES - d, 0), 1.0)
        b_s = jnp.where(m, pltpu.roll(B, SUBLANES - d, 0), 0.0)
        B = A * b_s + B
        A = A * a_s
    return A, B


def _gates(ra, ia, u, lsb):
    r = _sigmoid(ra)
    ig = _sigmoid(ia)
    la = (RG_LRU_C * r) * lsb
    a = jnp.exp(la)
    e2 = a * a
    em = -jnp.tanh(la) * (1.0 + e2)
    inv_mult = lax.rsqrt(em)
    return r, ig, a, e2, em * inv_mult, inv_mult


def _mesh_pos():
    x, y, c = lax.axis_index("x"), lax.axis_index("y"), lax.axis_index("c")
    chips = [(1 - x, y), (x, 1 - y), (1 - x, 1 - y)]
    return x, y, c, chips


def _gather_in_projection(x, ln_g, w_in, small):
    t = x.shape[0]
    rb_x = 512
    rb_mm = 1024
    n_mm = t // rb_mm
    half = w_in.shape[0] // 2
    n_ici = 3 * CHUNKS_PER_BLOCK

    def body(x_hbm, g_ref, wi_ref, sm_ref, proj_hbm, xn_ref, w12_ref, sm4_ref,
             xbuf, obuf, x_sems, o_sems, send_sems, recv_sems):
        x_, y_, c, chips = _mesh_pos()
        k = 2 * x_ + y_
        sib = (x_, y_, 1 - c)
        my_rows = pl.ds(pl.multiple_of(half * c, half), half)
        sib_rows = pl.ds(pl.multiple_of(half * (1 - c), half), half)

        sm4_ref[k] = sm_ref[...]

        def remote(ref, sem, to):
            return pltpu.make_async_remote_copy(src_ref=ref, dst_ref=ref, send_sem=send_sems.at[sem],
                                                recv_sem=recv_sems.at[sem], device_id=to, device_id_type=MESH)

        def chunk_of(chip, s):
            return CHUNKS_PER_BLOCK * (2 * chip[0] + chip[1]) + s

        ici = lambda m, s: 3 * s + m
        fwd = lambda m, s: n_ici + 3 * s + m
        sml = lambda m: 2 * n_ici + m

        sends = []
        for s in range(CHUNKS_PER_BLOCK):
            w12_ref[chunk_of((x_, y_), s)] = wi_ref[:, CHUNK * s:CHUNK * (s + 1)].astype(MXU_DTYPE)
            for m, chip in enumerate(chips):
                sends.append(remote(w12_ref.at[chunk_of((x_, y_), s), my_rows, :], ici(m, s), (*chip, c)))
                sends[-1].start()
        for m, chip in enumerate(chips):
            sends.append(remote(sm4_ref.at[k], sml(m), (*chip, c)))
            sends[-1].start()

        def x_copy(rb, slot):
            return pltpu.make_async_copy(x_hbm.at[pl.ds(rb * rb_x, rb_x), :], xbuf.at[slot], x_sems.at[slot])

        x_copy(0, 0).start()
        for rb in range(t // rb_x):
            slot = rb % 2
            x_copy(rb, slot).wait()
            if rb + 1 < t // rb_x:
                x_copy(rb + 1, 1 - slot).start()

            def norm_slab(sl, carry, rb=rb, slot=slot):
                xf = xbuf[slot, pl.ds(pl.multiple_of(sl * SLAB, SLAB), SLAB), :]
                r = lax.rsqrt(jnp.mean(xf * xf, axis=-1, keepdims=True) + RMS_EPS)
                xn_ref[pl.ds(pl.multiple_of(rb * rb_x + sl * SLAB, SLAB), SLAB), :] = ((xf * r) * g_ref[...]).astype(MXU_DTYPE)
                return carry

            lax.fori_loop(0, rb_x // SLAB, norm_slab, 0)

        def out_copy(q, i, slot):
            return pltpu.make_async_copy(obuf.at[slot], proj_hbm.at[q, pl.ds(pl.multiple_of(i * rb_mm, rb_mm), rb_mm), :],
                                         o_sems.at[slot])

        def project(q, very_first):
            def row_block(i, carry):
                slot = i % 2

                def wait_buffer():
                    out_copy(q, i, slot).wait()

                if very_first:
                    pl.when(i >= 2)(wait_buffer)
                else:
                    wait_buffer()
                obuf[slot] = _mm(xn_ref[pl.ds(pl.multiple_of(i * rb_mm, rb_mm), rb_mm), :], w12_ref[q])
                out_copy(q, i, slot).start()
                return carry

            lax.fori_loop(0, n_mm, row_block, 0)

        for s in range(CHUNKS_PER_BLOCK):
            project(chunk_of((x_, y_), s), very_first=(s == 0))

        order = [(m, s) for s in range(CHUNKS_PER_BLOCK) for m in range(3)]
        forwards = []
        for j, (m, s) in enumerate(order):
            q = chunk_of(chips[m], s)
            remote(w12_ref.at[q, my_rows, :], ici(m, s), sib).wait_recv()
            f = remote(w12_ref.at[q, my_rows, :], fwd(m, s), sib)
            f.start()
            forwards.append(f)
            if j > 0:
                pm, ps = order[j - 1]
                pq = chunk_of(chips[pm], ps)
                remote(w12_ref.at[pq, sib_rows, :], fwd(pm, ps), sib).wait_recv()
                project(pq, very_first=False)
        pm, ps = order[-1]
        pq = chunk_of(chips[pm], ps)
        remote(w12_ref.at[pq, sib_rows, :], fwd(pm, ps), sib).wait_recv()
        project(pq, very_first=False)

        for m, chip in enumerate(chips):
            remote(sm4_ref.at[2 * chip[0] + chip[1]], sml(m), sib).wait_recv()
        for cp in sends + forwards:
            cp.wait_send()
        for slot in range(2):
            out_copy(0, slot, slot).wait()

    assert n_mm % 2 == 0 and n_mm >= 2
    vm = pl.BlockSpec(memory_space=pltpu.VMEM)
    hbm = pl.BlockSpec(memory_space=pl.ANY)
    n_sems = 2 * n_ici + 3
    return pl.pallas_call(
        body,
        out_shape=(jax.ShapeDtypeStruct((N_CHUNKS, t, CHUNK), F32), jax.ShapeDtypeStruct((t, D_MODEL), MXU_DTYPE),
                   jax.ShapeDtypeStruct((N_CHUNKS, w_in.shape[0], CHUNK), MXU_DTYPE),
                   jax.ShapeDtypeStruct((N_CHIPS,) + small.shape, F32)),
        in_specs=[hbm, vm, vm, vm], out_specs=(hbm, vm, vm, vm),
        scratch_shapes=[pltpu.VMEM((2, rb_x, D_MODEL), F32), pltpu.VMEM((2, rb_mm, CHUNK), F32),
                        pltpu.SemaphoreType.DMA((2,)), pltpu.SemaphoreType.DMA((2,)),
                        pltpu.SemaphoreType.DMA((n_sems,)), pltpu.SemaphoreType.DMA((n_sems,))],
        compiler_params=_cp(), name="gather_in_projection",
    )(x, ln_g, w_in, small)


def _sum_over_devices(v):
    n_dev = 8

    def body(v_ref, o_ref, slots, send_sems, recv_sems):
        x, y, c, _ = _mesh_pos()
        me = 4 * x + 2 * y + c
        slots[me] = v_ref[...]
        cps = []
        for d in range(1, n_dev):
            peer = (1 - x if d & 4 else x, 1 - y if d & 2 else y, 1 - c if d & 1 else c)
            cps.append(pltpu.make_async_remote_copy(src_ref=slots.at[me], dst_ref=slots.at[me], send_sem=send_sems.at[d - 1],
                                                    recv_sem=recv_sems.at[d - 1], device_id=peer, device_id_type=MESH))
        for cp in cps:
            cp.start()
        for cp in cps:
            cp.wait()
        total = slots[0]
        for dev in range(1, n_dev):
            total = total + slots[dev]
        o_ref[...] = total

    vm = pl.BlockSpec(memory_space=pltpu.VMEM)
    return pl.pallas_call(
        body, out_shape=jax.ShapeDtypeStruct(v.shape, F32), in_specs=[vm], out_specs=vm,
        scratch_shapes=[pltpu.VMEM((n_dev,) + v.shape, F32), pltpu.SemaphoreType.DMA((n_dev - 1,)),
                        pltpu.SemaphoreType.DMA((n_dev - 1,))],
        compiler_params=_cp(), name="sum_over_devices",
    )(v)


def _allreduce_behind(step, when, in_ref, acc_s, rbufs, out_ref, send_sems, recv_sems):
    x, y, c, _ = _mesh_pos()
    peers = [(x, y, 1 - c), (1 - x, y, c), (x, 1 - y, c)]

    def exchange(ph):
        return pltpu.make_async_remote_copy(src_ref=acc_s, dst_ref=rbufs[ph], send_sem=send_sems.at[ph],
                                            recv_sem=recv_sems.at[ph], device_id=peers[ph], device_id_type=MESH)

    @pl.when(step == when[0])
    def _():
        acc_s[...] = in_ref[...]
        exchange(0).start()

    for ph in (1, 2):
        @pl.when(step == when[ph])
        def _(ph=ph):
            exchange(ph - 1).wait()
            acc_s[...] = acc_s[...] + rbufs[ph - 1][...]
            exchange(ph).start()

    @pl.when(step == when[3])
    def _():
        exchange(2).wait()
        out_ref[...] = acc_s[...] + rbufs[2][...]


def _exchange_sibling_halves(g, name):
    n, rows, cols = g.shape
    half = rows // 2

    def body(g_ref, r_ref, send_sem, recv_sem):
        x, y, c, _ = _mesh_pos()
        cp = pltpu.make_async_remote_copy(src_ref=g_ref.at[:, pl.ds(pl.multiple_of(half * (1 - c), half), half), :],
                                          dst_ref=r_ref, send_sem=send_sem, recv_sem=recv_sem,
                                          device_id=(x, y, 1 - c), device_id_type=MESH)
        cp.start()
        cp.wait()

    hbm = pl.BlockSpec(memory_space=pl.ANY)
    return pl.pallas_call(
        body, out_shape=jax.ShapeDtypeStruct((n, half, cols), g.dtype), in_specs=[hbm], out_specs=hbm,
        scratch_shapes=[pltpu.SemaphoreType.DMA, pltpu.SemaphoreType.DMA],
        compiler_params=_cp(), name=name,
    )(g)


def _add_own_half(g, r, c_arr, name):
    n, rr, cc = r.shape

    def body(c_ref, g_ref, r_ref, o_ref, ob_ref):
        s = g_ref[...] + r_ref[...].astype(F32)
        o_ref[...] = s
        ob_ref[...] = s.astype(jnp.bfloat16)

    blk = pl.BlockSpec((1, rr, cc), lambda q, c_ref: (q, 0, 0))
    return pl.pallas_call(
        body, out_shape=(jax.ShapeDtypeStruct(r.shape, F32), jax.ShapeDtypeStruct(r.shape, jnp.bfloat16)),
        grid_spec=pltpu.PrefetchScalarGridSpec(
            num_scalar_prefetch=1, grid=(n,),
            in_specs=[pl.BlockSpec((1, rr, cc), lambda q, c_ref: (q, c_ref[0], 0)), blk],
            out_specs=(blk, blk)),
        compiler_params=_cp(ARB), name=name,
    )(c_arr, g, r)


def _chip_block_copies(s_ref, r_ref, n_sub, send_sems, recv_sems):
    x, y, c, chips = _mesh_pos()
    cps = []
    for m, chip in enumerate(chips):
        kk = 2 * chip[0] + chip[1]
        cps.append(pltpu.make_async_remote_copy(
            src_ref=s_ref.at[pl.ds(n_sub * kk, n_sub)], dst_ref=r_ref.at[m],
            send_sem=send_sems.at[m], recv_sem=recv_sems.at[m], device_id=(*chip, c), device_id_type=MESH))
    return cps


def _gather_w_out(step, n_steps, wo_ref, wob_s, wo4_ref, local_sem, send_sems, recv_sems):
    x, y, c, chips = _mesh_pos()
    sib = (x, y, 1 - c)
    half = wo_ref.shape[0] // 2

    def rows(core):
        return pl.ds(pl.multiple_of(half * core, half), half)

    def block_half(chip, core):
        return wo4_ref.at[2 * chip[0] + chip[1], rows(core), :]

    def remote(src, dst, sem, to):
        return pltpu.make_async_remote_copy(src_ref=src, dst_ref=dst, send_sem=send_sems.at[sem], recv_sem=recv_sems.at[sem],
                                            device_id=to, device_id_type=MESH)

    local = pltpu.make_async_copy(wob_s, wo4_ref.at[2 * x + y], local_sem)
    ici = [remote(wob_s.at[rows(c), :], block_half((x, y), c), m, (*chip, c)) for m, chip in enumerate(chips)]
    fwd = [remote(block_half(chip, c), block_half(chip, c), 3 + m, sib) for m, chip in enumerate(chips)]

    @pl.when(step == 0)
    def _():
        wob_s[...] = wo_ref[...].astype(MXU_DTYPE)
        local.start()
        for cp in ici:
            cp.start()

    @pl.when(step == n_steps // 2)
    def _():
        for m, chip in enumerate(chips):
            remote(block_half(chip, c), block_half(chip, c), m, sib).wait_recv()
            fwd[m].start()

    @pl.when(step == n_steps - 1)
    def _():
        for m, chip in enumerate(chips):
            remote(block_half(chip, 1 - c), block_half(chip, 1 - c), 3 + m, sib).wait_recv()
        for cp in ici + fwd:
            cp.wait_send()
        local.wait()


def _chip_blocks_shape(s, n_sub):
    return jax.ShapeDtypeStruct((3, n_sub) + s.shape[1:], s.dtype)


def _sum_chip_blocks(s, r, kc_arr, n_sub, name):
    _, rr, cc = s.shape

    def body(kc_ref, s_ref, r_ref, o_ref):
        o_ref[...] = ((s_ref[...] + r_ref[0].astype(F32)) + r_ref[1].astype(F32)) + r_ref[2].astype(F32)

    return pl.pallas_call(
        body, out_shape=jax.ShapeDtypeStruct((n_sub, 2 * rr, cc), F32),
        grid_spec=pltpu.PrefetchScalarGridSpec(
            num_scalar_prefetch=1, grid=(n_sub,),
            in_specs=[pl.BlockSpec((1, rr, cc), lambda q, kc: (n_sub * kc[0] + q, 0, 0)),
                      pl.BlockSpec((3, 1, rr, cc), lambda q, kc: (0, q, 0, 0))],
            out_specs=pl.BlockSpec((1, rr, cc), lambda q, kc: (q, kc[1], 0))),
        compiler_params=_cp(ARB), name=name,
    )(kc_arr, s, r)


def _swap_sibling_halves(f_in, f_out):
    hi, ho = f_in.shape[1] // 2, f_out.shape[1] // 2

    def body(fi_in, fo_in, fi_ref, fo_ref, send_sems, recv_sems):
        del fi_in, fo_in
        x, y, c, _ = _mesh_pos()
        sib = (x, y, 1 - c)
        si = fi_ref.at[:, pl.ds(pl.multiple_of(hi * c, hi), hi), :]
        so = fo_ref.at[:, pl.ds(pl.multiple_of(ho * c, ho), ho), :]
        cps = [
            pltpu.make_async_remote_copy(src_ref=si, dst_ref=si, send_sem=send_sems.at[0], recv_sem=recv_sems.at[0],
                                         device_id=sib, device_id_type=MESH),
            pltpu.make_async_remote_copy(src_ref=so, dst_ref=so, send_sem=send_sems.at[1], recv_sem=recv_sems.at[1],
                                         device_id=sib, device_id_type=MESH),
        ]
        for cp in cps:
            cp.start()
        for cp in cps:
            cp.wait()

    hbm = pl.BlockSpec(memory_space=pl.ANY)
    return pl.pallas_call(
        body,
        out_shape=(jax.ShapeDtypeStruct(f_in.shape, F32), jax.ShapeDtypeStruct(f_out.shape, F32)),
        in_specs=[hbm, hbm], out_specs=(hbm, hbm), input_output_aliases={0: 0, 1: 1},
        scratch_shapes=[pltpu.SemaphoreType.DMA((2,)), pltpu.SemaphoreType.DMA((2,))],
        compiler_params=_cp(), name="swap_sibling_halves",
    )(f_in, f_out)


def _out_projection_loss(yc, yl, x, target, wo, final_g):
    t = x.shape[0]
    tm = 512

    def body(yc_ref, yl_ref, x_ref, t_ref, wo_ref, fg_ref, do_ref, dob_ref, dy_ref, st_ref, y_wo):
        @pl.when(pl.program_id(0) == 0)
        def _():
            st_ref[...] = jnp.zeros_like(st_ref)

        y_wo[...] = _mm(yc_ref[...], wo_ref[0:D_PART, :]) + _mm(yl_ref[...], wo_ref[D_PART:2 * D_PART, :])

        def norm_loss_slab(s, carry):
            g_sum, loss_sum = carry
            rows = pl.ds(pl.multiple_of(s * SLAB, SLAB), SLAB)
            o = x_ref[rows, :] + y_wo[rows, :]
            r2 = lax.rsqrt(jnp.mean(o * o, axis=-1, keepdims=True) + RMS_EPS)
            ohat = o * r2
            fg = fg_ref[...]
            diff = ohat * fg - t_ref[rows, :]
            dout = diff * (1.0 / D_MODEL)
            gp = dout * fg
            do = r2 * (gp - ohat * jnp.mean(gp * ohat, axis=-1, keepdims=True))
            do_ref[rows, :] = do
            dob_ref[rows, :] = do.astype(MXU_DTYPE)
            loss = 0.5 * jnp.sum(jnp.sum(diff * diff, axis=-1, keepdims=True) * (1.0 / D_MODEL), axis=0, keepdims=True)
            return g_sum + jnp.sum(dout * ohat, axis=0, keepdims=True), loss_sum + loss

        g_sum, loss_sum = lax.fori_loop(0, tm // SLAB, norm_loss_slab,
                                        (jnp.zeros((1, D_MODEL), F32), jnp.zeros((1, 1), F32)))
        st_ref[0:1, :] += g_sum
        st_ref[1:2, :] += jnp.broadcast_to(loss_sum, (1, D_MODEL))
        dy_ref[...] = _mm_nt(dob_ref[...], wo_ref[...])

    row = lambda i: (i, 0)
    fix = lambda i: (0, 0)
    return pl.pallas_call(
        body, grid=(t // tm,),
        in_specs=[pl.BlockSpec((tm, D_PART), row), pl.BlockSpec((tm, D_PART), row),
                  pl.BlockSpec((tm, D_MODEL), row), pl.BlockSpec((tm, D_MODEL), row),
                  pl.BlockSpec((2 * D_PART, D_MODEL), fix), pl.BlockSpec((1, D_MODEL), fix)],
        out_specs=(pl.BlockSpec((tm, D_MODEL), row), pl.BlockSpec((tm, D_MODEL), row),
                   pl.BlockSpec((tm, 2 * D_PART), row), pl.BlockSpec((SUBLANES, D_MODEL), fix)),
        out_shape=(jax.ShapeDtypeStruct((t, D_MODEL), F32), jax.ShapeDtypeStruct((t, D_MODEL), MXU_DTYPE),
                   jax.ShapeDtypeStruct((t, 2 * D_PART), F32), jax.ShapeDtypeStruct((SUBLANES, D_MODEL), F32)),
        scratch_shapes=[pltpu.VMEM((tm, D_MODEL), F32)],
        compiler_params=_cp(ARB), name="out_projection_loss",
    )(yc, yl, x, target, wo, final_g)


def _input_grad(dproj, w12, x, do, ln_g, sb_in):
    t = x.shape[0]
    tm = 1024

    def body(dp_ref, w_ref, x_ref, do_ref, g_ref, s_ref, gx_ref, st_ref, r_ref, acc, send_sems, recv_sems):
        i, p = pl.program_id(0), pl.program_id(1)

        @pl.when((i == 0) & (p == 0))
        def _():
            st_ref[...] = jnp.zeros_like(st_ref)
            for cp in _chip_block_copies(s_ref, r_ref, CHUNKS_PER_BLOCK, send_sems, recv_sems):
                cp.start()

        @pl.when((i == t // tm - 1) & (p == N_PARTS - 1))
        def _():
            for cp in _chip_block_copies(s_ref, r_ref, CHUNKS_PER_BLOCK, send_sems, recv_sems):
                cp.wait()

        @pl.when(p == 0)
        def _():
            acc[...] = jnp.zeros_like(acc)

        acc[...] += _mm_nt(dp_ref[0], jnp.concatenate([w_ref[0], w_ref[1]], axis=1))

        @pl.when(p == N_PARTS - 1)
        def _():
            def norm_bwd_slab(s, g_sum):
                rows = pl.ds(pl.multiple_of(s * SLAB, SLAB), SLAB)
                xf = x_ref[rows, :]
                r = lax.rsqrt(jnp.mean(xf * xf, axis=-1, keepdims=True) + RMS_EPS)
                xhat = xf * r
                dxn = acc[rows, :]
                dxh = dxn * g_ref[...]
                gx_ref[rows, :] = do_ref[rows, :] + r * (dxh - xhat * jnp.mean(dxh * xhat, axis=-1, keepdims=True))
                return g_sum + jnp.sum(dxn * xhat, axis=0, keepdims=True)

            st_ref[0:1, :] += lax.fori_loop(0, tm // SLAB, norm_bwd_slab, jnp.zeros((1, D_MODEL), F32))

    row = lambda i, p: (i, 0)
    fix = lambda i, p: (0, 0)
    return pl.pallas_call(
        body, grid=(t // tm, N_PARTS),
        in_specs=[
            pl.BlockSpec((1, tm, D_PART), lambda i, p: (p, i, 0)),
            pl.BlockSpec((2, D_MODEL, CHUNK), lambda i, p: (p, 0, 0)),
            pl.BlockSpec((tm, D_MODEL), row), pl.BlockSpec((tm, D_MODEL), row), pl.BlockSpec((1, D_MODEL), fix),
            pl.BlockSpec(memory_space=pl.ANY)],
        out_specs=(pl.BlockSpec((tm, D_MODEL), row), pl.BlockSpec((SUBLANES, D_MODEL), fix),
                   pl.BlockSpec(memory_space=pl.ANY)),
        out_shape=(jax.ShapeDtypeStruct((t, D_MODEL), F32), jax.ShapeDtypeStruct((SUBLANES, D_MODEL), F32),
                   _chip_blocks_shape(sb_in, CHUNKS_PER_BLOCK)),
        scratch_shapes=[pltpu.VMEM((tm, D_MODEL), F32), pltpu.SemaphoreType.DMA((3,)), pltpu.SemaphoreType.DMA((3,))],
        compiler_params=_cp(ARB, ARB), name="input_grad",
    )(dproj, w12, x, do, ln_g, sb_in)


def _w_in_grad(xn, dproj, small):
    t = xn.shape[0]
    small_shape = pltpu.VMEM(small.shape, F32)

    def body(xn_ref, dp_ref, sm_ref, o_ref, ob_ref, red_ref, acc_s, r0, r1, r2, send_sems, recv_sems):
        _allreduce_behind(pl.program_id(0), (0, 1, 3, N_PARTS - 1), sm_ref, acc_s, (r0, r1, r2), red_ref, send_sems, recv_sems)
        g = _mm_tn(xn_ref[...], dp_ref[0])
        for s in range(2):
            o_ref[s] = g[:, CHUNK * s:CHUNK * (s + 1)]
            ob_ref[s] = g[:, CHUNK * s:CHUNK * (s + 1)].astype(jnp.bfloat16)

    whole = pl.BlockSpec(small.shape, lambda p: (0, 0))
    pair = pl.BlockSpec((2, D_MODEL, CHUNK), lambda p: (p, 0, 0))
    return pl.pallas_call(
        body, grid=(N_PARTS,),
        in_specs=[pl.BlockSpec((t, D_MODEL), lambda p: (0, 0)),
                  pl.BlockSpec((1, t, D_PART), lambda p: (p, 0, 0)), whole],
        out_specs=(pair, pair, whole),
        out_shape=(jax.ShapeDtypeStruct((N_CHUNKS, D_MODEL, CHUNK), F32),
                   jax.ShapeDtypeStruct((N_CHUNKS, D_MODEL, CHUNK), jnp.bfloat16), jax.ShapeDtypeStruct(small.shape, F32)),
        scratch_shapes=[small_shape] * 4 + [pltpu.SemaphoreType.DMA((3,)), pltpu.SemaphoreType.DMA((3,))],
        compiler_params=_cp(ARB), name="w_in_grad",
    )(xn, dproj, small)


def _w_out_grad(yc, yl, dob):
    t = yc.shape[0]
    tk = 2048

    def body(yc_ref, yl_ref, do_ref, o_ref, ob_ref):
        j, kk = pl.program_id(0), pl.program_id(1)

        def accumulate(y_ref):
            @pl.when(kk == 0)
            def _():
                o_ref[...] = jnp.zeros_like(o_ref)

            o_ref[...] += _mm_tn(y_ref[...], do_ref[...])

            @pl.when(kk == t // tk - 1)
            def _():
                ob_ref[...] = o_ref[...].astype(jnp.bfloat16)

        pl.when(j == 0)(functools.partial(accumulate, yc_ref))
        pl.when(j == 1)(functools.partial(accumulate, yl_ref))

    def rows_of(half):
        return lambda j, kk: (jnp.where(j == half, kk, 0), 0)

    half = pl.BlockSpec((D_PART, D_MODEL), lambda j, kk: (j, 0))
    out, out_b = pl.pallas_call(
        body, grid=(2, t // tk),
        in_specs=[pl.BlockSpec((tk, D_PART), rows_of(0)), pl.BlockSpec((tk, D_PART), rows_of(1)),
                  pl.BlockSpec((tk, D_MODEL), lambda j, kk: (kk, 0))],
        out_specs=(half, half),
        out_shape=(jax.ShapeDtypeStruct((2 * D_PART, D_MODEL), F32), jax.ShapeDtypeStruct((2 * D_PART, D_MODEL), jnp.bfloat16)),
        compiler_params=_cp(ARB, ARB), name="w_out_grad",
    )(yc, yl, dob)
    blocks = (N_CHIPS, 2 * D_PART // N_CHIPS, D_MODEL)
    return out.reshape(blocks), out_b.reshape(blocks)


def _for_groups(n, fn, init, unroll=UNROLL):
    def trip(j, carry):
        for uu in range(unroll):
            carry = fn(j * unroll + uu, carry)
        return carry

    return lax.fori_loop(0, n // unroll, trip, init)


def _pvb(pv_ref, r):
    return jnp.broadcast_to(pv_ref[r:r + 1, :], (SUBLANES, pv_ref.shape[1]))


def _conv3(pv_ref, u, u1, u2):
    return (_pvb(pv_ref, PV_CONV_W) * u2 + _pvb(pv_ref, PV_CONV_W + 1) * u1) + _pvb(pv_ref, PV_CONV_W + 2) * u


def _conv4(pv_ref, v, v1, v2, v3):
    return ((((_pvb(pv_ref, PV_LRU_W) * v3 + _pvb(pv_ref, PV_LRU_W + 1) * v2) + _pvb(pv_ref, PV_LRU_W + 2) * v1)
             + _pvb(pv_ref, PV_LRU_W + 3) * v) + _pvb(pv_ref, PV_LRU_B))


def _mixer_forward(proj, pvec, wai, w_out):
    t = proj.shape[1]
    tb = 512
    ng = tb // SUBLANES
    nt = t // tb

    def body(bg_ref, cg_ref, xc_ref, gc_ref, xl_ref, gl_ref, pv_ref, wai_ref, wo_ref,
             yc_ref, yl_ref, h_ref, wo4_ref,
             ucp_s, xlp_s, ls_s, hbuf_s, u_s, gate_s, zc_s, zl_s, wob_s, local_sem, send_sems, recv_sems):
        _gather_w_out(pl.program_id(0) * nt + pl.program_id(1), NS * nt, wo_ref, wob_s, wo4_ref, local_sem, send_sems, recv_sems)

        @pl.when(pl.program_id(1) == 0)
        def _():
            ucp_s[...] = jnp.zeros_like(ucp_s)
            xlp_s[...] = jnp.zeros_like(xlp_s)
            hbuf_s[...] = jnp.zeros_like(hbuf_s)

        row = lax.broadcasted_iota(jnp.int32, (SUBLANES, LW), 0)
        ls_s[...] = _log_sigmoid(_pvb(pv_ref, PV_LAM))

        def conv_group(g, carry):
            ucp, xlp = carry
            sl = pl.ds(pl.multiple_of(g * SUBLANES, SUBLANES), SUBLANES)
            uc = cg_ref[sl, :] * xc_ref[sl, :]
            v = _conv3(pv_ref, uc, _shift_down(uc, ucp, 1, row), _shift_down(uc, ucp, 2, row))
            yc = bg_ref[sl, :] * v
            rr = lax.rsqrt(_head_mean(yc * yc, CONV_HEAD) + RMS_EPS)
            gc = gc_ref[sl, :]
            zc_s[sl, :] = ((yc * rr) * _pvb(pv_ref, PV_CG)) * (gc * _sigmoid(gc))
            xl = xl_ref[sl, :]
            u_s[sl, :] = _conv4(pv_ref, xl, _shift_down(xl, xlp, 1, row), _shift_down(xl, xlp, 2, row),
                                _shift_down(xl, xlp, 3, row))
            return uc, xl

        ucp, xlp = _for_groups(ng, conv_group, (ucp_s[...], xlp_s[...]), unroll=2 * UNROLL)
        ucp_s[...] = ucp
        xlp_s[...] = xlp

        gate_s[...] = _mm(u_s[...].astype(MXU_DTYPE), wai_ref[0])

        def lru_group(g, h_before):
            sl = pl.ds(pl.multiple_of(g * SUBLANES, SUBLANES), SUBLANES)
            u = u_s[sl, :]
            r, ig, a, e2, mult, _ = _gates(gate_s[sl, 0:LW] + _pvb(pv_ref, PV_BA),
                                           gate_s[sl, LW:2 * LW] + _pvb(pv_ref, PV_BI), u, ls_s[...])
            A, B = _scan8_fwd(a, mult * (ig * u), row)
            h = B + A * jnp.broadcast_to(h_before[SUBLANES - 1:SUBLANES, :], (SUBLANES, LW))
            h_ref[sl, :] = h
            rr = lax.rsqrt(_head_mean(h * h, LRU_HEAD) + RMS_EPS)
            gl = gl_ref[sl, :]
            zl_s[sl, :] = ((h * rr) * _pvb(pv_ref, PV_LG)) * (gl * _sigmoid(gl))
            return h

        hbuf_s[...] = _for_groups(ng, lru_group, hbuf_s[...], unroll=2 * UNROLL)
        yc_ref[...] = zc_s[...].astype(MXU_DTYPE)
        yl_ref[...] = zl_s[...].astype(MXU_DTYPE)

    def part(p):
        return pl.BlockSpec((None, tb, LW), lambda c, i: (2 * p + c // STRIPS_PER_CHUNK, i, c % STRIPS_PER_CHUNK))

    strip = pl.BlockSpec((tb, LW), lambda c, i: (i, c))
    return pl.pallas_call(
        body, grid=(NS, nt),
        in_specs=[part(p) for p in range(N_PARTS)] + [
            pl.BlockSpec((PV_ROWS, LW), lambda c, i: (0, c)),
            pl.BlockSpec((1, LW, 2 * LW), lambda c, i: (c, 0, 0)),
            pl.BlockSpec(w_out.shape, lambda c, i: (0, 0))],
        out_specs=(strip, strip, strip, pl.BlockSpec(memory_space=pl.ANY)),
        out_shape=(jax.ShapeDtypeStruct((t, D_PART), MXU_DTYPE), jax.ShapeDtypeStruct((t, D_PART), MXU_DTYPE),
                   jax.ShapeDtypeStruct((t, D_PART), F32), jax.ShapeDtypeStruct((N_CHIPS,) + w_out.shape, MXU_DTYPE)),
        scratch_shapes=[pltpu.VMEM((SUBLANES, LW), F32), pltpu.VMEM((SUBLANES, LW), F32), pltpu.VMEM((SUBLANES, LW), F32),
                        pltpu.VMEM((SUBLANES, LW), F32), pltpu.VMEM((tb, LW), F32), pltpu.VMEM((tb, 2 * LW), F32),
                        pltpu.VMEM((tb, LW), F32), pltpu.VMEM((tb, LW), F32), pltpu.VMEM(w_out.shape, MXU_DTYPE),
                        pltpu.SemaphoreType.DMA, pltpu.SemaphoreType.DMA((6,)), pltpu.SemaphoreType.DMA((6,))],
        compiler_params=_cp(ARB, ARB), name="mixer_forward",
    )(proj, proj, proj, proj, proj, proj, pvec, wai, w_out)


def _mixer_backward(proj, h, dy, pvec, wai, sb_out):
    t = proj.shape[1]
    tb = 512
    ng = tb // SUBLANES
    nt = t // tb
    gpb = tb // SUBLANES

    def body(bg_ref, cg_ref, xc_ref, gc_ref, xl_ref, gl_ref, h_ref, dyc_ref, dyl_ref,
             cgh_ref, xch_ref, xlh_ref, hh_ref, pv_ref, wai_ref, so_ref,
             dp_ref, gw_ref, sv_ref, ro_ref,
             ls_s, u_s, uce_s, xle_s, he_s, gate_s, dgate_s, du_s, gbuf_s,
             p0_s, p1_s, p2_s, p3_s, p4_s, p5_s, acc_s, an_s, dvn_s, dun_s, send_sems, recv_sems):
        i = pl.program_id(1)
        first_block = i == nt - 1

        @pl.when((pl.program_id(0) == 0) & (i == 0))
        def _():
            for cp in _chip_block_copies(so_ref, ro_ref, 1, send_sems, recv_sems):
                cp.start()

        @pl.when((pl.program_id(0) == NS - 1) & (i == nt - 1))
        def _():
            for cp in _chip_block_copies(so_ref, ro_ref, 1, send_sems, recv_sems):
                cp.wait()

        @pl.when(i == 0)
        def _():
            acc_s[...] = jnp.zeros_like(acc_s)
            gw_ref[...] = jnp.zeros_like(gw_ref)
            an_s[...] = jnp.zeros_like(an_s)
            dvn_s[...] = jnp.zeros_like(dvn_s)
            dun_s[...] = jnp.zeros_like(dun_s)
            gbuf_s[...] = jnp.zeros_like(gbuf_s)

        row = lax.broadcasted_iota(jnp.int32, (SUBLANES, LW), 0)
        ls_s[...] = _log_sigmoid(_pvb(pv_ref, PV_LAM))
        keep = jnp.where(first_block, 0.0, 1.0)
        uce_s[0:SUBLANES, :] = (cgh_ref[...] * xch_ref[...]) * keep
        xle_s[0:SUBLANES, :] = xlh_ref[...] * keep
        he_s[0:SUBLANES, :] = hh_ref[...] * keep
        xle_s[SUBLANES:SUBLANES + tb, :] = xl_ref[...]
        he_s[SUBLANES:SUBLANES + tb, :] = h_ref[...]

        def recompute_group(g, carry):
            r0 = pl.multiple_of(g * SUBLANES, SUBLANES)
            sl = pl.ds(r0, SUBLANES)
            uce_s[pl.ds(r0 + SUBLANES, SUBLANES), :] = cg_ref[sl, :] * xc_ref[sl, :]
            xl = xle_s[pl.ds(r0 + SUBLANES, SUBLANES), :]
            xlp = xle_s[sl, :]
            u_s[sl, :] = _conv4(pv_ref, xl, _shift_down(xl, xlp, 1, row), _shift_down(xl, xlp, 2, row),
                                _shift_down(xl, xlp, 3, row))
            return carry

        _for_groups(ng, recompute_group, 0)
        gate_s[...] = _mm(u_s[...].astype(MXU_DTYPE), wai_ref[0])

        def acc_add(k, v):
            acc_s[k] += v

        def main_group(gi, carry):
            a_next, dv_next, g_next = carry
            g = ng - 1 - gi
            r0 = pl.multiple_of(g * SUBLANES, SUBLANES)
            sl = pl.ds(r0, SUBLANES)
            sl_e = pl.ds(r0 + SUBLANES, SUBLANES)
            lsb = ls_s[...]
            u = u_s[sl, :]
            r, ig, a, e2, mult, inv_mult = _gates(gate_s[sl, 0:LW] + _pvb(pv_ref, PV_BA),
                                                  gate_s[sl, LW:2 * LW] + _pvb(pv_ref, PV_BI), u, lsb)
            gl = gl_ref[sl, :]
            sg = _sigmoid(gl)
            s_l = gl * sg
            h8 = he_s[sl_e, :]
            hprev = _shift_down(h8, he_s[sl, :], 1, row)
            rr = lax.rsqrt(_head_mean(h8 * h8, LRU_HEAD) + RMS_EPS)
            n = h8 * rr
            dz = dyl_ref[sl, :]
            lg = _pvb(pv_ref, PV_LG)
            acc_add(PV_LG, (dz * n) * s_l)
            p5_s[sl, :] = ((dz * n) * lg) * (sg * (1.0 + gl * (1.0 - sg)))
            dn = (dz * lg) * s_l
            dh = rr * (dn - n * _head_mean(dn * n, LRU_HEAD))
            A, B = _scan8_rev(_shift_up(a, a_next, 1, row), dh, row)
            gg = B + A * jnp.broadcast_to(g_next[0:1, :], (SUBLANES, LW))
            da = gg * hprev
            iu = ig * u
            diu = gg * mult
            dla = da * a - (gg * iu) * (e2 * inv_mult)
            acc_add(PV_LAM, dla * (RG_LRU_C * r))
            dra = (dla * (RG_LRU_C * lsb)) * (r * (1.0 - r))
            dia = (diu * u) * (ig * (1.0 - ig))
            dgate_s[sl, 0:LW] = dra
            dgate_s[sl, LW:2 * LW] = dia
            acc_add(PV_BA, dra)
            acc_add(PV_BI, dia)
            du_s[sl, :] = diu * ig
            bg = bg_ref[sl, :]
            gc = gc_ref[sl, :]
            uc = uce_s[sl_e, :]
            ucp = uce_s[sl, :]
            uc1 = _shift_down(uc, ucp, 1, row)
            uc2 = _shift_down(uc, ucp, 2, row)
            v = _conv3(pv_ref, uc, uc1, uc2)
            yc = bg * v
            rrc = lax.rsqrt(_head_mean(yc * yc, CONV_HEAD) + RMS_EPS)
            nc = yc * rrc
            sgc = _sigmoid(gc)
            s_c = gc * sgc
            dzc = dyc_ref[sl, :]
            cgain = _pvb(pv_ref, PV_CG)
            acc_add(PV_CG, (dzc * nc) * s_c)
            p3_s[sl, :] = ((dzc * nc) * cgain) * (sgc * (1.0 + gc * (1.0 - sgc)))
            dnc = (dzc * cgain) * s_c
            dyc = rrc * (dnc - nc * _head_mean(dnc * nc, CONV_HEAD))
            p0_s[sl, :] = dyc * v
            dv = dyc * bg
            duc = (_pvb(pv_ref, PV_CONV_W + 2) * dv + _pvb(pv_ref, PV_CONV_W + 1) * _shift_up(dv, dv_next, 1, row)
                   + _pvb(pv_ref, PV_CONV_W) * _shift_up(dv, dv_next, 2, row))
            acc_add(PV_CONV_W + 2, dv * uc)
            acc_add(PV_CONV_W + 1, dv * uc1)
            acc_add(PV_CONV_W, dv * uc2)
            p1_s[sl, :] = duc * xc_ref[sl, :]
            p2_s[sl, :] = duc * cg_ref[sl, :]
            return a, dv, gg

        a_next, dv_next, g_next = _for_groups(ng, main_group, (an_s[...], dvn_s[...], gbuf_s[...]))
        an_s[...] = a_next
        dvn_s[...] = dv_next
        gbuf_s[...] = g_next

        dgb = dgate_s[...].astype(MXU_DTYPE)
        du_s[...] += _mm_nt(dgb, wai_ref[0])
        gw_ref[0] += _mm_tn(u_s[...].astype(MXU_DTYPE), dgb)

        def lru_conv_group(gi, du_next):
            g = ng - 1 - gi
            r0 = pl.multiple_of(g * SUBLANES, SUBLANES)
            sl = pl.ds(r0, SUBLANES)
            du = du_s[sl, :]
            xl = xle_s[pl.ds(r0 + SUBLANES, SUBLANES), :]
            xlp = xle_s[sl, :]
            acc_add(PV_LRU_B, du)
            acc_add(PV_LRU_W + 3, du * xl)
            acc_add(PV_LRU_W + 2, du * _shift_down(xl, xlp, 1, row))
            acc_add(PV_LRU_W + 1, du * _shift_down(xl, xlp, 2, row))
            acc_add(PV_LRU_W, du * _shift_down(xl, xlp, 3, row))
            p4_s[sl, :] = (((_pvb(pv_ref, PV_LRU_W + 3) * du + _pvb(pv_ref, PV_LRU_W + 2) * _shift_up(du, du_next, 1, row))
                            + _pvb(pv_ref, PV_LRU_W + 1) * _shift_up(du, du_next, 2, row))
                           + _pvb(pv_ref, PV_LRU_W) * _shift_up(du, du_next, 3, row))
            return du

        dun_s[...] = _for_groups(ng, lru_conv_group, dun_s[...])

        for p, p_s in enumerate((p0_s, p1_s, p2_s, p3_s, p4_s, p5_s)):
            dp_ref[p] = p_s[...].astype(MXU_DTYPE)

        @pl.when(first_block)
        def _():
            sv_ref[...] = jnp.zeros_like(sv_ref)
            for k in range(N_ACC):
                tot = jnp.sum(acc_s[k], axis=0, keepdims=True)
                if k == PV_LAM:
                    tot = tot / (1.0 + jnp.exp(pv_ref[PV_LAM:PV_LAM + 1, :]))
                sv_ref[k:k + 1, :] = tot

    def part(p):
        return pl.BlockSpec((None, tb, LW), lambda c, i: (2 * p + c // STRIPS_PER_CHUNK, nt - 1 - i, c % STRIPS_PER_CHUNK))

    def halo(p):
        return pl.BlockSpec((None, SUBLANES, LW), lambda c, i: (2 * p + c // STRIPS_PER_CHUNK,
                                                                jnp.maximum((nt - 1 - i) * gpb - 1, 0), c % STRIPS_PER_CHUNK))

    strip = pl.BlockSpec((tb, LW), lambda c, i: (nt - 1 - i, c))
    big = pltpu.VMEM((tb, LW), F32)
    big_e = pltpu.VMEM((tb + SUBLANES, LW), F32)
    wide = pltpu.VMEM((tb, 2 * LW), F32)
    small = pltpu.VMEM((SUBLANES, LW), F32)
    outs = pl.pallas_call(
        body, grid=(NS, nt),
        in_specs=[part(p) for p in range(N_PARTS)] + [
            strip, strip, pl.BlockSpec((tb, LW), lambda c, i: (nt - 1 - i, NS + c)),
            halo(1), halo(2), halo(4),
            pl.BlockSpec((SUBLANES, LW), lambda c, i: (jnp.maximum((nt - 1 - i) * gpb - 1, 0), c)),
            pl.BlockSpec((PV_ROWS, LW), lambda c, i: (0, c)),
            pl.BlockSpec((1, LW, 2 * LW), lambda c, i: (c, 0, 0)),
            pl.BlockSpec(memory_space=pl.ANY)],
        out_specs=(pl.BlockSpec((N_PARTS, tb, LW), lambda c, i: (0, nt - 1 - i, c)),
                   pl.BlockSpec((1, LW, 2 * LW), lambda c, i: (c, 0, 0)),
                   pl.BlockSpec((PV_ROWS, LW), lambda c, i: (0, c)),
                   pl.BlockSpec(memory_space=pl.ANY)),
        out_shape=(jax.ShapeDtypeStruct((N_PARTS, t, D_PART), MXU_DTYPE),
                   jax.ShapeDtypeStruct((NS, LW, 2 * LW), F32), jax.ShapeDtypeStruct((PV_ROWS, D_PART), F32),
                   _chip_blocks_shape(sb_out, 1)),
        scratch_shapes=[small, big, big_e, big_e, big_e, wide, wide, big, small,
                        big, big, big, big, big, big, pltpu.VMEM((N_ACC, SUBLANES, LW), F32), small, small, small,
                        pltpu.SemaphoreType.DMA((3,)), pltpu.SemaphoreType.DMA((3,))],
        compiler_params=_cp(ARB, ARB), name="mixer_backward",
    )(proj, proj, proj, proj, proj, proj, h, dy, dy, proj, proj, proj, h, pvec, wai, sb_out)
    return outs


def _adamw(w, g, m, v):
    m = ADAM_B1 * m + (1.0 - ADAM_B1) * g
    v = ADAM_B2 * v + (1.0 - ADAM_B2) * (g * g)
    m_hat = m / (1.0 - ADAM_B1 ** ADAM_STEP)
    v_hat = v / (1.0 - ADAM_B2 ** ADAM_STEP)
    delta = -ADAM_LR * (m_hat / (jnp.sqrt(v_hat) + ADAM_EPS) + ADAM_WD * w)
    return delta, m, v


def _adam_w_in(w, m, v, g3):
    rows, cols = w.shape
    tr = 128

    def body(w_ref, m_ref, v_ref, g_ref, go_ref, d_ref, mo_ref, vo_ref):
        for s in range(CHUNKS_PER_BLOCK):
            cs = slice(CHUNK * s, CHUNK * (s + 1))
            g = g_ref[s]
            d, mn, vn = _adamw(w_ref[:, cs], g, m_ref[:, cs], v_ref[:, cs])
            go_ref[:, cs] = g
            d_ref[:, cs] = d
            mo_ref[:, cs] = mn
            vo_ref[:, cs] = vn

    blk = pl.BlockSpec((tr, cols), lambda i: (i, 0))
    return pl.pallas_call(
        body, grid=(rows // tr,),
        in_specs=[blk, blk, blk, pl.BlockSpec((CHUNKS_PER_BLOCK, tr, CHUNK), lambda i: (0, i, 0))],
        out_specs=(blk,) * 4, out_shape=(jax.ShapeDtypeStruct(w.shape, F32),) * 4,
        compiler_params=_cp(ARB), name="adam_w_in",
    )(w, m, v, g3)


def _adam_w_out(w, m, v, g):
    rows, cols = w.shape
    tr = 128

    def body(w_ref, m_ref, v_ref, g_ref, d_ref, mo_ref, vo_ref):
        d_ref[...], mo_ref[...], vo_ref[...] = _adamw(w_ref[...], g_ref[...], m_ref[...], v_ref[...])

    blk = pl.BlockSpec((tr, cols), lambda i: (i, 0))
    return pl.pallas_call(
        body, grid=(rows // tr,), in_specs=[blk] * 4, out_specs=(blk,) * 3,
        out_shape=(jax.ShapeDtypeStruct(w.shape, F32),) * 3,
        compiler_params=_cp(ARB), name="adam_w_out",
    )(w, m, v, g)


def _adam_small(ws, ms, vs, gs):
    n = len(ws)

    def body(*refs):
        w_r, m_r, v_r, g_r = refs[0:n], refs[n:2 * n], refs[2 * n:3 * n], refs[3 * n:4 * n]
        d_o, m_o, v_o = refs[4 * n:5 * n], refs[5 * n:6 * n], refs[6 * n:7 * n]
        for j in range(n):
            d_o[j][...], m_o[j][...], v_o[j][...] = _adamw(w_r[j][...], g_r[j][...], m_r[j][...], v_r[j][...])

    vm = pl.BlockSpec(memory_space=pltpu.VMEM)
    shapes = tuple(jax.ShapeDtypeStruct(w.shape, F32) for w in ws)
    outs = pl.pallas_call(
        body, in_specs=[vm] * (4 * n), out_specs=(vm,) * (3 * n), out_shape=shapes * 3,
        compiler_params=_cp(), name="adam_small",
    )(*ws, *ms, *vs, *gs)
    return outs[0:n], outs[n:2 * n], outs[2 * n:3 * n]


def _block_diag_strips(w):
    w4 = w.reshape(NS, HEADS_PER_STRIP, LRU_HEAD, LRU_HEAD)
    rows = [jnp.pad(w4[:, hh], ((0, 0), (0, 0), (LRU_HEAD * hh, LW - LRU_HEAD * (hh + 1)))) for hh in range(HEADS_PER_STRIP)]
    return jnp.concatenate(rows, axis=1)


def _strip_diag_blocks(g):
    g5 = g.reshape(NS, HEADS_PER_STRIP, LRU_HEAD, HEADS_PER_STRIP, LRU_HEAD)
    return jnp.stack([g5[:, hh, :, hh, :] for hh in range(HEADS_PER_STRIP)], axis=1).reshape(NS * HEADS_PER_STRIP, LRU_HEAD, LRU_HEAD)


def kernel(x, ln_g, w_in, conv_w, lru_conv_w, lru_conv_b, w_a, b_a, w_i, b_i, lam, conv_out_g, lru_out_g, w_out, final_g, loss_target, m_ln_g, m_w_in, m_conv_w, m_lru_conv_w, m_lru_conv_b, m_w_a, m_b_a, m_w_i, m_b_i, m_lam, m_conv_out_g, m_lru_out_g, m_w_out, m_final_g, v_ln_g, v_w_in, v_conv_w, v_lru_conv_w, v_lru_conv_b, v_w_a, v_b_a, v_w_i, v_b_i, v_lam, v_conv_out_g, v_lru_out_g, v_w_out, v_final_g):
    xi, yi, ci = lax.axis_index("x"), lax.axis_index("y"), lax.axis_index("c")
    k = 2 * xi + yi
    t = x.shape[1]
    x2 = x.reshape(t, D_MODEL)
    tgt2 = loss_target.reshape(t, D_MODEL)
    row = lambda a: a.reshape(1, -1)

    small = jnp.concatenate([conv_w, lru_conv_w, jnp.zeros((1, conv_w.shape[1]), F32)], axis=0)
    proj, xn, w12, sm4 = _gather_in_projection(x2, row(ln_g), w_in, small)
    convs = jnp.transpose(sm4, (1, 0, 2)).reshape(SUBLANES, D_PART)
    pvec = jnp.concatenate(
        [convs[0:7], row(lru_conv_b), row(b_a), row(b_i), row(lam), row(conv_out_g), row(lru_out_g),
         jnp.zeros((PV_ROWS - N_ACC, D_PART), F32)], axis=0)
    wai = jnp.concatenate([_block_diag_strips(w_a), _block_diag_strips(w_i)], axis=2).astype(MXU_DTYPE)

    c_arr = jnp.reshape(ci, (1,)).astype(jnp.int32)
    kc_arr = jnp.stack([k, ci]).astype(jnp.int32)
    yc, yl, h, wo4 = _mixer_forward(proj, pvec, wai, w_out)
    wo = wo4.reshape(2 * D_PART, D_MODEL)
    do, dob, dy, st_out = _out_projection_loss(yc, yl, x2, tgt2, wo, row(final_g))
    go4, go4b = _w_out_grad(yc, yl, dob)
    s_out, sb_out = _add_own_half(go4, _exchange_sibling_halves(go4b, "exchange_sibling_halves_out"), c_arr, "add_own_half_out")
    dproj, g_wai, svec, r2o = _mixer_backward(proj, h, dy, pvec, wai, sb_out)
    gwa = _strip_diag_blocks(g_wai[:, :, 0:LW]).reshape(LRU_HEAD, D_PART)
    gwi = _strip_diag_blocks(g_wai[:, :, LW:2 * LW]).reshape(LRU_HEAD, D_PART)
    g12, g12b, red = _w_in_grad(xn, dproj, jnp.concatenate([svec, st_out, gwa, gwi], axis=0))
    s_in, sb_in = _add_own_half(g12, _exchange_sibling_halves(g12b, "exchange_sibling_halves_in"), c_arr, "add_own_half_in")
    grad_x, st_in, r2i = _input_grad(dproj, w12, x2, do, row(ln_g), sb_in)
    f_in = _sum_chip_blocks(s_in, r2i, kc_arr, CHUNKS_PER_BLOCK, "sum_chip_blocks_in")
    f_out = _sum_chip_blocks(s_out, r2o, kc_arr, 1, "sum_chip_blocks_out")
    f_in, f_out = _swap_sibling_halves(f_in, f_out)

    red_ln = _sum_over_devices(st_in)
    r_out = PV_ROWS
    r_wa = PV_ROWS + SUBLANES
    r_wi = r_wa + LRU_HEAD
    loss = red[r_out + 1, 0]

    g_w_in, d_w_in, nm_w_in, nv_w_in = _adam_w_in(w_in, m_w_in, v_w_in, f_in)
    g_w_out = f_out[0]
    d_w_out, nm_w_out, nv_w_out = _adam_w_out(w_out, m_w_out, v_w_out, g_w_out)

    ncol = conv_w.shape[1]
    conv_cols = lax.dynamic_slice(red, (0, k * ncol), (SUBLANES, ncol))
    g_small = {
        "ln_g": red_ln[0], "conv_w": conv_cols[0:3], "lru_conv_w": conv_cols[3:7], "lru_conv_b": red[PV_LRU_B],
        "w_a": red[r_wa:r_wa + LRU_HEAD].reshape(w_a.shape), "b_a": red[PV_BA],
        "w_i": red[r_wi:r_wi + LRU_HEAD].reshape(w_i.shape), "b_i": red[PV_BI], "lam": red[PV_LAM],
        "conv_out_g": red[PV_CG], "lru_out_g": red[PV_LG], "final_g": red[r_out],
    }
    w_small = {"ln_g": ln_g, "conv_w": conv_w, "lru_conv_w": lru_conv_w, "lru_conv_b": lru_conv_b, "w_a": w_a, "b_a": b_a,
               "w_i": w_i, "b_i": b_i, "lam": lam, "conv_out_g": conv_out_g, "lru_out_g": lru_out_g, "final_g": final_g}
    m_small = {"ln_g": m_ln_g, "conv_w": m_conv_w, "lru_conv_w": m_lru_conv_w, "lru_conv_b": m_lru_conv_b, "w_a": m_w_a,
               "b_a": m_b_a, "w_i": m_w_i, "b_i": m_b_i, "lam": m_lam, "conv_out_g": m_conv_out_g,
               "lru_out_g": m_lru_out_g, "final_g": m_final_g}
    v_small = {"ln_g": v_ln_g, "conv_w": v_conv_w, "lru_conv_w": v_lru_conv_w, "lru_conv_b": v_lru_conv_b, "w_a": v_w_a,
               "b_a": v_b_a, "w_i": v_w_i, "b_i": v_b_i, "lam": v_lam, "conv_out_g": v_conv_out_g,
               "lru_out_g": v_lru_out_g, "final_g": v_final_g}
    names = list(w_small)
    as2d = lambda a: a.reshape(1, -1) if a.ndim == 1 else a
    d_s, m_s, v_s = _adam_small([as2d(w_small[n]) for n in names], [as2d(m_small[n]) for n in names],
                                [as2d(v_small[n]) for n in names], [as2d(g_small[n]) for n in names])
    back = lambda n, a: a.reshape(w_small[n].shape)
    grads = {n: g_small[n] for n in names}
    deltas = {n: back(n, a) for n, a in zip(names, d_s)}
    new_m = {n: back(n, a) for n, a in zip(names, m_s)}
    new_v = {n: back(n, a) for n, a in zip(names, v_s)}
    grads["w_in"], deltas["w_in"], new_m["w_in"], new_v["w_in"] = g_w_in, d_w_in, nm_w_in, nv_w_in
    grads["w_out"], deltas["w_out"], new_m["w_out"], new_v["w_out"] = g_w_out, d_w_out, nm_w_out, nv_w_out

    order = ["ln_g", "w_in", "conv_w", "lru_conv_w", "lru_conv_b", "w_a", "b_a", "w_i", "b_i", "lam", "conv_out_g",
             "lru_out_g", "w_out", "final_g"]
    return (loss, grad_x.reshape(x.shape), *[grads[n] for n in order], *[deltas[n] for n in order],
            *[new_m[n] for n in order], *[new_v[n] for n in order])
```

```python
import functools

import jax
import jax.numpy as jnp
from jax import lax
from jax.experimental import pallas as pl
from jax.experimental.pallas import tpu as pltpu

F32 = jnp.float32
MXU_DTYPE = jnp.bfloat16

D_MODEL = 1024
D_PART = 1024
N_PARTS = 6
CHUNK = 512
CHUNKS_PER_BLOCK = 3
N_CHUNKS = 12
N_CHIPS = 4
SUBLANES = 8
LANES = 128
LW = 256
UNROLL = 8
NS = D_PART // LW
STRIPS_PER_CHUNK = CHUNK // LW
CONV_HEAD = 128
LRU_HEAD = 64
HEADS_PER_STRIP = LW // LRU_HEAD
RMS_EPS = 1e-6
RG_LRU_C = 8.0
ADAM_LR = 0.001
ADAM_B1 = 0.9
ADAM_B2 = 0.999
ADAM_EPS = 1e-08
ADAM_WD = 0.01
ADAM_STEP = 10

PV_CONV_W = 0
PV_LRU_W = 3
PV_LRU_B = 7
PV_BA = 8
PV_BI = 9
PV_LAM = 10
PV_CG = 11
PV_LG = 12
PV_ROWS = 16
N_ACC = 13

SLAB = 128
MESH = pl.DeviceIdType.MESH
VMEM_LIMIT = 56 * 1024 * 1024
ARB = "arbitrary"


def _cp(*sem, **kw):
    return pltpu.CompilerParams(dimension_semantics=sem or None, vmem_limit_bytes=VMEM_LIMIT, **kw)


def _mm(a, b):
    return jnp.dot(a, b, preferred_element_type=F32)


def _mm_nt(a, b):
    return lax.dot_general(a, b, (((1,), (1,)), ((), ())), preferred_element_type=F32)


def _mm_tn(a, b):
    return lax.dot_general(a, b, (((0,), (0,)), ((), ())), preferred_element_type=F32)


def _sigmoid(x):
    return 0.5 * jnp.tanh(0.5 * x) + 0.5


def _log_sigmoid(x):
    z = jnp.exp(-jnp.abs(x))
    u = 1.0 + z
    log1p = jnp.where(u == 1.0, z, jnp.log(u) * z / (u - 1.0))
    return jnp.minimum(x, 0.0) - log1p


def _head_mean(z, head):
    out = []
    for k in range(z.shape[1] // LANES):
        zk = z[:, LANES * k:LANES * (k + 1)]
        if head == LANES:
            m = jnp.sum(zk, axis=-1, keepdims=True) * (1.0 / head)
            out.append(jnp.broadcast_to(m, zk.shape))
        else:
            lo = lax.broadcasted_iota(jnp.int32, zk.shape, 1) < head
            s_lo = jnp.sum(jnp.where(lo, zk, 0.0), axis=-1, keepdims=True)
            s_hi = jnp.sum(jnp.where(lo, 0.0, zk), axis=-1, keepdims=True)
            out.append(jnp.where(lo, s_lo, s_hi) * (1.0 / head))
    return jnp.concatenate(out, axis=1)


def _shift_down(cur, prev, d, row):
    return pltpu.roll(jnp.where(row < SUBLANES - d, cur, prev), d, 0)


def _shift_up(cur, nxt, d, row):
    return pltpu.roll(jnp.where(row >= d, cur, nxt), SUBLANES - d, 0)


def _scan8_fwd(a, b, row):
    A, B = a, b
    for d in (1, 2, 4):
        m = row >= d
        a_s = jnp.where(m, pltpu.roll(A, d, 0), 1.0)
        b_s = jnp.where(m, pltpu.roll(B, d, 0), 0.0)
        B = A * b_s + B
        A = A * a_s
    return A, B


def _scan8_rev(a, b, row):
    A, B = a, b
    for d in (1, 2, 4):
        m = row < SUBLANES - d
        a_s = jnp.where(m, pltpu.roll(A, SUBLANES - d, 0), 1.0)
        b_s = jnp.where(m, pltpu.roll(B, SUBLANES - d, 0), 0.0)
        B = A * b_s + B
        A = A * a_s
    return A, B


def _decay(r, lsb):
    la = (RG_LRU_C * r) * lsb
    a = jnp.exp(la)
    e2 = a * a
    em = -jnp.tanh(la) * (1.0 + e2)
    inv_mult = lax.rsqrt(em)
    return a, e2, em * inv_mult, inv_mult


def _mesh_pos():
    x, y, c = lax.axis_index("x"), lax.axis_index("y"), lax.axis_index("c")
    chips = [(1 - x, y), (x, 1 - y), (1 - x, 1 - y)]
    return x, y, c, chips


def _gather_in_projection(x, ln_g, w_in, small):
    t = x.shape[0]
    rb_x = 512
    rb_mm = 1024
    n_mm = t // rb_mm
    half = w_in.shape[0] // 2
    n_ici = 3 * CHUNKS_PER_BLOCK

    def body(x_hbm, g_ref, wi_ref, sm_ref, proj_hbm, xn_ref, w12_ref, sm4_ref,
             xbuf, obuf, x_sems, o_sems, send_sems, recv_sems):
        x_, y_, c, chips = _mesh_pos()
        k = 2 * x_ + y_
        sib = (x_, y_, 1 - c)
        my_rows = pl.ds(pl.multiple_of(half * c, half), half)
        sib_rows = pl.ds(pl.multiple_of(half * (1 - c), half), half)

        sm4_ref[k] = sm_ref[...]

        def remote(ref, sem, to):
            return pltpu.make_async_remote_copy(src_ref=ref, dst_ref=ref, send_sem=send_sems.at[sem],
                                                recv_sem=recv_sems.at[sem], device_id=to, device_id_type=MESH)

        def chunk_of(chip, s):
            return CHUNKS_PER_BLOCK * (2 * chip[0] + chip[1]) + s

        ici = lambda m, s: 3 * s + m
        fwd = lambda m, s: n_ici + 3 * s + m
        sml = lambda m: 2 * n_ici + m

        sends = []
        for s in range(CHUNKS_PER_BLOCK):
            w12_ref[chunk_of((x_, y_), s)] = wi_ref[:, CHUNK * s:CHUNK * (s + 1)].astype(MXU_DTYPE)
            for m, chip in enumerate(chips):
                sends.append(remote(w12_ref.at[chunk_of((x_, y_), s), my_rows, :], ici(m, s), (*chip, c)))
                sends[-1].start()
        for m, chip in enumerate(chips):
            sends.append(remote(sm4_ref.at[k], sml(m), (*chip, c)))
            sends[-1].start()

        def x_copy(rb, slot):
            return pltpu.make_async_copy(x_hbm.at[pl.ds(rb * rb_x, rb_x), :], xbuf.at[slot], x_sems.at[slot])

        x_copy(0, 0).start()
        for rb in range(t // rb_x):
            slot = rb % 2
            x_copy(rb, slot).wait()
            if rb + 1 < t // rb_x:
                x_copy(rb + 1, 1 - slot).start()

            def norm_slab(sl, carry, rb=rb, slot=slot):
                xf = xbuf[slot, pl.ds(pl.multiple_of(sl * SLAB, SLAB), SLAB), :]
                r = lax.rsqrt(jnp.mean(xf * xf, axis=-1, keepdims=True) + RMS_EPS)
                xn_ref[pl.ds(pl.multiple_of(rb * rb_x + sl * SLAB, SLAB), SLAB), :] = ((xf * r) * g_ref[...]).astype(MXU_DTYPE)
                return carry

            lax.fori_loop(0, rb_x // SLAB, norm_slab, 0)

        def out_copy(q, i, slot):
            return pltpu.make_async_copy(obuf.at[slot], proj_hbm.at[q, pl.ds(pl.multiple_of(i * rb_mm, rb_mm), rb_mm), :],
                                         o_sems.at[slot])

        def project(q, very_first):
            def row_block(i, carry):
                slot = i % 2

                def wait_buffer():
                    out_copy(q, i, slot).wait()

                if very_first:
                    pl.when(i >= 2)(wait_buffer)
                else:
                    wait_buffer()
                obuf[slot] = _mm(xn_ref[pl.ds(pl.multiple_of(i * rb_mm, rb_mm), rb_mm), :], w12_ref[q])
                out_copy(q, i, slot).start()
                return carry

            lax.fori_loop(0, n_mm, row_block, 0)

        for s in range(CHUNKS_PER_BLOCK):
            project(chunk_of((x_, y_), s), very_first=(s == 0))

        order = [(m, s) for s in range(CHUNKS_PER_BLOCK) for m in range(3)]
        forwards = []
        for j, (m, s) in enumerate(order):
            q = chunk_of(chips[m], s)
            remote(w12_ref.at[q, my_rows, :], ici(m, s), sib).wait_recv()
            f = remote(w12_ref.at[q, my_rows, :], fwd(m, s), sib)
            f.start()
            forwards.append(f)
            if j > 0:
                pm, ps = order[j - 1]
                pq = chunk_of(chips[pm], ps)
                remote(w12_ref.at[pq, sib_rows, :], fwd(pm, ps), sib).wait_recv()
                project(pq, very_first=False)
        pm, ps = order[-1]
        pq = chunk_of(chips[pm], ps)
        remote(w12_ref.at[pq, sib_rows, :], fwd(pm, ps), sib).wait_recv()
        project(pq, very_first=False)

        for m, chip in enumerate(chips):
            remote(sm4_ref.at[2 * chip[0] + chip[1]], sml(m), sib).wait_recv()
        for cp in sends + forwards:
            cp.wait_send()
        for slot in range(2):
            out_copy(0, slot, slot).wait()

    assert n_mm % 2 == 0 and n_mm >= 2
    vm = pl.BlockSpec(memory_space=pltpu.VMEM)
    hbm = pl.BlockSpec(memory_space=pl.ANY)
    n_sems = 2 * n_ici + 3
    return pl.pallas_call(
        body,
        out_shape=(jax.ShapeDtypeStruct((N_CHUNKS, t, CHUNK), F32), jax.ShapeDtypeStruct((t, D_MODEL), MXU_DTYPE),
                   jax.ShapeDtypeStruct((N_CHUNKS, w_in.shape[0], CHUNK), MXU_DTYPE),
                   jax.ShapeDtypeStruct((N_CHIPS,) + small.shape, F32)),
        in_specs=[hbm, vm, vm, vm], out_specs=(hbm, vm, vm, vm),
        scratch_shapes=[pltpu.VMEM((2, rb_x, D_MODEL), F32), pltpu.VMEM((2, rb_mm, CHUNK), F32),
                        pltpu.SemaphoreType.DMA((2,)), pltpu.SemaphoreType.DMA((2,)),
                        pltpu.SemaphoreType.DMA((n_sems,)), pltpu.SemaphoreType.DMA((n_sems,))],
        compiler_params=_cp(), name="gather_in_projection",
    )(x, ln_g, w_in, small)


def _sum_over_devices(v):
    n_dev = 8

    def body(v_ref, o_ref, slots, send_sems, recv_sems):
        x, y, c, _ = _mesh_pos()
        me = 4 * x + 2 * y + c
        slots[me] = v_ref[...]
        cps = []
        for d in range(1, n_dev):
            peer = (1 - x if d & 4 else x, 1 - y if d & 2 else y, 1 - c if d & 1 else c)
            cps.append(pltpu.make_async_remote_copy(src_ref=slots.at[me], dst_ref=slots.at[me], send_sem=send_sems.at[d - 1],
                                                    recv_sem=recv_sems.at[d - 1], device_id=peer, device_id_type=MESH))
        for cp in cps:
            cp.start()
        for cp in cps:
            cp.wait()
        total = slots[0]
        for dev in range(1, n_dev):
            total = total + slots[dev]
        o_ref[...] = total

    vm = pl.BlockSpec(memory_space=pltpu.VMEM)
    return pl.pallas_call(
        body, out_shape=jax.ShapeDtypeStruct(v.shape, F32), in_specs=[vm], out_specs=vm,
        scratch_shapes=[pltpu.VMEM((n_dev,) + v.shape, F32), pltpu.SemaphoreType.DMA((n_dev - 1,)),
                        pltpu.SemaphoreType.DMA((n_dev - 1,))],
        compiler_params=_cp(), name="sum_over_devices",
    )(v)


def _allreduce_behind(step, when, in_ref, acc_s, rbufs, out_ref, send_sems, recv_sems):
    x, y, c, _ = _mesh_pos()
    peers = [(x, y, 1 - c), (1 - x, y, c), (x, 1 - y, c)]

    def exchange(ph):
        return pltpu.make_async_remote_copy(src_ref=acc_s, dst_ref=rbufs[ph], send_sem=send_sems.at[ph],
                                            recv_sem=recv_sems.at[ph], device_id=peers[ph], device_id_type=MESH)

    @pl.when(step == when[0])
    def _():
        acc_s[...] = in_ref[...]
        exchange(0).start()

    for ph in (1, 2):
        @pl.when(step == when[ph])
        def _(ph=ph):
            exchange(ph - 1).wait()
            acc_s[...] = acc_s[...] + rbufs[ph - 1][...]
            exchange(ph).start()

    @pl.when(step == when[3])
    def _():
        exchange(2).wait()
        out_ref[...] = acc_s[...] + rbufs[2][...]


def _exchange_sibling_halves(g, name):
    n, rows, cols = g.shape
    half = rows // 2

    def body(g_ref, r_ref, send_sem, recv_sem):
        x, y, c, _ = _mesh_pos()
        cp = pltpu.make_async_remote_copy(src_ref=g_ref.at[:, pl.ds(pl.multiple_of(half * (1 - c), half), half), :],
                                          dst_ref=r_ref, send_sem=send_sem, recv_sem=recv_sem,
                                          device_id=(x, y, 1 - c), device_id_type=MESH)
        cp.start()
        cp.wait()

    hbm = pl.BlockSpec(memory_space=pl.ANY)
    return pl.pallas_call(
        body, out_shape=jax.ShapeDtypeStruct((n, half, cols), g.dtype), in_specs=[hbm], out_specs=hbm,
        scratch_shapes=[pltpu.SemaphoreType.DMA, pltpu.SemaphoreType.DMA],
        compiler_params=_cp(), name=name,
    )(g)


def _add_own_half(g, r, c_arr, name):
    n, rr, cc = r.shape

    def body(c_ref, g_ref, r_ref, o_ref, ob_ref):
        s = g_ref[...] + r_ref[...].astype(F32)
        o_ref[...] = s
        ob_ref[...] = s.astype(jnp.bfloat16)

    blk = pl.BlockSpec((1, rr, cc), lambda q, c_ref: (q, 0, 0))
    return pl.pallas_call(
        body, out_shape=(jax.ShapeDtypeStruct(r.shape, F32), jax.ShapeDtypeStruct(r.shape, jnp.bfloat16)),
        grid_spec=pltpu.PrefetchScalarGridSpec(
            num_scalar_prefetch=1, grid=(n,),
            in_specs=[pl.BlockSpec((1, rr, cc), lambda q, c_ref: (q, c_ref[0], 0)), blk],
            out_specs=(blk, blk)),
        compiler_params=_cp(ARB), name=name,
    )(c_arr, g, r)


def _chip_block_copies(s_ref, r_ref, n_sub, send_sems, recv_sems):
    x, y, c, chips = _mesh_pos()
    cps = []
    for m, chip in enumerate(chips):
        kk = 2 * chip[0] + chip[1]
        cps.append(pltpu.make_async_remote_copy(
            src_ref=s_ref.at[pl.ds(n_sub * kk, n_sub)], dst_ref=r_ref.at[m],
            send_sem=send_sems.at[m], recv_sem=recv_sems.at[m], device_id=(*chip, c), device_id_type=MESH))
    return cps


def _gather_w_out(step, n_steps, wo_ref, wob_s, wo4_ref, local_sem, send_sems, recv_sems):
    x, y, c, chips = _mesh_pos()
    sib = (x, y, 1 - c)
    half = wo_ref.shape[0] // 2

    def rows(core):
        return pl.ds(pl.multiple_of(half * core, half), half)

    def block_half(chip, core):
        return wo4_ref.at[2 * chip[0] + chip[1], rows(core), :]

    def remote(src, dst, sem, to):
        return pltpu.make_async_remote_copy(src_ref=src, dst_ref=dst, send_sem=send_sems.at[sem], recv_sem=recv_sems.at[sem],
                                            device_id=to, device_id_type=MESH)

    local = pltpu.make_async_copy(wob_s, wo4_ref.at[2 * x + y], local_sem)
    ici = [remote(wob_s.at[rows(c), :], block_half((x, y), c), m, (*chip, c)) for m, chip in enumerate(chips)]
    fwd = [remote(block_half(chip, c), block_half(chip, c), 3 + m, sib) for m, chip in enumerate(chips)]

    @pl.when(step == 0)
    def _():
        wob_s[...] = wo_ref[...].astype(MXU_DTYPE)
        local.start()
        for cp in ici:
            cp.start()

    @pl.when(step == n_steps // 2)
    def _():
        for m, chip in enumerate(chips):
            remote(block_half(chip, c), block_half(chip, c), m, sib).wait_recv()
            fwd[m].start()

    @pl.when(step == n_steps - 1)
    def _():
        for m, chip in enumerate(chips):
            remote(block_half(chip, 1 - c), block_half(chip, 1 - c), 3 + m, sib).wait_recv()
        for cp in ici + fwd:
            cp.wait_send()
        local.wait()


def _chip_blocks_shape(s, n_sub):
    return jax.ShapeDtypeStruct((3, n_sub) + s.shape[1:], s.dtype)


def _sum_chip_blocks(s, r, kc_arr, n_sub, name):
    _, rr, cc = s.shape

    def body(kc_ref, s_ref, r_ref, o_ref):
        o_ref[...] = ((s_ref[...] + r_ref[0].astype(F32)) + r_ref[1].astype(F32)) + r_ref[2].astype(F32)

    return pl.pallas_call(
        body, out_shape=jax.ShapeDtypeStruct((n_sub, 2 * rr, cc), F32),
        grid_spec=pltpu.PrefetchScalarGridSpec(
            num_scalar_prefetch=1, grid=(n_sub,),
            in_specs=[pl.BlockSpec((1, rr, cc), lambda q, kc: (n_sub * kc[0] + q, 0, 0)),
                      pl.BlockSpec((3, 1, rr, cc), lambda q, kc: (0, q, 0, 0))],
            out_specs=pl.BlockSpec((1, rr, cc), lambda q, kc: (q, kc[1], 0))),
        compiler_params=_cp(ARB), name=name,
    )(kc_arr, s, r)


def _swap_sibling_halves(f_in, f_out):
    hi, ho = f_in.shape[1] // 2, f_out.shape[1] // 2

    def body(fi_in, fo_in, fi_ref, fo_ref, send_sems, recv_sems):
        del fi_in, fo_in
        x, y, c, _ = _mesh_pos()
        sib = (x, y, 1 - c)
        si = fi_ref.at[:, pl.ds(pl.multiple_of(hi * c, hi), hi), :]
        so = fo_ref.at[:, pl.ds(pl.multiple_of(ho * c, ho), ho), :]
        cps = [
            pltpu.make_async_remote_copy(src_ref=si, dst_ref=si, send_sem=send_sems.at[0], recv_sem=recv_sems.at[0],
                                         device_id=sib, device_id_type=MESH),
            pltpu.make_async_remote_copy(src_ref=so, dst_ref=so, send_sem=send_sems.at[1], recv_sem=recv_sems.at[1],
                                         device_id=sib, device_id_type=MESH),
        ]
        for cp in cps:
            cp.start()
        for cp in cps:
            cp.wait()

    hbm = pl.BlockSpec(memory_space=pl.ANY)
    return pl.pallas_call(
        body,
        out_shape=(jax.ShapeDtypeStruct(f_in.shape, F32), jax.ShapeDtypeStruct(f_out.shape, F32)),
        in_specs=[hbm, hbm], out_specs=(hbm, hbm), input_output_aliases={0: 0, 1: 1},
        scratch_shapes=[pltpu.SemaphoreType.DMA((2,)), pltpu.SemaphoreType.DMA((2,))],
        compiler_params=_cp(), name="swap_sibling_halves",
    )(f_in, f_out)


def _out_projection_loss(yc, yl, x, target, wo, final_g):
    t = x.shape[0]
    tm = 512

    def body(yc_ref, yl_ref, x_ref, t_ref, wo_ref, fg_ref, do_ref, dob_ref, dy_ref, st_ref, y_wo):
        @pl.when(pl.program_id(0) == 0)
        def _():
            st_ref[...] = jnp.zeros_like(st_ref)

        y_wo[...] = _mm(yc_ref[...], wo_ref[0:D_PART, :]) + _mm(yl_ref[...], wo_ref[D_PART:2 * D_PART, :])

        def norm_loss_slab(s, carry):
            g_sum, loss_sum = carry
            rows = pl.ds(pl.multiple_of(s * SLAB, SLAB), SLAB)
            o = x_ref[rows, :] + y_wo[rows, :]
            r2 = lax.rsqrt(jnp.mean(o * o, axis=-1, keepdims=True) + RMS_EPS)
            ohat = o * r2
            fg = fg_ref[...]
            diff = ohat * fg - t_ref[rows, :]
            dout = diff * (1.0 / D_MODEL)
            gp = dout * fg
            do = r2 * (gp - ohat * jnp.mean(gp * ohat, axis=-1, keepdims=True))
            do_ref[rows, :] = do
            dob_ref[rows, :] = do.astype(MXU_DTYPE)
            loss = 0.5 * jnp.sum(jnp.sum(diff * diff, axis=-1, keepdims=True) * (1.0 / D_MODEL), axis=0, keepdims=True)
            return g_sum + jnp.sum(dout * ohat, axis=0, keepdims=True), loss_sum + loss

        g_sum, loss_sum = lax.fori_loop(0, tm // SLAB, norm_loss_slab,
                                        (jnp.zeros((1, D_MODEL), F32), jnp.zeros((1, 1), F32)))
        st_ref[0:1, :] += g_sum
        st_ref[1:2, :] += jnp.broadcast_to(loss_sum, (1, D_MODEL))
        dy_ref[...] = _mm_nt(dob_ref[...], wo_ref[...])

    row = lambda i: (i, 0)
    fix = lambda i: (0, 0)
    return pl.pallas_call(
        body, grid=(t // tm,),
        in_specs=[pl.BlockSpec((tm, D_PART), row), pl.BlockSpec((tm, D_PART), row),
                  pl.BlockSpec((tm, D_MODEL), row), pl.BlockSpec((tm, D_MODEL), row),
                  pl.BlockSpec((2 * D_PART, D_MODEL), fix), pl.BlockSpec((1, D_MODEL), fix)],
        out_specs=(pl.BlockSpec((tm, D_MODEL), row), pl.BlockSpec((tm, D_MODEL), row),
                   pl.BlockSpec((tm, 2 * D_PART), row), pl.BlockSpec((SUBLANES, D_MODEL), fix)),
        out_shape=(jax.ShapeDtypeStruct((t, D_MODEL), F32), jax.ShapeDtypeStruct((t, D_MODEL), MXU_DTYPE),
                   jax.ShapeDtypeStruct((t, 2 * D_PART), F32), jax.ShapeDtypeStruct((SUBLANES, D_MODEL), F32)),
        scratch_shapes=[pltpu.VMEM((tm, D_MODEL), F32)],
        compiler_params=_cp(ARB), name="out_projection_loss",
    )(yc, yl, x, target, wo, final_g)


def _input_grad(dproj, w12, x, do, ln_g, sb_in):
    t = x.shape[0]
    tm = 1024

    def body(dp_ref, w_ref, x_ref, do_ref, g_ref, s_ref, gx_ref, st_ref, r_ref, acc, send_sems, recv_sems):
        i, p = pl.program_id(0), pl.program_id(1)

        @pl.when((i == 0) & (p == 0))
        def _():
            st_ref[...] = jnp.zeros_like(st_ref)
            for cp in _chip_block_copies(s_ref, r_ref, CHUNKS_PER_BLOCK, send_sems, recv_sems):
                cp.start()

        @pl.when((i == t // tm - 1) & (p == N_PARTS - 1))
        def _():
            for cp in _chip_block_copies(s_ref, r_ref, CHUNKS_PER_BLOCK, send_sems, recv_sems):
                cp.wait()

        @pl.when(p == 0)
        def _():
            acc[...] = jnp.zeros_like(acc)

        acc[...] += _mm_nt(dp_ref[0], jnp.concatenate([w_ref[0], w_ref[1]], axis=1))

        @pl.when(p == N_PARTS - 1)
        def _():
            def norm_bwd_slab(s, g_sum):
                rows = pl.ds(pl.multiple_of(s * SLAB, SLAB), SLAB)
                xf = x_ref[rows, :]
                r = lax.rsqrt(jnp.mean(xf * xf, axis=-1, keepdims=True) + RMS_EPS)
                xhat = xf * r
                dxn = acc[rows, :]
                dxh = dxn * g_ref[...]
                gx_ref[rows, :] = do_ref[rows, :] + r * (dxh - xhat * jnp.mean(dxh * xhat, axis=-1, keepdims=True))
                return g_sum + jnp.sum(dxn * xhat, axis=0, keepdims=True)

            st_ref[0:1, :] += lax.fori_loop(0, tm // SLAB, norm_bwd_slab, jnp.zeros((1, D_MODEL), F32))

    row = lambda i, p: (i, 0)
    fix = lambda i, p: (0, 0)
    return pl.pallas_call(
        body, grid=(t // tm, N_PARTS),
        in_specs=[
            pl.BlockSpec((1, tm, D_PART), lambda i, p: (p, i, 0)),
            pl.BlockSpec((2, D_MODEL, CHUNK), lambda i, p: (p, 0, 0)),
            pl.BlockSpec((tm, D_MODEL), row), pl.BlockSpec((tm, D_MODEL), row), pl.BlockSpec((1, D_MODEL), fix),
            pl.BlockSpec(memory_space=pl.ANY)],
        out_specs=(pl.BlockSpec((tm, D_MODEL), row), pl.BlockSpec((SUBLANES, D_MODEL), fix),
                   pl.BlockSpec(memory_space=pl.ANY)),
        out_shape=(jax.ShapeDtypeStruct((t, D_MODEL), F32), jax.ShapeDtypeStruct((SUBLANES, D_MODEL), F32),
                   _chip_blocks_shape(sb_in, CHUNKS_PER_BLOCK)),
        scratch_shapes=[pltpu.VMEM((tm, D_MODEL), F32), pltpu.SemaphoreType.DMA((3,)), pltpu.SemaphoreType.DMA((3,))],
        compiler_params=_cp(ARB, ARB), name="input_grad",
    )(dproj, w12, x, do, ln_g, sb_in)


def _w_in_grad(xn, dproj, small):
    t = xn.shape[0]
    small_shape = pltpu.VMEM(small.shape, F32)

    def body(xn_ref, dp_ref, sm_ref, o_ref, ob_ref, red_ref, acc_s, r0, r1, r2, send_sems, recv_sems):
        _allreduce_behind(pl.program_id(0), (0, 1, 3, N_PARTS - 1), sm_ref, acc_s, (r0, r1, r2), red_ref, send_sems, recv_sems)
        g = _mm_tn(xn_ref[...], dp_ref[0])
        for s in range(2):
            o_ref[s] = g[:, CHUNK * s:CHUNK * (s + 1)]
            ob_ref[s] = g[:, CHUNK * s:CHUNK * (s + 1)].astype(jnp.bfloat16)

    whole = pl.BlockSpec(small.shape, lambda p: (0, 0))
    pair = pl.BlockSpec((2, D_MODEL, CHUNK), lambda p: (p, 0, 0))
    return pl.pallas_call(
        body, grid=(N_PARTS,),
        in_specs=[pl.BlockSpec((t, D_MODEL), lambda p: (0, 0)),
                  pl.BlockSpec((1, t, D_PART), lambda p: (p, 0, 0)), whole],
        out_specs=(pair, pair, whole),
        out_shape=(jax.ShapeDtypeStruct((N_CHUNKS, D_MODEL, CHUNK), F32),
                   jax.ShapeDtypeStruct((N_CHUNKS, D_MODEL, CHUNK), jnp.bfloat16), jax.ShapeDtypeStruct(small.shape, F32)),
        scratch_shapes=[small_shape] * 4 + [pltpu.SemaphoreType.DMA((3,)), pltpu.SemaphoreType.DMA((3,))],
        compiler_params=_cp(ARB), name="w_in_grad",
    )(xn, dproj, small)


def _w_out_grad(yc, yl, dob):
    t = yc.shape[0]
    tk = 2048

    def body(yc_ref, yl_ref, do_ref, o_ref, ob_ref):
        j, kk = pl.program_id(0), pl.program_id(1)

        def accumulate(y_ref):
            @pl.when(kk == 0)
            def _():
                o_ref[...] = jnp.zeros_like(o_ref)

            o_ref[...] += _mm_tn(y_ref[...], do_ref[...])

            @pl.when(kk == t // tk - 1)
            def _():
                ob_ref[...] = o_ref[...].astype(jnp.bfloat16)

        pl.when(j == 0)(functools.partial(accumulate, yc_ref))
        pl.when(j == 1)(functools.partial(accumulate, yl_ref))

    def rows_of(half):
        return lambda j, kk: (jnp.where(j == half, kk, 0), 0)

    half = pl.BlockSpec((D_PART, D_MODEL), lambda j, kk: (j, 0))
    out, out_b = pl.pallas_call(
        body, grid=(2, t // tk),
        in_specs=[pl.BlockSpec((tk, D_PART), rows_of(0)), pl.BlockSpec((tk, D_PART), rows_of(1)),
                  pl.BlockSpec((tk, D_MODEL), lambda j, kk: (kk, 0))],
        out_specs=(half, half),
        out_shape=(jax.ShapeDtypeStruct((2 * D_PART, D_MODEL), F32), jax.ShapeDtypeStruct((2 * D_PART, D_MODEL), jnp.bfloat16)),
        compiler_params=_cp(ARB, ARB), name="w_out_grad",
    )(yc, yl, dob)
    blocks = (N_CHIPS, 2 * D_PART // N_CHIPS, D_MODEL)
    return out.reshape(blocks), out_b.reshape(blocks)


def _for_groups(n, fn, init, unroll=UNROLL, stores=(), descending=False):
    assert unroll % 2 == 0 and n % unroll == 0

    def trip(j, carry):
        held = None
        for uu in range(unroll):
            idx = j * unroll + uu
            carry, values = fn(idx, carry)
            if uu % 2 == 0:
                held = values
                continue
            low_group = n - 1 - idx if descending else idx - 1
            rows = pl.ds(pl.multiple_of(low_group * SUBLANES, 2 * SUBLANES), 2 * SUBLANES)
            pairs = zip(values, held) if descending else zip(held, values)
            for store, (lo, hi) in zip(stores, pairs, strict=True):
                store(rows, jnp.concatenate([lo, hi], axis=0).astype(MXU_DTYPE))
        return carry

    return lax.fori_loop(0, n // unroll, trip, init)


def _rows_of(ref, *lead, cols=slice(None)):
    def store(rows, value):
        ref[(*lead, rows, cols)] = value

    return store


def _pvb(pv_ref, r):
    return jnp.broadcast_to(pv_ref[r:r + 1, :], (SUBLANES, pv_ref.shape[1]))


def _conv3(pv_ref, u, u1, u2):
    return (_pvb(pv_ref, PV_CONV_W) * u2 + _pvb(pv_ref, PV_CONV_W + 1) * u1) + _pvb(pv_ref, PV_CONV_W + 2) * u


def _conv4(pv_ref, v, v1, v2, v3):
    return ((((_pvb(pv_ref, PV_LRU_W) * v3 + _pvb(pv_ref, PV_LRU_W + 1) * v2) + _pvb(pv_ref, PV_LRU_W + 2) * v1)
             + _pvb(pv_ref, PV_LRU_W + 3) * v) + _pvb(pv_ref, PV_LRU_B))


def _mixer_forward(proj, pvec, wai, w_out):
    t = proj.shape[1]
    tb = 512
    ng = tb // SUBLANES
    nt = t // tb

    def body(bg_ref, cg_ref, xc_ref, gc_ref, xl_ref, gl_ref, pv_ref, wai_ref, wo_ref,
             yc_ref, yl_ref, h_ref, u_s, r_ref, ig_ref, wo4_ref,
             ucp_s, xlp_s, ls_s, hbuf_s, ub_s, gate_s, wob_s, local_sem, send_sems, recv_sems):
        _gather_w_out(pl.program_id(0) * nt + pl.program_id(1), NS * nt, wo_ref, wob_s, wo4_ref, local_sem, send_sems, recv_sems)

        @pl.when(pl.program_id(1) == 0)
        def _():
            ucp_s[...] = jnp.zeros_like(ucp_s)
            xlp_s[...] = jnp.zeros_like(xlp_s)
            hbuf_s[...] = jnp.zeros_like(hbuf_s)

        row = lax.broadcasted_iota(jnp.int32, (SUBLANES, LW), 0)
        ls_s[...] = _log_sigmoid(_pvb(pv_ref, PV_LAM))

        def conv_group(g, carry):
            ucp, xlp = carry
            sl = pl.ds(pl.multiple_of(g * SUBLANES, SUBLANES), SUBLANES)
            uc = cg_ref[sl, :] * xc_ref[sl, :]
            v = _conv3(pv_ref, uc, _shift_down(uc, ucp, 1, row), _shift_down(uc, ucp, 2, row))
            yc = bg_ref[sl, :] * v
            rr = lax.rsqrt(_head_mean(yc * yc, CONV_HEAD) + RMS_EPS)
            gc = gc_ref[sl, :]
            zc = ((yc * rr) * _pvb(pv_ref, PV_CG)) * (gc * _sigmoid(gc))
            xl = xl_ref[sl, :]
            u = _conv4(pv_ref, xl, _shift_down(xl, xlp, 1, row), _shift_down(xl, xlp, 2, row), _shift_down(xl, xlp, 3, row))
            u_s[sl, :] = u
            return (uc, xl), (zc, u)

        ucp, xlp = _for_groups(ng, conv_group, (ucp_s[...], xlp_s[...]), unroll=2 * UNROLL,
                               stores=(_rows_of(yc_ref), _rows_of(ub_s)))
        ucp_s[...] = ucp
        xlp_s[...] = xlp

        gate_s[...] = _mm(ub_s[...], wai_ref[0])

        def lru_group(g, h_before):
            sl = pl.ds(pl.multiple_of(g * SUBLANES, SUBLANES), SUBLANES)
            u = u_s[sl, :]
            r = _sigmoid(gate_s[sl, 0:LW] + _pvb(pv_ref, PV_BA))
            ig = _sigmoid(gate_s[sl, LW:2 * LW] + _pvb(pv_ref, PV_BI))
            r_ref[sl, :] = r
            ig_ref[sl, :] = ig
            a, _, mult, _ = _decay(r, ls_s[...])
            A, B = _scan8_fwd(a, mult * (ig * u), row)
            h = B + A * jnp.broadcast_to(h_before[SUBLANES - 1:SUBLANES, :], (SUBLANES, LW))
            h_ref[sl, :] = h
            rr = lax.rsqrt(_head_mean(h * h, LRU_HEAD) + RMS_EPS)
            gl = gl_ref[sl, :]
            return h, (((h * rr) * _pvb(pv_ref, PV_LG)) * (gl * _sigmoid(gl)),)

        hbuf_s[...] = _for_groups(ng, lru_group, hbuf_s[...], unroll=2 * UNROLL, stores=(_rows_of(yl_ref),))

    def part(p):
        return pl.BlockSpec((None, tb, LW), lambda c, i: (2 * p + c // STRIPS_PER_CHUNK, i, c % STRIPS_PER_CHUNK))

    strip = pl.BlockSpec((tb, LW), lambda c, i: (i, c))
    return pl.pallas_call(
        body, grid=(NS, nt),
        in_specs=[part(p) for p in range(N_PARTS)] + [
            pl.BlockSpec((PV_ROWS, LW), lambda c, i: (0, c)),
            pl.BlockSpec((1, LW, 2 * LW), lambda c, i: (c, 0, 0)),
            pl.BlockSpec(w_out.shape, lambda c, i: (0, 0))],
        out_specs=(strip,) * 6 + (pl.BlockSpec(memory_space=pl.ANY),),
        out_shape=(jax.ShapeDtypeStruct((t, D_PART), MXU_DTYPE),) * 2 + (jax.ShapeDtypeStruct((t, D_PART), F32),) * 4 + (
            jax.ShapeDtypeStruct((N_CHIPS,) + w_out.shape, MXU_DTYPE),),
        scratch_shapes=[pltpu.VMEM((SUBLANES, LW), F32), pltpu.VMEM((SUBLANES, LW), F32), pltpu.VMEM((SUBLANES, LW), F32),
                        pltpu.VMEM((SUBLANES, LW), F32), pltpu.VMEM((tb, LW), MXU_DTYPE),
                        pltpu.VMEM((tb, 2 * LW), F32), pltpu.VMEM(w_out.shape, MXU_DTYPE),
                        pltpu.SemaphoreType.DMA, pltpu.SemaphoreType.DMA((6,)), pltpu.SemaphoreType.DMA((6,))],
        compiler_params=_cp(ARB, ARB), name="mixer_forward",
    )(proj, proj, proj, proj, proj, proj, pvec, wai, w_out)


def _mixer_backward(proj, h, u, r, ig, dy, pvec, wai, sb_out):
    t = proj.shape[1]
    tb = 512
    ng = tb // SUBLANES
    nt = t // tb
    gpb = tb // SUBLANES

    def body(bg_ref, cg_ref, xc_ref, gc_ref, xl_ref, gl_ref, h_ref, u_ref, r_ref, ig_ref, dyc_ref, dyl_ref,
             cgh_ref, xch_ref, xlh_ref, hh_ref, pv_ref, wai_ref, so_ref,
             dp_ref, gw_ref, sv_ref, ro_ref,
             ls_s, ub_s, uce_s, xle_s, he_s, dgb_s, du_s, gbuf_s,
             acc_s, an_s, dvn_s, dun_s, send_sems, recv_sems):
        i = pl.program_id(1)
        first_block = i == nt - 1

        @pl.when((pl.program_id(0) == 0) & (i == 0))
        def _():
            for cp in _chip_block_copies(so_ref, ro_ref, 1, send_sems, recv_sems):
                cp.start()

        @pl.when((pl.program_id(0) == NS - 1) & (i == nt - 1))
        def _():
            for cp in _chip_block_copies(so_ref, ro_ref, 1, send_sems, recv_sems):
                cp.wait()

        @pl.when(i == 0)
        def _():
            acc_s[...] = jnp.zeros_like(acc_s)
            gw_ref[...] = jnp.zeros_like(gw_ref)
            an_s[...] = jnp.zeros_like(an_s)
            dvn_s[...] = jnp.zeros_like(dvn_s)
            dun_s[...] = jnp.zeros_like(dun_s)
            gbuf_s[...] = jnp.zeros_like(gbuf_s)

        row = lax.broadcasted_iota(jnp.int32, (SUBLANES, LW), 0)
        ls_s[...] = _log_sigmoid(_pvb(pv_ref, PV_LAM))
        keep = jnp.where(first_block, 0.0, 1.0)
        uce_s[0:SUBLANES, :] = (cgh_ref[...] * xch_ref[...]) * keep
        xle_s[0:SUBLANES, :] = xlh_ref[...] * keep
        he_s[0:SUBLANES, :] = hh_ref[...] * keep
        xle_s[SUBLANES:SUBLANES + tb, :] = xl_ref[...]
        he_s[SUBLANES:SUBLANES + tb, :] = h_ref[...]

        uce_s[SUBLANES:SUBLANES + tb, :] = cg_ref[...] * xc_ref[...]

        def acc_add(k, v):
            acc_s[k] += v

        def main_group(gi, carry):
            a_next, dv_next, g_next = carry
            g = ng - 1 - gi
            r0 = pl.multiple_of(g * SUBLANES, SUBLANES)
            sl = pl.ds(r0, SUBLANES)
            sl_e = pl.ds(r0 + SUBLANES, SUBLANES)
            lsb = ls_s[...]
            u = u_ref[sl, :]
            r = r_ref[sl, :]
            ig = ig_ref[sl, :]
            a, e2, mult, inv_mult = _decay(r, lsb)
            gl = gl_ref[sl, :]
            sg = _sigmoid(gl)
            s_l = gl * sg
            h8 = he_s[sl_e, :]
            hprev = _shift_down(h8, he_s[sl, :], 1, row)
            rr = lax.rsqrt(_head_mean(h8 * h8, LRU_HEAD) + RMS_EPS)
            n = h8 * rr
            dz = dyl_ref[sl, :]
            lg = _pvb(pv_ref, PV_LG)
            acc_add(PV_LG, (dz * n) * s_l)
            p5 = ((dz * n) * lg) * (sg * (1.0 + gl * (1.0 - sg)))
            dn = (dz * lg) * s_l
            dh = rr * (dn - n * _head_mean(dn * n, LRU_HEAD))
            A, B = _scan8_rev(_shift_up(a, a_next, 1, row), dh, row)
            gg = B + A * jnp.broadcast_to(g_next[0:1, :], (SUBLANES, LW))
            da = gg * hprev
            iu = ig * u
            diu = gg * mult
            dla = da * a - (gg * iu) * (e2 * inv_mult)
            acc_add(PV_LAM, dla * (RG_LRU_C * r))
            dra = (dla * (RG_LRU_C * lsb)) * (r * (1.0 - r))
            dia = (diu * u) * (ig * (1.0 - ig))
            acc_add(PV_BA, dra)
            acc_add(PV_BI, dia)
            du_s[sl, :] = diu * ig
            bg = bg_ref[sl, :]
            gc = gc_ref[sl, :]
            uc = uce_s[sl_e, :]
            ucp = uce_s[sl, :]
            uc1 = _shift_down(uc, ucp, 1, row)
            uc2 = _shift_down(uc, ucp, 2, row)
            v = _conv3(pv_ref, uc, uc1, uc2)
            yc = bg * v
            rrc = lax.rsqrt(_head_mean(yc * yc, CONV_HEAD) + RMS_EPS)
            nc = yc * rrc
            sgc = _sigmoid(gc)
            s_c = gc * sgc
            dzc = dyc_ref[sl, :]
            cgain = _pvb(pv_ref, PV_CG)
            acc_add(PV_CG, (dzc * nc) * s_c)
            p3 = ((dzc * nc) * cgain) * (sgc * (1.0 + gc * (1.0 - sgc)))
            dnc = (dzc * cgain) * s_c
            dyc = rrc * (dnc - nc * _head_mean(dnc * nc, CONV_HEAD))
            dv = dyc * bg
            duc = (_pvb(pv_ref, PV_CONV_W + 2) * dv + _pvb(pv_ref, PV_CONV_W + 1) * _shift_up(dv, dv_next, 1, row)
                   + _pvb(pv_ref, PV_CONV_W) * _shift_up(dv, dv_next, 2, row))
            acc_add(PV_CONV_W + 2, dv * uc)
            acc_add(PV_CONV_W + 1, dv * uc1)
            acc_add(PV_CONV_W, dv * uc2)
            return (a, dv, gg), (dyc * v, duc * xc_ref[sl, :], duc * cg_ref[sl, :], p3, p5, dra, dia, u)

        a_next, dv_next, g_next = _for_groups(
            ng, main_group, (an_s[...], dvn_s[...], gbuf_s[...]), descending=True,
            stores=(_rows_of(dp_ref, 0), _rows_of(dp_ref, 1), _rows_of(dp_ref, 2), _rows_of(dp_ref, 3), _rows_of(dp_ref, 5),
                    _rows_of(dgb_s, cols=slice(0, LW)), _rows_of(dgb_s, cols=slice(LW, 2 * LW)), _rows_of(ub_s)))
        an_s[...] = a_next
        dvn_s[...] = dv_next
        gbuf_s[...] = g_next

        dgb = dgb_s[...]
        du_s[...] += _mm_nt(dgb, wai_ref[0])
        gw_ref[0] += _mm_tn(ub_s[...], dgb)

        def lru_conv_group(gi, du_next):
            g = ng - 1 - gi
            r0 = pl.multiple_of(g * SUBLANES, SUBLANES)
            sl = pl.ds(r0, SUBLANES)
            du = du_s[sl, :]
            xl = xle_s[pl.ds(r0 + SUBLANES, SUBLANES), :]
            xlp = xle_s[sl, :]
            acc_add(PV_LRU_B, du)
            acc_add(PV_LRU_W + 3, du * xl)
            acc_add(PV_LRU_W + 2, du * _shift_down(xl, xlp, 1, row))
            acc_add(PV_LRU_W + 1, du * _shift_down(xl, xlp, 2, row))
            acc_add(PV_LRU_W, du * _shift_down(xl, xlp, 3, row))
            dxl = (((_pvb(pv_ref, PV_LRU_W + 3) * du + _pvb(pv_ref, PV_LRU_W + 2) * _shift_up(du, du_next, 1, row))
                    + _pvb(pv_ref, PV_LRU_W + 1) * _shift_up(du, du_next, 2, row))
                   + _pvb(pv_ref, PV_LRU_W) * _shift_up(du, du_next, 3, row))
            return du, (dxl,)

        dun_s[...] = _for_groups(ng, lru_conv_group, dun_s[...], descending=True, stores=(_rows_of(dp_ref, 4),))

        @pl.when(first_block)
        def _():
            sv_ref[...] = jnp.zeros_like(sv_ref)
            for k in range(N_ACC):
                tot = jnp.sum(acc_s[k], axis=0, keepdims=True)
                if k == PV_LAM:
                    tot = tot / (1.0 + jnp.exp(pv_ref[PV_LAM:PV_LAM + 1, :]))
                sv_ref[k:k + 1, :] = tot

    def part(p):
        return pl.BlockSpec((None, tb, LW), lambda c, i: (2 * p + c // STRIPS_PER_CHUNK, nt - 1 - i, c % STRIPS_PER_CHUNK))

    def halo(p):
        return pl.BlockSpec((None, SUBLANES, LW), lambda c, i: (2 * p + c // STRIPS_PER_CHUNK,
                                                                jnp.maximum((nt - 1 - i) * gpb - 1, 0), c % STRIPS_PER_CHUNK))

    strip = pl.BlockSpec((tb, LW), lambda c, i: (nt - 1 - i, c))
    big = pltpu.VMEM((tb, LW), F32)
    big_e = pltpu.VMEM((tb + SUBLANES, LW), F32)
    small = pltpu.VMEM((SUBLANES, LW), F32)
    outs = pl.pallas_call(
        body, grid=(NS, nt),
        in_specs=[part(p) for p in range(N_PARTS)] + [
            strip, strip, strip, strip, strip, pl.BlockSpec((tb, LW), lambda c, i: (nt - 1 - i, NS + c)),
            halo(1), halo(2), halo(4),
            pl.BlockSpec((SUBLANES, LW), lambda c, i: (jnp.maximum((nt - 1 - i) * gpb - 1, 0), c)),
            pl.BlockSpec((PV_ROWS, LW), lambda c, i: (0, c)),
            pl.BlockSpec((1, LW, 2 * LW), lambda c, i: (c, 0, 0)),
            pl.BlockSpec(memory_space=pl.ANY)],
        out_specs=(pl.BlockSpec((N_PARTS, tb, LW), lambda c, i: (0, nt - 1 - i, c)),
                   pl.BlockSpec((1, LW, 2 * LW), lambda c, i: (c, 0, 0)),
                   pl.BlockSpec((PV_ROWS, LW), lambda c, i: (0, c)),
                   pl.BlockSpec(memory_space=pl.ANY)),
        out_shape=(jax.ShapeDtypeStruct((N_PARTS, t, D_PART), MXU_DTYPE),
                   jax.ShapeDtypeStruct((NS, LW, 2 * LW), F32), jax.ShapeDtypeStruct((PV_ROWS, D_PART), F32),
                   _chip_blocks_shape(sb_out, 1)),
        scratch_shapes=[small, pltpu.VMEM((tb, LW), MXU_DTYPE), big_e, big_e, big_e,
                        pltpu.VMEM((tb, 2 * LW), MXU_DTYPE), big, small,
                        pltpu.VMEM((N_ACC, SUBLANES, LW), F32), small, small, small,
                        pltpu.SemaphoreType.DMA((3,)), pltpu.SemaphoreType.DMA((3,))],
        compiler_params=_cp(ARB, ARB), name="mixer_backward",
    )(proj, proj, proj, proj, proj, proj, h, u, r, ig, dy, dy, proj, proj, proj, h, pvec, wai, sb_out)
    return outs


def _adamw(w, g, m, v):
    m = ADAM_B1 * m + (1.0 - ADAM_B1) * g
    v = ADAM_B2 * v + (1.0 - ADAM_B2) * (g * g)
    m_hat = m / (1.0 - ADAM_B1 ** ADAM_STEP)
    v_hat = v / (1.0 - ADAM_B2 ** ADAM_STEP)
    delta = -ADAM_LR * (m_hat / (jnp.sqrt(v_hat) + ADAM_EPS) + ADAM_WD * w)
    return delta, m, v


def _adam_w_in(w, m, v, g3):
    rows, cols = w.shape
    tr = 128

    def body(w_ref, m_ref, v_ref, g_ref, go_ref, d_ref, mo_ref, vo_ref):
        for s in range(CHUNKS_PER_BLOCK):
            cs = slice(CHUNK * s, CHUNK * (s + 1))
            g = g_ref[s]
            d, mn, vn = _adamw(w_ref[:, cs], g, m_ref[:, cs], v_ref[:, cs])
            go_ref[:, cs] = g
            d_ref[:, cs] = d
            mo_ref[:, cs] = mn
            vo_ref[:, cs] = vn

    blk = pl.BlockSpec((tr, cols), lambda i: (i, 0))
    return pl.pallas_call(
        body, grid=(rows // tr,),
        in_specs=[blk, blk, blk, pl.BlockSpec((CHUNKS_PER_BLOCK, tr, CHUNK), lambda i: (0, i, 0))],
        out_specs=(blk,) * 4, out_shape=(jax.ShapeDtypeStruct(w.shape, F32),) * 4,
        compiler_params=_cp(ARB), name="adam_w_in",
    )(w, m, v, g3)


def _adam_w_out(w, m, v, g):
    rows, cols = w.shape
    tr = 128

    def body(w_ref, m_ref, v_ref, g_ref, d_ref, mo_ref, vo_ref):
        d_ref[...], mo_ref[...], vo_ref[...] = _adamw(w_ref[...], g_ref[...], m_ref[...], v_ref[...])

    blk = pl.BlockSpec((tr, cols), lambda i: (i, 0))
    return pl.pallas_call(
        body, grid=(rows // tr,), in_specs=[blk] * 4, out_specs=(blk,) * 3,
        out_shape=(jax.ShapeDtypeStruct(w.shape, F32),) * 3,
        compiler_params=_cp(ARB), name="adam_w_out",
    )(w, m, v, g)


def _adam_small(ws, ms, vs, gs):
    n = len(ws)

    def body(*refs):
        w_r, m_r, v_r, g_r = refs[0:n], refs[n:2 * n], refs[2 * n:3 * n], refs[3 * n:4 * n]
        d_o, m_o, v_o = refs[4 * n:5 * n], refs[5 * n:6 * n], refs[6 * n:7 * n]
        for j in range(n):
            d_o[j][...], m_o[j][...], v_o[j][...] = _adamw(w_r[j][...], g_r[j][...], m_r[j][...], v_r[j][...])

    vm = pl.BlockSpec(memory_space=pltpu.VMEM)
    shapes = tuple(jax.ShapeDtypeStruct(w.shape, F32) for w in ws)
    outs = pl.pallas_call(
        body, in_specs=[vm] * (4 * n), out_specs=(vm,) * (3 * n), out_shape=shapes * 3,
        compiler_params=_cp(), name="adam_small",
    )(*ws, *ms, *vs, *gs)
    return outs[0:n], outs[n:2 * n], outs[2 * n:3 * n]


def _block_diag_strips(w):
    w4 = w.reshape(NS, HEADS_PER_STRIP, LRU_HEAD, LRU_HEAD)
    rows = [jnp.pad(w4[:, hh], ((0, 0), (0, 0), (LRU_HEAD * hh, LW - LRU_HEAD * (hh + 1)))) for hh in range(HEADS_PER_STRIP)]
    return jnp.concatenate(rows, axis=1)


def _strip_diag_blocks(g):
    g5 = g.reshape(NS, HEADS_PER_STRIP, LRU_HEAD, HEADS_PER_STRIP, LRU_HEAD)
    return jnp.stack([g5[:, hh, :, hh, :] for hh in range(HEADS_PER_STRIP)], axis=1).reshape(NS * HEADS_PER_STRIP, LRU_HEAD, LRU_HEAD)


def kernel(x, ln_g, w_in, conv_w, lru_conv_w, lru_conv_b, w_a, b_a, w_i, b_i, lam, conv_out_g, lru_out_g, w_out, final_g, loss_target, m_ln_g, m_w_in, m_conv_w, m_lru_conv_w, m_lru_conv_b, m_w_a, m_b_a, m_w_i, m_b_i, m_lam, m_conv_out_g, m_lru_out_g, m_w_out, m_final_g, v_ln_g, v_w_in, v_conv_w, v_lru_conv_w, v_lru_conv_b, v_w_a, v_b_a, v_w_i, v_b_i, v_lam, v_conv_out_g, v_lru_out_g, v_w_out, v_final_g):
    xi, yi, ci = lax.axis_index("x"), lax.axis_index("y"), lax.axis_index("c")
    k = 2 * xi + yi
    t = x.shape[1]
    x2 = x.reshape(t, D_MODEL)
    tgt2 = loss_target.reshape(t, D_MODEL)
    row = lambda a: a.reshape(1, -1)

    small = jnp.concatenate([conv_w, lru_conv_w, jnp.zeros((1, conv_w.shape[1]), F32)], axis=0)
    proj, xn, w12, sm4 = _gather_in_projection(x2, row(ln_g), w_in, small)
    convs = jnp.transpose(sm4, (1, 0, 2)).reshape(SUBLANES, D_PART)
    pvec = jnp.concatenate(
        [convs[0:7], row(lru_conv_b), row(b_a), row(b_i), row(lam), row(conv_out_g), row(lru_out_g),
         jnp.zeros((PV_ROWS - N_ACC, D_PART), F32)], axis=0)
    wai = jnp.concatenate([_block_diag_strips(w_a), _block_diag_strips(w_i)], axis=2).astype(MXU_DTYPE)

    c_arr = jnp.reshape(ci, (1,)).astype(jnp.int32)
    kc_arr = jnp.stack([k, ci]).astype(jnp.int32)
    yc, yl, h, u, r, ig, wo4 = _mixer_forward(proj, pvec, wai, w_out)
    wo = wo4.reshape(2 * D_PART, D_MODEL)
    do, dob, dy, st_out = _out_projection_loss(yc, yl, x2, tgt2, wo, row(final_g))
    go4, go4b = _w_out_grad(yc, yl, dob)
    s_out, sb_out = _add_own_half(go4, _exchange_sibling_halves(go4b, "exchange_sibling_halves_out"), c_arr, "add_own_half_out")
    dproj, g_wai, svec, r2o = _mixer_backward(proj, h, u, r, ig, dy, pvec, wai, sb_out)
    gwa = _strip_diag_blocks(g_wai[:, :, 0:LW]).reshape(LRU_HEAD, D_PART)
    gwi = _strip_diag_blocks(g_wai[:, :, LW:2 * LW]).reshape(LRU_HEAD, D_PART)
    g12, g12b, red = _w_in_grad(xn, dproj, jnp.concatenate([svec, st_out, gwa, gwi], axis=0))
    s_in, sb_in = _add_own_half(g12, _exchange_sibling_halves(g12b, "exchange_sibling_halves_in"), c_arr, "add_own_half_in")
    grad_x, st_in, r2i = _input_grad(dproj, w12, x2, do, row(ln_g), sb_in)
    f_in = _sum_chip_blocks(s_in, r2i, kc_arr, CHUNKS_PER_BLOCK, "sum_chip_blocks_in")
    f_out = _sum_chip_blocks(s_out, r2o, kc_arr, 1, "sum_chip_blocks_out")
    f_in, f_out = _swap_sibling_halves(f_in, f_out)

    red_ln = _sum_over_devices(st_in)
    r_out = PV_ROWS
    r_wa = PV_ROWS + SUBLANES
    r_wi = r_wa + LRU_HEAD
    loss = red[r_out + 1, 0]

    g_w_in, d_w_in, nm_w_in, nv_w_in = _adam_w_in(w_in, m_w_in, v_w_in, f_in)
    g_w_out = f_out[0]
    d_w_out, nm_w_out, nv_w_out = _adam_w_out(w_out, m_w_out, v_w_out, g_w_out)

    ncol = conv_w.shape[1]
    conv_cols = lax.dynamic_slice(red, (0, k * ncol), (SUBLANES, ncol))
    g_small = {
        "ln_g": red_ln[0], "conv_w": conv_cols[0:3], "lru_conv_w": conv_cols[3:7], "lru_conv_b": red[PV_LRU_B],
        "w_a": red[r_wa:r_wa + LRU_HEAD].reshape(w_a.shape), "b_a": red[PV_BA],
        "w_i": red[r_wi:r_wi + LRU_HEAD].reshape(w_i.shape), "b_i": red[PV_BI], "lam": red[PV_LAM],
        "conv_out_g": red[PV_CG], "lru_out_g": red[PV_LG], "final_g": red[r_out],
    }
    w_small = {"ln_g": ln_g, "conv_w": conv_w, "lru_conv_w": lru_conv_w, "lru_conv_b": lru_conv_b, "w_a": w_a, "b_a": b_a,
               "w_i": w_i, "b_i": b_i, "lam": lam, "conv_out_g": conv_out_g, "lru_out_g": lru_out_g, "final_g": final_g}
    m_small = {"ln_g": m_ln_g, "conv_w": m_conv_w, "lru_conv_w": m_lru_conv_w, "lru_conv_b": m_lru_conv_b, "w_a": m_w_a,
               "b_a": m_b_a, "w_i": m_w_i, "b_i": m_b_i, "lam": m_lam, "conv_out_g": m_conv_out_g,
               "lru_out_g": m_lru_out_g, "final_g": m_final_g}
    v_small = {"ln_g": v_ln_g, "conv_w": v_conv_w, "lru_conv_w": v_lru_conv_w, "lru_conv_b": v_lru_conv_b, "w_a": v_w_a,
               "b_a": v_b_a, "w_i": v_w_i, "b_i": v_b_i, "lam": v_lam, "conv_out_g": v_conv_out_g,
               "lru_out_g": v_lru_out_g, "final_g": v_final_g}
    names = list(w_small)
    as2d = lambda a: a.reshape(1, -1) if a.ndim == 1 else a
    d_s, m_s, v_s = _adam_small([as2d(w_small[n]) for n in names], [as2d(m_small[n]) for n in names],
                                [as2d(v_small[n]) for n in names], [as2d(g_small[n]) for n in names])
    back = lambda n, a: a.reshape(w_small[n].shape)
    grads = {n: g_small[n] for n in names}
    deltas = {n: back(n, a) for n, a in zip(names, d_s)}
    new_m = {n: back(n, a) for n, a in zip(names, m_s)}
    new_v = {n: back(n, a) for n, a in zip(names, v_s)}
    grads["w_in"], deltas["w_in"], new_m["w_in"], new_v["w_in"] = g_w_in, d_w_in, nm_w_in, nv_w_in
    grads["w_out"], deltas["w_out"], new_m["w_out"], new_v["w_out"] = g_w_out, d_w_out, nm_w_out, nv_w_out

    order = ["ln_g", "w_in", "conv_w", "lru_conv_w", "lru_conv_b", "w_a", "b_a", "w_i", "b_i", "lam", "conv_out_g",
             "lru_out_g", "w_out", "final_g"]
    return (loss, grad_x.reshape(x.shape), *[grads[n] for n in order], *[deltas[n] for n in order],
            *[new_m[n] for n in order], *[new_v[n] for n in order])
```

```python
import functools

import jax
import jax.numpy as jnp
from jax import lax
from jax.experimental import pallas as pl
from jax.experimental.pallas import tpu as pltpu

F32 = jnp.float32
MXU_DTYPE = jnp.bfloat16

D_MODEL = 1024
D_PART = 1024
N_PARTS = 6
CHUNK = 512
CHUNKS_PER_BLOCK = 3
N_CHUNKS = 12
N_CHIPS = 4
SUBLANES = 8
LANES = 128
LW = 256
UNROLL = 8
NS = D_PART // LW
STRIPS_PER_CHUNK = CHUNK // LW
CONV_HEAD = 128
LRU_HEAD = 64
HEADS_PER_STRIP = LW // LRU_HEAD
RMS_EPS = 1e-6
RG_LRU_C = 8.0
ADAM_LR = 0.001
ADAM_B1 = 0.9
ADAM_B2 = 0.999
ADAM_EPS = 1e-08
ADAM_WD = 0.01
ADAM_STEP = 10

PV_CONV_W = 0
PV_LRU_W = 3
PV_LRU_B = 7
PV_BA = 8
PV_BI = 9
PV_LAM = 10
PV_CG = 11
PV_LG = 12
PV_ROWS = 16
N_ACC = 13

SLAB = 128
MESH = pl.DeviceIdType.MESH
VMEM_LIMIT = 56 * 1024 * 1024
ARB = "arbitrary"


def _cp(*sem, **kw):
    return pltpu.CompilerParams(dimension_semantics=sem or None, vmem_limit_bytes=VMEM_LIMIT, **kw)


def _mm(a, b):
    return jnp.dot(a, b, preferred_element_type=F32)


def _mm_nt(a, b):
    return lax.dot_general(a, b, (((1,), (1,)), ((), ())), preferred_element_type=F32)


def _mm_tn(a, b):
    return lax.dot_general(a, b, (((0,), (0,)), ((), ())), preferred_element_type=F32)


def _sigmoid(x):
    return 0.5 * jnp.tanh(0.5 * x) + 0.5


def _log_sigmoid(x):
    z = jnp.exp(-jnp.abs(x))
    u = 1.0 + z
    log1p = jnp.where(u == 1.0, z, jnp.log(u) * z / (u - 1.0))
    return jnp.minimum(x, 0.0) - log1p


def _head_mean(z, head):
    out = []
    for k in range(z.shape[1] // LANES):
        zk = z[:, LANES * k:LANES * (k + 1)]
        if head == LANES:
            m = jnp.sum(zk, axis=-1, keepdims=True) * (1.0 / head)
            out.append(jnp.broadcast_to(m, zk.shape))
        else:
            lo = lax.broadcasted_iota(jnp.int32, zk.shape, 1) < head
            s_lo = jnp.sum(jnp.where(lo, zk, 0.0), axis=-1, keepdims=True)
            s_hi = jnp.sum(jnp.where(lo, 0.0, zk), axis=-1, keepdims=True)
            out.append(jnp.where(lo, s_lo, s_hi) * (1.0 / head))
    return jnp.concatenate(out, axis=1)


def _shift_down(cur, prev, d, row):
    return pltpu.roll(jnp.where(row < SUBLANES - d, cur, prev), d, 0)


def _shift_up(cur, nxt, d, row):
    return pltpu.roll(jnp.where(row >= d, cur, nxt), SUBLANES - d, 0)


def _scan8_fwd(a, b, row):
    A, B = a, b
    for d in (1, 2, 4):
        m = row >= d
        a_s = jnp.where(m, pltpu.roll(A, d, 0), 1.0)
        b_s = jnp.where(m, pltpu.roll(B, d, 0), 0.0)
        B = A * b_s + B
        A = A * a_s
    return A, B


def _scan8_rev(a, b, row):
    A, B = a, b
    for d in (1, 2, 4):
        m = row < SUBLANES - d
        a_s = jnp.where(m, pltpu.roll(A, SUBLANES - d, 0), 1.0)
        b_s = jnp.where(m, pltpu.roll(B, SUBLANES - d, 0), 0.0)
        B = A * b_s + B
        A = A * a_s
    return A, B


def _decay(r, lsb):
    la = (RG_LRU_C * r) * lsb
    a = jnp.exp(la)
    e2 = a * a
    em = -jnp.tanh(la) * (1.0 + e2)
    inv_mult = lax.rsqrt(em)
    return a, e2, em * inv_mult, inv_mult


def _mesh_pos():
    x, y, c = lax.axis_index("x"), lax.axis_index("y"), lax.axis_index("c")
    chips = [(1 - x, y), (x, 1 - y), (1 - x, 1 - y)]
    return x, y, c, chips


def _gather_in_projection(x, ln_g, w_in, small):
    t = x.shape[0]
    rb_x = 512
    rb_mm = 1024
    n_mm = t // rb_mm
    half = w_in.shape[0] // 2
    n_ici = 3 * CHUNKS_PER_BLOCK

    def body(x_hbm, g_ref, wi_ref, sm_ref, proj_hbm, xn_ref, w12_ref, sm4_ref,
             xbuf, obuf, x_sems, o_sems, send_sems, recv_sems):
        x_, y_, c, chips = _mesh_pos()
        k = 2 * x_ + y_
        sib = (x_, y_, 1 - c)
        my_rows = pl.ds(pl.multiple_of(half * c, half), half)
        sib_rows = pl.ds(pl.multiple_of(half * (1 - c), half), half)

        sm4_ref[k] = sm_ref[...]

        def remote(ref, sem, to):
            return pltpu.make_async_remote_copy(src_ref=ref, dst_ref=ref, send_sem=send_sems.at[sem],
                                                recv_sem=recv_sems.at[sem], device_id=to, device_id_type=MESH)

        def chunk_of(chip, s):
            return CHUNKS_PER_BLOCK * (2 * chip[0] + chip[1]) + s

        ici = lambda m, s: 3 * s + m
        fwd = lambda m, s: n_ici + 3 * s + m
        sml = lambda m: 2 * n_ici + m

        sends = []
        for s in range(CHUNKS_PER_BLOCK):
            w12_ref[chunk_of((x_, y_), s)] = wi_ref[:, CHUNK * s:CHUNK * (s + 1)].astype(MXU_DTYPE)
            for m, chip in enumerate(chips):
                sends.append(remote(w12_ref.at[chunk_of((x_, y_), s), my_rows, :], ici(m, s), (*chip, c)))
                sends[-1].start()
        for m, chip in enumerate(chips):
            sends.append(remote(sm4_ref.at[k], sml(m), (*chip, c)))
            sends[-1].start()

        def x_copy(rb, slot):
            return pltpu.make_async_copy(x_hbm.at[pl.ds(rb * rb_x, rb_x), :], xbuf.at[slot], x_sems.at[slot])

        x_copy(0, 0).start()
        for rb in range(t // rb_x):
            slot = rb % 2
            x_copy(rb, slot).wait()
            if rb + 1 < t // rb_x:
                x_copy(rb + 1, 1 - slot).start()

            def norm_slab(sl, carry, rb=rb, slot=slot):
                xf = xbuf[slot, pl.ds(pl.multiple_of(sl * SLAB, SLAB), SLAB), :]
                r = lax.rsqrt(jnp.mean(xf * xf, axis=-1, keepdims=True) + RMS_EPS)
                xn_ref[pl.ds(pl.multiple_of(rb * rb_x + sl * SLAB, SLAB), SLAB), :] = ((xf * r) * g_ref[...]).astype(MXU_DTYPE)
                return carry

            lax.fori_loop(0, rb_x // SLAB, norm_slab, 0)

        def out_copy(q, i, slot):
            return pltpu.make_async_copy(obuf.at[slot], proj_hbm.at[q, pl.ds(pl.multiple_of(i * rb_mm, rb_mm), rb_mm), :],
                                         o_sems.at[slot])

        def project(q, very_first):
            def row_block(i, carry):
                slot = i % 2

                def wait_buffer():
                    out_copy(q, i, slot).wait()

                if very_first:
                    pl.when(i >= 2)(wait_buffer)
                else:
                    wait_buffer()
                obuf[slot] = _mm(xn_ref[pl.ds(pl.multiple_of(i * rb_mm, rb_mm), rb_mm), :], w12_ref[q])
                out_copy(q, i, slot).start()
                return carry

            lax.fori_loop(0, n_mm, row_block, 0)

        for s in range(CHUNKS_PER_BLOCK):
            project(chunk_of((x_, y_), s), very_first=(s == 0))

        order = [(m, s) for s in range(CHUNKS_PER_BLOCK) for m in range(3)]
        forwards = []
        for j, (m, s) in enumerate(order):
            q = chunk_of(chips[m], s)
            remote(w12_ref.at[q, my_rows, :], ici(m, s), sib).wait_recv()
            f = remote(w12_ref.at[q, my_rows, :], fwd(m, s), sib)
            f.start()
            forwards.append(f)
            if j > 0:
                pm, ps = order[j - 1]
                pq = chunk_of(chips[pm], ps)
                remote(w12_ref.at[pq, sib_rows, :], fwd(pm, ps), sib).wait_recv()
                project(pq, very_first=False)
        pm, ps = order[-1]
        pq = chunk_of(chips[pm], ps)
        remote(w12_ref.at[pq, sib_rows, :], fwd(pm, ps), sib).wait_recv()
        project(pq, very_first=False)

        for m, chip in enumerate(chips):
            remote(sm4_ref.at[2 * chip[0] + chip[1]], sml(m), sib).wait_recv()
        for cp in sends + forwards:
            cp.wait_send()
        for slot in range(2):
            out_copy(0, slot, slot).wait()

    assert n_mm % 2 == 0 and n_mm >= 2
    vm = pl.BlockSpec(memory_space=pltpu.VMEM)
    hbm = pl.BlockSpec(memory_space=pl.ANY)
    n_sems = 2 * n_ici + 3
    return pl.pallas_call(
        body,
        out_shape=(jax.ShapeDtypeStruct((N_CHUNKS, t, CHUNK), F32), jax.ShapeDtypeStruct((t, D_MODEL), MXU_DTYPE),
                   jax.ShapeDtypeStruct((N_CHUNKS, w_in.shape[0], CHUNK), MXU_DTYPE),
                   jax.ShapeDtypeStruct((N_CHIPS,) + small.shape, F32)),
        in_specs=[hbm, vm, vm, vm], out_specs=(hbm, vm, vm, vm),
        scratch_shapes=[pltpu.VMEM((2, rb_x, D_MODEL), F32), pltpu.VMEM((2, rb_mm, CHUNK), F32),
                        pltpu.SemaphoreType.DMA((2,)), pltpu.SemaphoreType.DMA((2,)),
                        pltpu.SemaphoreType.DMA((n_sems,)), pltpu.SemaphoreType.DMA((n_sems,))],
        compiler_params=_cp(), name="gather_in_projection",
    )(x, ln_g, w_in, small)


def _allreduce_behind(step, when, in_ref, acc_s, rbufs, out_ref, send_sems, recv_sems):
    x, y, c, _ = _mesh_pos()
    peers = [(x, y, 1 - c), (1 - x, y, c), (x, 1 - y, c)]

    def exchange(ph):
        return pltpu.make_async_remote_copy(src_ref=acc_s, dst_ref=rbufs[ph], send_sem=send_sems.at[ph],
                                            recv_sem=recv_sems.at[ph], device_id=peers[ph], device_id_type=MESH)

    @pl.when(step == when[0])
    def _():
        acc_s[...] = in_ref[...]
        exchange(0).start()

    for ph in (1, 2):
        @pl.when(step == when[ph])
        def _(ph=ph):
            exchange(ph - 1).wait()
            acc_s[...] = acc_s[...] + rbufs[ph - 1][...]
            exchange(ph).start()

    @pl.when(step == when[3])
    def _():
        exchange(2).wait()
        out_ref[...] = acc_s[...] + rbufs[2][...]


def _exchange_sibling_halves(g, name):
    n, rows, cols = g.shape
    half = rows // 2

    def body(g_ref, r_ref, send_sem, recv_sem):
        x, y, c, _ = _mesh_pos()
        cp = pltpu.make_async_remote_copy(src_ref=g_ref.at[:, pl.ds(pl.multiple_of(half * (1 - c), half), half), :],
                                          dst_ref=r_ref, send_sem=send_sem, recv_sem=recv_sem,
                                          device_id=(x, y, 1 - c), device_id_type=MESH)
        cp.start()
        cp.wait()

    hbm = pl.BlockSpec(memory_space=pl.ANY)
    return pl.pallas_call(
        body, out_shape=jax.ShapeDtypeStruct((n, half, cols), g.dtype), in_specs=[hbm], out_specs=hbm,
        scratch_shapes=[pltpu.SemaphoreType.DMA, pltpu.SemaphoreType.DMA],
        compiler_params=_cp(), name=name,
    )(g)


def _add_own_half(g, r, c_arr, name):
    n, rr, cc = r.shape

    def body(c_ref, g_ref, r_ref, o_ref, ob_ref):
        s = g_ref[...] + r_ref[...].astype(F32)
        o_ref[...] = s
        ob_ref[...] = s.astype(jnp.bfloat16)

    blk = pl.BlockSpec((1, rr, cc), lambda q, c_ref: (q, 0, 0))
    return pl.pallas_call(
        body, out_shape=(jax.ShapeDtypeStruct(r.shape, F32), jax.ShapeDtypeStruct(r.shape, jnp.bfloat16)),
        grid_spec=pltpu.PrefetchScalarGridSpec(
            num_scalar_prefetch=1, grid=(n,),
            in_specs=[pl.BlockSpec((1, rr, cc), lambda q, c_ref: (q, c_ref[0], 0)), blk],
            out_specs=(blk, blk)),
        compiler_params=_cp(ARB), name=name,
    )(c_arr, g, r)


def _chip_block_copies(s_ref, r_ref, n_sub, send_sems, recv_sems):
    x, y, c, chips = _mesh_pos()
    cps = []
    for m, chip in enumerate(chips):
        kk = 2 * chip[0] + chip[1]
        cps.append(pltpu.make_async_remote_copy(
            src_ref=s_ref.at[pl.ds(n_sub * kk, n_sub)], dst_ref=r_ref.at[m],
            send_sem=send_sems.at[m], recv_sem=recv_sems.at[m], device_id=(*chip, c), device_id_type=MESH))
    return cps


def _gather_w_out(step, n_steps, wo_ref, wob_s, wo4_ref, local_sem, send_sems, recv_sems):
    x, y, c, chips = _mesh_pos()
    sib = (x, y, 1 - c)
    half = wo_ref.shape[0] // 2

    def rows(core):
        return pl.ds(pl.multiple_of(half * core, half), half)

    def block_half(chip, core):
        return wo4_ref.at[2 * chip[0] + chip[1], rows(core), :]

    def remote(src, dst, sem, to):
        return pltpu.make_async_remote_copy(src_ref=src, dst_ref=dst, send_sem=send_sems.at[sem], recv_sem=recv_sems.at[sem],
                                            device_id=to, device_id_type=MESH)

    local = pltpu.make_async_copy(wob_s, wo4_ref.at[2 * x + y], local_sem)
    ici = [remote(wob_s.at[rows(c), :], block_half((x, y), c), m, (*chip, c)) for m, chip in enumerate(chips)]
    fwd = [remote(block_half(chip, c), block_half(chip, c), 3 + m, sib) for m, chip in enumerate(chips)]

    @pl.when(step == 0)
    def _():
        wob_s[...] = wo_ref[...].astype(MXU_DTYPE)
        local.start()
        for cp in ici:
            cp.start()

    @pl.when(step == n_steps // 2)
    def _():
        for m, chip in enumerate(chips):
            remote(block_half(chip, c), block_half(chip, c), m, sib).wait_recv()
            fwd[m].start()

    @pl.when(step == n_steps - 1)
    def _():
        for m, chip in enumerate(chips):
            remote(block_half(chip, 1 - c), block_half(chip, 1 - c), 3 + m, sib).wait_recv()
        for cp in ici + fwd:
            cp.wait_send()
        local.wait()


def _chip_blocks_shape(s, n_sub):
    return jax.ShapeDtypeStruct((3, n_sub) + s.shape[1:], s.dtype)


def _sum_chip_blocks(s, r, kc_arr, n_sub, name):
    _, rr, cc = s.shape

    def body(kc_ref, s_ref, r_ref, o_ref):
        o_ref[...] = ((s_ref[...] + r_ref[0].astype(F32)) + r_ref[1].astype(F32)) + r_ref[2].astype(F32)

    return pl.pallas_call(
        body, out_shape=jax.ShapeDtypeStruct((n_sub, 2 * rr, cc), F32),
        grid_spec=pltpu.PrefetchScalarGridSpec(
            num_scalar_prefetch=1, grid=(n_sub,),
            in_specs=[pl.BlockSpec((1, rr, cc), lambda q, kc: (n_sub * kc[0] + q, 0, 0)),
                      pl.BlockSpec((3, 1, rr, cc), lambda q, kc: (0, q, 0, 0))],
            out_specs=pl.BlockSpec((1, rr, cc), lambda q, kc: (q, kc[1], 0))),
        compiler_params=_cp(ARB), name=name,
    )(kc_arr, s, r)


def _swap_halves_and_sum(f_in, f_out, v):
    hi, ho = f_in.shape[1] // 2, f_out.shape[1] // 2
    n_dev = 8

    def body(fi_in, fo_in, v_ref, fi_ref, fo_ref, tot_ref, slots, send_sems, recv_sems):
        del fi_in, fo_in
        x, y, c, _ = _mesh_pos()
        sib = (x, y, 1 - c)
        me = 4 * x + 2 * y + c
        slots[me] = v_ref[...]
        si = fi_ref.at[:, pl.ds(pl.multiple_of(hi * c, hi), hi), :]
        so = fo_ref.at[:, pl.ds(pl.multiple_of(ho * c, ho), ho), :]

        def remote(ref, sem, to):
            return pltpu.make_async_remote_copy(src_ref=ref, dst_ref=ref, send_sem=send_sems.at[sem],
                                                recv_sem=recv_sems.at[sem], device_id=to, device_id_type=MESH)

        cps = [remote(si, n_dev - 1, sib), remote(so, n_dev, sib)]
        for d in range(1, n_dev):
            peer = (1 - x if d & 4 else x, 1 - y if d & 2 else y, 1 - c if d & 1 else c)
            cps.append(remote(slots.at[me], d - 1, peer))
        for cp in cps:
            cp.start()
        for cp in cps:
            cp.wait()
        total = slots[0]
        for dev in range(1, n_dev):
            total = total + slots[dev]
        tot_ref[...] = total

    hbm = pl.BlockSpec(memory_space=pl.ANY)
    vm = pl.BlockSpec(memory_space=pltpu.VMEM)
    return pl.pallas_call(
        body,
        out_shape=(jax.ShapeDtypeStruct(f_in.shape, F32), jax.ShapeDtypeStruct(f_out.shape, F32),
                   jax.ShapeDtypeStruct(v.shape, F32)),
        in_specs=[hbm, hbm, vm], out_specs=(hbm, hbm, vm), input_output_aliases={0: 0, 1: 1},
        scratch_shapes=[pltpu.VMEM((n_dev,) + v.shape, F32), pltpu.SemaphoreType.DMA((n_dev + 1,)),
                        pltpu.SemaphoreType.DMA((n_dev + 1,))],
        compiler_params=_cp(), name="swap_halves_and_sum",
    )(f_in, f_out, v)


def _out_projection_loss(yc, yl, x, target, wo, final_g):
    t = x.shape[0]
    tm = 512

    def body(yc_ref, yl_ref, x_ref, t_ref, wo_ref, fg_ref, do_ref, dob_ref, dy_ref, st_ref, y_wo):
        @pl.when(pl.program_id(0) == 0)
        def _():
            st_ref[...] = jnp.zeros_like(st_ref)

        y_wo[...] = _mm(yc_ref[...], wo_ref[0:D_PART, :]) + _mm(yl_ref[...], wo_ref[D_PART:2 * D_PART, :])

        def norm_loss_slab(s, carry):
            g_sum, loss_sum = carry
            rows = pl.ds(pl.multiple_of(s * SLAB, SLAB), SLAB)
            o = x_ref[rows, :] + y_wo[rows, :]
            r2 = lax.rsqrt(jnp.mean(o * o, axis=-1, keepdims=True) + RMS_EPS)
            ohat = o * r2
            fg = fg_ref[...]
            diff = ohat * fg - t_ref[rows, :]
            dout = diff * (1.0 / D_MODEL)
            gp = dout * fg
            do = r2 * (gp - ohat * jnp.mean(gp * ohat, axis=-1, keepdims=True))
            do_ref[rows, :] = do
            dob_ref[rows, :] = do.astype(MXU_DTYPE)
            loss = 0.5 * jnp.sum(jnp.sum(diff * diff, axis=-1, keepdims=True) * (1.0 / D_MODEL), axis=0, keepdims=True)
            return g_sum + jnp.sum(dout * ohat, axis=0, keepdims=True), loss_sum + loss

        g_sum, loss_sum = lax.fori_loop(0, tm // SLAB, norm_loss_slab,
                                        (jnp.zeros((1, D_MODEL), F32), jnp.zeros((1, 1), F32)))
        st_ref[0:1, :] += g_sum
        st_ref[1:2, :] += jnp.broadcast_to(loss_sum, (1, D_MODEL))
        dy_ref[...] = _mm_nt(dob_ref[...], wo_ref[...])

    row = lambda i: (i, 0)
    fix = lambda i: (0, 0)
    return pl.pallas_call(
        body, grid=(t // tm,),
        in_specs=[pl.BlockSpec((tm, D_PART), row), pl.BlockSpec((tm, D_PART), row),
                  pl.BlockSpec((tm, D_MODEL), row), pl.BlockSpec((tm, D_MODEL), row),
                  pl.BlockSpec((2 * D_PART, D_MODEL), fix), pl.BlockSpec((1, D_MODEL), fix)],
        out_specs=(pl.BlockSpec((tm, D_MODEL), row), pl.BlockSpec((tm, D_MODEL), row),
                   pl.BlockSpec((tm, 2 * D_PART), row), pl.BlockSpec((SUBLANES, D_MODEL), fix)),
        out_shape=(jax.ShapeDtypeStruct((t, D_MODEL), F32), jax.ShapeDtypeStruct((t, D_MODEL), MXU_DTYPE),
                   jax.ShapeDtypeStruct((t, 2 * D_PART), F32), jax.ShapeDtypeStruct((SUBLANES, D_MODEL), F32)),
        scratch_shapes=[pltpu.VMEM((tm, D_MODEL), F32)],
        compiler_params=_cp(ARB), name="out_projection_loss",
    )(yc, yl, x, target, wo, final_g)


def _input_grad(dproj, w12, x, do, ln_g, sb_in):
    t = x.shape[0]
    tm = 1024

    def body(dp_ref, w_ref, x_ref, do_ref, g_ref, s_ref, gx_ref, st_ref, r_ref, acc, send_sems, recv_sems):
        i, p = pl.program_id(0), pl.program_id(1)

        @pl.when((i == 0) & (p == 0))
        def _():
            st_ref[...] = jnp.zeros_like(st_ref)
            for cp in _chip_block_copies(s_ref, r_ref, CHUNKS_PER_BLOCK, send_sems, recv_sems):
                cp.start()

        @pl.when((i == t // tm - 1) & (p == N_PARTS - 1))
        def _():
            for cp in _chip_block_copies(s_ref, r_ref, CHUNKS_PER_BLOCK, send_sems, recv_sems):
                cp.wait()

        @pl.when(p == 0)
        def _():
            acc[...] = jnp.zeros_like(acc)

        acc[...] += _mm_nt(dp_ref[0], jnp.concatenate([w_ref[0], w_ref[1]], axis=1))

        @pl.when(p == N_PARTS - 1)
        def _():
            def norm_bwd_slab(s, g_sum):
                rows = pl.ds(pl.multiple_of(s * SLAB, SLAB), SLAB)
                xf = x_ref[rows, :]
                r = lax.rsqrt(jnp.mean(xf * xf, axis=-1, keepdims=True) + RMS_EPS)
                xhat = xf * r
                dxn = acc[rows, :]
                dxh = dxn * g_ref[...]
                gx_ref[rows, :] = do_ref[rows, :] + r * (dxh - xhat * jnp.mean(dxh * xhat, axis=-1, keepdims=True))
                return g_sum + jnp.sum(dxn * xhat, axis=0, keepdims=True)

            st_ref[0:1, :] += lax.fori_loop(0, tm // SLAB, norm_bwd_slab, jnp.zeros((1, D_MODEL), F32))

    row = lambda i, p: (i, 0)
    fix = lambda i, p: (0, 0)
    return pl.pallas_call(
        body, grid=(t // tm, N_PARTS),
        in_specs=[
            pl.BlockSpec((1, tm, D_PART), lambda i, p: (p, i, 0)),
            pl.BlockSpec((2, D_MODEL, CHUNK), lambda i, p: (p, 0, 0)),
            pl.BlockSpec((tm, D_MODEL), row), pl.BlockSpec((tm, D_MODEL), row), pl.BlockSpec((1, D_MODEL), fix),
            pl.BlockSpec(memory_space=pl.ANY)],
        out_specs=(pl.BlockSpec((tm, D_MODEL), row), pl.BlockSpec((SUBLANES, D_MODEL), fix),
                   pl.BlockSpec(memory_space=pl.ANY)),
        out_shape=(jax.ShapeDtypeStruct((t, D_MODEL), F32), jax.ShapeDtypeStruct((SUBLANES, D_MODEL), F32),
                   _chip_blocks_shape(sb_in, CHUNKS_PER_BLOCK)),
        scratch_shapes=[pltpu.VMEM((tm, D_MODEL), F32), pltpu.SemaphoreType.DMA((3,)), pltpu.SemaphoreType.DMA((3,))],
        compiler_params=_cp(ARB, ARB), name="input_grad",
    )(dproj, w12, x, do, ln_g, sb_in)


def _w_in_grad(xn, dproj, small):
    t = xn.shape[0]
    small_shape = pltpu.VMEM(small.shape, F32)

    def body(xn_ref, dp_ref, sm_ref, o_ref, ob_ref, red_ref, acc_s, r0, r1, r2, send_sems, recv_sems):
        _allreduce_behind(pl.program_id(0), (0, 1, 3, N_PARTS - 1), sm_ref, acc_s, (r0, r1, r2), red_ref, send_sems, recv_sems)
        g = _mm_tn(xn_ref[...], dp_ref[0])
        for s in range(2):
            o_ref[s] = g[:, CHUNK * s:CHUNK * (s + 1)]
            ob_ref[s] = g[:, CHUNK * s:CHUNK * (s + 1)].astype(jnp.bfloat16)

    whole = pl.BlockSpec(small.shape, lambda p: (0, 0))
    pair = pl.BlockSpec((2, D_MODEL, CHUNK), lambda p: (p, 0, 0))
    return pl.pallas_call(
        body, grid=(N_PARTS,),
        in_specs=[pl.BlockSpec((t, D_MODEL), lambda p: (0, 0)),
                  pl.BlockSpec((1, t, D_PART), lambda p: (p, 0, 0)), whole],
        out_specs=(pair, pair, whole),
        out_shape=(jax.ShapeDtypeStruct((N_CHUNKS, D_MODEL, CHUNK), F32),
                   jax.ShapeDtypeStruct((N_CHUNKS, D_MODEL, CHUNK), jnp.bfloat16), jax.ShapeDtypeStruct(small.shape, F32)),
        scratch_shapes=[small_shape] * 4 + [pltpu.SemaphoreType.DMA((3,)), pltpu.SemaphoreType.DMA((3,))],
        compiler_params=_cp(ARB), name="w_in_grad",
    )(xn, dproj, small)


def _w_out_grad(yc, yl, dob):
    t = yc.shape[0]
    tk = 2048

    def body(yc_ref, yl_ref, do_ref, o_ref, ob_ref):
        j, kk = pl.program_id(0), pl.program_id(1)

        def accumulate(y_ref):
            @pl.when(kk == 0)
            def _():
                o_ref[...] = jnp.zeros_like(o_ref)

            o_ref[...] += _mm_tn(y_ref[...], do_ref[...])

            @pl.when(kk == t // tk - 1)
            def _():
                ob_ref[...] = o_ref[...].astype(jnp.bfloat16)

        pl.when(j == 0)(functools.partial(accumulate, yc_ref))
        pl.when(j == 1)(functools.partial(accumulate, yl_ref))

    def rows_of(half):
        return lambda j, kk: (jnp.where(j == half, kk, 0), 0)

    half = pl.BlockSpec((D_PART, D_MODEL), lambda j, kk: (j, 0))
    out, out_b = pl.pallas_call(
        body, grid=(2, t // tk),
        in_specs=[pl.BlockSpec((tk, D_PART), rows_of(0)), pl.BlockSpec((tk, D_PART), rows_of(1)),
                  pl.BlockSpec((tk, D_MODEL), lambda j, kk: (kk, 0))],
        out_specs=(half, half),
        out_shape=(jax.ShapeDtypeStruct((2 * D_PART, D_MODEL), F32), jax.ShapeDtypeStruct((2 * D_PART, D_MODEL), jnp.bfloat16)),
        compiler_params=_cp(ARB, ARB), name="w_out_grad",
    )(yc, yl, dob)
    blocks = (N_CHIPS, 2 * D_PART // N_CHIPS, D_MODEL)
    return out.reshape(blocks), out_b.reshape(blocks)


def _for_groups(n, fn, init, unroll=UNROLL, stores=(), descending=False):
    assert unroll % 2 == 0 and n % unroll == 0

    def trip(j, carry):
        held = None
        for uu in range(unroll):
            idx = j * unroll + uu
            carry, values = fn(idx, carry)
            if uu % 2 == 0:
                held = values
                continue
            low_group = n - 1 - idx if descending else idx - 1
            rows = pl.ds(pl.multiple_of(low_group * SUBLANES, 2 * SUBLANES), 2 * SUBLANES)
            pairs = zip(values, held) if descending else zip(held, values)
            for store, (lo, hi) in zip(stores, pairs, strict=True):
                store(rows, jnp.concatenate([lo, hi], axis=0).astype(MXU_DTYPE))
        return carry

    return lax.fori_loop(0, n // unroll, trip, init)


def _rows_of(ref, *lead, cols=slice(None)):
    def store(rows, value):
        ref[(*lead, rows, cols)] = value

    return store


def _pvb(pv_ref, r):
    return jnp.broadcast_to(pv_ref[r:r + 1, :], (SUBLANES, pv_ref.shape[1]))


def _conv3(pv_ref, u, u1, u2):
    return (_pvb(pv_ref, PV_CONV_W) * u2 + _pvb(pv_ref, PV_CONV_W + 1) * u1) + _pvb(pv_ref, PV_CONV_W + 2) * u


def _conv4(pv_ref, v, v1, v2, v3):
    return ((((_pvb(pv_ref, PV_LRU_W) * v3 + _pvb(pv_ref, PV_LRU_W + 1) * v2) + _pvb(pv_ref, PV_LRU_W + 2) * v1)
             + _pvb(pv_ref, PV_LRU_W + 3) * v) + _pvb(pv_ref, PV_LRU_B))


def _mixer_forward(proj, pvec, wai, w_out):
    t = proj.shape[1]
    tb = 1024
    ng = tb // SUBLANES
    nt = t // tb

    def body(bg_ref, cg_ref, xc_ref, gc_ref, xl_ref, gl_ref, pv_ref, wai_ref, wo_ref,
             yc_ref, yl_ref, h_ref, u_s, r_ref, ig_ref, wo4_ref,
             ucp_s, xlp_s, ls_s, hbuf_s, ub_s, gate_s, wob_s, local_sem, send_sems, recv_sems):
        _gather_w_out(pl.program_id(0) * nt + pl.program_id(1), NS * nt, wo_ref, wob_s, wo4_ref, local_sem, send_sems, recv_sems)

        @pl.when(pl.program_id(1) == 0)
        def _():
            ucp_s[...] = jnp.zeros_like(ucp_s)
            xlp_s[...] = jnp.zeros_like(xlp_s)
            hbuf_s[...] = jnp.zeros_like(hbuf_s)

        row = lax.broadcasted_iota(jnp.int32, (SUBLANES, LW), 0)
        ls_s[...] = _log_sigmoid(_pvb(pv_ref, PV_LAM))

        def conv_group(g, carry):
            ucp, xlp = carry
            sl = pl.ds(pl.multiple_of(g * SUBLANES, SUBLANES), SUBLANES)
            uc = cg_ref[sl, :] * xc_ref[sl, :]
            v = _conv3(pv_ref, uc, _shift_down(uc, ucp, 1, row), _shift_down(uc, ucp, 2, row))
            yc = bg_ref[sl, :] * v
            rr = lax.rsqrt(_head_mean(yc * yc, CONV_HEAD) + RMS_EPS)
            gc = gc_ref[sl, :]
            zc = ((yc * rr) * _pvb(pv_ref, PV_CG)) * (gc * _sigmoid(gc))
            xl = xl_ref[sl, :]
            u = _conv4(pv_ref, xl, _shift_down(xl, xlp, 1, row), _shift_down(xl, xlp, 2, row), _shift_down(xl, xlp, 3, row))
            u_s[sl, :] = u
            return (uc, xl), (zc, u)

        ucp, xlp = _for_groups(ng, conv_group, (ucp_s[...], xlp_s[...]), unroll=2 * UNROLL,
                               stores=(_rows_of(yc_ref), _rows_of(ub_s)))
        ucp_s[...] = ucp
        xlp_s[...] = xlp

        gate_s[...] = _mm(ub_s[...], wai_ref[0])

        def lru_group(g, h_before):
            sl = pl.ds(pl.multiple_of(g * SUBLANES, SUBLANES), SUBLANES)
            u = u_s[sl, :]
            r = _sigmoid(gate_s[sl, 0:LW] + _pvb(pv_ref, PV_BA))
            ig = _sigmoid(gate_s[sl, LW:2 * LW] + _pvb(pv_ref, PV_BI))
            r_ref[sl, :] = r
            ig_ref[sl, :] = ig
            a, _, mult, _ = _decay(r, ls_s[...])
            A, B = _scan8_fwd(a, mult * (ig * u), row)
            h = B + A * jnp.broadcast_to(h_before[SUBLANES - 1:SUBLANES, :], (SUBLANES, LW))
            h_ref[sl, :] = h
            rr = lax.rsqrt(_head_mean(h * h, LRU_HEAD) + RMS_EPS)
            gl = gl_ref[sl, :]
            return h, (((h * rr) * _pvb(pv_ref, PV_LG)) * (gl * _sigmoid(gl)),)

        hbuf_s[...] = _for_groups(ng, lru_group, hbuf_s[...], unroll=2 * UNROLL, stores=(_rows_of(yl_ref),))

    def part(p):
        return pl.BlockSpec((None, tb, LW), lambda c, i: (2 * p + c // STRIPS_PER_CHUNK, i, c % STRIPS_PER_CHUNK))

    strip = pl.BlockSpec((tb, LW), lambda c, i: (i, c))
    return pl.pallas_call(
        body, grid=(NS, nt),
        in_specs=[part(p) for p in range(N_PARTS)] + [
            pl.BlockSpec((PV_ROWS, LW), lambda c, i: (0, c)),
            pl.BlockSpec((1, LW, 2 * LW), lambda c, i: (c, 0, 0)),
            pl.BlockSpec(w_out.shape, lambda c, i: (0, 0))],
        out_specs=(strip,) * 6 + (pl.BlockSpec(memory_space=pl.ANY),),
        out_shape=(jax.ShapeDtypeStruct((t, D_PART), MXU_DTYPE),) * 2 + (jax.ShapeDtypeStruct((t, D_PART), F32),) * 4 + (
            jax.ShapeDtypeStruct((N_CHIPS,) + w_out.shape, MXU_DTYPE),),
        scratch_shapes=[pltpu.VMEM((SUBLANES, LW), F32), pltpu.VMEM((SUBLANES, LW), F32), pltpu.VMEM((SUBLANES, LW), F32),
                        pltpu.VMEM((SUBLANES, LW), F32), pltpu.VMEM((tb, LW), MXU_DTYPE),
                        pltpu.VMEM((tb, 2 * LW), F32), pltpu.VMEM(w_out.shape, MXU_DTYPE),
                        pltpu.SemaphoreType.DMA, pltpu.SemaphoreType.DMA((6,)), pltpu.SemaphoreType.DMA((6,))],
        compiler_params=_cp(ARB, ARB), name="mixer_forward",
    )(proj, proj, proj, proj, proj, proj, pvec, wai, w_out)


def _mixer_backward(proj, h, u, r, ig, dy, pvec, wai, sb_out):
    t = proj.shape[1]
    tb = 1024
    ng = tb // SUBLANES
    nt = t // tb
    gpb = tb // SUBLANES

    def body(bg_ref, cg_ref, xc_ref, gc_ref, xl_ref, gl_ref, h_ref, u_ref, r_ref, ig_ref, dyc_ref, dyl_ref,
             cgh_ref, xch_ref, xlh_ref, hh_ref, pv_ref, wai_ref, so_ref,
             dp_ref, gw_ref, sv_ref, ro_ref,
             ls_s, ub_s, uce_s, xle_s, he_s, dgb_s, du_s, gbuf_s,
             acc_s, an_s, dvn_s, dun_s, send_sems, recv_sems):
        i = pl.program_id(1)
        first_block = i == nt - 1

        @pl.when((pl.program_id(0) == 0) & (i == 0))
        def _():
            for cp in _chip_block_copies(so_ref, ro_ref, 1, send_sems, recv_sems):
                cp.start()

        @pl.when((pl.program_id(0) == NS - 1) & (i == nt - 1))
        def _():
            for cp in _chip_block_copies(so_ref, ro_ref, 1, send_sems, recv_sems):
                cp.wait()

        @pl.when(i == 0)
        def _():
            acc_s[...] = jnp.zeros_like(acc_s)
            gw_ref[...] = jnp.zeros_like(gw_ref)
            an_s[...] = jnp.zeros_like(an_s)
            dvn_s[...] = jnp.zeros_like(dvn_s)
            dun_s[...] = jnp.zeros_like(dun_s)
            gbuf_s[...] = jnp.zeros_like(gbuf_s)

        row = lax.broadcasted_iota(jnp.int32, (SUBLANES, LW), 0)
        ls_s[...] = _log_sigmoid(_pvb(pv_ref, PV_LAM))
        keep = jnp.where(first_block, 0.0, 1.0)
        uce_s[0:SUBLANES, :] = (cgh_ref[...] * xch_ref[...]) * keep
        xle_s[0:SUBLANES, :] = xlh_ref[...] * keep
        he_s[0:SUBLANES, :] = hh_ref[...] * keep
        xle_s[SUBLANES:SUBLANES + tb, :] = xl_ref[...]
        he_s[SUBLANES:SUBLANES + tb, :] = h_ref[...]

        uce_s[SUBLANES:SUBLANES + tb, :] = cg_ref[...] * xc_ref[...]

        def acc_add(k, v):
            acc_s[k] += v

        def main_group(gi, carry):
            a_next, dv_next, g_next = carry
            g = ng - 1 - gi
            r0 = pl.multiple_of(g * SUBLANES, SUBLANES)
            sl = pl.ds(r0, SUBLANES)
            sl_e = pl.ds(r0 + SUBLANES, SUBLANES)
            lsb = ls_s[...]
            u = u_ref[sl, :]
            r = r_ref[sl, :]
            ig = ig_ref[sl, :]
            a, e2, mult, inv_mult = _decay(r, lsb)
            gl = gl_ref[sl, :]
            sg = _sigmoid(gl)
            s_l = gl * sg
            h8 = he_s[sl_e, :]
            hprev = _shift_down(h8, he_s[sl, :], 1, row)
            rr = lax.rsqrt(_head_mean(h8 * h8, LRU_HEAD) + RMS_EPS)
            n = h8 * rr
            dz = dyl_ref[sl, :]
            lg = _pvb(pv_ref, PV_LG)
            acc_add(PV_LG, (dz * n) * s_l)
            p5 = ((dz * n) * lg) * (sg * (1.0 + gl * (1.0 - sg)))
            dn = (dz * lg) * s_l
            dh = rr * (dn - n * _head_mean(dn * n, LRU_HEAD))
            A, B = _scan8_rev(_shift_up(a, a_next, 1, row), dh, row)
            gg = B + A * jnp.broadcast_to(g_next[0:1, :], (SUBLANES, LW))
            da = gg * hprev
            iu = ig * u
            diu = gg * mult
            dla = da * a - (gg * iu) * (e2 * inv_mult)
            acc_add(PV_LAM, dla * (RG_LRU_C * r))
            dra = (dla * (RG_LRU_C * lsb)) * (r * (1.0 - r))
            dia = (diu * u) * (ig * (1.0 - ig))
            acc_add(PV_BA, dra)
            acc_add(PV_BI, dia)
            du_s[sl, :] = diu * ig
            bg = bg_ref[sl, :]
            gc = gc_ref[sl, :]
            uc = uce_s[sl_e, :]
            ucp = uce_s[sl, :]
            uc1 = _shift_down(uc, ucp, 1, row)
            uc2 = _shift_down(uc, ucp, 2, row)
            v = _conv3(pv_ref, uc, uc1, uc2)
            yc = bg * v
            rrc = lax.rsqrt(_head_mean(yc * yc, CONV_HEAD) + RMS_EPS)
            nc = yc * rrc
            sgc = _sigmoid(gc)
            s_c = gc * sgc
            dzc = dyc_ref[sl, :]
            cgain = _pvb(pv_ref, PV_CG)
            acc_add(PV_CG, (dzc * nc) * s_c)
            p3 = ((dzc * nc) * cgain) * (sgc * (1.0 + gc * (1.0 - sgc)))
            dnc = (dzc * cgain) * s_c
            dyc = rrc * (dnc - nc * _head_mean(dnc * nc, CONV_HEAD))
            dv = dyc * bg
            duc = (_pvb(pv_ref, PV_CONV_W + 2) * dv + _pvb(pv_ref, PV_CONV_W + 1) * _shift_up(dv, dv_next, 1, row)
                   + _pvb(pv_ref, PV_CONV_W) * _shift_up(dv, dv_next, 2, row))
            acc_add(PV_CONV_W + 2, dv * uc)
            acc_add(PV_CONV_W + 1, dv * uc1)
            acc_add(PV_CONV_W, dv * uc2)
            return (a, dv, gg), (dyc * v, duc * xc_ref[sl, :], duc * cg_ref[sl, :], p3, p5, dra, dia, u)

        a_next, dv_next, g_next = _for_groups(
            ng, main_group, (an_s[...], dvn_s[...], gbuf_s[...]), descending=True,
            stores=(_rows_of(dp_ref, 0), _rows_of(dp_ref, 1), _rows_of(dp_ref, 2), _rows_of(dp_ref, 3), _rows_of(dp_ref, 5),
                    _rows_of(dgb_s, cols=slice(0, LW)), _rows_of(dgb_s, cols=slice(LW, 2 * LW)), _rows_of(ub_s)))
        an_s[...] = a_next
        dvn_s[...] = dv_next
        gbuf_s[...] = g_next

        dgb = dgb_s[...]
        du_s[...] += _mm_nt(dgb, wai_ref[0])
        gw_ref[0] += _mm_tn(ub_s[...], dgb)

        def lru_conv_group(gi, du_next):
            g = ng - 1 - gi
            r0 = pl.multiple_of(g * SUBLANES, SUBLANES)
            sl = pl.ds(r0, SUBLANES)
            du = du_s[sl, :]
            xl = xle_s[pl.ds(r0 + SUBLANES, SUBLANES), :]
            xlp = xle_s[sl, :]
            acc_add(PV_LRU_B, du)
            acc_add(PV_LRU_W + 3, du * xl)
            acc_add(PV_LRU_W + 2, du * _shift_down(xl, xlp, 1, row))
            acc_add(PV_LRU_W + 1, du * _shift_down(xl, xlp, 2, row))
            acc_add(PV_LRU_W, du * _shift_down(xl, xlp, 3, row))
            dxl = (((_pvb(pv_ref, PV_LRU_W + 3) * du + _pvb(pv_ref, PV_LRU_W + 2) * _shift_up(du, du_next, 1, row))
                    + _pvb(pv_ref, PV_LRU_W + 1) * _shift_up(du, du_next, 2, row))
                   + _pvb(pv_ref, PV_LRU_W) * _shift_up(du, du_next, 3, row))
            return du, (dxl,)

        dun_s[...] = _for_groups(ng, lru_conv_group, dun_s[...], descending=True, stores=(_rows_of(dp_ref, 4),))

        @pl.when(first_block)
        def _():
            sv_ref[...] = jnp.zeros_like(sv_ref)
            for k in range(N_ACC):
                tot = jnp.sum(acc_s[k], axis=0, keepdims=True)
                if k == PV_LAM:
                    tot = tot / (1.0 + jnp.exp(pv_ref[PV_LAM:PV_LAM + 1, :]))
                sv_ref[k:k + 1, :] = tot

    def part(p):
        return pl.BlockSpec((None, tb, LW), lambda c, i: (2 * p + c // STRIPS_PER_CHUNK, nt - 1 - i, c % STRIPS_PER_CHUNK))

    def halo(p):
        return pl.BlockSpec((None, SUBLANES, LW), lambda c, i: (2 * p + c // STRIPS_PER_CHUNK,
                                                                jnp.maximum((nt - 1 - i) * gpb - 1, 0), c % STRIPS_PER_CHUNK))

    strip = pl.BlockSpec((tb, LW), lambda c, i: (nt - 1 - i, c))
    big = pltpu.VMEM((tb, LW), F32)
    big_e = pltpu.VMEM((tb + SUBLANES, LW), F32)
    small = pltpu.VMEM((SUBLANES, LW), F32)
    outs = pl.pallas_call(
        body, grid=(NS, nt),
        in_specs=[part(p) for p in range(N_PARTS)] + [
            strip, strip, strip, strip, strip, pl.BlockSpec((tb, LW), lambda c, i: (nt - 1 - i, NS + c)),
            halo(1), halo(2), halo(4),
            pl.BlockSpec((SUBLANES, LW), lambda c, i: (jnp.maximum((nt - 1 - i) * gpb - 1, 0), c)),
            pl.BlockSpec((PV_ROWS, LW), lambda c, i: (0, c)),
            pl.BlockSpec((1, LW, 2 * LW), lambda c, i: (c, 0, 0)),
            pl.BlockSpec(memory_space=pl.ANY)],
        out_specs=(pl.BlockSpec((N_PARTS, tb, LW), lambda c, i: (0, nt - 1 - i, c)),
                   pl.BlockSpec((1, LW, 2 * LW), lambda c, i: (c, 0, 0)),
                   pl.BlockSpec((PV_ROWS, LW), lambda c, i: (0, c)),
                   pl.BlockSpec(memory_space=pl.ANY)),
        out_shape=(jax.ShapeDtypeStruct((N_PARTS, t, D_PART), MXU_DTYPE),
                   jax.ShapeDtypeStruct((NS, LW, 2 * LW), F32), jax.ShapeDtypeStruct((PV_ROWS, D_PART), F32),
                   _chip_blocks_shape(sb_out, 1)),
        scratch_shapes=[small, pltpu.VMEM((tb, LW), MXU_DTYPE), big_e, big_e, big_e,
                        pltpu.VMEM((tb, 2 * LW), MXU_DTYPE), big, small,
                        pltpu.VMEM((N_ACC, SUBLANES, LW), F32), small, small, small,
                        pltpu.SemaphoreType.DMA((3,)), pltpu.SemaphoreType.DMA((3,))],
        compiler_params=_cp(ARB, ARB), name="mixer_backward",
    )(proj, proj, proj, proj, proj, proj, h, u, r, ig, dy, dy, proj, proj, proj, h, pvec, wai, sb_out)
    return outs


def _adamw(w, g, m, v):
    m = ADAM_B1 * m + (1.0 - ADAM_B1) * g
    v = ADAM_B2 * v + (1.0 - ADAM_B2) * (g * g)
    m_hat = m / (1.0 - ADAM_B1 ** ADAM_STEP)
    v_hat = v / (1.0 - ADAM_B2 ** ADAM_STEP)
    delta = -ADAM_LR * (m_hat / (jnp.sqrt(v_hat) + ADAM_EPS) + ADAM_WD * w)
    return delta, m, v


def _adam_w_in(w, m, v, g3):
    rows, cols = w.shape
    tr = 128

    def body(w_ref, m_ref, v_ref, g_ref, go_ref, d_ref, mo_ref, vo_ref):
        for s in range(CHUNKS_PER_BLOCK):
            cs = slice(CHUNK * s, CHUNK * (s + 1))
            g = g_ref[s]
            d, mn, vn = _adamw(w_ref[:, cs], g, m_ref[:, cs], v_ref[:, cs])
            go_ref[:, cs] = g
            d_ref[:, cs] = d
            mo_ref[:, cs] = mn
            vo_ref[:, cs] = vn

    blk = pl.BlockSpec((tr, cols), lambda i: (i, 0))
    return pl.pallas_call(
        body, grid=(rows // tr,),
        in_specs=[blk, blk, blk, pl.BlockSpec((CHUNKS_PER_BLOCK, tr, CHUNK), lambda i: (0, i, 0))],
        out_specs=(blk,) * 4, out_shape=(jax.ShapeDtypeStruct(w.shape, F32),) * 4,
        compiler_params=_cp(ARB), name="adam_w_in",
    )(w, m, v, g3)


def _adam_w_out(w, m, v, g):
    rows, cols = w.shape
    tr = 128

    def body(w_ref, m_ref, v_ref, g_ref, d_ref, mo_ref, vo_ref):
        d_ref[...], mo_ref[...], vo_ref[...] = _adamw(w_ref[...], g_ref[...], m_ref[...], v_ref[...])

    blk = pl.BlockSpec((tr, cols), lambda i: (i, 0))
    return pl.pallas_call(
        body, grid=(rows // tr,), in_specs=[blk] * 4, out_specs=(blk,) * 3,
        out_shape=(jax.ShapeDtypeStruct(w.shape, F32),) * 3,
        compiler_params=_cp(ARB), name="adam_w_out",
    )(w, m, v, g)


def _adam_small(ws, ms, vs, gs):
    n = len(ws)

    def body(*refs):
        w_r, m_r, v_r, g_r = refs[0:n], refs[n:2 * n], refs[2 * n:3 * n], refs[3 * n:4 * n]
        d_o, m_o, v_o = refs[4 * n:5 * n], refs[5 * n:6 * n], refs[6 * n:7 * n]
        for j in range(n):
            d_o[j][...], m_o[j][...], v_o[j][...] = _adamw(w_r[j][...], g_r[j][...], m_r[j][...], v_r[j][...])

    vm = pl.BlockSpec(memory_space=pltpu.VMEM)
    shapes = tuple(jax.ShapeDtypeStruct(w.shape, F32) for w in ws)
    outs = pl.pallas_call(
        body, in_specs=[vm] * (4 * n), out_specs=(vm,) * (3 * n), out_shape=shapes * 3,
        compiler_params=_cp(), name="adam_small",
    )(*ws, *ms, *vs, *gs)
    return outs[0:n], outs[n:2 * n], outs[2 * n:3 * n]


def _block_diag_strips(w):
    w4 = w.reshape(NS, HEADS_PER_STRIP, LRU_HEAD, LRU_HEAD)
    rows = [jnp.pad(w4[:, hh], ((0, 0), (0, 0), (LRU_HEAD * hh, LW - LRU_HEAD * (hh + 1)))) for hh in range(HEADS_PER_STRIP)]
    return jnp.concatenate(rows, axis=1)


def _strip_diag_blocks(g):
    g5 = g.reshape(NS, HEADS_PER_STRIP, LRU_HEAD, HEADS_PER_STRIP, LRU_HEAD)
    return jnp.stack([g5[:, hh, :, hh, :] for hh in range(HEADS_PER_STRIP)], axis=1).reshape(NS * HEADS_PER_STRIP, LRU_HEAD, LRU_HEAD)


def kernel(x, ln_g, w_in, conv_w, lru_conv_w, lru_conv_b, w_a, b_a, w_i, b_i, lam, conv_out_g, lru_out_g, w_out, final_g, loss_target, m_ln_g, m_w_in, m_conv_w, m_lru_conv_w, m_lru_conv_b, m_w_a, m_b_a, m_w_i, m_b_i, m_lam, m_conv_out_g, m_lru_out_g, m_w_out, m_final_g, v_ln_g, v_w_in, v_conv_w, v_lru_conv_w, v_lru_conv_b, v_w_a, v_b_a, v_w_i, v_b_i, v_lam, v_conv_out_g, v_lru_out_g, v_w_out, v_final_g):
    xi, yi, ci = lax.axis_index("x"), lax.axis_index("y"), lax.axis_index("c")
    k = 2 * xi + yi
    t = x.shape[1]
    x2 = x.reshape(t, D_MODEL)
    tgt2 = loss_target.reshape(t, D_MODEL)
    row = lambda a: a.reshape(1, -1)

    small = jnp.concatenate([conv_w, lru_conv_w, jnp.zeros((1, conv_w.shape[1]), F32)], axis=0)
    proj, xn, w12, sm4 = _gather_in_projection(x2, row(ln_g), w_in, small)
    convs = jnp.transpose(sm4, (1, 0, 2)).reshape(SUBLANES, D_PART)
    pvec = jnp.concatenate(
        [convs[0:7], row(lru_conv_b), row(b_a), row(b_i), row(lam), row(conv_out_g), row(lru_out_g),
         jnp.zeros((PV_ROWS - N_ACC, D_PART), F32)], axis=0)
    wai = jnp.concatenate([_block_diag_strips(w_a), _block_diag_strips(w_i)], axis=2).astype(MXU_DTYPE)

    c_arr = jnp.reshape(ci, (1,)).astype(jnp.int32)
    kc_arr = jnp.stack([k, ci]).astype(jnp.int32)
    yc, yl, h, u, r, ig, wo4 = _mixer_forward(proj, pvec, wai, w_out)
    wo = wo4.reshape(2 * D_PART, D_MODEL)
    do, dob, dy, st_out = _out_projection_loss(yc, yl, x2, tgt2, wo, row(final_g))
    go4, go4b = _w_out_grad(yc, yl, dob)
    s_out, sb_out = _add_own_half(go4, _exchange_sibling_halves(go4b, "exchange_sibling_halves_out"), c_arr, "add_own_half_out")
    dproj, g_wai, svec, r2o = _mixer_backward(proj, h, u, r, ig, dy, pvec, wai, sb_out)
    gwa = _strip_diag_blocks(g_wai[:, :, 0:LW]).reshape(LRU_HEAD, D_PART)
    gwi = _strip_diag_blocks(g_wai[:, :, LW:2 * LW]).reshape(LRU_HEAD, D_PART)
    g12, g12b, red = _w_in_grad(xn, dproj, jnp.concatenate([svec, st_out, gwa, gwi], axis=0))
    s_in, sb_in = _add_own_half(g12, _exchange_sibling_halves(g12b, "exchange_sibling_halves_in"), c_arr, "add_own_half_in")
    grad_x, st_in, r2i = _input_grad(dproj, w12, x2, do, row(ln_g), sb_in)
    f_in = _sum_chip_blocks(s_in, r2i, kc_arr, CHUNKS_PER_BLOCK, "sum_chip_blocks_in")
    f_out = _sum_chip_blocks(s_out, r2o, kc_arr, 1, "sum_chip_blocks_out")
    f_in, f_out, red_ln = _swap_halves_and_sum(f_in, f_out, st_in)
    r_out = PV_ROWS
    r_wa = PV_ROWS + SUBLANES
    r_wi = r_wa + LRU_HEAD
    loss = red[r_out + 1, 0]

    g_w_in, d_w_in, nm_w_in, nv_w_in = _adam_w_in(w_in, m_w_in, v_w_in, f_in)
    g_w_out = f_out[0]
    d_w_out, nm_w_out, nv_w_out = _adam_w_out(w_out, m_w_out, v_w_out, g_w_out)

    ncol = conv_w.shape[1]
    conv_cols = lax.dynamic_slice(red, (0, k * ncol), (SUBLANES, ncol))
    g_small = {
        "ln_g": red_ln[0], "conv_w": conv_cols[0:3], "lru_conv_w": conv_cols[3:7], "lru_conv_b": red[PV_LRU_B],
        "w_a": red[r_wa:r_wa + LRU_HEAD].reshape(w_a.shape), "b_a": red[PV_BA],
        "w_i": red[r_wi:r_wi + LRU_HEAD].reshape(w_i.shape), "b_i": red[PV_BI], "lam": red[PV_LAM],
        "conv_out_g": red[PV_CG], "lru_out_g": red[PV_LG], "final_g": red[r_out],
    }
    w_small = {"ln_g": ln_g, "conv_w": conv_w, "lru_conv_w": lru_conv_w, "lru_conv_b": lru_conv_b, "w_a": w_a, "b_a": b_a,
               "w_i": w_i, "b_i": b_i, "lam": lam, "conv_out_g": conv_out_g, "lru_out_g": lru_out_g, "final_g": final_g}
    m_small = {"ln_g": m_ln_g, "conv_w": m_conv_w, "lru_conv_w": m_lru_conv_w, "lru_conv_b": m_lru_conv_b, "w_a": m_w_a,
               "b_a": m_b_a, "w_i": m_w_i, "b_i": m_b_i, "lam": m_lam, "conv_out_g": m_conv_out_g,
               "lru_out_g": m_lru_out_g, "final_g": m_final_g}
    v_small = {"ln_g": v_ln_g, "conv_w": v_conv_w, "lru_conv_w": v_lru_conv_w, "lru_conv_b": v_lru_conv_b, "w_a": v_w_a,
               "b_a": v_b_a, "w_i": v_w_i, "b_i": v_b_i, "lam": v_lam, "conv_out_g": v_conv_out_g,
               "lru_out_g": v_lru_out_g, "final_g": v_final_g}
    names = list(w_small)
    as2d = lambda a: a.reshape(1, -1) if a.ndim == 1 else a
    d_s, m_s, v_s = _adam_small([as2d(w_small[n]) for n in names], [as2d(m_small[n]) for n in names],
                                [as2d(v_small[n]) for n in names], [as2d(g_small[n]) for n in names])
    back = lambda n, a: a.reshape(w_small[n].shape)
    grads = {n: g_small[n] for n in names}
    deltas = {n: back(n, a) for n, a in zip(names, d_s)}
    new_m = {n: back(n, a) for n, a in zip(names, m_s)}
    new_v = {n: back(n, a) for n, a in zip(names, v_s)}
    grads["w_in"], deltas["w_in"], new_m["w_in"], new_v["w_in"] = g_w_in, d_w_in, nm_w_in, nv_w_in
    grads["w_out"], deltas["w_out"], new_m["w_out"], new_v["w_out"] = g_w_out, d_w_out, nm_w_out, nv_w_out

    order = ["ln_g", "w_in", "conv_w", "lru_conv_w", "lru_conv_b", "w_a", "b_a", "w_i", "b_i", "lam", "conv_out_g",
             "lru_out_g", "w_out", "final_g"]
    return (loss, grad_x.reshape(x.shape), *[grads[n] for n in order], *[deltas[n] for n in order],
            *[new_m[n] for n in order], *[new_v[n] for n in order])
```

```python
import functools

import jax
import jax.numpy as jnp
from jax import lax
from jax.experimental import pallas as pl
from jax.experimental.pallas import tpu as pltpu

F32 = jnp.float32
MXU_DTYPE = jnp.bfloat16

D_MODEL = 1024
D_PART = 1024
N_PARTS = 6
CHUNK = 512
CHUNKS_PER_BLOCK = 3
N_CHUNKS = 12
N_CHIPS = 4
SUBLANES = 8
LANES = 128
LW = 256
UNROLL = 8
NS = D_PART // LW
STRIPS_PER_CHUNK = CHUNK // LW
CONV_HEAD = 128
LRU_HEAD = 64
HEADS_PER_STRIP = LW // LRU_HEAD
RMS_EPS = 1e-6
RG_LRU_C = 8.0
ADAM_LR = 0.001
ADAM_B1 = 0.9
ADAM_B2 = 0.999
ADAM_EPS = 1e-08
ADAM_WD = 0.01
ADAM_STEP = 10

PV_CONV_W = 0
PV_LRU_W = 3
PV_LRU_B = 7
PV_BA = 8
PV_BI = 9
PV_LAM = 10
PV_CG = 11
PV_LG = 12
PV_ROWS = 16
N_ACC = 13

SLAB = 128
MESH = pl.DeviceIdType.MESH
VMEM_LIMIT = 56 * 1024 * 1024
ARB = "arbitrary"


def _cp(*sem, **kw):
    return pltpu.CompilerParams(dimension_semantics=sem or None, vmem_limit_bytes=VMEM_LIMIT, **kw)


def _mm(a, b):
    return jnp.dot(a, b, preferred_element_type=F32)


def _mm_nt(a, b):
    return lax.dot_general(a, b, (((1,), (1,)), ((), ())), preferred_element_type=F32)


def _mm_tn(a, b):
    return lax.dot_general(a, b, (((0,), (0,)), ((), ())), preferred_element_type=F32)


def _sigmoid(x):
    return 0.5 * jnp.tanh(0.5 * x) + 0.5


def _log_sigmoid(x):
    z = jnp.exp(-jnp.abs(x))
    u = 1.0 + z
    log1p = jnp.where(u == 1.0, z, jnp.log(u) * z / (u - 1.0))
    return jnp.minimum(x, 0.0) - log1p


def _head_mean(z, head):
    out = []
    for k in range(z.shape[1] // LANES):
        zk = z[:, LANES * k:LANES * (k + 1)]
        if head == LANES:
            m = jnp.sum(zk, axis=-1, keepdims=True) * (1.0 / head)
            out.append(jnp.broadcast_to(m, zk.shape))
        else:
            lo = lax.broadcasted_iota(jnp.int32, zk.shape, 1) < head
            s_lo = jnp.sum(jnp.where(lo, zk, 0.0), axis=-1, keepdims=True)
            s_hi = jnp.sum(jnp.where(lo, 0.0, zk), axis=-1, keepdims=True)
            out.append(jnp.where(lo, s_lo, s_hi) * (1.0 / head))
    return jnp.concatenate(out, axis=1)


def _shift_down(cur, prev, d, row):
    return pltpu.roll(jnp.where(row < SUBLANES - d, cur, prev), d, 0)


def _shift_up(cur, nxt, d, row):
    return pltpu.roll(jnp.where(row >= d, cur, nxt), SUBLANES - d, 0)


def _scan8_fwd(a, b, row):
    A, B = a, b
    for d in (1, 2, 4):
        m = row >= d
        a_s = jnp.where(m, pltpu.roll(A, d, 0), 1.0)
        b_s = jnp.where(m, pltpu.roll(B, d, 0), 0.0)
        B = A * b_s + B
        A = A * a_s
    return A, B


def _scan8_rev(a, b, row):
    A, B = a, b
    for d in (1, 2, 4):
        m = row < SUBLANES - d
        a_s = jnp.where(m, pltpu.roll(A, SUBLANES - d, 0), 1.0)
        b_s = jnp.where(m, pltpu.roll(B, SUBLANES - d, 0), 0.0)
        B = A * b_s + B
        A = A * a_s
    return A, B


def _decay(r, lsb):
    la = (RG_LRU_C * r) * lsb
    a = jnp.exp(la)
    e2 = a * a
    em = -jnp.tanh(la) * (1.0 + e2)
    inv_mult = lax.rsqrt(em)
    return a, e2, em * inv_mult, inv_mult


def _mesh_pos():
    x, y, c = lax.axis_index("x"), lax.axis_index("y"), lax.axis_index("c")
    chips = [(1 - x, y), (x, 1 - y), (1 - x, 1 - y)]
    return x, y, c, chips


def _gather_in_projection(x, ln_g, w_in, small):
    t = x.shape[0]
    rb_x = 512
    rb_mm = 1024
    n_mm = t // rb_mm
    half = w_in.shape[0] // 2

    def body(x_hbm, g_ref, wi_ref, sm_ref, proj_hbm, xn_ref, w12_ref, sm4_ref,
             xbuf, obuf, x_sems, o_sems, send_sems, recv_sems):
        x_, y_, c, chips = _mesh_pos()
        k = 2 * x_ + y_
        sib = (x_, y_, 1 - c)
        sm4_ref[k] = sm_ref[...]

        def remote(ref, sem, to):
            return pltpu.make_async_remote_copy(src_ref=ref, dst_ref=ref, send_sem=send_sems.at[sem],
                                                recv_sem=recv_sems.at[sem], device_id=to, device_id_type=MESH)

        def chunk_of(chip, s):
            return CHUNKS_PER_BLOCK * (2 * chip[0] + chip[1]) + s

        def piece(q, core, first=0, rows=half):
            return w12_ref.at[q, pl.ds(pl.multiple_of(half * core + first, SUBLANES * 2), rows), :]

        nbr_x, nbr_y, diagonal = chips
        quarter = half // 2
        DIAG = [(0, 0, half, 0), (1, 0, quarter, 0), (1, quarter, quarter, 1), (2, 0, half, 1)]
        ici = lambda m, s: 2 * s + m
        dgn = lambda j: 6 + j
        to_sib = 10
        sml = lambda m: 20 + m

        sends = []
        for s in range(CHUNKS_PER_BLOCK):
            w12_ref[chunk_of((x_, y_), s)] = wi_ref[:, CHUNK * s:CHUNK * (s + 1)].astype(MXU_DTYPE)
            for m, chip in enumerate((nbr_x, nbr_y)):
                sends.append(remote(piece(chunk_of((x_, y_), s), c), ici(m, s), (*chip, c)))
                sends[-1].start()
        for m, chip in enumerate(chips):
            sends.append(remote(sm4_ref.at[k], sml(m), (*chip, c)))
            sends[-1].start()

        def x_copy(rb, slot):
            return pltpu.make_async_copy(x_hbm.at[pl.ds(rb * rb_x, rb_x), :], xbuf.at[slot], x_sems.at[slot])

        x_copy(0, 0).start()
        for rb in range(t // rb_x):
            slot = rb % 2
            x_copy(rb, slot).wait()
            if rb + 1 < t // rb_x:
                x_copy(rb + 1, 1 - slot).start()

            def norm_slab(sl, carry, rb=rb, slot=slot):
                xf = xbuf[slot, pl.ds(pl.multiple_of(sl * SLAB, SLAB), SLAB), :]
                r = lax.rsqrt(jnp.mean(xf * xf, axis=-1, keepdims=True) + RMS_EPS)
                xn_ref[pl.ds(pl.multiple_of(rb * rb_x + sl * SLAB, SLAB), SLAB), :] = ((xf * r) * g_ref[...]).astype(MXU_DTYPE)
                return carry

            lax.fori_loop(0, rb_x // SLAB, norm_slab, 0)

        def out_copy(q, i, slot):
            return pltpu.make_async_copy(obuf.at[slot], proj_hbm.at[q, pl.ds(pl.multiple_of(i * rb_mm, rb_mm), rb_mm), :],
                                         o_sems.at[slot])

        def project(q, very_first):
            def row_block(i, carry):
                slot = i % 2

                def wait_buffer():
                    out_copy(q, i, slot).wait()

                if very_first:
                    pl.when(i >= 2)(wait_buffer)
                else:
                    wait_buffer()
                obuf[slot] = _mm(xn_ref[pl.ds(pl.multiple_of(i * rb_mm, rb_mm), rb_mm), :], w12_ref[q])
                out_copy(q, i, slot).start()
                return carry

            lax.fori_loop(0, n_mm, row_block, 0)

        for s in range(CHUNKS_PER_BLOCK):
            project(chunk_of((x_, y_), s), very_first=(s == 0))

        steps = []
        for s in range(CHUNKS_PER_BLOCK):
            for m, chip in enumerate((nbr_x, nbr_y)):
                onward = [(first, rows, dgn(j), chips[via]) for j, (cs, first, rows, via) in enumerate(DIAG)
                          if cs == s and via == 1 - m]
                steps.append((chunk_of(chip, s), [(0, half, ici(m, s))], onward))
        for s in range(CHUNKS_PER_BLOCK):
            steps.append((chunk_of(diagonal, s), [(first, rows, dgn(j)) for j, (cs, first, rows, _) in enumerate(DIAG) if cs == s], []))

        def project_when_whole(step):
            q, pieces, _ = step
            for first, rows, sem in pieces:
                remote(piece(q, 1 - c, first, rows), to_sib + sem, sib).wait_recv()
            project(q, very_first=False)

        passed = []
        for j, (q, pieces, onward) in enumerate(steps):
            for first, rows, sem in pieces:
                remote(piece(q, c, first, rows), sem, sib).wait_recv()
            for first, rows, sem, chip in onward:
                passed.append(remote(piece(q, c, first, rows), sem, (*chip, c)))
                passed[-1].start()
            for first, rows, sem in pieces:
                passed.append(remote(piece(q, c, first, rows), to_sib + sem, sib))
                passed[-1].start()
            if j > 0:
                project_when_whole(steps[j - 1])
        project_when_whole(steps[-1])

        for m, chip in enumerate(chips):
            remote(sm4_ref.at[2 * chip[0] + chip[1]], sml(m), sib).wait_recv()
        for cp in sends + passed:
            cp.wait_send()
        for slot in range(2):
            out_copy(0, slot, slot).wait()

    assert n_mm % 2 == 0 and n_mm >= 2
    vm = pl.BlockSpec(memory_space=pltpu.VMEM)
    hbm = pl.BlockSpec(memory_space=pl.ANY)
    n_sems = 23
    return pl.pallas_call(
        body,
        out_shape=(jax.ShapeDtypeStruct((N_CHUNKS, t, CHUNK), F32), jax.ShapeDtypeStruct((t, D_MODEL), MXU_DTYPE),
                   jax.ShapeDtypeStruct((N_CHUNKS, w_in.shape[0], CHUNK), MXU_DTYPE),
                   jax.ShapeDtypeStruct((N_CHIPS,) + small.shape, F32)),
        in_specs=[hbm, vm, vm, vm], out_specs=(hbm, vm, vm, vm),
        scratch_shapes=[pltpu.VMEM((2, rb_x, D_MODEL), F32), pltpu.VMEM((2, rb_mm, CHUNK), F32),
                        pltpu.SemaphoreType.DMA((2,)), pltpu.SemaphoreType.DMA((2,)),
                        pltpu.SemaphoreType.DMA((n_sems,)), pltpu.SemaphoreType.DMA((n_sems,))],
        compiler_params=_cp(), name="gather_in_projection",
    )(x, ln_g, w_in, small)


def _allreduce_behind(step, when, in_ref, acc_s, rbufs, out_ref, send_sems, recv_sems):
    x, y, c, _ = _mesh_pos()
    peers = [(x, y, 1 - c), (1 - x, y, c), (x, 1 - y, c)]

    def exchange(ph):
        return pltpu.make_async_remote_copy(src_ref=acc_s, dst_ref=rbufs[ph], send_sem=send_sems.at[ph],
                                            recv_sem=recv_sems.at[ph], device_id=peers[ph], device_id_type=MESH)

    @pl.when(step == when[0])
    def _():
        acc_s[...] = in_ref[...]
        exchange(0).start()

    for ph in (1, 2):
        @pl.when(step == when[ph])
        def _(ph=ph):
            exchange(ph - 1).wait()
            acc_s[...] = acc_s[...] + rbufs[ph - 1][...]
            exchange(ph).start()

    @pl.when(step == when[3])
    def _():
        exchange(2).wait()
        out_ref[...] = acc_s[...] + rbufs[2][...]


def _exchange_sibling_halves(g, name):
    n, rows, cols = g.shape
    half = rows // 2

    def body(g_ref, r_ref, send_sem, recv_sem):
        x, y, c, _ = _mesh_pos()
        cp = pltpu.make_async_remote_copy(src_ref=g_ref.at[:, pl.ds(pl.multiple_of(half * (1 - c), half), half), :],
                                          dst_ref=r_ref, send_sem=send_sem, recv_sem=recv_sem,
                                          device_id=(x, y, 1 - c), device_id_type=MESH)
        cp.start()
        cp.wait()

    hbm = pl.BlockSpec(memory_space=pl.ANY)
    return pl.pallas_call(
        body, out_shape=jax.ShapeDtypeStruct((n, half, cols), g.dtype), in_specs=[hbm], out_specs=hbm,
        scratch_shapes=[pltpu.SemaphoreType.DMA, pltpu.SemaphoreType.DMA],
        compiler_params=_cp(), name=name,
    )(g)


def _add_own_half(g, r, c_arr, name):
    n, rr, cc = r.shape

    def body(c_ref, g_ref, r_ref, o_ref, ob_ref):
        s = g_ref[...] + r_ref[...].astype(F32)
        o_ref[...] = s
        ob_ref[...] = s.astype(jnp.bfloat16)

    blk = pl.BlockSpec((1, rr, cc), lambda q, c_ref: (q, 0, 0))
    return pl.pallas_call(
        body, out_shape=(jax.ShapeDtypeStruct(r.shape, F32), jax.ShapeDtypeStruct(r.shape, jnp.bfloat16)),
        grid_spec=pltpu.PrefetchScalarGridSpec(
            num_scalar_prefetch=1, grid=(n,),
            in_specs=[pl.BlockSpec((1, rr, cc), lambda q, c_ref: (q, c_ref[0], 0)), blk],
            out_specs=(blk, blk)),
        compiler_params=_cp(ARB), name=name,
    )(c_arr, g, r)


def _chip_block_copies(s_ref, r_ref, n_sub, send_sems, recv_sems):
    x, y, c, chips = _mesh_pos()
    cps = []
    for m, chip in enumerate(chips):
        kk = 2 * chip[0] + chip[1]
        cps.append(pltpu.make_async_remote_copy(
            src_ref=s_ref.at[pl.ds(n_sub * kk, n_sub)], dst_ref=r_ref.at[m],
            send_sem=send_sems.at[m], recv_sem=recv_sems.at[m], device_id=(*chip, c), device_id_type=MESH))
    return cps


def _gather_w_out(step, n_steps, wo_ref, wob_s, wo4_ref, local_sem, send_sems, recv_sems):
    x, y, c, chips = _mesh_pos()
    sib = (x, y, 1 - c)
    half = wo_ref.shape[0] // 2

    def rows(core):
        return pl.ds(pl.multiple_of(half * core, half), half)

    def block_half(chip, core):
        return wo4_ref.at[2 * chip[0] + chip[1], rows(core), :]

    def remote(src, dst, sem, to):
        return pltpu.make_async_remote_copy(src_ref=src, dst_ref=dst, send_sem=send_sems.at[sem], recv_sem=recv_sems.at[sem],
                                            device_id=to, device_id_type=MESH)

    local = pltpu.make_async_copy(wob_s, wo4_ref.at[2 * x + y], local_sem)
    ici = [remote(wob_s.at[rows(c), :], block_half((x, y), c), m, (*chip, c)) for m, chip in enumerate(chips)]
    fwd = [remote(block_half(chip, c), block_half(chip, c), 3 + m, sib) for m, chip in enumerate(chips)]

    @pl.when(step == 0)
    def _():
        wob_s[...] = wo_ref[...].astype(MXU_DTYPE)
        local.start()
        for cp in ici:
            cp.start()

    @pl.when(step == n_steps // 2)
    def _():
        for m, chip in enumerate(chips):
            remote(block_half(chip, c), block_half(chip, c), m, sib).wait_recv()
            fwd[m].start()

    @pl.when(step == n_steps - 1)
    def _():
        for m, chip in enumerate(chips):
            remote(block_half(chip, 1 - c), block_half(chip, 1 - c), 3 + m, sib).wait_recv()
        for cp in ici + fwd:
            cp.wait_send()
        local.wait()


def _chip_blocks_shape(s, n_sub):
    return jax.ShapeDtypeStruct((3, n_sub) + s.shape[1:], s.dtype)


def _sum_chip_blocks(s, r, kc_arr, n_sub, name):
    _, rr, cc = s.shape

    def body(kc_ref, s_ref, r_ref, o_ref):
        o_ref[...] = ((s_ref[...] + r_ref[0].astype(F32)) + r_ref[1].astype(F32)) + r_ref[2].astype(F32)

    return pl.pallas_call(
        body, out_shape=jax.ShapeDtypeStruct((n_sub, 2 * rr, cc), F32),
        grid_spec=pltpu.PrefetchScalarGridSpec(
            num_scalar_prefetch=1, grid=(n_sub,),
            in_specs=[pl.BlockSpec((1, rr, cc), lambda q, kc: (n_sub * kc[0] + q, 0, 0)),
                      pl.BlockSpec((3, 1, rr, cc), lambda q, kc: (0, q, 0, 0))],
            out_specs=pl.BlockSpec((1, rr, cc), lambda q, kc: (q, kc[1], 0))),
        compiler_params=_cp(ARB), name=name,
    )(kc_arr, s, r)


def _swap_halves_and_sum(f_in, f_out, v):
    hi, ho = f_in.shape[1] // 2, f_out.shape[1] // 2
    n_dev = 8

    def body(fi_in, fo_in, v_ref, fi_ref, fo_ref, tot_ref, slots, send_sems, recv_sems):
        del fi_in, fo_in
        x, y, c, _ = _mesh_pos()
        sib = (x, y, 1 - c)
        me = 4 * x + 2 * y + c
        slots[me] = v_ref[...]
        si = fi_ref.at[:, pl.ds(pl.multiple_of(hi * c, hi), hi), :]
        so = fo_ref.at[:, pl.ds(pl.multiple_of(ho * c, ho), ho), :]

        def remote(ref, sem, to):
            return pltpu.make_async_remote_copy(src_ref=ref, dst_ref=ref, send_sem=send_sems.at[sem],
                                                recv_sem=recv_sems.at[sem], device_id=to, device_id_type=MESH)

        cps = [remote(si, n_dev - 1, sib), remote(so, n_dev, sib)]
        for d in range(1, n_dev):
            peer = (1 - x if d & 4 else x, 1 - y if d & 2 else y, 1 - c if d & 1 else c)
            cps.append(remote(slots.at[me], d - 1, peer))
        for cp in cps:
            cp.start()
        for cp in cps:
            cp.wait()
        total = slots[0]
        for dev in range(1, n_dev):
            total = total + slots[dev]
        tot_ref[...] = total

    hbm = pl.BlockSpec(memory_space=pl.ANY)
    vm = pl.BlockSpec(memory_space=pltpu.VMEM)
    return pl.pallas_call(
        body,
        out_shape=(jax.ShapeDtypeStruct(f_in.shape, F32), jax.ShapeDtypeStruct(f_out.shape, F32),
                   jax.ShapeDtypeStruct(v.shape, F32)),
        in_specs=[hbm, hbm, vm], out_specs=(hbm, hbm, vm), input_output_aliases={0: 0, 1: 1},
        scratch_shapes=[pltpu.VMEM((n_dev,) + v.shape, F32), pltpu.SemaphoreType.DMA((n_dev + 1,)),
                        pltpu.SemaphoreType.DMA((n_dev + 1,))],
        compiler_params=_cp(), name="swap_halves_and_sum",
    )(f_in, f_out, v)


def _out_projection_loss(yc, yl, x, target, wo, final_g):
    t = x.shape[0]
    tm = 512

    def body(yc_ref, yl_ref, x_ref, t_ref, wo_ref, fg_ref, do_ref, dob_ref, dy_ref, st_ref, y_wo):
        @pl.when(pl.program_id(0) == 0)
        def _():
            st_ref[...] = jnp.zeros_like(st_ref)

        y_wo[...] = _mm(yc_ref[...], wo_ref[0:D_PART, :]) + _mm(yl_ref[...], wo_ref[D_PART:2 * D_PART, :])

        def norm_loss_slab(s, carry):
            g_sum, loss_sum = carry
            rows = pl.ds(pl.multiple_of(s * SLAB, SLAB), SLAB)
            o = x_ref[rows, :] + y_wo[rows, :]
            r2 = lax.rsqrt(jnp.mean(o * o, axis=-1, keepdims=True) + RMS_EPS)
            ohat = o * r2
            fg = fg_ref[...]
            diff = ohat * fg - t_ref[rows, :]
            dout = diff * (1.0 / D_MODEL)
            gp = dout * fg
            do = r2 * (gp - ohat * jnp.mean(gp * ohat, axis=-1, keepdims=True))
            do_ref[rows, :] = do
            dob_ref[rows, :] = do.astype(MXU_DTYPE)
            loss = 0.5 * jnp.sum(jnp.sum(diff * diff, axis=-1, keepdims=True) * (1.0 / D_MODEL), axis=0, keepdims=True)
            return g_sum + jnp.sum(dout * ohat, axis=0, keepdims=True), loss_sum + loss

        g_sum, loss_sum = lax.fori_loop(0, tm // SLAB, norm_loss_slab,
                                        (jnp.zeros((1, D_MODEL), F32), jnp.zeros((1, 1), F32)))
        st_ref[0:1, :] += g_sum
        st_ref[1:2, :] += jnp.broadcast_to(loss_sum, (1, D_MODEL))
        dy_ref[...] = _mm_nt(dob_ref[...], wo_ref[...])

    row = lambda i: (i, 0)
    fix = lambda i: (0, 0)
    return pl.pallas_call(
        body, grid=(t // tm,),
        in_specs=[pl.BlockSpec((tm, D_PART), row), pl.BlockSpec((tm, D_PART), row),
                  pl.BlockSpec((tm, D_MODEL), row), pl.BlockSpec((tm, D_MODEL), row),
                  pl.BlockSpec((2 * D_PART, D_MODEL), fix), pl.BlockSpec((1, D_MODEL), fix)],
        out_specs=(pl.BlockSpec((tm, D_MODEL), row), pl.BlockSpec((tm, D_MODEL), row),
                   pl.BlockSpec((tm, 2 * D_PART), row), pl.BlockSpec((SUBLANES, D_MODEL), fix)),
        out_shape=(jax.ShapeDtypeStruct((t, D_MODEL), F32), jax.ShapeDtypeStruct((t, D_MODEL), MXU_DTYPE),
                   jax.ShapeDtypeStruct((t, 2 * D_PART), F32), jax.ShapeDtypeStruct((SUBLANES, D_MODEL), F32)),
        scratch_shapes=[pltpu.VMEM((tm, D_MODEL), F32)],
        compiler_params=_cp(ARB), name="out_projection_loss",
    )(yc, yl, x, target, wo, final_g)


def _input_grad(dproj, w12, x, do, ln_g, sb_in):
    t = x.shape[0]
    tm = 1024

    def body(dp_ref, w_ref, x_ref, do_ref, g_ref, s_ref, gx_ref, st_ref, r_ref, acc, send_sems, recv_sems):
        i, p = pl.program_id(0), pl.program_id(1)

        @pl.when((i == 0) & (p == 0))
        def _():
            st_ref[...] = jnp.zeros_like(st_ref)
            for cp in _chip_block_copies(s_ref, r_ref, CHUNKS_PER_BLOCK, send_sems, recv_sems):
                cp.start()

        @pl.when((i == t // tm - 1) & (p == N_PARTS - 1))
        def _():
            for cp in _chip_block_copies(s_ref, r_ref, CHUNKS_PER_BLOCK, send_sems, recv_sems):
                cp.wait()

        @pl.when(p == 0)
        def _():
            acc[...] = jnp.zeros_like(acc)

        acc[...] += _mm_nt(dp_ref[0], jnp.concatenate([w_ref[0], w_ref[1]], axis=1))

        @pl.when(p == N_PARTS - 1)
        def _():
            def norm_bwd_slab(s, g_sum):
                rows = pl.ds(pl.multiple_of(s * SLAB, SLAB), SLAB)
                xf = x_ref[rows, :]
                r = lax.rsqrt(jnp.mean(xf * xf, axis=-1, keepdims=True) + RMS_EPS)
                xhat = xf * r
                dxn = acc[rows, :]
                dxh = dxn * g_ref[...]
                gx_ref[rows, :] = do_ref[rows, :] + r * (dxh - xhat * jnp.mean(dxh * xhat, axis=-1, keepdims=True))
                return g_sum + jnp.sum(dxn * xhat, axis=0, keepdims=True)

            st_ref[0:1, :] += lax.fori_loop(0, tm // SLAB, norm_bwd_slab, jnp.zeros((1, D_MODEL), F32))

    row = lambda i, p: (i, 0)
    fix = lambda i, p: (0, 0)
    return pl.pallas_call(
        body, grid=(t // tm, N_PARTS),
        in_specs=[
            pl.BlockSpec((1, tm, D_PART), lambda i, p: (p, i, 0)),
            pl.BlockSpec((2, D_MODEL, CHUNK), lambda i, p: (p, 0, 0)),
            pl.BlockSpec((tm, D_MODEL), row), pl.BlockSpec((tm, D_MODEL), row), pl.BlockSpec((1, D_MODEL), fix),
            pl.BlockSpec(memory_space=pl.ANY)],
        out_specs=(pl.BlockSpec((tm, D_MODEL), row), pl.BlockSpec((SUBLANES, D_MODEL), fix),
                   pl.BlockSpec(memory_space=pl.ANY)),
        out_shape=(jax.ShapeDtypeStruct((t, D_MODEL), F32), jax.ShapeDtypeStruct((SUBLANES, D_MODEL), F32),
                   _chip_blocks_shape(sb_in, CHUNKS_PER_BLOCK)),
        scratch_shapes=[pltpu.VMEM((tm, D_MODEL), F32), pltpu.SemaphoreType.DMA((3,)), pltpu.SemaphoreType.DMA((3,))],
        compiler_params=_cp(ARB, ARB), name="input_grad",
    )(dproj, w12, x, do, ln_g, sb_in)


def _w_in_grad(xn, dproj, small):
    t = xn.shape[0]
    small_shape = pltpu.VMEM(small.shape, F32)

    def body(xn_ref, dp_ref, sm_ref, o_ref, ob_ref, red_ref, acc_s, r0, r1, r2, send_sems, recv_sems):
        _allreduce_behind(pl.program_id(0), (0, 1, 3, N_PARTS - 1), sm_ref, acc_s, (r0, r1, r2), red_ref, send_sems, recv_sems)
        g = _mm_tn(xn_ref[...], dp_ref[0])
        for s in range(2):
            o_ref[s] = g[:, CHUNK * s:CHUNK * (s + 1)]
            ob_ref[s] = g[:, CHUNK * s:CHUNK * (s + 1)].astype(jnp.bfloat16)

    whole = pl.BlockSpec(small.shape, lambda p: (0, 0))
    pair = pl.BlockSpec((2, D_MODEL, CHUNK), lambda p: (p, 0, 0))
    return pl.pallas_call(
        body, grid=(N_PARTS,),
        in_specs=[pl.BlockSpec((t, D_MODEL), lambda p: (0, 0)),
                  pl.BlockSpec((1, t, D_PART), lambda p: (p, 0, 0)), whole],
        out_specs=(pair, pair, whole),
        out_shape=(jax.ShapeDtypeStruct((N_CHUNKS, D_MODEL, CHUNK), F32),
                   jax.ShapeDtypeStruct((N_CHUNKS, D_MODEL, CHUNK), jnp.bfloat16), jax.ShapeDtypeStruct(small.shape, F32)),
        scratch_shapes=[small_shape] * 4 + [pltpu.SemaphoreType.DMA((3,)), pltpu.SemaphoreType.DMA((3,))],
        compiler_params=_cp(ARB), name="w_in_grad",
    )(xn, dproj, small)


def _w_out_grad(yc, yl, dob):
    t = yc.shape[0]
    tk = 2048

    def body(yc_ref, yl_ref, do_ref, o_ref, ob_ref):
        j, kk = pl.program_id(0), pl.program_id(1)

        def accumulate(y_ref):
            @pl.when(kk == 0)
            def _():
                o_ref[...] = jnp.zeros_like(o_ref)

            o_ref[...] += _mm_tn(y_ref[...], do_ref[...])

            @pl.when(kk == t // tk - 1)
            def _():
                ob_ref[...] = o_ref[...].astype(jnp.bfloat16)

        pl.when(j == 0)(functools.partial(accumulate, yc_ref))
        pl.when(j == 1)(functools.partial(accumulate, yl_ref))

    def rows_of(half):
        return lambda j, kk: (jnp.where(j == half, kk, 0), 0)

    half = pl.BlockSpec((D_PART, D_MODEL), lambda j, kk: (j, 0))
    out, out_b = pl.pallas_call(
        body, grid=(2, t // tk),
        in_specs=[pl.BlockSpec((tk, D_PART), rows_of(0)), pl.BlockSpec((tk, D_PART), rows_of(1)),
                  pl.BlockSpec((tk, D_MODEL), lambda j, kk: (kk, 0))],
        out_specs=(half, half),
        out_shape=(jax.ShapeDtypeStruct((2 * D_PART, D_MODEL), F32), jax.ShapeDtypeStruct((2 * D_PART, D_MODEL), jnp.bfloat16)),
        compiler_params=_cp(ARB, ARB), name="w_out_grad",
    )(yc, yl, dob)
    blocks = (N_CHIPS, 2 * D_PART // N_CHIPS, D_MODEL)
    return out.reshape(blocks), out_b.reshape(blocks)


def _for_groups(n, fn, init, unroll=UNROLL, stores=(), descending=False):
    assert unroll % 2 == 0 and n % unroll == 0

    def trip(j, carry):
        held = None
        for uu in range(unroll):
            idx = j * unroll + uu
            carry, values = fn(idx, carry)
            if uu % 2 == 0:
                held = values
                continue
            low_group = n - 1 - idx if descending else idx - 1
            rows = pl.ds(pl.multiple_of(low_group * SUBLANES, 2 * SUBLANES), 2 * SUBLANES)
            pairs = zip(values, held) if descending else zip(held, values)
            for store, (lo, hi) in zip(stores, pairs, strict=True):
                store(rows, jnp.concatenate([lo, hi], axis=0).astype(MXU_DTYPE))
        return carry

    return lax.fori_loop(0, n // unroll, trip, init)


def _rows_of(ref, *lead, cols=slice(None)):
    def store(rows, value):
        ref[(*lead, rows, cols)] = value

    return store


def _pvb(pv_ref, r):
    return jnp.broadcast_to(pv_ref[r:r + 1, :], (SUBLANES, pv_ref.shape[1]))


def _conv3(pv_ref, u, u1, u2):
    return (_pvb(pv_ref, PV_CONV_W) * u2 + _pvb(pv_ref, PV_CONV_W + 1) * u1) + _pvb(pv_ref, PV_CONV_W + 2) * u


def _conv4(pv_ref, v, v1, v2, v3):
    return ((((_pvb(pv_ref, PV_LRU_W) * v3 + _pvb(pv_ref, PV_LRU_W + 1) * v2) + _pvb(pv_ref, PV_LRU_W + 2) * v1)
             + _pvb(pv_ref, PV_LRU_W + 3) * v) + _pvb(pv_ref, PV_LRU_B))


def _mixer_forward(proj, pvec, wai, w_out):
    t = proj.shape[1]
    tb = 1024
    ng = tb // SUBLANES
    nt = t // tb

    def body(bg_ref, cg_ref, xc_ref, gc_ref, xl_ref, gl_ref, pv_ref, wai_ref, wo_ref,
             yc_ref, yl_ref, h_ref, u_s, r_ref, ig_ref, wo4_ref,
             ucp_s, xlp_s, ls_s, hbuf_s, ub_s, gate_s, wob_s, local_sem, send_sems, recv_sems):
        _gather_w_out(pl.program_id(0) * nt + pl.program_id(1), NS * nt, wo_ref, wob_s, wo4_ref, local_sem, send_sems, recv_sems)

        @pl.when(pl.program_id(1) == 0)
        def _():
            ucp_s[...] = jnp.zeros_like(ucp_s)
            xlp_s[...] = jnp.zeros_like(xlp_s)
            hbuf_s[...] = jnp.zeros_like(hbuf_s)

        row = lax.broadcasted_iota(jnp.int32, (SUBLANES, LW), 0)
        ls_s[...] = _log_sigmoid(_pvb(pv_ref, PV_LAM))

        def conv_group(g, carry):
            ucp, xlp = carry
            sl = pl.ds(pl.multiple_of(g * SUBLANES, SUBLANES), SUBLANES)
            uc = cg_ref[sl, :] * xc_ref[sl, :]
            v = _conv3(pv_ref, uc, _shift_down(uc, ucp, 1, row), _shift_down(uc, ucp, 2, row))
            yc = bg_ref[sl, :] * v
            rr = lax.rsqrt(_head_mean(yc * yc, CONV_HEAD) + RMS_EPS)
            gc = gc_ref[sl, :]
            zc = ((yc * rr) * _pvb(pv_ref, PV_CG)) * (gc * _sigmoid(gc))
            xl = xl_ref[sl, :]
            u = _conv4(pv_ref, xl, _shift_down(xl, xlp, 1, row), _shift_down(xl, xlp, 2, row), _shift_down(xl, xlp, 3, row))
            u_s[sl, :] = u
            return (uc, xl), (zc, u)

        ucp, xlp = _for_groups(ng, conv_group, (ucp_s[...], xlp_s[...]), unroll=2 * UNROLL,
                               stores=(_rows_of(yc_ref), _rows_of(ub_s)))
        ucp_s[...] = ucp
        xlp_s[...] = xlp

        gate_s[...] = _mm(ub_s[...], wai_ref[0])

        def lru_group(g, h_before):
            sl = pl.ds(pl.multiple_of(g * SUBLANES, SUBLANES), SUBLANES)
            u = u_s[sl, :]
            r = _sigmoid(gate_s[sl, 0:LW] + _pvb(pv_ref, PV_BA))
            ig = _sigmoid(gate_s[sl, LW:2 * LW] + _pvb(pv_ref, PV_BI))
            r_ref[sl, :] = r
            ig_ref[sl, :] = ig
            a, _, mult, _ = _decay(r, ls_s[...])
            A, B = _scan8_fwd(a, mult * (ig * u), row)
            h = B + A * jnp.broadcast_to(h_before[SUBLANES - 1:SUBLANES, :], (SUBLANES, LW))
            h_ref[sl, :] = h
            rr = lax.rsqrt(_head_mean(h * h, LRU_HEAD) + RMS_EPS)
            gl = gl_ref[sl, :]
            return h, (((h * rr) * _pvb(pv_ref, PV_LG)) * (gl * _sigmoid(gl)),)

        hbuf_s[...] = _for_groups(ng, lru_group, hbuf_s[...], unroll=2 * UNROLL, stores=(_rows_of(yl_ref),))

    def part(p):
        return pl.BlockSpec((None, tb, LW), lambda c, i: (2 * p + c // STRIPS_PER_CHUNK, i, c % STRIPS_PER_CHUNK))

    strip = pl.BlockSpec((tb, LW), lambda c, i: (i, c))
    return pl.pallas_call(
        body, grid=(NS, nt),
        in_specs=[part(p) for p in range(N_PARTS)] + [
            pl.BlockSpec((PV_ROWS, LW), lambda c, i: (0, c)),
            pl.BlockSpec((1, LW, 2 * LW), lambda c, i: (c, 0, 0)),
            pl.BlockSpec(w_out.shape, lambda c, i: (0, 0))],
        out_specs=(strip,) * 6 + (pl.BlockSpec(memory_space=pl.ANY),),
        out_shape=(jax.ShapeDtypeStruct((t, D_PART), MXU_DTYPE),) * 2 + (jax.ShapeDtypeStruct((t, D_PART), F32),) * 4 + (
            jax.ShapeDtypeStruct((N_CHIPS,) + w_out.shape, MXU_DTYPE),),
        scratch_shapes=[pltpu.VMEM((SUBLANES, LW), F32), pltpu.VMEM((SUBLANES, LW), F32), pltpu.VMEM((SUBLANES, LW), F32),
                        pltpu.VMEM((SUBLANES, LW), F32), pltpu.VMEM((tb, LW), MXU_DTYPE),
                        pltpu.VMEM((tb, 2 * LW), F32), pltpu.VMEM(w_out.shape, MXU_DTYPE),
                        pltpu.SemaphoreType.DMA, pltpu.SemaphoreType.DMA((6,)), pltpu.SemaphoreType.DMA((6,))],
        compiler_params=_cp(ARB, ARB), name="mixer_forward",
    )(proj, proj, proj, proj, proj, proj, pvec, wai, w_out)


def _mixer_backward(proj, h, u, r, ig, dy, pvec, wai, sb_out):
    t = proj.shape[1]
    tb = 1024
    ng = tb // SUBLANES
    nt = t // tb
    gpb = tb // SUBLANES

    def body(bg_ref, cg_ref, xc_ref, gc_ref, xl_ref, gl_ref, h_ref, u_ref, r_ref, ig_ref, dyc_ref, dyl_ref,
             cgh_ref, xch_ref, xlh_ref, hh_ref, pv_ref, wai_ref, so_ref,
             dp_ref, gw_ref, sv_ref, ro_ref,
             ls_s, ub_s, uce_s, xle_s, he_s, dgb_s, du_s, gbuf_s,
             acc_s, an_s, dvn_s, dun_s, send_sems, recv_sems):
        i = pl.program_id(1)
        first_block = i == nt - 1

        @pl.when((pl.program_id(0) == 0) & (i == 0))
        def _():
            for cp in _chip_block_copies(so_ref, ro_ref, 1, send_sems, recv_sems):
                cp.start()

        @pl.when((pl.program_id(0) == NS - 1) & (i == nt - 1))
        def _():
            for cp in _chip_block_copies(so_ref, ro_ref, 1, send_sems, recv_sems):
                cp.wait()

        @pl.when(i == 0)
        def _():
            acc_s[...] = jnp.zeros_like(acc_s)
            gw_ref[...] = jnp.zeros_like(gw_ref)
            an_s[...] = jnp.zeros_like(an_s)
            dvn_s[...] = jnp.zeros_like(dvn_s)
            dun_s[...] = jnp.zeros_like(dun_s)
            gbuf_s[...] = jnp.zeros_like(gbuf_s)

        row = lax.broadcasted_iota(jnp.int32, (SUBLANES, LW), 0)
        ls_s[...] = _log_sigmoid(_pvb(pv_ref, PV_LAM))
        keep = jnp.where(first_block, 0.0, 1.0)
        uce_s[0:SUBLANES, :] = (cgh_ref[...] * xch_ref[...]) * keep
        xle_s[0:SUBLANES, :] = xlh_ref[...] * keep
        he_s[0:SUBLANES, :] = hh_ref[...] * keep
        xle_s[SUBLANES:SUBLANES + tb, :] = xl_ref[...]
        he_s[SUBLANES:SUBLANES + tb, :] = h_ref[...]

        uce_s[SUBLANES:SUBLANES + tb, :] = cg_ref[...] * xc_ref[...]

        def acc_add(k, v):
            acc_s[k] += v

        def main_group(gi, carry):
            a_next, dv_next, g_next = carry
            g = ng - 1 - gi
            r0 = pl.multiple_of(g * SUBLANES, SUBLANES)
            sl = pl.ds(r0, SUBLANES)
            sl_e = pl.ds(r0 + SUBLANES, SUBLANES)
            lsb = ls_s[...]
            u = u_ref[sl, :]
            r = r_ref[sl, :]
            ig = ig_ref[sl, :]
            a, e2, mult, inv_mult = _decay(r, lsb)
            gl = gl_ref[sl, :]
            sg = _sigmoid(gl)
            s_l = gl * sg
            h8 = he_s[sl_e, :]
            hprev = _shift_down(h8, he_s[sl, :], 1, row)
            rr = lax.rsqrt(_head_mean(h8 * h8, LRU_HEAD) + RMS_EPS)
            n = h8 * rr
            dz = dyl_ref[sl, :]
            lg = _pvb(pv_ref, PV_LG)
            acc_add(PV_LG, (dz * n) * s_l)
            p5 = ((dz * n) * lg) * (sg * (1.0 + gl * (1.0 - sg)))
            dn = (dz * lg) * s_l
            dh = rr * (dn - n * _head_mean(dn * n, LRU_HEAD))
            A, B = _scan8_rev(_shift_up(a, a_next, 1, row), dh, row)
            gg = B + A * jnp.broadcast_to(g_next[0:1, :], (SUBLANES, LW))
            da = gg * hprev
            iu = ig * u
            diu = gg * mult
            dla = da * a - (gg * iu) * (e2 * inv_mult)
            acc_add(PV_LAM, dla * (RG_LRU_C * r))
            dra = (dla * (RG_LRU_C * lsb)) * (r * (1.0 - r))
            dia = (diu * u) * (ig * (1.0 - ig))
            acc_add(PV_BA, dra)
            acc_add(PV_BI, dia)
            du_s[sl, :] = diu * ig
            bg = bg_ref[sl, :]
            gc = gc_ref[sl, :]
            uc = uce_s[sl_e, :]
            ucp = uce_s[sl, :]
            uc1 = _shift_down(uc, ucp, 1, row)
            uc2 = _shift_down(uc, ucp, 2, row)
            v = _conv3(pv_ref, uc, uc1, uc2)
            yc = bg * v
            rrc = lax.rsqrt(_head_mean(yc * yc, CONV_HEAD) + RMS_EPS)
            nc = yc * rrc
            sgc = _sigmoid(gc)
            s_c = gc * sgc
            dzc = dyc_ref[sl, :]
            cgain = _pvb(pv_ref, PV_CG)
            acc_add(PV_CG, (dzc * nc) * s_c)
            p3 = ((dzc * nc) * cgain) * (sgc * (1.0 + gc * (1.0 - sgc)))
            dnc = (dzc * cgain) * s_c
            dyc = rrc * (dnc - nc * _head_mean(dnc * nc, CONV_HEAD))
            dv = dyc * bg
            duc = (_pvb(pv_ref, PV_CONV_W + 2) * dv + _pvb(pv_ref, PV_CONV_W + 1) * _shift_up(dv, dv_next, 1, row)
                   + _pvb(pv_ref, PV_CONV_W) * _shift_up(dv, dv_next, 2, row))
            acc_add(PV_CONV_W + 2, dv * uc)
            acc_add(PV_CONV_W + 1, dv * uc1)
            acc_add(PV_CONV_W, dv * uc2)
            return (a, dv, gg), (dyc * v, duc * xc_ref[sl, :], duc * cg_ref[sl, :], p3, p5, dra, dia, u)

        a_next, dv_next, g_next = _for_groups(
            ng, main_group, (an_s[...], dvn_s[...], gbuf_s[...]), descending=True,
            stores=(_rows_of(dp_ref, 0), _rows_of(dp_ref, 1), _rows_of(dp_ref, 2), _rows_of(dp_ref, 3), _rows_of(dp_ref, 5),
                    _rows_of(dgb_s, cols=slice(0, LW)), _rows_of(dgb_s, cols=slice(LW, 2 * LW)), _rows_of(ub_s)))
        an_s[...] = a_next
        dvn_s[...] = dv_next
        gbuf_s[...] = g_next

        dgb = dgb_s[...]
        du_s[...] += _mm_nt(dgb, wai_ref[0])
        gw_ref[0] += _mm_tn(ub_s[...], dgb)

        def lru_conv_group(gi, du_next):
            g = ng - 1 - gi
            r0 = pl.multiple_of(g * SUBLANES, SUBLANES)
            sl = pl.ds(r0, SUBLANES)
            du = du_s[sl, :]
            xl = xle_s[pl.ds(r0 + SUBLANES, SUBLANES), :]
            xlp = xle_s[sl, :]
            acc_add(PV_LRU_B, du)
            acc_add(PV_LRU_W + 3, du * xl)
            acc_add(PV_LRU_W + 2, du * _shift_down(xl, xlp, 1, row))
            acc_add(PV_LRU_W + 1, du * _shift_down(xl, xlp, 2, row))
            acc_add(PV_LRU_W, du * _shift_down(xl, xlp, 3, row))
            dxl = (((_pvb(pv_ref, PV_LRU_W + 3) * du + _pvb(pv_ref, PV_LRU_W + 2) * _shift_up(du, du_next, 1, row))
                    + _pvb(pv_ref, PV_LRU_W + 1) * _shift_up(du, du_next, 2, row))
                   + _pvb(pv_ref, PV_LRU_W) * _shift_up(du, du_next, 3, row))
            return du, (dxl,)

        dun_s[...] = _for_groups(ng, lru_conv_group, dun_s[...], descending=True, stores=(_rows_of(dp_ref, 4),))

        @pl.when(first_block)
        def _():
            sv_ref[...] = jnp.zeros_like(sv_ref)
            for k in range(N_ACC):
                tot = jnp.sum(acc_s[k], axis=0, keepdims=True)
                if k == PV_LAM:
                    tot = tot / (1.0 + jnp.exp(pv_ref[PV_LAM:PV_LAM + 1, :]))
                sv_ref[k:k + 1, :] = tot

    def part(p):
        return pl.BlockSpec((None, tb, LW), lambda c, i: (2 * p + c // STRIPS_PER_CHUNK, nt - 1 - i, c % STRIPS_PER_CHUNK))

    def halo(p):
        return pl.BlockSpec((None, SUBLANES, LW), lambda c, i: (2 * p + c // STRIPS_PER_CHUNK,
                                                                jnp.maximum((nt - 1 - i) * gpb - 1, 0), c % STRIPS_PER_CHUNK))

    strip = pl.BlockSpec((tb, LW), lambda c, i: (nt - 1 - i, c))
    big = pltpu.VMEM((tb, LW), F32)
    big_e = pltpu.VMEM((tb + SUBLANES, LW), F32)
    small = pltpu.VMEM((SUBLANES, LW), F32)
    outs = pl.pallas_call(
        body, grid=(NS, nt),
        in_specs=[part(p) for p in range(N_PARTS)] + [
            strip, strip, strip, strip, strip, pl.BlockSpec((tb, LW), lambda c, i: (nt - 1 - i, NS + c)),
            halo(1), halo(2), halo(4),
            pl.BlockSpec((SUBLANES, LW), lambda c, i: (jnp.maximum((nt - 1 - i) * gpb - 1, 0), c)),
            pl.BlockSpec((PV_ROWS, LW), lambda c, i: (0, c)),
            pl.BlockSpec((1, LW, 2 * LW), lambda c, i: (c, 0, 0)),
            pl.BlockSpec(memory_space=pl.ANY)],
        out_specs=(pl.BlockSpec((N_PARTS, tb, LW), lambda c, i: (0, nt - 1 - i, c)),
                   pl.BlockSpec((1, LW, 2 * LW), lambda c, i: (c, 0, 0)),
                   pl.BlockSpec((PV_ROWS, LW), lambda c, i: (0, c)),
                   pl.BlockSpec(memory_space=pl.ANY)),
        out_shape=(jax.ShapeDtypeStruct((N_PARTS, t, D_PART), MXU_DTYPE),
                   jax.ShapeDtypeStruct((NS, LW, 2 * LW), F32), jax.ShapeDtypeStruct((PV_ROWS, D_PART), F32),
                   _chip_blocks_shape(sb_out, 1)),
        scratch_shapes=[small, pltpu.VMEM((tb, LW), MXU_DTYPE), big_e, big_e, big_e,
                        pltpu.VMEM((tb, 2 * LW), MXU_DTYPE), big, small,
                        pltpu.VMEM((N_ACC, SUBLANES, LW), F32), small, small, small,
                        pltpu.SemaphoreType.DMA((3,)), pltpu.SemaphoreType.DMA((3,))],
        compiler_params=_cp(ARB, ARB), name="mixer_backward",
    )(proj, proj, proj, proj, proj, proj, h, u, r, ig, dy, dy, proj, proj, proj, h, pvec, wai, sb_out)
    return outs


def _adamw(w, g, m, v):
    m = ADAM_B1 * m + (1.0 - ADAM_B1) * g
    v = ADAM_B2 * v + (1.0 - ADAM_B2) * (g * g)
    m_hat = m / (1.0 - ADAM_B1 ** ADAM_STEP)
    v_hat = v / (1.0 - ADAM_B2 ** ADAM_STEP)
    delta = -ADAM_LR * (m_hat / (jnp.sqrt(v_hat) + ADAM_EPS) + ADAM_WD * w)
    return delta, m, v


def _adam_w_in(w, m, v, g3):
    rows, cols = w.shape
    tr = 128

    def body(w_ref, m_ref, v_ref, g_ref, go_ref, d_ref, mo_ref, vo_ref):
        for s in range(CHUNKS_PER_BLOCK):
            cs = slice(CHUNK * s, CHUNK * (s + 1))
            g = g_ref[s]
            d, mn, vn = _adamw(w_ref[:, cs], g, m_ref[:, cs], v_ref[:, cs])
            go_ref[:, cs] = g
            d_ref[:, cs] = d
            mo_ref[:, cs] = mn
            vo_ref[:, cs] = vn

    blk = pl.BlockSpec((tr, cols), lambda i: (i, 0))
    return pl.pallas_call(
        body, grid=(rows // tr,),
        in_specs=[blk, blk, blk, pl.BlockSpec((CHUNKS_PER_BLOCK, tr, CHUNK), lambda i: (0, i, 0))],
        out_specs=(blk,) * 4, out_shape=(jax.ShapeDtypeStruct(w.shape, F32),) * 4,
        compiler_params=_cp(ARB), name="adam_w_in",
    )(w, m, v, g3)


def _adam_w_out(w, m, v, g):
    rows, cols = w.shape
    tr = 128

    def body(w_ref, m_ref, v_ref, g_ref, d_ref, mo_ref, vo_ref):
        d_ref[...], mo_ref[...], vo_ref[...] = _adamw(w_ref[...], g_ref[...], m_ref[...], v_ref[...])

    blk = pl.BlockSpec((tr, cols), lambda i: (i, 0))
    return pl.pallas_call(
        body, grid=(rows // tr,), in_specs=[blk] * 4, out_specs=(blk,) * 3,
        out_shape=(jax.ShapeDtypeStruct(w.shape, F32),) * 3,
        compiler_params=_cp(ARB), name="adam_w_out",
    )(w, m, v, g)


def _adam_small(ws, ms, vs, gs):
    n = len(ws)

    def body(*refs):
        w_r, m_r, v_r, g_r = refs[0:n], refs[n:2 * n], refs[2 * n:3 * n], refs[3 * n:4 * n]
        d_o, m_o, v_o = refs[4 * n:5 * n], refs[5 * n:6 * n], refs[6 * n:7 * n]
        for j in range(n):
            d_o[j][...], m_o[j][...], v_o[j][...] = _adamw(w_r[j][...], g_r[j][...], m_r[j][...], v_r[j][...])

    vm = pl.BlockSpec(memory_space=pltpu.VMEM)
    shapes = tuple(jax.ShapeDtypeStruct(w.shape, F32) for w in ws)
    outs = pl.pallas_call(
        body, in_specs=[vm] * (4 * n), out_specs=(vm,) * (3 * n), out_shape=shapes * 3,
        compiler_params=_cp(), name="adam_small",
    )(*ws, *ms, *vs, *gs)
    return outs[0:n], outs[n:2 * n], outs[2 * n:3 * n]


def _block_diag_strips(w):
    w4 = w.reshape(NS, HEADS_PER_STRIP, LRU_HEAD, LRU_HEAD)
    rows = [jnp.pad(w4[:, hh], ((0, 0), (0, 0), (LRU_HEAD * hh, LW - LRU_HEAD * (hh + 1)))) for hh in range(HEADS_PER_STRIP)]
    return jnp.concatenate(rows, axis=1)


def _strip_diag_blocks(g):
    g5 = g.reshape(NS, HEADS_PER_STRIP, LRU_HEAD, HEADS_PER_STRIP, LRU_HEAD)
    return jnp.stack([g5[:, hh, :, hh, :] for hh in range(HEADS_PER_STRIP)], axis=1).reshape(NS * HEADS_PER_STRIP, LRU_HEAD, LRU_HEAD)


def kernel(x, ln_g, w_in, conv_w, lru_conv_w, lru_conv_b, w_a, b_a, w_i, b_i, lam, conv_out_g, lru_out_g, w_out, final_g, loss_target, m_ln_g, m_w_in, m_conv_w, m_lru_conv_w, m_lru_conv_b, m_w_a, m_b_a, m_w_i, m_b_i, m_lam, m_conv_out_g, m_lru_out_g, m_w_out, m_final_g, v_ln_g, v_w_in, v_conv_w, v_lru_conv_w, v_lru_conv_b, v_w_a, v_b_a, v_w_i, v_b_i, v_lam, v_conv_out_g, v_lru_out_g, v_w_out, v_final_g):
    xi, yi, ci = lax.axis_index("x"), lax.axis_index("y"), lax.axis_index("c")
    k = 2 * xi + yi
    t = x.shape[1]
    x2 = x.reshape(t, D_MODEL)
    tgt2 = loss_target.reshape(t, D_MODEL)
    row = lambda a: a.reshape(1, -1)

    small = jnp.concatenate([conv_w, lru_conv_w, jnp.zeros((1, conv_w.shape[1]), F32)], axis=0)
    proj, xn, w12, sm4 = _gather_in_projection(x2, row(ln_g), w_in, small)
    convs = jnp.transpose(sm4, (1, 0, 2)).reshape(SUBLANES, D_PART)
    pvec = jnp.concatenate(
        [convs[0:7], row(lru_conv_b), row(b_a), row(b_i), row(lam), row(conv_out_g), row(lru_out_g),
         jnp.zeros((PV_ROWS - N_ACC, D_PART), F32)], axis=0)
    wai = jnp.concatenate([_block_diag_strips(w_a), _block_diag_strips(w_i)], axis=2).astype(MXU_DTYPE)

    c_arr = jnp.reshape(ci, (1,)).astype(jnp.int32)
    kc_arr = jnp.stack([k, ci]).astype(jnp.int32)
    yc, yl, h, u, r, ig, wo4 = _mixer_forward(proj, pvec, wai, w_out)
    wo = wo4.reshape(2 * D_PART, D_MODEL)
    do, dob, dy, st_out = _out_projection_loss(yc, yl, x2, tgt2, wo, row(final_g))
    go4, go4b = _w_out_grad(yc, yl, dob)
    s_out, sb_out = _add_own_half(go4, _exchange_sibling_halves(go4b, "exchange_sibling_halves_out"), c_arr, "add_own_half_out")
    dproj, g_wai, svec, r2o = _mixer_backward(proj, h, u, r, ig, dy, pvec, wai, sb_out)
    gwa = _strip_diag_blocks(g_wai[:, :, 0:LW]).reshape(LRU_HEAD, D_PART)
    gwi = _strip_diag_blocks(g_wai[:, :, LW:2 * LW]).reshape(LRU_HEAD, D_PART)
    g12, g12b, red = _w_in_grad(xn, dproj, jnp.concatenate([svec, st_out, gwa, gwi], axis=0))
    s_in, sb_in = _add_own_half(g12, _exchange_sibling_halves(g12b, "exchange_sibling_halves_in"), c_arr, "add_own_half_in")
    grad_x, st_in, r2i = _input_grad(dproj, w12, x2, do, row(ln_g), sb_in)
    f_in = _sum_chip_blocks(s_in, r2i, kc_arr, CHUNKS_PER_BLOCK, "sum_chip_blocks_in")
    f_out = _sum_chip_blocks(s_out, r2o, kc_arr, 1, "sum_chip_blocks_out")
    f_in, f_out, red_ln = _swap_halves_and_sum(f_in, f_out, st_in)
    r_out = PV_ROWS
    r_wa = PV_ROWS + SUBLANES
    r_wi = r_wa + LRU_HEAD
    loss = red[r_out + 1, 0]

    g_w_in, d_w_in, nm_w_in, nv_w_in = _adam_w_in(w_in, m_w_in, v_w_in, f_in)
    g_w_out = f_out[0]
    d_w_out, nm_w_out, nv_w_out = _adam_w_out(w_out, m_w_out, v_w_out, g_w_out)

    ncol = conv_w.shape[1]
    conv_cols = lax.dynamic_slice(red, (0, k * ncol), (SUBLANES, ncol))
    g_small = {
        "ln_g": red_ln[0], "conv_w": conv_cols[0:3], "lru_conv_w": conv_cols[3:7], "lru_conv_b": red[PV_LRU_B],
        "w_a": red[r_wa:r_wa + LRU_HEAD].reshape(w_a.shape), "b_a": red[PV_BA],
        "w_i": red[r_wi:r_wi + LRU_HEAD].reshape(w_i.shape), "b_i": red[PV_BI], "lam": red[PV_LAM],
        "conv_out_g": red[PV_CG], "lru_out_g": red[PV_LG], "final_g": red[r_out],
    }
    w_small = {"ln_g": ln_g, "conv_w": conv_w, "lru_conv_w": lru_conv_w, "lru_conv_b": lru_conv_b, "w_a": w_a, "b_a": b_a,
               "w_i": w_i, "b_i": b_i, "lam": lam, "conv_out_g": conv_out_g, "lru_out_g": lru_out_g, "final_g": final_g}
    m_small = {"ln_g": m_ln_g, "conv_w": m_conv_w, "lru_conv_w": m_lru_conv_w, "lru_conv_b": m_lru_conv_b, "w_a": m_w_a,
               "b_a": m_b_a, "w_i": m_w_i, "b_i": m_b_i, "lam": m_lam, "conv_out_g": m_conv_out_g,
               "lru_out_g": m_lru_out_g, "final_g": m_final_g}
    v_small = {"ln_g": v_ln_g, "conv_w": v_conv_w, "lru_conv_w": v_lru_conv_w, "lru_conv_b": v_lru_conv_b, "w_a": v_w_a,
               "b_a": v_b_a, "w_i": v_w_i, "b_i": v_b_i, "lam": v_lam, "conv_out_g": v_conv_out_g,
               "lru_out_g": v_lru_out_g, "final_g": v_final_g}
    names = list(w_small)
    as2d = lambda a: a.reshape(1, -1) if a.ndim == 1 else a
    d_s, m_s, v_s = _adam_small([as2d(w_small[n]) for n in names], [as2d(m_small[n]) for n in names],
                                [as2d(v_small[n]) for n in names], [as2d(g_small[n]) for n in names])
    back = lambda n, a: a.reshape(w_small[n].shape)
    grads = {n: g_small[n] for n in names}
    deltas = {n: back(n, a) for n, a in zip(names, d_s)}
    new_m = {n: back(n, a) for n, a in zip(names, m_s)}
    new_v = {n: back(n, a) for n, a in zip(names, v_s)}
    grads["w_in"], deltas["w_in"], new_m["w_in"], new_v["w_in"] = g_w_in, d_w_in, nm_w_in, nv_w_in
    grads["w_out"], deltas["w_out"], new_m["w_out"], new_v["w_out"] = g_w_out, d_w_out, nm_w_out, nv_w_out

    order = ["ln_g", "w_in", "conv_w", "lru_conv_w", "lru_conv_b", "w_a", "b_a", "w_i", "b_i", "lam", "conv_out_g",
             "lru_out_g", "w_out", "final_g"]
    return (loss, grad_x.reshape(x.shape), *[grads[n] for n in order], *[deltas[n] for n in order],
            *[new_m[n] for n in order], *[new_v[n] for n in order])
```

```python
import functools

import jax
import jax.numpy as jnp
from jax import lax
from jax.experimental import pallas as pl
from jax.experimental.pallas import tpu as pltpu

F32 = jnp.float32
MXU_DTYPE = jnp.bfloat16

D_MODEL = 1024
D_PART = 1024
N_PARTS = 6
CHUNK = 512
CHUNKS_PER_BLOCK = 3
N_CHUNKS = 12
N_CHIPS = 4
SUBLANES = 8
LANES = 128
LW = 256
UNROLL = 8
NS = D_PART // LW
STRIPS_PER_CHUNK = CHUNK // LW
CONV_HEAD = 128
LRU_HEAD = 64
HEADS_PER_STRIP = LW // LRU_HEAD
RMS_EPS = 1e-6
RG_LRU_C = 8.0
ADAM_LR = 0.001
ADAM_B1 = 0.9
ADAM_B2 = 0.999
ADAM_EPS = 1e-08
ADAM_WD = 0.01
ADAM_STEP = 10

PV_CONV_W = 0
PV_LRU_W = 3
PV_LRU_B = 7
PV_BA = 8
PV_BI = 9
PV_LAM = 10
PV_CG = 11
PV_LG = 12
PV_ROWS = 16
N_ACC = 13

SLAB = 128
MESH = pl.DeviceIdType.MESH
VMEM_LIMIT = 56 * 1024 * 1024
ARB = "arbitrary"


def _cp(*sem, **kw):
    return pltpu.CompilerParams(dimension_semantics=sem or None, vmem_limit_bytes=VMEM_LIMIT, **kw)


def _mm(a, b):
    return jnp.dot(a, b, preferred_element_type=F32)


def _mm_nt(a, b):
    return lax.dot_general(a, b, (((1,), (1,)), ((), ())), preferred_element_type=F32)


def _mm_tn(a, b):
    return lax.dot_general(a, b, (((0,), (0,)), ((), ())), preferred_element_type=F32)


def _sigmoid(x):
    return 0.5 * jnp.tanh(0.5 * x) + 0.5


def _log_sigmoid(x):
    z = jnp.exp(-jnp.abs(x))
    u = 1.0 + z
    log1p = jnp.where(u == 1.0, z, jnp.log(u) * z / (u - 1.0))
    return jnp.minimum(x, 0.0) - log1p


def _head_mean(z, head):
    out = []
    for k in range(z.shape[1] // LANES):
        zk = z[:, LANES * k:LANES * (k + 1)]
        if head == LANES:
            m = jnp.sum(zk, axis=-1, keepdims=True) * (1.0 / head)
            out.append(jnp.broadcast_to(m, zk.shape))
        else:
            lo = lax.broadcasted_iota(jnp.int32, zk.shape, 1) < head
            s_lo = jnp.sum(jnp.where(lo, zk, 0.0), axis=-1, keepdims=True)
            s_hi = jnp.sum(jnp.where(lo, 0.0, zk), axis=-1, keepdims=True)
            out.append(jnp.where(lo, s_lo, s_hi) * (1.0 / head))
    return jnp.concatenate(out, axis=1)


def _shift_down(cur, prev, d, row):
    return pltpu.roll(jnp.where(row < SUBLANES - d, cur, prev), d, 0)


def _shift_up(cur, nxt, d, row):
    return pltpu.roll(jnp.where(row >= d, cur, nxt), SUBLANES - d, 0)


def _scan8_fwd(a, b, row):
    A, B = a, b
    for d in (1, 2, 4):
        m = row >= d
        a_s = jnp.where(m, pltpu.roll(A, d, 0), 1.0)
        b_s = jnp.where(m, pltpu.roll(B, d, 0), 0.0)
        B = A * b_s + B
        A = A * a_s
    return A, B


def _scan8_rev(a, b, row):
    A, B = a, b
    for d in (1, 2, 4):
        m = row < SUBLANES - d
        a_s = jnp.where(m, pltpu.roll(A, SUBLANES - d, 0), 1.0)
        b_s = jnp.where(m, pltpu.roll(B, SUBLANES - d, 0), 0.0)
        B = A * b_s + B
        A = A * a_s
    return A, B


def _decay(r, lsb):
    la = (RG_LRU_C * r) * lsb
    a = jnp.exp(la)
    e2 = a * a
    em = -jnp.tanh(la) * (1.0 + e2)
    inv_mult = lax.rsqrt(em)
    return a, e2, em * inv_mult, inv_mult


def _mesh_pos():
    x, y, c = lax.axis_index("x"), lax.axis_index("y"), lax.axis_index("c")
    chips = [(1 - x, y), (x, 1 - y), (1 - x, 1 - y)]
    return x, y, c, chips


def _gather_in_projection(x, ln_g, w_in, small):
    t = x.shape[0]
    rb_x = 512
    rb_mm = 1024
    n_mm = t // rb_mm
    half = w_in.shape[0] // 2

    def body(x_hbm, g_ref, wi_ref, sm_ref, proj_hbm, xn_ref, w12_ref, sm4_ref,
             xbuf, obuf, x_sems, o_sems, send_sems, recv_sems):
        x_, y_, c, chips = _mesh_pos()
        k = 2 * x_ + y_
        sib = (x_, y_, 1 - c)
        sm4_ref[k] = sm_ref[...]

        def remote(ref, sem, to):
            return pltpu.make_async_remote_copy(src_ref=ref, dst_ref=ref, send_sem=send_sems.at[sem],
                                                recv_sem=recv_sems.at[sem], device_id=to, device_id_type=MESH)

        def chunk_of(chip, s):
            return CHUNKS_PER_BLOCK * (2 * chip[0] + chip[1]) + s

        def piece(q, core, first=0, rows=half):
            return w12_ref.at[q, pl.ds(pl.multiple_of(half * core + first, SUBLANES * 2), rows), :]

        nbr_x, nbr_y, diagonal = chips
        quarter = half // 2
        DIAG = [(0, 0, half, 0), (1, 0, quarter, 0), (1, quarter, quarter, 1), (2, 0, half, 1)]
        ici = lambda m, s: 2 * s + m
        dgn = lambda j: 6 + j
        to_sib = 10
        sml = lambda m: 20 + m

        sends = []
        for s in range(CHUNKS_PER_BLOCK):
            w12_ref[chunk_of((x_, y_), s)] = wi_ref[:, CHUNK * s:CHUNK * (s + 1)].astype(MXU_DTYPE)
            for m, chip in enumerate((nbr_x, nbr_y)):
                sends.append(remote(piece(chunk_of((x_, y_), s), c), ici(m, s), (*chip, c)))
                sends[-1].start()
        for m, chip in enumerate(chips):
            sends.append(remote(sm4_ref.at[k], sml(m), (*chip, c)))
            sends[-1].start()

        def x_copy(rb, slot):
            return pltpu.make_async_copy(x_hbm.at[pl.ds(rb * rb_x, rb_x), :], xbuf.at[slot], x_sems.at[slot])

        x_copy(0, 0).start()
        for rb in range(t // rb_x):
            slot = rb % 2
            x_copy(rb, slot).wait()
            if rb + 1 < t // rb_x:
                x_copy(rb + 1, 1 - slot).start()

            def norm_slab(sl, carry, rb=rb, slot=slot):
                xf = xbuf[slot, pl.ds(pl.multiple_of(sl * SLAB, SLAB), SLAB), :]
                r = lax.rsqrt(jnp.mean(xf * xf, axis=-1, keepdims=True) + RMS_EPS)
                xn_ref[pl.ds(pl.multiple_of(rb * rb_x + sl * SLAB, SLAB), SLAB), :] = ((xf * r) * g_ref[...]).astype(MXU_DTYPE)
                return carry

            lax.fori_loop(0, rb_x // SLAB, norm_slab, 0)

        def out_copy(q, i):
            return pltpu.make_async_copy(obuf.at[i], proj_hbm.at[q, pl.ds(pl.multiple_of(i * rb_mm, rb_mm), rb_mm), :],
                                         o_sems.at[i])

        def project(q, very_first):
            def row_block(i, carry):
                if not very_first:
                    out_copy(q, i).wait()
                obuf[i] = _mm(xn_ref[pl.ds(pl.multiple_of(i * rb_mm, rb_mm), rb_mm), :], w12_ref[q])
                out_copy(q, i).start()
                return carry

            lax.fori_loop(0, n_mm, row_block, 0)

        for s in range(CHUNKS_PER_BLOCK):
            project(chunk_of((x_, y_), s), very_first=(s == 0))

        steps = []
        for s in range(CHUNKS_PER_BLOCK):
            for m, chip in enumerate((nbr_x, nbr_y)):
                onward = [(first, rows, dgn(j), chips[via]) for j, (cs, first, rows, via) in enumerate(DIAG)
                          if cs == s and via == 1 - m]
                steps.append((chunk_of(chip, s), [(0, half, ici(m, s))], onward))
        for s in range(CHUNKS_PER_BLOCK):
            steps.append((chunk_of(diagonal, s), [(first, rows, dgn(j)) for j, (cs, first, rows, _) in enumerate(DIAG) if cs == s], []))

        def project_when_whole(step):
            q, pieces, _ = step
            for first, rows, sem in pieces:
                remote(piece(q, 1 - c, first, rows), to_sib + sem, sib).wait_recv()
            project(q, very_first=False)

        passed = []
        for j, (q, pieces, onward) in enumerate(steps):
            for first, rows, sem in pieces:
                remote(piece(q, c, first, rows), sem, sib).wait_recv()
            for first, rows, sem, chip in onward:
                passed.append(remote(piece(q, c, first, rows), sem, (*chip, c)))
                passed[-1].start()
            for first, rows, sem in pieces:
                passed.append(remote(piece(q, c, first, rows), to_sib + sem, sib))
                passed[-1].start()
            if j > 0:
                project_when_whole(steps[j - 1])
        project_when_whole(steps[-1])

        for m, chip in enumerate(chips):
            remote(sm4_ref.at[2 * chip[0] + chip[1]], sml(m), sib).wait_recv()
        for cp in sends + passed:
            cp.wait_send()
        for i in range(n_mm):
            out_copy(0, i).wait()

    vm = pl.BlockSpec(memory_space=pltpu.VMEM)
    hbm = pl.BlockSpec(memory_space=pl.ANY)
    n_sems = 23
    return pl.pallas_call(
        body,
        out_shape=(jax.ShapeDtypeStruct((N_CHUNKS, t, CHUNK), F32), jax.ShapeDtypeStruct((t, D_MODEL), MXU_DTYPE),
                   jax.ShapeDtypeStruct((N_CHUNKS, w_in.shape[0], CHUNK), MXU_DTYPE),
                   jax.ShapeDtypeStruct((N_CHIPS,) + small.shape, F32)),
        in_specs=[hbm, vm, vm, vm], out_specs=(hbm, vm, vm, vm),
        scratch_shapes=[pltpu.VMEM((2, rb_x, D_MODEL), F32), pltpu.VMEM((n_mm, rb_mm, CHUNK), F32),
                        pltpu.SemaphoreType.DMA((2,)), pltpu.SemaphoreType.DMA((n_mm,)),
                        pltpu.SemaphoreType.DMA((n_sems,)), pltpu.SemaphoreType.DMA((n_sems,))],
        compiler_params=_cp(), name="gather_in_projection",
    )(x, ln_g, w_in, small)


def _allreduce_behind(step, when, in_ref, acc_s, rbufs, out_ref, send_sems, recv_sems):
    x, y, c, _ = _mesh_pos()
    peers = [(x, y, 1 - c), (1 - x, y, c), (x, 1 - y, c)]

    def exchange(ph):
        return pltpu.make_async_remote_copy(src_ref=acc_s, dst_ref=rbufs[ph], send_sem=send_sems.at[ph],
                                            recv_sem=recv_sems.at[ph], device_id=peers[ph], device_id_type=MESH)

    @pl.when(step == when[0])
    def _():
        acc_s[...] = in_ref[...]
        exchange(0).start()

    for ph in (1, 2):
        @pl.when(step == when[ph])
        def _(ph=ph):
            exchange(ph - 1).wait()
            acc_s[...] = acc_s[...] + rbufs[ph - 1][...]
            exchange(ph).start()

    @pl.when(step == when[3])
    def _():
        exchange(2).wait()
        out_ref[...] = acc_s[...] + rbufs[2][...]


def _exchange_sibling_halves(g, name):
    n, rows, cols = g.shape
    half = rows // 2

    def body(g_ref, r_ref, send_sem, recv_sem):
        x, y, c, _ = _mesh_pos()
        cp = pltpu.make_async_remote_copy(src_ref=g_ref.at[:, pl.ds(pl.multiple_of(half * (1 - c), half), half), :],
                                          dst_ref=r_ref, send_sem=send_sem, recv_sem=recv_sem,
                                          device_id=(x, y, 1 - c), device_id_type=MESH)
        cp.start()
        cp.wait()

    hbm = pl.BlockSpec(memory_space=pl.ANY)
    return pl.pallas_call(
        body, out_shape=jax.ShapeDtypeStruct((n, half, cols), g.dtype), in_specs=[hbm], out_specs=hbm,
        scratch_shapes=[pltpu.SemaphoreType.DMA, pltpu.SemaphoreType.DMA],
        compiler_params=_cp(), name=name,
    )(g)


def _add_own_half(g, r, c_arr, name):
    n, rr, cc = r.shape

    def body(c_ref, g_ref, r_ref, o_ref, ob_ref):
        s = g_ref[...] + r_ref[...].astype(F32)
        o_ref[...] = s
        ob_ref[...] = s.astype(jnp.bfloat16)

    blk = pl.BlockSpec((1, rr, cc), lambda q, c_ref: (q, 0, 0))
    return pl.pallas_call(
        body, out_shape=(jax.ShapeDtypeStruct(r.shape, F32), jax.ShapeDtypeStruct(r.shape, jnp.bfloat16)),
        grid_spec=pltpu.PrefetchScalarGridSpec(
            num_scalar_prefetch=1, grid=(n,),
            in_specs=[pl.BlockSpec((1, rr, cc), lambda q, c_ref: (q, c_ref[0], 0)), blk],
            out_specs=(blk, blk)),
        compiler_params=_cp(ARB), name=name,
    )(c_arr, g, r)


def _chip_block_copies(s_ref, r_ref, n_sub, send_sems, recv_sems):
    x, y, c, chips = _mesh_pos()
    cps = []
    for m, chip in enumerate(chips):
        kk = 2 * chip[0] + chip[1]
        cps.append(pltpu.make_async_remote_copy(
            src_ref=s_ref.at[pl.ds(n_sub * kk, n_sub)], dst_ref=r_ref.at[m],
            send_sem=send_sems.at[m], recv_sem=recv_sems.at[m], device_id=(*chip, c), device_id_type=MESH))
    return cps


def _gather_w_out(step, n_steps, wo_ref, wob_s, wo4_ref, local_sem, send_sems, recv_sems):
    x, y, c, chips = _mesh_pos()
    sib = (x, y, 1 - c)
    half = wo_ref.shape[0] // 2

    def rows(core):
        return pl.ds(pl.multiple_of(half * core, half), half)

    def block_half(chip, core):
        return wo4_ref.at[2 * chip[0] + chip[1], rows(core), :]

    def remote(src, dst, sem, to):
        return pltpu.make_async_remote_copy(src_ref=src, dst_ref=dst, send_sem=send_sems.at[sem], recv_sem=recv_sems.at[sem],
                                            device_id=to, device_id_type=MESH)

    local = pltpu.make_async_copy(wob_s, wo4_ref.at[2 * x + y], local_sem)
    ici = [remote(wob_s.at[rows(c), :], block_half((x, y), c), m, (*chip, c)) for m, chip in enumerate(chips)]
    fwd = [remote(block_half(chip, c), block_half(chip, c), 3 + m, sib) for m, chip in enumerate(chips)]

    @pl.when(step == 0)
    def _():
        wob_s[...] = wo_ref[...].astype(MXU_DTYPE)
        local.start()
        for cp in ici:
            cp.start()

    @pl.when(step == n_steps // 2)
    def _():
        for m, chip in enumerate(chips):
            remote(block_half(chip, c), block_half(chip, c), m, sib).wait_recv()
            fwd[m].start()

    @pl.when(step == n_steps - 1)
    def _():
        for m, chip in enumerate(chips):
            remote(block_half(chip, 1 - c), block_half(chip, 1 - c), 3 + m, sib).wait_recv()
        for cp in ici + fwd:
            cp.wait_send()
        local.wait()


def _chip_blocks_shape(s, n_sub):
    return jax.ShapeDtypeStruct((3, n_sub) + s.shape[1:], s.dtype)


def _sum_chip_blocks(s, r, kc_arr, n_sub, name):
    _, rr, cc = s.shape

    def body(kc_ref, s_ref, r_ref, o_ref):
        o_ref[...] = ((s_ref[...] + r_ref[0].astype(F32)) + r_ref[1].astype(F32)) + r_ref[2].astype(F32)

    return pl.pallas_call(
        body, out_shape=jax.ShapeDtypeStruct((n_sub, 2 * rr, cc), F32),
        grid_spec=pltpu.PrefetchScalarGridSpec(
            num_scalar_prefetch=1, grid=(n_sub,),
            in_specs=[pl.BlockSpec((1, rr, cc), lambda q, kc: (n_sub * kc[0] + q, 0, 0)),
                      pl.BlockSpec((3, 1, rr, cc), lambda q, kc: (0, q, 0, 0))],
            out_specs=pl.BlockSpec((1, rr, cc), lambda q, kc: (q, kc[1], 0))),
        compiler_params=_cp(ARB), name=name,
    )(kc_arr, s, r)


def _swap_halves_and_sum(f_in, f_out, v):
    hi, ho = f_in.shape[1] // 2, f_out.shape[1] // 2
    n_dev = 8

    def body(fi_in, fo_in, v_ref, fi_ref, fo_ref, tot_ref, slots, send_sems, recv_sems):
        del fi_in, fo_in
        x, y, c, _ = _mesh_pos()
        sib = (x, y, 1 - c)
        me = 4 * x + 2 * y + c
        slots[me] = v_ref[...]
        si = fi_ref.at[:, pl.ds(pl.multiple_of(hi * c, hi), hi), :]
        so = fo_ref.at[:, pl.ds(pl.multiple_of(ho * c, ho), ho), :]

        def remote(ref, sem, to):
            return pltpu.make_async_remote_copy(src_ref=ref, dst_ref=ref, send_sem=send_sems.at[sem],
                                                recv_sem=recv_sems.at[sem], device_id=to, device_id_type=MESH)

        cps = [remote(si, n_dev - 1, sib), remote(so, n_dev, sib)]
        for d in range(1, n_dev):
            peer = (1 - x if d & 4 else x, 1 - y if d & 2 else y, 1 - c if d & 1 else c)
            cps.append(remote(slots.at[me], d - 1, peer))
        for cp in cps:
            cp.start()
        for cp in cps:
            cp.wait()
        total = slots[0]
        for dev in range(1, n_dev):
            total = total + slots[dev]
        tot_ref[...] = total

    hbm = pl.BlockSpec(memory_space=pl.ANY)
    vm = pl.BlockSpec(memory_space=pltpu.VMEM)
    return pl.pallas_call(
        body,
        out_shape=(jax.ShapeDtypeStruct(f_in.shape, F32), jax.ShapeDtypeStruct(f_out.shape, F32),
                   jax.ShapeDtypeStruct(v.shape, F32)),
        in_specs=[hbm, hbm, vm], out_specs=(hbm, hbm, vm), input_output_aliases={0: 0, 1: 1},
        scratch_shapes=[pltpu.VMEM((n_dev,) + v.shape, F32), pltpu.SemaphoreType.DMA((n_dev + 1,)),
                        pltpu.SemaphoreType.DMA((n_dev + 1,))],
        compiler_params=_cp(), name="swap_halves_and_sum",
    )(f_in, f_out, v)


def _out_projection_loss(yc, yl, x, target, wo, final_g):
    t = x.shape[0]
    tm = 512

    def body(yc_ref, yl_ref, x_ref, t_ref, wo_ref, fg_ref, do_ref, dob_ref, dy_ref, st_ref, y_wo):
        @pl.when(pl.program_id(0) == 0)
        def _():
            st_ref[...] = jnp.zeros_like(st_ref)

        y_wo[...] = _mm(yc_ref[...], wo_ref[0:D_PART, :]) + _mm(yl_ref[...], wo_ref[D_PART:2 * D_PART, :])

        def norm_loss_slab(s, carry):
            g_sum, loss_sum = carry
            rows = pl.ds(pl.multiple_of(s * SLAB, SLAB), SLAB)
            o = x_ref[rows, :] + y_wo[rows, :]
            r2 = lax.rsqrt(jnp.mean(o * o, axis=-1, keepdims=True) + RMS_EPS)
            ohat = o * r2
            fg = fg_ref[...]
            diff = ohat * fg - t_ref[rows, :]
            dout = diff * (1.0 / D_MODEL)
            gp = dout * fg
            do = r2 * (gp - ohat * jnp.mean(gp * ohat, axis=-1, keepdims=True))
            do_ref[rows, :] = do
            dob_ref[rows, :] = do.astype(MXU_DTYPE)
            loss = 0.5 * jnp.sum(jnp.sum(diff * diff, axis=-1, keepdims=True) * (1.0 / D_MODEL), axis=0, keepdims=True)
            return g_sum + jnp.sum(dout * ohat, axis=0, keepdims=True), loss_sum + loss

        g_sum, loss_sum = lax.fori_loop(0, tm // SLAB, norm_loss_slab,
                                        (jnp.zeros((1, D_MODEL), F32), jnp.zeros((1, 1), F32)))
        st_ref[0:1, :] += g_sum
        st_ref[1:2, :] += jnp.broadcast_to(loss_sum, (1, D_MODEL))
        dy_ref[...] = _mm_nt(dob_ref[...], wo_ref[...])

    row = lambda i: (i, 0)
    fix = lambda i: (0, 0)
    return pl.pallas_call(
        body, grid=(t // tm,),
        in_specs=[pl.BlockSpec((tm, D_PART), row), pl.BlockSpec((tm, D_PART), row),
                  pl.BlockSpec((tm, D_MODEL), row), pl.BlockSpec((tm, D_MODEL), row),
                  pl.BlockSpec((2 * D_PART, D_MODEL), fix), pl.BlockSpec((1, D_MODEL), fix)],
        out_specs=(pl.BlockSpec((tm, D_MODEL), row), pl.BlockSpec((tm, D_MODEL), row),
                   pl.BlockSpec((tm, 2 * D_PART), row), pl.BlockSpec((SUBLANES, D_MODEL), fix)),
        out_shape=(jax.ShapeDtypeStruct((t, D_MODEL), F32), jax.ShapeDtypeStruct((t, D_MODEL), MXU_DTYPE),
                   jax.ShapeDtypeStruct((t, 2 * D_PART), F32), jax.ShapeDtypeStruct((SUBLANES, D_MODEL), F32)),
        scratch_shapes=[pltpu.VMEM((tm, D_MODEL), F32)],
        compiler_params=_cp(ARB), name="out_projection_loss",
    )(yc, yl, x, target, wo, final_g)


def _input_grad(dproj, w12, x, do, ln_g, sb_in):
    t = x.shape[0]
    tm = 1024

    def body(dp_ref, w_ref, x_ref, do_ref, g_ref, s_ref, gx_ref, st_ref, r_ref, acc, send_sems, recv_sems):
        i, p = pl.program_id(0), pl.program_id(1)

        @pl.when((i == 0) & (p == 0))
        def _():
            st_ref[...] = jnp.zeros_like(st_ref)
            for cp in _chip_block_copies(s_ref, r_ref, CHUNKS_PER_BLOCK, send_sems, recv_sems):
                cp.start()

        @pl.when((i == t // tm - 1) & (p == N_PARTS - 1))
        def _():
            for cp in _chip_block_copies(s_ref, r_ref, CHUNKS_PER_BLOCK, send_sems, recv_sems):
                cp.wait()

        @pl.when(p == 0)
        def _():
            acc[...] = jnp.zeros_like(acc)

        acc[...] += _mm_nt(dp_ref[0], jnp.concatenate([w_ref[0], w_ref[1]], axis=1))

        @pl.when(p == N_PARTS - 1)
        def _():
            def norm_bwd_slab(s, g_sum):
                rows = pl.ds(pl.multiple_of(s * SLAB, SLAB), SLAB)
                xf = x_ref[rows, :]
                r = lax.rsqrt(jnp.mean(xf * xf, axis=-1, keepdims=True) + RMS_EPS)
                xhat = xf * r
                dxn = acc[rows, :]
                dxh = dxn * g_ref[...]
                gx_ref[rows, :] = do_ref[rows, :] + r * (dxh - xhat * jnp.mean(dxh * xhat, axis=-1, keepdims=True))
                return g_sum + jnp.sum(dxn * xhat, axis=0, keepdims=True)

            st_ref[0:1, :] += lax.fori_loop(0, tm // SLAB, norm_bwd_slab, jnp.zeros((1, D_MODEL), F32))

    row = lambda i, p: (i, 0)
    fix = lambda i, p: (0, 0)
    return pl.pallas_call(
        body, grid=(t // tm, N_PARTS),
        in_specs=[
            pl.BlockSpec((1, tm, D_PART), lambda i, p: (p, i, 0)),
            pl.BlockSpec((2, D_MODEL, CHUNK), lambda i, p: (p, 0, 0)),
            pl.BlockSpec((tm, D_MODEL), row), pl.BlockSpec((tm, D_MODEL), row), pl.BlockSpec((1, D_MODEL), fix),
            pl.BlockSpec(memory_space=pl.ANY)],
        out_specs=(pl.BlockSpec((tm, D_MODEL), row), pl.BlockSpec((SUBLANES, D_MODEL), fix),
                   pl.BlockSpec(memory_space=pl.ANY)),
        out_shape=(jax.ShapeDtypeStruct((t, D_MODEL), F32), jax.ShapeDtypeStruct((SUBLANES, D_MODEL), F32),
                   _chip_blocks_shape(sb_in, CHUNKS_PER_BLOCK)),
        scratch_shapes=[pltpu.VMEM((tm, D_MODEL), F32), pltpu.SemaphoreType.DMA((3,)), pltpu.SemaphoreType.DMA((3,))],
        compiler_params=_cp(ARB, ARB), name="input_grad",
    )(dproj, w12, x, do, ln_g, sb_in)


def _w_in_grad(xn, dproj, small):
    t = xn.shape[0]
    small_shape = pltpu.VMEM(small.shape, F32)

    def body(xn_ref, dp_ref, sm_ref, o_ref, ob_ref, red_ref, acc_s, r0, r1, r2, send_sems, recv_sems):
        _allreduce_behind(pl.program_id(0), (0, 1, 3, N_PARTS - 1), sm_ref, acc_s, (r0, r1, r2), red_ref, send_sems, recv_sems)
        g = _mm_tn(xn_ref[...], dp_ref[0])
        for s in range(2):
            o_ref[s] = g[:, CHUNK * s:CHUNK * (s + 1)]
            ob_ref[s] = g[:, CHUNK * s:CHUNK * (s + 1)].astype(jnp.bfloat16)

    whole = pl.BlockSpec(small.shape, lambda p: (0, 0))
    pair = pl.BlockSpec((2, D_MODEL, CHUNK), lambda p: (p, 0, 0))
    return pl.pallas_call(
        body, grid=(N_PARTS,),
        in_specs=[pl.BlockSpec((t, D_MODEL), lambda p: (0, 0)),
                  pl.BlockSpec((1, t, D_PART), lambda p: (p, 0, 0)), whole],
        out_specs=(pair, pair, whole),
        out_shape=(jax.ShapeDtypeStruct((N_CHUNKS, D_MODEL, CHUNK), F32),
                   jax.ShapeDtypeStruct((N_CHUNKS, D_MODEL, CHUNK), jnp.bfloat16), jax.ShapeDtypeStruct(small.shape, F32)),
        scratch_shapes=[small_shape] * 4 + [pltpu.SemaphoreType.DMA((3,)), pltpu.SemaphoreType.DMA((3,))],
        compiler_params=_cp(ARB), name="w_in_grad",
    )(xn, dproj, small)


def _w_out_grad(yc, yl, dob):
    t = yc.shape[0]
    tk = 2048

    def body(yc_ref, yl_ref, do_ref, o_ref, ob_ref):
        j, kk = pl.program_id(0), pl.program_id(1)

        def accumulate(y_ref):
            @pl.when(kk == 0)
            def _():
                o_ref[...] = jnp.zeros_like(o_ref)

            o_ref[...] += _mm_tn(y_ref[...], do_ref[...])

            @pl.when(kk == t // tk - 1)
            def _():
                ob_ref[...] = o_ref[...].astype(jnp.bfloat16)

        pl.when(j == 0)(functools.partial(accumulate, yc_ref))
        pl.when(j == 1)(functools.partial(accumulate, yl_ref))

    def rows_of(half):
        return lambda j, kk: (jnp.where(j == half, kk, 0), 0)

    half = pl.BlockSpec((D_PART, D_MODEL), lambda j, kk: (j, 0))
    out, out_b = pl.pallas_call(
        body, grid=(2, t // tk),
        in_specs=[pl.BlockSpec((tk, D_PART), rows_of(0)), pl.BlockSpec((tk, D_PART), rows_of(1)),
                  pl.BlockSpec((tk, D_MODEL), lambda j, kk: (kk, 0))],
        out_specs=(half, half),
        out_shape=(jax.ShapeDtypeStruct((2 * D_PART, D_MODEL), F32), jax.ShapeDtypeStruct((2 * D_PART, D_MODEL), jnp.bfloat16)),
        compiler_params=_cp(ARB, ARB), name="w_out_grad",
    )(yc, yl, dob)
    blocks = (N_CHIPS, 2 * D_PART // N_CHIPS, D_MODEL)
    return out.reshape(blocks), out_b.reshape(blocks)


def _for_groups(n, fn, init, unroll=UNROLL, stores=(), descending=False):
    assert unroll % 2 == 0 and n % unroll == 0

    def trip(j, carry):
        held = None
        for uu in range(unroll):
            idx = j * unroll + uu
            carry, values = fn(idx, carry)
            if uu % 2 == 0:
                held = values
                continue
            low_group = n - 1 - idx if descending else idx - 1
            rows = pl.ds(pl.multiple_of(low_group * SUBLANES, 2 * SUBLANES), 2 * SUBLANES)
            pairs = zip(values, held) if descending else zip(held, values)
            for store, (lo, hi) in zip(stores, pairs, strict=True):
                store(rows, jnp.concatenate([lo, hi], axis=0).astype(MXU_DTYPE))
        return carry

    return lax.fori_loop(0, n // unroll, trip, init)


def _rows_of(ref, *lead, cols=slice(None)):
    def store(rows, value):
        ref[(*lead, rows, cols)] = value

    return store


def _pvb(pv_ref, r):
    return jnp.broadcast_to(pv_ref[r:r + 1, :], (SUBLANES, pv_ref.shape[1]))


def _conv3(pv_ref, u, u1, u2):
    return (_pvb(pv_ref, PV_CONV_W) * u2 + _pvb(pv_ref, PV_CONV_W + 1) * u1) + _pvb(pv_ref, PV_CONV_W + 2) * u


def _conv4(pv_ref, v, v1, v2, v3):
    return ((((_pvb(pv_ref, PV_LRU_W) * v3 + _pvb(pv_ref, PV_LRU_W + 1) * v2) + _pvb(pv_ref, PV_LRU_W + 2) * v1)
             + _pvb(pv_ref, PV_LRU_W + 3) * v) + _pvb(pv_ref, PV_LRU_B))


def _mixer_forward(proj, pvec, wai, w_out):
    t = proj.shape[1]
    tb = 1024
    ng = tb // SUBLANES
    nt = t // tb

    def body(bg_ref, cg_ref, xc_ref, gc_ref, xl_ref, gl_ref, pv_ref, wai_ref, wo_ref,
             yc_ref, yl_ref, h_ref, u_s, r_ref, ig_ref, wo4_ref,
             ucp_s, xlp_s, ls_s, hbuf_s, ub_s, gate_s, wob_s, local_sem, send_sems, recv_sems):
        _gather_w_out(pl.program_id(0) * nt + pl.program_id(1), NS * nt, wo_ref, wob_s, wo4_ref, local_sem, send_sems, recv_sems)

        @pl.when(pl.program_id(1) == 0)
        def _():
            ucp_s[...] = jnp.zeros_like(ucp_s)
            xlp_s[...] = jnp.zeros_like(xlp_s)
            hbuf_s[...] = jnp.zeros_like(hbuf_s)

        row = lax.broadcasted_iota(jnp.int32, (SUBLANES, LW), 0)
        ls_s[...] = _log_sigmoid(_pvb(pv_ref, PV_LAM))

        def conv_group(g, carry):
            ucp, xlp = carry
            sl = pl.ds(pl.multiple_of(g * SUBLANES, SUBLANES), SUBLANES)
            uc = cg_ref[sl, :] * xc_ref[sl, :]
            v = _conv3(pv_ref, uc, _shift_down(uc, ucp, 1, row), _shift_down(uc, ucp, 2, row))
            yc = bg_ref[sl, :] * v
            rr = lax.rsqrt(_head_mean(yc * yc, CONV_HEAD) + RMS_EPS)
            gc = gc_ref[sl, :]
            zc = ((yc * rr) * _pvb(pv_ref, PV_CG)) * (gc * _sigmoid(gc))
            xl = xl_ref[sl, :]
            u = _conv4(pv_ref, xl, _shift_down(xl, xlp, 1, row), _shift_down(xl, xlp, 2, row), _shift_down(xl, xlp, 3, row))
            u_s[sl, :] = u
            return (uc, xl), (zc, u)

        ucp, xlp = _for_groups(ng, conv_group, (ucp_s[...], xlp_s[...]), unroll=2 * UNROLL,
                               stores=(_rows_of(yc_ref), _rows_of(ub_s)))
        ucp_s[...] = ucp
        xlp_s[...] = xlp

        gate_s[...] = _mm(ub_s[...], wai_ref[0])

        def lru_group(g, h_before):
            sl = pl.ds(pl.multiple_of(g * SUBLANES, SUBLANES), SUBLANES)
            u = u_s[sl, :]
            r = _sigmoid(gate_s[sl, 0:LW] + _pvb(pv_ref, PV_BA))
            ig = _sigmoid(gate_s[sl, LW:2 * LW] + _pvb(pv_ref, PV_BI))
            r_ref[sl, :] = r
            ig_ref[sl, :] = ig
            a, _, mult, _ = _decay(r, ls_s[...])
            A, B = _scan8_fwd(a, mult * (ig * u), row)
            h = B + A * jnp.broadcast_to(h_before[SUBLANES - 1:SUBLANES, :], (SUBLANES, LW))
            h_ref[sl, :] = h
            rr = lax.rsqrt(_head_mean(h * h, LRU_HEAD) + RMS_EPS)
            gl = gl_ref[sl, :]
            return h, (((h * rr) * _pvb(pv_ref, PV_LG)) * (gl * _sigmoid(gl)),)

        hbuf_s[...] = _for_groups(ng, lru_group, hbuf_s[...], unroll=2 * UNROLL, stores=(_rows_of(yl_ref),))

    def part(p):
        return pl.BlockSpec((None, tb, LW), lambda c, i: (2 * p + c // STRIPS_PER_CHUNK, i, c % STRIPS_PER_CHUNK))

    strip = pl.BlockSpec((tb, LW), lambda c, i: (i, c))
    return pl.pallas_call(
        body, grid=(NS, nt),
        in_specs=[part(p) for p in range(N_PARTS)] + [
            pl.BlockSpec((PV_ROWS, LW), lambda c, i: (0, c)),
            pl.BlockSpec((1, LW, 2 * LW), lambda c, i: (c, 0, 0)),
            pl.BlockSpec(w_out.shape, lambda c, i: (0, 0))],
        out_specs=(strip,) * 6 + (pl.BlockSpec(memory_space=pl.ANY),),
        out_shape=(jax.ShapeDtypeStruct((t, D_PART), MXU_DTYPE),) * 2 + (jax.ShapeDtypeStruct((t, D_PART), F32),) * 4 + (
            jax.ShapeDtypeStruct((N_CHIPS,) + w_out.shape, MXU_DTYPE),),
        scratch_shapes=[pltpu.VMEM((SUBLANES, LW), F32), pltpu.VMEM((SUBLANES, LW), F32), pltpu.VMEM((SUBLANES, LW), F32),
                        pltpu.VMEM((SUBLANES, LW), F32), pltpu.VMEM((tb, LW), MXU_DTYPE),
                        pltpu.VMEM((tb, 2 * LW), F32), pltpu.VMEM(w_out.shape, MXU_DTYPE),
                        pltpu.SemaphoreType.DMA, pltpu.SemaphoreType.DMA((6,)), pltpu.SemaphoreType.DMA((6,))],
        compiler_params=_cp(ARB, ARB), name="mixer_forward",
    )(proj, proj, proj, proj, proj, proj, pvec, wai, w_out)


def _mixer_backward(proj, h, u, r, ig, dy, pvec, wai, sb_out):
    t = proj.shape[1]
    tb = 1024
    ng = tb // SUBLANES
    nt = t // tb
    gpb = tb // SUBLANES

    def body(bg_ref, cg_ref, xc_ref, gc_ref, xl_ref, gl_ref, h_ref, u_ref, r_ref, ig_ref, dyc_ref, dyl_ref,
             cgh_ref, xch_ref, xlh_ref, hh_ref, pv_ref, wai_ref, so_ref,
             dp_ref, gw_ref, sv_ref, ro_ref,
             ls_s, ub_s, uce_s, xle_s, he_s, dgb_s, du_s, gbuf_s,
             acc_s, an_s, dvn_s, dun_s, send_sems, recv_sems):
        i = pl.program_id(1)
        first_block = i == nt - 1

        @pl.when((pl.program_id(0) == 0) & (i == 0))
        def _():
            for cp in _chip_block_copies(so_ref, ro_ref, 1, send_sems, recv_sems):
                cp.start()

        @pl.when((pl.program_id(0) == NS - 1) & (i == nt - 1))
        def _():
            for cp in _chip_block_copies(so_ref, ro_ref, 1, send_sems, recv_sems):
                cp.wait()

        @pl.when(i == 0)
        def _():
            acc_s[...] = jnp.zeros_like(acc_s)
            gw_ref[...] = jnp.zeros_like(gw_ref)
            an_s[...] = jnp.zeros_like(an_s)
            dvn_s[...] = jnp.zeros_like(dvn_s)
            dun_s[...] = jnp.zeros_like(dun_s)
            gbuf_s[...] = jnp.zeros_like(gbuf_s)

        row = lax.broadcasted_iota(jnp.int32, (SUBLANES, LW), 0)
        ls_s[...] = _log_sigmoid(_pvb(pv_ref, PV_LAM))
        keep = jnp.where(first_block, 0.0, 1.0)
        uce_s[0:SUBLANES, :] = (cgh_ref[...] * xch_ref[...]) * keep
        xle_s[0:SUBLANES, :] = xlh_ref[...] * keep
        he_s[0:SUBLANES, :] = hh_ref[...] * keep
        xle_s[SUBLANES:SUBLANES + tb, :] = xl_ref[...]
        he_s[SUBLANES:SUBLANES + tb, :] = h_ref[...]

        uce_s[SUBLANES:SUBLANES + tb, :] = cg_ref[...] * xc_ref[...]

        def acc_add(k, v):
            acc_s[k] += v

        def main_group(gi, carry):
            a_next, dv_next, g_next = carry
            g = ng - 1 - gi
            r0 = pl.multiple_of(g * SUBLANES, SUBLANES)
            sl = pl.ds(r0, SUBLANES)
            sl_e = pl.ds(r0 + SUBLANES, SUBLANES)
            lsb = ls_s[...]
            u = u_ref[sl, :]
            r = r_ref[sl, :]
            ig = ig_ref[sl, :]
            a, e2, mult, inv_mult = _decay(r, lsb)
            gl = gl_ref[sl, :]
            sg = _sigmoid(gl)
            s_l = gl * sg
            h8 = he_s[sl_e, :]
            hprev = _shift_down(h8, he_s[sl, :], 1, row)
            rr = lax.rsqrt(_head_mean(h8 * h8, LRU_HEAD) + RMS_EPS)
            n = h8 * rr
            dz = dyl_ref[sl, :]
            lg = _pvb(pv_ref, PV_LG)
            acc_add(PV_LG, (dz * n) * s_l)
            p5 = ((dz * n) * lg) * (sg * (1.0 + gl * (1.0 - sg)))
            dn = (dz * lg) * s_l
            dh = rr * (dn - n * _head_mean(dn * n, LRU_HEAD))
            A, B = _scan8_rev(_shift_up(a, a_next, 1, row), dh, row)
            gg = B + A * jnp.broadcast_to(g_next[0:1, :], (SUBLANES, LW))
            da = gg * hprev
            iu = ig * u
            diu = gg * mult
            dla = da * a - (gg * iu) * (e2 * inv_mult)
            acc_add(PV_LAM, dla * (RG_LRU_C * r))
            dra = (dla * (RG_LRU_C * lsb)) * (r * (1.0 - r))
            dia = (diu * u) * (ig * (1.0 - ig))
            acc_add(PV_BA, dra)
            acc_add(PV_BI, dia)
            du_s[sl, :] = diu * ig
            bg = bg_ref[sl, :]
            gc = gc_ref[sl, :]
            uc = uce_s[sl_e, :]
            ucp = uce_s[sl, :]
            uc1 = _shift_down(uc, ucp, 1, row)
            uc2 = _shift_down(uc, ucp, 2, row)
            v = _conv3(pv_ref, uc, uc1, uc2)
            yc = bg * v
            rrc = lax.rsqrt(_head_mean(yc * yc, CONV_HEAD) + RMS_EPS)
            nc = yc * rrc
            sgc = _sigmoid(gc)
            s_c = gc * sgc
            dzc = dyc_ref[sl, :]
            cgain = _pvb(pv_ref, PV_CG)
            acc_add(PV_CG, (dzc * nc) * s_c)
            p3 = ((dzc * nc) * cgain) * (sgc * (1.0 + gc * (1.0 - sgc)))
            dnc = (dzc * cgain) * s_c
            dyc = rrc * (dnc - nc * _head_mean(dnc * nc, CONV_HEAD))
            dv = dyc * bg
            duc = (_pvb(pv_ref, PV_CONV_W + 2) * dv + _pvb(pv_ref, PV_CONV_W + 1) * _shift_up(dv, dv_next, 1, row)
                   + _pvb(pv_ref, PV_CONV_W) * _shift_up(dv, dv_next, 2, row))
            acc_add(PV_CONV_W + 2, dv * uc)
            acc_add(PV_CONV_W + 1, dv * uc1)
            acc_add(PV_CONV_W, dv * uc2)
            return (a, dv, gg), (dyc * v, duc * xc_ref[sl, :], duc * cg_ref[sl, :], p3, p5, dra, dia, u)

        a_next, dv_next, g_next = _for_groups(
            ng, main_group, (an_s[...], dvn_s[...], gbuf_s[...]), descending=True,
            stores=(_rows_of(dp_ref, 0), _rows_of(dp_ref, 1), _rows_of(dp_ref, 2), _rows_of(dp_ref, 3), _rows_of(dp_ref, 5),
                    _rows_of(dgb_s, cols=slice(0, LW)), _rows_of(dgb_s, cols=slice(LW, 2 * LW)), _rows_of(ub_s)))
        an_s[...] = a_next
        dvn_s[...] = dv_next
        gbuf_s[...] = g_next

        dgb = dgb_s[...]
        du_s[...] += _mm_nt(dgb, wai_ref[0])
        gw_ref[0] += _mm_tn(ub_s[...], dgb)

        def lru_conv_group(gi, du_next):
            g = ng - 1 - gi
            r0 = pl.multiple_of(g * SUBLANES, SUBLANES)
            sl = pl.ds(r0, SUBLANES)
            du = du_s[sl, :]
            xl = xle_s[pl.ds(r0 + SUBLANES, SUBLANES), :]
            xlp = xle_s[sl, :]
            acc_add(PV_LRU_B, du)
            acc_add(PV_LRU_W + 3, du * xl)
            acc_add(PV_LRU_W + 2, du * _shift_down(xl, xlp, 1, row))
            acc_add(PV_LRU_W + 1, du * _shift_down(xl, xlp, 2, row))
            acc_add(PV_LRU_W, du * _shift_down(xl, xlp, 3, row))
            dxl = (((_pvb(pv_ref, PV_LRU_W + 3) * du + _pvb(pv_ref, PV_LRU_W + 2) * _shift_up(du, du_next, 1, row))
                    + _pvb(pv_ref, PV_LRU_W + 1) * _shift_up(du, du_next, 2, row))
                   + _pvb(pv_ref, PV_LRU_W) * _shift_up(du, du_next, 3, row))
            return du, (dxl,)

        dun_s[...] = _for_groups(ng, lru_conv_group, dun_s[...], descending=True, stores=(_rows_of(dp_ref, 4),))

        @pl.when(first_block)
        def _():
            sv_ref[...] = jnp.zeros_like(sv_ref)
            for k in range(N_ACC):
                tot = jnp.sum(acc_s[k], axis=0, keepdims=True)
                if k == PV_LAM:
                    tot = tot / (1.0 + jnp.exp(pv_ref[PV_LAM:PV_LAM + 1, :]))
                sv_ref[k:k + 1, :] = tot

    def part(p):
        return pl.BlockSpec((None, tb, LW), lambda c, i: (2 * p + c // STRIPS_PER_CHUNK, nt - 1 - i, c % STRIPS_PER_CHUNK))

    def halo(p):
        return pl.BlockSpec((None, SUBLANES, LW), lambda c, i: (2 * p + c // STRIPS_PER_CHUNK,
                                                                jnp.maximum((nt - 1 - i) * gpb - 1, 0), c % STRIPS_PER_CHUNK))

    strip = pl.BlockSpec((tb, LW), lambda c, i: (nt - 1 - i, c))
    big = pltpu.VMEM((tb, LW), F32)
    big_e = pltpu.VMEM((tb + SUBLANES, LW), F32)
    small = pltpu.VMEM((SUBLANES, LW), F32)
    outs = pl.pallas_call(
        body, grid=(NS, nt),
        in_specs=[part(p) for p in range(N_PARTS)] + [
            strip, strip, strip, strip, strip, pl.BlockSpec((tb, LW), lambda c, i: (nt - 1 - i, NS + c)),
            halo(1), halo(2), halo(4),
            pl.BlockSpec((SUBLANES, LW), lambda c, i: (jnp.maximum((nt - 1 - i) * gpb - 1, 0), c)),
            pl.BlockSpec((PV_ROWS, LW), lambda c, i: (0, c)),
            pl.BlockSpec((1, LW, 2 * LW), lambda c, i: (c, 0, 0)),
            pl.BlockSpec(memory_space=pl.ANY)],
        out_specs=(pl.BlockSpec((N_PARTS, tb, LW), lambda c, i: (0, nt - 1 - i, c)),
                   pl.BlockSpec((1, LW, 2 * LW), lambda c, i: (c, 0, 0)),
                   pl.BlockSpec((PV_ROWS, LW), lambda c, i: (0, c)),
                   pl.BlockSpec(memory_space=pl.ANY)),
        out_shape=(jax.ShapeDtypeStruct((N_PARTS, t, D_PART), MXU_DTYPE),
                   jax.ShapeDtypeStruct((NS, LW, 2 * LW), F32), jax.ShapeDtypeStruct((PV_ROWS, D_PART), F32),
                   _chip_blocks_shape(sb_out, 1)),
        scratch_shapes=[small, pltpu.VMEM((tb, LW), MXU_DTYPE), big_e, big_e, big_e,
                        pltpu.VMEM((tb, 2 * LW), MXU_DTYPE), big, small,
                        pltpu.VMEM((N_ACC, SUBLANES, LW), F32), small, small, small,
                        pltpu.SemaphoreType.DMA((3,)), pltpu.SemaphoreType.DMA((3,))],
        compiler_params=_cp(ARB, ARB), name="mixer_backward",
    )(proj, proj, proj, proj, proj, proj, h, u, r, ig, dy, dy, proj, proj, proj, h, pvec, wai, sb_out)
    return outs


def _adamw(w, g, m, v):
    m = ADAM_B1 * m + (1.0 - ADAM_B1) * g
    v = ADAM_B2 * v + (1.0 - ADAM_B2) * (g * g)
    m_hat = m / (1.0 - ADAM_B1 ** ADAM_STEP)
    v_hat = v / (1.0 - ADAM_B2 ** ADAM_STEP)
    delta = -ADAM_LR * (m_hat / (jnp.sqrt(v_hat) + ADAM_EPS) + ADAM_WD * w)
    return delta, m, v


def _adam_w_in(w, m, v, g3):
    rows, cols = w.shape
    tr = 128

    def body(w_ref, m_ref, v_ref, g_ref, go_ref, d_ref, mo_ref, vo_ref):
        for s in range(CHUNKS_PER_BLOCK):
            cs = slice(CHUNK * s, CHUNK * (s + 1))
            g = g_ref[s]
            d, mn, vn = _adamw(w_ref[:, cs], g, m_ref[:, cs], v_ref[:, cs])
            go_ref[:, cs] = g
            d_ref[:, cs] = d
            mo_ref[:, cs] = mn
            vo_ref[:, cs] = vn

    blk = pl.BlockSpec((tr, cols), lambda i: (i, 0))
    return pl.pallas_call(
        body, grid=(rows // tr,),
        in_specs=[blk, blk, blk, pl.BlockSpec((CHUNKS_PER_BLOCK, tr, CHUNK), lambda i: (0, i, 0))],
        out_specs=(blk,) * 4, out_shape=(jax.ShapeDtypeStruct(w.shape, F32),) * 4,
        compiler_params=_cp(ARB), name="adam_w_in",
    )(w, m, v, g3)


def _adam_w_out(w, m, v, g):
    rows, cols = w.shape
    tr = 128

    def body(w_ref, m_ref, v_ref, g_ref, d_ref, mo_ref, vo_ref):
        d_ref[...], mo_ref[...], vo_ref[...] = _adamw(w_ref[...], g_ref[...], m_ref[...], v_ref[...])

    blk = pl.BlockSpec((tr, cols), lambda i: (i, 0))
    return pl.pallas_call(
        body, grid=(rows // tr,), in_specs=[blk] * 4, out_specs=(blk,) * 3,
        out_shape=(jax.ShapeDtypeStruct(w.shape, F32),) * 3,
        compiler_params=_cp(ARB), name="adam_w_out",
    )(w, m, v, g)


def _adam_small(ws, ms, vs, gs):
    n = len(ws)

    def body(*refs):
        w_r, m_r, v_r, g_r = refs[0:n], refs[n:2 * n], refs[2 * n:3 * n], refs[3 * n:4 * n]
        d_o, m_o, v_o = refs[4 * n:5 * n], refs[5 * n:6 * n], refs[6 * n:7 * n]
        for j in range(n):
            d_o[j][...], m_o[j][...], v_o[j][...] = _adamw(w_r[j][...], g_r[j][...], m_r[j][...], v_r[j][...])

    vm = pl.BlockSpec(memory_space=pltpu.VMEM)
    shapes = tuple(jax.ShapeDtypeStruct(w.shape, F32) for w in ws)
    outs = pl.pallas_call(
        body, in_specs=[vm] * (4 * n), out_specs=(vm,) * (3 * n), out_shape=shapes * 3,
        compiler_params=_cp(), name="adam_small",
    )(*ws, *ms, *vs, *gs)
    return outs[0:n], outs[n:2 * n], outs[2 * n:3 * n]


def _block_diag_strips(w):
    w4 = w.reshape(NS, HEADS_PER_STRIP, LRU_HEAD, LRU_HEAD)
    rows = [jnp.pad(w4[:, hh], ((0, 0), (0, 0), (LRU_HEAD * hh, LW - LRU_HEAD * (hh + 1)))) for hh in range(HEADS_PER_STRIP)]
    return jnp.concatenate(rows, axis=1)


def _strip_diag_blocks(g):
    g5 = g.reshape(NS, HEADS_PER_STRIP, LRU_HEAD, HEADS_PER_STRIP, LRU_HEAD)
    return jnp.stack([g5[:, hh, :, hh, :] for hh in range(HEADS_PER_STRIP)], axis=1).reshape(NS * HEADS_PER_STRIP, LRU_HEAD, LRU_HEAD)


def kernel(x, ln_g, w_in, conv_w, lru_conv_w, lru_conv_b, w_a, b_a, w_i, b_i, lam, conv_out_g, lru_out_g, w_out, final_g, loss_target, m_ln_g, m_w_in, m_conv_w, m_lru_conv_w, m_lru_conv_b, m_w_a, m_b_a, m_w_i, m_b_i, m_lam, m_conv_out_g, m_lru_out_g, m_w_out, m_final_g, v_ln_g, v_w_in, v_conv_w, v_lru_conv_w, v_lru_conv_b, v_w_a, v_b_a, v_w_i, v_b_i, v_lam, v_conv_out_g, v_lru_out_g, v_w_out, v_final_g):
    xi, yi, ci = lax.axis_index("x"), lax.axis_index("y"), lax.axis_index("c")
    k = 2 * xi + yi
    t = x.shape[1]
    x2 = x.reshape(t, D_MODEL)
    tgt2 = loss_target.reshape(t, D_MODEL)
    row = lambda a: a.reshape(1, -1)

    small = jnp.concatenate([conv_w, lru_conv_w, jnp.zeros((1, conv_w.shape[1]), F32)], axis=0)
    proj, xn, w12, sm4 = _gather_in_projection(x2, row(ln_g), w_in, small)
    convs = jnp.transpose(sm4, (1, 0, 2)).reshape(SUBLANES, D_PART)
    pvec = jnp.concatenate(
        [convs[0:7], row(lru_conv_b), row(b_a), row(b_i), row(lam), row(conv_out_g), row(lru_out_g),
         jnp.zeros((PV_ROWS - N_ACC, D_PART), F32)], axis=0)
    wai = jnp.concatenate([_block_diag_strips(w_a), _block_diag_strips(w_i)], axis=2).astype(MXU_DTYPE)

    c_arr = jnp.reshape(ci, (1,)).astype(jnp.int32)
    kc_arr = jnp.stack([k, ci]).astype(jnp.int32)
    yc, yl, h, u, r, ig, wo4 = _mixer_forward(proj, pvec, wai, w_out)
    wo = wo4.reshape(2 * D_PART, D_MODEL)
    do, dob, dy, st_out = _out_projection_loss(yc, yl, x2, tgt2, wo, row(final_g))
    go4, go4b = _w_out_grad(yc, yl, dob)
    s_out, sb_out = _add_own_half(go4, _exchange_sibling_halves(go4b, "exchange_sibling_halves_out"), c_arr, "add_own_half_out")
    dproj, g_wai, svec, r2o = _mixer_backward(proj, h, u, r, ig, dy, pvec, wai, sb_out)
    gwa = _strip_diag_blocks(g_wai[:, :, 0:LW]).reshape(LRU_HEAD, D_PART)
    gwi = _strip_diag_blocks(g_wai[:, :, LW:2 * LW]).reshape(LRU_HEAD, D_PART)
    g12, g12b, red = _w_in_grad(xn, dproj, jnp.concatenate([svec, st_out, gwa, gwi], axis=0))
    s_in, sb_in = _add_own_half(g12, _exchange_sibling_halves(g12b, "exchange_sibling_halves_in"), c_arr, "add_own_half_in")
    grad_x, st_in, r2i = _input_grad(dproj, w12, x2, do, row(ln_g), sb_in)
    f_in = _sum_chip_blocks(s_in, r2i, kc_arr, CHUNKS_PER_BLOCK, "sum_chip_blocks_in")
    f_out = _sum_chip_blocks(s_out, r2o, kc_arr, 1, "sum_chip_blocks_out")
    f_in, f_out, red_ln = _swap_halves_and_sum(f_in, f_out, st_in)
    r_out = PV_ROWS
    r_wa = PV_ROWS + SUBLANES
    r_wi = r_wa + LRU_HEAD
    loss = red[r_out + 1, 0]

    g_w_in, d_w_in, nm_w_in, nv_w_in = _adam_w_in(w_in, m_w_in, v_w_in, f_in)
    g_w_out = f_out[0]
    d_w_out, nm_w_out, nv_w_out = _adam_w_out(w_out, m_w_out, v_w_out, g_w_out)

    ncol = conv_w.shape[1]
    conv_cols = lax.dynamic_slice(red, (0, k * ncol), (SUBLANES, ncol))
    g_small = {
        "ln_g": red_ln[0], "conv_w": conv_cols[0:3], "lru_conv_w": conv_cols[3:7], "lru_conv_b": red[PV_LRU_B],
        "w_a": red[r_wa:r_wa + LRU_HEAD].reshape(w_a.shape), "b_a": red[PV_BA],
        "w_i": red[r_wi:r_wi + LRU_HEAD].reshape(w_i.shape), "b_i": red[PV_BI], "lam": red[PV_LAM],
        "conv_out_g": red[PV_CG], "lru_out_g": red[PV_LG], "final_g": red[r_out],
    }
    w_small = {"ln_g": ln_g, "conv_w": conv_w, "lru_conv_w": lru_conv_w, "lru_conv_b": lru_conv_b, "w_a": w_a, "b_a": b_a,
               "w_i": w_i, "b_i": b_i, "lam": lam, "conv_out_g": conv_out_g, "lru_out_g": lru_out_g, "final_g": final_g}
    m_small = {"ln_g": m_ln_g, "conv_w": m_conv_w, "lru_conv_w": m_lru_conv_w, "lru_conv_b": m_lru_conv_b, "w_a": m_w_a,
               "b_a": m_b_a, "w_i": m_w_i, "b_i": m_b_i, "lam": m_lam, "conv_out_g": m_conv_out_g,
               "lru_out_g": m_lru_out_g, "final_g": m_final_g}
    v_small = {"ln_g": v_ln_g, "conv_w": v_conv_w, "lru_conv_w": v_lru_conv_w, "lru_conv_b": v_lru_conv_b, "w_a": v_w_a,
               "b_a": v_b_a, "w_i": v_w_i, "b_i": v_b_i, "lam": v_lam, "conv_out_g": v_conv_out_g,
               "lru_out_g": v_lru_out_g, "final_g": v_final_g}
    names = list(w_small)
    as2d = lambda a: a.reshape(1, -1) if a.ndim == 1 else a
    d_s, m_s, v_s = _adam_small([as2d(w_small[n]) for n in names], [as2d(m_small[n]) for n in names],
                                [as2d(v_small[n]) for n in names], [as2d(g_small[n]) for n in names])
    back = lambda n, a: a.reshape(w_small[n].shape)
    grads = {n: g_small[n] for n in names}
    deltas = {n: back(n, a) for n, a in zip(names, d_s)}
    new_m = {n: back(n, a) for n, a in zip(names, m_s)}
    new_v = {n: back(n, a) for n, a in zip(names, v_s)}
    grads["w_in"], deltas["w_in"], new_m["w_in"], new_v["w_in"] = g_w_in, d_w_in, nm_w_in, nv_w_in
    grads["w_out"], deltas["w_out"], new_m["w_out"], new_v["w_out"] = g_w_out, d_w_out, nm_w_out, nv_w_out

    order = ["ln_g", "w_in", "conv_w", "lru_conv_w", "lru_conv_b", "w_a", "b_a", "w_i", "b_i", "lam", "conv_out_g",
             "lru_out_g", "w_out", "final_g"]
    return (loss, grad_x.reshape(x.shape), *[grads[n] for n in order], *[deltas[n] for n in order],
            *[new_m[n] for n in order], *[new_v[n] for n in order])
```

```python
import functools

import jax
import jax.numpy as jnp
from jax import lax
from jax.experimental import pallas as pl
from jax.experimental.pallas import tpu as pltpu

F32 = jnp.float32
MXU_DTYPE = jnp.bfloat16

D_MODEL = 1024
D_PART = 1024
N_PARTS = 6
CHUNK = 512
CHUNKS_PER_BLOCK = 3
N_CHUNKS = 12
N_CHIPS = 4
SUBLANES = 8
LANES = 128
LW = 256
UNROLL = 8
NS = D_PART // LW
STRIPS_PER_CHUNK = CHUNK // LW
CONV_HEAD = 128
LRU_HEAD = 64
HEADS_PER_STRIP = LW // LRU_HEAD
RMS_EPS = 1e-6
RG_LRU_C = 8.0
ADAM_LR = 0.001
ADAM_B1 = 0.9
ADAM_B2 = 0.999
ADAM_EPS = 1e-08
ADAM_WD = 0.01
ADAM_STEP = 10

PV_CONV_W = 0
PV_LRU_W = 3
PV_LRU_B = 7
PV_BA = 8
PV_BI = 9
PV_LAM = 10
PV_CG = 11
PV_LG = 12
PV_ROWS = 16
N_ACC = 13

SLAB = 128
MESH = pl.DeviceIdType.MESH
VMEM_LIMIT = 56 * 1024 * 1024
ARB = "arbitrary"


def _cp(*sem, **kw):
    return pltpu.CompilerParams(dimension_semantics=sem or None, vmem_limit_bytes=VMEM_LIMIT, **kw)


def _mm(a, b):
    return jnp.dot(a, b, preferred_element_type=F32)


def _mm_nt(a, b):
    return lax.dot_general(a, b, (((1,), (1,)), ((), ())), preferred_element_type=F32)


def _mm_tn(a, b):
    return lax.dot_general(a, b, (((0,), (0,)), ((), ())), preferred_element_type=F32)


def _sigmoid(x):
    return 0.5 * jnp.tanh(0.5 * x) + 0.5


def _log_sigmoid(x):
    z = jnp.exp(-jnp.abs(x))
    u = 1.0 + z
    log1p = jnp.where(u == 1.0, z, jnp.log(u) * z / (u - 1.0))
    return jnp.minimum(x, 0.0) - log1p


def _head_mean(z, head):
    out = []
    for k in range(z.shape[1] // LANES):
        zk = z[:, LANES * k:LANES * (k + 1)]
        if head == LANES:
            m = jnp.sum(zk, axis=-1, keepdims=True) * (1.0 / head)
            out.append(jnp.broadcast_to(m, zk.shape))
        else:
            lo = lax.broadcasted_iota(jnp.int32, zk.shape, 1) < head
            s_lo = jnp.sum(jnp.where(lo, zk, 0.0), axis=-1, keepdims=True)
            s_hi = jnp.sum(jnp.where(lo, 0.0, zk), axis=-1, keepdims=True)
            out.append(jnp.where(lo, s_lo, s_hi) * (1.0 / head))
    return jnp.concatenate(out, axis=1)


def _shift_down(cur, prev, d, row):
    return pltpu.roll(jnp.where(row < SUBLANES - d, cur, prev), d, 0)


def _shift_up(cur, nxt, d, row):
    return pltpu.roll(jnp.where(row >= d, cur, nxt), SUBLANES - d, 0)


def _scan8_fwd(a, b, row):
    A, B = a, b
    for d in (1, 2, 4):
        m = row >= d
        a_s = jnp.where(m, pltpu.roll(A, d, 0), 1.0)
        b_s = jnp.where(m, pltpu.roll(B, d, 0), 0.0)
        B = A * b_s + B
        A = A * a_s
    return A, B


def _scan8_rev(a, b, row):
    A, B = a, b
    for d in (1, 2, 4):
        m = row < SUBLANES - d
        a_s = jnp.where(m, pltpu.roll(A, SUBLANES - d, 0), 1.0)
        b_s = jnp.where(m, pltpu.roll(B, SUBLANES - d, 0), 0.0)
        B = A * b_s + B
        A = A * a_s
    return A, B


def _decay(r, lsb):
    la = (RG_LRU_C * r) * lsb
    a = jnp.exp(la)
    e2 = a * a
    em = -jnp.tanh(la) * (1.0 + e2)
    inv_mult = lax.rsqrt(em)
    return a, e2, em * inv_mult, inv_mult


def _mesh_pos():
    x, y, c = lax.axis_index("x"), lax.axis_index("y"), lax.axis_index("c")
    chips = [(1 - x, y), (x, 1 - y), (1 - x, 1 - y)]
    return x, y, c, chips


def _gather_in_projection(x, ln_g, w_in, small):
    t = x.shape[0]
    rb_x = 512
    rb_mm = 2048
    n_mm = t // rb_mm
    half = w_in.shape[0] // 2

    def body(x_hbm, g_ref, wi_ref, sm_ref, proj_hbm, xn_ref, w12_ref, sm4_ref,
             xbuf, obuf, x_sems, o_sems, send_sems, recv_sems):
        x_, y_, c, chips = _mesh_pos()
        k = 2 * x_ + y_
        sib = (x_, y_, 1 - c)
        sm4_ref[k] = sm_ref[...]

        def remote(ref, sem, to):
            return pltpu.make_async_remote_copy(src_ref=ref, dst_ref=ref, send_sem=send_sems.at[sem],
                                                recv_sem=recv_sems.at[sem], device_id=to, device_id_type=MESH)

        def chunk_of(chip, s):
            return CHUNKS_PER_BLOCK * (2 * chip[0] + chip[1]) + s

        def piece(q, core, first=0, rows=half):
            return w12_ref.at[q, pl.ds(pl.multiple_of(half * core + first, SUBLANES * 2), rows), :]

        nbr_x, nbr_y, diagonal = chips
        quarter = half // 2
        DIAG = [(0, 0, half, 0), (1, 0, quarter, 0), (1, quarter, quarter, 1), (2, 0, half, 1)]
        ici = lambda m, s: 2 * s + m
        dgn = lambda j: 6 + j
        to_sib = 10
        sml = lambda m: 20 + m

        sends = []
        for s in range(CHUNKS_PER_BLOCK):
            w12_ref[chunk_of((x_, y_), s)] = wi_ref[:, CHUNK * s:CHUNK * (s + 1)].astype(MXU_DTYPE)
            for m, chip in enumerate((nbr_x, nbr_y)):
                sends.append(remote(piece(chunk_of((x_, y_), s), c), ici(m, s), (*chip, c)))
                sends[-1].start()
        for m, chip in enumerate(chips):
            sends.append(remote(sm4_ref.at[k], sml(m), (*chip, c)))
            sends[-1].start()

        def x_copy(rb, slot):
            return pltpu.make_async_copy(x_hbm.at[pl.ds(rb * rb_x, rb_x), :], xbuf.at[slot], x_sems.at[slot])

        x_copy(0, 0).start()
        for rb in range(t // rb_x):
            slot = rb % 2
            x_copy(rb, slot).wait()
            if rb + 1 < t // rb_x:
                x_copy(rb + 1, 1 - slot).start()

            def norm_slab(sl, carry, rb=rb, slot=slot):
                xf = xbuf[slot, pl.ds(pl.multiple_of(sl * SLAB, SLAB), SLAB), :]
                r = lax.rsqrt(jnp.mean(xf * xf, axis=-1, keepdims=True) + RMS_EPS)
                xn_ref[pl.ds(pl.multiple_of(rb * rb_x + sl * SLAB, SLAB), SLAB), :] = ((xf * r) * g_ref[...]).astype(MXU_DTYPE)
                return carry

            lax.fori_loop(0, rb_x // SLAB, norm_slab, 0)

        def out_copy(q, i):
            return pltpu.make_async_copy(obuf.at[i], proj_hbm.at[q, pl.ds(pl.multiple_of(i * rb_mm, rb_mm), rb_mm), :],
                                         o_sems.at[i])

        def project(q, very_first):
            def row_block(i, carry):
                if not very_first:
                    out_copy(q, i).wait()
                obuf[i] = _mm(xn_ref[pl.ds(pl.multiple_of(i * rb_mm, rb_mm), rb_mm), :], w12_ref[q])
                out_copy(q, i).start()
                return carry

            lax.fori_loop(0, n_mm, row_block, 0)

        for s in range(CHUNKS_PER_BLOCK):
            project(chunk_of((x_, y_), s), very_first=(s == 0))

        steps = []
        for s in range(CHUNKS_PER_BLOCK):
            for m, chip in enumerate((nbr_x, nbr_y)):
                onward = [(first, rows, dgn(j), chips[via]) for j, (cs, first, rows, via) in enumerate(DIAG)
                          if cs == s and via == 1 - m]
                steps.append((chunk_of(chip, s), [(0, half, ici(m, s))], onward))
        for s in range(CHUNKS_PER_BLOCK):
            steps.append((chunk_of(diagonal, s), [(first, rows, dgn(j)) for j, (cs, first, rows, _) in enumerate(DIAG) if cs == s], []))

        def project_when_whole(step):
            q, pieces, _ = step
            for first, rows, sem in pieces:
                remote(piece(q, 1 - c, first, rows), to_sib + sem, sib).wait_recv()
            project(q, very_first=False)

        passed = []
        for j, (q, pieces, onward) in enumerate(steps):
            for first, rows, sem in pieces:
                remote(piece(q, c, first, rows), sem, sib).wait_recv()
            for first, rows, sem, chip in onward:
                passed.append(remote(piece(q, c, first, rows), sem, (*chip, c)))
                passed[-1].start()
            for first, rows, sem in pieces:
                passed.append(remote(piece(q, c, first, rows), to_sib + sem, sib))
                passed[-1].start()
            if j > 0:
                project_when_whole(steps[j - 1])
        project_when_whole(steps[-1])

        for m, chip in enumerate(chips):
            remote(sm4_ref.at[2 * chip[0] + chip[1]], sml(m), sib).wait_recv()
        for cp in sends + passed:
            cp.wait_send()
        for i in range(n_mm):
            out_copy(0, i).wait()

    vm = pl.BlockSpec(memory_space=pltpu.VMEM)
    hbm = pl.BlockSpec(memory_space=pl.ANY)
    n_sems = 23
    return pl.pallas_call(
        body,
        out_shape=(jax.ShapeDtypeStruct((N_CHUNKS, t, CHUNK), F32), jax.ShapeDtypeStruct((t, D_MODEL), MXU_DTYPE),
                   jax.ShapeDtypeStruct((N_CHUNKS, w_in.shape[0], CHUNK), MXU_DTYPE),
                   jax.ShapeDtypeStruct((N_CHIPS,) + small.shape, F32)),
        in_specs=[hbm, vm, vm, vm], out_specs=(hbm, vm, vm, vm),
        scratch_shapes=[pltpu.VMEM((2, rb_x, D_MODEL), F32), pltpu.VMEM((n_mm, rb_mm, CHUNK), F32),
                        pltpu.SemaphoreType.DMA((2,)), pltpu.SemaphoreType.DMA((n_mm,)),
                        pltpu.SemaphoreType.DMA((n_sems,)), pltpu.SemaphoreType.DMA((n_sems,))],
        compiler_params=_cp(), name="gather_in_projection",
    )(x, ln_g, w_in, small)


def _allreduce_behind(step, when, in_ref, acc_s, rbufs, out_ref, send_sems, recv_sems):
    x, y, c, _ = _mesh_pos()
    peers = [(x, y, 1 - c), (1 - x, y, c), (x, 1 - y, c)]

    def exchange(ph):
        return pltpu.make_async_remote_copy(src_ref=acc_s, dst_ref=rbufs[ph], send_sem=send_sems.at[ph],
                                            recv_sem=recv_sems.at[ph], device_id=peers[ph], device_id_type=MESH)

    @pl.when(step == when[0])
    def _():
        acc_s[...] = in_ref[...]
        exchange(0).start()

    for ph in (1, 2):
        @pl.when(step == when[ph])
        def _(ph=ph):
            exchange(ph - 1).wait()
            acc_s[...] = acc_s[...] + rbufs[ph - 1][...]
            exchange(ph).start()

    @pl.when(step == when[3])
    def _():
        exchange(2).wait()
        out_ref[...] = acc_s[...] + rbufs[2][...]


def _add_sibling_halves(g, gb, c_arr, name):
    n, rows, cols = g.shape
    half = rows // 2

    def body(c_ref, g_ref, gb_hbm, o_ref, ob_ref, rbuf, send_sems, recv_sems):
        q = pl.program_id(0)
        x, y, c, _ = _mesh_pos()
        theirs = pl.ds(pl.multiple_of(half * (1 - c), half), half)

        def copy(j):
            return pltpu.make_async_remote_copy(src_ref=gb_hbm.at[j, theirs, :], dst_ref=rbuf.at[j], send_sem=send_sems.at[j],
                                                recv_sem=recv_sems.at[j], device_id=(x, y, 1 - c), device_id_type=MESH)

        @pl.when(q == 0)
        def _():
            for j in range(n):
                copy(j).start()

        copy(q).wait_recv()
        s = g_ref[0] + rbuf[q].astype(F32)
        o_ref[0] = s
        ob_ref[0] = s.astype(jnp.bfloat16)

        @pl.when(q == n - 1)
        def _():
            for j in range(n):
                copy(j).wait_send()

    blk = pl.BlockSpec((1, half, cols), lambda q, c_ref: (q, 0, 0))
    return pl.pallas_call(
        body, out_shape=(jax.ShapeDtypeStruct((n, half, cols), F32), jax.ShapeDtypeStruct((n, half, cols), jnp.bfloat16)),
        grid_spec=pltpu.PrefetchScalarGridSpec(
            num_scalar_prefetch=1, grid=(n,),
            in_specs=[pl.BlockSpec((1, half, cols), lambda q, c_ref: (q, c_ref[0], 0)), pl.BlockSpec(memory_space=pl.ANY)],
            out_specs=(blk, blk),
            scratch_shapes=[pltpu.VMEM((n, half, cols), jnp.bfloat16), pltpu.SemaphoreType.DMA((n,)),
                            pltpu.SemaphoreType.DMA((n,))]),
        compiler_params=_cp(ARB), name=name,
    )(c_arr, g, gb)


def _chip_block_copies(s_ref, r_ref, n_sub, send_sems, recv_sems):
    x, y, c, chips = _mesh_pos()
    cps = []
    for m, chip in enumerate(chips):
        kk = 2 * chip[0] + chip[1]
        cps.append(pltpu.make_async_remote_copy(
            src_ref=s_ref.at[pl.ds(n_sub * kk, n_sub)], dst_ref=r_ref.at[m],
            send_sem=send_sems.at[m], recv_sem=recv_sems.at[m], device_id=(*chip, c), device_id_type=MESH))
    return cps


def _gather_w_out(step, n_steps, wo_ref, wob_s, wo4_ref, local_sem, send_sems, recv_sems):
    x, y, c, chips = _mesh_pos()
    sib = (x, y, 1 - c)
    half = wo_ref.shape[0] // 2

    def rows(core):
        return pl.ds(pl.multiple_of(half * core, half), half)

    def block_half(chip, core):
        return wo4_ref.at[2 * chip[0] + chip[1], rows(core), :]

    def remote(src, dst, sem, to):
        return pltpu.make_async_remote_copy(src_ref=src, dst_ref=dst, send_sem=send_sems.at[sem], recv_sem=recv_sems.at[sem],
                                            device_id=to, device_id_type=MESH)

    local = pltpu.make_async_copy(wob_s, wo4_ref.at[2 * x + y], local_sem)
    ici = [remote(wob_s.at[rows(c), :], block_half((x, y), c), m, (*chip, c)) for m, chip in enumerate(chips)]
    fwd = [remote(block_half(chip, c), block_half(chip, c), 3 + m, sib) for m, chip in enumerate(chips)]

    @pl.when(step == 0)
    def _():
        wob_s[...] = wo_ref[...].astype(MXU_DTYPE)
        local.start()
        for cp in ici:
            cp.start()

    @pl.when(step == n_steps // 2)
    def _():
        for m, chip in enumerate(chips):
            remote(block_half(chip, c), block_half(chip, c), m, sib).wait_recv()
            fwd[m].start()

    @pl.when(step == n_steps - 1)
    def _():
        for m, chip in enumerate(chips):
            remote(block_half(chip, 1 - c), block_half(chip, 1 - c), 3 + m, sib).wait_recv()
        for cp in ici + fwd:
            cp.wait_send()
        local.wait()


def _chip_blocks_shape(s, n_sub):
    return jax.ShapeDtypeStruct((3, n_sub) + s.shape[1:], s.dtype)


def _sum_chip_blocks(s, r, kc_arr, n_sub, name):
    _, rr, cc = s.shape

    def body(kc_ref, s_ref, r_ref, o_ref):
        o_ref[...] = ((s_ref[...] + r_ref[0].astype(F32)) + r_ref[1].astype(F32)) + r_ref[2].astype(F32)

    return pl.pallas_call(
        body, out_shape=jax.ShapeDtypeStruct((n_sub, 2 * rr, cc), F32),
        grid_spec=pltpu.PrefetchScalarGridSpec(
            num_scalar_prefetch=1, grid=(n_sub,),
            in_specs=[pl.BlockSpec((1, rr, cc), lambda q, kc: (n_sub * kc[0] + q, 0, 0)),
                      pl.BlockSpec((3, 1, rr, cc), lambda q, kc: (0, q, 0, 0))],
            out_specs=pl.BlockSpec((1, rr, cc), lambda q, kc: (q, kc[1], 0))),
        compiler_params=_cp(ARB), name=name,
    )(kc_arr, s, r)


def _swap_halves_and_sum(f_in, f_out, v):
    hi, ho = f_in.shape[1] // 2, f_out.shape[1] // 2
    n_dev = 8

    def body(fi_in, fo_in, v_ref, fi_ref, fo_ref, tot_ref, slots, send_sems, recv_sems):
        del fi_in, fo_in
        x, y, c, _ = _mesh_pos()
        sib = (x, y, 1 - c)
        me = 4 * x + 2 * y + c
        slots[me] = v_ref[...]
        si = fi_ref.at[:, pl.ds(pl.multiple_of(hi * c, hi), hi), :]
        so = fo_ref.at[:, pl.ds(pl.multiple_of(ho * c, ho), ho), :]

        def remote(ref, sem, to):
            return pltpu.make_async_remote_copy(src_ref=ref, dst_ref=ref, send_sem=send_sems.at[sem],
                                                recv_sem=recv_sems.at[sem], device_id=to, device_id_type=MESH)

        cps = [remote(si, n_dev - 1, sib), remote(so, n_dev, sib)]
        for d in range(1, n_dev):
            peer = (1 - x if d & 4 else x, 1 - y if d & 2 else y, 1 - c if d & 1 else c)
            cps.append(remote(slots.at[me], d - 1, peer))
        for cp in cps:
            cp.start()
        for cp in cps:
            cp.wait()
        total = slots[0]
        for dev in range(1, n_dev):
            total = total + slots[dev]
        tot_ref[...] = total

    hbm = pl.BlockSpec(memory_space=pl.ANY)
    vm = pl.BlockSpec(memory_space=pltpu.VMEM)
    return pl.pallas_call(
        body,
        out_shape=(jax.ShapeDtypeStruct(f_in.shape, F32), jax.ShapeDtypeStruct(f_out.shape, F32),
                   jax.ShapeDtypeStruct(v.shape, F32)),
        in_specs=[hbm, hbm, vm], out_specs=(hbm, hbm, vm), input_output_aliases={0: 0, 1: 1},
        scratch_shapes=[pltpu.VMEM((n_dev,) + v.shape, F32), pltpu.SemaphoreType.DMA((n_dev + 1,)),
                        pltpu.SemaphoreType.DMA((n_dev + 1,))],
        compiler_params=_cp(), name="swap_halves_and_sum",
    )(f_in, f_out, v)


def _out_projection_loss(yc, yl, x, target, wo, final_g):
    t = x.shape[0]
    tm = 512

    def body(yc_ref, yl_ref, x_ref, t_ref, wo_ref, fg_ref, do_ref, dob_ref, dy_ref, st_ref, y_wo):
        @pl.when(pl.program_id(0) == 0)
        def _():
            st_ref[...] = jnp.zeros_like(st_ref)

        y_wo[...] = _mm(yc_ref[...], wo_ref[0:D_PART, :]) + _mm(yl_ref[...], wo_ref[D_PART:2 * D_PART, :])

        def norm_loss_slab(s, carry):
            g_sum, loss_sum = carry
            rows = pl.ds(pl.multiple_of(s * SLAB, SLAB), SLAB)
            o = x_ref[rows, :] + y_wo[rows, :]
            r2 = lax.rsqrt(jnp.mean(o * o, axis=-1, keepdims=True) + RMS_EPS)
            ohat = o * r2
            fg = fg_ref[...]
            diff = ohat * fg - t_ref[rows, :]
            dout = diff * (1.0 / D_MODEL)
            gp = dout * fg
            do = r2 * (gp - ohat * jnp.mean(gp * ohat, axis=-1, keepdims=True))
            do_ref[rows, :] = do
            dob_ref[rows, :] = do.astype(MXU_DTYPE)
            loss = 0.5 * jnp.sum(jnp.sum(diff * diff, axis=-1, keepdims=True) * (1.0 / D_MODEL), axis=0, keepdims=True)
            return g_sum + jnp.sum(dout * ohat, axis=0, keepdims=True), loss_sum + loss

        g_sum, loss_sum = lax.fori_loop(0, tm // SLAB, norm_loss_slab,
                                        (jnp.zeros((1, D_MODEL), F32), jnp.zeros((1, 1), F32)))
        st_ref[0:1, :] += g_sum
        st_ref[1:2, :] += jnp.broadcast_to(loss_sum, (1, D_MODEL))
        dy_ref[...] = _mm_nt(dob_ref[...], wo_ref[...])

    row = lambda i: (i, 0)
    fix = lambda i: (0, 0)
    return pl.pallas_call(
        body, grid=(t // tm,),
        in_specs=[pl.BlockSpec((tm, D_PART), row), pl.BlockSpec((tm, D_PART), row),
                  pl.BlockSpec((tm, D_MODEL), row), pl.BlockSpec((tm, D_MODEL), row),
                  pl.BlockSpec((2 * D_PART, D_MODEL), fix), pl.BlockSpec((1, D_MODEL), fix)],
        out_specs=(pl.BlockSpec((tm, D_MODEL), row), pl.BlockSpec((tm, D_MODEL), row),
                   pl.BlockSpec((tm, 2 * D_PART), row), pl.BlockSpec((SUBLANES, D_MODEL), fix)),
        out_shape=(jax.ShapeDtypeStruct((t, D_MODEL), F32), jax.ShapeDtypeStruct((t, D_MODEL), MXU_DTYPE),
                   jax.ShapeDtypeStruct((t, 2 * D_PART), F32), jax.ShapeDtypeStruct((SUBLANES, D_MODEL), F32)),
        scratch_shapes=[pltpu.VMEM((tm, D_MODEL), F32)],
        compiler_params=_cp(ARB), name="out_projection_loss",
    )(yc, yl, x, target, wo, final_g)


def _input_grad(dproj, w12, x, do, ln_g, sb_in):
    t = x.shape[0]
    tm = 1024

    def body(dp_ref, w_ref, x_ref, do_ref, g_ref, s_ref, gx_ref, st_ref, r_ref, acc, send_sems, recv_sems):
        i, p = pl.program_id(0), pl.program_id(1)

        @pl.when((i == 0) & (p == 0))
        def _():
            st_ref[...] = jnp.zeros_like(st_ref)
            for cp in _chip_block_copies(s_ref, r_ref, CHUNKS_PER_BLOCK, send_sems, recv_sems):
                cp.start()

        @pl.when((i == t // tm - 1) & (p == N_PARTS - 1))
        def _():
            for cp in _chip_block_copies(s_ref, r_ref, CHUNKS_PER_BLOCK, send_sems, recv_sems):
                cp.wait()

        @pl.when(p == 0)
        def _():
            acc[...] = jnp.zeros_like(acc)

        acc[...] += _mm_nt(dp_ref[0], jnp.concatenate([w_ref[0], w_ref[1]], axis=1))

        @pl.when(p == N_PARTS - 1)
        def _():
            def norm_bwd_slab(s, g_sum):
                rows = pl.ds(pl.multiple_of(s * SLAB, SLAB), SLAB)
                xf = x_ref[rows, :]
                r = lax.rsqrt(jnp.mean(xf * xf, axis=-1, keepdims=True) + RMS_EPS)
                xhat = xf * r
                dxn = acc[rows, :]
                dxh = dxn * g_ref[...]
                gx_ref[rows, :] = do_ref[rows, :] + r * (dxh - xhat * jnp.mean(dxh * xhat, axis=-1, keepdims=True))
                return g_sum + jnp.sum(dxn * xhat, axis=0, keepdims=True)

            st_ref[0:1, :] += lax.fori_loop(0, tm // SLAB, norm_bwd_slab, jnp.zeros((1, D_MODEL), F32))

    row = lambda i, p: (i, 0)
    fix = lambda i, p: (0, 0)
    return pl.pallas_call(
        body, grid=(t // tm, N_PARTS),
        in_specs=[
            pl.BlockSpec((1, tm, D_PART), lambda i, p: (p, i, 0)),
            pl.BlockSpec((2, D_MODEL, CHUNK), lambda i, p: (p, 0, 0)),
            pl.BlockSpec((tm, D_MODEL), row), pl.BlockSpec((tm, D_MODEL), row), pl.BlockSpec((1, D_MODEL), fix),
            pl.BlockSpec(memory_space=pl.ANY)],
        out_specs=(pl.BlockSpec((tm, D_MODEL), row), pl.BlockSpec((SUBLANES, D_MODEL), fix),
                   pl.BlockSpec(memory_space=pl.ANY)),
        out_shape=(jax.ShapeDtypeStruct((t, D_MODEL), F32), jax.ShapeDtypeStruct((SUBLANES, D_MODEL), F32),
                   _chip_blocks_shape(sb_in, CHUNKS_PER_BLOCK)),
        scratch_shapes=[pltpu.VMEM((tm, D_MODEL), F32), pltpu.SemaphoreType.DMA((3,)), pltpu.SemaphoreType.DMA((3,))],
        compiler_params=_cp(ARB, ARB), name="input_grad",
    )(dproj, w12, x, do, ln_g, sb_in)


def _w_in_grad(xn, dproj, small):
    t = xn.shape[0]
    small_shape = pltpu.VMEM(small.shape, F32)

    def body(xn_ref, dp_ref, sm_ref, o_ref, ob_ref, red_ref, acc_s, r0, r1, r2, send_sems, recv_sems):
        _allreduce_behind(pl.program_id(0), (0, 1, 3, N_PARTS - 1), sm_ref, acc_s, (r0, r1, r2), red_ref, send_sems, recv_sems)
        g = _mm_tn(xn_ref[...], dp_ref[0])
        for s in range(2):
            o_ref[s] = g[:, CHUNK * s:CHUNK * (s + 1)]
            ob_ref[s] = g[:, CHUNK * s:CHUNK * (s + 1)].astype(jnp.bfloat16)

    whole = pl.BlockSpec(small.shape, lambda p: (0, 0))
    pair = pl.BlockSpec((2, D_MODEL, CHUNK), lambda p: (p, 0, 0))
    return pl.pallas_call(
        body, grid=(N_PARTS,),
        in_specs=[pl.BlockSpec((t, D_MODEL), lambda p: (0, 0)),
                  pl.BlockSpec((1, t, D_PART), lambda p: (p, 0, 0)), whole],
        out_specs=(pair, pair, whole),
        out_shape=(jax.ShapeDtypeStruct((N_CHUNKS, D_MODEL, CHUNK), F32),
                   jax.ShapeDtypeStruct((N_CHUNKS, D_MODEL, CHUNK), jnp.bfloat16), jax.ShapeDtypeStruct(small.shape, F32)),
        scratch_shapes=[small_shape] * 4 + [pltpu.SemaphoreType.DMA((3,)), pltpu.SemaphoreType.DMA((3,))],
        compiler_params=_cp(ARB), name="w_in_grad",
    )(xn, dproj, small)


def _w_out_grad(yc, yl, dob):
    t = yc.shape[0]
    tk = 2048

    def body(yc_ref, yl_ref, do_ref, o_ref, ob_ref):
        j, kk = pl.program_id(0), pl.program_id(1)

        def accumulate(y_ref):
            @pl.when(kk == 0)
            def _():
                o_ref[...] = jnp.zeros_like(o_ref)

            o_ref[...] += _mm_tn(y_ref[...], do_ref[...])

            @pl.when(kk == t // tk - 1)
            def _():
                ob_ref[...] = o_ref[...].astype(jnp.bfloat16)

        pl.when(j == 0)(functools.partial(accumulate, yc_ref))
        pl.when(j == 1)(functools.partial(accumulate, yl_ref))

    def rows_of(half):
        return lambda j, kk: (jnp.where(j == half, kk, 0), 0)

    half = pl.BlockSpec((D_PART, D_MODEL), lambda j, kk: (j, 0))
    out, out_b = pl.pallas_call(
        body, grid=(2, t // tk),
        in_specs=[pl.BlockSpec((tk, D_PART), rows_of(0)), pl.BlockSpec((tk, D_PART), rows_of(1)),
                  pl.BlockSpec((tk, D_MODEL), lambda j, kk: (kk, 0))],
        out_specs=(half, half),
        out_shape=(jax.ShapeDtypeStruct((2 * D_PART, D_MODEL), F32), jax.ShapeDtypeStruct((2 * D_PART, D_MODEL), jnp.bfloat16)),
        compiler_params=_cp(ARB, ARB), name="w_out_grad",
    )(yc, yl, dob)
    blocks = (N_CHIPS, 2 * D_PART // N_CHIPS, D_MODEL)
    return out.reshape(blocks), out_b.reshape(blocks)


def _for_groups(n, fn, init, unroll=UNROLL, stores=(), descending=False):
    assert unroll % 2 == 0 and n % unroll == 0

    def trip(j, carry):
        held = None
        for uu in range(unroll):
            idx = j * unroll + uu
            carry, values = fn(idx, carry)
            if uu % 2 == 0:
                held = values
                continue
            low_group = n - 1 - idx if descending else idx - 1
            rows = pl.ds(pl.multiple_of(low_group * SUBLANES, 2 * SUBLANES), 2 * SUBLANES)
            pairs = zip(values, held) if descending else zip(held, values)
            for store, (lo, hi) in zip(stores, pairs, strict=True):
                store(rows, jnp.concatenate([lo, hi], axis=0).astype(MXU_DTYPE))
        return carry

    return lax.fori_loop(0, n // unroll, trip, init)


def _rows_of(ref, *lead, cols=slice(None)):
    def store(rows, value):
        ref[(*lead, rows, cols)] = value

    return store


def _pvb(pv_ref, r):
    return jnp.broadcast_to(pv_ref[r:r + 1, :], (SUBLANES, pv_ref.shape[1]))


def _conv3(pv_ref, u, u1, u2):
    return (_pvb(pv_ref, PV_CONV_W) * u2 + _pvb(pv_ref, PV_CONV_W + 1) * u1) + _pvb(pv_ref, PV_CONV_W + 2) * u


def _conv4(pv_ref, v, v1, v2, v3):
    return ((((_pvb(pv_ref, PV_LRU_W) * v3 + _pvb(pv_ref, PV_LRU_W + 1) * v2) + _pvb(pv_ref, PV_LRU_W + 2) * v1)
             + _pvb(pv_ref, PV_LRU_W + 3) * v) + _pvb(pv_ref, PV_LRU_B))


def _mixer_forward(proj, pvec, wai, w_out):
    t = proj.shape[1]
    tb = 1024
    ng = tb // SUBLANES
    nt = t // tb

    def body(bg_ref, cg_ref, xc_ref, gc_ref, xl_ref, gl_ref, pv_ref, wai_ref, wo_ref,
             yc_ref, yl_ref, h_ref, u_s, r_ref, ig_ref, wo4_ref,
             ucp_s, xlp_s, ls_s, hbuf_s, ub_s, gate_s, wob_s, local_sem, send_sems, recv_sems):
        _gather_w_out(pl.program_id(0) * nt + pl.program_id(1), NS * nt, wo_ref, wob_s, wo4_ref, local_sem, send_sems, recv_sems)

        @pl.when(pl.program_id(1) == 0)
        def _():
            ucp_s[...] = jnp.zeros_like(ucp_s)
            xlp_s[...] = jnp.zeros_like(xlp_s)
            hbuf_s[...] = jnp.zeros_like(hbuf_s)

        row = lax.broadcasted_iota(jnp.int32, (SUBLANES, LW), 0)
        ls_s[...] = _log_sigmoid(_pvb(pv_ref, PV_LAM))

        def conv_group(g, carry):
            ucp, xlp = carry
            sl = pl.ds(pl.multiple_of(g * SUBLANES, SUBLANES), SUBLANES)
            uc = cg_ref[sl, :] * xc_ref[sl, :]
            v = _conv3(pv_ref, uc, _shift_down(uc, ucp, 1, row), _shift_down(uc, ucp, 2, row))
            yc = bg_ref[sl, :] * v
            rr = lax.rsqrt(_head_mean(yc * yc, CONV_HEAD) + RMS_EPS)
            gc = gc_ref[sl, :]
            zc = ((yc * rr) * _pvb(pv_ref, PV_CG)) * (gc * _sigmoid(gc))
            xl = xl_ref[sl, :]
            u = _conv4(pv_ref, xl, _shift_down(xl, xlp, 1, row), _shift_down(xl, xlp, 2, row), _shift_down(xl, xlp, 3, row))
            u_s[sl, :] = u
            return (uc, xl), (zc, u)

        ucp, xlp = _for_groups(ng, conv_group, (ucp_s[...], xlp_s[...]), unroll=2 * UNROLL,
                               stores=(_rows_of(yc_ref), _rows_of(ub_s)))
        ucp_s[...] = ucp
        xlp_s[...] = xlp

        gate_s[...] = _mm(ub_s[...], wai_ref[0])

        def lru_group(g, h_before):
            sl = pl.ds(pl.multiple_of(g * SUBLANES, SUBLANES), SUBLANES)
            u = u_s[sl, :]
            r = _sigmoid(gate_s[sl, 0:LW] + _pvb(pv_ref, PV_BA))
            ig = _sigmoid(gate_s[sl, LW:2 * LW] + _pvb(pv_ref, PV_BI))
            r_ref[sl, :] = r
            ig_ref[sl, :] = ig
            a, _, mult, _ = _decay(r, ls_s[...])
            A, B = _scan8_fwd(a, mult * (ig * u), row)
            h = B + A * jnp.broadcast_to(h_before[SUBLANES - 1:SUBLANES, :], (SUBLANES, LW))
            h_ref[sl, :] = h
            rr = lax.rsqrt(_head_mean(h * h, LRU_HEAD) + RMS_EPS)
            gl = gl_ref[sl, :]
            return h, (((h * rr) * _pvb(pv_ref, PV_LG)) * (gl * _sigmoid(gl)),)

        hbuf_s[...] = _for_groups(ng, lru_group, hbuf_s[...], unroll=2 * UNROLL, stores=(_rows_of(yl_ref),))

    def part(p):
        return pl.BlockSpec((None, tb, LW), lambda c, i: (2 * p + c // STRIPS_PER_CHUNK, i, c % STRIPS_PER_CHUNK))

    strip = pl.BlockSpec((tb, LW), lambda c, i: (i, c))
    return pl.pallas_call(
        body, grid=(NS, nt),
        in_specs=[part(p) for p in range(N_PARTS)] + [
            pl.BlockSpec((PV_ROWS, LW), lambda c, i: (0, c)),
            pl.BlockSpec((1, LW, 2 * LW), lambda c, i: (c, 0, 0)),
            pl.BlockSpec(w_out.shape, lambda c, i: (0, 0))],
        out_specs=(strip,) * 6 + (pl.BlockSpec(memory_space=pl.ANY),),
        out_shape=(jax.ShapeDtypeStruct((t, D_PART), MXU_DTYPE),) * 2 + (jax.ShapeDtypeStruct((t, D_PART), F32),) * 4 + (
            jax.ShapeDtypeStruct((N_CHIPS,) + w_out.shape, MXU_DTYPE),),
        scratch_shapes=[pltpu.VMEM((SUBLANES, LW), F32), pltpu.VMEM((SUBLANES, LW), F32), pltpu.VMEM((SUBLANES, LW), F32),
                        pltpu.VMEM((SUBLANES, LW), F32), pltpu.VMEM((tb, LW), MXU_DTYPE),
                        pltpu.VMEM((tb, 2 * LW), F32), pltpu.VMEM(w_out.shape, MXU_DTYPE),
                        pltpu.SemaphoreType.DMA, pltpu.SemaphoreType.DMA((6,)), pltpu.SemaphoreType.DMA((6,))],
        compiler_params=_cp(ARB, ARB), name="mixer_forward",
    )(proj, proj, proj, proj, proj, proj, pvec, wai, w_out)


def _mixer_backward(proj, h, u, r, ig, dy, pvec, wai, sb_out):
    t = proj.shape[1]
    tb = 1024
    ng = tb // SUBLANES
    nt = t // tb
    gpb = tb // SUBLANES

    def body(bg_ref, cg_ref, xc_ref, gc_ref, xl_ref, gl_ref, h_ref, u_ref, r_ref, ig_ref, dyc_ref, dyl_ref,
             cgh_ref, xch_ref, xlh_ref, hh_ref, pv_ref, wai_ref, so_ref,
             dp_ref, gw_ref, sv_ref, ro_ref,
             ls_s, ub_s, uce_s, xle_s, he_s, dgb_s, du_s, gbuf_s,
             acc_s, an_s, dvn_s, dun_s, send_sems, recv_sems):
        i = pl.program_id(1)
        first_block = i == nt - 1

        @pl.when((pl.program_id(0) == 0) & (i == 0))
        def _():
            for cp in _chip_block_copies(so_ref, ro_ref, 1, send_sems, recv_sems):
                cp.start()

        @pl.when((pl.program_id(0) == NS - 1) & (i == nt - 1))
        def _():
            for cp in _chip_block_copies(so_ref, ro_ref, 1, send_sems, recv_sems):
                cp.wait()

        @pl.when(i == 0)
        def _():
            acc_s[...] = jnp.zeros_like(acc_s)
            gw_ref[...] = jnp.zeros_like(gw_ref)
            an_s[...] = jnp.zeros_like(an_s)
            dvn_s[...] = jnp.zeros_like(dvn_s)
            dun_s[...] = jnp.zeros_like(dun_s)
            gbuf_s[...] = jnp.zeros_like(gbuf_s)

        row = lax.broadcasted_iota(jnp.int32, (SUBLANES, LW), 0)
        ls_s[...] = _log_sigmoid(_pvb(pv_ref, PV_LAM))
        keep = jnp.where(first_block, 0.0, 1.0)
        uce_s[0:SUBLANES, :] = (cgh_ref[...] * xch_ref[...]) * keep
        xle_s[0:SUBLANES, :] = xlh_ref[...] * keep
        he_s[0:SUBLANES, :] = hh_ref[...] * keep
        xle_s[SUBLANES:SUBLANES + tb, :] = xl_ref[...]
        he_s[SUBLANES:SUBLANES + tb, :] = h_ref[...]

        uce_s[SUBLANES:SUBLANES + tb, :] = cg_ref[...] * xc_ref[...]

        def acc_add(k, v):
            acc_s[k] += v

        def main_group(gi, carry):
            a_next, dv_next, g_next = carry
            g = ng - 1 - gi
            r0 = pl.multiple_of(g * SUBLANES, SUBLANES)
            sl = pl.ds(r0, SUBLANES)
            sl_e = pl.ds(r0 + SUBLANES, SUBLANES)
            lsb = ls_s[...]
            u = u_ref[sl, :]
            r = r_ref[sl, :]
            ig = ig_ref[sl, :]
            a, e2, mult, inv_mult = _decay(r, lsb)
            gl = gl_ref[sl, :]
            sg = _sigmoid(gl)
            s_l = gl * sg
            h8 = he_s[sl_e, :]
            hprev = _shift_down(h8, he_s[sl, :], 1, row)
            rr = lax.rsqrt(_head_mean(h8 * h8, LRU_HEAD) + RMS_EPS)
            n = h8 * rr
            dz = dyl_ref[sl, :]
            lg = _pvb(pv_ref, PV_LG)
            acc_add(PV_LG, (dz * n) * s_l)
            p5 = ((dz * n) * lg) * (sg * (1.0 + gl * (1.0 - sg)))
            dn = (dz * lg) * s_l
            dh = rr * (dn - n * _head_mean(dn * n, LRU_HEAD))
            A, B = _scan8_rev(_shift_up(a, a_next, 1, row), dh, row)
            gg = B + A * jnp.broadcast_to(g_next[0:1, :], (SUBLANES, LW))
            da = gg * hprev
            iu = ig * u
            diu = gg * mult
            dla = da * a - (gg * iu) * (e2 * inv_mult)
            acc_add(PV_LAM, dla * (RG_LRU_C * r))
            dra = (dla * (RG_LRU_C * lsb)) * (r * (1.0 - r))
            dia = (diu * u) * (ig * (1.0 - ig))
            acc_add(PV_BA, dra)
            acc_add(PV_BI, dia)
            du_s[sl, :] = diu * ig
            bg = bg_ref[sl, :]
            gc = gc_ref[sl, :]
            uc = uce_s[sl_e, :]
            ucp = uce_s[sl, :]
            uc1 = _shift_down(uc, ucp, 1, row)
            uc2 = _shift_down(uc, ucp, 2, row)
            v = _conv3(pv_ref, uc, uc1, uc2)
            yc = bg * v
            rrc = lax.rsqrt(_head_mean(yc * yc, CONV_HEAD) + RMS_EPS)
            nc = yc * rrc
            sgc = _sigmoid(gc)
            s_c = gc * sgc
            dzc = dyc_ref[sl, :]
            cgain = _pvb(pv_ref, PV_CG)
            acc_add(PV_CG, (dzc * nc) * s_c)
            p3 = ((dzc * nc) * cgain) * (sgc * (1.0 + gc * (1.0 - sgc)))
            dnc = (dzc * cgain) * s_c
            dyc = rrc * (dnc - nc * _head_mean(dnc * nc, CONV_HEAD))
            dv = dyc * bg
            duc = (_pvb(pv_ref, PV_CONV_W + 2) * dv + _pvb(pv_ref, PV_CONV_W + 1) * _shift_up(dv, dv_next, 1, row)
                   + _pvb(pv_ref, PV_CONV_W) * _shift_up(dv, dv_next, 2, row))
            acc_add(PV_CONV_W + 2, dv * uc)
            acc_add(PV_CONV_W + 1, dv * uc1)
            acc_add(PV_CONV_W, dv * uc2)
            return (a, dv, gg), (dyc * v, duc * xc_ref[sl, :], duc * cg_ref[sl, :], p3, p5, dra, dia, u)

        a_next, dv_next, g_next = _for_groups(
            ng, main_group, (an_s[...], dvn_s[...], gbuf_s[...]), descending=True,
            stores=(_rows_of(dp_ref, 0), _rows_of(dp_ref, 1), _rows_of(dp_ref, 2), _rows_of(dp_ref, 3), _rows_of(dp_ref, 5),
                    _rows_of(dgb_s, cols=slice(0, LW)), _rows_of(dgb_s, cols=slice(LW, 2 * LW)), _rows_of(ub_s)))
        an_s[...] = a_next
        dvn_s[...] = dv_next
        gbuf_s[...] = g_next

        dgb = dgb_s[...]
        du_s[...] += _mm_nt(dgb, wai_ref[0])
        gw_ref[0] += _mm_tn(ub_s[...], dgb)

        def lru_conv_group(gi, du_next):
            g = ng - 1 - gi
            r0 = pl.multiple_of(g * SUBLANES, SUBLANES)
            sl = pl.ds(r0, SUBLANES)
            du = du_s[sl, :]
            xl = xle_s[pl.ds(r0 + SUBLANES, SUBLANES), :]
            xlp = xle_s[sl, :]
            acc_add(PV_LRU_B, du)
            acc_add(PV_LRU_W + 3, du * xl)
            acc_add(PV_LRU_W + 2, du * _shift_down(xl, xlp, 1, row))
            acc_add(PV_LRU_W + 1, du * _shift_down(xl, xlp, 2, row))
            acc_add(PV_LRU_W, du * _shift_down(xl, xlp, 3, row))
            dxl = (((_pvb(pv_ref, PV_LRU_W + 3) * du + _pvb(pv_ref, PV_LRU_W + 2) * _shift_up(du, du_next, 1, row))
                    + _pvb(pv_ref, PV_LRU_W + 1) * _shift_up(du, du_next, 2, row))
                   + _pvb(pv_ref, PV_LRU_W) * _shift_up(du, du_next, 3, row))
            return du, (dxl,)

        dun_s[...] = _for_groups(ng, lru_conv_group, dun_s[...], descending=True, stores=(_rows_of(dp_ref, 4),))

        @pl.when(first_block)
        def _():
            sv_ref[...] = jnp.zeros_like(sv_ref)
            for k in range(N_ACC):
                tot = jnp.sum(acc_s[k], axis=0, keepdims=True)
                if k == PV_LAM:
                    tot = tot / (1.0 + jnp.exp(pv_ref[PV_LAM:PV_LAM + 1, :]))
                sv_ref[k:k + 1, :] = tot

    def part(p):
        return pl.BlockSpec((None, tb, LW), lambda c, i: (2 * p + c // STRIPS_PER_CHUNK, nt - 1 - i, c % STRIPS_PER_CHUNK))

    def halo(p):
        return pl.BlockSpec((None, SUBLANES, LW), lambda c, i: (2 * p + c // STRIPS_PER_CHUNK,
                                                                jnp.maximum((nt - 1 - i) * gpb - 1, 0), c % STRIPS_PER_CHUNK))

    strip = pl.BlockSpec((tb, LW), lambda c, i: (nt - 1 - i, c))
    big = pltpu.VMEM((tb, LW), F32)
    big_e = pltpu.VMEM((tb + SUBLANES, LW), F32)
    small = pltpu.VMEM((SUBLANES, LW), F32)
    outs = pl.pallas_call(
        body, grid=(NS, nt),
        in_specs=[part(p) for p in range(N_PARTS)] + [
            strip, strip, strip, strip, strip, pl.BlockSpec((tb, LW), lambda c, i: (nt - 1 - i, NS + c)),
            halo(1), halo(2), halo(4),
            pl.BlockSpec((SUBLANES, LW), lambda c, i: (jnp.maximum((nt - 1 - i) * gpb - 1, 0), c)),
            pl.BlockSpec((PV_ROWS, LW), lambda c, i: (0, c)),
            pl.BlockSpec((1, LW, 2 * LW), lambda c, i: (c, 0, 0)),
            pl.BlockSpec(memory_space=pl.ANY)],
        out_specs=(pl.BlockSpec((N_PARTS, tb, LW), lambda c, i: (0, nt - 1 - i, c)),
                   pl.BlockSpec((1, LW, 2 * LW), lambda c, i: (c, 0, 0)),
                   pl.BlockSpec((PV_ROWS, LW), lambda c, i: (0, c)),
                   pl.BlockSpec(memory_space=pl.ANY)),
        out_shape=(jax.ShapeDtypeStruct((N_PARTS, t, D_PART), MXU_DTYPE),
                   jax.ShapeDtypeStruct((NS, LW, 2 * LW), F32), jax.ShapeDtypeStruct((PV_ROWS, D_PART), F32),
                   _chip_blocks_shape(sb_out, 1)),
        scratch_shapes=[small, pltpu.VMEM((tb, LW), MXU_DTYPE), big_e, big_e, big_e,
                        pltpu.VMEM((tb, 2 * LW), MXU_DTYPE), big, small,
                        pltpu.VMEM((N_ACC, SUBLANES, LW), F32), small, small, small,
                        pltpu.SemaphoreType.DMA((3,)), pltpu.SemaphoreType.DMA((3,))],
        compiler_params=_cp(ARB, ARB), name="mixer_backward",
    )(proj, proj, proj, proj, proj, proj, h, u, r, ig, dy, dy, proj, proj, proj, h, pvec, wai, sb_out)
    return outs


def _adamw(w, g, m, v):
    m = ADAM_B1 * m + (1.0 - ADAM_B1) * g
    v = ADAM_B2 * v + (1.0 - ADAM_B2) * (g * g)
    m_hat = m / (1.0 - ADAM_B1 ** ADAM_STEP)
    v_hat = v / (1.0 - ADAM_B2 ** ADAM_STEP)
    delta = -ADAM_LR * (m_hat / (jnp.sqrt(v_hat) + ADAM_EPS) + ADAM_WD * w)
    return delta, m, v


def _adam_w_in(w, m, v, g3):
    rows, cols = w.shape
    tr = 128

    def body(w_ref, m_ref, v_ref, g_ref, go_ref, d_ref, mo_ref, vo_ref):
        for s in range(CHUNKS_PER_BLOCK):
            cs = slice(CHUNK * s, CHUNK * (s + 1))
            g = g_ref[s]
            d, mn, vn = _adamw(w_ref[:, cs], g, m_ref[:, cs], v_ref[:, cs])
            go_ref[:, cs] = g
            d_ref[:, cs] = d
            mo_ref[:, cs] = mn
            vo_ref[:, cs] = vn

    blk = pl.BlockSpec((tr, cols), lambda i: (i, 0))
    return pl.pallas_call(
        body, grid=(rows // tr,),
        in_specs=[blk, blk, blk, pl.BlockSpec((CHUNKS_PER_BLOCK, tr, CHUNK), lambda i: (0, i, 0))],
        out_specs=(blk,) * 4, out_shape=(jax.ShapeDtypeStruct(w.shape, F32),) * 4,
        compiler_params=_cp(ARB), name="adam_w_in",
    )(w, m, v, g3)


def _adam_w_out(w, m, v, g):
    rows, cols = w.shape
    tr = 128

    def body(w_ref, m_ref, v_ref, g_ref, d_ref, mo_ref, vo_ref):
        d_ref[...], mo_ref[...], vo_ref[...] = _adamw(w_ref[...], g_ref[...], m_ref[...], v_ref[...])

    blk = pl.BlockSpec((tr, cols), lambda i: (i, 0))
    return pl.pallas_call(
        body, grid=(rows // tr,), in_specs=[blk] * 4, out_specs=(blk,) * 3,
        out_shape=(jax.ShapeDtypeStruct(w.shape, F32),) * 3,
        compiler_params=_cp(ARB), name="adam_w_out",
    )(w, m, v, g)


def _adam_small(ws, ms, vs, gs):
    n = len(ws)

    def body(*refs):
        w_r, m_r, v_r, g_r = refs[0:n], refs[n:2 * n], refs[2 * n:3 * n], refs[3 * n:4 * n]
        d_o, m_o, v_o = refs[4 * n:5 * n], refs[5 * n:6 * n], refs[6 * n:7 * n]
        for j in range(n):
            d_o[j][...], m_o[j][...], v_o[j][...] = _adamw(w_r[j][...], g_r[j][...], m_r[j][...], v_r[j][...])

    vm = pl.BlockSpec(memory_space=pltpu.VMEM)
    shapes = tuple(jax.ShapeDtypeStruct(w.shape, F32) for w in ws)
    outs = pl.pallas_call(
        body, in_specs=[vm] * (4 * n), out_specs=(vm,) * (3 * n), out_shape=shapes * 3,
        compiler_params=_cp(), name="adam_small",
    )(*ws, *ms, *vs, *gs)
    return outs[0:n], outs[n:2 * n], outs[2 * n:3 * n]


def _block_diag_strips(w):
    w4 = w.reshape(NS, HEADS_PER_STRIP, LRU_HEAD, LRU_HEAD)
    rows = [jnp.pad(w4[:, hh], ((0, 0), (0, 0), (LRU_HEAD * hh, LW - LRU_HEAD * (hh + 1)))) for hh in range(HEADS_PER_STRIP)]
    return jnp.concatenate(rows, axis=1)


def _strip_diag_blocks(g):
    g5 = g.reshape(NS, HEADS_PER_STRIP, LRU_HEAD, HEADS_PER_STRIP, LRU_HEAD)
    return jnp.stack([g5[:, hh, :, hh, :] for hh in range(HEADS_PER_STRIP)], axis=1).reshape(NS * HEADS_PER_STRIP, LRU_HEAD, LRU_HEAD)


def kernel(x, ln_g, w_in, conv_w, lru_conv_w, lru_conv_b, w_a, b_a, w_i, b_i, lam, conv_out_g, lru_out_g, w_out, final_g, loss_target, m_ln_g, m_w_in, m_conv_w, m_lru_conv_w, m_lru_conv_b, m_w_a, m_b_a, m_w_i, m_b_i, m_lam, m_conv_out_g, m_lru_out_g, m_w_out, m_final_g, v_ln_g, v_w_in, v_conv_w, v_lru_conv_w, v_lru_conv_b, v_w_a, v_b_a, v_w_i, v_b_i, v_lam, v_conv_out_g, v_lru_out_g, v_w_out, v_final_g):
    xi, yi, ci = lax.axis_index("x"), lax.axis_index("y"), lax.axis_index("c")
    k = 2 * xi + yi
    t = x.shape[1]
    x2 = x.reshape(t, D_MODEL)
    tgt2 = loss_target.reshape(t, D_MODEL)
    row = lambda a: a.reshape(1, -1)

    small = jnp.concatenate([conv_w, lru_conv_w, jnp.zeros((1, conv_w.shape[1]), F32)], axis=0)
    proj, xn, w12, sm4 = _gather_in_projection(x2, row(ln_g), w_in, small)
    convs = jnp.transpose(sm4, (1, 0, 2)).reshape(SUBLANES, D_PART)
    pvec = jnp.concatenate(
        [convs[0:7], row(lru_conv_b), row(b_a), row(b_i), row(lam), row(conv_out_g), row(lru_out_g),
         jnp.zeros((PV_ROWS - N_ACC, D_PART), F32)], axis=0)
    wai = jnp.concatenate([_block_diag_strips(w_a), _block_diag_strips(w_i)], axis=2).astype(MXU_DTYPE)

    c_arr = jnp.reshape(ci, (1,)).astype(jnp.int32)
    kc_arr = jnp.stack([k, ci]).astype(jnp.int32)
    yc, yl, h, u, r, ig, wo4 = _mixer_forward(proj, pvec, wai, w_out)
    wo = wo4.reshape(2 * D_PART, D_MODEL)
    do, dob, dy, st_out = _out_projection_loss(yc, yl, x2, tgt2, wo, row(final_g))
    go4, go4b = _w_out_grad(yc, yl, dob)
    s_out, sb_out = _add_sibling_halves(go4, go4b, c_arr, "add_sibling_halves_out")
    dproj, g_wai, svec, r2o = _mixer_backward(proj, h, u, r, ig, dy, pvec, wai, sb_out)
    gwa = _strip_diag_blocks(g_wai[:, :, 0:LW]).reshape(LRU_HEAD, D_PART)
    gwi = _strip_diag_blocks(g_wai[:, :, LW:2 * LW]).reshape(LRU_HEAD, D_PART)
    g12, g12b, red = _w_in_grad(xn, dproj, jnp.concatenate([svec, st_out, gwa, gwi], axis=0))
    s_in, sb_in = _add_sibling_halves(g12, g12b, c_arr, "add_sibling_halves_in")
    grad_x, st_in, r2i = _input_grad(dproj, w12, x2, do, row(ln_g), sb_in)
    f_in = _sum_chip_blocks(s_in, r2i, kc_arr, CHUNKS_PER_BLOCK, "sum_chip_blocks_in")
    f_out = _sum_chip_blocks(s_out, r2o, kc_arr, 1, "sum_chip_blocks_out")
    f_in, f_out, red_ln = _swap_halves_and_sum(f_in, f_out, st_in)
    r_out = PV_ROWS
    r_wa = PV_ROWS + SUBLANES
    r_wi = r_wa + LRU_HEAD
    loss = red[r_out + 1, 0]

    g_w_in, d_w_in, nm_w_in, nv_w_in = _adam_w_in(w_in, m_w_in, v_w_in, f_in)
    g_w_out = f_out[0]
    d_w_out, nm_w_out, nv_w_out = _adam_w_out(w_out, m_w_out, v_w_out, g_w_out)

    ncol = conv_w.shape[1]
    conv_cols = lax.dynamic_slice(red, (0, k * ncol), (SUBLANES, ncol))
    g_small = {
        "ln_g": red_ln[0], "conv_w": conv_cols[0:3], "lru_conv_w": conv_cols[3:7], "lru_conv_b": red[PV_LRU_B],
        "w_a": red[r_wa:r_wa + LRU_HEAD].reshape(w_a.shape), "b_a": red[PV_BA],
        "w_i": red[r_wi:r_wi + LRU_HEAD].reshape(w_i.shape), "b_i": red[PV_BI], "lam": red[PV_LAM],
        "conv_out_g": red[PV_CG], "lru_out_g": red[PV_LG], "final_g": red[r_out],
    }
    w_small = {"ln_g": ln_g, "conv_w": conv_w, "lru_conv_w": lru_conv_w, "lru_conv_b": lru_conv_b, "w_a": w_a, "b_a": b_a,
               "w_i": w_i, "b_i": b_i, "lam": lam, "conv_out_g": conv_out_g, "lru_out_g": lru_out_g, "final_g": final_g}
    m_small = {"ln_g": m_ln_g, "conv_w": m_conv_w, "lru_conv_w": m_lru_conv_w, "lru_conv_b": m_lru_conv_b, "w_a": m_w_a,
               "b_a": m_b_a, "w_i": m_w_i, "b_i": m_b_i, "lam": m_lam, "conv_out_g": m_conv_out_g,
               "lru_out_g": m_lru_out_g, "final_g": m_final_g}
    v_small = {"ln_g": v_ln_g, "conv_w": v_conv_w, "lru_conv_w": v_lru_conv_w, "lru_conv_b": v_lru_conv_b, "w_a": v_w_a,
               "b_a": v_b_a, "w_i": v_w_i, "b_i": v_b_i, "lam": v_lam, "conv_out_g": v_conv_out_g,
               "lru_out_g": v_lru_out_g, "final_g": v_final_g}
    names = list(w_small)
    as2d = lambda a: a.reshape(1, -1) if a.ndim == 1 else a
    d_s, m_s, v_s = _adam_small([as2d(w_small[n]) for n in names], [as2d(m_small[n]) for n in names],
                                [as2d(v_small[n]) for n in names], [as2d(g_small[n]) for n in names])
    back = lambda n, a: a.reshape(w_small[n].shape)
    grads = {n: g_small[n] for n in names}
    deltas = {n: back(n, a) for n, a in zip(names, d_s)}
    new_m = {n: back(n, a) for n, a in zip(names, m_s)}
    new_v = {n: back(n, a) for n, a in zip(names, v_s)}
    grads["w_in"], deltas["w_in"], new_m["w_in"], new_v["w_in"] = g_w_in, d_w_in, nm_w_in, nv_w_in
    grads["w_out"], deltas["w_out"], new_m["w_out"], new_v["w_out"] = g_w_out, d_w_out, nm_w_out, nv_w_out

    order = ["ln_g", "w_in", "conv_w", "lru_conv_w", "lru_conv_b", "w_a", "b_a", "w_i", "b_i", "lam", "conv_out_g",
             "lru_out_g", "w_out", "final_g"]
    return (loss, grad_x.reshape(x.shape), *[grads[n] for n in order], *[deltas[n] for n in order],
            *[new_m[n] for n in order], *[new_v[n] for n in order])
```

```python
import functools

import jax
import jax.numpy as jnp
from jax import lax
from jax.experimental import pallas as pl
from jax.experimental.pallas import tpu as pltpu

F32 = jnp.float32
MXU_DTYPE = jnp.bfloat16

D_MODEL = 1024
D_PART = 1024
N_PARTS = 6
CHUNK = 512
CHUNKS_PER_BLOCK = 3
N_CHUNKS = 12
N_CHIPS = 4
SUBLANES = 8
LANES = 128
LW = 256
UNROLL = 8
NS = D_PART // LW
STRIPS_PER_CHUNK = CHUNK // LW
CONV_HEAD = 128
LRU_HEAD = 64
HEADS_PER_STRIP = LW // LRU_HEAD
RMS_EPS = 1e-6
RG_LRU_C = 8.0
ADAM_LR = 0.001
ADAM_B1 = 0.9
ADAM_B2 = 0.999
ADAM_EPS = 1e-08
ADAM_WD = 0.01
ADAM_STEP = 10

PV_CONV_W = 0
PV_LRU_W = 3
PV_LRU_B = 7
PV_BA = 8
PV_BI = 9
PV_LAM = 10
PV_CG = 11
PV_LG = 12
PV_ROWS = 16
N_ACC = 13

SLAB = 128
MESH = pl.DeviceIdType.MESH
VMEM_LIMIT = 56 * 1024 * 1024
ARB = "arbitrary"


def _cp(*sem, **kw):
    return pltpu.CompilerParams(dimension_semantics=sem or None, vmem_limit_bytes=VMEM_LIMIT, **kw)


def _mm(a, b):
    return jnp.dot(a, b, preferred_element_type=F32)


def _mm_nt(a, b):
    return lax.dot_general(a, b, (((1,), (1,)), ((), ())), preferred_element_type=F32)


def _mm_tn(a, b):
    return lax.dot_general(a, b, (((0,), (0,)), ((), ())), preferred_element_type=F32)


def _sigmoid(x):
    return 0.5 * jnp.tanh(0.5 * x) + 0.5


def _log_sigmoid(x):
    z = jnp.exp(-jnp.abs(x))
    u = 1.0 + z
    log1p = jnp.where(u == 1.0, z, jnp.log(u) * z / (u - 1.0))
    return jnp.minimum(x, 0.0) - log1p


def _head_mean(z, head):
    out = []
    for k in range(z.shape[1] // LANES):
        zk = z[:, LANES * k:LANES * (k + 1)]
        if head == LANES:
            m = jnp.sum(zk, axis=-1, keepdims=True) * (1.0 / head)
            out.append(jnp.broadcast_to(m, zk.shape))
        else:
            lo = lax.broadcasted_iota(jnp.int32, zk.shape, 1) < head
            s_lo = jnp.sum(jnp.where(lo, zk, 0.0), axis=-1, keepdims=True)
            s_hi = jnp.sum(jnp.where(lo, 0.0, zk), axis=-1, keepdims=True)
            out.append(jnp.where(lo, s_lo, s_hi) * (1.0 / head))
    return jnp.concatenate(out, axis=1)


def _shift_down(cur, prev, d, row):
    return pltpu.roll(jnp.where(row < SUBLANES - d, cur, prev), d, 0)


def _shift_up(cur, nxt, d, row):
    return pltpu.roll(jnp.where(row >= d, cur, nxt), SUBLANES - d, 0)


def _scan8_fwd(a, b, row):
    A, B = a, b
    for d in (1, 2, 4):
        m = row >= d
        a_s = jnp.where(m, pltpu.roll(A, d, 0), 1.0)
        b_s = jnp.where(m, pltpu.roll(B, d, 0), 0.0)
        B = A * b_s + B
        A = A * a_s
    return A, B


def _scan8_rev(a, b, row):
    A, B = a, b
    for d in (1, 2, 4):
        m = row < SUBLANES - d
        a_s = jnp.where(m, pltpu.roll(A, SUBLANES - d, 0), 1.0)
        b_s = jnp.where(m, pltpu.roll(B, SUBLANES - d, 0), 0.0)
        B = A * b_s + B
        A = A * a_s
    return A, B


def _decay(r, ls8):
    la = r * ls8
    a = jnp.exp(la)
    e2 = a * a
    em = -jnp.tanh(la) * (1.0 + e2)
    inv_mult = lax.rsqrt(em)
    return a, e2, em * inv_mult, inv_mult


def _mesh_pos():
    x, y, c = lax.axis_index("x"), lax.axis_index("y"), lax.axis_index("c")
    chips = [(1 - x, y), (x, 1 - y), (1 - x, 1 - y)]
    return x, y, c, chips


def _gather_in_projection(x, ln_g, w_in, small):
    t = x.shape[0]
    rb_x = 512
    rb_mm = 2048
    n_mm = t // rb_mm
    half = w_in.shape[0] // 2

    def body(x_hbm, g_ref, wi_ref, sm_ref, proj_hbm, xn_ref, w12_ref, sm4_ref,
             xbuf, obuf, x_sems, o_sems, send_sems, recv_sems):
        x_, y_, c, chips = _mesh_pos()
        k = 2 * x_ + y_
        sib = (x_, y_, 1 - c)
        sm4_ref[k] = sm_ref[...]

        def remote(ref, sem, to):
            return pltpu.make_async_remote_copy(src_ref=ref, dst_ref=ref, send_sem=send_sems.at[sem],
                                                recv_sem=recv_sems.at[sem], device_id=to, device_id_type=MESH)

        def chunk_of(chip, s):
            return CHUNKS_PER_BLOCK * (2 * chip[0] + chip[1]) + s

        def piece(q, core, first=0, rows=half):
            return w12_ref.at[q, pl.ds(pl.multiple_of(half * core + first, SUBLANES * 2), rows), :]

        nbr_x, nbr_y, diagonal = chips
        quarter = half // 2
        DIAG = [(0, 0, half, 0), (1, 0, quarter, 0), (1, quarter, quarter, 1), (2, 0, half, 1)]
        ici = lambda m, s: 2 * s + m
        dgn = lambda j: 6 + j
        to_sib = 10
        sml = lambda m: 20 + m

        sends = []
        for s in range(CHUNKS_PER_BLOCK):
            w12_ref[chunk_of((x_, y_), s)] = wi_ref[:, CHUNK * s:CHUNK * (s + 1)].astype(MXU_DTYPE)
            for m, chip in enumerate((nbr_x, nbr_y)):
                sends.append(remote(piece(chunk_of((x_, y_), s), c), ici(m, s), (*chip, c)))
                sends[-1].start()
        for m, chip in enumerate(chips):
            sends.append(remote(sm4_ref.at[k], sml(m), (*chip, c)))
            sends[-1].start()

        def x_copy(rb, slot):
            return pltpu.make_async_copy(x_hbm.at[pl.ds(rb * rb_x, rb_x), :], xbuf.at[slot], x_sems.at[slot])

        x_copy(0, 0).start()
        for rb in range(t // rb_x):
            slot = rb % 2
            x_copy(rb, slot).wait()
            if rb + 1 < t // rb_x:
                x_copy(rb + 1, 1 - slot).start()

            def norm_slab(sl, carry, rb=rb, slot=slot):
                xf = xbuf[slot, pl.ds(pl.multiple_of(sl * SLAB, SLAB), SLAB), :]
                r = lax.rsqrt(jnp.mean(xf * xf, axis=-1, keepdims=True) + RMS_EPS)
                xn_ref[pl.ds(pl.multiple_of(rb * rb_x + sl * SLAB, SLAB), SLAB), :] = ((xf * r) * g_ref[...]).astype(MXU_DTYPE)
                return carry

            lax.fori_loop(0, rb_x // SLAB, norm_slab, 0)

        def out_copy(q, i):
            return pltpu.make_async_copy(obuf.at[i], proj_hbm.at[q, pl.ds(pl.multiple_of(i * rb_mm, rb_mm), rb_mm), :],
                                         o_sems.at[i])

        def project(q, very_first):
            def row_block(i, carry):
                if not very_first:
                    out_copy(q, i).wait()
                obuf[i] = _mm(xn_ref[pl.ds(pl.multiple_of(i * rb_mm, rb_mm), rb_mm), :], w12_ref[q])
                out_copy(q, i).start()
                return carry

            lax.fori_loop(0, n_mm, row_block, 0)

        for s in range(CHUNKS_PER_BLOCK):
            project(chunk_of((x_, y_), s), very_first=(s == 0))

        steps = []
        for s in range(CHUNKS_PER_BLOCK):
            for m, chip in enumerate((nbr_x, nbr_y)):
                onward = [(first, rows, dgn(j), chips[via]) for j, (cs, first, rows, via) in enumerate(DIAG)
                          if cs == s and via == 1 - m]
                steps.append((chunk_of(chip, s), [(0, half, ici(m, s))], onward))
        for s in range(CHUNKS_PER_BLOCK):
            steps.append((chunk_of(diagonal, s), [(first, rows, dgn(j)) for j, (cs, first, rows, _) in enumerate(DIAG) if cs == s], []))

        def project_when_whole(step):
            q, pieces, _ = step
            for first, rows, sem in pieces:
                remote(piece(q, 1 - c, first, rows), to_sib + sem, sib).wait_recv()
            project(q, very_first=False)

        passed = []
        for j, (q, pieces, onward) in enumerate(steps):
            for first, rows, sem in pieces:
                remote(piece(q, c, first, rows), sem, sib).wait_recv()
            for first, rows, sem, chip in onward:
                passed.append(remote(piece(q, c, first, rows), sem, (*chip, c)))
                passed[-1].start()
            for first, rows, sem in pieces:
                passed.append(remote(piece(q, c, first, rows), to_sib + sem, sib))
                passed[-1].start()
            if j > 0:
                project_when_whole(steps[j - 1])
        project_when_whole(steps[-1])

        for m, chip in enumerate(chips):
            remote(sm4_ref.at[2 * chip[0] + chip[1]], sml(m), sib).wait_recv()
        for cp in sends + passed:
            cp.wait_send()
        for i in range(n_mm):
            out_copy(0, i).wait()

    vm = pl.BlockSpec(memory_space=pltpu.VMEM)
    hbm = pl.BlockSpec(memory_space=pl.ANY)
    n_sems = 23
    return pl.pallas_call(
        body,
        out_shape=(jax.ShapeDtypeStruct((N_CHUNKS, t, CHUNK), F32), jax.ShapeDtypeStruct((t, D_MODEL), MXU_DTYPE),
                   jax.ShapeDtypeStruct((N_CHUNKS, w_in.shape[0], CHUNK), MXU_DTYPE),
                   jax.ShapeDtypeStruct((N_CHIPS,) + small.shape, F32)),
        in_specs=[hbm, vm, vm, vm], out_specs=(hbm, vm, vm, vm),
        scratch_shapes=[pltpu.VMEM((2, rb_x, D_MODEL), F32), pltpu.VMEM((n_mm, rb_mm, CHUNK), F32),
                        pltpu.SemaphoreType.DMA((2,)), pltpu.SemaphoreType.DMA((n_mm,)),
                        pltpu.SemaphoreType.DMA((n_sems,)), pltpu.SemaphoreType.DMA((n_sems,))],
        compiler_params=_cp(), name="gather_in_projection",
    )(x, ln_g, w_in, small)


def _allreduce_behind(step, when, in_ref, acc_s, rbufs, out_ref, send_sems, recv_sems):
    x, y, c, _ = _mesh_pos()
    peers = [(x, y, 1 - c), (1 - x, y, c), (x, 1 - y, c)]

    def exchange(ph):
        return pltpu.make_async_remote_copy(src_ref=acc_s, dst_ref=rbufs[ph], send_sem=send_sems.at[ph],
                                            recv_sem=recv_sems.at[ph], device_id=peers[ph], device_id_type=MESH)

    @pl.when(step == when[0])
    def _():
        acc_s[...] = in_ref[...]
        exchange(0).start()

    for ph in (1, 2):
        @pl.when(step == when[ph])
        def _(ph=ph):
            exchange(ph - 1).wait()
            acc_s[...] = acc_s[...] + rbufs[ph - 1][...]
            exchange(ph).start()

    @pl.when(step == when[3])
    def _():
        exchange(2).wait()
        out_ref[...] = acc_s[...] + rbufs[2][...]


def _add_sibling_halves(g, gb, c_arr, name):
    n, rows, cols = g.shape
    half = rows // 2

    def body(c_ref, g_ref, gb_hbm, o_ref, ob_ref, rbuf, send_sems, recv_sems):
        q = pl.program_id(0)
        x, y, c, _ = _mesh_pos()
        theirs = pl.ds(pl.multiple_of(half * (1 - c), half), half)

        def copy(j):
            return pltpu.make_async_remote_copy(src_ref=gb_hbm.at[j, theirs, :], dst_ref=rbuf.at[j], send_sem=send_sems.at[j],
                                                recv_sem=recv_sems.at[j], device_id=(x, y, 1 - c), device_id_type=MESH)

        @pl.when(q == 0)
        def _():
            for j in range(n):
                copy(j).start()

        copy(q).wait_recv()
        s = g_ref[0] + rbuf[q].astype(F32)
        o_ref[0] = s
        ob_ref[0] = s.astype(jnp.bfloat16)

        @pl.when(q == n - 1)
        def _():
            for j in range(n):
                copy(j).wait_send()

    blk = pl.BlockSpec((1, half, cols), lambda q, c_ref: (q, 0, 0))
    return pl.pallas_call(
        body, out_shape=(jax.ShapeDtypeStruct((n, half, cols), F32), jax.ShapeDtypeStruct((n, half, cols), jnp.bfloat16)),
        grid_spec=pltpu.PrefetchScalarGridSpec(
            num_scalar_prefetch=1, grid=(n,),
            in_specs=[pl.BlockSpec((1, half, cols), lambda q, c_ref: (q, c_ref[0], 0)), pl.BlockSpec(memory_space=pl.ANY)],
            out_specs=(blk, blk),
            scratch_shapes=[pltpu.VMEM((n, half, cols), jnp.bfloat16), pltpu.SemaphoreType.DMA((n,)),
                            pltpu.SemaphoreType.DMA((n,))]),
        compiler_params=_cp(ARB), name=name,
    )(c_arr, g, gb)


def _chip_block_copies(s_ref, r_ref, n_sub, send_sems, recv_sems):
    x, y, c, chips = _mesh_pos()
    cps = []
    for m, chip in enumerate(chips):
        kk = 2 * chip[0] + chip[1]
        cps.append(pltpu.make_async_remote_copy(
            src_ref=s_ref.at[pl.ds(n_sub * kk, n_sub)], dst_ref=r_ref.at[m],
            send_sem=send_sems.at[m], recv_sem=recv_sems.at[m], device_id=(*chip, c), device_id_type=MESH))
    return cps


def _gather_w_out(step, n_steps, wo_ref, wob_s, wo4_ref, local_sem, send_sems, recv_sems):
    x, y, c, chips = _mesh_pos()
    sib = (x, y, 1 - c)
    half = wo_ref.shape[0] // 2

    def rows(core):
        return pl.ds(pl.multiple_of(half * core, half), half)

    def block_half(chip, core):
        return wo4_ref.at[2 * chip[0] + chip[1], rows(core), :]

    def remote(src, dst, sem, to):
        return pltpu.make_async_remote_copy(src_ref=src, dst_ref=dst, send_sem=send_sems.at[sem], recv_sem=recv_sems.at[sem],
                                            device_id=to, device_id_type=MESH)

    local = pltpu.make_async_copy(wob_s, wo4_ref.at[2 * x + y], local_sem)
    ici = [remote(wob_s.at[rows(c), :], block_half((x, y), c), m, (*chip, c)) for m, chip in enumerate(chips)]
    fwd = [remote(block_half(chip, c), block_half(chip, c), 3 + m, sib) for m, chip in enumerate(chips)]

    @pl.when(step == 0)
    def _():
        wob_s[...] = wo_ref[...].astype(MXU_DTYPE)
        local.start()
        for cp in ici:
            cp.start()

    @pl.when(step == n_steps // 2)
    def _():
        for m, chip in enumerate(chips):
            remote(block_half(chip, c), block_half(chip, c), m, sib).wait_recv()
            fwd[m].start()

    @pl.when(step == n_steps - 1)
    def _():
        for m, chip in enumerate(chips):
            remote(block_half(chip, 1 - c), block_half(chip, 1 - c), 3 + m, sib).wait_recv()
        for cp in ici + fwd:
            cp.wait_send()
        local.wait()


def _chip_blocks_shape(s, n_sub):
    return jax.ShapeDtypeStruct((3, n_sub) + s.shape[1:], s.dtype)


def _sum_chip_blocks(s, r, kc_arr, n_sub, name):
    _, rr, cc = s.shape

    def body(kc_ref, s_ref, r_ref, o_ref):
        o_ref[...] = ((s_ref[...] + r_ref[0].astype(F32)) + r_ref[1].astype(F32)) + r_ref[2].astype(F32)

    return pl.pallas_call(
        body, out_shape=jax.ShapeDtypeStruct((n_sub, 2 * rr, cc), F32),
        grid_spec=pltpu.PrefetchScalarGridSpec(
            num_scalar_prefetch=1, grid=(n_sub,),
            in_specs=[pl.BlockSpec((1, rr, cc), lambda q, kc: (n_sub * kc[0] + q, 0, 0)),
                      pl.BlockSpec((3, 1, rr, cc), lambda q, kc: (0, q, 0, 0))],
            out_specs=pl.BlockSpec((1, rr, cc), lambda q, kc: (q, kc[1], 0))),
        compiler_params=_cp(ARB), name=name,
    )(kc_arr, s, r)


def _swap_halves_and_sum(f_in, f_out, v):
    hi, ho = f_in.shape[1] // 2, f_out.shape[1] // 2
    n_dev = 8

    def body(fi_in, fo_in, v_ref, fi_ref, fo_ref, tot_ref, slots, send_sems, recv_sems):
        del fi_in, fo_in
        x, y, c, _ = _mesh_pos()
        sib = (x, y, 1 - c)
        me = 4 * x + 2 * y + c
        slots[me] = v_ref[...]
        si = fi_ref.at[:, pl.ds(pl.multiple_of(hi * c, hi), hi), :]
        so = fo_ref.at[:, pl.ds(pl.multiple_of(ho * c, ho), ho), :]

        def remote(ref, sem, to):
            return pltpu.make_async_remote_copy(src_ref=ref, dst_ref=ref, send_sem=send_sems.at[sem],
                                                recv_sem=recv_sems.at[sem], device_id=to, device_id_type=MESH)

        cps = [remote(si, n_dev - 1, sib), remote(so, n_dev, sib)]
        for d in range(1, n_dev):
            peer = (1 - x if d & 4 else x, 1 - y if d & 2 else y, 1 - c if d & 1 else c)
            cps.append(remote(slots.at[me], d - 1, peer))
        for cp in cps:
            cp.start()
        for cp in cps:
            cp.wait()
        total = slots[0]
        for dev in range(1, n_dev):
            total = total + slots[dev]
        tot_ref[...] = total

    hbm = pl.BlockSpec(memory_space=pl.ANY)
    vm = pl.BlockSpec(memory_space=pltpu.VMEM)
    return pl.pallas_call(
        body,
        out_shape=(jax.ShapeDtypeStruct(f_in.shape, F32), jax.ShapeDtypeStruct(f_out.shape, F32),
                   jax.ShapeDtypeStruct(v.shape, F32)),
        in_specs=[hbm, hbm, vm], out_specs=(hbm, hbm, vm), input_output_aliases={0: 0, 1: 1},
        scratch_shapes=[pltpu.VMEM((n_dev,) + v.shape, F32), pltpu.SemaphoreType.DMA((n_dev + 1,)),
                        pltpu.SemaphoreType.DMA((n_dev + 1,))],
        compiler_params=_cp(), name="swap_halves_and_sum",
    )(f_in, f_out, v)


def _out_projection_loss(yc, yl, x, target, wo, final_g):
    t = x.shape[0]
    tm = 512

    def body(yc_ref, yl_ref, x_ref, t_ref, wo_ref, fg_ref, do_ref, dob_ref, dy_ref, st_ref, y_wo):
        @pl.when(pl.program_id(0) == 0)
        def _():
            st_ref[...] = jnp.zeros_like(st_ref)

        y_wo[...] = _mm(yc_ref[...], wo_ref[0:D_PART, :]) + _mm(yl_ref[...], wo_ref[D_PART:2 * D_PART, :])

        def norm_loss_slab(s, carry):
            g_sum, loss_sum = carry
            rows = pl.ds(pl.multiple_of(s * SLAB, SLAB), SLAB)
            o = x_ref[rows, :] + y_wo[rows, :]
            r2 = lax.rsqrt(jnp.mean(o * o, axis=-1, keepdims=True) + RMS_EPS)
            ohat = o * r2
            fg = fg_ref[...]
            diff = ohat * fg - t_ref[rows, :]
            dout = diff * (1.0 / D_MODEL)
            gp = dout * fg
            do = r2 * (gp - ohat * jnp.mean(gp * ohat, axis=-1, keepdims=True))
            do_ref[rows, :] = do
            dob_ref[rows, :] = do.astype(MXU_DTYPE)
            loss = 0.5 * jnp.sum(jnp.sum(diff * diff, axis=-1, keepdims=True) * (1.0 / D_MODEL), axis=0, keepdims=True)
            return g_sum + jnp.sum(dout * ohat, axis=0, keepdims=True), loss_sum + loss

        g_sum, loss_sum = lax.fori_loop(0, tm // SLAB, norm_loss_slab,
                                        (jnp.zeros((1, D_MODEL), F32), jnp.zeros((1, 1), F32)))
        st_ref[0:1, :] += g_sum
        st_ref[1:2, :] += jnp.broadcast_to(loss_sum, (1, D_MODEL))
        dy_ref[...] = _mm_nt(dob_ref[...], wo_ref[...])

    row = lambda i: (i, 0)
    fix = lambda i: (0, 0)
    return pl.pallas_call(
        body, grid=(t // tm,),
        in_specs=[pl.BlockSpec((tm, D_PART), row), pl.BlockSpec((tm, D_PART), row),
                  pl.BlockSpec((tm, D_MODEL), row), pl.BlockSpec((tm, D_MODEL), row),
                  pl.BlockSpec((2 * D_PART, D_MODEL), fix), pl.BlockSpec((1, D_MODEL), fix)],
        out_specs=(pl.BlockSpec((tm, D_MODEL), row), pl.BlockSpec((tm, D_MODEL), row),
                   pl.BlockSpec((tm, 2 * D_PART), row), pl.BlockSpec((SUBLANES, D_MODEL), fix)),
        out_shape=(jax.ShapeDtypeStruct((t, D_MODEL), F32), jax.ShapeDtypeStruct((t, D_MODEL), MXU_DTYPE),
                   jax.ShapeDtypeStruct((t, 2 * D_PART), F32), jax.ShapeDtypeStruct((SUBLANES, D_MODEL), F32)),
        scratch_shapes=[pltpu.VMEM((tm, D_MODEL), F32)],
        compiler_params=_cp(ARB), name="out_projection_loss",
    )(yc, yl, x, target, wo, final_g)


def _input_grad(dproj, w12, x, do, ln_g, sb_in):
    t = x.shape[0]
    tm = 1024

    def body(dp_ref, w_ref, x_ref, do_ref, g_ref, s_ref, gx_ref, st_ref, r_ref, acc, send_sems, recv_sems):
        i, p = pl.program_id(0), pl.program_id(1)

        @pl.when((i == 0) & (p == 0))
        def _():
            st_ref[...] = jnp.zeros_like(st_ref)
            for cp in _chip_block_copies(s_ref, r_ref, CHUNKS_PER_BLOCK, send_sems, recv_sems):
                cp.start()

        @pl.when((i == t // tm - 1) & (p == N_PARTS - 1))
        def _():
            for cp in _chip_block_copies(s_ref, r_ref, CHUNKS_PER_BLOCK, send_sems, recv_sems):
                cp.wait()

        @pl.when(p == 0)
        def _():
            acc[...] = jnp.zeros_like(acc)

        acc[...] += _mm_nt(dp_ref[0], jnp.concatenate([w_ref[0], w_ref[1]], axis=1))

        @pl.when(p == N_PARTS - 1)
        def _():
            def norm_bwd_slab(s, g_sum):
                rows = pl.ds(pl.multiple_of(s * SLAB, SLAB), SLAB)
                xf = x_ref[rows, :]
                r = lax.rsqrt(jnp.mean(xf * xf, axis=-1, keepdims=True) + RMS_EPS)
                xhat = xf * r
                dxn = acc[rows, :]
                dxh = dxn * g_ref[...]
                gx_ref[rows, :] = do_ref[rows, :] + r * (dxh - xhat * jnp.mean(dxh * xhat, axis=-1, keepdims=True))
                return g_sum + jnp.sum(dxn * xhat, axis=0, keepdims=True)

            st_ref[0:1, :] += lax.fori_loop(0, tm // SLAB, norm_bwd_slab, jnp.zeros((1, D_MODEL), F32))

    row = lambda i, p: (i, 0)
    fix = lambda i, p: (0, 0)
    return pl.pallas_call(
        body, grid=(t // tm, N_PARTS),
        in_specs=[
            pl.BlockSpec((1, tm, D_PART), lambda i, p: (p, i, 0)),
            pl.BlockSpec((2, D_MODEL, CHUNK), lambda i, p: (p, 0, 0)),
            pl.BlockSpec((tm, D_MODEL), row), pl.BlockSpec((tm, D_MODEL), row), pl.BlockSpec((1, D_MODEL), fix),
            pl.BlockSpec(memory_space=pl.ANY)],
        out_specs=(pl.BlockSpec((tm, D_MODEL), row), pl.BlockSpec((SUBLANES, D_MODEL), fix),
                   pl.BlockSpec(memory_space=pl.ANY)),
        out_shape=(jax.ShapeDtypeStruct((t, D_MODEL), F32), jax.ShapeDtypeStruct((SUBLANES, D_MODEL), F32),
                   _chip_blocks_shape(sb_in, CHUNKS_PER_BLOCK)),
        scratch_shapes=[pltpu.VMEM((tm, D_MODEL), F32), pltpu.SemaphoreType.DMA((3,)), pltpu.SemaphoreType.DMA((3,))],
        compiler_params=_cp(ARB, ARB), name="input_grad",
    )(dproj, w12, x, do, ln_g, sb_in)


def _w_in_grad(xn, dproj, small):
    t = xn.shape[0]
    small_shape = pltpu.VMEM(small.shape, F32)

    def body(xn_ref, dp_ref, sm_ref, o_ref, ob_ref, red_ref, acc_s, r0, r1, r2, send_sems, recv_sems):
        _allreduce_behind(pl.program_id(0), (0, 1, 3, N_PARTS - 1), sm_ref, acc_s, (r0, r1, r2), red_ref, send_sems, recv_sems)
        g = _mm_tn(xn_ref[...], dp_ref[0])
        for s in range(2):
            o_ref[s] = g[:, CHUNK * s:CHUNK * (s + 1)]
            ob_ref[s] = g[:, CHUNK * s:CHUNK * (s + 1)].astype(jnp.bfloat16)

    whole = pl.BlockSpec(small.shape, lambda p: (0, 0))
    pair = pl.BlockSpec((2, D_MODEL, CHUNK), lambda p: (p, 0, 0))
    return pl.pallas_call(
        body, grid=(N_PARTS,),
        in_specs=[pl.BlockSpec((t, D_MODEL), lambda p: (0, 0)),
                  pl.BlockSpec((1, t, D_PART), lambda p: (p, 0, 0)), whole],
        out_specs=(pair, pair, whole),
        out_shape=(jax.ShapeDtypeStruct((N_CHUNKS, D_MODEL, CHUNK), F32),
                   jax.ShapeDtypeStruct((N_CHUNKS, D_MODEL, CHUNK), jnp.bfloat16), jax.ShapeDtypeStruct(small.shape, F32)),
        scratch_shapes=[small_shape] * 4 + [pltpu.SemaphoreType.DMA((3,)), pltpu.SemaphoreType.DMA((3,))],
        compiler_params=_cp(ARB), name="w_in_grad",
    )(xn, dproj, small)


def _w_out_grad(yc, yl, dob):
    t = yc.shape[0]
    tk = 2048

    def body(yc_ref, yl_ref, do_ref, o_ref, ob_ref):
        j, kk = pl.program_id(0), pl.program_id(1)

        def accumulate(y_ref):
            @pl.when(kk == 0)
            def _():
                o_ref[...] = jnp.zeros_like(o_ref)

            o_ref[...] += _mm_tn(y_ref[...], do_ref[...])

            @pl.when(kk == t // tk - 1)
            def _():
                ob_ref[...] = o_ref[...].astype(jnp.bfloat16)

        pl.when(j == 0)(functools.partial(accumulate, yc_ref))
        pl.when(j == 1)(functools.partial(accumulate, yl_ref))

    def rows_of(half):
        return lambda j, kk: (jnp.where(j == half, kk, 0), 0)

    half = pl.BlockSpec((D_PART, D_MODEL), lambda j, kk: (j, 0))
    out, out_b = pl.pallas_call(
        body, grid=(2, t // tk),
        in_specs=[pl.BlockSpec((tk, D_PART), rows_of(0)), pl.BlockSpec((tk, D_PART), rows_of(1)),
                  pl.BlockSpec((tk, D_MODEL), lambda j, kk: (kk, 0))],
        out_specs=(half, half),
        out_shape=(jax.ShapeDtypeStruct((2 * D_PART, D_MODEL), F32), jax.ShapeDtypeStruct((2 * D_PART, D_MODEL), jnp.bfloat16)),
        compiler_params=_cp(ARB, ARB), name="w_out_grad",
    )(yc, yl, dob)
    blocks = (N_CHIPS, 2 * D_PART // N_CHIPS, D_MODEL)
    return out.reshape(blocks), out_b.reshape(blocks)


def _for_groups(n, fn, init, unroll=UNROLL, stores=(), descending=False):
    assert unroll % 2 == 0 and n % unroll == 0

    def trip(j, carry):
        held = None
        for uu in range(unroll):
            idx = j * unroll + uu
            carry, values = fn(idx, carry)
            if uu % 2 == 0:
                held = values
                continue
            low_group = n - 1 - idx if descending else idx - 1
            rows = pl.ds(pl.multiple_of(low_group * SUBLANES, 2 * SUBLANES), 2 * SUBLANES)
            pairs = zip(values, held) if descending else zip(held, values)
            for store, (lo, hi) in zip(stores, pairs, strict=True):
                store(rows, jnp.concatenate([lo, hi], axis=0).astype(MXU_DTYPE))
        return carry

    return lax.fori_loop(0, n // unroll, trip, init)


def _rows_of(ref, *lead, cols=slice(None)):
    def store(rows, value):
        ref[(*lead, rows, cols)] = value

    return store


def _pvb(pv_ref, r):
    return jnp.broadcast_to(pv_ref[r:r + 1, :], (SUBLANES, pv_ref.shape[1]))


def _conv3(pv_ref, u, u1, u2):
    return (_pvb(pv_ref, PV_CONV_W) * u2 + _pvb(pv_ref, PV_CONV_W + 1) * u1) + _pvb(pv_ref, PV_CONV_W + 2) * u


def _conv4(pv_ref, v, v1, v2, v3):
    return ((((_pvb(pv_ref, PV_LRU_W) * v3 + _pvb(pv_ref, PV_LRU_W + 1) * v2) + _pvb(pv_ref, PV_LRU_W + 2) * v1)
             + _pvb(pv_ref, PV_LRU_W + 3) * v) + _pvb(pv_ref, PV_LRU_B))


def _mixer_forward(proj, pvec, wai, w_out):
    t = proj.shape[1]
    tb = 1024
    ng = tb // SUBLANES
    nt = t // tb

    def body(bg_ref, cg_ref, xc_ref, gc_ref, xl_ref, gl_ref, pv_ref, wai_ref, wo_ref,
             yc_ref, yl_ref, h_ref, u_s, r_ref, ig_ref, wo4_ref,
             ucp_s, xlp_s, ls_s, hbuf_s, ub_s, gate_s, wob_s, local_sem, send_sems, recv_sems):
        _gather_w_out(pl.program_id(0) * nt + pl.program_id(1), NS * nt, wo_ref, wob_s, wo4_ref, local_sem, send_sems, recv_sems)

        @pl.when(pl.program_id(1) == 0)
        def _():
            ucp_s[...] = jnp.zeros_like(ucp_s)
            xlp_s[...] = jnp.zeros_like(xlp_s)
            hbuf_s[...] = jnp.zeros_like(hbuf_s)

        row = lax.broadcasted_iota(jnp.int32, (SUBLANES, LW), 0)
        ls_s[...] = RG_LRU_C * _log_sigmoid(_pvb(pv_ref, PV_LAM))

        def conv_group(g, carry):
            ucp, xlp = carry
            sl = pl.ds(pl.multiple_of(g * SUBLANES, SUBLANES), SUBLANES)
            uc = cg_ref[sl, :] * xc_ref[sl, :]
            v = _conv3(pv_ref, uc, _shift_down(uc, ucp, 1, row), _shift_down(uc, ucp, 2, row))
            yc = bg_ref[sl, :] * v
            rr = lax.rsqrt(_head_mean(yc * yc, CONV_HEAD) + RMS_EPS)
            gc = gc_ref[sl, :]
            zc = ((yc * rr) * _pvb(pv_ref, PV_CG)) * (gc * _sigmoid(gc))
            xl = xl_ref[sl, :]
            u = _conv4(pv_ref, xl, _shift_down(xl, xlp, 1, row), _shift_down(xl, xlp, 2, row), _shift_down(xl, xlp, 3, row))
            u_s[sl, :] = u
            return (uc, xl), (zc, u)

        ucp, xlp = _for_groups(ng, conv_group, (ucp_s[...], xlp_s[...]), unroll=2 * UNROLL,
                               stores=(_rows_of(yc_ref), _rows_of(ub_s)))
        ucp_s[...] = ucp
        xlp_s[...] = xlp

        gate_s[...] = _mm(ub_s[...], wai_ref[0])

        def lru_group(g, h_before):
            sl = pl.ds(pl.multiple_of(g * SUBLANES, SUBLANES), SUBLANES)
            u = u_s[sl, :]
            r = _sigmoid(gate_s[sl, 0:LW] + _pvb(pv_ref, PV_BA))
            ig = _sigmoid(gate_s[sl, LW:2 * LW] + _pvb(pv_ref, PV_BI))
            r_ref[sl, :] = r
            ig_ref[sl, :] = ig
            a, _, mult, _ = _decay(r, ls_s[...])
            A, B = _scan8_fwd(a, mult * (ig * u), row)
            h = B + A * jnp.broadcast_to(h_before[SUBLANES - 1:SUBLANES, :], (SUBLANES, LW))
            h_ref[sl, :] = h
            rr = lax.rsqrt(_head_mean(h * h, LRU_HEAD) + RMS_EPS)
            gl = gl_ref[sl, :]
            return h, (((h * rr) * _pvb(pv_ref, PV_LG)) * (gl * _sigmoid(gl)),)

        hbuf_s[...] = _for_groups(ng, lru_group, hbuf_s[...], unroll=2 * UNROLL, stores=(_rows_of(yl_ref),))

    def part(p):
        return pl.BlockSpec((None, tb, LW), lambda c, i: (2 * p + c // STRIPS_PER_CHUNK, i, c % STRIPS_PER_CHUNK))

    strip = pl.BlockSpec((tb, LW), lambda c, i: (i, c))
    return pl.pallas_call(
        body, grid=(NS, nt),
        in_specs=[part(p) for p in range(N_PARTS)] + [
            pl.BlockSpec((PV_ROWS, LW), lambda c, i: (0, c)),
            pl.BlockSpec((1, LW, 2 * LW), lambda c, i: (c, 0, 0)),
            pl.BlockSpec(w_out.shape, lambda c, i: (0, 0))],
        out_specs=(strip,) * 6 + (pl.BlockSpec(memory_space=pl.ANY),),
        out_shape=(jax.ShapeDtypeStruct((t, D_PART), MXU_DTYPE),) * 2 + (jax.ShapeDtypeStruct((t, D_PART), F32),) * 4 + (
            jax.ShapeDtypeStruct((N_CHIPS,) + w_out.shape, MXU_DTYPE),),
        scratch_shapes=[pltpu.VMEM((SUBLANES, LW), F32), pltpu.VMEM((SUBLANES, LW), F32), pltpu.VMEM((SUBLANES, LW), F32),
                        pltpu.VMEM((SUBLANES, LW), F32), pltpu.VMEM((tb, LW), MXU_DTYPE),
                        pltpu.VMEM((tb, 2 * LW), F32), pltpu.VMEM(w_out.shape, MXU_DTYPE),
                        pltpu.SemaphoreType.DMA, pltpu.SemaphoreType.DMA((6,)), pltpu.SemaphoreType.DMA((6,))],
        compiler_params=_cp(ARB, ARB), name="mixer_forward",
    )(proj, proj, proj, proj, proj, proj, pvec, wai, w_out)


def _mixer_backward(proj, h, u, r, ig, dy, pvec, wai, sb_out):
    t = proj.shape[1]
    tb = 1024
    ng = tb // SUBLANES
    nt = t // tb
    gpb = tb // SUBLANES

    def body(bg_ref, cg_ref, xc_ref, gc_ref, xl_ref, gl_ref, h_ref, u_ref, r_ref, ig_ref, dyc_ref, dyl_ref,
             cgh_ref, xch_ref, xlh_ref, hh_ref, pv_ref, wai_ref, so_ref,
             dp_ref, gw_ref, sv_ref, ro_ref,
             ls_s, ub_s, uce_s, xle_s, he_s, dgb_s, du_s, gbuf_s,
             acc_s, an_s, dvn_s, dun_s, send_sems, recv_sems):
        i = pl.program_id(1)
        first_block = i == nt - 1

        @pl.when((pl.program_id(0) == 0) & (i == 0))
        def _():
            for cp in _chip_block_copies(so_ref, ro_ref, 1, send_sems, recv_sems):
                cp.start()

        @pl.when((pl.program_id(0) == NS - 1) & (i == nt - 1))
        def _():
            for cp in _chip_block_copies(so_ref, ro_ref, 1, send_sems, recv_sems):
                cp.wait()

        @pl.when(i == 0)
        def _():
            acc_s[...] = jnp.zeros_like(acc_s)
            gw_ref[...] = jnp.zeros_like(gw_ref)
            an_s[...] = jnp.zeros_like(an_s)
            dvn_s[...] = jnp.zeros_like(dvn_s)
            dun_s[...] = jnp.zeros_like(dun_s)
            gbuf_s[...] = jnp.zeros_like(gbuf_s)

        row = lax.broadcasted_iota(jnp.int32, (SUBLANES, LW), 0)
        ls_s[...] = RG_LRU_C * _log_sigmoid(_pvb(pv_ref, PV_LAM))
        keep = jnp.where(first_block, 0.0, 1.0)
        uce_s[0:SUBLANES, :] = (cgh_ref[...] * xch_ref[...]) * keep
        xle_s[0:SUBLANES, :] = xlh_ref[...] * keep
        he_s[0:SUBLANES, :] = hh_ref[...] * keep
        xle_s[SUBLANES:SUBLANES + tb, :] = xl_ref[...]
        he_s[SUBLANES:SUBLANES + tb, :] = h_ref[...]

        uce_s[SUBLANES:SUBLANES + tb, :] = cg_ref[...] * xc_ref[...]

        def acc_add(k, v):
            acc_s[k] += v

        def main_group(gi, carry):
            a_next, dv_next, g_next = carry
            g = ng - 1 - gi
            r0 = pl.multiple_of(g * SUBLANES, SUBLANES)
            sl = pl.ds(r0, SUBLANES)
            sl_e = pl.ds(r0 + SUBLANES, SUBLANES)
            lsb = ls_s[...]
            u = u_ref[sl, :]
            r = r_ref[sl, :]
            ig = ig_ref[sl, :]
            a, e2, mult, inv_mult = _decay(r, lsb)
            gl = gl_ref[sl, :]
            sg = _sigmoid(gl)
            s_l = gl * sg
            h8 = he_s[sl_e, :]
            hprev = _shift_down(h8, he_s[sl, :], 1, row)
            rr = lax.rsqrt(_head_mean(h8 * h8, LRU_HEAD) + RMS_EPS)
            n = h8 * rr
            dz = dyl_ref[sl, :]
            lg = _pvb(pv_ref, PV_LG)
            acc_add(PV_LG, (dz * n) * s_l)
            p5 = ((dz * n) * lg) * (sg + s_l * (1.0 - sg))
            dn = (dz * lg) * s_l
            dh = rr * (dn - n * _head_mean(dn * n, LRU_HEAD))
            A, B = _scan8_rev(_shift_up(a, a_next, 1, row), dh, row)
            gg = B + A * jnp.broadcast_to(g_next[0:1, :], (SUBLANES, LW))
            da = gg * hprev
            iu = ig * u
            diu = gg * mult
            dla = da * a - (gg * iu) * (e2 * inv_mult)
            acc_add(PV_LAM, dla * r)
            dra = (dla * lsb) * (r * (1.0 - r))
            dia = (diu * u) * (ig * (1.0 - ig))
            acc_add(PV_BA, dra)
            acc_add(PV_BI, dia)
            du_s[sl, :] = diu * ig
            bg = bg_ref[sl, :]
            gc = gc_ref[sl, :]
            uc = uce_s[sl_e, :]
            ucp = uce_s[sl, :]
            uc1 = _shift_down(uc, ucp, 1, row)
            uc2 = _shift_down(uc, ucp, 2, row)
            v = _conv3(pv_ref, uc, uc1, uc2)
            yc = bg * v
            rrc = lax.rsqrt(_head_mean(yc * yc, CONV_HEAD) + RMS_EPS)
            nc = yc * rrc
            sgc = _sigmoid(gc)
            s_c = gc * sgc
            dzc = dyc_ref[sl, :]
            cgain = _pvb(pv_ref, PV_CG)
            acc_add(PV_CG, (dzc * nc) * s_c)
            p3 = ((dzc * nc) * cgain) * (sgc + s_c * (1.0 - sgc))
            dnc = (dzc * cgain) * s_c
            dyc = rrc * (dnc - nc * _head_mean(dnc * nc, CONV_HEAD))
            dv = dyc * bg
            duc = (_pvb(pv_ref, PV_CONV_W + 2) * dv + _pvb(pv_ref, PV_CONV_W + 1) * _shift_up(dv, dv_next, 1, row)
                   + _pvb(pv_ref, PV_CONV_W) * _shift_up(dv, dv_next, 2, row))
            acc_add(PV_CONV_W + 2, dv * uc)
            acc_add(PV_CONV_W + 1, dv * uc1)
            acc_add(PV_CONV_W, dv * uc2)
            return (a, dv, gg), (dyc * v, duc * xc_ref[sl, :], duc * cg_ref[sl, :], p3, p5, dra, dia, u)

        a_next, dv_next, g_next = _for_groups(
            ng, main_group, (an_s[...], dvn_s[...], gbuf_s[...]), descending=True,
            stores=(_rows_of(dp_ref, 0), _rows_of(dp_ref, 1), _rows_of(dp_ref, 2), _rows_of(dp_ref, 3), _rows_of(dp_ref, 5),
                    _rows_of(dgb_s, cols=slice(0, LW)), _rows_of(dgb_s, cols=slice(LW, 2 * LW)), _rows_of(ub_s)))
        an_s[...] = a_next
        dvn_s[...] = dv_next
        gbuf_s[...] = g_next

        dgb = dgb_s[...]
        du_s[...] += _mm_nt(dgb, wai_ref[0])
        gw_ref[0] += _mm_tn(ub_s[...], dgb)

        def lru_conv_group(gi, du_next):
            g = ng - 1 - gi
            r0 = pl.multiple_of(g * SUBLANES, SUBLANES)
            sl = pl.ds(r0, SUBLANES)
            du = du_s[sl, :]
            xl = xle_s[pl.ds(r0 + SUBLANES, SUBLANES), :]
            xlp = xle_s[sl, :]
            acc_add(PV_LRU_B, du)
            acc_add(PV_LRU_W + 3, du * xl)
            acc_add(PV_LRU_W + 2, du * _shift_down(xl, xlp, 1, row))
            acc_add(PV_LRU_W + 1, du * _shift_down(xl, xlp, 2, row))
            acc_add(PV_LRU_W, du * _shift_down(xl, xlp, 3, row))
            dxl = (((_pvb(pv_ref, PV_LRU_W + 3) * du + _pvb(pv_ref, PV_LRU_W + 2) * _shift_up(du, du_next, 1, row))
                    + _pvb(pv_ref, PV_LRU_W + 1) * _shift_up(du, du_next, 2, row))
                   + _pvb(pv_ref, PV_LRU_W) * _shift_up(du, du_next, 3, row))
            return du, (dxl,)

        dun_s[...] = _for_groups(ng, lru_conv_group, dun_s[...], descending=True, stores=(_rows_of(dp_ref, 4),))

        @pl.when(first_block)
        def _():
            sv_ref[...] = jnp.zeros_like(sv_ref)
            for k in range(N_ACC):
                tot = jnp.sum(acc_s[k], axis=0, keepdims=True)
                if k == PV_LAM:
                    tot = (RG_LRU_C * tot) / (1.0 + jnp.exp(pv_ref[PV_LAM:PV_LAM + 1, :]))
                sv_ref[k:k + 1, :] = tot

    def part(p):
        return pl.BlockSpec((None, tb, LW), lambda c, i: (2 * p + c // STRIPS_PER_CHUNK, nt - 1 - i, c % STRIPS_PER_CHUNK))

    def halo(p):
        return pl.BlockSpec((None, SUBLANES, LW), lambda c, i: (2 * p + c // STRIPS_PER_CHUNK,
                                                                jnp.maximum((nt - 1 - i) * gpb - 1, 0), c % STRIPS_PER_CHUNK))

    strip = pl.BlockSpec((tb, LW), lambda c, i: (nt - 1 - i, c))
    big = pltpu.VMEM((tb, LW), F32)
    big_e = pltpu.VMEM((tb + SUBLANES, LW), F32)
    small = pltpu.VMEM((SUBLANES, LW), F32)
    outs = pl.pallas_call(
        body, grid=(NS, nt),
        in_specs=[part(p) for p in range(N_PARTS)] + [
            strip, strip, strip, strip, strip, pl.BlockSpec((tb, LW), lambda c, i: (nt - 1 - i, NS + c)),
            halo(1), halo(2), halo(4),
            pl.BlockSpec((SUBLANES, LW), lambda c, i: (jnp.maximum((nt - 1 - i) * gpb - 1, 0), c)),
            pl.BlockSpec((PV_ROWS, LW), lambda c, i: (0, c)),
            pl.BlockSpec((1, LW, 2 * LW), lambda c, i: (c, 0, 0)),
            pl.BlockSpec(memory_space=pl.ANY)],
        out_specs=(pl.BlockSpec((N_PARTS, tb, LW), lambda c, i: (0, nt - 1 - i, c)),
                   pl.BlockSpec((1, LW, 2 * LW), lambda c, i: (c, 0, 0)),
                   pl.BlockSpec((PV_ROWS, LW), lambda c, i: (0, c)),
                   pl.BlockSpec(memory_space=pl.ANY)),
        out_shape=(jax.ShapeDtypeStruct((N_PARTS, t, D_PART), MXU_DTYPE),
                   jax.ShapeDtypeStruct((NS, LW, 2 * LW), F32), jax.ShapeDtypeStruct((PV_ROWS, D_PART), F32),
                   _chip_blocks_shape(sb_out, 1)),
        scratch_shapes=[small, pltpu.VMEM((tb, LW), MXU_DTYPE), big_e, big_e, big_e,
                        pltpu.VMEM((tb, 2 * LW), MXU_DTYPE), big, small,
                        pltpu.VMEM((N_ACC, SUBLANES, LW), F32), small, small, small,
                        pltpu.SemaphoreType.DMA((3,)), pltpu.SemaphoreType.DMA((3,))],
        compiler_params=_cp(ARB, ARB), name="mixer_backward",
    )(proj, proj, proj, proj, proj, proj, h, u, r, ig, dy, dy, proj, proj, proj, h, pvec, wai, sb_out)
    return outs


def _adamw(w, g, m, v):
    m = ADAM_B1 * m + (1.0 - ADAM_B1) * g
    v = ADAM_B2 * v + (1.0 - ADAM_B2) * (g * g)
    m_hat = m / (1.0 - ADAM_B1 ** ADAM_STEP)
    v_hat = v / (1.0 - ADAM_B2 ** ADAM_STEP)
    delta = -ADAM_LR * (m_hat / (jnp.sqrt(v_hat) + ADAM_EPS) + ADAM_WD * w)
    return delta, m, v


def _adam_w_in(w, m, v, g3):
    rows, cols = w.shape
    tr = 128

    def body(w_ref, m_ref, v_ref, g_ref, go_ref, d_ref, mo_ref, vo_ref):
        for s in range(CHUNKS_PER_BLOCK):
            cs = slice(CHUNK * s, CHUNK * (s + 1))
            g = g_ref[s]
            d, mn, vn = _adamw(w_ref[:, cs], g, m_ref[:, cs], v_ref[:, cs])
            go_ref[:, cs] = g
            d_ref[:, cs] = d
            mo_ref[:, cs] = mn
            vo_ref[:, cs] = vn

    blk = pl.BlockSpec((tr, cols), lambda i: (i, 0))
    return pl.pallas_call(
        body, grid=(rows // tr,),
        in_specs=[blk, blk, blk, pl.BlockSpec((CHUNKS_PER_BLOCK, tr, CHUNK), lambda i: (0, i, 0))],
        out_specs=(blk,) * 4, out_shape=(jax.ShapeDtypeStruct(w.shape, F32),) * 4,
        compiler_params=_cp(ARB), name="adam_w_in",
    )(w, m, v, g3)


def _adam_w_out(w, m, v, g):
    rows, cols = w.shape
    tr = 128

    def body(w_ref, m_ref, v_ref, g_ref, d_ref, mo_ref, vo_ref):
        d_ref[...], mo_ref[...], vo_ref[...] = _adamw(w_ref[...], g_ref[...], m_ref[...], v_ref[...])

    blk = pl.BlockSpec((tr, cols), lambda i: (i, 0))
    return pl.pallas_call(
        body, grid=(rows // tr,), in_specs=[blk] * 4, out_specs=(blk,) * 3,
        out_shape=(jax.ShapeDtypeStruct(w.shape, F32),) * 3,
        compiler_params=_cp(ARB), name="adam_w_out",
    )(w, m, v, g)


def _adam_small(ws, ms, vs, gs):
    n = len(ws)

    def body(*refs):
        w_r, m_r, v_r, g_r = refs[0:n], refs[n:2 * n], refs[2 * n:3 * n], refs[3 * n:4 * n]
        d_o, m_o, v_o = refs[4 * n:5 * n], refs[5 * n:6 * n], refs[6 * n:7 * n]
        for j in range(n):
            d_o[j][...], m_o[j][...], v_o[j][...] = _adamw(w_r[j][...], g_r[j][...], m_r[j][...], v_r[j][...])

    vm = pl.BlockSpec(memory_space=pltpu.VMEM)
    shapes = tuple(jax.ShapeDtypeStruct(w.shape, F32) for w in ws)
    outs = pl.pallas_call(
        body, in_specs=[vm] * (4 * n), out_specs=(vm,) * (3 * n), out_shape=shapes * 3,
        compiler_params=_cp(), name="adam_small",
    )(*ws, *ms, *vs, *gs)
    return outs[0:n], outs[n:2 * n], outs[2 * n:3 * n]


def _block_diag_strips(w):
    w4 = w.reshape(NS, HEADS_PER_STRIP, LRU_HEAD, LRU_HEAD)
    rows = [jnp.pad(w4[:, hh], ((0, 0), (0, 0), (LRU_HEAD * hh, LW - LRU_HEAD * (hh + 1)))) for hh in range(HEADS_PER_STRIP)]
    return jnp.concatenate(rows, axis=1)


def _strip_diag_blocks(g):
    g5 = g.reshape(NS, HEADS_PER_STRIP, LRU_HEAD, HEADS_PER_STRIP, LRU_HEAD)
    return jnp.stack([g5[:, hh, :, hh, :] for hh in range(HEADS_PER_STRIP)], axis=1).reshape(NS * HEADS_PER_STRIP, LRU_HEAD, LRU_HEAD)


def kernel(x, ln_g, w_in, conv_w, lru_conv_w, lru_conv_b, w_a, b_a, w_i, b_i, lam, conv_out_g, lru_out_g, w_out, final_g, loss_target, m_ln_g, m_w_in, m_conv_w, m_lru_conv_w, m_lru_conv_b, m_w_a, m_b_a, m_w_i, m_b_i, m_lam, m_conv_out_g, m_lru_out_g, m_w_out, m_final_g, v_ln_g, v_w_in, v_conv_w, v_lru_conv_w, v_lru_conv_b, v_w_a, v_b_a, v_w_i, v_b_i, v_lam, v_conv_out_g, v_lru_out_g, v_w_out, v_final_g):
    xi, yi, ci = lax.axis_index("x"), lax.axis_index("y"), lax.axis_index("c")
    k = 2 * xi + yi
    t = x.shape[1]
    x2 = x.reshape(t, D_MODEL)
    tgt2 = loss_target.reshape(t, D_MODEL)
    row = lambda a: a.reshape(1, -1)

    small = jnp.concatenate([conv_w, lru_conv_w, jnp.zeros((1, conv_w.shape[1]), F32)], axis=0)
    proj, xn, w12, sm4 = _gather_in_projection(x2, row(ln_g), w_in, small)
    convs = jnp.transpose(sm4, (1, 0, 2)).reshape(SUBLANES, D_PART)
    pvec = jnp.concatenate(
        [convs[0:7], row(lru_conv_b), row(b_a), row(b_i), row(lam), row(conv_out_g), row(lru_out_g),
         jnp.zeros((PV_ROWS - N_ACC, D_PART), F32)], axis=0)
    wai = jnp.concatenate([_block_diag_strips(w_a), _block_diag_strips(w_i)], axis=2).astype(MXU_DTYPE)

    c_arr = jnp.reshape(ci, (1,)).astype(jnp.int32)
    kc_arr = jnp.stack([k, ci]).astype(jnp.int32)
    yc, yl, h, u, r, ig, wo4 = _mixer_forward(proj, pvec, wai, w_out)
    wo = wo4.reshape(2 * D_PART, D_MODEL)
    do, dob, dy, st_out = _out_projection_loss(yc, yl, x2, tgt2, wo, row(final_g))
    go4, go4b = _w_out_grad(yc, yl, dob)
    s_out, sb_out = _add_sibling_halves(go4, go4b, c_arr, "add_sibling_halves_out")
    dproj, g_wai, svec, r2o = _mixer_backward(proj, h, u, r, ig, dy, pvec, wai, sb_out)
    gwa = _strip_diag_blocks(g_wai[:, :, 0:LW]).reshape(LRU_HEAD, D_PART)
    gwi = _strip_diag_blocks(g_wai[:, :, LW:2 * LW]).reshape(LRU_HEAD, D_PART)
    g12, g12b, red = _w_in_grad(xn, dproj, jnp.concatenate([svec, st_out, gwa, gwi], axis=0))
    s_in, sb_in = _add_sibling_halves(g12, g12b, c_arr, "add_sibling_halves_in")
    grad_x, st_in, r2i = _input_grad(dproj, w12, x2, do, row(ln_g), sb_in)
    f_in = _sum_chip_blocks(s_in, r2i, kc_arr, CHUNKS_PER_BLOCK, "sum_chip_blocks_in")
    f_out = _sum_chip_blocks(s_out, r2o, kc_arr, 1, "sum_chip_blocks_out")
    f_in, f_out, red_ln = _swap_halves_and_sum(f_in, f_out, st_in)
    r_out = PV_ROWS
    r_wa = PV_ROWS + SUBLANES
    r_wi = r_wa + LRU_HEAD
    loss = red[r_out + 1, 0]

    g_w_in, d_w_in, nm_w_in, nv_w_in = _adam_w_in(w_in, m_w_in, v_w_in, f_in)
    g_w_out = f_out[0]
    d_w_out, nm_w_out, nv_w_out = _adam_w_out(w_out, m_w_out, v_w_out, g_w_out)

    ncol = conv_w.shape[1]
    conv_cols = lax.dynamic_slice(red, (0, k * ncol), (SUBLANES, ncol))
    g_small = {
        "ln_g": red_ln[0], "conv_w": conv_cols[0:3], "lru_conv_w": conv_cols[3:7], "lru_conv_b": red[PV_LRU_B],
        "w_a": red[r_wa:r_wa + LRU_HEAD].reshape(w_a.shape), "b_a": red[PV_BA],
        "w_i": red[r_wi:r_wi + LRU_HEAD].reshape(w_i.shape), "b_i": red[PV_BI], "lam": red[PV_LAM],
        "conv_out_g": red[PV_CG], "lru_out_g": red[PV_LG], "final_g": red[r_out],
    }
    w_small = {"ln_g": ln_g, "conv_w": conv_w, "lru_conv_w": lru_conv_w, "lru_conv_b": lru_conv_b, "w_a": w_a, "b_a": b_a,
               "w_i": w_i, "b_i": b_i, "lam": lam, "conv_out_g": conv_out_g, "lru_out_g": lru_out_g, "final_g": final_g}
    m_small = {"ln_g": m_ln_g, "conv_w": m_conv_w, "lru_conv_w": m_lru_conv_w, "lru_conv_b": m_lru_conv_b, "w_a": m_w_a,
               "b_a": m_b_a, "w_i": m_w_i, "b_i": m_b_i, "lam": m_lam, "conv_out_g": m_conv_out_g,
               "lru_out_g": m_lru_out_g, "final_g": m_final_g}
    v_small = {"ln_g": v_ln_g, "conv_w": v_conv_w, "lru_conv_w": v_lru_conv_w, "lru_conv_b": v_lru_conv_b, "w_a": v_w_a,
               "b_a": v_b_a, "w_i": v_w_i, "b_i": v_b_i, "lam": v_lam, "conv_out_g": v_conv_out_g,
               "lru_out_g": v_lru_out_g, "final_g": v_final_g}
    names = list(w_small)
    as2d = lambda a: a.reshape(1, -1) if a.ndim == 1 else a
    d_s, m_s, v_s = _adam_small([as2d(w_small[n]) for n in names], [as2d(m_small[n]) for n in names],
                                [as2d(v_small[n]) for n in names], [as2d(g_small[n]) for n in names])
    back = lambda n, a: a.reshape(w_small[n].shape)
    grads = {n: g_small[n] for n in names}
    deltas = {n: back(n, a) for n, a in zip(names, d_s)}
    new_m = {n: back(n, a) for n, a in zip(names, m_s)}
    new_v = {n: back(n, a) for n, a in zip(names, v_s)}
    grads["w_in"], deltas["w_in"], new_m["w_in"], new_v["w_in"] = g_w_in, d_w_in, nm_w_in, nv_w_in
    grads["w_out"], deltas["w_out"], new_m["w_out"], new_v["w_out"] = g_w_out, d_w_out, nm_w_out, nv_w_out

    order = ["ln_g", "w_in", "conv_w", "lru_conv_w", "lru_conv_b", "w_a", "b_a", "w_i", "b_i", "lam", "conv_out_g",
             "lru_out_g", "w_out", "final_g"]
    return (loss, grad_x.reshape(x.shape), *[grads[n] for n in order], *[deltas[n] for n in order],
            *[new_m[n] for n in order], *[new_v[n] for n in order])
```

```python
import functools

import jax
import jax.numpy as jnp
from jax import lax
from jax.experimental import pallas as pl
from jax.experimental.pallas import tpu as pltpu

F32 = jnp.float32
MXU_DTYPE = jnp.bfloat16

D_MODEL = 1024
D_PART = 1024
N_PARTS = 6
CHUNK = 512
CHUNKS_PER_BLOCK = 3
N_CHUNKS = 12
N_CHIPS = 4
SUBLANES = 8
LANES = 128
LW = 256
FWD_LW = 512
UNROLL = 8
NS = D_PART // LW
STRIPS_PER_CHUNK = CHUNK // LW
CONV_HEAD = 128
LRU_HEAD = 64
HEADS_PER_STRIP = LW // LRU_HEAD
RMS_EPS = 1e-6
RG_LRU_C = 8.0
ADAM_LR = 0.001
ADAM_B1 = 0.9
ADAM_B2 = 0.999
ADAM_EPS = 1e-08
ADAM_WD = 0.01
ADAM_STEP = 10

PV_CONV_W = 0
PV_LRU_W = 3
PV_LRU_B = 7
PV_BA = 8
PV_BI = 9
PV_LAM = 10
PV_CG = 11
PV_LG = 12
PV_ROWS = 16
N_ACC = 13

SLAB = 128
MESH = pl.DeviceIdType.MESH
VMEM_LIMIT = 56 * 1024 * 1024
ARB = "arbitrary"


def _cp(*sem, **kw):
    return pltpu.CompilerParams(dimension_semantics=sem or None, vmem_limit_bytes=VMEM_LIMIT, **kw)


def _mm(a, b):
    return jnp.dot(a, b, preferred_element_type=F32)


def _mm_nt(a, b):
    return lax.dot_general(a, b, (((1,), (1,)), ((), ())), preferred_element_type=F32)


def _mm_tn(a, b):
    return lax.dot_general(a, b, (((0,), (0,)), ((), ())), preferred_element_type=F32)


def _sigmoid(x):
    return 0.5 * jnp.tanh(0.5 * x) + 0.5


def _log_sigmoid(x):
    z = jnp.exp(-jnp.abs(x))
    u = 1.0 + z
    log1p = jnp.where(u == 1.0, z, jnp.log(u) * z / (u - 1.0))
    return jnp.minimum(x, 0.0) - log1p


def _head_mean(z, head):
    out = []
    for k in range(z.shape[1] // LANES):
        zk = z[:, LANES * k:LANES * (k + 1)]
        if head == LANES:
            m = jnp.sum(zk, axis=-1, keepdims=True) * (1.0 / head)
            out.append(jnp.broadcast_to(m, zk.shape))
        else:
            lo = lax.broadcasted_iota(jnp.int32, zk.shape, 1) < head
            s_lo = jnp.sum(jnp.where(lo, zk, 0.0), axis=-1, keepdims=True)
            s_hi = jnp.sum(jnp.where(lo, 0.0, zk), axis=-1, keepdims=True)
            out.append(jnp.where(lo, s_lo, s_hi) * (1.0 / head))
    return jnp.concatenate(out, axis=1)


def _shift_down(cur, prev, d, row):
    return pltpu.roll(jnp.where(row < SUBLANES - d, cur, prev), d, 0)


def _shift_up(cur, nxt, d, row):
    return pltpu.roll(jnp.where(row >= d, cur, nxt), SUBLANES - d, 0)


def _scan8_fwd(a, b, row):
    A, B = a, b
    for d in (1, 2, 4):
        m = row >= d
        a_s = jnp.where(m, pltpu.roll(A, d, 0), 1.0)
        b_s = jnp.where(m, pltpu.roll(B, d, 0), 0.0)
        B = A * b_s + B
        A = A * a_s
    return A, B


def _scan8_rev(a, b, row):
    A, B = a, b
    for d in (1, 2, 4):
        m = row < SUBLANES - d
        a_s = jnp.where(m, pltpu.roll(A, SUBLANES - d, 0), 1.0)
        b_s = jnp.where(m, pltpu.roll(B, SUBLANES - d, 0), 0.0)
        B = A * b_s + B
        A = A * a_s
    return A, B


def _decay(r, ls8):
    la = r * ls8
    a = jnp.exp(la)
    e2 = a * a
    em = -jnp.tanh(la) * (1.0 + e2)
    inv_mult = lax.rsqrt(em)
    return a, e2, em * inv_mult, inv_mult


def _mesh_pos():
    x, y, c = lax.axis_index("x"), lax.axis_index("y"), lax.axis_index("c")
    chips = [(1 - x, y), (x, 1 - y), (1 - x, 1 - y)]
    return x, y, c, chips


def _gather_in_projection(x, ln_g, w_in, small):
    t = x.shape[0]
    rb_x = 512
    rb_mm = 2048
    n_mm = t // rb_mm
    half = w_in.shape[0] // 2

    def body(x_hbm, g_ref, wi_ref, sm_ref, proj_hbm, xn_ref, w12_ref, sm4_ref,
             xbuf, obuf, x_sems, o_sems, send_sems, recv_sems):
        x_, y_, c, chips = _mesh_pos()
        k = 2 * x_ + y_
        sib = (x_, y_, 1 - c)
        sm4_ref[k] = sm_ref[...]

        def remote(ref, sem, to):
            return pltpu.make_async_remote_copy(src_ref=ref, dst_ref=ref, send_sem=send_sems.at[sem],
                                                recv_sem=recv_sems.at[sem], device_id=to, device_id_type=MESH)

        def chunk_of(chip, s):
            return CHUNKS_PER_BLOCK * (2 * chip[0] + chip[1]) + s

        def piece(q, core, first=0, rows=half):
            return w12_ref.at[q, pl.ds(pl.multiple_of(half * core + first, SUBLANES * 2), rows), :]

        nbr_x, nbr_y, diagonal = chips
        quarter = half // 2
        DIAG = [(0, 0, half, 0), (1, 0, quarter, 0), (1, quarter, quarter, 1), (2, 0, half, 1)]
        ici = lambda m, s: 2 * s + m
        dgn = lambda j: 6 + j
        to_sib = 10
        sml = lambda m: 20 + m

        sends = []
        for s in range(CHUNKS_PER_BLOCK):
            w12_ref[chunk_of((x_, y_), s)] = wi_ref[:, CHUNK * s:CHUNK * (s + 1)].astype(MXU_DTYPE)
            for m, chip in enumerate((nbr_x, nbr_y)):
                sends.append(remote(piece(chunk_of((x_, y_), s), c), ici(m, s), (*chip, c)))
                sends[-1].start()
        for m, chip in enumerate(chips):
            sends.append(remote(sm4_ref.at[k], sml(m), (*chip, c)))
            sends[-1].start()

        def x_copy(rb, slot):
            return pltpu.make_async_copy(x_hbm.at[pl.ds(rb * rb_x, rb_x), :], xbuf.at[slot], x_sems.at[slot])

        x_copy(0, 0).start()
        for rb in range(t // rb_x):
            slot = rb % 2
            x_copy(rb, slot).wait()
            if rb + 1 < t // rb_x:
                x_copy(rb + 1, 1 - slot).start()

            def norm_slab(sl, carry, rb=rb, slot=slot):
                xf = xbuf[slot, pl.ds(pl.multiple_of(sl * SLAB, SLAB), SLAB), :]
                r = lax.rsqrt(jnp.mean(xf * xf, axis=-1, keepdims=True) + RMS_EPS)
                xn_ref[pl.ds(pl.multiple_of(rb * rb_x + sl * SLAB, SLAB), SLAB), :] = ((xf * r) * g_ref[...]).astype(MXU_DTYPE)
                return carry

            lax.fori_loop(0, rb_x // SLAB, norm_slab, 0)

        def out_copy(q, i):
            return pltpu.make_async_copy(obuf.at[i], proj_hbm.at[q, pl.ds(pl.multiple_of(i * rb_mm, rb_mm), rb_mm), :],
                                         o_sems.at[i])

        def project(q, very_first):
            def row_block(i, carry):
                if not very_first:
                    out_copy(q, i).wait()
                obuf[i] = _mm(xn_ref[pl.ds(pl.multiple_of(i * rb_mm, rb_mm), rb_mm), :], w12_ref[q])
                out_copy(q, i).start()
                return carry

            lax.fori_loop(0, n_mm, row_block, 0)

        for s in range(CHUNKS_PER_BLOCK):
            project(chunk_of((x_, y_), s), very_first=(s == 0))

        steps = []
        for s in range(CHUNKS_PER_BLOCK):
            for m, chip in enumerate((nbr_x, nbr_y)):
                onward = [(first, rows, dgn(j), chips[via]) for j, (cs, first, rows, via) in enumerate(DIAG)
                          if cs == s and via == 1 - m]
                steps.append((chunk_of(chip, s), [(0, half, ici(m, s))], onward))
        for s in range(CHUNKS_PER_BLOCK):
            steps.append((chunk_of(diagonal, s), [(first, rows, dgn(j)) for j, (cs, first, rows, _) in enumerate(DIAG) if cs == s], []))

        def project_when_whole(step):
            q, pieces, _ = step
            for first, rows, sem in pieces:
                remote(piece(q, 1 - c, first, rows), to_sib + sem, sib).wait_recv()
            project(q, very_first=False)

        passed = []
        for j, (q, pieces, onward) in enumerate(steps):
            for first, rows, sem in pieces:
                remote(piece(q, c, first, rows), sem, sib).wait_recv()
            for first, rows, sem, chip in onward:
                passed.append(remote(piece(q, c, first, rows), sem, (*chip, c)))
                passed[-1].start()
            for first, rows, sem in pieces:
                passed.append(remote(piece(q, c, first, rows), to_sib + sem, sib))
                passed[-1].start()
            if j > 0:
                project_when_whole(steps[j - 1])
        project_when_whole(steps[-1])

        for m, chip in enumerate(chips):
            remote(sm4_ref.at[2 * chip[0] + chip[1]], sml(m), sib).wait_recv()
        for cp in sends + passed:
            cp.wait_send()
        for i in range(n_mm):
            out_copy(0, i).wait()

    vm = pl.BlockSpec(memory_space=pltpu.VMEM)
    hbm = pl.BlockSpec(memory_space=pl.ANY)
    n_sems = 23
    return pl.pallas_call(
        body,
        out_shape=(jax.ShapeDtypeStruct((N_CHUNKS, t, CHUNK), F32), jax.ShapeDtypeStruct((t, D_MODEL), MXU_DTYPE),
                   jax.ShapeDtypeStruct((N_CHUNKS, w_in.shape[0], CHUNK), MXU_DTYPE),
                   jax.ShapeDtypeStruct((N_CHIPS,) + small.shape, F32)),
        in_specs=[hbm, vm, vm, vm], out_specs=(hbm, vm, vm, vm),
        scratch_shapes=[pltpu.VMEM((2, rb_x, D_MODEL), F32), pltpu.VMEM((n_mm, rb_mm, CHUNK), F32),
                        pltpu.SemaphoreType.DMA((2,)), pltpu.SemaphoreType.DMA((n_mm,)),
                        pltpu.SemaphoreType.DMA((n_sems,)), pltpu.SemaphoreType.DMA((n_sems,))],
        compiler_params=_cp(), name="gather_in_projection",
    )(x, ln_g, w_in, small)


def _allreduce_behind(step, when, in_ref, acc_s, rbufs, out_ref, send_sems, recv_sems):
    x, y, c, _ = _mesh_pos()
    peers = [(x, y, 1 - c), (1 - x, y, c), (x, 1 - y, c)]

    def exchange(ph):
        return pltpu.make_async_remote_copy(src_ref=acc_s, dst_ref=rbufs[ph], send_sem=send_sems.at[ph],
                                            recv_sem=recv_sems.at[ph], device_id=peers[ph], device_id_type=MESH)

    @pl.when(step == when[0])
    def _():
        acc_s[...] = in_ref[...]
        exchange(0).start()

    for ph in (1, 2):
        @pl.when(step == when[ph])
        def _(ph=ph):
            exchange(ph - 1).wait()
            acc_s[...] = acc_s[...] + rbufs[ph - 1][...]
            exchange(ph).start()

    @pl.when(step == when[3])
    def _():
        exchange(2).wait()
        out_ref[...] = acc_s[...] + rbufs[2][...]


def _add_sibling_halves(g, gb, c_arr, name):
    n, rows, cols = g.shape
    half = rows // 2

    def body(c_ref, g_ref, gb_hbm, o_ref, ob_ref, rbuf, send_sems, recv_sems):
        q = pl.program_id(0)
        x, y, c, _ = _mesh_pos()
        theirs = pl.ds(pl.multiple_of(half * (1 - c), half), half)

        def copy(j):
            return pltpu.make_async_remote_copy(src_ref=gb_hbm.at[j, theirs, :], dst_ref=rbuf.at[j], send_sem=send_sems.at[j],
                                                recv_sem=recv_sems.at[j], device_id=(x, y, 1 - c), device_id_type=MESH)

        @pl.when(q == 0)
        def _():
            for j in range(n):
                copy(j).start()

        copy(q).wait_recv()
        s = g_ref[0] + rbuf[q].astype(F32)
        o_ref[0] = s
        ob_ref[0] = s.astype(jnp.bfloat16)

        @pl.when(q == n - 1)
        def _():
            for j in range(n):
                copy(j).wait_send()

    blk = pl.BlockSpec((1, half, cols), lambda q, c_ref: (q, 0, 0))
    return pl.pallas_call(
        body, out_shape=(jax.ShapeDtypeStruct((n, half, cols), F32), jax.ShapeDtypeStruct((n, half, cols), jnp.bfloat16)),
        grid_spec=pltpu.PrefetchScalarGridSpec(
            num_scalar_prefetch=1, grid=(n,),
            in_specs=[pl.BlockSpec((1, half, cols), lambda q, c_ref: (q, c_ref[0], 0)), pl.BlockSpec(memory_space=pl.ANY)],
            out_specs=(blk, blk),
            scratch_shapes=[pltpu.VMEM((n, half, cols), jnp.bfloat16), pltpu.SemaphoreType.DMA((n,)),
                            pltpu.SemaphoreType.DMA((n,))]),
        compiler_params=_cp(ARB), name=name,
    )(c_arr, g, gb)


def _chip_block_copies(s_ref, r_ref, n_sub, send_sems, recv_sems):
    x, y, c, chips = _mesh_pos()
    cps = []
    for m, chip in enumerate(chips):
        kk = 2 * chip[0] + chip[1]
        cps.append(pltpu.make_async_remote_copy(
            src_ref=s_ref.at[pl.ds(n_sub * kk, n_sub)], dst_ref=r_ref.at[m],
            send_sem=send_sems.at[m], recv_sem=recv_sems.at[m], device_id=(*chip, c), device_id_type=MESH))
    return cps


def _gather_w_out(step, n_steps, wo_ref, wob_s, wo4_ref, local_sem, send_sems, recv_sems):
    x, y, c, chips = _mesh_pos()
    sib = (x, y, 1 - c)
    half = wo_ref.shape[0] // 2

    def rows(core):
        return pl.ds(pl.multiple_of(half * core, half), half)

    def block_half(chip, core):
        return wo4_ref.at[2 * chip[0] + chip[1], rows(core), :]

    def remote(src, dst, sem, to):
        return pltpu.make_async_remote_copy(src_ref=src, dst_ref=dst, send_sem=send_sems.at[sem], recv_sem=recv_sems.at[sem],
                                            device_id=to, device_id_type=MESH)

    local = pltpu.make_async_copy(wob_s, wo4_ref.at[2 * x + y], local_sem)
    ici = [remote(wob_s.at[rows(c), :], block_half((x, y), c), m, (*chip, c)) for m, chip in enumerate(chips)]
    fwd = [remote(block_half(chip, c), block_half(chip, c), 3 + m, sib) for m, chip in enumerate(chips)]

    @pl.when(step == 0)
    def _():
        wob_s[...] = wo_ref[...].astype(MXU_DTYPE)
        local.start()
        for cp in ici:
            cp.start()

    @pl.when(step == n_steps // 2)
    def _():
        for m, chip in enumerate(chips):
            remote(block_half(chip, c), block_half(chip, c), m, sib).wait_recv()
            fwd[m].start()

    @pl.when(step == n_steps - 1)
    def _():
        for m, chip in enumerate(chips):
            remote(block_half(chip, 1 - c), block_half(chip, 1 - c), 3 + m, sib).wait_recv()
        for cp in ici + fwd:
            cp.wait_send()
        local.wait()


def _chip_blocks_shape(s, n_sub):
    return jax.ShapeDtypeStruct((3, n_sub) + s.shape[1:], s.dtype)


def _sum_chip_blocks(s, r, kc_arr, n_sub, name):
    _, rr, cc = s.shape

    def body(kc_ref, s_ref, r_ref, o_ref):
        o_ref[...] = ((s_ref[...] + r_ref[0].astype(F32)) + r_ref[1].astype(F32)) + r_ref[2].astype(F32)

    return pl.pallas_call(
        body, out_shape=jax.ShapeDtypeStruct((n_sub, 2 * rr, cc), F32),
        grid_spec=pltpu.PrefetchScalarGridSpec(
            num_scalar_prefetch=1, grid=(n_sub,),
            in_specs=[pl.BlockSpec((1, rr, cc), lambda q, kc: (n_sub * kc[0] + q, 0, 0)),
                      pl.BlockSpec((3, 1, rr, cc), lambda q, kc: (0, q, 0, 0))],
            out_specs=pl.BlockSpec((1, rr, cc), lambda q, kc: (q, kc[1], 0))),
        compiler_params=_cp(ARB), name=name,
    )(kc_arr, s, r)


def _swap_halves_and_sum(f_in, f_out, v):
    hi, ho = f_in.shape[1] // 2, f_out.shape[1] // 2
    n_dev = 8

    def body(fi_in, fo_in, v_ref, fi_ref, fo_ref, tot_ref, slots, send_sems, recv_sems):
        del fi_in, fo_in
        x, y, c, _ = _mesh_pos()
        sib = (x, y, 1 - c)
        me = 4 * x + 2 * y + c
        slots[me] = v_ref[...]
        si = fi_ref.at[:, pl.ds(pl.multiple_of(hi * c, hi), hi), :]
        so = fo_ref.at[:, pl.ds(pl.multiple_of(ho * c, ho), ho), :]

        def remote(ref, sem, to):
            return pltpu.make_async_remote_copy(src_ref=ref, dst_ref=ref, send_sem=send_sems.at[sem],
                                                recv_sem=recv_sems.at[sem], device_id=to, device_id_type=MESH)

        cps = [remote(si, n_dev - 1, sib), remote(so, n_dev, sib)]
        for d in range(1, n_dev):
            peer = (1 - x if d & 4 else x, 1 - y if d & 2 else y, 1 - c if d & 1 else c)
            cps.append(remote(slots.at[me], d - 1, peer))
        for cp in cps:
            cp.start()
        for cp in cps:
            cp.wait()
        total = slots[0]
        for dev in range(1, n_dev):
            total = total + slots[dev]
        tot_ref[...] = total

    hbm = pl.BlockSpec(memory_space=pl.ANY)
    vm = pl.BlockSpec(memory_space=pltpu.VMEM)
    return pl.pallas_call(
        body,
        out_shape=(jax.ShapeDtypeStruct(f_in.shape, F32), jax.ShapeDtypeStruct(f_out.shape, F32),
                   jax.ShapeDtypeStruct(v.shape, F32)),
        in_specs=[hbm, hbm, vm], out_specs=(hbm, hbm, vm), input_output_aliases={0: 0, 1: 1},
        scratch_shapes=[pltpu.VMEM((n_dev,) + v.shape, F32), pltpu.SemaphoreType.DMA((n_dev + 1,)),
                        pltpu.SemaphoreType.DMA((n_dev + 1,))],
        compiler_params=_cp(), name="swap_halves_and_sum",
    )(f_in, f_out, v)


def _out_projection_loss(yc, yl, x, target, wo, final_g):
    t = x.shape[0]
    tm = 512

    def body(yc_ref, yl_ref, x_ref, t_ref, wo_ref, fg_ref, do_ref, dob_ref, dy_ref, st_ref, y_wo):
        @pl.when(pl.program_id(0) == 0)
        def _():
            st_ref[...] = jnp.zeros_like(st_ref)

        y_wo[...] = _mm(yc_ref[...], wo_ref[0:D_PART, :]) + _mm(yl_ref[...], wo_ref[D_PART:2 * D_PART, :])

        def norm_loss_slab(s, carry):
            g_sum, loss_sum = carry
            rows = pl.ds(pl.multiple_of(s * SLAB, SLAB), SLAB)
            o = x_ref[rows, :] + y_wo[rows, :]
            r2 = lax.rsqrt(jnp.mean(o * o, axis=-1, keepdims=True) + RMS_EPS)
            ohat = o * r2
            fg = fg_ref[...]
            diff = ohat * fg - t_ref[rows, :]
            dout = diff * (1.0 / D_MODEL)
            gp = dout * fg
            do = r2 * (gp - ohat * jnp.mean(gp * ohat, axis=-1, keepdims=True))
            do_ref[rows, :] = do
            dob_ref[rows, :] = do.astype(MXU_DTYPE)
            loss = 0.5 * jnp.sum(jnp.sum(diff * diff, axis=-1, keepdims=True) * (1.0 / D_MODEL), axis=0, keepdims=True)
            return g_sum + jnp.sum(dout * ohat, axis=0, keepdims=True), loss_sum + loss

        g_sum, loss_sum = lax.fori_loop(0, tm // SLAB, norm_loss_slab,
                                        (jnp.zeros((1, D_MODEL), F32), jnp.zeros((1, 1), F32)))
        st_ref[0:1, :] += g_sum
        st_ref[1:2, :] += jnp.broadcast_to(loss_sum, (1, D_MODEL))
        dy_ref[...] = _mm_nt(dob_ref[...], wo_ref[...])

    row = lambda i: (i, 0)
    fix = lambda i: (0, 0)
    return pl.pallas_call(
        body, grid=(t // tm,),
        in_specs=[pl.BlockSpec((tm, D_PART), row), pl.BlockSpec((tm, D_PART), row),
                  pl.BlockSpec((tm, D_MODEL), row), pl.BlockSpec((tm, D_MODEL), row),
                  pl.BlockSpec((2 * D_PART, D_MODEL), fix), pl.BlockSpec((1, D_MODEL), fix)],
        out_specs=(pl.BlockSpec((tm, D_MODEL), row), pl.BlockSpec((tm, D_MODEL), row),
                   pl.BlockSpec((tm, 2 * D_PART), row), pl.BlockSpec((SUBLANES, D_MODEL), fix)),
        out_shape=(jax.ShapeDtypeStruct((t, D_MODEL), F32), jax.ShapeDtypeStruct((t, D_MODEL), MXU_DTYPE),
                   jax.ShapeDtypeStruct((t, 2 * D_PART), F32), jax.ShapeDtypeStruct((SUBLANES, D_MODEL), F32)),
        scratch_shapes=[pltpu.VMEM((tm, D_MODEL), F32)],
        compiler_params=_cp(ARB), name="out_projection_loss",
    )(yc, yl, x, target, wo, final_g)


def _input_grad(dproj, w12, x, do, ln_g, sb_in):
    t = x.shape[0]
    tm = 1024

    def body(dp_ref, w_ref, x_ref, do_ref, g_ref, s_ref, gx_ref, st_ref, r_ref, acc, send_sems, recv_sems):
        i, p = pl.program_id(0), pl.program_id(1)

        @pl.when((i == 0) & (p == 0))
        def _():
            st_ref[...] = jnp.zeros_like(st_ref)
            for cp in _chip_block_copies(s_ref, r_ref, CHUNKS_PER_BLOCK, send_sems, recv_sems):
                cp.start()

        @pl.when((i == t // tm - 1) & (p == N_PARTS - 1))
        def _():
            for cp in _chip_block_copies(s_ref, r_ref, CHUNKS_PER_BLOCK, send_sems, recv_sems):
                cp.wait()

        @pl.when(p == 0)
        def _():
            acc[...] = jnp.zeros_like(acc)

        acc[...] += _mm_nt(dp_ref[0], jnp.concatenate([w_ref[0], w_ref[1]], axis=1))

        @pl.when(p == N_PARTS - 1)
        def _():
            def norm_bwd_slab(s, g_sum):
                rows = pl.ds(pl.multiple_of(s * SLAB, SLAB), SLAB)
                xf = x_ref[rows, :]
                r = lax.rsqrt(jnp.mean(xf * xf, axis=-1, keepdims=True) + RMS_EPS)
                xhat = xf * r
                dxn = acc[rows, :]
                dxh = dxn * g_ref[...]
                gx_ref[rows, :] = do_ref[rows, :] + r * (dxh - xhat * jnp.mean(dxh * xhat, axis=-1, keepdims=True))
                return g_sum + jnp.sum(dxn * xhat, axis=0, keepdims=True)

            st_ref[0:1, :] += lax.fori_loop(0, tm // SLAB, norm_bwd_slab, jnp.zeros((1, D_MODEL), F32))

    row = lambda i, p: (i, 0)
    fix = lambda i, p: (0, 0)
    return pl.pallas_call(
        body, grid=(t // tm, N_PARTS),
        in_specs=[
            pl.BlockSpec((1, tm, D_PART), lambda i, p: (p, i, 0)),
            pl.BlockSpec((2, D_MODEL, CHUNK), lambda i, p: (p, 0, 0)),
            pl.BlockSpec((tm, D_MODEL), row), pl.BlockSpec((tm, D_MODEL), row), pl.BlockSpec((1, D_MODEL), fix),
            pl.BlockSpec(memory_space=pl.ANY)],
        out_specs=(pl.BlockSpec((tm, D_MODEL), row), pl.BlockSpec((SUBLANES, D_MODEL), fix),
                   pl.BlockSpec(memory_space=pl.ANY)),
        out_shape=(jax.ShapeDtypeStruct((t, D_MODEL), F32), jax.ShapeDtypeStruct((SUBLANES, D_MODEL), F32),
                   _chip_blocks_shape(sb_in, CHUNKS_PER_BLOCK)),
        scratch_shapes=[pltpu.VMEM((tm, D_MODEL), F32), pltpu.SemaphoreType.DMA((3,)), pltpu.SemaphoreType.DMA((3,))],
        compiler_params=_cp(ARB, ARB), name="input_grad",
    )(dproj, w12, x, do, ln_g, sb_in)


def _w_in_grad(xn, dproj, small):
    t = xn.shape[0]
    small_shape = pltpu.VMEM(small.shape, F32)

    def body(xn_ref, dp_ref, sm_ref, o_ref, ob_ref, red_ref, acc_s, r0, r1, r2, send_sems, recv_sems):
        _allreduce_behind(pl.program_id(0), (0, 1, 3, N_PARTS - 1), sm_ref, acc_s, (r0, r1, r2), red_ref, send_sems, recv_sems)
        g = _mm_tn(xn_ref[...], dp_ref[0])
        for s in range(2):
            o_ref[s] = g[:, CHUNK * s:CHUNK * (s + 1)]
            ob_ref[s] = g[:, CHUNK * s:CHUNK * (s + 1)].astype(jnp.bfloat16)

    whole = pl.BlockSpec(small.shape, lambda p: (0, 0))
    pair = pl.BlockSpec((2, D_MODEL, CHUNK), lambda p: (p, 0, 0))
    return pl.pallas_call(
        body, grid=(N_PARTS,),
        in_specs=[pl.BlockSpec((t, D_MODEL), lambda p: (0, 0)),
                  pl.BlockSpec((1, t, D_PART), lambda p: (p, 0, 0)), whole],
        out_specs=(pair, pair, whole),
        out_shape=(jax.ShapeDtypeStruct((N_CHUNKS, D_MODEL, CHUNK), F32),
                   jax.ShapeDtypeStruct((N_CHUNKS, D_MODEL, CHUNK), jnp.bfloat16), jax.ShapeDtypeStruct(small.shape, F32)),
        scratch_shapes=[small_shape] * 4 + [pltpu.SemaphoreType.DMA((3,)), pltpu.SemaphoreType.DMA((3,))],
        compiler_params=_cp(ARB), name="w_in_grad",
    )(xn, dproj, small)


def _w_out_grad(yc, yl, dob):
    t = yc.shape[0]
    tk = 2048

    def body(yc_ref, yl_ref, do_ref, o_ref, ob_ref):
        j, kk = pl.program_id(0), pl.program_id(1)

        def accumulate(y_ref):
            @pl.when(kk == 0)
            def _():
                o_ref[...] = jnp.zeros_like(o_ref)

            o_ref[...] += _mm_tn(y_ref[...], do_ref[...])

            @pl.when(kk == t // tk - 1)
            def _():
                ob_ref[...] = o_ref[...].astype(jnp.bfloat16)

        pl.when(j == 0)(functools.partial(accumulate, yc_ref))
        pl.when(j == 1)(functools.partial(accumulate, yl_ref))

    def rows_of(half):
        return lambda j, kk: (jnp.where(j == half, kk, 0), 0)

    half = pl.BlockSpec((D_PART, D_MODEL), lambda j, kk: (j, 0))
    out, out_b = pl.pallas_call(
        body, grid=(2, t // tk),
        in_specs=[pl.BlockSpec((tk, D_PART), rows_of(0)), pl.BlockSpec((tk, D_PART), rows_of(1)),
                  pl.BlockSpec((tk, D_MODEL), lambda j, kk: (kk, 0))],
        out_specs=(half, half),
        out_shape=(jax.ShapeDtypeStruct((2 * D_PART, D_MODEL), F32), jax.ShapeDtypeStruct((2 * D_PART, D_MODEL), jnp.bfloat16)),
        compiler_params=_cp(ARB, ARB), name="w_out_grad",
    )(yc, yl, dob)
    blocks = (N_CHIPS, 2 * D_PART // N_CHIPS, D_MODEL)
    return out.reshape(blocks), out_b.reshape(blocks)


def _for_groups(n, fn, init, unroll=UNROLL, stores=(), descending=False):
    assert unroll % 2 == 0 and n % unroll == 0

    def trip(j, carry):
        held = None
        for uu in range(unroll):
            idx = j * unroll + uu
            carry, values = fn(idx, carry)
            if uu % 2 == 0:
                held = values
                continue
            low_group = n - 1 - idx if descending else idx - 1
            rows = pl.ds(pl.multiple_of(low_group * SUBLANES, 2 * SUBLANES), 2 * SUBLANES)
            pairs = zip(values, held) if descending else zip(held, values)
            for store, (lo, hi) in zip(stores, pairs, strict=True):
                store(rows, jnp.concatenate([lo, hi], axis=0).astype(MXU_DTYPE))
        return carry

    return lax.fori_loop(0, n // unroll, trip, init)


def _rows_of(ref, *lead, cols=slice(None)):
    def store(rows, value):
        ref[(*lead, rows, cols)] = value

    return store


def _pvb(pv_ref, r):
    return jnp.broadcast_to(pv_ref[r:r + 1, :], (SUBLANES, pv_ref.shape[1]))


def _conv3(pv_ref, u, u1, u2):
    return (_pvb(pv_ref, PV_CONV_W) * u2 + _pvb(pv_ref, PV_CONV_W + 1) * u1) + _pvb(pv_ref, PV_CONV_W + 2) * u


def _conv4(pv_ref, v, v1, v2, v3):
    return ((((_pvb(pv_ref, PV_LRU_W) * v3 + _pvb(pv_ref, PV_LRU_W + 1) * v2) + _pvb(pv_ref, PV_LRU_W + 2) * v1)
             + _pvb(pv_ref, PV_LRU_W + 3) * v) + _pvb(pv_ref, PV_LRU_B))


def _mixer_forward(proj, pvec, wai, w_out):
    t = proj.shape[1]
    tb = 512
    ng = tb // SUBLANES
    nt = t // tb
    lw = FWD_LW
    ns = D_PART // lw
    per_chunk = CHUNK // lw

    def body(bg_ref, cg_ref, xc_ref, gc_ref, xl_ref, gl_ref, pv_ref, wai_ref, wo_ref,
             yc_ref, yl_ref, h_ref, u_s, r_ref, ig_ref, wo4_ref,
             ucp_s, xlp_s, ls_s, hbuf_s, ub_s, gate_s, wob_s, local_sem, send_sems, recv_sems):
        _gather_w_out(pl.program_id(0) * nt + pl.program_id(1), ns * nt, wo_ref, wob_s, wo4_ref, local_sem, send_sems, recv_sems)

        @pl.when(pl.program_id(1) == 0)
        def _():
            ucp_s[...] = jnp.zeros_like(ucp_s)
            xlp_s[...] = jnp.zeros_like(xlp_s)
            hbuf_s[...] = jnp.zeros_like(hbuf_s)

        row = lax.broadcasted_iota(jnp.int32, (SUBLANES, lw), 0)
        ls_s[...] = RG_LRU_C * _log_sigmoid(_pvb(pv_ref, PV_LAM))

        def conv_group(g, carry):
            ucp, xlp = carry
            sl = pl.ds(pl.multiple_of(g * SUBLANES, SUBLANES), SUBLANES)
            uc = cg_ref[sl, :] * xc_ref[sl, :]
            v = _conv3(pv_ref, uc, _shift_down(uc, ucp, 1, row), _shift_down(uc, ucp, 2, row))
            yc = bg_ref[sl, :] * v
            rr = lax.rsqrt(_head_mean(yc * yc, CONV_HEAD) + RMS_EPS)
            gc = gc_ref[sl, :]
            zc = ((yc * rr) * _pvb(pv_ref, PV_CG)) * (gc * _sigmoid(gc))
            xl = xl_ref[sl, :]
            u = _conv4(pv_ref, xl, _shift_down(xl, xlp, 1, row), _shift_down(xl, xlp, 2, row), _shift_down(xl, xlp, 3, row))
            u_s[sl, :] = u
            return (uc, xl), (zc, u)

        ucp, xlp = _for_groups(ng, conv_group, (ucp_s[...], xlp_s[...]), unroll=2 * UNROLL,
                               stores=(_rows_of(yc_ref), _rows_of(ub_s)))
        ucp_s[...] = ucp
        xlp_s[...] = xlp

        gate_s[...] = _mm(ub_s[...], wai_ref[0])

        def lru_group(g, h_before):
            sl = pl.ds(pl.multiple_of(g * SUBLANES, SUBLANES), SUBLANES)
            u = u_s[sl, :]
            r = _sigmoid(gate_s[sl, 0:lw] + _pvb(pv_ref, PV_BA))
            ig = _sigmoid(gate_s[sl, lw:2 * lw] + _pvb(pv_ref, PV_BI))
            r_ref[sl, :] = r
            ig_ref[sl, :] = ig
            a, _, mult, _ = _decay(r, ls_s[...])
            A, B = _scan8_fwd(a, mult * (ig * u), row)
            h = B + A * jnp.broadcast_to(h_before[SUBLANES - 1:SUBLANES, :], (SUBLANES, lw))
            h_ref[sl, :] = h
            rr = lax.rsqrt(_head_mean(h * h, LRU_HEAD) + RMS_EPS)
            gl = gl_ref[sl, :]
            return h, (((h * rr) * _pvb(pv_ref, PV_LG)) * (gl * _sigmoid(gl)),)

        hbuf_s[...] = _for_groups(ng, lru_group, hbuf_s[...], unroll=2 * UNROLL, stores=(_rows_of(yl_ref),))

    def part(p):
        return pl.BlockSpec((None, tb, lw), lambda c, i: (2 * p + c // per_chunk, i, c % per_chunk))

    strip = pl.BlockSpec((tb, lw), lambda c, i: (i, c))
    small = pltpu.VMEM((SUBLANES, lw), F32)
    return pl.pallas_call(
        body, grid=(ns, nt),
        in_specs=[part(p) for p in range(N_PARTS)] + [
            pl.BlockSpec((PV_ROWS, lw), lambda c, i: (0, c)),
            pl.BlockSpec((1, lw, 2 * lw), lambda c, i: (c, 0, 0)),
            pl.BlockSpec(w_out.shape, lambda c, i: (0, 0))],
        out_specs=(strip,) * 6 + (pl.BlockSpec(memory_space=pl.ANY),),
        out_shape=(jax.ShapeDtypeStruct((t, D_PART), MXU_DTYPE),) * 2 + (jax.ShapeDtypeStruct((t, D_PART), F32),) * 4 + (
            jax.ShapeDtypeStruct((N_CHIPS,) + w_out.shape, MXU_DTYPE),),
        scratch_shapes=[small, small, small, small, pltpu.VMEM((tb, lw), MXU_DTYPE),
                        pltpu.VMEM((tb, 2 * lw), F32), pltpu.VMEM(w_out.shape, MXU_DTYPE),
                        pltpu.SemaphoreType.DMA, pltpu.SemaphoreType.DMA((6,)), pltpu.SemaphoreType.DMA((6,))],
        compiler_params=_cp(ARB, ARB), name="mixer_forward",
    )(proj, proj, proj, proj, proj, proj, pvec, wai, w_out)


def _mixer_backward(proj, h, u, r, ig, dy, pvec, wai, sb_out):
    t = proj.shape[1]
    tb = 1024
    ng = tb // SUBLANES
    nt = t // tb
    gpb = tb // SUBLANES

    def body(bg_ref, cg_ref, xc_ref, gc_ref, xl_ref, gl_ref, h_ref, u_ref, r_ref, ig_ref, dyc_ref, dyl_ref,
             cgh_ref, xch_ref, xlh_ref, hh_ref, pv_ref, wai_ref, so_ref,
             dp_ref, gw_ref, sv_ref, ro_ref,
             ls_s, ub_s, uce_s, xle_s, he_s, dgb_s, du_s, gbuf_s,
             acc_s, an_s, dvn_s, dun_s, send_sems, recv_sems):
        i = pl.program_id(1)
        first_block = i == nt - 1

        @pl.when((pl.program_id(0) == 0) & (i == 0))
        def _():
            for cp in _chip_block_copies(so_ref, ro_ref, 1, send_sems, recv_sems):
                cp.start()

        @pl.when((pl.program_id(0) == NS - 1) & (i == nt - 1))
        def _():
            for cp in _chip_block_copies(so_ref, ro_ref, 1, send_sems, recv_sems):
                cp.wait()

        @pl.when(i == 0)
        def _():
            acc_s[...] = jnp.zeros_like(acc_s)
            gw_ref[...] = jnp.zeros_like(gw_ref)
            an_s[...] = jnp.zeros_like(an_s)
            dvn_s[...] = jnp.zeros_like(dvn_s)
            dun_s[...] = jnp.zeros_like(dun_s)
            gbuf_s[...] = jnp.zeros_like(gbuf_s)

        row = lax.broadcasted_iota(jnp.int32, (SUBLANES, LW), 0)
        ls_s[...] = RG_LRU_C * _log_sigmoid(_pvb(pv_ref, PV_LAM))
        keep = jnp.where(first_block, 0.0, 1.0)
        uce_s[0:SUBLANES, :] = (cgh_ref[...] * xch_ref[...]) * keep
        xle_s[0:SUBLANES, :] = xlh_ref[...] * keep
        he_s[0:SUBLANES, :] = hh_ref[...] * keep
        xle_s[SUBLANES:SUBLANES + tb, :] = xl_ref[...]
        he_s[SUBLANES:SUBLANES + tb, :] = h_ref[...]

        uce_s[SUBLANES:SUBLANES + tb, :] = cg_ref[...] * xc_ref[...]

        def acc_add(k, v):
            acc_s[k] += v

        def main_group(gi, carry):
            a_next, dv_next, g_next = carry
            g = ng - 1 - gi
            r0 = pl.multiple_of(g * SUBLANES, SUBLANES)
            sl = pl.ds(r0, SUBLANES)
            sl_e = pl.ds(r0 + SUBLANES, SUBLANES)
            lsb = ls_s[...]
            u = u_ref[sl, :]
            r = r_ref[sl, :]
            ig = ig_ref[sl, :]
            a, e2, mult, inv_mult = _decay(r, lsb)
            gl = gl_ref[sl, :]
            sg = _sigmoid(gl)
            s_l = gl * sg
            h8 = he_s[sl_e, :]
            hprev = _shift_down(h8, he_s[sl, :], 1, row)
            rr = lax.rsqrt(_head_mean(h8 * h8, LRU_HEAD) + RMS_EPS)
            n = h8 * rr
            dz = dyl_ref[sl, :]
            lg = _pvb(pv_ref, PV_LG)
            acc_add(PV_LG, (dz * n) * s_l)
            p5 = ((dz * n) * lg) * (sg + s_l * (1.0 - sg))
            dn = (dz * lg) * s_l
            dh = rr * (dn - n * _head_mean(dn * n, LRU_HEAD))
            A, B = _scan8_rev(_shift_up(a, a_next, 1, row), dh, row)
            gg = B + A * jnp.broadcast_to(g_next[0:1, :], (SUBLANES, LW))
            da = gg * hprev
            iu = ig * u
            diu = gg * mult
            dla = da * a - (gg * iu) * (e2 * inv_mult)
            acc_add(PV_LAM, dla * r)
            dra = (dla * lsb) * (r * (1.0 - r))
            dia = (diu * u) * (ig * (1.0 - ig))
            acc_add(PV_BA, dra)
            acc_add(PV_BI, dia)
            du_s[sl, :] = diu * ig
            bg = bg_ref[sl, :]
            gc = gc_ref[sl, :]
            uc = uce_s[sl_e, :]
            ucp = uce_s[sl, :]
            uc1 = _shift_down(uc, ucp, 1, row)
            uc2 = _shift_down(uc, ucp, 2, row)
            v = _conv3(pv_ref, uc, uc1, uc2)
            yc = bg * v
            rrc = lax.rsqrt(_head_mean(yc * yc, CONV_HEAD) + RMS_EPS)
            nc = yc * rrc
            sgc = _sigmoid(gc)
            s_c = gc * sgc
            dzc = dyc_ref[sl, :]
            cgain = _pvb(pv_ref, PV_CG)
            acc_add(PV_CG, (dzc * nc) * s_c)
            p3 = ((dzc * nc) * cgain) * (sgc + s_c * (1.0 - sgc))
            dnc = (dzc * cgain) * s_c
            dyc = rrc * (dnc - nc * _head_mean(dnc * nc, CONV_HEAD))
            dv = dyc * bg
            duc = (_pvb(pv_ref, PV_CONV_W + 2) * dv + _pvb(pv_ref, PV_CONV_W + 1) * _shift_up(dv, dv_next, 1, row)
                   + _pvb(pv_ref, PV_CONV_W) * _shift_up(dv, dv_next, 2, row))
            acc_add(PV_CONV_W + 2, dv * uc)
            acc_add(PV_CONV_W + 1, dv * uc1)
            acc_add(PV_CONV_W, dv * uc2)
            return (a, dv, gg), (dyc * v, duc * xc_ref[sl, :], duc * cg_ref[sl, :], p3, p5, dra, dia, u)

        a_next, dv_next, g_next = _for_groups(
            ng, main_group, (an_s[...], dvn_s[...], gbuf_s[...]), descending=True,
            stores=(_rows_of(dp_ref, 0), _rows_of(dp_ref, 1), _rows_of(dp_ref, 2), _rows_of(dp_ref, 3), _rows_of(dp_ref, 5),
                    _rows_of(dgb_s, cols=slice(0, LW)), _rows_of(dgb_s, cols=slice(LW, 2 * LW)), _rows_of(ub_s)))
        an_s[...] = a_next
        dvn_s[...] = dv_next
        gbuf_s[...] = g_next

        dgb = dgb_s[...]
        du_s[...] += _mm_nt(dgb, wai_ref[0])
        gw_ref[0] += _mm_tn(ub_s[...], dgb)

        def lru_conv_group(gi, du_next):
            g = ng - 1 - gi
            r0 = pl.multiple_of(g * SUBLANES, SUBLANES)
            sl = pl.ds(r0, SUBLANES)
            du = du_s[sl, :]
            xl = xle_s[pl.ds(r0 + SUBLANES, SUBLANES), :]
            xlp = xle_s[sl, :]
            acc_add(PV_LRU_B, du)
            acc_add(PV_LRU_W + 3, du * xl)
            acc_add(PV_LRU_W + 2, du * _shift_down(xl, xlp, 1, row))
            acc_add(PV_LRU_W + 1, du * _shift_down(xl, xlp, 2, row))
            acc_add(PV_LRU_W, du * _shift_down(xl, xlp, 3, row))
            dxl = (((_pvb(pv_ref, PV_LRU_W + 3) * du + _pvb(pv_ref, PV_LRU_W + 2) * _shift_up(du, du_next, 1, row))
                    + _pvb(pv_ref, PV_LRU_W + 1) * _shift_up(du, du_next, 2, row))
                   + _pvb(pv_ref, PV_LRU_W) * _shift_up(du, du_next, 3, row))
            return du, (dxl,)

        dun_s[...] = _for_groups(ng, lru_conv_group, dun_s[...], descending=True, stores=(_rows_of(dp_ref, 4),))

        @pl.when(first_block)
        def _():
            sv_ref[...] = jnp.zeros_like(sv_ref)
            for k in range(N_ACC):
                tot = jnp.sum(acc_s[k], axis=0, keepdims=True)
                if k == PV_LAM:
                    tot = (RG_LRU_C * tot) / (1.0 + jnp.exp(pv_ref[PV_LAM:PV_LAM + 1, :]))
                sv_ref[k:k + 1, :] = tot

    def part(p):
        return pl.BlockSpec((None, tb, LW), lambda c, i: (2 * p + c // STRIPS_PER_CHUNK, nt - 1 - i, c % STRIPS_PER_CHUNK))

    def halo(p):
        return pl.BlockSpec((None, SUBLANES, LW), lambda c, i: (2 * p + c // STRIPS_PER_CHUNK,
                                                                jnp.maximum((nt - 1 - i) * gpb - 1, 0), c % STRIPS_PER_CHUNK))

    strip = pl.BlockSpec((tb, LW), lambda c, i: (nt - 1 - i, c))
    big = pltpu.VMEM((tb, LW), F32)
    big_e = pltpu.VMEM((tb + SUBLANES, LW), F32)
    small = pltpu.VMEM((SUBLANES, LW), F32)
    outs = pl.pallas_call(
        body, grid=(NS, nt),
        in_specs=[part(p) for p in range(N_PARTS)] + [
            strip, strip, strip, strip, strip, pl.BlockSpec((tb, LW), lambda c, i: (nt - 1 - i, NS + c)),
            halo(1), halo(2), halo(4),
            pl.BlockSpec((SUBLANES, LW), lambda c, i: (jnp.maximum((nt - 1 - i) * gpb - 1, 0), c)),
            pl.BlockSpec((PV_ROWS, LW), lambda c, i: (0, c)),
            pl.BlockSpec((1, LW, 2 * LW), lambda c, i: (c, 0, 0)),
            pl.BlockSpec(memory_space=pl.ANY)],
        out_specs=(pl.BlockSpec((N_PARTS, tb, LW), lambda c, i: (0, nt - 1 - i, c)),
                   pl.BlockSpec((1, LW, 2 * LW), lambda c, i: (c, 0, 0)),
                   pl.BlockSpec((PV_ROWS, LW), lambda c, i: (0, c)),
                   pl.BlockSpec(memory_space=pl.ANY)),
        out_shape=(jax.ShapeDtypeStruct((N_PARTS, t, D_PART), MXU_DTYPE),
                   jax.ShapeDtypeStruct((NS, LW, 2 * LW), F32), jax.ShapeDtypeStruct((PV_ROWS, D_PART), F32),
                   _chip_blocks_shape(sb_out, 1)),
        scratch_shapes=[small, pltpu.VMEM((tb, LW), MXU_DTYPE), big_e, big_e, big_e,
                        pltpu.VMEM((tb, 2 * LW), MXU_DTYPE), big, small,
                        pltpu.VMEM((N_ACC, SUBLANES, LW), F32), small, small, small,
                        pltpu.SemaphoreType.DMA((3,)), pltpu.SemaphoreType.DMA((3,))],
        compiler_params=_cp(ARB, ARB), name="mixer_backward",
    )(proj, proj, proj, proj, proj, proj, h, u, r, ig, dy, dy, proj, proj, proj, h, pvec, wai, sb_out)
    return outs


def _adamw(w, g, m, v):
    m = ADAM_B1 * m + (1.0 - ADAM_B1) * g
    v = ADAM_B2 * v + (1.0 - ADAM_B2) * (g * g)
    m_hat = m / (1.0 - ADAM_B1 ** ADAM_STEP)
    v_hat = v / (1.0 - ADAM_B2 ** ADAM_STEP)
    delta = -ADAM_LR * (m_hat / (jnp.sqrt(v_hat) + ADAM_EPS) + ADAM_WD * w)
    return delta, m, v


def _adam_w_in(w, m, v, g3):
    rows, cols = w.shape
    tr = 128

    def body(w_ref, m_ref, v_ref, g_ref, go_ref, d_ref, mo_ref, vo_ref):
        for s in range(CHUNKS_PER_BLOCK):
            cs = slice(CHUNK * s, CHUNK * (s + 1))
            g = g_ref[s]
            d, mn, vn = _adamw(w_ref[:, cs], g, m_ref[:, cs], v_ref[:, cs])
            go_ref[:, cs] = g
            d_ref[:, cs] = d
            mo_ref[:, cs] = mn
            vo_ref[:, cs] = vn

    blk = pl.BlockSpec((tr, cols), lambda i: (i, 0))
    return pl.pallas_call(
        body, grid=(rows // tr,),
        in_specs=[blk, blk, blk, pl.BlockSpec((CHUNKS_PER_BLOCK, tr, CHUNK), lambda i: (0, i, 0))],
        out_specs=(blk,) * 4, out_shape=(jax.ShapeDtypeStruct(w.shape, F32),) * 4,
        compiler_params=_cp(ARB), name="adam_w_in",
    )(w, m, v, g3)


def _adam_w_out(w, m, v, g):
    rows, cols = w.shape
    tr = 128

    def body(w_ref, m_ref, v_ref, g_ref, d_ref, mo_ref, vo_ref):
        d_ref[...], mo_ref[...], vo_ref[...] = _adamw(w_ref[...], g_ref[...], m_ref[...], v_ref[...])

    blk = pl.BlockSpec((tr, cols), lambda i: (i, 0))
    return pl.pallas_call(
        body, grid=(rows // tr,), in_specs=[blk] * 4, out_specs=(blk,) * 3,
        out_shape=(jax.ShapeDtypeStruct(w.shape, F32),) * 3,
        compiler_params=_cp(ARB), name="adam_w_out",
    )(w, m, v, g)


def _adam_small(ws, ms, vs, gs):
    n = len(ws)

    def body(*refs):
        w_r, m_r, v_r, g_r = refs[0:n], refs[n:2 * n], refs[2 * n:3 * n], refs[3 * n:4 * n]
        d_o, m_o, v_o = refs[4 * n:5 * n], refs[5 * n:6 * n], refs[6 * n:7 * n]
        for j in range(n):
            d_o[j][...], m_o[j][...], v_o[j][...] = _adamw(w_r[j][...], g_r[j][...], m_r[j][...], v_r[j][...])

    vm = pl.BlockSpec(memory_space=pltpu.VMEM)
    shapes = tuple(jax.ShapeDtypeStruct(w.shape, F32) for w in ws)
    outs = pl.pallas_call(
        body, in_specs=[vm] * (4 * n), out_specs=(vm,) * (3 * n), out_shape=shapes * 3,
        compiler_params=_cp(), name="adam_small",
    )(*ws, *ms, *vs, *gs)
    return outs[0:n], outs[n:2 * n], outs[2 * n:3 * n]


def _block_diag_strips(w, lw):
    heads = lw // LRU_HEAD
    w4 = w.reshape(D_PART // lw, heads, LRU_HEAD, LRU_HEAD)
    rows = [jnp.pad(w4[:, hh], ((0, 0), (0, 0), (LRU_HEAD * hh, lw - LRU_HEAD * (hh + 1)))) for hh in range(heads)]
    return jnp.concatenate(rows, axis=1)


def _gate_matrices(w_a, w_i, lw):
    return jnp.concatenate([_block_diag_strips(w_a, lw), _block_diag_strips(w_i, lw)], axis=2).astype(MXU_DTYPE)


def _strip_diag_blocks(g):
    g5 = g.reshape(NS, HEADS_PER_STRIP, LRU_HEAD, HEADS_PER_STRIP, LRU_HEAD)
    return jnp.stack([g5[:, hh, :, hh, :] for hh in range(HEADS_PER_STRIP)], axis=1).reshape(NS * HEADS_PER_STRIP, LRU_HEAD, LRU_HEAD)


def kernel(x, ln_g, w_in, conv_w, lru_conv_w, lru_conv_b, w_a, b_a, w_i, b_i, lam, conv_out_g, lru_out_g, w_out, final_g, loss_target, m_ln_g, m_w_in, m_conv_w, m_lru_conv_w, m_lru_conv_b, m_w_a, m_b_a, m_w_i, m_b_i, m_lam, m_conv_out_g, m_lru_out_g, m_w_out, m_final_g, v_ln_g, v_w_in, v_conv_w, v_lru_conv_w, v_lru_conv_b, v_w_a, v_b_a, v_w_i, v_b_i, v_lam, v_conv_out_g, v_lru_out_g, v_w_out, v_final_g):
    xi, yi, ci = lax.axis_index("x"), lax.axis_index("y"), lax.axis_index("c")
    k = 2 * xi + yi
    t = x.shape[1]
    x2 = x.reshape(t, D_MODEL)
    tgt2 = loss_target.reshape(t, D_MODEL)
    row = lambda a: a.reshape(1, -1)

    small = jnp.concatenate([conv_w, lru_conv_w, jnp.zeros((1, conv_w.shape[1]), F32)], axis=0)
    proj, xn, w12, sm4 = _gather_in_projection(x2, row(ln_g), w_in, small)
    convs = jnp.transpose(sm4, (1, 0, 2)).reshape(SUBLANES, D_PART)
    pvec = jnp.concatenate(
        [convs[0:7], row(lru_conv_b), row(b_a), row(b_i), row(lam), row(conv_out_g), row(lru_out_g),
         jnp.zeros((PV_ROWS - N_ACC, D_PART), F32)], axis=0)
    wai = _gate_matrices(w_a, w_i, LW)

    c_arr = jnp.reshape(ci, (1,)).astype(jnp.int32)
    kc_arr = jnp.stack([k, ci]).astype(jnp.int32)
    yc, yl, h, u, r, ig, wo4 = _mixer_forward(proj, pvec, _gate_matrices(w_a, w_i, FWD_LW), w_out)
    wo = wo4.reshape(2 * D_PART, D_MODEL)
    do, dob, dy, st_out = _out_projection_loss(yc, yl, x2, tgt2, wo, row(final_g))
    go4, go4b = _w_out_grad(yc, yl, dob)
    s_out, sb_out = _add_sibling_halves(go4, go4b, c_arr, "add_sibling_halves_out")
    dproj, g_wai, svec, r2o = _mixer_backward(proj, h, u, r, ig, dy, pvec, wai, sb_out)
    gwa = _strip_diag_blocks(g_wai[:, :, 0:LW]).reshape(LRU_HEAD, D_PART)
    gwi = _strip_diag_blocks(g_wai[:, :, LW:2 * LW]).reshape(LRU_HEAD, D_PART)
    g12, g12b, red = _w_in_grad(xn, dproj, jnp.concatenate([svec, st_out, gwa, gwi], axis=0))
    s_in, sb_in = _add_sibling_halves(g12, g12b, c_arr, "add_sibling_halves_in")
    grad_x, st_in, r2i = _input_grad(dproj, w12, x2, do, row(ln_g), sb_in)
    f_in = _sum_chip_blocks(s_in, r2i, kc_arr, CHUNKS_PER_BLOCK, "sum_chip_blocks_in")
    f_out = _sum_chip_blocks(s_out, r2o, kc_arr, 1, "sum_chip_blocks_out")
    f_in, f_out, red_ln = _swap_halves_and_sum(f_in, f_out, st_in)
    r_out = PV_ROWS
    r_wa = PV_ROWS + SUBLANES
    r_wi = r_wa + LRU_HEAD
    loss = red[r_out + 1, 0]

    g_w_in, d_w_in, nm_w_in, nv_w_in = _adam_w_in(w_in, m_w_in, v_w_in, f_in)
    g_w_out = f_out[0]
    d_w_out, nm_w_out, nv_w_out = _adam_w_out(w_out, m_w_out, v_w_out, g_w_out)

    ncol = conv_w.shape[1]
    conv_cols = lax.dynamic_slice(red, (0, k * ncol), (SUBLANES, ncol))
    g_small = {
        "ln_g": red_ln[0], "conv_w": conv_cols[0:3], "lru_conv_w": conv_cols[3:7], "lru_conv_b": red[PV_LRU_B],
        "w_a": red[r_wa:r_wa + LRU_HEAD].reshape(w_a.shape), "b_a": red[PV_BA],
        "w_i": red[r_wi:r_wi + LRU_HEAD].reshape(w_i.shape), "b_i": red[PV_BI], "lam": red[PV_LAM],
        "conv_out_g": red[PV_CG], "lru_out_g": red[PV_LG], "final_g": red[r_out],
    }
    w_small = {"ln_g": ln_g, "conv_w": conv_w, "lru_conv_w": lru_conv_w, "lru_conv_b": lru_conv_b, "w_a": w_a, "b_a": b_a,
               "w_i": w_i, "b_i": b_i, "lam": lam, "conv_out_g": conv_out_g, "lru_out_g": lru_out_g, "final_g": final_g}
    m_small = {"ln_g": m_ln_g, "conv_w": m_conv_w, "lru_conv_w": m_lru_conv_w, "lru_conv_b": m_lru_conv_b, "w_a": m_w_a,
               "b_a": m_b_a, "w_i": m_w_i, "b_i": m_b_i, "lam": m_lam, "conv_out_g": m_conv_out_g,
               "lru_out_g": m_lru_out_g, "final_g": m_final_g}
    v_small = {"ln_g": v_ln_g, "conv_w": v_conv_w, "lru_conv_w": v_lru_conv_w, "lru_conv_b": v_lru_conv_b, "w_a": v_w_a,
               "b_a": v_b_a, "w_i": v_w_i, "b_i": v_b_i, "lam": v_lam, "conv_out_g": v_conv_out_g,
               "lru_out_g": v_lru_out_g, "final_g": v_final_g}
    names = list(w_small)
    as2d = lambda a: a.reshape(1, -1) if a.ndim == 1 else a
    d_s, m_s, v_s = _adam_small([as2d(w_small[n]) for n in names], [as2d(m_small[n]) for n in names],
                                [as2d(v_small[n]) for n in names], [as2d(g_small[n]) for n in names])
    back = lambda n, a: a.reshape(w_small[n].shape)
    grads = {n: g_small[n] for n in names}
    deltas = {n: back(n, a) for n, a in zip(names, d_s)}
    new_m = {n: back(n, a) for n, a in zip(names, m_s)}
    new_v = {n: back(n, a) for n, a in zip(names, v_s)}
    grads["w_in"], deltas["w_in"], new_m["w_in"], new_v["w_in"] = g_w_in, d_w_in, nm_w_in, nv_w_in
    grads["w_out"], deltas["w_out"], new_m["w_out"], new_v["w_out"] = g_w_out, d_w_out, nm_w_out, nv_w_out

    order = ["ln_g", "w_in", "conv_w", "lru_conv_w", "lru_conv_b", "w_a", "b_a", "w_i", "b_i", "lam", "conv_out_g",
             "lru_out_g", "w_out", "final_g"]
    return (loss, grad_x.reshape(x.shape), *[grads[n] for n in order], *[deltas[n] for n in order],
            *[new_m[n] for n in order], *[new_v[n] for n in order])
```

```python
import functools

import jax
import jax.numpy as jnp
from jax import lax
from jax.experimental import pallas as pl
from jax.experimental.pallas import tpu as pltpu

F32 = jnp.float32
MXU_DTYPE = jnp.bfloat16

D_MODEL = 1024
D_PART = 1024
N_PARTS = 6
CHUNK = 512
CHUNKS_PER_BLOCK = 3
N_CHUNKS = 12
N_CHIPS = 4
SUBLANES = 8
LANES = 128
LW = 256
FWD_LW = 512
UNROLL = 8
NS = D_PART // LW
STRIPS_PER_CHUNK = CHUNK // LW
CONV_HEAD = 128
LRU_HEAD = 64
HEADS_PER_STRIP = LW // LRU_HEAD
RMS_EPS = 1e-6
RG_LRU_C = 8.0
ADAM_LR = 0.001
ADAM_B1 = 0.9
ADAM_B2 = 0.999
ADAM_EPS = 1e-08
ADAM_WD = 0.01
ADAM_STEP = 10

PV_CONV_W = 0
PV_LRU_W = 3
PV_LRU_B = 7
PV_BA = 8
PV_BI = 9
PV_LAM = 10
PV_CG = 11
PV_LG = 12
PV_ROWS = 16
N_ACC = 13

SLAB = 128
MESH = pl.DeviceIdType.MESH
VMEM_LIMIT = 56 * 1024 * 1024
ARB = "arbitrary"


def _cp(*sem, **kw):
    return pltpu.CompilerParams(dimension_semantics=sem or None, vmem_limit_bytes=VMEM_LIMIT, **kw)


def _mm(a, b):
    return jnp.dot(a, b, preferred_element_type=F32)


def _mm_nt(a, b):
    return lax.dot_general(a, b, (((1,), (1,)), ((), ())), preferred_element_type=F32)


def _mm_tn(a, b):
    return lax.dot_general(a, b, (((0,), (0,)), ((), ())), preferred_element_type=F32)


def _sigmoid(x):
    return 0.5 * jnp.tanh(0.5 * x) + 0.5


def _log_sigmoid(x):
    z = jnp.exp(-jnp.abs(x))
    u = 1.0 + z
    log1p = jnp.where(u == 1.0, z, jnp.log(u) * z / (u - 1.0))
    return jnp.minimum(x, 0.0) - log1p


def _head_mean(z, head):
    out = []
    for k in range(z.shape[1] // LANES):
        zk = z[:, LANES * k:LANES * (k + 1)]
        if head == LANES:
            m = jnp.sum(zk, axis=-1, keepdims=True) * (1.0 / head)
            out.append(jnp.broadcast_to(m, zk.shape))
        else:
            lo = lax.broadcasted_iota(jnp.int32, zk.shape, 1) < head
            s_lo = jnp.sum(jnp.where(lo, zk, 0.0), axis=-1, keepdims=True)
            s_hi = jnp.sum(jnp.where(lo, 0.0, zk), axis=-1, keepdims=True)
            out.append(jnp.where(lo, s_lo, s_hi) * (1.0 / head))
    return jnp.concatenate(out, axis=1)


def _shift_down(cur, prev, d, row):
    return pltpu.roll(jnp.where(row < SUBLANES - d, cur, prev), d, 0)


def _shift_up(cur, nxt, d, row):
    return pltpu.roll(jnp.where(row >= d, cur, nxt), SUBLANES - d, 0)


def _scan8_fwd(a, b, row):
    A, B = a, b
    for d in (1, 2, 4):
        m = row >= d
        a_s = jnp.where(m, pltpu.roll(A, d, 0), 1.0)
        b_s = jnp.where(m, pltpu.roll(B, d, 0), 0.0)
        B = A * b_s + B
        A = A * a_s
    return A, B


def _scan8_rev(a, b, row):
    A, B = a, b
    for d in (1, 2, 4):
        m = row < SUBLANES - d
        a_s = jnp.where(m, pltpu.roll(A, SUBLANES - d, 0), 1.0)
        b_s = jnp.where(m, pltpu.roll(B, SUBLANES - d, 0), 0.0)
        B = A * b_s + B
        A = A * a_s
    return A, B


def _decay(r, ls8):
    la = r * ls8
    a = jnp.exp(la)
    e2 = a * a
    em = -jnp.tanh(la) * (1.0 + e2)
    inv_mult = lax.rsqrt(em)
    return a, e2, em * inv_mult, inv_mult


def _mesh_pos():
    x, y, c = lax.axis_index("x"), lax.axis_index("y"), lax.axis_index("c")
    chips = [(1 - x, y), (x, 1 - y), (1 - x, 1 - y)]
    return x, y, c, chips


def _gather_in_projection(x, ln_g, w_in, small):
    t = x.shape[0]
    rb_x = 512
    rb_mm = 2048
    n_mm = t // rb_mm
    half = w_in.shape[0] // 2

    def body(x_hbm, g_ref, wi_ref, sm_ref, proj_hbm, xn_ref, w12_ref, sm4_ref,
             xbuf, obuf, x_sems, o_sems, send_sems, recv_sems):
        x_, y_, c, chips = _mesh_pos()
        k = 2 * x_ + y_
        sib = (x_, y_, 1 - c)
        sm4_ref[k] = sm_ref[...]

        def remote(ref, sem, to):
            return pltpu.make_async_remote_copy(src_ref=ref, dst_ref=ref, send_sem=send_sems.at[sem],
                                                recv_sem=recv_sems.at[sem], device_id=to, device_id_type=MESH)

        def chunk_of(chip, s):
            return CHUNKS_PER_BLOCK * (2 * chip[0] + chip[1]) + s

        def piece(q, core, first=0, rows=half):
            return w12_ref.at[q, pl.ds(pl.multiple_of(half * core + first, SUBLANES * 2), rows), :]

        nbr_x, nbr_y, diagonal = chips
        quarter = half // 2
        DIAG = [(0, 0, half, 0), (1, 0, quarter, 0), (1, quarter, quarter, 1), (2, 0, half, 1)]
        ici = lambda m, s: 2 * s + m
        dgn = lambda j: 6 + j
        to_sib = 10
        sml = lambda m: 20 + m

        sends = []
        for s in range(CHUNKS_PER_BLOCK):
            w12_ref[chunk_of((x_, y_), s)] = wi_ref[:, CHUNK * s:CHUNK * (s + 1)].astype(MXU_DTYPE)
            for m, chip in enumerate((nbr_x, nbr_y)):
                sends.append(remote(piece(chunk_of((x_, y_), s), c), ici(m, s), (*chip, c)))
                sends[-1].start()
        for m, chip in enumerate(chips):
            sends.append(remote(sm4_ref.at[k], sml(m), (*chip, c)))
            sends[-1].start()

        def x_copy(rb, slot):
            return pltpu.make_async_copy(x_hbm.at[pl.ds(rb * rb_x, rb_x), :], xbuf.at[slot], x_sems.at[slot])

        x_copy(0, 0).start()
        for rb in range(t // rb_x):
            slot = rb % 2
            x_copy(rb, slot).wait()
            if rb + 1 < t // rb_x:
                x_copy(rb + 1, 1 - slot).start()

            def norm_slab(sl, carry, rb=rb, slot=slot):
                xf = xbuf[slot, pl.ds(pl.multiple_of(sl * SLAB, SLAB), SLAB), :]
                r = lax.rsqrt(jnp.mean(xf * xf, axis=-1, keepdims=True) + RMS_EPS)
                xn_ref[pl.ds(pl.multiple_of(rb * rb_x + sl * SLAB, SLAB), SLAB), :] = ((xf * r) * g_ref[...]).astype(MXU_DTYPE)
                return carry

            lax.fori_loop(0, rb_x // SLAB, norm_slab, 0)

        def out_copy(q, i):
            return pltpu.make_async_copy(obuf.at[i], proj_hbm.at[q, pl.ds(pl.multiple_of(i * rb_mm, rb_mm), rb_mm), :],
                                         o_sems.at[i])

        def project(q, very_first):
            def row_block(i, carry):
                if not very_first:
                    out_copy(q, i).wait()
                obuf[i] = _mm(xn_ref[pl.ds(pl.multiple_of(i * rb_mm, rb_mm), rb_mm), :], w12_ref[q])
                out_copy(q, i).start()
                return carry

            lax.fori_loop(0, n_mm, row_block, 0)

        for s in range(CHUNKS_PER_BLOCK):
            project(chunk_of((x_, y_), s), very_first=(s == 0))

        steps = []
        for s in range(CHUNKS_PER_BLOCK):
            for m, chip in enumerate((nbr_x, nbr_y)):
                onward = [(first, rows, dgn(j), chips[via]) for j, (cs, first, rows, via) in enumerate(DIAG)
                          if cs == s and via == 1 - m]
                steps.append((chunk_of(chip, s), [(0, half, ici(m, s))], onward))
        for s in range(CHUNKS_PER_BLOCK):
            steps.append((chunk_of(diagonal, s), [(first, rows, dgn(j)) for j, (cs, first, rows, _) in enumerate(DIAG) if cs == s], []))

        def project_when_whole(step):
            q, pieces, _ = step
            for first, rows, sem in pieces:
                remote(piece(q, 1 - c, first, rows), to_sib + sem, sib).wait_recv()
            project(q, very_first=False)

        passed = []
        for j, (q, pieces, onward) in enumerate(steps):
            for first, rows, sem in pieces:
                remote(piece(q, c, first, rows), sem, sib).wait_recv()
            for first, rows, sem, chip in onward:
                passed.append(remote(piece(q, c, first, rows), sem, (*chip, c)))
                passed[-1].start()
            for first, rows, sem in pieces:
                passed.append(remote(piece(q, c, first, rows), to_sib + sem, sib))
                passed[-1].start()
            if j > 0:
                project_when_whole(steps[j - 1])
        project_when_whole(steps[-1])

        for m, chip in enumerate(chips):
            remote(sm4_ref.at[2 * chip[0] + chip[1]], sml(m), sib).wait_recv()
        for cp in sends + passed:
            cp.wait_send()
        for i in range(n_mm):
            out_copy(0, i).wait()

    vm = pl.BlockSpec(memory_space=pltpu.VMEM)
    hbm = pl.BlockSpec(memory_space=pl.ANY)
    n_sems = 23
    return pl.pallas_call(
        body,
        out_shape=(jax.ShapeDtypeStruct((N_CHUNKS, t, CHUNK), F32), jax.ShapeDtypeStruct((t, D_MODEL), MXU_DTYPE),
                   jax.ShapeDtypeStruct((N_CHUNKS, w_in.shape[0], CHUNK), MXU_DTYPE),
                   jax.ShapeDtypeStruct((N_CHIPS,) + small.shape, F32)),
        in_specs=[hbm, vm, vm, vm], out_specs=(hbm, vm, vm, vm),
        scratch_shapes=[pltpu.VMEM((2, rb_x, D_MODEL), F32), pltpu.VMEM((n_mm, rb_mm, CHUNK), F32),
                        pltpu.SemaphoreType.DMA((2,)), pltpu.SemaphoreType.DMA((n_mm,)),
                        pltpu.SemaphoreType.DMA((n_sems,)), pltpu.SemaphoreType.DMA((n_sems,))],
        compiler_params=_cp(), name="gather_in_projection",
    )(x, ln_g, w_in, small)


def _allreduce_behind(step, when, in_ref, acc_s, rbufs, out_ref, send_sems, recv_sems):
    x, y, c, _ = _mesh_pos()
    peers = [(x, y, 1 - c), (1 - x, y, c), (x, 1 - y, c)]

    def exchange(ph):
        return pltpu.make_async_remote_copy(src_ref=acc_s, dst_ref=rbufs[ph], send_sem=send_sems.at[ph],
                                            recv_sem=recv_sems.at[ph], device_id=peers[ph], device_id_type=MESH)

    @pl.when(step == when[0])
    def _():
        acc_s[...] = in_ref[...]
        exchange(0).start()

    for ph in (1, 2):
        @pl.when(step == when[ph])
        def _(ph=ph):
            exchange(ph - 1).wait()
            acc_s[...] = acc_s[...] + rbufs[ph - 1][...]
            exchange(ph).start()

    @pl.when(step == when[3])
    def _():
        exchange(2).wait()
        out_ref[...] = acc_s[...] + rbufs[2][...]


def _add_sibling_halves(g, gb, c_arr, name):
    n, rows, cols = g.shape
    half = rows // 2

    def body(c_ref, g_ref, gb_hbm, o_ref, ob_ref, rbuf, send_sems, recv_sems):
        q = pl.program_id(0)
        x, y, c, _ = _mesh_pos()
        theirs = pl.ds(pl.multiple_of(half * (1 - c), half), half)

        def copy(j):
            return pltpu.make_async_remote_copy(src_ref=gb_hbm.at[j, theirs, :], dst_ref=rbuf.at[j], send_sem=send_sems.at[j],
                                                recv_sem=recv_sems.at[j], device_id=(x, y, 1 - c), device_id_type=MESH)

        @pl.when(q == 0)
        def _():
            for j in range(n):
                copy(j).start()

        copy(q).wait_recv()
        s = g_ref[0] + rbuf[q].astype(F32)
        o_ref[0] = s
        ob_ref[0] = s.astype(jnp.bfloat16)

        @pl.when(q == n - 1)
        def _():
            for j in range(n):
                copy(j).wait_send()

    blk = pl.BlockSpec((1, half, cols), lambda q, c_ref: (q, 0, 0))
    return pl.pallas_call(
        body, out_shape=(jax.ShapeDtypeStruct((n, half, cols), F32), jax.ShapeDtypeStruct((n, half, cols), jnp.bfloat16)),
        grid_spec=pltpu.PrefetchScalarGridSpec(
            num_scalar_prefetch=1, grid=(n,),
            in_specs=[pl.BlockSpec((1, half, cols), lambda q, c_ref: (q, c_ref[0], 0)), pl.BlockSpec(memory_space=pl.ANY)],
            out_specs=(blk, blk),
            scratch_shapes=[pltpu.VMEM((n, half, cols), jnp.bfloat16), pltpu.SemaphoreType.DMA((n,)),
                            pltpu.SemaphoreType.DMA((n,))]),
        compiler_params=_cp(ARB), name=name,
    )(c_arr, g, gb)


def _chip_block_copies(s_ref, r_ref, n_sub, send_sems, recv_sems):
    x, y, c, chips = _mesh_pos()
    cps = []
    for m, chip in enumerate(chips):
        kk = 2 * chip[0] + chip[1]
        cps.append(pltpu.make_async_remote_copy(
            src_ref=s_ref.at[pl.ds(n_sub * kk, n_sub)], dst_ref=r_ref.at[m],
            send_sem=send_sems.at[m], recv_sem=recv_sems.at[m], device_id=(*chip, c), device_id_type=MESH))
    return cps


def _gather_w_out(step, n_steps, wo_ref, wob_s, wo4_ref, local_sem, send_sems, recv_sems):
    x, y, c, chips = _mesh_pos()
    sib = (x, y, 1 - c)
    half = wo_ref.shape[0] // 2

    def rows(core):
        return pl.ds(pl.multiple_of(half * core, half), half)

    def block_half(chip, core):
        return wo4_ref.at[2 * chip[0] + chip[1], rows(core), :]

    def remote(src, dst, sem, to):
        return pltpu.make_async_remote_copy(src_ref=src, dst_ref=dst, send_sem=send_sems.at[sem], recv_sem=recv_sems.at[sem],
                                            device_id=to, device_id_type=MESH)

    local = pltpu.make_async_copy(wob_s, wo4_ref.at[2 * x + y], local_sem)
    ici = [remote(wob_s.at[rows(c), :], block_half((x, y), c), m, (*chip, c)) for m, chip in enumerate(chips)]
    fwd = [remote(block_half(chip, c), block_half(chip, c), 3 + m, sib) for m, chip in enumerate(chips)]

    @pl.when(step == 0)
    def _():
        wob_s[...] = wo_ref[...].astype(MXU_DTYPE)
        local.start()
        for cp in ici:
            cp.start()

    @pl.when(step == n_steps // 2)
    def _():
        for m, chip in enumerate(chips):
            remote(block_half(chip, c), block_half(chip, c), m, sib).wait_recv()
            fwd[m].start()

    @pl.when(step == n_steps - 1)
    def _():
        for m, chip in enumerate(chips):
            remote(block_half(chip, 1 - c), block_half(chip, 1 - c), 3 + m, sib).wait_recv()
        for cp in ici + fwd:
            cp.wait_send()
        local.wait()


def _chip_blocks_shape(s, n_sub):
    return jax.ShapeDtypeStruct((3, n_sub) + s.shape[1:], s.dtype)


def _finish_gradients(s_in, r_in, s_out, r_out, v):
    n_dev = 8
    n_in, n_out = r_in.shape[1], r_out.shape[1]

    def body(si_hbm, ri_hbm, so_hbm, ro_hbm, v_ref, fi_hbm, fo_hbm, tot_ref,
             a_in, b_in, a_out, b_out, slots, load_sems, store_sems, send_sems, recv_sems):
        x, y, c, _ = _mesh_pos()
        k = 2 * x + y
        sib = (x, y, 1 - c)
        me = 4 * x + 2 * y + c
        loads = [pltpu.make_async_copy(si_hbm.at[pl.ds(n_in * k, n_in)], a_in, load_sems.at[0]),
                 pltpu.make_async_copy(ri_hbm, b_in, load_sems.at[1]),
                 pltpu.make_async_copy(so_hbm.at[pl.ds(n_out * k, n_out)], a_out, load_sems.at[2]),
                 pltpu.make_async_copy(ro_hbm, b_out, load_sems.at[3])]
        for cp in loads:
            cp.start()
        slots[me] = v_ref[...]

        def remote(src, dst, sem, to):
            return pltpu.make_async_remote_copy(src_ref=src, dst_ref=dst, send_sem=send_sems.at[sem],
                                                recv_sem=recv_sems.at[sem], device_id=to, device_id_type=MESH)

        small = []
        for d in range(1, n_dev):
            peer = (1 - x if d & 4 else x, 1 - y if d & 2 else y, 1 - c if d & 1 else c)
            small.append(remote(slots.at[me], slots.at[me], d - 1, peer))
            small[-1].start()
        for cp in loads:
            cp.wait()
        big = []
        for j, (a, b, f_hbm) in enumerate(((a_in, b_in, fi_hbm), (a_out, b_out, fo_hbm))):
            a[...] = ((a[...] + b[0].astype(F32)) + b[1].astype(F32)) + b[2].astype(F32)
            half = a.shape[1]
            mine = f_hbm.at[:, pl.ds(pl.multiple_of(half * c, half), half), :]
            big.append(pltpu.make_async_copy(a, mine, store_sems.at[j]))
            big.append(remote(a, mine, n_dev - 1 + j, sib))
        for cp in big:
            cp.start()
        for cp in small + big:
            cp.wait()
        total = slots[0]
        for dev in range(1, n_dev):
            total = total + slots[dev]
        tot_ref[...] = total

    hbm = pl.BlockSpec(memory_space=pl.ANY)
    vm = pl.BlockSpec(memory_space=pltpu.VMEM)
    full = lambda s, n: (n, 2 * s.shape[1], s.shape[2])
    return pl.pallas_call(
        body,
        out_shape=(jax.ShapeDtypeStruct(full(s_in, n_in), F32), jax.ShapeDtypeStruct(full(s_out, n_out), F32),
                   jax.ShapeDtypeStruct(v.shape, F32)),
        in_specs=[hbm, hbm, hbm, hbm, vm], out_specs=(hbm, hbm, vm),
        scratch_shapes=[pltpu.VMEM((n_in,) + s_in.shape[1:], F32), pltpu.VMEM(r_in.shape, r_in.dtype),
                        pltpu.VMEM((n_out,) + s_out.shape[1:], F32), pltpu.VMEM(r_out.shape, r_out.dtype),
                        pltpu.VMEM((n_dev,) + v.shape, F32), pltpu.SemaphoreType.DMA((4,)), pltpu.SemaphoreType.DMA((2,)),
                        pltpu.SemaphoreType.DMA((n_dev + 1,)), pltpu.SemaphoreType.DMA((n_dev + 1,))],
        compiler_params=_cp(), name="finish_gradients",
    )(s_in, r_in, s_out, r_out, v)


def _out_projection_loss(yc, yl, x, target, wo, final_g):
    t = x.shape[0]
    tm = 512

    def body(yc_ref, yl_ref, x_ref, t_ref, wo_ref, fg_ref, do_ref, dob_ref, dy_ref, st_ref, y_wo):
        @pl.when(pl.program_id(0) == 0)
        def _():
            st_ref[...] = jnp.zeros_like(st_ref)

        y_wo[...] = _mm(yc_ref[...], wo_ref[0:D_PART, :]) + _mm(yl_ref[...], wo_ref[D_PART:2 * D_PART, :])

        def norm_loss_slab(s, carry):
            g_sum, loss_sum = carry
            rows = pl.ds(pl.multiple_of(s * SLAB, SLAB), SLAB)
            o = x_ref[rows, :] + y_wo[rows, :]
            r2 = lax.rsqrt(jnp.mean(o * o, axis=-1, keepdims=True) + RMS_EPS)
            ohat = o * r2
            fg = fg_ref[...]
            diff = ohat * fg - t_ref[rows, :]
            dout = diff * (1.0 / D_MODEL)
            gp = dout * fg
            do = r2 * (gp - ohat * jnp.mean(gp * ohat, axis=-1, keepdims=True))
            do_ref[rows, :] = do
            dob_ref[rows, :] = do.astype(MXU_DTYPE)
            loss = 0.5 * jnp.sum(jnp.sum(diff * diff, axis=-1, keepdims=True) * (1.0 / D_MODEL), axis=0, keepdims=True)
            return g_sum + jnp.sum(dout * ohat, axis=0, keepdims=True), loss_sum + loss

        g_sum, loss_sum = lax.fori_loop(0, tm // SLAB, norm_loss_slab,
                                        (jnp.zeros((1, D_MODEL), F32), jnp.zeros((1, 1), F32)))
        st_ref[0:1, :] += g_sum
        st_ref[1:2, :] += jnp.broadcast_to(loss_sum, (1, D_MODEL))
        dy_ref[...] = _mm_nt(dob_ref[...], wo_ref[...])

    row = lambda i: (i, 0)
    fix = lambda i: (0, 0)
    return pl.pallas_call(
        body, grid=(t // tm,),
        in_specs=[pl.BlockSpec((tm, D_PART), row), pl.BlockSpec((tm, D_PART), row),
                  pl.BlockSpec((tm, D_MODEL), row), pl.BlockSpec((tm, D_MODEL), row),
                  pl.BlockSpec((2 * D_PART, D_MODEL), fix), pl.BlockSpec((1, D_MODEL), fix)],
        out_specs=(pl.BlockSpec((tm, D_MODEL), row), pl.BlockSpec((tm, D_MODEL), row),
                   pl.BlockSpec((tm, 2 * D_PART), row), pl.BlockSpec((SUBLANES, D_MODEL), fix)),
        out_shape=(jax.ShapeDtypeStruct((t, D_MODEL), F32), jax.ShapeDtypeStruct((t, D_MODEL), MXU_DTYPE),
                   jax.ShapeDtypeStruct((t, 2 * D_PART), F32), jax.ShapeDtypeStruct((SUBLANES, D_MODEL), F32)),
        scratch_shapes=[pltpu.VMEM((tm, D_MODEL), F32)],
        compiler_params=_cp(ARB), name="out_projection_loss",
    )(yc, yl, x, target, wo, final_g)


def _input_grad(dproj, w12, x, do, ln_g, sb_in):
    t = x.shape[0]
    tm = 1024

    def body(dp_ref, w_ref, x_ref, do_ref, g_ref, s_ref, gx_ref, st_ref, r_ref, acc, send_sems, recv_sems):
        i, p = pl.program_id(0), pl.program_id(1)

        @pl.when((i == 0) & (p == 0))
        def _():
            st_ref[...] = jnp.zeros_like(st_ref)
            for cp in _chip_block_copies(s_ref, r_ref, CHUNKS_PER_BLOCK, send_sems, recv_sems):
                cp.start()

        @pl.when((i == t // tm - 1) & (p == N_PARTS - 1))
        def _():
            for cp in _chip_block_copies(s_ref, r_ref, CHUNKS_PER_BLOCK, send_sems, recv_sems):
                cp.wait()

        @pl.when(p == 0)
        def _():
            acc[...] = jnp.zeros_like(acc)

        acc[...] += _mm_nt(dp_ref[0], jnp.concatenate([w_ref[0], w_ref[1]], axis=1))

        @pl.when(p == N_PARTS - 1)
        def _():
            def norm_bwd_slab(s, g_sum):
                rows = pl.ds(pl.multiple_of(s * SLAB, SLAB), SLAB)
                xf = x_ref[rows, :]
                r = lax.rsqrt(jnp.mean(xf * xf, axis=-1, keepdims=True) + RMS_EPS)
                xhat = xf * r
                dxn = acc[rows, :]
                dxh = dxn * g_ref[...]
                gx_ref[rows, :] = do_ref[rows, :] + r * (dxh - xhat * jnp.mean(dxh * xhat, axis=-1, keepdims=True))
                return g_sum + jnp.sum(dxn * xhat, axis=0, keepdims=True)

            st_ref[0:1, :] += lax.fori_loop(0, tm // SLAB, norm_bwd_slab, jnp.zeros((1, D_MODEL), F32))

    row = lambda i, p: (i, 0)
    fix = lambda i, p: (0, 0)
    return pl.pallas_call(
        body, grid=(t // tm, N_PARTS),
        in_specs=[
            pl.BlockSpec((1, tm, D_PART), lambda i, p: (p, i, 0)),
            pl.BlockSpec((2, D_MODEL, CHUNK), lambda i, p: (p, 0, 0)),
            pl.BlockSpec((tm, D_MODEL), row), pl.BlockSpec((tm, D_MODEL), row), pl.BlockSpec((1, D_MODEL), fix),
            pl.BlockSpec(memory_space=pl.ANY)],
        out_specs=(pl.BlockSpec((tm, D_MODEL), row), pl.BlockSpec((SUBLANES, D_MODEL), fix),
                   pl.BlockSpec(memory_space=pl.ANY)),
        out_shape=(jax.ShapeDtypeStruct((t, D_MODEL), F32), jax.ShapeDtypeStruct((SUBLANES, D_MODEL), F32),
                   _chip_blocks_shape(sb_in, CHUNKS_PER_BLOCK)),
        scratch_shapes=[pltpu.VMEM((tm, D_MODEL), F32), pltpu.SemaphoreType.DMA((3,)), pltpu.SemaphoreType.DMA((3,))],
        compiler_params=_cp(ARB, ARB), name="input_grad",
    )(dproj, w12, x, do, ln_g, sb_in)


def _w_in_grad(xn, dproj, small):
    t = xn.shape[0]
    small_shape = pltpu.VMEM(small.shape, F32)

    def body(xn_ref, dp_ref, sm_ref, o_ref, ob_ref, red_ref, acc_s, r0, r1, r2, send_sems, recv_sems):
        _allreduce_behind(pl.program_id(0), (0, 1, 3, N_PARTS - 1), sm_ref, acc_s, (r0, r1, r2), red_ref, send_sems, recv_sems)
        g = _mm_tn(xn_ref[...], dp_ref[0])
        for s in range(2):
            o_ref[s] = g[:, CHUNK * s:CHUNK * (s + 1)]
            ob_ref[s] = g[:, CHUNK * s:CHUNK * (s + 1)].astype(jnp.bfloat16)

    whole = pl.BlockSpec(small.shape, lambda p: (0, 0))
    pair = pl.BlockSpec((2, D_MODEL, CHUNK), lambda p: (p, 0, 0))
    return pl.pallas_call(
        body, grid=(N_PARTS,),
        in_specs=[pl.BlockSpec((t, D_MODEL), lambda p: (0, 0)),
                  pl.BlockSpec((1, t, D_PART), lambda p: (p, 0, 0)), whole],
        out_specs=(pair, pair, whole),
        out_shape=(jax.ShapeDtypeStruct((N_CHUNKS, D_MODEL, CHUNK), F32),
                   jax.ShapeDtypeStruct((N_CHUNKS, D_MODEL, CHUNK), jnp.bfloat16), jax.ShapeDtypeStruct(small.shape, F32)),
        scratch_shapes=[small_shape] * 4 + [pltpu.SemaphoreType.DMA((3,)), pltpu.SemaphoreType.DMA((3,))],
        compiler_params=_cp(ARB), name="w_in_grad",
    )(xn, dproj, small)


def _w_out_grad(yc, yl, dob):
    t = yc.shape[0]
    tk = 2048

    def body(yc_ref, yl_ref, do_ref, o_ref, ob_ref):
        j, kk = pl.program_id(0), pl.program_id(1)

        def accumulate(y_ref):
            @pl.when(kk == 0)
            def _():
                o_ref[...] = jnp.zeros_like(o_ref)

            o_ref[...] += _mm_tn(y_ref[...], do_ref[...])

            @pl.when(kk == t // tk - 1)
            def _():
                ob_ref[...] = o_ref[...].astype(jnp.bfloat16)

        pl.when(j == 0)(functools.partial(accumulate, yc_ref))
        pl.when(j == 1)(functools.partial(accumulate, yl_ref))

    def rows_of(half):
        return lambda j, kk: (jnp.where(j == half, kk, 0), 0)

    half = pl.BlockSpec((D_PART, D_MODEL), lambda j, kk: (j, 0))
    out, out_b = pl.pallas_call(
        body, grid=(2, t // tk),
        in_specs=[pl.BlockSpec((tk, D_PART), rows_of(0)), pl.BlockSpec((tk, D_PART), rows_of(1)),
                  pl.BlockSpec((tk, D_MODEL), lambda j, kk: (kk, 0))],
        out_specs=(half, half),
        out_shape=(jax.ShapeDtypeStruct((2 * D_PART, D_MODEL), F32), jax.ShapeDtypeStruct((2 * D_PART, D_MODEL), jnp.bfloat16)),
        compiler_params=_cp(ARB, ARB), name="w_out_grad",
    )(yc, yl, dob)
    blocks = (N_CHIPS, 2 * D_PART // N_CHIPS, D_MODEL)
    return out.reshape(blocks), out_b.reshape(blocks)


def _for_groups(n, fn, init, unroll=UNROLL, stores=(), descending=False):
    assert unroll % 2 == 0 and n % unroll == 0

    def trip(j, carry):
        held = None
        for uu in range(unroll):
            idx = j * unroll + uu
            carry, values = fn(idx, carry)
            if uu % 2 == 0:
                held = values
                continue
            low_group = n - 1 - idx if descending else idx - 1
            rows = pl.ds(pl.multiple_of(low_group * SUBLANES, 2 * SUBLANES), 2 * SUBLANES)
            pairs = zip(values, held) if descending else zip(held, values)
            for store, (lo, hi) in zip(stores, pairs, strict=True):
                store(rows, jnp.concatenate([lo, hi], axis=0).astype(MXU_DTYPE))
        return carry

    return lax.fori_loop(0, n // unroll, trip, init)


def _rows_of(ref, *lead, cols=slice(None)):
    def store(rows, value):
        ref[(*lead, rows, cols)] = value

    return store


def _pvb(pv_ref, r):
    return jnp.broadcast_to(pv_ref[r:r + 1, :], (SUBLANES, pv_ref.shape[1]))


def _conv3(pv_ref, u, u1, u2):
    return (_pvb(pv_ref, PV_CONV_W) * u2 + _pvb(pv_ref, PV_CONV_W + 1) * u1) + _pvb(pv_ref, PV_CONV_W + 2) * u


def _conv4(pv_ref, v, v1, v2, v3):
    return ((((_pvb(pv_ref, PV_LRU_W) * v3 + _pvb(pv_ref, PV_LRU_W + 1) * v2) + _pvb(pv_ref, PV_LRU_W + 2) * v1)
             + _pvb(pv_ref, PV_LRU_W + 3) * v) + _pvb(pv_ref, PV_LRU_B))


def _mixer_forward(proj, pvec, wai, w_out):
    t = proj.shape[1]
    tb = 512
    ng = tb // SUBLANES
    nt = t // tb
    lw = FWD_LW
    ns = D_PART // lw
    per_chunk = CHUNK // lw

    def body(bg_ref, cg_ref, xc_ref, gc_ref, xl_ref, gl_ref, pv_ref, wai_ref, wo_ref,
             yc_ref, yl_ref, h_ref, u_s, r_ref, ig_ref, wo4_ref,
             ucp_s, xlp_s, ls_s, hbuf_s, ub_s, gate_s, wob_s, local_sem, send_sems, recv_sems):
        _gather_w_out(pl.program_id(0) * nt + pl.program_id(1), ns * nt, wo_ref, wob_s, wo4_ref, local_sem, send_sems, recv_sems)

        @pl.when(pl.program_id(1) == 0)
        def _():
            ucp_s[...] = jnp.zeros_like(ucp_s)
            xlp_s[...] = jnp.zeros_like(xlp_s)
            hbuf_s[...] = jnp.zeros_like(hbuf_s)

        row = lax.broadcasted_iota(jnp.int32, (SUBLANES, lw), 0)
        ls_s[...] = RG_LRU_C * _log_sigmoid(_pvb(pv_ref, PV_LAM))

        def conv_group(g, carry):
            ucp, xlp = carry
            sl = pl.ds(pl.multiple_of(g * SUBLANES, SUBLANES), SUBLANES)
            uc = cg_ref[sl, :] * xc_ref[sl, :]
            v = _conv3(pv_ref, uc, _shift_down(uc, ucp, 1, row), _shift_down(uc, ucp, 2, row))
            yc = bg_ref[sl, :] * v
            rr = lax.rsqrt(_head_mean(yc * yc, CONV_HEAD) + RMS_EPS)
            gc = gc_ref[sl, :]
            zc = ((yc * rr) * _pvb(pv_ref, PV_CG)) * (gc * _sigmoid(gc))
            xl = xl_ref[sl, :]
            u = _conv4(pv_ref, xl, _shift_down(xl, xlp, 1, row), _shift_down(xl, xlp, 2, row), _shift_down(xl, xlp, 3, row))
            u_s[sl, :] = u
            return (uc, xl), (zc, u)

        ucp, xlp = _for_groups(ng, conv_group, (ucp_s[...], xlp_s[...]), unroll=2 * UNROLL,
                               stores=(_rows_of(yc_ref), _rows_of(ub_s)))
        ucp_s[...] = ucp
        xlp_s[...] = xlp

        gate_s[...] = _mm(ub_s[...], wai_ref[0])

        def lru_group(g, h_before):
            sl = pl.ds(pl.multiple_of(g * SUBLANES, SUBLANES), SUBLANES)
            u = u_s[sl, :]
            r = _sigmoid(gate_s[sl, 0:lw] + _pvb(pv_ref, PV_BA))
            ig = _sigmoid(gate_s[sl, lw:2 * lw] + _pvb(pv_ref, PV_BI))
            r_ref[sl, :] = r
            ig_ref[sl, :] = ig
            a, _, mult, _ = _decay(r, ls_s[...])
            A, B = _scan8_fwd(a, mult * (ig * u), row)
            h = B + A * jnp.broadcast_to(h_before[SUBLANES - 1:SUBLANES, :], (SUBLANES, lw))
            h_ref[sl, :] = h
            rr = lax.rsqrt(_head_mean(h * h, LRU_HEAD) + RMS_EPS)
            gl = gl_ref[sl, :]
            return h, (((h * rr) * _pvb(pv_ref, PV_LG)) * (gl * _sigmoid(gl)),)

        hbuf_s[...] = _for_groups(ng, lru_group, hbuf_s[...], unroll=2 * UNROLL, stores=(_rows_of(yl_ref),))

    def part(p):
        return pl.BlockSpec((None, tb, lw), lambda c, i: (2 * p + c // per_chunk, i, c % per_chunk))

    strip = pl.BlockSpec((tb, lw), lambda c, i: (i, c))
    small = pltpu.VMEM((SUBLANES, lw), F32)
    return pl.pallas_call(
        body, grid=(ns, nt),
        in_specs=[part(p) for p in range(N_PARTS)] + [
            pl.BlockSpec((PV_ROWS, lw), lambda c, i: (0, c)),
            pl.BlockSpec((1, lw, 2 * lw), lambda c, i: (c, 0, 0)),
            pl.BlockSpec(w_out.shape, lambda c, i: (0, 0))],
        out_specs=(strip,) * 6 + (pl.BlockSpec(memory_space=pl.ANY),),
        out_shape=(jax.ShapeDtypeStruct((t, D_PART), MXU_DTYPE),) * 2 + (jax.ShapeDtypeStruct((t, D_PART), F32),) * 4 + (
            jax.ShapeDtypeStruct((N_CHIPS,) + w_out.shape, MXU_DTYPE),),
        scratch_shapes=[small, small, small, small, pltpu.VMEM((tb, lw), MXU_DTYPE),
                        pltpu.VMEM((tb, 2 * lw), F32), pltpu.VMEM(w_out.shape, MXU_DTYPE),
                        pltpu.SemaphoreType.DMA, pltpu.SemaphoreType.DMA((6,)), pltpu.SemaphoreType.DMA((6,))],
        compiler_params=_cp(ARB, ARB), name="mixer_forward",
    )(proj, proj, proj, proj, proj, proj, pvec, wai, w_out)


def _mixer_backward(proj, h, u, r, ig, dy, pvec, wai, sb_out):
    t = proj.shape[1]
    tb = 1024
    ng = tb // SUBLANES
    nt = t // tb
    gpb = tb // SUBLANES

    def body(bg_ref, cg_ref, xc_ref, gc_ref, xl_ref, gl_ref, h_ref, u_ref, r_ref, ig_ref, dyc_ref, dyl_ref,
             cgh_ref, xch_ref, xlh_ref, hh_ref, pv_ref, wai_ref, so_ref,
             dp_ref, gw_ref, sv_ref, ro_ref,
             ls_s, ub_s, uce_s, xle_s, he_s, dgb_s, du_s, gbuf_s,
             acc_s, an_s, dvn_s, dun_s, send_sems, recv_sems):
        i = pl.program_id(1)
        first_block = i == nt - 1

        @pl.when((pl.program_id(0) == 0) & (i == 0))
        def _():
            for cp in _chip_block_copies(so_ref, ro_ref, 1, send_sems, recv_sems):
                cp.start()

        @pl.when((pl.program_id(0) == NS - 1) & (i == nt - 1))
        def _():
            for cp in _chip_block_copies(so_ref, ro_ref, 1, send_sems, recv_sems):
                cp.wait()

        @pl.when(i == 0)
        def _():
            acc_s[...] = jnp.zeros_like(acc_s)
            gw_ref[...] = jnp.zeros_like(gw_ref)
            an_s[...] = jnp.zeros_like(an_s)
            dvn_s[...] = jnp.zeros_like(dvn_s)
            dun_s[...] = jnp.zeros_like(dun_s)
            gbuf_s[...] = jnp.zeros_like(gbuf_s)

        row = lax.broadcasted_iota(jnp.int32, (SUBLANES, LW), 0)
        ls_s[...] = RG_LRU_C * _log_sigmoid(_pvb(pv_ref, PV_LAM))
        keep = jnp.where(first_block, 0.0, 1.0)
        uce_s[0:SUBLANES, :] = (cgh_ref[...] * xch_ref[...]) * keep
        xle_s[0:SUBLANES, :] = xlh_ref[...] * keep
        he_s[0:SUBLANES, :] = hh_ref[...] * keep
        xle_s[SUBLANES:SUBLANES + tb, :] = xl_ref[...]
        he_s[SUBLANES:SUBLANES + tb, :] = h_ref[...]

        uce_s[SUBLANES:SUBLANES + tb, :] = cg_ref[...] * xc_ref[...]

        def acc_add(k, v):
            acc_s[k] += v

        def main_group(gi, carry):
            a_next, dv_next, g_next = carry
            g = ng - 1 - gi
            r0 = pl.multiple_of(g * SUBLANES, SUBLANES)
            sl = pl.ds(r0, SUBLANES)
            sl_e = pl.ds(r0 + SUBLANES, SUBLANES)
            lsb = ls_s[...]
            u = u_ref[sl, :]
            r = r_ref[sl, :]
            ig = ig_ref[sl, :]
            a, e2, mult, inv_mult = _decay(r, lsb)
            gl = gl_ref[sl, :]
            sg = _sigmoid(gl)
            s_l = gl * sg
            h8 = he_s[sl_e, :]
            hprev = _shift_down(h8, he_s[sl, :], 1, row)
            rr = lax.rsqrt(_head_mean(h8 * h8, LRU_HEAD) + RMS_EPS)
            n = h8 * rr
            dz = dyl_ref[sl, :]
            lg = _pvb(pv_ref, PV_LG)
            acc_add(PV_LG, (dz * n) * s_l)
            p5 = ((dz * n) * lg) * (sg + s_l * (1.0 - sg))
            dn = (dz * lg) * s_l
            dh = rr * (dn - n * _head_mean(dn * n, LRU_HEAD))
            A, B = _scan8_rev(_shift_up(a, a_next, 1, row), dh, row)
            gg = B + A * jnp.broadcast_to(g_next[0:1, :], (SUBLANES, LW))
            da = gg * hprev
            iu = ig * u
            diu = gg * mult
            dla = da * a - (gg * iu) * (e2 * inv_mult)
            acc_add(PV_LAM, dla * r)
            dra = (dla * lsb) * (r * (1.0 - r))
            dia = (diu * u) * (ig * (1.0 - ig))
            acc_add(PV_BA, dra)
            acc_add(PV_BI, dia)
            du_s[sl, :] = diu * ig
            bg = bg_ref[sl, :]
            gc = gc_ref[sl, :]
            uc = uce_s[sl_e, :]
            ucp = uce_s[sl, :]
            uc1 = _shift_down(uc, ucp, 1, row)
            uc2 = _shift_down(uc, ucp, 2, row)
            v = _conv3(pv_ref, uc, uc1, uc2)
            yc = bg * v
            rrc = lax.rsqrt(_head_mean(yc * yc, CONV_HEAD) + RMS_EPS)
            nc = yc * rrc
            sgc = _sigmoid(gc)
            s_c = gc * sgc
            dzc = dyc_ref[sl, :]
            cgain = _pvb(pv_ref, PV_CG)
            acc_add(PV_CG, (dzc * nc) * s_c)
            p3 = ((dzc * nc) * cgain) * (sgc + s_c * (1.0 - sgc))
            dnc = (dzc * cgain) * s_c
            dyc = rrc * (dnc - nc * _head_mean(dnc * nc, CONV_HEAD))
            dv = dyc * bg
            duc = (_pvb(pv_ref, PV_CONV_W + 2) * dv + _pvb(pv_ref, PV_CONV_W + 1) * _shift_up(dv, dv_next, 1, row)
                   + _pvb(pv_ref, PV_CONV_W) * _shift_up(dv, dv_next, 2, row))
            acc_add(PV_CONV_W + 2, dv * uc)
            acc_add(PV_CONV_W + 1, dv * uc1)
            acc_add(PV_CONV_W, dv * uc2)
            return (a, dv, gg), (dyc * v, duc * xc_ref[sl, :], duc * cg_ref[sl, :], p3, p5, dra, dia, u)

        a_next, dv_next, g_next = _for_groups(
            ng, main_group, (an_s[...], dvn_s[...], gbuf_s[...]), descending=True,
            stores=(_rows_of(dp_ref, 0), _rows_of(dp_ref, 1), _rows_of(dp_ref, 2), _rows_of(dp_ref, 3), _rows_of(dp_ref, 5),
                    _rows_of(dgb_s, cols=slice(0, LW)), _rows_of(dgb_s, cols=slice(LW, 2 * LW)), _rows_of(ub_s)))
        an_s[...] = a_next
        dvn_s[...] = dv_next
        gbuf_s[...] = g_next

        dgb = dgb_s[...]
        du_s[...] += _mm_nt(dgb, wai_ref[0])
        gw_ref[0] += _mm_tn(ub_s[...], dgb)

        def lru_conv_group(gi, du_next):
            g = ng - 1 - gi
            r0 = pl.multiple_of(g * SUBLANES, SUBLANES)
            sl = pl.ds(r0, SUBLANES)
            du = du_s[sl, :]
            xl = xle_s[pl.ds(r0 + SUBLANES, SUBLANES), :]
            xlp = xle_s[sl, :]
            acc_add(PV_LRU_B, du)
            acc_add(PV_LRU_W + 3, du * xl)
            acc_add(PV_LRU_W + 2, du * _shift_down(xl, xlp, 1, row))
            acc_add(PV_LRU_W + 1, du * _shift_down(xl, xlp, 2, row))
            acc_add(PV_LRU_W, du * _shift_down(xl, xlp, 3, row))
            dxl = (((_pvb(pv_ref, PV_LRU_W + 3) * du + _pvb(pv_ref, PV_LRU_W + 2) * _shift_up(du, du_next, 1, row))
                    + _pvb(pv_ref, PV_LRU_W + 1) * _shift_up(du, du_next, 2, row))
                   + _pvb(pv_ref, PV_LRU_W) * _shift_up(du, du_next, 3, row))
            return du, (dxl,)

        dun_s[...] = _for_groups(ng, lru_conv_group, dun_s[...], descending=True, stores=(_rows_of(dp_ref, 4),))

        @pl.when(first_block)
        def _():
            sv_ref[...] = jnp.zeros_like(sv_ref)
            for k in range(N_ACC):
                tot = jnp.sum(acc_s[k], axis=0, keepdims=True)
                if k == PV_LAM:
                    tot = (RG_LRU_C * tot) / (1.0 + jnp.exp(pv_ref[PV_LAM:PV_LAM + 1, :]))
                sv_ref[k:k + 1, :] = tot

    def part(p):
        return pl.BlockSpec((None, tb, LW), lambda c, i: (2 * p + c // STRIPS_PER_CHUNK, nt - 1 - i, c % STRIPS_PER_CHUNK))

    def halo(p):
        return pl.BlockSpec((None, SUBLANES, LW), lambda c, i: (2 * p + c // STRIPS_PER_CHUNK,
                                                                jnp.maximum((nt - 1 - i) * gpb - 1, 0), c % STRIPS_PER_CHUNK))

    strip = pl.BlockSpec((tb, LW), lambda c, i: (nt - 1 - i, c))
    big = pltpu.VMEM((tb, LW), F32)
    big_e = pltpu.VMEM((tb + SUBLANES, LW), F32)
    small = pltpu.VMEM((SUBLANES, LW), F32)
    outs = pl.pallas_call(
        body, grid=(NS, nt),
        in_specs=[part(p) for p in range(N_PARTS)] + [
            strip, strip, strip, strip, strip, pl.BlockSpec((tb, LW), lambda c, i: (nt - 1 - i, NS + c)),
            halo(1), halo(2), halo(4),
            pl.BlockSpec((SUBLANES, LW), lambda c, i: (jnp.maximum((nt - 1 - i) * gpb - 1, 0), c)),
            pl.BlockSpec((PV_ROWS, LW), lambda c, i: (0, c)),
            pl.BlockSpec((1, LW, 2 * LW), lambda c, i: (c, 0, 0)),
            pl.BlockSpec(memory_space=pl.ANY)],
        out_specs=(pl.BlockSpec((N_PARTS, tb, LW), lambda c, i: (0, nt - 1 - i, c)),
                   pl.BlockSpec((1, LW, 2 * LW), lambda c, i: (c, 0, 0)),
                   pl.BlockSpec((PV_ROWS, LW), lambda c, i: (0, c)),
                   pl.BlockSpec(memory_space=pl.ANY)),
        out_shape=(jax.ShapeDtypeStruct((N_PARTS, t, D_PART), MXU_DTYPE),
                   jax.ShapeDtypeStruct((NS, LW, 2 * LW), F32), jax.ShapeDtypeStruct((PV_ROWS, D_PART), F32),
                   _chip_blocks_shape(sb_out, 1)),
        scratch_shapes=[small, pltpu.VMEM((tb, LW), MXU_DTYPE), big_e, big_e, big_e,
                        pltpu.VMEM((tb, 2 * LW), MXU_DTYPE), big, small,
                        pltpu.VMEM((N_ACC, SUBLANES, LW), F32), small, small, small,
                        pltpu.SemaphoreType.DMA((3,)), pltpu.SemaphoreType.DMA((3,))],
        compiler_params=_cp(ARB, ARB), name="mixer_backward",
    )(proj, proj, proj, proj, proj, proj, h, u, r, ig, dy, dy, proj, proj, proj, h, pvec, wai, sb_out)
    return outs


def _adamw(w, g, m, v):
    m = ADAM_B1 * m + (1.0 - ADAM_B1) * g
    v = ADAM_B2 * v + (1.0 - ADAM_B2) * (g * g)
    m_hat = m / (1.0 - ADAM_B1 ** ADAM_STEP)
    v_hat = v / (1.0 - ADAM_B2 ** ADAM_STEP)
    delta = -ADAM_LR * (m_hat / (jnp.sqrt(v_hat) + ADAM_EPS) + ADAM_WD * w)
    return delta, m, v


def _adam_w_in(w, m, v, g3):
    rows, cols = w.shape
    tr = 128

    def body(w_ref, m_ref, v_ref, g_ref, go_ref, d_ref, mo_ref, vo_ref):
        for s in range(CHUNKS_PER_BLOCK):
            cs = slice(CHUNK * s, CHUNK * (s + 1))
            g = g_ref[s]
            d, mn, vn = _adamw(w_ref[:, cs], g, m_ref[:, cs], v_ref[:, cs])
            go_ref[:, cs] = g
            d_ref[:, cs] = d
            mo_ref[:, cs] = mn
            vo_ref[:, cs] = vn

    blk = pl.BlockSpec((tr, cols), lambda i: (i, 0))
    return pl.pallas_call(
        body, grid=(rows // tr,),
        in_specs=[blk, blk, blk, pl.BlockSpec((CHUNKS_PER_BLOCK, tr, CHUNK), lambda i: (0, i, 0))],
        out_specs=(blk,) * 4, out_shape=(jax.ShapeDtypeStruct(w.shape, F32),) * 4,
        compiler_params=_cp(ARB), name="adam_w_in",
    )(w, m, v, g3)


def _adam_w_out(w, m, v, g):
    rows, cols = w.shape
    tr = 128

    def body(w_ref, m_ref, v_ref, g_ref, d_ref, mo_ref, vo_ref):
        d_ref[...], mo_ref[...], vo_ref[...] = _adamw(w_ref[...], g_ref[...], m_ref[...], v_ref[...])

    blk = pl.BlockSpec((tr, cols), lambda i: (i, 0))
    return pl.pallas_call(
        body, grid=(rows // tr,), in_specs=[blk] * 4, out_specs=(blk,) * 3,
        out_shape=(jax.ShapeDtypeStruct(w.shape, F32),) * 3,
        compiler_params=_cp(ARB), name="adam_w_out",
    )(w, m, v, g)


def _adam_small(ws, ms, vs, gs):
    n = len(ws)

    def body(*refs):
        w_r, m_r, v_r, g_r = refs[0:n], refs[n:2 * n], refs[2 * n:3 * n], refs[3 * n:4 * n]
        d_o, m_o, v_o = refs[4 * n:5 * n], refs[5 * n:6 * n], refs[6 * n:7 * n]
        for j in range(n):
            d_o[j][...], m_o[j][...], v_o[j][...] = _adamw(w_r[j][...], g_r[j][...], m_r[j][...], v_r[j][...])

    vm = pl.BlockSpec(memory_space=pltpu.VMEM)
    shapes = tuple(jax.ShapeDtypeStruct(w.shape, F32) for w in ws)
    outs = pl.pallas_call(
        body, in_specs=[vm] * (4 * n), out_specs=(vm,) * (3 * n), out_shape=shapes * 3,
        compiler_params=_cp(), name="adam_small",
    )(*ws, *ms, *vs, *gs)
    return outs[0:n], outs[n:2 * n], outs[2 * n:3 * n]


def _block_diag_strips(w, lw):
    heads = lw // LRU_HEAD
    w4 = w.reshape(D_PART // lw, heads, LRU_HEAD, LRU_HEAD)
    rows = [jnp.pad(w4[:, hh], ((0, 0), (0, 0), (LRU_HEAD * hh, lw - LRU_HEAD * (hh + 1)))) for hh in range(heads)]
    return jnp.concatenate(rows, axis=1)


def _gate_matrices(w_a, w_i, lw):
    return jnp.concatenate([_block_diag_strips(w_a, lw), _block_diag_strips(w_i, lw)], axis=2).astype(MXU_DTYPE)


def _strip_diag_blocks(g):
    g5 = g.reshape(NS, HEADS_PER_STRIP, LRU_HEAD, HEADS_PER_STRIP, LRU_HEAD)
    return jnp.stack([g5[:, hh, :, hh, :] for hh in range(HEADS_PER_STRIP)], axis=1).reshape(NS * HEADS_PER_STRIP, LRU_HEAD, LRU_HEAD)


def kernel(x, ln_g, w_in, conv_w, lru_conv_w, lru_conv_b, w_a, b_a, w_i, b_i, lam, conv_out_g, lru_out_g, w_out, final_g, loss_target, m_ln_g, m_w_in, m_conv_w, m_lru_conv_w, m_lru_conv_b, m_w_a, m_b_a, m_w_i, m_b_i, m_lam, m_conv_out_g, m_lru_out_g, m_w_out, m_final_g, v_ln_g, v_w_in, v_conv_w, v_lru_conv_w, v_lru_conv_b, v_w_a, v_b_a, v_w_i, v_b_i, v_lam, v_conv_out_g, v_lru_out_g, v_w_out, v_final_g):
    xi, yi, ci = lax.axis_index("x"), lax.axis_index("y"), lax.axis_index("c")
    k = 2 * xi + yi
    t = x.shape[1]
    x2 = x.reshape(t, D_MODEL)
    tgt2 = loss_target.reshape(t, D_MODEL)
    row = lambda a: a.reshape(1, -1)

    small = jnp.concatenate([conv_w, lru_conv_w, jnp.zeros((1, conv_w.shape[1]), F32)], axis=0)
    proj, xn, w12, sm4 = _gather_in_projection(x2, row(ln_g), w_in, small)
    convs = jnp.transpose(sm4, (1, 0, 2)).reshape(SUBLANES, D_PART)
    pvec = jnp.concatenate(
        [convs[0:7], row(lru_conv_b), row(b_a), row(b_i), row(lam), row(conv_out_g), row(lru_out_g),
         jnp.zeros((PV_ROWS - N_ACC, D_PART), F32)], axis=0)
    wai = _gate_matrices(w_a, w_i, LW)

    c_arr = jnp.reshape(ci, (1,)).astype(jnp.int32)
    yc, yl, h, u, r, ig, wo4 = _mixer_forward(proj, pvec, _gate_matrices(w_a, w_i, FWD_LW), w_out)
    wo = wo4.reshape(2 * D_PART, D_MODEL)
    do, dob, dy, st_out = _out_projection_loss(yc, yl, x2, tgt2, wo, row(final_g))
    go4, go4b = _w_out_grad(yc, yl, dob)
    s_out, sb_out = _add_sibling_halves(go4, go4b, c_arr, "add_sibling_halves_out")
    dproj, g_wai, svec, r2o = _mixer_backward(proj, h, u, r, ig, dy, pvec, wai, sb_out)
    gwa = _strip_diag_blocks(g_wai[:, :, 0:LW]).reshape(LRU_HEAD, D_PART)
    gwi = _strip_diag_blocks(g_wai[:, :, LW:2 * LW]).reshape(LRU_HEAD, D_PART)
    g12, g12b, red = _w_in_grad(xn, dproj, jnp.concatenate([svec, st_out, gwa, gwi], axis=0))
    s_in, sb_in = _add_sibling_halves(g12, g12b, c_arr, "add_sibling_halves_in")
    grad_x, st_in, r2i = _input_grad(dproj, w12, x2, do, row(ln_g), sb_in)
    f_in, f_out, red_ln = _finish_gradients(s_in, r2i, s_out, r2o, st_in)
    r_out = PV_ROWS
    r_wa = PV_ROWS + SUBLANES
    r_wi = r_wa + LRU_HEAD
    loss = red[r_out + 1, 0]

    g_w_in, d_w_in, nm_w_in, nv_w_in = _adam_w_in(w_in, m_w_in, v_w_in, f_in)
    g_w_out = f_out[0]
    d_w_out, nm_w_out, nv_w_out = _adam_w_out(w_out, m_w_out, v_w_out, g_w_out)

    ncol = conv_w.shape[1]
    conv_cols = lax.dynamic_slice(red, (0, k * ncol), (SUBLANES, ncol))
    g_small = {
        "ln_g": red_ln[0], "conv_w": conv_cols[0:3], "lru_conv_w": conv_cols[3:7], "lru_conv_b": red[PV_LRU_B],
        "w_a": red[r_wa:r_wa + LRU_HEAD].reshape(w_a.shape), "b_a": red[PV_BA],
        "w_i": red[r_wi:r_wi + LRU_HEAD].reshape(w_i.shape), "b_i": red[PV_BI], "lam": red[PV_LAM],
        "conv_out_g": red[PV_CG], "lru_out_g": red[PV_LG], "final_g": red[r_out],
    }
    w_small = {"ln_g": ln_g, "conv_w": conv_w, "lru_conv_w": lru_conv_w, "lru_conv_b": lru_conv_b, "w_a": w_a, "b_a": b_a,
               "w_i": w_i, "b_i": b_i, "lam": lam, "conv_out_g": conv_out_g, "lru_out_g": lru_out_g, "final_g": final_g}
    m_small = {"ln_g": m_ln_g, "conv_w": m_conv_w, "lru_conv_w": m_lru_conv_w, "lru_conv_b": m_lru_conv_b, "w_a": m_w_a,
               "b_a": m_b_a, "w_i": m_w_i, "b_i": m_b_i, "lam": m_lam, "conv_out_g": m_conv_out_g,
               "lru_out_g": m_lru_out_g, "final_g": m_final_g}
    v_small = {"ln_g": v_ln_g, "conv_w": v_conv_w, "lru_conv_w": v_lru_conv_w, "lru_conv_b": v_lru_conv_b, "w_a": v_w_a,
               "b_a": v_b_a, "w_i": v_w_i, "b_i": v_b_i, "lam": v_lam, "conv_out_g": v_conv_out_g,
               "lru_out_g": v_lru_out_g, "final_g": v_final_g}
    names = list(w_small)
    as2d = lambda a: a.reshape(1, -1) if a.ndim == 1 else a
    d_s, m_s, v_s = _adam_small([as2d(w_small[n]) for n in names], [as2d(m_small[n]) for n in names],
                                [as2d(v_small[n]) for n in names], [as2d(g_small[n]) for n in names])
    back = lambda n, a: a.reshape(w_small[n].shape)
    grads = {n: g_small[n] for n in names}
    deltas = {n: back(n, a) for n, a in zip(names, d_s)}
    new_m = {n: back(n, a) for n, a in zip(names, m_s)}
    new_v = {n: back(n, a) for n, a in zip(names, v_s)}
    grads["w_in"], deltas["w_in"], new_m["w_in"], new_v["w_in"] = g_w_in, d_w_in, nm_w_in, nv_w_in
    grads["w_out"], deltas["w_out"], new_m["w_out"], new_v["w_out"] = g_w_out, d_w_out, nm_w_out, nv_w_out

    order = ["ln_g", "w_in", "conv_w", "lru_conv_w", "lru_conv_b", "w_a", "b_a", "w_i", "b_i", "lam", "conv_out_g",
             "lru_out_g", "w_out", "final_g"]
    return (loss, grad_x.reshape(x.shape), *[grads[n] for n in order], *[deltas[n] for n in order],
            *[new_m[n] for n in order], *[new_v[n] for n in order])
```

```python
import functools

import jax
import jax.numpy as jnp
from jax import lax
from jax.experimental import pallas as pl
from jax.experimental.pallas import tpu as pltpu

F32 = jnp.float32
MXU_DTYPE = jnp.bfloat16

D_MODEL = 1024
D_PART = 1024
N_PARTS = 6
CHUNK = 512
CHUNKS_PER_BLOCK = 3
N_CHUNKS = 12
N_CHIPS = 4
SUBLANES = 8
LANES = 128
LW = 256
FWD_LW = 512
UNROLL = 8
NS = D_PART // LW
STRIPS_PER_CHUNK = CHUNK // LW
CONV_HEAD = 128
LRU_HEAD = 64
HEADS_PER_STRIP = LW // LRU_HEAD
RMS_EPS = 1e-6
RG_LRU_C = 8.0
ADAM_LR = 0.001
ADAM_B1 = 0.9
ADAM_B2 = 0.999
ADAM_EPS = 1e-08
ADAM_WD = 0.01
ADAM_STEP = 10

PV_CONV_W = 0
PV_LRU_W = 3
PV_LRU_B = 7
PV_BA = 8
PV_BI = 9
PV_LAM = 10
PV_CG = 11
PV_LG = 12
PV_ROWS = 16
N_ACC = 13

SLAB = 128
MESH = pl.DeviceIdType.MESH
VMEM_LIMIT = 56 * 1024 * 1024
ARB = "arbitrary"


def _cp(*sem, **kw):
    return pltpu.CompilerParams(dimension_semantics=sem or None, vmem_limit_bytes=VMEM_LIMIT, **kw)


def _mm(a, b):
    return jnp.dot(a, b, preferred_element_type=F32)


def _mm_nt(a, b):
    return lax.dot_general(a, b, (((1,), (1,)), ((), ())), preferred_element_type=F32)


def _mm_tn(a, b):
    return lax.dot_general(a, b, (((0,), (0,)), ((), ())), preferred_element_type=F32)


def _sigmoid(x):
    return 0.5 * jnp.tanh(0.5 * x) + 0.5


def _log_sigmoid(x):
    z = jnp.exp(-jnp.abs(x))
    u = 1.0 + z
    log1p = jnp.where(u == 1.0, z, jnp.log(u) * z / (u - 1.0))
    return jnp.minimum(x, 0.0) - log1p


def _head_mean(z, head):
    out = []
    for k in range(z.shape[1] // LANES):
        zk = z[:, LANES * k:LANES * (k + 1)]
        if head == LANES:
            m = jnp.sum(zk, axis=-1, keepdims=True) * (1.0 / head)
            out.append(jnp.broadcast_to(m, zk.shape))
        else:
            lo = lax.broadcasted_iota(jnp.int32, zk.shape, 1) < head
            s_lo = jnp.sum(jnp.where(lo, zk, 0.0), axis=-1, keepdims=True)
            s_hi = jnp.sum(jnp.where(lo, 0.0, zk), axis=-1, keepdims=True)
            out.append(jnp.where(lo, s_lo, s_hi) * (1.0 / head))
    return jnp.concatenate(out, axis=1)


def _shift_down(cur, prev, d, row):
    return pltpu.roll(jnp.where(row < SUBLANES - d, cur, prev), d, 0)


def _shift_up(cur, nxt, d, row):
    return pltpu.roll(jnp.where(row >= d, cur, nxt), SUBLANES - d, 0)


def _scan8_fwd(a, b, row):
    A, B = a, b
    for d in (1, 2, 4):
        m = row >= d
        a_s = jnp.where(m, pltpu.roll(A, d, 0), 1.0)
        b_s = jnp.where(m, pltpu.roll(B, d, 0), 0.0)
        B = A * b_s + B
        A = A * a_s
    return A, B


def _scan8_rev(a, b, row):
    A, B = a, b
    for d in (1, 2, 4):
        m = row < SUBLANES - d
        a_s = jnp.where(m, pltpu.roll(A, SUBLANES - d, 0), 1.0)
        b_s = jnp.where(m, pltpu.roll(B, SUBLANES - d, 0), 0.0)
        B = A * b_s + B
        A = A * a_s
    return A, B


def _decay(r, ls8):
    la = r * ls8
    a = jnp.exp(la)
    e2 = a * a
    em = -jnp.tanh(la) * (1.0 + e2)
    inv_mult = lax.rsqrt(em)
    return a, e2, em * inv_mult, inv_mult


def _mesh_pos():
    x, y, c = lax.axis_index("x"), lax.axis_index("y"), lax.axis_index("c")
    chips = [(1 - x, y), (x, 1 - y), (1 - x, 1 - y)]
    return x, y, c, chips


def _gather_in_projection(x, ln_g, w_in, small):
    t = x.shape[0]
    rb_x = 512
    rb_mm = 2048
    n_mm = t // rb_mm
    half = w_in.shape[0] // 2

    def body(x_hbm, g_ref, wi_ref, sm_ref, proj_hbm, xn_ref, w12_ref, sm4_ref,
             xbuf, obuf, x_sems, o_sems, send_sems, recv_sems):
        x_, y_, c, chips = _mesh_pos()
        k = 2 * x_ + y_
        sib = (x_, y_, 1 - c)
        sm4_ref[k] = sm_ref[...]

        def remote(ref, sem, to):
            return pltpu.make_async_remote_copy(src_ref=ref, dst_ref=ref, send_sem=send_sems.at[sem],
                                                recv_sem=recv_sems.at[sem], device_id=to, device_id_type=MESH)

        def chunk_of(chip, s):
            return CHUNKS_PER_BLOCK * (2 * chip[0] + chip[1]) + s

        def piece(q, core, first=0, rows=half):
            return w12_ref.at[q, pl.ds(pl.multiple_of(half * core + first, SUBLANES * 2), rows), :]

        nbr_x, nbr_y, diagonal = chips
        quarter = half // 2
        DIAG = [(0, 0, half, 0), (1, 0, quarter, 0), (1, quarter, quarter, 1), (2, 0, half, 1)]
        ici = lambda m, s: 2 * s + m
        dgn = lambda j: 6 + j
        to_sib = 10
        sml = lambda m: 20 + m

        sends = []
        for s in range(CHUNKS_PER_BLOCK):
            w12_ref[chunk_of((x_, y_), s)] = wi_ref[:, CHUNK * s:CHUNK * (s + 1)].astype(MXU_DTYPE)
            for m, chip in enumerate((nbr_x, nbr_y)):
                sends.append(remote(piece(chunk_of((x_, y_), s), c), ici(m, s), (*chip, c)))
                sends[-1].start()
        for m, chip in enumerate(chips):
            sends.append(remote(sm4_ref.at[k], sml(m), (*chip, c)))
            sends[-1].start()

        def x_copy(rb, slot):
            return pltpu.make_async_copy(x_hbm.at[pl.ds(rb * rb_x, rb_x), :], xbuf.at[slot], x_sems.at[slot])

        x_copy(0, 0).start()
        for rb in range(t // rb_x):
            slot = rb % 2
            x_copy(rb, slot).wait()
            if rb + 1 < t // rb_x:
                x_copy(rb + 1, 1 - slot).start()

            def norm_slab(sl, carry, rb=rb, slot=slot):
                xf = xbuf[slot, pl.ds(pl.multiple_of(sl * SLAB, SLAB), SLAB), :]
                r = lax.rsqrt(jnp.mean(xf * xf, axis=-1, keepdims=True) + RMS_EPS)
                xn_ref[pl.ds(pl.multiple_of(rb * rb_x + sl * SLAB, SLAB), SLAB), :] = ((xf * r) * g_ref[...]).astype(MXU_DTYPE)
                return carry

            lax.fori_loop(0, rb_x // SLAB, norm_slab, 0)

        def out_copy(q, i):
            return pltpu.make_async_copy(obuf.at[i], proj_hbm.at[q, pl.ds(pl.multiple_of(i * rb_mm, rb_mm), rb_mm), :],
                                         o_sems.at[i])

        def project(q, very_first):
            def row_block(i, carry):
                if not very_first:
                    out_copy(q, i).wait()
                obuf[i] = _mm(xn_ref[pl.ds(pl.multiple_of(i * rb_mm, rb_mm), rb_mm), :], w12_ref[q])
                out_copy(q, i).start()
                return carry

            lax.fori_loop(0, n_mm, row_block, 0)

        for s in range(CHUNKS_PER_BLOCK):
            project(chunk_of((x_, y_), s), very_first=(s == 0))

        steps = []
        for s in range(CHUNKS_PER_BLOCK):
            for m, chip in enumerate((nbr_x, nbr_y)):
                onward = [(first, rows, dgn(j), chips[via]) for j, (cs, first, rows, via) in enumerate(DIAG)
                          if cs == s and via == 1 - m]
                steps.append((chunk_of(chip, s), [(0, half, ici(m, s))], onward))
        for s in range(CHUNKS_PER_BLOCK):
            steps.append((chunk_of(diagonal, s), [(first, rows, dgn(j)) for j, (cs, first, rows, _) in enumerate(DIAG) if cs == s], []))

        def project_when_whole(step):
            q, pieces, _ = step
            for first, rows, sem in pieces:
                remote(piece(q, 1 - c, first, rows), to_sib + sem, sib).wait_recv()
            project(q, very_first=False)

        passed = []
        for j, (q, pieces, onward) in enumerate(steps):
            for first, rows, sem in pieces:
                remote(piece(q, c, first, rows), sem, sib).wait_recv()
            for first, rows, sem, chip in onward:
                passed.append(remote(piece(q, c, first, rows), sem, (*chip, c)))
                passed[-1].start()
            for first, rows, sem in pieces:
                passed.append(remote(piece(q, c, first, rows), to_sib + sem, sib))
                passed[-1].start()
            if j > 0:
                project_when_whole(steps[j - 1])
        project_when_whole(steps[-1])

        for m, chip in enumerate(chips):
            remote(sm4_ref.at[2 * chip[0] + chip[1]], sml(m), sib).wait_recv()
        for cp in sends + passed:
            cp.wait_send()
        for i in range(n_mm):
            out_copy(0, i).wait()

    vm = pl.BlockSpec(memory_space=pltpu.VMEM)
    hbm = pl.BlockSpec(memory_space=pl.ANY)
    n_sems = 23
    return pl.pallas_call(
        body,
        out_shape=(jax.ShapeDtypeStruct((N_CHUNKS, t, CHUNK), F32), jax.ShapeDtypeStruct((t, D_MODEL), MXU_DTYPE),
                   jax.ShapeDtypeStruct((N_CHUNKS, w_in.shape[0], CHUNK), MXU_DTYPE),
                   jax.ShapeDtypeStruct((N_CHIPS,) + small.shape, F32)),
        in_specs=[hbm, vm, vm, vm], out_specs=(hbm, vm, vm, vm),
        scratch_shapes=[pltpu.VMEM((2, rb_x, D_MODEL), F32), pltpu.VMEM((n_mm, rb_mm, CHUNK), F32),
                        pltpu.SemaphoreType.DMA((2,)), pltpu.SemaphoreType.DMA((n_mm,)),
                        pltpu.SemaphoreType.DMA((n_sems,)), pltpu.SemaphoreType.DMA((n_sems,))],
        compiler_params=_cp(), name="gather_in_projection",
    )(x, ln_g, w_in, small)


def _allreduce_behind(step, when, in_ref, acc_s, rbufs, out_ref, send_sems, recv_sems):
    x, y, c, _ = _mesh_pos()
    peers = [(x, y, 1 - c), (1 - x, y, c), (x, 1 - y, c)]

    def exchange(ph):
        return pltpu.make_async_remote_copy(src_ref=acc_s, dst_ref=rbufs[ph], send_sem=send_sems.at[ph],
                                            recv_sem=recv_sems.at[ph], device_id=peers[ph], device_id_type=MESH)

    @pl.when(step == when[0])
    def _():
        acc_s[...] = in_ref[...]
        exchange(0).start()

    for ph in (1, 2):
        @pl.when(step == when[ph])
        def _(ph=ph):
            exchange(ph - 1).wait()
            acc_s[...] = acc_s[...] + rbufs[ph - 1][...]
            exchange(ph).start()

    @pl.when(step == when[3])
    def _():
        exchange(2).wait()
        out_ref[...] = acc_s[...] + rbufs[2][...]


def _add_sibling_halves(g, gb, c_arr, name):
    n, rows, cols = g.shape
    half = rows // 2

    def body(c_ref, g_ref, gb_hbm, o_ref, ob_ref, rbuf, send_sems, recv_sems):
        q = pl.program_id(0)
        x, y, c, _ = _mesh_pos()
        theirs = pl.ds(pl.multiple_of(half * (1 - c), half), half)

        def copy(j):
            return pltpu.make_async_remote_copy(src_ref=gb_hbm.at[j, theirs, :], dst_ref=rbuf.at[j], send_sem=send_sems.at[j],
                                                recv_sem=recv_sems.at[j], device_id=(x, y, 1 - c), device_id_type=MESH)

        @pl.when(q == 0)
        def _():
            for j in range(n):
                copy(j).start()

        copy(q).wait_recv()
        s = g_ref[0] + rbuf[q].astype(F32)
        o_ref[0] = s
        ob_ref[0] = s.astype(jnp.bfloat16)

        @pl.when(q == n - 1)
        def _():
            for j in range(n):
                copy(j).wait_send()

    blk = pl.BlockSpec((1, half, cols), lambda q, c_ref: (q, 0, 0))
    return pl.pallas_call(
        body, out_shape=(jax.ShapeDtypeStruct((n, half, cols), F32), jax.ShapeDtypeStruct((n, half, cols), jnp.bfloat16)),
        grid_spec=pltpu.PrefetchScalarGridSpec(
            num_scalar_prefetch=1, grid=(n,),
            in_specs=[pl.BlockSpec((1, half, cols), lambda q, c_ref: (q, c_ref[0], 0)), pl.BlockSpec(memory_space=pl.ANY)],
            out_specs=(blk, blk),
            scratch_shapes=[pltpu.VMEM((n, half, cols), jnp.bfloat16), pltpu.SemaphoreType.DMA((n,)),
                            pltpu.SemaphoreType.DMA((n,))]),
        compiler_params=_cp(ARB), name=name,
    )(c_arr, g, gb)


def _chip_block_copies(s_ref, r_ref, n_sub, send_sems, recv_sems):
    x, y, c, chips = _mesh_pos()
    cps = []
    for m, chip in enumerate(chips):
        kk = 2 * chip[0] + chip[1]
        cps.append(pltpu.make_async_remote_copy(
            src_ref=s_ref.at[pl.ds(n_sub * kk, n_sub)], dst_ref=r_ref.at[m],
            send_sem=send_sems.at[m], recv_sem=recv_sems.at[m], device_id=(*chip, c), device_id_type=MESH))
    return cps


def _gather_w_out(step, n_steps, wo_ref, wob_s, wo4_ref, local_sem, send_sems, recv_sems):
    x, y, c, chips = _mesh_pos()
    sib = (x, y, 1 - c)
    half = wo_ref.shape[0] // 2

    def rows(core):
        return pl.ds(pl.multiple_of(half * core, half), half)

    def block_half(chip, core):
        return wo4_ref.at[2 * chip[0] + chip[1], rows(core), :]

    def remote(src, dst, sem, to):
        return pltpu.make_async_remote_copy(src_ref=src, dst_ref=dst, send_sem=send_sems.at[sem], recv_sem=recv_sems.at[sem],
                                            device_id=to, device_id_type=MESH)

    local = pltpu.make_async_copy(wob_s, wo4_ref.at[2 * x + y], local_sem)
    ici = [remote(wob_s.at[rows(c), :], block_half((x, y), c), m, (*chip, c)) for m, chip in enumerate(chips)]
    fwd = [remote(block_half(chip, c), block_half(chip, c), 3 + m, sib) for m, chip in enumerate(chips)]

    @pl.when(step == 0)
    def _():
        wob_s[...] = wo_ref[...].astype(MXU_DTYPE)
        local.start()
        for cp in ici:
            cp.start()

    @pl.when(step == n_steps // 2)
    def _():
        for m, chip in enumerate(chips):
            remote(block_half(chip, c), block_half(chip, c), m, sib).wait_recv()
            fwd[m].start()

    @pl.when(step == n_steps - 1)
    def _():
        for m, chip in enumerate(chips):
            remote(block_half(chip, 1 - c), block_half(chip, 1 - c), 3 + m, sib).wait_recv()
        for cp in ici + fwd:
            cp.wait_send()
        local.wait()


def _chip_blocks_shape(s, n_sub):
    return jax.ShapeDtypeStruct((3, n_sub) + s.shape[1:], s.dtype)


def _finish_gradients(s_in, r_in, s_out, r_out, v):
    n_dev = 8
    n_in, n_out = r_in.shape[1], r_out.shape[1]

    def body(si_hbm, ri_hbm, so_hbm, ro_hbm, v_ref, fi_hbm, fo_hbm, tot_ref,
             a_in, b_in, a_out, b_out, slots, load_sems, store_sems, send_sems, recv_sems):
        x, y, c, _ = _mesh_pos()
        k = 2 * x + y
        sib = (x, y, 1 - c)
        me = 4 * x + 2 * y + c
        loads = [pltpu.make_async_copy(si_hbm.at[pl.ds(n_in * k, n_in)], a_in, load_sems.at[0]),
                 pltpu.make_async_copy(ri_hbm, b_in, load_sems.at[1]),
                 pltpu.make_async_copy(so_hbm.at[pl.ds(n_out * k, n_out)], a_out, load_sems.at[2]),
                 pltpu.make_async_copy(ro_hbm, b_out, load_sems.at[3])]
        for cp in loads:
            cp.start()
        slots[me] = v_ref[...]

        def remote(src, dst, sem, to):
            return pltpu.make_async_remote_copy(src_ref=src, dst_ref=dst, send_sem=send_sems.at[sem],
                                                recv_sem=recv_sems.at[sem], device_id=to, device_id_type=MESH)

        small = []
        for d in range(1, n_dev):
            peer = (1 - x if d & 4 else x, 1 - y if d & 2 else y, 1 - c if d & 1 else c)
            small.append(remote(slots.at[me], slots.at[me], d - 1, peer))
            small[-1].start()
        for cp in loads:
            cp.wait()
        big = []
        for j, (a, b, f_hbm) in enumerate(((a_in, b_in, fi_hbm), (a_out, b_out, fo_hbm))):
            a[...] = ((a[...] + b[0].astype(F32)) + b[1].astype(F32)) + b[2].astype(F32)
            half = a.shape[1]
            mine = f_hbm.at[:, pl.ds(pl.multiple_of(half * c, half), half), :]
            big.append(pltpu.make_async_copy(a, mine, store_sems.at[j]))
            big.append(remote(a, mine, n_dev - 1 + j, sib))
        for cp in big:
            cp.start()
        for cp in small + big:
            cp.wait()
        total = slots[0]
        for dev in range(1, n_dev):
            total = total + slots[dev]
        tot_ref[...] = total

    hbm = pl.BlockSpec(memory_space=pl.ANY)
    vm = pl.BlockSpec(memory_space=pltpu.VMEM)
    full = lambda s, n: (n, 2 * s.shape[1], s.shape[2])
    return pl.pallas_call(
        body,
        out_shape=(jax.ShapeDtypeStruct(full(s_in, n_in), F32), jax.ShapeDtypeStruct(full(s_out, n_out), F32),
                   jax.ShapeDtypeStruct(v.shape, F32)),
        in_specs=[hbm, hbm, hbm, hbm, vm], out_specs=(hbm, hbm, vm),
        scratch_shapes=[pltpu.VMEM((n_in,) + s_in.shape[1:], F32), pltpu.VMEM(r_in.shape, r_in.dtype),
                        pltpu.VMEM((n_out,) + s_out.shape[1:], F32), pltpu.VMEM(r_out.shape, r_out.dtype),
                        pltpu.VMEM((n_dev,) + v.shape, F32), pltpu.SemaphoreType.DMA((4,)), pltpu.SemaphoreType.DMA((2,)),
                        pltpu.SemaphoreType.DMA((n_dev + 1,)), pltpu.SemaphoreType.DMA((n_dev + 1,))],
        compiler_params=_cp(), name="finish_gradients",
    )(s_in, r_in, s_out, r_out, v)


def _out_projection_loss(yc, yl, x, target, wo, final_g):
    t = x.shape[0]
    tm = 512

    def body(yc_ref, yl_ref, x_ref, t_ref, wo_ref, fg_ref, do_ref, dob_ref, dy_ref, st_ref, y_wo):
        @pl.when(pl.program_id(0) == 0)
        def _():
            st_ref[...] = jnp.zeros_like(st_ref)

        y_wo[...] = _mm(yc_ref[...], wo_ref[0:D_PART, :]) + _mm(yl_ref[...], wo_ref[D_PART:2 * D_PART, :])

        def norm_loss_slab(s, carry):
            g_sum, loss_sum = carry
            rows = pl.ds(pl.multiple_of(s * SLAB, SLAB), SLAB)
            o = x_ref[rows, :] + y_wo[rows, :]
            r2 = lax.rsqrt(jnp.mean(o * o, axis=-1, keepdims=True) + RMS_EPS)
            ohat = o * r2
            fg = fg_ref[...]
            diff = ohat * fg - t_ref[rows, :]
            dout = diff * (1.0 / D_MODEL)
            gp = dout * fg
            do = r2 * (gp - ohat * jnp.mean(gp * ohat, axis=-1, keepdims=True))
            do_ref[rows, :] = do
            dob_ref[rows, :] = do.astype(MXU_DTYPE)
            loss = 0.5 * jnp.sum(jnp.sum(diff * diff, axis=-1, keepdims=True) * (1.0 / D_MODEL), axis=0, keepdims=True)
            return g_sum + jnp.sum(dout * ohat, axis=0, keepdims=True), loss_sum + loss

        g_sum, loss_sum = lax.fori_loop(0, tm // SLAB, norm_loss_slab,
                                        (jnp.zeros((1, D_MODEL), F32), jnp.zeros((1, 1), F32)))
        st_ref[0:1, :] += g_sum
        st_ref[1:2, :] += jnp.broadcast_to(loss_sum, (1, D_MODEL))
        dy_ref[...] = _mm_nt(dob_ref[...], wo_ref[...])

    row = lambda i: (i, 0)
    fix = lambda i: (0, 0)
    return pl.pallas_call(
        body, grid=(t // tm,),
        in_specs=[pl.BlockSpec((tm, D_PART), row), pl.BlockSpec((tm, D_PART), row),
                  pl.BlockSpec((tm, D_MODEL), row), pl.BlockSpec((tm, D_MODEL), row),
                  pl.BlockSpec((2 * D_PART, D_MODEL), fix), pl.BlockSpec((1, D_MODEL), fix)],
        out_specs=(pl.BlockSpec((tm, D_MODEL), row), pl.BlockSpec((tm, D_MODEL), row),
                   pl.BlockSpec((tm, 2 * D_PART), row), pl.BlockSpec((SUBLANES, D_MODEL), fix)),
        out_shape=(jax.ShapeDtypeStruct((t, D_MODEL), F32), jax.ShapeDtypeStruct((t, D_MODEL), MXU_DTYPE),
                   jax.ShapeDtypeStruct((t, 2 * D_PART), F32), jax.ShapeDtypeStruct((SUBLANES, D_MODEL), F32)),
        scratch_shapes=[pltpu.VMEM((tm, D_MODEL), F32)],
        compiler_params=_cp(ARB), name="out_projection_loss",
    )(yc, yl, x, target, wo, final_g)


def _input_grad(dproj, w12, x, do, ln_g, sb_in):
    t = x.shape[0]
    tm = 1024

    def body(dp_ref, w_ref, x_ref, do_ref, g_ref, s_ref, gx_ref, st_ref, r_ref, acc, send_sems, recv_sems):
        i, p = pl.program_id(0), pl.program_id(1)

        @pl.when((i == 0) & (p == 0))
        def _():
            st_ref[...] = jnp.zeros_like(st_ref)
            for cp in _chip_block_copies(s_ref, r_ref, CHUNKS_PER_BLOCK, send_sems, recv_sems):
                cp.start()

        @pl.when((i == t // tm - 1) & (p == N_PARTS - 1))
        def _():
            for cp in _chip_block_copies(s_ref, r_ref, CHUNKS_PER_BLOCK, send_sems, recv_sems):
                cp.wait()

        @pl.when(p == 0)
        def _():
            acc[...] = jnp.zeros_like(acc)

        acc[...] += _mm_nt(dp_ref[0], jnp.concatenate([w_ref[0], w_ref[1]], axis=1))

        @pl.when(p == N_PARTS - 1)
        def _():
            def norm_bwd_slab(s, g_sum):
                rows = pl.ds(pl.multiple_of(s * SLAB, SLAB), SLAB)
                xf = x_ref[rows, :]
                r = lax.rsqrt(jnp.mean(xf * xf, axis=-1, keepdims=True) + RMS_EPS)
                xhat = xf * r
                dxn = acc[rows, :]
                dxh = dxn * g_ref[...]
                gx_ref[rows, :] = do_ref[rows, :] + r * (dxh - xhat * jnp.mean(dxh * xhat, axis=-1, keepdims=True))
                return g_sum + jnp.sum(dxn * xhat, axis=0, keepdims=True)

            st_ref[0:1, :] += lax.fori_loop(0, tm // SLAB, norm_bwd_slab, jnp.zeros((1, D_MODEL), F32))

    row = lambda i, p: (i, 0)
    fix = lambda i, p: (0, 0)
    return pl.pallas_call(
        body, grid=(t // tm, N_PARTS),
        in_specs=[
            pl.BlockSpec((1, tm, D_PART), lambda i, p: (p, i, 0)),
            pl.BlockSpec((2, D_MODEL, CHUNK), lambda i, p: (p, 0, 0)),
            pl.BlockSpec((tm, D_MODEL), row), pl.BlockSpec((tm, D_MODEL), row), pl.BlockSpec((1, D_MODEL), fix),
            pl.BlockSpec(memory_space=pl.ANY)],
        out_specs=(pl.BlockSpec((tm, D_MODEL), row), pl.BlockSpec((SUBLANES, D_MODEL), fix),
                   pl.BlockSpec(memory_space=pl.ANY)),
        out_shape=(jax.ShapeDtypeStruct((t, D_MODEL), F32), jax.ShapeDtypeStruct((SUBLANES, D_MODEL), F32),
                   _chip_blocks_shape(sb_in, CHUNKS_PER_BLOCK)),
        scratch_shapes=[pltpu.VMEM((tm, D_MODEL), F32), pltpu.SemaphoreType.DMA((3,)), pltpu.SemaphoreType.DMA((3,))],
        compiler_params=_cp(ARB, ARB), name="input_grad",
    )(dproj, w12, x, do, ln_g, sb_in)


def _w_in_grad(xn, dproj, small):
    t = xn.shape[0]
    half = D_MODEL // 2
    small_shape = pltpu.VMEM(small.shape, F32)

    def body(xn_ref, dp_ref, sm_ref, s_ref, sb_ref, red_ref, acc_s, r0, r1, r2, g_s, mine_s, send_s, rbuf,
             ar_send, ar_recv, part_send, part_recv):
        p = pl.program_id(0)
        _allreduce_behind(p, (0, 1, 3, N_PARTS - 1), sm_ref, acc_s, (r0, r1, r2), red_ref, ar_send, ar_recv)
        x, y, c, _ = _mesh_pos()
        slot = p % 2

        def part_copy(sl):
            return pltpu.make_async_remote_copy(src_ref=send_s.at[sl], dst_ref=rbuf.at[sl], send_sem=part_send.at[sl],
                                                recv_sem=part_recv.at[sl], device_id=(x, y, 1 - c), device_id_type=MESH)

        @pl.when(p >= 1)
        def _():
            part_copy(1 - slot).wait_recv()
            s = mine_s[1 - slot] + rbuf[1 - slot].astype(F32)
            s_ref[...] = s
            sb_ref[...] = s.astype(jnp.bfloat16)

        @pl.when(p < N_PARTS)
        def _():
            @pl.when(p >= 2)
            def _():
                part_copy(slot).wait_send()

            g_s[...] = _mm_tn(xn_ref[...], dp_ref[0])
            for j in range(2):
                cols = slice(CHUNK * j, CHUNK * (j + 1))
                mine_s[slot, j] = g_s[pl.ds(pl.multiple_of(half * c, half), half), cols]
                send_s[slot, j] = g_s[pl.ds(pl.multiple_of(half * (1 - c), half), half), cols].astype(jnp.bfloat16)
            part_copy(slot).start()

        @pl.when(p == N_PARTS)
        def _():
            for sl in range(2):
                part_copy(sl).wait_send()

    whole = pl.BlockSpec(small.shape, lambda p: (0, 0))
    pair = pl.BlockSpec((2, half, CHUNK), lambda p: (jnp.maximum(p - 1, 0), 0, 0))
    return pl.pallas_call(
        body, grid=(N_PARTS + 1,),
        in_specs=[pl.BlockSpec((t, D_MODEL), lambda p: (0, 0)),
                  pl.BlockSpec((1, t, D_PART), lambda p: (jnp.minimum(p, N_PARTS - 1), 0, 0)), whole],
        out_specs=(pair, pair, whole),
        out_shape=(jax.ShapeDtypeStruct((N_CHUNKS, half, CHUNK), F32),
                   jax.ShapeDtypeStruct((N_CHUNKS, half, CHUNK), jnp.bfloat16), jax.ShapeDtypeStruct(small.shape, F32)),
        scratch_shapes=[small_shape] * 4 + [
            pltpu.VMEM((D_MODEL, D_PART), F32), pltpu.VMEM((2, 2, half, CHUNK), F32),
            pltpu.VMEM((2, 2, half, CHUNK), jnp.bfloat16), pltpu.VMEM((2, 2, half, CHUNK), jnp.bfloat16),
            pltpu.SemaphoreType.DMA((3,)), pltpu.SemaphoreType.DMA((3,)),
            pltpu.SemaphoreType.DMA((2,)), pltpu.SemaphoreType.DMA((2,))],
        compiler_params=_cp(ARB), name="w_in_grad",
    )(xn, dproj, small)


def _w_out_grad(yc, yl, dob):
    t = yc.shape[0]
    tk = 2048

    def body(yc_ref, yl_ref, do_ref, o_ref, ob_ref):
        j, kk = pl.program_id(0), pl.program_id(1)

        def accumulate(y_ref):
            @pl.when(kk == 0)
            def _():
                o_ref[...] = jnp.zeros_like(o_ref)

            o_ref[...] += _mm_tn(y_ref[...], do_ref[...])

            @pl.when(kk == t // tk - 1)
            def _():
                ob_ref[...] = o_ref[...].astype(jnp.bfloat16)

        pl.when(j == 0)(functools.partial(accumulate, yc_ref))
        pl.when(j == 1)(functools.partial(accumulate, yl_ref))

    def rows_of(half):
        return lambda j, kk: (jnp.where(j == half, kk, 0), 0)

    half = pl.BlockSpec((D_PART, D_MODEL), lambda j, kk: (j, 0))
    out, out_b = pl.pallas_call(
        body, grid=(2, t // tk),
        in_specs=[pl.BlockSpec((tk, D_PART), rows_of(0)), pl.BlockSpec((tk, D_PART), rows_of(1)),
                  pl.BlockSpec((tk, D_MODEL), lambda j, kk: (kk, 0))],
        out_specs=(half, half),
        out_shape=(jax.ShapeDtypeStruct((2 * D_PART, D_MODEL), F32), jax.ShapeDtypeStruct((2 * D_PART, D_MODEL), jnp.bfloat16)),
        compiler_params=_cp(ARB, ARB), name="w_out_grad",
    )(yc, yl, dob)
    blocks = (N_CHIPS, 2 * D_PART // N_CHIPS, D_MODEL)
    return out.reshape(blocks), out_b.reshape(blocks)


def _for_groups(n, fn, init, unroll=UNROLL, stores=(), descending=False):
    assert unroll % 2 == 0 and n % unroll == 0

    def trip(j, carry):
        held = None
        for uu in range(unroll):
            idx = j * unroll + uu
            carry, values = fn(idx, carry)
            if uu % 2 == 0:
                held = values
                continue
            low_group = n - 1 - idx if descending else idx - 1
            rows = pl.ds(pl.multiple_of(low_group * SUBLANES, 2 * SUBLANES), 2 * SUBLANES)
            pairs = zip(values, held) if descending else zip(held, values)
            for store, (lo, hi) in zip(stores, pairs, strict=True):
                store(rows, jnp.concatenate([lo, hi], axis=0).astype(MXU_DTYPE))
        return carry

    return lax.fori_loop(0, n // unroll, trip, init)


def _rows_of(ref, *lead, cols=slice(None)):
    def store(rows, value):
        ref[(*lead, rows, cols)] = value

    return store


def _pvb(pv_ref, r):
    return jnp.broadcast_to(pv_ref[r:r + 1, :], (SUBLANES, pv_ref.shape[1]))


def _conv3(pv_ref, u, u1, u2):
    return (_pvb(pv_ref, PV_CONV_W) * u2 + _pvb(pv_ref, PV_CONV_W + 1) * u1) + _pvb(pv_ref, PV_CONV_W + 2) * u


def _conv4(pv_ref, v, v1, v2, v3):
    return ((((_pvb(pv_ref, PV_LRU_W) * v3 + _pvb(pv_ref, PV_LRU_W + 1) * v2) + _pvb(pv_ref, PV_LRU_W + 2) * v1)
             + _pvb(pv_ref, PV_LRU_W + 3) * v) + _pvb(pv_ref, PV_LRU_B))


def _mixer_forward(proj, pvec, wai, w_out):
    t = proj.shape[1]
    tb = 512
    ng = tb // SUBLANES
    nt = t // tb
    lw = FWD_LW
    ns = D_PART // lw
    per_chunk = CHUNK // lw

    def body(bg_ref, cg_ref, xc_ref, gc_ref, xl_ref, gl_ref, pv_ref, wai_ref, wo_ref,
             yc_ref, yl_ref, h_ref, u_s, r_ref, ig_ref, wo4_ref,
             ucp_s, xlp_s, ls_s, hbuf_s, ub_s, gate_s, wob_s, local_sem, send_sems, recv_sems):
        _gather_w_out(pl.program_id(0) * nt + pl.program_id(1), ns * nt, wo_ref, wob_s, wo4_ref, local_sem, send_sems, recv_sems)

        @pl.when(pl.program_id(1) == 0)
        def _():
            ucp_s[...] = jnp.zeros_like(ucp_s)
            xlp_s[...] = jnp.zeros_like(xlp_s)
            hbuf_s[...] = jnp.zeros_like(hbuf_s)

        row = lax.broadcasted_iota(jnp.int32, (SUBLANES, lw), 0)
        ls_s[...] = RG_LRU_C * _log_sigmoid(_pvb(pv_ref, PV_LAM))

        def conv_group(g, carry):
            ucp, xlp = carry
            sl = pl.ds(pl.multiple_of(g * SUBLANES, SUBLANES), SUBLANES)
            uc = cg_ref[sl, :] * xc_ref[sl, :]
            v = _conv3(pv_ref, uc, _shift_down(uc, ucp, 1, row), _shift_down(uc, ucp, 2, row))
            yc = bg_ref[sl, :] * v
            rr = lax.rsqrt(_head_mean(yc * yc, CONV_HEAD) + RMS_EPS)
            gc = gc_ref[sl, :]
            zc = ((yc * rr) * _pvb(pv_ref, PV_CG)) * (gc * _sigmoid(gc))
            xl = xl_ref[sl, :]
            u = _conv4(pv_ref, xl, _shift_down(xl, xlp, 1, row), _shift_down(xl, xlp, 2, row), _shift_down(xl, xlp, 3, row))
            u_s[sl, :] = u
            return (uc, xl), (zc, u)

        ucp, xlp = _for_groups(ng, conv_group, (ucp_s[...], xlp_s[...]), unroll=2 * UNROLL,
                               stores=(_rows_of(yc_ref), _rows_of(ub_s)))
        ucp_s[...] = ucp
        xlp_s[...] = xlp

        gate_s[...] = _mm(ub_s[...], wai_ref[0])

        def lru_group(g, h_before):
            sl = pl.ds(pl.multiple_of(g * SUBLANES, SUBLANES), SUBLANES)
            u = u_s[sl, :]
            r = _sigmoid(gate_s[sl, 0:lw] + _pvb(pv_ref, PV_BA))
            ig = _sigmoid(gate_s[sl, lw:2 * lw] + _pvb(pv_ref, PV_BI))
            r_ref[sl, :] = r
            ig_ref[sl, :] = ig
            a, _, mult, _ = _decay(r, ls_s[...])
            A, B = _scan8_fwd(a, mult * (ig * u), row)
            h = B + A * jnp.broadcast_to(h_before[SUBLANES - 1:SUBLANES, :], (SUBLANES, lw))
            h_ref[sl, :] = h
            rr = lax.rsqrt(_head_mean(h * h, LRU_HEAD) + RMS_EPS)
            gl = gl_ref[sl, :]
            return h, (((h * rr) * _pvb(pv_ref, PV_LG)) * (gl * _sigmoid(gl)),)

        hbuf_s[...] = _for_groups(ng, lru_group, hbuf_s[...], unroll=2 * UNROLL, stores=(_rows_of(yl_ref),))

    def part(p):
        return pl.BlockSpec((None, tb, lw), lambda c, i: (2 * p + c // per_chunk, i, c % per_chunk))

    strip = pl.BlockSpec((tb, lw), lambda c, i: (i, c))
    small = pltpu.VMEM((SUBLANES, lw), F32)
    return pl.pallas_call(
        body, grid=(ns, nt),
        in_specs=[part(p) for p in range(N_PARTS)] + [
            pl.BlockSpec((PV_ROWS, lw), lambda c, i: (0, c)),
            pl.BlockSpec((1, lw, 2 * lw), lambda c, i: (c, 0, 0)),
            pl.BlockSpec(w_out.shape, lambda c, i: (0, 0))],
        out_specs=(strip,) * 6 + (pl.BlockSpec(memory_space=pl.ANY),),
        out_shape=(jax.ShapeDtypeStruct((t, D_PART), MXU_DTYPE),) * 2 + (jax.ShapeDtypeStruct((t, D_PART), F32),) * 4 + (
            jax.ShapeDtypeStruct((N_CHIPS,) + w_out.shape, MXU_DTYPE),),
        scratch_shapes=[small, small, small, small, pltpu.VMEM((tb, lw), MXU_DTYPE),
                        pltpu.VMEM((tb, 2 * lw), F32), pltpu.VMEM(w_out.shape, MXU_DTYPE),
                        pltpu.SemaphoreType.DMA, pltpu.SemaphoreType.DMA((6,)), pltpu.SemaphoreType.DMA((6,))],
        compiler_params=_cp(ARB, ARB), name="mixer_forward",
    )(proj, proj, proj, proj, proj, proj, pvec, wai, w_out)


def _mixer_backward(proj, h, u, r, ig, dy, pvec, wai, sb_out):
    t = proj.shape[1]
    tb = 1024
    ng = tb // SUBLANES
    nt = t // tb
    gpb = tb // SUBLANES

    def body(bg_ref, cg_ref, xc_ref, gc_ref, xl_ref, gl_ref, h_ref, u_ref, r_ref, ig_ref, dyc_ref, dyl_ref,
             cgh_ref, xch_ref, xlh_ref, hh_ref, pv_ref, wai_ref, so_ref,
             dp_ref, gw_ref, sv_ref, ro_ref,
             ls_s, ub_s, uce_s, xle_s, he_s, dgb_s, du_s, gbuf_s,
             acc_s, an_s, dvn_s, dun_s, send_sems, recv_sems):
        i = pl.program_id(1)
        first_block = i == nt - 1

        @pl.when((pl.program_id(0) == 0) & (i == 0))
        def _():
            for cp in _chip_block_copies(so_ref, ro_ref, 1, send_sems, recv_sems):
                cp.start()

        @pl.when((pl.program_id(0) == NS - 1) & (i == nt - 1))
        def _():
            for cp in _chip_block_copies(so_ref, ro_ref, 1, send_sems, recv_sems):
                cp.wait()

        @pl.when(i == 0)
        def _():
            acc_s[...] = jnp.zeros_like(acc_s)
            gw_ref[...] = jnp.zeros_like(gw_ref)
            an_s[...] = jnp.zeros_like(an_s)
            dvn_s[...] = jnp.zeros_like(dvn_s)
            dun_s[...] = jnp.zeros_like(dun_s)
            gbuf_s[...] = jnp.zeros_like(gbuf_s)

        row = lax.broadcasted_iota(jnp.int32, (SUBLANES, LW), 0)
        ls_s[...] = RG_LRU_C * _log_sigmoid(_pvb(pv_ref, PV_LAM))
        keep = jnp.where(first_block, 0.0, 1.0)
        uce_s[0:SUBLANES, :] = (cgh_ref[...] * xch_ref[...]) * keep
        xle_s[0:SUBLANES, :] = xlh_ref[...] * keep
        he_s[0:SUBLANES, :] = hh_ref[...] * keep
        xle_s[SUBLANES:SUBLANES + tb, :] = xl_ref[...]
        he_s[SUBLANES:SUBLANES + tb, :] = h_ref[...]

        uce_s[SUBLANES:SUBLANES + tb, :] = cg_ref[...] * xc_ref[...]

        def acc_add(k, v):
            acc_s[k] += v

        def main_group(gi, carry):
            a_next, dv_next, g_next = carry
            g = ng - 1 - gi
            r0 = pl.multiple_of(g * SUBLANES, SUBLANES)
            sl = pl.ds(r0, SUBLANES)
            sl_e = pl.ds(r0 + SUBLANES, SUBLANES)
            lsb = ls_s[...]
            u = u_ref[sl, :]
            r = r_ref[sl, :]
            ig = ig_ref[sl, :]
            a, e2, mult, inv_mult = _decay(r, lsb)
            gl = gl_ref[sl, :]
            sg = _sigmoid(gl)
            s_l = gl * sg
            h8 = he_s[sl_e, :]
            hprev = _shift_down(h8, he_s[sl, :], 1, row)
            rr = lax.rsqrt(_head_mean(h8 * h8, LRU_HEAD) + RMS_EPS)
            n = h8 * rr
            dz = dyl_ref[sl, :]
            lg = _pvb(pv_ref, PV_LG)
            acc_add(PV_LG, (dz * n) * s_l)
            p5 = ((dz * n) * lg) * (sg + s_l * (1.0 - sg))
            dn = (dz * lg) * s_l
            dh = rr * (dn - n * _head_mean(dn * n, LRU_HEAD))
            A, B = _scan8_rev(_shift_up(a, a_next, 1, row), dh, row)
            gg = B + A * jnp.broadcast_to(g_next[0:1, :], (SUBLANES, LW))
            da = gg * hprev
            iu = ig * u
            diu = gg * mult
            dla = da * a - (gg * iu) * (e2 * inv_mult)
            acc_add(PV_LAM, dla * r)
            dra = (dla * lsb) * (r * (1.0 - r))
            dia = (diu * u) * (ig * (1.0 - ig))
            acc_add(PV_BA, dra)
            acc_add(PV_BI, dia)
            du_s[sl, :] = diu * ig
            bg = bg_ref[sl, :]
            gc = gc_ref[sl, :]
            uc = uce_s[sl_e, :]
            ucp = uce_s[sl, :]
            uc1 = _shift_down(uc, ucp, 1, row)
            uc2 = _shift_down(uc, ucp, 2, row)
            v = _conv3(pv_ref, uc, uc1, uc2)
            yc = bg * v
            rrc = lax.rsqrt(_head_mean(yc * yc, CONV_HEAD) + RMS_EPS)
            nc = yc * rrc
            sgc = _sigmoid(gc)
            s_c = gc * sgc
            dzc = dyc_ref[sl, :]
            cgain = _pvb(pv_ref, PV_CG)
            acc_add(PV_CG, (dzc * nc) * s_c)
            p3 = ((dzc * nc) * cgain) * (sgc + s_c * (1.0 - sgc))
            dnc = (dzc * cgain) * s_c
            dyc = rrc * (dnc - nc * _head_mean(dnc * nc, CONV_HEAD))
            dv = dyc * bg
            duc = (_pvb(pv_ref, PV_CONV_W + 2) * dv + _pvb(pv_ref, PV_CONV_W + 1) * _shift_up(dv, dv_next, 1, row)
                   + _pvb(pv_ref, PV_CONV_W) * _shift_up(dv, dv_next, 2, row))
            acc_add(PV_CONV_W + 2, dv * uc)
            acc_add(PV_CONV_W + 1, dv * uc1)
            acc_add(PV_CONV_W, dv * uc2)
            return (a, dv, gg), (dyc * v, duc * xc_ref[sl, :], duc * cg_ref[sl, :], p3, p5, dra, dia, u)

        a_next, dv_next, g_next = _for_groups(
            ng, main_group, (an_s[...], dvn_s[...], gbuf_s[...]), descending=True,
            stores=(_rows_of(dp_ref, 0), _rows_of(dp_ref, 1), _rows_of(dp_ref, 2), _rows_of(dp_ref, 3), _rows_of(dp_ref, 5),
                    _rows_of(dgb_s, cols=slice(0, LW)), _rows_of(dgb_s, cols=slice(LW, 2 * LW)), _rows_of(ub_s)))
        an_s[...] = a_next
        dvn_s[...] = dv_next
        gbuf_s[...] = g_next

        dgb = dgb_s[...]
        du_s[...] += _mm_nt(dgb, wai_ref[0])
        gw_ref[0] += _mm_tn(ub_s[...], dgb)

        def lru_conv_group(gi, du_next):
            g = ng - 1 - gi
            r0 = pl.multiple_of(g * SUBLANES, SUBLANES)
            sl = pl.ds(r0, SUBLANES)
            du = du_s[sl, :]
            xl = xle_s[pl.ds(r0 + SUBLANES, SUBLANES), :]
            xlp = xle_s[sl, :]
            acc_add(PV_LRU_B, du)
            acc_add(PV_LRU_W + 3, du * xl)
            acc_add(PV_LRU_W + 2, du * _shift_down(xl, xlp, 1, row))
            acc_add(PV_LRU_W + 1, du * _shift_down(xl, xlp, 2, row))
            acc_add(PV_LRU_W, du * _shift_down(xl, xlp, 3, row))
            dxl = (((_pvb(pv_ref, PV_LRU_W + 3) * du + _pvb(pv_ref, PV_LRU_W + 2) * _shift_up(du, du_next, 1, row))
                    + _pvb(pv_ref, PV_LRU_W + 1) * _shift_up(du, du_next, 2, row))
                   + _pvb(pv_ref, PV_LRU_W) * _shift_up(du, du_next, 3, row))
            return du, (dxl,)

        dun_s[...] = _for_groups(ng, lru_conv_group, dun_s[...], descending=True, stores=(_rows_of(dp_ref, 4),))

        @pl.when(first_block)
        def _():
            sv_ref[...] = jnp.zeros_like(sv_ref)
            for k in range(N_ACC):
                tot = jnp.sum(acc_s[k], axis=0, keepdims=True)
                if k == PV_LAM:
                    tot = (RG_LRU_C * tot) / (1.0 + jnp.exp(pv_ref[PV_LAM:PV_LAM + 1, :]))
                sv_ref[k:k + 1, :] = tot

    def part(p):
        return pl.BlockSpec((None, tb, LW), lambda c, i: (2 * p + c // STRIPS_PER_CHUNK, nt - 1 - i, c % STRIPS_PER_CHUNK))

    def halo(p):
        return pl.BlockSpec((None, SUBLANES, LW), lambda c, i: (2 * p + c // STRIPS_PER_CHUNK,
                                                                jnp.maximum((nt - 1 - i) * gpb - 1, 0), c % STRIPS_PER_CHUNK))

    strip = pl.BlockSpec((tb, LW), lambda c, i: (nt - 1 - i, c))
    big = pltpu.VMEM((tb, LW), F32)
    big_e = pltpu.VMEM((tb + SUBLANES, LW), F32)
    small = pltpu.VMEM((SUBLANES, LW), F32)
    outs = pl.pallas_call(
        body, grid=(NS, nt),
        in_specs=[part(p) for p in range(N_PARTS)] + [
            strip, strip, strip, strip, strip, pl.BlockSpec((tb, LW), lambda c, i: (nt - 1 - i, NS + c)),
            halo(1), halo(2), halo(4),
            pl.BlockSpec((SUBLANES, LW), lambda c, i: (jnp.maximum((nt - 1 - i) * gpb - 1, 0), c)),
            pl.BlockSpec((PV_ROWS, LW), lambda c, i: (0, c)),
            pl.BlockSpec((1, LW, 2 * LW), lambda c, i: (c, 0, 0)),
            pl.BlockSpec(memory_space=pl.ANY)],
        out_specs=(pl.BlockSpec((N_PARTS, tb, LW), lambda c, i: (0, nt - 1 - i, c)),
                   pl.BlockSpec((1, LW, 2 * LW), lambda c, i: (c, 0, 0)),
                   pl.BlockSpec((PV_ROWS, LW), lambda c, i: (0, c)),
                   pl.BlockSpec(memory_space=pl.ANY)),
        out_shape=(jax.ShapeDtypeStruct((N_PARTS, t, D_PART), MXU_DTYPE),
                   jax.ShapeDtypeStruct((NS, LW, 2 * LW), F32), jax.ShapeDtypeStruct((PV_ROWS, D_PART), F32),
                   _chip_blocks_shape(sb_out, 1)),
        scratch_shapes=[small, pltpu.VMEM((tb, LW), MXU_DTYPE), big_e, big_e, big_e,
                        pltpu.VMEM((tb, 2 * LW), MXU_DTYPE), big, small,
                        pltpu.VMEM((N_ACC, SUBLANES, LW), F32), small, small, small,
                        pltpu.SemaphoreType.DMA((3,)), pltpu.SemaphoreType.DMA((3,))],
        compiler_params=_cp(ARB, ARB), name="mixer_backward",
    )(proj, proj, proj, proj, proj, proj, h, u, r, ig, dy, dy, proj, proj, proj, h, pvec, wai, sb_out)
    return outs


def _adamw(w, g, m, v):
    m = ADAM_B1 * m + (1.0 - ADAM_B1) * g
    v = ADAM_B2 * v + (1.0 - ADAM_B2) * (g * g)
    m_hat = m / (1.0 - ADAM_B1 ** ADAM_STEP)
    v_hat = v / (1.0 - ADAM_B2 ** ADAM_STEP)
    delta = -ADAM_LR * (m_hat / (jnp.sqrt(v_hat) + ADAM_EPS) + ADAM_WD * w)
    return delta, m, v


def _adam_w_in(w, m, v, g3):
    rows, cols = w.shape
    tr = 128

    def body(w_ref, m_ref, v_ref, g_ref, go_ref, d_ref, mo_ref, vo_ref):
        for s in range(CHUNKS_PER_BLOCK):
            cs = slice(CHUNK * s, CHUNK * (s + 1))
            g = g_ref[s]
            d, mn, vn = _adamw(w_ref[:, cs], g, m_ref[:, cs], v_ref[:, cs])
            go_ref[:, cs] = g
            d_ref[:, cs] = d
            mo_ref[:, cs] = mn
            vo_ref[:, cs] = vn

    blk = pl.BlockSpec((tr, cols), lambda i: (i, 0))
    return pl.pallas_call(
        body, grid=(rows // tr,),
        in_specs=[blk, blk, blk, pl.BlockSpec((CHUNKS_PER_BLOCK, tr, CHUNK), lambda i: (0, i, 0))],
        out_specs=(blk,) * 4, out_shape=(jax.ShapeDtypeStruct(w.shape, F32),) * 4,
        compiler_params=_cp(ARB), name="adam_w_in",
    )(w, m, v, g3)


def _adam_w_out(w, m, v, g):
    rows, cols = w.shape
    tr = 128

    def body(w_ref, m_ref, v_ref, g_ref, d_ref, mo_ref, vo_ref):
        d_ref[...], mo_ref[...], vo_ref[...] = _adamw(w_ref[...], g_ref[...], m_ref[...], v_ref[...])

    blk = pl.BlockSpec((tr, cols), lambda i: (i, 0))
    return pl.pallas_call(
        body, grid=(rows // tr,), in_specs=[blk] * 4, out_specs=(blk,) * 3,
        out_shape=(jax.ShapeDtypeStruct(w.shape, F32),) * 3,
        compiler_params=_cp(ARB), name="adam_w_out",
    )(w, m, v, g)


def _adam_small(ws, ms, vs, gs):
    n = len(ws)

    def body(*refs):
        w_r, m_r, v_r, g_r = refs[0:n], refs[n:2 * n], refs[2 * n:3 * n], refs[3 * n:4 * n]
        d_o, m_o, v_o = refs[4 * n:5 * n], refs[5 * n:6 * n], refs[6 * n:7 * n]
        for j in range(n):
            d_o[j][...], m_o[j][...], v_o[j][...] = _adamw(w_r[j][...], g_r[j][...], m_r[j][...], v_r[j][...])

    vm = pl.BlockSpec(memory_space=pltpu.VMEM)
    shapes = tuple(jax.ShapeDtypeStruct(w.shape, F32) for w in ws)
    outs = pl.pallas_call(
        body, in_specs=[vm] * (4 * n), out_specs=(vm,) * (3 * n), out_shape=shapes * 3,
        compiler_params=_cp(), name="adam_small",
    )(*ws, *ms, *vs, *gs)
    return outs[0:n], outs[n:2 * n], outs[2 * n:3 * n]


def _block_diag_strips(w, lw):
    heads = lw // LRU_HEAD
    w4 = w.reshape(D_PART // lw, heads, LRU_HEAD, LRU_HEAD)
    rows = [jnp.pad(w4[:, hh], ((0, 0), (0, 0), (LRU_HEAD * hh, lw - LRU_HEAD * (hh + 1)))) for hh in range(heads)]
    return jnp.concatenate(rows, axis=1)


def _gate_matrices(w_a, w_i, lw):
    return jnp.concatenate([_block_diag_strips(w_a, lw), _block_diag_strips(w_i, lw)], axis=2).astype(MXU_DTYPE)


def _strip_diag_blocks(g):
    g5 = g.reshape(NS, HEADS_PER_STRIP, LRU_HEAD, HEADS_PER_STRIP, LRU_HEAD)
    return jnp.stack([g5[:, hh, :, hh, :] for hh in range(HEADS_PER_STRIP)], axis=1).reshape(NS * HEADS_PER_STRIP, LRU_HEAD, LRU_HEAD)


def kernel(x, ln_g, w_in, conv_w, lru_conv_w, lru_conv_b, w_a, b_a, w_i, b_i, lam, conv_out_g, lru_out_g, w_out, final_g, loss_target, m_ln_g, m_w_in, m_conv_w, m_lru_conv_w, m_lru_conv_b, m_w_a, m_b_a, m_w_i, m_b_i, m_lam, m_conv_out_g, m_lru_out_g, m_w_out, m_final_g, v_ln_g, v_w_in, v_conv_w, v_lru_conv_w, v_lru_conv_b, v_w_a, v_b_a, v_w_i, v_b_i, v_lam, v_conv_out_g, v_lru_out_g, v_w_out, v_final_g):
    xi, yi, ci = lax.axis_index("x"), lax.axis_index("y"), lax.axis_index("c")
    k = 2 * xi + yi
    t = x.shape[1]
    x2 = x.reshape(t, D_MODEL)
    tgt2 = loss_target.reshape(t, D_MODEL)
    row = lambda a: a.reshape(1, -1)

    small = jnp.concatenate([conv_w, lru_conv_w, jnp.zeros((1, conv_w.shape[1]), F32)], axis=0)
    proj, xn, w12, sm4 = _gather_in_projection(x2, row(ln_g), w_in, small)
    convs = jnp.transpose(sm4, (1, 0, 2)).reshape(SUBLANES, D_PART)
    pvec = jnp.concatenate(
        [convs[0:7], row(lru_conv_b), row(b_a), row(b_i), row(lam), row(conv_out_g), row(lru_out_g),
         jnp.zeros((PV_ROWS - N_ACC, D_PART), F32)], axis=0)
    wai = _gate_matrices(w_a, w_i, LW)

    c_arr = jnp.reshape(ci, (1,)).astype(jnp.int32)
    yc, yl, h, u, r, ig, wo4 = _mixer_forward(proj, pvec, _gate_matrices(w_a, w_i, FWD_LW), w_out)
    wo = wo4.reshape(2 * D_PART, D_MODEL)
    do, dob, dy, st_out = _out_projection_loss(yc, yl, x2, tgt2, wo, row(final_g))
    go4, go4b = _w_out_grad(yc, yl, dob)
    s_out, sb_out = _add_sibling_halves(go4, go4b, c_arr, "add_sibling_halves_out")
    dproj, g_wai, svec, r2o = _mixer_backward(proj, h, u, r, ig, dy, pvec, wai, sb_out)
    gwa = _strip_diag_blocks(g_wai[:, :, 0:LW]).reshape(LRU_HEAD, D_PART)
    gwi = _strip_diag_blocks(g_wai[:, :, LW:2 * LW]).reshape(LRU_HEAD, D_PART)
    s_in, sb_in, red = _w_in_grad(xn, dproj, jnp.concatenate([svec, st_out, gwa, gwi], axis=0))
    grad_x, st_in, r2i = _input_grad(dproj, w12, x2, do, row(ln_g), sb_in)
    f_in, f_out, red_ln = _finish_gradients(s_in, r2i, s_out, r2o, st_in)
    r_out = PV_ROWS
    r_wa = PV_ROWS + SUBLANES
    r_wi = r_wa + LRU_HEAD
    loss = red[r_out + 1, 0]

    g_w_in, d_w_in, nm_w_in, nv_w_in = _adam_w_in(w_in, m_w_in, v_w_in, f_in)
    g_w_out = f_out[0]
    d_w_out, nm_w_out, nv_w_out = _adam_w_out(w_out, m_w_out, v_w_out, g_w_out)

    ncol = conv_w.shape[1]
    conv_cols = lax.dynamic_slice(red, (0, k * ncol), (SUBLANES, ncol))
    g_small = {
        "ln_g": red_ln[0], "conv_w": conv_cols[0:3], "lru_conv_w": conv_cols[3:7], "lru_conv_b": red[PV_LRU_B],
        "w_a": red[r_wa:r_wa + LRU_HEAD].reshape(w_a.shape), "b_a": red[PV_BA],
        "w_i": red[r_wi:r_wi + LRU_HEAD].reshape(w_i.shape), "b_i": red[PV_BI], "lam": red[PV_LAM],
        "conv_out_g": red[PV_CG], "lru_out_g": red[PV_LG], "final_g": red[r_out],
    }
    w_small = {"ln_g": ln_g, "conv_w": conv_w, "lru_conv_w": lru_conv_w, "lru_conv_b": lru_conv_b, "w_a": w_a, "b_a": b_a,
               "w_i": w_i, "b_i": b_i, "lam": lam, "conv_out_g": conv_out_g, "lru_out_g": lru_out_g, "final_g": final_g}
    m_small = {"ln_g": m_ln_g, "conv_w": m_conv_w, "lru_conv_w": m_lru_conv_w, "lru_conv_b": m_lru_conv_b, "w_a": m_w_a,
               "b_a": m_b_a, "w_i": m_w_i, "b_i": m_b_i, "lam": m_lam, "conv_out_g": m_conv_out_g,
               "lru_out_g": m_lru_out_g, "final_g": m_final_g}
    v_small = {"ln_g": v_ln_g, "conv_w": v_conv_w, "lru_conv_w": v_lru_conv_w, "lru_conv_b": v_lru_conv_b, "w_a": v_w_a,
               "b_a": v_b_a, "w_i": v_w_i, "b_i": v_b_i, "lam": v_lam, "conv_out_g": v_conv_out_g,
               "lru_out_g": v_lru_out_g, "final_g": v_final_g}
    names = list(w_small)
    as2d = lambda a: a.reshape(1, -1) if a.ndim == 1 else a
    d_s, m_s, v_s = _adam_small([as2d(w_small[n]) for n in names], [as2d(m_small[n]) for n in names],
                                [as2d(v_small[n]) for n in names], [as2d(g_small[n]) for n in names])
    back = lambda n, a: a.reshape(w_small[n].shape)
    grads = {n: g_small[n] for n in names}
    deltas = {n: back(n, a) for n, a in zip(names, d_s)}
    new_m = {n: back(n, a) for n, a in zip(names, m_s)}
    new_v = {n: back(n, a) for n, a in zip(names, v_s)}
    grads["w_in"], deltas["w_in"], new_m["w_in"], new_v["w_in"] = g_w_in, d_w_in, nm_w_in, nv_w_in
    grads["w_out"], deltas["w_out"], new_m["w_out"], new_v["w_out"] = g_w_out, d_w_out, nm_w_out, nv_w_out

    order = ["ln_g", "w_in", "conv_w", "lru_conv_w", "lru_conv_b", "w_a", "b_a", "w_i", "b_i", "lam", "conv_out_g",
             "lru_out_g", "w_out", "final_g"]
    return (loss, grad_x.reshape(x.shape), *[grads[n] for n in order], *[deltas[n] for n in order],
            *[new_m[n] for n in order], *[new_v[n] for n in order])
```

```python
import functools

import jax
import jax.numpy as jnp
from jax import lax
from jax.experimental import pallas as pl
from jax.experimental.pallas import tpu as pltpu

F32 = jnp.float32
MXU_DTYPE = jnp.bfloat16

D_MODEL = 1024
D_PART = 1024
N_PARTS = 6
CHUNK = 512
CHUNKS_PER_BLOCK = 3
N_CHUNKS = 12
N_CHIPS = 4
SUBLANES = 8
LANES = 128
LW = 256
FWD_LW = 512
UNROLL = 8
NS = D_PART // LW
STRIPS_PER_CHUNK = CHUNK // LW
CONV_HEAD = 128
LRU_HEAD = 64
HEADS_PER_STRIP = LW // LRU_HEAD
RMS_EPS = 1e-6
RG_LRU_C = 8.0
ADAM_LR = 0.001
ADAM_B1 = 0.9
ADAM_B2 = 0.999
ADAM_EPS = 1e-08
ADAM_WD = 0.01
ADAM_STEP = 10

PV_CONV_W = 0
PV_LRU_W = 3
PV_LRU_B = 7
PV_BA = 8
PV_BI = 9
PV_LAM = 10
PV_CG = 11
PV_LG = 12
PV_ROWS = 16
N_ACC = 13

SLAB = 128
MESH = pl.DeviceIdType.MESH
VMEM_LIMIT = 56 * 1024 * 1024
ARB = "arbitrary"


def _cp(*sem, **kw):
    return pltpu.CompilerParams(dimension_semantics=sem or None, vmem_limit_bytes=VMEM_LIMIT, **kw)


def _mm(a, b):
    return jnp.dot(a, b, preferred_element_type=F32)


def _mm_nt(a, b):
    return lax.dot_general(a, b, (((1,), (1,)), ((), ())), preferred_element_type=F32)


def _mm_tn(a, b):
    return lax.dot_general(a, b, (((0,), (0,)), ((), ())), preferred_element_type=F32)


def _sigmoid(x):
    return 0.5 * jnp.tanh(0.5 * x) + 0.5


def _log_sigmoid(x):
    z = jnp.exp(-jnp.abs(x))
    u = 1.0 + z
    log1p = jnp.where(u == 1.0, z, jnp.log(u) * z / (u - 1.0))
    return jnp.minimum(x, 0.0) - log1p


def _head_mean(z, head):
    out = []
    for k in range(z.shape[1] // LANES):
        zk = z[:, LANES * k:LANES * (k + 1)]
        if head == LANES:
            m = jnp.sum(zk, axis=-1, keepdims=True) * (1.0 / head)
            out.append(jnp.broadcast_to(m, zk.shape))
        else:
            lo = lax.broadcasted_iota(jnp.int32, zk.shape, 1) < head
            s_lo = jnp.sum(jnp.where(lo, zk, 0.0), axis=-1, keepdims=True)
            s_hi = jnp.sum(jnp.where(lo, 0.0, zk), axis=-1, keepdims=True)
            out.append(jnp.where(lo, s_lo, s_hi) * (1.0 / head))
    return jnp.concatenate(out, axis=1)


def _shift_down(cur, prev, d, row):
    return pltpu.roll(jnp.where(row < SUBLANES - d, cur, prev), d, 0)


def _shift_up(cur, nxt, d, row):
    return pltpu.roll(jnp.where(row >= d, cur, nxt), SUBLANES - d, 0)


def _scan8_fwd(a, b, row):
    A, B = a, b
    for d in (1, 2, 4):
        m = row >= d
        a_s = jnp.where(m, pltpu.roll(A, d, 0), 1.0)
        b_s = jnp.where(m, pltpu.roll(B, d, 0), 0.0)
        B = A * b_s + B
        A = A * a_s
    return A, B


def _scan8_rev(a, b, row):
    A, B = a, b
    for d in (1, 2, 4):
        m = row < SUBLANES - d
        a_s = jnp.where(m, pltpu.roll(A, SUBLANES - d, 0), 1.0)
        b_s = jnp.where(m, pltpu.roll(B, SUBLANES - d, 0), 0.0)
        B = A * b_s + B
        A = A * a_s
    return A, B


def _decay(r, ls8):
    la = r * ls8
    a = jnp.exp(la)
    e2 = a * a
    em = -jnp.tanh(la) * (1.0 + e2)
    inv_mult = lax.rsqrt(em)
    return a, e2, em * inv_mult, inv_mult


def _mesh_pos():
    x, y, c = lax.axis_index("x"), lax.axis_index("y"), lax.axis_index("c")
    chips = [(1 - x, y), (x, 1 - y), (1 - x, 1 - y)]
    return x, y, c, chips


def _gather_in_projection(x, ln_g, w_in, small):
    t = x.shape[0]
    rb_x = 512
    rb_mm = 2048
    n_mm = t // rb_mm
    half = w_in.shape[0] // 2

    def body(x_hbm, g_ref, wi_ref, sm_ref, proj_hbm, xn_ref, w12_ref, sm4_ref,
             xbuf, obuf, x_sems, o_sems, send_sems, recv_sems):
        x_, y_, c, chips = _mesh_pos()
        k = 2 * x_ + y_
        sib = (x_, y_, 1 - c)
        sm4_ref[k] = sm_ref[...]

        def remote(ref, sem, to):
            return pltpu.make_async_remote_copy(src_ref=ref, dst_ref=ref, send_sem=send_sems.at[sem],
                                                recv_sem=recv_sems.at[sem], device_id=to, device_id_type=MESH)

        def chunk_of(chip, s):
            return CHUNKS_PER_BLOCK * (2 * chip[0] + chip[1]) + s

        def piece(q, core, first=0, rows=half):
            return w12_ref.at[q, pl.ds(pl.multiple_of(half * core + first, SUBLANES * 2), rows), :]

        nbr_x, nbr_y, diagonal = chips
        quarter = half // 2
        DIAG = [(0, 0, half, 0), (1, 0, quarter, 0), (1, quarter, quarter, 1), (2, 0, half, 1)]
        ici = lambda m, s: 2 * s + m
        dgn = lambda j: 6 + j
        to_sib = 10
        sml = lambda m: 20 + m

        sends = []
        for s in range(CHUNKS_PER_BLOCK):
            w12_ref[chunk_of((x_, y_), s)] = wi_ref[:, CHUNK * s:CHUNK * (s + 1)].astype(MXU_DTYPE)
            for m, chip in enumerate((nbr_x, nbr_y)):
                sends.append(remote(piece(chunk_of((x_, y_), s), c), ici(m, s), (*chip, c)))
                sends[-1].start()
        for m, chip in enumerate(chips):
            sends.append(remote(sm4_ref.at[k], sml(m), (*chip, c)))
            sends[-1].start()

        def x_copy(rb, slot):
            return pltpu.make_async_copy(x_hbm.at[pl.ds(rb * rb_x, rb_x), :], xbuf.at[slot], x_sems.at[slot])

        x_copy(0, 0).start()
        for rb in range(t // rb_x):
            slot = rb % 2
            x_copy(rb, slot).wait()
            if rb + 1 < t // rb_x:
                x_copy(rb + 1, 1 - slot).start()

            def norm_slab(sl, carry, rb=rb, slot=slot):
                xf = xbuf[slot, pl.ds(pl.multiple_of(sl * SLAB, SLAB), SLAB), :]
                r = lax.rsqrt(jnp.mean(xf * xf, axis=-1, keepdims=True) + RMS_EPS)
                xn_ref[pl.ds(pl.multiple_of(rb * rb_x + sl * SLAB, SLAB), SLAB), :] = ((xf * r) * g_ref[...]).astype(MXU_DTYPE)
                return carry

            lax.fori_loop(0, rb_x // SLAB, norm_slab, 0)

        def out_copy(q, i):
            return pltpu.make_async_copy(obuf.at[i], proj_hbm.at[q, pl.ds(pl.multiple_of(i * rb_mm, rb_mm), rb_mm), :],
                                         o_sems.at[i])

        def project(q, very_first):
            def row_block(i, carry):
                if not very_first:
                    out_copy(q, i).wait()
                obuf[i] = _mm(xn_ref[pl.ds(pl.multiple_of(i * rb_mm, rb_mm), rb_mm), :], w12_ref[q])
                out_copy(q, i).start()
                return carry

            lax.fori_loop(0, n_mm, row_block, 0)

        for s in range(CHUNKS_PER_BLOCK):
            project(chunk_of((x_, y_), s), very_first=(s == 0))

        steps = []
        for s in range(CHUNKS_PER_BLOCK):
            for m, chip in enumerate((nbr_x, nbr_y)):
                onward = [(first, rows, dgn(j), chips[via]) for j, (cs, first, rows, via) in enumerate(DIAG)
                          if cs == s and via == 1 - m]
                steps.append((chunk_of(chip, s), [(0, half, ici(m, s))], onward))
        for s in range(CHUNKS_PER_BLOCK):
            steps.append((chunk_of(diagonal, s), [(first, rows, dgn(j)) for j, (cs, first, rows, _) in enumerate(DIAG) if cs == s], []))

        def project_when_whole(step):
            q, pieces, _ = step
            for first, rows, sem in pieces:
                remote(piece(q, 1 - c, first, rows), to_sib + sem, sib).wait_recv()
            project(q, very_first=False)

        passed = []
        for j, (q, pieces, onward) in enumerate(steps):
            for first, rows, sem in pieces:
                remote(piece(q, c, first, rows), sem, sib).wait_recv()
            for first, rows, sem, chip in onward:
                passed.append(remote(piece(q, c, first, rows), sem, (*chip, c)))
                passed[-1].start()
            for first, rows, sem in pieces:
                passed.append(remote(piece(q, c, first, rows), to_sib + sem, sib))
                passed[-1].start()
            if j > 0:
                project_when_whole(steps[j - 1])
        project_when_whole(steps[-1])

        for m, chip in enumerate(chips):
            remote(sm4_ref.at[2 * chip[0] + chip[1]], sml(m), sib).wait_recv()
        for cp in sends + passed:
            cp.wait_send()
        for i in range(n_mm):
            out_copy(0, i).wait()

    vm = pl.BlockSpec(memory_space=pltpu.VMEM)
    hbm = pl.BlockSpec(memory_space=pl.ANY)
    n_sems = 23
    return pl.pallas_call(
        body,
        out_shape=(jax.ShapeDtypeStruct((N_CHUNKS, t, CHUNK), F32), jax.ShapeDtypeStruct((t, D_MODEL), MXU_DTYPE),
                   jax.ShapeDtypeStruct((N_CHUNKS, w_in.shape[0], CHUNK), MXU_DTYPE),
                   jax.ShapeDtypeStruct((N_CHIPS,) + small.shape, F32)),
        in_specs=[hbm, vm, vm, vm], out_specs=(hbm, vm, vm, vm),
        scratch_shapes=[pltpu.VMEM((2, rb_x, D_MODEL), F32), pltpu.VMEM((n_mm, rb_mm, CHUNK), F32),
                        pltpu.SemaphoreType.DMA((2,)), pltpu.SemaphoreType.DMA((n_mm,)),
                        pltpu.SemaphoreType.DMA((n_sems,)), pltpu.SemaphoreType.DMA((n_sems,))],
        compiler_params=_cp(), name="gather_in_projection",
    )(x, ln_g, w_in, small)


def _allreduce_behind(step, when, in_ref, acc_s, rbufs, out_ref, send_sems, recv_sems):
    x, y, c, _ = _mesh_pos()
    peers = [(x, y, 1 - c), (1 - x, y, c), (x, 1 - y, c)]

    def exchange(ph):
        return pltpu.make_async_remote_copy(src_ref=acc_s, dst_ref=rbufs[ph], send_sem=send_sems.at[ph],
                                            recv_sem=recv_sems.at[ph], device_id=peers[ph], device_id_type=MESH)

    @pl.when(step == when[0])
    def _():
        acc_s[...] = in_ref[...]
        exchange(0).start()

    for ph in (1, 2):
        @pl.when(step == when[ph])
        def _(ph=ph):
            exchange(ph - 1).wait()
            acc_s[...] = acc_s[...] + rbufs[ph - 1][...]
            exchange(ph).start()

    @pl.when(step == when[3])
    def _():
        exchange(2).wait()
        out_ref[...] = acc_s[...] + rbufs[2][...]


def _add_sibling_halves(g, gb, c_arr, name):
    n, rows, cols = g.shape
    half = rows // 2
    per = 2
    steps = n // per

    def body(c_ref, g_ref, gb_hbm, o_ref, ob_ref, rbuf, send_sems, recv_sems):
        q = pl.program_id(0)
        x, y, c, _ = _mesh_pos()
        theirs = pl.ds(pl.multiple_of(half * (1 - c), half), half)

        def copy(j):
            blocks = pl.ds(j * per, per)
            return pltpu.make_async_remote_copy(src_ref=gb_hbm.at[blocks, theirs, :], dst_ref=rbuf.at[blocks], send_sem=send_sems.at[j],
                                                recv_sem=recv_sems.at[j], device_id=(x, y, 1 - c), device_id_type=MESH)

        @pl.when(q == 0)
        def _():
            for j in range(steps):
                copy(j).start()

        copy(q).wait_recv()
        s = g_ref[...] + rbuf[pl.ds(q * per, per)].astype(F32)
        o_ref[...] = s
        ob_ref[...] = s.astype(jnp.bfloat16)

        @pl.when(q == steps - 1)
        def _():
            for j in range(steps):
                copy(j).wait_send()

    blk = pl.BlockSpec((per, half, cols), lambda q, c_ref: (q, 0, 0))
    return pl.pallas_call(
        body, out_shape=(jax.ShapeDtypeStruct((n, half, cols), F32), jax.ShapeDtypeStruct((n, half, cols), jnp.bfloat16)),
        grid_spec=pltpu.PrefetchScalarGridSpec(
            num_scalar_prefetch=1, grid=(steps,),
            in_specs=[pl.BlockSpec((per, half, cols), lambda q, c_ref: (q, c_ref[0], 0)), pl.BlockSpec(memory_space=pl.ANY)],
            out_specs=(blk, blk),
            scratch_shapes=[pltpu.VMEM((n, half, cols), jnp.bfloat16), pltpu.SemaphoreType.DMA((steps,)),
                            pltpu.SemaphoreType.DMA((steps,))]),
        compiler_params=_cp(ARB), name=name,
    )(c_arr, g, gb)


def _chip_block_copies(s_ref, r_ref, n_sub, send_sems, recv_sems):
    x, y, c, chips = _mesh_pos()
    cps = []
    for m, chip in enumerate(chips):
        kk = 2 * chip[0] + chip[1]
        cps.append(pltpu.make_async_remote_copy(
            src_ref=s_ref.at[pl.ds(n_sub * kk, n_sub)], dst_ref=r_ref.at[m],
            send_sem=send_sems.at[m], recv_sem=recv_sems.at[m], device_id=(*chip, c), device_id_type=MESH))
    return cps


def _gather_w_out(step, n_steps, wo_ref, wob_s, wo4_ref, local_sem, send_sems, recv_sems):
    x, y, c, chips = _mesh_pos()
    sib = (x, y, 1 - c)
    half = wo_ref.shape[0] // 2

    def rows(core):
        return pl.ds(pl.multiple_of(half * core, half), half)

    def block_half(chip, core):
        return wo4_ref.at[2 * chip[0] + chip[1], rows(core), :]

    def remote(src, dst, sem, to):
        return pltpu.make_async_remote_copy(src_ref=src, dst_ref=dst, send_sem=send_sems.at[sem], recv_sem=recv_sems.at[sem],
                                            device_id=to, device_id_type=MESH)

    local = pltpu.make_async_copy(wob_s, wo4_ref.at[2 * x + y], local_sem)
    ici = [remote(wob_s.at[rows(c), :], block_half((x, y), c), m, (*chip, c)) for m, chip in enumerate(chips)]
    fwd = [remote(block_half(chip, c), block_half(chip, c), 3 + m, sib) for m, chip in enumerate(chips)]

    @pl.when(step == 0)
    def _():
        wob_s[...] = wo_ref[...].astype(MXU_DTYPE)
        local.start()
        for cp in ici:
            cp.start()

    @pl.when(step == n_steps // 2)
    def _():
        for m, chip in enumerate(chips):
            remote(block_half(chip, c), block_half(chip, c), m, sib).wait_recv()
            fwd[m].start()

    @pl.when(step == n_steps - 1)
    def _():
        for m, chip in enumerate(chips):
            remote(block_half(chip, 1 - c), block_half(chip, 1 - c), 3 + m, sib).wait_recv()
        for cp in ici + fwd:
            cp.wait_send()
        local.wait()


def _chip_blocks_shape(s, n_sub):
    return jax.ShapeDtypeStruct((3, n_sub) + s.shape[1:], s.dtype)


def _finish_gradients(s_in, r_in, s_out, r_out, v):
    n_dev = 8
    n_in, n_out = r_in.shape[1], r_out.shape[1]

    def body(si_hbm, ri_hbm, so_hbm, ro_hbm, v_ref, fi_hbm, fo_hbm, tot_ref,
             a_in, b_in, a_out, b_out, slots, load_sems, store_sems, send_sems, recv_sems):
        x, y, c, _ = _mesh_pos()
        k = 2 * x + y
        sib = (x, y, 1 - c)
        me = 4 * x + 2 * y + c
        loads = [pltpu.make_async_copy(si_hbm.at[pl.ds(n_in * k, n_in)], a_in, load_sems.at[0]),
                 pltpu.make_async_copy(ri_hbm, b_in, load_sems.at[1]),
                 pltpu.make_async_copy(so_hbm.at[pl.ds(n_out * k, n_out)], a_out, load_sems.at[2]),
                 pltpu.make_async_copy(ro_hbm, b_out, load_sems.at[3])]
        for cp in loads:
            cp.start()
        slots[me] = v_ref[...]

        def remote(src, dst, sem, to):
            return pltpu.make_async_remote_copy(src_ref=src, dst_ref=dst, send_sem=send_sems.at[sem],
                                                recv_sem=recv_sems.at[sem], device_id=to, device_id_type=MESH)

        small = []
        for d in range(1, n_dev):
            peer = (1 - x if d & 4 else x, 1 - y if d & 2 else y, 1 - c if d & 1 else c)
            small.append(remote(slots.at[me], slots.at[me], d - 1, peer))
            small[-1].start()
        for cp in loads:
            cp.wait()
        big = []
        for j, (a, b, f_hbm) in enumerate(((a_in, b_in, fi_hbm), (a_out, b_out, fo_hbm))):
            a[...] = ((a[...] + b[0].astype(F32)) + b[1].astype(F32)) + b[2].astype(F32)
            half = a.shape[1]
            mine = f_hbm.at[:, pl.ds(pl.multiple_of(half * c, half), half), :]
            big.append(pltpu.make_async_copy(a, mine, store_sems.at[j]))
            big.append(remote(a, mine, n_dev - 1 + j, sib))
        for cp in big:
            cp.start()
        for cp in small + big:
            cp.wait()
        total = slots[0]
        for dev in range(1, n_dev):
            total = total + slots[dev]
        tot_ref[...] = total

    hbm = pl.BlockSpec(memory_space=pl.ANY)
    vm = pl.BlockSpec(memory_space=pltpu.VMEM)
    full = lambda s, n: (n, 2 * s.shape[1], s.shape[2])
    return pl.pallas_call(
        body,
        out_shape=(jax.ShapeDtypeStruct(full(s_in, n_in), F32), jax.ShapeDtypeStruct(full(s_out, n_out), F32),
                   jax.ShapeDtypeStruct(v.shape, F32)),
        in_specs=[hbm, hbm, hbm, hbm, vm], out_specs=(hbm, hbm, vm),
        scratch_shapes=[pltpu.VMEM((n_in,) + s_in.shape[1:], F32), pltpu.VMEM(r_in.shape, r_in.dtype),
                        pltpu.VMEM((n_out,) + s_out.shape[1:], F32), pltpu.VMEM(r_out.shape, r_out.dtype),
                        pltpu.VMEM((n_dev,) + v.shape, F32), pltpu.SemaphoreType.DMA((4,)), pltpu.SemaphoreType.DMA((2,)),
                        pltpu.SemaphoreType.DMA((n_dev + 1,)), pltpu.SemaphoreType.DMA((n_dev + 1,))],
        compiler_params=_cp(), name="finish_gradients",
    )(s_in, r_in, s_out, r_out, v)


def _out_projection_loss(yc, yl, x, target, wo, final_g):
    t = x.shape[0]
    tm = 512

    def body(yc_ref, yl_ref, x_ref, t_ref, wo_ref, fg_ref, do_ref, dob_ref, dy_ref, st_ref, y_wo):
        @pl.when(pl.program_id(0) == 0)
        def _():
            st_ref[...] = jnp.zeros_like(st_ref)

        y_wo[...] = _mm(yc_ref[...], wo_ref[0:D_PART, :]) + _mm(yl_ref[...], wo_ref[D_PART:2 * D_PART, :])

        def norm_loss_slab(s, carry):
            g_sum, loss_sum = carry
            rows = pl.ds(pl.multiple_of(s * SLAB, SLAB), SLAB)
            o = x_ref[rows, :] + y_wo[rows, :]
            r2 = lax.rsqrt(jnp.mean(o * o, axis=-1, keepdims=True) + RMS_EPS)
            ohat = o * r2
            fg = fg_ref[...]
            diff = ohat * fg - t_ref[rows, :]
            dout = diff * (1.0 / D_MODEL)
            gp = dout * fg
            do = r2 * (gp - ohat * jnp.mean(gp * ohat, axis=-1, keepdims=True))
            do_ref[rows, :] = do
            dob_ref[rows, :] = do.astype(MXU_DTYPE)
            loss = 0.5 * jnp.sum(jnp.sum(diff * diff, axis=-1, keepdims=True) * (1.0 / D_MODEL), axis=0, keepdims=True)
            return g_sum + jnp.sum(dout * ohat, axis=0, keepdims=True), loss_sum + loss

        g_sum, loss_sum = lax.fori_loop(0, tm // SLAB, norm_loss_slab,
                                        (jnp.zeros((1, D_MODEL), F32), jnp.zeros((1, 1), F32)))
        st_ref[0:1, :] += g_sum
        st_ref[1:2, :] += jnp.broadcast_to(loss_sum, (1, D_MODEL))
        dy_ref[...] = _mm_nt(dob_ref[...], wo_ref[...])

    row = lambda i: (i, 0)
    fix = lambda i: (0, 0)
    return pl.pallas_call(
        body, grid=(t // tm,),
        in_specs=[pl.BlockSpec((tm, D_PART), row), pl.BlockSpec((tm, D_PART), row),
                  pl.BlockSpec((tm, D_MODEL), row), pl.BlockSpec((tm, D_MODEL), row),
                  pl.BlockSpec((2 * D_PART, D_MODEL), fix), pl.BlockSpec((1, D_MODEL), fix)],
        out_specs=(pl.BlockSpec((tm, D_MODEL), row), pl.BlockSpec((tm, D_MODEL), row),
                   pl.BlockSpec((tm, 2 * D_PART), row), pl.BlockSpec((SUBLANES, D_MODEL), fix)),
        out_shape=(jax.ShapeDtypeStruct((t, D_MODEL), F32), jax.ShapeDtypeStruct((t, D_MODEL), MXU_DTYPE),
                   jax.ShapeDtypeStruct((t, 2 * D_PART), F32), jax.ShapeDtypeStruct((SUBLANES, D_MODEL), F32)),
        scratch_shapes=[pltpu.VMEM((tm, D_MODEL), F32)],
        compiler_params=_cp(ARB), name="out_projection_loss",
    )(yc, yl, x, target, wo, final_g)


def _input_grad(dproj, w12, x, do, ln_g, sb_in):
    t = x.shape[0]
    tm = 1024

    def body(dp_ref, w_ref, x_ref, do_ref, g_ref, s_ref, gx_ref, st_ref, r_ref, acc, send_sems, recv_sems):
        i, p = pl.program_id(0), pl.program_id(1)

        @pl.when((i == 0) & (p == 0))
        def _():
            st_ref[...] = jnp.zeros_like(st_ref)
            for cp in _chip_block_copies(s_ref, r_ref, CHUNKS_PER_BLOCK, send_sems, recv_sems):
                cp.start()

        @pl.when((i == t // tm - 1) & (p == N_PARTS - 1))
        def _():
            for cp in _chip_block_copies(s_ref, r_ref, CHUNKS_PER_BLOCK, send_sems, recv_sems):
                cp.wait()

        @pl.when(p == 0)
        def _():
            acc[...] = jnp.zeros_like(acc)

        acc[...] += _mm_nt(dp_ref[0], jnp.concatenate([w_ref[0], w_ref[1]], axis=1))

        @pl.when(p == N_PARTS - 1)
        def _():
            def norm_bwd_slab(s, g_sum):
                rows = pl.ds(pl.multiple_of(s * SLAB, SLAB), SLAB)
                xf = x_ref[rows, :]
                r = lax.rsqrt(jnp.mean(xf * xf, axis=-1, keepdims=True) + RMS_EPS)
                xhat = xf * r
                dxn = acc[rows, :]
                dxh = dxn * g_ref[...]
                gx_ref[rows, :] = do_ref[rows, :] + r * (dxh - xhat * jnp.mean(dxh * xhat, axis=-1, keepdims=True))
                return g_sum + jnp.sum(dxn * xhat, axis=0, keepdims=True)

            st_ref[0:1, :] += lax.fori_loop(0, tm // SLAB, norm_bwd_slab, jnp.zeros((1, D_MODEL), F32))

    row = lambda i, p: (i, 0)
    fix = lambda i, p: (0, 0)
    return pl.pallas_call(
        body, grid=(t // tm, N_PARTS),
        in_specs=[
            pl.BlockSpec((1, tm, D_PART), lambda i, p: (p, i, 0)),
            pl.BlockSpec((2, D_MODEL, CHUNK), lambda i, p: (p, 0, 0)),
            pl.BlockSpec((tm, D_MODEL), row), pl.BlockSpec((tm, D_MODEL), row), pl.BlockSpec((1, D_MODEL), fix),
            pl.BlockSpec(memory_space=pl.ANY)],
        out_specs=(pl.BlockSpec((tm, D_MODEL), row), pl.BlockSpec((SUBLANES, D_MODEL), fix),
                   pl.BlockSpec(memory_space=pl.ANY)),
        out_shape=(jax.ShapeDtypeStruct((t, D_MODEL), F32), jax.ShapeDtypeStruct((SUBLANES, D_MODEL), F32),
                   _chip_blocks_shape(sb_in, CHUNKS_PER_BLOCK)),
        scratch_shapes=[pltpu.VMEM((tm, D_MODEL), F32), pltpu.SemaphoreType.DMA((3,)), pltpu.SemaphoreType.DMA((3,))],
        compiler_params=_cp(ARB, ARB), name="input_grad",
    )(dproj, w12, x, do, ln_g, sb_in)


def _w_in_grad(xn, dproj, small):
    t = xn.shape[0]
    small_shape = pltpu.VMEM(small.shape, F32)

    def body(xn_ref, dp_ref, sm_ref, o_ref, ob_ref, red_ref, acc_s, r0, r1, r2, send_sems, recv_sems):
        _allreduce_behind(pl.program_id(0), (0, 1, 3, N_PARTS - 1), sm_ref, acc_s, (r0, r1, r2), red_ref, send_sems, recv_sems)
        g = _mm_tn(xn_ref[...], dp_ref[0])
        for s in range(2):
            o_ref[s] = g[:, CHUNK * s:CHUNK * (s + 1)]
            ob_ref[s] = g[:, CHUNK * s:CHUNK * (s + 1)].astype(jnp.bfloat16)

    whole = pl.BlockSpec(small.shape, lambda p: (0, 0))
    pair = pl.BlockSpec((2, D_MODEL, CHUNK), lambda p: (p, 0, 0))
    return pl.pallas_call(
        body, grid=(N_PARTS,),
        in_specs=[pl.BlockSpec((t, D_MODEL), lambda p: (0, 0)),
                  pl.BlockSpec((1, t, D_PART), lambda p: (p, 0, 0)), whole],
        out_specs=(pair, pair, whole),
        out_shape=(jax.ShapeDtypeStruct((N_CHUNKS, D_MODEL, CHUNK), F32),
                   jax.ShapeDtypeStruct((N_CHUNKS, D_MODEL, CHUNK), jnp.bfloat16), jax.ShapeDtypeStruct(small.shape, F32)),
        scratch_shapes=[small_shape] * 4 + [pltpu.SemaphoreType.DMA((3,)), pltpu.SemaphoreType.DMA((3,))],
        compiler_params=_cp(ARB), name="w_in_grad",
    )(xn, dproj, small)


def _w_out_grad(yc, yl, dob):
    t = yc.shape[0]
    tk = 2048

    def body(yc_ref, yl_ref, do_ref, o_ref, ob_ref):
        j, kk = pl.program_id(0), pl.program_id(1)

        def accumulate(y_ref):
            @pl.when(kk == 0)
            def _():
                o_ref[...] = jnp.zeros_like(o_ref)

            o_ref[...] += _mm_tn(y_ref[...], do_ref[...])

            @pl.when(kk == t // tk - 1)
            def _():
                ob_ref[...] = o_ref[...].astype(jnp.bfloat16)

        pl.when(j == 0)(functools.partial(accumulate, yc_ref))
        pl.when(j == 1)(functools.partial(accumulate, yl_ref))

    def rows_of(half):
        return lambda j, kk: (jnp.where(j == half, kk, 0), 0)

    half = pl.BlockSpec((D_PART, D_MODEL), lambda j, kk: (j, 0))
    out, out_b = pl.pallas_call(
        body, grid=(2, t // tk),
        in_specs=[pl.BlockSpec((tk, D_PART), rows_of(0)), pl.BlockSpec((tk, D_PART), rows_of(1)),
                  pl.BlockSpec((tk, D_MODEL), lambda j, kk: (kk, 0))],
        out_specs=(half, half),
        out_shape=(jax.ShapeDtypeStruct((2 * D_PART, D_MODEL), F32), jax.ShapeDtypeStruct((2 * D_PART, D_MODEL), jnp.bfloat16)),
        compiler_params=_cp(ARB, ARB), name="w_out_grad",
    )(yc, yl, dob)
    blocks = (N_CHIPS, 2 * D_PART // N_CHIPS, D_MODEL)
    return out.reshape(blocks), out_b.reshape(blocks)


def _for_groups(n, fn, init, unroll=UNROLL, stores=(), descending=False):
    assert unroll % 2 == 0 and n % unroll == 0

    def trip(j, carry):
        held = None
        for uu in range(unroll):
            idx = j * unroll + uu
            carry, values = fn(idx, carry)
            if uu % 2 == 0:
                held = values
                continue
            low_group = n - 1 - idx if descending else idx - 1
            rows = pl.ds(pl.multiple_of(low_group * SUBLANES, 2 * SUBLANES), 2 * SUBLANES)
            pairs = zip(values, held) if descending else zip(held, values)
            for store, (lo, hi) in zip(stores, pairs, strict=True):
                store(rows, jnp.concatenate([lo, hi], axis=0).astype(MXU_DTYPE))
        return carry

    return lax.fori_loop(0, n // unroll, trip, init)


def _rows_of(ref, *lead, cols=slice(None)):
    def store(rows, value):
        ref[(*lead, rows, cols)] = value

    return store


def _pvb(pv_ref, r):
    return jnp.broadcast_to(pv_ref[r:r + 1, :], (SUBLANES, pv_ref.shape[1]))


def _conv3(pv_ref, u, u1, u2):
    return (_pvb(pv_ref, PV_CONV_W) * u2 + _pvb(pv_ref, PV_CONV_W + 1) * u1) + _pvb(pv_ref, PV_CONV_W + 2) * u


def _conv4(pv_ref, v, v1, v2, v3):
    return ((((_pvb(pv_ref, PV_LRU_W) * v3 + _pvb(pv_ref, PV_LRU_W + 1) * v2) + _pvb(pv_ref, PV_LRU_W + 2) * v1)
             + _pvb(pv_ref, PV_LRU_W + 3) * v) + _pvb(pv_ref, PV_LRU_B))


def _mixer_forward(proj, pvec, wai, w_out):
    t = proj.shape[1]
    tb = 512
    ng = tb // SUBLANES
    nt = t // tb
    lw = FWD_LW
    ns = D_PART // lw
    per_chunk = CHUNK // lw

    def body(bg_ref, cg_ref, xc_ref, gc_ref, xl_ref, gl_ref, pv_ref, wai_ref, wo_ref,
             yc_ref, yl_ref, h_ref, u_s, r_ref, ig_ref, wo4_ref,
             ucp_s, xlp_s, ls_s, hbuf_s, ub_s, gate_s, wob_s, local_sem, send_sems, recv_sems):
        _gather_w_out(pl.program_id(0) * nt + pl.program_id(1), ns * nt, wo_ref, wob_s, wo4_ref, local_sem, send_sems, recv_sems)

        @pl.when(pl.program_id(1) == 0)
        def _():
            ucp_s[...] = jnp.zeros_like(ucp_s)
            xlp_s[...] = jnp.zeros_like(xlp_s)
            hbuf_s[...] = jnp.zeros_like(hbuf_s)

        row = lax.broadcasted_iota(jnp.int32, (SUBLANES, lw), 0)
        ls_s[...] = RG_LRU_C * _log_sigmoid(_pvb(pv_ref, PV_LAM))

        def conv_group(g, carry):
            ucp, xlp = carry
            sl = pl.ds(pl.multiple_of(g * SUBLANES, SUBLANES), SUBLANES)
            uc = cg_ref[sl, :] * xc_ref[sl, :]
            v = _conv3(pv_ref, uc, _shift_down(uc, ucp, 1, row), _shift_down(uc, ucp, 2, row))
            yc = bg_ref[sl, :] * v
            rr = lax.rsqrt(_head_mean(yc * yc, CONV_HEAD) + RMS_EPS)
            gc = gc_ref[sl, :]
            zc = ((yc * rr) * _pvb(pv_ref, PV_CG)) * (gc * _sigmoid(gc))
            xl = xl_ref[sl, :]
            u = _conv4(pv_ref, xl, _shift_down(xl, xlp, 1, row), _shift_down(xl, xlp, 2, row), _shift_down(xl, xlp, 3, row))
            u_s[sl, :] = u
            return (uc, xl), (zc, u)

        ucp, xlp = _for_groups(ng, conv_group, (ucp_s[...], xlp_s[...]), unroll=2 * UNROLL,
                               stores=(_rows_of(yc_ref), _rows_of(ub_s)))
        ucp_s[...] = ucp
        xlp_s[...] = xlp

        gate_s[...] = _mm(ub_s[...], wai_ref[0])

        def lru_group(g, h_before):
            sl = pl.ds(pl.multiple_of(g * SUBLANES, SUBLANES), SUBLANES)
            u = u_s[sl, :]
            r = _sigmoid(gate_s[sl, 0:lw] + _pvb(pv_ref, PV_BA))
            ig = _sigmoid(gate_s[sl, lw:2 * lw] + _pvb(pv_ref, PV_BI))
            r_ref[sl, :] = r
            ig_ref[sl, :] = ig
            a, _, mult, _ = _decay(r, ls_s[...])
            A, B = _scan8_fwd(a, mult * (ig * u), row)
            h = B + A * jnp.broadcast_to(h_before[SUBLANES - 1:SUBLANES, :], (SUBLANES, lw))
            h_ref[sl, :] = h
            rr = lax.rsqrt(_head_mean(h * h, LRU_HEAD) + RMS_EPS)
            gl = gl_ref[sl, :]
            return h, (((h * rr) * _pvb(pv_ref, PV_LG)) * (gl * _sigmoid(gl)),)

        hbuf_s[...] = _for_groups(ng, lru_group, hbuf_s[...], unroll=2 * UNROLL, stores=(_rows_of(yl_ref),))

    def part(p):
        return pl.BlockSpec((None, tb, lw), lambda c, i: (2 * p + c // per_chunk, i, c % per_chunk))

    strip = pl.BlockSpec((tb, lw), lambda c, i: (i, c))
    small = pltpu.VMEM((SUBLANES, lw), F32)
    return pl.pallas_call(
        body, grid=(ns, nt),
        in_specs=[part(p) for p in range(N_PARTS)] + [
            pl.BlockSpec((PV_ROWS, lw), lambda c, i: (0, c)),
            pl.BlockSpec((1, lw, 2 * lw), lambda c, i: (c, 0, 0)),
            pl.BlockSpec(w_out.shape, lambda c, i: (0, 0))],
        out_specs=(strip,) * 6 + (pl.BlockSpec(memory_space=pl.ANY),),
        out_shape=(jax.ShapeDtypeStruct((t, D_PART), MXU_DTYPE),) * 2 + (jax.ShapeDtypeStruct((t, D_PART), F32),) * 4 + (
            jax.ShapeDtypeStruct((N_CHIPS,) + w_out.shape, MXU_DTYPE),),
        scratch_shapes=[small, small, small, small, pltpu.VMEM((tb, lw), MXU_DTYPE),
                        pltpu.VMEM((tb, 2 * lw), F32), pltpu.VMEM(w_out.shape, MXU_DTYPE),
                        pltpu.SemaphoreType.DMA, pltpu.SemaphoreType.DMA((6,)), pltpu.SemaphoreType.DMA((6,))],
        compiler_params=_cp(ARB, ARB), name="mixer_forward",
    )(proj, proj, proj, proj, proj, proj, pvec, wai, w_out)


def _mixer_backward(proj, h, u, r, ig, dy, pvec, wai, sb_out):
    t = proj.shape[1]
    tb = 1024
    ng = tb // SUBLANES
    nt = t // tb
    gpb = tb // SUBLANES

    def body(bg_ref, cg_ref, xc_ref, gc_ref, xl_ref, gl_ref, h_ref, u_ref, r_ref, ig_ref, dyc_ref, dyl_ref,
             cgh_ref, xch_ref, xlh_ref, hh_ref, pv_ref, wai_ref, so_ref,
             dp_ref, gw_ref, sv_ref, ro_ref,
             ls_s, ub_s, uce_s, xle_s, he_s, dgb_s, du_s, gbuf_s,
             acc_s, an_s, dvn_s, dun_s, send_sems, recv_sems):
        i = pl.program_id(1)
        first_block = i == nt - 1

        @pl.when((pl.program_id(0) == 0) & (i == 0))
        def _():
            for cp in _chip_block_copies(so_ref, ro_ref, 1, send_sems, recv_sems):
                cp.start()

        @pl.when((pl.program_id(0) == NS - 1) & (i == nt - 1))
        def _():
            for cp in _chip_block_copies(so_ref, ro_ref, 1, send_sems, recv_sems):
                cp.wait()

        @pl.when(i == 0)
        def _():
            acc_s[...] = jnp.zeros_like(acc_s)
            gw_ref[...] = jnp.zeros_like(gw_ref)
            an_s[...] = jnp.zeros_like(an_s)
            dvn_s[...] = jnp.zeros_like(dvn_s)
            dun_s[...] = jnp.zeros_like(dun_s)
            gbuf_s[...] = jnp.zeros_like(gbuf_s)

        row = lax.broadcasted_iota(jnp.int32, (SUBLANES, LW), 0)
        ls_s[...] = RG_LRU_C * _log_sigmoid(_pvb(pv_ref, PV_LAM))
        keep = jnp.where(first_block, 0.0, 1.0)
        uce_s[0:SUBLANES, :] = (cgh_ref[...] * xch_ref[...]) * keep
        xle_s[0:SUBLANES, :] = xlh_ref[...] * keep
        he_s[0:SUBLANES, :] = hh_ref[...] * keep
        xle_s[SUBLANES:SUBLANES + tb, :] = xl_ref[...]
        he_s[SUBLANES:SUBLANES + tb, :] = h_ref[...]

        uce_s[SUBLANES:SUBLANES + tb, :] = cg_ref[...] * xc_ref[...]

        def acc_add(k, v):
            acc_s[k] += v

        def main_group(gi, carry):
            a_next, dv_next, g_next = carry
            g = ng - 1 - gi
            r0 = pl.multiple_of(g * SUBLANES, SUBLANES)
            sl = pl.ds(r0, SUBLANES)
            sl_e = pl.ds(r0 + SUBLANES, SUBLANES)
            lsb = ls_s[...]
            u = u_ref[sl, :]
            r = r_ref[sl, :]
            ig = ig_ref[sl, :]
            a, e2, mult, inv_mult = _decay(r, lsb)
            gl = gl_ref[sl, :]
            sg = _sigmoid(gl)
            s_l = gl * sg
            h8 = he_s[sl_e, :]
            hprev = _shift_down(h8, he_s[sl, :], 1, row)
            rr = lax.rsqrt(_head_mean(h8 * h8, LRU_HEAD) + RMS_EPS)
            n = h8 * rr
            dz = dyl_ref[sl, :]
            lg = _pvb(pv_ref, PV_LG)
            acc_add(PV_LG, (dz * n) * s_l)
            p5 = ((dz * n) * lg) * (sg + s_l * (1.0 - sg))
            dn = (dz * lg) * s_l
            dh = rr * (dn - n * _head_mean(dn * n, LRU_HEAD))
            A, B = _scan8_rev(_shift_up(a, a_next, 1, row), dh, row)
            gg = B + A * jnp.broadcast_to(g_next[0:1, :], (SUBLANES, LW))
            da = gg * hprev
            iu = ig * u
            diu = gg * mult
            dla = da * a - (gg * iu) * (e2 * inv_mult)
            acc_add(PV_LAM, dla * r)
            dra = (dla * lsb) * (r * (1.0 - r))
            dia = (diu * u) * (ig * (1.0 - ig))
            acc_add(PV_BA, dra)
            acc_add(PV_BI, dia)
            du_s[sl, :] = diu * ig
            bg = bg_ref[sl, :]
            gc = gc_ref[sl, :]
            uc = uce_s[sl_e, :]
            ucp = uce_s[sl, :]
            uc1 = _shift_down(uc, ucp, 1, row)
            uc2 = _shift_down(uc, ucp, 2, row)
            v = _conv3(pv_ref, uc, uc1, uc2)
            yc = bg * v
            rrc = lax.rsqrt(_head_mean(yc * yc, CONV_HEAD) + RMS_EPS)
            nc = yc * rrc
            sgc = _sigmoid(gc)
            s_c = gc * sgc
            dzc = dyc_ref[sl, :]
            cgain = _pvb(pv_ref, PV_CG)
            acc_add(PV_CG, (dzc * nc) * s_c)
            p3 = ((dzc * nc) * cgain) * (sgc + s_c * (1.0 - sgc))
            dnc = (dzc * cgain) * s_c
            dyc = rrc * (dnc - nc * _head_mean(dnc * nc, CONV_HEAD))
            dv = dyc * bg
            duc = (_pvb(pv_ref, PV_CONV_W + 2) * dv + _pvb(pv_ref, PV_CONV_W + 1) * _shift_up(dv, dv_next, 1, row)
                   + _pvb(pv_ref, PV_CONV_W) * _shift_up(dv, dv_next, 2, row))
            acc_add(PV_CONV_W + 2, dv * uc)
            acc_add(PV_CONV_W + 1, dv * uc1)
            acc_add(PV_CONV_W, dv * uc2)
            return (a, dv, gg), (dyc * v, duc * xc_ref[sl, :], duc * cg_ref[sl, :], p3, p5, dra, dia, u)

        a_next, dv_next, g_next = _for_groups(
            ng, main_group, (an_s[...], dvn_s[...], gbuf_s[...]), descending=True,
            stores=(_rows_of(dp_ref, 0), _rows_of(dp_ref, 1), _rows_of(dp_ref, 2), _rows_of(dp_ref, 3), _rows_of(dp_ref, 5),
                    _rows_of(dgb_s, cols=slice(0, LW)), _rows_of(dgb_s, cols=slice(LW, 2 * LW)), _rows_of(ub_s)))
        an_s[...] = a_next
        dvn_s[...] = dv_next
        gbuf_s[...] = g_next

        dgb = dgb_s[...]
        du_s[...] += _mm_nt(dgb, wai_ref[0])
        gw_ref[0] += _mm_tn(ub_s[...], dgb)

        def lru_conv_group(gi, du_next):
            g = ng - 1 - gi
            r0 = pl.multiple_of(g * SUBLANES, SUBLANES)
            sl = pl.ds(r0, SUBLANES)
            du = du_s[sl, :]
            xl = xle_s[pl.ds(r0 + SUBLANES, SUBLANES), :]
            xlp = xle_s[sl, :]
            acc_add(PV_LRU_B, du)
            acc_add(PV_LRU_W + 3, du * xl)
            acc_add(PV_LRU_W + 2, du * _shift_down(xl, xlp, 1, row))
            acc_add(PV_LRU_W + 1, du * _shift_down(xl, xlp, 2, row))
            acc_add(PV_LRU_W, du * _shift_down(xl, xlp, 3, row))
            dxl = (((_pvb(pv_ref, PV_LRU_W + 3) * du + _pvb(pv_ref, PV_LRU_W + 2) * _shift_up(du, du_next, 1, row))
                    + _pvb(pv_ref, PV_LRU_W + 1) * _shift_up(du, du_next, 2, row))
                   + _pvb(pv_ref, PV_LRU_W) * _shift_up(du, du_next, 3, row))
            return du, (dxl,)

        dun_s[...] = _for_groups(ng, lru_conv_group, dun_s[...], descending=True, stores=(_rows_of(dp_ref, 4),))

        @pl.when(first_block)
        def _():
            sv_ref[...] = jnp.zeros_like(sv_ref)
            for k in range(N_ACC):
                tot = jnp.sum(acc_s[k], axis=0, keepdims=True)
                if k == PV_LAM:
                    tot = (RG_LRU_C * tot) / (1.0 + jnp.exp(pv_ref[PV_LAM:PV_LAM + 1, :]))
                sv_ref[k:k + 1, :] = tot

    def part(p):
        return pl.BlockSpec((None, tb, LW), lambda c, i: (2 * p + c // STRIPS_PER_CHUNK, nt - 1 - i, c % STRIPS_PER_CHUNK))

    def halo(p):
        return pl.BlockSpec((None, SUBLANES, LW), lambda c, i: (2 * p + c // STRIPS_PER_CHUNK,
                                                                jnp.maximum((nt - 1 - i) * gpb - 1, 0), c % STRIPS_PER_CHUNK))

    strip = pl.BlockSpec((tb, LW), lambda c, i: (nt - 1 - i, c))
    big = pltpu.VMEM((tb, LW), F32)
    big_e = pltpu.VMEM((tb + SUBLANES, LW), F32)
    small = pltpu.VMEM((SUBLANES, LW), F32)
    outs = pl.pallas_call(
        body, grid=(NS, nt),
        in_specs=[part(p) for p in range(N_PARTS)] + [
            strip, strip, strip, strip, strip, pl.BlockSpec((tb, LW), lambda c, i: (nt - 1 - i, NS + c)),
            halo(1), halo(2), halo(4),
            pl.BlockSpec((SUBLANES, LW), lambda c, i: (jnp.maximum((nt - 1 - i) * gpb - 1, 0), c)),
            pl.BlockSpec((PV_ROWS, LW), lambda c, i: (0, c)),
            pl.BlockSpec((1, LW, 2 * LW), lambda c, i: (c, 0, 0)),
            pl.BlockSpec(memory_space=pl.ANY)],
        out_specs=(pl.BlockSpec((N_PARTS, tb, LW), lambda c, i: (0, nt - 1 - i, c)),
                   pl.BlockSpec((1, LW, 2 * LW), lambda c, i: (c, 0, 0)),
                   pl.BlockSpec((PV_ROWS, LW), lambda c, i: (0, c)),
                   pl.BlockSpec(memory_space=pl.ANY)),
        out_shape=(jax.ShapeDtypeStruct((N_PARTS, t, D_PART), MXU_DTYPE),
                   jax.ShapeDtypeStruct((NS, LW, 2 * LW), F32), jax.ShapeDtypeStruct((PV_ROWS, D_PART), F32),
                   _chip_blocks_shape(sb_out, 1)),
        scratch_shapes=[small, pltpu.VMEM((tb, LW), MXU_DTYPE), big_e, big_e, big_e,
                        pltpu.VMEM((tb, 2 * LW), MXU_DTYPE), big, small,
                        pltpu.VMEM((N_ACC, SUBLANES, LW), F32), small, small, small,
                        pltpu.SemaphoreType.DMA((3,)), pltpu.SemaphoreType.DMA((3,))],
        compiler_params=_cp(ARB, ARB), name="mixer_backward",
    )(proj, proj, proj, proj, proj, proj, h, u, r, ig, dy, dy, proj, proj, proj, h, pvec, wai, sb_out)
    return outs


def _adamw(w, g, m, v):
    m = ADAM_B1 * m + (1.0 - ADAM_B1) * g
    v = ADAM_B2 * v + (1.0 - ADAM_B2) * (g * g)
    m_hat = m / (1.0 - ADAM_B1 ** ADAM_STEP)
    v_hat = v / (1.0 - ADAM_B2 ** ADAM_STEP)
    delta = -ADAM_LR * (m_hat / (jnp.sqrt(v_hat) + ADAM_EPS) + ADAM_WD * w)
    return delta, m, v


def _adam_w_in(w, m, v, g3):
    rows, cols = w.shape
    tr = 128

    def body(w_ref, m_ref, v_ref, g_ref, go_ref, d_ref, mo_ref, vo_ref):
        for s in range(CHUNKS_PER_BLOCK):
            cs = slice(CHUNK * s, CHUNK * (s + 1))
            g = g_ref[s]
            d, mn, vn = _adamw(w_ref[:, cs], g, m_ref[:, cs], v_ref[:, cs])
            go_ref[:, cs] = g
            d_ref[:, cs] = d
            mo_ref[:, cs] = mn
            vo_ref[:, cs] = vn

    blk = pl.BlockSpec((tr, cols), lambda i: (i, 0))
    return pl.pallas_call(
        body, grid=(rows // tr,),
        in_specs=[blk, blk, blk, pl.BlockSpec((CHUNKS_PER_BLOCK, tr, CHUNK), lambda i: (0, i, 0))],
        out_specs=(blk,) * 4, out_shape=(jax.ShapeDtypeStruct(w.shape, F32),) * 4,
        compiler_params=_cp(ARB), name="adam_w_in",
    )(w, m, v, g3)


def _adam_w_out(w, m, v, g):
    rows, cols = w.shape
    tr = 128

    def body(w_ref, m_ref, v_ref, g_ref, d_ref, mo_ref, vo_ref):
        d_ref[...], mo_ref[...], vo_ref[...] = _adamw(w_ref[...], g_ref[...], m_ref[...], v_ref[...])

    blk = pl.BlockSpec((tr, cols), lambda i: (i, 0))
    return pl.pallas_call(
        body, grid=(rows // tr,), in_specs=[blk] * 4, out_specs=(blk,) * 3,
        out_shape=(jax.ShapeDtypeStruct(w.shape, F32),) * 3,
        compiler_params=_cp(ARB), name="adam_w_out",
    )(w, m, v, g)


def _adam_small(ws, ms, vs, gs):
    n = len(ws)

    def body(*refs):
        w_r, m_r, v_r, g_r = refs[0:n], refs[n:2 * n], refs[2 * n:3 * n], refs[3 * n:4 * n]
        d_o, m_o, v_o = refs[4 * n:5 * n], refs[5 * n:6 * n], refs[6 * n:7 * n]
        for j in range(n):
            d_o[j][...], m_o[j][...], v_o[j][...] = _adamw(w_r[j][...], g_r[j][...], m_r[j][...], v_r[j][...])

    vm = pl.BlockSpec(memory_space=pltpu.VMEM)
    shapes = tuple(jax.ShapeDtypeStruct(w.shape, F32) for w in ws)
    outs = pl.pallas_call(
        body, in_specs=[vm] * (4 * n), out_specs=(vm,) * (3 * n), out_shape=shapes * 3,
        compiler_params=_cp(), name="adam_small",
    )(*ws, *ms, *vs, *gs)
    return outs[0:n], outs[n:2 * n], outs[2 * n:3 * n]


def _block_diag_strips(w, lw):
    heads = lw // LRU_HEAD
    w4 = w.reshape(D_PART // lw, heads, LRU_HEAD, LRU_HEAD)
    rows = [jnp.pad(w4[:, hh], ((0, 0), (0, 0), (LRU_HEAD * hh, lw - LRU_HEAD * (hh + 1)))) for hh in range(heads)]
    return jnp.concatenate(rows, axis=1)


def _gate_matrices(w_a, w_i, lw):
    return jnp.concatenate([_block_diag_strips(w_a, lw), _block_diag_strips(w_i, lw)], axis=2).astype(MXU_DTYPE)


def _strip_diag_blocks(g):
    g5 = g.reshape(NS, HEADS_PER_STRIP, LRU_HEAD, HEADS_PER_STRIP, LRU_HEAD)
    return jnp.stack([g5[:, hh, :, hh, :] for hh in range(HEADS_PER_STRIP)], axis=1).reshape(NS * HEADS_PER_STRIP, LRU_HEAD, LRU_HEAD)


def kernel(x, ln_g, w_in, conv_w, lru_conv_w, lru_conv_b, w_a, b_a, w_i, b_i, lam, conv_out_g, lru_out_g, w_out, final_g, loss_target, m_ln_g, m_w_in, m_conv_w, m_lru_conv_w, m_lru_conv_b, m_w_a, m_b_a, m_w_i, m_b_i, m_lam, m_conv_out_g, m_lru_out_g, m_w_out, m_final_g, v_ln_g, v_w_in, v_conv_w, v_lru_conv_w, v_lru_conv_b, v_w_a, v_b_a, v_w_i, v_b_i, v_lam, v_conv_out_g, v_lru_out_g, v_w_out, v_final_g):
    xi, yi, ci = lax.axis_index("x"), lax.axis_index("y"), lax.axis_index("c")
    k = 2 * xi + yi
    t = x.shape[1]
    x2 = x.reshape(t, D_MODEL)
    tgt2 = loss_target.reshape(t, D_MODEL)
    row = lambda a: a.reshape(1, -1)

    small = jnp.concatenate([conv_w, lru_conv_w, jnp.zeros((1, conv_w.shape[1]), F32)], axis=0)
    proj, xn, w12, sm4 = _gather_in_projection(x2, row(ln_g), w_in, small)
    convs = jnp.transpose(sm4, (1, 0, 2)).reshape(SUBLANES, D_PART)
    pvec = jnp.concatenate(
        [convs[0:7], row(lru_conv_b), row(b_a), row(b_i), row(lam), row(conv_out_g), row(lru_out_g),
         jnp.zeros((PV_ROWS - N_ACC, D_PART), F32)], axis=0)
    wai = _gate_matrices(w_a, w_i, LW)

    c_arr = jnp.reshape(ci, (1,)).astype(jnp.int32)
    yc, yl, h, u, r, ig, wo4 = _mixer_forward(proj, pvec, _gate_matrices(w_a, w_i, FWD_LW), w_out)
    wo = wo4.reshape(2 * D_PART, D_MODEL)
    do, dob, dy, st_out = _out_projection_loss(yc, yl, x2, tgt2, wo, row(final_g))
    go4, go4b = _w_out_grad(yc, yl, dob)
    s_out, sb_out = _add_sibling_halves(go4, go4b, c_arr, "add_sibling_halves_out")
    dproj, g_wai, svec, r2o = _mixer_backward(proj, h, u, r, ig, dy, pvec, wai, sb_out)
    gwa = _strip_diag_blocks(g_wai[:, :, 0:LW]).reshape(LRU_HEAD, D_PART)
    gwi = _strip_diag_blocks(g_wai[:, :, LW:2 * LW]).reshape(LRU_HEAD, D_PART)
    g12, g12b, red = _w_in_grad(xn, dproj, jnp.concatenate([svec, st_out, gwa, gwi], axis=0))
    s_in, sb_in = _add_sibling_halves(g12, g12b, c_arr, "add_sibling_halves_in")
    grad_x, st_in, r2i = _input_grad(dproj, w12, x2, do, row(ln_g), sb_in)
    f_in, f_out, red_ln = _finish_gradients(s_in, r2i, s_out, r2o, st_in)
    r_out = PV_ROWS
    r_wa = PV_ROWS + SUBLANES
    r_wi = r_wa + LRU_HEAD
    loss = red[r_out + 1, 0]

    g_w_in, d_w_in, nm_w_in, nv_w_in = _adam_w_in(w_in, m_w_in, v_w_in, f_in)
    g_w_out = f_out[0]
    d_w_out, nm_w_out, nv_w_out = _adam_w_out(w_out, m_w_out, v_w_out, g_w_out)

    ncol = conv_w.shape[1]
    conv_cols = lax.dynamic_slice(red, (0, k * ncol), (SUBLANES, ncol))
    g_small = {
        "ln_g": red_ln[0], "conv_w": conv_cols[0:3], "lru_conv_w": conv_cols[3:7], "lru_conv_b": red[PV_LRU_B],
        "w_a": red[r_wa:r_wa + LRU_HEAD].reshape(w_a.shape), "b_a": red[PV_BA],
        "w_i": red[r_wi:r_wi + LRU_HEAD].reshape(w_i.shape), "b_i": red[PV_BI], "lam": red[PV_LAM],
        "conv_out_g": red[PV_CG], "lru_out_g": red[PV_LG], "final_g": red[r_out],
    }
    w_small = {"ln_g": ln_g, "conv_w": conv_w, "lru_conv_w": lru_conv_w, "lru_conv_b": lru_conv_b, "w_a": w_a, "b_a": b_a,
               "w_i": w_i, "b_i": b_i, "lam": lam, "conv_out_g": conv_out_g, "lru_out_g": lru_out_g, "final_g": final_g}
    m_small = {"ln_g": m_ln_g, "conv_w": m_conv_w, "lru_conv_w": m_lru_conv_w, "lru_conv_b": m_lru_conv_b, "w_a": m_w_a,
               "b_a": m_b_a, "w_i": m_w_i, "b_i": m_b_i, "lam": m_lam, "conv_out_g": m_conv_out_g,
               "lru_out_g": m_lru_out_g, "final_g": m_final_g}
    v_small = {"ln_g": v_ln_g, "conv_w": v_conv_w, "lru_conv_w": v_lru_conv_w, "lru_conv_b": v_lru_conv_b, "w_a": v_w_a,
               "b_a": v_b_a, "w_i": v_w_i, "b_i": v_b_i, "lam": v_lam, "conv_out_g": v_conv_out_g,
               "lru_out_g": v_lru_out_g, "final_g": v_final_g}
    names = list(w_small)
    as2d = lambda a: a.reshape(1, -1) if a.ndim == 1 else a
    d_s, m_s, v_s = _adam_small([as2d(w_small[n]) for n in names], [as2d(m_small[n]) for n in names],
                                [as2d(v_small[n]) for n in names], [as2d(g_small[n]) for n in names])
    back = lambda n, a: a.reshape(w_small[n].shape)
    grads = {n: g_small[n] for n in names}
    deltas = {n: back(n, a) for n, a in zip(names, d_s)}
    new_m = {n: back(n, a) for n, a in zip(names, m_s)}
    new_v = {n: back(n, a) for n, a in zip(names, v_s)}
    grads["w_in"], deltas["w_in"], new_m["w_in"], new_v["w_in"] = g_w_in, d_w_in, nm_w_in, nv_w_in
    grads["w_out"], deltas["w_out"], new_m["w_out"], new_v["w_out"] = g_w_out, d_w_out, nm_w_out, nv_w_out

    order = ["ln_g", "w_in", "conv_w", "lru_conv_w", "lru_conv_b", "w_a", "b_a", "w_i", "b_i", "lam", "conv_out_g",
             "lru_out_g", "w_out", "final_g"]
    return (loss, grad_x.reshape(x.shape), *[grads[n] for n in order], *[deltas[n] for n in order],
            *[new_m[n] for n in order], *[new_v[n] for n in order])
```

```python
import functools

import jax
import jax.numpy as jnp
from jax import lax
from jax.experimental import pallas as pl
from jax.experimental.pallas import tpu as pltpu

F32 = jnp.float32
MXU_DTYPE = jnp.bfloat16

D_MODEL = 1024
D_PART = 1024
N_PARTS = 6
CHUNK = 512
CHUNKS_PER_BLOCK = 3
N_CHUNKS = 12
N_CHIPS = 4
SUBLANES = 8
LANES = 128
LW = 256
FWD_LW = 512
UNROLL = 8
NS = D_PART // LW
STRIPS_PER_CHUNK = CHUNK // LW
CONV_HEAD = 128
LRU_HEAD = 64
HEADS_PER_STRIP = LW // LRU_HEAD
RMS_EPS = 1e-6
RG_LRU_C = 8.0
ADAM_LR = 0.001
ADAM_B1 = 0.9
ADAM_B2 = 0.999
ADAM_EPS = 1e-08
ADAM_WD = 0.01
ADAM_STEP = 10

PV_CONV_W = 0
PV_LRU_W = 3
PV_LRU_B = 7
PV_BA = 8
PV_BI = 9
PV_LAM = 10
PV_CG = 11
PV_LG = 12
PV_ROWS = 16
N_ACC = 13

SLAB = 128
MESH = pl.DeviceIdType.MESH
VMEM_LIMIT = 56 * 1024 * 1024
ARB = "arbitrary"


def _cp(*sem, **kw):
    return pltpu.CompilerParams(dimension_semantics=sem or None, vmem_limit_bytes=VMEM_LIMIT, **kw)


def _mm(a, b):
    return jnp.dot(a, b, preferred_element_type=F32)


def _mm_nt(a, b):
    return lax.dot_general(a, b, (((1,), (1,)), ((), ())), preferred_element_type=F32)


def _mm_tn(a, b):
    return lax.dot_general(a, b, (((0,), (0,)), ((), ())), preferred_element_type=F32)


def _sigmoid(x):
    return 0.5 * jnp.tanh(0.5 * x) + 0.5


def _log_sigmoid(x):
    z = jnp.exp(-jnp.abs(x))
    u = 1.0 + z
    log1p = jnp.where(u == 1.0, z, jnp.log(u) * z / (u - 1.0))
    return jnp.minimum(x, 0.0) - log1p


def _head_mean(z, head):
    out = []
    for k in range(z.shape[1] // LANES):
        zk = z[:, LANES * k:LANES * (k + 1)]
        if head == LANES:
            m = jnp.sum(zk, axis=-1, keepdims=True) * (1.0 / head)
            out.append(jnp.broadcast_to(m, zk.shape))
        else:
            lo = lax.broadcasted_iota(jnp.int32, zk.shape, 1) < head
            s_lo = jnp.sum(jnp.where(lo, zk, 0.0), axis=-1, keepdims=True)
            s_hi = jnp.sum(jnp.where(lo, 0.0, zk), axis=-1, keepdims=True)
            out.append(jnp.where(lo, s_lo, s_hi) * (1.0 / head))
    return jnp.concatenate(out, axis=1)


def _shift_down(cur, prev, d, row):
    return pltpu.roll(jnp.where(row < SUBLANES - d, cur, prev), d, 0)


def _shift_up(cur, nxt, d, row):
    return pltpu.roll(jnp.where(row >= d, cur, nxt), SUBLANES - d, 0)


def _scan8_fwd(a, b, row):
    A, B = a, b
    for d in (1, 2, 4):
        m = row >= d
        a_s = jnp.where(m, pltpu.roll(A, d, 0), 1.0)
        b_s = jnp.where(m, pltpu.roll(B, d, 0), 0.0)
        B = A * b_s + B
        A = A * a_s
    return A, B


def _scan8_rev(a, b, row):
    A, B = a, b
    for d in (1, 2, 4):
        m = row < SUBLANES - d
        a_s = jnp.where(m, pltpu.roll(A, SUBLANES - d, 0), 1.0)
        b_s = jnp.where(m, pltpu.roll(B, SUBLANES - d, 0), 0.0)
        B = A * b_s + B
        A = A * a_s
    return A, B


def _decay(r, ls8):
    la = r * ls8
    a = jnp.exp(la)
    e2 = a * a
    em = -jnp.tanh(la) * (1.0 + e2)
    inv_mult = lax.rsqrt(em)
    return a, e2, em * inv_mult, inv_mult


def _mesh_pos():
    x, y, c = lax.axis_index("x"), lax.axis_index("y"), lax.axis_index("c")
    chips = [(1 - x, y), (x, 1 - y), (1 - x, 1 - y)]
    return x, y, c, chips


def _gather_in_projection(x, ln_g, w_in, small):
    t = x.shape[0]
    rb_x = 512
    rb_mm = 2048
    n_mm = t // rb_mm
    half = w_in.shape[0] // 2

    def body(x_hbm, g_ref, wi_ref, sm_ref, proj_hbm, xn_ref, w12_ref, sm4_ref,
             xbuf, obuf, x_sems, o_sems, send_sems, recv_sems):
        x_, y_, c, chips = _mesh_pos()
        k = 2 * x_ + y_
        sib = (x_, y_, 1 - c)
        sm4_ref[k] = sm_ref[...]

        def remote(ref, sem, to):
            return pltpu.make_async_remote_copy(src_ref=ref, dst_ref=ref, send_sem=send_sems.at[sem],
                                                recv_sem=recv_sems.at[sem], device_id=to, device_id_type=MESH)

        def chunk_of(chip, s):
            return CHUNKS_PER_BLOCK * (2 * chip[0] + chip[1]) + s

        def piece(q, core, first=0, rows=half):
            return w12_ref.at[q, pl.ds(pl.multiple_of(half * core + first, SUBLANES * 2), rows), :]

        nbr_x, nbr_y, diagonal = chips
        quarter = half // 2
        DIAG = [(0, 0, half, 0), (1, 0, quarter, 0), (1, quarter, quarter, 1), (2, 0, half, 1)]
        ici = lambda m, s: 2 * s + m
        dgn = lambda j: 6 + j
        to_sib = 10
        sml = lambda m: 20 + m

        sends = []
        for s in range(CHUNKS_PER_BLOCK):
            w12_ref[chunk_of((x_, y_), s)] = wi_ref[:, CHUNK * s:CHUNK * (s + 1)].astype(MXU_DTYPE)
            for m, chip in enumerate((nbr_x, nbr_y)):
                sends.append(remote(piece(chunk_of((x_, y_), s), c), ici(m, s), (*chip, c)))
                sends[-1].start()
        for m, chip in enumerate(chips):
            sends.append(remote(sm4_ref.at[k], sml(m), (*chip, c)))
            sends[-1].start()

        def x_copy(rb, slot):
            return pltpu.make_async_copy(x_hbm.at[pl.ds(rb * rb_x, rb_x), :], xbuf.at[slot], x_sems.at[slot])

        x_copy(0, 0).start()
        for rb in range(t // rb_x):
            slot = rb % 2
            x_copy(rb, slot).wait()
            if rb + 1 < t // rb_x:
                x_copy(rb + 1, 1 - slot).start()

            def norm_slab(sl, carry, rb=rb, slot=slot):
                xf = xbuf[slot, pl.ds(pl.multiple_of(sl * SLAB, SLAB), SLAB), :]
                r = lax.rsqrt(jnp.mean(xf * xf, axis=-1, keepdims=True) + RMS_EPS)
                xn_ref[pl.ds(pl.multiple_of(rb * rb_x + sl * SLAB, SLAB), SLAB), :] = ((xf * r) * g_ref[...]).astype(MXU_DTYPE)
                return carry

            lax.fori_loop(0, rb_x // SLAB, norm_slab, 0)

        def out_copy(q, i):
            return pltpu.make_async_copy(obuf.at[i], proj_hbm.at[q, pl.ds(pl.multiple_of(i * rb_mm, rb_mm), rb_mm), :],
                                         o_sems.at[i])

        def project(q, very_first):
            def row_block(i, carry):
                if not very_first:
                    out_copy(q, i).wait()
                obuf[i] = _mm(xn_ref[pl.ds(pl.multiple_of(i * rb_mm, rb_mm), rb_mm), :], w12_ref[q])
                out_copy(q, i).start()
                return carry

            lax.fori_loop(0, n_mm, row_block, 0)

        for s in range(CHUNKS_PER_BLOCK):
            project(chunk_of((x_, y_), s), very_first=(s == 0))

        steps = []
        for s in range(CHUNKS_PER_BLOCK):
            for m, chip in enumerate((nbr_x, nbr_y)):
                onward = [(first, rows, dgn(j), chips[via]) for j, (cs, first, rows, via) in enumerate(DIAG)
                          if cs == s and via == 1 - m]
                steps.append((chunk_of(chip, s), [(0, half, ici(m, s))], onward))
        for s in range(CHUNKS_PER_BLOCK):
            steps.append((chunk_of(diagonal, s), [(first, rows, dgn(j)) for j, (cs, first, rows, _) in enumerate(DIAG) if cs == s], []))

        def project_when_whole(step):
            q, pieces, _ = step
            for first, rows, sem in pieces:
                remote(piece(q, 1 - c, first, rows), to_sib + sem, sib).wait_recv()
            project(q, very_first=False)

        passed = []
        for j, (q, pieces, onward) in enumerate(steps):
            for first, rows, sem in pieces:
                remote(piece(q, c, first, rows), sem, sib).wait_recv()
            for first, rows, sem, chip in onward:
                passed.append(remote(piece(q, c, first, rows), sem, (*chip, c)))
                passed[-1].start()
            for first, rows, sem in pieces:
                passed.append(remote(piece(q, c, first, rows), to_sib + sem, sib))
                passed[-1].start()
            if j > 0:
                project_when_whole(steps[j - 1])
        project_when_whole(steps[-1])

        for m, chip in enumerate(chips):
            remote(sm4_ref.at[2 * chip[0] + chip[1]], sml(m), sib).wait_recv()
        for cp in sends + passed:
            cp.wait_send()
        for i in range(n_mm):
            out_copy(0, i).wait()

    vm = pl.BlockSpec(memory_space=pltpu.VMEM)
    hbm = pl.BlockSpec(memory_space=pl.ANY)
    n_sems = 23
    return pl.pallas_call(
        body,
        out_shape=(jax.ShapeDtypeStruct((N_CHUNKS, t, CHUNK), F32), jax.ShapeDtypeStruct((t, D_MODEL), MXU_DTYPE),
                   jax.ShapeDtypeStruct((N_CHUNKS, w_in.shape[0], CHUNK), MXU_DTYPE),
                   jax.ShapeDtypeStruct((N_CHIPS,) + small.shape, F32)),
        in_specs=[hbm, vm, vm, vm], out_specs=(hbm, vm, vm, vm),
        scratch_shapes=[pltpu.VMEM((2, rb_x, D_MODEL), F32), pltpu.VMEM((n_mm, rb_mm, CHUNK), F32),
                        pltpu.SemaphoreType.DMA((2,)), pltpu.SemaphoreType.DMA((n_mm,)),
                        pltpu.SemaphoreType.DMA((n_sems,)), pltpu.SemaphoreType.DMA((n_sems,))],
        compiler_params=_cp(), name="gather_in_projection",
    )(x, ln_g, w_in, small)


def _allreduce_behind(step, when, in_ref, acc_s, rbufs, out_ref, send_sems, recv_sems):
    x, y, c, _ = _mesh_pos()
    peers = [(x, y, 1 - c), (1 - x, y, c), (x, 1 - y, c)]

    def exchange(ph):
        return pltpu.make_async_remote_copy(src_ref=acc_s, dst_ref=rbufs[ph], send_sem=send_sems.at[ph],
                                            recv_sem=recv_sems.at[ph], device_id=peers[ph], device_id_type=MESH)

    @pl.when(step == when[0])
    def _():
        acc_s[...] = in_ref[...]
        exchange(0).start()

    for ph in (1, 2):
        @pl.when(step == when[ph])
        def _(ph=ph):
            exchange(ph - 1).wait()
            acc_s[...] = acc_s[...] + rbufs[ph - 1][...]
            exchange(ph).start()

    @pl.when(step == when[3])
    def _():
        exchange(2).wait()
        out_ref[...] = acc_s[...] + rbufs[2][...]


def _add_sibling_halves(g, gb, c_arr, name):
    n, rows, cols = g.shape
    half = rows // 2
    per = 2
    steps = n // per

    def body(c_ref, g_ref, gb_hbm, o_ref, ob_ref, rbuf, send_sems, recv_sems):
        q = pl.program_id(0)
        x, y, c, _ = _mesh_pos()
        theirs = pl.ds(pl.multiple_of(half * (1 - c), half), half)

        def copy(j):
            blocks = pl.ds(j * per, per)
            return pltpu.make_async_remote_copy(src_ref=gb_hbm.at[blocks, theirs, :], dst_ref=rbuf.at[blocks], send_sem=send_sems.at[j],
                                                recv_sem=recv_sems.at[j], device_id=(x, y, 1 - c), device_id_type=MESH)

        @pl.when(q == 0)
        def _():
            for j in range(steps):
                copy(j).start()

        copy(q).wait_recv()
        s = g_ref[...] + rbuf[pl.ds(q * per, per)].astype(F32)
        o_ref[...] = s
        ob_ref[...] = s.astype(jnp.bfloat16)

        @pl.when(q == steps - 1)
        def _():
            for j in range(steps):
                copy(j).wait_send()

    blk = pl.BlockSpec((per, half, cols), lambda q, c_ref: (q, 0, 0))
    return pl.pallas_call(
        body, out_shape=(jax.ShapeDtypeStruct((n, half, cols), F32), jax.ShapeDtypeStruct((n, half, cols), jnp.bfloat16)),
        grid_spec=pltpu.PrefetchScalarGridSpec(
            num_scalar_prefetch=1, grid=(steps,),
            in_specs=[pl.BlockSpec((per, half, cols), lambda q, c_ref: (q, c_ref[0], 0)), pl.BlockSpec(memory_space=pl.ANY)],
            out_specs=(blk, blk),
            scratch_shapes=[pltpu.VMEM((n, half, cols), jnp.bfloat16), pltpu.SemaphoreType.DMA((steps,)),
                            pltpu.SemaphoreType.DMA((steps,))]),
        compiler_params=_cp(ARB), name=name,
    )(c_arr, g, gb)


def _chip_block_copies(s_ref, r_ref, n_sub, send_sems, recv_sems):
    x, y, c, chips = _mesh_pos()
    cps = []
    for m, chip in enumerate(chips):
        kk = 2 * chip[0] + chip[1]
        cps.append(pltpu.make_async_remote_copy(
            src_ref=s_ref.at[pl.ds(n_sub * kk, n_sub)], dst_ref=r_ref.at[m],
            send_sem=send_sems.at[m], recv_sem=recv_sems.at[m], device_id=(*chip, c), device_id_type=MESH))
    return cps


def _gather_w_out(step, n_steps, wo_ref, wob_s, wo4_ref, local_sem, send_sems, recv_sems):
    x, y, c, chips = _mesh_pos()
    sib = (x, y, 1 - c)
    half = wo_ref.shape[0] // 2

    def rows(core):
        return pl.ds(pl.multiple_of(half * core, half), half)

    def block_half(chip, core):
        return wo4_ref.at[2 * chip[0] + chip[1], rows(core), :]

    def remote(src, dst, sem, to):
        return pltpu.make_async_remote_copy(src_ref=src, dst_ref=dst, send_sem=send_sems.at[sem], recv_sem=recv_sems.at[sem],
                                            device_id=to, device_id_type=MESH)

    local = pltpu.make_async_copy(wob_s, wo4_ref.at[2 * x + y], local_sem)
    ici = [remote(wob_s.at[rows(c), :], block_half((x, y), c), m, (*chip, c)) for m, chip in enumerate(chips)]
    fwd = [remote(block_half(chip, c), block_half(chip, c), 3 + m, sib) for m, chip in enumerate(chips)]

    @pl.when(step == 0)
    def _():
        wob_s[...] = wo_ref[...].astype(MXU_DTYPE)
        local.start()
        for cp in ici:
            cp.start()

    @pl.when(step == n_steps // 2)
    def _():
        for m, chip in enumerate(chips):
            remote(block_half(chip, c), block_half(chip, c), m, sib).wait_recv()
            fwd[m].start()

    @pl.when(step == n_steps - 1)
    def _():
        for m, chip in enumerate(chips):
            remote(block_half(chip, 1 - c), block_half(chip, 1 - c), 3 + m, sib).wait_recv()
        for cp in ici + fwd:
            cp.wait_send()
        local.wait()


def _chip_blocks_shape(s, n_sub):
    return jax.ShapeDtypeStruct((3, n_sub) + s.shape[1:], s.dtype)


def _finish_gradients_adamw(s_in, r_in, s_out, r_out, v, p_in, p_out):
    n_dev = 8
    n_in, n_out = r_in.shape[1], r_out.shape[1]
    slab = 2 * SUBLANES

    def body(si_hbm, ri_hbm, so_hbm, ro_hbm, v_ref, wi_hbm, mi_hbm, vi_hbm, wo_hbm, mo_hbm, vo_hbm,
             gi_hbm, di_hbm, nmi_hbm, nvi_hbm, go_hbm, do_hbm, nmo_hbm, nvo_hbm, tot_ref,
             g_in, b_in, g_out, b_out, wi, mi, vi, wo, mo, vo, slots, load_sems, store_sems, send_sems, recv_sems):
        x, y, c, _ = _mesh_pos()
        k = 2 * x + y
        sib = (x, y, 1 - c)
        me = 4 * x + 2 * y + c
        half_rows = lambda g, cc: pl.ds(pl.multiple_of(g.shape[1] // 2 * cc, g.shape[1] // 2), g.shape[1] // 2)
        loads = [(si_hbm.at[pl.ds(n_in * k, n_in)], g_in.at[:, half_rows(g_in, c), :]), (ri_hbm, b_in),
                 (so_hbm.at[pl.ds(n_out * k, n_out)], g_out.at[:, half_rows(g_out, c), :]), (ro_hbm, b_out),
                 (wi_hbm, wi), (mi_hbm, mi), (vi_hbm, vi), (wo_hbm, wo), (mo_hbm, mo), (vo_hbm, vo)]
        loads = [pltpu.make_async_copy(s, d, load_sems.at[j]) for j, (s, d) in enumerate(loads)]
        for cp in loads:
            cp.start()
        slots[me] = v_ref[...]

        def remote(src, dst, sem, to):
            return pltpu.make_async_remote_copy(src_ref=src, dst_ref=dst, send_sem=send_sems.at[sem],
                                                recv_sem=recv_sems.at[sem], device_id=to, device_id_type=MESH)

        small = []
        for d in range(1, n_dev):
            peer = (1 - x if d & 4 else x, 1 - y if d & 2 else y, 1 - c if d & 1 else c)
            small.append(remote(slots.at[me], slots.at[me], d - 1, peer))
            small[-1].start()
        for cp in loads[0:4]:
            cp.wait()
        blocks = ((g_in, b_in, (wi, mi, vi), gi_hbm, (di_hbm, nmi_hbm, nvi_hbm)),
                  (g_out, b_out, (wo, mo, vo), go_hbm, (do_hbm, nmo_hbm, nvo_hbm)))
        swaps = []
        for j, (g, b, _, _, _) in enumerate(blocks):
            mine = half_rows(g, c)
            g[:, mine, :] = ((g[:, mine, :] + b[0].astype(F32)) + b[1].astype(F32)) + b[2].astype(F32)
            swaps.append(remote(g.at[:, mine, :], g.at[:, mine, :], n_dev - 1 + j, sib))
            swaps[-1].start()
        for cp in loads[4:]:
            cp.wait()

        stores = []

        def store(src, dst):
            stores.append(pltpu.make_async_copy(src, dst, store_sems.at[len(stores)]))
            stores[-1].start()

        def adamw_half(g, p, outs, cc):
            w_s, m_s, v_s = p
            n, half, cols = g.shape[0], g.shape[1] // 2, g.shape[2]

            def step(i, carry):
                rs = pl.ds(pl.multiple_of(half * cc + slab * i, slab), slab)
                for q in range(n):
                    cs = slice(cols * q, cols * (q + 1))
                    w_s[rs, cs], m_s[rs, cs], v_s[rs, cs] = _adamw(w_s[rs, cs], g[q, rs, :], m_s[rs, cs], v_s[rs, cs])
                return carry

            lax.fori_loop(0, half // slab, step, 0)
            rows = half_rows(g, cc)
            for s, o_hbm in zip(p, outs):
                store(s.at[rows, :], o_hbm.at[rows, :])

        for g, _, p, _, outs in blocks:
            adamw_half(g, p, outs, c)
        for cp in swaps:
            cp.wait()
        for g, _, p, g_hbm, outs in blocks:
            for q in range(g.shape[0]):
                store(g.at[q], g_hbm.at[:, g.shape[2] * q:g.shape[2] * (q + 1)])
            adamw_half(g, p, outs, 1 - c)
        for cp in small + stores:
            cp.wait()
        total = slots[0]
        for dev in range(1, n_dev):
            total = total + slots[dev]
        tot_ref[...] = total

    hbm = pl.BlockSpec(memory_space=pl.ANY)
    vm = pl.BlockSpec(memory_space=pltpu.VMEM)
    full = lambda s, n: (n, 2 * s.shape[1], s.shape[2])
    n_stores = 2 * (2 * 3) + n_in + n_out
    w_i, w_o = p_in[0], p_out[0]
    assert w_i.shape == (2 * s_in.shape[1], n_in * s_in.shape[2]) and w_o.shape == (2 * s_out.shape[1], n_out * s_out.shape[2])
    return pl.pallas_call(
        body,
        out_shape=(jax.ShapeDtypeStruct(w_i.shape, F32),) * 4 + (jax.ShapeDtypeStruct(w_o.shape, F32),) * 4
        + (jax.ShapeDtypeStruct(v.shape, F32),),
        in_specs=[hbm, hbm, hbm, hbm, vm] + [hbm] * 6, out_specs=(hbm,) * 8 + (vm,),
        scratch_shapes=[pltpu.VMEM(full(s_in, n_in), F32), pltpu.VMEM(r_in.shape, r_in.dtype),
                        pltpu.VMEM(full(s_out, n_out), F32), pltpu.VMEM(r_out.shape, r_out.dtype)]
        + [pltpu.VMEM(w_i.shape, F32)] * 3 + [pltpu.VMEM(w_o.shape, F32)] * 3
        + [pltpu.VMEM((n_dev,) + v.shape, F32), pltpu.SemaphoreType.DMA((10,)), pltpu.SemaphoreType.DMA((n_stores,)),
           pltpu.SemaphoreType.DMA((n_dev + 1,)), pltpu.SemaphoreType.DMA((n_dev + 1,))],
        compiler_params=_cp(), name="finish_gradients_adamw",
    )(s_in, r_in, s_out, r_out, v, *p_in, *p_out)


def _out_projection_loss(yc, yl, x, target, wo, final_g):
    t = x.shape[0]
    tm = 512

    def body(yc_ref, yl_ref, x_ref, t_ref, wo_ref, fg_ref, do_ref, dob_ref, dy_ref, st_ref, y_wo):
        @pl.when(pl.program_id(0) == 0)
        def _():
            st_ref[...] = jnp.zeros_like(st_ref)

        y_wo[...] = _mm(yc_ref[...], wo_ref[0:D_PART, :]) + _mm(yl_ref[...], wo_ref[D_PART:2 * D_PART, :])

        def norm_loss_slab(s, carry):
            g_sum, loss_sum = carry
            rows = pl.ds(pl.multiple_of(s * SLAB, SLAB), SLAB)
            o = x_ref[rows, :] + y_wo[rows, :]
            r2 = lax.rsqrt(jnp.mean(o * o, axis=-1, keepdims=True) + RMS_EPS)
            ohat = o * r2
            fg = fg_ref[...]
            diff = ohat * fg - t_ref[rows, :]
            dout = diff * (1.0 / D_MODEL)
            gp = dout * fg
            do = r2 * (gp - ohat * jnp.mean(gp * ohat, axis=-1, keepdims=True))
            do_ref[rows, :] = do
            dob_ref[rows, :] = do.astype(MXU_DTYPE)
            loss = 0.5 * jnp.sum(jnp.sum(diff * diff, axis=-1, keepdims=True) * (1.0 / D_MODEL), axis=0, keepdims=True)
            return g_sum + jnp.sum(dout * ohat, axis=0, keepdims=True), loss_sum + loss

        g_sum, loss_sum = lax.fori_loop(0, tm // SLAB, norm_loss_slab,
                                        (jnp.zeros((1, D_MODEL), F32), jnp.zeros((1, 1), F32)))
        st_ref[0:1, :] += g_sum
        st_ref[1:2, :] += jnp.broadcast_to(loss_sum, (1, D_MODEL))
        dy_ref[...] = _mm_nt(dob_ref[...], wo_ref[...])

    row = lambda i: (i, 0)
    fix = lambda i: (0, 0)
    return pl.pallas_call(
        body, grid=(t // tm,),
        in_specs=[pl.BlockSpec((tm, D_PART), row), pl.BlockSpec((tm, D_PART), row),
                  pl.BlockSpec((tm, D_MODEL), row), pl.BlockSpec((tm, D_MODEL), row),
                  pl.BlockSpec((2 * D_PART, D_MODEL), fix), pl.BlockSpec((1, D_MODEL), fix)],
        out_specs=(pl.BlockSpec((tm, D_MODEL), row), pl.BlockSpec((tm, D_MODEL), row),
                   pl.BlockSpec((tm, 2 * D_PART), row), pl.BlockSpec((SUBLANES, D_MODEL), fix)),
        out_shape=(jax.ShapeDtypeStruct((t, D_MODEL), F32), jax.ShapeDtypeStruct((t, D_MODEL), MXU_DTYPE),
                   jax.ShapeDtypeStruct((t, 2 * D_PART), F32), jax.ShapeDtypeStruct((SUBLANES, D_MODEL), F32)),
        scratch_shapes=[pltpu.VMEM((tm, D_MODEL), F32)],
        compiler_params=_cp(ARB), name="out_projection_loss",
    )(yc, yl, x, target, wo, final_g)


def _input_grad(dproj, w12, x, do, ln_g, sb_in):
    t = x.shape[0]
    tm = 1024

    def body(dp_ref, w_ref, x_ref, do_ref, g_ref, s_ref, gx_ref, st_ref, r_ref, acc, send_sems, recv_sems):
        i, p = pl.program_id(0), pl.program_id(1)

        @pl.when((i == 0) & (p == 0))
        def _():
            st_ref[...] = jnp.zeros_like(st_ref)
            for cp in _chip_block_copies(s_ref, r_ref, CHUNKS_PER_BLOCK, send_sems, recv_sems):
                cp.start()

        @pl.when((i == t // tm - 1) & (p == N_PARTS - 1))
        def _():
            for cp in _chip_block_copies(s_ref, r_ref, CHUNKS_PER_BLOCK, send_sems, recv_sems):
                cp.wait()

        @pl.when(p == 0)
        def _():
            acc[...] = jnp.zeros_like(acc)

        acc[...] += _mm_nt(dp_ref[0], jnp.concatenate([w_ref[0], w_ref[1]], axis=1))

        @pl.when(p == N_PARTS - 1)
        def _():
            def norm_bwd_slab(s, g_sum):
                rows = pl.ds(pl.multiple_of(s * SLAB, SLAB), SLAB)
                xf = x_ref[rows, :]
                r = lax.rsqrt(jnp.mean(xf * xf, axis=-1, keepdims=True) + RMS_EPS)
                xhat = xf * r
                dxn = acc[rows, :]
                dxh = dxn * g_ref[...]
                gx_ref[rows, :] = do_ref[rows, :] + r * (dxh - xhat * jnp.mean(dxh * xhat, axis=-1, keepdims=True))
                return g_sum + jnp.sum(dxn * xhat, axis=0, keepdims=True)

            st_ref[0:1, :] += lax.fori_loop(0, tm // SLAB, norm_bwd_slab, jnp.zeros((1, D_MODEL), F32))

    row = lambda i, p: (i, 0)
    fix = lambda i, p: (0, 0)
    return pl.pallas_call(
        body, grid=(t // tm, N_PARTS),
        in_specs=[
            pl.BlockSpec((1, tm, D_PART), lambda i, p: (p, i, 0)),
            pl.BlockSpec((2, D_MODEL, CHUNK), lambda i, p: (p, 0, 0)),
            pl.BlockSpec((tm, D_MODEL), row), pl.BlockSpec((tm, D_MODEL), row), pl.BlockSpec((1, D_MODEL), fix),
            pl.BlockSpec(memory_space=pl.ANY)],
        out_specs=(pl.BlockSpec((tm, D_MODEL), row), pl.BlockSpec((SUBLANES, D_MODEL), fix),
                   pl.BlockSpec(memory_space=pl.ANY)),
        out_shape=(jax.ShapeDtypeStruct((t, D_MODEL), F32), jax.ShapeDtypeStruct((SUBLANES, D_MODEL), F32),
                   _chip_blocks_shape(sb_in, CHUNKS_PER_BLOCK)),
        scratch_shapes=[pltpu.VMEM((tm, D_MODEL), F32), pltpu.SemaphoreType.DMA((3,)), pltpu.SemaphoreType.DMA((3,))],
        compiler_params=_cp(ARB, ARB), name="input_grad",
    )(dproj, w12, x, do, ln_g, sb_in)


def _w_in_grad(xn, dproj, small):
    t = xn.shape[0]
    small_shape = pltpu.VMEM(small.shape, F32)

    def body(xn_ref, dp_ref, sm_ref, o_ref, ob_ref, red_ref, acc_s, r0, r1, r2, send_sems, recv_sems):
        _allreduce_behind(pl.program_id(0), (0, 1, 3, N_PARTS - 1), sm_ref, acc_s, (r0, r1, r2), red_ref, send_sems, recv_sems)
        g = _mm_tn(xn_ref[...], dp_ref[0])
        for s in range(2):
            o_ref[s] = g[:, CHUNK * s:CHUNK * (s + 1)]
            ob_ref[s] = g[:, CHUNK * s:CHUNK * (s + 1)].astype(jnp.bfloat16)

    whole = pl.BlockSpec(small.shape, lambda p: (0, 0))
    pair = pl.BlockSpec((2, D_MODEL, CHUNK), lambda p: (p, 0, 0))
    return pl.pallas_call(
        body, grid=(N_PARTS,),
        in_specs=[pl.BlockSpec((t, D_MODEL), lambda p: (0, 0)),
                  pl.BlockSpec((1, t, D_PART), lambda p: (p, 0, 0)), whole],
        out_specs=(pair, pair, whole),
        out_shape=(jax.ShapeDtypeStruct((N_CHUNKS, D_MODEL, CHUNK), F32),
                   jax.ShapeDtypeStruct((N_CHUNKS, D_MODEL, CHUNK), jnp.bfloat16), jax.ShapeDtypeStruct(small.shape, F32)),
        scratch_shapes=[small_shape] * 4 + [pltpu.SemaphoreType.DMA((3,)), pltpu.SemaphoreType.DMA((3,))],
        compiler_params=_cp(ARB), name="w_in_grad",
    )(xn, dproj, small)


def _w_out_grad(yc, yl, dob):
    t = yc.shape[0]
    tk = 2048

    def body(yc_ref, yl_ref, do_ref, o_ref, ob_ref):
        j, kk = pl.program_id(0), pl.program_id(1)

        def accumulate(y_ref):
            @pl.when(kk == 0)
            def _():
                o_ref[...] = jnp.zeros_like(o_ref)

            o_ref[...] += _mm_tn(y_ref[...], do_ref[...])

            @pl.when(kk == t // tk - 1)
            def _():
                ob_ref[...] = o_ref[...].astype(jnp.bfloat16)

        pl.when(j == 0)(functools.partial(accumulate, yc_ref))
        pl.when(j == 1)(functools.partial(accumulate, yl_ref))

    def rows_of(half):
        return lambda j, kk: (jnp.where(j == half, kk, 0), 0)

    half = pl.BlockSpec((D_PART, D_MODEL), lambda j, kk: (j, 0))
    out, out_b = pl.pallas_call(
        body, grid=(2, t // tk),
        in_specs=[pl.BlockSpec((tk, D_PART), rows_of(0)), pl.BlockSpec((tk, D_PART), rows_of(1)),
                  pl.BlockSpec((tk, D_MODEL), lambda j, kk: (kk, 0))],
        out_specs=(half, half),
        out_shape=(jax.ShapeDtypeStruct((2 * D_PART, D_MODEL), F32), jax.ShapeDtypeStruct((2 * D_PART, D_MODEL), jnp.bfloat16)),
        compiler_params=_cp(ARB, ARB), name="w_out_grad",
    )(yc, yl, dob)
    blocks = (N_CHIPS, 2 * D_PART // N_CHIPS, D_MODEL)
    return out.reshape(blocks), out_b.reshape(blocks)


def _for_groups(n, fn, init, unroll=UNROLL, stores=(), descending=False):
    assert unroll % 2 == 0 and n % unroll == 0

    def trip(j, carry):
        held = None
        for uu in range(unroll):
            idx = j * unroll + uu
            carry, values = fn(idx, carry)
            if uu % 2 == 0:
                held = values
                continue
            low_group = n - 1 - idx if descending else idx - 1
            rows = pl.ds(pl.multiple_of(low_group * SUBLANES, 2 * SUBLANES), 2 * SUBLANES)
            pairs = zip(values, held) if descending else zip(held, values)
            for store, (lo, hi) in zip(stores, pairs, strict=True):
                store(rows, jnp.concatenate([lo, hi], axis=0).astype(MXU_DTYPE))
        return carry

    return lax.fori_loop(0, n // unroll, trip, init)


def _rows_of(ref, *lead, cols=slice(None)):
    def store(rows, value):
        ref[(*lead, rows, cols)] = value

    return store


def _pvb(pv_ref, r):
    return jnp.broadcast_to(pv_ref[r:r + 1, :], (SUBLANES, pv_ref.shape[1]))


def _conv3(pv_ref, u, u1, u2):
    return (_pvb(pv_ref, PV_CONV_W) * u2 + _pvb(pv_ref, PV_CONV_W + 1) * u1) + _pvb(pv_ref, PV_CONV_W + 2) * u


def _conv4(pv_ref, v, v1, v2, v3):
    return ((((_pvb(pv_ref, PV_LRU_W) * v3 + _pvb(pv_ref, PV_LRU_W + 1) * v2) + _pvb(pv_ref, PV_LRU_W + 2) * v1)
             + _pvb(pv_ref, PV_LRU_W + 3) * v) + _pvb(pv_ref, PV_LRU_B))


def _mixer_forward(proj, pvec, wai, w_out):
    t = proj.shape[1]
    tb = 512
    ng = tb // SUBLANES
    nt = t // tb
    lw = FWD_LW
    ns = D_PART // lw
    per_chunk = CHUNK // lw

    def body(bg_ref, cg_ref, xc_ref, gc_ref, xl_ref, gl_ref, pv_ref, wai_ref, wo_ref,
             yc_ref, yl_ref, h_ref, u_s, r_ref, ig_ref, wo4_ref,
             ucp_s, xlp_s, ls_s, hbuf_s, ub_s, gate_s, wob_s, local_sem, send_sems, recv_sems):
        _gather_w_out(pl.program_id(0) * nt + pl.program_id(1), ns * nt, wo_ref, wob_s, wo4_ref, local_sem, send_sems, recv_sems)

        @pl.when(pl.program_id(1) == 0)
        def _():
            ucp_s[...] = jnp.zeros_like(ucp_s)
            xlp_s[...] = jnp.zeros_like(xlp_s)
            hbuf_s[...] = jnp.zeros_like(hbuf_s)

        row = lax.broadcasted_iota(jnp.int32, (SUBLANES, lw), 0)
        ls_s[...] = RG_LRU_C * _log_sigmoid(_pvb(pv_ref, PV_LAM))

        def conv_group(g, carry):
            ucp, xlp = carry
            sl = pl.ds(pl.multiple_of(g * SUBLANES, SUBLANES), SUBLANES)
            uc = cg_ref[sl, :] * xc_ref[sl, :]
            v = _conv3(pv_ref, uc, _shift_down(uc, ucp, 1, row), _shift_down(uc, ucp, 2, row))
            yc = bg_ref[sl, :] * v
            rr = lax.rsqrt(_head_mean(yc * yc, CONV_HEAD) + RMS_EPS)
            gc = gc_ref[sl, :]
            zc = ((yc * rr) * _pvb(pv_ref, PV_CG)) * (gc * _sigmoid(gc))
            xl = xl_ref[sl, :]
            u = _conv4(pv_ref, xl, _shift_down(xl, xlp, 1, row), _shift_down(xl, xlp, 2, row), _shift_down(xl, xlp, 3, row))
            u_s[sl, :] = u
            return (uc, xl), (zc, u)

        ucp, xlp = _for_groups(ng, conv_group, (ucp_s[...], xlp_s[...]), unroll=2 * UNROLL,
                               stores=(_rows_of(yc_ref), _rows_of(ub_s)))
        ucp_s[...] = ucp
        xlp_s[...] = xlp

        gate_s[...] = _mm(ub_s[...], wai_ref[0])

        def lru_group(g, h_before):
            sl = pl.ds(pl.multiple_of(g * SUBLANES, SUBLANES), SUBLANES)
            u = u_s[sl, :]
            r = _sigmoid(gate_s[sl, 0:lw] + _pvb(pv_ref, PV_BA))
            ig = _sigmoid(gate_s[sl, lw:2 * lw] + _pvb(pv_ref, PV_BI))
            r_ref[sl, :] = r
            ig_ref[sl, :] = ig
            a, _, mult, _ = _decay(r, ls_s[...])
            A, B = _scan8_fwd(a, mult * (ig * u), row)
            h = B + A * jnp.broadcast_to(h_before[SUBLANES - 1:SUBLANES, :], (SUBLANES, lw))
            h_ref[sl, :] = h
            rr = lax.rsqrt(_head_mean(h * h, LRU_HEAD) + RMS_EPS)
            gl = gl_ref[sl, :]
            return h, (((h * rr) * _pvb(pv_ref, PV_LG)) * (gl * _sigmoid(gl)),)

        hbuf_s[...] = _for_groups(ng, lru_group, hbuf_s[...], unroll=2 * UNROLL, stores=(_rows_of(yl_ref),))

    def part(p):
        return pl.BlockSpec((None, tb, lw), lambda c, i: (2 * p + c // per_chunk, i, c % per_chunk))

    strip = pl.BlockSpec((tb, lw), lambda c, i: (i, c))
    small = pltpu.VMEM((SUBLANES, lw), F32)
    return pl.pallas_call(
        body, grid=(ns, nt),
        in_specs=[part(p) for p in range(N_PARTS)] + [
            pl.BlockSpec((PV_ROWS, lw), lambda c, i: (0, c)),
            pl.BlockSpec((1, lw, 2 * lw), lambda c, i: (c, 0, 0)),
            pl.BlockSpec(w_out.shape, lambda c, i: (0, 0))],
        out_specs=(strip,) * 6 + (pl.BlockSpec(memory_space=pl.ANY),),
        out_shape=(jax.ShapeDtypeStruct((t, D_PART), MXU_DTYPE),) * 2 + (jax.ShapeDtypeStruct((t, D_PART), F32),) * 4 + (
            jax.ShapeDtypeStruct((N_CHIPS,) + w_out.shape, MXU_DTYPE),),
        scratch_shapes=[small, small, small, small, pltpu.VMEM((tb, lw), MXU_DTYPE),
                        pltpu.VMEM((tb, 2 * lw), F32), pltpu.VMEM(w_out.shape, MXU_DTYPE),
                        pltpu.SemaphoreType.DMA, pltpu.SemaphoreType.DMA((6,)), pltpu.SemaphoreType.DMA((6,))],
        compiler_params=_cp(ARB, ARB), name="mixer_forward",
    )(proj, proj, proj, proj, proj, proj, pvec, wai, w_out)


def _mixer_backward(proj, h, u, r, ig, dy, pvec, wai, sb_out):
    t = proj.shape[1]
    tb = 1024
    ng = tb // SUBLANES
    nt = t // tb
    gpb = tb // SUBLANES

    def body(bg_ref, cg_ref, xc_ref, gc_ref, xl_ref, gl_ref, h_ref, u_ref, r_ref, ig_ref, dyc_ref, dyl_ref,
             cgh_ref, xch_ref, xlh_ref, hh_ref, pv_ref, wai_ref, so_ref,
             dp_ref, gw_ref, sv_ref, ro_ref,
             ls_s, ub_s, uce_s, xle_s, he_s, dgb_s, du_s, gbuf_s,
             acc_s, an_s, dvn_s, dun_s, send_sems, recv_sems):
        i = pl.program_id(1)
        first_block = i == nt - 1

        @pl.when((pl.program_id(0) == 0) & (i == 0))
        def _():
            for cp in _chip_block_copies(so_ref, ro_ref, 1, send_sems, recv_sems):
                cp.start()

        @pl.when((pl.program_id(0) == NS - 1) & (i == nt - 1))
        def _():
            for cp in _chip_block_copies(so_ref, ro_ref, 1, send_sems, recv_sems):
                cp.wait()

        @pl.when(i == 0)
        def _():
            acc_s[...] = jnp.zeros_like(acc_s)
            gw_ref[...] = jnp.zeros_like(gw_ref)
            an_s[...] = jnp.zeros_like(an_s)
            dvn_s[...] = jnp.zeros_like(dvn_s)
            dun_s[...] = jnp.zeros_like(dun_s)
            gbuf_s[...] = jnp.zeros_like(gbuf_s)

        row = lax.broadcasted_iota(jnp.int32, (SUBLANES, LW), 0)
        ls_s[...] = RG_LRU_C * _log_sigmoid(_pvb(pv_ref, PV_LAM))
        keep = jnp.where(first_block, 0.0, 1.0)
        uce_s[0:SUBLANES, :] = (cgh_ref[...] * xch_ref[...]) * keep
        xle_s[0:SUBLANES, :] = xlh_ref[...] * keep
        he_s[0:SUBLANES, :] = hh_ref[...] * keep
        xle_s[SUBLANES:SUBLANES + tb, :] = xl_ref[...]
        he_s[SUBLANES:SUBLANES + tb, :] = h_ref[...]

        uce_s[SUBLANES:SUBLANES + tb, :] = cg_ref[...] * xc_ref[...]

        def acc_add(k, v):
            acc_s[k] += v

        def main_group(gi, carry):
            a_next, dv_next, g_next = carry
            g = ng - 1 - gi
            r0 = pl.multiple_of(g * SUBLANES, SUBLANES)
            sl = pl.ds(r0, SUBLANES)
            sl_e = pl.ds(r0 + SUBLANES, SUBLANES)
            lsb = ls_s[...]
            u = u_ref[sl, :]
            r = r_ref[sl, :]
            ig = ig_ref[sl, :]
            a, e2, mult, inv_mult = _decay(r, lsb)
            gl = gl_ref[sl, :]
            sg = _sigmoid(gl)
            s_l = gl * sg
            h8 = he_s[sl_e, :]
            hprev = _shift_down(h8, he_s[sl, :], 1, row)
            rr = lax.rsqrt(_head_mean(h8 * h8, LRU_HEAD) + RMS_EPS)
            n = h8 * rr
            dz = dyl_ref[sl, :]
            lg = _pvb(pv_ref, PV_LG)
            acc_add(PV_LG, (dz * n) * s_l)
            p5 = ((dz * n) * lg) * (sg + s_l * (1.0 - sg))
            dn = (dz * lg) * s_l
            dh = rr * (dn - n * _head_mean(dn * n, LRU_HEAD))
            A, B = _scan8_rev(_shift_up(a, a_next, 1, row), dh, row)
            gg = B + A * jnp.broadcast_to(g_next[0:1, :], (SUBLANES, LW))
            da = gg * hprev
            iu = ig * u
            diu = gg * mult
            dla = da * a - (gg * iu) * (e2 * inv_mult)
            acc_add(PV_LAM, dla * r)
            dra = (dla * lsb) * (r * (1.0 - r))
            dia = (diu * u) * (ig * (1.0 - ig))
            acc_add(PV_BA, dra)
            acc_add(PV_BI, dia)
            du_s[sl, :] = diu * ig
            bg = bg_ref[sl, :]
            gc = gc_ref[sl, :]
            uc = uce_s[sl_e, :]
            ucp = uce_s[sl, :]
            uc1 = _shift_down(uc, ucp, 1, row)
            uc2 = _shift_down(uc, ucp, 2, row)
            v = _conv3(pv_ref, uc, uc1, uc2)
            yc = bg * v
            rrc = lax.rsqrt(_head_mean(yc * yc, CONV_HEAD) + RMS_EPS)
            nc = yc * rrc
            sgc = _sigmoid(gc)
            s_c = gc * sgc
            dzc = dyc_ref[sl, :]
            cgain = _pvb(pv_ref, PV_CG)
            acc_add(PV_CG, (dzc * nc) * s_c)
            p3 = ((dzc * nc) * cgain) * (sgc + s_c * (1.0 - sgc))
            dnc = (dzc * cgain) * s_c
            dyc = rrc * (dnc - nc * _head_mean(dnc * nc, CONV_HEAD))
            dv = dyc * bg
            duc = (_pvb(pv_ref, PV_CONV_W + 2) * dv + _pvb(pv_ref, PV_CONV_W + 1) * _shift_up(dv, dv_next, 1, row)
                   + _pvb(pv_ref, PV_CONV_W) * _shift_up(dv, dv_next, 2, row))
            acc_add(PV_CONV_W + 2, dv * uc)
            acc_add(PV_CONV_W + 1, dv * uc1)
            acc_add(PV_CONV_W, dv * uc2)
            return (a, dv, gg), (dyc * v, duc * xc_ref[sl, :], duc * cg_ref[sl, :], p3, p5, dra, dia, u)

        a_next, dv_next, g_next = _for_groups(
            ng, main_group, (an_s[...], dvn_s[...], gbuf_s[...]), descending=True,
            stores=(_rows_of(dp_ref, 0), _rows_of(dp_ref, 1), _rows_of(dp_ref, 2), _rows_of(dp_ref, 3), _rows_of(dp_ref, 5),
                    _rows_of(dgb_s, cols=slice(0, LW)), _rows_of(dgb_s, cols=slice(LW, 2 * LW)), _rows_of(ub_s)))
        an_s[...] = a_next
        dvn_s[...] = dv_next
        gbuf_s[...] = g_next

        dgb = dgb_s[...]
        du_s[...] += _mm_nt(dgb, wai_ref[0])
        gw_ref[0] += _mm_tn(ub_s[...], dgb)

        def lru_conv_group(gi, du_next):
            g = ng - 1 - gi
            r0 = pl.multiple_of(g * SUBLANES, SUBLANES)
            sl = pl.ds(r0, SUBLANES)
            du = du_s[sl, :]
            xl = xle_s[pl.ds(r0 + SUBLANES, SUBLANES), :]
            xlp = xle_s[sl, :]
            acc_add(PV_LRU_B, du)
            acc_add(PV_LRU_W + 3, du * xl)
            acc_add(PV_LRU_W + 2, du * _shift_down(xl, xlp, 1, row))
            acc_add(PV_LRU_W + 1, du * _shift_down(xl, xlp, 2, row))
            acc_add(PV_LRU_W, du * _shift_down(xl, xlp, 3, row))
            dxl = (((_pvb(pv_ref, PV_LRU_W + 3) * du + _pvb(pv_ref, PV_LRU_W + 2) * _shift_up(du, du_next, 1, row))
                    + _pvb(pv_ref, PV_LRU_W + 1) * _shift_up(du, du_next, 2, row))
                   + _pvb(pv_ref, PV_LRU_W) * _shift_up(du, du_next, 3, row))
            return du, (dxl,)

        dun_s[...] = _for_groups(ng, lru_conv_group, dun_s[...], descending=True, stores=(_rows_of(dp_ref, 4),))

        @pl.when(first_block)
        def _():
            sv_ref[...] = jnp.zeros_like(sv_ref)
            for k in range(N_ACC):
                tot = jnp.sum(acc_s[k], axis=0, keepdims=True)
                if k == PV_LAM:
                    tot = (RG_LRU_C * tot) / (1.0 + jnp.exp(pv_ref[PV_LAM:PV_LAM + 1, :]))
                sv_ref[k:k + 1, :] = tot

    def part(p):
        return pl.BlockSpec((None, tb, LW), lambda c, i: (2 * p + c // STRIPS_PER_CHUNK, nt - 1 - i, c % STRIPS_PER_CHUNK))

    def halo(p):
        return pl.BlockSpec((None, SUBLANES, LW), lambda c, i: (2 * p + c // STRIPS_PER_CHUNK,
                                                                jnp.maximum((nt - 1 - i) * gpb - 1, 0), c % STRIPS_PER_CHUNK))

    strip = pl.BlockSpec((tb, LW), lambda c, i: (nt - 1 - i, c))
    big = pltpu.VMEM((tb, LW), F32)
    big_e = pltpu.VMEM((tb + SUBLANES, LW), F32)
    small = pltpu.VMEM((SUBLANES, LW), F32)
    outs = pl.pallas_call(
        body, grid=(NS, nt),
        in_specs=[part(p) for p in range(N_PARTS)] + [
            strip, strip, strip, strip, strip, pl.BlockSpec((tb, LW), lambda c, i: (nt - 1 - i, NS + c)),
            halo(1), halo(2), halo(4),
            pl.BlockSpec((SUBLANES, LW), lambda c, i: (jnp.maximum((nt - 1 - i) * gpb - 1, 0), c)),
            pl.BlockSpec((PV_ROWS, LW), lambda c, i: (0, c)),
            pl.BlockSpec((1, LW, 2 * LW), lambda c, i: (c, 0, 0)),
            pl.BlockSpec(memory_space=pl.ANY)],
        out_specs=(pl.BlockSpec((N_PARTS, tb, LW), lambda c, i: (0, nt - 1 - i, c)),
                   pl.BlockSpec((1, LW, 2 * LW), lambda c, i: (c, 0, 0)),
                   pl.BlockSpec((PV_ROWS, LW), lambda c, i: (0, c)),
                   pl.BlockSpec(memory_space=pl.ANY)),
        out_shape=(jax.ShapeDtypeStruct((N_PARTS, t, D_PART), MXU_DTYPE),
                   jax.ShapeDtypeStruct((NS, LW, 2 * LW), F32), jax.ShapeDtypeStruct((PV_ROWS, D_PART), F32),
                   _chip_blocks_shape(sb_out, 1)),
        scratch_shapes=[small, pltpu.VMEM((tb, LW), MXU_DTYPE), big_e, big_e, big_e,
                        pltpu.VMEM((tb, 2 * LW), MXU_DTYPE), big, small,
                        pltpu.VMEM((N_ACC, SUBLANES, LW), F32), small, small, small,
                        pltpu.SemaphoreType.DMA((3,)), pltpu.SemaphoreType.DMA((3,))],
        compiler_params=_cp(ARB, ARB), name="mixer_backward",
    )(proj, proj, proj, proj, proj, proj, h, u, r, ig, dy, dy, proj, proj, proj, h, pvec, wai, sb_out)
    return outs


def _adamw(w, g, m, v):
    m = ADAM_B1 * m + (1.0 - ADAM_B1) * g
    v = ADAM_B2 * v + (1.0 - ADAM_B2) * (g * g)
    m_hat = m / (1.0 - ADAM_B1 ** ADAM_STEP)
    v_hat = v / (1.0 - ADAM_B2 ** ADAM_STEP)
    delta = -ADAM_LR * (m_hat / (jnp.sqrt(v_hat) + ADAM_EPS) + ADAM_WD * w)
    return delta, m, v


def _adam_small(ws, ms, vs, gs):
    n = len(ws)

    def body(*refs):
        w_r, m_r, v_r, g_r = refs[0:n], refs[n:2 * n], refs[2 * n:3 * n], refs[3 * n:4 * n]
        d_o, m_o, v_o = refs[4 * n:5 * n], refs[5 * n:6 * n], refs[6 * n:7 * n]
        for j in range(n):
            d_o[j][...], m_o[j][...], v_o[j][...] = _adamw(w_r[j][...], g_r[j][...], m_r[j][...], v_r[j][...])

    vm = pl.BlockSpec(memory_space=pltpu.VMEM)
    shapes = tuple(jax.ShapeDtypeStruct(w.shape, F32) for w in ws)
    outs = pl.pallas_call(
        body, in_specs=[vm] * (4 * n), out_specs=(vm,) * (3 * n), out_shape=shapes * 3,
        compiler_params=_cp(), name="adam_small",
    )(*ws, *ms, *vs, *gs)
    return outs[0:n], outs[n:2 * n], outs[2 * n:3 * n]


def _block_diag_strips(w, lw):
    heads = lw // LRU_HEAD
    w4 = w.reshape(D_PART // lw, heads, LRU_HEAD, LRU_HEAD)
    rows = [jnp.pad(w4[:, hh], ((0, 0), (0, 0), (LRU_HEAD * hh, lw - LRU_HEAD * (hh + 1)))) for hh in range(heads)]
    return jnp.concatenate(rows, axis=1)


def _gate_matrices(w_a, w_i, lw):
    return jnp.concatenate([_block_diag_strips(w_a, lw), _block_diag_strips(w_i, lw)], axis=2).astype(MXU_DTYPE)


def _strip_diag_blocks(g):
    g5 = g.reshape(NS, HEADS_PER_STRIP, LRU_HEAD, HEADS_PER_STRIP, LRU_HEAD)
    return jnp.stack([g5[:, hh, :, hh, :] for hh in range(HEADS_PER_STRIP)], axis=1).reshape(NS * HEADS_PER_STRIP, LRU_HEAD, LRU_HEAD)


def kernel(x, ln_g, w_in, conv_w, lru_conv_w, lru_conv_b, w_a, b_a, w_i, b_i, lam, conv_out_g, lru_out_g, w_out, final_g, loss_target, m_ln_g, m_w_in, m_conv_w, m_lru_conv_w, m_lru_conv_b, m_w_a, m_b_a, m_w_i, m_b_i, m_lam, m_conv_out_g, m_lru_out_g, m_w_out, m_final_g, v_ln_g, v_w_in, v_conv_w, v_lru_conv_w, v_lru_conv_b, v_w_a, v_b_a, v_w_i, v_b_i, v_lam, v_conv_out_g, v_lru_out_g, v_w_out, v_final_g):
    xi, yi, ci = lax.axis_index("x"), lax.axis_index("y"), lax.axis_index("c")
    k = 2 * xi + yi
    t = x.shape[1]
    x2 = x.reshape(t, D_MODEL)
    tgt2 = loss_target.reshape(t, D_MODEL)
    row = lambda a: a.reshape(1, -1)

    small = jnp.concatenate([conv_w, lru_conv_w, jnp.zeros((1, conv_w.shape[1]), F32)], axis=0)
    proj, xn, w12, sm4 = _gather_in_projection(x2, row(ln_g), w_in, small)
    convs = jnp.transpose(sm4, (1, 0, 2)).reshape(SUBLANES, D_PART)
    pvec = jnp.concatenate(
        [convs[0:7], row(lru_conv_b), row(b_a), row(b_i), row(lam), row(conv_out_g), row(lru_out_g),
         jnp.zeros((PV_ROWS - N_ACC, D_PART), F32)], axis=0)
    wai = _gate_matrices(w_a, w_i, LW)

    c_arr = jnp.reshape(ci, (1,)).astype(jnp.int32)
    yc, yl, h, u, r, ig, wo4 = _mixer_forward(proj, pvec, _gate_matrices(w_a, w_i, FWD_LW), w_out)
    wo = wo4.reshape(2 * D_PART, D_MODEL)
    do, dob, dy, st_out = _out_projection_loss(yc, yl, x2, tgt2, wo, row(final_g))
    go4, go4b = _w_out_grad(yc, yl, dob)
    s_out, sb_out = _add_sibling_halves(go4, go4b, c_arr, "add_sibling_halves_out")
    dproj, g_wai, svec, r2o = _mixer_backward(proj, h, u, r, ig, dy, pvec, wai, sb_out)
    gwa = _strip_diag_blocks(g_wai[:, :, 0:LW]).reshape(LRU_HEAD, D_PART)
    gwi = _strip_diag_blocks(g_wai[:, :, LW:2 * LW]).reshape(LRU_HEAD, D_PART)
    g12, g12b, red = _w_in_grad(xn, dproj, jnp.concatenate([svec, st_out, gwa, gwi], axis=0))
    s_in, sb_in = _add_sibling_halves(g12, g12b, c_arr, "add_sibling_halves_in")
    grad_x, st_in, r2i = _input_grad(dproj, w12, x2, do, row(ln_g), sb_in)
    (g_w_in, d_w_in, nm_w_in, nv_w_in, g_w_out, d_w_out, nm_w_out, nv_w_out, red_ln) = _finish_gradients_adamw(
        s_in, r2i, s_out, r2o, st_in, (w_in, m_w_in, v_w_in), (w_out, m_w_out, v_w_out))
    r_out = PV_ROWS
    r_wa = PV_ROWS + SUBLANES
    r_wi = r_wa + LRU_HEAD
    loss = red[r_out + 1, 0]


    ncol = conv_w.shape[1]
    conv_cols = lax.dynamic_slice(red, (0, k * ncol), (SUBLANES, ncol))
    g_small = {
        "ln_g": red_ln[0], "conv_w": conv_cols[0:3], "lru_conv_w": conv_cols[3:7], "lru_conv_b": red[PV_LRU_B],
        "w_a": red[r_wa:r_wa + LRU_HEAD].reshape(w_a.shape), "b_a": red[PV_BA],
        "w_i": red[r_wi:r_wi + LRU_HEAD].reshape(w_i.shape), "b_i": red[PV_BI], "lam": red[PV_LAM],
        "conv_out_g": red[PV_CG], "lru_out_g": red[PV_LG], "final_g": red[r_out],
    }
    w_small = {"ln_g": ln_g, "conv_w": conv_w, "lru_conv_w": lru_conv_w, "lru_conv_b": lru_conv_b, "w_a": w_a, "b_a": b_a,
               "w_i": w_i, "b_i": b_i, "lam": lam, "conv_out_g": conv_out_g, "lru_out_g": lru_out_g, "final_g": final_g}
    m_small = {"ln_g": m_ln_g, "conv_w": m_conv_w, "lru_conv_w": m_lru_conv_w, "lru_conv_b": m_lru_conv_b, "w_a": m_w_a,
               "b_a": m_b_a, "w_i": m_w_i, "b_i": m_b_i, "lam": m_lam, "conv_out_g": m_conv_out_g,
               "lru_out_g": m_lru_out_g, "final_g": m_final_g}
    v_small = {"ln_g": v_ln_g, "conv_w": v_conv_w, "lru_conv_w": v_lru_conv_w, "lru_conv_b": v_lru_conv_b, "w_a": v_w_a,
               "b_a": v_b_a, "w_i": v_w_i, "b_i": v_b_i, "lam": v_lam, "conv_out_g": v_conv_out_g,
               "lru_out_g": v_lru_out_g, "final_g": v_final_g}
    names = list(w_small)
    as2d = lambda a: a.reshape(1, -1) if a.ndim == 1 else a
    d_s, m_s, v_s = _adam_small([as2d(w_small[n]) for n in names], [as2d(m_small[n]) for n in names],
                                [as2d(v_small[n]) for n in names], [as2d(g_small[n]) for n in names])
    back = lambda n, a: a.reshape(w_small[n].shape)
    grads = {n: g_small[n] for n in names}
    deltas = {n: back(n, a) for n, a in zip(names, d_s)}
    new_m = {n: back(n, a) for n, a in zip(names, m_s)}
    new_v = {n: back(n, a) for n, a in zip(names, v_s)}
    grads["w_in"], deltas["w_in"], new_m["w_in"], new_v["w_in"] = g_w_in, d_w_in, nm_w_in, nv_w_in
    grads["w_out"], deltas["w_out"], new_m["w_out"], new_v["w_out"] = g_w_out, d_w_out, nm_w_out, nv_w_out

    order = ["ln_g", "w_in", "conv_w", "lru_conv_w", "lru_conv_b", "w_a", "b_a", "w_i", "b_i", "lam", "conv_out_g",
             "lru_out_g", "w_out", "final_g"]
    return (loss, grad_x.reshape(x.shape), *[grads[n] for n in order], *[deltas[n] for n in order],
            *[new_m[n] for n in order], *[new_v[n] for n in order])
```

```python
import functools

import jax
import jax.numpy as jnp
from jax import lax
from jax.experimental import pallas as pl
from jax.experimental.pallas import tpu as pltpu

F32 = jnp.float32
MXU_DTYPE = jnp.bfloat16

D_MODEL = 1024
D_PART = 1024
N_PARTS = 6
CHUNK = 512
CHUNKS_PER_BLOCK = 3
N_CHUNKS = 12
N_CHIPS = 4
SUBLANES = 8
LANES = 128
LW = 256
FWD_LW = 512
UNROLL = 8
NS = D_PART // LW
STRIPS_PER_CHUNK = CHUNK // LW
CONV_HEAD = 128
LRU_HEAD = 64
HEADS_PER_STRIP = LW // LRU_HEAD
RMS_EPS = 1e-6
RG_LRU_C = 8.0
ADAM_LR = 0.001
ADAM_B1 = 0.9
ADAM_B2 = 0.999
ADAM_EPS = 1e-08
ADAM_WD = 0.01
ADAM_STEP = 10

PV_CONV_W = 0
PV_LRU_W = 3
PV_LRU_B = 7
PV_BA = 8
PV_BI = 9
PV_LAM = 10
PV_CG = 11
PV_LG = 12
PV_ROWS = 16
N_ACC = 13

SLAB = 128
MESH = pl.DeviceIdType.MESH
VMEM_LIMIT = 56 * 1024 * 1024
ARB = "arbitrary"


def _cp(*sem, **kw):
    return pltpu.CompilerParams(dimension_semantics=sem or None, vmem_limit_bytes=VMEM_LIMIT, **kw)


def _mm(a, b):
    return jnp.dot(a, b, preferred_element_type=F32)


def _mm_nt(a, b):
    return lax.dot_general(a, b, (((1,), (1,)), ((), ())), preferred_element_type=F32)


def _mm_tn(a, b):
    return lax.dot_general(a, b, (((0,), (0,)), ((), ())), preferred_element_type=F32)


def _sigmoid(x):
    return 0.5 * jnp.tanh(0.5 * x) + 0.5


def _log_sigmoid(x):
    z = jnp.exp(-jnp.abs(x))
    u = 1.0 + z
    log1p = jnp.where(u == 1.0, z, jnp.log(u) * z / (u - 1.0))
    return jnp.minimum(x, 0.0) - log1p


def _head_mean(z, head):
    out = []
    for k in range(z.shape[1] // LANES):
        zk = z[:, LANES * k:LANES * (k + 1)]
        if head == LANES:
            m = jnp.sum(zk, axis=-1, keepdims=True) * (1.0 / head)
            out.append(jnp.broadcast_to(m, zk.shape))
        else:
            lo = lax.broadcasted_iota(jnp.int32, zk.shape, 1) < head
            s_lo = jnp.sum(jnp.where(lo, zk, 0.0), axis=-1, keepdims=True)
            s_hi = jnp.sum(jnp.where(lo, 0.0, zk), axis=-1, keepdims=True)
            out.append(jnp.where(lo, s_lo, s_hi) * (1.0 / head))
    return jnp.concatenate(out, axis=1)


def _shift_down(cur, prev, d, row):
    return pltpu.roll(jnp.where(row < SUBLANES - d, cur, prev), d, 0)


def _shift_up(cur, nxt, d, row):
    return pltpu.roll(jnp.where(row >= d, cur, nxt), SUBLANES - d, 0)


def _scan8_fwd(a, b, row):
    A, B = a, b
    for d in (1, 2, 4):
        m = row >= d
        a_s = jnp.where(m, pltpu.roll(A, d, 0), 1.0)
        b_s = jnp.where(m, pltpu.roll(B, d, 0), 0.0)
        B = A * b_s + B
        A = A * a_s
    return A, B


def _scan8_rev(a, b, row):
    A, B = a, b
    for d in (1, 2, 4):
        m = row < SUBLANES - d
        a_s = jnp.where(m, pltpu.roll(A, SUBLANES - d, 0), 1.0)
        b_s = jnp.where(m, pltpu.roll(B, SUBLANES - d, 0), 0.0)
        B = A * b_s + B
        A = A * a_s
    return A, B


def _decay(r, ls8):
    la = r * ls8
    a = jnp.exp(la)
    e2 = a * a
    em = -jnp.tanh(la) * (1.0 + e2)
    inv_mult = lax.rsqrt(em)
    return a, e2, em * inv_mult, inv_mult


def _mesh_pos():
    x, y, c = lax.axis_index("x"), lax.axis_index("y"), lax.axis_index("c")
    chips = [(1 - x, y), (x, 1 - y), (1 - x, 1 - y)]
    return x, y, c, chips


def _gather_in_projection(x, ln_g, w_in, small):
    t = x.shape[0]
    rb_x = 512
    rb_mm = 2048
    n_mm = t // rb_mm
    half = w_in.shape[0] // 2

    def body(x_hbm, g_ref, wi_ref, sm_ref, proj_hbm, xn_ref, w12_ref, sm4_ref,
             xbuf, obuf, x_sems, o_sems, send_sems, recv_sems):
        x_, y_, c, chips = _mesh_pos()
        k = 2 * x_ + y_
        sib = (x_, y_, 1 - c)
        sm4_ref[k] = sm_ref[...]

        def remote(ref, sem, to):
            return pltpu.make_async_remote_copy(src_ref=ref, dst_ref=ref, send_sem=send_sems.at[sem],
                                                recv_sem=recv_sems.at[sem], device_id=to, device_id_type=MESH)

        def chunk_of(chip, s):
            return CHUNKS_PER_BLOCK * (2 * chip[0] + chip[1]) + s

        def piece(q, core, first=0, rows=half):
            return w12_ref.at[q, pl.ds(pl.multiple_of(half * core + first, SUBLANES * 2), rows), :]

        nbr_x, nbr_y, diagonal = chips
        quarter = half // 2
        DIAG = [(0, 0, half, 0), (1, 0, quarter, 0), (1, quarter, quarter, 1), (2, 0, half, 1)]
        ici = lambda m, s: 2 * s + m
        dgn = lambda j: 6 + j
        to_sib = 10
        sml = lambda m: 20 + m

        sends = []
        for s in range(CHUNKS_PER_BLOCK):
            w12_ref[chunk_of((x_, y_), s)] = wi_ref[:, CHUNK * s:CHUNK * (s + 1)].astype(MXU_DTYPE)
            for m, chip in enumerate((nbr_x, nbr_y)):
                sends.append(remote(piece(chunk_of((x_, y_), s), c), ici(m, s), (*chip, c)))
                sends[-1].start()
        for m, chip in enumerate(chips):
            sends.append(remote(sm4_ref.at[k], sml(m), (*chip, c)))
            sends[-1].start()

        def x_copy(rb, slot):
            return pltpu.make_async_copy(x_hbm.at[pl.ds(rb * rb_x, rb_x), :], xbuf.at[slot], x_sems.at[slot])

        x_copy(0, 0).start()
        for rb in range(t // rb_x):
            slot = rb % 2
            x_copy(rb, slot).wait()
            if rb + 1 < t // rb_x:
                x_copy(rb + 1, 1 - slot).start()

            def norm_slab(sl, carry, rb=rb, slot=slot):
                xf = xbuf[slot, pl.ds(pl.multiple_of(sl * SLAB, SLAB), SLAB), :]
                r = lax.rsqrt(jnp.mean(xf * xf, axis=-1, keepdims=True) + RMS_EPS)
                xn_ref[pl.ds(pl.multiple_of(rb * rb_x + sl * SLAB, SLAB), SLAB), :] = ((xf * r) * g_ref[...]).astype(MXU_DTYPE)
                return carry

            lax.fori_loop(0, rb_x // SLAB, norm_slab, 0)

        def out_copy(q, i):
            return pltpu.make_async_copy(obuf.at[i], proj_hbm.at[q, pl.ds(pl.multiple_of(i * rb_mm, rb_mm), rb_mm), :],
                                         o_sems.at[i])

        def project(q, very_first):
            def row_block(i, carry):
                if not very_first:
                    out_copy(q, i).wait()
                obuf[i] = _mm(xn_ref[pl.ds(pl.multiple_of(i * rb_mm, rb_mm), rb_mm), :], w12_ref[q])
                out_copy(q, i).start()
                return carry

            lax.fori_loop(0, n_mm, row_block, 0)

        for s in range(CHUNKS_PER_BLOCK):
            project(chunk_of((x_, y_), s), very_first=(s == 0))

        steps = []
        for s in range(CHUNKS_PER_BLOCK):
            for m, chip in enumerate((nbr_x, nbr_y)):
                onward = [(first, rows, dgn(j), chips[via]) for j, (cs, first, rows, via) in enumerate(DIAG)
                          if cs == s and via == 1 - m]
                steps.append((chunk_of(chip, s), [(0, half, ici(m, s))], onward))
        for s in range(CHUNKS_PER_BLOCK):
            steps.append((chunk_of(diagonal, s), [(first, rows, dgn(j)) for j, (cs, first, rows, _) in enumerate(DIAG) if cs == s], []))

        def project_when_whole(step):
            q, pieces, _ = step
            for first, rows, sem in pieces:
                remote(piece(q, 1 - c, first, rows), to_sib + sem, sib).wait_recv()
            project(q, very_first=False)

        passed = []
        for j, (q, pieces, onward) in enumerate(steps):
            for first, rows, sem in pieces:
                remote(piece(q, c, first, rows), sem, sib).wait_recv()
            for first, rows, sem, chip in onward:
                passed.append(remote(piece(q, c, first, rows), sem, (*chip, c)))
                passed[-1].start()
            for first, rows, sem in pieces:
                passed.append(remote(piece(q, c, first, rows), to_sib + sem, sib))
                passed[-1].start()
            if j > 0:
                project_when_whole(steps[j - 1])
        project_when_whole(steps[-1])

        for m, chip in enumerate(chips):
            remote(sm4_ref.at[2 * chip[0] + chip[1]], sml(m), sib).wait_recv()
        for cp in sends + passed:
            cp.wait_send()
        for i in range(n_mm):
            out_copy(0, i).wait()

    vm = pl.BlockSpec(memory_space=pltpu.VMEM)
    hbm = pl.BlockSpec(memory_space=pl.ANY)
    n_sems = 23
    return pl.pallas_call(
        body,
        out_shape=(jax.ShapeDtypeStruct((N_CHUNKS, t, CHUNK), F32), jax.ShapeDtypeStruct((t, D_MODEL), MXU_DTYPE),
                   jax.ShapeDtypeStruct((N_CHUNKS, w_in.shape[0], CHUNK), MXU_DTYPE),
                   jax.ShapeDtypeStruct((N_CHIPS,) + small.shape, F32)),
        in_specs=[hbm, vm, vm, vm], out_specs=(hbm, vm, vm, vm),
        scratch_shapes=[pltpu.VMEM((2, rb_x, D_MODEL), F32), pltpu.VMEM((n_mm, rb_mm, CHUNK), F32),
                        pltpu.SemaphoreType.DMA((2,)), pltpu.SemaphoreType.DMA((n_mm,)),
                        pltpu.SemaphoreType.DMA((n_sems,)), pltpu.SemaphoreType.DMA((n_sems,))],
        compiler_params=_cp(), name="gather_in_projection",
    )(x, ln_g, w_in, small)


def _allreduce_behind(step, when, in_ref, acc_s, rbufs, out_ref, send_sems, recv_sems):
    x, y, c, _ = _mesh_pos()
    peers = [(x, y, 1 - c), (1 - x, y, c), (x, 1 - y, c)]

    def exchange(ph):
        return pltpu.make_async_remote_copy(src_ref=acc_s, dst_ref=rbufs[ph], send_sem=send_sems.at[ph],
                                            recv_sem=recv_sems.at[ph], device_id=peers[ph], device_id_type=MESH)

    @pl.when(step == when[0])
    def _():
        acc_s[...] = in_ref[...]
        exchange(0).start()

    for ph in (1, 2):
        @pl.when(step == when[ph])
        def _(ph=ph):
            exchange(ph - 1).wait()
            acc_s[...] = acc_s[...] + rbufs[ph - 1][...]
            exchange(ph).start()

    @pl.when(step == when[3])
    def _():
        exchange(2).wait()
        out_ref[...] = acc_s[...] + rbufs[2][...]


def _add_sibling_halves(g, gb, c_arr, name):
    n, rows, cols = g.shape
    half = rows // 2
    per = 2
    steps = n // per

    def body(c_ref, g_ref, gb_hbm, o_ref, ob_ref, rbuf, send_sems, recv_sems):
        q = pl.program_id(0)
        x, y, c, _ = _mesh_pos()
        theirs = pl.ds(pl.multiple_of(half * (1 - c), half), half)

        def copy(j):
            blocks = pl.ds(j * per, per)
            return pltpu.make_async_remote_copy(src_ref=gb_hbm.at[blocks, theirs, :], dst_ref=rbuf.at[blocks], send_sem=send_sems.at[j],
                                                recv_sem=recv_sems.at[j], device_id=(x, y, 1 - c), device_id_type=MESH)

        @pl.when(q == 0)
        def _():
            for j in range(steps):
                copy(j).start()

        copy(q).wait_recv()
        s = g_ref[...] + rbuf[pl.ds(q * per, per)].astype(F32)
        o_ref[...] = s
        ob_ref[...] = s.astype(jnp.bfloat16)

        @pl.when(q == steps - 1)
        def _():
            for j in range(steps):
                copy(j).wait_send()

    blk = pl.BlockSpec((per, half, cols), lambda q, c_ref: (q, 0, 0))
    return pl.pallas_call(
        body, out_shape=(jax.ShapeDtypeStruct((n, half, cols), F32), jax.ShapeDtypeStruct((n, half, cols), jnp.bfloat16)),
        grid_spec=pltpu.PrefetchScalarGridSpec(
            num_scalar_prefetch=1, grid=(steps,),
            in_specs=[pl.BlockSpec((per, half, cols), lambda q, c_ref: (q, c_ref[0], 0)), pl.BlockSpec(memory_space=pl.ANY)],
            out_specs=(blk, blk),
            scratch_shapes=[pltpu.VMEM((n, half, cols), jnp.bfloat16), pltpu.SemaphoreType.DMA((steps,)),
                            pltpu.SemaphoreType.DMA((steps,))]),
        compiler_params=_cp(ARB), name=name,
    )(c_arr, g, gb)


def _chip_block_copies(s_ref, r_ref, n_sub, send_sems, recv_sems):
    x, y, c, chips = _mesh_pos()
    cps = []
    for m, chip in enumerate(chips):
        kk = 2 * chip[0] + chip[1]
        cps.append(pltpu.make_async_remote_copy(
            src_ref=s_ref.at[pl.ds(n_sub * kk, n_sub)], dst_ref=r_ref.at[m],
            send_sem=send_sems.at[m], recv_sem=recv_sems.at[m], device_id=(*chip, c), device_id_type=MESH))
    return cps


def _gather_w_out(step, n_steps, wo_ref, wob_s, wo4_ref, local_sem, send_sems, recv_sems):
    x, y, c, chips = _mesh_pos()
    sib = (x, y, 1 - c)
    half = wo_ref.shape[0] // 2

    def rows(core):
        return pl.ds(pl.multiple_of(half * core, half), half)

    def block_half(chip, core):
        return wo4_ref.at[2 * chip[0] + chip[1], rows(core), :]

    def remote(src, dst, sem, to):
        return pltpu.make_async_remote_copy(src_ref=src, dst_ref=dst, send_sem=send_sems.at[sem], recv_sem=recv_sems.at[sem],
                                            device_id=to, device_id_type=MESH)

    local = pltpu.make_async_copy(wob_s, wo4_ref.at[2 * x + y], local_sem)
    ici = [remote(wob_s.at[rows(c), :], block_half((x, y), c), m, (*chip, c)) for m, chip in enumerate(chips)]
    fwd = [remote(block_half(chip, c), block_half(chip, c), 3 + m, sib) for m, chip in enumerate(chips)]

    @pl.when(step == 0)
    def _():
        wob_s[...] = wo_ref[...].astype(MXU_DTYPE)
        local.start()
        for cp in ici:
            cp.start()

    @pl.when(step == n_steps // 2)
    def _():
        for m, chip in enumerate(chips):
            remote(block_half(chip, c), block_half(chip, c), m, sib).wait_recv()
            fwd[m].start()

    @pl.when(step == n_steps - 1)
    def _():
        for m, chip in enumerate(chips):
            remote(block_half(chip, 1 - c), block_half(chip, 1 - c), 3 + m, sib).wait_recv()
        for cp in ici + fwd:
            cp.wait_send()
        local.wait()


def _chip_blocks_shape(s, n_sub):
    return jax.ShapeDtypeStruct((3, n_sub) + s.shape[1:], s.dtype)


def _finish_gradients(s_in, r_in, s_out, r_out, v):
    n_dev = 8
    n_in, n_out = r_in.shape[1], r_out.shape[1]

    def body(si_hbm, ri_hbm, so_hbm, ro_hbm, v_ref, fi_hbm, fo_hbm, tot_ref,
             a_in, b_in, a_out, b_out, slots, load_sems, store_sems, send_sems, recv_sems):
        x, y, c, _ = _mesh_pos()
        k = 2 * x + y
        sib = (x, y, 1 - c)
        me = 4 * x + 2 * y + c
        loads = [pltpu.make_async_copy(si_hbm.at[pl.ds(n_in * k, n_in)], a_in, load_sems.at[0]),
                 pltpu.make_async_copy(ri_hbm, b_in, load_sems.at[1]),
                 pltpu.make_async_copy(so_hbm.at[pl.ds(n_out * k, n_out)], a_out, load_sems.at[2]),
                 pltpu.make_async_copy(ro_hbm, b_out, load_sems.at[3])]
        for cp in loads:
            cp.start()
        slots[me] = v_ref[...]

        def remote(src, dst, sem, to):
            return pltpu.make_async_remote_copy(src_ref=src, dst_ref=dst, send_sem=send_sems.at[sem],
                                                recv_sem=recv_sems.at[sem], device_id=to, device_id_type=MESH)

        small = []
        for d in range(1, n_dev):
            peer = (1 - x if d & 4 else x, 1 - y if d & 2 else y, 1 - c if d & 1 else c)
            small.append(remote(slots.at[me], slots.at[me], d - 1, peer))
            small[-1].start()
        for cp in loads:
            cp.wait()
        big = []
        for j, (a, b, f_hbm) in enumerate(((a_in, b_in, fi_hbm), (a_out, b_out, fo_hbm))):
            a[...] = ((a[...] + b[0].astype(F32)) + b[1].astype(F32)) + b[2].astype(F32)
            half = a.shape[1]
            mine = f_hbm.at[:, pl.ds(pl.multiple_of(half * c, half), half), :]
            big.append(pltpu.make_async_copy(a, mine, store_sems.at[j]))
            big.append(remote(a, mine, n_dev - 1 + j, sib))
        for cp in big:
            cp.start()
        for cp in small + big:
            cp.wait()
        total = slots[0]
        for dev in range(1, n_dev):
            total = total + slots[dev]
        tot_ref[...] = total

    hbm = pl.BlockSpec(memory_space=pl.ANY)
    vm = pl.BlockSpec(memory_space=pltpu.VMEM)
    full = lambda s, n: (n, 2 * s.shape[1], s.shape[2])
    return pl.pallas_call(
        body,
        out_shape=(jax.ShapeDtypeStruct(full(s_in, n_in), F32), jax.ShapeDtypeStruct(full(s_out, n_out), F32),
                   jax.ShapeDtypeStruct(v.shape, F32)),
        in_specs=[hbm, hbm, hbm, hbm, vm], out_specs=(hbm, hbm, vm),
        scratch_shapes=[pltpu.VMEM((n_in,) + s_in.shape[1:], F32), pltpu.VMEM(r_in.shape, r_in.dtype),
                        pltpu.VMEM((n_out,) + s_out.shape[1:], F32), pltpu.VMEM(r_out.shape, r_out.dtype),
                        pltpu.VMEM((n_dev,) + v.shape, F32), pltpu.SemaphoreType.DMA((4,)), pltpu.SemaphoreType.DMA((2,)),
                        pltpu.SemaphoreType.DMA((n_dev + 1,)), pltpu.SemaphoreType.DMA((n_dev + 1,))],
        compiler_params=_cp(), name="finish_gradients",
    )(s_in, r_in, s_out, r_out, v)


def _out_projection_loss(yc, yl, x, target, wo, final_g):
    t = x.shape[0]
    tm = 512

    def body(yc_ref, yl_ref, x_ref, t_ref, wo_ref, fg_ref, do_ref, dob_ref, dy_ref, st_ref, y_wo):
        @pl.when(pl.program_id(0) == 0)
        def _():
            st_ref[...] = jnp.zeros_like(st_ref)

        y_wo[...] = _mm(yc_ref[...], wo_ref[0:D_PART, :]) + _mm(yl_ref[...], wo_ref[D_PART:2 * D_PART, :])

        def norm_loss_slab(s, carry):
            g_sum, loss_sum = carry
            rows = pl.ds(pl.multiple_of(s * SLAB, SLAB), SLAB)
            o = x_ref[rows, :] + y_wo[rows, :]
            r2 = lax.rsqrt(jnp.mean(o * o, axis=-1, keepdims=True) + RMS_EPS)
            ohat = o * r2
            fg = fg_ref[...]
            diff = ohat * fg - t_ref[rows, :]
            dout = diff * (1.0 / D_MODEL)
            gp = dout * fg
            do = r2 * (gp - ohat * jnp.mean(gp * ohat, axis=-1, keepdims=True))
            do_ref[rows, :] = do
            dob_ref[rows, :] = do.astype(MXU_DTYPE)
            loss = 0.5 * jnp.sum(jnp.sum(diff * diff, axis=-1, keepdims=True) * (1.0 / D_MODEL), axis=0, keepdims=True)
            return g_sum + jnp.sum(dout * ohat, axis=0, keepdims=True), loss_sum + loss

        g_sum, loss_sum = lax.fori_loop(0, tm // SLAB, norm_loss_slab,
                                        (jnp.zeros((1, D_MODEL), F32), jnp.zeros((1, 1), F32)))
        st_ref[0:1, :] += g_sum
        st_ref[1:2, :] += jnp.broadcast_to(loss_sum, (1, D_MODEL))
        dy_ref[...] = _mm_nt(dob_ref[...], wo_ref[...])

    row = lambda i: (i, 0)
    fix = lambda i: (0, 0)
    return pl.pallas_call(
        body, grid=(t // tm,),
        in_specs=[pl.BlockSpec((tm, D_PART), row), pl.BlockSpec((tm, D_PART), row),
                  pl.BlockSpec((tm, D_MODEL), row), pl.BlockSpec((tm, D_MODEL), row),
                  pl.BlockSpec((2 * D_PART, D_MODEL), fix), pl.BlockSpec((1, D_MODEL), fix)],
        out_specs=(pl.BlockSpec((tm, D_MODEL), row), pl.BlockSpec((tm, D_MODEL), row),
                   pl.BlockSpec((tm, 2 * D_PART), row), pl.BlockSpec((SUBLANES, D_MODEL), fix)),
        out_shape=(jax.ShapeDtypeStruct((t, D_MODEL), F32), jax.ShapeDtypeStruct((t, D_MODEL), MXU_DTYPE),
                   jax.ShapeDtypeStruct((t, 2 * D_PART), F32), jax.ShapeDtypeStruct((SUBLANES, D_MODEL), F32)),
        scratch_shapes=[pltpu.VMEM((tm, D_MODEL), F32)],
        compiler_params=_cp(ARB), name="out_projection_loss",
    )(yc, yl, x, target, wo, final_g)


def _input_grad(dproj, w12, x, do, ln_g, sb_in):
    t = x.shape[0]
    tm = 1024

    def body(dp_ref, w_ref, x_ref, do_ref, g_ref, s_ref, gx_ref, st_ref, r_ref, acc, send_sems, recv_sems):
        i, p = pl.program_id(0), pl.program_id(1)

        @pl.when((i == 0) & (p == 0))
        def _():
            st_ref[...] = jnp.zeros_like(st_ref)
            for cp in _chip_block_copies(s_ref, r_ref, CHUNKS_PER_BLOCK, send_sems, recv_sems):
                cp.start()

        @pl.when((i == t // tm - 1) & (p == N_PARTS - 1))
        def _():
            for cp in _chip_block_copies(s_ref, r_ref, CHUNKS_PER_BLOCK, send_sems, recv_sems):
                cp.wait()

        @pl.when(p == 0)
        def _():
            acc[...] = jnp.zeros_like(acc)

        acc[...] += _mm_nt(dp_ref[0], jnp.concatenate([w_ref[0], w_ref[1]], axis=1))

        @pl.when(p == N_PARTS - 1)
        def _():
            def norm_bwd_slab(s, g_sum):
                rows = pl.ds(pl.multiple_of(s * SLAB, SLAB), SLAB)
                xf = x_ref[rows, :]
                r = lax.rsqrt(jnp.mean(xf * xf, axis=-1, keepdims=True) + RMS_EPS)
                xhat = xf * r
                dxn = acc[rows, :]
                dxh = dxn * g_ref[...]
                gx_ref[rows, :] = do_ref[rows, :] + r * (dxh - xhat * jnp.mean(dxh * xhat, axis=-1, keepdims=True))
                return g_sum + jnp.sum(dxn * xhat, axis=0, keepdims=True)

            st_ref[0:1, :] += lax.fori_loop(0, tm // SLAB, norm_bwd_slab, jnp.zeros((1, D_MODEL), F32))

    row = lambda i, p: (i, 0)
    fix = lambda i, p: (0, 0)
    return pl.pallas_call(
        body, grid=(t // tm, N_PARTS),
        in_specs=[
            pl.BlockSpec((1, tm, D_PART), lambda i, p: (p, i, 0)),
            pl.BlockSpec((2, D_MODEL, CHUNK), lambda i, p: (p, 0, 0)),
            pl.BlockSpec((tm, D_MODEL), row), pl.BlockSpec((tm, D_MODEL), row), pl.BlockSpec((1, D_MODEL), fix),
            pl.BlockSpec(memory_space=pl.ANY)],
        out_specs=(pl.BlockSpec((tm, D_MODEL), row), pl.BlockSpec((SUBLANES, D_MODEL), fix),
                   pl.BlockSpec(memory_space=pl.ANY)),
        out_shape=(jax.ShapeDtypeStruct((t, D_MODEL), F32), jax.ShapeDtypeStruct((SUBLANES, D_MODEL), F32),
                   _chip_blocks_shape(sb_in, CHUNKS_PER_BLOCK)),
        scratch_shapes=[pltpu.VMEM((tm, D_MODEL), F32), pltpu.SemaphoreType.DMA((3,)), pltpu.SemaphoreType.DMA((3,))],
        compiler_params=_cp(ARB, ARB), name="input_grad",
    )(dproj, w12, x, do, ln_g, sb_in)


def _w_in_grad(xn, dproj, small):
    t = xn.shape[0]
    small_shape = pltpu.VMEM(small.shape, F32)

    def body(xn_ref, dp_ref, sm_ref, o_ref, ob_ref, red_ref, acc_s, r0, r1, r2, send_sems, recv_sems):
        _allreduce_behind(pl.program_id(0), (0, 1, 3, N_PARTS - 1), sm_ref, acc_s, (r0, r1, r2), red_ref, send_sems, recv_sems)
        g = _mm_tn(xn_ref[...], dp_ref[0])
        for s in range(2):
            o_ref[s] = g[:, CHUNK * s:CHUNK * (s + 1)]
            ob_ref[s] = g[:, CHUNK * s:CHUNK * (s + 1)].astype(jnp.bfloat16)

    whole = pl.BlockSpec(small.shape, lambda p: (0, 0))
    pair = pl.BlockSpec((2, D_MODEL, CHUNK), lambda p: (p, 0, 0))
    return pl.pallas_call(
        body, grid=(N_PARTS,),
        in_specs=[pl.BlockSpec((t, D_MODEL), lambda p: (0, 0)),
                  pl.BlockSpec((1, t, D_PART), lambda p: (p, 0, 0)), whole],
        out_specs=(pair, pair, whole),
        out_shape=(jax.ShapeDtypeStruct((N_CHUNKS, D_MODEL, CHUNK), F32),
                   jax.ShapeDtypeStruct((N_CHUNKS, D_MODEL, CHUNK), jnp.bfloat16), jax.ShapeDtypeStruct(small.shape, F32)),
        scratch_shapes=[small_shape] * 4 + [pltpu.SemaphoreType.DMA((3,)), pltpu.SemaphoreType.DMA((3,))],
        compiler_params=_cp(ARB), name="w_in_grad",
    )(xn, dproj, small)


def _w_out_grad(yc, yl, dob):
    t = yc.shape[0]
    tk = 2048

    def body(yc_ref, yl_ref, do_ref, o_ref, ob_ref):
        j, kk = pl.program_id(0), pl.program_id(1)

        def accumulate(y_ref):
            @pl.when(kk == 0)
            def _():
                o_ref[...] = jnp.zeros_like(o_ref)

            o_ref[...] += _mm_tn(y_ref[...], do_ref[...])

            @pl.when(kk == t // tk - 1)
            def _():
                ob_ref[...] = o_ref[...].astype(jnp.bfloat16)

        pl.when(j == 0)(functools.partial(accumulate, yc_ref))
        pl.when(j == 1)(functools.partial(accumulate, yl_ref))

    def rows_of(half):
        return lambda j, kk: (jnp.where(j == half, kk, 0), 0)

    half = pl.BlockSpec((D_PART, D_MODEL), lambda j, kk: (j, 0))
    out, out_b = pl.pallas_call(
        body, grid=(2, t // tk),
        in_specs=[pl.BlockSpec((tk, D_PART), rows_of(0)), pl.BlockSpec((tk, D_PART), rows_of(1)),
                  pl.BlockSpec((tk, D_MODEL), lambda j, kk: (kk, 0))],
        out_specs=(half, half),
        out_shape=(jax.ShapeDtypeStruct((2 * D_PART, D_MODEL), F32), jax.ShapeDtypeStruct((2 * D_PART, D_MODEL), jnp.bfloat16)),
        compiler_params=_cp(ARB, ARB), name="w_out_grad",
    )(yc, yl, dob)
    blocks = (N_CHIPS, 2 * D_PART // N_CHIPS, D_MODEL)
    return out.reshape(blocks), out_b.reshape(blocks)


def _for_groups(n, fn, init, unroll=UNROLL, stores=(), descending=False):
    assert unroll % 2 == 0 and n % unroll == 0

    def trip(j, carry):
        held = None
        for uu in range(unroll):
            idx = j * unroll + uu
            carry, values = fn(idx, carry)
            if uu % 2 == 0:
                held = values
                continue
            low_group = n - 1 - idx if descending else idx - 1
            rows = pl.ds(pl.multiple_of(low_group * SUBLANES, 2 * SUBLANES), 2 * SUBLANES)
            pairs = zip(values, held) if descending else zip(held, values)
            for store, (lo, hi) in zip(stores, pairs, strict=True):
                store(rows, jnp.concatenate([lo, hi], axis=0).astype(MXU_DTYPE))
        return carry

    return lax.fori_loop(0, n // unroll, trip, init)


def _rows_of(ref, *lead, cols=slice(None)):
    def store(rows, value):
        ref[(*lead, rows, cols)] = value

    return store


def _pvb(pv_ref, r):
    return jnp.broadcast_to(pv_ref[r:r + 1, :], (SUBLANES, pv_ref.shape[1]))


def _conv3(pv_ref, u, u1, u2):
    return (_pvb(pv_ref, PV_CONV_W) * u2 + _pvb(pv_ref, PV_CONV_W + 1) * u1) + _pvb(pv_ref, PV_CONV_W + 2) * u


def _conv4(pv_ref, v, v1, v2, v3):
    return ((((_pvb(pv_ref, PV_LRU_W) * v3 + _pvb(pv_ref, PV_LRU_W + 1) * v2) + _pvb(pv_ref, PV_LRU_W + 2) * v1)
             + _pvb(pv_ref, PV_LRU_W + 3) * v) + _pvb(pv_ref, PV_LRU_B))


def _mixer_forward(proj, pvec, wai, w_out):
    t = proj.shape[1]
    tb = 512
    ng = tb // SUBLANES
    nt = t // tb
    lw = FWD_LW
    ns = D_PART // lw
    per_chunk = CHUNK // lw

    def body(bg_ref, cg_ref, xc_ref, gc_ref, xl_ref, gl_ref, pv_ref, wai_ref, wo_ref,
             yc_ref, yl_ref, h_ref, u_s, r_ref, ig_ref, wo4_ref,
             ucp_s, xlp_s, ls_s, hbuf_s, ub_s, gate_s, wob_s, local_sem, send_sems, recv_sems):
        _gather_w_out(pl.program_id(0) * nt + pl.program_id(1), ns * nt, wo_ref, wob_s, wo4_ref, local_sem, send_sems, recv_sems)

        @pl.when(pl.program_id(1) == 0)
        def _():
            ucp_s[...] = jnp.zeros_like(ucp_s)
            xlp_s[...] = jnp.zeros_like(xlp_s)
            hbuf_s[...] = jnp.zeros_like(hbuf_s)

        row = lax.broadcasted_iota(jnp.int32, (SUBLANES, lw), 0)
        ls_s[...] = RG_LRU_C * _log_sigmoid(_pvb(pv_ref, PV_LAM))

        def conv_group(g, carry):
            ucp, xlp = carry
            sl = pl.ds(pl.multiple_of(g * SUBLANES, SUBLANES), SUBLANES)
            uc = cg_ref[sl, :] * xc_ref[sl, :]
            v = _conv3(pv_ref, uc, _shift_down(uc, ucp, 1, row), _shift_down(uc, ucp, 2, row))
            yc = bg_ref[sl, :] * v
            rr = lax.rsqrt(_head_mean(yc * yc, CONV_HEAD) + RMS_EPS)
            gc = gc_ref[sl, :]
            zc = ((yc * rr) * _pvb(pv_ref, PV_CG)) * (gc * _sigmoid(gc))
            xl = xl_ref[sl, :]
            u = _conv4(pv_ref, xl, _shift_down(xl, xlp, 1, row), _shift_down(xl, xlp, 2, row), _shift_down(xl, xlp, 3, row))
            u_s[sl, :] = u
            return (uc, xl), (zc, u)

        ucp, xlp = _for_groups(ng, conv_group, (ucp_s[...], xlp_s[...]), unroll=2 * UNROLL,
                               stores=(_rows_of(yc_ref), _rows_of(ub_s)))
        ucp_s[...] = ucp
        xlp_s[...] = xlp

        gate_s[...] = _mm(ub_s[...], wai_ref[0])

        def lru_group(g, h_before):
            sl = pl.ds(pl.multiple_of(g * SUBLANES, SUBLANES), SUBLANES)
            u = u_s[sl, :]
            r = _sigmoid(gate_s[sl, 0:lw] + _pvb(pv_ref, PV_BA))
            ig = _sigmoid(gate_s[sl, lw:2 * lw] + _pvb(pv_ref, PV_BI))
            r_ref[sl, :] = r
            ig_ref[sl, :] = ig
            a, _, mult, _ = _decay(r, ls_s[...])
            A, B = _scan8_fwd(a, mult * (ig * u), row)
            h = B + A * jnp.broadcast_to(h_before[SUBLANES - 1:SUBLANES, :], (SUBLANES, lw))
            h_ref[sl, :] = h
            rr = lax.rsqrt(_head_mean(h * h, LRU_HEAD) + RMS_EPS)
            gl = gl_ref[sl, :]
            return h, (((h * rr) * _pvb(pv_ref, PV_LG)) * (gl * _sigmoid(gl)),)

        hbuf_s[...] = _for_groups(ng, lru_group, hbuf_s[...], unroll=2 * UNROLL, stores=(_rows_of(yl_ref),))

    def part(p):
        return pl.BlockSpec((None, tb, lw), lambda c, i: (2 * p + c // per_chunk, i, c % per_chunk))

    strip = pl.BlockSpec((tb, lw), lambda c, i: (i, c))
    small = pltpu.VMEM((SUBLANES, lw), F32)
    return pl.pallas_call(
        body, grid=(ns, nt),
        in_specs=[part(p) for p in range(N_PARTS)] + [
            pl.BlockSpec((PV_ROWS, lw), lambda c, i: (0, c)),
            pl.BlockSpec((1, lw, 2 * lw), lambda c, i: (c, 0, 0)),
            pl.BlockSpec(w_out.shape, lambda c, i: (0, 0))],
        out_specs=(strip,) * 6 + (pl.BlockSpec(memory_space=pl.ANY),),
        out_shape=(jax.ShapeDtypeStruct((t, D_PART), MXU_DTYPE),) * 2 + (jax.ShapeDtypeStruct((t, D_PART), F32),) * 4 + (
            jax.ShapeDtypeStruct((N_CHIPS,) + w_out.shape, MXU_DTYPE),),
        scratch_shapes=[small, small, small, small, pltpu.VMEM((tb, lw), MXU_DTYPE),
                        pltpu.VMEM((tb, 2 * lw), F32), pltpu.VMEM(w_out.shape, MXU_DTYPE),
                        pltpu.SemaphoreType.DMA, pltpu.SemaphoreType.DMA((6,)), pltpu.SemaphoreType.DMA((6,))],
        compiler_params=_cp(ARB, ARB), name="mixer_forward",
    )(proj, proj, proj, proj, proj, proj, pvec, wai, w_out)


def _mixer_backward(proj, h, u, r, ig, dy, pvec, wai, sb_out):
    t = proj.shape[1]
    tb = 1024
    ng = tb // SUBLANES
    nt = t // tb
    gpb = tb // SUBLANES

    def body(bg_ref, cg_ref, xc_ref, gc_ref, xl_ref, gl_ref, h_ref, u_ref, r_ref, ig_ref, dyc_ref, dyl_ref,
             cgh_ref, xch_ref, xlh_ref, hh_ref, pv_ref, wai_ref, so_ref,
             dp_ref, gw_ref, sv_ref, ro_ref,
             ls_s, ub_s, uce_s, xle_s, he_s, dgb_s, du_s, gbuf_s,
             acc_s, an_s, dvn_s, dun_s, send_sems, recv_sems):
        i = pl.program_id(1)
        first_block = i == nt - 1

        @pl.when((pl.program_id(0) == 0) & (i == 0))
        def _():
            for cp in _chip_block_copies(so_ref, ro_ref, 1, send_sems, recv_sems):
                cp.start()

        @pl.when((pl.program_id(0) == NS - 1) & (i == nt - 1))
        def _():
            for cp in _chip_block_copies(so_ref, ro_ref, 1, send_sems, recv_sems):
                cp.wait()

        @pl.when(i == 0)
        def _():
            acc_s[...] = jnp.zeros_like(acc_s)
            gw_ref[...] = jnp.zeros_like(gw_ref)
            an_s[...] = jnp.zeros_like(an_s)
            dvn_s[...] = jnp.zeros_like(dvn_s)
            dun_s[...] = jnp.zeros_like(dun_s)
            gbuf_s[...] = jnp.zeros_like(gbuf_s)

        row = lax.broadcasted_iota(jnp.int32, (SUBLANES, LW), 0)
        ls_s[...] = RG_LRU_C * _log_sigmoid(_pvb(pv_ref, PV_LAM))
        keep = jnp.where(first_block, 0.0, 1.0)
        uce_s[0:SUBLANES, :] = (cgh_ref[...] * xch_ref[...]) * keep
        xle_s[0:SUBLANES, :] = xlh_ref[...] * keep
        he_s[0:SUBLANES, :] = hh_ref[...] * keep
        xle_s[SUBLANES:SUBLANES + tb, :] = xl_ref[...]
        he_s[SUBLANES:SUBLANES + tb, :] = h_ref[...]

        uce_s[SUBLANES:SUBLANES + tb, :] = cg_ref[...] * xc_ref[...]

        def acc_add(k, v):
            acc_s[k] += v

        def main_group(gi, carry):
            a_next, dv_next, g_next = carry
            g = ng - 1 - gi
            r0 = pl.multiple_of(g * SUBLANES, SUBLANES)
            sl = pl.ds(r0, SUBLANES)
            sl_e = pl.ds(r0 + SUBLANES, SUBLANES)
            lsb = ls_s[...]
            u = u_ref[sl, :]
            r = r_ref[sl, :]
            ig = ig_ref[sl, :]
            a, e2, mult, inv_mult = _decay(r, lsb)
            gl = gl_ref[sl, :]
            sg = _sigmoid(gl)
            s_l = gl * sg
            h8 = he_s[sl_e, :]
            hprev = _shift_down(h8, he_s[sl, :], 1, row)
            rr = lax.rsqrt(_head_mean(h8 * h8, LRU_HEAD) + RMS_EPS)
            n = h8 * rr
            dz = dyl_ref[sl, :]
            lg = _pvb(pv_ref, PV_LG)
            acc_add(PV_LG, (dz * n) * s_l)
            p5 = ((dz * n) * lg) * (sg + s_l * (1.0 - sg))
            dn = (dz * lg) * s_l
            dh = rr * (dn - n * _head_mean(dn * n, LRU_HEAD))
            A, B = _scan8_rev(_shift_up(a, a_next, 1, row), dh, row)
            gg = B + A * jnp.broadcast_to(g_next[0:1, :], (SUBLANES, LW))
            da = gg * hprev
            iu = ig * u
            diu = gg * mult
            dla = da * a - (gg * iu) * (e2 * inv_mult)
            acc_add(PV_LAM, dla * r)
            dra = (dla * lsb) * (r * (1.0 - r))
            dia = (diu * u) * (ig * (1.0 - ig))
            acc_add(PV_BA, dra)
            acc_add(PV_BI, dia)
            du_s[sl, :] = diu * ig
            bg = bg_ref[sl, :]
            gc = gc_ref[sl, :]
            uc = uce_s[sl_e, :]
            ucp = uce_s[sl, :]
            uc1 = _shift_down(uc, ucp, 1, row)
            uc2 = _shift_down(uc, ucp, 2, row)
            v = _conv3(pv_ref, uc, uc1, uc2)
            yc = bg * v
            rrc = lax.rsqrt(_head_mean(yc * yc, CONV_HEAD) + RMS_EPS)
            nc = yc * rrc
            sgc = _sigmoid(gc)
            s_c = gc * sgc
            dzc = dyc_ref[sl, :]
            cgain = _pvb(pv_ref, PV_CG)
            acc_add(PV_CG, (dzc * nc) * s_c)
            p3 = ((dzc * nc) * cgain) * (sgc + s_c * (1.0 - sgc))
            dnc = (dzc * cgain) * s_c
            dyc = rrc * (dnc - nc * _head_mean(dnc * nc, CONV_HEAD))
            dv = dyc * bg
            duc = (_pvb(pv_ref, PV_CONV_W + 2) * dv + _pvb(pv_ref, PV_CONV_W + 1) * _shift_up(dv, dv_next, 1, row)
                   + _pvb(pv_ref, PV_CONV_W) * _shift_up(dv, dv_next, 2, row))
            acc_add(PV_CONV_W + 2, dv * uc)
            acc_add(PV_CONV_W + 1, dv * uc1)
            acc_add(PV_CONV_W, dv * uc2)
            return (a, dv, gg), (dyc * v, duc * xc_ref[sl, :], duc * cg_ref[sl, :], p3, p5, dra, dia, u)

        a_next, dv_next, g_next = _for_groups(
            ng, main_group, (an_s[...], dvn_s[...], gbuf_s[...]), descending=True,
            stores=(_rows_of(dp_ref, 0), _rows_of(dp_ref, 1), _rows_of(dp_ref, 2), _rows_of(dp_ref, 3), _rows_of(dp_ref, 5),
                    _rows_of(dgb_s, cols=slice(0, LW)), _rows_of(dgb_s, cols=slice(LW, 2 * LW)), _rows_of(ub_s)))
        an_s[...] = a_next
        dvn_s[...] = dv_next
        gbuf_s[...] = g_next

        dgb = dgb_s[...]
        du_s[...] += _mm_nt(dgb, wai_ref[0])
        gw_ref[0] += _mm_tn(ub_s[...], dgb)

        def lru_conv_group(gi, du_next):
            g = ng - 1 - gi
            r0 = pl.multiple_of(g * SUBLANES, SUBLANES)
            sl = pl.ds(r0, SUBLANES)
            du = du_s[sl, :]
            xl = xle_s[pl.ds(r0 + SUBLANES, SUBLANES), :]
            xlp = xle_s[sl, :]
            acc_add(PV_LRU_B, du)
            acc_add(PV_LRU_W + 3, du * xl)
            acc_add(PV_LRU_W + 2, du * _shift_down(xl, xlp, 1, row))
            acc_add(PV_LRU_W + 1, du * _shift_down(xl, xlp, 2, row))
            acc_add(PV_LRU_W, du * _shift_down(xl, xlp, 3, row))
            dxl = (((_pvb(pv_ref, PV_LRU_W + 3) * du + _pvb(pv_ref, PV_LRU_W + 2) * _shift_up(du, du_next, 1, row))
                    + _pvb(pv_ref, PV_LRU_W + 1) * _shift_up(du, du_next, 2, row))
                   + _pvb(pv_ref, PV_LRU_W) * _shift_up(du, du_next, 3, row))
            return du, (dxl,)

        dun_s[...] = _for_groups(ng, lru_conv_group, dun_s[...], descending=True, stores=(_rows_of(dp_ref, 4),))

        @pl.when(first_block)
        def _():
            sv_ref[...] = jnp.zeros_like(sv_ref)
            for k in range(N_ACC):
                tot = jnp.sum(acc_s[k], axis=0, keepdims=True)
                if k == PV_LAM:
                    tot = (RG_LRU_C * tot) / (1.0 + jnp.exp(pv_ref[PV_LAM:PV_LAM + 1, :]))
                sv_ref[k:k + 1, :] = tot

    def part(p):
        return pl.BlockSpec((None, tb, LW), lambda c, i: (2 * p + c // STRIPS_PER_CHUNK, nt - 1 - i, c % STRIPS_PER_CHUNK))

    def halo(p):
        return pl.BlockSpec((None, SUBLANES, LW), lambda c, i: (2 * p + c // STRIPS_PER_CHUNK,
                                                                jnp.maximum((nt - 1 - i) * gpb - 1, 0), c % STRIPS_PER_CHUNK))

    strip = pl.BlockSpec((tb, LW), lambda c, i: (nt - 1 - i, c))
    big = pltpu.VMEM((tb, LW), F32)
    big_e = pltpu.VMEM((tb + SUBLANES, LW), F32)
    small = pltpu.VMEM((SUBLANES, LW), F32)
    outs = pl.pallas_call(
        body, grid=(NS, nt),
        in_specs=[part(p) for p in range(N_PARTS)] + [
            strip, strip, strip, strip, strip, pl.BlockSpec((tb, LW), lambda c, i: (nt - 1 - i, NS + c)),
            halo(1), halo(2), halo(4),
            pl.BlockSpec((SUBLANES, LW), lambda c, i: (jnp.maximum((nt - 1 - i) * gpb - 1, 0), c)),
            pl.BlockSpec((PV_ROWS, LW), lambda c, i: (0, c)),
            pl.BlockSpec((1, LW, 2 * LW), lambda c, i: (c, 0, 0)),
            pl.BlockSpec(memory_space=pl.ANY)],
        out_specs=(pl.BlockSpec((N_PARTS, tb, LW), lambda c, i: (0, nt - 1 - i, c)),
                   pl.BlockSpec((1, LW, 2 * LW), lambda c, i: (c, 0, 0)),
                   pl.BlockSpec((PV_ROWS, LW), lambda c, i: (0, c)),
                   pl.BlockSpec(memory_space=pl.ANY)),
        out_shape=(jax.ShapeDtypeStruct((N_PARTS, t, D_PART), MXU_DTYPE),
                   jax.ShapeDtypeStruct((NS, LW, 2 * LW), F32), jax.ShapeDtypeStruct((PV_ROWS, D_PART), F32),
                   _chip_blocks_shape(sb_out, 1)),
        scratch_shapes=[small, pltpu.VMEM((tb, LW), MXU_DTYPE), big_e, big_e, big_e,
                        pltpu.VMEM((tb, 2 * LW), MXU_DTYPE), big, small,
                        pltpu.VMEM((N_ACC, SUBLANES, LW), F32), small, small, small,
                        pltpu.SemaphoreType.DMA((3,)), pltpu.SemaphoreType.DMA((3,))],
        compiler_params=_cp(ARB, ARB), name="mixer_backward",
    )(proj, proj, proj, proj, proj, proj, h, u, r, ig, dy, dy, proj, proj, proj, h, pvec, wai, sb_out)
    return outs


def _adamw(w, g, m, v):
    m = ADAM_B1 * m + (1.0 - ADAM_B1) * g
    v = ADAM_B2 * v + (1.0 - ADAM_B2) * (g * g)
    m_hat = m / (1.0 - ADAM_B1 ** ADAM_STEP)
    v_hat = v / (1.0 - ADAM_B2 ** ADAM_STEP)
    delta = -ADAM_LR * (m_hat / (jnp.sqrt(v_hat) + ADAM_EPS) + ADAM_WD * w)
    return delta, m, v


def _adamw_blocks(f_in, f_out, p_in, p_out):
    n_pieces = 4
    slab = 2 * SUBLANES

    def body(fi_hbm, fo_hbm, wi_hbm, mi_hbm, vi_hbm, wo_hbm, mo_hbm, vo_hbm,
             gi_hbm, di_hbm, nmi_hbm, nvi_hbm, do_hbm, nmo_hbm, nvo_hbm,
             g_in, g_out, wi, mi, vi, wo, mo, vo, load_sems, store_sems):
        blocks = ((g_in, fi_hbm, (wi, mi, vi), (wi_hbm, mi_hbm, vi_hbm), (di_hbm, nmi_hbm, nvi_hbm)),
                  (g_out, fo_hbm, (wo, mo, vo), (wo_hbm, mo_hbm, vo_hbm), (do_hbm, nmo_hbm, nvo_hbm)))
        loads, stores = [], []

        def start(src, dst, sems, group):
            group.append(pltpu.make_async_copy(src, dst, sems.at[len(group)]))
            group[-1].start()

        piece = lambda g, i: slice(g.shape[1] // n_pieces * i, g.shape[1] // n_pieces * (i + 1))
        for i in range(n_pieces):
            for g, f_hbm, p, p_hbm, _ in blocks:
                rows = piece(g, i)
                start(f_hbm.at[:, rows, :], g.at[:, rows, :], load_sems, loads)
                for s, s_hbm in zip(p, p_hbm):
                    start(s_hbm.at[rows, :], s.at[rows, :], load_sems, loads)
        per_piece = len(loads) // n_pieces
        for i in range(n_pieces):
            for cp in loads[per_piece * i:per_piece * (i + 1)]:
                cp.wait()
            for g, _, (w_s, m_s, v_s), _, outs in blocks:
                rows = piece(g, i)
                n, cols = g.shape[0], g.shape[2]

                def step(t, carry):
                    rs = pl.ds(pl.multiple_of(rows.start + slab * t, slab), slab)
                    for q in range(n):
                        cs = slice(cols * q, cols * (q + 1))
                        w_s[rs, cs], m_s[rs, cs], v_s[rs, cs] = _adamw(w_s[rs, cs], g[q, rs, :], m_s[rs, cs], v_s[rs, cs])
                    return carry

                lax.fori_loop(0, (rows.stop - rows.start) // slab, step, 0)
                for s, o_hbm in zip((w_s, m_s, v_s), outs):
                    start(s.at[rows, :], o_hbm.at[rows, :], store_sems, stores)
            rows, cols = piece(g_in, i), g_in.shape[2]
            for q in range(g_in.shape[0]):
                start(g_in.at[q, rows, :], gi_hbm.at[rows, cols * q:cols * (q + 1)], store_sems, stores)
        for cp in stores:
            cp.wait()

    w_i, w_o = p_in[0], p_out[0]
    n_in = f_in.shape[0]
    assert w_i.shape == (f_in.shape[1], n_in * f_in.shape[2]) and w_o.shape == (f_out.shape[1], f_out.shape[0] * f_out.shape[2])
    hbm = pl.BlockSpec(memory_space=pl.ANY)
    return pl.pallas_call(
        body, in_specs=[hbm] * 8, out_specs=(hbm,) * 7,
        out_shape=(jax.ShapeDtypeStruct(w_i.shape, F32),) * 4 + (jax.ShapeDtypeStruct(w_o.shape, F32),) * 3,
        scratch_shapes=[pltpu.VMEM(f_in.shape, F32), pltpu.VMEM(f_out.shape, F32)]
        + [pltpu.VMEM(w_i.shape, F32)] * 3 + [pltpu.VMEM(w_o.shape, F32)] * 3
        + [pltpu.SemaphoreType.DMA((n_pieces * 8,)), pltpu.SemaphoreType.DMA((n_pieces * (6 + n_in),))],
        compiler_params=_cp(), name="adamw_blocks",
    )(f_in, f_out, *p_in, *p_out)


def _adam_small(ws, ms, vs, gs):
    n = len(ws)

    def body(*refs):
        w_r, m_r, v_r, g_r = refs[0:n], refs[n:2 * n], refs[2 * n:3 * n], refs[3 * n:4 * n]
        d_o, m_o, v_o = refs[4 * n:5 * n], refs[5 * n:6 * n], refs[6 * n:7 * n]
        for j in range(n):
            d_o[j][...], m_o[j][...], v_o[j][...] = _adamw(w_r[j][...], g_r[j][...], m_r[j][...], v_r[j][...])

    vm = pl.BlockSpec(memory_space=pltpu.VMEM)
    shapes = tuple(jax.ShapeDtypeStruct(w.shape, F32) for w in ws)
    outs = pl.pallas_call(
        body, in_specs=[vm] * (4 * n), out_specs=(vm,) * (3 * n), out_shape=shapes * 3,
        compiler_params=_cp(), name="adam_small",
    )(*ws, *ms, *vs, *gs)
    return outs[0:n], outs[n:2 * n], outs[2 * n:3 * n]


def _block_diag_strips(w, lw):
    heads = lw // LRU_HEAD
    w4 = w.reshape(D_PART // lw, heads, LRU_HEAD, LRU_HEAD)
    rows = [jnp.pad(w4[:, hh], ((0, 0), (0, 0), (LRU_HEAD * hh, lw - LRU_HEAD * (hh + 1)))) for hh in range(heads)]
    return jnp.concatenate(rows, axis=1)


def _gate_matrices(w_a, w_i, lw):
    return jnp.concatenate([_block_diag_strips(w_a, lw), _block_diag_strips(w_i, lw)], axis=2).astype(MXU_DTYPE)


def _strip_diag_blocks(g):
    g5 = g.reshape(NS, HEADS_PER_STRIP, LRU_HEAD, HEADS_PER_STRIP, LRU_HEAD)
    return jnp.stack([g5[:, hh, :, hh, :] for hh in range(HEADS_PER_STRIP)], axis=1).reshape(NS * HEADS_PER_STRIP, LRU_HEAD, LRU_HEAD)


def kernel(x, ln_g, w_in, conv_w, lru_conv_w, lru_conv_b, w_a, b_a, w_i, b_i, lam, conv_out_g, lru_out_g, w_out, final_g, loss_target, m_ln_g, m_w_in, m_conv_w, m_lru_conv_w, m_lru_conv_b, m_w_a, m_b_a, m_w_i, m_b_i, m_lam, m_conv_out_g, m_lru_out_g, m_w_out, m_final_g, v_ln_g, v_w_in, v_conv_w, v_lru_conv_w, v_lru_conv_b, v_w_a, v_b_a, v_w_i, v_b_i, v_lam, v_conv_out_g, v_lru_out_g, v_w_out, v_final_g):
    xi, yi, ci = lax.axis_index("x"), lax.axis_index("y"), lax.axis_index("c")
    k = 2 * xi + yi
    t = x.shape[1]
    x2 = x.reshape(t, D_MODEL)
    tgt2 = loss_target.reshape(t, D_MODEL)
    row = lambda a: a.reshape(1, -1)

    small = jnp.concatenate([conv_w, lru_conv_w, jnp.zeros((1, conv_w.shape[1]), F32)], axis=0)
    proj, xn, w12, sm4 = _gather_in_projection(x2, row(ln_g), w_in, small)
    convs = jnp.transpose(sm4, (1, 0, 2)).reshape(SUBLANES, D_PART)
    pvec = jnp.concatenate(
        [convs[0:7], row(lru_conv_b), row(b_a), row(b_i), row(lam), row(conv_out_g), row(lru_out_g),
         jnp.zeros((PV_ROWS - N_ACC, D_PART), F32)], axis=0)
    wai = _gate_matrices(w_a, w_i, LW)

    c_arr = jnp.reshape(ci, (1,)).astype(jnp.int32)
    yc, yl, h, u, r, ig, wo4 = _mixer_forward(proj, pvec, _gate_matrices(w_a, w_i, FWD_LW), w_out)
    wo = wo4.reshape(2 * D_PART, D_MODEL)
    do, dob, dy, st_out = _out_projection_loss(yc, yl, x2, tgt2, wo, row(final_g))
    go4, go4b = _w_out_grad(yc, yl, dob)
    s_out, sb_out = _add_sibling_halves(go4, go4b, c_arr, "add_sibling_halves_out")
    dproj, g_wai, svec, r2o = _mixer_backward(proj, h, u, r, ig, dy, pvec, wai, sb_out)
    gwa = _strip_diag_blocks(g_wai[:, :, 0:LW]).reshape(LRU_HEAD, D_PART)
    gwi = _strip_diag_blocks(g_wai[:, :, LW:2 * LW]).reshape(LRU_HEAD, D_PART)
    g12, g12b, red = _w_in_grad(xn, dproj, jnp.concatenate([svec, st_out, gwa, gwi], axis=0))
    s_in, sb_in = _add_sibling_halves(g12, g12b, c_arr, "add_sibling_halves_in")
    grad_x, st_in, r2i = _input_grad(dproj, w12, x2, do, row(ln_g), sb_in)
    f_in, f_out, red_ln = _finish_gradients(s_in, r2i, s_out, r2o, st_in)
    r_out = PV_ROWS
    r_wa = PV_ROWS + SUBLANES
    r_wi = r_wa + LRU_HEAD
    loss = red[r_out + 1, 0]

    g_w_in, d_w_in, nm_w_in, nv_w_in, d_w_out, nm_w_out, nv_w_out = _adamw_blocks(
        f_in, f_out, (w_in, m_w_in, v_w_in), (w_out, m_w_out, v_w_out))
    g_w_out = f_out[0]

    ncol = conv_w.shape[1]
    conv_cols = lax.dynamic_slice(red, (0, k * ncol), (SUBLANES, ncol))
    g_small = {
        "ln_g": red_ln[0], "conv_w": conv_cols[0:3], "lru_conv_w": conv_cols[3:7], "lru_conv_b": red[PV_LRU_B],
        "w_a": red[r_wa:r_wa + LRU_HEAD].reshape(w_a.shape), "b_a": red[PV_BA],
        "w_i": red[r_wi:r_wi + LRU_HEAD].reshape(w_i.shape), "b_i": red[PV_BI], "lam": red[PV_LAM],
        "conv_out_g": red[PV_CG], "lru_out_g": red[PV_LG], "final_g": red[r_out],
    }
    w_small = {"ln_g": ln_g, "conv_w": conv_w, "lru_conv_w": lru_conv_w, "lru_conv_b": lru_conv_b, "w_a": w_a, "b_a": b_a,
               "w_i": w_i, "b_i": b_i, "lam": lam, "conv_out_g": conv_out_g, "lru_out_g": lru_out_g, "final_g": final_g}
    m_small = {"ln_g": m_ln_g, "conv_w": m_conv_w, "lru_conv_w": m_lru_conv_w, "lru_conv_b": m_lru_conv_b, "w_a": m_w_a,
               "b_a": m_b_a, "w_i": m_w_i, "b_i": m_b_i, "lam": m_lam, "conv_out_g": m_conv_out_g,
               "lru_out_g": m_lru_out_g, "final_g": m_final_g}
    v_small = {"ln_g": v_ln_g, "conv_w": v_conv_w, "lru_conv_w": v_lru_conv_w, "lru_conv_b": v_lru_conv_b, "w_a": v_w_a,
               "b_a": v_b_a, "w_i": v_w_i, "b_i": v_b_i, "lam": v_lam, "conv_out_g": v_conv_out_g,
               "lru_out_g": v_lru_out_g, "final_g": v_final_g}
    names = list(w_small)
    as2d = lambda a: a.reshape(1, -1) if a.ndim == 1 else a
    d_s, m_s, v_s = _adam_small([as2d(w_small[n]) for n in names], [as2d(m_small[n]) for n in names],
                                [as2d(v_small[n]) for n in names], [as2d(g_small[n]) for n in names])
    back = lambda n, a: a.reshape(w_small[n].shape)
    grads = {n: g_small[n] for n in names}
    deltas = {n: back(n, a) for n, a in zip(names, d_s)}
    new_m = {n: back(n, a) for n, a in zip(names, m_s)}
    new_v = {n: back(n, a) for n, a in zip(names, v_s)}
    grads["w_in"], deltas["w_in"], new_m["w_in"], new_v["w_in"] = g_w_in, d_w_in, nm_w_in, nv_w_in
    grads["w_out"], deltas["w_out"], new_m["w_out"], new_v["w_out"] = g_w_out, d_w_out, nm_w_out, nv_w_out

    order = ["ln_g", "w_in", "conv_w", "lru_conv_w", "lru_conv_b", "w_a", "b_a", "w_i", "b_i", "lam", "conv_out_g",
             "lru_out_g", "w_out", "final_g"]
    return (loss, grad_x.reshape(x.shape), *[grads[n] for n in order], *[deltas[n] for n in order],
            *[new_m[n] for n in order], *[new_v[n] for n in order])
```

```python
import functools

import jax
import jax.numpy as jnp
from jax import lax
from jax.experimental import pallas as pl
from jax.experimental.pallas import tpu as pltpu

F32 = jnp.float32
MXU_DTYPE = jnp.bfloat16

D_MODEL = 1024
D_PART = 1024
N_PARTS = 6
CHUNK = 512
CHUNKS_PER_BLOCK = 3
N_CHUNKS = 12
N_CHIPS = 4
SUBLANES = 8
LANES = 128
LW = 256
FWD_LW = 512
UNROLL = 8
NS = D_PART // LW
STRIPS_PER_CHUNK = CHUNK // LW
CONV_HEAD = 128
LRU_HEAD = 64
HEADS_PER_STRIP = LW // LRU_HEAD
RMS_EPS = 1e-6
RG_LRU_C = 8.0
ADAM_LR = 0.001
ADAM_B1 = 0.9
ADAM_B2 = 0.999
ADAM_EPS = 1e-08
ADAM_WD = 0.01
ADAM_STEP = 10

PV_CONV_W = 0
PV_LRU_W = 3
PV_LRU_B = 7
PV_BA = 8
PV_BI = 9
PV_LAM = 10
PV_CG = 11
PV_LG = 12
PV_ROWS = 16
N_ACC = 13

SLAB = 128
MESH = pl.DeviceIdType.MESH
VMEM_LIMIT = 56 * 1024 * 1024
ARB = "arbitrary"


def _cp(*sem, **kw):
    return pltpu.CompilerParams(dimension_semantics=sem or None, vmem_limit_bytes=VMEM_LIMIT, **kw)


def _mm(a, b):
    return jnp.dot(a, b, preferred_element_type=F32)


def _mm_nt(a, b):
    return lax.dot_general(a, b, (((1,), (1,)), ((), ())), preferred_element_type=F32)


def _mm_tn(a, b):
    return lax.dot_general(a, b, (((0,), (0,)), ((), ())), preferred_element_type=F32)


def _sigmoid(x):
    return 0.5 * jnp.tanh(0.5 * x) + 0.5


def _log_sigmoid(x):
    z = jnp.exp(-jnp.abs(x))
    u = 1.0 + z
    log1p = jnp.where(u == 1.0, z, jnp.log(u) * z / (u - 1.0))
    return jnp.minimum(x, 0.0) - log1p


def _head_mean(z, head):
    out = []
    for k in range(z.shape[1] // LANES):
        zk = z[:, LANES * k:LANES * (k + 1)]
        if head == LANES:
            m = jnp.sum(zk, axis=-1, keepdims=True) * (1.0 / head)
            out.append(jnp.broadcast_to(m, zk.shape))
        else:
            lo = lax.broadcasted_iota(jnp.int32, zk.shape, 1) < head
            s_lo = jnp.sum(jnp.where(lo, zk, 0.0), axis=-1, keepdims=True)
            s_hi = jnp.sum(jnp.where(lo, 0.0, zk), axis=-1, keepdims=True)
            out.append(jnp.where(lo, s_lo, s_hi) * (1.0 / head))
    return jnp.concatenate(out, axis=1)


def _shift_down(cur, prev, d, row):
    return pltpu.roll(jnp.where(row < SUBLANES - d, cur, prev), d, 0)


def _shift_up(cur, nxt, d, row):
    return pltpu.roll(jnp.where(row >= d, cur, nxt), SUBLANES - d, 0)


def _scan8_fwd(a, b, row):
    A, B = a, b
    for d in (1, 2, 4):
        m = row >= d
        a_s = jnp.where(m, pltpu.roll(A, d, 0), 1.0)
        b_s = jnp.where(m, pltpu.roll(B, d, 0), 0.0)
        B = A * b_s + B
        A = A * a_s
    return A, B


def _scan8_rev(a, b, row):
    A, B = a, b
    for d in (1, 2, 4):
        m = row < SUBLANES - d
        a_s = jnp.where(m, pltpu.roll(A, SUBLANES - d, 0), 1.0)
        b_s = jnp.where(m, pltpu.roll(B, SUBLANES - d, 0), 0.0)
        B = A * b_s + B
        A = A * a_s
    return A, B


def _decay(r, ls8):
    la = r * ls8
    a = jnp.exp(la)
    e2 = a * a
    em = -jnp.tanh(la) * (1.0 + e2)
    inv_mult = lax.rsqrt(em)
    return a, e2, em * inv_mult, inv_mult


def _mesh_pos():
    x, y, c = lax.axis_index("x"), lax.axis_index("y"), lax.axis_index("c")
    chips = [(1 - x, y), (x, 1 - y), (1 - x, 1 - y)]
    return x, y, c, chips


def _gather_in_projection(x, ln_g, w_in, small):
    t = x.shape[0]
    rb_x = 512
    rb_mm = 2048
    n_mm = t // rb_mm
    half = w_in.shape[0] // 2

    def body(x_hbm, g_ref, wi_ref, sm_ref, proj_hbm, xn_ref, w12_ref, sm4_ref,
             xbuf, obuf, x_sems, o_sems, send_sems, recv_sems):
        x_, y_, c, chips = _mesh_pos()
        k = 2 * x_ + y_
        sib = (x_, y_, 1 - c)
        sm4_ref[k] = sm_ref[...]

        def remote(ref, sem, to):
            return pltpu.make_async_remote_copy(src_ref=ref, dst_ref=ref, send_sem=send_sems.at[sem],
                                                recv_sem=recv_sems.at[sem], device_id=to, device_id_type=MESH)

        def chunk_of(chip, s):
            return CHUNKS_PER_BLOCK * (2 * chip[0] + chip[1]) + s

        def piece(q, core, first=0, rows=half):
            return w12_ref.at[q, pl.ds(pl.multiple_of(half * core + first, SUBLANES * 2), rows), :]

        nbr_x, nbr_y, diagonal = chips
        quarter = half // 2
        DIAG = [(0, 0, half, 0), (1, 0, quarter, 0), (1, quarter, quarter, 1), (2, 0, half, 1)]
        ici = lambda m, s: 2 * s + m
        dgn = lambda j: 6 + j
        to_sib = 10
        sml = lambda m: 20 + m

        sends = []
        for s in range(CHUNKS_PER_BLOCK):
            w12_ref[chunk_of((x_, y_), s)] = wi_ref[:, CHUNK * s:CHUNK * (s + 1)].astype(MXU_DTYPE)
            for m, chip in enumerate((nbr_x, nbr_y)):
                sends.append(remote(piece(chunk_of((x_, y_), s), c), ici(m, s), (*chip, c)))
                sends[-1].start()
        for m, chip in enumerate(chips):
            sends.append(remote(sm4_ref.at[k], sml(m), (*chip, c)))
            sends[-1].start()

        def x_copy(rb, slot):
            return pltpu.make_async_copy(x_hbm.at[pl.ds(rb * rb_x, rb_x), :], xbuf.at[slot], x_sems.at[slot])

        x_copy(0, 0).start()
        for rb in range(t // rb_x):
            slot = rb % 2
            x_copy(rb, slot).wait()
            if rb + 1 < t // rb_x:
                x_copy(rb + 1, 1 - slot).start()

            def norm_slab(sl, carry, rb=rb, slot=slot):
                xf = xbuf[slot, pl.ds(pl.multiple_of(sl * SLAB, SLAB), SLAB), :]
                r = lax.rsqrt(jnp.mean(xf * xf, axis=-1, keepdims=True) + RMS_EPS)
                xn_ref[pl.ds(pl.multiple_of(rb * rb_x + sl * SLAB, SLAB), SLAB), :] = ((xf * r) * g_ref[...]).astype(MXU_DTYPE)
                return carry

            lax.fori_loop(0, rb_x // SLAB, norm_slab, 0)

        def out_copy(q, i):
            return pltpu.make_async_copy(obuf.at[i], proj_hbm.at[q, pl.ds(pl.multiple_of(i * rb_mm, rb_mm), rb_mm), :],
                                         o_sems.at[i])

        def project(q, very_first):
            def row_block(i, carry):
                if not very_first:
                    out_copy(q, i).wait()
                obuf[i] = _mm(xn_ref[pl.ds(pl.multiple_of(i * rb_mm, rb_mm), rb_mm), :], w12_ref[q])
                out_copy(q, i).start()
                return carry

            lax.fori_loop(0, n_mm, row_block, 0)

        for s in range(CHUNKS_PER_BLOCK):
            project(chunk_of((x_, y_), s), very_first=(s == 0))

        steps = []
        for s in range(CHUNKS_PER_BLOCK):
            for m, chip in enumerate((nbr_x, nbr_y)):
                onward = [(first, rows, dgn(j), chips[via]) for j, (cs, first, rows, via) in enumerate(DIAG)
                          if cs == s and via == 1 - m]
                steps.append((chunk_of(chip, s), [(0, half, ici(m, s))], onward))
        for s in range(CHUNKS_PER_BLOCK):
            steps.append((chunk_of(diagonal, s), [(first, rows, dgn(j)) for j, (cs, first, rows, _) in enumerate(DIAG) if cs == s], []))

        def project_when_whole(step):
            q, pieces, _ = step
            for first, rows, sem in pieces:
                remote(piece(q, 1 - c, first, rows), to_sib + sem, sib).wait_recv()
            project(q, very_first=False)

        passed = []
        for j, (q, pieces, onward) in enumerate(steps):
            for first, rows, sem in pieces:
                remote(piece(q, c, first, rows), sem, sib).wait_recv()
            for first, rows, sem, chip in onward:
                passed.append(remote(piece(q, c, first, rows), sem, (*chip, c)))
                passed[-1].start()
            for first, rows, sem in pieces:
                passed.append(remote(piece(q, c, first, rows), to_sib + sem, sib))
                passed[-1].start()
            if j > 0:
                project_when_whole(steps[j - 1])
        project_when_whole(steps[-1])

        for m, chip in enumerate(chips):
            remote(sm4_ref.at[2 * chip[0] + chip[1]], sml(m), sib).wait_recv()
        for cp in sends + passed:
            cp.wait_send()
        for i in range(n_mm):
            out_copy(0, i).wait()

    vm = pl.BlockSpec(memory_space=pltpu.VMEM)
    hbm = pl.BlockSpec(memory_space=pl.ANY)
    n_sems = 23
    return pl.pallas_call(
        body,
        out_shape=(jax.ShapeDtypeStruct((N_CHUNKS, t, CHUNK), F32), jax.ShapeDtypeStruct((t, D_MODEL), MXU_DTYPE),
                   jax.ShapeDtypeStruct((N_CHUNKS, w_in.shape[0], CHUNK), MXU_DTYPE),
                   jax.ShapeDtypeStruct((N_CHIPS,) + small.shape, F32)),
        in_specs=[hbm, vm, vm, vm], out_specs=(hbm, vm, vm, vm),
        scratch_shapes=[pltpu.VMEM((2, rb_x, D_MODEL), F32), pltpu.VMEM((n_mm, rb_mm, CHUNK), F32),
                        pltpu.SemaphoreType.DMA((2,)), pltpu.SemaphoreType.DMA((n_mm,)),
                        pltpu.SemaphoreType.DMA((n_sems,)), pltpu.SemaphoreType.DMA((n_sems,))],
        compiler_params=_cp(), name="gather_in_projection",
    )(x, ln_g, w_in, small)


def _allreduce_behind(step, when, in_ref, acc_s, rbufs, out_ref, send_sems, recv_sems):
    x, y, c, _ = _mesh_pos()
    peers = [(x, y, 1 - c), (1 - x, y, c), (x, 1 - y, c)]

    def exchange(ph):
        return pltpu.make_async_remote_copy(src_ref=acc_s, dst_ref=rbufs[ph], send_sem=send_sems.at[ph],
                                            recv_sem=recv_sems.at[ph], device_id=peers[ph], device_id_type=MESH)

    @pl.when(step == when[0])
    def _():
        acc_s[...] = in_ref[...]
        exchange(0).start()

    for ph in (1, 2):
        @pl.when(step == when[ph])
        def _(ph=ph):
            exchange(ph - 1).wait()
            acc_s[...] = acc_s[...] + rbufs[ph - 1][...]
            exchange(ph).start()

    @pl.when(step == when[3])
    def _():
        exchange(2).wait()
        out_ref[...] = acc_s[...] + rbufs[2][...]


def _add_sibling_halves(g, gb, c_arr, name):
    n, rows, cols = g.shape
    half = rows // 2
    per = 2
    steps = n // per

    def body(c_ref, g_ref, gb_hbm, o_ref, ob_ref, rbuf, send_sems, recv_sems):
        q = pl.program_id(0)
        x, y, c, _ = _mesh_pos()
        theirs = pl.ds(pl.multiple_of(half * (1 - c), half), half)

        def copy(j):
            blocks = pl.ds(j * per, per)
            return pltpu.make_async_remote_copy(src_ref=gb_hbm.at[blocks, theirs, :], dst_ref=rbuf.at[blocks], send_sem=send_sems.at[j],
                                                recv_sem=recv_sems.at[j], device_id=(x, y, 1 - c), device_id_type=MESH)

        @pl.when(q == 0)
        def _():
            for j in range(steps):
                copy(j).start()

        copy(q).wait_recv()
        s = g_ref[...] + rbuf[pl.ds(q * per, per)].astype(F32)
        o_ref[...] = s
        ob_ref[...] = s.astype(jnp.bfloat16)

        @pl.when(q == steps - 1)
        def _():
            for j in range(steps):
                copy(j).wait_send()

    blk = pl.BlockSpec((per, half, cols), lambda q, c_ref: (q, 0, 0))
    return pl.pallas_call(
        body, out_shape=(jax.ShapeDtypeStruct((n, half, cols), F32), jax.ShapeDtypeStruct((n, half, cols), jnp.bfloat16)),
        grid_spec=pltpu.PrefetchScalarGridSpec(
            num_scalar_prefetch=1, grid=(steps,),
            in_specs=[pl.BlockSpec((per, half, cols), lambda q, c_ref: (q, c_ref[0], 0)), pl.BlockSpec(memory_space=pl.ANY)],
            out_specs=(blk, blk),
            scratch_shapes=[pltpu.VMEM((n, half, cols), jnp.bfloat16), pltpu.SemaphoreType.DMA((steps,)),
                            pltpu.SemaphoreType.DMA((steps,))]),
        compiler_params=_cp(ARB), name=name,
    )(c_arr, g, gb)


def _chip_block_copies(s_ref, r_ref, n_sub, send_sems, recv_sems):
    x, y, c, chips = _mesh_pos()
    cps = []
    for m, chip in enumerate(chips):
        kk = 2 * chip[0] + chip[1]
        cps.append(pltpu.make_async_remote_copy(
            src_ref=s_ref.at[pl.ds(n_sub * kk, n_sub)], dst_ref=r_ref.at[m],
            send_sem=send_sems.at[m], recv_sem=recv_sems.at[m], device_id=(*chip, c), device_id_type=MESH))
    return cps


def _gather_w_out(step, n_steps, wo_ref, wob_s, wo4_ref, local_sem, send_sems, recv_sems):
    x, y, c, chips = _mesh_pos()
    sib = (x, y, 1 - c)
    half = wo_ref.shape[0] // 2

    def rows(core):
        return pl.ds(pl.multiple_of(half * core, half), half)

    def block_half(chip, core):
        return wo4_ref.at[2 * chip[0] + chip[1], rows(core), :]

    def remote(src, dst, sem, to):
        return pltpu.make_async_remote_copy(src_ref=src, dst_ref=dst, send_sem=send_sems.at[sem], recv_sem=recv_sems.at[sem],
                                            device_id=to, device_id_type=MESH)

    local = pltpu.make_async_copy(wob_s, wo4_ref.at[2 * x + y], local_sem)
    ici = [remote(wob_s.at[rows(c), :], block_half((x, y), c), m, (*chip, c)) for m, chip in enumerate(chips)]
    fwd = [remote(block_half(chip, c), block_half(chip, c), 3 + m, sib) for m, chip in enumerate(chips)]

    @pl.when(step == 0)
    def _():
        wob_s[...] = wo_ref[...].astype(MXU_DTYPE)
        local.start()
        for cp in ici:
            cp.start()

    @pl.when(step == n_steps // 2)
    def _():
        for m, chip in enumerate(chips):
            remote(block_half(chip, c), block_half(chip, c), m, sib).wait_recv()
            fwd[m].start()

    @pl.when(step == n_steps - 1)
    def _():
        for m, chip in enumerate(chips):
            remote(block_half(chip, 1 - c), block_half(chip, 1 - c), 3 + m, sib).wait_recv()
        for cp in ici + fwd:
            cp.wait_send()
        local.wait()


def _chip_blocks_shape(s, n_sub):
    return jax.ShapeDtypeStruct((3, n_sub) + s.shape[1:], s.dtype)


def _finish_gradients(s_in, r_in, s_out, r_out, v):
    n_dev = 8
    n_in, n_out = r_in.shape[1], r_out.shape[1]

    def body(si_hbm, ri_hbm, so_hbm, ro_hbm, v_ref, fi_hbm, fo_hbm, tot_ref,
             a_in, b_in, a_out, b_out, slots, load_sems, store_sems, send_sems, recv_sems):
        x, y, c, _ = _mesh_pos()
        k = 2 * x + y
        sib = (x, y, 1 - c)
        me = 4 * x + 2 * y + c
        loads = [pltpu.make_async_copy(si_hbm.at[pl.ds(n_in * k, n_in)], a_in, load_sems.at[0]),
                 pltpu.make_async_copy(ri_hbm, b_in, load_sems.at[1]),
                 pltpu.make_async_copy(so_hbm.at[pl.ds(n_out * k, n_out)], a_out, load_sems.at[2]),
                 pltpu.make_async_copy(ro_hbm, b_out, load_sems.at[3])]
        for cp in loads:
            cp.start()
        slots[me] = v_ref[...]

        def remote(src, dst, sem, to):
            return pltpu.make_async_remote_copy(src_ref=src, dst_ref=dst, send_sem=send_sems.at[sem],
                                                recv_sem=recv_sems.at[sem], device_id=to, device_id_type=MESH)

        small = []
        for d in range(1, n_dev):
            peer = (1 - x if d & 4 else x, 1 - y if d & 2 else y, 1 - c if d & 1 else c)
            small.append(remote(slots.at[me], slots.at[me], d - 1, peer))
            small[-1].start()
        for cp in loads:
            cp.wait()
        big = []
        for j, (a, b, f_hbm) in enumerate(((a_in, b_in, fi_hbm), (a_out, b_out, fo_hbm))):
            a[...] = ((a[...] + b[0].astype(F32)) + b[1].astype(F32)) + b[2].astype(F32)
            half = a.shape[1]
            mine = f_hbm.at[:, pl.ds(pl.multiple_of(half * c, half), half), :]
            big.append(pltpu.make_async_copy(a, mine, store_sems.at[j]))
            big.append(remote(a, mine, n_dev - 1 + j, sib))
        for cp in big:
            cp.start()
        for cp in small + big:
            cp.wait()
        total = slots[0]
        for dev in range(1, n_dev):
            total = total + slots[dev]
        tot_ref[...] = total

    hbm = pl.BlockSpec(memory_space=pl.ANY)
    vm = pl.BlockSpec(memory_space=pltpu.VMEM)
    full = lambda s, n: (n, 2 * s.shape[1], s.shape[2])
    return pl.pallas_call(
        body,
        out_shape=(jax.ShapeDtypeStruct(full(s_in, n_in), F32), jax.ShapeDtypeStruct(full(s_out, n_out), F32),
                   jax.ShapeDtypeStruct(v.shape, F32)),
        in_specs=[hbm, hbm, hbm, hbm, vm], out_specs=(hbm, hbm, vm),
        scratch_shapes=[pltpu.VMEM((n_in,) + s_in.shape[1:], F32), pltpu.VMEM(r_in.shape, r_in.dtype),
                        pltpu.VMEM((n_out,) + s_out.shape[1:], F32), pltpu.VMEM(r_out.shape, r_out.dtype),
                        pltpu.VMEM((n_dev,) + v.shape, F32), pltpu.SemaphoreType.DMA((4,)), pltpu.SemaphoreType.DMA((2,)),
                        pltpu.SemaphoreType.DMA((n_dev + 1,)), pltpu.SemaphoreType.DMA((n_dev + 1,))],
        compiler_params=_cp(), name="finish_gradients",
    )(s_in, r_in, s_out, r_out, v)


def _out_projection_loss(yc, yl, x, target, wo, final_g):
    t = x.shape[0]
    tm = 512

    def body(yc_ref, yl_ref, x_ref, t_ref, wo_ref, fg_ref, do_ref, dob_ref, dy_ref, st_ref, y_wo):
        @pl.when(pl.program_id(0) == 0)
        def _():
            st_ref[...] = jnp.zeros_like(st_ref)

        y_wo[...] = _mm(yc_ref[...], wo_ref[0:D_PART, :]) + _mm(yl_ref[...], wo_ref[D_PART:2 * D_PART, :])

        def norm_loss_slab(s, carry):
            g_sum, loss_sum = carry
            rows = pl.ds(pl.multiple_of(s * SLAB, SLAB), SLAB)
            o = x_ref[rows, :] + y_wo[rows, :]
            r2 = lax.rsqrt(jnp.mean(o * o, axis=-1, keepdims=True) + RMS_EPS)
            ohat = o * r2
            fg = fg_ref[...]
            diff = ohat * fg - t_ref[rows, :]
            dout = diff * (1.0 / D_MODEL)
            gp = dout * fg
            do = r2 * (gp - ohat * jnp.mean(gp * ohat, axis=-1, keepdims=True))
            do_ref[rows, :] = do
            dob_ref[rows, :] = do.astype(MXU_DTYPE)
            loss = 0.5 * jnp.sum(jnp.sum(diff * diff, axis=-1, keepdims=True) * (1.0 / D_MODEL), axis=0, keepdims=True)
            return g_sum + jnp.sum(dout * ohat, axis=0, keepdims=True), loss_sum + loss

        g_sum, loss_sum = lax.fori_loop(0, tm // SLAB, norm_loss_slab,
                                        (jnp.zeros((1, D_MODEL), F32), jnp.zeros((1, 1), F32)))
        st_ref[0:1, :] += g_sum
        st_ref[1:2, :] += jnp.broadcast_to(loss_sum, (1, D_MODEL))
        dy_ref[...] = _mm_nt(dob_ref[...], wo_ref[...])

    row = lambda i: (i, 0)
    fix = lambda i: (0, 0)
    return pl.pallas_call(
        body, grid=(t // tm,),
        in_specs=[pl.BlockSpec((tm, D_PART), row), pl.BlockSpec((tm, D_PART), row),
                  pl.BlockSpec((tm, D_MODEL), row), pl.BlockSpec((tm, D_MODEL), row),
                  pl.BlockSpec((2 * D_PART, D_MODEL), fix), pl.BlockSpec((1, D_MODEL), fix)],
        out_specs=(pl.BlockSpec((tm, D_MODEL), row), pl.BlockSpec((tm, D_MODEL), row),
                   pl.BlockSpec((tm, 2 * D_PART), row), pl.BlockSpec((SUBLANES, D_MODEL), fix)),
        out_shape=(jax.ShapeDtypeStruct((t, D_MODEL), F32), jax.ShapeDtypeStruct((t, D_MODEL), MXU_DTYPE),
                   jax.ShapeDtypeStruct((t, 2 * D_PART), F32), jax.ShapeDtypeStruct((SUBLANES, D_MODEL), F32)),
        scratch_shapes=[pltpu.VMEM((tm, D_MODEL), F32)],
        compiler_params=_cp(ARB), name="out_projection_loss",
    )(yc, yl, x, target, wo, final_g)


def _input_grad(dproj, w12, x, do, ln_g, sb_in):
    t = x.shape[0]
    tm = 1024
    head = t - tm

    def tile_step(p, dp_ref, w_ref, x_ref, do_ref, g_ref, gx_ref, st_ref, acc):
        @pl.when(p == 0)
        def _():
            acc[...] = jnp.zeros_like(acc)

        acc[...] += _mm_nt(dp_ref[0], jnp.concatenate([w_ref[0], w_ref[1]], axis=1))

        @pl.when(p == N_PARTS - 1)
        def _():
            def norm_bwd_slab(s, g_sum):
                rows = pl.ds(pl.multiple_of(s * SLAB, SLAB), SLAB)
                xf = x_ref[rows, :]
                r = lax.rsqrt(jnp.mean(xf * xf, axis=-1, keepdims=True) + RMS_EPS)
                xhat = xf * r
                dxn = acc[rows, :]
                dxh = dxn * g_ref[...]
                gx_ref[rows, :] = do_ref[rows, :] + r * (dxh - xhat * jnp.mean(dxh * xhat, axis=-1, keepdims=True))
                return g_sum + jnp.sum(dxn * xhat, axis=0, keepdims=True)

            st_ref[0:1, :] += lax.fori_loop(0, tm // SLAB, norm_bwd_slab, jnp.zeros((1, D_MODEL), F32))

    def body(dp_ref, w_ref, x_ref, do_ref, g_ref, s_ref, gx_ref, st_ref, r_ref, acc, send_sems, recv_sems):
        i, p = pl.program_id(0), pl.program_id(1)

        @pl.when((i == 0) & (p == 0))
        def _():
            st_ref[...] = jnp.zeros_like(st_ref)
            for cp in _chip_block_copies(s_ref, r_ref, CHUNKS_PER_BLOCK, send_sems, recv_sems):
                cp.start()

        @pl.when((i == head // tm - 1) & (p == N_PARTS - 1))
        def _():
            for cp in _chip_block_copies(s_ref, r_ref, CHUNKS_PER_BLOCK, send_sems, recv_sems):
                cp.wait()

        tile_step(p, dp_ref, w_ref, x_ref, do_ref, g_ref, gx_ref, st_ref, acc)

    def last_body(dp_ref, w_ref, x_hbm, do_hbm, g_ref, gxh_hbm, sth_ref, gx_hbm, st_ref, x_s, do_s, gx_s, acc, sems):
        p = pl.program_id(0)
        last_rows = pl.ds(head, tm)
        loads = [pltpu.make_async_copy(x_hbm.at[last_rows], x_s, sems.at[0]),
                 pltpu.make_async_copy(do_hbm.at[last_rows], do_s, sems.at[1])]
        move = pltpu.make_async_copy(gxh_hbm, gx_hbm.at[pl.ds(0, head)], sems.at[2])
        store = pltpu.make_async_copy(gx_s, gx_hbm.at[last_rows], sems.at[3])

        @pl.when(p == 0)
        def _():
            st_ref[...] = sth_ref[...]
            for cp in loads + [move]:
                cp.start()

        @pl.when(p == N_PARTS - 1)
        def _():
            for cp in loads:
                cp.wait()

        tile_step(p, dp_ref, w_ref, x_s, do_s, g_ref, gx_s, st_ref, acc)

        @pl.when(p == N_PARTS - 1)
        def _():
            store.start()
            store.wait()
            move.wait()

    row = lambda i, p: (i, 0)
    fix = lambda i, p: (0, 0)
    hbm = pl.BlockSpec(memory_space=pl.ANY)
    tile = pltpu.VMEM((tm, D_MODEL), F32)
    gx_head, st_head, r2 = pl.pallas_call(
        body, grid=(head // tm, N_PARTS),
        in_specs=[
            pl.BlockSpec((1, tm, D_PART), lambda i, p: (p, i, 0)),
            pl.BlockSpec((2, D_MODEL, CHUNK), lambda i, p: (p, 0, 0)),
            pl.BlockSpec((tm, D_MODEL), row), pl.BlockSpec((tm, D_MODEL), row), pl.BlockSpec((1, D_MODEL), fix), hbm],
        out_specs=(pl.BlockSpec((tm, D_MODEL), row), pl.BlockSpec((SUBLANES, D_MODEL), fix), hbm),
        out_shape=(jax.ShapeDtypeStruct((head, D_MODEL), F32), jax.ShapeDtypeStruct((SUBLANES, D_MODEL), F32),
                   _chip_blocks_shape(sb_in, CHUNKS_PER_BLOCK)),
        scratch_shapes=[tile, pltpu.SemaphoreType.DMA((3,)), pltpu.SemaphoreType.DMA((3,))],
        compiler_params=_cp(ARB, ARB), name="input_grad",
    )(dproj, w12, x, do, ln_g, sb_in)
    fix1 = lambda p: (0, 0)
    gx, st = pl.pallas_call(
        last_body, grid=(N_PARTS,),
        in_specs=[
            pl.BlockSpec((1, tm, D_PART), lambda p: (p, head // tm, 0)),
            pl.BlockSpec((2, D_MODEL, CHUNK), lambda p: (p, 0, 0)),
            hbm, hbm, pl.BlockSpec((1, D_MODEL), fix1), hbm, pl.BlockSpec((SUBLANES, D_MODEL), fix1)],
        out_specs=(hbm, pl.BlockSpec((SUBLANES, D_MODEL), fix1)),
        out_shape=(jax.ShapeDtypeStruct((t, D_MODEL), F32), jax.ShapeDtypeStruct((SUBLANES, D_MODEL), F32)),
        scratch_shapes=[tile, tile, tile, tile, pltpu.SemaphoreType.DMA((4,))],
        compiler_params=_cp(ARB), name="input_grad_last_tile",
    )(dproj, w12, x, do, ln_g, gx_head, st_head)
    return gx, st, r2


def _w_in_grad(xn, dproj, small):
    t = xn.shape[0]
    small_shape = pltpu.VMEM(small.shape, F32)

    def body(xn_ref, dp_ref, sm_ref, o_ref, ob_ref, red_ref, acc_s, r0, r1, r2, send_sems, recv_sems):
        _allreduce_behind(pl.program_id(0), (0, 1, 3, N_PARTS - 1), sm_ref, acc_s, (r0, r1, r2), red_ref, send_sems, recv_sems)
        g = _mm_tn(xn_ref[...], dp_ref[0])
        for s in range(2):
            o_ref[s] = g[:, CHUNK * s:CHUNK * (s + 1)]
            ob_ref[s] = g[:, CHUNK * s:CHUNK * (s + 1)].astype(jnp.bfloat16)

    whole = pl.BlockSpec(small.shape, lambda p: (0, 0))
    pair = pl.BlockSpec((2, D_MODEL, CHUNK), lambda p: (p, 0, 0))
    return pl.pallas_call(
        body, grid=(N_PARTS,),
        in_specs=[pl.BlockSpec((t, D_MODEL), lambda p: (0, 0)),
                  pl.BlockSpec((1, t, D_PART), lambda p: (p, 0, 0)), whole],
        out_specs=(pair, pair, whole),
        out_shape=(jax.ShapeDtypeStruct((N_CHUNKS, D_MODEL, CHUNK), F32),
                   jax.ShapeDtypeStruct((N_CHUNKS, D_MODEL, CHUNK), jnp.bfloat16), jax.ShapeDtypeStruct(small.shape, F32)),
        scratch_shapes=[small_shape] * 4 + [pltpu.SemaphoreType.DMA((3,)), pltpu.SemaphoreType.DMA((3,))],
        compiler_params=_cp(ARB), name="w_in_grad",
    )(xn, dproj, small)


def _w_out_grad(yc, yl, dob):
    t = yc.shape[0]
    tk = 2048

    def body(yc_ref, yl_ref, do_ref, o_ref, ob_ref):
        j, kk = pl.program_id(0), pl.program_id(1)

        def accumulate(y_ref):
            @pl.when(kk == 0)
            def _():
                o_ref[...] = jnp.zeros_like(o_ref)

            o_ref[...] += _mm_tn(y_ref[...], do_ref[...])

            @pl.when(kk == t // tk - 1)
            def _():
                ob_ref[...] = o_ref[...].astype(jnp.bfloat16)

        pl.when(j == 0)(functools.partial(accumulate, yc_ref))
        pl.when(j == 1)(functools.partial(accumulate, yl_ref))

    def rows_of(half):
        return lambda j, kk: (jnp.where(j == half, kk, 0), 0)

    half = pl.BlockSpec((D_PART, D_MODEL), lambda j, kk: (j, 0))
    out, out_b = pl.pallas_call(
        body, grid=(2, t // tk),
        in_specs=[pl.BlockSpec((tk, D_PART), rows_of(0)), pl.BlockSpec((tk, D_PART), rows_of(1)),
                  pl.BlockSpec((tk, D_MODEL), lambda j, kk: (kk, 0))],
        out_specs=(half, half),
        out_shape=(jax.ShapeDtypeStruct((2 * D_PART, D_MODEL), F32), jax.ShapeDtypeStruct((2 * D_PART, D_MODEL), jnp.bfloat16)),
        compiler_params=_cp(ARB, ARB), name="w_out_grad",
    )(yc, yl, dob)
    blocks = (N_CHIPS, 2 * D_PART // N_CHIPS, D_MODEL)
    return out.reshape(blocks), out_b.reshape(blocks)


def _for_groups(n, fn, init, unroll=UNROLL, stores=(), descending=False):
    assert unroll % 2 == 0 and n % unroll == 0

    def trip(j, carry):
        held = None
        for uu in range(unroll):
            idx = j * unroll + uu
            carry, values = fn(idx, carry)
            if uu % 2 == 0:
                held = values
                continue
            low_group = n - 1 - idx if descending else idx - 1
            rows = pl.ds(pl.multiple_of(low_group * SUBLANES, 2 * SUBLANES), 2 * SUBLANES)
            pairs = zip(values, held) if descending else zip(held, values)
            for store, (lo, hi) in zip(stores, pairs, strict=True):
                store(rows, jnp.concatenate([lo, hi], axis=0).astype(MXU_DTYPE))
        return carry

    return lax.fori_loop(0, n // unroll, trip, init)


def _rows_of(ref, *lead, cols=slice(None)):
    def store(rows, value):
        ref[(*lead, rows, cols)] = value

    return store


def _pvb(pv_ref, r):
    return jnp.broadcast_to(pv_ref[r:r + 1, :], (SUBLANES, pv_ref.shape[1]))


def _conv3(pv_ref, u, u1, u2):
    return (_pvb(pv_ref, PV_CONV_W) * u2 + _pvb(pv_ref, PV_CONV_W + 1) * u1) + _pvb(pv_ref, PV_CONV_W + 2) * u


def _conv4(pv_ref, v, v1, v2, v3):
    return ((((_pvb(pv_ref, PV_LRU_W) * v3 + _pvb(pv_ref, PV_LRU_W + 1) * v2) + _pvb(pv_ref, PV_LRU_W + 2) * v1)
             + _pvb(pv_ref, PV_LRU_W + 3) * v) + _pvb(pv_ref, PV_LRU_B))


def _mixer_forward(proj, pvec, wai, w_out):
    t = proj.shape[1]
    tb = 512
    ng = tb // SUBLANES
    nt = t // tb
    lw = FWD_LW
    ns = D_PART // lw
    per_chunk = CHUNK // lw

    def body(bg_ref, cg_ref, xc_ref, gc_ref, xl_ref, gl_ref, pv_ref, wai_ref, wo_ref,
             yc_ref, yl_ref, h_ref, u_s, r_ref, ig_ref, wo4_ref,
             ucp_s, xlp_s, ls_s, hbuf_s, ub_s, gate_s, wob_s, local_sem, send_sems, recv_sems):
        _gather_w_out(pl.program_id(0) * nt + pl.program_id(1), ns * nt, wo_ref, wob_s, wo4_ref, local_sem, send_sems, recv_sems)

        @pl.when(pl.program_id(1) == 0)
        def _():
            ucp_s[...] = jnp.zeros_like(ucp_s)
            xlp_s[...] = jnp.zeros_like(xlp_s)
            hbuf_s[...] = jnp.zeros_like(hbuf_s)

        row = lax.broadcasted_iota(jnp.int32, (SUBLANES, lw), 0)
        ls_s[...] = RG_LRU_C * _log_sigmoid(_pvb(pv_ref, PV_LAM))

        def conv_group(g, carry):
            ucp, xlp = carry
            sl = pl.ds(pl.multiple_of(g * SUBLANES, SUBLANES), SUBLANES)
            uc = cg_ref[sl, :] * xc_ref[sl, :]
            v = _conv3(pv_ref, uc, _shift_down(uc, ucp, 1, row), _shift_down(uc, ucp, 2, row))
            yc = bg_ref[sl, :] * v
            rr = lax.rsqrt(_head_mean(yc * yc, CONV_HEAD) + RMS_EPS)
            gc = gc_ref[sl, :]
            zc = ((yc * rr) * _pvb(pv_ref, PV_CG)) * (gc * _sigmoid(gc))
            xl = xl_ref[sl, :]
            u = _conv4(pv_ref, xl, _shift_down(xl, xlp, 1, row), _shift_down(xl, xlp, 2, row), _shift_down(xl, xlp, 3, row))
            u_s[sl, :] = u
            return (uc, xl), (zc, u)

        ucp, xlp = _for_groups(ng, conv_group, (ucp_s[...], xlp_s[...]), unroll=2 * UNROLL,
                               stores=(_rows_of(yc_ref), _rows_of(ub_s)))
        ucp_s[...] = ucp
        xlp_s[...] = xlp

        gate_s[...] = _mm(ub_s[...], wai_ref[0])

        def lru_group(g, h_before):
            sl = pl.ds(pl.multiple_of(g * SUBLANES, SUBLANES), SUBLANES)
            u = u_s[sl, :]
            r = _sigmoid(gate_s[sl, 0:lw] + _pvb(pv_ref, PV_BA))
            ig = _sigmoid(gate_s[sl, lw:2 * lw] + _pvb(pv_ref, PV_BI))
            r_ref[sl, :] = r
            ig_ref[sl, :] = ig
            a, _, mult, _ = _decay(r, ls_s[...])
            A, B = _scan8_fwd(a, mult * (ig * u), row)
            h = B + A * jnp.broadcast_to(h_before[SUBLANES - 1:SUBLANES, :], (SUBLANES, lw))
            h_ref[sl, :] = h
            rr = lax.rsqrt(_head_mean(h * h, LRU_HEAD) + RMS_EPS)
            gl = gl_ref[sl, :]
            return h, (((h * rr) * _pvb(pv_ref, PV_LG)) * (gl * _sigmoid(gl)),)

        hbuf_s[...] = _for_groups(ng, lru_group, hbuf_s[...], unroll=2 * UNROLL, stores=(_rows_of(yl_ref),))

    def part(p):
        return pl.BlockSpec((None, tb, lw), lambda c, i: (2 * p + c // per_chunk, i, c % per_chunk))

    strip = pl.BlockSpec((tb, lw), lambda c, i: (i, c))
    small = pltpu.VMEM((SUBLANES, lw), F32)
    return pl.pallas_call(
        body, grid=(ns, nt),
        in_specs=[part(p) for p in range(N_PARTS)] + [
            pl.BlockSpec((PV_ROWS, lw), lambda c, i: (0, c)),
            pl.BlockSpec((1, lw, 2 * lw), lambda c, i: (c, 0, 0)),
            pl.BlockSpec(w_out.shape, lambda c, i: (0, 0))],
        out_specs=(strip,) * 6 + (pl.BlockSpec(memory_space=pl.ANY),),
        out_shape=(jax.ShapeDtypeStruct((t, D_PART), MXU_DTYPE),) * 2 + (jax.ShapeDtypeStruct((t, D_PART), F32),) * 4 + (
            jax.ShapeDtypeStruct((N_CHIPS,) + w_out.shape, MXU_DTYPE),),
        scratch_shapes=[small, small, small, small, pltpu.VMEM((tb, lw), MXU_DTYPE),
                        pltpu.VMEM((tb, 2 * lw), F32), pltpu.VMEM(w_out.shape, MXU_DTYPE),
                        pltpu.SemaphoreType.DMA, pltpu.SemaphoreType.DMA((6,)), pltpu.SemaphoreType.DMA((6,))],
        compiler_params=_cp(ARB, ARB), name="mixer_forward",
    )(proj, proj, proj, proj, proj, proj, pvec, wai, w_out)


def _mixer_backward(proj, h, u, r, ig, dy, pvec, wai, sb_out):
    t = proj.shape[1]
    tb = 1024
    ng = tb // SUBLANES
    nt = t // tb
    gpb = tb // SUBLANES

    def body(bg_ref, cg_ref, xc_ref, gc_ref, xl_ref, gl_ref, h_ref, u_ref, r_ref, ig_ref, dyc_ref, dyl_ref,
             cgh_ref, xch_ref, xlh_ref, hh_ref, pv_ref, wai_ref, so_ref,
             dp_ref, gw_ref, sv_ref, ro_ref,
             ls_s, ub_s, uce_s, xle_s, he_s, dgb_s, du_s, gbuf_s,
             acc_s, an_s, dvn_s, dun_s, send_sems, recv_sems):
        i = pl.program_id(1)
        first_block = i == nt - 1

        @pl.when((pl.program_id(0) == 0) & (i == 0))
        def _():
            for cp in _chip_block_copies(so_ref, ro_ref, 1, send_sems, recv_sems):
                cp.start()

        @pl.when((pl.program_id(0) == NS - 1) & (i == nt - 1))
        def _():
            for cp in _chip_block_copies(so_ref, ro_ref, 1, send_sems, recv_sems):
                cp.wait()

        @pl.when(i == 0)
        def _():
            acc_s[...] = jnp.zeros_like(acc_s)
            gw_ref[...] = jnp.zeros_like(gw_ref)
            an_s[...] = jnp.zeros_like(an_s)
            dvn_s[...] = jnp.zeros_like(dvn_s)
            dun_s[...] = jnp.zeros_like(dun_s)
            gbuf_s[...] = jnp.zeros_like(gbuf_s)

        row = lax.broadcasted_iota(jnp.int32, (SUBLANES, LW), 0)
        ls_s[...] = RG_LRU_C * _log_sigmoid(_pvb(pv_ref, PV_LAM))
        keep = jnp.where(first_block, 0.0, 1.0)
        uce_s[0:SUBLANES, :] = (cgh_ref[...] * xch_ref[...]) * keep
        xle_s[0:SUBLANES, :] = xlh_ref[...] * keep
        he_s[0:SUBLANES, :] = hh_ref[...] * keep
        xle_s[SUBLANES:SUBLANES + tb, :] = xl_ref[...]
        he_s[SUBLANES:SUBLANES + tb, :] = h_ref[...]

        uce_s[SUBLANES:SUBLANES + tb, :] = cg_ref[...] * xc_ref[...]

        def acc_add(k, v):
            acc_s[k] += v

        def main_group(gi, carry):
            a_next, dv_next, g_next = carry
            g = ng - 1 - gi
            r0 = pl.multiple_of(g * SUBLANES, SUBLANES)
            sl = pl.ds(r0, SUBLANES)
            sl_e = pl.ds(r0 + SUBLANES, SUBLANES)
            lsb = ls_s[...]
            u = u_ref[sl, :]
            r = r_ref[sl, :]
            ig = ig_ref[sl, :]
            a, e2, mult, inv_mult = _decay(r, lsb)
            gl = gl_ref[sl, :]
            sg = _sigmoid(gl)
            s_l = gl * sg
            h8 = he_s[sl_e, :]
            hprev = _shift_down(h8, he_s[sl, :], 1, row)
            rr = lax.rsqrt(_head_mean(h8 * h8, LRU_HEAD) + RMS_EPS)
            n = h8 * rr
            dz = dyl_ref[sl, :]
            lg = _pvb(pv_ref, PV_LG)
            acc_add(PV_LG, (dz * n) * s_l)
            p5 = ((dz * n) * lg) * (sg + s_l * (1.0 - sg))
            dn = (dz * lg) * s_l
            dh = rr * (dn - n * _head_mean(dn * n, LRU_HEAD))
            A, B = _scan8_rev(_shift_up(a, a_next, 1, row), dh, row)
            gg = B + A * jnp.broadcast_to(g_next[0:1, :], (SUBLANES, LW))
            da = gg * hprev
            iu = ig * u
            diu = gg * mult
            dla = da * a - (gg * iu) * (e2 * inv_mult)
            acc_add(PV_LAM, dla * r)
            dra = (dla * lsb) * (r * (1.0 - r))
            dia = (diu * u) * (ig * (1.0 - ig))
            acc_add(PV_BA, dra)
            acc_add(PV_BI, dia)
            du_s[sl, :] = diu * ig
            bg = bg_ref[sl, :]
            gc = gc_ref[sl, :]
            uc = uce_s[sl_e, :]
            ucp = uce_s[sl, :]
            uc1 = _shift_down(uc, ucp, 1, row)
            uc2 = _shift_down(uc, ucp, 2, row)
            v = _conv3(pv_ref, uc, uc1, uc2)
            yc = bg * v
            rrc = lax.rsqrt(_head_mean(yc * yc, CONV_HEAD) + RMS_EPS)
            nc = yc * rrc
            sgc = _sigmoid(gc)
            s_c = gc * sgc
            dzc = dyc_ref[sl, :]
            cgain = _pvb(pv_ref, PV_CG)
            acc_add(PV_CG, (dzc * nc) * s_c)
            p3 = ((dzc * nc) * cgain) * (sgc + s_c * (1.0 - sgc))
            dnc = (dzc * cgain) * s_c
            dyc = rrc * (dnc - nc * _head_mean(dnc * nc, CONV_HEAD))
            dv = dyc * bg
            duc = (_pvb(pv_ref, PV_CONV_W + 2) * dv + _pvb(pv_ref, PV_CONV_W + 1) * _shift_up(dv, dv_next, 1, row)
                   + _pvb(pv_ref, PV_CONV_W) * _shift_up(dv, dv_next, 2, row))
            acc_add(PV_CONV_W + 2, dv * uc)
            acc_add(PV_CONV_W + 1, dv * uc1)
            acc_add(PV_CONV_W, dv * uc2)
            return (a, dv, gg), (dyc * v, duc * xc_ref[sl, :], duc * cg_ref[sl, :], p3, p5, dra, dia, u)

        a_next, dv_next, g_next = _for_groups(
            ng, main_group, (an_s[...], dvn_s[...], gbuf_s[...]), descending=True,
            stores=(_rows_of(dp_ref, 0), _rows_of(dp_ref, 1), _rows_of(dp_ref, 2), _rows_of(dp_ref, 3), _rows_of(dp_ref, 5),
                    _rows_of(dgb_s, cols=slice(0, LW)), _rows_of(dgb_s, cols=slice(LW, 2 * LW)), _rows_of(ub_s)))
        an_s[...] = a_next
        dvn_s[...] = dv_next
        gbuf_s[...] = g_next

        dgb = dgb_s[...]
        du_s[...] += _mm_nt(dgb, wai_ref[0])
        gw_ref[0] += _mm_tn(ub_s[...], dgb)

        def lru_conv_group(gi, du_next):
            g = ng - 1 - gi
            r0 = pl.multiple_of(g * SUBLANES, SUBLANES)
            sl = pl.ds(r0, SUBLANES)
            du = du_s[sl, :]
            xl = xle_s[pl.ds(r0 + SUBLANES, SUBLANES), :]
            xlp = xle_s[sl, :]
            acc_add(PV_LRU_B, du)
            acc_add(PV_LRU_W + 3, du * xl)
            acc_add(PV_LRU_W + 2, du * _shift_down(xl, xlp, 1, row))
            acc_add(PV_LRU_W + 1, du * _shift_down(xl, xlp, 2, row))
            acc_add(PV_LRU_W, du * _shift_down(xl, xlp, 3, row))
            dxl = (((_pvb(pv_ref, PV_LRU_W + 3) * du + _pvb(pv_ref, PV_LRU_W + 2) * _shift_up(du, du_next, 1, row))
                    + _pvb(pv_ref, PV_LRU_W + 1) * _shift_up(du, du_next, 2, row))
                   + _pvb(pv_ref, PV_LRU_W) * _shift_up(du, du_next, 3, row))
            return du, (dxl,)

        dun_s[...] = _for_groups(ng, lru_conv_group, dun_s[...], descending=True, stores=(_rows_of(dp_ref, 4),))

        @pl.when(first_block)
        def _():
            sv_ref[...] = jnp.zeros_like(sv_ref)
            for k in range(N_ACC):
                tot = jnp.sum(acc_s[k], axis=0, keepdims=True)
                if k == PV_LAM:
                    tot = (RG_LRU_C * tot) / (1.0 + jnp.exp(pv_ref[PV_LAM:PV_LAM + 1, :]))
                sv_ref[k:k + 1, :] = tot

    def part(p):
        return pl.BlockSpec((None, tb, LW), lambda c, i: (2 * p + c // STRIPS_PER_CHUNK, nt - 1 - i, c % STRIPS_PER_CHUNK))

    def halo(p):
        return pl.BlockSpec((None, SUBLANES, LW), lambda c, i: (2 * p + c // STRIPS_PER_CHUNK,
                                                                jnp.maximum((nt - 1 - i) * gpb - 1, 0), c % STRIPS_PER_CHUNK))

    strip = pl.BlockSpec((tb, LW), lambda c, i: (nt - 1 - i, c))
    big = pltpu.VMEM((tb, LW), F32)
    big_e = pltpu.VMEM((tb + SUBLANES, LW), F32)
    small = pltpu.VMEM((SUBLANES, LW), F32)
    outs = pl.pallas_call(
        body, grid=(NS, nt),
        in_specs=[part(p) for p in range(N_PARTS)] + [
            strip, strip, strip, strip, strip, pl.BlockSpec((tb, LW), lambda c, i: (nt - 1 - i, NS + c)),
            halo(1), halo(2), halo(4),
            pl.BlockSpec((SUBLANES, LW), lambda c, i: (jnp.maximum((nt - 1 - i) * gpb - 1, 0), c)),
            pl.BlockSpec((PV_ROWS, LW), lambda c, i: (0, c)),
            pl.BlockSpec((1, LW, 2 * LW), lambda c, i: (c, 0, 0)),
            pl.BlockSpec(memory_space=pl.ANY)],
        out_specs=(pl.BlockSpec((N_PARTS, tb, LW), lambda c, i: (0, nt - 1 - i, c)),
                   pl.BlockSpec((1, LW, 2 * LW), lambda c, i: (c, 0, 0)),
                   pl.BlockSpec((PV_ROWS, LW), lambda c, i: (0, c)),
                   pl.BlockSpec(memory_space=pl.ANY)),
        out_shape=(jax.ShapeDtypeStruct((N_PARTS, t, D_PART), MXU_DTYPE),
                   jax.ShapeDtypeStruct((NS, LW, 2 * LW), F32), jax.ShapeDtypeStruct((PV_ROWS, D_PART), F32),
                   _chip_blocks_shape(sb_out, 1)),
        scratch_shapes=[small, pltpu.VMEM((tb, LW), MXU_DTYPE), big_e, big_e, big_e,
                        pltpu.VMEM((tb, 2 * LW), MXU_DTYPE), big, small,
                        pltpu.VMEM((N_ACC, SUBLANES, LW), F32), small, small, small,
                        pltpu.SemaphoreType.DMA((3,)), pltpu.SemaphoreType.DMA((3,))],
        compiler_params=_cp(ARB, ARB), name="mixer_backward",
    )(proj, proj, proj, proj, proj, proj, h, u, r, ig, dy, dy, proj, proj, proj, h, pvec, wai, sb_out)
    return outs


def _adamw(w, g, m, v):
    m = ADAM_B1 * m + (1.0 - ADAM_B1) * g
    v = ADAM_B2 * v + (1.0 - ADAM_B2) * (g * g)
    m_hat = m / (1.0 - ADAM_B1 ** ADAM_STEP)
    v_hat = v / (1.0 - ADAM_B2 ** ADAM_STEP)
    delta = -ADAM_LR * (m_hat / (jnp.sqrt(v_hat) + ADAM_EPS) + ADAM_WD * w)
    return delta, m, v


def _adamw_blocks(f_in, f_out, p_in, p_out):
    n_pieces = 4
    slab = 2 * SUBLANES

    def body(fi_hbm, fo_hbm, wi_hbm, mi_hbm, vi_hbm, wo_hbm, mo_hbm, vo_hbm,
             gi_hbm, di_hbm, nmi_hbm, nvi_hbm, do_hbm, nmo_hbm, nvo_hbm,
             g_in, g_out, wi, mi, vi, wo, mo, vo, load_sems, store_sems):
        blocks = ((g_in, fi_hbm, (wi, mi, vi), (wi_hbm, mi_hbm, vi_hbm), (di_hbm, nmi_hbm, nvi_hbm)),
                  (g_out, fo_hbm, (wo, mo, vo), (wo_hbm, mo_hbm, vo_hbm), (do_hbm, nmo_hbm, nvo_hbm)))
        loads, stores = [], []

        def start(src, dst, sems, group):
            group.append(pltpu.make_async_copy(src, dst, sems.at[len(group)]))
            group[-1].start()

        piece = lambda g, i: slice(g.shape[1] // n_pieces * i, g.shape[1] // n_pieces * (i + 1))
        for i in range(n_pieces):
            for g, f_hbm, p, p_hbm, _ in blocks:
                rows = piece(g, i)
                start(f_hbm.at[:, rows, :], g.at[:, rows, :], load_sems, loads)
                for s, s_hbm in zip(p, p_hbm):
                    start(s_hbm.at[rows, :], s.at[rows, :], load_sems, loads)
        per_piece = len(loads) // n_pieces
        for i in range(n_pieces):
            for cp in loads[per_piece * i:per_piece * (i + 1)]:
                cp.wait()
            for g, _, (w_s, m_s, v_s), _, outs in blocks:
                rows = piece(g, i)
                n, cols = g.shape[0], g.shape[2]

                def step(t, carry):
                    rs = pl.ds(pl.multiple_of(rows.start + slab * t, slab), slab)
                    for q in range(n):
                        cs = slice(cols * q, cols * (q + 1))
                        w_s[rs, cs], m_s[rs, cs], v_s[rs, cs] = _adamw(w_s[rs, cs], g[q, rs, :], m_s[rs, cs], v_s[rs, cs])
                    return carry

                lax.fori_loop(0, (rows.stop - rows.start) // slab, step, 0)
                for s, o_hbm in zip((w_s, m_s, v_s), outs):
                    start(s.at[rows, :], o_hbm.at[rows, :], store_sems, stores)
            rows, cols = piece(g_in, i), g_in.shape[2]
            for q in range(g_in.shape[0]):
                start(g_in.at[q, rows, :], gi_hbm.at[rows, cols * q:cols * (q + 1)], store_sems, stores)
        for cp in stores:
            cp.wait()

    w_i, w_o = p_in[0], p_out[0]
    n_in = f_in.shape[0]
    assert w_i.shape == (f_in.shape[1], n_in * f_in.shape[2]) and w_o.shape == (f_out.shape[1], f_out.shape[0] * f_out.shape[2])
    hbm = pl.BlockSpec(memory_space=pl.ANY)
    return pl.pallas_call(
        body, in_specs=[hbm] * 8, out_specs=(hbm,) * 7,
        out_shape=(jax.ShapeDtypeStruct(w_i.shape, F32),) * 4 + (jax.ShapeDtypeStruct(w_o.shape, F32),) * 3,
        scratch_shapes=[pltpu.VMEM(f_in.shape, F32), pltpu.VMEM(f_out.shape, F32)]
        + [pltpu.VMEM(w_i.shape, F32)] * 3 + [pltpu.VMEM(w_o.shape, F32)] * 3
        + [pltpu.SemaphoreType.DMA((n_pieces * 8,)), pltpu.SemaphoreType.DMA((n_pieces * (6 + n_in),))],
        compiler_params=_cp(), name="adamw_blocks",
    )(f_in, f_out, *p_in, *p_out)


def _adam_small(ws, ms, vs, gs):
    n = len(ws)

    def body(*refs):
        w_r, m_r, v_r, g_r = refs[0:n], refs[n:2 * n], refs[2 * n:3 * n], refs[3 * n:4 * n]
        d_o, m_o, v_o = refs[4 * n:5 * n], refs[5 * n:6 * n], refs[6 * n:7 * n]
        for j in range(n):
            d_o[j][...], m_o[j][...], v_o[j][...] = _adamw(w_r[j][...], g_r[j][...], m_r[j][...], v_r[j][...])

    vm = pl.BlockSpec(memory_space=pltpu.VMEM)
    shapes = tuple(jax.ShapeDtypeStruct(w.shape, F32) for w in ws)
    outs = pl.pallas_call(
        body, in_specs=[vm] * (4 * n), out_specs=(vm,) * (3 * n), out_shape=shapes * 3,
        compiler_params=_cp(), name="adam_small",
    )(*ws, *ms, *vs, *gs)
    return outs[0:n], outs[n:2 * n], outs[2 * n:3 * n]


def _block_diag_strips(w, lw):
    heads = lw // LRU_HEAD
    w4 = w.reshape(D_PART // lw, heads, LRU_HEAD, LRU_HEAD)
    rows = [jnp.pad(w4[:, hh], ((0, 0), (0, 0), (LRU_HEAD * hh, lw - LRU_HEAD * (hh + 1)))) for hh in range(heads)]
    return jnp.concatenate(rows, axis=1)


def _gate_matrices(w_a, w_i, lw):
    return jnp.concatenate([_block_diag_strips(w_a, lw), _block_diag_strips(w_i, lw)], axis=2).astype(MXU_DTYPE)


def _strip_diag_blocks(g):
    g5 = g.reshape(NS, HEADS_PER_STRIP, LRU_HEAD, HEADS_PER_STRIP, LRU_HEAD)
    return jnp.stack([g5[:, hh, :, hh, :] for hh in range(HEADS_PER_STRIP)], axis=1).reshape(NS * HEADS_PER_STRIP, LRU_HEAD, LRU_HEAD)


def kernel(x, ln_g, w_in, conv_w, lru_conv_w, lru_conv_b, w_a, b_a, w_i, b_i, lam, conv_out_g, lru_out_g, w_out, final_g, loss_target, m_ln_g, m_w_in, m_conv_w, m_lru_conv_w, m_lru_conv_b, m_w_a, m_b_a, m_w_i, m_b_i, m_lam, m_conv_out_g, m_lru_out_g, m_w_out, m_final_g, v_ln_g, v_w_in, v_conv_w, v_lru_conv_w, v_lru_conv_b, v_w_a, v_b_a, v_w_i, v_b_i, v_lam, v_conv_out_g, v_lru_out_g, v_w_out, v_final_g):
    xi, yi, ci = lax.axis_index("x"), lax.axis_index("y"), lax.axis_index("c")
    k = 2 * xi + yi
    t = x.shape[1]
    x2 = x.reshape(t, D_MODEL)
    tgt2 = loss_target.reshape(t, D_MODEL)
    row = lambda a: a.reshape(1, -1)

    small = jnp.concatenate([conv_w, lru_conv_w, jnp.zeros((1, conv_w.shape[1]), F32)], axis=0)
    proj, xn, w12, sm4 = _gather_in_projection(x2, row(ln_g), w_in, small)
    convs = jnp.transpose(sm4, (1, 0, 2)).reshape(SUBLANES, D_PART)
    pvec = jnp.concatenate(
        [convs[0:7], row(lru_conv_b), row(b_a), row(b_i), row(lam), row(conv_out_g), row(lru_out_g),
         jnp.zeros((PV_ROWS - N_ACC, D_PART), F32)], axis=0)
    wai = _gate_matrices(w_a, w_i, LW)

    c_arr = jnp.reshape(ci, (1,)).astype(jnp.int32)
    yc, yl, h, u, r, ig, wo4 = _mixer_forward(proj, pvec, _gate_matrices(w_a, w_i, FWD_LW), w_out)
    wo = wo4.reshape(2 * D_PART, D_MODEL)
    do, dob, dy, st_out = _out_projection_loss(yc, yl, x2, tgt2, wo, row(final_g))
    go4, go4b = _w_out_grad(yc, yl, dob)
    s_out, sb_out = _add_sibling_halves(go4, go4b, c_arr, "add_sibling_halves_out")
    dproj, g_wai, svec, r2o = _mixer_backward(proj, h, u, r, ig, dy, pvec, wai, sb_out)
    gwa = _strip_diag_blocks(g_wai[:, :, 0:LW]).reshape(LRU_HEAD, D_PART)
    gwi = _strip_diag_blocks(g_wai[:, :, LW:2 * LW]).reshape(LRU_HEAD, D_PART)
    g12, g12b, red = _w_in_grad(xn, dproj, jnp.concatenate([svec, st_out, gwa, gwi], axis=0))
    s_in, sb_in = _add_sibling_halves(g12, g12b, c_arr, "add_sibling_halves_in")
    grad_x, st_in, r2i = _input_grad(dproj, w12, x2, do, row(ln_g), sb_in)
    f_in, f_out, red_ln = _finish_gradients(s_in, r2i, s_out, r2o, st_in)
    r_out = PV_ROWS
    r_wa = PV_ROWS + SUBLANES
    r_wi = r_wa + LRU_HEAD
    loss = red[r_out + 1, 0]

    g_w_in, d_w_in, nm_w_in, nv_w_in, d_w_out, nm_w_out, nv_w_out = _adamw_blocks(
        f_in, f_out, (w_in, m_w_in, v_w_in), (w_out, m_w_out, v_w_out))
    g_w_out = f_out[0]

    ncol = conv_w.shape[1]
    conv_cols = lax.dynamic_slice(red, (0, k * ncol), (SUBLANES, ncol))
    g_small = {
        "ln_g": red_ln[0], "conv_w": conv_cols[0:3], "lru_conv_w": conv_cols[3:7], "lru_conv_b": red[PV_LRU_B],
        "w_a": red[r_wa:r_wa + LRU_HEAD].reshape(w_a.shape), "b_a": red[PV_BA],
        "w_i": red[r_wi:r_wi + LRU_HEAD].reshape(w_i.shape), "b_i": red[PV_BI], "lam": red[PV_LAM],
        "conv_out_g": red[PV_CG], "lru_out_g": red[PV_LG], "final_g": red[r_out],
    }
    w_small = {"ln_g": ln_g, "conv_w": conv_w, "lru_conv_w": lru_conv_w, "lru_conv_b": lru_conv_b, "w_a": w_a, "b_a": b_a,
               "w_i": w_i, "b_i": b_i, "lam": lam, "conv_out_g": conv_out_g, "lru_out_g": lru_out_g, "final_g": final_g}
    m_small = {"ln_g": m_ln_g, "conv_w": m_conv_w, "lru_conv_w": m_lru_conv_w, "lru_conv_b": m_lru_conv_b, "w_a": m_w_a,
               "b_a": m_b_a, "w_i": m_w_i, "b_i": m_b_i, "lam": m_lam, "conv_out_g": m_conv_out_g,
               "lru_out_g": m_lru_out_g, "final_g": m_final_g}
    v_small = {"ln_g": v_ln_g, "conv_w": v_conv_w, "lru_conv_w": v_lru_conv_w, "lru_conv_b": v_lru_conv_b, "w_a": v_w_a,
               "b_a": v_b_a, "w_i": v_w_i, "b_i": v_b_i, "lam": v_lam, "conv_out_g": v_conv_out_g,
               "lru_out_g": v_lru_out_g, "final_g": v_final_g}
    names = list(w_small)
    as2d = lambda a: a.reshape(1, -1) if a.ndim == 1 else a
    d_s, m_s, v_s = _adam_small([as2d(w_small[n]) for n in names], [as2d(m_small[n]) for n in names],
                                [as2d(v_small[n]) for n in names], [as2d(g_small[n]) for n in names])
    back = lambda n, a: a.reshape(w_small[n].shape)
    grads = {n: g_small[n] for n in names}
    deltas = {n: back(n, a) for n, a in zip(names, d_s)}
    new_m = {n: back(n, a) for n, a in zip(names, m_s)}
    new_v = {n: back(n, a) for n, a in zip(names, v_s)}
    grads["w_in"], deltas["w_in"], new_m["w_in"], new_v["w_in"] = g_w_in, d_w_in, nm_w_in, nv_w_in
    grads["w_out"], deltas["w_out"], new_m["w_out"], new_v["w_out"] = g_w_out, d_w_out, nm_w_out, nv_w_out

    order = ["ln_g", "w_in", "conv_w", "lru_conv_w", "lru_conv_b", "w_a", "b_a", "w_i", "b_i", "lam", "conv_out_g",
             "lru_out_g", "w_out", "final_g"]
    return (loss, grad_x.reshape(x.shape), *[grads[n] for n in order], *[deltas[n] for n in order],
            *[new_m[n] for n in order], *[new_v[n] for n in order])
```

```python
import functools

import jax
import jax.numpy as jnp
from jax import lax
from jax.experimental import pallas as pl
from jax.experimental.pallas import tpu as pltpu

F32 = jnp.float32
MXU_DTYPE = jnp.bfloat16

D_MODEL = 1024
D_PART = 1024
N_PARTS = 6
CHUNK = 512
CHUNKS_PER_BLOCK = 3
N_CHUNKS = 12
N_CHIPS = 4
SUBLANES = 8
LANES = 128
LW = 256
FWD_LW = 512
UNROLL = 8
NS = D_PART // LW
STRIPS_PER_CHUNK = CHUNK // LW
CONV_HEAD = 128
LRU_HEAD = 64
HEADS_PER_STRIP = LW // LRU_HEAD
RMS_EPS = 1e-6
RG_LRU_C = 8.0
ADAM_LR = 0.001
ADAM_B1 = 0.9
ADAM_B2 = 0.999
ADAM_EPS = 1e-08
ADAM_WD = 0.01
ADAM_STEP = 10

PV_CONV_W = 0
PV_LRU_W = 3
PV_LRU_B = 7
PV_BA = 8
PV_BI = 9
PV_LAM = 10
PV_CG = 11
PV_LG = 12
PV_ROWS = 16
N_ACC = 13

SLAB = 128
MESH = pl.DeviceIdType.MESH
VMEM_LIMIT = 56 * 1024 * 1024
ARB = "arbitrary"


def _cp(*sem, **kw):
    return pltpu.CompilerParams(dimension_semantics=sem or None, vmem_limit_bytes=VMEM_LIMIT, **kw)


def _mm(a, b):
    return jnp.dot(a, b, preferred_element_type=F32)


def _mm_nt(a, b):
    return lax.dot_general(a, b, (((1,), (1,)), ((), ())), preferred_element_type=F32)


def _mm_tn(a, b):
    return lax.dot_general(a, b, (((0,), (0,)), ((), ())), preferred_element_type=F32)


def _sigmoid(x):
    return 0.5 * jnp.tanh(0.5 * x) + 0.5


def _log_sigmoid(x):
    z = jnp.exp(-jnp.abs(x))
    u = 1.0 + z
    log1p = jnp.where(u == 1.0, z, jnp.log(u) * z / (u - 1.0))
    return jnp.minimum(x, 0.0) - log1p


def _head_mean(z, head):
    out = []
    for k in range(z.shape[1] // LANES):
        zk = z[:, LANES * k:LANES * (k + 1)]
        if head == LANES:
            m = jnp.sum(zk, axis=-1, keepdims=True) * (1.0 / head)
            out.append(jnp.broadcast_to(m, zk.shape))
        else:
            lo = lax.broadcasted_iota(jnp.int32, zk.shape, 1) < head
            s_lo = jnp.sum(jnp.where(lo, zk, 0.0), axis=-1, keepdims=True)
            s_hi = jnp.sum(jnp.where(lo, 0.0, zk), axis=-1, keepdims=True)
            out.append(jnp.where(lo, s_lo, s_hi) * (1.0 / head))
    return jnp.concatenate(out, axis=1)


def _shift_down(cur, prev, d, row):
    return pltpu.roll(jnp.where(row < SUBLANES - d, cur, prev), d, 0)


def _shift_up(cur, nxt, d, row):
    return pltpu.roll(jnp.where(row >= d, cur, nxt), SUBLANES - d, 0)


def _scan8_fwd(a, b, row):
    A, B = a, b
    for d in (1, 2, 4):
        m = row >= d
        a_s = jnp.where(m, pltpu.roll(A, d, 0), 1.0)
        b_s = jnp.where(m, pltpu.roll(B, d, 0), 0.0)
        B = A * b_s + B
        A = A * a_s
    return A, B


def _scan8_rev(a, b, row):
    A, B = a, b
    for d in (1, 2, 4):
        m = row < SUBLANES - d
        a_s = jnp.where(m, pltpu.roll(A, SUBLANES - d, 0), 1.0)
        b_s = jnp.where(m, pltpu.roll(B, SUBLANES - d, 0), 0.0)
        B = A * b_s + B
        A = A * a_s
    return A, B


def _decay(r, ls8):
    la = r * ls8
    a = jnp.exp(la)
    e2 = a * a
    em = -jnp.tanh(la) * (1.0 + e2)
    inv_mult = lax.rsqrt(em)
    return a, e2, em * inv_mult, inv_mult


def _mesh_pos():
    x, y, c = lax.axis_index("x"), lax.axis_index("y"), lax.axis_index("c")
    chips = [(1 - x, y), (x, 1 - y), (1 - x, 1 - y)]
    return x, y, c, chips


def _gather_in_projection(x, ln_g, w_in, small):
    t = x.shape[0]
    rb_x = 512
    rb_mm = 2048
    n_mm = t // rb_mm
    half = w_in.shape[0] // 2

    def body(x_hbm, g_ref, wi_ref, sm_ref, proj_hbm, xn_ref, w12_ref, sm4_ref,
             xbuf, obuf, x_sems, o_sems, send_sems, recv_sems):
        x_, y_, c, chips = _mesh_pos()
        k = 2 * x_ + y_
        sib = (x_, y_, 1 - c)
        sm4_ref[k] = sm_ref[...]

        def remote(ref, sem, to):
            return pltpu.make_async_remote_copy(src_ref=ref, dst_ref=ref, send_sem=send_sems.at[sem],
                                                recv_sem=recv_sems.at[sem], device_id=to, device_id_type=MESH)

        def chunk_of(chip, s):
            return CHUNKS_PER_BLOCK * (2 * chip[0] + chip[1]) + s

        def piece(q, core, first=0, rows=half):
            return w12_ref.at[q, pl.ds(pl.multiple_of(half * core + first, SUBLANES * 2), rows), :]

        nbr_x, nbr_y, diagonal = chips
        quarter = half // 2
        DIAG = [(0, 0, half, 0), (1, 0, quarter, 0), (1, quarter, quarter, 1), (2, 0, half, 1)]
        ici = lambda m, s: 2 * s + m
        dgn = lambda j: 6 + j
        to_sib = 10
        sml = lambda m: 20 + m

        sends = []
        for s in range(CHUNKS_PER_BLOCK):
            w12_ref[chunk_of((x_, y_), s)] = wi_ref[:, CHUNK * s:CHUNK * (s + 1)].astype(MXU_DTYPE)
            for m, chip in enumerate((nbr_x, nbr_y)):
                sends.append(remote(piece(chunk_of((x_, y_), s), c), ici(m, s), (*chip, c)))
                sends[-1].start()
        for m, chip in enumerate(chips):
            sends.append(remote(sm4_ref.at[k], sml(m), (*chip, c)))
            sends[-1].start()

        def x_copy(rb, slot):
            return pltpu.make_async_copy(x_hbm.at[pl.ds(rb * rb_x, rb_x), :], xbuf.at[slot], x_sems.at[slot])

        x_copy(0, 0).start()
        for rb in range(t // rb_x):
            slot = rb % 2
            x_copy(rb, slot).wait()
            if rb + 1 < t // rb_x:
                x_copy(rb + 1, 1 - slot).start()

            def norm_slab(sl, carry, rb=rb, slot=slot):
                xf = xbuf[slot, pl.ds(pl.multiple_of(sl * SLAB, SLAB), SLAB), :]
                r = lax.rsqrt(jnp.mean(xf * xf, axis=-1, keepdims=True) + RMS_EPS)
                xn_ref[pl.ds(pl.multiple_of(rb * rb_x + sl * SLAB, SLAB), SLAB), :] = ((xf * r) * g_ref[...]).astype(MXU_DTYPE)
                return carry

            lax.fori_loop(0, rb_x // SLAB, norm_slab, 0)

        def out_copy(q, i):
            return pltpu.make_async_copy(obuf.at[i], proj_hbm.at[q, pl.ds(pl.multiple_of(i * rb_mm, rb_mm), rb_mm), :],
                                         o_sems.at[i])

        def project(q, very_first):
            def row_block(i, carry):
                if not very_first:
                    out_copy(q, i).wait()
                obuf[i] = _mm(xn_ref[pl.ds(pl.multiple_of(i * rb_mm, rb_mm), rb_mm), :], w12_ref[q])
                out_copy(q, i).start()
                return carry

            lax.fori_loop(0, n_mm, row_block, 0)

        for s in range(CHUNKS_PER_BLOCK):
            project(chunk_of((x_, y_), s), very_first=(s == 0))

        steps = []
        for s in range(CHUNKS_PER_BLOCK):
            for m, chip in enumerate((nbr_x, nbr_y)):
                onward = [(first, rows, dgn(j), chips[via]) for j, (cs, first, rows, via) in enumerate(DIAG)
                          if cs == s and via == 1 - m]
                steps.append((chunk_of(chip, s), [(0, half, ici(m, s))], onward))
        for s in range(CHUNKS_PER_BLOCK):
            steps.append((chunk_of(diagonal, s), [(first, rows, dgn(j)) for j, (cs, first, rows, _) in enumerate(DIAG) if cs == s], []))

        def project_when_whole(step):
            q, pieces, _ = step
            for first, rows, sem in pieces:
                remote(piece(q, 1 - c, first, rows), to_sib + sem, sib).wait_recv()
            project(q, very_first=False)

        passed = []
        for j, (q, pieces, onward) in enumerate(steps):
            for first, rows, sem in pieces:
                remote(piece(q, c, first, rows), sem, sib).wait_recv()
            for first, rows, sem, chip in onward:
                passed.append(remote(piece(q, c, first, rows), sem, (*chip, c)))
                passed[-1].start()
            for first, rows, sem in pieces:
                passed.append(remote(piece(q, c, first, rows), to_sib + sem, sib))
                passed[-1].start()
            if j > 0:
                project_when_whole(steps[j - 1])
        project_when_whole(steps[-1])

        for m, chip in enumerate(chips):
            remote(sm4_ref.at[2 * chip[0] + chip[1]], sml(m), sib).wait_recv()
        for cp in sends + passed:
            cp.wait_send()
        for i in range(n_mm):
            out_copy(0, i).wait()

    vm = pl.BlockSpec(memory_space=pltpu.VMEM)
    hbm = pl.BlockSpec(memory_space=pl.ANY)
    n_sems = 23
    return pl.pallas_call(
        body,
        out_shape=(jax.ShapeDtypeStruct((N_CHUNKS, t, CHUNK), F32), jax.ShapeDtypeStruct((t, D_MODEL), MXU_DTYPE),
                   jax.ShapeDtypeStruct((N_CHUNKS, w_in.shape[0], CHUNK), MXU_DTYPE),
                   jax.ShapeDtypeStruct((N_CHIPS,) + small.shape, F32)),
        in_specs=[hbm, vm, vm, vm], out_specs=(hbm, vm, vm, vm),
        scratch_shapes=[pltpu.VMEM((2, rb_x, D_MODEL), F32), pltpu.VMEM((n_mm, rb_mm, CHUNK), F32),
                        pltpu.SemaphoreType.DMA((2,)), pltpu.SemaphoreType.DMA((n_mm,)),
                        pltpu.SemaphoreType.DMA((n_sems,)), pltpu.SemaphoreType.DMA((n_sems,))],
        compiler_params=_cp(), name="gather_in_projection",
    )(x, ln_g, w_in, small)


def _allreduce_behind(step, when, in_ref, acc_s, rbufs, out_ref, send_sems, recv_sems):
    x, y, c, _ = _mesh_pos()
    peers = [(x, y, 1 - c), (1 - x, y, c), (x, 1 - y, c)]

    def exchange(ph):
        return pltpu.make_async_remote_copy(src_ref=acc_s, dst_ref=rbufs[ph], send_sem=send_sems.at[ph],
                                            recv_sem=recv_sems.at[ph], device_id=peers[ph], device_id_type=MESH)

    @pl.when(step == when[0])
    def _():
        acc_s[...] = in_ref[...]
        exchange(0).start()

    for ph in (1, 2):
        @pl.when(step == when[ph])
        def _(ph=ph):
            exchange(ph - 1).wait()
            acc_s[...] = acc_s[...] + rbufs[ph - 1][...]
            exchange(ph).start()

    @pl.when(step == when[3])
    def _():
        exchange(2).wait()
        out_ref[...] = acc_s[...] + rbufs[2][...]


def _add_sibling_halves(g, gb, c_arr, name):
    n, rows, cols = g.shape
    half = rows // 2
    per = 2
    steps = n // per

    def body(c_ref, g_ref, gb_hbm, o_ref, ob_ref, rbuf, send_sems, recv_sems):
        q = pl.program_id(0)
        x, y, c, _ = _mesh_pos()
        theirs = pl.ds(pl.multiple_of(half * (1 - c), half), half)

        def copy(j):
            blocks = pl.ds(j * per, per)
            return pltpu.make_async_remote_copy(src_ref=gb_hbm.at[blocks, theirs, :], dst_ref=rbuf.at[blocks], send_sem=send_sems.at[j],
                                                recv_sem=recv_sems.at[j], device_id=(x, y, 1 - c), device_id_type=MESH)

        @pl.when(q == 0)
        def _():
            for j in range(steps):
                copy(j).start()

        copy(q).wait_recv()
        s = g_ref[...] + rbuf[pl.ds(q * per, per)].astype(F32)
        o_ref[...] = s
        ob_ref[...] = s.astype(jnp.bfloat16)

        @pl.when(q == steps - 1)
        def _():
            for j in range(steps):
                copy(j).wait_send()

    blk = pl.BlockSpec((per, half, cols), lambda q, c_ref: (q, 0, 0))
    return pl.pallas_call(
        body, out_shape=(jax.ShapeDtypeStruct((n, half, cols), F32), jax.ShapeDtypeStruct((n, half, cols), jnp.bfloat16)),
        grid_spec=pltpu.PrefetchScalarGridSpec(
            num_scalar_prefetch=1, grid=(steps,),
            in_specs=[pl.BlockSpec((per, half, cols), lambda q, c_ref: (q, c_ref[0], 0)), pl.BlockSpec(memory_space=pl.ANY)],
            out_specs=(blk, blk),
            scratch_shapes=[pltpu.VMEM((n, half, cols), jnp.bfloat16), pltpu.SemaphoreType.DMA((steps,)),
                            pltpu.SemaphoreType.DMA((steps,))]),
        compiler_params=_cp(ARB), name=name,
    )(c_arr, g, gb)


def _chip_block_copies(s_ref, r_ref, n_sub, send_sems, recv_sems):
    x, y, c, chips = _mesh_pos()
    cps = []
    for m, chip in enumerate(chips):
        kk = 2 * chip[0] + chip[1]
        cps.append(pltpu.make_async_remote_copy(
            src_ref=s_ref.at[pl.ds(n_sub * kk, n_sub)], dst_ref=r_ref.at[m],
            send_sem=send_sems.at[m], recv_sem=recv_sems.at[m], device_id=(*chip, c), device_id_type=MESH))
    return cps


def _gather_w_out(step, n_steps, wo_ref, wob_s, wo4_ref, local_sem, send_sems, recv_sems):
    x, y, c, chips = _mesh_pos()
    sib = (x, y, 1 - c)
    half = wo_ref.shape[0] // 2

    def rows(core):
        return pl.ds(pl.multiple_of(half * core, half), half)

    def block_half(chip, core):
        return wo4_ref.at[2 * chip[0] + chip[1], rows(core), :]

    def remote(src, dst, sem, to):
        return pltpu.make_async_remote_copy(src_ref=src, dst_ref=dst, send_sem=send_sems.at[sem], recv_sem=recv_sems.at[sem],
                                            device_id=to, device_id_type=MESH)

    local = pltpu.make_async_copy(wob_s, wo4_ref.at[2 * x + y], local_sem)
    ici = [remote(wob_s.at[rows(c), :], block_half((x, y), c), m, (*chip, c)) for m, chip in enumerate(chips)]
    fwd = [remote(block_half(chip, c), block_half(chip, c), 3 + m, sib) for m, chip in enumerate(chips)]

    @pl.when(step == 0)
    def _():
        wob_s[...] = wo_ref[...].astype(MXU_DTYPE)
        local.start()
        for cp in ici:
            cp.start()

    @pl.when(step == n_steps // 2)
    def _():
        for m, chip in enumerate(chips):
            remote(block_half(chip, c), block_half(chip, c), m, sib).wait_recv()
            fwd[m].start()

    @pl.when(step == n_steps - 1)
    def _():
        for m, chip in enumerate(chips):
            remote(block_half(chip, 1 - c), block_half(chip, 1 - c), 3 + m, sib).wait_recv()
        for cp in ici + fwd:
            cp.wait_send()
        local.wait()


def _chip_blocks_shape(s, n_sub):
    return jax.ShapeDtypeStruct((3, n_sub) + s.shape[1:], s.dtype)


def _finish_gradients(s_in, r_in, s_out, r_out, v):
    n_dev = 8
    n_in, n_out = r_in.shape[1], r_out.shape[1]

    def body(si_hbm, ri_hbm, so_hbm, ro_hbm, v_ref, fi_hbm, fo_hbm, tot_ref,
             a_in, b_in, a_out, b_out, slots, load_sems, store_sems, send_sems, recv_sems):
        x, y, c, _ = _mesh_pos()
        k = 2 * x + y
        sib = (x, y, 1 - c)
        me = 4 * x + 2 * y + c
        loads = [pltpu.make_async_copy(si_hbm.at[pl.ds(n_in * k, n_in)], a_in, load_sems.at[0]),
                 pltpu.make_async_copy(ri_hbm, b_in, load_sems.at[1]),
                 pltpu.make_async_copy(so_hbm.at[pl.ds(n_out * k, n_out)], a_out, load_sems.at[2]),
                 pltpu.make_async_copy(ro_hbm, b_out, load_sems.at[3])]
        for cp in loads:
            cp.start()
        slots[me] = v_ref[...]

        def remote(src, dst, sem, to):
            return pltpu.make_async_remote_copy(src_ref=src, dst_ref=dst, send_sem=send_sems.at[sem],
                                                recv_sem=recv_sems.at[sem], device_id=to, device_id_type=MESH)

        small = []
        for d in range(1, n_dev):
            peer = (1 - x if d & 4 else x, 1 - y if d & 2 else y, 1 - c if d & 1 else c)
            small.append(remote(slots.at[me], slots.at[me], d - 1, peer))
            small[-1].start()
        for cp in loads:
            cp.wait()
        big = []
        for j, (a, b, f_hbm) in enumerate(((a_in, b_in, fi_hbm), (a_out, b_out, fo_hbm))):
            a[...] = ((a[...] + b[0].astype(F32)) + b[1].astype(F32)) + b[2].astype(F32)
            half = a.shape[1]
            mine = f_hbm.at[:, pl.ds(pl.multiple_of(half * c, half), half), :]
            big.append(pltpu.make_async_copy(a, mine, store_sems.at[j]))
            big.append(remote(a, mine, n_dev - 1 + j, sib))
        for cp in big:
            cp.start()
        for cp in small + big:
            cp.wait()
        total = slots[0]
        for dev in range(1, n_dev):
            total = total + slots[dev]
        tot_ref[...] = total

    hbm = pl.BlockSpec(memory_space=pl.ANY)
    vm = pl.BlockSpec(memory_space=pltpu.VMEM)
    full = lambda s, n: (n, 2 * s.shape[1], s.shape[2])
    return pl.pallas_call(
        body,
        out_shape=(jax.ShapeDtypeStruct(full(s_in, n_in), F32), jax.ShapeDtypeStruct(full(s_out, n_out), F32),
                   jax.ShapeDtypeStruct(v.shape, F32)),
        in_specs=[hbm, hbm, hbm, hbm, vm], out_specs=(hbm, hbm, vm),
        scratch_shapes=[pltpu.VMEM((n_in,) + s_in.shape[1:], F32), pltpu.VMEM(r_in.shape, r_in.dtype),
                        pltpu.VMEM((n_out,) + s_out.shape[1:], F32), pltpu.VMEM(r_out.shape, r_out.dtype),
                        pltpu.VMEM((n_dev,) + v.shape, F32), pltpu.SemaphoreType.DMA((4,)), pltpu.SemaphoreType.DMA((2,)),
                        pltpu.SemaphoreType.DMA((n_dev + 1,)), pltpu.SemaphoreType.DMA((n_dev + 1,))],
        compiler_params=_cp(), name="finish_gradients",
    )(s_in, r_in, s_out, r_out, v)


def _out_projection_loss(yc, yl, x, target, wo, final_g):
    t = x.shape[0]
    tm = 512

    def body(yc_ref, yl_ref, x_ref, t_ref, wo_ref, fg_ref, do_ref, dob_ref, dy_ref, st_ref, y_wo):
        @pl.when(pl.program_id(0) == 0)
        def _():
            st_ref[...] = jnp.zeros_like(st_ref)

        y_wo[...] = _mm(yc_ref[...], wo_ref[0:D_PART, :]) + _mm(yl_ref[...], wo_ref[D_PART:2 * D_PART, :])

        def norm_loss_slab(s, carry):
            g_sum, loss_sum = carry
            rows = pl.ds(pl.multiple_of(s * SLAB, SLAB), SLAB)
            o = x_ref[rows, :] + y_wo[rows, :]
            r2 = lax.rsqrt(jnp.mean(o * o, axis=-1, keepdims=True) + RMS_EPS)
            ohat = o * r2
            fg = fg_ref[...]
            diff = ohat * fg - t_ref[rows, :]
            dout = diff * (1.0 / D_MODEL)
            gp = dout * fg
            do = r2 * (gp - ohat * jnp.mean(gp * ohat, axis=-1, keepdims=True))
            do_ref[rows, :] = do
            dob_ref[rows, :] = do.astype(MXU_DTYPE)
            loss = 0.5 * jnp.sum(jnp.sum(diff * diff, axis=-1, keepdims=True) * (1.0 / D_MODEL), axis=0, keepdims=True)
            return g_sum + jnp.sum(dout * ohat, axis=0, keepdims=True), loss_sum + loss

        g_sum, loss_sum = lax.fori_loop(0, tm // SLAB, norm_loss_slab,
                                        (jnp.zeros((1, D_MODEL), F32), jnp.zeros((1, 1), F32)))
        st_ref[0:1, :] += g_sum
        st_ref[1:2, :] += jnp.broadcast_to(loss_sum, (1, D_MODEL))
        dy_ref[...] = _mm_nt(dob_ref[...], wo_ref[...])

    row = lambda i: (i, 0)
    fix = lambda i: (0, 0)
    return pl.pallas_call(
        body, grid=(t // tm,),
        in_specs=[pl.BlockSpec((tm, D_PART), row), pl.BlockSpec((tm, D_PART), row),
                  pl.BlockSpec((tm, D_MODEL), row), pl.BlockSpec((tm, D_MODEL), row),
                  pl.BlockSpec((2 * D_PART, D_MODEL), fix), pl.BlockSpec((1, D_MODEL), fix)],
        out_specs=(pl.BlockSpec((tm, D_MODEL), row), pl.BlockSpec((tm, D_MODEL), row),
                   pl.BlockSpec((tm, 2 * D_PART), row), pl.BlockSpec((SUBLANES, D_MODEL), fix)),
        out_shape=(jax.ShapeDtypeStruct((t, D_MODEL), F32), jax.ShapeDtypeStruct((t, D_MODEL), MXU_DTYPE),
                   jax.ShapeDtypeStruct((t, 2 * D_PART), F32), jax.ShapeDtypeStruct((SUBLANES, D_MODEL), F32)),
        scratch_shapes=[pltpu.VMEM((tm, D_MODEL), F32)],
        compiler_params=_cp(ARB), name="out_projection_loss",
    )(yc, yl, x, target, wo, final_g)


def _input_grad(dproj, w12, x, do, ln_g, sb_in):
    t = x.shape[0]
    tm = 1024

    def body(dp_ref, w_ref, x_ref, do_ref, g_ref, s_ref, gx_ref, st_ref, r_ref, acc, send_sems, recv_sems):
        i, p = pl.program_id(0), pl.program_id(1)

        @pl.when((i == 0) & (p == 0))
        def _():
            st_ref[...] = jnp.zeros_like(st_ref)
            for cp in _chip_block_copies(s_ref, r_ref, CHUNKS_PER_BLOCK, send_sems, recv_sems):
                cp.start()

        @pl.when((i == t // tm - 1) & (p == N_PARTS - 1))
        def _():
            for cp in _chip_block_copies(s_ref, r_ref, CHUNKS_PER_BLOCK, send_sems, recv_sems):
                cp.wait()

        @pl.when(p == 0)
        def _():
            acc[...] = jnp.zeros_like(acc)

        acc[...] += _mm_nt(dp_ref[0], jnp.concatenate([w_ref[0], w_ref[1]], axis=1))

        @pl.when(p == N_PARTS - 1)
        def _():
            def norm_bwd_slab(s, g_sum):
                rows = pl.ds(pl.multiple_of(s * SLAB, SLAB), SLAB)
                xf = x_ref[rows, :]
                r = lax.rsqrt(jnp.mean(xf * xf, axis=-1, keepdims=True) + RMS_EPS)
                xhat = xf * r
                dxn = acc[rows, :]
                dxh = dxn * g_ref[...]
                gx_ref[rows, :] = do_ref[rows, :] + r * (dxh - xhat * jnp.mean(dxh * xhat, axis=-1, keepdims=True))
                return g_sum + jnp.sum(dxn * xhat, axis=0, keepdims=True)

            st_ref[0:1, :] += lax.fori_loop(0, tm // SLAB, norm_bwd_slab, jnp.zeros((1, D_MODEL), F32))

    row = lambda i, p: (i, 0)
    fix = lambda i, p: (0, 0)
    return pl.pallas_call(
        body, grid=(t // tm, N_PARTS),
        in_specs=[
            pl.BlockSpec((1, tm, D_PART), lambda i, p: (p, i, 0)),
            pl.BlockSpec((2, D_MODEL, CHUNK), lambda i, p: (p, 0, 0)),
            pl.BlockSpec((tm, D_MODEL), row), pl.BlockSpec((tm, D_MODEL), row), pl.BlockSpec((1, D_MODEL), fix),
            pl.BlockSpec(memory_space=pl.ANY)],
        out_specs=(pl.BlockSpec((tm, D_MODEL), row), pl.BlockSpec((SUBLANES, D_MODEL), fix),
                   pl.BlockSpec(memory_space=pl.ANY)),
        out_shape=(jax.ShapeDtypeStruct((t, D_MODEL), F32), jax.ShapeDtypeStruct((SUBLANES, D_MODEL), F32),
                   _chip_blocks_shape(sb_in, CHUNKS_PER_BLOCK)),
        scratch_shapes=[pltpu.VMEM((tm, D_MODEL), F32), pltpu.SemaphoreType.DMA((3,)), pltpu.SemaphoreType.DMA((3,))],
        compiler_params=_cp(ARB, ARB), name="input_grad",
    )(dproj, w12, x, do, ln_g, sb_in)


def _w_in_grad(xn, dproj, small):
    t = xn.shape[0]
    small_shape = pltpu.VMEM(small.shape, F32)

    def body(xn_ref, dp_ref, sm_ref, o_ref, ob_ref, red_ref, acc_s, r0, r1, r2, send_sems, recv_sems):
        _allreduce_behind(pl.program_id(0), (0, 1, 3, N_PARTS - 1), sm_ref, acc_s, (r0, r1, r2), red_ref, send_sems, recv_sems)
        g = _mm_tn(xn_ref[...], dp_ref[0])
        for s in range(2):
            o_ref[s] = g[:, CHUNK * s:CHUNK * (s + 1)]
            ob_ref[s] = g[:, CHUNK * s:CHUNK * (s + 1)].astype(jnp.bfloat16)

    whole = pl.BlockSpec(small.shape, lambda p: (0, 0))
    pair = pl.BlockSpec((2, D_MODEL, CHUNK), lambda p: (p, 0, 0))
    return pl.pallas_call(
        body, grid=(N_PARTS,),
        in_specs=[pl.BlockSpec((t, D_MODEL), lambda p: (0, 0)),
                  pl.BlockSpec((1, t, D_PART), lambda p: (p, 0, 0)), whole],
        out_specs=(pair, pair, whole),
        out_shape=(jax.ShapeDtypeStruct((N_CHUNKS, D_MODEL, CHUNK), F32),
                   jax.ShapeDtypeStruct((N_CHUNKS, D_MODEL, CHUNK), jnp.bfloat16), jax.ShapeDtypeStruct(small.shape, F32)),
        scratch_shapes=[small_shape] * 4 + [pltpu.SemaphoreType.DMA((3,)), pltpu.SemaphoreType.DMA((3,))],
        compiler_params=_cp(ARB), name="w_in_grad",
    )(xn, dproj, small)


def _w_out_grad(yc, yl, dob):
    t = yc.shape[0]
    tk = 2048

    def body(yc_ref, yl_ref, do_ref, o_ref, ob_ref):
        j, kk = pl.program_id(0), pl.program_id(1)

        def accumulate(y_ref):
            @pl.when(kk == 0)
            def _():
                o_ref[...] = jnp.zeros_like(o_ref)

            o_ref[...] += _mm_tn(y_ref[...], do_ref[...])

            @pl.when(kk == t // tk - 1)
            def _():
                ob_ref[...] = o_ref[...].astype(jnp.bfloat16)

        pl.when(j == 0)(functools.partial(accumulate, yc_ref))
        pl.when(j == 1)(functools.partial(accumulate, yl_ref))

    def rows_of(half):
        return lambda j, kk: (jnp.where(j == half, kk, 0), 0)

    half = pl.BlockSpec((D_PART, D_MODEL), lambda j, kk: (j, 0))
    out, out_b = pl.pallas_call(
        body, grid=(2, t // tk),
        in_specs=[pl.BlockSpec((tk, D_PART), rows_of(0)), pl.BlockSpec((tk, D_PART), rows_of(1)),
                  pl.BlockSpec((tk, D_MODEL), lambda j, kk: (kk, 0))],
        out_specs=(half, half),
        out_shape=(jax.ShapeDtypeStruct((2 * D_PART, D_MODEL), F32), jax.ShapeDtypeStruct((2 * D_PART, D_MODEL), jnp.bfloat16)),
        compiler_params=_cp(ARB, ARB), name="w_out_grad",
    )(yc, yl, dob)
    blocks = (N_CHIPS, 2 * D_PART // N_CHIPS, D_MODEL)
    return out.reshape(blocks), out_b.reshape(blocks)


def _for_groups(n, fn, init, unroll=UNROLL, stores=(), descending=False):
    assert unroll % 2 == 0 and n % unroll == 0

    def trip(j, carry):
        held = None
        for uu in range(unroll):
            idx = j * unroll + uu
            carry, values = fn(idx, carry)
            if uu % 2 == 0:
                held = values
                continue
            low_group = n - 1 - idx if descending else idx - 1
            rows = pl.ds(pl.multiple_of(low_group * SUBLANES, 2 * SUBLANES), 2 * SUBLANES)
            pairs = zip(values, held) if descending else zip(held, values)
            for store, (lo, hi) in zip(stores, pairs, strict=True):
                store(rows, jnp.concatenate([lo, hi], axis=0).astype(MXU_DTYPE))
        return carry

    return lax.fori_loop(0, n // unroll, trip, init)


def _rows_of(ref, *lead, cols=slice(None)):
    def store(rows, value):
        ref[(*lead, rows, cols)] = value

    return store


def _pvb(pv_ref, r):
    return jnp.broadcast_to(pv_ref[r:r + 1, :], (SUBLANES, pv_ref.shape[1]))


def _conv3(pv_ref, u, u1, u2):
    return (_pvb(pv_ref, PV_CONV_W) * u2 + _pvb(pv_ref, PV_CONV_W + 1) * u1) + _pvb(pv_ref, PV_CONV_W + 2) * u


def _conv4(pv_ref, v, v1, v2, v3):
    return ((((_pvb(pv_ref, PV_LRU_W) * v3 + _pvb(pv_ref, PV_LRU_W + 1) * v2) + _pvb(pv_ref, PV_LRU_W + 2) * v1)
             + _pvb(pv_ref, PV_LRU_W + 3) * v) + _pvb(pv_ref, PV_LRU_B))


def _mixer_forward(proj, pvec, wai, w_out):
    t = proj.shape[1]
    tb = 512
    ng = tb // SUBLANES
    nt = t // tb
    lw = FWD_LW
    ns = D_PART // lw
    per_chunk = CHUNK // lw

    def body(bg_ref, cg_ref, xc_ref, gc_ref, xl_ref, gl_ref, pv_ref, wai_ref, wo_ref,
             yc_ref, yl_ref, h_ref, u_s, r_ref, ig_ref, wo4_ref,
             ucp_s, xlp_s, ls_s, hbuf_s, ub_s, gate_s, wob_s, local_sem, send_sems, recv_sems):
        _gather_w_out(pl.program_id(0) * nt + pl.program_id(1), ns * nt, wo_ref, wob_s, wo4_ref, local_sem, send_sems, recv_sems)

        @pl.when(pl.program_id(1) == 0)
        def _():
            ucp_s[...] = jnp.zeros_like(ucp_s)
            xlp_s[...] = jnp.zeros_like(xlp_s)
            hbuf_s[...] = jnp.zeros_like(hbuf_s)

        row = lax.broadcasted_iota(jnp.int32, (SUBLANES, lw), 0)
        ls_s[...] = RG_LRU_C * _log_sigmoid(_pvb(pv_ref, PV_LAM))

        def conv_group(g, carry):
            ucp, xlp = carry
            sl = pl.ds(pl.multiple_of(g * SUBLANES, SUBLANES), SUBLANES)
            uc = cg_ref[sl, :] * xc_ref[sl, :]
            v = _conv3(pv_ref, uc, _shift_down(uc, ucp, 1, row), _shift_down(uc, ucp, 2, row))
            yc = bg_ref[sl, :] * v
            rr = lax.rsqrt(_head_mean(yc * yc, CONV_HEAD) + RMS_EPS)
            gc = gc_ref[sl, :]
            zc = ((yc * rr) * _pvb(pv_ref, PV_CG)) * (gc * _sigmoid(gc))
            xl = xl_ref[sl, :]
            u = _conv4(pv_ref, xl, _shift_down(xl, xlp, 1, row), _shift_down(xl, xlp, 2, row), _shift_down(xl, xlp, 3, row))
            u_s[sl, :] = u
            return (uc, xl), (zc, u)

        ucp, xlp = _for_groups(ng, conv_group, (ucp_s[...], xlp_s[...]), unroll=2 * UNROLL,
                               stores=(_rows_of(yc_ref), _rows_of(ub_s)))
        ucp_s[...] = ucp
        xlp_s[...] = xlp

        gate_s[...] = _mm(ub_s[...], wai_ref[0])

        def lru_group(g, h_before):
            sl = pl.ds(pl.multiple_of(g * SUBLANES, SUBLANES), SUBLANES)
            u = u_s[sl, :]
            r = _sigmoid(gate_s[sl, 0:lw] + _pvb(pv_ref, PV_BA))
            ig = _sigmoid(gate_s[sl, lw:2 * lw] + _pvb(pv_ref, PV_BI))
            r_ref[sl, :] = r
            ig_ref[sl, :] = ig
            a, _, mult, _ = _decay(r, ls_s[...])
            A, B = _scan8_fwd(a, mult * (ig * u), row)
            h = B + A * jnp.broadcast_to(h_before[SUBLANES - 1:SUBLANES, :], (SUBLANES, lw))
            h_ref[sl, :] = h
            rr = lax.rsqrt(_head_mean(h * h, LRU_HEAD) + RMS_EPS)
            gl = gl_ref[sl, :]
            return h, (((h * rr) * _pvb(pv_ref, PV_LG)) * (gl * _sigmoid(gl)),)

        hbuf_s[...] = _for_groups(ng, lru_group, hbuf_s[...], unroll=2 * UNROLL, stores=(_rows_of(yl_ref),))

    def part(p):
        return pl.BlockSpec((None, tb, lw), lambda c, i: (2 * p + c // per_chunk, i, c % per_chunk))

    strip = pl.BlockSpec((tb, lw), lambda c, i: (i, c))
    small = pltpu.VMEM((SUBLANES, lw), F32)
    return pl.pallas_call(
        body, grid=(ns, nt),
        in_specs=[part(p) for p in range(N_PARTS)] + [
            pl.BlockSpec((PV_ROWS, lw), lambda c, i: (0, c)),
            pl.BlockSpec((1, lw, 2 * lw), lambda c, i: (c, 0, 0)),
            pl.BlockSpec(w_out.shape, lambda c, i: (0, 0))],
        out_specs=(strip,) * 6 + (pl.BlockSpec(memory_space=pl.ANY),),
        out_shape=(jax.ShapeDtypeStruct((t, D_PART), MXU_DTYPE),) * 2 + (jax.ShapeDtypeStruct((t, D_PART), F32),) * 4 + (
            jax.ShapeDtypeStruct((N_CHIPS,) + w_out.shape, MXU_DTYPE),),
        scratch_shapes=[small, small, small, small, pltpu.VMEM((tb, lw), MXU_DTYPE),
                        pltpu.VMEM((tb, 2 * lw), F32), pltpu.VMEM(w_out.shape, MXU_DTYPE),
                        pltpu.SemaphoreType.DMA, pltpu.SemaphoreType.DMA((6,)), pltpu.SemaphoreType.DMA((6,))],
        compiler_params=_cp(ARB, ARB), name="mixer_forward",
    )(proj, proj, proj, proj, proj, proj, pvec, wai, w_out)


def _mixer_backward(proj, h, u, r, ig, dy, pvec, wai, go, gob):
    n_blocks, half, cols = go.shape[0], go.shape[1] // 2, go.shape[2]
    t = proj.shape[1]
    tb = 1024
    ng = tb // SUBLANES
    nt = t // tb
    gpb = tb // SUBLANES

    def body(bg_ref, cg_ref, xc_ref, gc_ref, xl_ref, gl_ref, h_ref, u_ref, r_ref, ig_ref, dyc_ref, dyl_ref,
             cgh_ref, xch_ref, xlh_ref, hh_ref, pv_ref, wai_ref, go_hbm, gob_hbm,
             dp_ref, gw_ref, sv_ref, ro_ref, so_hbm,
             ls_s, ub_s, uce_s, xle_s, he_s, dgb_s, du_s, gbuf_s,
             acc_s, an_s, dvn_s, dun_s, sum_s, arrival_s, sumb_s, send_sems, recv_sems, sibling_sems):
        i = pl.program_id(1)
        first_block = i == nt - 1

        step = pl.program_id(0) * nt + i
        mesh_x, mesh_y, core, _ = _mesh_pos()
        mine = pl.ds(pl.multiple_of(half * core, half), half)
        theirs = pl.ds(pl.multiple_of(half * (1 - core), half), half)
        load = pltpu.make_async_copy(go_hbm.at[:, mine, :], sum_s, sibling_sems.at[0])
        swap = pltpu.make_async_remote_copy(src_ref=gob_hbm.at[:, theirs, :], dst_ref=arrival_s, send_sem=sibling_sems.at[1],
                                            recv_sem=sibling_sems.at[2], device_id=(mesh_x, mesh_y, 1 - core), device_id_type=MESH)
        store = pltpu.make_async_copy(sum_s, so_hbm, sibling_sems.at[3])

        @pl.when(step == 0)
        def _():
            load.start()
            swap.start()

        @pl.when(step == 1)
        def _():
            load.wait()
            swap.wait()
            sum_s[...] = sum_s[...] + arrival_s[...].astype(F32)
            sumb_s[...] = sum_s[...].astype(jnp.bfloat16)
            store.start()
            for cp in _chip_block_copies(sumb_s, ro_ref, 1, send_sems, recv_sems):
                cp.start()

        @pl.when(step == NS * nt - 1)
        def _():
            store.wait()
            for cp in _chip_block_copies(sumb_s, ro_ref, 1, send_sems, recv_sems):
                cp.wait()

        @pl.when(i == 0)
        def _():
            acc_s[...] = jnp.zeros_like(acc_s)
            gw_ref[...] = jnp.zeros_like(gw_ref)
            an_s[...] = jnp.zeros_like(an_s)
            dvn_s[...] = jnp.zeros_like(dvn_s)
            dun_s[...] = jnp.zeros_like(dun_s)
            gbuf_s[...] = jnp.zeros_like(gbuf_s)

        row = lax.broadcasted_iota(jnp.int32, (SUBLANES, LW), 0)
        ls_s[...] = RG_LRU_C * _log_sigmoid(_pvb(pv_ref, PV_LAM))
        keep = jnp.where(first_block, 0.0, 1.0)
        uce_s[0:SUBLANES, :] = (cgh_ref[...] * xch_ref[...]) * keep
        xle_s[0:SUBLANES, :] = xlh_ref[...] * keep
        he_s[0:SUBLANES, :] = hh_ref[...] * keep
        xle_s[SUBLANES:SUBLANES + tb, :] = xl_ref[...]
        he_s[SUBLANES:SUBLANES + tb, :] = h_ref[...]

        uce_s[SUBLANES:SUBLANES + tb, :] = cg_ref[...] * xc_ref[...]

        def acc_add(k, v):
            acc_s[k] += v

        def main_group(gi, carry):
            a_next, dv_next, g_next = carry
            g = ng - 1 - gi
            r0 = pl.multiple_of(g * SUBLANES, SUBLANES)
            sl = pl.ds(r0, SUBLANES)
            sl_e = pl.ds(r0 + SUBLANES, SUBLANES)
            lsb = ls_s[...]
            u = u_ref[sl, :]
            r = r_ref[sl, :]
            ig = ig_ref[sl, :]
            a, e2, mult, inv_mult = _decay(r, lsb)
            gl = gl_ref[sl, :]
            sg = _sigmoid(gl)
            s_l = gl * sg
            h8 = he_s[sl_e, :]
            hprev = _shift_down(h8, he_s[sl, :], 1, row)
            rr = lax.rsqrt(_head_mean(h8 * h8, LRU_HEAD) + RMS_EPS)
            n = h8 * rr
            dz = dyl_ref[sl, :]
            lg = _pvb(pv_ref, PV_LG)
            acc_add(PV_LG, (dz * n) * s_l)
            p5 = ((dz * n) * lg) * (sg + s_l * (1.0 - sg))
            dn = (dz * lg) * s_l
            dh = rr * (dn - n * _head_mean(dn * n, LRU_HEAD))
            A, B = _scan8_rev(_shift_up(a, a_next, 1, row), dh, row)
            gg = B + A * jnp.broadcast_to(g_next[0:1, :], (SUBLANES, LW))
            da = gg * hprev
            iu = ig * u
            diu = gg * mult
            dla = da * a - (gg * iu) * (e2 * inv_mult)
            acc_add(PV_LAM, dla * r)
            dra = (dla * lsb) * (r * (1.0 - r))
            dia = (diu * u) * (ig * (1.0 - ig))
            acc_add(PV_BA, dra)
            acc_add(PV_BI, dia)
            du_s[sl, :] = diu * ig
            bg = bg_ref[sl, :]
            gc = gc_ref[sl, :]
            uc = uce_s[sl_e, :]
            ucp = uce_s[sl, :]
            uc1 = _shift_down(uc, ucp, 1, row)
            uc2 = _shift_down(uc, ucp, 2, row)
            v = _conv3(pv_ref, uc, uc1, uc2)
            yc = bg * v
            rrc = lax.rsqrt(_head_mean(yc * yc, CONV_HEAD) + RMS_EPS)
            nc = yc * rrc
            sgc = _sigmoid(gc)
            s_c = gc * sgc
            dzc = dyc_ref[sl, :]
            cgain = _pvb(pv_ref, PV_CG)
            acc_add(PV_CG, (dzc * nc) * s_c)
            p3 = ((dzc * nc) * cgain) * (sgc + s_c * (1.0 - sgc))
            dnc = (dzc * cgain) * s_c
            dyc = rrc * (dnc - nc * _head_mean(dnc * nc, CONV_HEAD))
            dv = dyc * bg
            duc = (_pvb(pv_ref, PV_CONV_W + 2) * dv + _pvb(pv_ref, PV_CONV_W + 1) * _shift_up(dv, dv_next, 1, row)
                   + _pvb(pv_ref, PV_CONV_W) * _shift_up(dv, dv_next, 2, row))
            acc_add(PV_CONV_W + 2, dv * uc)
            acc_add(PV_CONV_W + 1, dv * uc1)
            acc_add(PV_CONV_W, dv * uc2)
            return (a, dv, gg), (dyc * v, duc * xc_ref[sl, :], duc * cg_ref[sl, :], p3, p5, dra, dia, u)

        a_next, dv_next, g_next = _for_groups(
            ng, main_group, (an_s[...], dvn_s[...], gbuf_s[...]), descending=True,
            stores=(_rows_of(dp_ref, 0), _rows_of(dp_ref, 1), _rows_of(dp_ref, 2), _rows_of(dp_ref, 3), _rows_of(dp_ref, 5),
                    _rows_of(dgb_s, cols=slice(0, LW)), _rows_of(dgb_s, cols=slice(LW, 2 * LW)), _rows_of(ub_s)))
        an_s[...] = a_next
        dvn_s[...] = dv_next
        gbuf_s[...] = g_next

        dgb = dgb_s[...]
        du_s[...] += _mm_nt(dgb, wai_ref[0])
        gw_ref[0] += _mm_tn(ub_s[...], dgb)

        def lru_conv_group(gi, du_next):
            g = ng - 1 - gi
            r0 = pl.multiple_of(g * SUBLANES, SUBLANES)
            sl = pl.ds(r0, SUBLANES)
            du = du_s[sl, :]
            xl = xle_s[pl.ds(r0 + SUBLANES, SUBLANES), :]
            xlp = xle_s[sl, :]
            acc_add(PV_LRU_B, du)
            acc_add(PV_LRU_W + 3, du * xl)
            acc_add(PV_LRU_W + 2, du * _shift_down(xl, xlp, 1, row))
            acc_add(PV_LRU_W + 1, du * _shift_down(xl, xlp, 2, row))
            acc_add(PV_LRU_W, du * _shift_down(xl, xlp, 3, row))
            dxl = (((_pvb(pv_ref, PV_LRU_W + 3) * du + _pvb(pv_ref, PV_LRU_W + 2) * _shift_up(du, du_next, 1, row))
                    + _pvb(pv_ref, PV_LRU_W + 1) * _shift_up(du, du_next, 2, row))
                   + _pvb(pv_ref, PV_LRU_W) * _shift_up(du, du_next, 3, row))
            return du, (dxl,)

        dun_s[...] = _for_groups(ng, lru_conv_group, dun_s[...], descending=True, stores=(_rows_of(dp_ref, 4),))

        @pl.when(first_block)
        def _():
            sv_ref[...] = jnp.zeros_like(sv_ref)
            for k in range(N_ACC):
                tot = jnp.sum(acc_s[k], axis=0, keepdims=True)
                if k == PV_LAM:
                    tot = (RG_LRU_C * tot) / (1.0 + jnp.exp(pv_ref[PV_LAM:PV_LAM + 1, :]))
                sv_ref[k:k + 1, :] = tot

    def part(p):
        return pl.BlockSpec((None, tb, LW), lambda c, i: (2 * p + c // STRIPS_PER_CHUNK, nt - 1 - i, c % STRIPS_PER_CHUNK))

    def halo(p):
        return pl.BlockSpec((None, SUBLANES, LW), lambda c, i: (2 * p + c // STRIPS_PER_CHUNK,
                                                                jnp.maximum((nt - 1 - i) * gpb - 1, 0), c % STRIPS_PER_CHUNK))

    strip = pl.BlockSpec((tb, LW), lambda c, i: (nt - 1 - i, c))
    big = pltpu.VMEM((tb, LW), F32)
    big_e = pltpu.VMEM((tb + SUBLANES, LW), F32)
    small = pltpu.VMEM((SUBLANES, LW), F32)
    outs = pl.pallas_call(
        body, grid=(NS, nt),
        in_specs=[part(p) for p in range(N_PARTS)] + [
            strip, strip, strip, strip, strip, pl.BlockSpec((tb, LW), lambda c, i: (nt - 1 - i, NS + c)),
            halo(1), halo(2), halo(4),
            pl.BlockSpec((SUBLANES, LW), lambda c, i: (jnp.maximum((nt - 1 - i) * gpb - 1, 0), c)),
            pl.BlockSpec((PV_ROWS, LW), lambda c, i: (0, c)),
            pl.BlockSpec((1, LW, 2 * LW), lambda c, i: (c, 0, 0)),
            pl.BlockSpec(memory_space=pl.ANY), pl.BlockSpec(memory_space=pl.ANY)],
        out_specs=(pl.BlockSpec((N_PARTS, tb, LW), lambda c, i: (0, nt - 1 - i, c)),
                   pl.BlockSpec((1, LW, 2 * LW), lambda c, i: (c, 0, 0)),
                   pl.BlockSpec((PV_ROWS, LW), lambda c, i: (0, c)),
                   pl.BlockSpec(memory_space=pl.ANY), pl.BlockSpec(memory_space=pl.ANY)),
        out_shape=(jax.ShapeDtypeStruct((N_PARTS, t, D_PART), MXU_DTYPE),
                   jax.ShapeDtypeStruct((NS, LW, 2 * LW), F32), jax.ShapeDtypeStruct((PV_ROWS, D_PART), F32),
                   jax.ShapeDtypeStruct((3, 1, half, cols), jnp.bfloat16),
                   jax.ShapeDtypeStruct((n_blocks, half, cols), F32)),
        scratch_shapes=[small, pltpu.VMEM((tb, LW), MXU_DTYPE), big_e, big_e, big_e,
                        pltpu.VMEM((tb, 2 * LW), MXU_DTYPE), big, small,
                        pltpu.VMEM((N_ACC, SUBLANES, LW), F32), small, small, small,
                        pltpu.VMEM((n_blocks, half, cols), F32), pltpu.VMEM((n_blocks, half, cols), jnp.bfloat16),
                        pltpu.VMEM((n_blocks, half, cols), jnp.bfloat16),
                        pltpu.SemaphoreType.DMA((3,)), pltpu.SemaphoreType.DMA((3,)), pltpu.SemaphoreType.DMA((4,))],
        compiler_params=_cp(ARB, ARB), name="mixer_backward",
    )(proj, proj, proj, proj, proj, proj, h, u, r, ig, dy, dy, proj, proj, proj, h, pvec, wai, go, gob)
    return outs


def _adamw(w, g, m, v):
    m = ADAM_B1 * m + (1.0 - ADAM_B1) * g
    v = ADAM_B2 * v + (1.0 - ADAM_B2) * (g * g)
    m_hat = m / (1.0 - ADAM_B1 ** ADAM_STEP)
    v_hat = v / (1.0 - ADAM_B2 ** ADAM_STEP)
    delta = -ADAM_LR * (m_hat / (jnp.sqrt(v_hat) + ADAM_EPS) + ADAM_WD * w)
    return delta, m, v


def _adamw_blocks(f_in, f_out, p_in, p_out):
    n_pieces = 4
    slab = 2 * SUBLANES

    def body(fi_hbm, fo_hbm, wi_hbm, mi_hbm, vi_hbm, wo_hbm, mo_hbm, vo_hbm,
             gi_hbm, di_hbm, nmi_hbm, nvi_hbm, do_hbm, nmo_hbm, nvo_hbm,
             g_in, g_out, wi, mi, vi, wo, mo, vo, load_sems, store_sems):
        blocks = ((g_in, fi_hbm, (wi, mi, vi), (wi_hbm, mi_hbm, vi_hbm), (di_hbm, nmi_hbm, nvi_hbm)),
                  (g_out, fo_hbm, (wo, mo, vo), (wo_hbm, mo_hbm, vo_hbm), (do_hbm, nmo_hbm, nvo_hbm)))
        loads, stores = [], []

        def start(src, dst, sems, group):
            group.append(pltpu.make_async_copy(src, dst, sems.at[len(group)]))
            group[-1].start()

        piece = lambda g, i: slice(g.shape[1] // n_pieces * i, g.shape[1] // n_pieces * (i + 1))
        for i in range(n_pieces):
            for g, f_hbm, p, p_hbm, _ in blocks:
                rows = piece(g, i)
                start(f_hbm.at[:, rows, :], g.at[:, rows, :], load_sems, loads)
                for s, s_hbm in zip(p, p_hbm):
                    start(s_hbm.at[rows, :], s.at[rows, :], load_sems, loads)
        per_piece = len(loads) // n_pieces
        for i in range(n_pieces):
            for cp in loads[per_piece * i:per_piece * (i + 1)]:
                cp.wait()
            for g, _, (w_s, m_s, v_s), _, outs in blocks:
                rows = piece(g, i)
                n, cols = g.shape[0], g.shape[2]

                def step(t, carry):
                    rs = pl.ds(pl.multiple_of(rows.start + slab * t, slab), slab)
                    for q in range(n):
                        cs = slice(cols * q, cols * (q + 1))
                        w_s[rs, cs], m_s[rs, cs], v_s[rs, cs] = _adamw(w_s[rs, cs], g[q, rs, :], m_s[rs, cs], v_s[rs, cs])
                    return carry

                lax.fori_loop(0, (rows.stop - rows.start) // slab, step, 0)
                for s, o_hbm in zip((w_s, m_s, v_s), outs):
                    start(s.at[rows, :], o_hbm.at[rows, :], store_sems, stores)
            rows, cols = piece(g_in, i), g_in.shape[2]
            for q in range(g_in.shape[0]):
                start(g_in.at[q, rows, :], gi_hbm.at[rows, cols * q:cols * (q + 1)], store_sems, stores)
        for cp in stores:
            cp.wait()

    w_i, w_o = p_in[0], p_out[0]
    n_in = f_in.shape[0]
    assert w_i.shape == (f_in.shape[1], n_in * f_in.shape[2]) and w_o.shape == (f_out.shape[1], f_out.shape[0] * f_out.shape[2])
    hbm = pl.BlockSpec(memory_space=pl.ANY)
    return pl.pallas_call(
        body, in_specs=[hbm] * 8, out_specs=(hbm,) * 7,
        out_shape=(jax.ShapeDtypeStruct(w_i.shape, F32),) * 4 + (jax.ShapeDtypeStruct(w_o.shape, F32),) * 3,
        scratch_shapes=[pltpu.VMEM(f_in.shape, F32), pltpu.VMEM(f_out.shape, F32)]
        + [pltpu.VMEM(w_i.shape, F32)] * 3 + [pltpu.VMEM(w_o.shape, F32)] * 3
        + [pltpu.SemaphoreType.DMA((n_pieces * 8,)), pltpu.SemaphoreType.DMA((n_pieces * (6 + n_in),))],
        compiler_params=_cp(), name="adamw_blocks",
    )(f_in, f_out, *p_in, *p_out)


def _adam_small(ws, ms, vs, gs):
    n = len(ws)

    def body(*refs):
        w_r, m_r, v_r, g_r = refs[0:n], refs[n:2 * n], refs[2 * n:3 * n], refs[3 * n:4 * n]
        d_o, m_o, v_o = refs[4 * n:5 * n], refs[5 * n:6 * n], refs[6 * n:7 * n]
        for j in range(n):
            d_o[j][...], m_o[j][...], v_o[j][...] = _adamw(w_r[j][...], g_r[j][...], m_r[j][...], v_r[j][...])

    vm = pl.BlockSpec(memory_space=pltpu.VMEM)
    shapes = tuple(jax.ShapeDtypeStruct(w.shape, F32) for w in ws)
    outs = pl.pallas_call(
        body, in_specs=[vm] * (4 * n), out_specs=(vm,) * (3 * n), out_shape=shapes * 3,
        compiler_params=_cp(), name="adam_small",
    )(*ws, *ms, *vs, *gs)
    return outs[0:n], outs[n:2 * n], outs[2 * n:3 * n]


def _block_diag_strips(w, lw):
    heads = lw // LRU_HEAD
    w4 = w.reshape(D_PART // lw, heads, LRU_HEAD, LRU_HEAD)
    rows = [jnp.pad(w4[:, hh], ((0, 0), (0, 0), (LRU_HEAD * hh, lw - LRU_HEAD * (hh + 1)))) for hh in range(heads)]
    return jnp.concatenate(rows, axis=1)


def _gate_matrices(w_a, w_i, lw):
    return jnp.concatenate([_block_diag_strips(w_a, lw), _block_diag_strips(w_i, lw)], axis=2).astype(MXU_DTYPE)


def _strip_diag_blocks(g):
    g5 = g.reshape(NS, HEADS_PER_STRIP, LRU_HEAD, HEADS_PER_STRIP, LRU_HEAD)
    return jnp.stack([g5[:, hh, :, hh, :] for hh in range(HEADS_PER_STRIP)], axis=1).reshape(NS * HEADS_PER_STRIP, LRU_HEAD, LRU_HEAD)


def kernel(x, ln_g, w_in, conv_w, lru_conv_w, lru_conv_b, w_a, b_a, w_i, b_i, lam, conv_out_g, lru_out_g, w_out, final_g, loss_target, m_ln_g, m_w_in, m_conv_w, m_lru_conv_w, m_lru_conv_b, m_w_a, m_b_a, m_w_i, m_b_i, m_lam, m_conv_out_g, m_lru_out_g, m_w_out, m_final_g, v_ln_g, v_w_in, v_conv_w, v_lru_conv_w, v_lru_conv_b, v_w_a, v_b_a, v_w_i, v_b_i, v_lam, v_conv_out_g, v_lru_out_g, v_w_out, v_final_g):
    xi, yi, ci = lax.axis_index("x"), lax.axis_index("y"), lax.axis_index("c")
    k = 2 * xi + yi
    t = x.shape[1]
    x2 = x.reshape(t, D_MODEL)
    tgt2 = loss_target.reshape(t, D_MODEL)
    row = lambda a: a.reshape(1, -1)

    small = jnp.concatenate([conv_w, lru_conv_w, jnp.zeros((1, conv_w.shape[1]), F32)], axis=0)
    proj, xn, w12, sm4 = _gather_in_projection(x2, row(ln_g), w_in, small)
    convs = jnp.transpose(sm4, (1, 0, 2)).reshape(SUBLANES, D_PART)
    pvec = jnp.concatenate(
        [convs[0:7], row(lru_conv_b), row(b_a), row(b_i), row(lam), row(conv_out_g), row(lru_out_g),
         jnp.zeros((PV_ROWS - N_ACC, D_PART), F32)], axis=0)
    wai = _gate_matrices(w_a, w_i, LW)

    c_arr = jnp.reshape(ci, (1,)).astype(jnp.int32)
    yc, yl, h, u, r, ig, wo4 = _mixer_forward(proj, pvec, _gate_matrices(w_a, w_i, FWD_LW), w_out)
    wo = wo4.reshape(2 * D_PART, D_MODEL)
    do, dob, dy, st_out = _out_projection_loss(yc, yl, x2, tgt2, wo, row(final_g))
    go4, go4b = _w_out_grad(yc, yl, dob)
    dproj, g_wai, svec, r2o, s_out = _mixer_backward(proj, h, u, r, ig, dy, pvec, wai, go4, go4b)
    gwa = _strip_diag_blocks(g_wai[:, :, 0:LW]).reshape(LRU_HEAD, D_PART)
    gwi = _strip_diag_blocks(g_wai[:, :, LW:2 * LW]).reshape(LRU_HEAD, D_PART)
    g12, g12b, red = _w_in_grad(xn, dproj, jnp.concatenate([svec, st_out, gwa, gwi], axis=0))
    s_in, sb_in = _add_sibling_halves(g12, g12b, c_arr, "add_sibling_halves_in")
    grad_x, st_in, r2i = _input_grad(dproj, w12, x2, do, row(ln_g), sb_in)
    f_in, f_out, red_ln = _finish_gradients(s_in, r2i, s_out, r2o, st_in)
    r_out = PV_ROWS
    r_wa = PV_ROWS + SUBLANES
    r_wi = r_wa + LRU_HEAD
    loss = red[r_out + 1, 0]

    g_w_in, d_w_in, nm_w_in, nv_w_in, d_w_out, nm_w_out, nv_w_out = _adamw_blocks(
        f_in, f_out, (w_in, m_w_in, v_w_in), (w_out, m_w_out, v_w_out))
    g_w_out = f_out[0]

    ncol = conv_w.shape[1]
    conv_cols = lax.dynamic_slice(red, (0, k * ncol), (SUBLANES, ncol))
    g_small = {
        "ln_g": red_ln[0], "conv_w": conv_cols[0:3], "lru_conv_w": conv_cols[3:7], "lru_conv_b": red[PV_LRU_B],
        "w_a": red[r_wa:r_wa + LRU_HEAD].reshape(w_a.shape), "b_a": red[PV_BA],
        "w_i": red[r_wi:r_wi + LRU_HEAD].reshape(w_i.shape), "b_i": red[PV_BI], "lam": red[PV_LAM],
        "conv_out_g": red[PV_CG], "lru_out_g": red[PV_LG], "final_g": red[r_out],
    }
    w_small = {"ln_g": ln_g, "conv_w": conv_w, "lru_conv_w": lru_conv_w, "lru_conv_b": lru_conv_b, "w_a": w_a, "b_a": b_a,
               "w_i": w_i, "b_i": b_i, "lam": lam, "conv_out_g": conv_out_g, "lru_out_g": lru_out_g, "final_g": final_g}
    m_small = {"ln_g": m_ln_g, "conv_w": m_conv_w, "lru_conv_w": m_lru_conv_w, "lru_conv_b": m_lru_conv_b, "w_a": m_w_a,
               "b_a": m_b_a, "w_i": m_w_i, "b_i": m_b_i, "lam": m_lam, "conv_out_g": m_conv_out_g,
               "lru_out_g": m_lru_out_g, "final_g": m_final_g}
    v_small = {"ln_g": v_ln_g, "conv_w": v_conv_w, "lru_conv_w": v_lru_conv_w, "lru_conv_b": v_lru_conv_b, "w_a": v_w_a,
               "b_a": v_b_a, "w_i": v_w_i, "b_i": v_b_i, "lam": v_lam, "conv_out_g": v_conv_out_g,
               "lru_out_g": v_lru_out_g, "final_g": v_final_g}
    names = list(w_small)
    as2d = lambda a: a.reshape(1, -1) if a.ndim == 1 else a
    d_s, m_s, v_s = _adam_small([as2d(w_small[n]) for n in names], [as2d(m_small[n]) for n in names],
                                [as2d(v_small[n]) for n in names], [as2d(g_small[n]) for n in names])
    back = lambda n, a: a.reshape(w_small[n].shape)
    grads = {n: g_small[n] for n in names}
    deltas = {n: back(n, a) for n, a in zip(names, d_s)}
    new_m = {n: back(n, a) for n, a in zip(names, m_s)}
    new_v = {n: back(n, a) for n, a in zip(names, v_s)}
    grads["w_in"], deltas["w_in"], new_m["w_in"], new_v["w_in"] = g_w_in, d_w_in, nm_w_in, nv_w_in
    grads["w_out"], deltas["w_out"], new_m["w_out"], new_v["w_out"] = g_w_out, d_w_out, nm_w_out, nv_w_out

    order = ["ln_g", "w_in", "conv_w", "lru_conv_w", "lru_conv_b", "w_a", "b_a", "w_i", "b_i", "lam", "conv_out_g",
             "lru_out_g", "w_out", "final_g"]
    return (loss, grad_x.reshape(x.shape), *[grads[n] for n in order], *[deltas[n] for n in order],
            *[new_m[n] for n in order], *[new_v[n] for n in order])
```

```python
import functools

import jax
import jax.numpy as jnp
from jax import lax
from jax.experimental import pallas as pl
from jax.experimental.pallas import tpu as pltpu

F32 = jnp.float32
MXU_DTYPE = jnp.bfloat16

D_MODEL = 1024
D_PART = 1024
N_PARTS = 6
CHUNK = 512
CHUNKS_PER_BLOCK = 3
N_CHUNKS = 12
N_CHIPS = 4
SUBLANES = 8
LANES = 128
LW = 256
FWD_LW = 512
UNROLL = 8
NS = D_PART // LW
STRIPS_PER_CHUNK = CHUNK // LW
CONV_HEAD = 128
LRU_HEAD = 64
HEADS_PER_STRIP = LW // LRU_HEAD
RMS_EPS = 1e-6
RG_LRU_C = 8.0
ADAM_LR = 0.001
ADAM_B1 = 0.9
ADAM_B2 = 0.999
ADAM_EPS = 1e-08
ADAM_WD = 0.01
ADAM_STEP = 10

PV_CONV_W = 0
PV_LRU_W = 3
PV_LRU_B = 7
PV_BA = 8
PV_BI = 9
PV_LAM = 10
PV_CG = 11
PV_LG = 12
PV_ROWS = 16
N_ACC = 13

SLAB = 128
MESH = pl.DeviceIdType.MESH
VMEM_LIMIT = 56 * 1024 * 1024
ARB = "arbitrary"


def _cp(*sem, **kw):
    return pltpu.CompilerParams(dimension_semantics=sem or None, vmem_limit_bytes=VMEM_LIMIT, **kw)


def _mm(a, b):
    return jnp.dot(a, b, preferred_element_type=F32)


def _mm_nt(a, b):
    return lax.dot_general(a, b, (((1,), (1,)), ((), ())), preferred_element_type=F32)


def _mm_tn(a, b):
    return lax.dot_general(a, b, (((0,), (0,)), ((), ())), preferred_element_type=F32)


def _sigmoid(x):
    return 0.5 * jnp.tanh(0.5 * x) + 0.5


def _log_sigmoid(x):
    z = jnp.exp(-jnp.abs(x))
    u = 1.0 + z
    log1p = jnp.where(u == 1.0, z, jnp.log(u) * z / (u - 1.0))
    return jnp.minimum(x, 0.0) - log1p


def _head_mean(z, head):
    out = []
    for k in range(z.shape[1] // LANES):
        zk = z[:, LANES * k:LANES * (k + 1)]
        if head == LANES:
            m = jnp.sum(zk, axis=-1, keepdims=True) * (1.0 / head)
            out.append(jnp.broadcast_to(m, zk.shape))
        else:
            lo = lax.broadcasted_iota(jnp.int32, zk.shape, 1) < head
            s_lo = jnp.sum(jnp.where(lo, zk, 0.0), axis=-1, keepdims=True)
            s_hi = jnp.sum(jnp.where(lo, 0.0, zk), axis=-1, keepdims=True)
            out.append(jnp.where(lo, s_lo, s_hi) * (1.0 / head))
    return jnp.concatenate(out, axis=1)


def _shift_down(cur, prev, d, row):
    return pltpu.roll(jnp.where(row < SUBLANES - d, cur, prev), d, 0)


def _shift_up(cur, nxt, d, row):
    return pltpu.roll(jnp.where(row >= d, cur, nxt), SUBLANES - d, 0)


def _scan8_fwd(a, b, row):
    A, B = a, b
    for d in (1, 2, 4):
        m = row >= d
        a_s = jnp.where(m, pltpu.roll(A, d, 0), 1.0)
        b_s = jnp.where(m, pltpu.roll(B, d, 0), 0.0)
        B = A * b_s + B
        A = A * a_s
    return A, B


def _scan8_rev(a, b, row):
    A, B = a, b
    for d in (1, 2, 4):
        m = row < SUBLANES - d
        a_s = jnp.where(m, pltpu.roll(A, SUBLANES - d, 0), 1.0)
        b_s = jnp.where(m, pltpu.roll(B, SUBLANES - d, 0), 0.0)
        B = A * b_s + B
        A = A * a_s
    return A, B


def _decay(r, ls8):
    la = r * ls8
    a = jnp.exp(la)
    e2 = a * a
    em = -jnp.tanh(la) * (1.0 + e2)
    inv_mult = lax.rsqrt(em)
    return a, e2, em * inv_mult, inv_mult


def _mesh_pos():
    x, y, c = lax.axis_index("x"), lax.axis_index("y"), lax.axis_index("c")
    chips = [(1 - x, y), (x, 1 - y), (1 - x, 1 - y)]
    return x, y, c, chips


def _gather_in_projection(x, ln_g, w_in, small):
    t = x.shape[0]
    rb_x = 512
    rb_mm = 2048
    n_mm = t // rb_mm
    half = w_in.shape[0] // 2

    def body(x_hbm, g_ref, wi_ref, sm_ref, proj_hbm, xn_ref, w12_ref, sm4_ref,
             xbuf, obuf, x_sems, o_sems, send_sems, recv_sems):
        x_, y_, c, chips = _mesh_pos()
        k = 2 * x_ + y_
        sib = (x_, y_, 1 - c)
        sm4_ref[k] = sm_ref[...]

        def remote(ref, sem, to):
            return pltpu.make_async_remote_copy(src_ref=ref, dst_ref=ref, send_sem=send_sems.at[sem],
                                                recv_sem=recv_sems.at[sem], device_id=to, device_id_type=MESH)

        def chunk_of(chip, s):
            return CHUNKS_PER_BLOCK * (2 * chip[0] + chip[1]) + s

        def piece(q, core, first=0, rows=half):
            return w12_ref.at[q, pl.ds(pl.multiple_of(half * core + first, SUBLANES * 2), rows), :]

        nbr_x, nbr_y, diagonal = chips
        quarter = half // 2
        DIAG = [(0, 0, half, 0), (1, 0, quarter, 0), (1, quarter, quarter, 1), (2, 0, half, 1)]
        ici = lambda m, s: 2 * s + m
        dgn = lambda j: 6 + j
        to_sib = 10
        sml = lambda m: 20 + m

        sends = []
        for s in range(CHUNKS_PER_BLOCK):
            w12_ref[chunk_of((x_, y_), s)] = wi_ref[:, CHUNK * s:CHUNK * (s + 1)].astype(MXU_DTYPE)
            for m, chip in enumerate((nbr_x, nbr_y)):
                sends.append(remote(piece(chunk_of((x_, y_), s), c), ici(m, s), (*chip, c)))
                sends[-1].start()
        for m, chip in enumerate(chips):
            sends.append(remote(sm4_ref.at[k], sml(m), (*chip, c)))
            sends[-1].start()

        def x_copy(rb, slot):
            return pltpu.make_async_copy(x_hbm.at[pl.ds(rb * rb_x, rb_x), :], xbuf.at[slot], x_sems.at[slot])

        x_copy(0, 0).start()
        for rb in range(t // rb_x):
            slot = rb % 2
            x_copy(rb, slot).wait()
            if rb + 1 < t // rb_x:
                x_copy(rb + 1, 1 - slot).start()

            def norm_slab(sl, carry, rb=rb, slot=slot):
                xf = xbuf[slot, pl.ds(pl.multiple_of(sl * SLAB, SLAB), SLAB), :]
                r = lax.rsqrt(jnp.mean(xf * xf, axis=-1, keepdims=True) + RMS_EPS)
                xn_ref[pl.ds(pl.multiple_of(rb * rb_x + sl * SLAB, SLAB), SLAB), :] = ((xf * r) * g_ref[...]).astype(MXU_DTYPE)
                return carry

            lax.fori_loop(0, rb_x // SLAB, norm_slab, 0)

        def out_copy(q, i):
            return pltpu.make_async_copy(obuf.at[i], proj_hbm.at[q, pl.ds(pl.multiple_of(i * rb_mm, rb_mm), rb_mm), :],
                                         o_sems.at[i])

        def project(q, very_first):
            def row_block(i, carry):
                if not very_first:
                    out_copy(q, i).wait()
                obuf[i] = _mm(xn_ref[pl.ds(pl.multiple_of(i * rb_mm, rb_mm), rb_mm), :], w12_ref[q])
                out_copy(q, i).start()
                return carry

            lax.fori_loop(0, n_mm, row_block, 0)

        for s in range(CHUNKS_PER_BLOCK):
            project(chunk_of((x_, y_), s), very_first=(s == 0))

        steps = []
        for s in range(CHUNKS_PER_BLOCK):
            for m, chip in enumerate((nbr_x, nbr_y)):
                onward = [(first, rows, dgn(j), chips[via]) for j, (cs, first, rows, via) in enumerate(DIAG)
                          if cs == s and via == 1 - m]
                steps.append((chunk_of(chip, s), [(0, half, ici(m, s))], onward))
        for s in range(CHUNKS_PER_BLOCK):
            steps.append((chunk_of(diagonal, s), [(first, rows, dgn(j)) for j, (cs, first, rows, _) in enumerate(DIAG) if cs == s], []))

        def project_when_whole(step):
            q, pieces, _ = step
            for first, rows, sem in pieces:
                remote(piece(q, 1 - c, first, rows), to_sib + sem, sib).wait_recv()
            project(q, very_first=False)

        passed = []
        for j, (q, pieces, onward) in enumerate(steps):
            for first, rows, sem in pieces:
                remote(piece(q, c, first, rows), sem, sib).wait_recv()
            for first, rows, sem, chip in onward:
                passed.append(remote(piece(q, c, first, rows), sem, (*chip, c)))
                passed[-1].start()
            for first, rows, sem in pieces:
                passed.append(remote(piece(q, c, first, rows), to_sib + sem, sib))
                passed[-1].start()
            if j > 0:
                project_when_whole(steps[j - 1])
        project_when_whole(steps[-1])

        for m, chip in enumerate(chips):
            remote(sm4_ref.at[2 * chip[0] + chip[1]], sml(m), sib).wait_recv()
        for cp in sends + passed:
            cp.wait_send()
        for i in range(n_mm):
            out_copy(0, i).wait()

    vm = pl.BlockSpec(memory_space=pltpu.VMEM)
    hbm = pl.BlockSpec(memory_space=pl.ANY)
    n_sems = 23
    return pl.pallas_call(
        body,
        out_shape=(jax.ShapeDtypeStruct((N_CHUNKS, t, CHUNK), F32), jax.ShapeDtypeStruct((t, D_MODEL), MXU_DTYPE),
                   jax.ShapeDtypeStruct((N_CHUNKS, w_in.shape[0], CHUNK), MXU_DTYPE),
                   jax.ShapeDtypeStruct((N_CHIPS,) + small.shape, F32)),
        in_specs=[hbm, vm, vm, vm], out_specs=(hbm, vm, vm, vm),
        scratch_shapes=[pltpu.VMEM((2, rb_x, D_MODEL), F32), pltpu.VMEM((n_mm, rb_mm, CHUNK), F32),
                        pltpu.SemaphoreType.DMA((2,)), pltpu.SemaphoreType.DMA((n_mm,)),
                        pltpu.SemaphoreType.DMA((n_sems,)), pltpu.SemaphoreType.DMA((n_sems,))],
        compiler_params=_cp(), name="gather_in_projection",
    )(x, ln_g, w_in, small)


def _allreduce_behind(step, when, in_ref, acc_s, rbufs, out_ref, send_sems, recv_sems):
    x, y, c, _ = _mesh_pos()
    peers = [(x, y, 1 - c), (1 - x, y, c), (x, 1 - y, c)]

    def exchange(ph):
        return pltpu.make_async_remote_copy(src_ref=acc_s, dst_ref=rbufs[ph], send_sem=send_sems.at[ph],
                                            recv_sem=recv_sems.at[ph], device_id=peers[ph], device_id_type=MESH)

    @pl.when(step == when[0])
    def _():
        acc_s[...] = in_ref[...]
        exchange(0).start()

    for ph in (1, 2):
        @pl.when(step == when[ph])
        def _(ph=ph):
            exchange(ph - 1).wait()
            acc_s[...] = acc_s[...] + rbufs[ph - 1][...]
            exchange(ph).start()

    @pl.when(step == when[3])
    def _():
        exchange(2).wait()
        out_ref[...] = acc_s[...] + rbufs[2][...]


def _add_sibling_halves(g, gb, c_arr, name):
    n, rows, cols = g.shape
    half = rows // 2
    per = n // N_CHIPS
    steps = N_CHIPS

    def body(c_ref, g_ref, gb_hbm, o_ref, ob_ref, rbuf, send_sems, recv_sems):
        q = pl.program_id(0)
        x, y, c, _ = _mesh_pos()
        theirs = pl.ds(pl.multiple_of(half * (1 - c), half), half)

        def copy(j):
            blocks = pl.ds(j * per, per)
            return pltpu.make_async_remote_copy(src_ref=gb_hbm.at[blocks, theirs, :], dst_ref=rbuf.at[blocks], send_sem=send_sems.at[j],
                                                recv_sem=recv_sems.at[j], device_id=(x, y, 1 - c), device_id_type=MESH)

        @pl.when(q == 0)
        def _():
            for j in range(steps):
                copy(j).start()

        copy(q).wait_recv()
        s = g_ref[...] + rbuf[pl.ds(q * per, per)].astype(F32)
        ob_ref[...] = s.astype(jnp.bfloat16)

        @pl.when(q == 2 * x + y)
        def _():
            o_ref[...] = s

        @pl.when(q == steps - 1)
        def _():
            for j in range(steps):
                copy(j).wait_send()

    return pl.pallas_call(
        body, out_shape=(jax.ShapeDtypeStruct((per, half, cols), F32), jax.ShapeDtypeStruct((n, half, cols), jnp.bfloat16)),
        grid_spec=pltpu.PrefetchScalarGridSpec(
            num_scalar_prefetch=1, grid=(steps,),
            in_specs=[pl.BlockSpec((per, half, cols), lambda q, c_ref: (q, c_ref[0], 0)), pl.BlockSpec(memory_space=pl.ANY)],
            out_specs=(pl.BlockSpec((per, half, cols), lambda q, c_ref: (0, 0, 0)),
                       pl.BlockSpec((per, half, cols), lambda q, c_ref: (q, 0, 0))),
            scratch_shapes=[pltpu.VMEM((n, half, cols), jnp.bfloat16), pltpu.SemaphoreType.DMA((steps,)),
                            pltpu.SemaphoreType.DMA((steps,))]),
        compiler_params=_cp(ARB), name=name,
    )(c_arr, g, gb)


def _chip_block_copies(s_ref, r_ref, n_sub, send_sems, recv_sems):
    x, y, c, chips = _mesh_pos()
    cps = []
    for m, chip in enumerate(chips):
        kk = 2 * chip[0] + chip[1]
        cps.append(pltpu.make_async_remote_copy(
            src_ref=s_ref.at[pl.ds(n_sub * kk, n_sub)], dst_ref=r_ref.at[m],
            send_sem=send_sems.at[m], recv_sem=recv_sems.at[m], device_id=(*chip, c), device_id_type=MESH))
    return cps


def _gather_w_out(step, n_steps, wo_ref, wob_s, wo4_ref, local_sem, send_sems, recv_sems):
    x, y, c, chips = _mesh_pos()
    sib = (x, y, 1 - c)
    half = wo_ref.shape[0] // 2

    def rows(core):
        return pl.ds(pl.multiple_of(half * core, half), half)

    def block_half(chip, core):
        return wo4_ref.at[2 * chip[0] + chip[1], rows(core), :]

    def remote(src, dst, sem, to):
        return pltpu.make_async_remote_copy(src_ref=src, dst_ref=dst, send_sem=send_sems.at[sem], recv_sem=recv_sems.at[sem],
                                            device_id=to, device_id_type=MESH)

    local = pltpu.make_async_copy(wob_s, wo4_ref.at[2 * x + y], local_sem)
    ici = [remote(wob_s.at[rows(c), :], block_half((x, y), c), m, (*chip, c)) for m, chip in enumerate(chips)]
    fwd = [remote(block_half(chip, c), block_half(chip, c), 3 + m, sib) for m, chip in enumerate(chips)]

    @pl.when(step == 0)
    def _():
        wob_s[...] = wo_ref[...].astype(MXU_DTYPE)
        local.start()
        for cp in ici:
            cp.start()

    @pl.when(step == n_steps // 2)
    def _():
        for m, chip in enumerate(chips):
            remote(block_half(chip, c), block_half(chip, c), m, sib).wait_recv()
            fwd[m].start()

    @pl.when(step == n_steps - 1)
    def _():
        for m, chip in enumerate(chips):
            remote(block_half(chip, 1 - c), block_half(chip, 1 - c), 3 + m, sib).wait_recv()
        for cp in ici + fwd:
            cp.wait_send()
        local.wait()


def _chip_blocks_shape(s, n_sub):
    return jax.ShapeDtypeStruct((3, n_sub) + s.shape[1:], s.dtype)


def _finish_gradients(s_in, r_in, s_out, r_out, v):
    n_dev = 8
    n_in, n_out = r_in.shape[1], r_out.shape[1]

    def body(si_hbm, ri_hbm, so_hbm, ro_hbm, v_ref, fi_hbm, fo_hbm, tot_ref,
             a_in, b_in, a_out, b_out, slots, load_sems, store_sems, send_sems, recv_sems):
        x, y, c, _ = _mesh_pos()
        sib = (x, y, 1 - c)
        me = 4 * x + 2 * y + c
        loads = [pltpu.make_async_copy(si_hbm, a_in, load_sems.at[0]),
                 pltpu.make_async_copy(ri_hbm, b_in, load_sems.at[1]),
                 pltpu.make_async_copy(so_hbm, a_out, load_sems.at[2]),
                 pltpu.make_async_copy(ro_hbm, b_out, load_sems.at[3])]
        for cp in loads:
            cp.start()
        slots[me] = v_ref[...]

        def remote(src, dst, sem, to):
            return pltpu.make_async_remote_copy(src_ref=src, dst_ref=dst, send_sem=send_sems.at[sem],
                                                recv_sem=recv_sems.at[sem], device_id=to, device_id_type=MESH)

        small = []
        for d in range(1, n_dev):
            peer = (1 - x if d & 4 else x, 1 - y if d & 2 else y, 1 - c if d & 1 else c)
            small.append(remote(slots.at[me], slots.at[me], d - 1, peer))
            small[-1].start()
        for cp in loads:
            cp.wait()
        big = []
        for j, (a, b, f_hbm) in enumerate(((a_in, b_in, fi_hbm), (a_out, b_out, fo_hbm))):
            a[...] = ((a[...] + b[0].astype(F32)) + b[1].astype(F32)) + b[2].astype(F32)
            half = a.shape[1]
            mine = f_hbm.at[:, pl.ds(pl.multiple_of(half * c, half), half), :]
            big.append(pltpu.make_async_copy(a, mine, store_sems.at[j]))
            big.append(remote(a, mine, n_dev - 1 + j, sib))
        for cp in big:
            cp.start()
        for cp in small + big:
            cp.wait()
        total = slots[0]
        for dev in range(1, n_dev):
            total = total + slots[dev]
        tot_ref[...] = total

    hbm = pl.BlockSpec(memory_space=pl.ANY)
    vm = pl.BlockSpec(memory_space=pltpu.VMEM)
    full = lambda s, n: (n, 2 * s.shape[1], s.shape[2])
    return pl.pallas_call(
        body,
        out_shape=(jax.ShapeDtypeStruct(full(s_in, n_in), F32), jax.ShapeDtypeStruct(full(s_out, n_out), F32),
                   jax.ShapeDtypeStruct(v.shape, F32)),
        in_specs=[hbm, hbm, hbm, hbm, vm], out_specs=(hbm, hbm, vm),
        scratch_shapes=[pltpu.VMEM((n_in,) + s_in.shape[1:], F32), pltpu.VMEM(r_in.shape, r_in.dtype),
                        pltpu.VMEM((n_out,) + s_out.shape[1:], F32), pltpu.VMEM(r_out.shape, r_out.dtype),
                        pltpu.VMEM((n_dev,) + v.shape, F32), pltpu.SemaphoreType.DMA((4,)), pltpu.SemaphoreType.DMA((2,)),
                        pltpu.SemaphoreType.DMA((n_dev + 1,)), pltpu.SemaphoreType.DMA((n_dev + 1,))],
        compiler_params=_cp(), name="finish_gradients",
    )(s_in, r_in, s_out, r_out, v)


def _out_projection_loss(yc, yl, x, target, wo, final_g):
    t = x.shape[0]
    tm = 512

    def body(yc_ref, yl_ref, x_ref, t_ref, wo_ref, fg_ref, do_ref, dob_ref, dy_ref, st_ref, y_wo):
        @pl.when(pl.program_id(0) == 0)
        def _():
            st_ref[...] = jnp.zeros_like(st_ref)

        y_wo[...] = _mm(yc_ref[...], wo_ref[0:D_PART, :]) + _mm(yl_ref[...], wo_ref[D_PART:2 * D_PART, :])

        def norm_loss_slab(s, carry):
            g_sum, loss_sum = carry
            rows = pl.ds(pl.multiple_of(s * SLAB, SLAB), SLAB)
            o = x_ref[rows, :] + y_wo[rows, :]
            r2 = lax.rsqrt(jnp.mean(o * o, axis=-1, keepdims=True) + RMS_EPS)
            ohat = o * r2
            fg = fg_ref[...]
            diff = ohat * fg - t_ref[rows, :]
            dout = diff * (1.0 / D_MODEL)
            gp = dout * fg
            do = r2 * (gp - ohat * jnp.mean(gp * ohat, axis=-1, keepdims=True))
            do_ref[rows, :] = do
            dob_ref[rows, :] = do.astype(MXU_DTYPE)
            loss = 0.5 * jnp.sum(jnp.sum(diff * diff, axis=-1, keepdims=True) * (1.0 / D_MODEL), axis=0, keepdims=True)
            return g_sum + jnp.sum(dout * ohat, axis=0, keepdims=True), loss_sum + loss

        g_sum, loss_sum = lax.fori_loop(0, tm // SLAB, norm_loss_slab,
                                        (jnp.zeros((1, D_MODEL), F32), jnp.zeros((1, 1), F32)))
        st_ref[0:1, :] += g_sum
        st_ref[1:2, :] += jnp.broadcast_to(loss_sum, (1, D_MODEL))
        dy_ref[...] = _mm_nt(dob_ref[...], wo_ref[...])

    row = lambda i: (i, 0)
    fix = lambda i: (0, 0)
    return pl.pallas_call(
        body, grid=(t // tm,),
        in_specs=[pl.BlockSpec((tm, D_PART), row), pl.BlockSpec((tm, D_PART), row),
                  pl.BlockSpec((tm, D_MODEL), row), pl.BlockSpec((tm, D_MODEL), row),
                  pl.BlockSpec((2 * D_PART, D_MODEL), fix), pl.BlockSpec((1, D_MODEL), fix)],
        out_specs=(pl.BlockSpec((tm, D_MODEL), row), pl.BlockSpec((tm, D_MODEL), row),
                   pl.BlockSpec((tm, 2 * D_PART), row), pl.BlockSpec((SUBLANES, D_MODEL), fix)),
        out_shape=(jax.ShapeDtypeStruct((t, D_MODEL), F32), jax.ShapeDtypeStruct((t, D_MODEL), MXU_DTYPE),
                   jax.ShapeDtypeStruct((t, 2 * D_PART), F32), jax.ShapeDtypeStruct((SUBLANES, D_MODEL), F32)),
        scratch_shapes=[pltpu.VMEM((tm, D_MODEL), F32)],
        compiler_params=_cp(ARB), name="out_projection_loss",
    )(yc, yl, x, target, wo, final_g)


def _input_grad(dproj, w12, x, do, ln_g, sb_in):
    t = x.shape[0]
    tm = 1024

    def body(dp_ref, w_ref, x_ref, do_ref, g_ref, s_ref, gx_ref, st_ref, r_ref, acc, send_sems, recv_sems):
        i, p = pl.program_id(0), pl.program_id(1)

        @pl.when((i == 0) & (p == 0))
        def _():
            st_ref[...] = jnp.zeros_like(st_ref)
            for cp in _chip_block_copies(s_ref, r_ref, CHUNKS_PER_BLOCK, send_sems, recv_sems):
                cp.start()

        @pl.when((i == t // tm - 1) & (p == N_PARTS - 1))
        def _():
            for cp in _chip_block_copies(s_ref, r_ref, CHUNKS_PER_BLOCK, send_sems, recv_sems):
                cp.wait()

        @pl.when(p == 0)
        def _():
            acc[...] = jnp.zeros_like(acc)

        acc[...] += _mm_nt(dp_ref[0], jnp.concatenate([w_ref[0], w_ref[1]], axis=1))

        @pl.when(p == N_PARTS - 1)
        def _():
            def norm_bwd_slab(s, g_sum):
                rows = pl.ds(pl.multiple_of(s * SLAB, SLAB), SLAB)
                xf = x_ref[rows, :]
                r = lax.rsqrt(jnp.mean(xf * xf, axis=-1, keepdims=True) + RMS_EPS)
                xhat = xf * r
                dxn = acc[rows, :]
                dxh = dxn * g_ref[...]
                gx_ref[rows, :] = do_ref[rows, :] + r * (dxh - xhat * jnp.mean(dxh * xhat, axis=-1, keepdims=True))
                return g_sum + jnp.sum(dxn * xhat, axis=0, keepdims=True)

            st_ref[0:1, :] += lax.fori_loop(0, tm // SLAB, norm_bwd_slab, jnp.zeros((1, D_MODEL), F32))

    row = lambda i, p: (i, 0)
    fix = lambda i, p: (0, 0)
    return pl.pallas_call(
        body, grid=(t // tm, N_PARTS),
        in_specs=[
            pl.BlockSpec((1, tm, D_PART), lambda i, p: (p, i, 0)),
            pl.BlockSpec((2, D_MODEL, CHUNK), lambda i, p: (p, 0, 0)),
            pl.BlockSpec((tm, D_MODEL), row), pl.BlockSpec((tm, D_MODEL), row), pl.BlockSpec((1, D_MODEL), fix),
            pl.BlockSpec(memory_space=pl.ANY)],
        out_specs=(pl.BlockSpec((tm, D_MODEL), row), pl.BlockSpec((SUBLANES, D_MODEL), fix),
                   pl.BlockSpec(memory_space=pl.ANY)),
        out_shape=(jax.ShapeDtypeStruct((t, D_MODEL), F32), jax.ShapeDtypeStruct((SUBLANES, D_MODEL), F32),
                   _chip_blocks_shape(sb_in, CHUNKS_PER_BLOCK)),
        scratch_shapes=[pltpu.VMEM((tm, D_MODEL), F32), pltpu.SemaphoreType.DMA((3,)), pltpu.SemaphoreType.DMA((3,))],
        compiler_params=_cp(ARB, ARB), name="input_grad",
    )(dproj, w12, x, do, ln_g, sb_in)


def _w_in_grad(xn, dproj, small):
    t = xn.shape[0]
    small_shape = pltpu.VMEM(small.shape, F32)

    def body(xn_ref, dp_ref, sm_ref, o_ref, ob_ref, red_ref, acc_s, r0, r1, r2, send_sems, recv_sems):
        _allreduce_behind(pl.program_id(0), (0, 1, 3, N_PARTS - 1), sm_ref, acc_s, (r0, r1, r2), red_ref, send_sems, recv_sems)
        g = _mm_tn(xn_ref[...], dp_ref[0])
        for s in range(2):
            o_ref[s] = g[:, CHUNK * s:CHUNK * (s + 1)]
            ob_ref[s] = g[:, CHUNK * s:CHUNK * (s + 1)].astype(jnp.bfloat16)

    whole = pl.BlockSpec(small.shape, lambda p: (0, 0))
    pair = pl.BlockSpec((2, D_MODEL, CHUNK), lambda p: (p, 0, 0))
    return pl.pallas_call(
        body, grid=(N_PARTS,),
        in_specs=[pl.BlockSpec((t, D_MODEL), lambda p: (0, 0)),
                  pl.BlockSpec((1, t, D_PART), lambda p: (p, 0, 0)), whole],
        out_specs=(pair, pair, whole),
        out_shape=(jax.ShapeDtypeStruct((N_CHUNKS, D_MODEL, CHUNK), F32),
                   jax.ShapeDtypeStruct((N_CHUNKS, D_MODEL, CHUNK), jnp.bfloat16), jax.ShapeDtypeStruct(small.shape, F32)),
        scratch_shapes=[small_shape] * 4 + [pltpu.SemaphoreType.DMA((3,)), pltpu.SemaphoreType.DMA((3,))],
        compiler_params=_cp(ARB), name="w_in_grad",
    )(xn, dproj, small)


def _w_out_grad(yc, yl, dob):
    t = yc.shape[0]
    tk = 2048

    def body(yc_ref, yl_ref, do_ref, o_ref, ob_ref):
        j, kk = pl.program_id(0), pl.program_id(1)

        def accumulate(y_ref):
            @pl.when(kk == 0)
            def _():
                o_ref[...] = jnp.zeros_like(o_ref)

            o_ref[...] += _mm_tn(y_ref[...], do_ref[...])

            @pl.when(kk == t // tk - 1)
            def _():
                ob_ref[...] = o_ref[...].astype(jnp.bfloat16)

        pl.when(j == 0)(functools.partial(accumulate, yc_ref))
        pl.when(j == 1)(functools.partial(accumulate, yl_ref))

    def rows_of(half):
        return lambda j, kk: (jnp.where(j == half, kk, 0), 0)

    half = pl.BlockSpec((D_PART, D_MODEL), lambda j, kk: (j, 0))
    out, out_b = pl.pallas_call(
        body, grid=(2, t // tk),
        in_specs=[pl.BlockSpec((tk, D_PART), rows_of(0)), pl.BlockSpec((tk, D_PART), rows_of(1)),
                  pl.BlockSpec((tk, D_MODEL), lambda j, kk: (kk, 0))],
        out_specs=(half, half),
        out_shape=(jax.ShapeDtypeStruct((2 * D_PART, D_MODEL), F32), jax.ShapeDtypeStruct((2 * D_PART, D_MODEL), jnp.bfloat16)),
        compiler_params=_cp(ARB, ARB), name="w_out_grad",
    )(yc, yl, dob)
    blocks = (N_CHIPS, 2 * D_PART // N_CHIPS, D_MODEL)
    return out.reshape(blocks), out_b.reshape(blocks)


def _for_groups(n, fn, init, unroll=UNROLL, stores=(), descending=False):
    assert unroll % 2 == 0 and n % unroll == 0

    def trip(j, carry):
        held = None
        for uu in range(unroll):
            idx = j * unroll + uu
            carry, values = fn(idx, carry)
            if uu % 2 == 0:
                held = values
                continue
            low_group = n - 1 - idx if descending else idx - 1
            rows = pl.ds(pl.multiple_of(low_group * SUBLANES, 2 * SUBLANES), 2 * SUBLANES)
            pairs = zip(values, held) if descending else zip(held, values)
            for store, (lo, hi) in zip(stores, pairs, strict=True):
                store(rows, jnp.concatenate([lo, hi], axis=0).astype(MXU_DTYPE))
        return carry

    return lax.fori_loop(0, n // unroll, trip, init)


def _rows_of(ref, *lead, cols=slice(None)):
    def store(rows, value):
        ref[(*lead, rows, cols)] = value

    return store


def _pvb(pv_ref, r):
    return jnp.broadcast_to(pv_ref[r:r + 1, :], (SUBLANES, pv_ref.shape[1]))


def _conv3(pv_ref, u, u1, u2):
    return (_pvb(pv_ref, PV_CONV_W) * u2 + _pvb(pv_ref, PV_CONV_W + 1) * u1) + _pvb(pv_ref, PV_CONV_W + 2) * u


def _conv4(pv_ref, v, v1, v2, v3):
    return ((((_pvb(pv_ref, PV_LRU_W) * v3 + _pvb(pv_ref, PV_LRU_W + 1) * v2) + _pvb(pv_ref, PV_LRU_W + 2) * v1)
             + _pvb(pv_ref, PV_LRU_W + 3) * v) + _pvb(pv_ref, PV_LRU_B))


def _mixer_forward(proj, pvec, wai, w_out):
    t = proj.shape[1]
    tb = 512
    ng = tb // SUBLANES
    nt = t // tb
    lw = FWD_LW
    ns = D_PART // lw
    per_chunk = CHUNK // lw

    def body(bg_ref, cg_ref, xc_ref, gc_ref, xl_ref, gl_ref, pv_ref, wai_ref, wo_ref,
             yc_ref, yl_ref, h_ref, u_s, r_ref, ig_ref, wo4_ref,
             ucp_s, xlp_s, ls_s, hbuf_s, ub_s, gate_s, wob_s, local_sem, send_sems, recv_sems):
        _gather_w_out(pl.program_id(0) * nt + pl.program_id(1), ns * nt, wo_ref, wob_s, wo4_ref, local_sem, send_sems, recv_sems)

        @pl.when(pl.program_id(1) == 0)
        def _():
            ucp_s[...] = jnp.zeros_like(ucp_s)
            xlp_s[...] = jnp.zeros_like(xlp_s)
            hbuf_s[...] = jnp.zeros_like(hbuf_s)

        row = lax.broadcasted_iota(jnp.int32, (SUBLANES, lw), 0)
        ls_s[...] = RG_LRU_C * _log_sigmoid(_pvb(pv_ref, PV_LAM))

        def conv_group(g, carry):
            ucp, xlp = carry
            sl = pl.ds(pl.multiple_of(g * SUBLANES, SUBLANES), SUBLANES)
            uc = cg_ref[sl, :] * xc_ref[sl, :]
            v = _conv3(pv_ref, uc, _shift_down(uc, ucp, 1, row), _shift_down(uc, ucp, 2, row))
            yc = bg_ref[sl, :] * v
            rr = lax.rsqrt(_head_mean(yc * yc, CONV_HEAD) + RMS_EPS)
            gc = gc_ref[sl, :]
            zc = ((yc * rr) * _pvb(pv_ref, PV_CG)) * (gc * _sigmoid(gc))
            xl = xl_ref[sl, :]
            u = _conv4(pv_ref, xl, _shift_down(xl, xlp, 1, row), _shift_down(xl, xlp, 2, row), _shift_down(xl, xlp, 3, row))
            u_s[sl, :] = u
            return (uc, xl), (zc, u)

        ucp, xlp = _for_groups(ng, conv_group, (ucp_s[...], xlp_s[...]), unroll=2 * UNROLL,
                               stores=(_rows_of(yc_ref), _rows_of(ub_s)))
        ucp_s[...] = ucp
        xlp_s[...] = xlp

        gate_s[...] = _mm(ub_s[...], wai_ref[0])

        def lru_group(g, h_before):
            sl = pl.ds(pl.multiple_of(g * SUBLANES, SUBLANES), SUBLANES)
            u = u_s[sl, :]
            r = _sigmoid(gate_s[sl, 0:lw] + _pvb(pv_ref, PV_BA))
            ig = _sigmoid(gate_s[sl, lw:2 * lw] + _pvb(pv_ref, PV_BI))
            r_ref[sl, :] = r
            ig_ref[sl, :] = ig
            a, _, mult, _ = _decay(r, ls_s[...])
            A, B = _scan8_fwd(a, mult * (ig * u), row)
            h = B + A * jnp.broadcast_to(h_before[SUBLANES - 1:SUBLANES, :], (SUBLANES, lw))
            h_ref[sl, :] = h
            rr = lax.rsqrt(_head_mean(h * h, LRU_HEAD) + RMS_EPS)
            gl = gl_ref[sl, :]
            return h, (((h * rr) * _pvb(pv_ref, PV_LG)) * (gl * _sigmoid(gl)),)

        hbuf_s[...] = _for_groups(ng, lru_group, hbuf_s[...], unroll=2 * UNROLL, stores=(_rows_of(yl_ref),))

    def part(p):
        return pl.BlockSpec((None, tb, lw), lambda c, i: (2 * p + c // per_chunk, i, c % per_chunk))

    strip = pl.BlockSpec((tb, lw), lambda c, i: (i, c))
    small = pltpu.VMEM((SUBLANES, lw), F32)
    return pl.pallas_call(
        body, grid=(ns, nt),
        in_specs=[part(p) for p in range(N_PARTS)] + [
            pl.BlockSpec((PV_ROWS, lw), lambda c, i: (0, c)),
            pl.BlockSpec((1, lw, 2 * lw), lambda c, i: (c, 0, 0)),
            pl.BlockSpec(w_out.shape, lambda c, i: (0, 0))],
        out_specs=(strip,) * 6 + (pl.BlockSpec(memory_space=pl.ANY),),
        out_shape=(jax.ShapeDtypeStruct((t, D_PART), MXU_DTYPE),) * 2 + (jax.ShapeDtypeStruct((t, D_PART), F32),) * 4 + (
            jax.ShapeDtypeStruct((N_CHIPS,) + w_out.shape, MXU_DTYPE),),
        scratch_shapes=[small, small, small, small, pltpu.VMEM((tb, lw), MXU_DTYPE),
                        pltpu.VMEM((tb, 2 * lw), F32), pltpu.VMEM(w_out.shape, MXU_DTYPE),
                        pltpu.SemaphoreType.DMA, pltpu.SemaphoreType.DMA((6,)), pltpu.SemaphoreType.DMA((6,))],
        compiler_params=_cp(ARB, ARB), name="mixer_forward",
    )(proj, proj, proj, proj, proj, proj, pvec, wai, w_out)


def _mixer_backward(proj, h, u, r, ig, dy, pvec, wai, go, gob):
    n_blocks, half, cols = go.shape[0], go.shape[1] // 2, go.shape[2]
    t = proj.shape[1]
    tb = 1024
    ng = tb // SUBLANES
    nt = t // tb
    gpb = tb // SUBLANES

    def body(bg_ref, cg_ref, xc_ref, gc_ref, xl_ref, gl_ref, h_ref, u_ref, r_ref, ig_ref, dyc_ref, dyl_ref,
             cgh_ref, xch_ref, xlh_ref, hh_ref, pv_ref, wai_ref, go_hbm, gob_hbm,
             dp_ref, gw_ref, sv_ref, ro_ref, so_hbm,
             ls_s, ub_s, uce_s, xle_s, he_s, dgb_s, du_s, gbuf_s,
             acc_s, an_s, dvn_s, dun_s, sum_s, arrival_s, sumb_s, send_sems, recv_sems, sibling_sems):
        i = pl.program_id(1)
        first_block = i == nt - 1

        step = pl.program_id(0) * nt + i
        mesh_x, mesh_y, core, _ = _mesh_pos()
        mine = pl.ds(pl.multiple_of(half * core, half), half)
        theirs = pl.ds(pl.multiple_of(half * (1 - core), half), half)
        load = pltpu.make_async_copy(go_hbm.at[:, mine, :], sum_s, sibling_sems.at[0])
        swap = pltpu.make_async_remote_copy(src_ref=gob_hbm.at[:, theirs, :], dst_ref=arrival_s, send_sem=sibling_sems.at[1],
                                            recv_sem=sibling_sems.at[2], device_id=(mesh_x, mesh_y, 1 - core), device_id_type=MESH)
        store = pltpu.make_async_copy(sum_s.at[pl.ds(2 * mesh_x + mesh_y, 1)], so_hbm, sibling_sems.at[3])

        @pl.when(step == 0)
        def _():
            load.start()
            swap.start()

        @pl.when(step == 1)
        def _():
            load.wait()
            swap.wait()
            sum_s[...] = sum_s[...] + arrival_s[...].astype(F32)
            sumb_s[...] = sum_s[...].astype(jnp.bfloat16)
            store.start()
            for cp in _chip_block_copies(sumb_s, ro_ref, 1, send_sems, recv_sems):
                cp.start()

        @pl.when(step == NS * nt - 1)
        def _():
            store.wait()
            for cp in _chip_block_copies(sumb_s, ro_ref, 1, send_sems, recv_sems):
                cp.wait()

        @pl.when(i == 0)
        def _():
            acc_s[...] = jnp.zeros_like(acc_s)
            gw_ref[...] = jnp.zeros_like(gw_ref)
            an_s[...] = jnp.zeros_like(an_s)
            dvn_s[...] = jnp.zeros_like(dvn_s)
            dun_s[...] = jnp.zeros_like(dun_s)
            gbuf_s[...] = jnp.zeros_like(gbuf_s)

        row = lax.broadcasted_iota(jnp.int32, (SUBLANES, LW), 0)
        ls_s[...] = RG_LRU_C * _log_sigmoid(_pvb(pv_ref, PV_LAM))
        keep = jnp.where(first_block, 0.0, 1.0)
        uce_s[0:SUBLANES, :] = (cgh_ref[...] * xch_ref[...]) * keep
        xle_s[0:SUBLANES, :] = xlh_ref[...] * keep
        he_s[0:SUBLANES, :] = hh_ref[...] * keep
        xle_s[SUBLANES:SUBLANES + tb, :] = xl_ref[...]
        he_s[SUBLANES:SUBLANES + tb, :] = h_ref[...]

        uce_s[SUBLANES:SUBLANES + tb, :] = cg_ref[...] * xc_ref[...]

        def acc_add(k, v):
            acc_s[k] += v

        def main_group(gi, carry):
            a_next, dv_next, g_next = carry
            g = ng - 1 - gi
            r0 = pl.multiple_of(g * SUBLANES, SUBLANES)
            sl = pl.ds(r0, SUBLANES)
            sl_e = pl.ds(r0 + SUBLANES, SUBLANES)
            lsb = ls_s[...]
            u = u_ref[sl, :]
            r = r_ref[sl, :]
            ig = ig_ref[sl, :]
            a, e2, mult, inv_mult = _decay(r, lsb)
            gl = gl_ref[sl, :]
            sg = _sigmoid(gl)
            s_l = gl * sg
            h8 = he_s[sl_e, :]
            hprev = _shift_down(h8, he_s[sl, :], 1, row)
            rr = lax.rsqrt(_head_mean(h8 * h8, LRU_HEAD) + RMS_EPS)
            n = h8 * rr
            dz = dyl_ref[sl, :]
            lg = _pvb(pv_ref, PV_LG)
            acc_add(PV_LG, (dz * n) * s_l)
            p5 = ((dz * n) * lg) * (sg + s_l * (1.0 - sg))
            dn = (dz * lg) * s_l
            dh = rr * (dn - n * _head_mean(dn * n, LRU_HEAD))
            A, B = _scan8_rev(_shift_up(a, a_next, 1, row), dh, row)
            gg = B + A * jnp.broadcast_to(g_next[0:1, :], (SUBLANES, LW))
            da = gg * hprev
            iu = ig * u
            diu = gg * mult
            dla = da * a - (gg * iu) * (e2 * inv_mult)
            acc_add(PV_LAM, dla * r)
            dra = (dla * lsb) * (r * (1.0 - r))
            dia = (diu * u) * (ig * (1.0 - ig))
            acc_add(PV_BA, dra)
            acc_add(PV_BI, dia)
            du_s[sl, :] = diu * ig
            bg = bg_ref[sl, :]
            gc = gc_ref[sl, :]
            uc = uce_s[sl_e, :]
            ucp = uce_s[sl, :]
            uc1 = _shift_down(uc, ucp, 1, row)
            uc2 = _shift_down(uc, ucp, 2, row)
            v = _conv3(pv_ref, uc, uc1, uc2)
            yc = bg * v
            rrc = lax.rsqrt(_head_mean(yc * yc, CONV_HEAD) + RMS_EPS)
            nc = yc * rrc
            sgc = _sigmoid(gc)
            s_c = gc * sgc
            dzc = dyc_ref[sl, :]
            cgain = _pvb(pv_ref, PV_CG)
            acc_add(PV_CG, (dzc * nc) * s_c)
            p3 = ((dzc * nc) * cgain) * (sgc + s_c * (1.0 - sgc))
            dnc = (dzc * cgain) * s_c
            dyc = rrc * (dnc - nc * _head_mean(dnc * nc, CONV_HEAD))
            dv = dyc * bg
            duc = (_pvb(pv_ref, PV_CONV_W + 2) * dv + _pvb(pv_ref, PV_CONV_W + 1) * _shift_up(dv, dv_next, 1, row)
                   + _pvb(pv_ref, PV_CONV_W) * _shift_up(dv, dv_next, 2, row))
            acc_add(PV_CONV_W + 2, dv * uc)
            acc_add(PV_CONV_W + 1, dv * uc1)
            acc_add(PV_CONV_W, dv * uc2)
            return (a, dv, gg), (dyc * v, duc * xc_ref[sl, :], duc * cg_ref[sl, :], p3, p5, dra, dia, u)

        a_next, dv_next, g_next = _for_groups(
            ng, main_group, (an_s[...], dvn_s[...], gbuf_s[...]), descending=True,
            stores=(_rows_of(dp_ref, 0), _rows_of(dp_ref, 1), _rows_of(dp_ref, 2), _rows_of(dp_ref, 3), _rows_of(dp_ref, 5),
                    _rows_of(dgb_s, cols=slice(0, LW)), _rows_of(dgb_s, cols=slice(LW, 2 * LW)), _rows_of(ub_s)))
        an_s[...] = a_next
        dvn_s[...] = dv_next
        gbuf_s[...] = g_next

        dgb = dgb_s[...]
        du_s[...] += _mm_nt(dgb, wai_ref[0])
        gw_ref[0] += _mm_tn(ub_s[...], dgb)

        def lru_conv_group(gi, du_next):
            g = ng - 1 - gi
            r0 = pl.multiple_of(g * SUBLANES, SUBLANES)
            sl = pl.ds(r0, SUBLANES)
            du = du_s[sl, :]
            xl = xle_s[pl.ds(r0 + SUBLANES, SUBLANES), :]
            xlp = xle_s[sl, :]
            acc_add(PV_LRU_B, du)
            acc_add(PV_LRU_W + 3, du * xl)
            acc_add(PV_LRU_W + 2, du * _shift_down(xl, xlp, 1, row))
            acc_add(PV_LRU_W + 1, du * _shift_down(xl, xlp, 2, row))
            acc_add(PV_LRU_W, du * _shift_down(xl, xlp, 3, row))
            dxl = (((_pvb(pv_ref, PV_LRU_W + 3) * du + _pvb(pv_ref, PV_LRU_W + 2) * _shift_up(du, du_next, 1, row))
                    + _pvb(pv_ref, PV_LRU_W + 1) * _shift_up(du, du_next, 2, row))
                   + _pvb(pv_ref, PV_LRU_W) * _shift_up(du, du_next, 3, row))
            return du, (dxl,)

        dun_s[...] = _for_groups(ng, lru_conv_group, dun_s[...], descending=True, stores=(_rows_of(dp_ref, 4),))

        @pl.when(first_block)
        def _():
            sv_ref[...] = jnp.zeros_like(sv_ref)
            for k in range(N_ACC):
                tot = jnp.sum(acc_s[k], axis=0, keepdims=True)
                if k == PV_LAM:
                    tot = (RG_LRU_C * tot) / (1.0 + jnp.exp(pv_ref[PV_LAM:PV_LAM + 1, :]))
                sv_ref[k:k + 1, :] = tot

    def part(p):
        return pl.BlockSpec((None, tb, LW), lambda c, i: (2 * p + c // STRIPS_PER_CHUNK, nt - 1 - i, c % STRIPS_PER_CHUNK))

    def halo(p):
        return pl.BlockSpec((None, SUBLANES, LW), lambda c, i: (2 * p + c // STRIPS_PER_CHUNK,
                                                                jnp.maximum((nt - 1 - i) * gpb - 1, 0), c % STRIPS_PER_CHUNK))

    strip = pl.BlockSpec((tb, LW), lambda c, i: (nt - 1 - i, c))
    big = pltpu.VMEM((tb, LW), F32)
    big_e = pltpu.VMEM((tb + SUBLANES, LW), F32)
    small = pltpu.VMEM((SUBLANES, LW), F32)
    outs = pl.pallas_call(
        body, grid=(NS, nt),
        in_specs=[part(p) for p in range(N_PARTS)] + [
            strip, strip, strip, strip, strip, pl.BlockSpec((tb, LW), lambda c, i: (nt - 1 - i, NS + c)),
            halo(1), halo(2), halo(4),
            pl.BlockSpec((SUBLANES, LW), lambda c, i: (jnp.maximum((nt - 1 - i) * gpb - 1, 0), c)),
            pl.BlockSpec((PV_ROWS, LW), lambda c, i: (0, c)),
            pl.BlockSpec((1, LW, 2 * LW), lambda c, i: (c, 0, 0)),
            pl.BlockSpec(memory_space=pl.ANY), pl.BlockSpec(memory_space=pl.ANY)],
        out_specs=(pl.BlockSpec((N_PARTS, tb, LW), lambda c, i: (0, nt - 1 - i, c)),
                   pl.BlockSpec((1, LW, 2 * LW), lambda c, i: (c, 0, 0)),
                   pl.BlockSpec((PV_ROWS, LW), lambda c, i: (0, c)),
                   pl.BlockSpec(memory_space=pl.ANY), pl.BlockSpec(memory_space=pl.ANY)),
        out_shape=(jax.ShapeDtypeStruct((N_PARTS, t, D_PART), MXU_DTYPE),
                   jax.ShapeDtypeStruct((NS, LW, 2 * LW), F32), jax.ShapeDtypeStruct((PV_ROWS, D_PART), F32),
                   jax.ShapeDtypeStruct((3, 1, half, cols), jnp.bfloat16),
                   jax.ShapeDtypeStruct((1, half, cols), F32)),
        scratch_shapes=[small, pltpu.VMEM((tb, LW), MXU_DTYPE), big_e, big_e, big_e,
                        pltpu.VMEM((tb, 2 * LW), MXU_DTYPE), big, small,
                        pltpu.VMEM((N_ACC, SUBLANES, LW), F32), small, small, small,
                        pltpu.VMEM((n_blocks, half, cols), F32), pltpu.VMEM((n_blocks, half, cols), jnp.bfloat16),
                        pltpu.VMEM((n_blocks, half, cols), jnp.bfloat16),
                        pltpu.SemaphoreType.DMA((3,)), pltpu.SemaphoreType.DMA((3,)), pltpu.SemaphoreType.DMA((4,))],
        compiler_params=_cp(ARB, ARB), name="mixer_backward",
    )(proj, proj, proj, proj, proj, proj, h, u, r, ig, dy, dy, proj, proj, proj, h, pvec, wai, go, gob)
    return outs


def _adamw(w, g, m, v):
    m = ADAM_B1 * m + (1.0 - ADAM_B1) * g
    v = ADAM_B2 * v + (1.0 - ADAM_B2) * (g * g)
    m_hat = m / (1.0 - ADAM_B1 ** ADAM_STEP)
    v_hat = v / (1.0 - ADAM_B2 ** ADAM_STEP)
    delta = -ADAM_LR * (m_hat / (jnp.sqrt(v_hat) + ADAM_EPS) + ADAM_WD * w)
    return delta, m, v


def _adamw_blocks(f_in, f_out, p_in, p_out):
    n_pieces = 4
    slab = 2 * SUBLANES

    def body(fi_hbm, fo_hbm, wi_hbm, mi_hbm, vi_hbm, wo_hbm, mo_hbm, vo_hbm,
             gi_hbm, di_hbm, nmi_hbm, nvi_hbm, do_hbm, nmo_hbm, nvo_hbm,
             g_in, g_out, wi, mi, vi, wo, mo, vo, load_sems, store_sems):
        blocks = ((g_in, fi_hbm, (wi, mi, vi), (wi_hbm, mi_hbm, vi_hbm), (di_hbm, nmi_hbm, nvi_hbm)),
                  (g_out, fo_hbm, (wo, mo, vo), (wo_hbm, mo_hbm, vo_hbm), (do_hbm, nmo_hbm, nvo_hbm)))
        loads, stores = [], []

        def start(src, dst, sems, group):
            group.append(pltpu.make_async_copy(src, dst, sems.at[len(group)]))
            group[-1].start()

        piece = lambda g, i: slice(g.shape[1] // n_pieces * i, g.shape[1] // n_pieces * (i + 1))
        for i in range(n_pieces):
            for g, f_hbm, p, p_hbm, _ in blocks:
                rows = piece(g, i)
                start(f_hbm.at[:, rows, :], g.at[:, rows, :], load_sems, loads)
                for s, s_hbm in zip(p, p_hbm):
                    start(s_hbm.at[rows, :], s.at[rows, :], load_sems, loads)
        per_piece = len(loads) // n_pieces
        for i in range(n_pieces):
            for cp in loads[per_piece * i:per_piece * (i + 1)]:
                cp.wait()
            for g, _, (w_s, m_s, v_s), _, outs in blocks:
                rows = piece(g, i)
                n, cols = g.shape[0], g.shape[2]

                def step(t, carry):
                    rs = pl.ds(pl.multiple_of(rows.start + slab * t, slab), slab)
                    for q in range(n):
                        cs = slice(cols * q, cols * (q + 1))
                        w_s[rs, cs], m_s[rs, cs], v_s[rs, cs] = _adamw(w_s[rs, cs], g[q, rs, :], m_s[rs, cs], v_s[rs, cs])
                    return carry

                lax.fori_loop(0, (rows.stop - rows.start) // slab, step, 0)
                for s, o_hbm in zip((w_s, m_s, v_s), outs):
                    start(s.at[rows, :], o_hbm.at[rows, :], store_sems, stores)
            rows, cols = piece(g_in, i), g_in.shape[2]
            for q in range(g_in.shape[0]):
                start(g_in.at[q, rows, :], gi_hbm.at[rows, cols * q:cols * (q + 1)], store_sems, stores)
        for cp in stores:
            cp.wait()

    w_i, w_o = p_in[0], p_out[0]
    n_in = f_in.shape[0]
    assert w_i.shape == (f_in.shape[1], n_in * f_in.shape[2]) and w_o.shape == (f_out.shape[1], f_out.shape[0] * f_out.shape[2])
    hbm = pl.BlockSpec(memory_space=pl.ANY)
    return pl.pallas_call(
        body, in_specs=[hbm] * 8, out_specs=(hbm,) * 7,
        out_shape=(jax.ShapeDtypeStruct(w_i.shape, F32),) * 4 + (jax.ShapeDtypeStruct(w_o.shape, F32),) * 3,
        scratch_shapes=[pltpu.VMEM(f_in.shape, F32), pltpu.VMEM(f_out.shape, F32)]
        + [pltpu.VMEM(w_i.shape, F32)] * 3 + [pltpu.VMEM(w_o.shape, F32)] * 3
        + [pltpu.SemaphoreType.DMA((n_pieces * 8,)), pltpu.SemaphoreType.DMA((n_pieces * (6 + n_in),))],
        compiler_params=_cp(), name="adamw_blocks",
    )(f_in, f_out, *p_in, *p_out)


def _adam_small(ws, ms, vs, gs):
    n = len(ws)

    def body(*refs):
        w_r, m_r, v_r, g_r = refs[0:n], refs[n:2 * n], refs[2 * n:3 * n], refs[3 * n:4 * n]
        d_o, m_o, v_o = refs[4 * n:5 * n], refs[5 * n:6 * n], refs[6 * n:7 * n]
        for j in range(n):
            d_o[j][...], m_o[j][...], v_o[j][...] = _adamw(w_r[j][...], g_r[j][...], m_r[j][...], v_r[j][...])

    vm = pl.BlockSpec(memory_space=pltpu.VMEM)
    shapes = tuple(jax.ShapeDtypeStruct(w.shape, F32) for w in ws)
    outs = pl.pallas_call(
        body, in_specs=[vm] * (4 * n), out_specs=(vm,) * (3 * n), out_shape=shapes * 3,
        compiler_params=_cp(), name="adam_small",
    )(*ws, *ms, *vs, *gs)
    return outs[0:n], outs[n:2 * n], outs[2 * n:3 * n]


def _block_diag_strips(w, lw):
    heads = lw // LRU_HEAD
    w4 = w.reshape(D_PART // lw, heads, LRU_HEAD, LRU_HEAD)
    rows = [jnp.pad(w4[:, hh], ((0, 0), (0, 0), (LRU_HEAD * hh, lw - LRU_HEAD * (hh + 1)))) for hh in range(heads)]
    return jnp.concatenate(rows, axis=1)


def _gate_matrices(w_a, w_i, lw):
    return jnp.concatenate([_block_diag_strips(w_a, lw), _block_diag_strips(w_i, lw)], axis=2).astype(MXU_DTYPE)


def _strip_diag_blocks(g):
    g5 = g.reshape(NS, HEADS_PER_STRIP, LRU_HEAD, HEADS_PER_STRIP, LRU_HEAD)
    return jnp.stack([g5[:, hh, :, hh, :] for hh in range(HEADS_PER_STRIP)], axis=1).reshape(NS * HEADS_PER_STRIP, LRU_HEAD, LRU_HEAD)


def kernel(x, ln_g, w_in, conv_w, lru_conv_w, lru_conv_b, w_a, b_a, w_i, b_i, lam, conv_out_g, lru_out_g, w_out, final_g, loss_target, m_ln_g, m_w_in, m_conv_w, m_lru_conv_w, m_lru_conv_b, m_w_a, m_b_a, m_w_i, m_b_i, m_lam, m_conv_out_g, m_lru_out_g, m_w_out, m_final_g, v_ln_g, v_w_in, v_conv_w, v_lru_conv_w, v_lru_conv_b, v_w_a, v_b_a, v_w_i, v_b_i, v_lam, v_conv_out_g, v_lru_out_g, v_w_out, v_final_g):
    xi, yi, ci = lax.axis_index("x"), lax.axis_index("y"), lax.axis_index("c")
    k = 2 * xi + yi
    t = x.shape[1]
    x2 = x.reshape(t, D_MODEL)
    tgt2 = loss_target.reshape(t, D_MODEL)
    row = lambda a: a.reshape(1, -1)

    small = jnp.concatenate([conv_w, lru_conv_w, jnp.zeros((1, conv_w.shape[1]), F32)], axis=0)
    proj, xn, w12, sm4 = _gather_in_projection(x2, row(ln_g), w_in, small)
    convs = jnp.transpose(sm4, (1, 0, 2)).reshape(SUBLANES, D_PART)
    pvec = jnp.concatenate(
        [convs[0:7], row(lru_conv_b), row(b_a), row(b_i), row(lam), row(conv_out_g), row(lru_out_g),
         jnp.zeros((PV_ROWS - N_ACC, D_PART), F32)], axis=0)
    wai = _gate_matrices(w_a, w_i, LW)

    c_arr = jnp.reshape(ci, (1,)).astype(jnp.int32)
    yc, yl, h, u, r, ig, wo4 = _mixer_forward(proj, pvec, _gate_matrices(w_a, w_i, FWD_LW), w_out)
    wo = wo4.reshape(2 * D_PART, D_MODEL)
    do, dob, dy, st_out = _out_projection_loss(yc, yl, x2, tgt2, wo, row(final_g))
    go4, go4b = _w_out_grad(yc, yl, dob)
    dproj, g_wai, svec, r2o, s_out = _mixer_backward(proj, h, u, r, ig, dy, pvec, wai, go4, go4b)
    gwa = _strip_diag_blocks(g_wai[:, :, 0:LW]).reshape(LRU_HEAD, D_PART)
    gwi = _strip_diag_blocks(g_wai[:, :, LW:2 * LW]).reshape(LRU_HEAD, D_PART)
    g12, g12b, red = _w_in_grad(xn, dproj, jnp.concatenate([svec, st_out, gwa, gwi], axis=0))
    s_in, sb_in = _add_sibling_halves(g12, g12b, c_arr, "add_sibling_halves_in")
    grad_x, st_in, r2i = _input_grad(dproj, w12, x2, do, row(ln_g), sb_in)
    f_in, f_out, red_ln = _finish_gradients(s_in, r2i, s_out, r2o, st_in)
    r_out = PV_ROWS
    r_wa = PV_ROWS + SUBLANES
    r_wi = r_wa + LRU_HEAD
    loss = red[r_out + 1, 0]

    g_w_in, d_w_in, nm_w_in, nv_w_in, d_w_out, nm_w_out, nv_w_out = _adamw_blocks(
        f_in, f_out, (w_in, m_w_in, v_w_in), (w_out, m_w_out, v_w_out))
    g_w_out = f_out[0]

    ncol = conv_w.shape[1]
    conv_cols = lax.dynamic_slice(red, (0, k * ncol), (SUBLANES, ncol))
    g_small = {
        "ln_g": red_ln[0], "conv_w": conv_cols[0:3], "lru_conv_w": conv_cols[3:7], "lru_conv_b": red[PV_LRU_B],
        "w_a": red[r_wa:r_wa + LRU_HEAD].reshape(w_a.shape), "b_a": red[PV_BA],
        "w_i": red[r_wi:r_wi + LRU_HEAD].reshape(w_i.shape), "b_i": red[PV_BI], "lam": red[PV_LAM],
        "conv_out_g": red[PV_CG], "lru_out_g": red[PV_LG], "final_g": red[r_out],
    }
    w_small = {"ln_g": ln_g, "conv_w": conv_w, "lru_conv_w": lru_conv_w, "lru_conv_b": lru_conv_b, "w_a": w_a, "b_a": b_a,
               "w_i": w_i, "b_i": b_i, "lam": lam, "conv_out_g": conv_out_g, "lru_out_g": lru_out_g, "final_g": final_g}
    m_small = {"ln_g": m_ln_g, "conv_w": m_conv_w, "lru_conv_w": m_lru_conv_w, "lru_conv_b": m_lru_conv_b, "w_a": m_w_a,
               "b_a": m_b_a, "w_i": m_w_i, "b_i": m_b_i, "lam": m_lam, "conv_out_g": m_conv_out_g,
               "lru_out_g": m_lru_out_g, "final_g": m_final_g}
    v_small = {"ln_g": v_ln_g, "conv_w": v_conv_w, "lru_conv_w": v_lru_conv_w, "lru_conv_b": v_lru_conv_b, "w_a": v_w_a,
               "b_a": v_b_a, "w_i": v_w_i, "b_i": v_b_i, "lam": v_lam, "conv_out_g": v_conv_out_g,
               "lru_out_g": v_lru_out_g, "final_g": v_final_g}
    names = list(w_small)
    as2d = lambda a: a.reshape(1, -1) if a.ndim == 1 else a
    d_s, m_s, v_s = _adam_small([as2d(w_small[n]) for n in names], [as2d(m_small[n]) for n in names],
                                [as2d(v_small[n]) for n in names], [as2d(g_small[n]) for n in names])
    back = lambda n, a: a.reshape(w_small[n].shape)
    grads = {n: g_small[n] for n in names}
    deltas = {n: back(n, a) for n, a in zip(names, d_s)}
    new_m = {n: back(n, a) for n, a in zip(names, m_s)}
    new_v = {n: back(n, a) for n, a in zip(names, v_s)}
    grads["w_in"], deltas["w_in"], new_m["w_in"], new_v["w_in"] = g_w_in, d_w_in, nm_w_in, nv_w_in
    grads["w_out"], deltas["w_out"], new_m["w_out"], new_v["w_out"] = g_w_out, d_w_out, nm_w_out, nv_w_out

    order = ["ln_g", "w_in", "conv_w", "lru_conv_w", "lru_conv_b", "w_a", "b_a", "w_i", "b_i", "lam", "conv_out_g",
             "lru_out_g", "w_out", "final_g"]
    return (loss, grad_x.reshape(x.shape), *[grads[n] for n in order], *[deltas[n] for n in order],
            *[new_m[n] for n in order], *[new_v[n] for n in order])
```

```python
import functools

import jax
import jax.numpy as jnp
from jax import lax
from jax.experimental import pallas as pl
from jax.experimental.pallas import tpu as pltpu

F32 = jnp.float32
MXU_DTYPE = jnp.bfloat16

D_MODEL = 1024
D_PART = 1024
N_PARTS = 6
CHUNK = 512
CHUNKS_PER_BLOCK = 3
N_CHUNKS = 12
N_CHIPS = 4
SUBLANES = 8
LANES = 128
LW = 256
FWD_LW = 512
UNROLL = 8
NS = D_PART // LW
STRIPS_PER_CHUNK = CHUNK // LW
CONV_HEAD = 128
LRU_HEAD = 64
HEADS_PER_STRIP = LW // LRU_HEAD
RMS_EPS = 1e-6
RG_LRU_C = 8.0
ADAM_LR = 0.001
ADAM_B1 = 0.9
ADAM_B2 = 0.999
ADAM_EPS = 1e-08
ADAM_WD = 0.01
ADAM_STEP = 10

PV_CONV_W = 0
PV_LRU_W = 3
PV_LRU_B = 7
PV_BA = 8
PV_BI = 9
PV_LAM = 10
PV_CG = 11
PV_LG = 12
PV_ROWS = 16
N_ACC = 13

SLAB = 128
MESH = pl.DeviceIdType.MESH
VMEM_LIMIT = 56 * 1024 * 1024
ARB = "arbitrary"


def _cp(*sem, **kw):
    return pltpu.CompilerParams(dimension_semantics=sem or None, vmem_limit_bytes=VMEM_LIMIT, **kw)


def _mm(a, b):
    return jnp.dot(a, b, preferred_element_type=F32)


def _mm_nt(a, b):
    return lax.dot_general(a, b, (((1,), (1,)), ((), ())), preferred_element_type=F32)


def _mm_tn(a, b):
    return lax.dot_general(a, b, (((0,), (0,)), ((), ())), preferred_element_type=F32)


def _sigmoid(x):
    return 0.5 * jnp.tanh(0.5 * x) + 0.5


def _log_sigmoid(x):
    z = jnp.exp(-jnp.abs(x))
    u = 1.0 + z
    log1p = jnp.where(u == 1.0, z, jnp.log(u) * z / (u - 1.0))
    return jnp.minimum(x, 0.0) - log1p


def _head_mean(z, head):
    out = []
    for k in range(z.shape[1] // LANES):
        zk = z[:, LANES * k:LANES * (k + 1)]
        if head == LANES:
            m = jnp.sum(zk, axis=-1, keepdims=True) * (1.0 / head)
            out.append(jnp.broadcast_to(m, zk.shape))
        else:
            lo = lax.broadcasted_iota(jnp.int32, zk.shape, 1) < head
            s_lo = jnp.sum(jnp.where(lo, zk, 0.0), axis=-1, keepdims=True)
            s_hi = jnp.sum(jnp.where(lo, 0.0, zk), axis=-1, keepdims=True)
            out.append(jnp.where(lo, s_lo, s_hi) * (1.0 / head))
    return jnp.concatenate(out, axis=1)


def _shift_down(cur, prev, d, row):
    return pltpu.roll(jnp.where(row < SUBLANES - d, cur, prev), d, 0)


def _shift_up(cur, nxt, d, row):
    return pltpu.roll(jnp.where(row >= d, cur, nxt), SUBLANES - d, 0)


def _scan8_fwd(a, b, row):
    A, B = a, b
    for d in (1, 2, 4):
        m = row >= d
        a_s = jnp.where(m, pltpu.roll(A, d, 0), 1.0)
        b_s = jnp.where(m, pltpu.roll(B, d, 0), 0.0)
        B = A * b_s + B
        A = A * a_s
    return A, B


def _scan8_rev(a, b, row):
    A, B = a, b
    for d in (1, 2, 4):
        m = row < SUBLANES - d
        a_s = jnp.where(m, pltpu.roll(A, SUBLANES - d, 0), 1.0)
        b_s = jnp.where(m, pltpu.roll(B, SUBLANES - d, 0), 0.0)
        B = A * b_s + B
        A = A * a_s
    return A, B


def _decay(r, ls8):
    la = r * ls8
    a = jnp.exp(la)
    e2 = a * a
    em = -jnp.tanh(la) * (1.0 + e2)
    inv_mult = lax.rsqrt(em)
    return a, e2, em * inv_mult, inv_mult


def _mesh_pos():
    x, y, c = lax.axis_index("x"), lax.axis_index("y"), lax.axis_index("c")
    chips = [(1 - x, y), (x, 1 - y), (1 - x, 1 - y)]
    return x, y, c, chips


def _gather_in_projection(x, ln_g, w_in, small):
    t = x.shape[0]
    rb_x = 512
    rb_mm = 2048
    n_mm = t // rb_mm
    half = w_in.shape[0] // 2

    def body(x_hbm, g_ref, wi_ref, sm_ref, proj_hbm, xn_ref, w12_ref, sm4_ref,
             xbuf, obuf, x_sems, o_sems, send_sems, recv_sems):
        x_, y_, c, chips = _mesh_pos()
        k = 2 * x_ + y_
        sib = (x_, y_, 1 - c)
        sm4_ref[k] = sm_ref[...]

        def remote(ref, sem, to):
            return pltpu.make_async_remote_copy(src_ref=ref, dst_ref=ref, send_sem=send_sems.at[sem],
                                                recv_sem=recv_sems.at[sem], device_id=to, device_id_type=MESH)

        def chunk_of(chip, s):
            return CHUNKS_PER_BLOCK * (2 * chip[0] + chip[1]) + s

        def piece(q, core, first=0, rows=half):
            return w12_ref.at[q, pl.ds(pl.multiple_of(half * core + first, SUBLANES * 2), rows), :]

        nbr_x, nbr_y, diagonal = chips
        quarter = half // 2
        DIAG = [(0, 0, half, 0), (1, 0, quarter, 0), (1, quarter, quarter, 1), (2, 0, half, 1)]
        ici = lambda m, s: 2 * s + m
        dgn = lambda j: 6 + j
        to_sib = 10
        sml = lambda m: 20 + m

        sends = []
        for s in range(CHUNKS_PER_BLOCK):
            w12_ref[chunk_of((x_, y_), s)] = wi_ref[:, CHUNK * s:CHUNK * (s + 1)].astype(MXU_DTYPE)
            for m, chip in enumerate((nbr_x, nbr_y)):
                sends.append(remote(piece(chunk_of((x_, y_), s), c), ici(m, s), (*chip, c)))
                sends[-1].start()
        for m, chip in enumerate(chips):
            sends.append(remote(sm4_ref.at[k], sml(m), (*chip, c)))
            sends[-1].start()

        def x_copy(rb, slot):
            return pltpu.make_async_copy(x_hbm.at[pl.ds(rb * rb_x, rb_x), :], xbuf.at[slot], x_sems.at[slot])

        x_copy(0, 0).start()
        for rb in range(t // rb_x):
            slot = rb % 2
            x_copy(rb, slot).wait()
            if rb + 1 < t // rb_x:
                x_copy(rb + 1, 1 - slot).start()

            def norm_slab(sl, carry, rb=rb, slot=slot):
                xf = xbuf[slot, pl.ds(pl.multiple_of(sl * SLAB, SLAB), SLAB), :]
                r = lax.rsqrt(jnp.mean(xf * xf, axis=-1, keepdims=True) + RMS_EPS)
                xn_ref[pl.ds(pl.multiple_of(rb * rb_x + sl * SLAB, SLAB), SLAB), :] = ((xf * r) * g_ref[...]).astype(MXU_DTYPE)
                return carry

            lax.fori_loop(0, rb_x // SLAB, norm_slab, 0)

        def out_copy(q, i):
            return pltpu.make_async_copy(obuf.at[i], proj_hbm.at[q, pl.ds(pl.multiple_of(i * rb_mm, rb_mm), rb_mm), :],
                                         o_sems.at[i])

        def project(q, very_first):
            def row_block(i, carry):
                if not very_first:
                    out_copy(q, i).wait()
                obuf[i] = _mm(xn_ref[pl.ds(pl.multiple_of(i * rb_mm, rb_mm), rb_mm), :], w12_ref[q])
                out_copy(q, i).start()
                return carry

            lax.fori_loop(0, n_mm, row_block, 0)

        for s in range(CHUNKS_PER_BLOCK):
            project(chunk_of((x_, y_), s), very_first=(s == 0))

        steps = []
        for s in range(CHUNKS_PER_BLOCK):
            for m, chip in enumerate((nbr_x, nbr_y)):
                onward = [(first, rows, dgn(j), chips[via]) for j, (cs, first, rows, via) in enumerate(DIAG)
                          if cs == s and via == 1 - m]
                steps.append((chunk_of(chip, s), [(0, half, ici(m, s))], onward))
        for s in range(CHUNKS_PER_BLOCK):
            steps.append((chunk_of(diagonal, s), [(first, rows, dgn(j)) for j, (cs, first, rows, _) in enumerate(DIAG) if cs == s], []))

        def project_when_whole(step):
            q, pieces, _ = step
            for first, rows, sem in pieces:
                remote(piece(q, 1 - c, first, rows), to_sib + sem, sib).wait_recv()
            project(q, very_first=False)

        passed = []
        for j, (q, pieces, onward) in enumerate(steps):
            for first, rows, sem in pieces:
                remote(piece(q, c, first, rows), sem, sib).wait_recv()
            for first, rows, sem, chip in onward:
                passed.append(remote(piece(q, c, first, rows), sem, (*chip, c)))
                passed[-1].start()
            for first, rows, sem in pieces:
                passed.append(remote(piece(q, c, first, rows), to_sib + sem, sib))
                passed[-1].start()
            if j > 0:
                project_when_whole(steps[j - 1])
        project_when_whole(steps[-1])

        for m, chip in enumerate(chips):
            remote(sm4_ref.at[2 * chip[0] + chip[1]], sml(m), sib).wait_recv()
        for cp in sends + passed:
            cp.wait_send()
        for i in range(n_mm):
            out_copy(0, i).wait()

    vm = pl.BlockSpec(memory_space=pltpu.VMEM)
    hbm = pl.BlockSpec(memory_space=pl.ANY)
    n_sems = 23
    return pl.pallas_call(
        body,
        out_shape=(jax.ShapeDtypeStruct((N_CHUNKS, t, CHUNK), F32), jax.ShapeDtypeStruct((t, D_MODEL), MXU_DTYPE),
                   jax.ShapeDtypeStruct((N_CHUNKS, w_in.shape[0], CHUNK), MXU_DTYPE),
                   jax.ShapeDtypeStruct((N_CHIPS,) + small.shape, F32)),
        in_specs=[hbm, vm, vm, vm], out_specs=(hbm, vm, vm, vm),
        scratch_shapes=[pltpu.VMEM((2, rb_x, D_MODEL), F32), pltpu.VMEM((n_mm, rb_mm, CHUNK), F32),
                        pltpu.SemaphoreType.DMA((2,)), pltpu.SemaphoreType.DMA((n_mm,)),
                        pltpu.SemaphoreType.DMA((n_sems,)), pltpu.SemaphoreType.DMA((n_sems,))],
        compiler_params=_cp(), name="gather_in_projection",
    )(x, ln_g, w_in, small)


def _allreduce_behind(step, when, in_ref, acc_s, rbufs, out_ref, send_sems, recv_sems):
    x, y, c, _ = _mesh_pos()
    peers = [(x, y, 1 - c), (1 - x, y, c), (x, 1 - y, c)]

    def exchange(ph):
        return pltpu.make_async_remote_copy(src_ref=acc_s, dst_ref=rbufs[ph], send_sem=send_sems.at[ph],
                                            recv_sem=recv_sems.at[ph], device_id=peers[ph], device_id_type=MESH)

    @pl.when(step == when[0])
    def _():
        acc_s[...] = in_ref[...]
        exchange(0).start()

    for ph in (1, 2):
        @pl.when(step == when[ph])
        def _(ph=ph):
            exchange(ph - 1).wait()
            acc_s[...] = acc_s[...] + rbufs[ph - 1][...]
            exchange(ph).start()

    @pl.when(step == when[3])
    def _():
        exchange(2).wait()
        out_ref[...] = acc_s[...] + rbufs[2][...]


def _add_sibling_halves(g, gb, c_arr, name):
    n, rows, cols = g.shape
    half = rows // 2
    per = 2
    steps = n // per

    def body(c_ref, g_ref, gb_hbm, o_ref, ob_ref, rbuf, send_sems, recv_sems):
        q = pl.program_id(0)
        x, y, c, _ = _mesh_pos()
        theirs = pl.ds(pl.multiple_of(half * (1 - c), half), half)

        def copy(j):
            blocks = pl.ds(j * per, per)
            return pltpu.make_async_remote_copy(src_ref=gb_hbm.at[blocks, theirs, :], dst_ref=rbuf.at[blocks], send_sem=send_sems.at[j],
                                                recv_sem=recv_sems.at[j], device_id=(x, y, 1 - c), device_id_type=MESH)

        @pl.when(q == 0)
        def _():
            for j in range(steps):
                copy(j).start()

        copy(q).wait_recv()
        s = g_ref[...] + rbuf[pl.ds(q * per, per)].astype(F32)
        o_ref[...] = s
        ob_ref[...] = s.astype(jnp.bfloat16)

        @pl.when(q == steps - 1)
        def _():
            for j in range(steps):
                copy(j).wait_send()

    blk = pl.BlockSpec((per, half, cols), lambda q, c_ref: (q, 0, 0))
    return pl.pallas_call(
        body, out_shape=(jax.ShapeDtypeStruct((n, half, cols), F32), jax.ShapeDtypeStruct((n, half, cols), jnp.bfloat16)),
        grid_spec=pltpu.PrefetchScalarGridSpec(
            num_scalar_prefetch=1, grid=(steps,),
            in_specs=[pl.BlockSpec((per, half, cols), lambda q, c_ref: (q, c_ref[0], 0)), pl.BlockSpec(memory_space=pl.ANY)],
            out_specs=(blk, blk),
            scratch_shapes=[pltpu.VMEM((n, half, cols), jnp.bfloat16), pltpu.SemaphoreType.DMA((steps,)),
                            pltpu.SemaphoreType.DMA((steps,))]),
        compiler_params=_cp(ARB), name=name,
    )(c_arr, g, gb)


def _chip_block_copies(s_ref, r_ref, n_sub, send_sems, recv_sems):
    x, y, c, chips = _mesh_pos()
    cps = []
    for m, chip in enumerate(chips):
        kk = 2 * chip[0] + chip[1]
        for sub in range(n_sub):
            j = n_sub * m + sub
            cps.append(pltpu.make_async_remote_copy(
                src_ref=s_ref.at[pl.ds(n_sub * kk + sub, 1)], dst_ref=r_ref.at[m, pl.ds(sub, 1)],
                send_sem=send_sems.at[j], recv_sem=recv_sems.at[j], device_id=(*chip, c), device_id_type=MESH))
    return cps


def _gather_w_out(step, n_steps, wo_ref, wob_s, wo4_ref, local_sem, send_sems, recv_sems):
    x, y, c, chips = _mesh_pos()
    sib = (x, y, 1 - c)
    half = wo_ref.shape[0] // 2

    def rows(core):
        return pl.ds(pl.multiple_of(half * core, half), half)

    def block_half(chip, core):
        return wo4_ref.at[2 * chip[0] + chip[1], rows(core), :]

    def remote(src, dst, sem, to):
        return pltpu.make_async_remote_copy(src_ref=src, dst_ref=dst, send_sem=send_sems.at[sem], recv_sem=recv_sems.at[sem],
                                            device_id=to, device_id_type=MESH)

    local = pltpu.make_async_copy(wob_s, wo4_ref.at[2 * x + y], local_sem)
    ici = [remote(wob_s.at[rows(c), :], block_half((x, y), c), m, (*chip, c)) for m, chip in enumerate(chips)]
    fwd = [remote(block_half(chip, c), block_half(chip, c), 3 + m, sib) for m, chip in enumerate(chips)]

    @pl.when(step == 0)
    def _():
        wob_s[...] = wo_ref[...].astype(MXU_DTYPE)
        local.start()
        for cp in ici:
            cp.start()

    @pl.when(step == n_steps // 2)
    def _():
        for m, chip in enumerate(chips):
            remote(block_half(chip, c), block_half(chip, c), m, sib).wait_recv()
            fwd[m].start()

    @pl.when(step == n_steps - 1)
    def _():
        for m, chip in enumerate(chips):
            remote(block_half(chip, 1 - c), block_half(chip, 1 - c), 3 + m, sib).wait_recv()
        for cp in ici + fwd:
            cp.wait_send()
        local.wait()


def _chip_blocks_shape(s, n_sub):
    return jax.ShapeDtypeStruct((3, n_sub) + s.shape[1:], s.dtype)


def _finish_gradients(s_in, r_in, s_out, r_out, v):
    n_dev = 8
    n_in, n_out = r_in.shape[1], r_out.shape[1]

    def body(si_hbm, ri_hbm, so_hbm, ro_hbm, v_ref, fi_hbm, fo_hbm, tot_ref,
             a_in, b_in, a_out, b_out, slots, load_sems, store_sems, send_sems, recv_sems):
        x, y, c, _ = _mesh_pos()
        k = 2 * x + y
        sib = (x, y, 1 - c)
        me = 4 * x + 2 * y + c
        loads = [pltpu.make_async_copy(si_hbm.at[pl.ds(n_in * k, n_in)], a_in, load_sems.at[0]),
                 pltpu.make_async_copy(ri_hbm, b_in, load_sems.at[1]),
                 pltpu.make_async_copy(so_hbm.at[pl.ds(n_out * k, n_out)], a_out, load_sems.at[2]),
                 pltpu.make_async_copy(ro_hbm, b_out, load_sems.at[3])]
        for cp in loads:
            cp.start()
        slots[me] = v_ref[...]

        def remote(src, dst, sem, to):
            return pltpu.make_async_remote_copy(src_ref=src, dst_ref=dst, send_sem=send_sems.at[sem],
                                                recv_sem=recv_sems.at[sem], device_id=to, device_id_type=MESH)

        small = []
        for d in range(1, n_dev):
            peer = (1 - x if d & 4 else x, 1 - y if d & 2 else y, 1 - c if d & 1 else c)
            small.append(remote(slots.at[me], slots.at[me], d - 1, peer))
            small[-1].start()
        for cp in loads:
            cp.wait()
        big = []
        for j, (a, b, f_hbm) in enumerate(((a_in, b_in, fi_hbm), (a_out, b_out, fo_hbm))):
            a[...] = ((a[...] + b[0].astype(F32)) + b[1].astype(F32)) + b[2].astype(F32)
            half = a.shape[1]
            mine = f_hbm.at[:, pl.ds(pl.multiple_of(half * c, half), half), :]
            big.append(pltpu.make_async_copy(a, mine, store_sems.at[j]))
            big.append(remote(a, mine, n_dev - 1 + j, sib))
        for cp in big:
            cp.start()
        for cp in small + big:
            cp.wait()
        total = slots[0]
        for dev in range(1, n_dev):
            total = total + slots[dev]
        tot_ref[...] = total

    hbm = pl.BlockSpec(memory_space=pl.ANY)
    vm = pl.BlockSpec(memory_space=pltpu.VMEM)
    full = lambda s, n: (n, 2 * s.shape[1], s.shape[2])
    return pl.pallas_call(
        body,
        out_shape=(jax.ShapeDtypeStruct(full(s_in, n_in), F32), jax.ShapeDtypeStruct(full(s_out, n_out), F32),
                   jax.ShapeDtypeStruct(v.shape, F32)),
        in_specs=[hbm, hbm, hbm, hbm, vm], out_specs=(hbm, hbm, vm),
        scratch_shapes=[pltpu.VMEM((n_in,) + s_in.shape[1:], F32), pltpu.VMEM(r_in.shape, r_in.dtype),
                        pltpu.VMEM((n_out,) + s_out.shape[1:], F32), pltpu.VMEM(r_out.shape, r_out.dtype),
                        pltpu.VMEM((n_dev,) + v.shape, F32), pltpu.SemaphoreType.DMA((4,)), pltpu.SemaphoreType.DMA((2,)),
                        pltpu.SemaphoreType.DMA((n_dev + 1,)), pltpu.SemaphoreType.DMA((n_dev + 1,))],
        compiler_params=_cp(), name="finish_gradients",
    )(s_in, r_in, s_out, r_out, v)


def _out_projection_loss(yc, yl, x, target, wo, final_g):
    t = x.shape[0]
    tm = 512

    def body(yc_ref, yl_ref, x_ref, t_ref, wo_ref, fg_ref, do_ref, dob_ref, dy_ref, st_ref, y_wo):
        @pl.when(pl.program_id(0) == 0)
        def _():
            st_ref[...] = jnp.zeros_like(st_ref)

        y_wo[...] = _mm(yc_ref[...], wo_ref[0:D_PART, :]) + _mm(yl_ref[...], wo_ref[D_PART:2 * D_PART, :])

        def norm_loss_slab(s, carry):
            g_sum, loss_sum = carry
            rows = pl.ds(pl.multiple_of(s * SLAB, SLAB), SLAB)
            o = x_ref[rows, :] + y_wo[rows, :]
            r2 = lax.rsqrt(jnp.mean(o * o, axis=-1, keepdims=True) + RMS_EPS)
            ohat = o * r2
            fg = fg_ref[...]
            diff = ohat * fg - t_ref[rows, :]
            dout = diff * (1.0 / D_MODEL)
            gp = dout * fg
            do = r2 * (gp - ohat * jnp.mean(gp * ohat, axis=-1, keepdims=True))
            do_ref[rows, :] = do
            dob_ref[rows, :] = do.astype(MXU_DTYPE)
            loss = 0.5 * jnp.sum(jnp.sum(diff * diff, axis=-1, keepdims=True) * (1.0 / D_MODEL), axis=0, keepdims=True)
            return g_sum + jnp.sum(dout * ohat, axis=0, keepdims=True), loss_sum + loss

        g_sum, loss_sum = lax.fori_loop(0, tm // SLAB, norm_loss_slab,
                                        (jnp.zeros((1, D_MODEL), F32), jnp.zeros((1, 1), F32)))
        st_ref[0:1, :] += g_sum
        st_ref[1:2, :] += jnp.broadcast_to(loss_sum, (1, D_MODEL))
        dy_ref[...] = _mm_nt(dob_ref[...], wo_ref[...])

    row = lambda i: (i, 0)
    fix = lambda i: (0, 0)
    return pl.pallas_call(
        body, grid=(t // tm,),
        in_specs=[pl.BlockSpec((tm, D_PART), row), pl.BlockSpec((tm, D_PART), row),
                  pl.BlockSpec((tm, D_MODEL), row), pl.BlockSpec((tm, D_MODEL), row),
                  pl.BlockSpec((2 * D_PART, D_MODEL), fix), pl.BlockSpec((1, D_MODEL), fix)],
        out_specs=(pl.BlockSpec((tm, D_MODEL), row), pl.BlockSpec((tm, D_MODEL), row),
                   pl.BlockSpec((tm, 2 * D_PART), row), pl.BlockSpec((SUBLANES, D_MODEL), fix)),
        out_shape=(jax.ShapeDtypeStruct((t, D_MODEL), F32), jax.ShapeDtypeStruct((t, D_MODEL), MXU_DTYPE),
                   jax.ShapeDtypeStruct((t, 2 * D_PART), F32), jax.ShapeDtypeStruct((SUBLANES, D_MODEL), F32)),
        scratch_shapes=[pltpu.VMEM((tm, D_MODEL), F32)],
        compiler_params=_cp(ARB), name="out_projection_loss",
    )(yc, yl, x, target, wo, final_g)


def _input_grad(dproj, w12, x, do, ln_g, sb_in):
    t = x.shape[0]
    tm = 1024

    def body(dp_ref, w_ref, x_ref, do_ref, g_ref, s_ref, gx_ref, st_ref, r_ref, acc, send_sems, recv_sems):
        i, p = pl.program_id(0), pl.program_id(1)

        @pl.when((i == 0) & (p == 0))
        def _():
            st_ref[...] = jnp.zeros_like(st_ref)
            for cp in _chip_block_copies(s_ref, r_ref, CHUNKS_PER_BLOCK, send_sems, recv_sems):
                cp.start()

        @pl.when((i == t // tm - 1) & (p == N_PARTS - 1))
        def _():
            for cp in _chip_block_copies(s_ref, r_ref, CHUNKS_PER_BLOCK, send_sems, recv_sems):
                cp.wait()

        @pl.when(p == 0)
        def _():
            acc[...] = jnp.zeros_like(acc)

        acc[...] += _mm_nt(dp_ref[0], jnp.concatenate([w_ref[0], w_ref[1]], axis=1))

        @pl.when(p == N_PARTS - 1)
        def _():
            def norm_bwd_slab(s, g_sum):
                rows = pl.ds(pl.multiple_of(s * SLAB, SLAB), SLAB)
                xf = x_ref[rows, :]
                r = lax.rsqrt(jnp.mean(xf * xf, axis=-1, keepdims=True) + RMS_EPS)
                xhat = xf * r
                dxn = acc[rows, :]
                dxh = dxn * g_ref[...]
                gx_ref[rows, :] = do_ref[rows, :] + r * (dxh - xhat * jnp.mean(dxh * xhat, axis=-1, keepdims=True))
                return g_sum + jnp.sum(dxn * xhat, axis=0, keepdims=True)

            st_ref[0:1, :] += lax.fori_loop(0, tm // SLAB, norm_bwd_slab, jnp.zeros((1, D_MODEL), F32))

    row = lambda i, p: (i, 0)
    fix = lambda i, p: (0, 0)
    return pl.pallas_call(
        body, grid=(t // tm, N_PARTS),
        in_specs=[
            pl.BlockSpec((1, tm, D_PART), lambda i, p: (p, i, 0)),
            pl.BlockSpec((2, D_MODEL, CHUNK), lambda i, p: (p, 0, 0)),
            pl.BlockSpec((tm, D_MODEL), row), pl.BlockSpec((tm, D_MODEL), row), pl.BlockSpec((1, D_MODEL), fix),
            pl.BlockSpec(memory_space=pl.ANY)],
        out_specs=(pl.BlockSpec((tm, D_MODEL), row), pl.BlockSpec((SUBLANES, D_MODEL), fix),
                   pl.BlockSpec(memory_space=pl.ANY)),
        out_shape=(jax.ShapeDtypeStruct((t, D_MODEL), F32), jax.ShapeDtypeStruct((SUBLANES, D_MODEL), F32),
                   _chip_blocks_shape(sb_in, CHUNKS_PER_BLOCK)),
        scratch_shapes=[pltpu.VMEM((tm, D_MODEL), F32), pltpu.SemaphoreType.DMA((3 * CHUNKS_PER_BLOCK,)),
                        pltpu.SemaphoreType.DMA((3 * CHUNKS_PER_BLOCK,))],
        compiler_params=_cp(ARB, ARB), name="input_grad",
    )(dproj, w12, x, do, ln_g, sb_in)


def _w_in_grad(xn, dproj, small):
    t = xn.shape[0]
    small_shape = pltpu.VMEM(small.shape, F32)

    def body(xn_ref, dp_ref, sm_ref, o_ref, ob_ref, red_ref, acc_s, r0, r1, r2, send_sems, recv_sems):
        _allreduce_behind(pl.program_id(0), (0, 1, 3, N_PARTS - 1), sm_ref, acc_s, (r0, r1, r2), red_ref, send_sems, recv_sems)
        g = _mm_tn(xn_ref[...], dp_ref[0])
        for s in range(2):
            o_ref[s] = g[:, CHUNK * s:CHUNK * (s + 1)]
            ob_ref[s] = g[:, CHUNK * s:CHUNK * (s + 1)].astype(jnp.bfloat16)

    whole = pl.BlockSpec(small.shape, lambda p: (0, 0))
    pair = pl.BlockSpec((2, D_MODEL, CHUNK), lambda p: (p, 0, 0))
    return pl.pallas_call(
        body, grid=(N_PARTS,),
        in_specs=[pl.BlockSpec((t, D_MODEL), lambda p: (0, 0)),
                  pl.BlockSpec((1, t, D_PART), lambda p: (p, 0, 0)), whole],
        out_specs=(pair, pair, whole),
        out_shape=(jax.ShapeDtypeStruct((N_CHUNKS, D_MODEL, CHUNK), F32),
                   jax.ShapeDtypeStruct((N_CHUNKS, D_MODEL, CHUNK), jnp.bfloat16), jax.ShapeDtypeStruct(small.shape, F32)),
        scratch_shapes=[small_shape] * 4 + [pltpu.SemaphoreType.DMA((3,)), pltpu.SemaphoreType.DMA((3,))],
        compiler_params=_cp(ARB), name="w_in_grad",
    )(xn, dproj, small)


def _w_out_grad(yc, yl, dob):
    t = yc.shape[0]
    tk = 2048

    def body(yc_ref, yl_ref, do_ref, o_ref, ob_ref):
        j, kk = pl.program_id(0), pl.program_id(1)

        def accumulate(y_ref):
            @pl.when(kk == 0)
            def _():
                o_ref[...] = jnp.zeros_like(o_ref)

            o_ref[...] += _mm_tn(y_ref[...], do_ref[...])

            @pl.when(kk == t // tk - 1)
            def _():
                ob_ref[...] = o_ref[...].astype(jnp.bfloat16)

        pl.when(j == 0)(functools.partial(accumulate, yc_ref))
        pl.when(j == 1)(functools.partial(accumulate, yl_ref))

    def rows_of(half):
        return lambda j, kk: (jnp.where(j == half, kk, 0), 0)

    half = pl.BlockSpec((D_PART, D_MODEL), lambda j, kk: (j, 0))
    out, out_b = pl.pallas_call(
        body, grid=(2, t // tk),
        in_specs=[pl.BlockSpec((tk, D_PART), rows_of(0)), pl.BlockSpec((tk, D_PART), rows_of(1)),
                  pl.BlockSpec((tk, D_MODEL), lambda j, kk: (kk, 0))],
        out_specs=(half, half),
        out_shape=(jax.ShapeDtypeStruct((2 * D_PART, D_MODEL), F32), jax.ShapeDtypeStruct((2 * D_PART, D_MODEL), jnp.bfloat16)),
        compiler_params=_cp(ARB, ARB), name="w_out_grad",
    )(yc, yl, dob)
    blocks = (N_CHIPS, 2 * D_PART // N_CHIPS, D_MODEL)
    return out.reshape(blocks), out_b.reshape(blocks)


def _for_groups(n, fn, init, unroll=UNROLL, stores=(), descending=False):
    assert unroll % 2 == 0 and n % unroll == 0

    def trip(j, carry):
        held = None
        for uu in range(unroll):
            idx = j * unroll + uu
            carry, values = fn(idx, carry)
            if uu % 2 == 0:
                held = values
                continue
            low_group = n - 1 - idx if descending else idx - 1
            rows = pl.ds(pl.multiple_of(low_group * SUBLANES, 2 * SUBLANES), 2 * SUBLANES)
            pairs = zip(values, held) if descending else zip(held, values)
            for store, (lo, hi) in zip(stores, pairs, strict=True):
                store(rows, jnp.concatenate([lo, hi], axis=0).astype(MXU_DTYPE))
        return carry

    return lax.fori_loop(0, n // unroll, trip, init)


def _rows_of(ref, *lead, cols=slice(None)):
    def store(rows, value):
        ref[(*lead, rows, cols)] = value

    return store


def _pvb(pv_ref, r):
    return jnp.broadcast_to(pv_ref[r:r + 1, :], (SUBLANES, pv_ref.shape[1]))


def _conv3(pv_ref, u, u1, u2):
    return (_pvb(pv_ref, PV_CONV_W) * u2 + _pvb(pv_ref, PV_CONV_W + 1) * u1) + _pvb(pv_ref, PV_CONV_W + 2) * u


def _conv4(pv_ref, v, v1, v2, v3):
    return ((((_pvb(pv_ref, PV_LRU_W) * v3 + _pvb(pv_ref, PV_LRU_W + 1) * v2) + _pvb(pv_ref, PV_LRU_W + 2) * v1)
             + _pvb(pv_ref, PV_LRU_W + 3) * v) + _pvb(pv_ref, PV_LRU_B))


def _mixer_forward(proj, pvec, wai, w_out):
    t = proj.shape[1]
    tb = 512
    ng = tb // SUBLANES
    nt = t // tb
    lw = FWD_LW
    ns = D_PART // lw
    per_chunk = CHUNK // lw

    def body(bg_ref, cg_ref, xc_ref, gc_ref, xl_ref, gl_ref, pv_ref, wai_ref, wo_ref,
             yc_ref, yl_ref, h_ref, u_s, r_ref, ig_ref, wo4_ref,
             ucp_s, xlp_s, ls_s, hbuf_s, ub_s, gate_s, wob_s, local_sem, send_sems, recv_sems):
        _gather_w_out(pl.program_id(0) * nt + pl.program_id(1), ns * nt, wo_ref, wob_s, wo4_ref, local_sem, send_sems, recv_sems)

        @pl.when(pl.program_id(1) == 0)
        def _():
            ucp_s[...] = jnp.zeros_like(ucp_s)
            xlp_s[...] = jnp.zeros_like(xlp_s)
            hbuf_s[...] = jnp.zeros_like(hbuf_s)

        row = lax.broadcasted_iota(jnp.int32, (SUBLANES, lw), 0)
        ls_s[...] = RG_LRU_C * _log_sigmoid(_pvb(pv_ref, PV_LAM))

        def conv_group(g, carry):
            ucp, xlp = carry
            sl = pl.ds(pl.multiple_of(g * SUBLANES, SUBLANES), SUBLANES)
            uc = cg_ref[sl, :] * xc_ref[sl, :]
            v = _conv3(pv_ref, uc, _shift_down(uc, ucp, 1, row), _shift_down(uc, ucp, 2, row))
            yc = bg_ref[sl, :] * v
            rr = lax.rsqrt(_head_mean(yc * yc, CONV_HEAD) + RMS_EPS)
            gc = gc_ref[sl, :]
            zc = ((yc * rr) * _pvb(pv_ref, PV_CG)) * (gc * _sigmoid(gc))
            xl = xl_ref[sl, :]
            u = _conv4(pv_ref, xl, _shift_down(xl, xlp, 1, row), _shift_down(xl, xlp, 2, row), _shift_down(xl, xlp, 3, row))
            u_s[sl, :] = u
            return (uc, xl), (zc, u)

        ucp, xlp = _for_groups(ng, conv_group, (ucp_s[...], xlp_s[...]), unroll=2 * UNROLL,
                               stores=(_rows_of(yc_ref), _rows_of(ub_s)))
        ucp_s[...] = ucp
        xlp_s[...] = xlp

        gate_s[...] = _mm(ub_s[...], wai_ref[0])

        def lru_group(g, h_before):
            sl = pl.ds(pl.multiple_of(g * SUBLANES, SUBLANES), SUBLANES)
            u = u_s[sl, :]
            r = _sigmoid(gate_s[sl, 0:lw] + _pvb(pv_ref, PV_BA))
            ig = _sigmoid(gate_s[sl, lw:2 * lw] + _pvb(pv_ref, PV_BI))
            r_ref[sl, :] = r
            ig_ref[sl, :] = ig
            a, _, mult, _ = _decay(r, ls_s[...])
            A, B = _scan8_fwd(a, mult * (ig * u), row)
            h = B + A * jnp.broadcast_to(h_before[SUBLANES - 1:SUBLANES, :], (SUBLANES, lw))
            h_ref[sl, :] = h
            rr = lax.rsqrt(_head_mean(h * h, LRU_HEAD) + RMS_EPS)
            gl = gl_ref[sl, :]
            return h, (((h * rr) * _pvb(pv_ref, PV_LG)) * (gl * _sigmoid(gl)),)

        hbuf_s[...] = _for_groups(ng, lru_group, hbuf_s[...], unroll=2 * UNROLL, stores=(_rows_of(yl_ref),))

    def part(p):
        return pl.BlockSpec((None, tb, lw), lambda c, i: (2 * p + c // per_chunk, i, c % per_chunk))

    strip = pl.BlockSpec((tb, lw), lambda c, i: (i, c))
    small = pltpu.VMEM((SUBLANES, lw), F32)
    return pl.pallas_call(
        body, grid=(ns, nt),
        in_specs=[part(p) for p in range(N_PARTS)] + [
            pl.BlockSpec((PV_ROWS, lw), lambda c, i: (0, c)),
            pl.BlockSpec((1, lw, 2 * lw), lambda c, i: (c, 0, 0)),
            pl.BlockSpec(w_out.shape, lambda c, i: (0, 0))],
        out_specs=(strip,) * 6 + (pl.BlockSpec(memory_space=pl.ANY),),
        out_shape=(jax.ShapeDtypeStruct((t, D_PART), MXU_DTYPE),) * 2 + (jax.ShapeDtypeStruct((t, D_PART), F32),) * 4 + (
            jax.ShapeDtypeStruct((N_CHIPS,) + w_out.shape, MXU_DTYPE),),
        scratch_shapes=[small, small, small, small, pltpu.VMEM((tb, lw), MXU_DTYPE),
                        pltpu.VMEM((tb, 2 * lw), F32), pltpu.VMEM(w_out.shape, MXU_DTYPE),
                        pltpu.SemaphoreType.DMA, pltpu.SemaphoreType.DMA((6,)), pltpu.SemaphoreType.DMA((6,))],
        compiler_params=_cp(ARB, ARB), name="mixer_forward",
    )(proj, proj, proj, proj, proj, proj, pvec, wai, w_out)


def _mixer_backward(proj, h, u, r, ig, dy, pvec, wai, go, gob):
    n_blocks, half, cols = go.shape[0], go.shape[1] // 2, go.shape[2]
    t = proj.shape[1]
    tb = 1024
    ng = tb // SUBLANES
    nt = t // tb
    gpb = tb // SUBLANES

    def body(bg_ref, cg_ref, xc_ref, gc_ref, xl_ref, gl_ref, h_ref, u_ref, r_ref, ig_ref, dyc_ref, dyl_ref,
             cgh_ref, xch_ref, xlh_ref, hh_ref, pv_ref, wai_ref, go_hbm, gob_hbm,
             dp_ref, gw_ref, sv_ref, ro_ref, so_hbm,
             ls_s, ub_s, uce_s, xle_s, he_s, dgb_s, du_s, gbuf_s,
             acc_s, an_s, dvn_s, dun_s, sum_s, arrival_s, sumb_s, send_sems, recv_sems, sibling_sems):
        i = pl.program_id(1)
        first_block = i == nt - 1

        step = pl.program_id(0) * nt + i
        mesh_x, mesh_y, core, _ = _mesh_pos()
        mine = pl.ds(pl.multiple_of(half * core, half), half)
        theirs = pl.ds(pl.multiple_of(half * (1 - core), half), half)
        load = pltpu.make_async_copy(go_hbm.at[:, mine, :], sum_s, sibling_sems.at[0])
        swap = pltpu.make_async_remote_copy(src_ref=gob_hbm.at[:, theirs, :], dst_ref=arrival_s, send_sem=sibling_sems.at[1],
                                            recv_sem=sibling_sems.at[2], device_id=(mesh_x, mesh_y, 1 - core), device_id_type=MESH)
        store = pltpu.make_async_copy(sum_s, so_hbm, sibling_sems.at[3])

        @pl.when(step == 0)
        def _():
            load.start()
            swap.start()

        @pl.when(step == 1)
        def _():
            load.wait()
            swap.wait()
            sum_s[...] = sum_s[...] + arrival_s[...].astype(F32)
            sumb_s[...] = sum_s[...].astype(jnp.bfloat16)
            store.start()
            for cp in _chip_block_copies(sumb_s, ro_ref, 1, send_sems, recv_sems):
                cp.start()

        @pl.when(step == NS * nt - 1)
        def _():
            store.wait()
            for cp in _chip_block_copies(sumb_s, ro_ref, 1, send_sems, recv_sems):
                cp.wait()

        @pl.when(i == 0)
        def _():
            acc_s[...] = jnp.zeros_like(acc_s)
            gw_ref[...] = jnp.zeros_like(gw_ref)
            an_s[...] = jnp.zeros_like(an_s)
            dvn_s[...] = jnp.zeros_like(dvn_s)
            dun_s[...] = jnp.zeros_like(dun_s)
            gbuf_s[...] = jnp.zeros_like(gbuf_s)

        row = lax.broadcasted_iota(jnp.int32, (SUBLANES, LW), 0)
        ls_s[...] = RG_LRU_C * _log_sigmoid(_pvb(pv_ref, PV_LAM))
        keep = jnp.where(first_block, 0.0, 1.0)
        uce_s[0:SUBLANES, :] = (cgh_ref[...] * xch_ref[...]) * keep
        xle_s[0:SUBLANES, :] = xlh_ref[...] * keep
        he_s[0:SUBLANES, :] = hh_ref[...] * keep
        xle_s[SUBLANES:SUBLANES + tb, :] = xl_ref[...]
        he_s[SUBLANES:SUBLANES + tb, :] = h_ref[...]

        uce_s[SUBLANES:SUBLANES + tb, :] = cg_ref[...] * xc_ref[...]

        def acc_add(k, v):
            acc_s[k] += v

        def main_group(gi, carry):
            a_next, dv_next, g_next = carry
            g = ng - 1 - gi
            r0 = pl.multiple_of(g * SUBLANES, SUBLANES)
            sl = pl.ds(r0, SUBLANES)
            sl_e = pl.ds(r0 + SUBLANES, SUBLANES)
            lsb = ls_s[...]
            u = u_ref[sl, :]
            r = r_ref[sl, :]
            ig = ig_ref[sl, :]
            a, e2, mult, inv_mult = _decay(r, lsb)
            gl = gl_ref[sl, :]
            sg = _sigmoid(gl)
            s_l = gl * sg
            h8 = he_s[sl_e, :]
            hprev = _shift_down(h8, he_s[sl, :], 1, row)
            rr = lax.rsqrt(_head_mean(h8 * h8, LRU_HEAD) + RMS_EPS)
            n = h8 * rr
            dz = dyl_ref[sl, :]
            lg = _pvb(pv_ref, PV_LG)
            acc_add(PV_LG, (dz * n) * s_l)
            p5 = ((dz * n) * lg) * (sg + s_l * (1.0 - sg))
            dn = (dz * lg) * s_l
            dh = rr * (dn - n * _head_mean(dn * n, LRU_HEAD))
            A, B = _scan8_rev(_shift_up(a, a_next, 1, row), dh, row)
            gg = B + A * jnp.broadcast_to(g_next[0:1, :], (SUBLANES, LW))
            da = gg * hprev
            iu = ig * u
            diu = gg * mult
            dla = da * a - (gg * iu) * (e2 * inv_mult)
            acc_add(PV_LAM, dla * r)
            dra = (dla * lsb) * (r * (1.0 - r))
            dia = (diu * u) * (ig * (1.0 - ig))
            acc_add(PV_BA, dra)
            acc_add(PV_BI, dia)
            du_s[sl, :] = diu * ig
            bg = bg_ref[sl, :]
            gc = gc_ref[sl, :]
            uc = uce_s[sl_e, :]
            ucp = uce_s[sl, :]
            uc1 = _shift_down(uc, ucp, 1, row)
            uc2 = _shift_down(uc, ucp, 2, row)
            v = _conv3(pv_ref, uc, uc1, uc2)
            yc = bg * v
            rrc = lax.rsqrt(_head_mean(yc * yc, CONV_HEAD) + RMS_EPS)
            nc = yc * rrc
            sgc = _sigmoid(gc)
            s_c = gc * sgc
            dzc = dyc_ref[sl, :]
            cgain = _pvb(pv_ref, PV_CG)
            acc_add(PV_CG, (dzc * nc) * s_c)
            p3 = ((dzc * nc) * cgain) * (sgc + s_c * (1.0 - sgc))
            dnc = (dzc * cgain) * s_c
            dyc = rrc * (dnc - nc * _head_mean(dnc * nc, CONV_HEAD))
            dv = dyc * bg
            duc = (_pvb(pv_ref, PV_CONV_W + 2) * dv + _pvb(pv_ref, PV_CONV_W + 1) * _shift_up(dv, dv_next, 1, row)
                   + _pvb(pv_ref, PV_CONV_W) * _shift_up(dv, dv_next, 2, row))
            acc_add(PV_CONV_W + 2, dv * uc)
            acc_add(PV_CONV_W + 1, dv * uc1)
            acc_add(PV_CONV_W, dv * uc2)
            return (a, dv, gg), (dyc * v, duc * xc_ref[sl, :], duc * cg_ref[sl, :], p3, p5, dra, dia, u)

        a_next, dv_next, g_next = _for_groups(
            ng, main_group, (an_s[...], dvn_s[...], gbuf_s[...]), descending=True,
            stores=(_rows_of(dp_ref, 0), _rows_of(dp_ref, 1), _rows_of(dp_ref, 2), _rows_of(dp_ref, 3), _rows_of(dp_ref, 5),
                    _rows_of(dgb_s, cols=slice(0, LW)), _rows_of(dgb_s, cols=slice(LW, 2 * LW)), _rows_of(ub_s)))
        an_s[...] = a_next
        dvn_s[...] = dv_next
        gbuf_s[...] = g_next

        dgb = dgb_s[...]
        du_s[...] += _mm_nt(dgb, wai_ref[0])
        gw_ref[0] += _mm_tn(ub_s[...], dgb)

        def lru_conv_group(gi, du_next):
            g = ng - 1 - gi
            r0 = pl.multiple_of(g * SUBLANES, SUBLANES)
            sl = pl.ds(r0, SUBLANES)
            du = du_s[sl, :]
            xl = xle_s[pl.ds(r0 + SUBLANES, SUBLANES), :]
            xlp = xle_s[sl, :]
            acc_add(PV_LRU_B, du)
            acc_add(PV_LRU_W + 3, du * xl)
            acc_add(PV_LRU_W + 2, du * _shift_down(xl, xlp, 1, row))
            acc_add(PV_LRU_W + 1, du * _shift_down(xl, xlp, 2, row))
            acc_add(PV_LRU_W, du * _shift_down(xl, xlp, 3, row))
            dxl = (((_pvb(pv_ref, PV_LRU_W + 3) * du + _pvb(pv_ref, PV_LRU_W + 2) * _shift_up(du, du_next, 1, row))
                    + _pvb(pv_ref, PV_LRU_W + 1) * _shift_up(du, du_next, 2, row))
                   + _pvb(pv_ref, PV_LRU_W) * _shift_up(du, du_next, 3, row))
            return du, (dxl,)

        dun_s[...] = _for_groups(ng, lru_conv_group, dun_s[...], descending=True, stores=(_rows_of(dp_ref, 4),))

        @pl.when(first_block)
        def _():
            sv_ref[...] = jnp.zeros_like(sv_ref)
            for k in range(N_ACC):
                tot = jnp.sum(acc_s[k], axis=0, keepdims=True)
                if k == PV_LAM:
                    tot = (RG_LRU_C * tot) / (1.0 + jnp.exp(pv_ref[PV_LAM:PV_LAM + 1, :]))
                sv_ref[k:k + 1, :] = tot

    def part(p):
        return pl.BlockSpec((None, tb, LW), lambda c, i: (2 * p + c // STRIPS_PER_CHUNK, nt - 1 - i, c % STRIPS_PER_CHUNK))

    def halo(p):
        return pl.BlockSpec((None, SUBLANES, LW), lambda c, i: (2 * p + c // STRIPS_PER_CHUNK,
                                                                jnp.maximum((nt - 1 - i) * gpb - 1, 0), c % STRIPS_PER_CHUNK))

    strip = pl.BlockSpec((tb, LW), lambda c, i: (nt - 1 - i, c))
    big = pltpu.VMEM((tb, LW), F32)
    big_e = pltpu.VMEM((tb + SUBLANES, LW), F32)
    small = pltpu.VMEM((SUBLANES, LW), F32)
    outs = pl.pallas_call(
        body, grid=(NS, nt),
        in_specs=[part(p) for p in range(N_PARTS)] + [
            strip, strip, strip, strip, strip, pl.BlockSpec((tb, LW), lambda c, i: (nt - 1 - i, NS + c)),
            halo(1), halo(2), halo(4),
            pl.BlockSpec((SUBLANES, LW), lambda c, i: (jnp.maximum((nt - 1 - i) * gpb - 1, 0), c)),
            pl.BlockSpec((PV_ROWS, LW), lambda c, i: (0, c)),
            pl.BlockSpec((1, LW, 2 * LW), lambda c, i: (c, 0, 0)),
            pl.BlockSpec(memory_space=pl.ANY), pl.BlockSpec(memory_space=pl.ANY)],
        out_specs=(pl.BlockSpec((N_PARTS, tb, LW), lambda c, i: (0, nt - 1 - i, c)),
                   pl.BlockSpec((1, LW, 2 * LW), lambda c, i: (c, 0, 0)),
                   pl.BlockSpec((PV_ROWS, LW), lambda c, i: (0, c)),
                   pl.BlockSpec(memory_space=pl.ANY), pl.BlockSpec(memory_space=pl.ANY)),
        out_shape=(jax.ShapeDtypeStruct((N_PARTS, t, D_PART), MXU_DTYPE),
                   jax.ShapeDtypeStruct((NS, LW, 2 * LW), F32), jax.ShapeDtypeStruct((PV_ROWS, D_PART), F32),
                   jax.ShapeDtypeStruct((3, 1, half, cols), jnp.bfloat16),
                   jax.ShapeDtypeStruct((n_blocks, half, cols), F32)),
        scratch_shapes=[small, pltpu.VMEM((tb, LW), MXU_DTYPE), big_e, big_e, big_e,
                        pltpu.VMEM((tb, 2 * LW), MXU_DTYPE), big, small,
                        pltpu.VMEM((N_ACC, SUBLANES, LW), F32), small, small, small,
                        pltpu.VMEM((n_blocks, half, cols), F32), pltpu.VMEM((n_blocks, half, cols), jnp.bfloat16),
                        pltpu.VMEM((n_blocks, half, cols), jnp.bfloat16),
                        pltpu.SemaphoreType.DMA((3,)), pltpu.SemaphoreType.DMA((3,)), pltpu.SemaphoreType.DMA((4,))],
        compiler_params=_cp(ARB, ARB), name="mixer_backward",
    )(proj, proj, proj, proj, proj, proj, h, u, r, ig, dy, dy, proj, proj, proj, h, pvec, wai, go, gob)
    return outs


def _adamw(w, g, m, v):
    m = ADAM_B1 * m + (1.0 - ADAM_B1) * g
    v = ADAM_B2 * v + (1.0 - ADAM_B2) * (g * g)
    m_hat = m / (1.0 - ADAM_B1 ** ADAM_STEP)
    v_hat = v / (1.0 - ADAM_B2 ** ADAM_STEP)
    delta = -ADAM_LR * (m_hat / (jnp.sqrt(v_hat) + ADAM_EPS) + ADAM_WD * w)
    return delta, m, v


def _adamw_blocks(f_in, f_out, p_in, p_out):
    n_pieces = 4
    slab = 2 * SUBLANES

    def body(fi_hbm, fo_hbm, wi_hbm, mi_hbm, vi_hbm, wo_hbm, mo_hbm, vo_hbm,
             gi_hbm, di_hbm, nmi_hbm, nvi_hbm, do_hbm, nmo_hbm, nvo_hbm,
             g_in, g_out, wi, mi, vi, wo, mo, vo, load_sems, store_sems):
        blocks = ((g_in, fi_hbm, (wi, mi, vi), (wi_hbm, mi_hbm, vi_hbm), (di_hbm, nmi_hbm, nvi_hbm)),
                  (g_out, fo_hbm, (wo, mo, vo), (wo_hbm, mo_hbm, vo_hbm), (do_hbm, nmo_hbm, nvo_hbm)))
        loads, stores = [], []

        def start(src, dst, sems, group):
            group.append(pltpu.make_async_copy(src, dst, sems.at[len(group)]))
            group[-1].start()

        piece = lambda g, i: slice(g.shape[1] // n_pieces * i, g.shape[1] // n_pieces * (i + 1))
        for i in range(n_pieces):
            for g, f_hbm, p, p_hbm, _ in blocks:
                rows = piece(g, i)
                start(f_hbm.at[:, rows, :], g.at[:, rows, :], load_sems, loads)
                for s, s_hbm in zip(p, p_hbm):
                    start(s_hbm.at[rows, :], s.at[rows, :], load_sems, loads)
        per_piece = len(loads) // n_pieces
        for i in range(n_pieces):
            for cp in loads[per_piece * i:per_piece * (i + 1)]:
                cp.wait()
            for g, _, (w_s, m_s, v_s), _, outs in blocks:
                rows = piece(g, i)
                n, cols = g.shape[0], g.shape[2]

                def step(t, carry):
                    rs = pl.ds(pl.multiple_of(rows.start + slab * t, slab), slab)
                    for q in range(n):
                        cs = slice(cols * q, cols * (q + 1))
                        w_s[rs, cs], m_s[rs, cs], v_s[rs, cs] = _adamw(w_s[rs, cs], g[q, rs, :], m_s[rs, cs], v_s[rs, cs])
                    return carry

                lax.fori_loop(0, (rows.stop - rows.start) // slab, step, 0)
                for s, o_hbm in zip((w_s, m_s, v_s), outs):
                    start(s.at[rows, :], o_hbm.at[rows, :], store_sems, stores)
            rows, cols = piece(g_in, i), g_in.shape[2]
            for q in range(g_in.shape[0]):
                start(g_in.at[q, rows, :], gi_hbm.at[rows, cols * q:cols * (q + 1)], store_sems, stores)
        for cp in stores:
            cp.wait()

    w_i, w_o = p_in[0], p_out[0]
    n_in = f_in.shape[0]
    assert w_i.shape == (f_in.shape[1], n_in * f_in.shape[2]) and w_o.shape == (f_out.shape[1], f_out.shape[0] * f_out.shape[2])
    hbm = pl.BlockSpec(memory_space=pl.ANY)
    return pl.pallas_call(
        body, in_specs=[hbm] * 8, out_specs=(hbm,) * 7,
        out_shape=(jax.ShapeDtypeStruct(w_i.shape, F32),) * 4 + (jax.ShapeDtypeStruct(w_o.shape, F32),) * 3,
        scratch_shapes=[pltpu.VMEM(f_in.shape, F32), pltpu.VMEM(f_out.shape, F32)]
        + [pltpu.VMEM(w_i.shape, F32)] * 3 + [pltpu.VMEM(w_o.shape, F32)] * 3
        + [pltpu.SemaphoreType.DMA((n_pieces * 8,)), pltpu.SemaphoreType.DMA((n_pieces * (6 + n_in),))],
        compiler_params=_cp(), name="adamw_blocks",
    )(f_in, f_out, *p_in, *p_out)


def _adam_small(ws, ms, vs, gs):
    n = len(ws)

    def body(*refs):
        w_r, m_r, v_r, g_r = refs[0:n], refs[n:2 * n], refs[2 * n:3 * n], refs[3 * n:4 * n]
        d_o, m_o, v_o = refs[4 * n:5 * n], refs[5 * n:6 * n], refs[6 * n:7 * n]
        for j in range(n):
            d_o[j][...], m_o[j][...], v_o[j][...] = _adamw(w_r[j][...], g_r[j][...], m_r[j][...], v_r[j][...])

    vm = pl.BlockSpec(memory_space=pltpu.VMEM)
    shapes = tuple(jax.ShapeDtypeStruct(w.shape, F32) for w in ws)
    outs = pl.pallas_call(
        body, in_specs=[vm] * (4 * n), out_specs=(vm,) * (3 * n), out_shape=shapes * 3,
        compiler_params=_cp(), name="adam_small",
    )(*ws, *ms, *vs, *gs)
    return outs[0:n], outs[n:2 * n], outs[2 * n:3 * n]


def _block_diag_strips(w, lw):
    heads = lw // LRU_HEAD
    w4 = w.reshape(D_PART // lw, heads, LRU_HEAD, LRU_HEAD)
    rows = [jnp.pad(w4[:, hh], ((0, 0), (0, 0), (LRU_HEAD * hh, lw - LRU_HEAD * (hh + 1)))) for hh in range(heads)]
    return jnp.concatenate(rows, axis=1)


def _gate_matrices(w_a, w_i, lw):
    return jnp.concatenate([_block_diag_strips(w_a, lw), _block_diag_strips(w_i, lw)], axis=2).astype(MXU_DTYPE)


def _strip_diag_blocks(g):
    g5 = g.reshape(NS, HEADS_PER_STRIP, LRU_HEAD, HEADS_PER_STRIP, LRU_HEAD)
    return jnp.stack([g5[:, hh, :, hh, :] for hh in range(HEADS_PER_STRIP)], axis=1).reshape(NS * HEADS_PER_STRIP, LRU_HEAD, LRU_HEAD)


def kernel(x, ln_g, w_in, conv_w, lru_conv_w, lru_conv_b, w_a, b_a, w_i, b_i, lam, conv_out_g, lru_out_g, w_out, final_g, loss_target, m_ln_g, m_w_in, m_conv_w, m_lru_conv_w, m_lru_conv_b, m_w_a, m_b_a, m_w_i, m_b_i, m_lam, m_conv_out_g, m_lru_out_g, m_w_out, m_final_g, v_ln_g, v_w_in, v_conv_w, v_lru_conv_w, v_lru_conv_b, v_w_a, v_b_a, v_w_i, v_b_i, v_lam, v_conv_out_g, v_lru_out_g, v_w_out, v_final_g):
    xi, yi, ci = lax.axis_index("x"), lax.axis_index("y"), lax.axis_index("c")
    k = 2 * xi + yi
    t = x.shape[1]
    x2 = x.reshape(t, D_MODEL)
    tgt2 = loss_target.reshape(t, D_MODEL)
    row = lambda a: a.reshape(1, -1)

    small = jnp.concatenate([conv_w, lru_conv_w, jnp.zeros((1, conv_w.shape[1]), F32)], axis=0)
    proj, xn, w12, sm4 = _gather_in_projection(x2, row(ln_g), w_in, small)
    convs = jnp.transpose(sm4, (1, 0, 2)).reshape(SUBLANES, D_PART)
    pvec = jnp.concatenate(
        [convs[0:7], row(lru_conv_b), row(b_a), row(b_i), row(lam), row(conv_out_g), row(lru_out_g),
         jnp.zeros((PV_ROWS - N_ACC, D_PART), F32)], axis=0)
    wai = _gate_matrices(w_a, w_i, LW)

    c_arr = jnp.reshape(ci, (1,)).astype(jnp.int32)
    yc, yl, h, u, r, ig, wo4 = _mixer_forward(proj, pvec, _gate_matrices(w_a, w_i, FWD_LW), w_out)
    wo = wo4.reshape(2 * D_PART, D_MODEL)
    do, dob, dy, st_out = _out_projection_loss(yc, yl, x2, tgt2, wo, row(final_g))
    go4, go4b = _w_out_grad(yc, yl, dob)
    dproj, g_wai, svec, r2o, s_out = _mixer_backward(proj, h, u, r, ig, dy, pvec, wai, go4, go4b)
    gwa = _strip_diag_blocks(g_wai[:, :, 0:LW]).reshape(LRU_HEAD, D_PART)
    gwi = _strip_diag_blocks(g_wai[:, :, LW:2 * LW]).reshape(LRU_HEAD, D_PART)
    g12, g12b, red = _w_in_grad(xn, dproj, jnp.concatenate([svec, st_out, gwa, gwi], axis=0))
    s_in, sb_in = _add_sibling_halves(g12, g12b, c_arr, "add_sibling_halves_in")
    grad_x, st_in, r2i = _input_grad(dproj, w12, x2, do, row(ln_g), sb_in)
    f_in, f_out, red_ln = _finish_gradients(s_in, r2i, s_out, r2o, st_in)
    r_out = PV_ROWS
    r_wa = PV_ROWS + SUBLANES
    r_wi = r_wa + LRU_HEAD
    loss = red[r_out + 1, 0]

    g_w_in, d_w_in, nm_w_in, nv_w_in, d_w_out, nm_w_out, nv_w_out = _adamw_blocks(
        f_in, f_out, (w_in, m_w_in, v_w_in), (w_out, m_w_out, v_w_out))
    g_w_out = f_out[0]

    ncol = conv_w.shape[1]
    conv_cols = lax.dynamic_slice(red, (0, k * ncol), (SUBLANES, ncol))
    g_small = {
        "ln_g": red_ln[0], "conv_w": conv_cols[0:3], "lru_conv_w": conv_cols[3:7], "lru_conv_b": red[PV_LRU_B],
        "w_a": red[r_wa:r_wa + LRU_HEAD].reshape(w_a.shape), "b_a": red[PV_BA],
        "w_i": red[r_wi:r_wi + LRU_HEAD].reshape(w_i.shape), "b_i": red[PV_BI], "lam": red[PV_LAM],
        "conv_out_g": red[PV_CG], "lru_out_g": red[PV_LG], "final_g": red[r_out],
    }
    w_small = {"ln_g": ln_g, "conv_w": conv_w, "lru_conv_w": lru_conv_w, "lru_conv_b": lru_conv_b, "w_a": w_a, "b_a": b_a,
               "w_i": w_i, "b_i": b_i, "lam": lam, "conv_out_g": conv_out_g, "lru_out_g": lru_out_g, "final_g": final_g}
    m_small = {"ln_g": m_ln_g, "conv_w": m_conv_w, "lru_conv_w": m_lru_conv_w, "lru_conv_b": m_lru_conv_b, "w_a": m_w_a,
               "b_a": m_b_a, "w_i": m_w_i, "b_i": m_b_i, "lam": m_lam, "conv_out_g": m_conv_out_g,
               "lru_out_g": m_lru_out_g, "final_g": m_final_g}
    v_small = {"ln_g": v_ln_g, "conv_w": v_conv_w, "lru_conv_w": v_lru_conv_w, "lru_conv_b": v_lru_conv_b, "w_a": v_w_a,
               "b_a": v_b_a, "w_i": v_w_i, "b_i": v_b_i, "lam": v_lam, "conv_out_g": v_conv_out_g,
               "lru_out_g": v_lru_out_g, "final_g": v_final_g}
    names = list(w_small)
    as2d = lambda a: a.reshape(1, -1) if a.ndim == 1 else a
    d_s, m_s, v_s = _adam_small([as2d(w_small[n]) for n in names], [as2d(m_small[n]) for n in names],
                                [as2d(v_small[n]) for n in names], [as2d(g_small[n]) for n in names])
    back = lambda n, a: a.reshape(w_small[n].shape)
    grads = {n: g_small[n] for n in names}
    deltas = {n: back(n, a) for n, a in zip(names, d_s)}
    new_m = {n: back(n, a) for n, a in zip(names, m_s)}
    new_v = {n: back(n, a) for n, a in zip(names, v_s)}
    grads["w_in"], deltas["w_in"], new_m["w_in"], new_v["w_in"] = g_w_in, d_w_in, nm_w_in, nv_w_in
    grads["w_out"], deltas["w_out"], new_m["w_out"], new_v["w_out"] = g_w_out, d_w_out, nm_w_out, nv_w_out

    order = ["ln_g", "w_in", "conv_w", "lru_conv_w", "lru_conv_b", "w_a", "b_a", "w_i", "b_i", "lam", "conv_out_g",
             "lru_out_g", "w_out", "final_g"]
    return (loss, grad_x.reshape(x.shape), *[grads[n] for n in order], *[deltas[n] for n in order],
            *[new_m[n] for n in order], *[new_v[n] for n in order])
```

```python
import functools

import jax
import jax.numpy as jnp
from jax import lax
from jax.experimental import pallas as pl
from jax.experimental.pallas import tpu as pltpu

F32 = jnp.float32
MXU_DTYPE = jnp.bfloat16

D_MODEL = 1024
D_PART = 1024
N_PARTS = 6
CHUNK = 512
CHUNKS_PER_BLOCK = 3
N_CHUNKS = 12
N_CHIPS = 4
SUBLANES = 8
LANES = 128
LW = 256
FWD_LW = 512
UNROLL = 8
NS = D_PART // LW
STRIPS_PER_CHUNK = CHUNK // LW
CONV_HEAD = 128
LRU_HEAD = 64
HEADS_PER_STRIP = LW // LRU_HEAD
RMS_EPS = 1e-6
RG_LRU_C = 8.0
ADAM_LR = 0.001
ADAM_B1 = 0.9
ADAM_B2 = 0.999
ADAM_EPS = 1e-08
ADAM_WD = 0.01
ADAM_STEP = 10

PV_CONV_W = 0
PV_LRU_W = 3
PV_LRU_B = 7
PV_BA = 8
PV_BI = 9
PV_LAM = 10
PV_CG = 11
PV_LG = 12
PV_ROWS = 16
N_ACC = 13

SLAB = 128
MESH = pl.DeviceIdType.MESH
VMEM_LIMIT = 56 * 1024 * 1024
ARB = "arbitrary"


def _cp(*sem, **kw):
    return pltpu.CompilerParams(dimension_semantics=sem or None, vmem_limit_bytes=VMEM_LIMIT, **kw)


def _mm(a, b):
    return jnp.dot(a, b, preferred_element_type=F32)


def _mm_nt(a, b):
    return lax.dot_general(a, b, (((1,), (1,)), ((), ())), preferred_element_type=F32)


def _mm_tn(a, b):
    return lax.dot_general(a, b, (((0,), (0,)), ((), ())), preferred_element_type=F32)


def _sigmoid(x):
    return 0.5 * jnp.tanh(0.5 * x) + 0.5


def _log_sigmoid(x):
    z = jnp.exp(-jnp.abs(x))
    u = 1.0 + z
    log1p = jnp.where(u == 1.0, z, jnp.log(u) * z / (u - 1.0))
    return jnp.minimum(x, 0.0) - log1p


def _head_mean(z, head):
    out = []
    for k in range(z.shape[1] // LANES):
        zk = z[:, LANES * k:LANES * (k + 1)]
        if head == LANES:
            m = jnp.sum(zk, axis=-1, keepdims=True) * (1.0 / head)
            out.append(jnp.broadcast_to(m, zk.shape))
        else:
            lo = lax.broadcasted_iota(jnp.int32, zk.shape, 1) < head
            s_lo = jnp.sum(jnp.where(lo, zk, 0.0), axis=-1, keepdims=True)
            s_hi = jnp.sum(jnp.where(lo, 0.0, zk), axis=-1, keepdims=True)
            out.append(jnp.where(lo, s_lo, s_hi) * (1.0 / head))
    return jnp.concatenate(out, axis=1)


def _shift_down(cur, prev, d, row):
    return pltpu.roll(jnp.where(row < SUBLANES - d, cur, prev), d, 0)


def _shift_up(cur, nxt, d, row):
    return pltpu.roll(jnp.where(row >= d, cur, nxt), SUBLANES - d, 0)


def _scan8_fwd(a, b, row):
    A, B = a, b
    for d in (1, 2, 4):
        m = row >= d
        a_s = jnp.where(m, pltpu.roll(A, d, 0), 1.0)
        b_s = jnp.where(m, pltpu.roll(B, d, 0), 0.0)
        B = A * b_s + B
        A = A * a_s
    return A, B


def _scan8_rev(a, b, row):
    A, B = a, b
    for d in (1, 2, 4):
        m = row < SUBLANES - d
        a_s = jnp.where(m, pltpu.roll(A, SUBLANES - d, 0), 1.0)
        b_s = jnp.where(m, pltpu.roll(B, SUBLANES - d, 0), 0.0)
        B = A * b_s + B
        A = A * a_s
    return A, B


def _decay(r, ls8):
    la = r * ls8
    a = jnp.exp(la)
    e2 = a * a
    em = -jnp.tanh(la) * (1.0 + e2)
    inv_mult = lax.rsqrt(em)
    return a, e2, em * inv_mult, inv_mult


def _mesh_pos():
    x, y, c = lax.axis_index("x"), lax.axis_index("y"), lax.axis_index("c")
    chips = [(1 - x, y), (x, 1 - y), (1 - x, 1 - y)]
    return x, y, c, chips


def _gather_in_projection(x, ln_g, w_in, small):
    t = x.shape[0]
    rb_x = 512
    rb_mm = 2048
    n_mm = t // rb_mm
    half = w_in.shape[0] // 2

    def body(x_hbm, g_ref, wi_ref, sm_ref, proj_hbm, xn_ref, w12_ref, sm4_ref,
             xbuf, obuf, x_sems, o_sems, send_sems, recv_sems):
        x_, y_, c, chips = _mesh_pos()
        k = 2 * x_ + y_
        sib = (x_, y_, 1 - c)
        sm4_ref[k] = sm_ref[...]

        def remote(ref, sem, to):
            return pltpu.make_async_remote_copy(src_ref=ref, dst_ref=ref, send_sem=send_sems.at[sem],
                                                recv_sem=recv_sems.at[sem], device_id=to, device_id_type=MESH)

        def chunk_of(chip, s):
            return CHUNKS_PER_BLOCK * (2 * chip[0] + chip[1]) + s

        def piece(q, core, first=0, rows=half):
            return w12_ref.at[q, pl.ds(pl.multiple_of(half * core + first, SUBLANES * 2), rows), :]

        nbr_x, nbr_y, diagonal = chips
        quarter = half // 2
        DIAG = [(0, 0, half, 0), (1, 0, quarter, 0), (1, quarter, quarter, 1), (2, 0, half, 1)]
        ici = lambda m, s: 2 * s + m
        dgn = lambda j: 6 + j
        to_sib = 10
        sml = lambda m: 20 + m

        sends = []
        for s in range(CHUNKS_PER_BLOCK):
            w12_ref[chunk_of((x_, y_), s)] = wi_ref[:, CHUNK * s:CHUNK * (s + 1)].astype(MXU_DTYPE)
            for m, chip in enumerate((nbr_x, nbr_y)):
                sends.append(remote(piece(chunk_of((x_, y_), s), c), ici(m, s), (*chip, c)))
                sends[-1].start()
        for m, chip in enumerate(chips):
            sends.append(remote(sm4_ref.at[k], sml(m), (*chip, c)))
            sends[-1].start()

        def x_copy(rb, slot):
            return pltpu.make_async_copy(x_hbm.at[pl.ds(rb * rb_x, rb_x), :], xbuf.at[slot], x_sems.at[slot])

        x_copy(0, 0).start()
        for rb in range(t // rb_x):
            slot = rb % 2
            x_copy(rb, slot).wait()
            if rb + 1 < t // rb_x:
                x_copy(rb + 1, 1 - slot).start()

            def norm_slab(sl, carry, rb=rb, slot=slot):
                xf = xbuf[slot, pl.ds(pl.multiple_of(sl * SLAB, SLAB), SLAB), :]
                r = lax.rsqrt(jnp.mean(xf * xf, axis=-1, keepdims=True) + RMS_EPS)
                xn_ref[pl.ds(pl.multiple_of(rb * rb_x + sl * SLAB, SLAB), SLAB), :] = ((xf * r) * g_ref[...]).astype(MXU_DTYPE)
                return carry

            lax.fori_loop(0, rb_x // SLAB, norm_slab, 0)

        def out_copy(q, i):
            return pltpu.make_async_copy(obuf.at[i], proj_hbm.at[q, pl.ds(pl.multiple_of(i * rb_mm, rb_mm), rb_mm), :],
                                         o_sems.at[i])

        def project(q, very_first):
            def row_block(i, carry):
                if not very_first:
                    out_copy(q, i).wait()
                obuf[i] = _mm(xn_ref[pl.ds(pl.multiple_of(i * rb_mm, rb_mm), rb_mm), :], w12_ref[q])
                out_copy(q, i).start()
                return carry

            lax.fori_loop(0, n_mm, row_block, 0)

        for s in range(CHUNKS_PER_BLOCK):
            project(chunk_of((x_, y_), s), very_first=(s == 0))

        steps = []
        for s in range(CHUNKS_PER_BLOCK):
            for m, chip in enumerate((nbr_x, nbr_y)):
                onward = [(first, rows, dgn(j), chips[via]) for j, (cs, first, rows, via) in enumerate(DIAG)
                          if cs == s and via == 1 - m]
                steps.append((chunk_of(chip, s), [(0, half, ici(m, s))], onward))
        for s in range(CHUNKS_PER_BLOCK):
            steps.append((chunk_of(diagonal, s), [(first, rows, dgn(j)) for j, (cs, first, rows, _) in enumerate(DIAG) if cs == s], []))

        def project_when_whole(step):
            q, pieces, _ = step
            for first, rows, sem in pieces:
                remote(piece(q, 1 - c, first, rows), to_sib + sem, sib).wait_recv()
            project(q, very_first=False)

        passed = []
        for j, (q, pieces, onward) in enumerate(steps):
            for first, rows, sem in pieces:
                remote(piece(q, c, first, rows), sem, sib).wait_recv()
            for first, rows, sem, chip in onward:
                passed.append(remote(piece(q, c, first, rows), sem, (*chip, c)))
                passed[-1].start()
            for first, rows, sem in pieces:
                passed.append(remote(piece(q, c, first, rows), to_sib + sem, sib))
                passed[-1].start()
            if j > 0:
                project_when_whole(steps[j - 1])
        project_when_whole(steps[-1])

        for m, chip in enumerate(chips):
            remote(sm4_ref.at[2 * chip[0] + chip[1]], sml(m), sib).wait_recv()
        for cp in sends + passed:
            cp.wait_send()
        for i in range(n_mm):
            out_copy(0, i).wait()

    vm = pl.BlockSpec(memory_space=pltpu.VMEM)
    hbm = pl.BlockSpec(memory_space=pl.ANY)
    n_sems = 23
    return pl.pallas_call(
        body,
        out_shape=(jax.ShapeDtypeStruct((N_CHUNKS, t, CHUNK), F32), jax.ShapeDtypeStruct((t, D_MODEL), MXU_DTYPE),
                   jax.ShapeDtypeStruct((N_CHUNKS, w_in.shape[0], CHUNK), MXU_DTYPE),
                   jax.ShapeDtypeStruct((N_CHIPS,) + small.shape, F32)),
        in_specs=[hbm, vm, vm, vm], out_specs=(hbm, vm, vm, vm),
        scratch_shapes=[pltpu.VMEM((2, rb_x, D_MODEL), F32), pltpu.VMEM((n_mm, rb_mm, CHUNK), F32),
                        pltpu.SemaphoreType.DMA((2,)), pltpu.SemaphoreType.DMA((n_mm,)),
                        pltpu.SemaphoreType.DMA((n_sems,)), pltpu.SemaphoreType.DMA((n_sems,))],
        compiler_params=_cp(), name="gather_in_projection",
    )(x, ln_g, w_in, small)


def _allreduce_behind(step, when, in_ref, acc_s, rbufs, out_ref, send_sems, recv_sems):
    x, y, c, _ = _mesh_pos()
    peers = [(x, y, 1 - c), (1 - x, y, c), (x, 1 - y, c)]

    def exchange(ph):
        return pltpu.make_async_remote_copy(src_ref=acc_s, dst_ref=rbufs[ph], send_sem=send_sems.at[ph],
                                            recv_sem=recv_sems.at[ph], device_id=peers[ph], device_id_type=MESH)

    @pl.when(step == when[0])
    def _():
        acc_s[...] = in_ref[...]
        exchange(0).start()

    for ph in (1, 2):
        @pl.when(step == when[ph])
        def _(ph=ph):
            exchange(ph - 1).wait()
            acc_s[...] = acc_s[...] + rbufs[ph - 1][...]
            exchange(ph).start()

    @pl.when(step == when[3])
    def _():
        exchange(2).wait()
        out_ref[...] = acc_s[...] + rbufs[2][...]


def _add_sibling_halves(g, gb, c_arr, name):
    n, rows, cols = g.shape
    half = rows // 2
    per = 2
    steps = n // per

    def body(c_ref, g_ref, gb_hbm, o_ref, ob_ref, rbuf, send_sems, recv_sems):
        q = pl.program_id(0)
        x, y, c, _ = _mesh_pos()
        theirs = pl.ds(pl.multiple_of(half * (1 - c), half), half)

        def copy(j):
            blocks = pl.ds(j * per, per)
            return pltpu.make_async_remote_copy(src_ref=gb_hbm.at[blocks, theirs, :], dst_ref=rbuf.at[blocks], send_sem=send_sems.at[j],
                                                recv_sem=recv_sems.at[j], device_id=(x, y, 1 - c), device_id_type=MESH)

        @pl.when(q == 0)
        def _():
            for j in range(steps):
                copy(j).start()

        copy(q).wait_recv()
        s = g_ref[...] + rbuf[pl.ds(q * per, per)].astype(F32)
        o_ref[...] = s
        ob_ref[...] = s.astype(jnp.bfloat16)

        @pl.when(q == steps - 1)
        def _():
            for j in range(steps):
                copy(j).wait_send()

    blk = pl.BlockSpec((per, half, cols), lambda q, c_ref: (q, 0, 0))
    return pl.pallas_call(
        body, out_shape=(jax.ShapeDtypeStruct((n, half, cols), F32), jax.ShapeDtypeStruct((n, half, cols), jnp.bfloat16)),
        grid_spec=pltpu.PrefetchScalarGridSpec(
            num_scalar_prefetch=1, grid=(steps,),
            in_specs=[pl.BlockSpec((per, half, cols), lambda q, c_ref: (q, c_ref[0], 0)), pl.BlockSpec(memory_space=pl.ANY)],
            out_specs=(blk, blk),
            scratch_shapes=[pltpu.VMEM((n, half, cols), jnp.bfloat16), pltpu.SemaphoreType.DMA((steps,)),
                            pltpu.SemaphoreType.DMA((steps,))]),
        compiler_params=_cp(ARB), name=name,
    )(c_arr, g, gb)


def _chip_block_copies(s_ref, r_ref, n_sub, send_sems, recv_sems):
    x, y, c, chips = _mesh_pos()
    cps = []
    for m, chip in enumerate(chips):
        kk = 2 * chip[0] + chip[1]
        cps.append(pltpu.make_async_remote_copy(
            src_ref=s_ref.at[pl.ds(n_sub * kk, n_sub)], dst_ref=r_ref.at[m],
            send_sem=send_sems.at[m], recv_sem=recv_sems.at[m], device_id=(*chip, c), device_id_type=MESH))
    return cps


def _gather_w_out(step, n_steps, wo_ref, wob_s, wo4_ref, local_sem, send_sems, recv_sems):
    x, y, c, chips = _mesh_pos()
    sib = (x, y, 1 - c)
    half = wo_ref.shape[0] // 2

    def rows(core):
        return pl.ds(pl.multiple_of(half * core, half), half)

    def block_half(chip, core):
        return wo4_ref.at[2 * chip[0] + chip[1], rows(core), :]

    def remote(src, dst, sem, to):
        return pltpu.make_async_remote_copy(src_ref=src, dst_ref=dst, send_sem=send_sems.at[sem], recv_sem=recv_sems.at[sem],
                                            device_id=to, device_id_type=MESH)

    local = pltpu.make_async_copy(wob_s, wo4_ref.at[2 * x + y], local_sem)
    ici = [remote(wob_s.at[rows(c), :], block_half((x, y), c), m, (*chip, c)) for m, chip in enumerate(chips)]
    fwd = [remote(block_half(chip, c), block_half(chip, c), 3 + m, sib) for m, chip in enumerate(chips)]

    @pl.when(step == 0)
    def _():
        wob_s[...] = wo_ref[...].astype(MXU_DTYPE)
        local.start()
        for cp in ici:
            cp.start()

    @pl.when(step == n_steps // 2)
    def _():
        for m, chip in enumerate(chips):
            remote(block_half(chip, c), block_half(chip, c), m, sib).wait_recv()
            fwd[m].start()

    @pl.when(step == n_steps - 1)
    def _():
        for m, chip in enumerate(chips):
            remote(block_half(chip, 1 - c), block_half(chip, 1 - c), 3 + m, sib).wait_recv()
        for cp in ici + fwd:
            cp.wait_send()
        local.wait()


def _two_phase_chip_copies(sb_ref, s_ref, r_ref, passing_s, own_s, sum_s, send_sems, recv_sems, load_sems):
    x, y, c, (xn, yn, diag) = _mesh_pos()
    block = lambda chip: pl.ds(CHUNKS_PER_BLOCK * (2 * chip[0] + chip[1]), CHUNKS_PER_BLOCK)
    half = sb_ref.shape[1] // 2

    def remote(src, dst, j, to):
        return pltpu.make_async_remote_copy(src_ref=src, dst_ref=dst, send_sem=send_sems.at[j], recv_sem=recv_sems.at[j],
                                            device_id=(*to, c), device_id_type=MESH)

    first, loads, second = [], [], []
    for h, (p1, p2) in enumerate(((xn, yn), (yn, xn))):
        rows = pl.ds(half * h, half)
        first.append(remote(sb_ref.at[block(p1), rows, :], r_ref.at[0, :, rows, :], 3 * h, p1))
        first.append(remote(sb_ref.at[block(diag), rows, :], passing_s.at[h], 3 * h + 1, p1))
        second.append(remote(sum_s.at[h], r_ref.at[1, :, rows, :], 3 * h + 2, p2))
        loads.append(pltpu.make_async_copy(s_ref.at[block(p2), rows, :], own_s.at[h], load_sems.at[h]))
    return first, loads, second


def _chip_blocks_shape(s, n_sub):
    return jax.ShapeDtypeStruct((3, n_sub) + s.shape[1:], s.dtype)


def _finish_gradients(s_in, r_in, s_out, r_out, v):
    n_dev = 8
    n_in, n_out = r_in.shape[1], r_out.shape[1]

    def body(si_hbm, ri_hbm, so_hbm, ro_hbm, v_ref, fi_hbm, fo_hbm, tot_ref,
             a_in, b_in, a_out, b_out, slots, load_sems, store_sems, send_sems, recv_sems):
        x, y, c, _ = _mesh_pos()
        k = 2 * x + y
        sib = (x, y, 1 - c)
        me = 4 * x + 2 * y + c
        loads = [pltpu.make_async_copy(si_hbm.at[pl.ds(n_in * k, n_in)], a_in, load_sems.at[0]),
                 pltpu.make_async_copy(ri_hbm, b_in, load_sems.at[1]),
                 pltpu.make_async_copy(so_hbm.at[pl.ds(n_out * k, n_out)], a_out, load_sems.at[2]),
                 pltpu.make_async_copy(ro_hbm, b_out, load_sems.at[3])]
        for cp in loads:
            cp.start()
        slots[me] = v_ref[...]

        def remote(src, dst, sem, to):
            return pltpu.make_async_remote_copy(src_ref=src, dst_ref=dst, send_sem=send_sems.at[sem],
                                                recv_sem=recv_sems.at[sem], device_id=to, device_id_type=MESH)

        small = []
        for d in range(1, n_dev):
            peer = (1 - x if d & 4 else x, 1 - y if d & 2 else y, 1 - c if d & 1 else c)
            small.append(remote(slots.at[me], slots.at[me], d - 1, peer))
            small[-1].start()
        for cp in loads:
            cp.wait()
        big = []
        for j, (a, b, f_hbm) in enumerate(((a_in, b_in, fi_hbm), (a_out, b_out, fo_hbm))):
            total = a[...]
            for m in range(b.shape[0]):
                total = total + b[m].astype(F32)
            a[...] = total
            half = a.shape[1]
            mine = f_hbm.at[:, pl.ds(pl.multiple_of(half * c, half), half), :]
            big.append(pltpu.make_async_copy(a, mine, store_sems.at[j]))
            big.append(remote(a, mine, n_dev - 1 + j, sib))
        for cp in big:
            cp.start()
        for cp in small + big:
            cp.wait()
        total = slots[0]
        for dev in range(1, n_dev):
            total = total + slots[dev]
        tot_ref[...] = total

    hbm = pl.BlockSpec(memory_space=pl.ANY)
    vm = pl.BlockSpec(memory_space=pltpu.VMEM)
    full = lambda s, n: (n, 2 * s.shape[1], s.shape[2])
    return pl.pallas_call(
        body,
        out_shape=(jax.ShapeDtypeStruct(full(s_in, n_in), F32), jax.ShapeDtypeStruct(full(s_out, n_out), F32),
                   jax.ShapeDtypeStruct(v.shape, F32)),
        in_specs=[hbm, hbm, hbm, hbm, vm], out_specs=(hbm, hbm, vm),
        scratch_shapes=[pltpu.VMEM((n_in,) + s_in.shape[1:], F32), pltpu.VMEM(r_in.shape, r_in.dtype),
                        pltpu.VMEM((n_out,) + s_out.shape[1:], F32), pltpu.VMEM(r_out.shape, r_out.dtype),
                        pltpu.VMEM((n_dev,) + v.shape, F32), pltpu.SemaphoreType.DMA((4,)), pltpu.SemaphoreType.DMA((2,)),
                        pltpu.SemaphoreType.DMA((n_dev + 1,)), pltpu.SemaphoreType.DMA((n_dev + 1,))],
        compiler_params=_cp(), name="finish_gradients",
    )(s_in, r_in, s_out, r_out, v)


def _out_projection_loss(yc, yl, x, target, wo, final_g):
    t = x.shape[0]
    tm = 512

    def body(yc_ref, yl_ref, x_ref, t_ref, wo_ref, fg_ref, do_ref, dob_ref, dy_ref, st_ref, y_wo):
        @pl.when(pl.program_id(0) == 0)
        def _():
            st_ref[...] = jnp.zeros_like(st_ref)

        y_wo[...] = _mm(yc_ref[...], wo_ref[0:D_PART, :]) + _mm(yl_ref[...], wo_ref[D_PART:2 * D_PART, :])

        def norm_loss_slab(s, carry):
            g_sum, loss_sum = carry
            rows = pl.ds(pl.multiple_of(s * SLAB, SLAB), SLAB)
            o = x_ref[rows, :] + y_wo[rows, :]
            r2 = lax.rsqrt(jnp.mean(o * o, axis=-1, keepdims=True) + RMS_EPS)
            ohat = o * r2
            fg = fg_ref[...]
            diff = ohat * fg - t_ref[rows, :]
            dout = diff * (1.0 / D_MODEL)
            gp = dout * fg
            do = r2 * (gp - ohat * jnp.mean(gp * ohat, axis=-1, keepdims=True))
            do_ref[rows, :] = do
            dob_ref[rows, :] = do.astype(MXU_DTYPE)
            loss = 0.5 * jnp.sum(jnp.sum(diff * diff, axis=-1, keepdims=True) * (1.0 / D_MODEL), axis=0, keepdims=True)
            return g_sum + jnp.sum(dout * ohat, axis=0, keepdims=True), loss_sum + loss

        g_sum, loss_sum = lax.fori_loop(0, tm // SLAB, norm_loss_slab,
                                        (jnp.zeros((1, D_MODEL), F32), jnp.zeros((1, 1), F32)))
        st_ref[0:1, :] += g_sum
        st_ref[1:2, :] += jnp.broadcast_to(loss_sum, (1, D_MODEL))
        dy_ref[...] = _mm_nt(dob_ref[...], wo_ref[...])

    row = lambda i: (i, 0)
    fix = lambda i: (0, 0)
    return pl.pallas_call(
        body, grid=(t // tm,),
        in_specs=[pl.BlockSpec((tm, D_PART), row), pl.BlockSpec((tm, D_PART), row),
                  pl.BlockSpec((tm, D_MODEL), row), pl.BlockSpec((tm, D_MODEL), row),
                  pl.BlockSpec((2 * D_PART, D_MODEL), fix), pl.BlockSpec((1, D_MODEL), fix)],
        out_specs=(pl.BlockSpec((tm, D_MODEL), row), pl.BlockSpec((tm, D_MODEL), row),
                   pl.BlockSpec((tm, 2 * D_PART), row), pl.BlockSpec((SUBLANES, D_MODEL), fix)),
        out_shape=(jax.ShapeDtypeStruct((t, D_MODEL), F32), jax.ShapeDtypeStruct((t, D_MODEL), MXU_DTYPE),
                   jax.ShapeDtypeStruct((t, 2 * D_PART), F32), jax.ShapeDtypeStruct((SUBLANES, D_MODEL), F32)),
        scratch_shapes=[pltpu.VMEM((tm, D_MODEL), F32)],
        compiler_params=_cp(ARB), name="out_projection_loss",
    )(yc, yl, x, target, wo, final_g)


def _input_grad(dproj, w12, x, do, ln_g, sb_in, s_in):
    t = x.shape[0]
    tm = 1024
    n_steps = t // tm * N_PARTS
    pass_step = n_steps // 2 - 1
    half = sb_in.shape[1] // 2

    def body(dp_ref, w_ref, x_ref, do_ref, g_ref, sb_ref, s_ref, gx_ref, st_ref, r_ref, acc,
             passing_s, own_s, sum_s, send_sems, recv_sems, load_sems):
        i, p = pl.program_id(0), pl.program_id(1)
        step = i * N_PARTS + p
        first, loads, second = _two_phase_chip_copies(sb_ref, s_ref, r_ref, passing_s, own_s, sum_s,
                                                      send_sems, recv_sems, load_sems)

        @pl.when(step == 0)
        def _():
            st_ref[...] = jnp.zeros_like(st_ref)
            for cp in first + loads:
                cp.start()

        @pl.when(step == pass_step)
        def _():
            for cp in first + loads:
                cp.wait()
            sum_s[...] = (own_s[...] + passing_s[...].astype(F32)).astype(jnp.bfloat16)
            for cp in second:
                cp.start()

        @pl.when(step == n_steps - 1)
        def _():
            for cp in second:
                cp.wait()

        @pl.when(p == 0)
        def _():
            acc[...] = jnp.zeros_like(acc)

        acc[...] += _mm_nt(dp_ref[0], jnp.concatenate([w_ref[0], w_ref[1]], axis=1))

        @pl.when(p == N_PARTS - 1)
        def _():
            def norm_bwd_slab(s, g_sum):
                rows = pl.ds(pl.multiple_of(s * SLAB, SLAB), SLAB)
                xf = x_ref[rows, :]
                r = lax.rsqrt(jnp.mean(xf * xf, axis=-1, keepdims=True) + RMS_EPS)
                xhat = xf * r
                dxn = acc[rows, :]
                dxh = dxn * g_ref[...]
                gx_ref[rows, :] = do_ref[rows, :] + r * (dxh - xhat * jnp.mean(dxh * xhat, axis=-1, keepdims=True))
                return g_sum + jnp.sum(dxn * xhat, axis=0, keepdims=True)

            st_ref[0:1, :] += lax.fori_loop(0, tm // SLAB, norm_bwd_slab, jnp.zeros((1, D_MODEL), F32))

    row = lambda i, p: (i, 0)
    fix = lambda i, p: (0, 0)
    return pl.pallas_call(
        body, grid=(t // tm, N_PARTS),
        in_specs=[
            pl.BlockSpec((1, tm, D_PART), lambda i, p: (p, i, 0)),
            pl.BlockSpec((2, D_MODEL, CHUNK), lambda i, p: (p, 0, 0)),
            pl.BlockSpec((tm, D_MODEL), row), pl.BlockSpec((tm, D_MODEL), row), pl.BlockSpec((1, D_MODEL), fix),
            pl.BlockSpec(memory_space=pl.ANY), pl.BlockSpec(memory_space=pl.ANY)],
        out_specs=(pl.BlockSpec((tm, D_MODEL), row), pl.BlockSpec((SUBLANES, D_MODEL), fix),
                   pl.BlockSpec(memory_space=pl.ANY)),
        out_shape=(jax.ShapeDtypeStruct((t, D_MODEL), F32), jax.ShapeDtypeStruct((SUBLANES, D_MODEL), F32),
                   jax.ShapeDtypeStruct((2, CHUNKS_PER_BLOCK) + sb_in.shape[1:], sb_in.dtype)),
        scratch_shapes=[pltpu.VMEM((tm, D_MODEL), F32),
                        pltpu.VMEM((2, CHUNKS_PER_BLOCK, half, sb_in.shape[2]), sb_in.dtype),
                        pltpu.VMEM((2, CHUNKS_PER_BLOCK, half, sb_in.shape[2]), F32),
                        pltpu.VMEM((2, CHUNKS_PER_BLOCK, half, sb_in.shape[2]), sb_in.dtype),
                        pltpu.SemaphoreType.DMA((6,)), pltpu.SemaphoreType.DMA((6,)), pltpu.SemaphoreType.DMA((2,))],
        compiler_params=_cp(ARB, ARB), name="input_grad",
    )(dproj, w12, x, do, ln_g, sb_in, s_in)


def _w_in_grad(xn, dproj, small):
    t = xn.shape[0]
    small_shape = pltpu.VMEM(small.shape, F32)

    def body(xn_ref, dp_ref, sm_ref, o_ref, ob_ref, red_ref, acc_s, r0, r1, r2, send_sems, recv_sems):
        _allreduce_behind(pl.program_id(0), (0, 1, 3, N_PARTS - 1), sm_ref, acc_s, (r0, r1, r2), red_ref, send_sems, recv_sems)
        g = _mm_tn(xn_ref[...], dp_ref[0])
        for s in range(2):
            o_ref[s] = g[:, CHUNK * s:CHUNK * (s + 1)]
            ob_ref[s] = g[:, CHUNK * s:CHUNK * (s + 1)].astype(jnp.bfloat16)

    whole = pl.BlockSpec(small.shape, lambda p: (0, 0))
    pair = pl.BlockSpec((2, D_MODEL, CHUNK), lambda p: (p, 0, 0))
    return pl.pallas_call(
        body, grid=(N_PARTS,),
        in_specs=[pl.BlockSpec((t, D_MODEL), lambda p: (0, 0)),
                  pl.BlockSpec((1, t, D_PART), lambda p: (p, 0, 0)), whole],
        out_specs=(pair, pair, whole),
        out_shape=(jax.ShapeDtypeStruct((N_CHUNKS, D_MODEL, CHUNK), F32),
                   jax.ShapeDtypeStruct((N_CHUNKS, D_MODEL, CHUNK), jnp.bfloat16), jax.ShapeDtypeStruct(small.shape, F32)),
        scratch_shapes=[small_shape] * 4 + [pltpu.SemaphoreType.DMA((3,)), pltpu.SemaphoreType.DMA((3,))],
        compiler_params=_cp(ARB), name="w_in_grad",
    )(xn, dproj, small)


def _w_out_grad(yc, yl, dob):
    t = yc.shape[0]
    tk = 2048

    def body(yc_ref, yl_ref, do_ref, o_ref, ob_ref):
        j, kk = pl.program_id(0), pl.program_id(1)

        def accumulate(y_ref):
            @pl.when(kk == 0)
            def _():
                o_ref[...] = jnp.zeros_like(o_ref)

            o_ref[...] += _mm_tn(y_ref[...], do_ref[...])

            @pl.when(kk == t // tk - 1)
            def _():
                ob_ref[...] = o_ref[...].astype(jnp.bfloat16)

        pl.when(j == 0)(functools.partial(accumulate, yc_ref))
        pl.when(j == 1)(functools.partial(accumulate, yl_ref))

    def rows_of(half):
        return lambda j, kk: (jnp.where(j == half, kk, 0), 0)

    half = pl.BlockSpec((D_PART, D_MODEL), lambda j, kk: (j, 0))
    out, out_b = pl.pallas_call(
        body, grid=(2, t // tk),
        in_specs=[pl.BlockSpec((tk, D_PART), rows_of(0)), pl.BlockSpec((tk, D_PART), rows_of(1)),
                  pl.BlockSpec((tk, D_MODEL), lambda j, kk: (kk, 0))],
        out_specs=(half, half),
        out_shape=(jax.ShapeDtypeStruct((2 * D_PART, D_MODEL), F32), jax.ShapeDtypeStruct((2 * D_PART, D_MODEL), jnp.bfloat16)),
        compiler_params=_cp(ARB, ARB), name="w_out_grad",
    )(yc, yl, dob)
    blocks = (N_CHIPS, 2 * D_PART // N_CHIPS, D_MODEL)
    return out.reshape(blocks), out_b.reshape(blocks)


def _for_groups(n, fn, init, unroll=UNROLL, stores=(), descending=False):
    assert unroll % 2 == 0 and n % unroll == 0

    def trip(j, carry):
        held = None
        for uu in range(unroll):
            idx = j * unroll + uu
            carry, values = fn(idx, carry)
            if uu % 2 == 0:
                held = values
                continue
            low_group = n - 1 - idx if descending else idx - 1
            rows = pl.ds(pl.multiple_of(low_group * SUBLANES, 2 * SUBLANES), 2 * SUBLANES)
            pairs = zip(values, held) if descending else zip(held, values)
            for store, (lo, hi) in zip(stores, pairs, strict=True):
                store(rows, jnp.concatenate([lo, hi], axis=0).astype(MXU_DTYPE))
        return carry

    return lax.fori_loop(0, n // unroll, trip, init)


def _rows_of(ref, *lead, cols=slice(None)):
    def store(rows, value):
        ref[(*lead, rows, cols)] = value

    return store


def _pvb(pv_ref, r):
    return jnp.broadcast_to(pv_ref[r:r + 1, :], (SUBLANES, pv_ref.shape[1]))


def _conv3(pv_ref, u, u1, u2):
    return (_pvb(pv_ref, PV_CONV_W) * u2 + _pvb(pv_ref, PV_CONV_W + 1) * u1) + _pvb(pv_ref, PV_CONV_W + 2) * u


def _conv4(pv_ref, v, v1, v2, v3):
    return ((((_pvb(pv_ref, PV_LRU_W) * v3 + _pvb(pv_ref, PV_LRU_W + 1) * v2) + _pvb(pv_ref, PV_LRU_W + 2) * v1)
             + _pvb(pv_ref, PV_LRU_W + 3) * v) + _pvb(pv_ref, PV_LRU_B))


def _mixer_forward(proj, pvec, wai, w_out):
    t = proj.shape[1]
    tb = 512
    ng = tb // SUBLANES
    nt = t // tb
    lw = FWD_LW
    ns = D_PART // lw
    per_chunk = CHUNK // lw

    def body(bg_ref, cg_ref, xc_ref, gc_ref, xl_ref, gl_ref, pv_ref, wai_ref, wo_ref,
             yc_ref, yl_ref, h_ref, u_s, r_ref, ig_ref, wo4_ref,
             ucp_s, xlp_s, ls_s, hbuf_s, ub_s, gate_s, wob_s, local_sem, send_sems, recv_sems):
        _gather_w_out(pl.program_id(0) * nt + pl.program_id(1), ns * nt, wo_ref, wob_s, wo4_ref, local_sem, send_sems, recv_sems)

        @pl.when(pl.program_id(1) == 0)
        def _():
            ucp_s[...] = jnp.zeros_like(ucp_s)
            xlp_s[...] = jnp.zeros_like(xlp_s)
            hbuf_s[...] = jnp.zeros_like(hbuf_s)

        row = lax.broadcasted_iota(jnp.int32, (SUBLANES, lw), 0)
        ls_s[...] = RG_LRU_C * _log_sigmoid(_pvb(pv_ref, PV_LAM))

        def conv_group(g, carry):
            ucp, xlp = carry
            sl = pl.ds(pl.multiple_of(g * SUBLANES, SUBLANES), SUBLANES)
            uc = cg_ref[sl, :] * xc_ref[sl, :]
            v = _conv3(pv_ref, uc, _shift_down(uc, ucp, 1, row), _shift_down(uc, ucp, 2, row))
            yc = bg_ref[sl, :] * v
            rr = lax.rsqrt(_head_mean(yc * yc, CONV_HEAD) + RMS_EPS)
            gc = gc_ref[sl, :]
            zc = ((yc * rr) * _pvb(pv_ref, PV_CG)) * (gc * _sigmoid(gc))
            xl = xl_ref[sl, :]
            u = _conv4(pv_ref, xl, _shift_down(xl, xlp, 1, row), _shift_down(xl, xlp, 2, row), _shift_down(xl, xlp, 3, row))
            u_s[sl, :] = u
            return (uc, xl), (zc, u)

        ucp, xlp = _for_groups(ng, conv_group, (ucp_s[...], xlp_s[...]), unroll=2 * UNROLL,
                               stores=(_rows_of(yc_ref), _rows_of(ub_s)))
        ucp_s[...] = ucp
        xlp_s[...] = xlp

        gate_s[...] = _mm(ub_s[...], wai_ref[0])

        def lru_group(g, h_before):
            sl = pl.ds(pl.multiple_of(g * SUBLANES, SUBLANES), SUBLANES)
            u = u_s[sl, :]
            r = _sigmoid(gate_s[sl, 0:lw] + _pvb(pv_ref, PV_BA))
            ig = _sigmoid(gate_s[sl, lw:2 * lw] + _pvb(pv_ref, PV_BI))
            r_ref[sl, :] = r
            ig_ref[sl, :] = ig
            a, _, mult, _ = _decay(r, ls_s[...])
            A, B = _scan8_fwd(a, mult * (ig * u), row)
            h = B + A * jnp.broadcast_to(h_before[SUBLANES - 1:SUBLANES, :], (SUBLANES, lw))
            h_ref[sl, :] = h
            rr = lax.rsqrt(_head_mean(h * h, LRU_HEAD) + RMS_EPS)
            gl = gl_ref[sl, :]
            return h, (((h * rr) * _pvb(pv_ref, PV_LG)) * (gl * _sigmoid(gl)),)

        hbuf_s[...] = _for_groups(ng, lru_group, hbuf_s[...], unroll=2 * UNROLL, stores=(_rows_of(yl_ref),))

    def part(p):
        return pl.BlockSpec((None, tb, lw), lambda c, i: (2 * p + c // per_chunk, i, c % per_chunk))

    strip = pl.BlockSpec((tb, lw), lambda c, i: (i, c))
    small = pltpu.VMEM((SUBLANES, lw), F32)
    return pl.pallas_call(
        body, grid=(ns, nt),
        in_specs=[part(p) for p in range(N_PARTS)] + [
            pl.BlockSpec((PV_ROWS, lw), lambda c, i: (0, c)),
            pl.BlockSpec((1, lw, 2 * lw), lambda c, i: (c, 0, 0)),
            pl.BlockSpec(w_out.shape, lambda c, i: (0, 0))],
        out_specs=(strip,) * 6 + (pl.BlockSpec(memory_space=pl.ANY),),
        out_shape=(jax.ShapeDtypeStruct((t, D_PART), MXU_DTYPE),) * 2 + (jax.ShapeDtypeStruct((t, D_PART), F32),) * 4 + (
            jax.ShapeDtypeStruct((N_CHIPS,) + w_out.shape, MXU_DTYPE),),
        scratch_shapes=[small, small, small, small, pltpu.VMEM((tb, lw), MXU_DTYPE),
                        pltpu.VMEM((tb, 2 * lw), F32), pltpu.VMEM(w_out.shape, MXU_DTYPE),
                        pltpu.SemaphoreType.DMA, pltpu.SemaphoreType.DMA((6,)), pltpu.SemaphoreType.DMA((6,))],
        compiler_params=_cp(ARB, ARB), name="mixer_forward",
    )(proj, proj, proj, proj, proj, proj, pvec, wai, w_out)


def _mixer_backward(proj, h, u, r, ig, dy, pvec, wai, go, gob):
    n_blocks, half, cols = go.shape[0], go.shape[1] // 2, go.shape[2]
    t = proj.shape[1]
    tb = 1024
    ng = tb // SUBLANES
    nt = t // tb
    gpb = tb // SUBLANES

    def body(bg_ref, cg_ref, xc_ref, gc_ref, xl_ref, gl_ref, h_ref, u_ref, r_ref, ig_ref, dyc_ref, dyl_ref,
             cgh_ref, xch_ref, xlh_ref, hh_ref, pv_ref, wai_ref, go_hbm, gob_hbm,
             dp_ref, gw_ref, sv_ref, ro_ref, so_hbm,
             ls_s, ub_s, uce_s, xle_s, he_s, dgb_s, du_s, gbuf_s,
             acc_s, an_s, dvn_s, dun_s, sum_s, arrival_s, sumb_s, send_sems, recv_sems, sibling_sems):
        i = pl.program_id(1)
        first_block = i == nt - 1

        step = pl.program_id(0) * nt + i
        mesh_x, mesh_y, core, _ = _mesh_pos()
        mine = pl.ds(pl.multiple_of(half * core, half), half)
        theirs = pl.ds(pl.multiple_of(half * (1 - core), half), half)
        load = pltpu.make_async_copy(go_hbm.at[:, mine, :], sum_s, sibling_sems.at[0])
        swap = pltpu.make_async_remote_copy(src_ref=gob_hbm.at[:, theirs, :], dst_ref=arrival_s, send_sem=sibling_sems.at[1],
                                            recv_sem=sibling_sems.at[2], device_id=(mesh_x, mesh_y, 1 - core), device_id_type=MESH)
        store = pltpu.make_async_copy(sum_s, so_hbm, sibling_sems.at[3])

        @pl.when(step == 0)
        def _():
            load.start()
            swap.start()

        @pl.when(step == 1)
        def _():
            load.wait()
            swap.wait()
            sum_s[...] = sum_s[...] + arrival_s[...].astype(F32)
            sumb_s[...] = sum_s[...].astype(jnp.bfloat16)
            store.start()
            for cp in _chip_block_copies(sumb_s, ro_ref, 1, send_sems, recv_sems):
                cp.start()

        @pl.when(step == NS * nt - 1)
        def _():
            store.wait()
            for cp in _chip_block_copies(sumb_s, ro_ref, 1, send_sems, recv_sems):
                cp.wait()

        @pl.when(i == 0)
        def _():
            acc_s[...] = jnp.zeros_like(acc_s)
            gw_ref[...] = jnp.zeros_like(gw_ref)
            an_s[...] = jnp.zeros_like(an_s)
            dvn_s[...] = jnp.zeros_like(dvn_s)
            dun_s[...] = jnp.zeros_like(dun_s)
            gbuf_s[...] = jnp.zeros_like(gbuf_s)

        row = lax.broadcasted_iota(jnp.int32, (SUBLANES, LW), 0)
        ls_s[...] = RG_LRU_C * _log_sigmoid(_pvb(pv_ref, PV_LAM))
        keep = jnp.where(first_block, 0.0, 1.0)
        uce_s[0:SUBLANES, :] = (cgh_ref[...] * xch_ref[...]) * keep
        xle_s[0:SUBLANES, :] = xlh_ref[...] * keep
        he_s[0:SUBLANES, :] = hh_ref[...] * keep
        xle_s[SUBLANES:SUBLANES + tb, :] = xl_ref[...]
        he_s[SUBLANES:SUBLANES + tb, :] = h_ref[...]

        uce_s[SUBLANES:SUBLANES + tb, :] = cg_ref[...] * xc_ref[...]

        def acc_add(k, v):
            acc_s[k] += v

        def main_group(gi, carry):
            a_next, dv_next, g_next = carry
            g = ng - 1 - gi
            r0 = pl.multiple_of(g * SUBLANES, SUBLANES)
            sl = pl.ds(r0, SUBLANES)
            sl_e = pl.ds(r0 + SUBLANES, SUBLANES)
            lsb = ls_s[...]
            u = u_ref[sl, :]
            r = r_ref[sl, :]
            ig = ig_ref[sl, :]
            a, e2, mult, inv_mult = _decay(r, lsb)
            gl = gl_ref[sl, :]
            sg = _sigmoid(gl)
            s_l = gl * sg
            h8 = he_s[sl_e, :]
            hprev = _shift_down(h8, he_s[sl, :], 1, row)
            rr = lax.rsqrt(_head_mean(h8 * h8, LRU_HEAD) + RMS_EPS)
            n = h8 * rr
            dz = dyl_ref[sl, :]
            lg = _pvb(pv_ref, PV_LG)
            acc_add(PV_LG, (dz * n) * s_l)
            p5 = ((dz * n) * lg) * (sg + s_l * (1.0 - sg))
            dn = (dz * lg) * s_l
            dh = rr * (dn - n * _head_mean(dn * n, LRU_HEAD))
            A, B = _scan8_rev(_shift_up(a, a_next, 1, row), dh, row)
            gg = B + A * jnp.broadcast_to(g_next[0:1, :], (SUBLANES, LW))
            da = gg * hprev
            iu = ig * u
            diu = gg * mult
            dla = da * a - (gg * iu) * (e2 * inv_mult)
            acc_add(PV_LAM, dla * r)
            dra = (dla * lsb) * (r * (1.0 - r))
            dia = (diu * u) * (ig * (1.0 - ig))
            acc_add(PV_BA, dra)
            acc_add(PV_BI, dia)
            du_s[sl, :] = diu * ig
            bg = bg_ref[sl, :]
            gc = gc_ref[sl, :]
            uc = uce_s[sl_e, :]
            ucp = uce_s[sl, :]
            uc1 = _shift_down(uc, ucp, 1, row)
            uc2 = _shift_down(uc, ucp, 2, row)
            v = _conv3(pv_ref, uc, uc1, uc2)
            yc = bg * v
            rrc = lax.rsqrt(_head_mean(yc * yc, CONV_HEAD) + RMS_EPS)
            nc = yc * rrc
            sgc = _sigmoid(gc)
            s_c = gc * sgc
            dzc = dyc_ref[sl, :]
            cgain = _pvb(pv_ref, PV_CG)
            acc_add(PV_CG, (dzc * nc) * s_c)
            p3 = ((dzc * nc) * cgain) * (sgc + s_c * (1.0 - sgc))
            dnc = (dzc * cgain) * s_c
            dyc = rrc * (dnc - nc * _head_mean(dnc * nc, CONV_HEAD))
            dv = dyc * bg
            duc = (_pvb(pv_ref, PV_CONV_W + 2) * dv + _pvb(pv_ref, PV_CONV_W + 1) * _shift_up(dv, dv_next, 1, row)
                   + _pvb(pv_ref, PV_CONV_W) * _shift_up(dv, dv_next, 2, row))
            acc_add(PV_CONV_W + 2, dv * uc)
            acc_add(PV_CONV_W + 1, dv * uc1)
            acc_add(PV_CONV_W, dv * uc2)
            return (a, dv, gg), (dyc * v, duc * xc_ref[sl, :], duc * cg_ref[sl, :], p3, p5, dra, dia, u)

        a_next, dv_next, g_next = _for_groups(
            ng, main_group, (an_s[...], dvn_s[...], gbuf_s[...]), descending=True,
            stores=(_rows_of(dp_ref, 0), _rows_of(dp_ref, 1), _rows_of(dp_ref, 2), _rows_of(dp_ref, 3), _rows_of(dp_ref, 5),
                    _rows_of(dgb_s, cols=slice(0, LW)), _rows_of(dgb_s, cols=slice(LW, 2 * LW)), _rows_of(ub_s)))
        an_s[...] = a_next
        dvn_s[...] = dv_next
        gbuf_s[...] = g_next

        dgb = dgb_s[...]
        du_s[...] += _mm_nt(dgb, wai_ref[0])
        gw_ref[0] += _mm_tn(ub_s[...], dgb)

        def lru_conv_group(gi, du_next):
            g = ng - 1 - gi
            r0 = pl.multiple_of(g * SUBLANES, SUBLANES)
            sl = pl.ds(r0, SUBLANES)
            du = du_s[sl, :]
            xl = xle_s[pl.ds(r0 + SUBLANES, SUBLANES), :]
            xlp = xle_s[sl, :]
            acc_add(PV_LRU_B, du)
            acc_add(PV_LRU_W + 3, du * xl)
            acc_add(PV_LRU_W + 2, du * _shift_down(xl, xlp, 1, row))
            acc_add(PV_LRU_W + 1, du * _shift_down(xl, xlp, 2, row))
            acc_add(PV_LRU_W, du * _shift_down(xl, xlp, 3, row))
            dxl = (((_pvb(pv_ref, PV_LRU_W + 3) * du + _pvb(pv_ref, PV_LRU_W + 2) * _shift_up(du, du_next, 1, row))
                    + _pvb(pv_ref, PV_LRU_W + 1) * _shift_up(du, du_next, 2, row))
                   + _pvb(pv_ref, PV_LRU_W) * _shift_up(du, du_next, 3, row))
            return du, (dxl,)

        dun_s[...] = _for_groups(ng, lru_conv_group, dun_s[...], descending=True, stores=(_rows_of(dp_ref, 4),))

        @pl.when(first_block)
        def _():
            sv_ref[...] = jnp.zeros_like(sv_ref)
            for k in range(N_ACC):
                tot = jnp.sum(acc_s[k], axis=0, keepdims=True)
                if k == PV_LAM:
                    tot = (RG_LRU_C * tot) / (1.0 + jnp.exp(pv_ref[PV_LAM:PV_LAM + 1, :]))
                sv_ref[k:k + 1, :] = tot

    def part(p):
        return pl.BlockSpec((None, tb, LW), lambda c, i: (2 * p + c // STRIPS_PER_CHUNK, nt - 1 - i, c % STRIPS_PER_CHUNK))

    def halo(p):
        return pl.BlockSpec((None, SUBLANES, LW), lambda c, i: (2 * p + c // STRIPS_PER_CHUNK,
                                                                jnp.maximum((nt - 1 - i) * gpb - 1, 0), c % STRIPS_PER_CHUNK))

    strip = pl.BlockSpec((tb, LW), lambda c, i: (nt - 1 - i, c))
    big = pltpu.VMEM((tb, LW), F32)
    big_e = pltpu.VMEM((tb + SUBLANES, LW), F32)
    small = pltpu.VMEM((SUBLANES, LW), F32)
    outs = pl.pallas_call(
        body, grid=(NS, nt),
        in_specs=[part(p) for p in range(N_PARTS)] + [
            strip, strip, strip, strip, strip, pl.BlockSpec((tb, LW), lambda c, i: (nt - 1 - i, NS + c)),
            halo(1), halo(2), halo(4),
            pl.BlockSpec((SUBLANES, LW), lambda c, i: (jnp.maximum((nt - 1 - i) * gpb - 1, 0), c)),
            pl.BlockSpec((PV_ROWS, LW), lambda c, i: (0, c)),
            pl.BlockSpec((1, LW, 2 * LW), lambda c, i: (c, 0, 0)),
            pl.BlockSpec(memory_space=pl.ANY), pl.BlockSpec(memory_space=pl.ANY)],
        out_specs=(pl.BlockSpec((N_PARTS, tb, LW), lambda c, i: (0, nt - 1 - i, c)),
                   pl.BlockSpec((1, LW, 2 * LW), lambda c, i: (c, 0, 0)),
                   pl.BlockSpec((PV_ROWS, LW), lambda c, i: (0, c)),
                   pl.BlockSpec(memory_space=pl.ANY), pl.BlockSpec(memory_space=pl.ANY)),
        out_shape=(jax.ShapeDtypeStruct((N_PARTS, t, D_PART), MXU_DTYPE),
                   jax.ShapeDtypeStruct((NS, LW, 2 * LW), F32), jax.ShapeDtypeStruct((PV_ROWS, D_PART), F32),
                   jax.ShapeDtypeStruct((3, 1, half, cols), jnp.bfloat16),
                   jax.ShapeDtypeStruct((n_blocks, half, cols), F32)),
        scratch_shapes=[small, pltpu.VMEM((tb, LW), MXU_DTYPE), big_e, big_e, big_e,
                        pltpu.VMEM((tb, 2 * LW), MXU_DTYPE), big, small,
                        pltpu.VMEM((N_ACC, SUBLANES, LW), F32), small, small, small,
                        pltpu.VMEM((n_blocks, half, cols), F32), pltpu.VMEM((n_blocks, half, cols), jnp.bfloat16),
                        pltpu.VMEM((n_blocks, half, cols), jnp.bfloat16),
                        pltpu.SemaphoreType.DMA((3,)), pltpu.SemaphoreType.DMA((3,)), pltpu.SemaphoreType.DMA((4,))],
        compiler_params=_cp(ARB, ARB), name="mixer_backward",
    )(proj, proj, proj, proj, proj, proj, h, u, r, ig, dy, dy, proj, proj, proj, h, pvec, wai, go, gob)
    return outs


def _adamw(w, g, m, v):
    m = ADAM_B1 * m + (1.0 - ADAM_B1) * g
    v = ADAM_B2 * v + (1.0 - ADAM_B2) * (g * g)
    m_hat = m / (1.0 - ADAM_B1 ** ADAM_STEP)
    v_hat = v / (1.0 - ADAM_B2 ** ADAM_STEP)
    delta = -ADAM_LR * (m_hat / (jnp.sqrt(v_hat) + ADAM_EPS) + ADAM_WD * w)
    return delta, m, v


def _adamw_blocks(f_in, f_out, p_in, p_out):
    n_pieces = 4
    slab = 2 * SUBLANES

    def body(fi_hbm, fo_hbm, wi_hbm, mi_hbm, vi_hbm, wo_hbm, mo_hbm, vo_hbm,
             gi_hbm, di_hbm, nmi_hbm, nvi_hbm, do_hbm, nmo_hbm, nvo_hbm,
             g_in, g_out, wi, mi, vi, wo, mo, vo, load_sems, store_sems):
        blocks = ((g_in, fi_hbm, (wi, mi, vi), (wi_hbm, mi_hbm, vi_hbm), (di_hbm, nmi_hbm, nvi_hbm)),
                  (g_out, fo_hbm, (wo, mo, vo), (wo_hbm, mo_hbm, vo_hbm), (do_hbm, nmo_hbm, nvo_hbm)))
        loads, stores = [], []

        def start(src, dst, sems, group):
            group.append(pltpu.make_async_copy(src, dst, sems.at[len(group)]))
            group[-1].start()

        piece = lambda g, i: slice(g.shape[1] // n_pieces * i, g.shape[1] // n_pieces * (i + 1))
        for i in range(n_pieces):
            for g, f_hbm, p, p_hbm, _ in blocks:
                rows = piece(g, i)
                start(f_hbm.at[:, rows, :], g.at[:, rows, :], load_sems, loads)
                for s, s_hbm in zip(p, p_hbm):
                    start(s_hbm.at[rows, :], s.at[rows, :], load_sems, loads)
        per_piece = len(loads) // n_pieces
        for i in range(n_pieces):
            for cp in loads[per_piece * i:per_piece * (i + 1)]:
                cp.wait()
            for g, _, (w_s, m_s, v_s), _, outs in blocks:
                rows = piece(g, i)
                n, cols = g.shape[0], g.shape[2]

                def step(t, carry):
                    rs = pl.ds(pl.multiple_of(rows.start + slab * t, slab), slab)
                    for q in range(n):
                        cs = slice(cols * q, cols * (q + 1))
                        w_s[rs, cs], m_s[rs, cs], v_s[rs, cs] = _adamw(w_s[rs, cs], g[q, rs, :], m_s[rs, cs], v_s[rs, cs])
                    return carry

                lax.fori_loop(0, (rows.stop - rows.start) // slab, step, 0)
                for s, o_hbm in zip((w_s, m_s, v_s), outs):
                    start(s.at[rows, :], o_hbm.at[rows, :], store_sems, stores)
            rows, cols = piece(g_in, i), g_in.shape[2]
            for q in range(g_in.shape[0]):
                start(g_in.at[q, rows, :], gi_hbm.at[rows, cols * q:cols * (q + 1)], store_sems, stores)
        for cp in stores:
            cp.wait()

    w_i, w_o = p_in[0], p_out[0]
    n_in = f_in.shape[0]
    assert w_i.shape == (f_in.shape[1], n_in * f_in.shape[2]) and w_o.shape == (f_out.shape[1], f_out.shape[0] * f_out.shape[2])
    hbm = pl.BlockSpec(memory_space=pl.ANY)
    return pl.pallas_call(
        body, in_specs=[hbm] * 8, out_specs=(hbm,) * 7,
        out_shape=(jax.ShapeDtypeStruct(w_i.shape, F32),) * 4 + (jax.ShapeDtypeStruct(w_o.shape, F32),) * 3,
        scratch_shapes=[pltpu.VMEM(f_in.shape, F32), pltpu.VMEM(f_out.shape, F32)]
        + [pltpu.VMEM(w_i.shape, F32)] * 3 + [pltpu.VMEM(w_o.shape, F32)] * 3
        + [pltpu.SemaphoreType.DMA((n_pieces * 8,)), pltpu.SemaphoreType.DMA((n_pieces * (6 + n_in),))],
        compiler_params=_cp(), name="adamw_blocks",
    )(f_in, f_out, *p_in, *p_out)


def _adam_small(ws, ms, vs, gs):
    n = len(ws)

    def body(*refs):
        w_r, m_r, v_r, g_r = refs[0:n], refs[n:2 * n], refs[2 * n:3 * n], refs[3 * n:4 * n]
        d_o, m_o, v_o = refs[4 * n:5 * n], refs[5 * n:6 * n], refs[6 * n:7 * n]
        for j in range(n):
            d_o[j][...], m_o[j][...], v_o[j][...] = _adamw(w_r[j][...], g_r[j][...], m_r[j][...], v_r[j][...])

    vm = pl.BlockSpec(memory_space=pltpu.VMEM)
    shapes = tuple(jax.ShapeDtypeStruct(w.shape, F32) for w in ws)
    outs = pl.pallas_call(
        body, in_specs=[vm] * (4 * n), out_specs=(vm,) * (3 * n), out_shape=shapes * 3,
        compiler_params=_cp(), name="adam_small",
    )(*ws, *ms, *vs, *gs)
    return outs[0:n], outs[n:2 * n], outs[2 * n:3 * n]


def _block_diag_strips(w, lw):
    heads = lw // LRU_HEAD
    w4 = w.reshape(D_PART // lw, heads, LRU_HEAD, LRU_HEAD)
    rows = [jnp.pad(w4[:, hh], ((0, 0), (0, 0), (LRU_HEAD * hh, lw - LRU_HEAD * (hh + 1)))) for hh in range(heads)]
    return jnp.concatenate(rows, axis=1)


def _gate_matrices(w_a, w_i, lw):
    return jnp.concatenate([_block_diag_strips(w_a, lw), _block_diag_strips(w_i, lw)], axis=2).astype(MXU_DTYPE)


def _strip_diag_blocks(g):
    g5 = g.reshape(NS, HEADS_PER_STRIP, LRU_HEAD, HEADS_PER_STRIP, LRU_HEAD)
    return jnp.stack([g5[:, hh, :, hh, :] for hh in range(HEADS_PER_STRIP)], axis=1).reshape(NS * HEADS_PER_STRIP, LRU_HEAD, LRU_HEAD)


def kernel(x, ln_g, w_in, conv_w, lru_conv_w, lru_conv_b, w_a, b_a, w_i, b_i, lam, conv_out_g, lru_out_g, w_out, final_g, loss_target, m_ln_g, m_w_in, m_conv_w, m_lru_conv_w, m_lru_conv_b, m_w_a, m_b_a, m_w_i, m_b_i, m_lam, m_conv_out_g, m_lru_out_g, m_w_out, m_final_g, v_ln_g, v_w_in, v_conv_w, v_lru_conv_w, v_lru_conv_b, v_w_a, v_b_a, v_w_i, v_b_i, v_lam, v_conv_out_g, v_lru_out_g, v_w_out, v_final_g):
    xi, yi, ci = lax.axis_index("x"), lax.axis_index("y"), lax.axis_index("c")
    k = 2 * xi + yi
    t = x.shape[1]
    x2 = x.reshape(t, D_MODEL)
    tgt2 = loss_target.reshape(t, D_MODEL)
    row = lambda a: a.reshape(1, -1)

    small = jnp.concatenate([conv_w, lru_conv_w, jnp.zeros((1, conv_w.shape[1]), F32)], axis=0)
    proj, xn, w12, sm4 = _gather_in_projection(x2, row(ln_g), w_in, small)
    convs = jnp.transpose(sm4, (1, 0, 2)).reshape(SUBLANES, D_PART)
    pvec = jnp.concatenate(
        [convs[0:7], row(lru_conv_b), row(b_a), row(b_i), row(lam), row(conv_out_g), row(lru_out_g),
         jnp.zeros((PV_ROWS - N_ACC, D_PART), F32)], axis=0)
    wai = _gate_matrices(w_a, w_i, LW)

    c_arr = jnp.reshape(ci, (1,)).astype(jnp.int32)
    yc, yl, h, u, r, ig, wo4 = _mixer_forward(proj, pvec, _gate_matrices(w_a, w_i, FWD_LW), w_out)
    wo = wo4.reshape(2 * D_PART, D_MODEL)
    do, dob, dy, st_out = _out_projection_loss(yc, yl, x2, tgt2, wo, row(final_g))
    go4, go4b = _w_out_grad(yc, yl, dob)
    dproj, g_wai, svec, r2o, s_out = _mixer_backward(proj, h, u, r, ig, dy, pvec, wai, go4, go4b)
    gwa = _strip_diag_blocks(g_wai[:, :, 0:LW]).reshape(LRU_HEAD, D_PART)
    gwi = _strip_diag_blocks(g_wai[:, :, LW:2 * LW]).reshape(LRU_HEAD, D_PART)
    g12, g12b, red = _w_in_grad(xn, dproj, jnp.concatenate([svec, st_out, gwa, gwi], axis=0))
    s_in, sb_in = _add_sibling_halves(g12, g12b, c_arr, "add_sibling_halves_in")
    grad_x, st_in, r2i = _input_grad(dproj, w12, x2, do, row(ln_g), sb_in, s_in)
    f_in, f_out, red_ln = _finish_gradients(s_in, r2i, s_out, r2o, st_in)
    r_out = PV_ROWS
    r_wa = PV_ROWS + SUBLANES
    r_wi = r_wa + LRU_HEAD
    loss = red[r_out + 1, 0]

    g_w_in, d_w_in, nm_w_in, nv_w_in, d_w_out, nm_w_out, nv_w_out = _adamw_blocks(
        f_in, f_out, (w_in, m_w_in, v_w_in), (w_out, m_w_out, v_w_out))
    g_w_out = f_out[0]

    ncol = conv_w.shape[1]
    conv_cols = lax.dynamic_slice(red, (0, k * ncol), (SUBLANES, ncol))
    g_small = {
        "ln_g": red_ln[0], "conv_w": conv_cols[0:3], "lru_conv_w": conv_cols[3:7], "lru_conv_b": red[PV_LRU_B],
        "w_a": red[r_wa:r_wa + LRU_HEAD].reshape(w_a.shape), "b_a": red[PV_BA],
        "w_i": red[r_wi:r_wi + LRU_HEAD].reshape(w_i.shape), "b_i": red[PV_BI], "lam": red[PV_LAM],
        "conv_out_g": red[PV_CG], "lru_out_g": red[PV_LG], "final_g": red[r_out],
    }
    w_small = {"ln_g": ln_g, "conv_w": conv_w, "lru_conv_w": lru_conv_w, "lru_conv_b": lru_conv_b, "w_a": w_a, "b_a": b_a,
               "w_i": w_i, "b_i": b_i, "lam": lam, "conv_out_g": conv_out_g, "lru_out_g": lru_out_g, "final_g": final_g}
    m_small = {"ln_g": m_ln_g, "conv_w": m_conv_w, "lru_conv_w": m_lru_conv_w, "lru_conv_b": m_lru_conv_b, "w_a": m_w_a,
               "b_a": m_b_a, "w_i": m_w_i, "b_i": m_b_i, "lam": m_lam, "conv_out_g": m_conv_out_g,
               "lru_out_g": m_lru_out_g, "final_g": m_final_g}
    v_small = {"ln_g": v_ln_g, "conv_w": v_conv_w, "lru_conv_w": v_lru_conv_w, "lru_conv_b": v_lru_conv_b, "w_a": v_w_a,
               "b_a": v_b_a, "w_i": v_w_i, "b_i": v_b_i, "lam": v_lam, "conv_out_g": v_conv_out_g,
               "lru_out_g": v_lru_out_g, "final_g": v_final_g}
    names = list(w_small)
    as2d = lambda a: a.reshape(1, -1) if a.ndim == 1 else a
    d_s, m_s, v_s = _adam_small([as2d(w_small[n]) for n in names], [as2d(m_small[n]) for n in names],
                                [as2d(v_small[n]) for n in names], [as2d(g_small[n]) for n in names])
    back = lambda n, a: a.reshape(w_small[n].shape)
    grads = {n: g_small[n] for n in names}
    deltas = {n: back(n, a) for n, a in zip(names, d_s)}
    new_m = {n: back(n, a) for n, a in zip(names, m_s)}
    new_v = {n: back(n, a) for n, a in zip(names, v_s)}
    grads["w_in"], deltas["w_in"], new_m["w_in"], new_v["w_in"] = g_w_in, d_w_in, nm_w_in, nv_w_in
    grads["w_out"], deltas["w_out"], new_m["w_out"], new_v["w_out"] = g_w_out, d_w_out, nm_w_out, nv_w_out

    order = ["ln_g", "w_in", "conv_w", "lru_conv_w", "lru_conv_b", "w_a", "b_a", "w_i", "b_i", "lam", "conv_out_g",
             "lru_out_g", "w_out", "final_g"]
    return (loss, grad_x.reshape(x.shape), *[grads[n] for n in order], *[deltas[n] for n in order],
            *[new_m[n] for n in order], *[new_v[n] for n in order])
```

```python
import functools

import jax
import jax.numpy as jnp
from jax import lax
from jax.experimental import pallas as pl
from jax.experimental.pallas import tpu as pltpu

F32 = jnp.float32
MXU_DTYPE = jnp.bfloat16

D_MODEL = 1024
D_PART = 1024
N_PARTS = 6
CHUNK = 512
CHUNKS_PER_BLOCK = 3
N_CHUNKS = 12
N_CHIPS = 4
SUBLANES = 8
LANES = 128
LW = 256
FWD_LW = 512
UNROLL = 8
NS = D_PART // LW
STRIPS_PER_CHUNK = CHUNK // LW
CONV_HEAD = 128
LRU_HEAD = 64
HEADS_PER_STRIP = LW // LRU_HEAD
RMS_EPS = 1e-6
RG_LRU_C = 8.0
ADAM_LR = 0.001
ADAM_B1 = 0.9
ADAM_B2 = 0.999
ADAM_EPS = 1e-08
ADAM_WD = 0.01
ADAM_STEP = 10

PV_CONV_W = 0
PV_LRU_W = 3
PV_LRU_B = 7
PV_BA = 8
PV_BI = 9
PV_LAM = 10
PV_CG = 11
PV_LG = 12
PV_ROWS = 16
N_ACC = 13

SLAB = 128
MESH = pl.DeviceIdType.MESH
VMEM_LIMIT = 56 * 1024 * 1024
ARB = "arbitrary"


def _cp(*sem, **kw):
    return pltpu.CompilerParams(dimension_semantics=sem or None, vmem_limit_bytes=VMEM_LIMIT, **kw)


def _mm(a, b):
    return jnp.dot(a, b, preferred_element_type=F32)


def _mm_nt(a, b):
    return lax.dot_general(a, b, (((1,), (1,)), ((), ())), preferred_element_type=F32)


def _mm_tn(a, b):
    return lax.dot_general(a, b, (((0,), (0,)), ((), ())), preferred_element_type=F32)


def _sigmoid(x):
    return 0.5 * jnp.tanh(0.5 * x) + 0.5


def _log_sigmoid(x):
    z = jnp.exp(-jnp.abs(x))
    u = 1.0 + z
    log1p = jnp.where(u == 1.0, z, jnp.log(u) * z / (u - 1.0))
    return jnp.minimum(x, 0.0) - log1p


def _head_mean(z, head):
    out = []
    for k in range(z.shape[1] // LANES):
        zk = z[:, LANES * k:LANES * (k + 1)]
        if head == LANES:
            m = jnp.sum(zk, axis=-1, keepdims=True) * (1.0 / head)
            out.append(jnp.broadcast_to(m, zk.shape))
        else:
            lo = lax.broadcasted_iota(jnp.int32, zk.shape, 1) < head
            s_lo = jnp.sum(jnp.where(lo, zk, 0.0), axis=-1, keepdims=True)
            s_hi = jnp.sum(jnp.where(lo, 0.0, zk), axis=-1, keepdims=True)
            out.append(jnp.where(lo, s_lo, s_hi) * (1.0 / head))
    return jnp.concatenate(out, axis=1)


def _shift_down(cur, prev, d, row):
    return pltpu.roll(jnp.where(row < SUBLANES - d, cur, prev), d, 0)


def _shift_up(cur, nxt, d, row):
    return pltpu.roll(jnp.where(row >= d, cur, nxt), SUBLANES - d, 0)


def _scan8_fwd(a, b, row):
    A, B = a, b
    for d in (1, 2, 4):
        m = row >= d
        a_s = jnp.where(m, pltpu.roll(A, d, 0), 1.0)
        b_s = jnp.where(m, pltpu.roll(B, d, 0), 0.0)
        B = A * b_s + B
        A = A * a_s
    return A, B


def _scan8_rev(a, b, row):
    A, B = a, b
    for d in (1, 2, 4):
        m = row < SUBLANES - d
        a_s = jnp.where(m, pltpu.roll(A, SUBLANES - d, 0), 1.0)
        b_s = jnp.where(m, pltpu.roll(B, SUBLANES - d, 0), 0.0)
        B = A * b_s + B
        A = A * a_s
    return A, B


def _decay(r, ls8):
    la = r * ls8
    a = jnp.exp(la)
    e2 = a * a
    em = -jnp.tanh(la) * (1.0 + e2)
    inv_mult = lax.rsqrt(em)
    return a, e2, em * inv_mult, inv_mult


def _mesh_pos():
    x, y, c = lax.axis_index("x"), lax.axis_index("y"), lax.axis_index("c")
    chips = [(1 - x, y), (x, 1 - y), (1 - x, 1 - y)]
    return x, y, c, chips


def _gather_in_projection(x, ln_g, w_in, small):
    t = x.shape[0]
    rb_x = 512
    rb_mm = 2048
    n_mm = t // rb_mm
    half = w_in.shape[0] // 2

    def body(x_hbm, g_ref, wi_ref, sm_ref, proj_hbm, xn_ref, w12_ref, sm4_ref,
             xbuf, obuf, x_sems, o_sems, send_sems, recv_sems):
        x_, y_, c, chips = _mesh_pos()
        k = 2 * x_ + y_
        sib = (x_, y_, 1 - c)
        sm4_ref[k] = sm_ref[...]

        def remote(ref, sem, to):
            return pltpu.make_async_remote_copy(src_ref=ref, dst_ref=ref, send_sem=send_sems.at[sem],
                                                recv_sem=recv_sems.at[sem], device_id=to, device_id_type=MESH)

        def chunk_of(chip, s):
            return CHUNKS_PER_BLOCK * (2 * chip[0] + chip[1]) + s

        def piece(q, core, first=0, rows=half):
            return w12_ref.at[q, pl.ds(pl.multiple_of(half * core + first, SUBLANES * 2), rows), :]

        nbr_x, nbr_y, diagonal = chips
        quarter = half // 2
        DIAG = [(0, 0, half, 0), (1, 0, quarter, 0), (1, quarter, quarter, 1), (2, 0, half, 1)]
        ici = lambda m, s: 2 * s + m
        dgn = lambda j: 6 + j
        to_sib = 10
        sml = lambda m: 20 + m

        sends = []
        for s in range(CHUNKS_PER_BLOCK):
            w12_ref[chunk_of((x_, y_), s)] = wi_ref[:, CHUNK * s:CHUNK * (s + 1)].astype(MXU_DTYPE)
            for m, chip in enumerate((nbr_x, nbr_y)):
                sends.append(remote(piece(chunk_of((x_, y_), s), c), ici(m, s), (*chip, c)))
                sends[-1].start()
        for m, chip in enumerate(chips):
            sends.append(remote(sm4_ref.at[k], sml(m), (*chip, c)))
            sends[-1].start()

        def x_copy(rb, slot):
            return pltpu.make_async_copy(x_hbm.at[pl.ds(rb * rb_x, rb_x), :], xbuf.at[slot], x_sems.at[slot])

        x_copy(0, 0).start()
        for rb in range(t // rb_x):
            slot = rb % 2
            x_copy(rb, slot).wait()
            if rb + 1 < t // rb_x:
                x_copy(rb + 1, 1 - slot).start()

            def norm_slab(sl, carry, rb=rb, slot=slot):
                xf = xbuf[slot, pl.ds(pl.multiple_of(sl * SLAB, SLAB), SLAB), :]
                r = lax.rsqrt(jnp.mean(xf * xf, axis=-1, keepdims=True) + RMS_EPS)
                xn_ref[pl.ds(pl.multiple_of(rb * rb_x + sl * SLAB, SLAB), SLAB), :] = ((xf * r) * g_ref[...]).astype(MXU_DTYPE)
                return carry

            lax.fori_loop(0, rb_x // SLAB, norm_slab, 0)

        def out_copy(q, i):
            return pltpu.make_async_copy(obuf.at[i], proj_hbm.at[q, pl.ds(pl.multiple_of(i * rb_mm, rb_mm), rb_mm), :],
                                         o_sems.at[i])

        def project(q, very_first):
            def row_block(i, carry):
                if not very_first:
                    out_copy(q, i).wait()
                obuf[i] = _mm(xn_ref[pl.ds(pl.multiple_of(i * rb_mm, rb_mm), rb_mm), :], w12_ref[q])
                out_copy(q, i).start()
                return carry

            lax.fori_loop(0, n_mm, row_block, 0)

        for s in range(CHUNKS_PER_BLOCK):
            project(chunk_of((x_, y_), s), very_first=(s == 0))

        steps = []
        for s in range(CHUNKS_PER_BLOCK):
            for m, chip in enumerate((nbr_x, nbr_y)):
                onward = [(first, rows, dgn(j), chips[via]) for j, (cs, first, rows, via) in enumerate(DIAG)
                          if cs == s and via == 1 - m]
                steps.append((chunk_of(chip, s), [(0, half, ici(m, s))], onward))
        for s in range(CHUNKS_PER_BLOCK):
            steps.append((chunk_of(diagonal, s), [(first, rows, dgn(j)) for j, (cs, first, rows, _) in enumerate(DIAG) if cs == s], []))

        def project_when_whole(step):
            q, pieces, _ = step
            for first, rows, sem in pieces:
                remote(piece(q, 1 - c, first, rows), to_sib + sem, sib).wait_recv()
            project(q, very_first=False)

        passed = []
        for j, (q, pieces, onward) in enumerate(steps):
            for first, rows, sem in pieces:
                remote(piece(q, c, first, rows), sem, sib).wait_recv()
            for first, rows, sem, chip in onward:
                passed.append(remote(piece(q, c, first, rows), sem, (*chip, c)))
                passed[-1].start()
            for first, rows, sem in pieces:
                passed.append(remote(piece(q, c, first, rows), to_sib + sem, sib))
                passed[-1].start()
            if j > 0:
                project_when_whole(steps[j - 1])
        project_when_whole(steps[-1])

        for m, chip in enumerate(chips):
            remote(sm4_ref.at[2 * chip[0] + chip[1]], sml(m), sib).wait_recv()
        for cp in sends + passed:
            cp.wait_send()
        for i in range(n_mm):
            out_copy(0, i).wait()

    vm = pl.BlockSpec(memory_space=pltpu.VMEM)
    hbm = pl.BlockSpec(memory_space=pl.ANY)
    n_sems = 23
    return pl.pallas_call(
        body,
        out_shape=(jax.ShapeDtypeStruct((N_CHUNKS, t, CHUNK), F32), jax.ShapeDtypeStruct((t, D_MODEL), MXU_DTYPE),
                   jax.ShapeDtypeStruct((N_CHUNKS, w_in.shape[0], CHUNK), MXU_DTYPE),
                   jax.ShapeDtypeStruct((N_CHIPS,) + small.shape, F32)),
        in_specs=[hbm, vm, vm, vm], out_specs=(hbm, vm, vm, vm),
        scratch_shapes=[pltpu.VMEM((2, rb_x, D_MODEL), F32), pltpu.VMEM((n_mm, rb_mm, CHUNK), F32),
                        pltpu.SemaphoreType.DMA((2,)), pltpu.SemaphoreType.DMA((n_mm,)),
                        pltpu.SemaphoreType.DMA((n_sems,)), pltpu.SemaphoreType.DMA((n_sems,))],
        compiler_params=_cp(), name="gather_in_projection",
    )(x, ln_g, w_in, small)


def _allreduce_behind(step, when, in_ref, acc_s, rbufs, out_ref, send_sems, recv_sems):
    x, y, c, _ = _mesh_pos()
    peers = [(x, y, 1 - c), (1 - x, y, c), (x, 1 - y, c)]

    def exchange(ph):
        return pltpu.make_async_remote_copy(src_ref=acc_s, dst_ref=rbufs[ph], send_sem=send_sems.at[ph],
                                            recv_sem=recv_sems.at[ph], device_id=peers[ph], device_id_type=MESH)

    @pl.when(step == when[0])
    def _():
        acc_s[...] = in_ref[...]
        exchange(0).start()

    for ph in (1, 2):
        @pl.when(step == when[ph])
        def _(ph=ph):
            exchange(ph - 1).wait()
            acc_s[...] = acc_s[...] + rbufs[ph - 1][...]
            exchange(ph).start()

    @pl.when(step == when[3])
    def _():
        exchange(2).wait()
        out_ref[...] = acc_s[...] + rbufs[2][...]


def _add_sibling_halves(g, gb, c_arr, name):
    n, rows, cols = g.shape
    half = rows // 2
    per = 2
    steps = n // per

    def body(c_ref, g_ref, gb_hbm, o_ref, ob_ref, rbuf, send_sems, recv_sems):
        q = pl.program_id(0)
        x, y, c, _ = _mesh_pos()
        theirs = pl.ds(pl.multiple_of(half * (1 - c), half), half)

        def copy(j):
            blocks = pl.ds(j * per, per)
            return pltpu.make_async_remote_copy(src_ref=gb_hbm.at[blocks, theirs, :], dst_ref=rbuf.at[blocks], send_sem=send_sems.at[j],
                                                recv_sem=recv_sems.at[j], device_id=(x, y, 1 - c), device_id_type=MESH)

        @pl.when(q == 0)
        def _():
            for j in range(steps):
                copy(j).start()

        copy(q).wait_recv()
        s = g_ref[...] + rbuf[pl.ds(q * per, per)].astype(F32)
        o_ref[...] = s
        ob_ref[...] = s.astype(jnp.bfloat16)

        @pl.when(q == steps - 1)
        def _():
            for j in range(steps):
                copy(j).wait_send()

    blk = pl.BlockSpec((per, half, cols), lambda q, c_ref: (q, 0, 0))
    return pl.pallas_call(
        body, out_shape=(jax.ShapeDtypeStruct((n, half, cols), F32), jax.ShapeDtypeStruct((n, half, cols), jnp.bfloat16)),
        grid_spec=pltpu.PrefetchScalarGridSpec(
            num_scalar_prefetch=1, grid=(steps,),
            in_specs=[pl.BlockSpec((per, half, cols), lambda q, c_ref: (q, c_ref[0], 0)), pl.BlockSpec(memory_space=pl.ANY)],
            out_specs=(blk, blk),
            scratch_shapes=[pltpu.VMEM((n, half, cols), jnp.bfloat16), pltpu.SemaphoreType.DMA((steps,)),
                            pltpu.SemaphoreType.DMA((steps,))]),
        compiler_params=_cp(ARB), name=name,
    )(c_arr, g, gb)


def _chip_block_copies(s_ref, r_ref, n_sub, send_sems, recv_sems):
    x, y, c, chips = _mesh_pos()
    cps = []
    for m, chip in enumerate(chips):
        kk = 2 * chip[0] + chip[1]
        cps.append(pltpu.make_async_remote_copy(
            src_ref=s_ref.at[pl.ds(n_sub * kk, n_sub)], dst_ref=r_ref.at[m],
            send_sem=send_sems.at[m], recv_sem=recv_sems.at[m], device_id=(*chip, c), device_id_type=MESH))
    return cps


def _gather_w_out(step, n_steps, wo_ref, wob_s, wo4_ref, local_sem, send_sems, recv_sems):
    x, y, c, chips = _mesh_pos()
    sib = (x, y, 1 - c)
    half = wo_ref.shape[0] // 2

    def rows(core):
        return pl.ds(pl.multiple_of(half * core, half), half)

    def block_half(chip, core):
        return wo4_ref.at[2 * chip[0] + chip[1], rows(core), :]

    def remote(src, dst, sem, to):
        return pltpu.make_async_remote_copy(src_ref=src, dst_ref=dst, send_sem=send_sems.at[sem], recv_sem=recv_sems.at[sem],
                                            device_id=to, device_id_type=MESH)

    local = pltpu.make_async_copy(wob_s, wo4_ref.at[2 * x + y], local_sem)
    ici = [remote(wob_s.at[rows(c), :], block_half((x, y), c), m, (*chip, c)) for m, chip in enumerate(chips)]
    fwd = [remote(block_half(chip, c), block_half(chip, c), 3 + m, sib) for m, chip in enumerate(chips)]

    @pl.when(step == 0)
    def _():
        wob_s[...] = wo_ref[...].astype(MXU_DTYPE)
        local.start()
        for cp in ici:
            cp.start()

    @pl.when(step == n_steps // 2)
    def _():
        for m, chip in enumerate(chips):
            remote(block_half(chip, c), block_half(chip, c), m, sib).wait_recv()
            fwd[m].start()

    @pl.when(step == n_steps - 1)
    def _():
        for m, chip in enumerate(chips):
            remote(block_half(chip, 1 - c), block_half(chip, 1 - c), 3 + m, sib).wait_recv()
        for cp in ici + fwd:
            cp.wait_send()
        local.wait()


def _chip_blocks_shape(s, n_sub):
    return jax.ShapeDtypeStruct((3, n_sub) + s.shape[1:], s.dtype)


def _finish_gradients(s_in, r_in, s_out, r_out, v):
    n_dev = 8
    n_in, n_out = r_in.shape[1], r_out.shape[1]

    def body(si_hbm, ri_hbm, so_hbm, ro_hbm, v_ref, fi_hbm, fo_hbm, tot_ref,
             a_in, b_in, a_out, b_out, slots, load_sems, store_sems, send_sems, recv_sems):
        x, y, c, _ = _mesh_pos()
        k = 2 * x + y
        sib = (x, y, 1 - c)
        me = 4 * x + 2 * y + c
        loads = [pltpu.make_async_copy(si_hbm.at[pl.ds(n_in * k, n_in)], a_in, load_sems.at[0]),
                 pltpu.make_async_copy(ri_hbm, b_in, load_sems.at[1]),
                 pltpu.make_async_copy(so_hbm.at[pl.ds(n_out * k, n_out)], a_out, load_sems.at[2]),
                 pltpu.make_async_copy(ro_hbm, b_out, load_sems.at[3])]
        for cp in loads:
            cp.start()
        slots[me] = v_ref[...]

        def remote(src, dst, sem, to):
            return pltpu.make_async_remote_copy(src_ref=src, dst_ref=dst, send_sem=send_sems.at[sem],
                                                recv_sem=recv_sems.at[sem], device_id=to, device_id_type=MESH)

        small = []
        for d in range(1, n_dev):
            peer = (1 - x if d & 4 else x, 1 - y if d & 2 else y, 1 - c if d & 1 else c)
            small.append(remote(slots.at[me], slots.at[me], d - 1, peer))
            small[-1].start()
        for cp in loads:
            cp.wait()
        big = []
        for j, (a, b, f_hbm) in enumerate(((a_in, b_in, fi_hbm), (a_out, b_out, fo_hbm))):
            a[...] = ((a[...] + b[0].astype(F32)) + b[1].astype(F32)) + b[2].astype(F32)
            half = a.shape[1]
            mine = f_hbm.at[:, pl.ds(pl.multiple_of(half * c, half), half), :]
            big.append(pltpu.make_async_copy(a, mine, store_sems.at[j]))
            big.append(remote(a, mine, n_dev - 1 + j, sib))
        for cp in big:
            cp.start()
        for cp in small + big:
            cp.wait()
        total = slots[0]
        for dev in range(1, n_dev):
            total = total + slots[dev]
        tot_ref[...] = total

    hbm = pl.BlockSpec(memory_space=pl.ANY)
    vm = pl.BlockSpec(memory_space=pltpu.VMEM)
    full = lambda s, n: (n, 2 * s.shape[1], s.shape[2])
    return pl.pallas_call(
        body,
        out_shape=(jax.ShapeDtypeStruct(full(s_in, n_in), F32), jax.ShapeDtypeStruct(full(s_out, n_out), F32),
                   jax.ShapeDtypeStruct(v.shape, F32)),
        in_specs=[hbm, hbm, hbm, hbm, vm], out_specs=(hbm, hbm, vm),
        scratch_shapes=[pltpu.VMEM((n_in,) + s_in.shape[1:], F32), pltpu.VMEM(r_in.shape, r_in.dtype),
                        pltpu.VMEM((n_out,) + s_out.shape[1:], F32), pltpu.VMEM(r_out.shape, r_out.dtype),
                        pltpu.VMEM((n_dev,) + v.shape, F32), pltpu.SemaphoreType.DMA((4,)), pltpu.SemaphoreType.DMA((2,)),
                        pltpu.SemaphoreType.DMA((n_dev + 1,)), pltpu.SemaphoreType.DMA((n_dev + 1,))],
        compiler_params=_cp(), name="finish_gradients",
    )(s_in, r_in, s_out, r_out, v)


def _out_projection_loss(yc, yl, x, target, wo, final_g):
    t = x.shape[0]
    tm = 512

    def body(yc_ref, yl_ref, x_ref, t_ref, wo_ref, fg_ref, do_ref, dob_ref, dy_ref, st_ref, y_wo):
        @pl.when(pl.program_id(0) == 0)
        def _():
            st_ref[...] = jnp.zeros_like(st_ref)

        y_wo[...] = _mm(yc_ref[...], wo_ref[0:D_PART, :]) + _mm(yl_ref[...], wo_ref[D_PART:2 * D_PART, :])

        def norm_loss_slab(s, carry):
            g_sum, loss_sum = carry
            rows = pl.ds(pl.multiple_of(s * SLAB, SLAB), SLAB)
            o = x_ref[rows, :] + y_wo[rows, :]
            r2 = lax.rsqrt(jnp.mean(o * o, axis=-1, keepdims=True) + RMS_EPS)
            ohat = o * r2
            fg = fg_ref[...]
            diff = ohat * fg - t_ref[rows, :]
            dout = diff * (1.0 / D_MODEL)
            gp = dout * fg
            do = r2 * (gp - ohat * jnp.mean(gp * ohat, axis=-1, keepdims=True))
            do_ref[rows, :] = do
            dob_ref[rows, :] = do.astype(MXU_DTYPE)
            loss = 0.5 * jnp.sum(jnp.sum(diff * diff, axis=-1, keepdims=True) * (1.0 / D_MODEL), axis=0, keepdims=True)
            return g_sum + jnp.sum(dout * ohat, axis=0, keepdims=True), loss_sum + loss

        g_sum, loss_sum = lax.fori_loop(0, tm // SLAB, norm_loss_slab,
                                        (jnp.zeros((1, D_MODEL), F32), jnp.zeros((1, 1), F32)))
        st_ref[0:1, :] += g_sum
        st_ref[1:2, :] += jnp.broadcast_to(loss_sum, (1, D_MODEL))
        dy_ref[...] = _mm_nt(dob_ref[...], wo_ref[...])

    row = lambda i: (i, 0)
    fix = lambda i: (0, 0)
    return pl.pallas_call(
        body, grid=(t // tm,),
        in_specs=[pl.BlockSpec((tm, D_PART), row), pl.BlockSpec((tm, D_PART), row),
                  pl.BlockSpec((tm, D_MODEL), row), pl.BlockSpec((tm, D_MODEL), row),
                  pl.BlockSpec((2 * D_PART, D_MODEL), fix), pl.BlockSpec((1, D_MODEL), fix)],
        out_specs=(pl.BlockSpec((tm, D_MODEL), row), pl.BlockSpec((tm, D_MODEL), row),
                   pl.BlockSpec((tm, 2 * D_PART), row), pl.BlockSpec((SUBLANES, D_MODEL), fix)),
        out_shape=(jax.ShapeDtypeStruct((t, D_MODEL), F32), jax.ShapeDtypeStruct((t, D_MODEL), MXU_DTYPE),
                   jax.ShapeDtypeStruct((t, 2 * D_PART), F32), jax.ShapeDtypeStruct((SUBLANES, D_MODEL), F32)),
        scratch_shapes=[pltpu.VMEM((tm, D_MODEL), F32)],
        compiler_params=_cp(ARB), name="out_projection_loss",
    )(yc, yl, x, target, wo, final_g)


def _input_grad(dproj, w12, x, do, ln_g, sb_in):
    t = x.shape[0]
    tm = 1024

    def body(dp_ref, w_ref, x_ref, do_ref, g_ref, s_ref, gx_ref, st_ref, r_ref, acc, send_sems, recv_sems):
        i, p = pl.program_id(0), pl.program_id(1)

        @pl.when((i == 0) & (p == 0))
        def _():
            st_ref[...] = jnp.zeros_like(st_ref)
            for cp in _chip_block_copies(s_ref, r_ref, CHUNKS_PER_BLOCK, send_sems, recv_sems):
                cp.start()

        @pl.when((i == t // tm - 1) & (p == N_PARTS - 1))
        def _():
            for cp in _chip_block_copies(s_ref, r_ref, CHUNKS_PER_BLOCK, send_sems, recv_sems):
                cp.wait()

        @pl.when(p == 0)
        def _():
            acc[...] = jnp.zeros_like(acc)

        acc[...] += _mm_nt(dp_ref[0], jnp.concatenate([w_ref[0], w_ref[1]], axis=1))

        @pl.when(p == N_PARTS - 1)
        def _():
            def norm_bwd_slab(s, g_sum):
                rows = pl.ds(pl.multiple_of(s * SLAB, SLAB), SLAB)
                xf = x_ref[rows, :]
                r = lax.rsqrt(jnp.mean(xf * xf, axis=-1, keepdims=True) + RMS_EPS)
                xhat = xf * r
                dxn = acc[rows, :]
                dxh = dxn * g_ref[...]
                gx_ref[rows, :] = do_ref[rows, :] + r * (dxh - xhat * jnp.mean(dxh * xhat, axis=-1, keepdims=True))
                return g_sum + jnp.sum(dxn * xhat, axis=0, keepdims=True)

            st_ref[0:1, :] += lax.fori_loop(0, tm // SLAB, norm_bwd_slab, jnp.zeros((1, D_MODEL), F32))

    row = lambda i, p: (i, 0)
    fix = lambda i, p: (0, 0)
    return pl.pallas_call(
        body, grid=(t // tm, N_PARTS),
        in_specs=[
            pl.BlockSpec((1, tm, D_PART), lambda i, p: (p, i, 0)),
            pl.BlockSpec((2, D_MODEL, CHUNK), lambda i, p: (p, 0, 0)),
            pl.BlockSpec((tm, D_MODEL), row), pl.BlockSpec((tm, D_MODEL), row), pl.BlockSpec((1, D_MODEL), fix),
            pl.BlockSpec(memory_space=pl.ANY)],
        out_specs=(pl.BlockSpec((tm, D_MODEL), row), pl.BlockSpec((SUBLANES, D_MODEL), fix),
                   pl.BlockSpec(memory_space=pl.ANY)),
        out_shape=(jax.ShapeDtypeStruct((t, D_MODEL), F32), jax.ShapeDtypeStruct((SUBLANES, D_MODEL), F32),
                   _chip_blocks_shape(sb_in, CHUNKS_PER_BLOCK)),
        scratch_shapes=[pltpu.VMEM((tm, D_MODEL), F32), pltpu.SemaphoreType.DMA((3,)), pltpu.SemaphoreType.DMA((3,))],
        compiler_params=_cp(ARB, ARB), name="input_grad",
    )(dproj, w12, x, do, ln_g, sb_in)


def _w_in_grad(xn, dproj, small):
    t = xn.shape[0]
    small_shape = pltpu.VMEM(small.shape, F32)

    def body(xn_ref, dp_ref, sm_ref, o_ref, ob_ref, red_ref, acc_s, r0, r1, r2, send_sems, recv_sems):
        _allreduce_behind(pl.program_id(0), (0, 1, 3, N_PARTS - 1), sm_ref, acc_s, (r0, r1, r2), red_ref, send_sems, recv_sems)
        g = _mm_tn(xn_ref[...], dp_ref[0])
        for s in range(2):
            o_ref[s] = g[:, CHUNK * s:CHUNK * (s + 1)]
            ob_ref[s] = g[:, CHUNK * s:CHUNK * (s + 1)].astype(jnp.bfloat16)

    whole = pl.BlockSpec(small.shape, lambda p: (0, 0))
    pair = pl.BlockSpec((2, D_MODEL, CHUNK), lambda p: (p, 0, 0))
    return pl.pallas_call(
        body, grid=(N_PARTS,),
        in_specs=[pl.BlockSpec((t, D_MODEL), lambda p: (0, 0)),
                  pl.BlockSpec((1, t, D_PART), lambda p: (p, 0, 0)), whole],
        out_specs=(pair, pair, whole),
        out_shape=(jax.ShapeDtypeStruct((N_CHUNKS, D_MODEL, CHUNK), F32),
                   jax.ShapeDtypeStruct((N_CHUNKS, D_MODEL, CHUNK), jnp.bfloat16), jax.ShapeDtypeStruct(small.shape, F32)),
        scratch_shapes=[small_shape] * 4 + [pltpu.SemaphoreType.DMA((3,)), pltpu.SemaphoreType.DMA((3,))],
        compiler_params=_cp(ARB), name="w_in_grad",
    )(xn, dproj, small)


def _w_out_grad(yc, yl, dob):
    t = yc.shape[0]
    tk = 2048

    def body(yc_ref, yl_ref, do_ref, o_ref, ob_ref):
        j, kk = pl.program_id(0), pl.program_id(1)

        def accumulate(y_ref):
            @pl.when(kk == 0)
            def _():
                o_ref[...] = jnp.zeros_like(o_ref)

            o_ref[...] += _mm_tn(y_ref[...], do_ref[...])

            @pl.when(kk == t // tk - 1)
            def _():
                ob_ref[...] = o_ref[...].astype(jnp.bfloat16)

        pl.when(j == 0)(functools.partial(accumulate, yc_ref))
        pl.when(j == 1)(functools.partial(accumulate, yl_ref))

    def rows_of(half):
        return lambda j, kk: (jnp.where(j == half, kk, 0), 0)

    half = pl.BlockSpec((D_PART, D_MODEL), lambda j, kk: (j, 0))
    out, out_b = pl.pallas_call(
        body, grid=(2, t // tk),
        in_specs=[pl.BlockSpec((tk, D_PART), rows_of(0)), pl.BlockSpec((tk, D_PART), rows_of(1)),
                  pl.BlockSpec((tk, D_MODEL), lambda j, kk: (kk, 0))],
        out_specs=(half, half),
        out_shape=(jax.ShapeDtypeStruct((2 * D_PART, D_MODEL), F32), jax.ShapeDtypeStruct((2 * D_PART, D_MODEL), jnp.bfloat16)),
        compiler_params=_cp(ARB, ARB), name="w_out_grad",
    )(yc, yl, dob)
    blocks = (N_CHIPS, 2 * D_PART // N_CHIPS, D_MODEL)
    return out.reshape(blocks), out_b.reshape(blocks)


def _for_groups(n, fn, init, unroll=UNROLL, stores=(), descending=False):
    assert unroll % 2 == 0 and n % unroll == 0

    def trip(j, carry):
        held = None
        for uu in range(unroll):
            idx = j * unroll + uu
            carry, values = fn(idx, carry)
            if uu % 2 == 0:
                held = values
                continue
            low_group = n - 1 - idx if descending else idx - 1
            rows = pl.ds(pl.multiple_of(low_group * SUBLANES, 2 * SUBLANES), 2 * SUBLANES)
            pairs = zip(values, held) if descending else zip(held, values)
            for store, (lo, hi) in zip(stores, pairs, strict=True):
                store(rows, jnp.concatenate([lo, hi], axis=0).astype(MXU_DTYPE))
        return carry

    return lax.fori_loop(0, n // unroll, trip, init)


def _rows_of(ref, *lead, cols=slice(None)):
    def store(rows, value):
        ref[(*lead, rows, cols)] = value

    return store


def _pvb(pv_ref, r):
    return jnp.broadcast_to(pv_ref[r:r + 1, :], (SUBLANES, pv_ref.shape[1]))


def _conv3(pv_ref, u, u1, u2):
    return (_pvb(pv_ref, PV_CONV_W) * u2 + _pvb(pv_ref, PV_CONV_W + 1) * u1) + _pvb(pv_ref, PV_CONV_W + 2) * u


def _conv4(pv_ref, v, v1, v2, v3):
    return ((((_pvb(pv_ref, PV_LRU_W) * v3 + _pvb(pv_ref, PV_LRU_W + 1) * v2) + _pvb(pv_ref, PV_LRU_W + 2) * v1)
             + _pvb(pv_ref, PV_LRU_W + 3) * v) + _pvb(pv_ref, PV_LRU_B))


def _mixer_forward(proj, pvec, wai, w_out):
    t = proj.shape[1]
    tb = 512
    ng = tb // SUBLANES
    nt = t // tb
    lw = FWD_LW
    ns = D_PART // lw
    per_chunk = CHUNK // lw

    def body(bg_ref, cg_ref, xc_ref, gc_ref, xl_ref, gl_ref, pv_ref, wai_ref, wo_ref,
             yc_ref, yl_ref, h_ref, u_s, r_ref, ig_ref, wo4_ref,
             ucp_s, xlp_s, ls_s, hbuf_s, ub_s, gate_s, wob_s, local_sem, send_sems, recv_sems):
        _gather_w_out(pl.program_id(0) * nt + pl.program_id(1), ns * nt, wo_ref, wob_s, wo4_ref, local_sem, send_sems, recv_sems)

        @pl.when(pl.program_id(1) == 0)
        def _():
            ucp_s[...] = jnp.zeros_like(ucp_s)
            xlp_s[...] = jnp.zeros_like(xlp_s)
            hbuf_s[...] = jnp.zeros_like(hbuf_s)

        row = lax.broadcasted_iota(jnp.int32, (SUBLANES, lw), 0)
        ls_s[...] = RG_LRU_C * _log_sigmoid(_pvb(pv_ref, PV_LAM))

        def conv_group(g, carry):
            ucp, xlp = carry
            sl = pl.ds(pl.multiple_of(g * SUBLANES, SUBLANES), SUBLANES)
            uc = cg_ref[sl, :] * xc_ref[sl, :]
            v = _conv3(pv_ref, uc, _shift_down(uc, ucp, 1, row), _shift_down(uc, ucp, 2, row))
            yc = bg_ref[sl, :] * v
            rr = lax.rsqrt(_head_mean(yc * yc, CONV_HEAD) + RMS_EPS)
            gc = gc_ref[sl, :]
            zc = ((yc * rr) * _pvb(pv_ref, PV_CG)) * (gc * _sigmoid(gc))
            xl = xl_ref[sl, :]
            u = _conv4(pv_ref, xl, _shift_down(xl, xlp, 1, row), _shift_down(xl, xlp, 2, row), _shift_down(xl, xlp, 3, row))
            u_s[sl, :] = u
            return (uc, xl), (zc, u)

        ucp, xlp = _for_groups(ng, conv_group, (ucp_s[...], xlp_s[...]), unroll=2 * UNROLL,
                               stores=(_rows_of(yc_ref), _rows_of(ub_s)))
        ucp_s[...] = ucp
        xlp_s[...] = xlp

        gate_s[...] = _mm(ub_s[...], wai_ref[0])

        def lru_group(g, h_before):
            sl = pl.ds(pl.multiple_of(g * SUBLANES, SUBLANES), SUBLANES)
            u = u_s[sl, :]
            r = _sigmoid(gate_s[sl, 0:lw] + _pvb(pv_ref, PV_BA))
            ig = _sigmoid(gate_s[sl, lw:2 * lw] + _pvb(pv_ref, PV_BI))
            r_ref[sl, :] = r
            ig_ref[sl, :] = ig
            a, _, mult, _ = _decay(r, ls_s[...])
            A, B = _scan8_fwd(a, mult * (ig * u), row)
            h = B + A * jnp.broadcast_to(h_before[SUBLANES - 1:SUBLANES, :], (SUBLANES, lw))
            h_ref[sl, :] = h
            rr = lax.rsqrt(_head_mean(h * h, LRU_HEAD) + RMS_EPS)
            gl = gl_ref[sl, :]
            return h, (((h * rr) * _pvb(pv_ref, PV_LG)) * (gl * _sigmoid(gl)),)

        hbuf_s[...] = _for_groups(ng, lru_group, hbuf_s[...], unroll=2 * UNROLL, stores=(_rows_of(yl_ref),))

    def part(p):
        return pl.BlockSpec((None, tb, lw), lambda c, i: (2 * p + c // per_chunk, i, c % per_chunk))

    strip = pl.BlockSpec((tb, lw), lambda c, i: (i, c))
    small = pltpu.VMEM((SUBLANES, lw), F32)
    return pl.pallas_call(
        body, grid=(ns, nt),
        in_specs=[part(p) for p in range(N_PARTS)] + [
            pl.BlockSpec((PV_ROWS, lw), lambda c, i: (0, c)),
            pl.BlockSpec((1, lw, 2 * lw), lambda c, i: (c, 0, 0)),
            pl.BlockSpec(w_out.shape, lambda c, i: (0, 0))],
        out_specs=(strip,) * 6 + (pl.BlockSpec(memory_space=pl.ANY),),
        out_shape=(jax.ShapeDtypeStruct((t, D_PART), MXU_DTYPE),) * 2 + (jax.ShapeDtypeStruct((t, D_PART), F32),) * 4 + (
            jax.ShapeDtypeStruct((N_CHIPS,) + w_out.shape, MXU_DTYPE),),
        scratch_shapes=[small, small, small, small, pltpu.VMEM((tb, lw), MXU_DTYPE),
                        pltpu.VMEM((tb, 2 * lw), F32), pltpu.VMEM(w_out.shape, MXU_DTYPE),
                        pltpu.SemaphoreType.DMA, pltpu.SemaphoreType.DMA((6,)), pltpu.SemaphoreType.DMA((6,))],
        compiler_params=_cp(ARB, ARB), name="mixer_forward",
    )(proj, proj, proj, proj, proj, proj, pvec, wai, w_out)


def _mixer_backward(proj, h, u, r, ig, dy, pvec, wai, go, gob):
    n_blocks, half, cols = go.shape[0], go.shape[1] // 2, go.shape[2]
    t = proj.shape[1]
    tb = 1024
    ng = tb // SUBLANES
    nt = t // tb
    gpb = tb // SUBLANES

    def body(bg_ref, cg_ref, xc_ref, gc_ref, xl_ref, gl_ref, h_ref, u_ref, r_ref, ig_ref, dyc_ref, dyl_ref,
             cgh_ref, xch_ref, xlh_ref, hh_ref, pv_ref, wai_ref, go_hbm, gob_hbm,
             dp_ref, gw_ref, sv_ref, ro_ref, so_hbm,
             ls_s, ub_s, uce_s, xle_s, he_s, dgb_s, du_s, gbuf_s,
             acc_s, an_s, dvn_s, dun_s, sum_s, arrival_s, sumb_s, send_sems, recv_sems, sibling_sems):
        i = pl.program_id(1)
        first_block = i == nt - 1

        step = pl.program_id(0) * nt + i
        mesh_x, mesh_y, core, _ = _mesh_pos()
        mine = pl.ds(pl.multiple_of(half * core, half), half)
        theirs = pl.ds(pl.multiple_of(half * (1 - core), half), half)
        load = pltpu.make_async_copy(go_hbm.at[:, mine, :], sum_s, sibling_sems.at[0])
        swap = pltpu.make_async_remote_copy(src_ref=gob_hbm.at[:, theirs, :], dst_ref=arrival_s, send_sem=sibling_sems.at[1],
                                            recv_sem=sibling_sems.at[2], device_id=(mesh_x, mesh_y, 1 - core), device_id_type=MESH)
        store = pltpu.make_async_copy(sum_s, so_hbm, sibling_sems.at[3])

        @pl.when(step == 0)
        def _():
            load.start()
            swap.start()

        @pl.when(step == 1)
        def _():
            load.wait()
            swap.wait()
            sum_s[...] = sum_s[...] + arrival_s[...].astype(F32)
            sumb_s[...] = sum_s[...].astype(jnp.bfloat16)
            store.start()
            for cp in _chip_block_copies(sumb_s, ro_ref, 1, send_sems, recv_sems):
                cp.start()

        @pl.when(step == NS * nt - 1)
        def _():
            store.wait()
            for cp in _chip_block_copies(sumb_s, ro_ref, 1, send_sems, recv_sems):
                cp.wait()

        @pl.when(i == 0)
        def _():
            acc_s[...] = jnp.zeros_like(acc_s)
            gw_ref[...] = jnp.zeros_like(gw_ref)
            an_s[...] = jnp.zeros_like(an_s)
            dvn_s[...] = jnp.zeros_like(dvn_s)
            dun_s[...] = jnp.zeros_like(dun_s)
            gbuf_s[...] = jnp.zeros_like(gbuf_s)

        row = lax.broadcasted_iota(jnp.int32, (SUBLANES, LW), 0)
        ls_s[...] = RG_LRU_C * _log_sigmoid(_pvb(pv_ref, PV_LAM))
        keep = jnp.where(first_block, 0.0, 1.0)
        uce_s[0:SUBLANES, :] = (cgh_ref[...] * xch_ref[...]) * keep
        xle_s[0:SUBLANES, :] = xlh_ref[...] * keep
        he_s[0:SUBLANES, :] = hh_ref[...] * keep
        xle_s[SUBLANES:SUBLANES + tb, :] = xl_ref[...]
        he_s[SUBLANES:SUBLANES + tb, :] = h_ref[...]

        uce_s[SUBLANES:SUBLANES + tb, :] = cg_ref[...] * xc_ref[...]

        def acc_add(k, v):
            acc_s[k] += v

        def main_group(gi, carry):
            a_next, dv_next, g_next = carry
            g = ng - 1 - gi
            r0 = pl.multiple_of(g * SUBLANES, SUBLANES)
            sl = pl.ds(r0, SUBLANES)
            sl_e = pl.ds(r0 + SUBLANES, SUBLANES)
            lsb = ls_s[...]
            u = u_ref[sl, :]
            r = r_ref[sl, :]
            ig = ig_ref[sl, :]
            a, e2, mult, inv_mult = _decay(r, lsb)
            gl = gl_ref[sl, :]
            sg = _sigmoid(gl)
            s_l = gl * sg
            h8 = he_s[sl_e, :]
            hprev = _shift_down(h8, he_s[sl, :], 1, row)
            rr = lax.rsqrt(_head_mean(h8 * h8, LRU_HEAD) + RMS_EPS)
            n = h8 * rr
            dz = dyl_ref[sl, :]
            lg = _pvb(pv_ref, PV_LG)
            acc_add(PV_LG, (dz * n) * s_l)
            p5 = ((dz * n) * lg) * (sg + s_l * (1.0 - sg))
            dn = (dz * lg) * s_l
            dh = rr * (dn - n * _head_mean(dn * n, LRU_HEAD))
            A, B = _scan8_rev(_shift_up(a, a_next, 1, row), dh, row)
            gg = B + A * jnp.broadcast_to(g_next[0:1, :], (SUBLANES, LW))
            da = gg * hprev
            iu = ig * u
            diu = gg * mult
            dla = da * a - (gg * iu) * (e2 * inv_mult)
            acc_add(PV_LAM, dla * r)
            dra = (dla * lsb) * (r * (1.0 - r))
            dia = (diu * u) * (ig * (1.0 - ig))
            acc_add(PV_BA, dra)
            acc_add(PV_BI, dia)
            du_s[sl, :] = diu * ig
            bg = bg_ref[sl, :]
            gc = gc_ref[sl, :]
            uc = uce_s[sl_e, :]
            ucp = uce_s[sl, :]
            uc1 = _shift_down(uc, ucp, 1, row)
            uc2 = _shift_down(uc, ucp, 2, row)
            v = _conv3(pv_ref, uc, uc1, uc2)
            yc = bg * v
            rrc = lax.rsqrt(_head_mean(yc * yc, CONV_HEAD) + RMS_EPS)
            nc = yc * rrc
            sgc = _sigmoid(gc)
            s_c = gc * sgc
            dzc = dyc_ref[sl, :]
            cgain = _pvb(pv_ref, PV_CG)
            acc_add(PV_CG, (dzc * nc) * s_c)
            p3 = ((dzc * nc) * cgain) * (sgc + s_c * (1.0 - sgc))
            dnc = (dzc * cgain) * s_c
            dyc = rrc * (dnc - nc * _head_mean(dnc * nc, CONV_HEAD))
            dv = dyc * bg
            duc = (_pvb(pv_ref, PV_CONV_W + 2) * dv + _pvb(pv_ref, PV_CONV_W + 1) * _shift_up(dv, dv_next, 1, row)
                   + _pvb(pv_ref, PV_CONV_W) * _shift_up(dv, dv_next, 2, row))
            acc_add(PV_CONV_W + 2, dv * uc)
            acc_add(PV_CONV_W + 1, dv * uc1)
            acc_add(PV_CONV_W, dv * uc2)
            return (a, dv, gg), (dyc * v, duc * xc_ref[sl, :], duc * cg_ref[sl, :], p3, p5, dra, dia, u)

        a_next, dv_next, g_next = _for_groups(
            ng, main_group, (an_s[...], dvn_s[...], gbuf_s[...]), descending=True,
            stores=(_rows_of(dp_ref, 0), _rows_of(dp_ref, 1), _rows_of(dp_ref, 2), _rows_of(dp_ref, 3), _rows_of(dp_ref, 5),
                    _rows_of(dgb_s, cols=slice(0, LW)), _rows_of(dgb_s, cols=slice(LW, 2 * LW)), _rows_of(ub_s)))
        an_s[...] = a_next
        dvn_s[...] = dv_next
        gbuf_s[...] = g_next

        dgb = dgb_s[...]
        du_s[...] += _mm_nt(dgb, wai_ref[0])
        gw_ref[0] += _mm_tn(ub_s[...], dgb)

        def lru_conv_group(gi, du_next):
            g = ng - 1 - gi
            r0 = pl.multiple_of(g * SUBLANES, SUBLANES)
            sl = pl.ds(r0, SUBLANES)
            du = du_s[sl, :]
            xl = xle_s[pl.ds(r0 + SUBLANES, SUBLANES), :]
            xlp = xle_s[sl, :]
            acc_add(PV_LRU_B, du)
            acc_add(PV_LRU_W + 3, du * xl)
            acc_add(PV_LRU_W + 2, du * _shift_down(xl, xlp, 1, row))
            acc_add(PV_LRU_W + 1, du * _shift_down(xl, xlp, 2, row))
            acc_add(PV_LRU_W, du * _shift_down(xl, xlp, 3, row))
            dxl = (((_pvb(pv_ref, PV_LRU_W + 3) * du + _pvb(pv_ref, PV_LRU_W + 2) * _shift_up(du, du_next, 1, row))
                    + _pvb(pv_ref, PV_LRU_W + 1) * _shift_up(du, du_next, 2, row))
                   + _pvb(pv_ref, PV_LRU_W) * _shift_up(du, du_next, 3, row))
            return du, (dxl,)

        dun_s[...] = _for_groups(ng, lru_conv_group, dun_s[...], descending=True, stores=(_rows_of(dp_ref, 4),))

        @pl.when(first_block)
        def _():
            sv_ref[...] = jnp.zeros_like(sv_ref)
            for k in range(N_ACC):
                tot = jnp.sum(acc_s[k], axis=0, keepdims=True)
                if k == PV_LAM:
                    tot = (RG_LRU_C * tot) / (1.0 + jnp.exp(pv_ref[PV_LAM:PV_LAM + 1, :]))
                sv_ref[k:k + 1, :] = tot

    def part(p):
        return pl.BlockSpec((None, tb, LW), lambda c, i: (2 * p + c // STRIPS_PER_CHUNK, nt - 1 - i, c % STRIPS_PER_CHUNK))

    def halo(p):
        return pl.BlockSpec((None, SUBLANES, LW), lambda c, i: (2 * p + c // STRIPS_PER_CHUNK,
                                                                jnp.maximum((nt - 1 - i) * gpb - 1, 0), c % STRIPS_PER_CHUNK))

    strip = pl.BlockSpec((tb, LW), lambda c, i: (nt - 1 - i, c))
    big = pltpu.VMEM((tb, LW), F32)
    big_e = pltpu.VMEM((tb + SUBLANES, LW), F32)
    small = pltpu.VMEM((SUBLANES, LW), F32)
    outs = pl.pallas_call(
        body, grid=(NS, nt),
        in_specs=[part(p) for p in range(N_PARTS)] + [
            strip, strip, strip, strip, strip, pl.BlockSpec((tb, LW), lambda c, i: (nt - 1 - i, NS + c)),
            halo(1), halo(2), halo(4),
            pl.BlockSpec((SUBLANES, LW), lambda c, i: (jnp.maximum((nt - 1 - i) * gpb - 1, 0), c)),
            pl.BlockSpec((PV_ROWS, LW), lambda c, i: (0, c)),
            pl.BlockSpec((1, LW, 2 * LW), lambda c, i: (c, 0, 0)),
            pl.BlockSpec(memory_space=pl.ANY), pl.BlockSpec(memory_space=pl.ANY)],
        out_specs=(pl.BlockSpec((N_PARTS, tb, LW), lambda c, i: (0, nt - 1 - i, c)),
                   pl.BlockSpec((1, LW, 2 * LW), lambda c, i: (c, 0, 0)),
                   pl.BlockSpec((PV_ROWS, LW), lambda c, i: (0, c)),
                   pl.BlockSpec(memory_space=pl.ANY), pl.BlockSpec(memory_space=pl.ANY)),
        out_shape=(jax.ShapeDtypeStruct((N_PARTS, t, D_PART), MXU_DTYPE),
                   jax.ShapeDtypeStruct((NS, LW, 2 * LW), F32), jax.ShapeDtypeStruct((PV_ROWS, D_PART), F32),
                   jax.ShapeDtypeStruct((3, 1, half, cols), jnp.bfloat16),
                   jax.ShapeDtypeStruct((n_blocks, half, cols), F32)),
        scratch_shapes=[small, pltpu.VMEM((tb, LW), MXU_DTYPE), big_e, big_e, big_e,
                        pltpu.VMEM((tb, 2 * LW), MXU_DTYPE), big, small,
                        pltpu.VMEM((N_ACC, SUBLANES, LW), F32), small, small, small,
                        pltpu.VMEM((n_blocks, half, cols), F32), pltpu.VMEM((n_blocks, half, cols), jnp.bfloat16),
                        pltpu.VMEM((n_blocks, half, cols), jnp.bfloat16),
                        pltpu.SemaphoreType.DMA((3,)), pltpu.SemaphoreType.DMA((3,)), pltpu.SemaphoreType.DMA((4,))],
        compiler_params=_cp(ARB, ARB), name="mixer_backward",
    )(proj, proj, proj, proj, proj, proj, h, u, r, ig, dy, dy, proj, proj, proj, h, pvec, wai, go, gob)
    return outs


def _adamw(w, g, m, v):
    m = ADAM_B1 * m + (1.0 - ADAM_B1) * g
    v = ADAM_B2 * v + (1.0 - ADAM_B2) * (g * g)
    m_hat = m / (1.0 - ADAM_B1 ** ADAM_STEP)
    v_hat = v / (1.0 - ADAM_B2 ** ADAM_STEP)
    delta = -ADAM_LR * (m_hat / (jnp.sqrt(v_hat) + ADAM_EPS) + ADAM_WD * w)
    return delta, m, v


def _adamw_blocks(f_in, f_out, p_in, p_out):
    n_pieces = 4
    slab = 2 * SUBLANES

    def body(fi_hbm, fo_hbm, wi_hbm, mi_hbm, vi_hbm, wo_hbm, mo_hbm, vo_hbm,
             gi_hbm, di_hbm, nmi_hbm, nvi_hbm, do_hbm, nmo_hbm, nvo_hbm,
             g_in, g_out, wi, mi, vi, wo, mo, vo, load_sems, store_sems):
        blocks = ((g_in, fi_hbm, (wi, mi, vi), (wi_hbm, mi_hbm, vi_hbm), (di_hbm, nmi_hbm, nvi_hbm)),
                  (g_out, fo_hbm, (wo, mo, vo), (wo_hbm, mo_hbm, vo_hbm), (do_hbm, nmo_hbm, nvo_hbm)))
        loads, stores = [], []

        def start(src, dst, sems, group):
            group.append(pltpu.make_async_copy(src, dst, sems.at[len(group)]))
            group[-1].start(priority=1 if group is stores else 0)

        piece = lambda g, i: slice(g.shape[1] // n_pieces * i, g.shape[1] // n_pieces * (i + 1))
        for i in range(n_pieces):
            for g, f_hbm, p, p_hbm, _ in blocks:
                rows = piece(g, i)
                start(f_hbm.at[:, rows, :], g.at[:, rows, :], load_sems, loads)
                for s, s_hbm in zip(p, p_hbm):
                    start(s_hbm.at[rows, :], s.at[rows, :], load_sems, loads)
        per_piece = len(loads) // n_pieces
        for i in range(n_pieces):
            for cp in loads[per_piece * i:per_piece * (i + 1)]:
                cp.wait()
            for g, _, (w_s, m_s, v_s), _, outs in blocks:
                rows = piece(g, i)
                n, cols = g.shape[0], g.shape[2]

                def step(t, carry):
                    rs = pl.ds(pl.multiple_of(rows.start + slab * t, slab), slab)
                    for q in range(n):
                        cs = slice(cols * q, cols * (q + 1))
                        w_s[rs, cs], m_s[rs, cs], v_s[rs, cs] = _adamw(w_s[rs, cs], g[q, rs, :], m_s[rs, cs], v_s[rs, cs])
                    return carry

                lax.fori_loop(0, (rows.stop - rows.start) // slab, step, 0)
                for s, o_hbm in zip((w_s, m_s, v_s), outs):
                    start(s.at[rows, :], o_hbm.at[rows, :], store_sems, stores)
            rows, cols = piece(g_in, i), g_in.shape[2]
            for q in range(g_in.shape[0]):
                start(g_in.at[q, rows, :], gi_hbm.at[rows, cols * q:cols * (q + 1)], store_sems, stores)
        for cp in stores:
            cp.wait()

    w_i, w_o = p_in[0], p_out[0]
    n_in = f_in.shape[0]
    assert w_i.shape == (f_in.shape[1], n_in * f_in.shape[2]) and w_o.shape == (f_out.shape[1], f_out.shape[0] * f_out.shape[2])
    hbm = pl.BlockSpec(memory_space=pl.ANY)
    return pl.pallas_call(
        body, in_specs=[hbm] * 8, out_specs=(hbm,) * 7,
        out_shape=(jax.ShapeDtypeStruct(w_i.shape, F32),) * 4 + (jax.ShapeDtypeStruct(w_o.shape, F32),) * 3,
        scratch_shapes=[pltpu.VMEM(f_in.shape, F32), pltpu.VMEM(f_out.shape, F32)]
        + [pltpu.VMEM(w_i.shape, F32)] * 3 + [pltpu.VMEM(w_o.shape, F32)] * 3
        + [pltpu.SemaphoreType.DMA((n_pieces * 8,)), pltpu.SemaphoreType.DMA((n_pieces * (6 + n_in),))],
        compiler_params=_cp(), name="adamw_blocks",
    )(f_in, f_out, *p_in, *p_out)


def _adam_small(ws, ms, vs, gs):
    n = len(ws)

    def body(*refs):
        w_r, m_r, v_r, g_r = refs[0:n], refs[n:2 * n], refs[2 * n:3 * n], refs[3 * n:4 * n]
        d_o, m_o, v_o = refs[4 * n:5 * n], refs[5 * n:6 * n], refs[6 * n:7 * n]
        for j in range(n):
            d_o[j][...], m_o[j][...], v_o[j][...] = _adamw(w_r[j][...], g_r[j][...], m_r[j][...], v_r[j][...])

    vm = pl.BlockSpec(memory_space=pltpu.VMEM)
    shapes = tuple(jax.ShapeDtypeStruct(w.shape, F32) for w in ws)
    outs = pl.pallas_call(
        body, in_specs=[vm] * (4 * n), out_specs=(vm,) * (3 * n), out_shape=shapes * 3,
        compiler_params=_cp(), name="adam_small",
    )(*ws, *ms, *vs, *gs)
    return outs[0:n], outs[n:2 * n], outs[2 * n:3 * n]


def _block_diag_strips(w, lw):
    heads = lw // LRU_HEAD
    w4 = w.reshape(D_PART // lw, heads, LRU_HEAD, LRU_HEAD)
    rows = [jnp.pad(w4[:, hh], ((0, 0), (0, 0), (LRU_HEAD * hh, lw - LRU_HEAD * (hh + 1)))) for hh in range(heads)]
    return jnp.concatenate(rows, axis=1)


def _gate_matrices(w_a, w_i, lw):
    return jnp.concatenate([_block_diag_strips(w_a, lw), _block_diag_strips(w_i, lw)], axis=2).astype(MXU_DTYPE)


def _strip_diag_blocks(g):
    g5 = g.reshape(NS, HEADS_PER_STRIP, LRU_HEAD, HEADS_PER_STRIP, LRU_HEAD)
    return jnp.stack([g5[:, hh, :, hh, :] for hh in range(HEADS_PER_STRIP)], axis=1).reshape(NS * HEADS_PER_STRIP, LRU_HEAD, LRU_HEAD)


def kernel(x, ln_g, w_in, conv_w, lru_conv_w, lru_conv_b, w_a, b_a, w_i, b_i, lam, conv_out_g, lru_out_g, w_out, final_g, loss_target, m_ln_g, m_w_in, m_conv_w, m_lru_conv_w, m_lru_conv_b, m_w_a, m_b_a, m_w_i, m_b_i, m_lam, m_conv_out_g, m_lru_out_g, m_w_out, m_final_g, v_ln_g, v_w_in, v_conv_w, v_lru_conv_w, v_lru_conv_b, v_w_a, v_b_a, v_w_i, v_b_i, v_lam, v_conv_out_g, v_lru_out_g, v_w_out, v_final_g):
    xi, yi, ci = lax.axis_index("x"), lax.axis_index("y"), lax.axis_index("c")
    k = 2 * xi + yi
    t = x.shape[1]
    x2 = x.reshape(t, D_MODEL)
    tgt2 = loss_target.reshape(t, D_MODEL)
    row = lambda a: a.reshape(1, -1)

    small = jnp.concatenate([conv_w, lru_conv_w, jnp.zeros((1, conv_w.shape[1]), F32)], axis=0)
    proj, xn, w12, sm4 = _gather_in_projection(x2, row(ln_g), w_in, small)
    convs = jnp.transpose(sm4, (1, 0, 2)).reshape(SUBLANES, D_PART)
    pvec = jnp.concatenate(
        [convs[0:7], row(lru_conv_b), row(b_a), row(b_i), row(lam), row(conv_out_g), row(lru_out_g),
         jnp.zeros((PV_ROWS - N_ACC, D_PART), F32)], axis=0)
    wai = _gate_matrices(w_a, w_i, LW)

    c_arr = jnp.reshape(ci, (1,)).astype(jnp.int32)
    yc, yl, h, u, r, ig, wo4 = _mixer_forward(proj, pvec, _gate_matrices(w_a, w_i, FWD_LW), w_out)
    wo = wo4.reshape(2 * D_PART, D_MODEL)
    do, dob, dy, st_out = _out_projection_loss(yc, yl, x2, tgt2, wo, row(final_g))
    go4, go4b = _w_out_grad(yc, yl, dob)
    dproj, g_wai, svec, r2o, s_out = _mixer_backward(proj, h, u, r, ig, dy, pvec, wai, go4, go4b)
    gwa = _strip_diag_blocks(g_wai[:, :, 0:LW]).reshape(LRU_HEAD, D_PART)
    gwi = _strip_diag_blocks(g_wai[:, :, LW:2 * LW]).reshape(LRU_HEAD, D_PART)
    g12, g12b, red = _w_in_grad(xn, dproj, jnp.concatenate([svec, st_out, gwa, gwi], axis=0))
    s_in, sb_in = _add_sibling_halves(g12, g12b, c_arr, "add_sibling_halves_in")
    grad_x, st_in, r2i = _input_grad(dproj, w12, x2, do, row(ln_g), sb_in)
    f_in, f_out, red_ln = _finish_gradients(s_in, r2i, s_out, r2o, st_in)
    r_out = PV_ROWS
    r_wa = PV_ROWS + SUBLANES
    r_wi = r_wa + LRU_HEAD
    loss = red[r_out + 1, 0]

    g_w_in, d_w_in, nm_w_in, nv_w_in, d_w_out, nm_w_out, nv_w_out = _adamw_blocks(
        f_in, f_out, (w_in, m_w_in, v_w_in), (w_out, m_w_out, v_w_out))
    g_w_out = f_out[0]

    ncol = conv_w.shape[1]
    conv_cols = lax.dynamic_slice(red, (0, k * ncol), (SUBLANES, ncol))
    g_small = {
        "ln_g": red_ln[0], "conv_w": conv_cols[0:3], "lru_conv_w": conv_cols[3:7], "lru_conv_b": red[PV_LRU_B],
        "w_a": red[r_wa:r_wa + LRU_HEAD].reshape(w_a.shape), "b_a": red[PV_BA],
        "w_i": red[r_wi:r_wi + LRU_HEAD].reshape(w_i.shape), "b_i": red[PV_BI], "lam": red[PV_LAM],
        "conv_out_g": red[PV_CG], "lru_out_g": red[PV_LG], "final_g": red[r_out],
    }
    w_small = {"ln_g": ln_g, "conv_w": conv_w, "lru_conv_w": lru_conv_w, "lru_conv_b": lru_conv_b, "w_a": w_a, "b_a": b_a,
               "w_i": w_i, "b_i": b_i, "lam": lam, "conv_out_g": conv_out_g, "lru_out_g": lru_out_g, "final_g": final_g}
    m_small = {"ln_g": m_ln_g, "conv_w": m_conv_w, "lru_conv_w": m_lru_conv_w, "lru_conv_b": m_lru_conv_b, "w_a": m_w_a,
               "b_a": m_b_a, "w_i": m_w_i, "b_i": m_b_i, "lam": m_lam, "conv_out_g": m_conv_out_g,
               "lru_out_g": m_lru_out_g, "final_g": m_final_g}
    v_small = {"ln_g": v_ln_g, "conv_w": v_conv_w, "lru_conv_w": v_lru_conv_w, "lru_conv_b": v_lru_conv_b, "w_a": v_w_a,
               "b_a": v_b_a, "w_i": v_w_i, "b_i": v_b_i, "lam": v_lam, "conv_out_g": v_conv_out_g,
               "lru_out_g": v_lru_out_g, "final_g": v_final_g}
    names = list(w_small)
    as2d = lambda a: a.reshape(1, -1) if a.ndim == 1 else a
    d_s, m_s, v_s = _adam_small([as2d(w_small[n]) for n in names], [as2d(m_small[n]) for n in names],
                                [as2d(v_small[n]) for n in names], [as2d(g_small[n]) for n in names])
    back = lambda n, a: a.reshape(w_small[n].shape)
    grads = {n: g_small[n] for n in names}
    deltas = {n: back(n, a) for n, a in zip(names, d_s)}
    new_m = {n: back(n, a) for n, a in zip(names, m_s)}
    new_v = {n: back(n, a) for n, a in zip(names, v_s)}
    grads["w_in"], deltas["w_in"], new_m["w_in"], new_v["w_in"] = g_w_in, d_w_in, nm_w_in, nv_w_in
    grads["w_out"], deltas["w_out"], new_m["w_out"], new_v["w_out"] = g_w_out, d_w_out, nm_w_out, nv_w_out

    order = ["ln_g", "w_in", "conv_w", "lru_conv_w", "lru_conv_b", "w_a", "b_a", "w_i", "b_i", "lam", "conv_out_g",
             "lru_out_g", "w_out", "final_g"]
    return (loss, grad_x.reshape(x.shape), *[grads[n] for n in order], *[deltas[n] for n in order],
            *[new_m[n] for n in order], *[new_v[n] for n in order])
```
